```python
import math
import jax
import jax.numpy as jnp
from jax import lax
import numpy as np

D_MODEL = 1024
BATCH = 8
SEQ = 8192
DEPTH = 1

D_MIX = D_MODEL
GDN_HEADS = 4
GDN_HEAD_DIM = 128
GDN_WIDTH = GDN_HEADS * GDN_HEAD_DIM
CONV_WIDTH = 5
CHUNK = 64
SWA_HEADS = 8
SWA_HEAD_DIM = 64
SWA_WIDTH = SWA_HEADS * SWA_HEAD_DIM
DILATION_PATTERNS = ((128, 1), (512, 4), (2048, 16))
BAND_BLOCK = 64
REL_BUCKETS = 32
REL_MAX_DISTANCE = 1024
D_FF = 2816
EPS = 1e-6
NEG_BIG = -1e30
SPLITS = (3 * GDN_WIDTH, GDN_WIDTH, 2 * GDN_HEADS, 2 * GDN_HEADS, 3 * SWA_WIDTH)
N_IN = sum(SPLITS)

kernel_name = "hybrid_gdn_dilated_swa_macaron"


def rms_norm(x, w):
    xf = x.astype(jnp.float32)
    y = xf * lax.rsqrt(jnp.mean(xf * xf, axis=-1, keepdims=True) + EPS)
    return (y * w.astype(jnp.float32)).astype(x.dtype)


def l2_normalize(x):
    return x * lax.rsqrt(jnp.sum(x * x, axis=-1, keepdims=True) + EPS)


def swiglu(x, w_gate, w_up, w_down):
    return (jax.nn.silu(x @ w_gate) * (x @ w_up)) @ w_down


def t5_bucket(rel):
    nb = REL_BUCKETS // 2
    bucket = (rel > 0).astype(np.int32) * nb
    n = np.abs(rel)
    max_exact = nb // 2
    large = max_exact + (np.log(np.maximum(n, 1) / max_exact)
                         / math.log(REL_MAX_DISTANCE / max_exact) * (nb - max_exact)).astype(np.int32)
    large = np.minimum(large, nb - 1)
    return (bucket + np.where(n < max_exact, n, large)).astype(np.int32)


def short_conv(x, w):
    c, kw = w.shape
    rhs = jnp.transpose(w).astype(x.dtype)[:, None, :]
    return lax.conv_general_dilated(x, rhs, window_strides=(1,),
                                    padding=((kw // 2, kw // 2),),
                                    dimension_numbers=('NWC', 'WIO', 'NWC'),
                                    feature_group_count=c)


def gated_delta_chunked(q, k, v, g, beta):
    b, h, t, dk = q.shape
    dv = v.shape[-1]
    nc = t // CHUNK
    q = q * (dk ** -0.5)

    def chunks(a):
        return a.reshape(b, h, nc, CHUNK, *a.shape[3:])

    q, k, v, g, beta = chunks(q), chunks(k), chunks(v), chunks(g), chunks(beta)
    g = jnp.cumsum(g, axis=-1)
    incl = jnp.tril(jnp.ones((CHUNK, CHUNK), dtype=bool))
    strict = jnp.tril(jnp.ones((CHUNK, CHUNK), dtype=bool), -1)
    diff = g[..., :, None] - g[..., None, :]
    decay = jnp.where(incl, jnp.exp(jnp.where(incl, diff, 0.0)), 0.0)
    kb = k * beta[..., None]
    lmat = jnp.where(strict, jnp.einsum('bhncd,bhnjd->bhncj', kb, k) * decay, 0.0)
    eye = jnp.eye(CHUNK, dtype=jnp.float32)
    tmat = lax.linalg.triangular_solve(eye + lmat, jnp.broadcast_to(eye, lmat.shape),
                                       left_side=True, lower=True, unit_diagonal=True)
    u = tmat @ (v * beta[..., None])
    w = tmat @ (kb * jnp.exp(g)[..., None])
    intra = jnp.einsum('bhncd,bhnjd->bhncj', q, k) * decay
    qg = q * jnp.exp(g)[..., None]
    g_last = g[..., -1]
    kdec = k * jnp.exp(g_last[..., None] - g)[..., None]

    def step(state, inp):
        u_c, w_c, qg_c, intra_c, kdec_c, gl_c = inp
        v_new = u_c - w_c @ state
        o_c = qg_c @ state + intra_c @ v_new
        state = state * jnp.exp(gl_c)[..., None, None] + jnp.swapaxes(kdec_c, -1, -2) @ v_new
        return state, o_c

    xs = tuple(jnp.moveaxis(a, 2, 0) for a in (u, w, qg, intra, kdec, g_last))
    state0 = jnp.zeros((b, h, dk, dv), jnp.float32)
    _, o = lax.scan(step, state0, xs)
    return jnp.moveaxis(o, 0, 2).reshape(b, h, t, dv)


def _reverse(a):
    return jnp.flip(a, axis=2)


def gdn_mixer(qkv, z, a, beta_logit, conv_w, a_log, dt_bias, norm_w):
    b, s, _ = qkv.shape
    f32 = jnp.float32
    qkv = jax.nn.silu(short_conv(qkv, conv_w)).astype(f32)
    qkv = qkv.reshape(b, s, 3, GDN_HEADS, GDN_HEAD_DIM).transpose(2, 0, 3, 1, 4)
    q, k, v = l2_normalize(qkv[0]), l2_normalize(qkv[1]), qkv[2]
    a = a.astype(f32).reshape(b, s, 2, GDN_HEADS)
    g = -jnp.exp(a_log.astype(f32)) * jax.nn.softplus(a + dt_bias.astype(f32))
    beta = jax.nn.sigmoid(beta_logit.astype(f32).reshape(b, s, 2, GDN_HEADS))
    g = g.transpose(2, 0, 3, 1)
    beta = beta.transpose(2, 0, 3, 1)
    o_fwd = gated_delta_chunked(q, k, v, g[0], beta[0])
    o_bwd = _reverse(gated_delta_chunked(_reverse(q), _reverse(k), _reverse(v),
                                         _reverse(g[1]), _reverse(beta[1])))
    o = (o_fwd + o_bwd).transpose(0, 2, 1, 3)
    zg = jax.nn.silu(z.astype(f32).reshape(b, s, GDN_HEADS, GDN_HEAD_DIM))
    o = rms_norm(o, norm_w) * zg
    return o.reshape(b, s, GDN_WIDTH).astype(z.dtype)


def dilated_band_attention(q, k, v, rel_bias, window, dilation):
    b, s, h, dh = q.shape
    radius = window // (2 * dilation)
    blk = BAND_BLOCK
    length = s // dilation
    nb = -(-length // blk)
    lp = nb * blk

    def to_blocks(a):
        a = a.reshape(b, length, dilation, h, dh).transpose(0, 2, 3, 1, 4)
        a = jnp.pad(a, ((0, 0), (0, 0), (0, 0), (0, lp - length), (0, 0)))
        return a.reshape(b, dilation, h, nb, blk, dh)

    def band(a):
        ap = jnp.pad(a, ((0, 0), (0, 0), (0, 0), (1, 1), (0, 0), (0, 0)))
        return jnp.concatenate([ap[:, :, :, :-2], ap[:, :, :, 1:-1], ap[:, :, :, 2:]], axis=-2)

    qb = to_blocks(q)
    kw = band(to_blocks(k))
    vw = band(to_blocks(v))
    qi = np.arange(blk)[:, None]
    ki = np.arange(3 * blk)[None, :]
    rel = ki - blk - qi
    key_t = (np.arange(nb)[:, None, None] - 1) * blk + ki[None]
    valid = (np.abs(rel)[None] <= radius) & (key_t >= 0) & (key_t < length)
    bias = jnp.transpose(rel_bias[t5_bucket(rel * dilation)], (2, 0, 1))[:, None]
    logits = jnp.einsum('bdhnqe,bdhnke->bdhnqk', qb, kw, preferred_element_type=jnp.float32)
    logits = jnp.where(valid, logits + bias.astype(jnp.float32), NEG_BIG)
    m = jnp.max(logits, axis=-1, keepdims=True)
    p = jnp.exp(logits - m)
    den = jnp.sum(p, axis=-1, keepdims=True)
    o = jnp.einsum('bdhnqk,bdhnke->bdhnqe', p, vw.astype(jnp.float32)) / den
    lse = (m + jnp.log(den))[..., 0]
    o = o.reshape(b, dilation, h, lp, dh)[:, :, :, :length].transpose(0, 3, 1, 2, 4).reshape(b, s, h, dh)
    lse = lse.reshape(b, dilation, h, lp)[..., :length].transpose(0, 3, 1, 2).reshape(b, s, h)
    return o, lse


def dilated_mixer(qkv, q_norm_w, k_norm_w, rel_bias):
    b, s, _ = qkv.shape
    qkv = qkv.reshape(b, s, 3, SWA_HEADS, SWA_HEAD_DIM)
    q = rms_norm(qkv[:, :, 0], q_norm_w) * (SWA_HEAD_DIM ** -0.5)
    k = rms_norm(qkv[:, :, 1], k_norm_w)
    v = qkv[:, :, 2]
    outs, lses = [], []
    for window, dilation in DILATION_PATTERNS:
        o_p, lse_p = dilated_band_attention(q, k, v, rel_bias, window, dilation)
        outs.append(o_p)
        lses.append(lse_p)
    wts = jax.nn.softmax(jnp.stack(lses, axis=0), axis=0)
    o = jnp.sum(wts[..., None] * jnp.stack(outs, axis=0), axis=0)
    return o.reshape(b, s, SWA_WIDTH).astype(qkv.dtype)


def _fwd_setup_inputs(seed: int = 0) -> dict:
    key = jax.random.key(seed)
    ks = jax.random.split(key, 24)
    f32 = jnp.float32

    def dense(k, shape, fan_in):
        return jax.random.normal(k, shape, f32) * (fan_in ** -0.5)

    def gain(k, shape):
        return 1.0 + 0.02 * jax.random.normal(k, shape, f32)

    x = jax.random.normal(ks[0], (BATCH, SEQ, D_MODEL), f32)
    ffn1_norm = gain(ks[1], (DEPTH, D_MODEL))
    ffn1_w_gate = dense(ks[2], (DEPTH, D_MODEL, D_FF), D_MODEL)
    ffn1_w_up = dense(ks[3], (DEPTH, D_MODEL, D_FF), D_MODEL)
    ffn1_w_down = dense(ks[4], (DEPTH, D_FF, D_MODEL), D_FF)
    mix_norm = gain(ks[5], (DEPTH, D_MODEL))
    w_in = dense(ks[6], (DEPTH, D_MODEL, N_IN), D_MODEL)
    conv_w = dense(ks[7], (DEPTH, 3 * GDN_WIDTH, CONV_WIDTH), CONV_WIDTH)
    a_log = jnp.log(jax.random.uniform(ks[8], (DEPTH, 2, GDN_HEADS), f32, 1.0, 16.0))
    dt = jnp.exp(jax.random.uniform(ks[9], (DEPTH, 2, GDN_HEADS), f32, math.log(1e-3), math.log(1e-1)))
    dt_bias = dt + jnp.log(-jnp.expm1(-dt))
    gdn_norm_w = gain(ks[10], (DEPTH, GDN_HEAD_DIM))
    q_norm_w = gain(ks[11], (DEPTH, SWA_HEAD_DIM))
    k_norm_w = gain(ks[12], (DEPTH, SWA_HEAD_DIM))
    rel_bias = 0.2 * jax.random.normal(ks[13], (REL_BUCKETS, SWA_HEADS), f32)
    w_out = dense(ks[14], (DEPTH, D_MIX, D_MODEL), D_MIX)
    ffn2_norm = gain(ks[15], (DEPTH, D_MODEL))
    ffn2_w_gate = dense(ks[16], (DEPTH, D_MODEL, D_FF), D_MODEL)
    ffn2_w_up = dense(ks[17], (DEPTH, D_MODEL, D_FF), D_MODEL)
    ffn2_w_down = dense(ks[18], (DEPTH, D_FF, D_MODEL), D_FF)
    final_norm = gain(ks[19], (DEPTH, D_MODEL))
    return {"x": x, "ffn1_norm": ffn1_norm, "ffn1_w_gate": ffn1_w_gate, "ffn1_w_up": ffn1_w_up,
            "ffn1_w_down": ffn1_w_down, "mix_norm": mix_norm, "w_in": w_in, "conv_w": conv_w,
            "a_log": a_log, "dt_bias": dt_bias, "gdn_norm_w": gdn_norm_w, "q_norm_w": q_norm_w,
            "k_norm_w": k_norm_w, "rel_bias": rel_bias, "w_out": w_out, "ffn2_norm": ffn2_norm,
            "ffn2_w_gate": ffn2_w_gate, "ffn2_w_up": ffn2_w_up, "ffn2_w_down": ffn2_w_down,
            "final_norm": final_norm}


def _fwd_reference(x, ffn1_norm, ffn1_w_gate, ffn1_w_up, ffn1_w_down, mix_norm, w_in, conv_w,
              a_log, dt_bias, gdn_norm_w, q_norm_w, k_norm_w, rel_bias, w_out, ffn2_norm,
              ffn2_w_gate, ffn2_w_up, ffn2_w_down, final_norm):
    split_at = np.cumsum(SPLITS)[:-1].tolist()
    for l in range(DEPTH):
        x = x + 0.5 * swiglu(rms_norm(x, ffn1_norm[l]), ffn1_w_gate[l], ffn1_w_up[l], ffn1_w_down[l])
        h = rms_norm(x, mix_norm[l])
        proj = h @ w_in[l]
        qkv_a, z_a, a_a, b_a, qkv_b = jnp.split(proj, split_at, axis=-1)
        o_a = gdn_mixer(qkv_a, z_a, a_a, b_a, conv_w[l], a_log[l], dt_bias[l], gdn_norm_w[l])
        o_b = dilated_mixer(qkv_b, q_norm_w[l], k_norm_w[l], rel_bias)
        x = x + jnp.concatenate([o_a, o_b], axis=-1) @ w_out[l]
        x = x + 0.5 * swiglu(rms_norm(x, ffn2_norm[l]), ffn2_w_gate[l], ffn2_w_up[l], ffn2_w_down[l])
        x = rms_norm(x, final_norm[l])
    return x


import jax as _jax
import jax.numpy as _jnp

TWIN_FORMAT = 'train_step'
FWD_PARAMS = ['x', 'ffn1_norm', 'ffn1_w_gate', 'ffn1_w_up', 'ffn1_w_down', 'mix_norm', 'w_in', 'conv_w', 'a_log', 'dt_bias', 'gdn_norm_w', 'q_norm_w', 'k_norm_w', 'rel_bias', 'w_out', 'ffn2_norm', 'ffn2_w_gate', 'ffn2_w_up', 'ffn2_w_down', 'final_norm']
TWIN_WEIGHTS = ['ffn1_norm', 'ffn1_w_gate', 'ffn1_w_up', 'ffn1_w_down', 'mix_norm', 'w_in', 'conv_w', 'a_log', 'dt_bias', 'gdn_norm_w', 'q_norm_w', 'k_norm_w', 'rel_bias', 'w_out', 'ffn2_norm', 'ffn2_w_gate', 'ffn2_w_up', 'ffn2_w_down', 'final_norm']
TWIN_DIFF_INPUT = 'x'
TWIN_INPUTS = ['x', 'ffn1_norm', 'ffn1_w_gate', 'ffn1_w_up', 'ffn1_w_down', 'mix_norm', 'w_in', 'conv_w', 'a_log', 'dt_bias', 'gdn_norm_w', 'q_norm_w', 'k_norm_w', 'rel_bias', 'w_out', 'ffn2_norm', 'ffn2_w_gate', 'ffn2_w_up', 'ffn2_w_down', 'final_norm', 'loss_target', 'm_ffn1_norm', 'm_ffn1_w_gate', 'm_ffn1_w_up', 'm_ffn1_w_down', 'm_mix_norm', 'm_w_in', 'm_conv_w', 'm_a_log', 'm_dt_bias', 'm_gdn_norm_w', 'm_q_norm_w', 'm_k_norm_w', 'm_rel_bias', 'm_w_out', 'm_ffn2_norm', 'm_ffn2_w_gate', 'm_ffn2_w_up', 'm_ffn2_w_down', 'm_final_norm', 'v_ffn1_norm', 'v_ffn1_w_gate', 'v_ffn1_w_up', 'v_ffn1_w_down', 'v_mix_norm', 'v_w_in', 'v_conv_w', 'v_a_log', 'v_dt_bias', 'v_gdn_norm_w', 'v_q_norm_w', 'v_k_norm_w', 'v_rel_bias', 'v_w_out', 'v_ffn2_norm', 'v_ffn2_w_gate', 'v_ffn2_w_up', 'v_ffn2_w_down', 'v_final_norm']
TWIN_OUTPUTS = ['loss', 'grad_x', 'grad_ffn1_norm', 'grad_ffn1_w_gate', 'grad_ffn1_w_up', 'grad_ffn1_w_down', 'grad_mix_norm', 'grad_w_in', 'grad_conv_w', 'grad_a_log', 'grad_dt_bias', 'grad_gdn_norm_w', 'grad_q_norm_w', 'grad_k_norm_w', 'grad_rel_bias', 'grad_w_out', 'grad_ffn2_norm', 'grad_ffn2_w_gate', 'grad_ffn2_w_up', 'grad_ffn2_w_down', 'grad_final_norm', 'delta_ffn1_norm', 'delta_ffn1_w_gate', 'delta_ffn1_w_up', 'delta_ffn1_w_down', 'delta_mix_norm', 'delta_w_in', 'delta_conv_w', 'delta_a_log', 'delta_dt_bias', 'delta_gdn_norm_w', 'delta_q_norm_w', 'delta_k_norm_w', 'delta_rel_bias', 'delta_w_out', 'delta_ffn2_norm', 'delta_ffn2_w_gate', 'delta_ffn2_w_up', 'delta_ffn2_w_down', 'delta_final_norm', 'new_m_ffn1_norm', 'new_m_ffn1_w_gate', 'new_m_ffn1_w_up', 'new_m_ffn1_w_down', 'new_m_mix_norm', 'new_m_w_in', 'new_m_conv_w', 'new_m_a_log', 'new_m_dt_bias', 'new_m_gdn_norm_w', 'new_m_q_norm_w', 'new_m_k_norm_w', 'new_m_rel_bias', 'new_m_w_out', 'new_m_ffn2_norm', 'new_m_ffn2_w_gate', 'new_m_ffn2_w_up', 'new_m_ffn2_w_down', 'new_m_final_norm', 'new_v_ffn1_norm', 'new_v_ffn1_w_gate', 'new_v_ffn1_w_up', 'new_v_ffn1_w_down', 'new_v_mix_norm', 'new_v_w_in', 'new_v_conv_w', 'new_v_a_log', 'new_v_dt_bias', 'new_v_gdn_norm_w', 'new_v_q_norm_w', 'new_v_k_norm_w', 'new_v_rel_bias', 'new_v_w_out', 'new_v_ffn2_norm', 'new_v_ffn2_w_gate', 'new_v_ffn2_w_up', 'new_v_ffn2_w_down', 'new_v_final_norm']
TWIN_LEAF_KINDS = {'loss': 'loss', 'grad_x': 'grad_x', 'grad_ffn1_norm': 'grad_w', 'grad_ffn1_w_gate': 'grad_w', 'grad_ffn1_w_up': 'grad_w', 'grad_ffn1_w_down': 'grad_w', 'grad_mix_norm': 'grad_w', 'grad_w_in': 'grad_w', 'grad_conv_w': 'grad_w', 'grad_a_log': 'grad_w', 'grad_dt_bias': 'grad_w', 'grad_gdn_norm_w': 'grad_w', 'grad_q_norm_w': 'grad_w', 'grad_k_norm_w': 'grad_w', 'grad_rel_bias': 'grad_w', 'grad_w_out': 'grad_w', 'grad_ffn2_norm': 'grad_w', 'grad_ffn2_w_gate': 'grad_w', 'grad_ffn2_w_up': 'grad_w', 'grad_ffn2_w_down': 'grad_w', 'grad_final_norm': 'grad_w', 'delta_ffn1_norm': 'delta_w', 'delta_ffn1_w_gate': 'delta_w', 'delta_ffn1_w_up': 'delta_w', 'delta_ffn1_w_down': 'delta_w', 'delta_mix_norm': 'delta_w', 'delta_w_in': 'delta_w', 'delta_conv_w': 'delta_w', 'delta_a_log': 'delta_w', 'delta_dt_bias': 'delta_w', 'delta_gdn_norm_w': 'delta_w', 'delta_q_norm_w': 'delta_w', 'delta_k_norm_w': 'delta_w', 'delta_rel_bias': 'delta_w', 'delta_w_out': 'delta_w', 'delta_ffn2_norm': 'delta_w', 'delta_ffn2_w_gate': 'delta_w', 'delta_ffn2_w_up': 'delta_w', 'delta_ffn2_w_down': 'delta_w', 'delta_final_norm': 'delta_w', 'new_m_ffn1_norm': 'new_m', 'new_m_ffn1_w_gate': 'new_m', 'new_m_ffn1_w_up': 'new_m', 'new_m_ffn1_w_down': 'new_m', 'new_m_mix_norm': 'new_m', 'new_m_w_in': 'new_m', 'new_m_conv_w': 'new_m', 'new_m_a_log': 'new_m', 'new_m_dt_bias': 'new_m', 'new_m_gdn_norm_w': 'new_m', 'new_m_q_norm_w': 'new_m', 'new_m_k_norm_w': 'new_m', 'new_m_rel_bias': 'new_m', 'new_m_w_out': 'new_m', 'new_m_ffn2_norm': 'new_m', 'new_m_ffn2_w_gate': 'new_m', 'new_m_ffn2_w_up': 'new_m', 'new_m_ffn2_w_down': 'new_m', 'new_m_final_norm': 'new_m', 'new_v_ffn1_norm': 'new_v', 'new_v_ffn1_w_gate': 'new_v', 'new_v_ffn1_w_up': 'new_v', 'new_v_ffn1_w_down': 'new_v', 'new_v_mix_norm': 'new_v', 'new_v_w_in': 'new_v', 'new_v_conv_w': 'new_v', 'new_v_a_log': 'new_v', 'new_v_dt_bias': 'new_v', 'new_v_gdn_norm_w': 'new_v', 'new_v_q_norm_w': 'new_v', 'new_v_k_norm_w': 'new_v', 'new_v_rel_bias': 'new_v', 'new_v_w_out': 'new_v', 'new_v_ffn2_norm': 'new_v', 'new_v_ffn2_w_gate': 'new_v', 'new_v_ffn2_w_up': 'new_v', 'new_v_ffn2_w_down': 'new_v', 'new_v_final_norm': 'new_v'}


def _forward(args):
    return _fwd_reference(*[args[k] for k in FWD_PARAMS])


def _output_shape():
    def fwd():
        inp = _fwd_setup_inputs(0)
        return _fwd_reference(*[inp[k] for k in FWD_PARAMS])
    out = _jax.eval_shape(fwd)
    return out.shape, out.dtype

N_MICROBATCH = 1
ADAM_LR = 0.001
ADAM_B1 = 0.9
ADAM_B2 = 0.999
ADAM_EPS = 1e-08
ADAM_WD = 0.01
ADAM_STEP = 10
PER_EXAMPLE_BATCH_AXIS = {'x': 0, 'loss_target': 0}
SHARED_INPUTS = []
_WEIGHT_DTYPES = {'ffn1_norm': _jnp.float32, 'ffn1_w_gate': _jnp.float32, 'ffn1_w_up': _jnp.float32, 'ffn1_w_down': _jnp.float32, 'mix_norm': _jnp.float32, 'w_in': _jnp.float32, 'conv_w': _jnp.float32, 'a_log': _jnp.float32, 'dt_bias': _jnp.float32, 'gdn_norm_w': _jnp.float32, 'q_norm_w': _jnp.float32, 'k_norm_w': _jnp.float32, 'rel_bias': _jnp.float32, 'w_out': _jnp.float32, 'ffn2_norm': _jnp.float32, 'ffn2_w_gate': _jnp.float32, 'ffn2_w_up': _jnp.float32, 'ffn2_w_down': _jnp.float32, 'final_norm': _jnp.float32}
MOMENT_SCALE = {'ffn1_norm': 1.203095e-01, 'ffn1_w_gate': 5.094935e-02, 'ffn1_w_up': 4.941951e-02, 'ffn1_w_down': 8.196639e-02, 'mix_norm': 1.671396e-01, 'w_in': 8.413506e-02, 'conv_w': 1.017612e-01, 'a_log': 5.225543e-01, 'dt_bias': 4.892912e-01, 'gdn_norm_w': 3.208200e-01, 'q_norm_w': 9.374396e-02, 'k_norm_w': 9.317271e-02, 'rel_bias': 4.294118e-02, 'w_out': 9.832471e-02, 'ffn2_norm': 9.434777e-02, 'ffn2_w_gate': 3.979578e-02, 'ffn2_w_up': 3.883627e-02, 'ffn2_w_down': 6.404988e-02, 'final_norm': 6.393961e+01}


def _to_microbatches(a, axis):
    t = _jnp.moveaxis(a, axis, 0)
    t = t.reshape((N_MICROBATCH, t.shape[0] // N_MICROBATCH) + t.shape[1:])
    return _jnp.moveaxis(t, 1, axis + 1)


def setup_inputs(seed: int = 0) -> dict:
    inp = _fwd_setup_inputs(seed)
    key = _jax.random.fold_in(_jax.random.key(seed), 7919)
    shape, _ = _output_shape()
    out = dict(inp)
    out["loss_target"] = _jax.random.normal(_jax.random.fold_in(key, 0), shape, _jnp.float32)
    for i, name in enumerate(TWIN_WEIGHTS):
        w = inp[name].astype(_jnp.float32)
        if MOMENT_SCALE is None:
            s = _jnp.sqrt(_jnp.mean(_jnp.square(w)) + 1e-30)
        else:
            s = MOMENT_SCALE[name]
        km, kv = _jax.random.split(_jax.random.fold_in(key, i + 1))
        out[name] = w
        out["m_" + name] = s * _jax.random.normal(km, w.shape, _jnp.float32)
        out["v_" + name] = (s * s) * _jax.random.uniform(kv, w.shape, _jnp.float32, 0.5, 1.5)
    if N_MICROBATCH > 1:
        for name, axis in PER_EXAMPLE_BATCH_AXIS.items():
            out[name] = _to_microbatches(out[name], axis)
    return {'x': out['x'], 'ffn1_norm': out['ffn1_norm'], 'ffn1_w_gate': out['ffn1_w_gate'], 'ffn1_w_up': out['ffn1_w_up'], 'ffn1_w_down': out['ffn1_w_down'], 'mix_norm': out['mix_norm'], 'w_in': out['w_in'], 'conv_w': out['conv_w'], 'a_log': out['a_log'], 'dt_bias': out['dt_bias'], 'gdn_norm_w': out['gdn_norm_w'], 'q_norm_w': out['q_norm_w'], 'k_norm_w': out['k_norm_w'], 'rel_bias': out['rel_bias'], 'w_out': out['w_out'], 'ffn2_norm': out['ffn2_norm'], 'ffn2_w_gate': out['ffn2_w_gate'], 'ffn2_w_up': out['ffn2_w_up'], 'ffn2_w_down': out['ffn2_w_down'], 'final_norm': out['final_norm'], 'loss_target': out['loss_target'], 'm_ffn1_norm': out['m_ffn1_norm'], 'm_ffn1_w_gate': out['m_ffn1_w_gate'], 'm_ffn1_w_up': out['m_ffn1_w_up'], 'm_ffn1_w_down': out['m_ffn1_w_down'], 'm_mix_norm': out['m_mix_norm'], 'm_w_in': out['m_w_in'], 'm_conv_w': out['m_conv_w'], 'm_a_log': out['m_a_log'], 'm_dt_bias': out['m_dt_bias'], 'm_gdn_norm_w': out['m_gdn_norm_w'], 'm_q_norm_w': out['m_q_norm_w'], 'm_k_norm_w': out['m_k_norm_w'], 'm_rel_bias': out['m_rel_bias'], 'm_w_out': out['m_w_out'], 'm_ffn2_norm': out['m_ffn2_norm'], 'm_ffn2_w_gate': out['m_ffn2_w_gate'], 'm_ffn2_w_up': out['m_ffn2_w_up'], 'm_ffn2_w_down': out['m_ffn2_w_down'], 'm_final_norm': out['m_final_norm'], 'v_ffn1_norm': out['v_ffn1_norm'], 'v_ffn1_w_gate': out['v_ffn1_w_gate'], 'v_ffn1_w_up': out['v_ffn1_w_up'], 'v_ffn1_w_down': out['v_ffn1_w_down'], 'v_mix_norm': out['v_mix_norm'], 'v_w_in': out['v_w_in'], 'v_conv_w': out['v_conv_w'], 'v_a_log': out['v_a_log'], 'v_dt_bias': out['v_dt_bias'], 'v_gdn_norm_w': out['v_gdn_norm_w'], 'v_q_norm_w': out['v_q_norm_w'], 'v_k_norm_w': out['v_k_norm_w'], 'v_rel_bias': out['v_rel_bias'], 'v_w_out': out['v_w_out'], 'v_ffn2_norm': out['v_ffn2_norm'], 'v_ffn2_w_gate': out['v_ffn2_w_gate'], 'v_ffn2_w_up': out['v_ffn2_w_up'], 'v_ffn2_w_down': out['v_ffn2_w_down'], 'v_final_norm': out['v_final_norm']}


def _loss(weights, diff, rest, loss_target):
    with _jax.named_scope("forward"):
        args = {**rest, TWIN_DIFF_INPUT: diff, **{k: w.astype(_WEIGHT_DTYPES[k]) for k, w in weights.items()}}
        y = _forward(args)
    with _jax.named_scope("loss_head"):
        err = _jnp.square(y.astype(_jnp.float32) - loss_target)
        return 0.5 * _jnp.sum(_jnp.mean(err, axis=-1)) if err.ndim else 0.5 * err


def _adamw(w, g, m, v):
    m = ADAM_B1 * m + (1.0 - ADAM_B1) * g
    v = ADAM_B2 * v + (1.0 - ADAM_B2) * _jnp.square(g)
    m_hat = m / (1.0 - ADAM_B1 ** ADAM_STEP)
    v_hat = v / (1.0 - ADAM_B2 ** ADAM_STEP)
    delta = -ADAM_LR * (m_hat / (_jnp.sqrt(v_hat) + ADAM_EPS) + ADAM_WD * w)
    return delta, m, v


def reference(x, ffn1_norm, ffn1_w_gate, ffn1_w_up, ffn1_w_down, mix_norm, w_in, conv_w, a_log, dt_bias, gdn_norm_w, q_norm_w, k_norm_w, rel_bias, w_out, ffn2_norm, ffn2_w_gate, ffn2_w_up, ffn2_w_down, final_norm, loss_target, m_ffn1_norm, m_ffn1_w_gate, m_ffn1_w_up, m_ffn1_w_down, m_mix_norm, m_w_in, m_conv_w, m_a_log, m_dt_bias, m_gdn_norm_w, m_q_norm_w, m_k_norm_w, m_rel_bias, m_w_out, m_ffn2_norm, m_ffn2_w_gate, m_ffn2_w_up, m_ffn2_w_down, m_final_norm, v_ffn1_norm, v_ffn1_w_gate, v_ffn1_w_up, v_ffn1_w_down, v_mix_norm, v_w_in, v_conv_w, v_a_log, v_dt_bias, v_gdn_norm_w, v_q_norm_w, v_k_norm_w, v_rel_bias, v_w_out, v_ffn2_norm, v_ffn2_w_gate, v_ffn2_w_up, v_ffn2_w_down, v_final_norm):
    given = dict(x=x, ffn1_norm=ffn1_norm, ffn1_w_gate=ffn1_w_gate, ffn1_w_up=ffn1_w_up, ffn1_w_down=ffn1_w_down, mix_norm=mix_norm, w_in=w_in, conv_w=conv_w, a_log=a_log, dt_bias=dt_bias, gdn_norm_w=gdn_norm_w, q_norm_w=q_norm_w, k_norm_w=k_norm_w, rel_bias=rel_bias, w_out=w_out, ffn2_norm=ffn2_norm, ffn2_w_gate=ffn2_w_gate, ffn2_w_up=ffn2_w_up, ffn2_w_down=ffn2_w_down, final_norm=final_norm, loss_target=loss_target, m_ffn1_norm=m_ffn1_norm, m_ffn1_w_gate=m_ffn1_w_gate, m_ffn1_w_up=m_ffn1_w_up, m_ffn1_w_down=m_ffn1_w_down, m_mix_norm=m_mix_norm, m_w_in=m_w_in, m_conv_w=m_conv_w, m_a_log=m_a_log, m_dt_bias=m_dt_bias, m_gdn_norm_w=m_gdn_norm_w, m_q_norm_w=m_q_norm_w, m_k_norm_w=m_k_norm_w, m_rel_bias=m_rel_bias, m_w_out=m_w_out, m_ffn2_norm=m_ffn2_norm, m_ffn2_w_gate=m_ffn2_w_gate, m_ffn2_w_up=m_ffn2_w_up, m_ffn2_w_down=m_ffn2_w_down, m_final_norm=m_final_norm, v_ffn1_norm=v_ffn1_norm, v_ffn1_w_gate=v_ffn1_w_gate, v_ffn1_w_up=v_ffn1_w_up, v_ffn1_w_down=v_ffn1_w_down, v_mix_norm=v_mix_norm, v_w_in=v_w_in, v_conv_w=v_conv_w, v_a_log=v_a_log, v_dt_bias=v_dt_bias, v_gdn_norm_w=v_gdn_norm_w, v_q_norm_w=v_q_norm_w, v_k_norm_w=v_k_norm_w, v_rel_bias=v_rel_bias, v_w_out=v_w_out, v_ffn2_norm=v_ffn2_norm, v_ffn2_w_gate=v_ffn2_w_gate, v_ffn2_w_up=v_ffn2_w_up, v_ffn2_w_down=v_ffn2_w_down, v_final_norm=v_final_norm)
    weights = {n: given[n] for n in TWIN_WEIGHTS}
    shared = {n: given[n] for n in SHARED_INPUTS}
    per_example = {n: given[n] for n in ['x']}
    grad_fn = _jax.value_and_grad(_loss, argnums=(0, 1))

    def one_microbatch(ex, loss_target):
        ex = dict(ex)
        diff = ex.pop(TWIN_DIFF_INPUT)
        return grad_fn(weights, diff, {**shared, **ex}, loss_target)

    if N_MICROBATCH == 1:
        loss, (grad_w, grad_x) = one_microbatch(per_example, given["loss_target"])
    else:
        def body(carry, xs):
            loss_sum, grad_sum = carry
            l_k, (gw_k, gx_k) = one_microbatch(xs[0], xs[1])
            with _jax.named_scope("update"):
                return (loss_sum + l_k, _jax.tree.map(_jnp.add, grad_sum, gw_k)), gx_k

        init = (_jnp.zeros((), _jnp.float32), _jax.tree.map(_jnp.zeros_like, weights))
        (loss, grad_w), grad_x = _jax.lax.scan(body, init, (per_example, given["loss_target"]))
    with _jax.named_scope("update"):
        delta_w, new_m, new_v = {}, {}, {}
        for n in TWIN_WEIGHTS:
            delta_w[n], new_m[n], new_v[n] = _adamw(weights[n], grad_w[n], given["m_" + n], given["v_" + n])
    return (loss, grad_x, *[grad_w[n] for n in TWIN_WEIGHTS], *[delta_w[n] for n in TWIN_WEIGHTS],
            *[new_m[n] for n in TWIN_WEIGHTS], *[new_v[n] for n in TWIN_WEIGHTS])
```

```python
import functools
import math

import numpy as np
import jax
import jax.numpy as jnp
from jax import lax
from jax.experimental import pallas as pl
from jax.experimental.pallas import tpu as pltpu

F32 = jnp.float32
BF16 = jnp.bfloat16

D_MODEL = 1024
D_FF = 2816
GDN_HEADS = 4
GDN_HEAD_DIM = 128
GDN_WIDTH = 512
CONV_WIDTH = 5
CHUNK = 64
SWA_HEADS = 8
SWA_HEAD_DIM = 64
SWA_WIDTH = 512
DILATION_PATTERNS = ((128, 1), (512, 4), (2048, 16))
REL_BUCKETS = 32
REL_MAX_DISTANCE = 1024
EPS = 1e-6
NEG_BIG = -1e30
N_DEV = 8

ADAM_LR = 0.001
ADAM_B1 = 0.9
ADAM_B2 = 0.999
ADAM_EPS = 1e-08
ADAM_WD = 0.01
ADAM_STEP = 10

QKV_A = 3 * GDN_WIDTH
OFF_Z = QKV_A
OFF_B = OFF_Z + GDN_WIDTH
OFF_AB = OFF_B + 3 * SWA_WIDTH
N_PAD = OFF_AB + 128
N_IN = 3600

V7X_VMEM_LIMIT_BYTES = 56 * 1024 * 1024
LANE = 128
ATT_BQ = 128
ATT_HALO = 64
CONV_ROWS = 256

NN = (((1,), (0,)), ((), ()))
NT = (((1,), (1,)), ((), ()))
TN = (((0,), (0,)), ((), ()))


def _params(*sem):
    return pltpu.CompilerParams(dimension_semantics=sem, vmem_limit_bytes=V7X_VMEM_LIMIT_BYTES)


def _dot(a, b, dn=NN):
    return lax.dot_general(a.astype(BF16), b.astype(BF16), dn, preferred_element_type=F32)


def _dot_hi(a, b, dn=NN):
    return lax.dot_general(a, b, dn, precision=lax.Precision.HIGHEST, preferred_element_type=F32)


def _sigmoid(x):
    return 1.0 / (1.0 + jnp.exp(-x))


def _matmul(pairs, *, ta=False, tb=False, out_dtype=F32, tm, tn, tk, name, res=None, alpha=None):
    a0, b0 = pairs[0]
    m = a0.shape[1] if ta else a0.shape[0]
    k = a0.shape[0] if ta else a0.shape[1]
    n = b0.shape[0] if tb else b0.shape[1]
    tm, tn, tk = min(tm, m), min(tn, n), min(tk, k)
    assert m % tm == 0 and n % tn == 0 and k % tk == 0, (name, m, n, k, tm, tn, tk)
    nk = k // tk
    npairs = len(pairs)
    dn = (((0 if ta else 1,), (1 if tb else 0,)), ((), ()))

    def body(*refs):
        ins = refs[:2 * npairs]
        pos = 2 * npairs
        r_ref = None
        if res is not None:
            r_ref = refs[pos]
            pos += 1
        o_ref, acc = refs[pos], refs[pos + 1]
        kk = pl.program_id(2)

        @pl.when(kk == 0)
        def _():
            acc[...] = jnp.zeros_like(acc)

        t = None
        for p in range(npairs):
            d = _dot(ins[2 * p][...], ins[2 * p + 1][...], dn)
            t = d if t is None else t + d
        acc[...] += t

        @pl.when(kk == nk - 1)
        def _():
            r = acc[...]
            if alpha is not None:
                r = r * alpha
            if r_ref is not None:
                r = r_ref[...] + r
            o_ref[...] = r.astype(out_dtype)

    a_spec = pl.BlockSpec((tk, tm), lambda i, j, kk: (kk, i)) if ta else pl.BlockSpec((tm, tk), lambda i, j, kk: (i, kk))
    b_spec = pl.BlockSpec((tn, tk), lambda i, j, kk: (j, kk)) if tb else pl.BlockSpec((tk, tn), lambda i, j, kk: (kk, j))
    o_spec = pl.BlockSpec((tm, tn), lambda i, j, kk: (i, j))
    in_specs = [a_spec, b_spec] * npairs + ([o_spec] if res is not None else [])
    args = [t for pr in pairs for t in pr] + ([res] if res is not None else [])
    return pl.pallas_call(
        body, name=name, grid=(m // tm, n // tn, nk), in_specs=in_specs, out_specs=o_spec,
        out_shape=jax.ShapeDtypeStruct((m, n), out_dtype), scratch_shapes=[pltpu.VMEM((tm, tn), F32)],
        compiler_params=_params("parallel", "parallel", "arbitrary"),
    )(*args)


def _rms_fwd(x, w, name):
    s, d = x.shape
    tm = min(512, s)

    def body(x_ref, w_ref, n_ref, r_ref):
        xv = x_ref[...]
        r = lax.rsqrt(jnp.mean(xv * xv, axis=-1, keepdims=True) + EPS)
        n_ref[...] = (xv * r * w_ref[...]).astype(BF16)
        r_ref[...] = r

    return pl.pallas_call(
        body, name=name, grid=(s // tm,),
        in_specs=[pl.BlockSpec((tm, d), lambda i: (i, 0)), pl.BlockSpec((1, d), lambda i: (0, 0))],
        out_specs=[pl.BlockSpec((tm, d), lambda i: (i, 0)), pl.BlockSpec((tm, 1), lambda i: (i, 0))],
        out_shape=[jax.ShapeDtypeStruct((s, d), BF16), jax.ShapeDtypeStruct((s, 1), F32)],
        compiler_params=_params("parallel"),
    )(x, w)


def _rms_bwd(dn, x, r, w, dres, name):
    s, d = x.shape
    tm = min(512, s)

    def body(dn_ref, x_ref, r_ref, w_ref, dres_ref, dx_ref, dw_ref):
        @pl.when(pl.program_id(0) == 0)
        def _():
            dw_ref[...] = jnp.zeros_like(dw_ref)

        rv = r_ref[...]
        xhat = x_ref[...] * rv
        g = dn_ref[...]
        t = g * w_ref[...]
        dx_ref[...] = dres_ref[...] + rv * (t - xhat * jnp.mean(t * xhat, axis=-1, keepdims=True))
        dw_ref[...] += jnp.sum(g * xhat, axis=0, keepdims=True)

    row = pl.BlockSpec((tm, d), lambda i: (i, 0))
    vec = pl.BlockSpec((1, d), lambda i: (0, 0))
    return pl.pallas_call(
        body, name=name, grid=(s // tm,),
        in_specs=[row, row, pl.BlockSpec((tm, 1), lambda i: (i, 0)), vec, row],
        out_specs=[row, vec],
        out_shape=[jax.ShapeDtypeStruct((s, d), F32), jax.ShapeDtypeStruct((1, d), F32)],
        compiler_params=_params("arbitrary"),
    )(dn, x, r, w, dres)


def _final_loss(x3, wf, tgt):
    s, d = x3.shape
    tm = min(512, s)

    def body(x_ref, w_ref, t_ref, loss_ref, dx_ref, dw_ref):
        @pl.when(pl.program_id(0) == 0)
        def _():
            dw_ref[...] = jnp.zeros_like(dw_ref)
            loss_ref[...] = jnp.zeros_like(loss_ref)

        xv = x_ref[...]
        wv = w_ref[...]
        r = lax.rsqrt(jnp.mean(xv * xv, axis=-1, keepdims=True) + EPS)
        xhat = xv * r
        e = xhat * wv - t_ref[...]
        part = 0.5 * jnp.sum(jnp.mean(e * e, axis=-1, keepdims=True), axis=0, keepdims=True)
        loss_ref[...] += jnp.broadcast_to(part, loss_ref.shape)
        dy = e * (1.0 / d)
        dw_ref[...] += jnp.sum(dy * xhat, axis=0, keepdims=True)
        t = dy * wv
        dx_ref[...] = r * (t - xhat * jnp.mean(t * xhat, axis=-1, keepdims=True))

    row = pl.BlockSpec((tm, d), lambda i: (i, 0))
    vec = pl.BlockSpec((1, d), lambda i: (0, 0))
    return pl.pallas_call(
        body, name="final_loss", grid=(s // tm,),
        in_specs=[row, vec, row],
        out_specs=[pl.BlockSpec((1, LANE), lambda i: (0, 0)), row, vec],
        out_shape=[jax.ShapeDtypeStruct((1, LANE), F32), jax.ShapeDtypeStruct((s, d), F32),
                   jax.ShapeDtypeStruct((1, d), F32)],
        compiler_params=_params("arbitrary"),
    )(x3, wf, tgt)


def _ffn_up(n, wg, wu, name):
    s, d = n.shape
    f = wg.shape[1]
    tm, tn = min(512, s), f // 2

    def body(n_ref, wg_ref, wu_ref, g_ref, u_ref, a_ref):
        nv = n_ref[...]
        g = _dot(nv, wg_ref[...])
        u = _dot(nv, wu_ref[...])
        g_ref[...] = g
        u_ref[...] = u
        a_ref[...] = (g * _sigmoid(g) * u).astype(BF16)

    o = pl.BlockSpec((tm, tn), lambda i, j: (i, j))
    wspec = pl.BlockSpec((d, tn), lambda i, j: (0, j))
    return pl.pallas_call(
        body, name=name, grid=(s // tm, f // tn),
        in_specs=[pl.BlockSpec((tm, d), lambda i, j: (i, 0)), wspec, wspec],
        out_specs=[o, o, o],
        out_shape=[jax.ShapeDtypeStruct((s, f), F32), jax.ShapeDtypeStruct((s, f), F32),
                   jax.ShapeDtypeStruct((s, f), BF16)],
        compiler_params=_params("parallel", "parallel"),
    )(n, wg, wu)


def _ffn_dact(dx, wd, g, u, name):
    s, d = dx.shape
    f = wd.shape[0]
    tm, tn = min(512, s), f // 2

    def body(dx_ref, wd_ref, g_ref, u_ref, dg_ref, du_ref):
        da = 0.5 * _dot(dx_ref[...], wd_ref[...], NT)
        gv = g_ref[...]
        sg = _sigmoid(gv)
        du_ref[...] = (da * gv * sg).astype(BF16)
        dg_ref[...] = (da * u_ref[...] * (sg * (1.0 + gv * (1.0 - sg)))).astype(BF16)

    o = pl.BlockSpec((tm, tn), lambda i, j: (i, j))
    return pl.pallas_call(
        body, name=name, grid=(s // tm, f // tn),
        in_specs=[pl.BlockSpec((tm, d), lambda i, j: (i, 0)), pl.BlockSpec((tn, d), lambda i, j: (j, 0)), o, o],
        out_specs=[o, o],
        out_shape=[jax.ShapeDtypeStruct((s, f), BF16), jax.ShapeDtypeStruct((s, f), BF16)],
        compiler_params=_params("parallel", "parallel"),
    )(dx, wd, g, u)


def _ffn_forward(x, norm_w, wg, wu, wd, tag):
    n, r = _rms_fwd(x, norm_w, f"{tag}_norm")
    g, u, a = _ffn_up(n, wg, wu, f"{tag}_up")
    y = _matmul([(a, wd)], tm=512, tn=1024, tk=1408, name=f"{tag}_down", res=x, alpha=0.5)
    return y, (n, r, g, u, a)


def _ffn_backward(dy, x, norm_w, wg, wu, wd, saved, tag):
    n, r, g, u, a = saved
    dwd = _matmul([(a, dy)], ta=True, tm=1408, tn=1024, tk=512, name=f"{tag}_dwd", alpha=0.5)
    dg, du = _ffn_dact(dy, wd, g, u, f"{tag}_dact")
    dwg = _matmul([(n, dg)], ta=True, tm=1024, tn=1408, tk=512, name=f"{tag}_dwg")
    dwu = _matmul([(n, du)], ta=True, tm=1024, tn=1408, tk=512, name=f"{tag}_dwu")
    dn = _matmul([(dg, wg), (du, wu)], tb=True, tm=512, tn=1024, tk=1408, name=f"{tag}_dn")
    dx, dnorm = _rms_bwd(dn, x, r, norm_w, dy, f"{tag}_dnorm")
    return dx, dnorm, dwg, dwu, dwd


Q_SCALE = GDN_HEAD_DIM ** -0.5
CONV_HALO = 8


def _conv_taps(win, w_ref, rows, sign):
    n = rows + 2 * CONV_HALO
    acc = None
    for t in range(CONV_WIDTH):
        o = sign * (t - CONV_WIDTH // 2)
        sh = win if o == 0 else pltpu.roll(win, (-o) % n, 0)
        term = sh[CONV_HALO:CONV_HALO + rows] * w_ref[t:t + 1, :]
        acc = term if acc is None else acc + term
    return acc


def _gdn_conv_fwd(p_pad, conv_wt):
    s = p_pad.shape[0]
    rows = min(CONV_ROWS, s)
    nblk = QKV_A // LANE

    def body(p_ref, w_ref, c_ref, y_ref, pad):
        j = pl.program_id(0)
        zeros = jnp.zeros((CONV_HALO, LANE), F32)
        pad[0:CONV_HALO, :] = zeros
        pad[CONV_HALO + s:2 * CONV_HALO + s, :] = zeros
        pad[CONV_HALO:CONV_HALO + s, :] = p_ref[...]

        def chunk(ci, carry):
            b = pl.multiple_of(ci * rows, rows)
            win = pad[pl.ds(b, rows + 2 * CONV_HALO), :]
            c = _conv_taps(win, w_ref, rows, 1)
            c_ref[pl.ds(b, rows), :] = c
            act = c * _sigmoid(c)
            nrm = lax.rsqrt(jnp.sum(act * act, axis=-1, keepdims=True) + EPS)
            mult = jnp.where(j < GDN_HEADS, nrm * Q_SCALE, jnp.where(j < 2 * GDN_HEADS, nrm, 1.0))
            y_ref[pl.ds(b, rows), :] = act * mult
            return carry

        lax.fori_loop(0, s // rows, chunk, 0)

    col = pl.BlockSpec((s, LANE), lambda j: (0, j))
    return pl.pallas_call(
        body, name="gdn_conv_fwd", grid=(nblk,),
        in_specs=[col, pl.BlockSpec((8, LANE), lambda j: (0, j))],
        out_specs=[col, col],
        out_shape=[jax.ShapeDtypeStruct((s, QKV_A), F32), jax.ShapeDtypeStruct((s, QKV_A), F32)],
        scratch_shapes=[pltpu.VMEM((s + 2 * CONV_HALO, LANE), F32)],
        compiler_params=_params("parallel"),
    )(p_pad, conv_wt)


def _gdn_conv_bwd(dy, c_pre, p_pad, conv_wt):
    s = p_pad.shape[0]
    rows = min(CONV_ROWS, s)
    nblk = QKV_A // LANE

    def body(dy_ref, c_ref, p_ref, w_ref, dp_ref, dw_ref, ppad, dcpad):
        j = pl.program_id(0)
        zeros = jnp.zeros((CONV_HALO, LANE), F32)
        for buf in (ppad, dcpad):
            buf[0:CONV_HALO, :] = zeros
            buf[CONV_HALO + s:2 * CONV_HALO + s, :] = zeros
        ppad[CONV_HALO:CONV_HALO + s, :] = p_ref[...]

        def act_bwd(ci, carry):
            b = pl.multiple_of(ci * rows, rows)
            c = c_ref[pl.ds(b, rows), :]
            g = dy_ref[pl.ds(b, rows), :]
            sg = _sigmoid(c)
            act = c * sg
            nrm = lax.rsqrt(jnp.sum(act * act, axis=-1, keepdims=True) + EPS)
            yh = act * nrm
            scale = jnp.where(j < GDN_HEADS, Q_SCALE, 1.0)
            dact_qk = (scale * nrm) * (g - yh * jnp.sum(g * yh, axis=-1, keepdims=True))
            dact = jnp.where(j < 2 * GDN_HEADS, dact_qk, g)
            dcpad[pl.ds(pl.multiple_of(b + CONV_HALO, CONV_HALO), rows), :] = dact * (sg * (1.0 + c * (1.0 - sg)))
            return carry

        lax.fori_loop(0, s // rows, act_bwd, 0)
        tap = lax.broadcasted_iota(jnp.int32, (8, LANE), 0)

        def taps_bwd(ci, dw):
            b = pl.multiple_of(ci * rows, rows)
            dcw = dcpad[pl.ds(b, rows + 2 * CONV_HALO), :]
            dp_ref[pl.ds(b, rows), :] = _conv_taps(dcw, w_ref, rows, -1)
            pw = ppad[pl.ds(b, rows + 2 * CONV_HALO), :]
            dc = dcw[CONV_HALO:CONV_HALO + rows]
            n = rows + 2 * CONV_HALO
            for t in range(CONV_WIDTH):
                o = t - CONV_WIDTH // 2
                sh = pw if o == 0 else pltpu.roll(pw, (-o) % n, 0)
                row = jnp.sum(dc * sh[CONV_HALO:CONV_HALO + rows], axis=0, keepdims=True)
                dw = dw + jnp.where(tap == t, row, 0.0)
            return dw

        dw_ref[...] = lax.fori_loop(0, s // rows, taps_bwd, jnp.zeros((8, LANE), F32))

    col = pl.BlockSpec((s, LANE), lambda j: (0, j))
    wspec = pl.BlockSpec((8, LANE), lambda j: (0, j))
    return pl.pallas_call(
        body, name="gdn_conv_bwd", grid=(nblk,),
        in_specs=[col, col, col, wspec],
        out_specs=[col, wspec],
        out_shape=[jax.ShapeDtypeStruct((s, QKV_A), F32), jax.ShapeDtypeStruct((8, QKV_A), F32)],
        scratch_shapes=[pltpu.VMEM((s + 2 * CONV_HALO, LANE), F32), pltpu.VMEM((s + 2 * CONV_HALO, LANE), F32)],
        compiler_params=_params("parallel"),
    )(dy, c_pre, p_pad, conv_wt)


def _softplus(x):
    return jnp.maximum(x, 0.0) + jnp.log(1.0 + jnp.exp(-jnp.abs(x)))


def _gdn_gates_fwd(p_pad, alog_row, dt_row):
    s = p_pad.shape[0]
    tm = min(1024, s)

    def body(p_ref, al_ref, dt_ref, o_ref):
        x = p_ref[...]
        lane = lax.broadcasted_iota(jnp.int32, x.shape, 1)
        g = -jnp.exp(al_ref[...]) * _softplus(x + dt_ref[...])
        o_ref[...] = jnp.where(lane < 8, g, jnp.where(lane < 16, _sigmoid(x), 0.0))

    vec = pl.BlockSpec((1, LANE), lambda i: (0, 0))
    return pl.pallas_call(
        body, name="gdn_gates_fwd", grid=(s // tm,),
        in_specs=[pl.BlockSpec((tm, LANE), lambda i: (i, OFF_AB // LANE)), vec, vec],
        out_specs=pl.BlockSpec((tm, LANE), lambda i: (i, 0)),
        out_shape=jax.ShapeDtypeStruct((s, LANE), F32),
        compiler_params=_params("parallel"),
    )(p_pad, alog_row, dt_row)


def _gdn_gates_bwd(dgb_f, dgb_r, p_pad, gb, alog_row, dt_row):
    s = p_pad.shape[0]
    tm = min(1024, s)

    def body(df_ref, dr_ref, p_ref, gb_ref, al_ref, dt_ref, dp_ref, sum_ref):
        @pl.when(pl.program_id(0) == 0)
        def _():
            sum_ref[...] = jnp.zeros_like(sum_ref)

        x = p_ref[...]
        gbv = gb_ref[...]
        dgb = df_ref[...] + dr_ref[...]
        lane = lax.broadcasted_iota(jnp.int32, x.shape, 1)
        da = dgb * (-jnp.exp(al_ref[...])) * _sigmoid(x + dt_ref[...])
        db = dgb * gbv * (1.0 - gbv)
        dp_ref[...] = jnp.where(lane < 8, da, jnp.where(lane < 16, db, 0.0))
        row = lax.broadcasted_iota(jnp.int32, (8, LANE), 0)
        lane8 = lax.broadcasted_iota(jnp.int32, (8, LANE), 1)
        d_alog = jnp.sum(dgb * gbv, axis=0, keepdims=True)
        d_dt = jnp.sum(da, axis=0, keepdims=True)
        upd = jnp.where(row == 0, d_alog, jnp.where(row == 1, d_dt, 0.0))
        sum_ref[...] += jnp.where(lane8 < 8, upd, 0.0)

    vec = pl.BlockSpec((1, LANE), lambda i: (0, 0))
    blk = pl.BlockSpec((tm, LANE), lambda i: (i, 0))
    return pl.pallas_call(
        body, name="gdn_gates_bwd", grid=(s // tm,),
        in_specs=[blk, blk, pl.BlockSpec((tm, LANE), lambda i: (i, OFF_AB // LANE)), blk, vec, vec],
        out_specs=[blk, pl.BlockSpec((8, LANE), lambda i: (0, 0))],
        out_shape=[jax.ShapeDtypeStruct((s, LANE), F32), jax.ShapeDtypeStruct((8, LANE), F32)],
        compiler_params=_params("arbitrary"),
    )(dgb_f, dgb_r, p_pad, gb, alog_row, dt_row)


def _chunk_masks(rev):
    row = lax.broadcasted_iota(jnp.int32, (CHUNK, CHUNK), 0)
    col = lax.broadcasted_iota(jnp.int32, (CHUNK, CHUNK), 1)
    le = (col >= row) if rev else (col <= row)
    strict = (col > row) if rev else (col < row)
    return le, strict, row == col


def _chunk_common(q, k, v, g, beta, gc, masks):
    le, strict, eye = masks
    gc_row = _dot_hi(jnp.ones((CHUNK, CHUNK), F32), jnp.where(eye, gc, 0.0))
    decay = jnp.where(le, jnp.exp(jnp.where(le, gc - gc_row, 0.0)), 0.0)
    eg = jnp.exp(gc)
    gl = jnp.sum(g, axis=0, keepdims=True)
    kb = k * beta
    vb = v * beta
    kbeg = kb * eg
    lm = jnp.where(strict, _dot(kb, k, NT) * decay, 0.0)
    intra = _dot(q, k, NT) * decay
    qg = q * eg
    edec = jnp.exp(gl - gc)
    kdec = k * edec
    return dict(decay=decay, eg=eg, gl=gl, kb=kb, vb=vb, kbeg=kbeg, lm=lm, intra=intra, qg=qg, edec=edec, kdec=kdec)


def _unit_lower_inverse(lm, eye):
    x = -lm
    t = eye.astype(F32) + x
    p = x
    for _ in range(5):
        p = _dot_hi(p, p)
        t = t + _dot_hi(t, p)
    return t


def _gate_lanes(rev, h):
    d = 1 if rev else 0
    return d * GDN_HEADS + h, 8 + d * GDN_HEADS + h


def _delta_fwd(y, gb, rev):
    s = y.shape[0]
    nc = s // CHUNK
    hd = GDN_HEAD_DIM

    def chunk_of(n):
        return nc - 1 - n if rev else n

    def body(q_ref, k_ref, v_ref, gb_ref, o_ref, s_all, t_all, state):
        @pl.when(pl.program_id(0) == 0)
        def _():
            state[...] = jnp.zeros_like(state)

        masks = _chunk_masks(rev)
        gbv = gb_ref[...]
        gcm = _dot_hi(masks[0].astype(F32), gbv)
        for h in range(GDN_HEADS):
            gi, bi = _gate_lanes(rev, h)
            sl = slice(h * hd, (h + 1) * hd)
            q, k, v = q_ref[:, sl], k_ref[:, sl], v_ref[:, sl]
            g, beta, gc = gbv[:, gi:gi + 1], gbv[:, bi:bi + 1], gcm[:, gi:gi + 1]
            cm = _chunk_common(q, k, v, g, beta, gc, masks)
            tinv = _unit_lower_inverse(cm["lm"], masks[2])
            u = _dot(tinv, cm["vb"])
            w = _dot(tinv, cm["kbeg"])
            st = state[h]
            v_new = u - _dot(w, st)
            o_ref[:, sl] = _dot(cm["qg"], st) + _dot(cm["intra"], v_new)
            s_all[0, h] = st
            t_all[0, h] = tinv
            state[h] = st * jnp.exp(cm["gl"]) + _dot(cm["kdec"], v_new, TN)

    def col(j):
        return pl.BlockSpec((CHUNK, GDN_WIDTH), lambda n: (chunk_of(n), j))

    return pl.pallas_call(
        body, name="delta_fwd_r" if rev else "delta_fwd_f", grid=(nc,),
        in_specs=[col(0), col(1), col(2), pl.BlockSpec((CHUNK, LANE), lambda n: (chunk_of(n), 0))],
        out_specs=[pl.BlockSpec((CHUNK, GDN_WIDTH), lambda n: (chunk_of(n), 0)),
                   pl.BlockSpec((1, GDN_HEADS, hd, hd), lambda n: (chunk_of(n), 0, 0, 0)),
                   pl.BlockSpec((1, GDN_HEADS, CHUNK, CHUNK), lambda n: (chunk_of(n), 0, 0, 0))],
        out_shape=[jax.ShapeDtypeStruct((s, GDN_WIDTH), F32),
                   jax.ShapeDtypeStruct((nc, GDN_HEADS, hd, hd), F32),
                   jax.ShapeDtypeStruct((nc, GDN_HEADS, CHUNK, CHUNK), F32)],
        scratch_shapes=[pltpu.VMEM((GDN_HEADS, hd, hd), F32)],
        compiler_params=_params("arbitrary"),
    )(y, y, y, gb)


def _delta_bwd(y, gb, do, s_all, t_all, rev):
    s = y.shape[0]
    nc = s // CHUNK
    hd = GDN_HEAD_DIM

    def chunk_of(n):
        return n if rev else nc - 1 - n

    def body(q_ref, k_ref, v_ref, gb_ref, do_ref, s_ref, t_ref, dq_ref, dk_ref, dv_ref, dgb_ref, dstate):
        @pl.when(pl.program_id(0) == 0)
        def _():
            dstate[...] = jnp.zeros_like(dstate)

        masks = _chunk_masks(rev)
        le, strict, _ = masks
        le_t = _chunk_masks(not rev)[0].astype(F32)
        gbv = gb_ref[...]
        gcm = _dot_hi(le.astype(F32), gbv)
        lane = lax.broadcasted_iota(jnp.int32, (CHUNK, LANE), 1)
        ones_cl = jnp.ones((CHUNK, LANE), F32)
        dgc_tile = jnp.zeros((CHUNK, LANE), F32)
        rest_tile = jnp.zeros((CHUNK, LANE), F32)
        for h in range(GDN_HEADS):
            gi, bi = _gate_lanes(rev, h)
            sl = slice(h * hd, (h + 1) * hd)
            q, k, v = q_ref[:, sl], k_ref[:, sl], v_ref[:, sl]
            g, beta, gc = gbv[:, gi:gi + 1], gbv[:, bi:bi + 1], gcm[:, gi:gi + 1]
            cm = _chunk_common(q, k, v, g, beta, gc, masks)
            tinv = t_ref[0, h]
            st = s_ref[0, h]
            ds_out = dstate[h]
            dov = do_ref[:, sl]
            u = _dot(tinv, cm["vb"])
            w = _dot(tinv, cm["kbeg"])
            v_new = u - _dot(w, st)
            egl = jnp.exp(cm["gl"])
            d_qg = _dot(dov, st, NT)
            d_intra = _dot(dov, v_new, NT)
            dv_new = _dot(cm["intra"], dov, TN) + _dot(cm["kdec"], ds_out)
            d_kdec = _dot(v_new, ds_out, NT)
            dstate[h] = _dot(cm["qg"], dov, TN) + egl * ds_out - _dot(w, dv_new, TN)
            dgl = egl * jnp.sum(jnp.sum(st * ds_out, axis=1, keepdims=True), axis=0, keepdims=True)
            dw = -_dot(dv_new, st, NT)
            dvb = _dot(tinv, dv_new, TN)
            dkbeg = _dot(tinv, dw, TN)
            dlm = jnp.where(strict, -(_dot(dvb, u, NT) + _dot(dkbeg, w, NT)), 0.0)
            d_a = dlm * cm["decay"]
            d_qk = d_intra * cm["decay"]
            e = dlm * cm["lm"] + d_intra * cm["intra"]
            dgc = jnp.sum(e, axis=1, keepdims=True) - _dot_hi(e, ones_cl, TN)[:, 0:1]
            dkb = _dot(d_a, k) + dkbeg * cm["eg"]
            dk = _dot(d_a, cm["kb"], TN) + _dot(d_qk, q, TN)
            dq = _dot(d_qk, k) + d_qg * cm["eg"]
            dgc = dgc + jnp.sum(d_qg * cm["qg"], axis=1, keepdims=True)
            dgc = dgc + jnp.sum(dkbeg * cm["kbeg"], axis=1, keepdims=True)
            tdec = jnp.sum(d_kdec * cm["kdec"], axis=1, keepdims=True)
            dk = dk + d_kdec * cm["edec"] + dkb * beta
            dgc = dgc - tdec
            dgl = dgl + jnp.sum(tdec, axis=0, keepdims=True)
            dbeta = jnp.sum(dvb * v, axis=1, keepdims=True) + jnp.sum(dkb * k, axis=1, keepdims=True)
            dq_ref[:, sl] = dq
            dk_ref[:, sl] = dk
            dv_ref[:, sl] = dvb * beta
            dgc_tile = dgc_tile + jnp.where(lane == gi, dgc, 0.0)
            rest_tile = rest_tile + jnp.where(lane == gi, dgl, 0.0) + jnp.where(lane == bi, dbeta, 0.0)
        dgb_ref[...] = _dot_hi(le_t, dgc_tile) + rest_tile

    def col(j):
        return pl.BlockSpec((CHUNK, GDN_WIDTH), lambda n: (chunk_of(n), j))

    first = pl.BlockSpec((CHUNK, GDN_WIDTH), lambda n: (chunk_of(n), 0))
    return pl.pallas_call(
        body, name="delta_bwd_r" if rev else "delta_bwd_f", grid=(nc,),
        in_specs=[col(0), col(1), col(2), pl.BlockSpec((CHUNK, LANE), lambda n: (chunk_of(n), 0)), first,
                  pl.BlockSpec((1, GDN_HEADS, hd, hd), lambda n: (chunk_of(n), 0, 0, 0)),
                  pl.BlockSpec((1, GDN_HEADS, CHUNK, CHUNK), lambda n: (chunk_of(n), 0, 0, 0))],
        out_specs=[first, first, first, pl.BlockSpec((CHUNK, LANE), lambda n: (chunk_of(n), 0))],
        out_shape=[jax.ShapeDtypeStruct((s, GDN_WIDTH), F32)] * 3 + [jax.ShapeDtypeStruct((s, LANE), F32)],
        scratch_shapes=[pltpu.VMEM((GDN_HEADS, hd, hd), F32)],
        compiler_params=_params("arbitrary"),
    )(y, y, y, gb, do, s_all, t_all)


def _gdn_post_fwd(o_f, o_r, p_pad, norm_row):
    s = o_f.shape[0]
    tm = min(512, s)
    hd = GDN_HEAD_DIM

    def body(of_ref, or_ref, z_ref, w_ref, out_ref, osum_ref):
        o = of_ref[...] + or_ref[...]
        osum_ref[...] = o
        z = z_ref[...]
        gate = z * _sigmoid(z)
        for h in range(GDN_HEADS):
            sl = slice(h * hd, (h + 1) * hd)
            oh = o[:, sl]
            r = lax.rsqrt(jnp.mean(oh * oh, axis=-1, keepdims=True) + EPS)
            out_ref[:, sl] = (oh * r * w_ref[...] * gate[:, sl]).astype(BF16)

    blk = pl.BlockSpec((tm, GDN_WIDTH), lambda i: (i, 0))
    return pl.pallas_call(
        body, name="gdn_post_fwd", grid=(s // tm,),
        in_specs=[blk, blk, pl.BlockSpec((tm, GDN_WIDTH), lambda i: (i, OFF_Z // GDN_WIDTH)),
                  pl.BlockSpec((1, hd), lambda i: (0, 0))],
        out_specs=[blk, blk],
        out_shape=[jax.ShapeDtypeStruct((s, GDN_WIDTH), BF16), jax.ShapeDtypeStruct((s, GDN_WIDTH), F32)],
        compiler_params=_params("parallel"),
    )(o_f, o_r, p_pad, norm_row)


def _gdn_post_bwd(d_out, o_sum, p_pad, norm_row):
    s = o_sum.shape[0]
    tm = min(512, s)
    hd = GDN_HEAD_DIM

    def body(d_ref, o_ref, z_ref, w_ref, do_ref, dz_ref, dw_ref):
        @pl.when(pl.program_id(0) == 0)
        def _():
            dw_ref[...] = jnp.zeros_like(dw_ref)

        z = z_ref[...]
        sg = _sigmoid(z)
        gate = z * sg
        dgate = sg * (1.0 + z * (1.0 - sg))
        wv = w_ref[...]
        dw = jnp.zeros((1, hd), F32)
        for h in range(GDN_HEADS):
            sl = slice(h * hd, (h + 1) * hd)
            oh = o_ref[:, sl]
            dh = d_ref[:, sl]
            r = lax.rsqrt(jnp.mean(oh * oh, axis=-1, keepdims=True) + EPS)
            ohat = oh * r
            dz_ref[:, sl] = dh * ohat * wv * dgate[:, sl]
            drn = dh * gate[:, sl]
            t = drn * wv
            do_ref[:, sl] = r * (t - ohat * jnp.mean(t * ohat, axis=-1, keepdims=True))
            dw = dw + jnp.sum(drn * ohat, axis=0, keepdims=True)
        dw_ref[...] += dw

    blk = pl.BlockSpec((tm, GDN_WIDTH), lambda i: (i, 0))
    vec = pl.BlockSpec((1, hd), lambda i: (0, 0))
    return pl.pallas_call(
        body, name="gdn_post_bwd", grid=(s // tm,),
        in_specs=[blk, blk, pl.BlockSpec((tm, GDN_WIDTH), lambda i: (i, OFF_Z // GDN_WIDTH)), vec],
        out_specs=[blk, blk, vec],
        out_shape=[jax.ShapeDtypeStruct((s, GDN_WIDTH), F32), jax.ShapeDtypeStruct((s, GDN_WIDTH), F32),
                   jax.ShapeDtypeStruct((1, hd), F32)],
        compiler_params=_params("arbitrary"),
    )(d_out, o_sum, p_pad, norm_row)


def _add2(a, b, name):
    s, w = a.shape
    tm = next(t for t in (1024, 640, 512, 256, 128, 64, 8) if s % t == 0)

    def body(a_ref, b_ref, o_ref):
        o_ref[...] = a_ref[...] + b_ref[...]

    blk = pl.BlockSpec((tm, w), lambda i: (i, 0))
    return pl.pallas_call(body, name=name, grid=(s // tm,), in_specs=[blk, blk], out_specs=blk,
                          out_shape=jax.ShapeDtypeStruct((s, w), F32), compiler_params=_params("parallel"))(a, b)


def _gdn_forward(p_pad, conv_wt, alog_row, dt_row, norm_row):
    c_pre, y = _gdn_conv_fwd(p_pad, conv_wt)
    gb = _gdn_gates_fwd(p_pad, alog_row, dt_row)
    o_f, s_f, t_f = _delta_fwd(y, gb, False)
    o_r, s_r, t_r = _delta_fwd(y, gb, True)
    out, o_sum = _gdn_post_fwd(o_f, o_r, p_pad, norm_row)
    return out, (c_pre, y, gb, s_f, t_f, s_r, t_r, o_sum)


def _gdn_backward(d_out, p_pad, conv_wt, alog_row, dt_row, norm_row, saved):
    c_pre, y, gb, s_f, t_f, s_r, t_r, o_sum = saved
    do, dz, dnorm = _gdn_post_bwd(d_out, o_sum, p_pad, norm_row)
    dq_f, dk_f, dv_f, dgb_f = _delta_bwd(y, gb, do, s_f, t_f, False)
    dq_r, dk_r, dv_r, dgb_r = _delta_bwd(y, gb, do, s_r, t_r, True)
    dy = jnp.concatenate([_add2(dq_f, dq_r, "gdn_dq"), _add2(dk_f, dk_r, "gdn_dk"), _add2(dv_f, dv_r, "gdn_dv")], axis=1)
    dp_qkv, dconv = _gdn_conv_bwd(dy, c_pre, p_pad, conv_wt)
    dp_ab, gate_sums = _gdn_gates_bwd(dgb_f, dgb_r, p_pad, gb, alog_row, dt_row)
    return dp_qkv, dz, dp_ab, dconv, gate_sums, dnorm


ATT_BK = ATT_BQ + 2 * ATT_HALO
SWA_SCALE = SWA_HEAD_DIM ** -0.5


def _t5_bucket(rel):
    nb = REL_BUCKETS // 2
    bucket = (rel > 0).astype(np.int32) * nb
    n = np.abs(rel)
    max_exact = nb // 2
    large = max_exact + (np.log(np.maximum(n, 1) / max_exact)
                         / math.log(REL_MAX_DISTANCE / max_exact) * (nb - max_exact)).astype(np.int32)
    large = np.minimum(large, nb - 1)
    return (bucket + np.where(n < max_exact, n, large)).astype(np.int32)


def _band_tables(dilation, queries_are_rows_of_block):
    blk = np.arange(ATT_BQ)
    band = np.arange(ATT_BK) - ATT_HALO
    if queries_are_rows_of_block:
        rel = band[None, :] - blk[:, None]
        band_idx = np.broadcast_to(np.arange(ATT_BK)[None, :], rel.shape)
    else:
        rel = blk[None, :] - band[:, None]
        band_idx = np.broadcast_to(np.arange(ATT_BK)[:, None], rel.shape)
    base = np.abs(rel) <= ATT_HALO
    not_prev = band_idx >= ATT_HALO
    not_next = band_idx < ATT_HALO + ATT_BQ
    valid = np.stack([base & not_prev, base, base & not_next, base & not_prev & not_next])
    return valid, _t5_bucket(rel * dilation)


def _bias_tiles(rel_bias, dilation, queries_are_rows_of_block):
    valid, bucket = _band_tables(dilation, queries_are_rows_of_block)
    rb = jnp.transpose(rel_bias[bucket], (2, 0, 1))
    return jnp.where(valid[:, None], rb[None], NEG_BIG).astype(F32)


def _group_sum(x, bd):
    hi = x.astype(BF16)
    lo = (x - hi.astype(F32)).astype(BF16)
    return jnp.dot(hi, bd, preferred_element_type=F32) + jnp.dot(lo, bd, preferred_element_type=F32)


def _head_block_diag():
    idx = np.arange(SWA_WIDTH) // SWA_HEAD_DIM
    return jnp.asarray(idx[:, None] == idx[None, :], BF16)


def _swa_pre_fwd(p_pad, qw_row, kw_row, bd):
    s = p_pad.shape[0]
    tm = min(512, s)
    inv = 1.0 / SWA_HEAD_DIM

    def body(q_ref, k_ref, v_ref, qw_ref, kw_ref, bd_ref, qo_ref, ko_ref, vo_ref):
        bdv = bd_ref[...]
        q = q_ref[...]
        k = k_ref[...]
        rq = lax.rsqrt(_group_sum(q * q, bdv) * inv + EPS)
        rk = lax.rsqrt(_group_sum(k * k, bdv) * inv + EPS)
        qo_ref[...] = (q * rq * qw_ref[...] * SWA_SCALE).astype(BF16)
        ko_ref[...] = (k * rk * kw_ref[...]).astype(BF16)
        vo_ref[...] = v_ref[...].astype(BF16)

    base = OFF_B // SWA_WIDTH
    blk = pl.BlockSpec((tm, SWA_WIDTH), lambda i: (i, 0))
    vec = pl.BlockSpec((1, SWA_WIDTH), lambda i: (0, 0))
    return pl.pallas_call(
        body, name="swa_pre_fwd", grid=(s // tm,),
        in_specs=[pl.BlockSpec((tm, SWA_WIDTH), lambda i: (i, base)), pl.BlockSpec((tm, SWA_WIDTH), lambda i: (i, base + 1)),
                  pl.BlockSpec((tm, SWA_WIDTH), lambda i: (i, base + 2)), vec, vec,
                  pl.BlockSpec((SWA_WIDTH, SWA_WIDTH), lambda i: (0, 0))],
        out_specs=[blk, blk, blk],
        out_shape=[jax.ShapeDtypeStruct((s, SWA_WIDTH), BF16)] * 3,
        compiler_params=_params("parallel"),
    )(p_pad, p_pad, p_pad, qw_row, kw_row, bd)


def _swa_pre_bwd(dqs, dks, dvs, p_pad, qw_row, kw_row, bd):
    s = p_pad.shape[0]
    tm = min(256, s)
    inv = 1.0 / SWA_HEAD_DIM
    npat = len(dqs)

    def body(*refs):
        dq_refs, dk_refs, dv_refs = refs[:npat], refs[npat:2 * npat], refs[2 * npat:3 * npat]
        q_ref, k_ref, qw_ref, kw_ref, bd_ref, dp_ref, dqw_ref, dkw_ref = refs[3 * npat:]

        @pl.when(pl.program_id(0) == 0)
        def _():
            dqw_ref[...] = jnp.zeros_like(dqw_ref)
            dkw_ref[...] = jnp.zeros_like(dkw_ref)

        bdv = bd_ref[...]

        def norm_bwd(x, g, w, scale):
            r = lax.rsqrt(_group_sum(x * x, bdv) * inv + EPS)
            xhat = x * r
            t = g * w * scale
            dx = r * (t - xhat * (_group_sum(t * xhat, bdv) * inv))
            return dx, jnp.sum(g * scale * xhat, axis=0, keepdims=True)

        def total(rs):
            t = rs[0][...]
            for r in rs[1:]:
                t = t + r[...]
            return t

        dq, dqw = norm_bwd(q_ref[...], total(dq_refs), qw_ref[...], SWA_SCALE)
        dk, dkw = norm_bwd(k_ref[...], total(dk_refs), kw_ref[...], 1.0)
        dp_ref[:, 0:SWA_WIDTH] = dq
        dp_ref[:, SWA_WIDTH:2 * SWA_WIDTH] = dk
        dp_ref[:, 2 * SWA_WIDTH:3 * SWA_WIDTH] = total(dv_refs)
        dqw_ref[...] += dqw
        dkw_ref[...] += dkw

    base = OFF_B // SWA_WIDTH
    blk = pl.BlockSpec((tm, SWA_WIDTH), lambda i: (i, 0))
    vec = pl.BlockSpec((1, SWA_WIDTH), lambda i: (0, 0))
    return pl.pallas_call(
        body, name="swa_pre_bwd", grid=(s // tm,),
        in_specs=[blk] * (3 * npat) + [pl.BlockSpec((tm, SWA_WIDTH), lambda i: (i, base)),
                                      pl.BlockSpec((tm, SWA_WIDTH), lambda i: (i, base + 1)), vec, vec,
                                      pl.BlockSpec((SWA_WIDTH, SWA_WIDTH), lambda i: (0, 0))],
        out_specs=[pl.BlockSpec((tm, 3 * SWA_WIDTH), lambda i: (i, 0)), vec, vec],
        out_shape=[jax.ShapeDtypeStruct((s, 3 * SWA_WIDTH), F32), jax.ShapeDtypeStruct((1, SWA_WIDTH), F32),
                   jax.ShapeDtypeStruct((1, SWA_WIDTH), F32)],
        compiler_params=_params("arbitrary"),
    )(*dqs, *dks, *dvs, p_pad, p_pad, qw_row, kw_row, bd)


def _band_specs(length):
    per = ATT_BQ // ATT_HALO
    last = length // ATT_HALO - 1
    prev = pl.BlockSpec((ATT_HALO, SWA_WIDTH), lambda r, t: (jnp.maximum(t * per - 1, 0), r))
    cur = pl.BlockSpec((ATT_BQ, SWA_WIDTH), lambda r, t: (t, r))
    nxt = pl.BlockSpec((ATT_HALO, SWA_WIDTH), lambda r, t: (jnp.minimum((t + 1) * per, last), r))
    return [prev, cur, nxt]


def _tile_variant(t, nb):
    if nb == 1:
        return 3
    return jnp.where(t == 0, 0, jnp.where(t == nb - 1, 2, 1))


def _band(refs):
    return jnp.concatenate([r[...] for r in refs], axis=0)


def _att_fwd(q, k, v, bias, dilation):
    s = q.shape[0]
    length = s // dilation
    nb = length // ATT_BQ
    view = (length, dilation * SWA_WIDTH)
    hd = SWA_HEAD_DIM

    def body(q_ref, kp, kc, kn, vp, vc, vn, b_ref, o_ref, lse_ref):
        kb, vb = _band((kp, kc, kn)), _band((vp, vc, vn))
        qv = q_ref[...]
        for h in range(SWA_HEADS):
            sl = slice(h * hd, (h + 1) * hd)
            sc = _dot(qv[:, sl], kb[:, sl], NT) + b_ref[0, h]
            m = jnp.max(sc, axis=-1, keepdims=True)
            p = jnp.exp(sc - m)
            den = jnp.sum(p, axis=-1, keepdims=True)
            o_ref[:, sl] = _dot(p, vb[:, sl]) / den
            lse_ref[:, sl] = jnp.broadcast_to(m + jnp.log(den), (ATT_BQ, hd))

    cur = pl.BlockSpec((ATT_BQ, SWA_WIDTH), lambda r, t: (t, r))
    bspec = pl.BlockSpec((1, SWA_HEADS, ATT_BQ, ATT_BK), lambda r, t: (_tile_variant(t, nb), 0, 0, 0))
    o, lse = pl.pallas_call(
        body, name=f"att_fwd_d{dilation}", grid=(dilation, nb),
        in_specs=[cur] + _band_specs(length) * 2 + [bspec],
        out_specs=[cur, cur],
        out_shape=[jax.ShapeDtypeStruct(view, F32)] * 2,
        compiler_params=_params("parallel", "parallel"),
    )(q.reshape(view), *([k.reshape(view)] * 3), *([v.reshape(view)] * 3), bias)
    return o.reshape(s, SWA_WIDTH), lse.reshape(s, SWA_WIDTH)


def _att_dq(q, k, v, dop, lse, cp, bias, dilation):
    s = q.shape[0]
    length = s // dilation
    nb = length // ATT_BQ
    view = (length, dilation * SWA_WIDTH)
    hd = SWA_HEAD_DIM

    def body(q_ref, kp, kc, kn, vp, vc, vn, do_ref, lse_ref, cp_ref, b_ref, dq_ref, db_ref):
        @pl.when((pl.program_id(0) == 0) & (pl.program_id(1) == 0))
        def _():
            db_ref[...] = jnp.zeros_like(db_ref)

        var = _tile_variant(pl.program_id(1), nb)
        kb, vb = _band((kp, kc, kn)), _band((vp, vc, vn))
        qv, dov, lsev, cpv = q_ref[...], do_ref[...], lse_ref[...], cp_ref[...]
        for h in range(SWA_HEADS):
            sl = slice(h * hd, (h + 1) * hd)
            sc = _dot(qv[:, sl], kb[:, sl], NT) + b_ref[0, h]
            p = jnp.exp(sc - lsev[:, h * hd:h * hd + 1])
            dp = _dot(dov[:, sl], vb[:, sl], NT)
            ds = p * (dp + cpv[:, h * hd:h * hd + 1])
            dq_ref[:, sl] = _dot(ds, kb[:, sl])
            db_ref[var, h] += ds

    cur = pl.BlockSpec((ATT_BQ, SWA_WIDTH), lambda r, t: (t, r))
    bspec = pl.BlockSpec((1, SWA_HEADS, ATT_BQ, ATT_BK), lambda r, t: (_tile_variant(t, nb), 0, 0, 0))
    dq, db = pl.pallas_call(
        body, name=f"att_dq_d{dilation}", grid=(dilation, nb),
        in_specs=[cur] + _band_specs(length) * 2 + [cur, cur, cur, bspec],
        out_specs=[cur, pl.BlockSpec((4, SWA_HEADS, ATT_BQ, ATT_BK), lambda r, t: (0, 0, 0, 0))],
        out_shape=[jax.ShapeDtypeStruct(view, F32), jax.ShapeDtypeStruct((4, SWA_HEADS, ATT_BQ, ATT_BK), F32)],
        compiler_params=_params("arbitrary", "arbitrary"),
    )(q.reshape(view), *([k.reshape(view)] * 3), *([v.reshape(view)] * 3), dop.reshape(view), lse.reshape(view),
      cp.reshape(view), bias)
    return dq.reshape(s, SWA_WIDTH), db


def _att_dkv(q, k, v, dop, lse, cp, bias_t, dilation):
    s = q.shape[0]
    length = s // dilation
    nb = length // ATT_BQ
    view = (length, dilation * SWA_WIDTH)
    hd = SWA_HEAD_DIM

    def body(k_ref, v_ref, qp, qc, qn, dp_, dc_, dn_, lp, lc, ln, cp_, cc_, cn_, b_ref, dk_ref, dv_ref):
        qb, dob = _band((qp, qc, qn)), _band((dp_, dc_, dn_))
        lseb, cpb = _band((lp, lc, ln)), _band((cp_, cc_, cn_))
        kv, vv = k_ref[...], v_ref[...]
        for h in range(SWA_HEADS):
            sl = slice(h * hd, (h + 1) * hd)
            sc = _dot(qb[:, sl], kv[:, sl], NT) + b_ref[0, h]
            p = jnp.exp(sc - lseb[:, h * hd:h * hd + 1])
            dv_ref[:, sl] = _dot(p, dob[:, sl], TN)
            dp = _dot(dob[:, sl], vv[:, sl], NT)
            ds = p * (dp + cpb[:, h * hd:h * hd + 1])
            dk_ref[:, sl] = _dot(ds, qb[:, sl], TN)

    cur = pl.BlockSpec((ATT_BQ, SWA_WIDTH), lambda r, t: (t, r))
    bspec = pl.BlockSpec((1, SWA_HEADS, ATT_BK, ATT_BQ), lambda r, t: (_tile_variant(t, nb), 0, 0, 0))
    dk, dv = pl.pallas_call(
        body, name=f"att_dkv_d{dilation}", grid=(dilation, nb),
        in_specs=[cur, cur] + _band_specs(length) * 4 + [bspec],
        out_specs=[cur, cur],
        out_shape=[jax.ShapeDtypeStruct(view, F32)] * 2,
        compiler_params=_params("parallel", "parallel"),
    )(k.reshape(view), v.reshape(view), *([q.reshape(view)] * 3), *([dop.reshape(view)] * 3),
      *([lse.reshape(view)] * 3), *([cp.reshape(view)] * 3), bias_t)
    return dk.reshape(s, SWA_WIDTH), dv.reshape(s, SWA_WIDTH)


def _pattern_weights(lses):
    m = lses[0]
    for l in lses[1:]:
        m = jnp.maximum(m, l)
    es = [jnp.exp(l - m) for l in lses]
    den = es[0]
    for e in es[1:]:
        den = den + e
    return [e / den for e in es]


def _combine_fwd(outs, lses):
    s = outs[0].shape[0]
    tm = min(512, s)
    npat = len(outs)

    def body(*refs):
        ws = _pattern_weights([r[...] for r in refs[npat:2 * npat]])
        o = ws[0] * refs[0][...]
        for p in range(1, npat):
            o = o + ws[p] * refs[p][...]
        refs[2 * npat][...] = o.astype(BF16)

    blk = pl.BlockSpec((tm, SWA_WIDTH), lambda i: (i, 0))
    return pl.pallas_call(
        body, name="swa_combine_fwd", grid=(s // tm,), in_specs=[blk] * (2 * npat), out_specs=blk,
        out_shape=jax.ShapeDtypeStruct((s, SWA_WIDTH), BF16), compiler_params=_params("parallel"),
    )(*outs, *lses)


def _combine_bwd(d_out, outs, lses, bd):
    s = d_out.shape[0]
    tm = min(512, s)
    npat = len(outs)

    def body(*refs):
        d_ref, bd_ref = refs[0], refs[1 + 2 * npat]
        o_refs, l_refs = refs[1:1 + npat], refs[1 + npat:1 + 2 * npat]
        out_refs = refs[2 + 2 * npat:]
        ws = _pattern_weights([r[...] for r in l_refs])
        dov = d_ref[...]
        o = ws[0] * o_refs[0][...]
        for p in range(1, npat):
            o = o + ws[p] * o_refs[p][...]
        rd = _group_sum(dov * o, bd_ref[...])
        for p in range(npat):
            out_refs[p][...] = (ws[p] * dov).astype(BF16)
            out_refs[npat + p][...] = -ws[p] * rd

    blk = pl.BlockSpec((tm, SWA_WIDTH), lambda i: (i, 0))
    res = pl.pallas_call(
        body, name="swa_combine_bwd", grid=(s // tm,),
        in_specs=[blk] * (1 + 2 * npat) + [pl.BlockSpec((SWA_WIDTH, SWA_WIDTH), lambda i: (0, 0))],
        out_specs=[blk] * (2 * npat),
        out_shape=[jax.ShapeDtypeStruct((s, SWA_WIDTH), BF16)] * npat + [jax.ShapeDtypeStruct((s, SWA_WIDTH), F32)] * npat,
        compiler_params=_params("parallel"),
    )(d_out, *outs, *lses, bd)
    return res[:npat], res[npat:]


def _rel_bias_grad(dbs, buckets):
    npat = len(dbs)

    def body(*refs):
        db_refs, bk_refs, o_ref = refs[:npat], refs[npat:2 * npat], refs[2 * npat]
        row = lax.broadcasted_iota(jnp.int32, (REL_BUCKETS, LANE), 0)
        lane = lax.broadcasted_iota(jnp.int32, (REL_BUCKETS, LANE), 1)
        tiles = [[db_refs[p][0, h] + db_refs[p][1, h] + db_refs[p][2, h] + db_refs[p][3, h] for h in range(SWA_HEADS)]
                 for p in range(npat)]
        bks = [r[...] for r in bk_refs]

        def one_bucket(b, acc):
            for h in range(SWA_HEADS):
                tot = jnp.zeros((1, 1), F32)
                for p in range(npat):
                    sel = jnp.where(bks[p] == b, tiles[p][h], 0.0)
                    tot = tot + jnp.sum(jnp.sum(sel, axis=1, keepdims=True), axis=0, keepdims=True)
                acc = acc + jnp.where((row == b) & (lane == h), tot, 0.0)
            return acc

        o_ref[...] = lax.fori_loop(0, REL_BUCKETS, one_bucket, jnp.zeros((REL_BUCKETS, LANE), F32))

    full4 = pl.BlockSpec((4, SWA_HEADS, ATT_BQ, ATT_BK), lambda: (0, 0, 0, 0))
    full2 = pl.BlockSpec((ATT_BQ, ATT_BK), lambda: (0, 0))
    return pl.pallas_call(
        body, name="rel_bias_grad", in_specs=[full4] * npat + [full2] * npat,
        out_specs=pl.BlockSpec((REL_BUCKETS, LANE), lambda: (0, 0)),
        out_shape=jax.ShapeDtypeStruct((REL_BUCKETS, LANE), F32),
        compiler_params=pltpu.CompilerParams(vmem_limit_bytes=V7X_VMEM_LIMIT_BYTES),
    )(*dbs, *buckets)


def _swa_forward(p_pad, qw_row, kw_row, rel_bias, bd):
    q, k, v = _swa_pre_fwd(p_pad, qw_row, kw_row, bd)
    outs, lses = [], []
    for _, dil in DILATION_PATTERNS:
        o, lse = _att_fwd(q, k, v, _bias_tiles(rel_bias, dil, True), dil)
        outs.append(o)
        lses.append(lse)
    return _combine_fwd(outs, lses), (q, k, v, outs, lses)


def _swa_backward(d_out, p_pad, qw_row, kw_row, rel_bias, bd, saved):
    q, k, v, outs, lses = saved
    dops, cps = _combine_bwd(d_out, outs, lses, bd)
    dqs, dks, dvs, dbs, buckets = [], [], [], [], []
    for p, (_, dil) in enumerate(DILATION_PATTERNS):
        dq, db = _att_dq(q, k, v, dops[p], lses[p], cps[p], _bias_tiles(rel_bias, dil, True), dil)
        dk, dv = _att_dkv(q, k, v, dops[p], lses[p], cps[p], _bias_tiles(rel_bias, dil, False), dil)
        dqs.append(dq)
        dks.append(dk)
        dvs.append(dv)
        dbs.append(db)
        buckets.append(jnp.asarray(_band_tables(dil, True)[1]))
    dp, dqw, dkw = _swa_pre_bwd(dqs, dks, dvs, p_pad, qw_row, kw_row, bd)
    return dp, dqw, dkw, _rel_bias_grad(dbs, buckets)


def _lane_row(v):
    flat = v.reshape(-1).astype(F32)
    return jnp.zeros((1, LANE), F32).at[0, :flat.shape[0]].set(flat)


def _pad_w_in(w_in):
    z = jnp.zeros((w_in.shape[0], N_PAD - OFF_AB - 16), w_in.dtype)
    return jnp.concatenate([w_in[:, :OFF_B], w_in[:, OFF_B + 16:], w_in[:, OFF_B:OFF_B + 16], z], axis=1)


def _local_step(x, tgt, wts, small):
    bd = _head_block_diag()
    conv_wt = jnp.zeros((8, QKV_A), F32).at[:CONV_WIDTH].set(small["conv_w"].T)
    alog_row, dt_row = _lane_row(small["a_log"]), _lane_row(small["dt_bias"])
    gnorm_row = small["gdn_norm_w"].reshape(1, GDN_HEAD_DIM)
    qw_row = jnp.tile(small["q_norm_w"].reshape(-1), SWA_HEADS).reshape(1, SWA_WIDTH)
    kw_row = jnp.tile(small["k_norm_w"].reshape(-1), SWA_HEADS).reshape(1, SWA_WIDTH)
    rel_bias = small["rel_bias"]
    win_pad = _pad_w_in(wts["w_in"])
    wo_a, wo_b = wts["w_out"][:GDN_WIDTH], wts["w_out"][GDN_WIDTH:]

    x1, sv1 = _ffn_forward(x, small["ffn1_norm"], wts["ffn1_w_gate"], wts["ffn1_w_up"], wts["ffn1_w_down"], "ffn1")
    n2, r2 = _rms_fwd(x1, small["mix_norm"], "mix_norm")
    p_pad = _matmul([(n2, win_pad)], tm=256, tn=N_PAD, tk=D_MODEL, name="w_in")
    o_a, sva = _gdn_forward(p_pad, conv_wt, alog_row, dt_row, gnorm_row)
    o_b, svb = _swa_forward(p_pad, qw_row, kw_row, rel_bias, bd)
    x2 = _matmul([(o_a, wo_a), (o_b, wo_b)], tm=512, tn=D_MODEL, tk=GDN_WIDTH, name="w_out", res=x1)
    x3, sv2 = _ffn_forward(x2, small["ffn2_norm"], wts["ffn2_w_gate"], wts["ffn2_w_up"], wts["ffn2_w_down"], "ffn2")
    loss_row, dx3, d_final = _final_loss(x3, small["final_norm"], tgt)

    dx2, d_ffn2_norm, dwg2, dwu2, dwd2 = _ffn_backward(
        dx3, x2, small["ffn2_norm"], wts["ffn2_w_gate"], wts["ffn2_w_up"], wts["ffn2_w_down"], sv2, "ffn2")
    d_oa = _matmul([(dx2, wo_a)], tb=True, tm=512, tn=GDN_WIDTH, tk=D_MODEL, name="w_out_da")
    d_ob = _matmul([(dx2, wo_b)], tb=True, tm=512, tn=SWA_WIDTH, tk=D_MODEL, name="w_out_db")
    dwo_a = _matmul([(o_a, dx2)], ta=True, tm=GDN_WIDTH, tn=D_MODEL, tk=512, name="w_out_dwa")
    dwo_b = _matmul([(o_b, dx2)], ta=True, tm=SWA_WIDTH, tn=D_MODEL, tk=512, name="w_out_dwb")
    dp_qkv, dz, dp_ab, dconv, gate_sums, d_gnorm = _gdn_backward(d_oa, p_pad, conv_wt, alog_row, dt_row, gnorm_row, sva)
    dp_b, dqw, dkw, d_rel = _swa_backward(d_ob, p_pad, qw_row, kw_row, rel_bias, bd, svb)
    segs = [(dp_qkv, 0, QKV_A), (dz, OFF_Z, GDN_WIDTH), (dp_b, OFF_B, 3 * SWA_WIDTH), (dp_ab, OFF_AB, LANE)]
    dn2 = None
    dwin_parts = []
    for i, (dseg, off, width) in enumerate(segs):
        dwin_parts.append(_matmul([(n2, dseg)], ta=True, tm=D_MODEL, tn=width, tk=512, name=f"w_in_dw{i}"))
        dn2 = _matmul([(dseg, win_pad[:, off:off + width])], tb=True, tm=512, tn=D_MODEL, tk=width,
                      name=f"w_in_dn{i}", res=dn2)
    dx1, d_mix_norm = _rms_bwd(dn2, x1, r2, small["mix_norm"], dx2, "mix_dnorm")
    dx, d_ffn1_norm, dwg1, dwu1, dwd1 = _ffn_backward(
        dx1, x, small["ffn1_norm"], wts["ffn1_w_gate"], wts["ffn1_w_up"], wts["ffn1_w_down"], sv1, "ffn1")

    d_w_in = jnp.concatenate([dwin_parts[0], dwin_parts[1], dwin_parts[3][:, :16], dwin_parts[2]], axis=1)
    grads = {
        "ffn1_norm": d_ffn1_norm, "ffn1_w_gate": dwg1, "ffn1_w_up": dwu1, "ffn1_w_down": dwd1,
        "mix_norm": d_mix_norm, "w_in": d_w_in, "conv_w": dconv[:CONV_WIDTH].T,
        "a_log": gate_sums[0, :8].reshape(2, GDN_HEADS), "dt_bias": gate_sums[1, :8].reshape(2, GDN_HEADS),
        "gdn_norm_w": d_gnorm, "q_norm_w": dqw.reshape(SWA_HEADS, SWA_HEAD_DIM).sum(0, keepdims=True),
        "k_norm_w": dkw.reshape(SWA_HEADS, SWA_HEAD_DIM).sum(0, keepdims=True), "rel_bias": d_rel[:, :SWA_HEADS],
        "w_out": jnp.concatenate([dwo_a, dwo_b], axis=0), "ffn2_norm": d_ffn2_norm,
        "ffn2_w_gate": dwg2, "ffn2_w_up": dwu2, "ffn2_w_down": dwd2, "final_norm": d_final,
    }
    return loss_row, dx, grads


MESH_IDS = pl.DeviceIdType.MESH
ANY = pl.BlockSpec(memory_space=pl.ANY)


def _all_gather(v, name):
    m, n = v.shape

    def body(x_ref, out_ref, send_sems, recv_sems, local_sem):
        x, y, c = lax.axis_index("x"), lax.axis_index("y"), lax.axis_index("c")
        me, sibling = (x, y, c), (x, y, 1 - c)
        chips = [(1 - x, y), (x, 1 - y), (1 - x, 1 - y)]

        def rows(px, py, pc):
            return out_ref.at[pl.ds((4 * px + 2 * py + pc) * m, m), :]

        def copy(k, block, to, src=None):
            return pltpu.make_async_remote_copy(
                src_ref=rows(*block) if src is None else src, dst_ref=rows(*block),
                send_sem=send_sems.at[k], recv_sem=recv_sems.at[k], device_id=to, device_id_type=MESH_IDS)

        mine = pltpu.make_async_copy(x_ref, rows(*me), local_sem)
        mine.start()
        first = [copy(0, me, sibling, src=x_ref)]
        first += [copy(1 + j, me, (*chip, c), src=x_ref) for j, chip in enumerate(chips)]
        for cp in first:
            cp.start()
        passed = [copy(4 + j, (*chip, c), sibling) for j, chip in enumerate(chips)]
        for j, chip in enumerate(chips):
            copy(1 + j, (*chip, c), me).wait_recv()
            passed[j].start()
        copy(0, sibling, me).wait_recv()
        for j, chip in enumerate(chips):
            copy(4 + j, (*chip, 1 - c), me).wait_recv()
        for cp in first + passed:
            cp.wait_send()
        mine.wait()

    return pl.pallas_call(
        body, name=name, in_specs=[ANY], out_specs=ANY,
        out_shape=jax.ShapeDtypeStruct((N_DEV * m, n), v.dtype),
        scratch_shapes=[pltpu.SemaphoreType.DMA((7,)), pltpu.SemaphoreType.DMA((7,)), pltpu.SemaphoreType.DMA],
        compiler_params=pltpu.CompilerParams(vmem_limit_bytes=V7X_VMEM_LIMIT_BYTES),
    )(v)


def _sibling_swap(v, name):
    def body(v_ref, out_ref, send_sem, recv_sem):
        x, y, c = lax.axis_index("x"), lax.axis_index("y"), lax.axis_index("c")
        cp = pltpu.make_async_remote_copy(src_ref=v_ref, dst_ref=out_ref, send_sem=send_sem, recv_sem=recv_sem,
                                          device_id=(x, y, 1 - c), device_id_type=MESH_IDS)
        cp.start()
        cp.wait()

    return pl.pallas_call(
        body, name=name, in_specs=[ANY], out_specs=ANY, out_shape=jax.ShapeDtypeStruct(v.shape, v.dtype),
        scratch_shapes=[pltpu.SemaphoreType.DMA, pltpu.SemaphoreType.DMA],
        compiler_params=pltpu.CompilerParams(vmem_limit_bytes=V7X_VMEM_LIMIT_BYTES),
    )(v)


def _chip_exchange(t, name):
    def body(t_ref, out_ref, send_sems, recv_sems, local_sem):
        x, y, c = lax.axis_index("x"), lax.axis_index("y"), lax.axis_index("c")
        mine = 2 * x + y
        chips = [(1 - x, y), (x, 1 - y), (1 - x, 1 - y)]
        own = pltpu.make_async_copy(t_ref.at[mine], out_ref.at[mine], local_sem)
        own.start()
        copies = [pltpu.make_async_remote_copy(
            src_ref=t_ref.at[2 * px + py], dst_ref=out_ref.at[mine], send_sem=send_sems.at[j], recv_sem=recv_sems.at[j],
            device_id=(px, py, c), device_id_type=MESH_IDS) for j, (px, py) in enumerate(chips)]
        for cp in copies:
            cp.start()
        for j, (px, py) in enumerate(chips):
            pltpu.make_async_remote_copy(
                src_ref=t_ref.at[mine], dst_ref=out_ref.at[2 * px + py], send_sem=send_sems.at[j],
                recv_sem=recv_sems.at[j], device_id=(px, py, c), device_id_type=MESH_IDS).wait_recv()
        for cp in copies:
            cp.wait_send()
        own.wait()

    return pl.pallas_call(
        body, name=name, in_specs=[ANY], out_specs=ANY, out_shape=jax.ShapeDtypeStruct(t.shape, t.dtype),
        scratch_shapes=[pltpu.SemaphoreType.DMA((3,)), pltpu.SemaphoreType.DMA((3,)), pltpu.SemaphoreType.DMA],
        compiler_params=pltpu.CompilerParams(vmem_limit_bytes=V7X_VMEM_LIMIT_BYTES),
    )(t)


def _adamw(parts, w, m, v, name):
    nparts, r, n = parts.shape
    tr = r
    for cand in (136, 104, 64, 8):
        if r % cand == 0:
            tr = cand
            break
    bc1 = 1.0 - ADAM_B1 ** ADAM_STEP
    bc2 = 1.0 - ADAM_B2 ** ADAM_STEP

    def body(p_ref, w_ref, m_ref, v_ref, g_ref, d_ref, nm_ref, nv_ref):
        g = p_ref[0]
        for k in range(1, nparts):
            g = g + p_ref[k]
        mn = ADAM_B1 * m_ref[...] + (1.0 - ADAM_B1) * g
        vn = ADAM_B2 * v_ref[...] + (1.0 - ADAM_B2) * (g * g)
        m_hat = mn / bc1
        v_hat = vn / bc2
        g_ref[...] = g
        nm_ref[...] = mn
        nv_ref[...] = vn
        d_ref[...] = -ADAM_LR * (m_hat / (jnp.sqrt(v_hat) + ADAM_EPS) + ADAM_WD * w_ref[...])

    blk = pl.BlockSpec((tr, n), lambda i: (i, 0))
    return pl.pallas_call(
        body, name=name, grid=(r // tr,),
        in_specs=[pl.BlockSpec((nparts, tr, n), lambda i: (0, i, 0)), blk, blk, blk],
        out_specs=[blk] * 4, out_shape=[jax.ShapeDtypeStruct((r, n), F32)] * 4,
        compiler_params=_params("parallel"),
    )(parts, w, m, v)


BIG = ("ffn1_w_gate", "ffn1_w_up", "ffn1_w_down", "w_in", "w_out", "ffn2_w_gate", "ffn2_w_up", "ffn2_w_down")
COL_SHARDED = ("ffn1_w_gate", "ffn1_w_up", "w_in", "ffn2_w_gate", "ffn2_w_up")
SMALL = ("ffn1_norm", "mix_norm", "a_log", "dt_bias", "gdn_norm_w", "q_norm_w", "k_norm_w", "rel_bias",
         "ffn2_norm", "final_norm")
WEIGHTS = ("ffn1_norm", "ffn1_w_gate", "ffn1_w_up", "ffn1_w_down", "mix_norm", "w_in", "conv_w", "a_log", "dt_bias",
           "gdn_norm_w", "q_norm_w", "k_norm_w", "rel_bias", "w_out", "ffn2_norm", "ffn2_w_gate", "ffn2_w_up",
           "ffn2_w_down", "final_norm")
PACK_WIDTH = 1024
PACK_ROW_MULTIPLE = 32


def _pack(arrays, width, row_multiple):
    flat = jnp.concatenate([a.reshape(-1) for a in arrays])
    rows = -(-flat.shape[0] // width)
    rows = -(-rows // row_multiple) * row_multiple
    return jnp.pad(flat, (0, rows * width - flat.shape[0])).reshape(rows, width)


def _unpack(packed, shapes):
    flat = packed.reshape(-1)
    out, pos = [], 0
    for shp in shapes:
        size = int(np.prod(shp))
        out.append(flat[pos:pos + size].reshape(shp))
        pos += size
    return out


def _blocks_of(name, full):
    if name in COL_SHARDED:
        rows, cols = full.shape
        return full.reshape(rows, N_DEV, cols // N_DEV).transpose(1, 0, 2).reshape(N_DEV, -1)
    return full.reshape(N_DEV, -1)


def _full_of(name, blocks, shard_shape):
    rows, cols = shard_shape
    if name in COL_SHARDED:
        return blocks.reshape(N_DEV, rows, cols).transpose(1, 0, 2).reshape(rows, N_DEV * cols)
    return blocks.reshape(N_DEV * rows, cols)


def kernel(x, ffn1_norm, ffn1_w_gate, ffn1_w_up, ffn1_w_down, mix_norm, w_in, conv_w, a_log, dt_bias, gdn_norm_w, q_norm_w, k_norm_w, rel_bias, w_out, ffn2_norm, ffn2_w_gate, ffn2_w_up, ffn2_w_down, final_norm, loss_target, m_ffn1_norm, m_ffn1_w_gate, m_ffn1_w_up, m_ffn1_w_down, m_mix_norm, m_w_in, m_conv_w, m_a_log, m_dt_bias, m_gdn_norm_w, m_q_norm_w, m_k_norm_w, m_rel_bias, m_w_out, m_ffn2_norm, m_ffn2_w_gate, m_ffn2_w_up, m_ffn2_w_down, m_final_norm, v_ffn1_norm, v_ffn1_w_gate, v_ffn1_w_up, v_ffn1_w_down, v_mix_norm, v_w_in, v_conv_w, v_a_log, v_dt_bias, v_gdn_norm_w, v_q_norm_w, v_k_norm_w, v_rel_bias, v_w_out, v_ffn2_norm, v_ffn2_w_gate, v_ffn2_w_up, v_ffn2_w_down, v_final_norm):
    w = dict(ffn1_norm=ffn1_norm, ffn1_w_gate=ffn1_w_gate, ffn1_w_up=ffn1_w_up, ffn1_w_down=ffn1_w_down, mix_norm=mix_norm, w_in=w_in, conv_w=conv_w, a_log=a_log, dt_bias=dt_bias, gdn_norm_w=gdn_norm_w, q_norm_w=q_norm_w, k_norm_w=k_norm_w, rel_bias=rel_bias, w_out=w_out, ffn2_norm=ffn2_norm, ffn2_w_gate=ffn2_w_gate, ffn2_w_up=ffn2_w_up, ffn2_w_down=ffn2_w_down, final_norm=final_norm)
    mom = dict(ffn1_norm=m_ffn1_norm, ffn1_w_gate=m_ffn1_w_gate, ffn1_w_up=m_ffn1_w_up, ffn1_w_down=m_ffn1_w_down, mix_norm=m_mix_norm, w_in=m_w_in, conv_w=m_conv_w, a_log=m_a_log, dt_bias=m_dt_bias, gdn_norm_w=m_gdn_norm_w, q_norm_w=m_q_norm_w, k_norm_w=m_k_norm_w, rel_bias=m_rel_bias, w_out=m_w_out, ffn2_norm=m_ffn2_norm, ffn2_w_gate=m_ffn2_w_gate, ffn2_w_up=m_ffn2_w_up, ffn2_w_down=m_ffn2_w_down, final_norm=m_final_norm)
    var = dict(ffn1_norm=v_ffn1_norm, ffn1_w_gate=v_ffn1_w_gate, ffn1_w_up=v_ffn1_w_up, ffn1_w_down=v_ffn1_w_down, mix_norm=v_mix_norm, w_in=v_w_in, conv_w=v_conv_w, a_log=v_a_log, dt_bias=v_dt_bias, gdn_norm_w=v_gdn_norm_w, q_norm_w=v_q_norm_w, k_norm_w=v_k_norm_w, rel_bias=v_rel_bias, w_out=v_w_out, ffn2_norm=v_ffn2_norm, ffn2_w_gate=v_ffn2_w_gate, ffn2_w_up=v_ffn2_w_up, ffn2_w_down=v_ffn2_w_down, final_norm=v_final_norm)
    ix, iy, ic = lax.axis_index("x"), lax.axis_index("y"), lax.axis_index("c")
    me = 4 * ix + 2 * iy + ic

    shard = {n: w[n][0] for n in BIG}
    shard_shapes = [shard[n].shape for n in BIG]

    sent = _pack([shard[n].astype(BF16) for n in BIG], PACK_WIDTH, PACK_ROW_MULTIPLE)
    pack_rows = sent.shape[0]
    gathered = _all_gather(sent, "gather_weights").reshape(N_DEV, pack_rows * PACK_WIDTH)
    wts, pos = {}, 0
    for n, shp in zip(BIG, shard_shapes):
        size = shp[0] * shp[1]
        wts[n] = _full_of(n, gathered[:, pos:pos + size], shp)
        pos += size

    small = {n: w[n][0] if n not in ("rel_bias",) else w[n] for n in SMALL}
    small = {n: (a.reshape(1, -1) if n.endswith("norm") else a) for n, a in small.items()}
    conv_all = _all_gather(_pack([w["conv_w"][0]], LANE, 8), "gather_conv").reshape(N_DEV, -1)
    conv_shard_shape = w["conv_w"][0].shape
    conv_elems = conv_shard_shape[0] * conv_shard_shape[1]
    small["conv_w"] = conv_all[:, :conv_elems].reshape(N_DEV * conv_shard_shape[0], conv_shard_shape[1])
    loss_row, grad_x, grads = _local_step(x[0], loss_target[0], wts, small)
    loss = lax.psum(loss_row[0, 0], ("x", "y", "c"))

    blocks = jnp.concatenate([_blocks_of(n, grads[n]) for n in BIG], axis=1)
    blocks = jnp.pad(blocks, ((0, 0), (0, pack_rows * PACK_WIDTH - blocks.shape[1])))
    by_core = blocks.reshape(4, 2, pack_rows, PACK_WIDTH)
    keep = lax.dynamic_index_in_dim(by_core, ic, axis=1, keepdims=False)
    give = lax.dynamic_index_in_dim(by_core, 1 - ic, axis=1, keepdims=False)
    got = _sibling_swap(give.reshape(4 * pack_rows, PACK_WIDTH), "grads_to_sibling").reshape(keep.shape)
    chip_sum = _add2(keep.reshape(4 * pack_rows, PACK_WIDTH), got.reshape(4 * pack_rows, PACK_WIDTH),
                     "grads_chip_sum").reshape(keep.shape)
    parts = _chip_exchange(chip_sum, "grads_to_chips")
    w_pack = _pack([shard[n] for n in BIG], PACK_WIDTH, PACK_ROW_MULTIPLE)
    m_pack = _pack([mom[n][0] for n in BIG], PACK_WIDTH, PACK_ROW_MULTIPLE)
    v_pack = _pack([var[n][0] for n in BIG], PACK_WIDTH, PACK_ROW_MULTIPLE)
    big_out = [_unpack(a, shard_shapes) for a in _adamw(parts, w_pack, m_pack, v_pack, "adamw_big")]

    small_names = SMALL + ("conv_w",)
    small_shapes = [grads[n].shape for n in small_names]
    g_small = _pack([grads[n] for n in small_names], LANE, 8)
    small_rows = g_small.shape[0]
    all_small = _all_gather(g_small, "gather_small_grads").reshape(N_DEV, small_rows, LANE)
    rep_shapes = [grads[n].shape for n in SMALL]
    zero_conv = jnp.zeros(small_shapes[-1], F32)
    ws = _pack([w[n].reshape(grads[n].shape) for n in SMALL] + [zero_conv], LANE, 8)
    ms = _pack([mom[n].reshape(grads[n].shape) for n in SMALL] + [zero_conv], LANE, 8)
    vs = _pack([var[n].reshape(grads[n].shape) for n in SMALL] + [zero_conv], LANE, 8)
    small_out = [_unpack(a, small_shapes) for a in _adamw(all_small, ws, ms, vs, "adamw_small")]
    conv_g = lax.dynamic_slice_in_dim(small_out[0][-1], me * conv_shard_shape[0], conv_shard_shape[0], axis=0)
    conv_out = [_unpack(a, [conv_shard_shape])[0] for a in _adamw(
        _pack([conv_g], LANE, 8)[None], _pack([w["conv_w"][0]], LANE, 8), _pack([mom["conv_w"][0]], LANE, 8),
        _pack([var["conv_w"][0]], LANE, 8), "adamw_conv")]

    def leaf(kind, n):
        if n in BIG:
            val = big_out[kind][BIG.index(n)]
        elif n == "conv_w":
            val = conv_out[kind]
        else:
            val = small_out[kind][SMALL.index(n)]
        return val.reshape(w[n].shape)

    outs = [loss, grad_x[None]]
    for kind in range(4):
        outs += [leaf(kind, n) for n in WEIGHTS]
    return tuple(outs)
```

```python
import functools
import math

import numpy as np
import jax
import jax.numpy as jnp
from jax import lax
from jax.experimental import pallas as pl
from jax.experimental.pallas import tpu as pltpu

F32 = jnp.float32
BF16 = jnp.bfloat16

D_MODEL = 1024
D_FF = 2816
GDN_HEADS = 4
GDN_HEAD_DIM = 128
GDN_WIDTH = 512
CONV_WIDTH = 5
CHUNK = 64
SWA_HEADS = 8
SWA_HEAD_DIM = 64
SWA_WIDTH = 512
DILATION_PATTERNS = ((128, 1), (512, 4), (2048, 16))
REL_BUCKETS = 32
REL_MAX_DISTANCE = 1024
EPS = 1e-6
NEG_BIG = -1e30
N_DEV = 8

ADAM_LR = 0.001
ADAM_B1 = 0.9
ADAM_B2 = 0.999
ADAM_EPS = 1e-08
ADAM_WD = 0.01
ADAM_STEP = 10

QKV_A = 3 * GDN_WIDTH
OFF_Z = QKV_A
OFF_B = OFF_Z + GDN_WIDTH
OFF_AB = OFF_B + 3 * SWA_WIDTH
N_PAD = OFF_AB + 128
N_IN = 3600

V7X_VMEM_LIMIT_BYTES = 56 * 1024 * 1024
LANE = 128
ATT_BQ = 128
ATT_HALO = 64
CONV_ROWS = 256

NN = (((1,), (0,)), ((), ()))
NT = (((1,), (1,)), ((), ()))
TN = (((0,), (0,)), ((), ()))


def _params(*sem):
    return pltpu.CompilerParams(dimension_semantics=sem, vmem_limit_bytes=V7X_VMEM_LIMIT_BYTES)


def _dot(a, b, dn=NN):
    return lax.dot_general(a.astype(BF16), b.astype(BF16), dn, preferred_element_type=F32)


def _dot_hi(a, b, dn=NN):
    return lax.dot_general(a, b, dn, precision=lax.Precision.HIGHEST, preferred_element_type=F32)


def _sigmoid(x):
    return 1.0 / (1.0 + jnp.exp(-x))


def _matmul(pairs, *, ta=False, tb=False, out_dtype=F32, tm, tn, tk, name, res=None, alpha=None, shard_cols=None):
    a0, b0 = pairs[0]
    m = a0.shape[1] if ta else a0.shape[0]
    k = a0.shape[0] if ta else a0.shape[1]
    n = b0.shape[0] if tb else b0.shape[1]
    tm, tn, tk = min(tm, m), min(tn, n), min(tk, k)
    assert m % tm == 0 and n % tn == 0 and k % tk == 0, (name, m, n, k, tm, tn, tk)
    nk = k // tk
    npairs = len(pairs)
    dn = (((0 if ta else 1,), (1 if tb else 0,)), ((), ()))

    def body(*refs):
        ins = refs[:2 * npairs]
        pos = 2 * npairs
        r_ref = None
        if res is not None:
            r_ref = refs[pos]
            pos += 1
        o_ref, acc = refs[pos], refs[pos + 1]
        kk = pl.program_id(2)

        @pl.when(kk == 0)
        def _():
            acc[...] = jnp.zeros_like(acc)

        t = None
        for p in range(npairs):
            d = _dot(ins[2 * p][...], ins[2 * p + 1][...], dn)
            t = d if t is None else t + d
        acc[...] += t

        @pl.when(kk == nk - 1)
        def _():
            r = acc[...]
            if alpha is not None:
                r = r * alpha
            if r_ref is not None:
                r = r_ref[...] + r
            if shard_cols is None:
                o_ref[...] = r.astype(out_dtype)
            else:
                for sh in range(tn // shard_cols):
                    o_ref[sh] = r[:, sh * shard_cols:(sh + 1) * shard_cols].astype(out_dtype)

    a_spec = pl.BlockSpec((tk, tm), lambda i, j, kk: (kk, i)) if ta else pl.BlockSpec((tm, tk), lambda i, j, kk: (i, kk))
    b_spec = pl.BlockSpec((tn, tk), lambda i, j, kk: (j, kk)) if tb else pl.BlockSpec((tk, tn), lambda i, j, kk: (kk, j))
    o_spec = pl.BlockSpec((tm, tn), lambda i, j, kk: (i, j))
    in_specs = [a_spec, b_spec] * npairs + ([o_spec] if res is not None else [])
    args = [t for pr in pairs for t in pr] + ([res] if res is not None else [])
    out_spec, out_shape = o_spec, (m, n)
    if shard_cols is not None:
        assert res is None and tn % shard_cols == 0
        out_spec = pl.BlockSpec((tn // shard_cols, tm, shard_cols), lambda i, j, kk: (j, i, 0))
        out_shape = (n // shard_cols, m, shard_cols)
    return pl.pallas_call(
        body, name=name, grid=(m // tm, n // tn, nk), in_specs=in_specs, out_specs=out_spec,
        out_shape=jax.ShapeDtypeStruct(out_shape, out_dtype), scratch_shapes=[pltpu.VMEM((tm, tn), F32)],
        compiler_params=_params("parallel", "parallel", "arbitrary"),
    )(*args)


def _rms_fwd(x, w, name):
    s, d = x.shape
    tm = min(512, s)

    def body(x_ref, w_ref, n_ref, r_ref):
        xv = x_ref[...]
        r = lax.rsqrt(jnp.mean(xv * xv, axis=-1, keepdims=True) + EPS)
        n_ref[...] = (xv * r * w_ref[...]).astype(BF16)
        r_ref[...] = r

    return pl.pallas_call(
        body, name=name, grid=(s // tm,),
        in_specs=[pl.BlockSpec((tm, d), lambda i: (i, 0)), pl.BlockSpec((1, d), lambda i: (0, 0))],
        out_specs=[pl.BlockSpec((tm, d), lambda i: (i, 0)), pl.BlockSpec((tm, 1), lambda i: (i, 0))],
        out_shape=[jax.ShapeDtypeStruct((s, d), BF16), jax.ShapeDtypeStruct((s, 1), F32)],
        compiler_params=_params("parallel"),
    )(x, w)


def _rms_bwd(dn, x, r, w, dres, name):
    s, d = x.shape
    tm = min(512, s)

    def body(dn_ref, x_ref, r_ref, w_ref, dres_ref, dx_ref, dw_ref):
        @pl.when(pl.program_id(0) == 0)
        def _():
            dw_ref[...] = jnp.zeros_like(dw_ref)

        rv = r_ref[...]
        xhat = x_ref[...] * rv
        g = dn_ref[...]
        t = g * w_ref[...]
        dx_ref[...] = dres_ref[...] + rv * (t - xhat * jnp.mean(t * xhat, axis=-1, keepdims=True))
        dw_ref[...] += jnp.sum(g * xhat, axis=0, keepdims=True)

    row = pl.BlockSpec((tm, d), lambda i: (i, 0))
    vec = pl.BlockSpec((1, d), lambda i: (0, 0))
    return pl.pallas_call(
        body, name=name, grid=(s // tm,),
        in_specs=[row, row, pl.BlockSpec((tm, 1), lambda i: (i, 0)), vec, row],
        out_specs=[row, vec],
        out_shape=[jax.ShapeDtypeStruct((s, d), F32), jax.ShapeDtypeStruct((1, d), F32)],
        compiler_params=_params("arbitrary"),
    )(dn, x, r, w, dres)


def _final_loss(x3, wf, tgt):
    s, d = x3.shape
    tm = min(512, s)

    def body(x_ref, w_ref, t_ref, loss_ref, dx_ref, dw_ref):
        @pl.when(pl.program_id(0) == 0)
        def _():
            dw_ref[...] = jnp.zeros_like(dw_ref)
            loss_ref[...] = jnp.zeros_like(loss_ref)

        xv = x_ref[...]
        wv = w_ref[...]
        r = lax.rsqrt(jnp.mean(xv * xv, axis=-1, keepdims=True) + EPS)
        xhat = xv * r
        e = xhat * wv - t_ref[...]
        part = 0.5 * jnp.sum(jnp.mean(e * e, axis=-1, keepdims=True), axis=0, keepdims=True)
        loss_ref[...] += jnp.broadcast_to(part, loss_ref.shape)
        dy = e * (1.0 / d)
        dw_ref[...] += jnp.sum(dy * xhat, axis=0, keepdims=True)
        t = dy * wv
        dx_ref[...] = r * (t - xhat * jnp.mean(t * xhat, axis=-1, keepdims=True))

    row = pl.BlockSpec((tm, d), lambda i: (i, 0))
    vec = pl.BlockSpec((1, d), lambda i: (0, 0))
    return pl.pallas_call(
        body, name="final_loss", grid=(s // tm,),
        in_specs=[row, vec, row],
        out_specs=[pl.BlockSpec((1, LANE), lambda i: (0, 0)), row, vec],
        out_shape=[jax.ShapeDtypeStruct((1, LANE), F32), jax.ShapeDtypeStruct((s, d), F32),
                   jax.ShapeDtypeStruct((1, d), F32)],
        compiler_params=_params("arbitrary"),
    )(x3, wf, tgt)


def _ffn_up(n, wg, wu, name):
    s, d = n.shape
    f = wg.shape[1]
    tm, tn = min(512, s), f // 2

    def body(n_ref, wg_ref, wu_ref, g_ref, u_ref, a_ref):
        nv = n_ref[...]
        g = _dot(nv, wg_ref[...])
        u = _dot(nv, wu_ref[...])
        g_ref[...] = g
        u_ref[...] = u
        a_ref[...] = (g * _sigmoid(g) * u).astype(BF16)

    o = pl.BlockSpec((tm, tn), lambda i, j: (i, j))
    wspec = pl.BlockSpec((d, tn), lambda i, j: (0, j))
    return pl.pallas_call(
        body, name=name, grid=(s // tm, f // tn),
        in_specs=[pl.BlockSpec((tm, d), lambda i, j: (i, 0)), wspec, wspec],
        out_specs=[o, o, o],
        out_shape=[jax.ShapeDtypeStruct((s, f), F32), jax.ShapeDtypeStruct((s, f), F32),
                   jax.ShapeDtypeStruct((s, f), BF16)],
        compiler_params=_params("parallel", "parallel"),
    )(n, wg, wu)


def _ffn_dact(dx, wd, g, u, name):
    s, d = dx.shape
    f = wd.shape[0]
    tm, tn = min(512, s), f // 2

    def body(dx_ref, wd_ref, g_ref, u_ref, dg_ref, du_ref):
        da = 0.5 * _dot(dx_ref[...], wd_ref[...], NT)
        gv = g_ref[...]
        sg = _sigmoid(gv)
        du_ref[...] = (da * gv * sg).astype(BF16)
        dg_ref[...] = (da * u_ref[...] * (sg * (1.0 + gv * (1.0 - sg)))).astype(BF16)

    o = pl.BlockSpec((tm, tn), lambda i, j: (i, j))
    return pl.pallas_call(
        body, name=name, grid=(s // tm, f // tn),
        in_specs=[pl.BlockSpec((tm, d), lambda i, j: (i, 0)), pl.BlockSpec((tn, d), lambda i, j: (j, 0)), o, o],
        out_specs=[o, o],
        out_shape=[jax.ShapeDtypeStruct((s, f), BF16), jax.ShapeDtypeStruct((s, f), BF16)],
        compiler_params=_params("parallel", "parallel"),
    )(dx, wd, g, u)


def _ffn_forward(x, norm_w, wg, wu, wd, tag):
    n, r = _rms_fwd(x, norm_w, f"{tag}_norm")
    g, u, a = _ffn_up(n, wg, wu, f"{tag}_up")
    y = _matmul([(a, wd)], tm=512, tn=1024, tk=1408, name=f"{tag}_down", res=x, alpha=0.5)
    return y, (n, r, g, u, a)


def _ffn_backward(dy, x, norm_w, wg, wu, wd, saved, tag):
    n, r, g, u, a = saved
    dwd = _matmul([(a, dy)], ta=True, tm=1408, tn=1024, tk=512, name=f"{tag}_dwd", alpha=0.5)
    dg, du = _ffn_dact(dy, wd, g, u, f"{tag}_dact")
    cols = wg.shape[1] // N_DEV
    dwg = _matmul([(n, dg)], ta=True, tm=1024, tn=1408, tk=512, name=f"{tag}_dwg", shard_cols=cols)
    dwu = _matmul([(n, du)], ta=True, tm=1024, tn=1408, tk=512, name=f"{tag}_dwu", shard_cols=cols)
    dn = _matmul([(dg, wg), (du, wu)], tb=True, tm=512, tn=1024, tk=1408, name=f"{tag}_dn")
    dx, dnorm = _rms_bwd(dn, x, r, norm_w, dy, f"{tag}_dnorm")
    return dx, dnorm, dwg, dwu, dwd


Q_SCALE = GDN_HEAD_DIM ** -0.5
CONV_HALO = 8


def _conv_taps(win, w_ref, rows, sign):
    n = rows + 2 * CONV_HALO
    acc = None
    for t in range(CONV_WIDTH):
        o = sign * (t - CONV_WIDTH // 2)
        sh = win if o == 0 else pltpu.roll(win, (-o) % n, 0)
        term = sh[CONV_HALO:CONV_HALO + rows] * w_ref[t:t + 1, :]
        acc = term if acc is None else acc + term
    return acc


def _gdn_conv_fwd(p_pad, conv_wt):
    s = p_pad.shape[0]
    rows = min(CONV_ROWS, s)
    nblk = QKV_A // LANE

    def body(p_ref, w_ref, c_ref, y_ref, pad):
        j = pl.program_id(0)
        zeros = jnp.zeros((CONV_HALO, LANE), F32)
        pad[0:CONV_HALO, :] = zeros
        pad[CONV_HALO + s:2 * CONV_HALO + s, :] = zeros
        pad[CONV_HALO:CONV_HALO + s, :] = p_ref[...]

        def chunk(ci, carry):
            b = pl.multiple_of(ci * rows, rows)
            win = pad[pl.ds(b, rows + 2 * CONV_HALO), :]
            c = _conv_taps(win, w_ref, rows, 1)
            c_ref[pl.ds(b, rows), :] = c
            act = c * _sigmoid(c)
            nrm = lax.rsqrt(jnp.sum(act * act, axis=-1, keepdims=True) + EPS)
            mult = jnp.where(j < GDN_HEADS, nrm * Q_SCALE, jnp.where(j < 2 * GDN_HEADS, nrm, 1.0))
            y_ref[pl.ds(b, rows), :] = act * mult
            return carry

        lax.fori_loop(0, s // rows, chunk, 0)

    col = pl.BlockSpec((s, LANE), lambda j: (0, j))
    return pl.pallas_call(
        body, name="gdn_conv_fwd", grid=(nblk,),
        in_specs=[col, pl.BlockSpec((8, LANE), lambda j: (0, j))],
        out_specs=[col, col],
        out_shape=[jax.ShapeDtypeStruct((s, QKV_A), F32), jax.ShapeDtypeStruct((s, QKV_A), F32)],
        scratch_shapes=[pltpu.VMEM((s + 2 * CONV_HALO, LANE), F32)],
        compiler_params=_params("parallel"),
    )(p_pad, conv_wt)


def _gdn_conv_bwd(dy_f, dy_r, c_pre, p_pad, conv_wt):
    s = p_pad.shape[0]
    rows = min(CONV_ROWS, s)
    nblk = QKV_A // LANE

    def body(dyf_ref, dyr_ref, c_ref, p_ref, w_ref, dp_ref, dw_ref, ppad, dcpad):
        j = pl.program_id(0)
        zeros = jnp.zeros((CONV_HALO, LANE), F32)
        for buf in (ppad, dcpad):
            buf[0:CONV_HALO, :] = zeros
            buf[CONV_HALO + s:2 * CONV_HALO + s, :] = zeros
        ppad[CONV_HALO:CONV_HALO + s, :] = p_ref[...]

        def act_bwd(ci, carry):
            b = pl.multiple_of(ci * rows, rows)
            c = c_ref[pl.ds(b, rows), :]
            g = dyf_ref[pl.ds(b, rows), :] + dyr_ref[pl.ds(b, rows), :]
            sg = _sigmoid(c)
            act = c * sg
            nrm = lax.rsqrt(jnp.sum(act * act, axis=-1, keepdims=True) + EPS)
            yh = act * nrm
            scale = jnp.where(j < GDN_HEADS, Q_SCALE, 1.0)
            dact_qk = (scale * nrm) * (g - yh * jnp.sum(g * yh, axis=-1, keepdims=True))
            dact = jnp.where(j < 2 * GDN_HEADS, dact_qk, g)
            dcpad[pl.ds(pl.multiple_of(b + CONV_HALO, CONV_HALO), rows), :] = dact * (sg * (1.0 + c * (1.0 - sg)))
            return carry

        lax.fori_loop(0, s // rows, act_bwd, 0)
        tap = lax.broadcasted_iota(jnp.int32, (8, LANE), 0)

        def taps_bwd(ci, dw):
            b = pl.multiple_of(ci * rows, rows)
            dcw = dcpad[pl.ds(b, rows + 2 * CONV_HALO), :]
            dp_ref[pl.ds(b, rows), :] = _conv_taps(dcw, w_ref, rows, -1)
            pw = ppad[pl.ds(b, rows + 2 * CONV_HALO), :]
            dc = dcw[CONV_HALO:CONV_HALO + rows]
            n = rows + 2 * CONV_HALO
            for t in range(CONV_WIDTH):
                o = t - CONV_WIDTH // 2
                sh = pw if o == 0 else pltpu.roll(pw, (-o) % n, 0)
                row = jnp.sum(dc * sh[CONV_HALO:CONV_HALO + rows], axis=0, keepdims=True)
                dw = dw + jnp.where(tap == t, row, 0.0)
            return dw

        dw_ref[...] = lax.fori_loop(0, s // rows, taps_bwd, jnp.zeros((8, LANE), F32))

    col = pl.BlockSpec((s, LANE), lambda j: (0, j))
    wspec = pl.BlockSpec((8, LANE), lambda j: (0, j))
    return pl.pallas_call(
        body, name="gdn_conv_bwd", grid=(nblk,),
        in_specs=[col, col, col, col, wspec],
        out_specs=[col, wspec],
        out_shape=[jax.ShapeDtypeStruct((s, QKV_A), F32), jax.ShapeDtypeStruct((8, QKV_A), F32)],
        scratch_shapes=[pltpu.VMEM((s + 2 * CONV_HALO, LANE), F32), pltpu.VMEM((s + 2 * CONV_HALO, LANE), F32)],
        compiler_params=_params("parallel"),
    )(dy_f, dy_r, c_pre, p_pad, conv_wt)


def _softplus(x):
    return jnp.maximum(x, 0.0) + jnp.log(1.0 + jnp.exp(-jnp.abs(x)))


def _gdn_gates_fwd(p_pad, alog_row, dt_row):
    s = p_pad.shape[0]
    tm = min(1024, s)

    def body(p_ref, al_ref, dt_ref, o_ref):
        x = p_ref[...]
        lane = lax.broadcasted_iota(jnp.int32, x.shape, 1)
        g = -jnp.exp(al_ref[...]) * _softplus(x + dt_ref[...])
        o_ref[...] = jnp.where(lane < 8, g, jnp.where(lane < 16, _sigmoid(x), 0.0))

    vec = pl.BlockSpec((1, LANE), lambda i: (0, 0))
    return pl.pallas_call(
        body, name="gdn_gates_fwd", grid=(s // tm,),
        in_specs=[pl.BlockSpec((tm, LANE), lambda i: (i, OFF_AB // LANE)), vec, vec],
        out_specs=pl.BlockSpec((tm, LANE), lambda i: (i, 0)),
        out_shape=jax.ShapeDtypeStruct((s, LANE), F32),
        compiler_params=_params("parallel"),
    )(p_pad, alog_row, dt_row)


def _gdn_gates_bwd(dgb_f, dgb_r, p_pad, gb, alog_row, dt_row):
    s = p_pad.shape[0]
    tm = min(1024, s)

    def body(df_ref, dr_ref, p_ref, gb_ref, al_ref, dt_ref, dp_ref, sum_ref):
        @pl.when(pl.program_id(0) == 0)
        def _():
            sum_ref[...] = jnp.zeros_like(sum_ref)

        x = p_ref[...]
        gbv = gb_ref[...]
        dgb = df_ref[...] + dr_ref[...]
        lane = lax.broadcasted_iota(jnp.int32, x.shape, 1)
        da = dgb * (-jnp.exp(al_ref[...])) * _sigmoid(x + dt_ref[...])
        db = dgb * gbv * (1.0 - gbv)
        dp_ref[...] = jnp.where(lane < 8, da, jnp.where(lane < 16, db, 0.0))
        row = lax.broadcasted_iota(jnp.int32, (8, LANE), 0)
        lane8 = lax.broadcasted_iota(jnp.int32, (8, LANE), 1)
        d_alog = jnp.sum(dgb * gbv, axis=0, keepdims=True)
        d_dt = jnp.sum(da, axis=0, keepdims=True)
        upd = jnp.where(row == 0, d_alog, jnp.where(row == 1, d_dt, 0.0))
        sum_ref[...] += jnp.where(lane8 < 8, upd, 0.0)

    vec = pl.BlockSpec((1, LANE), lambda i: (0, 0))
    blk = pl.BlockSpec((tm, LANE), lambda i: (i, 0))
    return pl.pallas_call(
        body, name="gdn_gates_bwd", grid=(s // tm,),
        in_specs=[blk, blk, pl.BlockSpec((tm, LANE), lambda i: (i, OFF_AB // LANE)), blk, vec, vec],
        out_specs=[blk, pl.BlockSpec((8, LANE), lambda i: (0, 0))],
        out_shape=[jax.ShapeDtypeStruct((s, LANE), F32), jax.ShapeDtypeStruct((8, LANE), F32)],
        compiler_params=_params("arbitrary"),
    )(dgb_f, dgb_r, p_pad, gb, alog_row, dt_row)


def _chunk_masks(rev):
    row = lax.broadcasted_iota(jnp.int32, (CHUNK, CHUNK), 0)
    col = lax.broadcasted_iota(jnp.int32, (CHUNK, CHUNK), 1)
    le = (col >= row) if rev else (col <= row)
    strict = (col > row) if rev else (col < row)
    return le, strict, row == col


def _chunk_common(q, k, v, g, beta, gc, masks):
    le, strict, eye = masks
    gc_row = _dot_hi(jnp.ones((CHUNK, CHUNK), F32), jnp.where(eye, gc, 0.0))
    decay = jnp.where(le, jnp.exp(jnp.where(le, gc - gc_row, 0.0)), 0.0)
    eg = jnp.exp(gc)
    gl = jnp.sum(g, axis=0, keepdims=True)
    kb = k * beta
    vb = v * beta
    kbeg = kb * eg
    lm = jnp.where(strict, _dot(kb, k, NT) * decay, 0.0)
    intra = _dot(q, k, NT) * decay
    qg = q * eg
    edec = jnp.exp(gl - gc)
    kdec = k * edec
    return dict(decay=decay, eg=eg, gl=gl, kb=kb, vb=vb, kbeg=kbeg, lm=lm, intra=intra, qg=qg, edec=edec, kdec=kdec)


def _unit_lower_inverse(lm, eye):
    x = -lm
    t = eye.astype(F32) + x
    p = x
    for _ in range(5):
        p = _dot_hi(p, p)
        t = t + _dot_hi(t, p)
    return t


def _gate_lanes(rev, h):
    d = 1 if rev else 0
    return d * GDN_HEADS + h, 8 + d * GDN_HEADS + h


def _delta_fwd(y, gb, rev):
    s = y.shape[0]
    nc = s // CHUNK
    hd = GDN_HEAD_DIM

    def chunk_of(n):
        return nc - 1 - n if rev else n

    def body(q_ref, k_ref, v_ref, gb_ref, o_ref, s_all, t_all, state):
        @pl.when(pl.program_id(0) == 0)
        def _():
            state[...] = jnp.zeros_like(state)

        masks = _chunk_masks(rev)
        gbv = gb_ref[...]
        gcm = _dot_hi(masks[0].astype(F32), gbv)
        for h in range(GDN_HEADS):
            gi, bi = _gate_lanes(rev, h)
            sl = slice(h * hd, (h + 1) * hd)
            q, k, v = q_ref[:, sl], k_ref[:, sl], v_ref[:, sl]
            g, beta, gc = gbv[:, gi:gi + 1], gbv[:, bi:bi + 1], gcm[:, gi:gi + 1]
            cm = _chunk_common(q, k, v, g, beta, gc, masks)
            tinv = _unit_lower_inverse(cm["lm"], masks[2])
            u = _dot(tinv, cm["vb"])
            w = _dot(tinv, cm["kbeg"])
            st = state[h]
            v_new = u - _dot(w, st)
            o_ref[:, sl] = _dot(cm["qg"], st) + _dot(cm["intra"], v_new)
            s_all[0, h] = st
            t_all[0, h] = tinv
            state[h] = st * jnp.exp(cm["gl"]) + _dot(cm["kdec"], v_new, TN)

    def col(j):
        return pl.BlockSpec((CHUNK, GDN_WIDTH), lambda n: (chunk_of(n), j))

    return pl.pallas_call(
        body, name="delta_fwd_r" if rev else "delta_fwd_f", grid=(nc,),
        in_specs=[col(0), col(1), col(2), pl.BlockSpec((CHUNK, LANE), lambda n: (chunk_of(n), 0))],
        out_specs=[pl.BlockSpec((CHUNK, GDN_WIDTH), lambda n: (chunk_of(n), 0)),
                   pl.BlockSpec((1, GDN_HEADS, hd, hd), lambda n: (chunk_of(n), 0, 0, 0)),
                   pl.BlockSpec((1, GDN_HEADS, CHUNK, CHUNK), lambda n: (chunk_of(n), 0, 0, 0))],
        out_shape=[jax.ShapeDtypeStruct((s, GDN_WIDTH), F32),
                   jax.ShapeDtypeStruct((nc, GDN_HEADS, hd, hd), F32),
                   jax.ShapeDtypeStruct((nc, GDN_HEADS, CHUNK, CHUNK), F32)],
        scratch_shapes=[pltpu.VMEM((GDN_HEADS, hd, hd), F32)],
        compiler_params=_params("arbitrary"),
    )(y, y, y, gb)


def _delta_bwd(y, gb, do, s_all, t_all, rev):
    s = y.shape[0]
    nc = s // CHUNK
    hd = GDN_HEAD_DIM

    def chunk_of(n):
        return n if rev else nc - 1 - n

    def body(q_ref, k_ref, v_ref, gb_ref, do_ref, s_ref, t_ref, dy_ref, dgb_ref, dstate):
        @pl.when(pl.program_id(0) == 0)
        def _():
            dstate[...] = jnp.zeros_like(dstate)

        masks = _chunk_masks(rev)
        le, strict, _ = masks
        le_t = _chunk_masks(not rev)[0].astype(F32)
        gbv = gb_ref[...]
        gcm = _dot_hi(le.astype(F32), gbv)
        lane = lax.broadcasted_iota(jnp.int32, (CHUNK, LANE), 1)
        ones_cl = jnp.ones((CHUNK, LANE), F32)
        dgc_tile = jnp.zeros((CHUNK, LANE), F32)
        rest_tile = jnp.zeros((CHUNK, LANE), F32)
        for h in range(GDN_HEADS):
            gi, bi = _gate_lanes(rev, h)
            sl = slice(h * hd, (h + 1) * hd)
            q, k, v = q_ref[:, sl], k_ref[:, sl], v_ref[:, sl]
            g, beta, gc = gbv[:, gi:gi + 1], gbv[:, bi:bi + 1], gcm[:, gi:gi + 1]
            cm = _chunk_common(q, k, v, g, beta, gc, masks)
            tinv = t_ref[0, h]
            st = s_ref[0, h]
            ds_out = dstate[h]
            dov = do_ref[:, sl]
            u = _dot(tinv, cm["vb"])
            w = _dot(tinv, cm["kbeg"])
            v_new = u - _dot(w, st)
            egl = jnp.exp(cm["gl"])
            d_qg = _dot(dov, st, NT)
            d_intra = _dot(dov, v_new, NT)
            dv_new = _dot(cm["intra"], dov, TN) + _dot(cm["kdec"], ds_out)
            d_kdec = _dot(v_new, ds_out, NT)
            dstate[h] = _dot(cm["qg"], dov, TN) + egl * ds_out - _dot(w, dv_new, TN)
            dgl = egl * jnp.sum(jnp.sum(st * ds_out, axis=1, keepdims=True), axis=0, keepdims=True)
            dw = -_dot(dv_new, st, NT)
            dvb = _dot(tinv, dv_new, TN)
            dkbeg = _dot(tinv, dw, TN)
            dlm = jnp.where(strict, -(_dot(dvb, u, NT) + _dot(dkbeg, w, NT)), 0.0)
            d_a = dlm * cm["decay"]
            d_qk = d_intra * cm["decay"]
            e = dlm * cm["lm"] + d_intra * cm["intra"]
            dgc = jnp.sum(e, axis=1, keepdims=True) - _dot_hi(e, ones_cl, TN)[:, 0:1]
            dkb = _dot(d_a, k) + dkbeg * cm["eg"]
            dk = _dot(d_a, cm["kb"], TN) + _dot(d_qk, q, TN)
            dq = _dot(d_qk, k) + d_qg * cm["eg"]
            dgc = dgc + jnp.sum(d_qg * cm["qg"], axis=1, keepdims=True)
            dgc = dgc + jnp.sum(dkbeg * cm["kbeg"], axis=1, keepdims=True)
            tdec = jnp.sum(d_kdec * cm["kdec"], axis=1, keepdims=True)
            dk = dk + d_kdec * cm["edec"] + dkb * beta
            dgc = dgc - tdec
            dgl = dgl + jnp.sum(tdec, axis=0, keepdims=True)
            dbeta = jnp.sum(dvb * v, axis=1, keepdims=True) + jnp.sum(dkb * k, axis=1, keepdims=True)
            dy_ref[:, h * hd:(h + 1) * hd] = dq
            dy_ref[:, GDN_WIDTH + h * hd:GDN_WIDTH + (h + 1) * hd] = dk
            dy_ref[:, 2 * GDN_WIDTH + h * hd:2 * GDN_WIDTH + (h + 1) * hd] = dvb * beta
            dgc_tile = dgc_tile + jnp.where(lane == gi, dgc, 0.0)
            rest_tile = rest_tile + jnp.where(lane == gi, dgl, 0.0) + jnp.where(lane == bi, dbeta, 0.0)
        dgb_ref[...] = _dot_hi(le_t, dgc_tile) + rest_tile

    def col(j):
        return pl.BlockSpec((CHUNK, GDN_WIDTH), lambda n: (chunk_of(n), j))

    first = pl.BlockSpec((CHUNK, GDN_WIDTH), lambda n: (chunk_of(n), 0))
    return pl.pallas_call(
        body, name="delta_bwd_r" if rev else "delta_bwd_f", grid=(nc,),
        in_specs=[col(0), col(1), col(2), pl.BlockSpec((CHUNK, LANE), lambda n: (chunk_of(n), 0)), first,
                  pl.BlockSpec((1, GDN_HEADS, hd, hd), lambda n: (chunk_of(n), 0, 0, 0)),
                  pl.BlockSpec((1, GDN_HEADS, CHUNK, CHUNK), lambda n: (chunk_of(n), 0, 0, 0))],
        out_specs=[pl.BlockSpec((CHUNK, QKV_A), lambda n: (chunk_of(n), 0)),
                   pl.BlockSpec((CHUNK, LANE), lambda n: (chunk_of(n), 0))],
        out_shape=[jax.ShapeDtypeStruct((s, QKV_A), F32), jax.ShapeDtypeStruct((s, LANE), F32)],
        scratch_shapes=[pltpu.VMEM((GDN_HEADS, hd, hd), F32)],
        compiler_params=_params("arbitrary"),
    )(y, y, y, gb, do, s_all, t_all)


def _gdn_post_fwd(o_f, o_r, p_pad, norm_row):
    s = o_f.shape[0]
    tm = min(512, s)
    hd = GDN_HEAD_DIM

    def body(of_ref, or_ref, z_ref, w_ref, out_ref, osum_ref):
        o = of_ref[...] + or_ref[...]
        osum_ref[...] = o
        z = z_ref[...]
        gate = z * _sigmoid(z)
        for h in range(GDN_HEADS):
            sl = slice(h * hd, (h + 1) * hd)
            oh = o[:, sl]
            r = lax.rsqrt(jnp.mean(oh * oh, axis=-1, keepdims=True) + EPS)
            out_ref[:, sl] = (oh * r * w_ref[...] * gate[:, sl]).astype(BF16)

    blk = pl.BlockSpec((tm, GDN_WIDTH), lambda i: (i, 0))
    return pl.pallas_call(
        body, name="gdn_post_fwd", grid=(s // tm,),
        in_specs=[blk, blk, pl.BlockSpec((tm, GDN_WIDTH), lambda i: (i, OFF_Z // GDN_WIDTH)),
                  pl.BlockSpec((1, hd), lambda i: (0, 0))],
        out_specs=[blk, blk],
        out_shape=[jax.ShapeDtypeStruct((s, GDN_WIDTH), BF16), jax.ShapeDtypeStruct((s, GDN_WIDTH), F32)],
        compiler_params=_params("parallel"),
    )(o_f, o_r, p_pad, norm_row)


def _gdn_post_bwd(d_out, o_sum, p_pad, norm_row):
    s = o_sum.shape[0]
    tm = min(512, s)
    hd = GDN_HEAD_DIM

    def body(d_ref, o_ref, z_ref, w_ref, do_ref, dz_ref, dw_ref):
        @pl.when(pl.program_id(0) == 0)
        def _():
            dw_ref[...] = jnp.zeros_like(dw_ref)

        z = z_ref[...]
        sg = _sigmoid(z)
        gate = z * sg
        dgate = sg * (1.0 + z * (1.0 - sg))
        wv = w_ref[...]
        dw = jnp.zeros((1, hd), F32)
        for h in range(GDN_HEADS):
            sl = slice(h * hd, (h + 1) * hd)
            oh = o_ref[:, sl]
            dh = d_ref[:, sl]
            r = lax.rsqrt(jnp.mean(oh * oh, axis=-1, keepdims=True) + EPS)
            ohat = oh * r
            dz_ref[:, sl] = dh * ohat * wv * dgate[:, sl]
            drn = dh * gate[:, sl]
            t = drn * wv
            do_ref[:, sl] = r * (t - ohat * jnp.mean(t * ohat, axis=-1, keepdims=True))
            dw = dw + jnp.sum(drn * ohat, axis=0, keepdims=True)
        dw_ref[...] += dw

    blk = pl.BlockSpec((tm, GDN_WIDTH), lambda i: (i, 0))
    vec = pl.BlockSpec((1, hd), lambda i: (0, 0))
    return pl.pallas_call(
        body, name="gdn_post_bwd", grid=(s // tm,),
        in_specs=[blk, blk, pl.BlockSpec((tm, GDN_WIDTH), lambda i: (i, OFF_Z // GDN_WIDTH)), vec],
        out_specs=[blk, blk, vec],
        out_shape=[jax.ShapeDtypeStruct((s, GDN_WIDTH), F32), jax.ShapeDtypeStruct((s, GDN_WIDTH), F32),
                   jax.ShapeDtypeStruct((1, hd), F32)],
        compiler_params=_params("arbitrary"),
    )(d_out, o_sum, p_pad, norm_row)


def _add2(a, b, name):
    s, w = a.shape
    tm = next(t for t in (1024, 640, 512, 256, 128, 64, 8) if s % t == 0)

    def body(a_ref, b_ref, o_ref):
        o_ref[...] = a_ref[...] + b_ref[...]

    blk = pl.BlockSpec((tm, w), lambda i: (i, 0))
    return pl.pallas_call(body, name=name, grid=(s // tm,), in_specs=[blk, blk], out_specs=blk,
                          out_shape=jax.ShapeDtypeStruct((s, w), F32), compiler_params=_params("parallel"))(a, b)


def _gdn_forward(p_pad, conv_wt, alog_row, dt_row, norm_row):
    c_pre, y = _gdn_conv_fwd(p_pad, conv_wt)
    gb = _gdn_gates_fwd(p_pad, alog_row, dt_row)
    o_f, s_f, t_f = _delta_fwd(y, gb, False)
    o_r, s_r, t_r = _delta_fwd(y, gb, True)
    out, o_sum = _gdn_post_fwd(o_f, o_r, p_pad, norm_row)
    return out, (c_pre, y, gb, s_f, t_f, s_r, t_r, o_sum)


def _gdn_backward(d_out, p_pad, conv_wt, alog_row, dt_row, norm_row, saved):
    c_pre, y, gb, s_f, t_f, s_r, t_r, o_sum = saved
    do, dz, dnorm = _gdn_post_bwd(d_out, o_sum, p_pad, norm_row)
    dy_f, dgb_f = _delta_bwd(y, gb, do, s_f, t_f, False)
    dy_r, dgb_r = _delta_bwd(y, gb, do, s_r, t_r, True)
    dp_qkv, dconv = _gdn_conv_bwd(dy_f, dy_r, c_pre, p_pad, conv_wt)
    dp_ab, gate_sums = _gdn_gates_bwd(dgb_f, dgb_r, p_pad, gb, alog_row, dt_row)
    return dp_qkv, dz, dp_ab, dconv, gate_sums, dnorm


ATT_BK = ATT_BQ + 2 * ATT_HALO
SWA_SCALE = SWA_HEAD_DIM ** -0.5


def _t5_bucket(rel):
    nb = REL_BUCKETS // 2
    bucket = (rel > 0).astype(np.int32) * nb
    n = np.abs(rel)
    max_exact = nb // 2
    large = max_exact + (np.log(np.maximum(n, 1) / max_exact)
                         / math.log(REL_MAX_DISTANCE / max_exact) * (nb - max_exact)).astype(np.int32)
    large = np.minimum(large, nb - 1)
    return (bucket + np.where(n < max_exact, n, large)).astype(np.int32)


def _band_tables(dilation, queries_are_rows_of_block):
    blk = np.arange(ATT_BQ)
    band = np.arange(ATT_BK) - ATT_HALO
    if queries_are_rows_of_block:
        rel = band[None, :] - blk[:, None]
        band_idx = np.broadcast_to(np.arange(ATT_BK)[None, :], rel.shape)
    else:
        rel = blk[None, :] - band[:, None]
        band_idx = np.broadcast_to(np.arange(ATT_BK)[:, None], rel.shape)
    base = np.abs(rel) <= ATT_HALO
    not_prev = band_idx >= ATT_HALO
    not_next = band_idx < ATT_HALO + ATT_BQ
    valid = np.stack([base & not_prev, base, base & not_next, base & not_prev & not_next])
    return valid, _t5_bucket(rel * dilation)


def _bias_tiles(rel_bias, dilation, queries_are_rows_of_block):
    valid, bucket = _band_tables(dilation, queries_are_rows_of_block)
    onehot = (jnp.asarray(bucket.reshape(-1, 1)) == jnp.arange(REL_BUCKETS, dtype=jnp.int32)[None, :]).astype(F32)
    rb = jnp.dot(onehot, rel_bias.astype(F32), precision=lax.Precision.HIGHEST)
    rb = rb.T.reshape((SWA_HEADS,) + bucket.shape)
    return jnp.where(valid[:, None], rb[None], NEG_BIG).astype(F32)


def _group_sum(x, bd):
    hi = x.astype(BF16)
    lo = (x - hi.astype(F32)).astype(BF16)
    return jnp.dot(hi, bd, preferred_element_type=F32) + jnp.dot(lo, bd, preferred_element_type=F32)


def _head_block_diag():
    idx = np.arange(SWA_WIDTH) // SWA_HEAD_DIM
    return jnp.asarray(idx[:, None] == idx[None, :], BF16)


def _swa_pre_fwd(p_pad, qw_row, kw_row, bd):
    s = p_pad.shape[0]
    tm = min(512, s)
    inv = 1.0 / SWA_HEAD_DIM

    def body(q_ref, k_ref, v_ref, qw_ref, kw_ref, bd_ref, qo_ref, ko_ref, vo_ref):
        bdv = bd_ref[...]
        q = q_ref[...]
        k = k_ref[...]
        rq = lax.rsqrt(_group_sum(q * q, bdv) * inv + EPS)
        rk = lax.rsqrt(_group_sum(k * k, bdv) * inv + EPS)
        qo_ref[...] = (q * rq * qw_ref[...] * SWA_SCALE).astype(BF16)
        ko_ref[...] = (k * rk * kw_ref[...]).astype(BF16)
        vo_ref[...] = v_ref[...].astype(BF16)

    base = OFF_B // SWA_WIDTH
    blk = pl.BlockSpec((tm, SWA_WIDTH), lambda i: (i, 0))
    vec = pl.BlockSpec((1, SWA_WIDTH), lambda i: (0, 0))
    return pl.pallas_call(
        body, name="swa_pre_fwd", grid=(s // tm,),
        in_specs=[pl.BlockSpec((tm, SWA_WIDTH), lambda i: (i, base)), pl.BlockSpec((tm, SWA_WIDTH), lambda i: (i, base + 1)),
                  pl.BlockSpec((tm, SWA_WIDTH), lambda i: (i, base + 2)), vec, vec,
                  pl.BlockSpec((SWA_WIDTH, SWA_WIDTH), lambda i: (0, 0))],
        out_specs=[blk, blk, blk],
        out_shape=[jax.ShapeDtypeStruct((s, SWA_WIDTH), BF16)] * 3,
        compiler_params=_params("parallel"),
    )(p_pad, p_pad, p_pad, qw_row, kw_row, bd)


def _swa_pre_bwd(dqs, dks, dvs, p_pad, qw_row, kw_row, bd):
    s = p_pad.shape[0]
    tm = min(256, s)
    inv = 1.0 / SWA_HEAD_DIM
    npat = len(dqs)

    def body(*refs):
        dq_refs, dk_refs, dv_refs = refs[:npat], refs[npat:2 * npat], refs[2 * npat:3 * npat]
        q_ref, k_ref, qw_ref, kw_ref, bd_ref, dp_ref, dqw_ref, dkw_ref = refs[3 * npat:]

        @pl.when(pl.program_id(0) == 0)
        def _():
            dqw_ref[...] = jnp.zeros_like(dqw_ref)
            dkw_ref[...] = jnp.zeros_like(dkw_ref)

        bdv = bd_ref[...]

        def norm_bwd(x, g, w, scale):
            r = lax.rsqrt(_group_sum(x * x, bdv) * inv + EPS)
            xhat = x * r
            t = g * w * scale
            dx = r * (t - xhat * (_group_sum(t * xhat, bdv) * inv))
            return dx, jnp.sum(g * scale * xhat, axis=0, keepdims=True)

        def total(rs):
            t = rs[0][...]
            for r in rs[1:]:
                t = t + r[...]
            return t

        dq, dqw = norm_bwd(q_ref[...], total(dq_refs), qw_ref[...], SWA_SCALE)
        dk, dkw = norm_bwd(k_ref[...], total(dk_refs), kw_ref[...], 1.0)
        dp_ref[:, 0:SWA_WIDTH] = dq
        dp_ref[:, SWA_WIDTH:2 * SWA_WIDTH] = dk
        dp_ref[:, 2 * SWA_WIDTH:3 * SWA_WIDTH] = total(dv_refs)
        dqw_ref[...] += dqw
        dkw_ref[...] += dkw

    base = OFF_B // SWA_WIDTH
    blk = pl.BlockSpec((tm, SWA_WIDTH), lambda i: (i, 0))
    vec = pl.BlockSpec((1, SWA_WIDTH), lambda i: (0, 0))
    return pl.pallas_call(
        body, name="swa_pre_bwd", grid=(s // tm,),
        in_specs=[blk] * (3 * npat) + [pl.BlockSpec((tm, SWA_WIDTH), lambda i: (i, base)),
                                      pl.BlockSpec((tm, SWA_WIDTH), lambda i: (i, base + 1)), vec, vec,
                                      pl.BlockSpec((SWA_WIDTH, SWA_WIDTH), lambda i: (0, 0))],
        out_specs=[pl.BlockSpec((tm, 3 * SWA_WIDTH), lambda i: (i, 0)), vec, vec],
        out_shape=[jax.ShapeDtypeStruct((s, 3 * SWA_WIDTH), F32), jax.ShapeDtypeStruct((1, SWA_WIDTH), F32),
                   jax.ShapeDtypeStruct((1, SWA_WIDTH), F32)],
        compiler_params=_params("arbitrary"),
    )(*dqs, *dks, *dvs, p_pad, p_pad, qw_row, kw_row, bd)


def _band_specs(length):
    per = ATT_BQ // ATT_HALO
    last = length // ATT_HALO - 1
    prev = pl.BlockSpec((ATT_HALO, SWA_WIDTH), lambda r, t: (jnp.maximum(t * per - 1, 0), r))
    cur = pl.BlockSpec((ATT_BQ, SWA_WIDTH), lambda r, t: (t, r))
    nxt = pl.BlockSpec((ATT_HALO, SWA_WIDTH), lambda r, t: (jnp.minimum((t + 1) * per, last), r))
    return [prev, cur, nxt]


def _tile_variant(t, nb):
    if nb == 1:
        return 3
    return jnp.where(t == 0, 0, jnp.where(t == nb - 1, 2, 1))


def _band(refs):
    return jnp.concatenate([r[...] for r in refs], axis=0)


def _att_fwd(q, k, v, bias, dilation):
    s = q.shape[0]
    length = s // dilation
    nb = length // ATT_BQ
    view = (length, dilation * SWA_WIDTH)
    hd = SWA_HEAD_DIM

    def body(q_ref, kp, kc, kn, vp, vc, vn, b_ref, o_ref, lse_ref):
        kb, vb = _band((kp, kc, kn)), _band((vp, vc, vn))
        qv = q_ref[...]
        for h in range(SWA_HEADS):
            sl = slice(h * hd, (h + 1) * hd)
            sc = _dot(qv[:, sl], kb[:, sl], NT) + b_ref[0, h]
            m = jnp.max(sc, axis=-1, keepdims=True)
            p = jnp.exp(sc - m)
            den = jnp.sum(p, axis=-1, keepdims=True)
            o_ref[:, sl] = _dot(p, vb[:, sl]) / den
            lse_ref[:, sl] = jnp.broadcast_to(m + jnp.log(den), (ATT_BQ, hd))

    cur = pl.BlockSpec((ATT_BQ, SWA_WIDTH), lambda r, t: (t, r))
    bspec = pl.BlockSpec((1, SWA_HEADS, ATT_BQ, ATT_BK), lambda r, t: (_tile_variant(t, nb), 0, 0, 0))
    o, lse = pl.pallas_call(
        body, name=f"att_fwd_d{dilation}", grid=(dilation, nb),
        in_specs=[cur] + _band_specs(length) * 2 + [bspec],
        out_specs=[cur, cur],
        out_shape=[jax.ShapeDtypeStruct(view, F32)] * 2,
        compiler_params=_params("parallel", "parallel"),
    )(q.reshape(view), *([k.reshape(view)] * 3), *([v.reshape(view)] * 3), bias)
    return o.reshape(s, SWA_WIDTH), lse.reshape(s, SWA_WIDTH)


def _att_dq(q, k, v, dop, lse, cp, bias, dilation):
    s = q.shape[0]
    length = s // dilation
    nb = length // ATT_BQ
    view = (length, dilation * SWA_WIDTH)
    hd = SWA_HEAD_DIM

    def body(q_ref, kp, kc, kn, vp, vc, vn, do_ref, lse_ref, cp_ref, b_ref, dq_ref, db_ref):
        @pl.when((pl.program_id(0) == 0) & (pl.program_id(1) == 0))
        def _():
            db_ref[...] = jnp.zeros_like(db_ref)

        var = _tile_variant(pl.program_id(1), nb)
        kb, vb = _band((kp, kc, kn)), _band((vp, vc, vn))
        qv, dov, lsev, cpv = q_ref[...], do_ref[...], lse_ref[...], cp_ref[...]
        for h in range(SWA_HEADS):
            sl = slice(h * hd, (h + 1) * hd)
            sc = _dot(qv[:, sl], kb[:, sl], NT) + b_ref[0, h]
            p = jnp.exp(sc - lsev[:, h * hd:h * hd + 1])
            dp = _dot(dov[:, sl], vb[:, sl], NT)
            ds = p * (dp + cpv[:, h * hd:h * hd + 1])
            dq_ref[:, sl] = _dot(ds, kb[:, sl])
            db_ref[var, h] += ds

    cur = pl.BlockSpec((ATT_BQ, SWA_WIDTH), lambda r, t: (t, r))
    bspec = pl.BlockSpec((1, SWA_HEADS, ATT_BQ, ATT_BK), lambda r, t: (_tile_variant(t, nb), 0, 0, 0))
    dq, db = pl.pallas_call(
        body, name=f"att_dq_d{dilation}", grid=(dilation, nb),
        in_specs=[cur] + _band_specs(length) * 2 + [cur, cur, cur, bspec],
        out_specs=[cur, pl.BlockSpec((4, SWA_HEADS, ATT_BQ, ATT_BK), lambda r, t: (0, 0, 0, 0))],
        out_shape=[jax.ShapeDtypeStruct(view, F32), jax.ShapeDtypeStruct((4, SWA_HEADS, ATT_BQ, ATT_BK), F32)],
        compiler_params=_params("arbitrary", "arbitrary"),
    )(q.reshape(view), *([k.reshape(view)] * 3), *([v.reshape(view)] * 3), dop.reshape(view), lse.reshape(view),
      cp.reshape(view), bias)
    return dq.reshape(s, SWA_WIDTH), db


def _att_dkv(q, k, v, dop, lse, cp, bias_t, dilation):
    s = q.shape[0]
    length = s // dilation
    nb = length // ATT_BQ
    view = (length, dilation * SWA_WIDTH)
    hd = SWA_HEAD_DIM

    def body(k_ref, v_ref, qp, qc, qn, dp_, dc_, dn_, lp, lc, ln, cp_, cc_, cn_, b_ref, dk_ref, dv_ref):
        qb, dob = _band((qp, qc, qn)), _band((dp_, dc_, dn_))
        lseb, cpb = _band((lp, lc, ln)), _band((cp_, cc_, cn_))
        kv, vv = k_ref[...], v_ref[...]
        for h in range(SWA_HEADS):
            sl = slice(h * hd, (h + 1) * hd)
            sc = _dot(qb[:, sl], kv[:, sl], NT) + b_ref[0, h]
            p = jnp.exp(sc - lseb[:, h * hd:h * hd + 1])
            dv_ref[:, sl] = _dot(p, dob[:, sl], TN)
            dp = _dot(dob[:, sl], vv[:, sl], NT)
            ds = p * (dp + cpb[:, h * hd:h * hd + 1])
            dk_ref[:, sl] = _dot(ds, qb[:, sl], TN)

    cur = pl.BlockSpec((ATT_BQ, SWA_WIDTH), lambda r, t: (t, r))
    bspec = pl.BlockSpec((1, SWA_HEADS, ATT_BK, ATT_BQ), lambda r, t: (_tile_variant(t, nb), 0, 0, 0))
    dk, dv = pl.pallas_call(
        body, name=f"att_dkv_d{dilation}", grid=(dilation, nb),
        in_specs=[cur, cur] + _band_specs(length) * 4 + [bspec],
        out_specs=[cur, cur],
        out_shape=[jax.ShapeDtypeStruct(view, F32)] * 2,
        compiler_params=_params("parallel", "parallel"),
    )(k.reshape(view), v.reshape(view), *([q.reshape(view)] * 3), *([dop.reshape(view)] * 3),
      *([lse.reshape(view)] * 3), *([cp.reshape(view)] * 3), bias_t)
    return dk.reshape(s, SWA_WIDTH), dv.reshape(s, SWA_WIDTH)


def _pattern_weights(lses):
    m = lses[0]
    for l in lses[1:]:
        m = jnp.maximum(m, l)
    es = [jnp.exp(l - m) for l in lses]
    den = es[0]
    for e in es[1:]:
        den = den + e
    return [e / den for e in es]


def _combine_fwd(outs, lses):
    s = outs[0].shape[0]
    tm = min(512, s)
    npat = len(outs)

    def body(*refs):
        ws = _pattern_weights([r[...] for r in refs[npat:2 * npat]])
        o = ws[0] * refs[0][...]
        for p in range(1, npat):
            o = o + ws[p] * refs[p][...]
        refs[2 * npat][...] = o.astype(BF16)

    blk = pl.BlockSpec((tm, SWA_WIDTH), lambda i: (i, 0))
    return pl.pallas_call(
        body, name="swa_combine_fwd", grid=(s // tm,), in_specs=[blk] * (2 * npat), out_specs=blk,
        out_shape=jax.ShapeDtypeStruct((s, SWA_WIDTH), BF16), compiler_params=_params("parallel"),
    )(*outs, *lses)


def _combine_bwd(d_out, outs, lses, bd):
    s = d_out.shape[0]
    tm = min(512, s)
    npat = len(outs)

    def body(*refs):
        d_ref, bd_ref = refs[0], refs[1 + 2 * npat]
        o_refs, l_refs = refs[1:1 + npat], refs[1 + npat:1 + 2 * npat]
        out_refs = refs[2 + 2 * npat:]
        ws = _pattern_weights([r[...] for r in l_refs])
        dov = d_ref[...]
        o = ws[0] * o_refs[0][...]
        for p in range(1, npat):
            o = o + ws[p] * o_refs[p][...]
        rd = _group_sum(dov * o, bd_ref[...])
        for p in range(npat):
            out_refs[p][...] = (ws[p] * dov).astype(BF16)
            out_refs[npat + p][...] = -ws[p] * rd

    blk = pl.BlockSpec((tm, SWA_WIDTH), lambda i: (i, 0))
    res = pl.pallas_call(
        body, name="swa_combine_bwd", grid=(s // tm,),
        in_specs=[blk] * (1 + 2 * npat) + [pl.BlockSpec((SWA_WIDTH, SWA_WIDTH), lambda i: (0, 0))],
        out_specs=[blk] * (2 * npat),
        out_shape=[jax.ShapeDtypeStruct((s, SWA_WIDTH), BF16)] * npat + [jax.ShapeDtypeStruct((s, SWA_WIDTH), F32)] * npat,
        compiler_params=_params("parallel"),
    )(d_out, *outs, *lses, bd)
    return res[:npat], res[npat:]


def _rel_bias_grad(dbs, buckets):
    npat = len(dbs)

    def body(*refs):
        db_refs, bk_refs, o_ref = refs[:npat], refs[npat:2 * npat], refs[2 * npat]
        row = lax.broadcasted_iota(jnp.int32, (REL_BUCKETS, LANE), 0)
        lane = lax.broadcasted_iota(jnp.int32, (REL_BUCKETS, LANE), 1)
        tiles = [[db_refs[p][0, h] + db_refs[p][1, h] + db_refs[p][2, h] + db_refs[p][3, h] for h in range(SWA_HEADS)]
                 for p in range(npat)]
        bks = [r[...] for r in bk_refs]

        def one_bucket(b, acc):
            for h in range(SWA_HEADS):
                tot = jnp.zeros((1, 1), F32)
                for p in range(npat):
                    sel = jnp.where(bks[p] == b, tiles[p][h], 0.0)
                    tot = tot + jnp.sum(jnp.sum(sel, axis=1, keepdims=True), axis=0, keepdims=True)
                acc = acc + jnp.where((row == b) & (lane == h), tot, 0.0)
            return acc

        o_ref[...] = lax.fori_loop(0, REL_BUCKETS, one_bucket, jnp.zeros((REL_BUCKETS, LANE), F32))

    full4 = pl.BlockSpec((4, SWA_HEADS, ATT_BQ, ATT_BK), lambda: (0, 0, 0, 0))
    full2 = pl.BlockSpec((ATT_BQ, ATT_BK), lambda: (0, 0))
    return pl.pallas_call(
        body, name="rel_bias_grad", in_specs=[full4] * npat + [full2] * npat,
        out_specs=pl.BlockSpec((REL_BUCKETS, LANE), lambda: (0, 0)),
        out_shape=jax.ShapeDtypeStruct((REL_BUCKETS, LANE), F32),
        compiler_params=pltpu.CompilerParams(vmem_limit_bytes=V7X_VMEM_LIMIT_BYTES),
    )(*dbs, *buckets)


def _swa_forward(p_pad, qw_row, kw_row, rel_bias, bd):
    q, k, v = _swa_pre_fwd(p_pad, qw_row, kw_row, bd)
    outs, lses = [], []
    for _, dil in DILATION_PATTERNS:
        o, lse = _att_fwd(q, k, v, _bias_tiles(rel_bias, dil, True), dil)
        outs.append(o)
        lses.append(lse)
    return _combine_fwd(outs, lses), (q, k, v, outs, lses)


def _swa_backward(d_out, p_pad, qw_row, kw_row, rel_bias, bd, saved):
    q, k, v, outs, lses = saved
    dops, cps = _combine_bwd(d_out, outs, lses, bd)
    dqs, dks, dvs, dbs, buckets = [], [], [], [], []
    for p, (_, dil) in enumerate(DILATION_PATTERNS):
        dq, db = _att_dq(q, k, v, dops[p], lses[p], cps[p], _bias_tiles(rel_bias, dil, True), dil)
        dk, dv = _att_dkv(q, k, v, dops[p], lses[p], cps[p], _bias_tiles(rel_bias, dil, False), dil)
        dqs.append(dq)
        dks.append(dk)
        dvs.append(dv)
        dbs.append(db)
        buckets.append(jnp.asarray(_band_tables(dil, True)[1]))
    dp, dqw, dkw = _swa_pre_bwd(dqs, dks, dvs, p_pad, qw_row, kw_row, bd)
    return dp, dqw, dkw, _rel_bias_grad(dbs, buckets)


def _lane_row(v):
    flat = v.reshape(-1).astype(F32)
    return jnp.zeros((1, LANE), F32).at[0, :flat.shape[0]].set(flat)


W_IN_SHARD = N_IN // N_DEV
W_IN_RUNS = ((0, QKV_A, 0), (QKV_A, OFF_B, QKV_A), (OFF_B, OFF_B + 16, OFF_AB), (OFF_B + 16, N_IN, OFF_B))
W_IN_SEGMENTS = ((0, QKV_A), (OFF_Z, GDN_WIDTH), (OFF_B, 3 * SWA_WIDTH), (OFF_AB, LANE))


def _w_in_pieces(shard):
    lo, hi = shard * W_IN_SHARD, (shard + 1) * W_IN_SHARD
    out = []
    for first, last, dst in W_IN_RUNS:
        a, b = max(lo, first), min(hi, last)
        if a < b:
            out.append((a - lo, b - a, dst + a - first))
    return out


def _cols_from_slabs(w3, name):
    nd, r, wd = w3.shape
    half = nd // 2

    def body(w_ref, o_ref):
        for sh in range(half):
            o_ref[:, wd * sh:wd * (sh + 1)] = w_ref[sh]

    return pl.pallas_call(
        body, name=name, grid=(2,), in_specs=[pl.BlockSpec((half, r, wd), lambda j: (j, 0, 0))],
        out_specs=pl.BlockSpec((r, half * wd), lambda j: (0, j)),
        out_shape=jax.ShapeDtypeStruct((r, nd * wd), w3.dtype), compiler_params=_params("parallel"),
    )(w3)


def _w_in_from_slabs(w3):
    nd, r, _ = w3.shape

    def body(w_ref, o_ref):
        o_ref[:, OFF_AB:N_PAD] = jnp.zeros((r, N_PAD - OFF_AB), w3.dtype)
        for sh in range(nd):
            for src, length, dst in _w_in_pieces(sh):
                o_ref[:, dst:dst + length] = w_ref[sh, :, src:src + length]

    return pl.pallas_call(
        body, name="w_in_from_slabs", out_shape=jax.ShapeDtypeStruct((r, N_PAD), w3.dtype),
        compiler_params=pltpu.CompilerParams(vmem_limit_bytes=V7X_VMEM_LIMIT_BYTES),
    )(w3)


def _w_in_grad_slabs(parts):
    r = parts[0].shape[0]

    def body(*refs):
        o_ref = refs[len(parts)]
        for sh in range(N_DEV):
            for src, length, dst in _w_in_pieces(sh):
                seg = next(i for i, (off, width) in enumerate(W_IN_SEGMENTS) if off <= dst < off + width)
                at = dst - W_IN_SEGMENTS[seg][0]
                o_ref[sh, :, src:src + length] = refs[seg][:, at:at + length]

    return pl.pallas_call(
        body, name="w_in_grad_slabs", out_shape=jax.ShapeDtypeStruct((N_DEV, r, W_IN_SHARD), F32),
        compiler_params=pltpu.CompilerParams(vmem_limit_bytes=V7X_VMEM_LIMIT_BYTES),
    )(*parts)


def _local_step(x, tgt, wts, small):
    bd = _head_block_diag()
    conv_wt = jnp.zeros((8, QKV_A), F32).at[:CONV_WIDTH].set(small["conv_w"].T)
    alog_row, dt_row = _lane_row(small["a_log"]), _lane_row(small["dt_bias"])
    gnorm_row = small["gdn_norm_w"].reshape(1, GDN_HEAD_DIM)
    qw_row = jnp.tile(small["q_norm_w"].reshape(-1), SWA_HEADS).reshape(1, SWA_WIDTH)
    kw_row = jnp.tile(small["k_norm_w"].reshape(-1), SWA_HEADS).reshape(1, SWA_WIDTH)
    rel_bias = small["rel_bias"]
    win_pad = wts["w_in_pad"]
    wo_a, wo_b = wts["w_out"][:GDN_WIDTH], wts["w_out"][GDN_WIDTH:]

    x1, sv1 = _ffn_forward(x, small["ffn1_norm"], wts["ffn1_w_gate"], wts["ffn1_w_up"], wts["ffn1_w_down"], "ffn1")
    n2, r2 = _rms_fwd(x1, small["mix_norm"], "mix_norm")
    p_pad = _matmul([(n2, win_pad)], tm=256, tn=N_PAD, tk=D_MODEL, name="w_in")
    o_a, sva = _gdn_forward(p_pad, conv_wt, alog_row, dt_row, gnorm_row)
    o_b, svb = _swa_forward(p_pad, qw_row, kw_row, rel_bias, bd)
    x2 = _matmul([(o_a, wo_a), (o_b, wo_b)], tm=512, tn=D_MODEL, tk=GDN_WIDTH, name="w_out", res=x1)
    x3, sv2 = _ffn_forward(x2, small["ffn2_norm"], wts["ffn2_w_gate"], wts["ffn2_w_up"], wts["ffn2_w_down"], "ffn2")
    loss_row, dx3, d_final = _final_loss(x3, small["final_norm"], tgt)

    dx2, d_ffn2_norm, dwg2, dwu2, dwd2 = _ffn_backward(
        dx3, x2, small["ffn2_norm"], wts["ffn2_w_gate"], wts["ffn2_w_up"], wts["ffn2_w_down"], sv2, "ffn2")
    d_oa = _matmul([(dx2, wo_a)], tb=True, tm=512, tn=GDN_WIDTH, tk=D_MODEL, name="w_out_da")
    d_ob = _matmul([(dx2, wo_b)], tb=True, tm=512, tn=SWA_WIDTH, tk=D_MODEL, name="w_out_db")
    dwo_a = _matmul([(o_a, dx2)], ta=True, tm=GDN_WIDTH, tn=D_MODEL, tk=512, name="w_out_dwa")
    dwo_b = _matmul([(o_b, dx2)], ta=True, tm=SWA_WIDTH, tn=D_MODEL, tk=512, name="w_out_dwb")
    dp_qkv, dz, dp_ab, dconv, gate_sums, d_gnorm = _gdn_backward(d_oa, p_pad, conv_wt, alog_row, dt_row, gnorm_row, sva)
    dp_b, dqw, dkw, d_rel = _swa_backward(d_ob, p_pad, qw_row, kw_row, rel_bias, bd, svb)
    segs = [(dp_qkv, 0, QKV_A), (dz, OFF_Z, GDN_WIDTH), (dp_b, OFF_B, 3 * SWA_WIDTH), (dp_ab, OFF_AB, LANE)]
    dn2 = None
    dwin_parts = []
    for i, (dseg, off, width) in enumerate(segs):
        dwin_parts.append(_matmul([(n2, dseg)], ta=True, tm=D_MODEL, tn=width, tk=512, name=f"w_in_dw{i}"))
        dn2 = _matmul([(dseg, win_pad[:, off:off + width])], tb=True, tm=512, tn=D_MODEL, tk=width,
                      name=f"w_in_dn{i}", res=dn2)
    dx1, d_mix_norm = _rms_bwd(dn2, x1, r2, small["mix_norm"], dx2, "mix_dnorm")
    dx, d_ffn1_norm, dwg1, dwu1, dwd1 = _ffn_backward(
        dx1, x, small["ffn1_norm"], wts["ffn1_w_gate"], wts["ffn1_w_up"], wts["ffn1_w_down"], sv1, "ffn1")

    def row_slabs(full):
        return full.reshape(N_DEV, full.shape[0] // N_DEV, full.shape[1])

    dwd1, dwd2 = row_slabs(dwd1), row_slabs(dwd2)
    grads = {
        "ffn1_norm": d_ffn1_norm, "ffn1_w_gate": dwg1, "ffn1_w_up": dwu1, "ffn1_w_down": dwd1,
        "mix_norm": d_mix_norm, "w_in": _w_in_grad_slabs(dwin_parts), "conv_w": dconv[:CONV_WIDTH].T,
        "a_log": gate_sums[0, :8].reshape(2, GDN_HEADS), "dt_bias": gate_sums[1, :8].reshape(2, GDN_HEADS),
        "gdn_norm_w": d_gnorm, "q_norm_w": dqw.reshape(SWA_HEADS, SWA_HEAD_DIM).sum(0, keepdims=True),
        "k_norm_w": dkw.reshape(SWA_HEADS, SWA_HEAD_DIM).sum(0, keepdims=True), "rel_bias": d_rel[:, :SWA_HEADS],
        "w_out": row_slabs(jnp.concatenate([dwo_a, dwo_b], axis=0)), "ffn2_norm": d_ffn2_norm,
        "ffn2_w_gate": dwg2, "ffn2_w_up": dwu2, "ffn2_w_down": dwd2, "final_norm": d_final,
    }
    return loss_row, dx, grads


MESH_IDS = pl.DeviceIdType.MESH
ANY = pl.BlockSpec(memory_space=pl.ANY)


def _all_gather(v, name):
    m, n = v.shape

    def body(x_ref, out_ref, send_sems, recv_sems, local_sem):
        x, y, c = lax.axis_index("x"), lax.axis_index("y"), lax.axis_index("c")
        me, sibling = (x, y, c), (x, y, 1 - c)
        chips = [(1 - x, y), (x, 1 - y), (1 - x, 1 - y)]

        def rows(px, py, pc):
            return out_ref.at[pl.ds((4 * px + 2 * py + pc) * m, m), :]

        def copy(k, block, to, src=None):
            return pltpu.make_async_remote_copy(
                src_ref=rows(*block) if src is None else src, dst_ref=rows(*block),
                send_sem=send_sems.at[k], recv_sem=recv_sems.at[k], device_id=to, device_id_type=MESH_IDS)

        mine = pltpu.make_async_copy(x_ref, rows(*me), local_sem)
        mine.start()
        first = [copy(0, me, sibling, src=x_ref)]
        first += [copy(1 + j, me, (*chip, c), src=x_ref) for j, chip in enumerate(chips)]
        for cp in first:
            cp.start()
        passed = [copy(4 + j, (*chip, c), sibling) for j, chip in enumerate(chips)]
        for j, chip in enumerate(chips):
            copy(1 + j, (*chip, c), me).wait_recv()
            passed[j].start()
        copy(0, sibling, me).wait_recv()
        for j, chip in enumerate(chips):
            copy(4 + j, (*chip, 1 - c), me).wait_recv()
        for cp in first + passed:
            cp.wait_send()
        mine.wait()

    return pl.pallas_call(
        body, name=name, in_specs=[ANY], out_specs=ANY,
        out_shape=jax.ShapeDtypeStruct((N_DEV * m, n), v.dtype),
        scratch_shapes=[pltpu.SemaphoreType.DMA((7,)), pltpu.SemaphoreType.DMA((7,)), pltpu.SemaphoreType.DMA],
        compiler_params=pltpu.CompilerParams(vmem_limit_bytes=V7X_VMEM_LIMIT_BYTES),
    )(v)


def _sibling_swap(v, name):
    def body(v_ref, out_ref, send_sem, recv_sem):
        x, y, c = lax.axis_index("x"), lax.axis_index("y"), lax.axis_index("c")
        cp = pltpu.make_async_remote_copy(src_ref=v_ref, dst_ref=out_ref, send_sem=send_sem, recv_sem=recv_sem,
                                          device_id=(x, y, 1 - c), device_id_type=MESH_IDS)
        cp.start()
        cp.wait()

    return pl.pallas_call(
        body, name=name, in_specs=[ANY], out_specs=ANY, out_shape=jax.ShapeDtypeStruct(v.shape, v.dtype),
        scratch_shapes=[pltpu.SemaphoreType.DMA, pltpu.SemaphoreType.DMA],
        compiler_params=pltpu.CompilerParams(vmem_limit_bytes=V7X_VMEM_LIMIT_BYTES),
    )(v)


def _chip_exchange(t, name):
    def body(t_ref, out_ref, send_sems, recv_sems, local_sem):
        x, y, c = lax.axis_index("x"), lax.axis_index("y"), lax.axis_index("c")
        mine = 2 * x + y
        chips = [(1 - x, y), (x, 1 - y), (1 - x, 1 - y)]
        own = pltpu.make_async_copy(t_ref.at[mine], out_ref.at[mine], local_sem)
        own.start()
        copies = [pltpu.make_async_remote_copy(
            src_ref=t_ref.at[2 * px + py], dst_ref=out_ref.at[mine], send_sem=send_sems.at[j], recv_sem=recv_sems.at[j],
            device_id=(px, py, c), device_id_type=MESH_IDS) for j, (px, py) in enumerate(chips)]
        for cp in copies:
            cp.start()
        for j, (px, py) in enumerate(chips):
            pltpu.make_async_remote_copy(
                src_ref=t_ref.at[mine], dst_ref=out_ref.at[2 * px + py], send_sem=send_sems.at[j],
                recv_sem=recv_sems.at[j], device_id=(px, py, c), device_id_type=MESH_IDS).wait_recv()
        for cp in copies:
            cp.wait_send()
        own.wait()

    return pl.pallas_call(
        body, name=name, in_specs=[ANY], out_specs=ANY, out_shape=jax.ShapeDtypeStruct(t.shape, t.dtype),
        scratch_shapes=[pltpu.SemaphoreType.DMA((3,)), pltpu.SemaphoreType.DMA((3,)), pltpu.SemaphoreType.DMA],
        compiler_params=pltpu.CompilerParams(vmem_limit_bytes=V7X_VMEM_LIMIT_BYTES),
    )(t)


def _adamw(parts, w, m, v, name):
    nparts, r, n = parts.shape
    tr = r
    for cand in (256, 176, 128, 104, 64, 8):
        if r % cand == 0:
            tr = cand
            break
    bc1 = 1.0 - ADAM_B1 ** ADAM_STEP
    bc2 = 1.0 - ADAM_B2 ** ADAM_STEP

    def body(p_ref, w_ref, m_ref, v_ref, g_ref, d_ref, nm_ref, nv_ref):
        g = p_ref[0]
        for k in range(1, nparts):
            g = g + p_ref[k]
        mn = ADAM_B1 * m_ref[...] + (1.0 - ADAM_B1) * g
        vn = ADAM_B2 * v_ref[...] + (1.0 - ADAM_B2) * (g * g)
        m_hat = mn / bc1
        v_hat = vn / bc2
        g_ref[...] = g
        nm_ref[...] = mn
        nv_ref[...] = vn
        d_ref[...] = -ADAM_LR * (m_hat / (jnp.sqrt(v_hat) + ADAM_EPS) + ADAM_WD * w_ref[...])

    blk = pl.BlockSpec((tr, n), lambda i: (i, 0))
    return pl.pallas_call(
        body, name=name, grid=(r // tr,),
        in_specs=[pl.BlockSpec((nparts, tr, n), lambda i: (0, i, 0)), blk, blk, blk],
        out_specs=[blk] * 4, out_shape=[jax.ShapeDtypeStruct((r, n), F32)] * 4,
        compiler_params=_params("parallel"),
    )(parts, w, m, v)


def _mesh_place():
    x, y, c = lax.axis_index("x"), lax.axis_index("y"), lax.axis_index("c")
    return x, y, c, [(1 - x, y), (x, 1 - y), (1 - x, 1 - y)]


def _all_gather_many(vs, name):
    na = len(vs)

    def body(*refs):
        x_refs, out_refs = refs[:na], refs[na:2 * na]
        send_sems, recv_sems, local_sems = refs[2 * na:]
        x, y, c, chips = _mesh_place()
        me, sibling = (x, y, c), (x, y, 1 - c)

        def slab(i, px, py, pc):
            return out_refs[i].at[4 * px + 2 * py + pc]

        def copy(i, k, block, to, src=None):
            return pltpu.make_async_remote_copy(
                src_ref=slab(i, *block) if src is None else src, dst_ref=slab(i, *block),
                send_sem=send_sems.at[i, k], recv_sem=recv_sems.at[i, k], device_id=to, device_id_type=MESH_IDS)

        mine = [pltpu.make_async_copy(x_refs[i], slab(i, *me), local_sems.at[i]) for i in range(na)]
        first = []
        for i in range(na):
            mine[i].start()
            first.append(copy(i, 0, me, sibling, src=x_refs[i]))
            first += [copy(i, 1 + j, me, (*chip, c), src=x_refs[i]) for j, chip in enumerate(chips)]
        for cp in first:
            cp.start()
        passed = []
        for j, chip in enumerate(chips):
            for i in range(na):
                copy(i, 1 + j, (*chip, c), me).wait_recv()
                passed.append(copy(i, 4 + j, (*chip, c), sibling))
                passed[-1].start()
        for i in range(na):
            copy(i, 0, sibling, me).wait_recv()
        for j, chip in enumerate(chips):
            for i in range(na):
                copy(i, 4 + j, (*chip, 1 - c), me).wait_recv()
        for cp in first + passed:
            cp.wait_send()
        for cp in mine:
            cp.wait()

    return pl.pallas_call(
        body, name=name, in_specs=[ANY] * na, out_specs=[ANY] * na,
        out_shape=[jax.ShapeDtypeStruct((N_DEV,) + v.shape, v.dtype) for v in vs],
        scratch_shapes=[pltpu.SemaphoreType.DMA((na, 7)), pltpu.SemaphoreType.DMA((na, 7)), pltpu.SemaphoreType.DMA((na,))],
        compiler_params=pltpu.CompilerParams(vmem_limit_bytes=V7X_VMEM_LIMIT_BYTES),
    )(*vs)


def _sibling_swap_many(gs, name):
    na = len(gs)

    def body(*refs):
        g_refs, out_refs = refs[:na], refs[na:2 * na]
        send_sems, recv_sems = refs[2 * na:]
        x, y, c, _ = _mesh_place()
        copies = [pltpu.make_async_remote_copy(
            src_ref=g_refs[i].at[2 * k + 1 - c], dst_ref=out_refs[i].at[k], send_sem=send_sems.at[i, k],
            recv_sem=recv_sems.at[i, k], device_id=(x, y, 1 - c), device_id_type=MESH_IDS)
            for i in range(na) for k in range(4)]
        for cp in copies:
            cp.start()
        for cp in copies:
            cp.wait()

    return pl.pallas_call(
        body, name=name, in_specs=[ANY] * na, out_specs=[ANY] * na,
        out_shape=[jax.ShapeDtypeStruct((4,) + g.shape[1:], g.dtype) for g in gs],
        scratch_shapes=[pltpu.SemaphoreType.DMA((na, 4)), pltpu.SemaphoreType.DMA((na, 4))],
        compiler_params=pltpu.CompilerParams(vmem_limit_bytes=V7X_VMEM_LIMIT_BYTES),
    )(*gs)


def _chip_sum(g, got, core, name):
    _, r, n = g.shape

    def body(c_ref, g_ref, got_ref, o_ref):
        o_ref[...] = g_ref[...] + got_ref[...]

    return pl.pallas_call(
        body, name=name,
        grid_spec=pltpu.PrefetchScalarGridSpec(
            num_scalar_prefetch=1, grid=(4,),
            in_specs=[pl.BlockSpec((None, r, n), lambda k, c_ref: (2 * k + c_ref[0], 0, 0)),
                      pl.BlockSpec((None, r, n), lambda k, c_ref: (k, 0, 0))],
            out_specs=pl.BlockSpec((None, r, n), lambda k, c_ref: (k, 0, 0))),
        out_shape=jax.ShapeDtypeStruct((4, r, n), g.dtype), compiler_params=_params("parallel"),
    )(core, g, got)


def _chip_exchange_many(ts, name):
    na = len(ts)

    def body(*refs):
        t_refs, out_refs = refs[:na], refs[na:2 * na]
        send_sems, recv_sems, local_sems = refs[2 * na:]
        x, y, c, chips = _mesh_place()
        mine = 2 * x + y
        own = [pltpu.make_async_copy(t_refs[i].at[mine], out_refs[i].at[mine], local_sems.at[i]) for i in range(na)]
        for cp in own:
            cp.start()
        copies = [pltpu.make_async_remote_copy(
            src_ref=t_refs[i].at[2 * px + py], dst_ref=out_refs[i].at[mine], send_sem=send_sems.at[i, j],
            recv_sem=recv_sems.at[i, j], device_id=(px, py, c), device_id_type=MESH_IDS)
            for j, (px, py) in enumerate(chips) for i in range(na)]
        for cp in copies:
            cp.start()
        for j, (px, py) in enumerate(chips):
            for i in range(na):
                pltpu.make_async_remote_copy(
                    src_ref=t_refs[i].at[mine], dst_ref=out_refs[i].at[2 * px + py], send_sem=send_sems.at[i, j],
                    recv_sem=recv_sems.at[i, j], device_id=(px, py, c), device_id_type=MESH_IDS).wait_recv()
        for cp in copies:
            cp.wait_send()
        for cp in own:
            cp.wait()

    return pl.pallas_call(
        body, name=name, in_specs=[ANY] * na, out_specs=[ANY] * na,
        out_shape=[jax.ShapeDtypeStruct(t.shape, t.dtype) for t in ts],
        scratch_shapes=[pltpu.SemaphoreType.DMA((na, 3)), pltpu.SemaphoreType.DMA((na, 3)), pltpu.SemaphoreType.DMA((na,))],
        compiler_params=pltpu.CompilerParams(vmem_limit_bytes=V7X_VMEM_LIMIT_BYTES),
    )(*ts)


BIG = ("ffn1_w_gate", "ffn1_w_up", "ffn1_w_down", "w_in", "w_out", "ffn2_w_gate", "ffn2_w_up", "ffn2_w_down")
COL_SHARDED = ("ffn1_w_gate", "ffn1_w_up", "w_in", "ffn2_w_gate", "ffn2_w_up")
SMALL = ("ffn1_norm", "mix_norm", "a_log", "dt_bias", "gdn_norm_w", "q_norm_w", "k_norm_w", "rel_bias",
         "ffn2_norm", "final_norm")
WEIGHTS = ("ffn1_norm", "ffn1_w_gate", "ffn1_w_up", "ffn1_w_down", "mix_norm", "w_in", "conv_w", "a_log", "dt_bias",
           "gdn_norm_w", "q_norm_w", "k_norm_w", "rel_bias", "w_out", "ffn2_norm", "ffn2_w_gate", "ffn2_w_up",
           "ffn2_w_down", "final_norm")
PACK_WIDTH = 1024
PACK_ROW_MULTIPLE = 32


def _pack(arrays, width, row_multiple):
    flat = jnp.concatenate([a.reshape(-1) for a in arrays])
    rows = -(-flat.shape[0] // width)
    rows = -(-rows // row_multiple) * row_multiple
    return jnp.pad(flat, (0, rows * width - flat.shape[0])).reshape(rows, width)


def _unpack(packed, shapes):
    flat = packed.reshape(-1)
    out, pos = [], 0
    for shp in shapes:
        size = int(np.prod(shp))
        out.append(flat[pos:pos + size].reshape(shp))
        pos += size
    return out


def _blocks_of(name, full):
    if name in COL_SHARDED:
        rows, cols = full.shape
        return full.reshape(rows, N_DEV, cols // N_DEV).transpose(1, 0, 2).reshape(N_DEV, -1)
    return full.reshape(N_DEV, -1)


def _full_of(name, blocks, shard_shape):
    rows, cols = shard_shape
    if name in COL_SHARDED:
        return blocks.reshape(N_DEV, rows, cols).transpose(1, 0, 2).reshape(rows, N_DEV * cols)
    return blocks.reshape(N_DEV * rows, cols)


def kernel(x, ffn1_norm, ffn1_w_gate, ffn1_w_up, ffn1_w_down, mix_norm, w_in, conv_w, a_log, dt_bias, gdn_norm_w, q_norm_w, k_norm_w, rel_bias, w_out, ffn2_norm, ffn2_w_gate, ffn2_w_up, ffn2_w_down, final_norm, loss_target, m_ffn1_norm, m_ffn1_w_gate, m_ffn1_w_up, m_ffn1_w_down, m_mix_norm, m_w_in, m_conv_w, m_a_log, m_dt_bias, m_gdn_norm_w, m_q_norm_w, m_k_norm_w, m_rel_bias, m_w_out, m_ffn2_norm, m_ffn2_w_gate, m_ffn2_w_up, m_ffn2_w_down, m_final_norm, v_ffn1_norm, v_ffn1_w_gate, v_ffn1_w_up, v_ffn1_w_down, v_mix_norm, v_w_in, v_conv_w, v_a_log, v_dt_bias, v_gdn_norm_w, v_q_norm_w, v_k_norm_w, v_rel_bias, v_w_out, v_ffn2_norm, v_ffn2_w_gate, v_ffn2_w_up, v_ffn2_w_down, v_final_norm):
    w = dict(ffn1_norm=ffn1_norm, ffn1_w_gate=ffn1_w_gate, ffn1_w_up=ffn1_w_up, ffn1_w_down=ffn1_w_down, mix_norm=mix_norm, w_in=w_in, conv_w=conv_w, a_log=a_log, dt_bias=dt_bias, gdn_norm_w=gdn_norm_w, q_norm_w=q_norm_w, k_norm_w=k_norm_w, rel_bias=rel_bias, w_out=w_out, ffn2_norm=ffn2_norm, ffn2_w_gate=ffn2_w_gate, ffn2_w_up=ffn2_w_up, ffn2_w_down=ffn2_w_down, final_norm=final_norm)
    mom = dict(ffn1_norm=m_ffn1_norm, ffn1_w_gate=m_ffn1_w_gate, ffn1_w_up=m_ffn1_w_up, ffn1_w_down=m_ffn1_w_down, mix_norm=m_mix_norm, w_in=m_w_in, conv_w=m_conv_w, a_log=m_a_log, dt_bias=m_dt_bias, gdn_norm_w=m_gdn_norm_w, q_norm_w=m_q_norm_w, k_norm_w=m_k_norm_w, rel_bias=m_rel_bias, w_out=m_w_out, ffn2_norm=m_ffn2_norm, ffn2_w_gate=m_ffn2_w_gate, ffn2_w_up=m_ffn2_w_up, ffn2_w_down=m_ffn2_w_down, final_norm=m_final_norm)
    var = dict(ffn1_norm=v_ffn1_norm, ffn1_w_gate=v_ffn1_w_gate, ffn1_w_up=v_ffn1_w_up, ffn1_w_down=v_ffn1_w_down, mix_norm=v_mix_norm, w_in=v_w_in, conv_w=v_conv_w, a_log=v_a_log, dt_bias=v_dt_bias, gdn_norm_w=v_gdn_norm_w, q_norm_w=v_q_norm_w, k_norm_w=v_k_norm_w, rel_bias=v_rel_bias, w_out=v_w_out, ffn2_norm=v_ffn2_norm, ffn2_w_gate=v_ffn2_w_gate, ffn2_w_up=v_ffn2_w_up, ffn2_w_down=v_ffn2_w_down, final_norm=v_final_norm)
    ix, iy, ic = lax.axis_index("x"), lax.axis_index("y"), lax.axis_index("c")
    me = 4 * ix + 2 * iy + ic

    shard = {n: w[n][0] for n in BIG}

    conv_shard_shape = w["conv_w"][0].shape
    conv_elems = conv_shard_shape[0] * conv_shard_shape[1]
    gathered = _all_gather_many([shard[n].astype(BF16) for n in BIG] + [_pack([w["conv_w"][0]], LANE, 8)],
                                "gather_weights")
    slabs = dict(zip(BIG, gathered))
    wts = {}
    for n in BIG:
        if n == "w_in":
            wts["w_in_pad"] = _w_in_from_slabs(slabs[n])
        elif n in COL_SHARDED:
            wts[n] = _cols_from_slabs(slabs[n], f"{n}_cols")
        else:
            wts[n] = slabs[n].reshape(N_DEV * slabs[n].shape[1], slabs[n].shape[2])

    small = {n: w[n][0] if n not in ("rel_bias",) else w[n] for n in SMALL}
    small = {n: (a.reshape(1, -1) if n.endswith("norm") else a) for n, a in small.items()}
    conv_all = gathered[-1].reshape(N_DEV, -1)
    small["conv_w"] = conv_all[:, :conv_elems].reshape(N_DEV * conv_shard_shape[0], conv_shard_shape[1])
    loss_row, grad_x, grads = _local_step(x[0], loss_target[0], wts, small)
    loss = lax.psum(loss_row[0, 0], ("x", "y", "c"))

    gots = _sibling_swap_many([grads[n] for n in BIG], "grads_to_sibling")
    core = ic.astype(jnp.int32).reshape(1)
    sums = [_chip_sum(grads[n], got, core, f"{n}_chip_sum") for n, got in zip(BIG, gots)]
    parts = _chip_exchange_many(sums, "grads_to_chips")
    big_out = [[], [], [], []]
    for n, part in zip(BIG, parts):
        for kind, val in enumerate(_adamw(part, shard[n], mom[n][0], var[n][0], f"{n}_adamw")):
            big_out[kind].append(val)

    small_names = SMALL + ("conv_w",)
    small_shapes = [grads[n].shape for n in small_names]
    g_small = _pack([grads[n] for n in small_names], LANE, 8)
    small_rows = g_small.shape[0]
    all_small = _all_gather(g_small, "gather_small_grads").reshape(N_DEV, small_rows, LANE)
    rep_shapes = [grads[n].shape for n in SMALL]
    zero_conv = jnp.zeros(small_shapes[-1], F32)
    ws = _pack([w[n].reshape(grads[n].shape) for n in SMALL] + [zero_conv], LANE, 8)
    ms = _pack([mom[n].reshape(grads[n].shape) for n in SMALL] + [zero_conv], LANE, 8)
    vs = _pack([var[n].reshape(grads[n].shape) for n in SMALL] + [zero_conv], LANE, 8)
    small_out = [_unpack(a, small_shapes) for a in _adamw(all_small, ws, ms, vs, "adamw_small")]
    conv_g = lax.dynamic_slice_in_dim(small_out[0][-1], me * conv_shard_shape[0], conv_shard_shape[0], axis=0)
    conv_out = [_unpack(a, [conv_shard_shape])[0] for a in _adamw(
        _pack([conv_g], LANE, 8)[None], _pack([w["conv_w"][0]], LANE, 8), _pack([mom["conv_w"][0]], LANE, 8),
        _pack([var["conv_w"][0]], LANE, 8), "adamw_conv")]

    def leaf(kind, n):
        if n in BIG:
            val = big_out[kind][BIG.index(n)]
        elif n == "conv_w":
            val = conv_out[kind]
        else:
            val = small_out[kind][SMALL.index(n)]
        return val.reshape(w[n].shape)

    outs = [loss, grad_x[None]]
    for kind in range(4):
        outs += [leaf(kind, n) for n in WEIGHTS]
    return tuple(outs)
```

```python
import functools
import math

import numpy as np
import jax
import jax.numpy as jnp
from jax import lax
from jax.experimental import pallas as pl
from jax.experimental.pallas import tpu as pltpu

F32 = jnp.float32
BF16 = jnp.bfloat16

D_MODEL = 1024
D_FF = 2816
GDN_HEADS = 4
GDN_HEAD_DIM = 128
GDN_WIDTH = 512
CONV_WIDTH = 5
CHUNK = 64
SWA_HEADS = 8
SWA_HEAD_DIM = 64
SWA_WIDTH = 512
DILATION_PATTERNS = ((128, 1), (512, 4), (2048, 16))
REL_BUCKETS = 32
REL_MAX_DISTANCE = 1024
EPS = 1e-6
NEG_BIG = -1e30
N_DEV = 8

ADAM_LR = 0.001
ADAM_B1 = 0.9
ADAM_B2 = 0.999
ADAM_EPS = 1e-08
ADAM_WD = 0.01
ADAM_STEP = 10

QKV_A = 3 * GDN_WIDTH
OFF_Z = QKV_A
OFF_B = OFF_Z + GDN_WIDTH
OFF_AB = OFF_B + 3 * SWA_WIDTH
N_PAD = OFF_AB + 128
N_IN = 3600

V7X_VMEM_LIMIT_BYTES = 56 * 1024 * 1024
LANE = 128
ATT_BQ = 128
ATT_HALO = 64
CONV_ROWS = 256

NN = (((1,), (0,)), ((), ()))
NT = (((1,), (1,)), ((), ()))
TN = (((0,), (0,)), ((), ()))


def _params(*sem):
    return pltpu.CompilerParams(dimension_semantics=sem, vmem_limit_bytes=V7X_VMEM_LIMIT_BYTES)


def _dot(a, b, dn=NN):
    return lax.dot_general(a.astype(BF16), b.astype(BF16), dn, preferred_element_type=F32)


def _dot_hi(a, b, dn=NN):
    return lax.dot_general(a, b, dn, precision=lax.Precision.HIGHEST, preferred_element_type=F32)


def _sigmoid(x):
    return 1.0 / (1.0 + jnp.exp(-x))


def _matmul(pairs, *, ta=False, tb=False, out_dtype=F32, tm, tn, tk, name, res=None, alpha=None, shard_cols=None):
    a0, b0 = pairs[0]
    m = a0.shape[1] if ta else a0.shape[0]
    k = a0.shape[0] if ta else a0.shape[1]
    n = b0.shape[0] if tb else b0.shape[1]
    tm, tn, tk = min(tm, m), min(tn, n), min(tk, k)
    assert m % tm == 0 and n % tn == 0 and k % tk == 0, (name, m, n, k, tm, tn, tk)
    nk = k // tk
    npairs = len(pairs)
    dn = (((0 if ta else 1,), (1 if tb else 0,)), ((), ()))

    def body(*refs):
        ins = refs[:2 * npairs]
        pos = 2 * npairs
        r_ref = None
        if res is not None:
            r_ref = refs[pos]
            pos += 1
        o_ref, acc = refs[pos], refs[pos + 1]
        kk = pl.program_id(2)

        @pl.when(kk == 0)
        def _():
            acc[...] = jnp.zeros_like(acc)

        t = None
        for p in range(npairs):
            d = _dot(ins[2 * p][...], ins[2 * p + 1][...], dn)
            t = d if t is None else t + d
        acc[...] += t

        @pl.when(kk == nk - 1)
        def _():
            r = acc[...]
            if alpha is not None:
                r = r * alpha
            if r_ref is not None:
                r = r_ref[...] + r
            if shard_cols is None:
                o_ref[...] = r.astype(out_dtype)
            else:
                for sh in range(tn // shard_cols):
                    o_ref[sh] = r[:, sh * shard_cols:(sh + 1) * shard_cols].astype(out_dtype)

    a_spec = pl.BlockSpec((tk, tm), lambda i, j, kk: (kk, i)) if ta else pl.BlockSpec((tm, tk), lambda i, j, kk: (i, kk))
    b_spec = pl.BlockSpec((tn, tk), lambda i, j, kk: (j, kk)) if tb else pl.BlockSpec((tk, tn), lambda i, j, kk: (kk, j))
    o_spec = pl.BlockSpec((tm, tn), lambda i, j, kk: (i, j))
    in_specs = [a_spec, b_spec] * npairs + ([o_spec] if res is not None else [])
    args = [t for pr in pairs for t in pr] + ([res] if res is not None else [])
    out_spec, out_shape = o_spec, (m, n)
    if shard_cols is not None:
        assert res is None and tn % shard_cols == 0
        out_spec = pl.BlockSpec((tn // shard_cols, tm, shard_cols), lambda i, j, kk: (j, i, 0))
        out_shape = (n // shard_cols, m, shard_cols)
    return pl.pallas_call(
        body, name=name, grid=(m // tm, n // tn, nk), in_specs=in_specs, out_specs=out_spec,
        out_shape=jax.ShapeDtypeStruct(out_shape, out_dtype), scratch_shapes=[pltpu.VMEM((tm, tn), F32)],
        compiler_params=_params("parallel", "parallel", "arbitrary"),
    )(*args)


def _rms_fwd(x, w, name):
    s, d = x.shape
    tm = min(512, s)

    def body(x_ref, w_ref, n_ref, r_ref):
        xv = x_ref[...]
        r = lax.rsqrt(jnp.mean(xv * xv, axis=-1, keepdims=True) + EPS)
        n_ref[...] = (xv * r * w_ref[...]).astype(BF16)
        r_ref[...] = r

    return pl.pallas_call(
        body, name=name, grid=(s // tm,),
        in_specs=[pl.BlockSpec((tm, d), lambda i: (i, 0)), pl.BlockSpec((1, d), lambda i: (0, 0))],
        out_specs=[pl.BlockSpec((tm, d), lambda i: (i, 0)), pl.BlockSpec((tm, 1), lambda i: (i, 0))],
        out_shape=[jax.ShapeDtypeStruct((s, d), BF16), jax.ShapeDtypeStruct((s, 1), F32)],
        compiler_params=_params("parallel"),
    )(x, w)


def _rms_bwd(dn, x, r, w, dres, name):
    s, d = x.shape
    tm = min(512, s)

    def body(dn_ref, x_ref, r_ref, w_ref, dres_ref, dx_ref, dw_ref):
        @pl.when(pl.program_id(0) == 0)
        def _():
            dw_ref[...] = jnp.zeros_like(dw_ref)

        rv = r_ref[...]
        xhat = x_ref[...] * rv
        g = dn_ref[...]
        t = g * w_ref[...]
        dx_ref[...] = dres_ref[...] + rv * (t - xhat * jnp.mean(t * xhat, axis=-1, keepdims=True))
        dw_ref[...] += jnp.sum(g * xhat, axis=0, keepdims=True)

    row = pl.BlockSpec((tm, d), lambda i: (i, 0))
    vec = pl.BlockSpec((1, d), lambda i: (0, 0))
    return pl.pallas_call(
        body, name=name, grid=(s // tm,),
        in_specs=[row, row, pl.BlockSpec((tm, 1), lambda i: (i, 0)), vec, row],
        out_specs=[row, vec],
        out_shape=[jax.ShapeDtypeStruct((s, d), F32), jax.ShapeDtypeStruct((1, d), F32)],
        compiler_params=_params("arbitrary"),
    )(dn, x, r, w, dres)


def _final_loss(x3, wf, tgt):
    s, d = x3.shape
    tm = min(512, s)

    def body(x_ref, w_ref, t_ref, loss_ref, dx_ref, dw_ref):
        @pl.when(pl.program_id(0) == 0)
        def _():
            dw_ref[...] = jnp.zeros_like(dw_ref)
            loss_ref[...] = jnp.zeros_like(loss_ref)

        xv = x_ref[...]
        wv = w_ref[...]
        r = lax.rsqrt(jnp.mean(xv * xv, axis=-1, keepdims=True) + EPS)
        xhat = xv * r
        e = xhat * wv - t_ref[...]
        part = 0.5 * jnp.sum(jnp.mean(e * e, axis=-1, keepdims=True), axis=0, keepdims=True)
        loss_ref[...] += jnp.broadcast_to(part, loss_ref.shape)
        dy = e * (1.0 / d)
        dw_ref[...] += jnp.sum(dy * xhat, axis=0, keepdims=True)
        t = dy * wv
        dx_ref[...] = r * (t - xhat * jnp.mean(t * xhat, axis=-1, keepdims=True))

    row = pl.BlockSpec((tm, d), lambda i: (i, 0))
    vec = pl.BlockSpec((1, d), lambda i: (0, 0))
    return pl.pallas_call(
        body, name="final_loss", grid=(s // tm,),
        in_specs=[row, vec, row],
        out_specs=[pl.BlockSpec((1, LANE), lambda i: (0, 0)), row, vec],
        out_shape=[jax.ShapeDtypeStruct((1, LANE), F32), jax.ShapeDtypeStruct((s, d), F32),
                   jax.ShapeDtypeStruct((1, d), F32)],
        compiler_params=_params("arbitrary"),
    )(x3, wf, tgt)


def _ffn_up(n, wg, wu, name):
    s, d = n.shape
    f = wg.shape[1]
    tm, tn = min(512, s), f // 2

    def body(n_ref, wg_ref, wu_ref, g_ref, u_ref, a_ref):
        nv = n_ref[...]
        g = _dot(nv, wg_ref[...])
        u = _dot(nv, wu_ref[...])
        g_ref[...] = g
        u_ref[...] = u
        a_ref[...] = (g * _sigmoid(g) * u).astype(BF16)

    o = pl.BlockSpec((tm, tn), lambda i, j: (i, j))
    wspec = pl.BlockSpec((d, tn), lambda i, j: (0, j))
    return pl.pallas_call(
        body, name=name, grid=(s // tm, f // tn),
        in_specs=[pl.BlockSpec((tm, d), lambda i, j: (i, 0)), wspec, wspec],
        out_specs=[o, o, o],
        out_shape=[jax.ShapeDtypeStruct((s, f), F32), jax.ShapeDtypeStruct((s, f), F32),
                   jax.ShapeDtypeStruct((s, f), BF16)],
        compiler_params=_params("parallel", "parallel"),
    )(n, wg, wu)


def _ffn_dact(dx, wd, g, u, name):
    s, d = dx.shape
    f = wd.shape[0]
    tm, tn = min(512, s), f // 2

    def body(dx_ref, wd_ref, g_ref, u_ref, dg_ref, du_ref):
        da = 0.5 * _dot(dx_ref[...], wd_ref[...], NT)
        gv = g_ref[...]
        sg = _sigmoid(gv)
        du_ref[...] = (da * gv * sg).astype(BF16)
        dg_ref[...] = (da * u_ref[...] * (sg * (1.0 + gv * (1.0 - sg)))).astype(BF16)

    o = pl.BlockSpec((tm, tn), lambda i, j: (i, j))
    return pl.pallas_call(
        body, name=name, grid=(s // tm, f // tn),
        in_specs=[pl.BlockSpec((tm, d), lambda i, j: (i, 0)), pl.BlockSpec((tn, d), lambda i, j: (j, 0)), o, o],
        out_specs=[o, o],
        out_shape=[jax.ShapeDtypeStruct((s, f), BF16), jax.ShapeDtypeStruct((s, f), BF16)],
        compiler_params=_params("parallel", "parallel"),
    )(dx, wd, g, u)


def _ffn_forward(x, norm_w, wg, wu, wd, tag):
    n, r = _rms_fwd(x, norm_w, f"{tag}_norm")
    g, u, a = _ffn_up(n, wg, wu, f"{tag}_up")
    y = _matmul([(a, wd)], tm=512, tn=1024, tk=1408, name=f"{tag}_down", res=x, alpha=0.5)
    return y, (n, r, g, u, a)


def _ffn_backward(dy, x, norm_w, wg, wu, wd, saved, tag):
    n, r, g, u, a = saved
    dwd = _matmul([(a, dy)], ta=True, tm=1408, tn=1024, tk=512, name=f"{tag}_dwd", alpha=0.5)
    dg, du = _ffn_dact(dy, wd, g, u, f"{tag}_dact")
    cols = wg.shape[1] // N_DEV
    dwg = _matmul([(n, dg)], ta=True, tm=1024, tn=1408, tk=512, name=f"{tag}_dwg", shard_cols=cols)
    dwu = _matmul([(n, du)], ta=True, tm=1024, tn=1408, tk=512, name=f"{tag}_dwu", shard_cols=cols)
    dn = _matmul([(dg, wg), (du, wu)], tb=True, tm=512, tn=1024, tk=1408, name=f"{tag}_dn")
    dx, dnorm = _rms_bwd(dn, x, r, norm_w, dy, f"{tag}_dnorm")
    return dx, dnorm, dwg, dwu, dwd


Q_SCALE = GDN_HEAD_DIM ** -0.5
CONV_HALO = 8


def _conv_taps(win, w_ref, rows, sign):
    n = rows + 2 * CONV_HALO
    acc = None
    for t in range(CONV_WIDTH):
        o = sign * (t - CONV_WIDTH // 2)
        sh = win if o == 0 else pltpu.roll(win, (-o) % n, 0)
        term = sh[CONV_HALO:CONV_HALO + rows] * w_ref[t:t + 1, :]
        acc = term if acc is None else acc + term
    return acc


def _gdn_conv_fwd(p_pad, conv_wt):
    s = p_pad.shape[0]
    rows = min(CONV_ROWS, s)
    nblk = QKV_A // LANE

    def body(p_ref, w_ref, c_ref, y_ref, pad):
        j = pl.program_id(0)
        zeros = jnp.zeros((CONV_HALO, LANE), F32)
        pad[0:CONV_HALO, :] = zeros
        pad[CONV_HALO + s:2 * CONV_HALO + s, :] = zeros
        pad[CONV_HALO:CONV_HALO + s, :] = p_ref[...]

        def chunk(ci, carry):
            b = pl.multiple_of(ci * rows, rows)
            win = pad[pl.ds(b, rows + 2 * CONV_HALO), :]
            c = _conv_taps(win, w_ref, rows, 1)
            c_ref[pl.ds(b, rows), :] = c
            act = c * _sigmoid(c)
            nrm = lax.rsqrt(jnp.sum(act * act, axis=-1, keepdims=True) + EPS)
            mult = jnp.where(j < GDN_HEADS, nrm * Q_SCALE, jnp.where(j < 2 * GDN_HEADS, nrm, 1.0))
            y_ref[pl.ds(b, rows), :] = act * mult
            return carry

        lax.fori_loop(0, s // rows, chunk, 0)

    col = pl.BlockSpec((s, LANE), lambda j: (0, j))
    return pl.pallas_call(
        body, name="gdn_conv_fwd", grid=(nblk,),
        in_specs=[col, pl.BlockSpec((8, LANE), lambda j: (0, j))],
        out_specs=[col, col],
        out_shape=[jax.ShapeDtypeStruct((s, QKV_A), F32), jax.ShapeDtypeStruct((s, QKV_A), F32)],
        scratch_shapes=[pltpu.VMEM((s + 2 * CONV_HALO, LANE), F32)],
        compiler_params=_params("parallel"),
    )(p_pad, conv_wt)


def _gdn_conv_bwd(dy_f, dy_r, c_pre, p_pad, conv_wt):
    s = p_pad.shape[0]
    rows = min(CONV_ROWS, s)
    nblk = QKV_A // LANE

    def body(dyf_ref, dyr_ref, c_ref, p_ref, w_ref, dp_ref, dw_ref, ppad, dcpad):
        j = pl.program_id(0)
        zeros = jnp.zeros((CONV_HALO, LANE), F32)
        for buf in (ppad, dcpad):
            buf[0:CONV_HALO, :] = zeros
            buf[CONV_HALO + s:2 * CONV_HALO + s, :] = zeros
        ppad[CONV_HALO:CONV_HALO + s, :] = p_ref[...]

        def act_bwd(ci, carry):
            b = pl.multiple_of(ci * rows, rows)
            c = c_ref[pl.ds(b, rows), :]
            g = dyf_ref[pl.ds(b, rows), :] + dyr_ref[pl.ds(b, rows), :]
            sg = _sigmoid(c)
            act = c * sg
            nrm = lax.rsqrt(jnp.sum(act * act, axis=-1, keepdims=True) + EPS)
            yh = act * nrm
            scale = jnp.where(j < GDN_HEADS, Q_SCALE, 1.0)
            dact_qk = (scale * nrm) * (g - yh * jnp.sum(g * yh, axis=-1, keepdims=True))
            dact = jnp.where(j < 2 * GDN_HEADS, dact_qk, g)
            dcpad[pl.ds(pl.multiple_of(b + CONV_HALO, CONV_HALO), rows), :] = dact * (sg * (1.0 + c * (1.0 - sg)))
            return carry

        lax.fori_loop(0, s // rows, act_bwd, 0)
        tap = lax.broadcasted_iota(jnp.int32, (8, LANE), 0)

        def taps_bwd(ci, dw):
            b = pl.multiple_of(ci * rows, rows)
            dcw = dcpad[pl.ds(b, rows + 2 * CONV_HALO), :]
            dp_ref[pl.ds(b, rows), :] = _conv_taps(dcw, w_ref, rows, -1)
            pw = ppad[pl.ds(b, rows + 2 * CONV_HALO), :]
            dc = dcw[CONV_HALO:CONV_HALO + rows]
            n = rows + 2 * CONV_HALO
            for t in range(CONV_WIDTH):
                o = t - CONV_WIDTH // 2
                sh = pw if o == 0 else pltpu.roll(pw, (-o) % n, 0)
                row = jnp.sum(dc * sh[CONV_HALO:CONV_HALO + rows], axis=0, keepdims=True)
                dw = dw + jnp.where(tap == t, row, 0.0)
            return dw

        dw_ref[...] = lax.fori_loop(0, s // rows, taps_bwd, jnp.zeros((8, LANE), F32))

    col = pl.BlockSpec((s, LANE), lambda j: (0, j))
    wspec = pl.BlockSpec((8, LANE), lambda j: (0, j))
    return pl.pallas_call(
        body, name="gdn_conv_bwd", grid=(nblk,),
        in_specs=[col, col, col, col, wspec],
        out_specs=[col, wspec],
        out_shape=[jax.ShapeDtypeStruct((s, QKV_A), F32), jax.ShapeDtypeStruct((8, QKV_A), F32)],
        scratch_shapes=[pltpu.VMEM((s + 2 * CONV_HALO, LANE), F32), pltpu.VMEM((s + 2 * CONV_HALO, LANE), F32)],
        compiler_params=_params("parallel"),
    )(dy_f, dy_r, c_pre, p_pad, conv_wt)


def _softplus(x):
    return jnp.maximum(x, 0.0) + jnp.log(1.0 + jnp.exp(-jnp.abs(x)))


def _gdn_gates_fwd(p_pad, alog_row, dt_row):
    s = p_pad.shape[0]
    tm = min(1024, s)

    def body(p_ref, al_ref, dt_ref, o_ref):
        x = p_ref[...]
        lane = lax.broadcasted_iota(jnp.int32, x.shape, 1)
        g = -jnp.exp(al_ref[...]) * _softplus(x + dt_ref[...])
        o_ref[...] = jnp.where(lane < 8, g, jnp.where(lane < 16, _sigmoid(x), 0.0))

    vec = pl.BlockSpec((1, LANE), lambda i: (0, 0))
    return pl.pallas_call(
        body, name="gdn_gates_fwd", grid=(s // tm,),
        in_specs=[pl.BlockSpec((tm, LANE), lambda i: (i, OFF_AB // LANE)), vec, vec],
        out_specs=pl.BlockSpec((tm, LANE), lambda i: (i, 0)),
        out_shape=jax.ShapeDtypeStruct((s, LANE), F32),
        compiler_params=_params("parallel"),
    )(p_pad, alog_row, dt_row)


def _gdn_gates_bwd(dgb_f, dgb_r, p_pad, gb, alog_row, dt_row):
    s = p_pad.shape[0]
    tm = min(1024, s)

    def body(df_ref, dr_ref, p_ref, gb_ref, al_ref, dt_ref, dp_ref, sum_ref):
        @pl.when(pl.program_id(0) == 0)
        def _():
            sum_ref[...] = jnp.zeros_like(sum_ref)

        x = p_ref[...]
        gbv = gb_ref[...]
        dgb = df_ref[...] + dr_ref[...]
        lane = lax.broadcasted_iota(jnp.int32, x.shape, 1)
        da = dgb * (-jnp.exp(al_ref[...])) * _sigmoid(x + dt_ref[...])
        db = dgb * gbv * (1.0 - gbv)
        dp_ref[...] = jnp.where(lane < 8, da, jnp.where(lane < 16, db, 0.0))
        row = lax.broadcasted_iota(jnp.int32, (8, LANE), 0)
        lane8 = lax.broadcasted_iota(jnp.int32, (8, LANE), 1)
        d_alog = jnp.sum(dgb * gbv, axis=0, keepdims=True)
        d_dt = jnp.sum(da, axis=0, keepdims=True)
        upd = jnp.where(row == 0, d_alog, jnp.where(row == 1, d_dt, 0.0))
        sum_ref[...] += jnp.where(lane8 < 8, upd, 0.0)

    vec = pl.BlockSpec((1, LANE), lambda i: (0, 0))
    blk = pl.BlockSpec((tm, LANE), lambda i: (i, 0))
    return pl.pallas_call(
        body, name="gdn_gates_bwd", grid=(s // tm,),
        in_specs=[blk, blk, pl.BlockSpec((tm, LANE), lambda i: (i, OFF_AB // LANE)), blk, vec, vec],
        out_specs=[blk, pl.BlockSpec((8, LANE), lambda i: (0, 0))],
        out_shape=[jax.ShapeDtypeStruct((s, LANE), F32), jax.ShapeDtypeStruct((8, LANE), F32)],
        compiler_params=_params("arbitrary"),
    )(dgb_f, dgb_r, p_pad, gb, alog_row, dt_row)


def _chunk_masks(rev):
    row = lax.broadcasted_iota(jnp.int32, (CHUNK, CHUNK), 0)
    col = lax.broadcasted_iota(jnp.int32, (CHUNK, CHUNK), 1)
    le = (col >= row) if rev else (col <= row)
    strict = (col > row) if rev else (col < row)
    return le, strict, row == col


def _chunk_common(q, k, v, g, beta, gc, masks):
    le, strict, eye = masks
    gc_row = _dot_hi(jnp.ones((CHUNK, CHUNK), F32), jnp.where(eye, gc, 0.0))
    decay = jnp.where(le, jnp.exp(jnp.where(le, gc - gc_row, 0.0)), 0.0)
    eg = jnp.exp(gc)
    gl = jnp.sum(g, axis=0, keepdims=True)
    kb = k * beta
    vb = v * beta
    kbeg = kb * eg
    lm = jnp.where(strict, _dot(kb, k, NT) * decay, 0.0)
    intra = _dot(q, k, NT) * decay
    qg = q * eg
    edec = jnp.exp(gl - gc)
    kdec = k * edec
    return dict(decay=decay, eg=eg, gl=gl, kb=kb, vb=vb, kbeg=kbeg, lm=lm, intra=intra, qg=qg, edec=edec, kdec=kdec)


def _unit_lower_inverse(lm, eye):
    x = -lm
    t = eye.astype(F32) + x
    p = x
    for _ in range(5):
        p = _dot_hi(p, p)
        t = t + _dot_hi(t, p)
    return t


def _gate_lanes(rev, h):
    d = 1 if rev else 0
    return d * GDN_HEADS + h, 8 + d * GDN_HEADS + h


def _delta_fwd(y, gb, rev):
    s = y.shape[0]
    nc = s // CHUNK
    hd = GDN_HEAD_DIM

    def chunk_of(n):
        return nc - 1 - n if rev else n

    def body(q_ref, k_ref, v_ref, gb_ref, o_ref, s_all, t_all, state):
        @pl.when(pl.program_id(0) == 0)
        def _():
            state[...] = jnp.zeros_like(state)

        masks = _chunk_masks(rev)
        gbv = gb_ref[...]
        gcm = _dot_hi(masks[0].astype(F32), gbv)
        for h in range(GDN_HEADS):
            gi, bi = _gate_lanes(rev, h)
            sl = slice(h * hd, (h + 1) * hd)
            q, k, v = q_ref[:, sl], k_ref[:, sl], v_ref[:, sl]
            g, beta, gc = gbv[:, gi:gi + 1], gbv[:, bi:bi + 1], gcm[:, gi:gi + 1]
            cm = _chunk_common(q, k, v, g, beta, gc, masks)
            tinv = _unit_lower_inverse(cm["lm"], masks[2])
            u = _dot(tinv, cm["vb"])
            w = _dot(tinv, cm["kbeg"])
            st = state[h]
            v_new = u - _dot(w, st)
            o_ref[:, sl] = _dot(cm["qg"], st) + _dot(cm["intra"], v_new)
            s_all[0, h] = st
            t_all[0, h] = tinv
            state[h] = st * jnp.exp(cm["gl"]) + _dot(cm["kdec"], v_new, TN)

    def col(j):
        return pl.BlockSpec((CHUNK, GDN_WIDTH), lambda n: (chunk_of(n), j))

    return pl.pallas_call(
        body, name="delta_fwd_r" if rev else "delta_fwd_f", grid=(nc,),
        in_specs=[col(0), col(1), col(2), pl.BlockSpec((CHUNK, LANE), lambda n: (chunk_of(n), 0))],
        out_specs=[pl.BlockSpec((CHUNK, GDN_WIDTH), lambda n: (chunk_of(n), 0)),
                   pl.BlockSpec((1, GDN_HEADS, hd, hd), lambda n: (chunk_of(n), 0, 0, 0)),
                   pl.BlockSpec((1, GDN_HEADS, CHUNK, CHUNK), lambda n: (chunk_of(n), 0, 0, 0))],
        out_shape=[jax.ShapeDtypeStruct((s, GDN_WIDTH), F32),
                   jax.ShapeDtypeStruct((nc, GDN_HEADS, hd, hd), F32),
                   jax.ShapeDtypeStruct((nc, GDN_HEADS, CHUNK, CHUNK), F32)],
        scratch_shapes=[pltpu.VMEM((GDN_HEADS, hd, hd), F32)],
        compiler_params=_params("arbitrary"),
    )(y, y, y, gb)


def _delta_bwd(y, gb, do, s_all, t_all, rev):
    s = y.shape[0]
    nc = s // CHUNK
    hd = GDN_HEAD_DIM

    def chunk_of(n):
        return n if rev else nc - 1 - n

    def body(q_ref, k_ref, v_ref, gb_ref, do_ref, s_ref, t_ref, dy_ref, dgb_ref, dstate):
        @pl.when(pl.program_id(0) == 0)
        def _():
            dstate[...] = jnp.zeros_like(dstate)

        masks = _chunk_masks(rev)
        le, strict, _ = masks
        le_t = _chunk_masks(not rev)[0].astype(F32)
        gbv = gb_ref[...]
        gcm = _dot_hi(le.astype(F32), gbv)
        lane = lax.broadcasted_iota(jnp.int32, (CHUNK, LANE), 1)
        ones_cl = jnp.ones((CHUNK, LANE), F32)
        dgc_tile = jnp.zeros((CHUNK, LANE), F32)
        rest_tile = jnp.zeros((CHUNK, LANE), F32)
        for h in range(GDN_HEADS):
            gi, bi = _gate_lanes(rev, h)
            sl = slice(h * hd, (h + 1) * hd)
            q, k, v = q_ref[:, sl], k_ref[:, sl], v_ref[:, sl]
            g, beta, gc = gbv[:, gi:gi + 1], gbv[:, bi:bi + 1], gcm[:, gi:gi + 1]
            cm = _chunk_common(q, k, v, g, beta, gc, masks)
            tinv = t_ref[0, h]
            st = s_ref[0, h]
            ds_out = dstate[h]
            dov = do_ref[:, sl]
            u = _dot(tinv, cm["vb"])
            w = _dot(tinv, cm["kbeg"])
            v_new = u - _dot(w, st)
            egl = jnp.exp(cm["gl"])
            d_qg = _dot(dov, st, NT)
            d_intra = _dot(dov, v_new, NT)
            dv_new = _dot(cm["intra"], dov, TN) + _dot(cm["kdec"], ds_out)
            d_kdec = _dot(v_new, ds_out, NT)
            dstate[h] = _dot(cm["qg"], dov, TN) + egl * ds_out - _dot(w, dv_new, TN)
            dgl = egl * jnp.sum(jnp.sum(st * ds_out, axis=1, keepdims=True), axis=0, keepdims=True)
            dw = -_dot(dv_new, st, NT)
            dvb = _dot(tinv, dv_new, TN)
            dkbeg = _dot(tinv, dw, TN)
            dlm = jnp.where(strict, -(_dot(dvb, u, NT) + _dot(dkbeg, w, NT)), 0.0)
            d_a = dlm * cm["decay"]
            d_qk = d_intra * cm["decay"]
            e = dlm * cm["lm"] + d_intra * cm["intra"]
            dgc = jnp.sum(e, axis=1, keepdims=True) - _dot_hi(e, ones_cl, TN)[:, 0:1]
            dkb = _dot(d_a, k) + dkbeg * cm["eg"]
            dk = _dot(d_a, cm["kb"], TN) + _dot(d_qk, q, TN)
            dq = _dot(d_qk, k) + d_qg * cm["eg"]
            dgc = dgc + jnp.sum(d_qg * cm["qg"], axis=1, keepdims=True)
            dgc = dgc + jnp.sum(dkbeg * cm["kbeg"], axis=1, keepdims=True)
            tdec = jnp.sum(d_kdec * cm["kdec"], axis=1, keepdims=True)
            dk = dk + d_kdec * cm["edec"] + dkb * beta
            dgc = dgc - tdec
            dgl = dgl + jnp.sum(tdec, axis=0, keepdims=True)
            dbeta = jnp.sum(dvb * v, axis=1, keepdims=True) + jnp.sum(dkb * k, axis=1, keepdims=True)
            dy_ref[:, h * hd:(h + 1) * hd] = dq
            dy_ref[:, GDN_WIDTH + h * hd:GDN_WIDTH + (h + 1) * hd] = dk
            dy_ref[:, 2 * GDN_WIDTH + h * hd:2 * GDN_WIDTH + (h + 1) * hd] = dvb * beta
            dgc_tile = dgc_tile + jnp.where(lane == gi, dgc, 0.0)
            rest_tile = rest_tile + jnp.where(lane == gi, dgl, 0.0) + jnp.where(lane == bi, dbeta, 0.0)
        dgb_ref[...] = _dot_hi(le_t, dgc_tile) + rest_tile

    def col(j):
        return pl.BlockSpec((CHUNK, GDN_WIDTH), lambda n: (chunk_of(n), j))

    first = pl.BlockSpec((CHUNK, GDN_WIDTH), lambda n: (chunk_of(n), 0))
    return pl.pallas_call(
        body, name="delta_bwd_r" if rev else "delta_bwd_f", grid=(nc,),
        in_specs=[col(0), col(1), col(2), pl.BlockSpec((CHUNK, LANE), lambda n: (chunk_of(n), 0)), first,
                  pl.BlockSpec((1, GDN_HEADS, hd, hd), lambda n: (chunk_of(n), 0, 0, 0)),
                  pl.BlockSpec((1, GDN_HEADS, CHUNK, CHUNK), lambda n: (chunk_of(n), 0, 0, 0))],
        out_specs=[pl.BlockSpec((CHUNK, QKV_A), lambda n: (chunk_of(n), 0)),
                   pl.BlockSpec((CHUNK, LANE), lambda n: (chunk_of(n), 0))],
        out_shape=[jax.ShapeDtypeStruct((s, QKV_A), F32), jax.ShapeDtypeStruct((s, LANE), F32)],
        scratch_shapes=[pltpu.VMEM((GDN_HEADS, hd, hd), F32)],
        compiler_params=_params("arbitrary"),
    )(y, y, y, gb, do, s_all, t_all)


BNN = (((2,), (1,)), ((0,), (0,)))
BNT = (((2,), (2,)), ((0,), (0,)))
BTN = (((1,), (1,)), ((0,), (0,)))
NB = 2 * GDN_HEADS


def _bdot(a, b, dn=BNN):
    return lax.dot_general(a.astype(BF16), b.astype(BF16), dn, preferred_element_type=F32)


def _dot3(a, b, dn):
    ah = a.astype(BF16)
    al = (a - ah.astype(F32)).astype(BF16)
    bh = b.astype(BF16)
    bl = (b - bh.astype(F32)).astype(BF16)

    def d(x, y):
        return lax.dot_general(x, y, dn, preferred_element_type=F32)

    return d(ah, bh) + d(ah, bl) + d(al, bh)


def _both(f_val, r_val):
    return jnp.stack([f_val] * GDN_HEADS + [r_val] * GDN_HEADS)


def _heads(ref_f, ref_r):
    hd = GDN_HEAD_DIM
    return jnp.stack([ref_f[:, h * hd:(h + 1) * hd] for h in range(GDN_HEADS)]
                     + [ref_r[:, h * hd:(h + 1) * hd] for h in range(GDN_HEADS)])


def _gate_cols(tile_f, tile_r, base):
    return jnp.stack([tile_f[:, base + h:base + h + 1] for h in range(GDN_HEADS)]
                     + [tile_r[:, base + GDN_HEADS + h:base + GDN_HEADS + h + 1] for h in range(GDN_HEADS)])


def _chunk_common2(q, k, v, gbf, gbr):
    mf, mr = _chunk_masks(False), _chunk_masks(True)
    le, strict = _both(mf[0], mr[0]), _both(mf[1], mr[1])
    eye = mf[2]
    gcm_f = _dot3(mf[0].astype(F32), gbf, NN)
    gcm_r = _dot3(mr[0].astype(F32), gbr, NN)
    g, beta, gc = _gate_cols(gbf, gbr, 0), _gate_cols(gbf, gbr, 8), _gate_cols(gcm_f, gcm_r, 0)
    gc_row = _dot3(jnp.ones((NB, CHUNK, CHUNK), F32), jnp.where(eye[None], gc, 0.0), BNN)
    decay = jnp.where(le, jnp.exp(jnp.where(le, gc - gc_row, 0.0)), 0.0)
    eg = jnp.exp(gc)
    gl = jnp.sum(g, axis=1, keepdims=True)
    kb = k * beta
    vb = v * beta
    kbeg = kb * eg
    lm = jnp.where(strict, _bdot(kb, k, BNT) * decay, 0.0)
    intra = _bdot(q, k, BNT) * decay
    edec = jnp.exp(gl - gc)
    return dict(strict=strict, eye=eye, beta=beta, decay=decay, eg=eg, gl=gl, kb=kb, vb=vb, kbeg=kbeg,
                lm=lm, intra=intra, qg=q * eg, edec=edec, kdec=k * edec)


def _unit_triangular_inverse(lm, eye):
    x = -lm
    t = eye[None].astype(F32) + x
    p = x
    for _ in range(5):
        p = _dot3(p, p, BNN)
        t = t + _dot3(t, p, BNN)
    return t


def _delta_fwd2(y, gb):
    s = y.shape[0]
    nc = s // CHUNK
    hd = GDN_HEAD_DIM

    def body(qf, kf, vf, gf, qr, kr, vr, gr, of_ref, or_ref, sf_all, sr_all, tf_all, tr_all, state):
        @pl.when(pl.program_id(0) == 0)
        def _():
            state[...] = jnp.zeros_like(state)

        q, k, v = _heads(qf, qr), _heads(kf, kr), _heads(vf, vr)
        cm = _chunk_common2(q, k, v, gf[...], gr[...])
        tinv = _unit_triangular_inverse(cm["lm"], cm["eye"])
        u = _bdot(tinv, cm["vb"])
        w = _bdot(tinv, cm["kbeg"])
        st = state[...]
        v_new = u - _bdot(w, st)
        o = _bdot(cm["qg"], st) + _bdot(cm["intra"], v_new)
        state[...] = st * jnp.exp(cm["gl"]) + _bdot(cm["kdec"], v_new, BTN)
        for h in range(GDN_HEADS):
            of_ref[:, h * hd:(h + 1) * hd] = o[h]
            or_ref[:, h * hd:(h + 1) * hd] = o[GDN_HEADS + h]
        sf_all[0] = st[:GDN_HEADS]
        sr_all[0] = st[GDN_HEADS:]
        tf_all[0] = tinv[:GDN_HEADS]
        tr_all[0] = tinv[GDN_HEADS:]

    def col(j, rev):
        return pl.BlockSpec((CHUNK, GDN_WIDTH), (lambda n: (nc - 1 - n, j)) if rev else (lambda n: (n, j)))

    def gate(rev):
        return pl.BlockSpec((CHUNK, LANE), (lambda n: (nc - 1 - n, 0)) if rev else (lambda n: (n, 0)))

    def per_chunk(d1, d2, rev):
        return pl.BlockSpec((1, GDN_HEADS, d1, d2), (lambda n: (nc - 1 - n, 0, 0, 0)) if rev else (lambda n: (n, 0, 0, 0)))

    return pl.pallas_call(
        body, name="delta_fwd", grid=(nc,),
        in_specs=[col(0, False), col(1, False), col(2, False), gate(False), col(0, True), col(1, True), col(2, True), gate(True)],
        out_specs=[col(0, False), col(0, True), per_chunk(hd, hd, False), per_chunk(hd, hd, True),
                   per_chunk(CHUNK, CHUNK, False), per_chunk(CHUNK, CHUNK, True)],
        out_shape=[jax.ShapeDtypeStruct((s, GDN_WIDTH), F32)] * 2 + [jax.ShapeDtypeStruct((nc, GDN_HEADS, hd, hd), F32)] * 2
        + [jax.ShapeDtypeStruct((nc, GDN_HEADS, CHUNK, CHUNK), F32)] * 2,
        scratch_shapes=[pltpu.VMEM((NB, hd, hd), F32)],
        compiler_params=_params("arbitrary"),
    )(y, y, y, gb, y, y, y, gb)


def _delta_bwd2(y, gb, do, sf_all, sr_all, tf_all, tr_all):
    s = y.shape[0]
    nc = s // CHUNK
    hd = GDN_HEAD_DIM

    def body(qf, kf, vf, gf, dof, sf, tf, qr, kr, vr, gr, dor, sr, tr, dyf_ref, dyr_ref, dgf_ref, dgr_ref, dstate):
        @pl.when(pl.program_id(0) == 0)
        def _():
            dstate[...] = jnp.zeros_like(dstate)

        q, k, v, dov = _heads(qf, qr), _heads(kf, kr), _heads(vf, vr), _heads(dof, dor)
        cm = _chunk_common2(q, k, v, gf[...], gr[...])
        tinv = jnp.concatenate([tf[0], tr[0]], axis=0)
        st = jnp.concatenate([sf[0], sr[0]], axis=0)
        ds_out = dstate[...]
        decay, lm, intra, qg, kdec, kbeg, eg, kb, beta = (
            cm[n] for n in ("decay", "lm", "intra", "qg", "kdec", "kbeg", "eg", "kb", "beta"))
        u = _bdot(tinv, cm["vb"])
        w = _bdot(tinv, kbeg)
        v_new = u - _bdot(w, st)
        egl = jnp.exp(cm["gl"])
        d_qg = _bdot(dov, st, BNT)
        d_intra = _bdot(dov, v_new, BNT)
        dv_new = _bdot(intra, dov, BTN) + _bdot(kdec, ds_out)
        d_kdec = _bdot(v_new, ds_out, BNT)
        dstate[...] = _bdot(qg, dov, BTN) + egl * ds_out - _bdot(w, dv_new, BTN)
        dgl = egl * jnp.sum(jnp.sum(st * ds_out, axis=2, keepdims=True), axis=1, keepdims=True)
        dw = -_bdot(dv_new, st, BNT)
        dvb = _bdot(tinv, dv_new, BTN)
        dkbeg = _bdot(tinv, dw, BTN)
        dlm = jnp.where(cm["strict"], -(_bdot(dvb, u, BNT) + _bdot(dkbeg, w, BNT)), 0.0)
        d_a = dlm * decay
        d_qk = d_intra * decay
        e = dlm * lm + d_intra * intra
        colsum = _dot3(e, jnp.ones((NB, CHUNK, LANE), F32), BTN)[:, :, 0:1]
        dgc = jnp.sum(e, axis=2, keepdims=True) - colsum
        dkb = _bdot(d_a, k) + dkbeg * eg
        dk = _bdot(d_a, kb, BTN) + _bdot(d_qk, q, BTN)
        dq = _bdot(d_qk, k) + d_qg * eg
        dgc = dgc + jnp.sum(d_qg * qg, axis=2, keepdims=True) + jnp.sum(dkbeg * kbeg, axis=2, keepdims=True)
        tdec = jnp.sum(d_kdec * kdec, axis=2, keepdims=True)
        dk = dk + d_kdec * cm["edec"] + dkb * beta
        dgc = dgc - tdec
        dgl = dgl + jnp.sum(tdec, axis=1, keepdims=True)
        dbeta = jnp.sum(dvb * v, axis=2, keepdims=True) + jnp.sum(dkb * k, axis=2, keepdims=True)
        dv = dvb * beta
        lane = lax.broadcasted_iota(jnp.int32, (CHUNK, LANE), 1)
        for rev, dy_ref, dg_ref in ((False, dyf_ref, dgf_ref), (True, dyr_ref, dgr_ref)):
            dgc_tile = jnp.zeros((CHUNK, LANE), F32)
            rest = jnp.zeros((CHUNK, LANE), F32)
            for h in range(GDN_HEADS):
                b = (GDN_HEADS if rev else 0) + h
                gi, bi = _gate_lanes(rev, h)
                dgc_tile = dgc_tile + jnp.where(lane == gi, dgc[b], 0.0)
                rest = rest + jnp.where(lane == gi, dgl[b], 0.0) + jnp.where(lane == bi, dbeta[b], 0.0)
                dy_ref[:, h * hd:(h + 1) * hd] = dq[b]
                dy_ref[:, GDN_WIDTH + h * hd:GDN_WIDTH + (h + 1) * hd] = dk[b]
                dy_ref[:, 2 * GDN_WIDTH + h * hd:2 * GDN_WIDTH + (h + 1) * hd] = dv[b]
            le_t = _chunk_masks(not rev)[0].astype(F32)
            dg_ref[...] = _dot3(le_t, dgc_tile, NN) + rest

    def col(j, rev):
        return pl.BlockSpec((CHUNK, GDN_WIDTH), (lambda n: (n, j)) if rev else (lambda n: (nc - 1 - n, j)))

    def wide(width, rev):
        return pl.BlockSpec((CHUNK, width), (lambda n: (n, 0)) if rev else (lambda n: (nc - 1 - n, 0)))

    def per_chunk(d1, d2, rev):
        return pl.BlockSpec((1, GDN_HEADS, d1, d2), (lambda n: (n, 0, 0, 0)) if rev else (lambda n: (nc - 1 - n, 0, 0, 0)))

    def side(rev):
        return [col(0, rev), col(1, rev), col(2, rev), wide(LANE, rev), wide(GDN_WIDTH, rev), per_chunk(hd, hd, rev),
                per_chunk(CHUNK, CHUNK, rev)]

    return pl.pallas_call(
        body, name="delta_bwd", grid=(nc,),
        in_specs=side(False) + side(True),
        out_specs=[wide(QKV_A, False), wide(QKV_A, True), wide(LANE, False), wide(LANE, True)],
        out_shape=[jax.ShapeDtypeStruct((s, QKV_A), F32)] * 2 + [jax.ShapeDtypeStruct((s, LANE), F32)] * 2,
        scratch_shapes=[pltpu.VMEM((NB, hd, hd), F32)],
        compiler_params=_params("arbitrary"),
    )(y, y, y, gb, do, sf_all, tf_all, y, y, y, gb, do, sr_all, tr_all)


def _gdn_post_fwd(o_f, o_r, p_pad, norm_row):
    s = o_f.shape[0]
    tm = min(512, s)
    hd = GDN_HEAD_DIM

    def body(of_ref, or_ref, z_ref, w_ref, out_ref, osum_ref):
        o = of_ref[...] + or_ref[...]
        osum_ref[...] = o
        z = z_ref[...]
        gate = z * _sigmoid(z)
        for h in range(GDN_HEADS):
            sl = slice(h * hd, (h + 1) * hd)
            oh = o[:, sl]
            r = lax.rsqrt(jnp.mean(oh * oh, axis=-1, keepdims=True) + EPS)
            out_ref[:, sl] = (oh * r * w_ref[...] * gate[:, sl]).astype(BF16)

    blk = pl.BlockSpec((tm, GDN_WIDTH), lambda i: (i, 0))
    return pl.pallas_call(
        body, name="gdn_post_fwd", grid=(s // tm,),
        in_specs=[blk, blk, pl.BlockSpec((tm, GDN_WIDTH), lambda i: (i, OFF_Z // GDN_WIDTH)),
                  pl.BlockSpec((1, hd), lambda i: (0, 0))],
        out_specs=[blk, blk],
        out_shape=[jax.ShapeDtypeStruct((s, GDN_WIDTH), BF16), jax.ShapeDtypeStruct((s, GDN_WIDTH), F32)],
        compiler_params=_params("parallel"),
    )(o_f, o_r, p_pad, norm_row)


def _gdn_post_bwd(d_out, o_sum, p_pad, norm_row):
    s = o_sum.shape[0]
    tm = min(512, s)
    hd = GDN_HEAD_DIM

    def body(d_ref, o_ref, z_ref, w_ref, do_ref, dz_ref, dw_ref):
        @pl.when(pl.program_id(0) == 0)
        def _():
            dw_ref[...] = jnp.zeros_like(dw_ref)

        z = z_ref[...]
        sg = _sigmoid(z)
        gate = z * sg
        dgate = sg * (1.0 + z * (1.0 - sg))
        wv = w_ref[...]
        dw = jnp.zeros((1, hd), F32)
        for h in range(GDN_HEADS):
            sl = slice(h * hd, (h + 1) * hd)
            oh = o_ref[:, sl]
            dh = d_ref[:, sl]
            r = lax.rsqrt(jnp.mean(oh * oh, axis=-1, keepdims=True) + EPS)
            ohat = oh * r
            dz_ref[:, sl] = dh * ohat * wv * dgate[:, sl]
            drn = dh * gate[:, sl]
            t = drn * wv
            do_ref[:, sl] = r * (t - ohat * jnp.mean(t * ohat, axis=-1, keepdims=True))
            dw = dw + jnp.sum(drn * ohat, axis=0, keepdims=True)
        dw_ref[...] += dw

    blk = pl.BlockSpec((tm, GDN_WIDTH), lambda i: (i, 0))
    vec = pl.BlockSpec((1, hd), lambda i: (0, 0))
    return pl.pallas_call(
        body, name="gdn_post_bwd", grid=(s // tm,),
        in_specs=[blk, blk, pl.BlockSpec((tm, GDN_WIDTH), lambda i: (i, OFF_Z // GDN_WIDTH)), vec],
        out_specs=[blk, blk, vec],
        out_shape=[jax.ShapeDtypeStruct((s, GDN_WIDTH), F32), jax.ShapeDtypeStruct((s, GDN_WIDTH), F32),
                   jax.ShapeDtypeStruct((1, hd), F32)],
        compiler_params=_params("arbitrary"),
    )(d_out, o_sum, p_pad, norm_row)


def _add2(a, b, name):
    s, w = a.shape
    tm = next(t for t in (1024, 640, 512, 256, 128, 64, 8) if s % t == 0)

    def body(a_ref, b_ref, o_ref):
        o_ref[...] = a_ref[...] + b_ref[...]

    blk = pl.BlockSpec((tm, w), lambda i: (i, 0))
    return pl.pallas_call(body, name=name, grid=(s // tm,), in_specs=[blk, blk], out_specs=blk,
                          out_shape=jax.ShapeDtypeStruct((s, w), F32), compiler_params=_params("parallel"))(a, b)


def _gdn_forward(p_pad, conv_wt, alog_row, dt_row, norm_row):
    c_pre, y = _gdn_conv_fwd(p_pad, conv_wt)
    gb = _gdn_gates_fwd(p_pad, alog_row, dt_row)
    o_f, o_r, s_f, s_r, t_f, t_r = _delta_fwd2(y, gb)
    out, o_sum = _gdn_post_fwd(o_f, o_r, p_pad, norm_row)
    return out, (c_pre, y, gb, s_f, t_f, s_r, t_r, o_sum)


def _gdn_backward(d_out, p_pad, conv_wt, alog_row, dt_row, norm_row, saved):
    c_pre, y, gb, s_f, t_f, s_r, t_r, o_sum = saved
    do, dz, dnorm = _gdn_post_bwd(d_out, o_sum, p_pad, norm_row)
    dy_f, dy_r, dgb_f, dgb_r = _delta_bwd2(y, gb, do, s_f, s_r, t_f, t_r)
    dp_qkv, dconv = _gdn_conv_bwd(dy_f, dy_r, c_pre, p_pad, conv_wt)
    dp_ab, gate_sums = _gdn_gates_bwd(dgb_f, dgb_r, p_pad, gb, alog_row, dt_row)
    return dp_qkv, dz, dp_ab, dconv, gate_sums, dnorm


ATT_BK = ATT_BQ + 2 * ATT_HALO
SWA_SCALE = SWA_HEAD_DIM ** -0.5


def _t5_bucket(rel):
    nb = REL_BUCKETS // 2
    bucket = (rel > 0).astype(np.int32) * nb
    n = np.abs(rel)
    max_exact = nb // 2
    large = max_exact + (np.log(np.maximum(n, 1) / max_exact)
                         / math.log(REL_MAX_DISTANCE / max_exact) * (nb - max_exact)).astype(np.int32)
    large = np.minimum(large, nb - 1)
    return (bucket + np.where(n < max_exact, n, large)).astype(np.int32)


def _band_tables(dilation, queries_are_rows_of_block):
    blk = np.arange(ATT_BQ)
    band = np.arange(ATT_BK) - ATT_HALO
    if queries_are_rows_of_block:
        rel = band[None, :] - blk[:, None]
        band_idx = np.broadcast_to(np.arange(ATT_BK)[None, :], rel.shape)
    else:
        rel = blk[None, :] - band[:, None]
        band_idx = np.broadcast_to(np.arange(ATT_BK)[:, None], rel.shape)
    base = np.abs(rel) <= ATT_HALO
    not_prev = band_idx >= ATT_HALO
    not_next = band_idx < ATT_HALO + ATT_BQ
    valid = np.stack([base & not_prev, base, base & not_next, base & not_prev & not_next])
    return valid, _t5_bucket(rel * dilation)


def _bias_tiles(rel_bias, dilation, queries_are_rows_of_block):
    valid, bucket = _band_tables(dilation, queries_are_rows_of_block)
    onehot = (jnp.asarray(bucket.reshape(-1, 1)) == jnp.arange(REL_BUCKETS, dtype=jnp.int32)[None, :]).astype(F32)
    rb = jnp.dot(onehot, rel_bias.astype(F32), precision=lax.Precision.HIGHEST)
    rb = rb.T.reshape((SWA_HEADS,) + bucket.shape)
    return jnp.where(valid[:, None], rb[None], NEG_BIG).astype(F32)


def _group_sum(x, bd):
    hi = x.astype(BF16)
    lo = (x - hi.astype(F32)).astype(BF16)
    return jnp.dot(hi, bd, preferred_element_type=F32) + jnp.dot(lo, bd, preferred_element_type=F32)


def _head_block_diag():
    idx = np.arange(SWA_WIDTH) // SWA_HEAD_DIM
    return jnp.asarray(idx[:, None] == idx[None, :], BF16)


def _swa_pre_fwd(p_pad, qw_row, kw_row, bd):
    s = p_pad.shape[0]
    tm = min(512, s)
    inv = 1.0 / SWA_HEAD_DIM

    def body(q_ref, k_ref, v_ref, qw_ref, kw_ref, bd_ref, qo_ref, ko_ref, vo_ref):
        bdv = bd_ref[...]
        q = q_ref[...]
        k = k_ref[...]
        rq = lax.rsqrt(_group_sum(q * q, bdv) * inv + EPS)
        rk = lax.rsqrt(_group_sum(k * k, bdv) * inv + EPS)
        qo_ref[...] = (q * rq * qw_ref[...] * SWA_SCALE).astype(BF16)
        ko_ref[...] = (k * rk * kw_ref[...]).astype(BF16)
        vo_ref[...] = v_ref[...].astype(BF16)

    base = OFF_B // SWA_WIDTH
    blk = pl.BlockSpec((tm, SWA_WIDTH), lambda i: (i, 0))
    vec = pl.BlockSpec((1, SWA_WIDTH), lambda i: (0, 0))
    return pl.pallas_call(
        body, name="swa_pre_fwd", grid=(s // tm,),
        in_specs=[pl.BlockSpec((tm, SWA_WIDTH), lambda i: (i, base)), pl.BlockSpec((tm, SWA_WIDTH), lambda i: (i, base + 1)),
                  pl.BlockSpec((tm, SWA_WIDTH), lambda i: (i, base + 2)), vec, vec,
                  pl.BlockSpec((SWA_WIDTH, SWA_WIDTH), lambda i: (0, 0))],
        out_specs=[blk, blk, blk],
        out_shape=[jax.ShapeDtypeStruct((s, SWA_WIDTH), BF16)] * 3,
        compiler_params=_params("parallel"),
    )(p_pad, p_pad, p_pad, qw_row, kw_row, bd)


def _swa_pre_bwd(dqs, dks, dvs, p_pad, qw_row, kw_row, bd):
    s = p_pad.shape[0]
    tm = min(256, s)
    inv = 1.0 / SWA_HEAD_DIM
    npat = len(dqs)

    def body(*refs):
        dq_refs, dk_refs, dv_refs = refs[:npat], refs[npat:2 * npat], refs[2 * npat:3 * npat]
        q_ref, k_ref, qw_ref, kw_ref, bd_ref, dp_ref, dqw_ref, dkw_ref = refs[3 * npat:]

        @pl.when(pl.program_id(0) == 0)
        def _():
            dqw_ref[...] = jnp.zeros_like(dqw_ref)
            dkw_ref[...] = jnp.zeros_like(dkw_ref)

        bdv = bd_ref[...]

        def norm_bwd(x, g, w, scale):
            r = lax.rsqrt(_group_sum(x * x, bdv) * inv + EPS)
            xhat = x * r
            t = g * w * scale
            dx = r * (t - xhat * (_group_sum(t * xhat, bdv) * inv))
            return dx, jnp.sum(g * scale * xhat, axis=0, keepdims=True)

        def total(rs):
            t = rs[0][...]
            for r in rs[1:]:
                t = t + r[...]
            return t

        dq, dqw = norm_bwd(q_ref[...], total(dq_refs), qw_ref[...], SWA_SCALE)
        dk, dkw = norm_bwd(k_ref[...], total(dk_refs), kw_ref[...], 1.0)
        dp_ref[:, 0:SWA_WIDTH] = dq
        dp_ref[:, SWA_WIDTH:2 * SWA_WIDTH] = dk
        dp_ref[:, 2 * SWA_WIDTH:3 * SWA_WIDTH] = total(dv_refs)
        dqw_ref[...] += dqw
        dkw_ref[...] += dkw

    base = OFF_B // SWA_WIDTH
    blk = pl.BlockSpec((tm, SWA_WIDTH), lambda i: (i, 0))
    vec = pl.BlockSpec((1, SWA_WIDTH), lambda i: (0, 0))
    return pl.pallas_call(
        body, name="swa_pre_bwd", grid=(s // tm,),
        in_specs=[blk] * (3 * npat) + [pl.BlockSpec((tm, SWA_WIDTH), lambda i: (i, base)),
                                      pl.BlockSpec((tm, SWA_WIDTH), lambda i: (i, base + 1)), vec, vec,
                                      pl.BlockSpec((SWA_WIDTH, SWA_WIDTH), lambda i: (0, 0))],
        out_specs=[pl.BlockSpec((tm, 3 * SWA_WIDTH), lambda i: (i, 0)), vec, vec],
        out_shape=[jax.ShapeDtypeStruct((s, 3 * SWA_WIDTH), F32), jax.ShapeDtypeStruct((1, SWA_WIDTH), F32),
                   jax.ShapeDtypeStruct((1, SWA_WIDTH), F32)],
        compiler_params=_params("arbitrary"),
    )(*dqs, *dks, *dvs, p_pad, p_pad, qw_row, kw_row, bd)


def _band_specs(length):
    per = ATT_BQ // ATT_HALO
    last = length // ATT_HALO - 1
    prev = pl.BlockSpec((ATT_HALO, SWA_WIDTH), lambda r, t: (jnp.maximum(t * per - 1, 0), r))
    cur = pl.BlockSpec((ATT_BQ, SWA_WIDTH), lambda r, t: (t, r))
    nxt = pl.BlockSpec((ATT_HALO, SWA_WIDTH), lambda r, t: (jnp.minimum((t + 1) * per, last), r))
    return [prev, cur, nxt]


def _tile_variant(t, nb):
    if nb == 1:
        return 3
    return jnp.where(t == 0, 0, jnp.where(t == nb - 1, 2, 1))


def _band(refs):
    return jnp.concatenate([r[...] for r in refs], axis=0)


def _att_fwd(q, k, v, bias, dilation):
    s = q.shape[0]
    length = s // dilation
    nb = length // ATT_BQ
    view = (length, dilation * SWA_WIDTH)
    hd = SWA_HEAD_DIM

    def body(q_ref, kp, kc, kn, vp, vc, vn, b_ref, o_ref, lse_ref):
        kb, vb = _band((kp, kc, kn)), _band((vp, vc, vn))
        qv = q_ref[...]
        for h in range(SWA_HEADS):
            sl = slice(h * hd, (h + 1) * hd)
            sc = _dot(qv[:, sl], kb[:, sl], NT) + b_ref[0, h]
            m = jnp.max(sc, axis=-1, keepdims=True)
            p = jnp.exp(sc - m)
            den = jnp.sum(p, axis=-1, keepdims=True)
            o_ref[:, sl] = _dot(p, vb[:, sl]) / den
            lse_ref[:, sl] = jnp.broadcast_to(m + jnp.log(den), (ATT_BQ, hd))

    cur = pl.BlockSpec((ATT_BQ, SWA_WIDTH), lambda r, t: (t, r))
    bspec = pl.BlockSpec((1, SWA_HEADS, ATT_BQ, ATT_BK), lambda r, t: (_tile_variant(t, nb), 0, 0, 0))
    o, lse = pl.pallas_call(
        body, name=f"att_fwd_d{dilation}", grid=(dilation, nb),
        in_specs=[cur] + _band_specs(length) * 2 + [bspec],
        out_specs=[cur, cur],
        out_shape=[jax.ShapeDtypeStruct(view, F32)] * 2,
        compiler_params=_params("parallel", "parallel"),
    )(q.reshape(view), *([k.reshape(view)] * 3), *([v.reshape(view)] * 3), bias)
    return o.reshape(s, SWA_WIDTH), lse.reshape(s, SWA_WIDTH)


def _att_dq(q, k, v, dop, lse, cp, bias, dilation):
    s = q.shape[0]
    length = s // dilation
    nb = length // ATT_BQ
    view = (length, dilation * SWA_WIDTH)
    hd = SWA_HEAD_DIM

    def body(q_ref, kp, kc, kn, vp, vc, vn, do_ref, lse_ref, cp_ref, b_ref, dq_ref, db_ref):
        @pl.when((pl.program_id(0) == 0) & (pl.program_id(1) == 0))
        def _():
            db_ref[...] = jnp.zeros_like(db_ref)

        var = _tile_variant(pl.program_id(1), nb)
        kb, vb = _band((kp, kc, kn)), _band((vp, vc, vn))
        qv, dov, lsev, cpv = q_ref[...], do_ref[...], lse_ref[...], cp_ref[...]
        for h in range(SWA_HEADS):
            sl = slice(h * hd, (h + 1) * hd)
            sc = _dot(qv[:, sl], kb[:, sl], NT) + b_ref[0, h]
            p = jnp.exp(sc - lsev[:, h * hd:h * hd + 1])
            dp = _dot(dov[:, sl], vb[:, sl], NT)
            ds = p * (dp + cpv[:, h * hd:h * hd + 1])
            dq_ref[:, sl] = _dot(ds, kb[:, sl])
            db_ref[var, h] += ds

    cur = pl.BlockSpec((ATT_BQ, SWA_WIDTH), lambda r, t: (t, r))
    bspec = pl.BlockSpec((1, SWA_HEADS, ATT_BQ, ATT_BK), lambda r, t: (_tile_variant(t, nb), 0, 0, 0))
    dq, db = pl.pallas_call(
        body, name=f"att_dq_d{dilation}", grid=(dilation, nb),
        in_specs=[cur] + _band_specs(length) * 2 + [cur, cur, cur, bspec],
        out_specs=[cur, pl.BlockSpec((4, SWA_HEADS, ATT_BQ, ATT_BK), lambda r, t: (0, 0, 0, 0))],
        out_shape=[jax.ShapeDtypeStruct(view, F32), jax.ShapeDtypeStruct((4, SWA_HEADS, ATT_BQ, ATT_BK), F32)],
        compiler_params=_params("arbitrary", "arbitrary"),
    )(q.reshape(view), *([k.reshape(view)] * 3), *([v.reshape(view)] * 3), dop.reshape(view), lse.reshape(view),
      cp.reshape(view), bias)
    return dq.reshape(s, SWA_WIDTH), db


def _att_dkv(q, k, v, dop, lse, cp, bias_t, dilation):
    s = q.shape[0]
    length = s // dilation
    nb = length // ATT_BQ
    view = (length, dilation * SWA_WIDTH)
    hd = SWA_HEAD_DIM

    def body(k_ref, v_ref, qp, qc, qn, dp_, dc_, dn_, lp, lc, ln, cp_, cc_, cn_, b_ref, dk_ref, dv_ref):
        qb, dob = _band((qp, qc, qn)), _band((dp_, dc_, dn_))
        lseb, cpb = _band((lp, lc, ln)), _band((cp_, cc_, cn_))
        kv, vv = k_ref[...], v_ref[...]
        for h in range(SWA_HEADS):
            sl = slice(h * hd, (h + 1) * hd)
            sc = _dot(qb[:, sl], kv[:, sl], NT) + b_ref[0, h]
            p = jnp.exp(sc - lseb[:, h * hd:h * hd + 1])
            dv_ref[:, sl] = _dot(p, dob[:, sl], TN)
            dp = _dot(dob[:, sl], vv[:, sl], NT)
            ds = p * (dp + cpb[:, h * hd:h * hd + 1])
            dk_ref[:, sl] = _dot(ds, qb[:, sl], TN)

    cur = pl.BlockSpec((ATT_BQ, SWA_WIDTH), lambda r, t: (t, r))
    bspec = pl.BlockSpec((1, SWA_HEADS, ATT_BK, ATT_BQ), lambda r, t: (_tile_variant(t, nb), 0, 0, 0))
    dk, dv = pl.pallas_call(
        body, name=f"att_dkv_d{dilation}", grid=(dilation, nb),
        in_specs=[cur, cur] + _band_specs(length) * 4 + [bspec],
        out_specs=[cur, cur],
        out_shape=[jax.ShapeDtypeStruct(view, F32)] * 2,
        compiler_params=_params("parallel", "parallel"),
    )(k.reshape(view), v.reshape(view), *([q.reshape(view)] * 3), *([dop.reshape(view)] * 3),
      *([lse.reshape(view)] * 3), *([cp.reshape(view)] * 3), bias_t)
    return dk.reshape(s, SWA_WIDTH), dv.reshape(s, SWA_WIDTH)


def _pattern_weights(lses):
    m = lses[0]
    for l in lses[1:]:
        m = jnp.maximum(m, l)
    es = [jnp.exp(l - m) for l in lses]
    den = es[0]
    for e in es[1:]:
        den = den + e
    return [e / den for e in es]


def _combine_fwd(outs, lses):
    s = outs[0].shape[0]
    tm = min(512, s)
    npat = len(outs)

    def body(*refs):
        ws = _pattern_weights([r[...] for r in refs[npat:2 * npat]])
        o = ws[0] * refs[0][...]
        for p in range(1, npat):
            o = o + ws[p] * refs[p][...]
        refs[2 * npat][...] = o.astype(BF16)

    blk = pl.BlockSpec((tm, SWA_WIDTH), lambda i: (i, 0))
    return pl.pallas_call(
        body, name="swa_combine_fwd", grid=(s // tm,), in_specs=[blk] * (2 * npat), out_specs=blk,
        out_shape=jax.ShapeDtypeStruct((s, SWA_WIDTH), BF16), compiler_params=_params("parallel"),
    )(*outs, *lses)


def _combine_bwd(d_out, outs, lses, bd):
    s = d_out.shape[0]
    tm = min(512, s)
    npat = len(outs)

    def body(*refs):
        d_ref, bd_ref = refs[0], refs[1 + 2 * npat]
        o_refs, l_refs = refs[1:1 + npat], refs[1 + npat:1 + 2 * npat]
        out_refs = refs[2 + 2 * npat:]
        ws = _pattern_weights([r[...] for r in l_refs])
        dov = d_ref[...]
        o = ws[0] * o_refs[0][...]
        for p in range(1, npat):
            o = o + ws[p] * o_refs[p][...]
        rd = _group_sum(dov * o, bd_ref[...])
        for p in range(npat):
            out_refs[p][...] = (ws[p] * dov).astype(BF16)
            out_refs[npat + p][...] = -ws[p] * rd

    blk = pl.BlockSpec((tm, SWA_WIDTH), lambda i: (i, 0))
    res = pl.pallas_call(
        body, name="swa_combine_bwd", grid=(s // tm,),
        in_specs=[blk] * (1 + 2 * npat) + [pl.BlockSpec((SWA_WIDTH, SWA_WIDTH), lambda i: (0, 0))],
        out_specs=[blk] * (2 * npat),
        out_shape=[jax.ShapeDtypeStruct((s, SWA_WIDTH), BF16)] * npat + [jax.ShapeDtypeStruct((s, SWA_WIDTH), F32)] * npat,
        compiler_params=_params("parallel"),
    )(d_out, *outs, *lses, bd)
    return res[:npat], res[npat:]


def _rel_bias_grad(dbs, buckets):
    npat = len(dbs)

    def body(*refs):
        db_refs, bk_refs, o_ref = refs[:npat], refs[npat:2 * npat], refs[2 * npat]
        row = lax.broadcasted_iota(jnp.int32, (REL_BUCKETS, LANE), 0)
        lane = lax.broadcasted_iota(jnp.int32, (REL_BUCKETS, LANE), 1)
        tiles = [[db_refs[p][0, h] + db_refs[p][1, h] + db_refs[p][2, h] + db_refs[p][3, h] for h in range(SWA_HEADS)]
                 for p in range(npat)]
        bks = [r[...] for r in bk_refs]

        def one_bucket(b, acc):
            for h in range(SWA_HEADS):
                tot = jnp.zeros((1, 1), F32)
                for p in range(npat):
                    sel = jnp.where(bks[p] == b, tiles[p][h], 0.0)
                    tot = tot + jnp.sum(jnp.sum(sel, axis=1, keepdims=True), axis=0, keepdims=True)
                acc = acc + jnp.where((row == b) & (lane == h), tot, 0.0)
            return acc

        o_ref[...] = lax.fori_loop(0, REL_BUCKETS, one_bucket, jnp.zeros((REL_BUCKETS, LANE), F32))

    full4 = pl.BlockSpec((4, SWA_HEADS, ATT_BQ, ATT_BK), lambda: (0, 0, 0, 0))
    full2 = pl.BlockSpec((ATT_BQ, ATT_BK), lambda: (0, 0))
    return pl.pallas_call(
        body, name="rel_bias_grad", in_specs=[full4] * npat + [full2] * npat,
        out_specs=pl.BlockSpec((REL_BUCKETS, LANE), lambda: (0, 0)),
        out_shape=jax.ShapeDtypeStruct((REL_BUCKETS, LANE), F32),
        compiler_params=pltpu.CompilerParams(vmem_limit_bytes=V7X_VMEM_LIMIT_BYTES),
    )(*dbs, *buckets)


def _swa_forward(p_pad, qw_row, kw_row, rel_bias, bd):
    q, k, v = _swa_pre_fwd(p_pad, qw_row, kw_row, bd)
    outs, lses = [], []
    for _, dil in DILATION_PATTERNS:
        o, lse = _att_fwd(q, k, v, _bias_tiles(rel_bias, dil, True), dil)
        outs.append(o)
        lses.append(lse)
    return _combine_fwd(outs, lses), (q, k, v, outs, lses)


def _swa_backward(d_out, p_pad, qw_row, kw_row, rel_bias, bd, saved):
    q, k, v, outs, lses = saved
    dops, cps = _combine_bwd(d_out, outs, lses, bd)
    dqs, dks, dvs, dbs, buckets = [], [], [], [], []
    for p, (_, dil) in enumerate(DILATION_PATTERNS):
        dq, db = _att_dq(q, k, v, dops[p], lses[p], cps[p], _bias_tiles(rel_bias, dil, True), dil)
        dk, dv = _att_dkv(q, k, v, dops[p], lses[p], cps[p], _bias_tiles(rel_bias, dil, False), dil)
        dqs.append(dq)
        dks.append(dk)
        dvs.append(dv)
        dbs.append(db)
        buckets.append(jnp.asarray(_band_tables(dil, True)[1]))
    dp, dqw, dkw = _swa_pre_bwd(dqs, dks, dvs, p_pad, qw_row, kw_row, bd)
    return dp, dqw, dkw, _rel_bias_grad(dbs, buckets)


def _lane_row(v):
    flat = v.reshape(-1).astype(F32)
    return jnp.zeros((1, LANE), F32).at[0, :flat.shape[0]].set(flat)


W_IN_SHARD = N_IN // N_DEV
W_IN_RUNS = ((0, QKV_A, 0), (QKV_A, OFF_B, QKV_A), (OFF_B, OFF_B + 16, OFF_AB), (OFF_B + 16, N_IN, OFF_B))
W_IN_SEGMENTS = ((0, QKV_A), (OFF_Z, GDN_WIDTH), (OFF_B, 3 * SWA_WIDTH), (OFF_AB, LANE))


def _w_in_pieces(shard):
    lo, hi = shard * W_IN_SHARD, (shard + 1) * W_IN_SHARD
    out = []
    for first, last, dst in W_IN_RUNS:
        a, b = max(lo, first), min(hi, last)
        if a < b:
            out.append((a - lo, b - a, dst + a - first))
    return out


def _cols_from_slabs(w3, name):
    nd, r, wd = w3.shape
    half = nd // 2

    def body(w_ref, o_ref):
        for sh in range(half):
            o_ref[:, wd * sh:wd * (sh + 1)] = w_ref[sh]

    return pl.pallas_call(
        body, name=name, grid=(2,), in_specs=[pl.BlockSpec((half, r, wd), lambda j: (j, 0, 0))],
        out_specs=pl.BlockSpec((r, half * wd), lambda j: (0, j)),
        out_shape=jax.ShapeDtypeStruct((r, nd * wd), w3.dtype), compiler_params=_params("parallel"),
    )(w3)


def _w_in_from_slabs(w3):
    nd, r, _ = w3.shape

    def body(w_ref, o_ref):
        o_ref[:, OFF_AB:N_PAD] = jnp.zeros((r, N_PAD - OFF_AB), w3.dtype)
        for sh in range(nd):
            for src, length, dst in _w_in_pieces(sh):
                o_ref[:, dst:dst + length] = w_ref[sh, :, src:src + length]

    return pl.pallas_call(
        body, name="w_in_from_slabs", out_shape=jax.ShapeDtypeStruct((r, N_PAD), w3.dtype),
        compiler_params=pltpu.CompilerParams(vmem_limit_bytes=V7X_VMEM_LIMIT_BYTES),
    )(w3)


def _w_in_grad_slabs(parts):
    r = parts[0].shape[0]

    def body(*refs):
        o_ref = refs[len(parts)]
        for sh in range(N_DEV):
            for src, length, dst in _w_in_pieces(sh):
                seg = next(i for i, (off, width) in enumerate(W_IN_SEGMENTS) if off <= dst < off + width)
                at = dst - W_IN_SEGMENTS[seg][0]
                o_ref[sh, :, src:src + length] = refs[seg][:, at:at + length]

    return pl.pallas_call(
        body, name="w_in_grad_slabs", out_shape=jax.ShapeDtypeStruct((N_DEV, r, W_IN_SHARD), F32),
        compiler_params=pltpu.CompilerParams(vmem_limit_bytes=V7X_VMEM_LIMIT_BYTES),
    )(*parts)


def _local_step(x, tgt, wts, small):
    bd = _head_block_diag()
    conv_wt = jnp.zeros((8, QKV_A), F32).at[:CONV_WIDTH].set(small["conv_w"].T)
    alog_row, dt_row = _lane_row(small["a_log"]), _lane_row(small["dt_bias"])
    gnorm_row = small["gdn_norm_w"].reshape(1, GDN_HEAD_DIM)
    qw_row = jnp.tile(small["q_norm_w"].reshape(-1), SWA_HEADS).reshape(1, SWA_WIDTH)
    kw_row = jnp.tile(small["k_norm_w"].reshape(-1), SWA_HEADS).reshape(1, SWA_WIDTH)
    rel_bias = small["rel_bias"]
    win_pad = wts["w_in_pad"]
    wo_a, wo_b = wts["w_out"][:GDN_WIDTH], wts["w_out"][GDN_WIDTH:]

    x1, sv1 = _ffn_forward(x, small["ffn1_norm"], wts["ffn1_w_gate"], wts["ffn1_w_up"], wts["ffn1_w_down"], "ffn1")
    n2, r2 = _rms_fwd(x1, small["mix_norm"], "mix_norm")
    p_pad = _matmul([(n2, win_pad)], tm=256, tn=N_PAD, tk=D_MODEL, name="w_in")
    o_a, sva = _gdn_forward(p_pad, conv_wt, alog_row, dt_row, gnorm_row)
    o_b, svb = _swa_forward(p_pad, qw_row, kw_row, rel_bias, bd)
    x2 = _matmul([(o_a, wo_a), (o_b, wo_b)], tm=512, tn=D_MODEL, tk=GDN_WIDTH, name="w_out", res=x1)
    x3, sv2 = _ffn_forward(x2, small["ffn2_norm"], wts["ffn2_w_gate"], wts["ffn2_w_up"], wts["ffn2_w_down"], "ffn2")
    loss_row, dx3, d_final = _final_loss(x3, small["final_norm"], tgt)

    dx2, d_ffn2_norm, dwg2, dwu2, dwd2 = _ffn_backward(
        dx3, x2, small["ffn2_norm"], wts["ffn2_w_gate"], wts["ffn2_w_up"], wts["ffn2_w_down"], sv2, "ffn2")
    d_oa = _matmul([(dx2, wo_a)], tb=True, tm=512, tn=GDN_WIDTH, tk=D_MODEL, name="w_out_da")
    d_ob = _matmul([(dx2, wo_b)], tb=True, tm=512, tn=SWA_WIDTH, tk=D_MODEL, name="w_out_db")
    dwo_a = _matmul([(o_a, dx2)], ta=True, tm=GDN_WIDTH, tn=D_MODEL, tk=512, name="w_out_dwa")
    dwo_b = _matmul([(o_b, dx2)], ta=True, tm=SWA_WIDTH, tn=D_MODEL, tk=512, name="w_out_dwb")
    dp_qkv, dz, dp_ab, dconv, gate_sums, d_gnorm = _gdn_backward(d_oa, p_pad, conv_wt, alog_row, dt_row, gnorm_row, sva)
    dp_b, dqw, dkw, d_rel = _swa_backward(d_ob, p_pad, qw_row, kw_row, rel_bias, bd, svb)
    segs = [(dp_qkv, 0, QKV_A), (dz, OFF_Z, GDN_WIDTH), (dp_b, OFF_B, 3 * SWA_WIDTH), (dp_ab, OFF_AB, LANE)]
    dn2 = None
    dwin_parts = []
    for i, (dseg, off, width) in enumerate(segs):
        dwin_parts.append(_matmul([(n2, dseg)], ta=True, tm=D_MODEL, tn=width, tk=512, name=f"w_in_dw{i}"))
        dn2 = _matmul([(dseg, win_pad[:, off:off + width])], tb=True, tm=512, tn=D_MODEL, tk=width,
                      name=f"w_in_dn{i}", res=dn2)
    dx1, d_mix_norm = _rms_bwd(dn2, x1, r2, small["mix_norm"], dx2, "mix_dnorm")
    dx, d_ffn1_norm, dwg1, dwu1, dwd1 = _ffn_backward(
        dx1, x, small["ffn1_norm"], wts["ffn1_w_gate"], wts["ffn1_w_up"], wts["ffn1_w_down"], sv1, "ffn1")

    def row_slabs(full):
        return full.reshape(N_DEV, full.shape[0] // N_DEV, full.shape[1])

    dwd1, dwd2 = row_slabs(dwd1), row_slabs(dwd2)
    grads = {
        "ffn1_norm": d_ffn1_norm, "ffn1_w_gate": dwg1, "ffn1_w_up": dwu1, "ffn1_w_down": dwd1,
        "mix_norm": d_mix_norm, "w_in": _w_in_grad_slabs(dwin_parts), "conv_w": dconv[:CONV_WIDTH].T,
        "a_log": gate_sums[0, :8].reshape(2, GDN_HEADS), "dt_bias": gate_sums[1, :8].reshape(2, GDN_HEADS),
        "gdn_norm_w": d_gnorm, "q_norm_w": dqw.reshape(SWA_HEADS, SWA_HEAD_DIM).sum(0, keepdims=True),
        "k_norm_w": dkw.reshape(SWA_HEADS, SWA_HEAD_DIM).sum(0, keepdims=True), "rel_bias": d_rel[:, :SWA_HEADS],
        "w_out": row_slabs(jnp.concatenate([dwo_a, dwo_b], axis=0)), "ffn2_norm": d_ffn2_norm,
        "ffn2_w_gate": dwg2, "ffn2_w_up": dwu2, "ffn2_w_down": dwd2, "final_norm": d_final,
    }
    return loss_row, dx, grads


MESH_IDS = pl.DeviceIdType.MESH
ANY = pl.BlockSpec(memory_space=pl.ANY)


def _all_gather(v, name):
    m, n = v.shape

    def body(x_ref, out_ref, send_sems, recv_sems, local_sem):
        x, y, c = lax.axis_index("x"), lax.axis_index("y"), lax.axis_index("c")
        me, sibling = (x, y, c), (x, y, 1 - c)
        chips = [(1 - x, y), (x, 1 - y), (1 - x, 1 - y)]

        def rows(px, py, pc):
            return out_ref.at[pl.ds((4 * px + 2 * py + pc) * m, m), :]

        def copy(k, block, to, src=None):
            return pltpu.make_async_remote_copy(
                src_ref=rows(*block) if src is None else src, dst_ref=rows(*block),
                send_sem=send_sems.at[k], recv_sem=recv_sems.at[k], device_id=to, device_id_type=MESH_IDS)

        mine = pltpu.make_async_copy(x_ref, rows(*me), local_sem)
        mine.start()
        first = [copy(0, me, sibling, src=x_ref)]
        first += [copy(1 + j, me, (*chip, c), src=x_ref) for j, chip in enumerate(chips)]
        for cp in first:
            cp.start()
        passed = [copy(4 + j, (*chip, c), sibling) for j, chip in enumerate(chips)]
        for j, chip in enumerate(chips):
            copy(1 + j, (*chip, c), me).wait_recv()
            passed[j].start()
        copy(0, sibling, me).wait_recv()
        for j, chip in enumerate(chips):
            copy(4 + j, (*chip, 1 - c), me).wait_recv()
        for cp in first + passed:
            cp.wait_send()
        mine.wait()

    return pl.pallas_call(
        body, name=name, in_specs=[ANY], out_specs=ANY,
        out_shape=jax.ShapeDtypeStruct((N_DEV * m, n), v.dtype),
        scratch_shapes=[pltpu.SemaphoreType.DMA((7,)), pltpu.SemaphoreType.DMA((7,)), pltpu.SemaphoreType.DMA],
        compiler_params=pltpu.CompilerParams(vmem_limit_bytes=V7X_VMEM_LIMIT_BYTES),
    )(v)


def _sibling_swap(v, name):
    def body(v_ref, out_ref, send_sem, recv_sem):
        x, y, c = lax.axis_index("x"), lax.axis_index("y"), lax.axis_index("c")
        cp = pltpu.make_async_remote_copy(src_ref=v_ref, dst_ref=out_ref, send_sem=send_sem, recv_sem=recv_sem,
                                          device_id=(x, y, 1 - c), device_id_type=MESH_IDS)
        cp.start()
        cp.wait()

    return pl.pallas_call(
        body, name=name, in_specs=[ANY], out_specs=ANY, out_shape=jax.ShapeDtypeStruct(v.shape, v.dtype),
        scratch_shapes=[pltpu.SemaphoreType.DMA, pltpu.SemaphoreType.DMA],
        compiler_params=pltpu.CompilerParams(vmem_limit_bytes=V7X_VMEM_LIMIT_BYTES),
    )(v)


def _chip_exchange(t, name):
    def body(t_ref, out_ref, send_sems, recv_sems, local_sem):
        x, y, c = lax.axis_index("x"), lax.axis_index("y"), lax.axis_index("c")
        mine = 2 * x + y
        chips = [(1 - x, y), (x, 1 - y), (1 - x, 1 - y)]
        own = pltpu.make_async_copy(t_ref.at[mine], out_ref.at[mine], local_sem)
        own.start()
        copies = [pltpu.make_async_remote_copy(
            src_ref=t_ref.at[2 * px + py], dst_ref=out_ref.at[mine], send_sem=send_sems.at[j], recv_sem=recv_sems.at[j],
            device_id=(px, py, c), device_id_type=MESH_IDS) for j, (px, py) in enumerate(chips)]
        for cp in copies:
            cp.start()
        for j, (px, py) in enumerate(chips):
            pltpu.make_async_remote_copy(
                src_ref=t_ref.at[mine], dst_ref=out_ref.at[2 * px + py], send_sem=send_sems.at[j],
                recv_sem=recv_sems.at[j], device_id=(px, py, c), device_id_type=MESH_IDS).wait_recv()
        for cp in copies:
            cp.wait_send()
        own.wait()

    return pl.pallas_call(
        body, name=name, in_specs=[ANY], out_specs=ANY, out_shape=jax.ShapeDtypeStruct(t.shape, t.dtype),
        scratch_shapes=[pltpu.SemaphoreType.DMA((3,)), pltpu.SemaphoreType.DMA((3,)), pltpu.SemaphoreType.DMA],
        compiler_params=pltpu.CompilerParams(vmem_limit_bytes=V7X_VMEM_LIMIT_BYTES),
    )(t)


def _adamw(parts, w, m, v, name):
    nparts, r, n = parts.shape
    tr = r
    for cand in (256, 176, 128, 104, 64, 8):
        if r % cand == 0:
            tr = cand
            break
    bc1 = 1.0 - ADAM_B1 ** ADAM_STEP
    bc2 = 1.0 - ADAM_B2 ** ADAM_STEP

    def body(p_ref, w_ref, m_ref, v_ref, g_ref, d_ref, nm_ref, nv_ref):
        g = p_ref[0].astype(F32)
        for k in range(1, nparts):
            g = g + p_ref[k].astype(F32)
        mn = ADAM_B1 * m_ref[...] + (1.0 - ADAM_B1) * g
        vn = ADAM_B2 * v_ref[...] + (1.0 - ADAM_B2) * (g * g)
        m_hat = mn / bc1
        v_hat = vn / bc2
        g_ref[...] = g
        nm_ref[...] = mn
        nv_ref[...] = vn
        d_ref[...] = -ADAM_LR * (m_hat / (jnp.sqrt(v_hat) + ADAM_EPS) + ADAM_WD * w_ref[...])

    blk = pl.BlockSpec((tr, n), lambda i: (i, 0))
    return pl.pallas_call(
        body, name=name, grid=(r // tr,),
        in_specs=[pl.BlockSpec((nparts, tr, n), lambda i: (0, i, 0)), blk, blk, blk],
        out_specs=[blk] * 4, out_shape=[jax.ShapeDtypeStruct((r, n), F32)] * 4,
        compiler_params=_params("parallel"),
    )(parts, w, m, v)


def _mesh_place():
    x, y, c = lax.axis_index("x"), lax.axis_index("y"), lax.axis_index("c")
    return x, y, c, [(1 - x, y), (x, 1 - y), (1 - x, 1 - y)]


def _all_gather_many(vs, name):
    na = len(vs)

    def body(*refs):
        x_refs, out_refs = refs[:na], refs[na:2 * na]
        send_sems, recv_sems, local_sems = refs[2 * na:]
        x, y, c, chips = _mesh_place()
        me, sibling = (x, y, c), (x, y, 1 - c)

        def slab(i, px, py, pc):
            return out_refs[i].at[4 * px + 2 * py + pc]

        def copy(i, k, block, to, src=None):
            return pltpu.make_async_remote_copy(
                src_ref=slab(i, *block) if src is None else src, dst_ref=slab(i, *block),
                send_sem=send_sems.at[i, k], recv_sem=recv_sems.at[i, k], device_id=to, device_id_type=MESH_IDS)

        mine = [pltpu.make_async_copy(x_refs[i], slab(i, *me), local_sems.at[i]) for i in range(na)]
        first = []
        for i in range(na):
            mine[i].start()
            first.append(copy(i, 0, me, sibling, src=x_refs[i]))
            first += [copy(i, 1 + j, me, (*chip, c), src=x_refs[i]) for j, chip in enumerate(chips)]
        for cp in first:
            cp.start()
        passed = []
        for j, chip in enumerate(chips):
            for i in range(na):
                copy(i, 1 + j, (*chip, c), me).wait_recv()
                passed.append(copy(i, 4 + j, (*chip, c), sibling))
                passed[-1].start()
        for i in range(na):
            copy(i, 0, sibling, me).wait_recv()
        for j, chip in enumerate(chips):
            for i in range(na):
                copy(i, 4 + j, (*chip, 1 - c), me).wait_recv()
        for cp in first + passed:
            cp.wait_send()
        for cp in mine:
            cp.wait()

    return pl.pallas_call(
        body, name=name, in_specs=[ANY] * na, out_specs=[ANY] * na,
        out_shape=[jax.ShapeDtypeStruct((N_DEV,) + v.shape, v.dtype) for v in vs],
        scratch_shapes=[pltpu.SemaphoreType.DMA((na, 7)), pltpu.SemaphoreType.DMA((na, 7)), pltpu.SemaphoreType.DMA((na,))],
        compiler_params=pltpu.CompilerParams(vmem_limit_bytes=V7X_VMEM_LIMIT_BYTES),
    )(*vs)


def _sibling_swap_many(gs, name):
    na = len(gs)

    def body(*refs):
        g_refs, out_refs = refs[:na], refs[na:2 * na]
        send_sems, recv_sems = refs[2 * na:]
        x, y, c, _ = _mesh_place()
        copies = [pltpu.make_async_remote_copy(
            src_ref=g_refs[i].at[2 * k + 1 - c], dst_ref=out_refs[i].at[k], send_sem=send_sems.at[i, k],
            recv_sem=recv_sems.at[i, k], device_id=(x, y, 1 - c), device_id_type=MESH_IDS)
            for i in range(na) for k in range(4)]
        for cp in copies:
            cp.start()
        for cp in copies:
            cp.wait()

    return pl.pallas_call(
        body, name=name, in_specs=[ANY] * na, out_specs=[ANY] * na,
        out_shape=[jax.ShapeDtypeStruct((4,) + g.shape[1:], g.dtype) for g in gs],
        scratch_shapes=[pltpu.SemaphoreType.DMA((na, 4)), pltpu.SemaphoreType.DMA((na, 4))],
        compiler_params=pltpu.CompilerParams(vmem_limit_bytes=V7X_VMEM_LIMIT_BYTES),
    )(*gs)


def _chip_sum(g, got, core, name):
    _, r, n = g.shape

    def body(c_ref, g_ref, got_ref, o_ref):
        o_ref[...] = (g_ref[...] + got_ref[...]).astype(BF16)

    return pl.pallas_call(
        body, name=name,
        grid_spec=pltpu.PrefetchScalarGridSpec(
            num_scalar_prefetch=1, grid=(4,),
            in_specs=[pl.BlockSpec((None, r, n), lambda k, c_ref: (2 * k + c_ref[0], 0, 0)),
                      pl.BlockSpec((None, r, n), lambda k, c_ref: (k, 0, 0))],
            out_specs=pl.BlockSpec((None, r, n), lambda k, c_ref: (k, 0, 0))),
        out_shape=jax.ShapeDtypeStruct((4, r, n), BF16), compiler_params=_params("parallel"),
    )(core, g, got)


def _chip_exchange_many(ts, name):
    na = len(ts)

    def body(*refs):
        t_refs, out_refs = refs[:na], refs[na:2 * na]
        send_sems, recv_sems, local_sems = refs[2 * na:]
        x, y, c, chips = _mesh_place()
        mine = 2 * x + y
        own = [pltpu.make_async_copy(t_refs[i].at[mine], out_refs[i].at[mine], local_sems.at[i]) for i in range(na)]
        for cp in own:
            cp.start()
        copies = [pltpu.make_async_remote_copy(
            src_ref=t_refs[i].at[2 * px + py], dst_ref=out_refs[i].at[mine], send_sem=send_sems.at[i, j],
            recv_sem=recv_sems.at[i, j], device_id=(px, py, c), device_id_type=MESH_IDS)
            for j, (px, py) in enumerate(chips) for i in range(na)]
        for cp in copies:
            cp.start()
        for j, (px, py) in enumerate(chips):
            for i in range(na):
                pltpu.make_async_remote_copy(
                    src_ref=t_refs[i].at[mine], dst_ref=out_refs[i].at[2 * px + py], send_sem=send_sems.at[i, j],
                    recv_sem=recv_sems.at[i, j], device_id=(px, py, c), device_id_type=MESH_IDS).wait_recv()
        for cp in copies:
            cp.wait_send()
        for cp in own:
            cp.wait()

    return pl.pallas_call(
        body, name=name, in_specs=[ANY] * na, out_specs=[ANY] * na,
        out_shape=[jax.ShapeDtypeStruct(t.shape, t.dtype) for t in ts],
        scratch_shapes=[pltpu.SemaphoreType.DMA((na, 3)), pltpu.SemaphoreType.DMA((na, 3)), pltpu.SemaphoreType.DMA((na,))],
        compiler_params=pltpu.CompilerParams(vmem_limit_bytes=V7X_VMEM_LIMIT_BYTES),
    )(*ts)


BIG = ("ffn1_w_gate", "ffn1_w_up", "ffn1_w_down", "w_in", "w_out", "ffn2_w_gate", "ffn2_w_up", "ffn2_w_down")
COL_SHARDED = ("ffn1_w_gate", "ffn1_w_up", "w_in", "ffn2_w_gate", "ffn2_w_up")
SMALL = ("ffn1_norm", "mix_norm", "a_log", "dt_bias", "gdn_norm_w", "q_norm_w", "k_norm_w", "rel_bias",
         "ffn2_norm", "final_norm")
WEIGHTS = ("ffn1_norm", "ffn1_w_gate", "ffn1_w_up", "ffn1_w_down", "mix_norm", "w_in", "conv_w", "a_log", "dt_bias",
           "gdn_norm_w", "q_norm_w", "k_norm_w", "rel_bias", "w_out", "ffn2_norm", "ffn2_w_gate", "ffn2_w_up",
           "ffn2_w_down", "final_norm")
PACK_WIDTH = 1024
PACK_ROW_MULTIPLE = 32


def _pack(arrays, width, row_multiple):
    flat = jnp.concatenate([a.reshape(-1) for a in arrays])
    rows = -(-flat.shape[0] // width)
    rows = -(-rows // row_multiple) * row_multiple
    return jnp.pad(flat, (0, rows * width - flat.shape[0])).reshape(rows, width)


def _unpack(packed, shapes):
    flat = packed.reshape(-1)
    out, pos = [], 0
    for shp in shapes:
        size = int(np.prod(shp))
        out.append(flat[pos:pos + size].reshape(shp))
        pos += size
    return out


def _blocks_of(name, full):
    if name in COL_SHARDED:
        rows, cols = full.shape
        return full.reshape(rows, N_DEV, cols // N_DEV).transpose(1, 0, 2).reshape(N_DEV, -1)
    return full.reshape(N_DEV, -1)


def _full_of(name, blocks, shard_shape):
    rows, cols = shard_shape
    if name in COL_SHARDED:
        return blocks.reshape(N_DEV, rows, cols).transpose(1, 0, 2).reshape(rows, N_DEV * cols)
    return blocks.reshape(N_DEV * rows, cols)


def kernel(x, ffn1_norm, ffn1_w_gate, ffn1_w_up, ffn1_w_down, mix_norm, w_in, conv_w, a_log, dt_bias, gdn_norm_w, q_norm_w, k_norm_w, rel_bias, w_out, ffn2_norm, ffn2_w_gate, ffn2_w_up, ffn2_w_down, final_norm, loss_target, m_ffn1_norm, m_ffn1_w_gate, m_ffn1_w_up, m_ffn1_w_down, m_mix_norm, m_w_in, m_conv_w, m_a_log, m_dt_bias, m_gdn_norm_w, m_q_norm_w, m_k_norm_w, m_rel_bias, m_w_out, m_ffn2_norm, m_ffn2_w_gate, m_ffn2_w_up, m_ffn2_w_down, m_final_norm, v_ffn1_norm, v_ffn1_w_gate, v_ffn1_w_up, v_ffn1_w_down, v_mix_norm, v_w_in, v_conv_w, v_a_log, v_dt_bias, v_gdn_norm_w, v_q_norm_w, v_k_norm_w, v_rel_bias, v_w_out, v_ffn2_norm, v_ffn2_w_gate, v_ffn2_w_up, v_ffn2_w_down, v_final_norm):
    w = dict(ffn1_norm=ffn1_norm, ffn1_w_gate=ffn1_w_gate, ffn1_w_up=ffn1_w_up, ffn1_w_down=ffn1_w_down, mix_norm=mix_norm, w_in=w_in, conv_w=conv_w, a_log=a_log, dt_bias=dt_bias, gdn_norm_w=gdn_norm_w, q_norm_w=q_norm_w, k_norm_w=k_norm_w, rel_bias=rel_bias, w_out=w_out, ffn2_norm=ffn2_norm, ffn2_w_gate=ffn2_w_gate, ffn2_w_up=ffn2_w_up, ffn2_w_down=ffn2_w_down, final_norm=final_norm)
    mom = dict(ffn1_norm=m_ffn1_norm, ffn1_w_gate=m_ffn1_w_gate, ffn1_w_up=m_ffn1_w_up, ffn1_w_down=m_ffn1_w_down, mix_norm=m_mix_norm, w_in=m_w_in, conv_w=m_conv_w, a_log=m_a_log, dt_bias=m_dt_bias, gdn_norm_w=m_gdn_norm_w, q_norm_w=m_q_norm_w, k_norm_w=m_k_norm_w, rel_bias=m_rel_bias, w_out=m_w_out, ffn2_norm=m_ffn2_norm, ffn2_w_gate=m_ffn2_w_gate, ffn2_w_up=m_ffn2_w_up, ffn2_w_down=m_ffn2_w_down, final_norm=m_final_norm)
    var = dict(ffn1_norm=v_ffn1_norm, ffn1_w_gate=v_ffn1_w_gate, ffn1_w_up=v_ffn1_w_up, ffn1_w_down=v_ffn1_w_down, mix_norm=v_mix_norm, w_in=v_w_in, conv_w=v_conv_w, a_log=v_a_log, dt_bias=v_dt_bias, gdn_norm_w=v_gdn_norm_w, q_norm_w=v_q_norm_w, k_norm_w=v_k_norm_w, rel_bias=v_rel_bias, w_out=v_w_out, ffn2_norm=v_ffn2_norm, ffn2_w_gate=v_ffn2_w_gate, ffn2_w_up=v_ffn2_w_up, ffn2_w_down=v_ffn2_w_down, final_norm=v_final_norm)
    ix, iy, ic = lax.axis_index("x"), lax.axis_index("y"), lax.axis_index("c")
    me = 4 * ix + 2 * iy + ic

    shard = {n: w[n][0] for n in BIG}

    conv_shard_shape = w["conv_w"][0].shape
    conv_elems = conv_shard_shape[0] * conv_shard_shape[1]
    gathered = _all_gather_many([shard[n].astype(BF16) for n in BIG] + [_pack([w["conv_w"][0]], LANE, 8)],
                                "gather_weights")
    slabs = dict(zip(BIG, gathered))
    wts = {}
    for n in BIG:
        if n == "w_in":
            wts["w_in_pad"] = _w_in_from_slabs(slabs[n])
        elif n in COL_SHARDED:
            wts[n] = _cols_from_slabs(slabs[n], f"{n}_cols")
        else:
            wts[n] = slabs[n].reshape(N_DEV * slabs[n].shape[1], slabs[n].shape[2])

    small = {n: w[n][0] if n not in ("rel_bias",) else w[n] for n in SMALL}
    small = {n: (a.reshape(1, -1) if n.endswith("norm") else a) for n, a in small.items()}
    conv_all = gathered[-1].reshape(N_DEV, -1)
    small["conv_w"] = conv_all[:, :conv_elems].reshape(N_DEV * conv_shard_shape[0], conv_shard_shape[1])
    loss_row, grad_x, grads = _local_step(x[0], loss_target[0], wts, small)
    loss = lax.psum(loss_row[0, 0], ("x", "y", "c"))

    gots = _sibling_swap_many([grads[n] for n in BIG], "grads_to_sibling")
    core = ic.astype(jnp.int32).reshape(1)
    sums = [_chip_sum(grads[n], got, core, f"{n}_chip_sum") for n, got in zip(BIG, gots)]
    parts = _chip_exchange_many(sums, "grads_to_chips")
    big_out = [[], [], [], []]
    for n, part in zip(BIG, parts):
        for kind, val in enumerate(_adamw(part, shard[n], mom[n][0], var[n][0], f"{n}_adamw")):
            big_out[kind].append(val)

    small_names = SMALL + ("conv_w",)
    small_shapes = [grads[n].shape for n in small_names]
    g_small = _pack([grads[n] for n in small_names], LANE, 8)
    small_rows = g_small.shape[0]
    all_small = _all_gather(g_small, "gather_small_grads").reshape(N_DEV, small_rows, LANE)
    rep_shapes = [grads[n].shape for n in SMALL]
    zero_conv = jnp.zeros(small_shapes[-1], F32)
    ws = _pack([w[n].reshape(grads[n].shape) for n in SMALL] + [zero_conv], LANE, 8)
    ms = _pack([mom[n].reshape(grads[n].shape) for n in SMALL] + [zero_conv], LANE, 8)
    vs = _pack([var[n].reshape(grads[n].shape) for n in SMALL] + [zero_conv], LANE, 8)
    small_out = [_unpack(a, small_shapes) for a in _adamw(all_small, ws, ms, vs, "adamw_small")]
    conv_g = lax.dynamic_slice_in_dim(small_out[0][-1], me * conv_shard_shape[0], conv_shard_shape[0], axis=0)
    conv_out = [_unpack(a, [conv_shard_shape])[0] for a in _adamw(
        _pack([conv_g], LANE, 8)[None], _pack([w["conv_w"][0]], LANE, 8), _pack([mom["conv_w"][0]], LANE, 8),
        _pack([var["conv_w"][0]], LANE, 8), "adamw_conv")]

    def leaf(kind, n):
        if n in BIG:
            val = big_out[kind][BIG.index(n)]
        elif n == "conv_w":
            val = conv_out[kind]
        else:
            val = small_out[kind][SMALL.index(n)]
        return val.reshape(w[n].shape)

    outs = [loss, grad_x[None]]
    for kind in range(4):
        outs += [leaf(kind, n) for n in WEIGHTS]
    return tuple(outs)
```

```python
import functools
import math

import numpy as np
import jax
import jax.numpy as jnp
from jax import lax
from jax.experimental import pallas as pl
from jax.experimental.pallas import tpu as pltpu

F32 = jnp.float32
BF16 = jnp.bfloat16

D_MODEL = 1024
D_FF = 2816
GDN_HEADS = 4
GDN_HEAD_DIM = 128
GDN_WIDTH = 512
CONV_WIDTH = 5
CHUNK = 64
SWA_HEADS = 8
SWA_HEAD_DIM = 64
SWA_WIDTH = 512
DILATION_PATTERNS = ((128, 1), (512, 4), (2048, 16))
REL_BUCKETS = 32
REL_MAX_DISTANCE = 1024
EPS = 1e-6
NEG_BIG = -1e30
N_DEV = 8

ADAM_LR = 0.001
ADAM_B1 = 0.9
ADAM_B2 = 0.999
ADAM_EPS = 1e-08
ADAM_WD = 0.01
ADAM_STEP = 10

QKV_A = 3 * GDN_WIDTH
OFF_Z = QKV_A
OFF_B = OFF_Z + GDN_WIDTH
OFF_AB = OFF_B + 3 * SWA_WIDTH
N_PAD = OFF_AB + 128
N_IN = 3600

V7X_VMEM_LIMIT_BYTES = 56 * 1024 * 1024
LANE = 128
ATT_BQ = 128
ATT_HALO = 64
CONV_ROWS = 256

NN = (((1,), (0,)), ((), ()))
NT = (((1,), (1,)), ((), ()))
TN = (((0,), (0,)), ((), ()))


def _params(*sem):
    return pltpu.CompilerParams(dimension_semantics=sem, vmem_limit_bytes=V7X_VMEM_LIMIT_BYTES)


def _dot(a, b, dn=NN):
    return lax.dot_general(a.astype(BF16), b.astype(BF16), dn, preferred_element_type=F32)


def _dot_hi(a, b, dn=NN):
    return lax.dot_general(a, b, dn, precision=lax.Precision.HIGHEST, preferred_element_type=F32)


def _sigmoid(x):
    return 1.0 / (1.0 + jnp.exp(-x))


def _matmul(pairs, *, ta=False, tb=False, out_dtype=F32, tm, tn, tk, name, res=None, alpha=None, shard_cols=None):
    a0, b0 = pairs[0]
    m = a0.shape[1] if ta else a0.shape[0]
    k = a0.shape[0] if ta else a0.shape[1]
    n = b0.shape[0] if tb else b0.shape[1]
    tm, tn, tk = min(tm, m), min(tn, n), min(tk, k)
    assert m % tm == 0 and n % tn == 0 and k % tk == 0, (name, m, n, k, tm, tn, tk)
    nk = k // tk
    npairs = len(pairs)
    dn = (((0 if ta else 1,), (1 if tb else 0,)), ((), ()))

    def body(*refs):
        ins = refs[:2 * npairs]
        pos = 2 * npairs
        r_ref = None
        if res is not None:
            r_ref = refs[pos]
            pos += 1
        o_ref, acc = refs[pos], refs[pos + 1]
        kk = pl.program_id(2)
        t = None
        for p in range(npairs):
            d = _dot(ins[2 * p][...], ins[2 * p + 1][...], dn)
            t = d if t is None else t + d

        if nk > 1:
            @pl.when(kk == 0)
            def _():
                acc[...] = t

            @pl.when(kk > 0)
            def _():
                acc[...] += t

        @pl.when(kk == nk - 1)
        def _():
            r = acc[...] if nk > 1 else t
            if alpha is not None:
                r = r * alpha
            if r_ref is not None:
                r = r_ref[...] + r
            if shard_cols is None:
                o_ref[...] = r.astype(out_dtype)
            else:
                for sh in range(tn // shard_cols):
                    o_ref[sh] = r[:, sh * shard_cols:(sh + 1) * shard_cols].astype(out_dtype)

    a_spec = pl.BlockSpec((tk, tm), lambda i, j, kk: (kk, i)) if ta else pl.BlockSpec((tm, tk), lambda i, j, kk: (i, kk))
    b_spec = pl.BlockSpec((tn, tk), lambda i, j, kk: (j, kk)) if tb else pl.BlockSpec((tk, tn), lambda i, j, kk: (kk, j))
    o_spec = pl.BlockSpec((tm, tn), lambda i, j, kk: (i, j))
    in_specs = [a_spec, b_spec] * npairs + ([o_spec] if res is not None else [])
    args = [t for pr in pairs for t in pr] + ([res] if res is not None else [])
    out_spec, out_shape = o_spec, (m, n)
    if shard_cols is not None:
        assert res is None and tn % shard_cols == 0
        out_spec = pl.BlockSpec((tn // shard_cols, tm, shard_cols), lambda i, j, kk: (j, i, 0))
        out_shape = (n // shard_cols, m, shard_cols)
    return pl.pallas_call(
        body, name=name, grid=(m // tm, n // tn, nk), in_specs=in_specs, out_specs=out_spec,
        out_shape=jax.ShapeDtypeStruct(out_shape, out_dtype),
        scratch_shapes=[pltpu.VMEM((tm, tn) if nk > 1 else (8, LANE), F32)],
        compiler_params=_params("parallel", "parallel", "arbitrary"),
    )(*args)


def _rms_fwd(x, w, name):
    s, d = x.shape
    tm = min(512, s)

    def body(x_ref, w_ref, n_ref, r_ref):
        xv = x_ref[...]
        r = lax.rsqrt(jnp.mean(xv * xv, axis=-1, keepdims=True) + EPS)
        n_ref[...] = (xv * r * w_ref[...]).astype(BF16)
        r_ref[...] = r

    return pl.pallas_call(
        body, name=name, grid=(s // tm,),
        in_specs=[pl.BlockSpec((tm, d), lambda i: (i, 0)), pl.BlockSpec((1, d), lambda i: (0, 0))],
        out_specs=[pl.BlockSpec((tm, d), lambda i: (i, 0)), pl.BlockSpec((tm, 1), lambda i: (i, 0))],
        out_shape=[jax.ShapeDtypeStruct((s, d), BF16), jax.ShapeDtypeStruct((s, 1), F32)],
        compiler_params=_params("parallel"),
    )(x, w)


def _rms_bwd(dn, x, r, w, dres, name):
    s, d = x.shape
    tm = min(512, s)

    def body(dn_ref, x_ref, r_ref, w_ref, dres_ref, dx_ref, dw_ref):
        @pl.when(pl.program_id(0) == 0)
        def _():
            dw_ref[...] = jnp.zeros_like(dw_ref)

        rv = r_ref[...]
        xhat = x_ref[...] * rv
        g = dn_ref[...]
        t = g * w_ref[...]
        dx_ref[...] = dres_ref[...] + rv * (t - xhat * jnp.mean(t * xhat, axis=-1, keepdims=True))
        dw_ref[...] += jnp.sum(g * xhat, axis=0, keepdims=True)

    row = pl.BlockSpec((tm, d), lambda i: (i, 0))
    vec = pl.BlockSpec((1, d), lambda i: (0, 0))
    return pl.pallas_call(
        body, name=name, grid=(s // tm,),
        in_specs=[row, row, pl.BlockSpec((tm, 1), lambda i: (i, 0)), vec, row],
        out_specs=[row, vec],
        out_shape=[jax.ShapeDtypeStruct((s, d), F32), jax.ShapeDtypeStruct((1, d), F32)],
        compiler_params=_params("arbitrary"),
    )(dn, x, r, w, dres)


def _final_loss(x3, wf, tgt):
    s, d = x3.shape
    tm = min(512, s)

    def body(x_ref, w_ref, t_ref, loss_ref, dx_ref, dw_ref):
        @pl.when(pl.program_id(0) == 0)
        def _():
            dw_ref[...] = jnp.zeros_like(dw_ref)
            loss_ref[...] = jnp.zeros_like(loss_ref)

        xv = x_ref[...]
        wv = w_ref[...]
        r = lax.rsqrt(jnp.mean(xv * xv, axis=-1, keepdims=True) + EPS)
        xhat = xv * r
        e = xhat * wv - t_ref[...]
        part = 0.5 * jnp.sum(jnp.mean(e * e, axis=-1, keepdims=True), axis=0, keepdims=True)
        loss_ref[...] += jnp.broadcast_to(part, loss_ref.shape)
        dy = e * (1.0 / d)
        dw_ref[...] += jnp.sum(dy * xhat, axis=0, keepdims=True)
        t = dy * wv
        dx_ref[...] = r * (t - xhat * jnp.mean(t * xhat, axis=-1, keepdims=True))

    row = pl.BlockSpec((tm, d), lambda i: (i, 0))
    vec = pl.BlockSpec((1, d), lambda i: (0, 0))
    return pl.pallas_call(
        body, name="final_loss", grid=(s // tm,),
        in_specs=[row, vec, row],
        out_specs=[pl.BlockSpec((1, LANE), lambda i: (0, 0)), row, vec],
        out_shape=[jax.ShapeDtypeStruct((1, LANE), F32), jax.ShapeDtypeStruct((s, d), F32),
                   jax.ShapeDtypeStruct((1, d), F32)],
        compiler_params=_params("arbitrary"),
    )(x3, wf, tgt)


def _ffn_up(n, wg, wu, name):
    s, d = n.shape
    f = wg.shape[1]
    tm, tn = min(512, s), f // 2

    def body(n_ref, wg_ref, wu_ref, g_ref, u_ref, a_ref):
        nv = n_ref[...]
        g = _dot(nv, wg_ref[...])
        u = _dot(nv, wu_ref[...])
        g_ref[...] = g.astype(BF16)
        u_ref[...] = u.astype(BF16)
        a_ref[...] = (g * _sigmoid(g) * u).astype(BF16)

    o = pl.BlockSpec((tm, tn), lambda j, i: (i, j))
    wspec = pl.BlockSpec((d, tn), lambda j, i: (0, j))
    return pl.pallas_call(
        body, name=name, grid=(f // tn, s // tm),
        in_specs=[pl.BlockSpec((tm, d), lambda j, i: (i, 0)), wspec, wspec],
        out_specs=[o, o, o],
        out_shape=[jax.ShapeDtypeStruct((s, f), BF16)] * 3,
        compiler_params=_params("parallel", "parallel"),
    )(n, wg, wu)


def _ffn_dact(dx, wd, g, u, name):
    s, d = dx.shape
    f = wd.shape[0]
    tm, tn = min(512, s), f // 2

    def body(dx_ref, wd_ref, g_ref, u_ref, dg_ref, du_ref):
        da = 0.5 * _dot(dx_ref[...], wd_ref[...], NT)
        gv = g_ref[...].astype(F32)
        sg = _sigmoid(gv)
        du_ref[...] = (da * gv * sg).astype(BF16)
        dg_ref[...] = (da * u_ref[...].astype(F32) * (sg * (1.0 + gv * (1.0 - sg)))).astype(BF16)

    o = pl.BlockSpec((tm, tn), lambda j, i: (i, j))
    return pl.pallas_call(
        body, name=name, grid=(f // tn, s // tm),
        in_specs=[pl.BlockSpec((tm, d), lambda j, i: (i, 0)), pl.BlockSpec((tn, d), lambda j, i: (j, 0)), o, o],
        out_specs=[o, o],
        out_shape=[jax.ShapeDtypeStruct((s, f), BF16), jax.ShapeDtypeStruct((s, f), BF16)],
        compiler_params=_params("parallel", "parallel"),
    )(dx, wd, g, u)


def _ffn_forward(x, norm_w, wg, wu, wd, tag):
    n, r = _rms_fwd(x, norm_w, f"{tag}_norm")
    g, u, a = _ffn_up(n, wg, wu, f"{tag}_up")
    y = _matmul([(a, wd)], tm=512, tn=1024, tk=wd.shape[0], name=f"{tag}_down", res=x, alpha=0.5)
    return y, (n, r, g, u, a)


def _ffn_backward(dy, x, norm_w, wg, wu, wd, saved, tag):
    n, r, g, u, a = saved
    dwd = _matmul([(a, dy)], ta=True, tm=1408, tn=1024, tk=1024, name=f"{tag}_dwd", alpha=0.5)
    dg, du = _ffn_dact(dy, wd, g, u, f"{tag}_dact")
    cols = wg.shape[1] // N_DEV
    dwg = _matmul([(n, dg)], ta=True, tm=1024, tn=1408, tk=1024, name=f"{tag}_dwg", shard_cols=cols)
    dwu = _matmul([(n, du)], ta=True, tm=1024, tn=1408, tk=1024, name=f"{tag}_dwu", shard_cols=cols)
    dn = _matmul([(dg, wg), (du, wu)], tb=True, tm=512, tn=1024, tk=wg.shape[1], name=f"{tag}_dn")
    dx, dnorm = _rms_bwd(dn, x, r, norm_w, dy, f"{tag}_dnorm")
    return dx, dnorm, dwg, dwu, dwd


Q_SCALE = GDN_HEAD_DIM ** -0.5
CONV_HALO = 8


def _conv_taps(win, w_ref, rows, sign):
    n = rows + 2 * CONV_HALO
    acc = None
    for t in range(CONV_WIDTH):
        o = sign * (t - CONV_WIDTH // 2)
        sh = win if o == 0 else pltpu.roll(win, (-o) % n, 0)
        term = sh[CONV_HALO:CONV_HALO + rows] * w_ref[t:t + 1, :]
        acc = term if acc is None else acc + term
    return acc


def _gdn_conv_fwd(p_pad, conv_wt):
    s = p_pad.shape[0]
    rows = min(CONV_ROWS, s)
    nblk = QKV_A // LANE

    def body(p_ref, w_ref, c_ref, y_ref, pad):
        j = pl.program_id(0)
        zeros = jnp.zeros((CONV_HALO, LANE), F32)
        pad[0:CONV_HALO, :] = zeros
        pad[CONV_HALO + s:2 * CONV_HALO + s, :] = zeros
        pad[CONV_HALO:CONV_HALO + s, :] = p_ref[...]

        def chunk(ci, carry):
            b = pl.multiple_of(ci * rows, rows)
            win = pad[pl.ds(b, rows + 2 * CONV_HALO), :]
            c = _conv_taps(win, w_ref, rows, 1)
            c_ref[pl.ds(b, rows), :] = c
            act = c * _sigmoid(c)
            nrm = lax.rsqrt(jnp.sum(act * act, axis=-1, keepdims=True) + EPS)
            mult = jnp.where(j < GDN_HEADS, nrm * Q_SCALE, jnp.where(j < 2 * GDN_HEADS, nrm, 1.0))
            y_ref[pl.ds(b, rows), :] = act * mult
            return carry

        lax.fori_loop(0, s // rows, chunk, 0)

    col = pl.BlockSpec((s, LANE), lambda j: (0, j))
    return pl.pallas_call(
        body, name="gdn_conv_fwd", grid=(nblk,),
        in_specs=[col, pl.BlockSpec((8, LANE), lambda j: (0, j))],
        out_specs=[col, col],
        out_shape=[jax.ShapeDtypeStruct((s, QKV_A), F32), jax.ShapeDtypeStruct((s, QKV_A), F32)],
        scratch_shapes=[pltpu.VMEM((s + 2 * CONV_HALO, LANE), F32)],
        compiler_params=_params("parallel"),
    )(p_pad, conv_wt)


def _gdn_conv_bwd(dy_f, dy_r, c_pre, p_pad, conv_wt):
    s = p_pad.shape[0]
    rows = min(CONV_ROWS, s)
    nblk = QKV_A // LANE

    def body(dyf_ref, dyr_ref, c_ref, p_ref, w_ref, dp_ref, dw_ref, ppad, dcpad):
        j = pl.program_id(0)
        zeros = jnp.zeros((CONV_HALO, LANE), F32)
        for buf in (ppad, dcpad):
            buf[0:CONV_HALO, :] = zeros
            buf[CONV_HALO + s:2 * CONV_HALO + s, :] = zeros
        ppad[CONV_HALO:CONV_HALO + s, :] = p_ref[...]

        def act_bwd(ci, carry):
            b = pl.multiple_of(ci * rows, rows)
            c = c_ref[pl.ds(b, rows), :]
            g = dyf_ref[pl.ds(b, rows), :] + dyr_ref[pl.ds(b, rows), :]
            sg = _sigmoid(c)
            act = c * sg
            nrm = lax.rsqrt(jnp.sum(act * act, axis=-1, keepdims=True) + EPS)
            yh = act * nrm
            scale = jnp.where(j < GDN_HEADS, Q_SCALE, 1.0)
            dact_qk = (scale * nrm) * (g - yh * jnp.sum(g * yh, axis=-1, keepdims=True))
            dact = jnp.where(j < 2 * GDN_HEADS, dact_qk, g)
            dcpad[pl.ds(pl.multiple_of(b + CONV_HALO, CONV_HALO), rows), :] = dact * (sg * (1.0 + c * (1.0 - sg)))
            return carry

        lax.fori_loop(0, s // rows, act_bwd, 0)
        tap = lax.broadcasted_iota(jnp.int32, (8, LANE), 0)

        def taps_bwd(ci, dw):
            b = pl.multiple_of(ci * rows, rows)
            dcw = dcpad[pl.ds(b, rows + 2 * CONV_HALO), :]
            dp_ref[pl.ds(b, rows), :] = _conv_taps(dcw, w_ref, rows, -1)
            pw = ppad[pl.ds(b, rows + 2 * CONV_HALO), :]
            dc = dcw[CONV_HALO:CONV_HALO + rows]
            n = rows + 2 * CONV_HALO
            for t in range(CONV_WIDTH):
                o = t - CONV_WIDTH // 2
                sh = pw if o == 0 else pltpu.roll(pw, (-o) % n, 0)
                row = jnp.sum(dc * sh[CONV_HALO:CONV_HALO + rows], axis=0, keepdims=True)
                dw = dw + jnp.where(tap == t, row, 0.0)
            return dw

        dw_ref[...] = lax.fori_loop(0, s // rows, taps_bwd, jnp.zeros((8, LANE), F32))

    col = pl.BlockSpec((s, LANE), lambda j: (0, j))
    wspec = pl.BlockSpec((8, LANE), lambda j: (0, j))
    return pl.pallas_call(
        body, name="gdn_conv_bwd", grid=(nblk,),
        in_specs=[col, col, col, col, wspec],
        out_specs=[col, wspec],
        out_shape=[jax.ShapeDtypeStruct((s, QKV_A), F32), jax.ShapeDtypeStruct((8, QKV_A), F32)],
        scratch_shapes=[pltpu.VMEM((s + 2 * CONV_HALO, LANE), F32), pltpu.VMEM((s + 2 * CONV_HALO, LANE), F32)],
        compiler_params=_params("parallel"),
    )(dy_f, dy_r, c_pre, p_pad, conv_wt)


def _softplus(x):
    return jnp.maximum(x, 0.0) + jnp.log(1.0 + jnp.exp(-jnp.abs(x)))


def _gdn_gates_fwd(p_pad, alog_row, dt_row):
    s = p_pad.shape[0]
    tm = min(1024, s)

    def body(p_ref, al_ref, dt_ref, o_ref):
        x = p_ref[...]
        lane = lax.broadcasted_iota(jnp.int32, x.shape, 1)
        g = -jnp.exp(al_ref[...]) * _softplus(x + dt_ref[...])
        o_ref[...] = jnp.where(lane < 8, g, jnp.where(lane < 16, _sigmoid(x), 0.0))

    vec = pl.BlockSpec((1, LANE), lambda i: (0, 0))
    return pl.pallas_call(
        body, name="gdn_gates_fwd", grid=(s // tm,),
        in_specs=[pl.BlockSpec((tm, LANE), lambda i: (i, OFF_AB // LANE)), vec, vec],
        out_specs=pl.BlockSpec((tm, LANE), lambda i: (i, 0)),
        out_shape=jax.ShapeDtypeStruct((s, LANE), F32),
        compiler_params=_params("parallel"),
    )(p_pad, alog_row, dt_row)


def _gdn_gates_bwd(dgb_f, dgb_r, p_pad, gb, alog_row, dt_row):
    s = p_pad.shape[0]
    tm = min(1024, s)

    def body(df_ref, dr_ref, p_ref, gb_ref, al_ref, dt_ref, dp_ref, sum_ref):
        @pl.when(pl.program_id(0) == 0)
        def _():
            sum_ref[...] = jnp.zeros_like(sum_ref)

        x = p_ref[...]
        gbv = gb_ref[...]
        dgb = df_ref[...] + dr_ref[...]
        lane = lax.broadcasted_iota(jnp.int32, x.shape, 1)
        da = dgb * (-jnp.exp(al_ref[...])) * _sigmoid(x + dt_ref[...])
        db = dgb * gbv * (1.0 - gbv)
        dp_ref[...] = jnp.where(lane < 8, da, jnp.where(lane < 16, db, 0.0))
        row = lax.broadcasted_iota(jnp.int32, (8, LANE), 0)
        lane8 = lax.broadcasted_iota(jnp.int32, (8, LANE), 1)
        d_alog = jnp.sum(dgb * gbv, axis=0, keepdims=True)
        d_dt = jnp.sum(da, axis=0, keepdims=True)
        upd = jnp.where(row == 0, d_alog, jnp.where(row == 1, d_dt, 0.0))
        sum_ref[...] += jnp.where(lane8 < 8, upd, 0.0)

    vec = pl.BlockSpec((1, LANE), lambda i: (0, 0))
    blk = pl.BlockSpec((tm, LANE), lambda i: (i, 0))
    return pl.pallas_call(
        body, name="gdn_gates_bwd", grid=(s // tm,),
        in_specs=[blk, blk, pl.BlockSpec((tm, LANE), lambda i: (i, OFF_AB // LANE)), blk, vec, vec],
        out_specs=[blk, pl.BlockSpec((8, LANE), lambda i: (0, 0))],
        out_shape=[jax.ShapeDtypeStruct((s, LANE), F32), jax.ShapeDtypeStruct((8, LANE), F32)],
        compiler_params=_params("arbitrary"),
    )(dgb_f, dgb_r, p_pad, gb, alog_row, dt_row)


def _chunk_masks(rev):
    row = lax.broadcasted_iota(jnp.int32, (CHUNK, CHUNK), 0)
    col = lax.broadcasted_iota(jnp.int32, (CHUNK, CHUNK), 1)
    le = (col >= row) if rev else (col <= row)
    strict = (col > row) if rev else (col < row)
    return le, strict, row == col


def _chunk_common(q, k, v, g, beta, gc, masks):
    le, strict, eye = masks
    gc_row = _dot_hi(jnp.ones((CHUNK, CHUNK), F32), jnp.where(eye, gc, 0.0))
    decay = jnp.where(le, jnp.exp(jnp.where(le, gc - gc_row, 0.0)), 0.0)
    eg = jnp.exp(gc)
    gl = jnp.sum(g, axis=0, keepdims=True)
    kb = k * beta
    vb = v * beta
    kbeg = kb * eg
    lm = jnp.where(strict, _dot(kb, k, NT) * decay, 0.0)
    intra = _dot(q, k, NT) * decay
    qg = q * eg
    edec = jnp.exp(gl - gc)
    kdec = k * edec
    return dict(decay=decay, eg=eg, gl=gl, kb=kb, vb=vb, kbeg=kbeg, lm=lm, intra=intra, qg=qg, edec=edec, kdec=kdec)


def _unit_lower_inverse(lm, eye):
    x = -lm
    t = eye.astype(F32) + x
    p = x
    for _ in range(5):
        p = _dot_hi(p, p)
        t = t + _dot_hi(t, p)
    return t


def _gate_lanes(rev, h):
    d = 1 if rev else 0
    return d * GDN_HEADS + h, 8 + d * GDN_HEADS + h


def _delta_fwd(y, gb, rev):
    s = y.shape[0]
    nc = s // CHUNK
    hd = GDN_HEAD_DIM

    def chunk_of(n):
        return nc - 1 - n if rev else n

    def body(q_ref, k_ref, v_ref, gb_ref, o_ref, s_all, t_all, state):
        @pl.when(pl.program_id(0) == 0)
        def _():
            state[...] = jnp.zeros_like(state)

        masks = _chunk_masks(rev)
        gbv = gb_ref[...]
        gcm = _dot_hi(masks[0].astype(F32), gbv)
        for h in range(GDN_HEADS):
            gi, bi = _gate_lanes(rev, h)
            sl = slice(h * hd, (h + 1) * hd)
            q, k, v = q_ref[:, sl], k_ref[:, sl], v_ref[:, sl]
            g, beta, gc = gbv[:, gi:gi + 1], gbv[:, bi:bi + 1], gcm[:, gi:gi + 1]
            cm = _chunk_common(q, k, v, g, beta, gc, masks)
            tinv = _unit_lower_inverse(cm["lm"], masks[2])
            u = _dot(tinv, cm["vb"])
            w = _dot(tinv, cm["kbeg"])
            st = state[h]
            v_new = u - _dot(w, st)
            o_ref[:, sl] = _dot(cm["qg"], st) + _dot(cm["intra"], v_new)
            s_all[0, h] = st
            t_all[0, h] = tinv
            state[h] = st * jnp.exp(cm["gl"]) + _dot(cm["kdec"], v_new, TN)

    def col(j):
        return pl.BlockSpec((CHUNK, GDN_WIDTH), lambda n: (chunk_of(n), j))

    return pl.pallas_call(
        body, name="delta_fwd_r" if rev else "delta_fwd_f", grid=(nc,),
        in_specs=[col(0), col(1), col(2), pl.BlockSpec((CHUNK, LANE), lambda n: (chunk_of(n), 0))],
        out_specs=[pl.BlockSpec((CHUNK, GDN_WIDTH), lambda n: (chunk_of(n), 0)),
                   pl.BlockSpec((1, GDN_HEADS, hd, hd), lambda n: (chunk_of(n), 0, 0, 0)),
                   pl.BlockSpec((1, GDN_HEADS, CHUNK, CHUNK), lambda n: (chunk_of(n), 0, 0, 0))],
        out_shape=[jax.ShapeDtypeStruct((s, GDN_WIDTH), F32),
                   jax.ShapeDtypeStruct((nc, GDN_HEADS, hd, hd), F32),
                   jax.ShapeDtypeStruct((nc, GDN_HEADS, CHUNK, CHUNK), F32)],
        scratch_shapes=[pltpu.VMEM((GDN_HEADS, hd, hd), F32)],
        compiler_params=_params("arbitrary"),
    )(y, y, y, gb)


def _delta_bwd(y, gb, do, s_all, t_all, rev):
    s = y.shape[0]
    nc = s // CHUNK
    hd = GDN_HEAD_DIM

    def chunk_of(n):
        return n if rev else nc - 1 - n

    def body(q_ref, k_ref, v_ref, gb_ref, do_ref, s_ref, t_ref, dy_ref, dgb_ref, dstate):
        @pl.when(pl.program_id(0) == 0)
        def _():
            dstate[...] = jnp.zeros_like(dstate)

        masks = _chunk_masks(rev)
        le, strict, _ = masks
        le_t = _chunk_masks(not rev)[0].astype(F32)
        gbv = gb_ref[...]
        gcm = _dot_hi(le.astype(F32), gbv)
        lane = lax.broadcasted_iota(jnp.int32, (CHUNK, LANE), 1)
        ones_cl = jnp.ones((CHUNK, LANE), F32)
        dgc_tile = jnp.zeros((CHUNK, LANE), F32)
        rest_tile = jnp.zeros((CHUNK, LANE), F32)
        for h in range(GDN_HEADS):
            gi, bi = _gate_lanes(rev, h)
            sl = slice(h * hd, (h + 1) * hd)
            q, k, v = q_ref[:, sl], k_ref[:, sl], v_ref[:, sl]
            g, beta, gc = gbv[:, gi:gi + 1], gbv[:, bi:bi + 1], gcm[:, gi:gi + 1]
            cm = _chunk_common(q, k, v, g, beta, gc, masks)
            tinv = t_ref[0, h]
            st = s_ref[0, h]
            ds_out = dstate[h]
            dov = do_ref[:, sl]
            u = _dot(tinv, cm["vb"])
            w = _dot(tinv, cm["kbeg"])
            v_new = u - _dot(w, st)
            egl = jnp.exp(cm["gl"])
            d_qg = _dot(dov, st, NT)
            d_intra = _dot(dov, v_new, NT)
            dv_new = _dot(cm["intra"], dov, TN) + _dot(cm["kdec"], ds_out)
            d_kdec = _dot(v_new, ds_out, NT)
            dstate[h] = _dot(cm["qg"], dov, TN) + egl * ds_out - _dot(w, dv_new, TN)
            dgl = egl * jnp.sum(jnp.sum(st * ds_out, axis=1, keepdims=True), axis=0, keepdims=True)
            dw = -_dot(dv_new, st, NT)
            dvb = _dot(tinv, dv_new, TN)
            dkbeg = _dot(tinv, dw, TN)
            dlm = jnp.where(strict, -(_dot(dvb, u, NT) + _dot(dkbeg, w, NT)), 0.0)
            d_a = dlm * cm["decay"]
            d_qk = d_intra * cm["decay"]
            e = dlm * cm["lm"] + d_intra * cm["intra"]
            dgc = jnp.sum(e, axis=1, keepdims=True) - _dot_hi(e, ones_cl, TN)[:, 0:1]
            dkb = _dot(d_a, k) + dkbeg * cm["eg"]
            dk = _dot(d_a, cm["kb"], TN) + _dot(d_qk, q, TN)
            dq = _dot(d_qk, k) + d_qg * cm["eg"]
            dgc = dgc + jnp.sum(d_qg * cm["qg"], axis=1, keepdims=True)
            dgc = dgc + jnp.sum(dkbeg * cm["kbeg"], axis=1, keepdims=True)
            tdec = jnp.sum(d_kdec * cm["kdec"], axis=1, keepdims=True)
            dk = dk + d_kdec * cm["edec"] + dkb * beta
            dgc = dgc - tdec
            dgl = dgl + jnp.sum(tdec, axis=0, keepdims=True)
            dbeta = jnp.sum(dvb * v, axis=1, keepdims=True) + jnp.sum(dkb * k, axis=1, keepdims=True)
            dy_ref[:, h * hd:(h + 1) * hd] = dq
            dy_ref[:, GDN_WIDTH + h * hd:GDN_WIDTH + (h + 1) * hd] = dk
            dy_ref[:, 2 * GDN_WIDTH + h * hd:2 * GDN_WIDTH + (h + 1) * hd] = dvb * beta
            dgc_tile = dgc_tile + jnp.where(lane == gi, dgc, 0.0)
            rest_tile = rest_tile + jnp.where(lane == gi, dgl, 0.0) + jnp.where(lane == bi, dbeta, 0.0)
        dgb_ref[...] = _dot_hi(le_t, dgc_tile) + rest_tile

    def col(j):
        return pl.BlockSpec((CHUNK, GDN_WIDTH), lambda n: (chunk_of(n), j))

    first = pl.BlockSpec((CHUNK, GDN_WIDTH), lambda n: (chunk_of(n), 0))
    return pl.pallas_call(
        body, name="delta_bwd_r" if rev else "delta_bwd_f", grid=(nc,),
        in_specs=[col(0), col(1), col(2), pl.BlockSpec((CHUNK, LANE), lambda n: (chunk_of(n), 0)), first,
                  pl.BlockSpec((1, GDN_HEADS, hd, hd), lambda n: (chunk_of(n), 0, 0, 0)),
                  pl.BlockSpec((1, GDN_HEADS, CHUNK, CHUNK), lambda n: (chunk_of(n), 0, 0, 0))],
        out_specs=[pl.BlockSpec((CHUNK, QKV_A), lambda n: (chunk_of(n), 0)),
                   pl.BlockSpec((CHUNK, LANE), lambda n: (chunk_of(n), 0))],
        out_shape=[jax.ShapeDtypeStruct((s, QKV_A), F32), jax.ShapeDtypeStruct((s, LANE), F32)],
        scratch_shapes=[pltpu.VMEM((GDN_HEADS, hd, hd), F32)],
        compiler_params=_params("arbitrary"),
    )(y, y, y, gb, do, s_all, t_all)


BNN = (((2,), (1,)), ((0,), (0,)))
BNT = (((2,), (2,)), ((0,), (0,)))
BTN = (((1,), (1,)), ((0,), (0,)))
NB = 2 * GDN_HEADS


def _bdot(a, b, dn=BNN):
    return lax.dot_general(a.astype(BF16), b.astype(BF16), dn, preferred_element_type=F32)


def _dot3(a, b, dn):
    ah = a.astype(BF16)
    al = (a - ah.astype(F32)).astype(BF16)
    bh = b.astype(BF16)
    bl = (b - bh.astype(F32)).astype(BF16)

    def d(x, y):
        return lax.dot_general(x, y, dn, preferred_element_type=F32)

    return d(ah, bh) + d(ah, bl) + d(al, bh)


def _both(f_val, r_val):
    return jnp.stack([f_val] * GDN_HEADS + [r_val] * GDN_HEADS)


def _heads(ref_f, ref_r):
    hd = GDN_HEAD_DIM
    return jnp.stack([ref_f[:, h * hd:(h + 1) * hd] for h in range(GDN_HEADS)]
                     + [ref_r[:, h * hd:(h + 1) * hd] for h in range(GDN_HEADS)])


def _gate_cols(tile_f, tile_r, base):
    return jnp.stack([tile_f[:, base + h:base + h + 1] for h in range(GDN_HEADS)]
                     + [tile_r[:, base + GDN_HEADS + h:base + GDN_HEADS + h + 1] for h in range(GDN_HEADS)])


def _chunk_common2(q, k, v, gbf, gbr):
    mf, mr = _chunk_masks(False), _chunk_masks(True)
    le, strict = _both(mf[0], mr[0]), _both(mf[1], mr[1])
    eye = mf[2]
    gcm_f = _dot3(mf[0].astype(F32), gbf, NN)
    gcm_r = _dot3(mr[0].astype(F32), gbr, NN)
    g, beta, gc = _gate_cols(gbf, gbr, 0), _gate_cols(gbf, gbr, 8), _gate_cols(gcm_f, gcm_r, 0)
    gc_row = _dot3(jnp.ones((NB, CHUNK, CHUNK), F32), jnp.where(eye[None], gc, 0.0), BNN)
    decay = jnp.where(le, jnp.exp(jnp.where(le, gc - gc_row, 0.0)), 0.0)
    eg = jnp.exp(gc)
    gl = jnp.sum(g, axis=1, keepdims=True)
    kb = k * beta
    vb = v * beta
    kbeg = kb * eg
    lm = jnp.where(strict, _bdot(kb, k, BNT) * decay, 0.0)
    intra = _bdot(q, k, BNT) * decay
    edec = jnp.exp(gl - gc)
    return dict(strict=strict, eye=eye, beta=beta, decay=decay, eg=eg, gl=gl, kb=kb, vb=vb, kbeg=kbeg,
                lm=lm, intra=intra, qg=q * eg, edec=edec, kdec=k * edec)


def _unit_triangular_inverse(lm, eye):
    x = -lm
    t = eye[None].astype(F32) + x
    p = x
    for _ in range(5):
        p = _dot3(p, p, BNN)
        t = t + _dot3(t, p, BNN)
    return t


def _delta_fwd2(y, gb):
    s = y.shape[0]
    nc = s // CHUNK
    hd = GDN_HEAD_DIM

    def body(qf, kf, vf, gf, qr, kr, vr, gr, of_ref, or_ref, sf_all, sr_all, tf_all, tr_all, state):
        @pl.when(pl.program_id(0) == 0)
        def _():
            state[...] = jnp.zeros_like(state)

        q, k, v = _heads(qf, qr), _heads(kf, kr), _heads(vf, vr)
        cm = _chunk_common2(q, k, v, gf[...], gr[...])
        tinv = _unit_triangular_inverse(cm["lm"], cm["eye"])
        u = _bdot(tinv, cm["vb"])
        w = _bdot(tinv, cm["kbeg"])
        st = state[...]
        v_new = u - _bdot(w, st)
        o = _bdot(cm["qg"], st) + _bdot(cm["intra"], v_new)
        state[...] = st * jnp.exp(cm["gl"]) + _bdot(cm["kdec"], v_new, BTN)
        for h in range(GDN_HEADS):
            of_ref[:, h * hd:(h + 1) * hd] = o[h]
            or_ref[:, h * hd:(h + 1) * hd] = o[GDN_HEADS + h]
        sf_all[0] = st[:GDN_HEADS]
        sr_all[0] = st[GDN_HEADS:]
        tf_all[0] = tinv[:GDN_HEADS]
        tr_all[0] = tinv[GDN_HEADS:]

    def col(j, rev):
        return pl.BlockSpec((CHUNK, GDN_WIDTH), (lambda n: (nc - 1 - n, j)) if rev else (lambda n: (n, j)))

    def gate(rev):
        return pl.BlockSpec((CHUNK, LANE), (lambda n: (nc - 1 - n, 0)) if rev else (lambda n: (n, 0)))

    def per_chunk(d1, d2, rev):
        return pl.BlockSpec((1, GDN_HEADS, d1, d2), (lambda n: (nc - 1 - n, 0, 0, 0)) if rev else (lambda n: (n, 0, 0, 0)))

    return pl.pallas_call(
        body, name="delta_fwd", grid=(nc,),
        in_specs=[col(0, False), col(1, False), col(2, False), gate(False), col(0, True), col(1, True), col(2, True), gate(True)],
        out_specs=[col(0, False), col(0, True), per_chunk(hd, hd, False), per_chunk(hd, hd, True),
                   per_chunk(CHUNK, CHUNK, False), per_chunk(CHUNK, CHUNK, True)],
        out_shape=[jax.ShapeDtypeStruct((s, GDN_WIDTH), F32)] * 2 + [jax.ShapeDtypeStruct((nc, GDN_HEADS, hd, hd), F32)] * 2
        + [jax.ShapeDtypeStruct((nc, GDN_HEADS, CHUNK, CHUNK), F32)] * 2,
        scratch_shapes=[pltpu.VMEM((NB, hd, hd), F32)],
        compiler_params=_params("arbitrary"),
    )(y, y, y, gb, y, y, y, gb)


def _delta_bwd2(y, gb, do, sf_all, sr_all, tf_all, tr_all):
    s = y.shape[0]
    nc = s // CHUNK
    hd = GDN_HEAD_DIM

    def body(qf, kf, vf, gf, dof, sf, tf, qr, kr, vr, gr, dor, sr, tr, dyf_ref, dyr_ref, dgf_ref, dgr_ref, dstate):
        @pl.when(pl.program_id(0) == 0)
        def _():
            dstate[...] = jnp.zeros_like(dstate)

        q, k, v, dov = _heads(qf, qr), _heads(kf, kr), _heads(vf, vr), _heads(dof, dor)
        cm = _chunk_common2(q, k, v, gf[...], gr[...])
        tinv = jnp.concatenate([tf[0], tr[0]], axis=0)
        st = jnp.concatenate([sf[0], sr[0]], axis=0)
        ds_out = dstate[...]
        decay, lm, intra, qg, kdec, kbeg, eg, kb, beta = (
            cm[n] for n in ("decay", "lm", "intra", "qg", "kdec", "kbeg", "eg", "kb", "beta"))
        u = _bdot(tinv, cm["vb"])
        w = _bdot(tinv, kbeg)
        v_new = u - _bdot(w, st)
        egl = jnp.exp(cm["gl"])
        d_qg = _bdot(dov, st, BNT)
        d_intra = _bdot(dov, v_new, BNT)
        dv_new = _bdot(intra, dov, BTN) + _bdot(kdec, ds_out)
        d_kdec = _bdot(v_new, ds_out, BNT)
        dstate[...] = _bdot(qg, dov, BTN) + egl * ds_out - _bdot(w, dv_new, BTN)
        dgl = egl * jnp.sum(jnp.sum(st * ds_out, axis=2, keepdims=True), axis=1, keepdims=True)
        dw = -_bdot(dv_new, st, BNT)
        dvb = _bdot(tinv, dv_new, BTN)
        dkbeg = _bdot(tinv, dw, BTN)
        dlm = jnp.where(cm["strict"], -(_bdot(dvb, u, BNT) + _bdot(dkbeg, w, BNT)), 0.0)
        d_a = dlm * decay
        d_qk = d_intra * decay
        e = dlm * lm + d_intra * intra
        colsum = _dot3(e, jnp.ones((NB, CHUNK, LANE), F32), BTN)[:, :, 0:1]
        dgc = jnp.sum(e, axis=2, keepdims=True) - colsum
        dkb = _bdot(d_a, k) + dkbeg * eg
        dk = _bdot(d_a, kb, BTN) + _bdot(d_qk, q, BTN)
        dq = _bdot(d_qk, k) + d_qg * eg
        dgc = dgc + jnp.sum(d_qg * qg, axis=2, keepdims=True) + jnp.sum(dkbeg * kbeg, axis=2, keepdims=True)
        tdec = jnp.sum(d_kdec * kdec, axis=2, keepdims=True)
        dk = dk + d_kdec * cm["edec"] + dkb * beta
        dgc = dgc - tdec
        dgl = dgl + jnp.sum(tdec, axis=1, keepdims=True)
        dbeta = jnp.sum(dvb * v, axis=2, keepdims=True) + jnp.sum(dkb * k, axis=2, keepdims=True)
        dv = dvb * beta
        lane = lax.broadcasted_iota(jnp.int32, (CHUNK, LANE), 1)
        for rev, dy_ref, dg_ref in ((False, dyf_ref, dgf_ref), (True, dyr_ref, dgr_ref)):
            dgc_tile = jnp.zeros((CHUNK, LANE), F32)
            rest = jnp.zeros((CHUNK, LANE), F32)
            for h in range(GDN_HEADS):
                b = (GDN_HEADS if rev else 0) + h
                gi, bi = _gate_lanes(rev, h)
                dgc_tile = dgc_tile + jnp.where(lane == gi, dgc[b], 0.0)
                rest = rest + jnp.where(lane == gi, dgl[b], 0.0) + jnp.where(lane == bi, dbeta[b], 0.0)
                dy_ref[:, h * hd:(h + 1) * hd] = dq[b]
                dy_ref[:, GDN_WIDTH + h * hd:GDN_WIDTH + (h + 1) * hd] = dk[b]
                dy_ref[:, 2 * GDN_WIDTH + h * hd:2 * GDN_WIDTH + (h + 1) * hd] = dv[b]
            le_t = _chunk_masks(not rev)[0].astype(F32)
            dg_ref[...] = _dot3(le_t, dgc_tile, NN) + rest

    def col(j, rev):
        return pl.BlockSpec((CHUNK, GDN_WIDTH), (lambda n: (n, j)) if rev else (lambda n: (nc - 1 - n, j)))

    def wide(width, rev):
        return pl.BlockSpec((CHUNK, width), (lambda n: (n, 0)) if rev else (lambda n: (nc - 1 - n, 0)))

    def per_chunk(d1, d2, rev):
        return pl.BlockSpec((1, GDN_HEADS, d1, d2), (lambda n: (n, 0, 0, 0)) if rev else (lambda n: (nc - 1 - n, 0, 0, 0)))

    def side(rev):
        return [col(0, rev), col(1, rev), col(2, rev), wide(LANE, rev), wide(GDN_WIDTH, rev), per_chunk(hd, hd, rev),
                per_chunk(CHUNK, CHUNK, rev)]

    return pl.pallas_call(
        body, name="delta_bwd", grid=(nc,),
        in_specs=side(False) + side(True),
        out_specs=[wide(QKV_A, False), wide(QKV_A, True), wide(LANE, False), wide(LANE, True)],
        out_shape=[jax.ShapeDtypeStruct((s, QKV_A), F32)] * 2 + [jax.ShapeDtypeStruct((s, LANE), F32)] * 2,
        scratch_shapes=[pltpu.VMEM((NB, hd, hd), F32)],
        compiler_params=_params("arbitrary"),
    )(y, y, y, gb, do, sf_all, tf_all, y, y, y, gb, do, sr_all, tr_all)


def _gdn_post_fwd(o_f, o_r, p_pad, norm_row):
    s = o_f.shape[0]
    tm = min(512, s)
    hd = GDN_HEAD_DIM

    def body(of_ref, or_ref, z_ref, w_ref, out_ref, osum_ref):
        o = of_ref[...] + or_ref[...]
        osum_ref[...] = o
        z = z_ref[...]
        gate = z * _sigmoid(z)
        for h in range(GDN_HEADS):
            sl = slice(h * hd, (h + 1) * hd)
            oh = o[:, sl]
            r = lax.rsqrt(jnp.mean(oh * oh, axis=-1, keepdims=True) + EPS)
            out_ref[:, sl] = (oh * r * w_ref[...] * gate[:, sl]).astype(BF16)

    blk = pl.BlockSpec((tm, GDN_WIDTH), lambda i: (i, 0))
    return pl.pallas_call(
        body, name="gdn_post_fwd", grid=(s // tm,),
        in_specs=[blk, blk, pl.BlockSpec((tm, GDN_WIDTH), lambda i: (i, OFF_Z // GDN_WIDTH)),
                  pl.BlockSpec((1, hd), lambda i: (0, 0))],
        out_specs=[blk, blk],
        out_shape=[jax.ShapeDtypeStruct((s, GDN_WIDTH), BF16), jax.ShapeDtypeStruct((s, GDN_WIDTH), F32)],
        compiler_params=_params("parallel"),
    )(o_f, o_r, p_pad, norm_row)


def _gdn_post_bwd(d_out, o_sum, p_pad, norm_row):
    s = o_sum.shape[0]
    tm = min(512, s)
    hd = GDN_HEAD_DIM

    def body(d_ref, o_ref, z_ref, w_ref, do_ref, dz_ref, dw_ref):
        @pl.when(pl.program_id(0) == 0)
        def _():
            dw_ref[...] = jnp.zeros_like(dw_ref)

        z = z_ref[...]
        sg = _sigmoid(z)
        gate = z * sg
        dgate = sg * (1.0 + z * (1.0 - sg))
        wv = w_ref[...]
        dw = jnp.zeros((1, hd), F32)
        for h in range(GDN_HEADS):
            sl = slice(h * hd, (h + 1) * hd)
            oh = o_ref[:, sl]
            dh = d_ref[:, sl]
            r = lax.rsqrt(jnp.mean(oh * oh, axis=-1, keepdims=True) + EPS)
            ohat = oh * r
            dz_ref[:, sl] = dh * ohat * wv * dgate[:, sl]
            drn = dh * gate[:, sl]
            t = drn * wv
            do_ref[:, sl] = r * (t - ohat * jnp.mean(t * ohat, axis=-1, keepdims=True))
            dw = dw + jnp.sum(drn * ohat, axis=0, keepdims=True)
        dw_ref[...] += dw

    blk = pl.BlockSpec((tm, GDN_WIDTH), lambda i: (i, 0))
    vec = pl.BlockSpec((1, hd), lambda i: (0, 0))
    return pl.pallas_call(
        body, name="gdn_post_bwd", grid=(s // tm,),
        in_specs=[blk, blk, pl.BlockSpec((tm, GDN_WIDTH), lambda i: (i, OFF_Z // GDN_WIDTH)), vec],
        out_specs=[blk, blk, vec],
        out_shape=[jax.ShapeDtypeStruct((s, GDN_WIDTH), F32), jax.ShapeDtypeStruct((s, GDN_WIDTH), F32),
                   jax.ShapeDtypeStruct((1, hd), F32)],
        compiler_params=_params("arbitrary"),
    )(d_out, o_sum, p_pad, norm_row)


def _add2(a, b, name):
    s, w = a.shape
    tm = next(t for t in (1024, 640, 512, 256, 128, 64, 8) if s % t == 0)

    def body(a_ref, b_ref, o_ref):
        o_ref[...] = a_ref[...] + b_ref[...]

    blk = pl.BlockSpec((tm, w), lambda i: (i, 0))
    return pl.pallas_call(body, name=name, grid=(s // tm,), in_specs=[blk, blk], out_specs=blk,
                          out_shape=jax.ShapeDtypeStruct((s, w), F32), compiler_params=_params("parallel"))(a, b)


def _gdn_forward(p_pad, conv_wt, alog_row, dt_row, norm_row):
    c_pre, y = _gdn_conv_fwd(p_pad, conv_wt)
    gb = _gdn_gates_fwd(p_pad, alog_row, dt_row)
    o_f, o_r, s_f, s_r, t_f, t_r = _delta_fwd2(y, gb)
    out, o_sum = _gdn_post_fwd(o_f, o_r, p_pad, norm_row)
    return out, (c_pre, y, gb, s_f, t_f, s_r, t_r, o_sum)


def _gdn_backward(d_out, p_pad, conv_wt, alog_row, dt_row, norm_row, saved):
    c_pre, y, gb, s_f, t_f, s_r, t_r, o_sum = saved
    do, dz, dnorm = _gdn_post_bwd(d_out, o_sum, p_pad, norm_row)
    dy_f, dy_r, dgb_f, dgb_r = _delta_bwd2(y, gb, do, s_f, s_r, t_f, t_r)
    dp_qkv, dconv = _gdn_conv_bwd(dy_f, dy_r, c_pre, p_pad, conv_wt)
    dp_ab, gate_sums = _gdn_gates_bwd(dgb_f, dgb_r, p_pad, gb, alog_row, dt_row)
    return dp_qkv, dz, dp_ab, dconv, gate_sums, dnorm


ATT_BK = ATT_BQ + 2 * ATT_HALO
SWA_SCALE = SWA_HEAD_DIM ** -0.5


def _t5_bucket(rel):
    nb = REL_BUCKETS // 2
    bucket = (rel > 0).astype(np.int32) * nb
    n = np.abs(rel)
    max_exact = nb // 2
    large = max_exact + (np.log(np.maximum(n, 1) / max_exact)
                         / math.log(REL_MAX_DISTANCE / max_exact) * (nb - max_exact)).astype(np.int32)
    large = np.minimum(large, nb - 1)
    return (bucket + np.where(n < max_exact, n, large)).astype(np.int32)


def _band_tables(dilation, queries_are_rows_of_block):
    blk = np.arange(ATT_BQ)
    band = np.arange(ATT_BK) - ATT_HALO
    if queries_are_rows_of_block:
        rel = band[None, :] - blk[:, None]
        band_idx = np.broadcast_to(np.arange(ATT_BK)[None, :], rel.shape)
    else:
        rel = blk[None, :] - band[:, None]
        band_idx = np.broadcast_to(np.arange(ATT_BK)[:, None], rel.shape)
    base = np.abs(rel) <= ATT_HALO
    not_prev = band_idx >= ATT_HALO
    not_next = band_idx < ATT_HALO + ATT_BQ
    valid = np.stack([base & not_prev, base, base & not_next, base & not_prev & not_next])
    return valid, _t5_bucket(rel * dilation)


def _bias_tiles(rel_bias, dilation, queries_are_rows_of_block):
    valid, bucket = _band_tables(dilation, queries_are_rows_of_block)
    onehot = (jnp.asarray(bucket.reshape(-1, 1)) == jnp.arange(REL_BUCKETS, dtype=jnp.int32)[None, :]).astype(F32)
    rb = jnp.dot(onehot, rel_bias.astype(F32), precision=lax.Precision.HIGHEST)
    rb = rb.T.reshape((SWA_HEADS,) + bucket.shape)
    return jnp.where(valid[:, None], rb[None], NEG_BIG).astype(F32)


def _group_sum(x, bd):
    hi = x.astype(BF16)
    lo = (x - hi.astype(F32)).astype(BF16)
    return jnp.dot(hi, bd, preferred_element_type=F32) + jnp.dot(lo, bd, preferred_element_type=F32)


def _head_block_diag():
    idx = np.arange(SWA_WIDTH) // SWA_HEAD_DIM
    return jnp.asarray(idx[:, None] == idx[None, :], BF16)


def _swa_pre_fwd(p_pad, qw_row, kw_row, bd):
    s = p_pad.shape[0]
    tm = min(512, s)
    inv = 1.0 / SWA_HEAD_DIM

    def body(q_ref, k_ref, v_ref, qw_ref, kw_ref, bd_ref, qo_ref, ko_ref, vo_ref):
        bdv = bd_ref[...]
        q = q_ref[...]
        k = k_ref[...]
        rq = lax.rsqrt(_group_sum(q * q, bdv) * inv + EPS)
        rk = lax.rsqrt(_group_sum(k * k, bdv) * inv + EPS)
        qo_ref[...] = (q * rq * qw_ref[...] * SWA_SCALE).astype(BF16)
        ko_ref[...] = (k * rk * kw_ref[...]).astype(BF16)
        vo_ref[...] = v_ref[...].astype(BF16)

    base = OFF_B // SWA_WIDTH
    blk = pl.BlockSpec((tm, SWA_WIDTH), lambda i: (i, 0))
    vec = pl.BlockSpec((1, SWA_WIDTH), lambda i: (0, 0))
    return pl.pallas_call(
        body, name="swa_pre_fwd", grid=(s // tm,),
        in_specs=[pl.BlockSpec((tm, SWA_WIDTH), lambda i: (i, base)), pl.BlockSpec((tm, SWA_WIDTH), lambda i: (i, base + 1)),
                  pl.BlockSpec((tm, SWA_WIDTH), lambda i: (i, base + 2)), vec, vec,
                  pl.BlockSpec((SWA_WIDTH, SWA_WIDTH), lambda i: (0, 0))],
        out_specs=[blk, blk, blk],
        out_shape=[jax.ShapeDtypeStruct((s, SWA_WIDTH), BF16)] * 3,
        compiler_params=_params("parallel"),
    )(p_pad, p_pad, p_pad, qw_row, kw_row, bd)


def _swa_pre_bwd(dqs, dks, dvs, p_pad, qw_row, kw_row, bd):
    s = p_pad.shape[0]
    tm = min(256, s)
    inv = 1.0 / SWA_HEAD_DIM
    npat = len(dqs)

    def body(*refs):
        dq_refs, dk_refs, dv_refs = refs[:npat], refs[npat:2 * npat], refs[2 * npat:3 * npat]
        q_ref, k_ref, qw_ref, kw_ref, bd_ref, dp_ref, dqw_ref, dkw_ref = refs[3 * npat:]

        @pl.when(pl.program_id(0) == 0)
        def _():
            dqw_ref[...] = jnp.zeros_like(dqw_ref)
            dkw_ref[...] = jnp.zeros_like(dkw_ref)

        bdv = bd_ref[...]

        def norm_bwd(x, g, w, scale):
            r = lax.rsqrt(_group_sum(x * x, bdv) * inv + EPS)
            xhat = x * r
            t = g * w * scale
            dx = r * (t - xhat * (_group_sum(t * xhat, bdv) * inv))
            return dx, jnp.sum(g * scale * xhat, axis=0, keepdims=True)

        def total(rs):
            t = rs[0][...]
            for r in rs[1:]:
                t = t + r[...]
            return t

        dq, dqw = norm_bwd(q_ref[...], total(dq_refs), qw_ref[...], SWA_SCALE)
        dk, dkw = norm_bwd(k_ref[...], total(dk_refs), kw_ref[...], 1.0)
        dp_ref[:, 0:SWA_WIDTH] = dq
        dp_ref[:, SWA_WIDTH:2 * SWA_WIDTH] = dk
        dp_ref[:, 2 * SWA_WIDTH:3 * SWA_WIDTH] = total(dv_refs)
        dqw_ref[...] += dqw
        dkw_ref[...] += dkw

    base = OFF_B // SWA_WIDTH
    blk = pl.BlockSpec((tm, SWA_WIDTH), lambda i: (i, 0))
    vec = pl.BlockSpec((1, SWA_WIDTH), lambda i: (0, 0))
    return pl.pallas_call(
        body, name="swa_pre_bwd", grid=(s // tm,),
        in_specs=[blk] * (3 * npat) + [pl.BlockSpec((tm, SWA_WIDTH), lambda i: (i, base)),
                                      pl.BlockSpec((tm, SWA_WIDTH), lambda i: (i, base + 1)), vec, vec,
                                      pl.BlockSpec((SWA_WIDTH, SWA_WIDTH), lambda i: (0, 0))],
        out_specs=[pl.BlockSpec((tm, 3 * SWA_WIDTH), lambda i: (i, 0)), vec, vec],
        out_shape=[jax.ShapeDtypeStruct((s, 3 * SWA_WIDTH), F32), jax.ShapeDtypeStruct((1, SWA_WIDTH), F32),
                   jax.ShapeDtypeStruct((1, SWA_WIDTH), F32)],
        compiler_params=_params("arbitrary"),
    )(*dqs, *dks, *dvs, p_pad, p_pad, qw_row, kw_row, bd)


def _band_specs(length):
    per = ATT_BQ // ATT_HALO
    last = length // ATT_HALO - 1
    prev = pl.BlockSpec((ATT_HALO, SWA_WIDTH), lambda r, t: (jnp.maximum(t * per - 1, 0), r))
    cur = pl.BlockSpec((ATT_BQ, SWA_WIDTH), lambda r, t: (t, r))
    nxt = pl.BlockSpec((ATT_HALO, SWA_WIDTH), lambda r, t: (jnp.minimum((t + 1) * per, last), r))
    return [prev, cur, nxt]


def _tile_variant(t, nb):
    if nb == 1:
        return 3
    return jnp.where(t == 0, 0, jnp.where(t == nb - 1, 2, 1))


def _band(refs):
    return jnp.concatenate([r[...] for r in refs], axis=0)


def _att_fwd(q, k, v, bias, dilation):
    s = q.shape[0]
    length = s // dilation
    nb = length // ATT_BQ
    view = (length, dilation * SWA_WIDTH)
    hd = SWA_HEAD_DIM

    def body(q_ref, kp, kc, kn, vp, vc, vn, b_ref, o_ref, lse_ref):
        kb, vb = _band((kp, kc, kn)), _band((vp, vc, vn))
        qv = q_ref[...]
        for h in range(SWA_HEADS):
            sl = slice(h * hd, (h + 1) * hd)
            sc = _dot(qv[:, sl], kb[:, sl], NT) + b_ref[0, h]
            m = jnp.max(sc, axis=-1, keepdims=True)
            p = jnp.exp(sc - m)
            den = jnp.sum(p, axis=-1, keepdims=True)
            o_ref[:, sl] = _dot(p, vb[:, sl]) / den
            lse_ref[:, sl] = jnp.broadcast_to(m + jnp.log(den), (ATT_BQ, hd))

    cur = pl.BlockSpec((ATT_BQ, SWA_WIDTH), lambda r, t: (t, r))
    bspec = pl.BlockSpec((1, SWA_HEADS, ATT_BQ, ATT_BK), lambda r, t: (_tile_variant(t, nb), 0, 0, 0))
    o, lse = pl.pallas_call(
        body, name=f"att_fwd_d{dilation}", grid=(dilation, nb),
        in_specs=[cur] + _band_specs(length) * 2 + [bspec],
        out_specs=[cur, cur],
        out_shape=[jax.ShapeDtypeStruct(view, F32)] * 2,
        compiler_params=_params("parallel", "parallel"),
    )(q.reshape(view), *([k.reshape(view)] * 3), *([v.reshape(view)] * 3), bias)
    return o.reshape(s, SWA_WIDTH), lse.reshape(s, SWA_WIDTH)


def _att_dq(q, k, v, dop, lse, cp, bias, dilation):
    s = q.shape[0]
    length = s // dilation
    nb = length // ATT_BQ
    view = (length, dilation * SWA_WIDTH)
    hd = SWA_HEAD_DIM

    def body(q_ref, kp, kc, kn, vp, vc, vn, do_ref, lse_ref, cp_ref, b_ref, dq_ref, db_ref):
        @pl.when((pl.program_id(0) == 0) & (pl.program_id(1) == 0))
        def _():
            db_ref[...] = jnp.zeros_like(db_ref)

        var = _tile_variant(pl.program_id(1), nb)
        kb, vb = _band((kp, kc, kn)), _band((vp, vc, vn))
        qv, dov, lsev, cpv = q_ref[...], do_ref[...], lse_ref[...], cp_ref[...]
        for h in range(SWA_HEADS):
            sl = slice(h * hd, (h + 1) * hd)
            sc = _dot(qv[:, sl], kb[:, sl], NT) + b_ref[0, h]
            p = jnp.exp(sc - lsev[:, h * hd:h * hd + 1])
            dp = _dot(dov[:, sl], vb[:, sl], NT)
            ds = p * (dp + cpv[:, h * hd:h * hd + 1])
            dq_ref[:, sl] = _dot(ds, kb[:, sl])
            db_ref[var, h] += ds

    cur = pl.BlockSpec((ATT_BQ, SWA_WIDTH), lambda r, t: (t, r))
    bspec = pl.BlockSpec((1, SWA_HEADS, ATT_BQ, ATT_BK), lambda r, t: (_tile_variant(t, nb), 0, 0, 0))
    dq, db = pl.pallas_call(
        body, name=f"att_dq_d{dilation}", grid=(dilation, nb),
        in_specs=[cur] + _band_specs(length) * 2 + [cur, cur, cur, bspec],
        out_specs=[cur, pl.BlockSpec((4, SWA_HEADS, ATT_BQ, ATT_BK), lambda r, t: (0, 0, 0, 0))],
        out_shape=[jax.ShapeDtypeStruct(view, F32), jax.ShapeDtypeStruct((4, SWA_HEADS, ATT_BQ, ATT_BK), F32)],
        compiler_params=_params("arbitrary", "arbitrary"),
    )(q.reshape(view), *([k.reshape(view)] * 3), *([v.reshape(view)] * 3), dop.reshape(view), lse.reshape(view),
      cp.reshape(view), bias)
    return dq.reshape(s, SWA_WIDTH), db


def _att_dkv(q, k, v, dop, lse, cp, bias_t, dilation):
    s = q.shape[0]
    length = s // dilation
    nb = length // ATT_BQ
    view = (length, dilation * SWA_WIDTH)
    hd = SWA_HEAD_DIM

    def body(k_ref, v_ref, qp, qc, qn, dp_, dc_, dn_, lp, lc, ln, cp_, cc_, cn_, b_ref, dk_ref, dv_ref):
        qb, dob = _band((qp, qc, qn)), _band((dp_, dc_, dn_))
        lseb, cpb = _band((lp, lc, ln)), _band((cp_, cc_, cn_))
        kv, vv = k_ref[...], v_ref[...]
        for h in range(SWA_HEADS):
            sl = slice(h * hd, (h + 1) * hd)
            sc = _dot(qb[:, sl], kv[:, sl], NT) + b_ref[0, h]
            p = jnp.exp(sc - lseb[:, h * hd:h * hd + 1])
            dv_ref[:, sl] = _dot(p, dob[:, sl], TN)
            dp = _dot(dob[:, sl], vv[:, sl], NT)
            ds = p * (dp + cpb[:, h * hd:h * hd + 1])
            dk_ref[:, sl] = _dot(ds, qb[:, sl], TN)

    cur = pl.BlockSpec((ATT_BQ, SWA_WIDTH), lambda r, t: (t, r))
    bspec = pl.BlockSpec((1, SWA_HEADS, ATT_BK, ATT_BQ), lambda r, t: (_tile_variant(t, nb), 0, 0, 0))
    dk, dv = pl.pallas_call(
        body, name=f"att_dkv_d{dilation}", grid=(dilation, nb),
        in_specs=[cur, cur] + _band_specs(length) * 4 + [bspec],
        out_specs=[cur, cur],
        out_shape=[jax.ShapeDtypeStruct(view, F32)] * 2,
        compiler_params=_params("parallel", "parallel"),
    )(k.reshape(view), v.reshape(view), *([q.reshape(view)] * 3), *([dop.reshape(view)] * 3),
      *([lse.reshape(view)] * 3), *([cp.reshape(view)] * 3), bias_t)
    return dk.reshape(s, SWA_WIDTH), dv.reshape(s, SWA_WIDTH)


def _pattern_weights(lses):
    m = lses[0]
    for l in lses[1:]:
        m = jnp.maximum(m, l)
    es = [jnp.exp(l - m) for l in lses]
    den = es[0]
    for e in es[1:]:
        den = den + e
    return [e / den for e in es]


def _combine_fwd(outs, lses):
    s = outs[0].shape[0]
    tm = min(512, s)
    npat = len(outs)

    def body(*refs):
        ws = _pattern_weights([r[...] for r in refs[npat:2 * npat]])
        o = ws[0] * refs[0][...]
        for p in range(1, npat):
            o = o + ws[p] * refs[p][...]
        refs[2 * npat][...] = o.astype(BF16)

    blk = pl.BlockSpec((tm, SWA_WIDTH), lambda i: (i, 0))
    return pl.pallas_call(
        body, name="swa_combine_fwd", grid=(s // tm,), in_specs=[blk] * (2 * npat), out_specs=blk,
        out_shape=jax.ShapeDtypeStruct((s, SWA_WIDTH), BF16), compiler_params=_params("parallel"),
    )(*outs, *lses)


def _combine_bwd(d_out, outs, lses, bd):
    s = d_out.shape[0]
    tm = min(512, s)
    npat = len(outs)

    def body(*refs):
        d_ref, bd_ref = refs[0], refs[1 + 2 * npat]
        o_refs, l_refs = refs[1:1 + npat], refs[1 + npat:1 + 2 * npat]
        out_refs = refs[2 + 2 * npat:]
        ws = _pattern_weights([r[...] for r in l_refs])
        dov = d_ref[...]
        o = ws[0] * o_refs[0][...]
        for p in range(1, npat):
            o = o + ws[p] * o_refs[p][...]
        rd = _group_sum(dov * o, bd_ref[...])
        for p in range(npat):
            out_refs[p][...] = (ws[p] * dov).astype(BF16)
            out_refs[npat + p][...] = -ws[p] * rd

    blk = pl.BlockSpec((tm, SWA_WIDTH), lambda i: (i, 0))
    res = pl.pallas_call(
        body, name="swa_combine_bwd", grid=(s // tm,),
        in_specs=[blk] * (1 + 2 * npat) + [pl.BlockSpec((SWA_WIDTH, SWA_WIDTH), lambda i: (0, 0))],
        out_specs=[blk] * (2 * npat),
        out_shape=[jax.ShapeDtypeStruct((s, SWA_WIDTH), BF16)] * npat + [jax.ShapeDtypeStruct((s, SWA_WIDTH), F32)] * npat,
        compiler_params=_params("parallel"),
    )(d_out, *outs, *lses, bd)
    return res[:npat], res[npat:]


def _rel_bias_grad(dbs, buckets):
    npat = len(dbs)

    def body(*refs):
        db_refs, bk_refs, o_ref = refs[:npat], refs[npat:2 * npat], refs[2 * npat]
        row = lax.broadcasted_iota(jnp.int32, (REL_BUCKETS, LANE), 0)
        lane = lax.broadcasted_iota(jnp.int32, (REL_BUCKETS, LANE), 1)
        tiles = [[db_refs[p][0, h] + db_refs[p][1, h] + db_refs[p][2, h] + db_refs[p][3, h] for h in range(SWA_HEADS)]
                 for p in range(npat)]
        bks = [r[...] for r in bk_refs]

        def one_bucket(b, acc):
            for h in range(SWA_HEADS):
                tot = jnp.zeros((1, 1), F32)
                for p in range(npat):
                    sel = jnp.where(bks[p] == b, tiles[p][h], 0.0)
                    tot = tot + jnp.sum(jnp.sum(sel, axis=1, keepdims=True), axis=0, keepdims=True)
                acc = acc + jnp.where((row == b) & (lane == h), tot, 0.0)
            return acc

        o_ref[...] = lax.fori_loop(0, REL_BUCKETS, one_bucket, jnp.zeros((REL_BUCKETS, LANE), F32))

    full4 = pl.BlockSpec((4, SWA_HEADS, ATT_BQ, ATT_BK), lambda: (0, 0, 0, 0))
    full2 = pl.BlockSpec((ATT_BQ, ATT_BK), lambda: (0, 0))
    return pl.pallas_call(
        body, name="rel_bias_grad", in_specs=[full4] * npat + [full2] * npat,
        out_specs=pl.BlockSpec((REL_BUCKETS, LANE), lambda: (0, 0)),
        out_shape=jax.ShapeDtypeStruct((REL_BUCKETS, LANE), F32),
        compiler_params=pltpu.CompilerParams(vmem_limit_bytes=V7X_VMEM_LIMIT_BYTES),
    )(*dbs, *buckets)


def _swa_forward(p_pad, qw_row, kw_row, rel_bias, bd):
    q, k, v = _swa_pre_fwd(p_pad, qw_row, kw_row, bd)
    outs, lses = [], []
    for _, dil in DILATION_PATTERNS:
        o, lse = _att_fwd(q, k, v, _bias_tiles(rel_bias, dil, True), dil)
        outs.append(o)
        lses.append(lse)
    return _combine_fwd(outs, lses), (q, k, v, outs, lses)


def _swa_backward(d_out, p_pad, qw_row, kw_row, rel_bias, bd, saved):
    q, k, v, outs, lses = saved
    dops, cps = _combine_bwd(d_out, outs, lses, bd)
    dqs, dks, dvs, dbs, buckets = [], [], [], [], []
    for p, (_, dil) in enumerate(DILATION_PATTERNS):
        dq, db = _att_dq(q, k, v, dops[p], lses[p], cps[p], _bias_tiles(rel_bias, dil, True), dil)
        dk, dv = _att_dkv(q, k, v, dops[p], lses[p], cps[p], _bias_tiles(rel_bias, dil, False), dil)
        dqs.append(dq)
        dks.append(dk)
        dvs.append(dv)
        dbs.append(db)
        buckets.append(jnp.asarray(_band_tables(dil, True)[1]))
    dp, dqw, dkw = _swa_pre_bwd(dqs, dks, dvs, p_pad, qw_row, kw_row, bd)
    return dp, dqw, dkw, _rel_bias_grad(dbs, buckets)


def _lane_row(v):
    flat = v.reshape(-1).astype(F32)
    return jnp.zeros((1, LANE), F32).at[0, :flat.shape[0]].set(flat)


W_IN_SHARD = N_IN // N_DEV
W_IN_RUNS = ((0, QKV_A, 0), (QKV_A, OFF_B, QKV_A), (OFF_B, OFF_B + 16, OFF_AB), (OFF_B + 16, N_IN, OFF_B))
W_IN_SEGMENTS = ((0, QKV_A), (OFF_Z, GDN_WIDTH), (OFF_B, 3 * SWA_WIDTH), (OFF_AB, LANE))


def _w_in_pieces(shard):
    lo, hi = shard * W_IN_SHARD, (shard + 1) * W_IN_SHARD
    out = []
    for first, last, dst in W_IN_RUNS:
        a, b = max(lo, first), min(hi, last)
        if a < b:
            out.append((a - lo, b - a, dst + a - first))
    return out


def _cols_from_slabs(w3, name):
    nd, r, wd = w3.shape
    half = nd // 2

    def body(w_ref, o_ref):
        for sh in range(half):
            o_ref[:, wd * sh:wd * (sh + 1)] = w_ref[sh]

    return pl.pallas_call(
        body, name=name, grid=(2,), in_specs=[pl.BlockSpec((half, r, wd), lambda j: (j, 0, 0))],
        out_specs=pl.BlockSpec((r, half * wd), lambda j: (0, j)),
        out_shape=jax.ShapeDtypeStruct((r, nd * wd), w3.dtype), compiler_params=_params("parallel"),
    )(w3)


def _w_in_from_slabs(w3):
    nd, r, _ = w3.shape

    def body(w_ref, o_ref):
        o_ref[:, OFF_AB:N_PAD] = jnp.zeros((r, N_PAD - OFF_AB), w3.dtype)
        for sh in range(nd):
            for src, length, dst in _w_in_pieces(sh):
                o_ref[:, dst:dst + length] = w_ref[sh, :, src:src + length]

    return pl.pallas_call(
        body, name="w_in_from_slabs", out_shape=jax.ShapeDtypeStruct((r, N_PAD), w3.dtype),
        compiler_params=pltpu.CompilerParams(vmem_limit_bytes=V7X_VMEM_LIMIT_BYTES),
    )(w3)


def _w_in_grad_slabs(parts):
    r = parts[0].shape[0]

    def body(*refs):
        o_ref = refs[len(parts)]
        for sh in range(N_DEV):
            for src, length, dst in _w_in_pieces(sh):
                seg = next(i for i, (off, width) in enumerate(W_IN_SEGMENTS) if off <= dst < off + width)
                at = dst - W_IN_SEGMENTS[seg][0]
                o_ref[sh, :, src:src + length] = refs[seg][:, at:at + length]

    return pl.pallas_call(
        body, name="w_in_grad_slabs", out_shape=jax.ShapeDtypeStruct((N_DEV, r, W_IN_SHARD), F32),
        compiler_params=pltpu.CompilerParams(vmem_limit_bytes=V7X_VMEM_LIMIT_BYTES),
    )(*parts)


def _local_step(x, tgt, wts, small):
    bd = _head_block_diag()
    conv_wt = jnp.zeros((8, QKV_A), F32).at[:CONV_WIDTH].set(small["conv_w"].T)
    alog_row, dt_row = _lane_row(small["a_log"]), _lane_row(small["dt_bias"])
    gnorm_row = small["gdn_norm_w"].reshape(1, GDN_HEAD_DIM)
    qw_row = jnp.tile(small["q_norm_w"].reshape(-1), SWA_HEADS).reshape(1, SWA_WIDTH)
    kw_row = jnp.tile(small["k_norm_w"].reshape(-1), SWA_HEADS).reshape(1, SWA_WIDTH)
    rel_bias = small["rel_bias"]
    win_pad = wts["w_in_pad"]
    wo_a, wo_b = wts["w_out"][:GDN_WIDTH], wts["w_out"][GDN_WIDTH:]

    x1, sv1 = _ffn_forward(x, small["ffn1_norm"], wts["ffn1_w_gate"], wts["ffn1_w_up"], wts["ffn1_w_down"], "ffn1")
    n2, r2 = _rms_fwd(x1, small["mix_norm"], "mix_norm")
    p_pad = _matmul([(n2, win_pad)], tm=256, tn=N_PAD, tk=D_MODEL, name="w_in")
    o_a, sva = _gdn_forward(p_pad, conv_wt, alog_row, dt_row, gnorm_row)
    o_b, svb = _swa_forward(p_pad, qw_row, kw_row, rel_bias, bd)
    x2 = _matmul([(o_a, wo_a), (o_b, wo_b)], tm=512, tn=D_MODEL, tk=GDN_WIDTH, name="w_out", res=x1)
    x3, sv2 = _ffn_forward(x2, small["ffn2_norm"], wts["ffn2_w_gate"], wts["ffn2_w_up"], wts["ffn2_w_down"], "ffn2")
    loss_row, dx3, d_final = _final_loss(x3, small["final_norm"], tgt)

    dx2, d_ffn2_norm, dwg2, dwu2, dwd2 = _ffn_backward(
        dx3, x2, small["ffn2_norm"], wts["ffn2_w_gate"], wts["ffn2_w_up"], wts["ffn2_w_down"], sv2, "ffn2")
    d_oa = _matmul([(dx2, wo_a)], tb=True, tm=512, tn=GDN_WIDTH, tk=D_MODEL, name="w_out_da")
    d_ob = _matmul([(dx2, wo_b)], tb=True, tm=512, tn=SWA_WIDTH, tk=D_MODEL, name="w_out_db")
    dwo_a = _matmul([(o_a, dx2)], ta=True, tm=GDN_WIDTH, tn=D_MODEL, tk=512, name="w_out_dwa")
    dwo_b = _matmul([(o_b, dx2)], ta=True, tm=SWA_WIDTH, tn=D_MODEL, tk=512, name="w_out_dwb")
    dp_qkv, dz, dp_ab, dconv, gate_sums, d_gnorm = _gdn_backward(d_oa, p_pad, conv_wt, alog_row, dt_row, gnorm_row, sva)
    dp_b, dqw, dkw, d_rel = _swa_backward(d_ob, p_pad, qw_row, kw_row, rel_bias, bd, svb)
    segs = [(dp_qkv, 0, QKV_A), (dz, OFF_Z, GDN_WIDTH), (dp_b, OFF_B, 3 * SWA_WIDTH), (dp_ab, OFF_AB, LANE)]
    dn2 = None
    dwin_parts = []
    for i, (dseg, off, width) in enumerate(segs):
        dwin_parts.append(_matmul([(n2, dseg)], ta=True, tm=D_MODEL, tn=width, tk=512, name=f"w_in_dw{i}"))
        dn2 = _matmul([(dseg, win_pad[:, off:off + width])], tb=True, tm=512, tn=D_MODEL, tk=width,
                      name=f"w_in_dn{i}", res=dn2)
    dx1, d_mix_norm = _rms_bwd(dn2, x1, r2, small["mix_norm"], dx2, "mix_dnorm")
    dx, d_ffn1_norm, dwg1, dwu1, dwd1 = _ffn_backward(
        dx1, x, small["ffn1_norm"], wts["ffn1_w_gate"], wts["ffn1_w_up"], wts["ffn1_w_down"], sv1, "ffn1")

    def row_slabs(full):
        return full.reshape(N_DEV, full.shape[0] // N_DEV, full.shape[1])

    dwd1, dwd2 = row_slabs(dwd1), row_slabs(dwd2)
    grads = {
        "ffn1_norm": d_ffn1_norm, "ffn1_w_gate": dwg1, "ffn1_w_up": dwu1, "ffn1_w_down": dwd1,
        "mix_norm": d_mix_norm, "w_in": _w_in_grad_slabs(dwin_parts), "conv_w": dconv[:CONV_WIDTH].T,
        "a_log": gate_sums[0, :8].reshape(2, GDN_HEADS), "dt_bias": gate_sums[1, :8].reshape(2, GDN_HEADS),
        "gdn_norm_w": d_gnorm, "q_norm_w": dqw.reshape(SWA_HEADS, SWA_HEAD_DIM).sum(0, keepdims=True),
        "k_norm_w": dkw.reshape(SWA_HEADS, SWA_HEAD_DIM).sum(0, keepdims=True), "rel_bias": d_rel[:, :SWA_HEADS],
        "w_out": row_slabs(jnp.concatenate([dwo_a, dwo_b], axis=0)), "ffn2_norm": d_ffn2_norm,
        "ffn2_w_gate": dwg2, "ffn2_w_up": dwu2, "ffn2_w_down": dwd2, "final_norm": d_final,
    }
    return loss_row, dx, grads


MESH_IDS = pl.DeviceIdType.MESH
ANY = pl.BlockSpec(memory_space=pl.ANY)


def _all_gather(v, name):
    m, n = v.shape

    def body(x_ref, out_ref, send_sems, recv_sems, local_sem):
        x, y, c = lax.axis_index("x"), lax.axis_index("y"), lax.axis_index("c")
        me, sibling = (x, y, c), (x, y, 1 - c)
        chips = [(1 - x, y), (x, 1 - y), (1 - x, 1 - y)]

        def rows(px, py, pc):
            return out_ref.at[pl.ds((4 * px + 2 * py + pc) * m, m), :]

        def copy(k, block, to, src=None):
            return pltpu.make_async_remote_copy(
                src_ref=rows(*block) if src is None else src, dst_ref=rows(*block),
                send_sem=send_sems.at[k], recv_sem=recv_sems.at[k], device_id=to, device_id_type=MESH_IDS)

        mine = pltpu.make_async_copy(x_ref, rows(*me), local_sem)
        mine.start()
        first = [copy(0, me, sibling, src=x_ref)]
        first += [copy(1 + j, me, (*chip, c), src=x_ref) for j, chip in enumerate(chips)]
        for cp in first:
            cp.start()
        passed = [copy(4 + j, (*chip, c), sibling) for j, chip in enumerate(chips)]
        for j, chip in enumerate(chips):
            copy(1 + j, (*chip, c), me).wait_recv()
            passed[j].start()
        copy(0, sibling, me).wait_recv()
        for j, chip in enumerate(chips):
            copy(4 + j, (*chip, 1 - c), me).wait_recv()
        for cp in first + passed:
            cp.wait_send()
        mine.wait()

    return pl.pallas_call(
        body, name=name, in_specs=[ANY], out_specs=ANY,
        out_shape=jax.ShapeDtypeStruct((N_DEV * m, n), v.dtype),
        scratch_shapes=[pltpu.SemaphoreType.DMA((7,)), pltpu.SemaphoreType.DMA((7,)), pltpu.SemaphoreType.DMA],
        compiler_params=pltpu.CompilerParams(vmem_limit_bytes=V7X_VMEM_LIMIT_BYTES),
    )(v)


def _sibling_swap(v, name):
    def body(v_ref, out_ref, send_sem, recv_sem):
        x, y, c = lax.axis_index("x"), lax.axis_index("y"), lax.axis_index("c")
        cp = pltpu.make_async_remote_copy(src_ref=v_ref, dst_ref=out_ref, send_sem=send_sem, recv_sem=recv_sem,
                                          device_id=(x, y, 1 - c), device_id_type=MESH_IDS)
        cp.start()
        cp.wait()

    return pl.pallas_call(
        body, name=name, in_specs=[ANY], out_specs=ANY, out_shape=jax.ShapeDtypeStruct(v.shape, v.dtype),
        scratch_shapes=[pltpu.SemaphoreType.DMA, pltpu.SemaphoreType.DMA],
        compiler_params=pltpu.CompilerParams(vmem_limit_bytes=V7X_VMEM_LIMIT_BYTES),
    )(v)


def _chip_exchange(t, name):
    def body(t_ref, out_ref, send_sems, recv_sems, local_sem):
        x, y, c = lax.axis_index("x"), lax.axis_index("y"), lax.axis_index("c")
        mine = 2 * x + y
        chips = [(1 - x, y), (x, 1 - y), (1 - x, 1 - y)]
        own = pltpu.make_async_copy(t_ref.at[mine], out_ref.at[mine], local_sem)
        own.start()
        copies = [pltpu.make_async_remote_copy(
            src_ref=t_ref.at[2 * px + py], dst_ref=out_ref.at[mine], send_sem=send_sems.at[j], recv_sem=recv_sems.at[j],
            device_id=(px, py, c), device_id_type=MESH_IDS) for j, (px, py) in enumerate(chips)]
        for cp in copies:
            cp.start()
        for j, (px, py) in enumerate(chips):
            pltpu.make_async_remote_copy(
                src_ref=t_ref.at[mine], dst_ref=out_ref.at[2 * px + py], send_sem=send_sems.at[j],
                recv_sem=recv_sems.at[j], device_id=(px, py, c), device_id_type=MESH_IDS).wait_recv()
        for cp in copies:
            cp.wait_send()
        own.wait()

    return pl.pallas_call(
        body, name=name, in_specs=[ANY], out_specs=ANY, out_shape=jax.ShapeDtypeStruct(t.shape, t.dtype),
        scratch_shapes=[pltpu.SemaphoreType.DMA((3,)), pltpu.SemaphoreType.DMA((3,)), pltpu.SemaphoreType.DMA],
        compiler_params=pltpu.CompilerParams(vmem_limit_bytes=V7X_VMEM_LIMIT_BYTES),
    )(t)


def _adamw(parts, w, m, v, name):
    nparts, r, n = parts.shape
    tr = r
    for cand in (256, 176, 128, 104, 64, 8):
        if r % cand == 0:
            tr = cand
            break
    bc1 = 1.0 - ADAM_B1 ** ADAM_STEP
    bc2 = 1.0 - ADAM_B2 ** ADAM_STEP

    def body(p_ref, w_ref, m_ref, v_ref, g_ref, d_ref, nm_ref, nv_ref):
        g = p_ref[0].astype(F32)
        for k in range(1, nparts):
            g = g + p_ref[k].astype(F32)
        mn = ADAM_B1 * m_ref[...] + (1.0 - ADAM_B1) * g
        vn = ADAM_B2 * v_ref[...] + (1.0 - ADAM_B2) * (g * g)
        m_hat = mn / bc1
        v_hat = vn / bc2
        g_ref[...] = g
        nm_ref[...] = mn
        nv_ref[...] = vn
        d_ref[...] = -ADAM_LR * (m_hat / (jnp.sqrt(v_hat) + ADAM_EPS) + ADAM_WD * w_ref[...])

    blk = pl.BlockSpec((tr, n), lambda i: (i, 0))
    return pl.pallas_call(
        body, name=name, grid=(r // tr,),
        in_specs=[pl.BlockSpec((nparts, tr, n), lambda i: (0, i, 0)), blk, blk, blk],
        out_specs=[blk] * 4, out_shape=[jax.ShapeDtypeStruct((r, n), F32)] * 4,
        compiler_params=_params("parallel"),
    )(parts, w, m, v)


def _mesh_place():
    x, y, c = lax.axis_index("x"), lax.axis_index("y"), lax.axis_index("c")
    return x, y, c, [(1 - x, y), (x, 1 - y), (1 - x, 1 - y)]


def _all_gather_many(vs, name):
    na = len(vs)

    def body(*refs):
        x_refs, out_refs = refs[:na], refs[na:2 * na]
        send_sems, recv_sems, local_sems = refs[2 * na:]
        x, y, c, chips = _mesh_place()
        me, sibling = (x, y, c), (x, y, 1 - c)

        def slab(i, px, py, pc):
            return out_refs[i].at[4 * px + 2 * py + pc]

        def copy(i, k, block, to, src=None):
            return pltpu.make_async_remote_copy(
                src_ref=slab(i, *block) if src is None else src, dst_ref=slab(i, *block),
                send_sem=send_sems.at[i, k], recv_sem=recv_sems.at[i, k], device_id=to, device_id_type=MESH_IDS)

        mine = [pltpu.make_async_copy(x_refs[i], slab(i, *me), local_sems.at[i]) for i in range(na)]
        first = []
        for i in range(na):
            mine[i].start()
            first.append(copy(i, 0, me, sibling, src=x_refs[i]))
            first += [copy(i, 1 + j, me, (*chip, c), src=x_refs[i]) for j, chip in enumerate(chips)]
        for cp in first:
            cp.start()
        passed = []
        for j, chip in enumerate(chips):
            for i in range(na):
                copy(i, 1 + j, (*chip, c), me).wait_recv()
                passed.append(copy(i, 4 + j, (*chip, c), sibling))
                passed[-1].start()
        for i in range(na):
            copy(i, 0, sibling, me).wait_recv()
        for j, chip in enumerate(chips):
            for i in range(na):
                copy(i, 4 + j, (*chip, 1 - c), me).wait_recv()
        for cp in first + passed:
            cp.wait_send()
        for cp in mine:
            cp.wait()

    return pl.pallas_call(
        body, name=name, in_specs=[ANY] * na, out_specs=[ANY] * na,
        out_shape=[jax.ShapeDtypeStruct((N_DEV,) + v.shape, v.dtype) for v in vs],
        scratch_shapes=[pltpu.SemaphoreType.DMA((na, 7)), pltpu.SemaphoreType.DMA((na, 7)), pltpu.SemaphoreType.DMA((na,))],
        compiler_params=pltpu.CompilerParams(vmem_limit_bytes=V7X_VMEM_LIMIT_BYTES),
    )(*vs)


def _sibling_swap_many(gs, name):
    na = len(gs)

    def body(*refs):
        g_refs, out_refs = refs[:na], refs[na:2 * na]
        send_sems, recv_sems = refs[2 * na:]
        x, y, c, _ = _mesh_place()
        copies = [pltpu.make_async_remote_copy(
            src_ref=g_refs[i].at[2 * k + 1 - c], dst_ref=out_refs[i].at[k], send_sem=send_sems.at[i, k],
            recv_sem=recv_sems.at[i, k], device_id=(x, y, 1 - c), device_id_type=MESH_IDS)
            for i in range(na) for k in range(4)]
        for cp in copies:
            cp.start()
        for cp in copies:
            cp.wait()

    return pl.pallas_call(
        body, name=name, in_specs=[ANY] * na, out_specs=[ANY] * na,
        out_shape=[jax.ShapeDtypeStruct((4,) + g.shape[1:], g.dtype) for g in gs],
        scratch_shapes=[pltpu.SemaphoreType.DMA((na, 4)), pltpu.SemaphoreType.DMA((na, 4))],
        compiler_params=pltpu.CompilerParams(vmem_limit_bytes=V7X_VMEM_LIMIT_BYTES),
    )(*gs)


def _chip_sum(g, got, core, name):
    _, r, n = g.shape

    def body(c_ref, g_ref, got_ref, o_ref):
        o_ref[...] = (g_ref[...] + got_ref[...]).astype(BF16)

    return pl.pallas_call(
        body, name=name,
        grid_spec=pltpu.PrefetchScalarGridSpec(
            num_scalar_prefetch=1, grid=(4,),
            in_specs=[pl.BlockSpec((None, r, n), lambda k, c_ref: (2 * k + c_ref[0], 0, 0)),
                      pl.BlockSpec((None, r, n), lambda k, c_ref: (k, 0, 0))],
            out_specs=pl.BlockSpec((None, r, n), lambda k, c_ref: (k, 0, 0))),
        out_shape=jax.ShapeDtypeStruct((4, r, n), BF16), compiler_params=_params("parallel"),
    )(core, g, got)


def _chip_exchange_many(ts, name):
    na = len(ts)

    def body(*refs):
        t_refs, out_refs = refs[:na], refs[na:2 * na]
        send_sems, recv_sems, local_sems = refs[2 * na:]
        x, y, c, chips = _mesh_place()
        mine = 2 * x + y
        own = [pltpu.make_async_copy(t_refs[i].at[mine], out_refs[i].at[mine], local_sems.at[i]) for i in range(na)]
        for cp in own:
            cp.start()
        copies = [pltpu.make_async_remote_copy(
            src_ref=t_refs[i].at[2 * px + py], dst_ref=out_refs[i].at[mine], send_sem=send_sems.at[i, j],
            recv_sem=recv_sems.at[i, j], device_id=(px, py, c), device_id_type=MESH_IDS)
            for j, (px, py) in enumerate(chips) for i in range(na)]
        for cp in copies:
            cp.start()
        for j, (px, py) in enumerate(chips):
            for i in range(na):
                pltpu.make_async_remote_copy(
                    src_ref=t_refs[i].at[mine], dst_ref=out_refs[i].at[2 * px + py], send_sem=send_sems.at[i, j],
                    recv_sem=recv_sems.at[i, j], device_id=(px, py, c), device_id_type=MESH_IDS).wait_recv()
        for cp in copies:
            cp.wait_send()
        for cp in own:
            cp.wait()

    return pl.pallas_call(
        body, name=name, in_specs=[ANY] * na, out_specs=[ANY] * na,
        out_shape=[jax.ShapeDtypeStruct(t.shape, t.dtype) for t in ts],
        scratch_shapes=[pltpu.SemaphoreType.DMA((na, 3)), pltpu.SemaphoreType.DMA((na, 3)), pltpu.SemaphoreType.DMA((na,))],
        compiler_params=pltpu.CompilerParams(vmem_limit_bytes=V7X_VMEM_LIMIT_BYTES),
    )(*ts)


BIG = ("ffn1_w_gate", "ffn1_w_up", "ffn1_w_down", "w_in", "w_out", "ffn2_w_gate", "ffn2_w_up", "ffn2_w_down")
COL_SHARDED = ("ffn1_w_gate", "ffn1_w_up", "w_in", "ffn2_w_gate", "ffn2_w_up")
SMALL = ("ffn1_norm", "mix_norm", "a_log", "dt_bias", "gdn_norm_w", "q_norm_w", "k_norm_w", "rel_bias",
         "ffn2_norm", "final_norm")
WEIGHTS = ("ffn1_norm", "ffn1_w_gate", "ffn1_w_up", "ffn1_w_down", "mix_norm", "w_in", "conv_w", "a_log", "dt_bias",
           "gdn_norm_w", "q_norm_w", "k_norm_w", "rel_bias", "w_out", "ffn2_norm", "ffn2_w_gate", "ffn2_w_up",
           "ffn2_w_down", "final_norm")
PACK_WIDTH = 1024
PACK_ROW_MULTIPLE = 32


def _pack(arrays, width, row_multiple):
    flat = jnp.concatenate([a.reshape(-1) for a in arrays])
    rows = -(-flat.shape[0] // width)
    rows = -(-rows // row_multiple) * row_multiple
    return jnp.pad(flat, (0, rows * width - flat.shape[0])).reshape(rows, width)


def _unpack(packed, shapes):
    flat = packed.reshape(-1)
    out, pos = [], 0
    for shp in shapes:
        size = int(np.prod(shp))
        out.append(flat[pos:pos + size].reshape(shp))
        pos += size
    return out


def _blocks_of(name, full):
    if name in COL_SHARDED:
        rows, cols = full.shape
        return full.reshape(rows, N_DEV, cols // N_DEV).transpose(1, 0, 2).reshape(N_DEV, -1)
    return full.reshape(N_DEV, -1)


def _full_of(name, blocks, shard_shape):
    rows, cols = shard_shape
    if name in COL_SHARDED:
        return blocks.reshape(N_DEV, rows, cols).transpose(1, 0, 2).reshape(rows, N_DEV * cols)
    return blocks.reshape(N_DEV * rows, cols)


def kernel(x, ffn1_norm, ffn1_w_gate, ffn1_w_up, ffn1_w_down, mix_norm, w_in, conv_w, a_log, dt_bias, gdn_norm_w, q_norm_w, k_norm_w, rel_bias, w_out, ffn2_norm, ffn2_w_gate, ffn2_w_up, ffn2_w_down, final_norm, loss_target, m_ffn1_norm, m_ffn1_w_gate, m_ffn1_w_up, m_ffn1_w_down, m_mix_norm, m_w_in, m_conv_w, m_a_log, m_dt_bias, m_gdn_norm_w, m_q_norm_w, m_k_norm_w, m_rel_bias, m_w_out, m_ffn2_norm, m_ffn2_w_gate, m_ffn2_w_up, m_ffn2_w_down, m_final_norm, v_ffn1_norm, v_ffn1_w_gate, v_ffn1_w_up, v_ffn1_w_down, v_mix_norm, v_w_in, v_conv_w, v_a_log, v_dt_bias, v_gdn_norm_w, v_q_norm_w, v_k_norm_w, v_rel_bias, v_w_out, v_ffn2_norm, v_ffn2_w_gate, v_ffn2_w_up, v_ffn2_w_down, v_final_norm):
    w = dict(ffn1_norm=ffn1_norm, ffn1_w_gate=ffn1_w_gate, ffn1_w_up=ffn1_w_up, ffn1_w_down=ffn1_w_down, mix_norm=mix_norm, w_in=w_in, conv_w=conv_w, a_log=a_log, dt_bias=dt_bias, gdn_norm_w=gdn_norm_w, q_norm_w=q_norm_w, k_norm_w=k_norm_w, rel_bias=rel_bias, w_out=w_out, ffn2_norm=ffn2_norm, ffn2_w_gate=ffn2_w_gate, ffn2_w_up=ffn2_w_up, ffn2_w_down=ffn2_w_down, final_norm=final_norm)
    mom = dict(ffn1_norm=m_ffn1_norm, ffn1_w_gate=m_ffn1_w_gate, ffn1_w_up=m_ffn1_w_up, ffn1_w_down=m_ffn1_w_down, mix_norm=m_mix_norm, w_in=m_w_in, conv_w=m_conv_w, a_log=m_a_log, dt_bias=m_dt_bias, gdn_norm_w=m_gdn_norm_w, q_norm_w=m_q_norm_w, k_norm_w=m_k_norm_w, rel_bias=m_rel_bias, w_out=m_w_out, ffn2_norm=m_ffn2_norm, ffn2_w_gate=m_ffn2_w_gate, ffn2_w_up=m_ffn2_w_up, ffn2_w_down=m_ffn2_w_down, final_norm=m_final_norm)
    var = dict(ffn1_norm=v_ffn1_norm, ffn1_w_gate=v_ffn1_w_gate, ffn1_w_up=v_ffn1_w_up, ffn1_w_down=v_ffn1_w_down, mix_norm=v_mix_norm, w_in=v_w_in, conv_w=v_conv_w, a_log=v_a_log, dt_bias=v_dt_bias, gdn_norm_w=v_gdn_norm_w, q_norm_w=v_q_norm_w, k_norm_w=v_k_norm_w, rel_bias=v_rel_bias, w_out=v_w_out, ffn2_norm=v_ffn2_norm, ffn2_w_gate=v_ffn2_w_gate, ffn2_w_up=v_ffn2_w_up, ffn2_w_down=v_ffn2_w_down, final_norm=v_final_norm)
    ix, iy, ic = lax.axis_index("x"), lax.axis_index("y"), lax.axis_index("c")
    me = 4 * ix + 2 * iy + ic

    shard = {n: w[n][0] for n in BIG}

    conv_shard_shape = w["conv_w"][0].shape
    conv_elems = conv_shard_shape[0] * conv_shard_shape[1]
    gathered = _all_gather_many([shard[n].astype(BF16) for n in BIG] + [_pack([w["conv_w"][0]], LANE, 8)],
                                "gather_weights")
    slabs = dict(zip(BIG, gathered))
    wts = {}
    for n in BIG:
        if n == "w_in":
            wts["w_in_pad"] = _w_in_from_slabs(slabs[n])
        elif n in COL_SHARDED:
            wts[n] = _cols_from_slabs(slabs[n], f"{n}_cols")
        else:
            wts[n] = slabs[n].reshape(N_DEV * slabs[n].shape[1], slabs[n].shape[2])

    small = {n: w[n][0] if n not in ("rel_bias",) else w[n] for n in SMALL}
    small = {n: (a.reshape(1, -1) if n.endswith("norm") else a) for n, a in small.items()}
    conv_all = gathered[-1].reshape(N_DEV, -1)
    small["conv_w"] = conv_all[:, :conv_elems].reshape(N_DEV * conv_shard_shape[0], conv_shard_shape[1])
    loss_row, grad_x, grads = _local_step(x[0], loss_target[0], wts, small)
    loss = lax.psum(loss_row[0, 0], ("x", "y", "c"))

    gots = _sibling_swap_many([grads[n] for n in BIG], "grads_to_sibling")
    core = ic.astype(jnp.int32).reshape(1)
    sums = [_chip_sum(grads[n], got, core, f"{n}_chip_sum") for n, got in zip(BIG, gots)]
    parts = _chip_exchange_many(sums, "grads_to_chips")
    big_out = [[], [], [], []]
    for n, part in zip(BIG, parts):
        for kind, val in enumerate(_adamw(part, shard[n], mom[n][0], var[n][0], f"{n}_adamw")):
            big_out[kind].append(val)

    small_names = SMALL + ("conv_w",)
    small_shapes = [grads[n].shape for n in small_names]
    g_small = _pack([grads[n] for n in small_names], LANE, 8)
    small_rows = g_small.shape[0]
    all_small = _all_gather(g_small, "gather_small_grads").reshape(N_DEV, small_rows, LANE)
    rep_shapes = [grads[n].shape for n in SMALL]
    zero_conv = jnp.zeros(small_shapes[-1], F32)
    ws = _pack([w[n].reshape(grads[n].shape) for n in SMALL] + [zero_conv], LANE, 8)
    ms = _pack([mom[n].reshape(grads[n].shape) for n in SMALL] + [zero_conv], LANE, 8)
    vs = _pack([var[n].reshape(grads[n].shape) for n in SMALL] + [zero_conv], LANE, 8)
    small_out = [_unpack(a, small_shapes) for a in _adamw(all_small, ws, ms, vs, "adamw_small")]
    conv_g = lax.dynamic_slice_in_dim(small_out[0][-1], me * conv_shard_shape[0], conv_shard_shape[0], axis=0)
    conv_out = [_unpack(a, [conv_shard_shape])[0] for a in _adamw(
        _pack([conv_g], LANE, 8)[None], _pack([w["conv_w"][0]], LANE, 8), _pack([mom["conv_w"][0]], LANE, 8),
        _pack([var["conv_w"][0]], LANE, 8), "adamw_conv")]

    def leaf(kind, n):
        if n in BIG:
            val = big_out[kind][BIG.index(n)]
        elif n == "conv_w":
            val = conv_out[kind]
        else:
            val = small_out[kind][SMALL.index(n)]
        return val.reshape(w[n].shape)

    outs = [loss, grad_x[None]]
    for kind in range(4):
        outs += [leaf(kind, n) for n in WEIGHTS]
    return tuple(outs)
```

```python
import functools
import math

import numpy as np
import jax
import jax.numpy as jnp
from jax import lax
from jax.experimental import pallas as pl
from jax.experimental.pallas import tpu as pltpu

F32 = jnp.float32
BF16 = jnp.bfloat16

D_MODEL = 1024
D_FF = 2816
GDN_HEADS = 4
GDN_HEAD_DIM = 128
GDN_WIDTH = 512
CONV_WIDTH = 5
CHUNK = 64
SWA_HEADS = 8
SWA_HEAD_DIM = 64
SWA_WIDTH = 512
DILATION_PATTERNS = ((128, 1), (512, 4), (2048, 16))
REL_BUCKETS = 32
REL_MAX_DISTANCE = 1024
EPS = 1e-6
NEG_BIG = -1e30
N_DEV = 8

ADAM_LR = 0.001
ADAM_B1 = 0.9
ADAM_B2 = 0.999
ADAM_EPS = 1e-08
ADAM_WD = 0.01
ADAM_STEP = 10

QKV_A = 3 * GDN_WIDTH
OFF_Z = QKV_A
OFF_B = OFF_Z + GDN_WIDTH
OFF_AB = OFF_B + 3 * SWA_WIDTH
N_PAD = OFF_AB + 128
N_IN = 3600

V7X_VMEM_LIMIT_BYTES = 56 * 1024 * 1024
LANE = 128
ATT_BQ = 128
ATT_HALO = 64
CONV_ROWS = 256

NN = (((1,), (0,)), ((), ()))
NT = (((1,), (1,)), ((), ()))
TN = (((0,), (0,)), ((), ()))


def _params(*sem):
    return pltpu.CompilerParams(dimension_semantics=sem, vmem_limit_bytes=V7X_VMEM_LIMIT_BYTES)


def _dot(a, b, dn=NN):
    return lax.dot_general(a.astype(BF16), b.astype(BF16), dn, preferred_element_type=F32)


def _dot_hi(a, b, dn=NN):
    return lax.dot_general(a, b, dn, precision=lax.Precision.HIGHEST, preferred_element_type=F32)


def _sigmoid(x):
    return 1.0 / (1.0 + jnp.exp(-x))


def _matmul(pairs, *, ta=False, tb=False, out_dtype=F32, tm, tn, tk, name, res=None, alpha=None, shard_cols=None):
    a0, b0 = pairs[0]
    m = a0.shape[1] if ta else a0.shape[0]
    k = a0.shape[0] if ta else a0.shape[1]
    n = b0.shape[0] if tb else b0.shape[1]
    tm, tn, tk = min(tm, m), min(tn, n), min(tk, k)
    assert m % tm == 0 and n % tn == 0 and k % tk == 0, (name, m, n, k, tm, tn, tk)
    nk = k // tk
    npairs = len(pairs)
    dn = (((0 if ta else 1,), (1 if tb else 0,)), ((), ()))

    def body(*refs):
        ins = refs[:2 * npairs]
        pos = 2 * npairs
        r_ref = None
        if res is not None:
            r_ref = refs[pos]
            pos += 1
        o_ref, acc = refs[pos], refs[pos + 1]
        kk = pl.program_id(2)
        t = None
        for p in range(npairs):
            d = _dot(ins[2 * p][...], ins[2 * p + 1][...], dn)
            t = d if t is None else t + d

        if nk > 1:
            @pl.when(kk == 0)
            def _():
                acc[...] = t

            @pl.when((kk > 0) & (kk < nk - 1))
            def _():
                acc[...] += t

        @pl.when(kk == nk - 1)
        def _():
            r = acc[...] + t if nk > 1 else t
            if alpha is not None:
                r = r * alpha
            if r_ref is not None:
                r = r_ref[...] + r
            if shard_cols is None:
                o_ref[...] = r.astype(out_dtype)
            else:
                for sh in range(tn // shard_cols):
                    o_ref[sh] = r[:, sh * shard_cols:(sh + 1) * shard_cols].astype(out_dtype)

    a_spec = pl.BlockSpec((tk, tm), lambda i, j, kk: (kk, i)) if ta else pl.BlockSpec((tm, tk), lambda i, j, kk: (i, kk))
    b_spec = pl.BlockSpec((tn, tk), lambda i, j, kk: (j, kk)) if tb else pl.BlockSpec((tk, tn), lambda i, j, kk: (kk, j))
    o_spec = pl.BlockSpec((tm, tn), lambda i, j, kk: (i, j))
    in_specs = [a_spec, b_spec] * npairs + ([o_spec] if res is not None else [])
    args = [t for pr in pairs for t in pr] + ([res] if res is not None else [])
    out_spec, out_shape = o_spec, (m, n)
    if shard_cols is not None:
        assert res is None and tn % shard_cols == 0
        out_spec = pl.BlockSpec((tn // shard_cols, tm, shard_cols), lambda i, j, kk: (j, i, 0))
        out_shape = (n // shard_cols, m, shard_cols)
    return pl.pallas_call(
        body, name=name, grid=(m // tm, n // tn, nk), in_specs=in_specs, out_specs=out_spec,
        out_shape=jax.ShapeDtypeStruct(out_shape, out_dtype),
        scratch_shapes=[pltpu.VMEM((tm, tn) if nk > 1 else (8, LANE), F32)],
        compiler_params=_params("parallel", "parallel", "arbitrary"),
    )(*args)


def _rms_fwd(x, w, name):
    s, d = x.shape
    tm = min(512, s)

    def body(x_ref, w_ref, n_ref, r_ref):
        xv = x_ref[...]
        r = lax.rsqrt(jnp.mean(xv * xv, axis=-1, keepdims=True) + EPS)
        n_ref[...] = (xv * r * w_ref[...]).astype(BF16)
        r_ref[...] = r

    return pl.pallas_call(
        body, name=name, grid=(s // tm,),
        in_specs=[pl.BlockSpec((tm, d), lambda i: (i, 0)), pl.BlockSpec((1, d), lambda i: (0, 0))],
        out_specs=[pl.BlockSpec((tm, d), lambda i: (i, 0)), pl.BlockSpec((tm, 1), lambda i: (i, 0))],
        out_shape=[jax.ShapeDtypeStruct((s, d), BF16), jax.ShapeDtypeStruct((s, 1), F32)],
        compiler_params=_params("parallel"),
    )(x, w)


def _rms_bwd(dn, x, r, w, dres, name):
    s, d = x.shape
    tm = min(512, s)

    def body(dn_ref, x_ref, r_ref, w_ref, dres_ref, dx_ref, dw_ref):
        @pl.when(pl.program_id(0) == 0)
        def _():
            dw_ref[...] = jnp.zeros_like(dw_ref)

        rv = r_ref[...]
        xhat = x_ref[...] * rv
        g = dn_ref[...]
        t = g * w_ref[...]
        dx_ref[...] = dres_ref[...] + rv * (t - xhat * jnp.mean(t * xhat, axis=-1, keepdims=True))
        dw_ref[...] += jnp.sum(g * xhat, axis=0, keepdims=True)

    row = pl.BlockSpec((tm, d), lambda i: (i, 0))
    vec = pl.BlockSpec((1, d), lambda i: (0, 0))
    return pl.pallas_call(
        body, name=name, grid=(s // tm,),
        in_specs=[row, row, pl.BlockSpec((tm, 1), lambda i: (i, 0)), vec, row],
        out_specs=[row, vec],
        out_shape=[jax.ShapeDtypeStruct((s, d), F32), jax.ShapeDtypeStruct((1, d), F32)],
        compiler_params=_params("arbitrary"),
    )(dn, x, r, w, dres)


def _final_loss(x3, wf, tgt):
    s, d = x3.shape
    tm = min(512, s)

    def body(x_ref, w_ref, t_ref, loss_ref, dx_ref, dw_ref):
        @pl.when(pl.program_id(0) == 0)
        def _():
            dw_ref[...] = jnp.zeros_like(dw_ref)
            loss_ref[...] = jnp.zeros_like(loss_ref)

        xv = x_ref[...]
        wv = w_ref[...]
        r = lax.rsqrt(jnp.mean(xv * xv, axis=-1, keepdims=True) + EPS)
        xhat = xv * r
        e = xhat * wv - t_ref[...]
        part = 0.5 * jnp.sum(jnp.mean(e * e, axis=-1, keepdims=True), axis=0, keepdims=True)
        loss_ref[...] += jnp.broadcast_to(part, loss_ref.shape)
        dy = e * (1.0 / d)
        dw_ref[...] += jnp.sum(dy * xhat, axis=0, keepdims=True)
        t = dy * wv
        dx_ref[...] = r * (t - xhat * jnp.mean(t * xhat, axis=-1, keepdims=True))

    row = pl.BlockSpec((tm, d), lambda i: (i, 0))
    vec = pl.BlockSpec((1, d), lambda i: (0, 0))
    return pl.pallas_call(
        body, name="final_loss", grid=(s // tm,),
        in_specs=[row, vec, row],
        out_specs=[pl.BlockSpec((1, LANE), lambda i: (0, 0)), row, vec],
        out_shape=[jax.ShapeDtypeStruct((1, LANE), F32), jax.ShapeDtypeStruct((s, d), F32),
                   jax.ShapeDtypeStruct((1, d), F32)],
        compiler_params=_params("arbitrary"),
    )(x3, wf, tgt)


def _ffn_up(n, wg, wu, name):
    s, d = n.shape
    f = wg.shape[1]
    tm, tn = min(512, s), f // 2

    def body(n_ref, wg_ref, wu_ref, g_ref, u_ref, a_ref):
        nv = n_ref[...]
        g = _dot(nv, wg_ref[...])
        u = _dot(nv, wu_ref[...])
        g_ref[...] = g.astype(BF16)
        u_ref[...] = u.astype(BF16)
        a_ref[...] = (g * _sigmoid(g) * u).astype(BF16)

    o = pl.BlockSpec((tm, tn), lambda j, i: (i, j))
    wspec = pl.BlockSpec((d, tn), lambda j, i: (0, j))
    return pl.pallas_call(
        body, name=name, grid=(f // tn, s // tm),
        in_specs=[pl.BlockSpec((tm, d), lambda j, i: (i, 0)), wspec, wspec],
        out_specs=[o, o, o],
        out_shape=[jax.ShapeDtypeStruct((s, f), BF16)] * 3,
        compiler_params=_params("parallel", "parallel"),
    )(n, wg, wu)


def _ffn_dact(dx, wd, g, u, name):
    s, d = dx.shape
    f = wd.shape[0]
    tm, tn = min(512, s), f // 2

    def body(dx_ref, wd_ref, g_ref, u_ref, dg_ref, du_ref):
        da = 0.5 * _dot(dx_ref[...], wd_ref[...], NT)
        gv = g_ref[...].astype(F32)
        sg = _sigmoid(gv)
        du_ref[...] = (da * gv * sg).astype(BF16)
        dg_ref[...] = (da * u_ref[...].astype(F32) * (sg * (1.0 + gv * (1.0 - sg)))).astype(BF16)

    o = pl.BlockSpec((tm, tn), lambda j, i: (i, j))
    return pl.pallas_call(
        body, name=name, grid=(f // tn, s // tm),
        in_specs=[pl.BlockSpec((tm, d), lambda j, i: (i, 0)), pl.BlockSpec((tn, d), lambda j, i: (j, 0)), o, o],
        out_specs=[o, o],
        out_shape=[jax.ShapeDtypeStruct((s, f), BF16), jax.ShapeDtypeStruct((s, f), BF16)],
        compiler_params=_params("parallel", "parallel"),
    )(dx, wd, g, u)


def _ffn_forward(x, norm_w, wg, wu, wd, tag):
    n, r = _rms_fwd(x, norm_w, f"{tag}_norm")
    g, u, a = _ffn_up(n, wg, wu, f"{tag}_up")
    y = _matmul([(a, wd)], tm=512, tn=1024, tk=wd.shape[0], name=f"{tag}_down", res=x, alpha=0.5)
    return y, (n, r, g, u, a)


def _ffn_backward(dy, x, norm_w, wg, wu, wd, saved, tag):
    n, r, g, u, a = saved
    dwd = _matmul([(a, dy)], ta=True, tm=1408, tn=1024, tk=2048, name=f"{tag}_dwd", alpha=0.5)
    dg, du = _ffn_dact(dy, wd, g, u, f"{tag}_dact")
    cols = wg.shape[1] // N_DEV
    dwg = _matmul([(n, dg)], ta=True, tm=512, tn=1408, tk=4096, name=f"{tag}_dwg", shard_cols=cols)
    dwu = _matmul([(n, du)], ta=True, tm=512, tn=1408, tk=4096, name=f"{tag}_dwu", shard_cols=cols)
    dn = _matmul([(dg, wg), (du, wu)], tb=True, tm=512, tn=1024, tk=wg.shape[1], name=f"{tag}_dn")
    dx, dnorm = _rms_bwd(dn, x, r, norm_w, dy, f"{tag}_dnorm")
    return dx, dnorm, dwg, dwu, dwd


Q_SCALE = GDN_HEAD_DIM ** -0.5
CONV_HALO = 8


def _conv_taps(win, w_ref, rows, sign):
    n = rows + 2 * CONV_HALO
    acc = None
    for t in range(CONV_WIDTH):
        o = sign * (t - CONV_WIDTH // 2)
        sh = win if o == 0 else pltpu.roll(win, (-o) % n, 0)
        term = sh[CONV_HALO:CONV_HALO + rows] * w_ref[t:t + 1, :]
        acc = term if acc is None else acc + term
    return acc


def _gdn_conv_fwd(p_pad, conv_wt):
    s = p_pad.shape[0]
    rows = min(CONV_ROWS, s)
    nblk = QKV_A // LANE

    def body(p_ref, w_ref, c_ref, y_ref, pad):
        j = pl.program_id(0)
        zeros = jnp.zeros((CONV_HALO, LANE), F32)
        pad[0:CONV_HALO, :] = zeros
        pad[CONV_HALO + s:2 * CONV_HALO + s, :] = zeros
        pad[CONV_HALO:CONV_HALO + s, :] = p_ref[...]

        def chunk(ci, carry):
            b = pl.multiple_of(ci * rows, rows)
            win = pad[pl.ds(b, rows + 2 * CONV_HALO), :]
            c = _conv_taps(win, w_ref, rows, 1)
            c_ref[pl.ds(b, rows), :] = c
            act = c * _sigmoid(c)
            nrm = lax.rsqrt(jnp.sum(act * act, axis=-1, keepdims=True) + EPS)
            mult = jnp.where(j < GDN_HEADS, nrm * Q_SCALE, jnp.where(j < 2 * GDN_HEADS, nrm, 1.0))
            y_ref[pl.ds(b, rows), :] = act * mult
            return carry

        lax.fori_loop(0, s // rows, chunk, 0)

    col = pl.BlockSpec((s, LANE), lambda j: (0, j))
    return pl.pallas_call(
        body, name="gdn_conv_fwd", grid=(nblk,),
        in_specs=[col, pl.BlockSpec((8, LANE), lambda j: (0, j))],
        out_specs=[col, col],
        out_shape=[jax.ShapeDtypeStruct((s, QKV_A), F32), jax.ShapeDtypeStruct((s, QKV_A), F32)],
        scratch_shapes=[pltpu.VMEM((s + 2 * CONV_HALO, LANE), F32)],
        compiler_params=_params("parallel"),
    )(p_pad, conv_wt)


def _gdn_conv_bwd(dy_f, dy_r, c_pre, p_pad, conv_wt):
    s = p_pad.shape[0]
    rows = min(CONV_ROWS, s)
    nblk = QKV_A // LANE

    def body(dyf_ref, dyr_ref, c_ref, p_ref, w_ref, dp_ref, dw_ref, ppad, dcpad):
        j = pl.program_id(0)
        zeros = jnp.zeros((CONV_HALO, LANE), F32)
        for buf in (ppad, dcpad):
            buf[0:CONV_HALO, :] = zeros
            buf[CONV_HALO + s:2 * CONV_HALO + s, :] = zeros
        ppad[CONV_HALO:CONV_HALO + s, :] = p_ref[...]

        def act_bwd(ci, carry):
            b = pl.multiple_of(ci * rows, rows)
            c = c_ref[pl.ds(b, rows), :]
            g = dyf_ref[pl.ds(b, rows), :] + dyr_ref[pl.ds(b, rows), :]
            sg = _sigmoid(c)
            act = c * sg
            nrm = lax.rsqrt(jnp.sum(act * act, axis=-1, keepdims=True) + EPS)
            yh = act * nrm
            scale = jnp.where(j < GDN_HEADS, Q_SCALE, 1.0)
            dact_qk = (scale * nrm) * (g - yh * jnp.sum(g * yh, axis=-1, keepdims=True))
            dact = jnp.where(j < 2 * GDN_HEADS, dact_qk, g)
            dcpad[pl.ds(pl.multiple_of(b + CONV_HALO, CONV_HALO), rows), :] = dact * (sg * (1.0 + c * (1.0 - sg)))
            return carry

        lax.fori_loop(0, s // rows, act_bwd, 0)
        tap = lax.broadcasted_iota(jnp.int32, (8, LANE), 0)

        def taps_bwd(ci, dw):
            b = pl.multiple_of(ci * rows, rows)
            dcw = dcpad[pl.ds(b, rows + 2 * CONV_HALO), :]
            dp_ref[pl.ds(b, rows), :] = _conv_taps(dcw, w_ref, rows, -1)
            pw = ppad[pl.ds(b, rows + 2 * CONV_HALO), :]
            dc = dcw[CONV_HALO:CONV_HALO + rows]
            n = rows + 2 * CONV_HALO
            for t in range(CONV_WIDTH):
                o = t - CONV_WIDTH // 2
                sh = pw if o == 0 else pltpu.roll(pw, (-o) % n, 0)
                row = jnp.sum(dc * sh[CONV_HALO:CONV_HALO + rows], axis=0, keepdims=True)
                dw = dw + jnp.where(tap == t, row, 0.0)
            return dw

        dw_ref[...] = lax.fori_loop(0, s // rows, taps_bwd, jnp.zeros((8, LANE), F32))

    col = pl.BlockSpec((s, LANE), lambda j: (0, j))
    wspec = pl.BlockSpec((8, LANE), lambda j: (0, j))
    return pl.pallas_call(
        body, name="gdn_conv_bwd", grid=(nblk,),
        in_specs=[col, col, col, col, wspec],
        out_specs=[col, wspec],
        out_shape=[jax.ShapeDtypeStruct((s, QKV_A), F32), jax.ShapeDtypeStruct((8, QKV_A), F32)],
        scratch_shapes=[pltpu.VMEM((s + 2 * CONV_HALO, LANE), F32), pltpu.VMEM((s + 2 * CONV_HALO, LANE), F32)],
        compiler_params=_params("parallel"),
    )(dy_f, dy_r, c_pre, p_pad, conv_wt)


def _softplus(x):
    return jnp.maximum(x, 0.0) + jnp.log(1.0 + jnp.exp(-jnp.abs(x)))


def _gdn_gates_fwd(p_pad, alog_row, dt_row):
    s = p_pad.shape[0]
    tm = min(1024, s)

    def body(p_ref, al_ref, dt_ref, o_ref):
        x = p_ref[...]
        lane = lax.broadcasted_iota(jnp.int32, x.shape, 1)
        g = -jnp.exp(al_ref[...]) * _softplus(x + dt_ref[...])
        o_ref[...] = jnp.where(lane < 8, g, jnp.where(lane < 16, _sigmoid(x), 0.0))

    vec = pl.BlockSpec((1, LANE), lambda i: (0, 0))
    return pl.pallas_call(
        body, name="gdn_gates_fwd", grid=(s // tm,),
        in_specs=[pl.BlockSpec((tm, LANE), lambda i: (i, OFF_AB // LANE)), vec, vec],
        out_specs=pl.BlockSpec((tm, LANE), lambda i: (i, 0)),
        out_shape=jax.ShapeDtypeStruct((s, LANE), F32),
        compiler_params=_params("parallel"),
    )(p_pad, alog_row, dt_row)


def _gdn_gates_bwd(dgb_f, dgb_r, p_pad, gb, alog_row, dt_row):
    s = p_pad.shape[0]
    tm = min(1024, s)

    def body(df_ref, dr_ref, p_ref, gb_ref, al_ref, dt_ref, dp_ref, sum_ref):
        @pl.when(pl.program_id(0) == 0)
        def _():
            sum_ref[...] = jnp.zeros_like(sum_ref)

        x = p_ref[...]
        gbv = gb_ref[...]
        dgb = df_ref[...] + dr_ref[...]
        lane = lax.broadcasted_iota(jnp.int32, x.shape, 1)
        da = dgb * (-jnp.exp(al_ref[...])) * _sigmoid(x + dt_ref[...])
        db = dgb * gbv * (1.0 - gbv)
        dp_ref[...] = jnp.where(lane < 8, da, jnp.where(lane < 16, db, 0.0))
        row = lax.broadcasted_iota(jnp.int32, (8, LANE), 0)
        lane8 = lax.broadcasted_iota(jnp.int32, (8, LANE), 1)
        d_alog = jnp.sum(dgb * gbv, axis=0, keepdims=True)
        d_dt = jnp.sum(da, axis=0, keepdims=True)
        upd = jnp.where(row == 0, d_alog, jnp.where(row == 1, d_dt, 0.0))
        sum_ref[...] += jnp.where(lane8 < 8, upd, 0.0)

    vec = pl.BlockSpec((1, LANE), lambda i: (0, 0))
    blk = pl.BlockSpec((tm, LANE), lambda i: (i, 0))
    return pl.pallas_call(
        body, name="gdn_gates_bwd", grid=(s // tm,),
        in_specs=[blk, blk, pl.BlockSpec((tm, LANE), lambda i: (i, OFF_AB // LANE)), blk, vec, vec],
        out_specs=[blk, pl.BlockSpec((8, LANE), lambda i: (0, 0))],
        out_shape=[jax.ShapeDtypeStruct((s, LANE), F32), jax.ShapeDtypeStruct((8, LANE), F32)],
        compiler_params=_params("arbitrary"),
    )(dgb_f, dgb_r, p_pad, gb, alog_row, dt_row)


def _chunk_masks(rev):
    row = lax.broadcasted_iota(jnp.int32, (CHUNK, CHUNK), 0)
    col = lax.broadcasted_iota(jnp.int32, (CHUNK, CHUNK), 1)
    le = (col >= row) if rev else (col <= row)
    strict = (col > row) if rev else (col < row)
    return le, strict, row == col


def _chunk_common(q, k, v, g, beta, gc, masks):
    le, strict, eye = masks
    gc_row = _dot_hi(jnp.ones((CHUNK, CHUNK), F32), jnp.where(eye, gc, 0.0))
    decay = jnp.where(le, jnp.exp(jnp.where(le, gc - gc_row, 0.0)), 0.0)
    eg = jnp.exp(gc)
    gl = jnp.sum(g, axis=0, keepdims=True)
    kb = k * beta
    vb = v * beta
    kbeg = kb * eg
    lm = jnp.where(strict, _dot(kb, k, NT) * decay, 0.0)
    intra = _dot(q, k, NT) * decay
    qg = q * eg
    edec = jnp.exp(gl - gc)
    kdec = k * edec
    return dict(decay=decay, eg=eg, gl=gl, kb=kb, vb=vb, kbeg=kbeg, lm=lm, intra=intra, qg=qg, edec=edec, kdec=kdec)


def _unit_lower_inverse(lm, eye):
    x = -lm
    t = eye.astype(F32) + x
    p = x
    for _ in range(5):
        p = _dot_hi(p, p)
        t = t + _dot_hi(t, p)
    return t


def _gate_lanes(rev, h):
    d = 1 if rev else 0
    return d * GDN_HEADS + h, 8 + d * GDN_HEADS + h


def _delta_fwd(y, gb, rev):
    s = y.shape[0]
    nc = s // CHUNK
    hd = GDN_HEAD_DIM

    def chunk_of(n):
        return nc - 1 - n if rev else n

    def body(q_ref, k_ref, v_ref, gb_ref, o_ref, s_all, t_all, state):
        @pl.when(pl.program_id(0) == 0)
        def _():
            state[...] = jnp.zeros_like(state)

        masks = _chunk_masks(rev)
        gbv = gb_ref[...]
        gcm = _dot_hi(masks[0].astype(F32), gbv)
        for h in range(GDN_HEADS):
            gi, bi = _gate_lanes(rev, h)
            sl = slice(h * hd, (h + 1) * hd)
            q, k, v = q_ref[:, sl], k_ref[:, sl], v_ref[:, sl]
            g, beta, gc = gbv[:, gi:gi + 1], gbv[:, bi:bi + 1], gcm[:, gi:gi + 1]
            cm = _chunk_common(q, k, v, g, beta, gc, masks)
            tinv = _unit_lower_inverse(cm["lm"], masks[2])
            u = _dot(tinv, cm["vb"])
            w = _dot(tinv, cm["kbeg"])
            st = state[h]
            v_new = u - _dot(w, st)
            o_ref[:, sl] = _dot(cm["qg"], st) + _dot(cm["intra"], v_new)
            s_all[0, h] = st
            t_all[0, h] = tinv
            state[h] = st * jnp.exp(cm["gl"]) + _dot(cm["kdec"], v_new, TN)

    def col(j):
        return pl.BlockSpec((CHUNK, GDN_WIDTH), lambda n: (chunk_of(n), j))

    return pl.pallas_call(
        body, name="delta_fwd_r" if rev else "delta_fwd_f", grid=(nc,),
        in_specs=[col(0), col(1), col(2), pl.BlockSpec((CHUNK, LANE), lambda n: (chunk_of(n), 0))],
        out_specs=[pl.BlockSpec((CHUNK, GDN_WIDTH), lambda n: (chunk_of(n), 0)),
                   pl.BlockSpec((1, GDN_HEADS, hd, hd), lambda n: (chunk_of(n), 0, 0, 0)),
                   pl.BlockSpec((1, GDN_HEADS, CHUNK, CHUNK), lambda n: (chunk_of(n), 0, 0, 0))],
        out_shape=[jax.ShapeDtypeStruct((s, GDN_WIDTH), F32),
                   jax.ShapeDtypeStruct((nc, GDN_HEADS, hd, hd), F32),
                   jax.ShapeDtypeStruct((nc, GDN_HEADS, CHUNK, CHUNK), F32)],
        scratch_shapes=[pltpu.VMEM((GDN_HEADS, hd, hd), F32)],
        compiler_params=_params("arbitrary"),
    )(y, y, y, gb)


def _delta_bwd(y, gb, do, s_all, t_all, rev):
    s = y.shape[0]
    nc = s // CHUNK
    hd = GDN_HEAD_DIM

    def chunk_of(n):
        return n if rev else nc - 1 - n

    def body(q_ref, k_ref, v_ref, gb_ref, do_ref, s_ref, t_ref, dy_ref, dgb_ref, dstate):
        @pl.when(pl.program_id(0) == 0)
        def _():
            dstate[...] = jnp.zeros_like(dstate)

        masks = _chunk_masks(rev)
        le, strict, _ = masks
        le_t = _chunk_masks(not rev)[0].astype(F32)
        gbv = gb_ref[...]
        gcm = _dot_hi(le.astype(F32), gbv)
        lane = lax.broadcasted_iota(jnp.int32, (CHUNK, LANE), 1)
        ones_cl = jnp.ones((CHUNK, LANE), F32)
        dgc_tile = jnp.zeros((CHUNK, LANE), F32)
        rest_tile = jnp.zeros((CHUNK, LANE), F32)
        for h in range(GDN_HEADS):
            gi, bi = _gate_lanes(rev, h)
            sl = slice(h * hd, (h + 1) * hd)
            q, k, v = q_ref[:, sl], k_ref[:, sl], v_ref[:, sl]
            g, beta, gc = gbv[:, gi:gi + 1], gbv[:, bi:bi + 1], gcm[:, gi:gi + 1]
            cm = _chunk_common(q, k, v, g, beta, gc, masks)
            tinv = t_ref[0, h]
            st = s_ref[0, h]
            ds_out = dstate[h]
            dov = do_ref[:, sl]
            u = _dot(tinv, cm["vb"])
            w = _dot(tinv, cm["kbeg"])
            v_new = u - _dot(w, st)
            egl = jnp.exp(cm["gl"])
            d_qg = _dot(dov, st, NT)
            d_intra = _dot(dov, v_new, NT)
            dv_new = _dot(cm["intra"], dov, TN) + _dot(cm["kdec"], ds_out)
            d_kdec = _dot(v_new, ds_out, NT)
            dstate[h] = _dot(cm["qg"], dov, TN) + egl * ds_out - _dot(w, dv_new, TN)
            dgl = egl * jnp.sum(jnp.sum(st * ds_out, axis=1, keepdims=True), axis=0, keepdims=True)
            dw = -_dot(dv_new, st, NT)
            dvb = _dot(tinv, dv_new, TN)
            dkbeg = _dot(tinv, dw, TN)
            dlm = jnp.where(strict, -(_dot(dvb, u, NT) + _dot(dkbeg, w, NT)), 0.0)
            d_a = dlm * cm["decay"]
            d_qk = d_intra * cm["decay"]
            e = dlm * cm["lm"] + d_intra * cm["intra"]
            dgc = jnp.sum(e, axis=1, keepdims=True) - _dot_hi(e, ones_cl, TN)[:, 0:1]
            dkb = _dot(d_a, k) + dkbeg * cm["eg"]
            dk = _dot(d_a, cm["kb"], TN) + _dot(d_qk, q, TN)
            dq = _dot(d_qk, k) + d_qg * cm["eg"]
            dgc = dgc + jnp.sum(d_qg * cm["qg"], axis=1, keepdims=True)
            dgc = dgc + jnp.sum(dkbeg * cm["kbeg"], axis=1, keepdims=True)
            tdec = jnp.sum(d_kdec * cm["kdec"], axis=1, keepdims=True)
            dk = dk + d_kdec * cm["edec"] + dkb * beta
            dgc = dgc - tdec
            dgl = dgl + jnp.sum(tdec, axis=0, keepdims=True)
            dbeta = jnp.sum(dvb * v, axis=1, keepdims=True) + jnp.sum(dkb * k, axis=1, keepdims=True)
            dy_ref[:, h * hd:(h + 1) * hd] = dq
            dy_ref[:, GDN_WIDTH + h * hd:GDN_WIDTH + (h + 1) * hd] = dk
            dy_ref[:, 2 * GDN_WIDTH + h * hd:2 * GDN_WIDTH + (h + 1) * hd] = dvb * beta
            dgc_tile = dgc_tile + jnp.where(lane == gi, dgc, 0.0)
            rest_tile = rest_tile + jnp.where(lane == gi, dgl, 0.0) + jnp.where(lane == bi, dbeta, 0.0)
        dgb_ref[...] = _dot_hi(le_t, dgc_tile) + rest_tile

    def col(j):
        return pl.BlockSpec((CHUNK, GDN_WIDTH), lambda n: (chunk_of(n), j))

    first = pl.BlockSpec((CHUNK, GDN_WIDTH), lambda n: (chunk_of(n), 0))
    return pl.pallas_call(
        body, name="delta_bwd_r" if rev else "delta_bwd_f", grid=(nc,),
        in_specs=[col(0), col(1), col(2), pl.BlockSpec((CHUNK, LANE), lambda n: (chunk_of(n), 0)), first,
                  pl.BlockSpec((1, GDN_HEADS, hd, hd), lambda n: (chunk_of(n), 0, 0, 0)),
                  pl.BlockSpec((1, GDN_HEADS, CHUNK, CHUNK), lambda n: (chunk_of(n), 0, 0, 0))],
        out_specs=[pl.BlockSpec((CHUNK, QKV_A), lambda n: (chunk_of(n), 0)),
                   pl.BlockSpec((CHUNK, LANE), lambda n: (chunk_of(n), 0))],
        out_shape=[jax.ShapeDtypeStruct((s, QKV_A), F32), jax.ShapeDtypeStruct((s, LANE), F32)],
        scratch_shapes=[pltpu.VMEM((GDN_HEADS, hd, hd), F32)],
        compiler_params=_params("arbitrary"),
    )(y, y, y, gb, do, s_all, t_all)


BNN = (((2,), (1,)), ((0,), (0,)))
BNT = (((2,), (2,)), ((0,), (0,)))
BTN = (((1,), (1,)), ((0,), (0,)))
NB = 2 * GDN_HEADS


def _bdot(a, b, dn=BNN):
    return lax.dot_general(a.astype(BF16), b.astype(BF16), dn, preferred_element_type=F32)


def _dot3(a, b, dn):
    ah = a.astype(BF16)
    al = (a - ah.astype(F32)).astype(BF16)
    bh = b.astype(BF16)
    bl = (b - bh.astype(F32)).astype(BF16)

    def d(x, y):
        return lax.dot_general(x, y, dn, preferred_element_type=F32)

    return d(ah, bh) + d(ah, bl) + d(al, bh)


def _both(f_val, r_val):
    return jnp.stack([f_val] * GDN_HEADS + [r_val] * GDN_HEADS)


def _heads(ref_f, ref_r):
    hd = GDN_HEAD_DIM
    return jnp.stack([ref_f[:, h * hd:(h + 1) * hd] for h in range(GDN_HEADS)]
                     + [ref_r[:, h * hd:(h + 1) * hd] for h in range(GDN_HEADS)])


def _gate_cols(tile_f, tile_r, base):
    return jnp.stack([tile_f[:, base + h:base + h + 1] for h in range(GDN_HEADS)]
                     + [tile_r[:, base + GDN_HEADS + h:base + GDN_HEADS + h + 1] for h in range(GDN_HEADS)])


def _chunk_common2(q, k, v, gbf, gbr):
    mf, mr = _chunk_masks(False), _chunk_masks(True)
    le, strict = _both(mf[0], mr[0]), _both(mf[1], mr[1])
    eye = mf[2]
    gcm_f = _dot3(mf[0].astype(F32), gbf, NN)
    gcm_r = _dot3(mr[0].astype(F32), gbr, NN)
    g, beta, gc = _gate_cols(gbf, gbr, 0), _gate_cols(gbf, gbr, 8), _gate_cols(gcm_f, gcm_r, 0)
    gc_row = _dot3(jnp.ones((NB, CHUNK, CHUNK), F32), jnp.where(eye[None], gc, 0.0), BNN)
    decay = jnp.where(le, jnp.exp(jnp.where(le, gc - gc_row, 0.0)), 0.0)
    eg = jnp.exp(gc)
    gl = jnp.sum(g, axis=1, keepdims=True)
    kb = k * beta
    vb = v * beta
    kbeg = kb * eg
    lm = jnp.where(strict, _bdot(kb, k, BNT) * decay, 0.0)
    intra = _bdot(q, k, BNT) * decay
    edec = jnp.exp(gl - gc)
    return dict(strict=strict, eye=eye, beta=beta, decay=decay, eg=eg, gl=gl, kb=kb, vb=vb, kbeg=kbeg,
                lm=lm, intra=intra, qg=q * eg, edec=edec, kdec=k * edec)


def _unit_triangular_inverse(lm, eye):
    x = -lm
    t = eye[None].astype(F32) + x
    p = x
    for _ in range(5):
        p = _dot3(p, p, BNN)
        t = t + _dot3(t, p, BNN)
    return t


def _delta_fwd2(y, gb):
    s = y.shape[0]
    nc = s // CHUNK
    hd = GDN_HEAD_DIM

    def body(qf, kf, vf, gf, qr, kr, vr, gr, of_ref, or_ref, sf_all, sr_all, tf_all, tr_all, state):
        @pl.when(pl.program_id(0) == 0)
        def _():
            state[...] = jnp.zeros_like(state)

        q, k, v = _heads(qf, qr), _heads(kf, kr), _heads(vf, vr)
        cm = _chunk_common2(q, k, v, gf[...], gr[...])
        tinv = _unit_triangular_inverse(cm["lm"], cm["eye"])
        u = _bdot(tinv, cm["vb"])
        w = _bdot(tinv, cm["kbeg"])
        st = state[...]
        v_new = u - _bdot(w, st)
        o = _bdot(cm["qg"], st) + _bdot(cm["intra"], v_new)
        state[...] = st * jnp.exp(cm["gl"]) + _bdot(cm["kdec"], v_new, BTN)
        for h in range(GDN_HEADS):
            of_ref[:, h * hd:(h + 1) * hd] = o[h]
            or_ref[:, h * hd:(h + 1) * hd] = o[GDN_HEADS + h]
        sf_all[0] = st[:GDN_HEADS]
        sr_all[0] = st[GDN_HEADS:]
        tf_all[0] = tinv[:GDN_HEADS]
        tr_all[0] = tinv[GDN_HEADS:]

    def col(j, rev):
        return pl.BlockSpec((CHUNK, GDN_WIDTH), (lambda n: (nc - 1 - n, j)) if rev else (lambda n: (n, j)))

    def gate(rev):
        return pl.BlockSpec((CHUNK, LANE), (lambda n: (nc - 1 - n, 0)) if rev else (lambda n: (n, 0)))

    def per_chunk(d1, d2, rev):
        return pl.BlockSpec((1, GDN_HEADS, d1, d2), (lambda n: (nc - 1 - n, 0, 0, 0)) if rev else (lambda n: (n, 0, 0, 0)))

    return pl.pallas_call(
        body, name="delta_fwd", grid=(nc,),
        in_specs=[col(0, False), col(1, False), col(2, False), gate(False), col(0, True), col(1, True), col(2, True), gate(True)],
        out_specs=[col(0, False), col(0, True), per_chunk(hd, hd, False), per_chunk(hd, hd, True),
                   per_chunk(CHUNK, CHUNK, False), per_chunk(CHUNK, CHUNK, True)],
        out_shape=[jax.ShapeDtypeStruct((s, GDN_WIDTH), F32)] * 2 + [jax.ShapeDtypeStruct((nc, GDN_HEADS, hd, hd), F32)] * 2
        + [jax.ShapeDtypeStruct((nc, GDN_HEADS, CHUNK, CHUNK), F32)] * 2,
        scratch_shapes=[pltpu.VMEM((NB, hd, hd), F32)],
        compiler_params=_params("arbitrary"),
    )(y, y, y, gb, y, y, y, gb)


def _delta_bwd2(y, gb, do, sf_all, sr_all, tf_all, tr_all):
    s = y.shape[0]
    nc = s // CHUNK
    hd = GDN_HEAD_DIM

    def body(qf, kf, vf, gf, dof, sf, tf, qr, kr, vr, gr, dor, sr, tr, dyf_ref, dyr_ref, dgf_ref, dgr_ref, dstate):
        @pl.when(pl.program_id(0) == 0)
        def _():
            dstate[...] = jnp.zeros_like(dstate)

        q, k, v, dov = _heads(qf, qr), _heads(kf, kr), _heads(vf, vr), _heads(dof, dor)
        cm = _chunk_common2(q, k, v, gf[...], gr[...])
        tinv = jnp.concatenate([tf[0], tr[0]], axis=0)
        st = jnp.concatenate([sf[0], sr[0]], axis=0)
        ds_out = dstate[...]
        decay, lm, intra, qg, kdec, kbeg, eg, kb, beta = (
            cm[n] for n in ("decay", "lm", "intra", "qg", "kdec", "kbeg", "eg", "kb", "beta"))
        u = _bdot(tinv, cm["vb"])
        w = _bdot(tinv, kbeg)
        v_new = u - _bdot(w, st)
        egl = jnp.exp(cm["gl"])
        d_qg = _bdot(dov, st, BNT)
        d_intra = _bdot(dov, v_new, BNT)
        dv_new = _bdot(intra, dov, BTN) + _bdot(kdec, ds_out)
        d_kdec = _bdot(v_new, ds_out, BNT)
        dstate[...] = _bdot(qg, dov, BTN) + egl * ds_out - _bdot(w, dv_new, BTN)
        dgl = egl * jnp.sum(jnp.sum(st * ds_out, axis=2, keepdims=True), axis=1, keepdims=True)
        dw = -_bdot(dv_new, st, BNT)
        dvb = _bdot(tinv, dv_new, BTN)
        dkbeg = _bdot(tinv, dw, BTN)
        dlm = jnp.where(cm["strict"], -(_bdot(dvb, u, BNT) + _bdot(dkbeg, w, BNT)), 0.0)
        d_a = dlm * decay
        d_qk = d_intra * decay
        e = dlm * lm + d_intra * intra
        colsum = _dot3(e, jnp.ones((NB, CHUNK, LANE), F32), BTN)[:, :, 0:1]
        dgc = jnp.sum(e, axis=2, keepdims=True) - colsum
        dkb = _bdot(d_a, k) + dkbeg * eg
        dk = _bdot(d_a, kb, BTN) + _bdot(d_qk, q, BTN)
        dq = _bdot(d_qk, k) + d_qg * eg
        dgc = dgc + jnp.sum(d_qg * qg, axis=2, keepdims=True) + jnp.sum(dkbeg * kbeg, axis=2, keepdims=True)
        tdec = jnp.sum(d_kdec * kdec, axis=2, keepdims=True)
        dk = dk + d_kdec * cm["edec"] + dkb * beta
        dgc = dgc - tdec
        dgl = dgl + jnp.sum(tdec, axis=1, keepdims=True)
        dbeta = jnp.sum(dvb * v, axis=2, keepdims=True) + jnp.sum(dkb * k, axis=2, keepdims=True)
        dv = dvb * beta
        lane = lax.broadcasted_iota(jnp.int32, (CHUNK, LANE), 1)
        for rev, dy_ref, dg_ref in ((False, dyf_ref, dgf_ref), (True, dyr_ref, dgr_ref)):
            dgc_tile = jnp.zeros((CHUNK, LANE), F32)
            rest = jnp.zeros((CHUNK, LANE), F32)
            for h in range(GDN_HEADS):
                b = (GDN_HEADS if rev else 0) + h
                gi, bi = _gate_lanes(rev, h)
                dgc_tile = dgc_tile + jnp.where(lane == gi, dgc[b], 0.0)
                rest = rest + jnp.where(lane == gi, dgl[b], 0.0) + jnp.where(lane == bi, dbeta[b], 0.0)
                dy_ref[:, h * hd:(h + 1) * hd] = dq[b]
                dy_ref[:, GDN_WIDTH + h * hd:GDN_WIDTH + (h + 1) * hd] = dk[b]
                dy_ref[:, 2 * GDN_WIDTH + h * hd:2 * GDN_WIDTH + (h + 1) * hd] = dv[b]
            le_t = _chunk_masks(not rev)[0].astype(F32)
            dg_ref[...] = _dot3(le_t, dgc_tile, NN) + rest

    def col(j, rev):
        return pl.BlockSpec((CHUNK, GDN_WIDTH), (lambda n: (n, j)) if rev else (lambda n: (nc - 1 - n, j)))

    def wide(width, rev):
        return pl.BlockSpec((CHUNK, width), (lambda n: (n, 0)) if rev else (lambda n: (nc - 1 - n, 0)))

    def per_chunk(d1, d2, rev):
        return pl.BlockSpec((1, GDN_HEADS, d1, d2), (lambda n: (n, 0, 0, 0)) if rev else (lambda n: (nc - 1 - n, 0, 0, 0)))

    def side(rev):
        return [col(0, rev), col(1, rev), col(2, rev), wide(LANE, rev), wide(GDN_WIDTH, rev), per_chunk(hd, hd, rev),
                per_chunk(CHUNK, CHUNK, rev)]

    return pl.pallas_call(
        body, name="delta_bwd", grid=(nc,),
        in_specs=side(False) + side(True),
        out_specs=[wide(QKV_A, False), wide(QKV_A, True), wide(LANE, False), wide(LANE, True)],
        out_shape=[jax.ShapeDtypeStruct((s, QKV_A), F32)] * 2 + [jax.ShapeDtypeStruct((s, LANE), F32)] * 2,
        scratch_shapes=[pltpu.VMEM((NB, hd, hd), F32)],
        compiler_params=_params("arbitrary"),
    )(y, y, y, gb, do, sf_all, tf_all, y, y, y, gb, do, sr_all, tr_all)


def _gdn_post_fwd(o_f, o_r, p_pad, norm_row):
    s = o_f.shape[0]
    tm = min(512, s)
    hd = GDN_HEAD_DIM

    def body(of_ref, or_ref, z_ref, w_ref, out_ref, osum_ref):
        o = of_ref[...] + or_ref[...]
        osum_ref[...] = o
        z = z_ref[...]
        gate = z * _sigmoid(z)
        for h in range(GDN_HEADS):
            sl = slice(h * hd, (h + 1) * hd)
            oh = o[:, sl]
            r = lax.rsqrt(jnp.mean(oh * oh, axis=-1, keepdims=True) + EPS)
            out_ref[:, sl] = (oh * r * w_ref[...] * gate[:, sl]).astype(BF16)

    blk = pl.BlockSpec((tm, GDN_WIDTH), lambda i: (i, 0))
    return pl.pallas_call(
        body, name="gdn_post_fwd", grid=(s // tm,),
        in_specs=[blk, blk, pl.BlockSpec((tm, GDN_WIDTH), lambda i: (i, OFF_Z // GDN_WIDTH)),
                  pl.BlockSpec((1, hd), lambda i: (0, 0))],
        out_specs=[blk, blk],
        out_shape=[jax.ShapeDtypeStruct((s, GDN_WIDTH), BF16), jax.ShapeDtypeStruct((s, GDN_WIDTH), F32)],
        compiler_params=_params("parallel"),
    )(o_f, o_r, p_pad, norm_row)


def _gdn_post_bwd(d_out, o_sum, p_pad, norm_row):
    s = o_sum.shape[0]
    tm = min(512, s)
    hd = GDN_HEAD_DIM

    def body(d_ref, o_ref, z_ref, w_ref, do_ref, dz_ref, dw_ref):
        @pl.when(pl.program_id(0) == 0)
        def _():
            dw_ref[...] = jnp.zeros_like(dw_ref)

        z = z_ref[...]
        sg = _sigmoid(z)
        gate = z * sg
        dgate = sg * (1.0 + z * (1.0 - sg))
        wv = w_ref[...]
        dw = jnp.zeros((1, hd), F32)
        for h in range(GDN_HEADS):
            sl = slice(h * hd, (h + 1) * hd)
            oh = o_ref[:, sl]
            dh = d_ref[:, sl]
            r = lax.rsqrt(jnp.mean(oh * oh, axis=-1, keepdims=True) + EPS)
            ohat = oh * r
            dz_ref[:, sl] = dh * ohat * wv * dgate[:, sl]
            drn = dh * gate[:, sl]
            t = drn * wv
            do_ref[:, sl] = r * (t - ohat * jnp.mean(t * ohat, axis=-1, keepdims=True))
            dw = dw + jnp.sum(drn * ohat, axis=0, keepdims=True)
        dw_ref[...] += dw

    blk = pl.BlockSpec((tm, GDN_WIDTH), lambda i: (i, 0))
    vec = pl.BlockSpec((1, hd), lambda i: (0, 0))
    return pl.pallas_call(
        body, name="gdn_post_bwd", grid=(s // tm,),
        in_specs=[blk, blk, pl.BlockSpec((tm, GDN_WIDTH), lambda i: (i, OFF_Z // GDN_WIDTH)), vec],
        out_specs=[blk, blk, vec],
        out_shape=[jax.ShapeDtypeStruct((s, GDN_WIDTH), F32), jax.ShapeDtypeStruct((s, GDN_WIDTH), F32),
                   jax.ShapeDtypeStruct((1, hd), F32)],
        compiler_params=_params("arbitrary"),
    )(d_out, o_sum, p_pad, norm_row)


def _add2(a, b, name):
    s, w = a.shape
    tm = next(t for t in (1024, 640, 512, 256, 128, 64, 8) if s % t == 0)

    def body(a_ref, b_ref, o_ref):
        o_ref[...] = a_ref[...] + b_ref[...]

    blk = pl.BlockSpec((tm, w), lambda i: (i, 0))
    return pl.pallas_call(body, name=name, grid=(s // tm,), in_specs=[blk, blk], out_specs=blk,
                          out_shape=jax.ShapeDtypeStruct((s, w), F32), compiler_params=_params("parallel"))(a, b)


def _gdn_forward(p_pad, conv_wt, alog_row, dt_row, norm_row):
    c_pre, y = _gdn_conv_fwd(p_pad, conv_wt)
    gb = _gdn_gates_fwd(p_pad, alog_row, dt_row)
    o_f, o_r, s_f, s_r, t_f, t_r = _delta_fwd2(y, gb)
    out, o_sum = _gdn_post_fwd(o_f, o_r, p_pad, norm_row)
    return out, (c_pre, y, gb, s_f, t_f, s_r, t_r, o_sum)


def _gdn_backward(d_out, p_pad, conv_wt, alog_row, dt_row, norm_row, saved):
    c_pre, y, gb, s_f, t_f, s_r, t_r, o_sum = saved
    do, dz, dnorm = _gdn_post_bwd(d_out, o_sum, p_pad, norm_row)
    dy_f, dy_r, dgb_f, dgb_r = _delta_bwd2(y, gb, do, s_f, s_r, t_f, t_r)
    dp_qkv, dconv = _gdn_conv_bwd(dy_f, dy_r, c_pre, p_pad, conv_wt)
    dp_ab, gate_sums = _gdn_gates_bwd(dgb_f, dgb_r, p_pad, gb, alog_row, dt_row)
    return dp_qkv, dz, dp_ab, dconv, gate_sums, dnorm


ATT_BK = ATT_BQ + 2 * ATT_HALO
SWA_SCALE = SWA_HEAD_DIM ** -0.5


def _t5_bucket(rel):
    nb = REL_BUCKETS // 2
    bucket = (rel > 0).astype(np.int32) * nb
    n = np.abs(rel)
    max_exact = nb // 2
    large = max_exact + (np.log(np.maximum(n, 1) / max_exact)
                         / math.log(REL_MAX_DISTANCE / max_exact) * (nb - max_exact)).astype(np.int32)
    large = np.minimum(large, nb - 1)
    return (bucket + np.where(n < max_exact, n, large)).astype(np.int32)


def _band_tables(dilation, queries_are_rows_of_block):
    blk = np.arange(ATT_BQ)
    band = np.arange(ATT_BK) - ATT_HALO
    if queries_are_rows_of_block:
        rel = band[None, :] - blk[:, None]
        band_idx = np.broadcast_to(np.arange(ATT_BK)[None, :], rel.shape)
    else:
        rel = blk[None, :] - band[:, None]
        band_idx = np.broadcast_to(np.arange(ATT_BK)[:, None], rel.shape)
    base = np.abs(rel) <= ATT_HALO
    not_prev = band_idx >= ATT_HALO
    not_next = band_idx < ATT_HALO + ATT_BQ
    valid = np.stack([base & not_prev, base, base & not_next, base & not_prev & not_next])
    return valid, _t5_bucket(rel * dilation)


def _bias_tiles(rel_bias, dilation, queries_are_rows_of_block):
    valid, bucket = _band_tables(dilation, queries_are_rows_of_block)
    onehot = (jnp.asarray(bucket.reshape(-1, 1)) == jnp.arange(REL_BUCKETS, dtype=jnp.int32)[None, :]).astype(F32)
    rb = jnp.dot(onehot, rel_bias.astype(F32), precision=lax.Precision.HIGHEST)
    rb = rb.T.reshape((SWA_HEADS,) + bucket.shape)
    return jnp.where(valid[:, None], rb[None], NEG_BIG).astype(F32)


def _group_sum(x, bd):
    hi = x.astype(BF16)
    lo = (x - hi.astype(F32)).astype(BF16)
    return jnp.dot(hi, bd, preferred_element_type=F32) + jnp.dot(lo, bd, preferred_element_type=F32)


def _head_block_diag():
    idx = np.arange(SWA_WIDTH) // SWA_HEAD_DIM
    return jnp.asarray(idx[:, None] == idx[None, :], BF16)


def _swa_pre_fwd(p_pad, qw_row, kw_row, bd):
    s = p_pad.shape[0]
    tm = min(512, s)
    inv = 1.0 / SWA_HEAD_DIM

    def body(q_ref, k_ref, v_ref, qw_ref, kw_ref, bd_ref, qo_ref, ko_ref, vo_ref):
        bdv = bd_ref[...]
        q = q_ref[...]
        k = k_ref[...]
        rq = lax.rsqrt(_group_sum(q * q, bdv) * inv + EPS)
        rk = lax.rsqrt(_group_sum(k * k, bdv) * inv + EPS)
        qo_ref[...] = (q * rq * qw_ref[...] * SWA_SCALE).astype(BF16)
        ko_ref[...] = (k * rk * kw_ref[...]).astype(BF16)
        vo_ref[...] = v_ref[...].astype(BF16)

    base = OFF_B // SWA_WIDTH
    blk = pl.BlockSpec((tm, SWA_WIDTH), lambda i: (i, 0))
    vec = pl.BlockSpec((1, SWA_WIDTH), lambda i: (0, 0))
    return pl.pallas_call(
        body, name="swa_pre_fwd", grid=(s // tm,),
        in_specs=[pl.BlockSpec((tm, SWA_WIDTH), lambda i: (i, base)), pl.BlockSpec((tm, SWA_WIDTH), lambda i: (i, base + 1)),
                  pl.BlockSpec((tm, SWA_WIDTH), lambda i: (i, base + 2)), vec, vec,
                  pl.BlockSpec((SWA_WIDTH, SWA_WIDTH), lambda i: (0, 0))],
        out_specs=[blk, blk, blk],
        out_shape=[jax.ShapeDtypeStruct((s, SWA_WIDTH), BF16)] * 3,
        compiler_params=_params("parallel"),
    )(p_pad, p_pad, p_pad, qw_row, kw_row, bd)


def _swa_pre_bwd(dqs, dks, dvs, p_pad, qw_row, kw_row, bd):
    s = p_pad.shape[0]
    tm = min(256, s)
    inv = 1.0 / SWA_HEAD_DIM
    npat = len(dqs)

    def body(*refs):
        dq_refs, dk_refs, dv_refs = refs[:npat], refs[npat:2 * npat], refs[2 * npat:3 * npat]
        q_ref, k_ref, qw_ref, kw_ref, bd_ref, dp_ref, dqw_ref, dkw_ref = refs[3 * npat:]

        @pl.when(pl.program_id(0) == 0)
        def _():
            dqw_ref[...] = jnp.zeros_like(dqw_ref)
            dkw_ref[...] = jnp.zeros_like(dkw_ref)

        bdv = bd_ref[...]

        def norm_bwd(x, g, w, scale):
            r = lax.rsqrt(_group_sum(x * x, bdv) * inv + EPS)
            xhat = x * r
            t = g * w * scale
            dx = r * (t - xhat * (_group_sum(t * xhat, bdv) * inv))
            return dx, jnp.sum(g * scale * xhat, axis=0, keepdims=True)

        def total(rs):
            t = rs[0][...].astype(F32)
            for r in rs[1:]:
                t = t + r[...].astype(F32)
            return t

        dq, dqw = norm_bwd(q_ref[...], total(dq_refs), qw_ref[...], SWA_SCALE)
        dk, dkw = norm_bwd(k_ref[...], total(dk_refs), kw_ref[...], 1.0)
        dp_ref[:, 0:SWA_WIDTH] = dq
        dp_ref[:, SWA_WIDTH:2 * SWA_WIDTH] = dk
        dp_ref[:, 2 * SWA_WIDTH:3 * SWA_WIDTH] = total(dv_refs)
        dqw_ref[...] += dqw
        dkw_ref[...] += dkw

    base = OFF_B // SWA_WIDTH
    blk = pl.BlockSpec((tm, SWA_WIDTH), lambda i: (i, 0))
    vec = pl.BlockSpec((1, SWA_WIDTH), lambda i: (0, 0))
    return pl.pallas_call(
        body, name="swa_pre_bwd", grid=(s // tm,),
        in_specs=[blk] * (3 * npat) + [pl.BlockSpec((tm, SWA_WIDTH), lambda i: (i, base)),
                                      pl.BlockSpec((tm, SWA_WIDTH), lambda i: (i, base + 1)), vec, vec,
                                      pl.BlockSpec((SWA_WIDTH, SWA_WIDTH), lambda i: (0, 0))],
        out_specs=[pl.BlockSpec((tm, 3 * SWA_WIDTH), lambda i: (i, 0)), vec, vec],
        out_shape=[jax.ShapeDtypeStruct((s, 3 * SWA_WIDTH), F32), jax.ShapeDtypeStruct((1, SWA_WIDTH), F32),
                   jax.ShapeDtypeStruct((1, SWA_WIDTH), F32)],
        compiler_params=_params("arbitrary"),
    )(*dqs, *dks, *dvs, p_pad, p_pad, qw_row, kw_row, bd)


def _band_specs(length):
    per = ATT_BQ // ATT_HALO
    last = length // ATT_HALO - 1
    prev = pl.BlockSpec((ATT_HALO, SWA_WIDTH), lambda r, t: (jnp.maximum(t * per - 1, 0), r))
    cur = pl.BlockSpec((ATT_BQ, SWA_WIDTH), lambda r, t: (t, r))
    nxt = pl.BlockSpec((ATT_HALO, SWA_WIDTH), lambda r, t: (jnp.minimum((t + 1) * per, last), r))
    return [prev, cur, nxt]


def _tile_variant(t, nb):
    if nb == 1:
        return 3
    return jnp.where(t == 0, 0, jnp.where(t == nb - 1, 2, 1))


def _band(refs):
    return jnp.concatenate([r[...] for r in refs], axis=0)


def _att_fwd(q, k, v, bias, dilation):
    s = q.shape[0]
    length = s // dilation
    nb = length // ATT_BQ
    view = (length, dilation * SWA_WIDTH)
    hd = SWA_HEAD_DIM

    def body(q_ref, kp, kc, kn, vp, vc, vn, b_ref, o_ref, lse_ref):
        kb, vb = _band((kp, kc, kn)), _band((vp, vc, vn))
        qv = q_ref[...]
        for h in range(SWA_HEADS):
            sl = slice(h * hd, (h + 1) * hd)
            sc = _dot(qv[:, sl], kb[:, sl], NT) + b_ref[0, h]
            m = jnp.max(sc, axis=-1, keepdims=True)
            p = jnp.exp(sc - m)
            den = jnp.sum(p, axis=-1, keepdims=True)
            o_ref[:, sl] = _dot(p, vb[:, sl]) / den
            lse_ref[:, sl] = jnp.broadcast_to(m + jnp.log(den), (ATT_BQ, hd))

    cur = pl.BlockSpec((ATT_BQ, SWA_WIDTH), lambda r, t: (t, r))
    bspec = pl.BlockSpec((1, SWA_HEADS, ATT_BQ, ATT_BK), lambda r, t: (_tile_variant(t, nb), 0, 0, 0))
    o, lse = pl.pallas_call(
        body, name=f"att_fwd_d{dilation}", grid=(dilation, nb),
        in_specs=[cur] + _band_specs(length) * 2 + [bspec],
        out_specs=[cur, cur],
        out_shape=[jax.ShapeDtypeStruct(view, F32)] * 2,
        compiler_params=_params("parallel", "parallel"),
    )(q.reshape(view), *([k.reshape(view)] * 3), *([v.reshape(view)] * 3), bias)
    return o.reshape(s, SWA_WIDTH), lse.reshape(s, SWA_WIDTH)


def _att_dq(q, k, v, dop, lse, cp, bias, dilation):
    s = q.shape[0]
    length = s // dilation
    nb = length // ATT_BQ
    view = (length, dilation * SWA_WIDTH)
    hd = SWA_HEAD_DIM

    def body(q_ref, kp, kc, kn, vp, vc, vn, do_ref, lse_ref, cp_ref, b_ref, dq_ref, db_ref):
        @pl.when((pl.program_id(0) == 0) & (pl.program_id(1) == 0))
        def _():
            db_ref[...] = jnp.zeros_like(db_ref)

        var = _tile_variant(pl.program_id(1), nb)
        kb, vb = _band((kp, kc, kn)), _band((vp, vc, vn))
        qv, dov, lsev, cpv = q_ref[...], do_ref[...], lse_ref[...], cp_ref[...]
        for h in range(SWA_HEADS):
            sl = slice(h * hd, (h + 1) * hd)
            sc = _dot(qv[:, sl], kb[:, sl], NT) + b_ref[0, h]
            p = jnp.exp(sc - lsev[:, h * hd:h * hd + 1])
            dp = _dot(dov[:, sl], vb[:, sl], NT)
            ds = p * (dp + cpv[:, h * hd:h * hd + 1])
            dq_ref[:, sl] = _dot(ds, kb[:, sl])
            db_ref[var, h] += ds

    cur = pl.BlockSpec((ATT_BQ, SWA_WIDTH), lambda r, t: (t, r))
    bspec = pl.BlockSpec((1, SWA_HEADS, ATT_BQ, ATT_BK), lambda r, t: (_tile_variant(t, nb), 0, 0, 0))
    dq, db = pl.pallas_call(
        body, name=f"att_dq_d{dilation}", grid=(dilation, nb),
        in_specs=[cur] + _band_specs(length) * 2 + [cur, cur, cur, bspec],
        out_specs=[cur, pl.BlockSpec((4, SWA_HEADS, ATT_BQ, ATT_BK), lambda r, t: (0, 0, 0, 0))],
        out_shape=[jax.ShapeDtypeStruct(view, F32), jax.ShapeDtypeStruct((4, SWA_HEADS, ATT_BQ, ATT_BK), F32)],
        compiler_params=_params("arbitrary", "arbitrary"),
    )(q.reshape(view), *([k.reshape(view)] * 3), *([v.reshape(view)] * 3), dop.reshape(view), lse.reshape(view),
      cp.reshape(view), bias)
    return dq.reshape(s, SWA_WIDTH), db


def _att_dkv(q, k, v, dop, lse, cp, bias_t, dilation):
    s = q.shape[0]
    length = s // dilation
    nb = length // ATT_BQ
    view = (length, dilation * SWA_WIDTH)
    hd = SWA_HEAD_DIM

    def body(k_ref, v_ref, qp, qc, qn, dp_, dc_, dn_, lp, lc, ln, cp_, cc_, cn_, b_ref, dk_ref, dv_ref):
        qb, dob = _band((qp, qc, qn)), _band((dp_, dc_, dn_))
        lseb, cpb = _band((lp, lc, ln)), _band((cp_, cc_, cn_))
        kv, vv = k_ref[...], v_ref[...]
        for h in range(SWA_HEADS):
            sl = slice(h * hd, (h + 1) * hd)
            sc = _dot(qb[:, sl], kv[:, sl], NT) + b_ref[0, h]
            p = jnp.exp(sc - lseb[:, h * hd:h * hd + 1])
            dv_ref[:, sl] = _dot(p, dob[:, sl], TN)
            dp = _dot(dob[:, sl], vv[:, sl], NT)
            ds = p * (dp + cpb[:, h * hd:h * hd + 1])
            dk_ref[:, sl] = _dot(ds, qb[:, sl], TN)

    cur = pl.BlockSpec((ATT_BQ, SWA_WIDTH), lambda r, t: (t, r))
    bspec = pl.BlockSpec((1, SWA_HEADS, ATT_BK, ATT_BQ), lambda r, t: (_tile_variant(t, nb), 0, 0, 0))
    dk, dv = pl.pallas_call(
        body, name=f"att_dkv_d{dilation}", grid=(dilation, nb),
        in_specs=[cur, cur] + _band_specs(length) * 4 + [bspec],
        out_specs=[cur, cur],
        out_shape=[jax.ShapeDtypeStruct(view, F32)] * 2,
        compiler_params=_params("parallel", "parallel"),
    )(k.reshape(view), v.reshape(view), *([q.reshape(view)] * 3), *([dop.reshape(view)] * 3),
      *([lse.reshape(view)] * 3), *([cp.reshape(view)] * 3), bias_t)
    return dk.reshape(s, SWA_WIDTH), dv.reshape(s, SWA_WIDTH)


N_PAIRS = SWA_HEADS // 2


def _pairs(x):
    return jnp.stack([x[:, LANE * p:LANE * (p + 1)] for p in range(N_PAIRS)])


def _per_head_rows(x):
    first = lax.broadcasted_iota(jnp.int32, x.shape, 2) < SWA_HEAD_DIM
    zero = jnp.zeros_like(x)
    return jnp.concatenate([jnp.where(first, x, zero), jnp.where(first, zero, x)], axis=1)


def _per_head_cols(x):
    return jnp.stack([jnp.concatenate([x[:, LANE * p:LANE * p + 1],
                                       x[:, LANE * p + SWA_HEAD_DIM:LANE * p + SWA_HEAD_DIM + 1]], axis=0)
                      for p in range(N_PAIRS)])


def _merge_heads(x, rows):
    first = lax.broadcasted_iota(jnp.int32, (N_PAIRS, rows, LANE), 2) < SWA_HEAD_DIM
    return jnp.where(first, x[:, :rows], x[:, rows:])


def _store_pairs(ref, x):
    for p in range(N_PAIRS):
        ref[:, LANE * p:LANE * (p + 1)] = x[p].astype(ref.dtype)


def _att_fwd2(q, k, v, bias, dilation):
    s = q.shape[0]
    length = s // dilation
    nb = length // ATT_BQ
    view = (length, dilation * SWA_WIDTH)

    def body(q_ref, kp, kc, kn, vp, vc, vn, b_ref, o_ref, lse_ref):
        kb, vb = _pairs(_band((kp, kc, kn))), _pairs(_band((vp, vc, vn)))
        qm = _per_head_rows(_pairs(q_ref[...]))
        sc = _bdot(qm, kb, BNT) + b_ref[0].reshape(N_PAIRS, 2 * ATT_BQ, ATT_BK)
        m = jnp.max(sc, axis=-1, keepdims=True)
        p = jnp.exp(sc - m)
        den = jnp.sum(p, axis=-1, keepdims=True)
        o = _bdot(p, vb) / den
        _store_pairs(o_ref, _merge_heads(o, ATT_BQ))
        lse = jnp.broadcast_to(m + jnp.log(den), (N_PAIRS, 2 * ATT_BQ, LANE))
        _store_pairs(lse_ref, _merge_heads(lse, ATT_BQ))

    cur = pl.BlockSpec((ATT_BQ, SWA_WIDTH), lambda r, t: (t, r))
    bspec = pl.BlockSpec((1, SWA_HEADS, ATT_BQ, ATT_BK), lambda r, t: (_tile_variant(t, nb), 0, 0, 0))
    o, lse = pl.pallas_call(
        body, name=f"att_fwd_d{dilation}", grid=(dilation, nb),
        in_specs=[cur] + _band_specs(length) * 2 + [bspec],
        out_specs=[cur, cur],
        out_shape=[jax.ShapeDtypeStruct(view, BF16), jax.ShapeDtypeStruct(view, F32)],
        compiler_params=_params("parallel", "parallel"),
    )(q.reshape(view), *([k.reshape(view)] * 3), *([v.reshape(view)] * 3), bias)
    return o.reshape(s, SWA_WIDTH), lse.reshape(s, SWA_WIDTH)


def _att_dq2(q, k, v, dop, lse, cp, bias, dilation):
    s = q.shape[0]
    length = s // dilation
    nb = length // ATT_BQ
    view = (length, dilation * SWA_WIDTH)

    def body(q_ref, kp, kc, kn, vp, vc, vn, do_ref, lse_ref, cp_ref, b_ref, dq_ref, db_ref):
        @pl.when((pl.program_id(0) == 0) & (pl.program_id(1) == 0))
        def _():
            db_ref[...] = jnp.zeros_like(db_ref)

        var = _tile_variant(pl.program_id(1), nb)
        kb, vb = _pairs(_band((kp, kc, kn))), _pairs(_band((vp, vc, vn)))
        qm = _per_head_rows(_pairs(q_ref[...]))
        dom = _per_head_rows(_pairs(do_ref[...]))
        sc = _bdot(qm, kb, BNT) + b_ref[0].reshape(N_PAIRS, 2 * ATT_BQ, ATT_BK)
        p = jnp.exp(sc - _per_head_cols(lse_ref[...]))
        ds = p * (_bdot(dom, vb, BNT) + _per_head_cols(cp_ref[...]))
        _store_pairs(dq_ref, _merge_heads(_bdot(ds, kb), ATT_BQ))
        db_ref[var] += ds.reshape(SWA_HEADS, ATT_BQ, ATT_BK)

    cur = pl.BlockSpec((ATT_BQ, SWA_WIDTH), lambda r, t: (t, r))
    bspec = pl.BlockSpec((1, SWA_HEADS, ATT_BQ, ATT_BK), lambda r, t: (_tile_variant(t, nb), 0, 0, 0))
    dq, db = pl.pallas_call(
        body, name=f"att_dq_d{dilation}", grid=(dilation, nb),
        in_specs=[cur] + _band_specs(length) * 2 + [cur, cur, cur, bspec],
        out_specs=[cur, pl.BlockSpec((4, SWA_HEADS, ATT_BQ, ATT_BK), lambda r, t: (0, 0, 0, 0))],
        out_shape=[jax.ShapeDtypeStruct(view, BF16), jax.ShapeDtypeStruct((4, SWA_HEADS, ATT_BQ, ATT_BK), F32)],
        compiler_params=_params("arbitrary", "arbitrary"),
    )(q.reshape(view), *([k.reshape(view)] * 3), *([v.reshape(view)] * 3), dop.reshape(view), lse.reshape(view),
      cp.reshape(view), bias)
    return dq.reshape(s, SWA_WIDTH), db


def _att_dkv2(q, k, v, dop, lse, cp, bias_t, dilation):
    s = q.shape[0]
    length = s // dilation
    nb = length // ATT_BQ
    view = (length, dilation * SWA_WIDTH)

    def body(k_ref, v_ref, qp, qc, qn, dp_, dc_, dn_, lp, lc, ln, cp_, cc_, cn_, b_ref, dk_ref, dv_ref):
        qm = _per_head_rows(_pairs(_band((qp, qc, qn))))
        dom = _per_head_rows(_pairs(_band((dp_, dc_, dn_))))
        lsev = _per_head_cols(_band((lp, lc, ln)))
        cpv = _per_head_cols(_band((cp_, cc_, cn_)))
        kv, vv = _pairs(k_ref[...]), _pairs(v_ref[...])
        sc = _bdot(qm, kv, BNT) + b_ref[0].reshape(N_PAIRS, 2 * ATT_BK, ATT_BQ)
        p = jnp.exp(sc - lsev)
        _store_pairs(dv_ref, _bdot(p, dom, BTN))
        ds = p * (_bdot(dom, vv, BNT) + cpv)
        _store_pairs(dk_ref, _bdot(ds, qm, BTN))

    cur = pl.BlockSpec((ATT_BQ, SWA_WIDTH), lambda r, t: (t, r))
    bspec = pl.BlockSpec((1, SWA_HEADS, ATT_BK, ATT_BQ), lambda r, t: (_tile_variant(t, nb), 0, 0, 0))
    dk, dv = pl.pallas_call(
        body, name=f"att_dkv_d{dilation}", grid=(dilation, nb),
        in_specs=[cur, cur] + _band_specs(length) * 4 + [bspec],
        out_specs=[cur, cur],
        out_shape=[jax.ShapeDtypeStruct(view, BF16)] * 2,
        compiler_params=_params("parallel", "parallel"),
    )(k.reshape(view), v.reshape(view), *([q.reshape(view)] * 3), *([dop.reshape(view)] * 3),
      *([lse.reshape(view)] * 3), *([cp.reshape(view)] * 3), bias_t)
    return dk.reshape(s, SWA_WIDTH), dv.reshape(s, SWA_WIDTH)


def _pattern_weights(lses):
    m = lses[0]
    for l in lses[1:]:
        m = jnp.maximum(m, l)
    es = [jnp.exp(l - m) for l in lses]
    den = es[0]
    for e in es[1:]:
        den = den + e
    return [e / den for e in es]


def _combine_fwd(outs, lses):
    s = outs[0].shape[0]
    tm = min(512, s)
    npat = len(outs)

    def body(*refs):
        ws = _pattern_weights([r[...] for r in refs[npat:2 * npat]])
        o = ws[0] * refs[0][...]
        for p in range(1, npat):
            o = o + ws[p] * refs[p][...]
        refs[2 * npat][...] = o.astype(BF16)

    blk = pl.BlockSpec((tm, SWA_WIDTH), lambda i: (i, 0))
    return pl.pallas_call(
        body, name="swa_combine_fwd", grid=(s // tm,), in_specs=[blk] * (2 * npat), out_specs=blk,
        out_shape=jax.ShapeDtypeStruct((s, SWA_WIDTH), BF16), compiler_params=_params("parallel"),
    )(*outs, *lses)


def _combine_bwd(d_out, outs, lses, bd):
    s = d_out.shape[0]
    tm = min(512, s)
    npat = len(outs)

    def body(*refs):
        d_ref, bd_ref = refs[0], refs[1 + 2 * npat]
        o_refs, l_refs = refs[1:1 + npat], refs[1 + npat:1 + 2 * npat]
        out_refs = refs[2 + 2 * npat:]
        ws = _pattern_weights([r[...] for r in l_refs])
        dov = d_ref[...]
        o = ws[0] * o_refs[0][...]
        for p in range(1, npat):
            o = o + ws[p] * o_refs[p][...]
        rd = _group_sum(dov * o, bd_ref[...])
        for p in range(npat):
            out_refs[p][...] = (ws[p] * dov).astype(BF16)
            out_refs[npat + p][...] = -ws[p] * rd

    blk = pl.BlockSpec((tm, SWA_WIDTH), lambda i: (i, 0))
    res = pl.pallas_call(
        body, name="swa_combine_bwd", grid=(s // tm,),
        in_specs=[blk] * (1 + 2 * npat) + [pl.BlockSpec((SWA_WIDTH, SWA_WIDTH), lambda i: (0, 0))],
        out_specs=[blk] * (2 * npat),
        out_shape=[jax.ShapeDtypeStruct((s, SWA_WIDTH), BF16)] * npat + [jax.ShapeDtypeStruct((s, SWA_WIDTH), F32)] * npat,
        compiler_params=_params("parallel"),
    )(d_out, *outs, *lses, bd)
    return res[:npat], res[npat:]


def _rel_bias_grad(dbs, buckets):
    npat = len(dbs)

    def body(*refs):
        db_refs, bk_refs, o_ref = refs[:npat], refs[npat:2 * npat], refs[2 * npat]
        row = lax.broadcasted_iota(jnp.int32, (REL_BUCKETS, LANE), 0)
        lane = lax.broadcasted_iota(jnp.int32, (REL_BUCKETS, LANE), 1)
        tiles = [[db_refs[p][0, h] + db_refs[p][1, h] + db_refs[p][2, h] + db_refs[p][3, h] for h in range(SWA_HEADS)]
                 for p in range(npat)]
        bks = [r[...] for r in bk_refs]

        def one_bucket(b, acc):
            for h in range(SWA_HEADS):
                tot = jnp.zeros((1, 1), F32)
                for p in range(npat):
                    sel = jnp.where(bks[p] == b, tiles[p][h], 0.0)
                    tot = tot + jnp.sum(jnp.sum(sel, axis=1, keepdims=True), axis=0, keepdims=True)
                acc = acc + jnp.where((row == b) & (lane == h), tot, 0.0)
            return acc

        o_ref[...] = lax.fori_loop(0, REL_BUCKETS, one_bucket, jnp.zeros((REL_BUCKETS, LANE), F32))

    full4 = pl.BlockSpec((4, SWA_HEADS, ATT_BQ, ATT_BK), lambda: (0, 0, 0, 0))
    full2 = pl.BlockSpec((ATT_BQ, ATT_BK), lambda: (0, 0))
    return pl.pallas_call(
        body, name="rel_bias_grad", in_specs=[full4] * npat + [full2] * npat,
        out_specs=pl.BlockSpec((REL_BUCKETS, LANE), lambda: (0, 0)),
        out_shape=jax.ShapeDtypeStruct((REL_BUCKETS, LANE), F32),
        compiler_params=pltpu.CompilerParams(vmem_limit_bytes=V7X_VMEM_LIMIT_BYTES),
    )(*dbs, *buckets)


def _swa_forward(p_pad, qw_row, kw_row, rel_bias, bd):
    q, k, v = _swa_pre_fwd(p_pad, qw_row, kw_row, bd)
    outs, lses = [], []
    for _, dil in DILATION_PATTERNS:
        o, lse = _att_fwd2(q, k, v, _bias_tiles(rel_bias, dil, True), dil)
        outs.append(o)
        lses.append(lse)
    return _combine_fwd(outs, lses), (q, k, v, outs, lses)


def _swa_backward(d_out, p_pad, qw_row, kw_row, rel_bias, bd, saved):
    q, k, v, outs, lses = saved
    dops, cps = _combine_bwd(d_out, outs, lses, bd)
    dqs, dks, dvs, dbs, buckets = [], [], [], [], []
    for p, (_, dil) in enumerate(DILATION_PATTERNS):
        dq, db = _att_dq2(q, k, v, dops[p], lses[p], cps[p], _bias_tiles(rel_bias, dil, True), dil)
        dk, dv = _att_dkv2(q, k, v, dops[p], lses[p], cps[p], _bias_tiles(rel_bias, dil, False), dil)
        dqs.append(dq)
        dks.append(dk)
        dvs.append(dv)
        dbs.append(db)
        buckets.append(jnp.asarray(_band_tables(dil, True)[1]))
    dp, dqw, dkw = _swa_pre_bwd(dqs, dks, dvs, p_pad, qw_row, kw_row, bd)
    return dp, dqw, dkw, _rel_bias_grad(dbs, buckets)


def _lane_row(v):
    flat = v.reshape(-1).astype(F32)
    return jnp.zeros((1, LANE), F32).at[0, :flat.shape[0]].set(flat)


W_IN_SHARD = N_IN // N_DEV
W_IN_RUNS = ((0, QKV_A, 0), (QKV_A, OFF_B, QKV_A), (OFF_B, OFF_B + 16, OFF_AB), (OFF_B + 16, N_IN, OFF_B))
W_IN_SEGMENTS = ((0, QKV_A), (OFF_Z, GDN_WIDTH), (OFF_B, 3 * SWA_WIDTH), (OFF_AB, LANE))


def _w_in_pieces(shard):
    lo, hi = shard * W_IN_SHARD, (shard + 1) * W_IN_SHARD
    out = []
    for first, last, dst in W_IN_RUNS:
        a, b = max(lo, first), min(hi, last)
        if a < b:
            out.append((a - lo, b - a, dst + a - first))
    return out


def _cols_from_slabs(w3, name):
    nd, r, wd = w3.shape
    half = nd // 2

    def body(w_ref, o_ref):
        for sh in range(half):
            o_ref[:, wd * sh:wd * (sh + 1)] = w_ref[sh]

    return pl.pallas_call(
        body, name=name, grid=(2,), in_specs=[pl.BlockSpec((half, r, wd), lambda j: (j, 0, 0))],
        out_specs=pl.BlockSpec((r, half * wd), lambda j: (0, j)),
        out_shape=jax.ShapeDtypeStruct((r, nd * wd), w3.dtype), compiler_params=_params("parallel"),
    )(w3)


def _w_in_from_slabs(w3):
    nd, r, _ = w3.shape

    def body(w_ref, o_ref):
        o_ref[:, OFF_AB:N_PAD] = jnp.zeros((r, N_PAD - OFF_AB), w3.dtype)
        for sh in range(nd):
            for src, length, dst in _w_in_pieces(sh):
                o_ref[:, dst:dst + length] = w_ref[sh, :, src:src + length]

    return pl.pallas_call(
        body, name="w_in_from_slabs", out_shape=jax.ShapeDtypeStruct((r, N_PAD), w3.dtype),
        compiler_params=pltpu.CompilerParams(vmem_limit_bytes=V7X_VMEM_LIMIT_BYTES),
    )(w3)


def _w_in_grad_slabs(parts):
    r = parts[0].shape[0]

    def body(*refs):
        o_ref = refs[len(parts)]
        for sh in range(N_DEV):
            for src, length, dst in _w_in_pieces(sh):
                seg = next(i for i, (off, width) in enumerate(W_IN_SEGMENTS) if off <= dst < off + width)
                at = dst - W_IN_SEGMENTS[seg][0]
                o_ref[sh, :, src:src + length] = refs[seg][:, at:at + length]

    return pl.pallas_call(
        body, name="w_in_grad_slabs", out_shape=jax.ShapeDtypeStruct((N_DEV, r, W_IN_SHARD), F32),
        compiler_params=pltpu.CompilerParams(vmem_limit_bytes=V7X_VMEM_LIMIT_BYTES),
    )(*parts)


def _local_step(x, tgt, wts, small):
    bd = _head_block_diag()
    conv_wt = jnp.zeros((8, QKV_A), F32).at[:CONV_WIDTH].set(small["conv_w"].T)
    alog_row, dt_row = _lane_row(small["a_log"]), _lane_row(small["dt_bias"])
    gnorm_row = small["gdn_norm_w"].reshape(1, GDN_HEAD_DIM)
    qw_row = jnp.tile(small["q_norm_w"].reshape(-1), SWA_HEADS).reshape(1, SWA_WIDTH)
    kw_row = jnp.tile(small["k_norm_w"].reshape(-1), SWA_HEADS).reshape(1, SWA_WIDTH)
    rel_bias = small["rel_bias"]
    win_pad = wts["w_in_pad"]
    wo_a, wo_b = wts["w_out"][:GDN_WIDTH], wts["w_out"][GDN_WIDTH:]

    x1, sv1 = _ffn_forward(x, small["ffn1_norm"], wts["ffn1_w_gate"], wts["ffn1_w_up"], wts["ffn1_w_down"], "ffn1")
    n2, r2 = _rms_fwd(x1, small["mix_norm"], "mix_norm")
    p_pad = _matmul([(n2, win_pad)], tm=256, tn=N_PAD, tk=D_MODEL, name="w_in")
    o_a, sva = _gdn_forward(p_pad, conv_wt, alog_row, dt_row, gnorm_row)
    o_b, svb = _swa_forward(p_pad, qw_row, kw_row, rel_bias, bd)
    x2 = _matmul([(o_a, wo_a), (o_b, wo_b)], tm=512, tn=D_MODEL, tk=GDN_WIDTH, name="w_out", res=x1)
    x3, sv2 = _ffn_forward(x2, small["ffn2_norm"], wts["ffn2_w_gate"], wts["ffn2_w_up"], wts["ffn2_w_down"], "ffn2")
    loss_row, dx3, d_final = _final_loss(x3, small["final_norm"], tgt)

    dx2, d_ffn2_norm, dwg2, dwu2, dwd2 = _ffn_backward(
        dx3, x2, small["ffn2_norm"], wts["ffn2_w_gate"], wts["ffn2_w_up"], wts["ffn2_w_down"], sv2, "ffn2")
    d_oa = _matmul([(dx2, wo_a)], tb=True, tm=512, tn=GDN_WIDTH, tk=D_MODEL, name="w_out_da")
    d_ob = _matmul([(dx2, wo_b)], tb=True, tm=512, tn=SWA_WIDTH, tk=D_MODEL, name="w_out_db")
    dwo_a = _matmul([(o_a, dx2)], ta=True, tm=GDN_WIDTH, tn=D_MODEL, tk=2048, name="w_out_dwa")
    dwo_b = _matmul([(o_b, dx2)], ta=True, tm=SWA_WIDTH, tn=D_MODEL, tk=2048, name="w_out_dwb")
    dp_qkv, dz, dp_ab, dconv, gate_sums, d_gnorm = _gdn_backward(d_oa, p_pad, conv_wt, alog_row, dt_row, gnorm_row, sva)
    dp_b, dqw, dkw, d_rel = _swa_backward(d_ob, p_pad, qw_row, kw_row, rel_bias, bd, svb)
    segs = [(dp_qkv, 0, QKV_A), (dz, OFF_Z, GDN_WIDTH), (dp_b, OFF_B, 3 * SWA_WIDTH), (dp_ab, OFF_AB, LANE)]
    dn2 = None
    dwin_parts = []
    for i, (dseg, off, width) in enumerate(segs):
        dwin_parts.append(_matmul([(n2, dseg)], ta=True, tm=512, tn=width, tk=2048, name=f"w_in_dw{i}"))
        dn2 = _matmul([(dseg, win_pad[:, off:off + width])], tb=True, tm=512, tn=D_MODEL, tk=width,
                      name=f"w_in_dn{i}", res=dn2)
    dx1, d_mix_norm = _rms_bwd(dn2, x1, r2, small["mix_norm"], dx2, "mix_dnorm")
    dx, d_ffn1_norm, dwg1, dwu1, dwd1 = _ffn_backward(
        dx1, x, small["ffn1_norm"], wts["ffn1_w_gate"], wts["ffn1_w_up"], wts["ffn1_w_down"], sv1, "ffn1")

    def row_slabs(full):
        return full.reshape(N_DEV, full.shape[0] // N_DEV, full.shape[1])

    dwd1, dwd2 = row_slabs(dwd1), row_slabs(dwd2)
    grads = {
        "ffn1_norm": d_ffn1_norm, "ffn1_w_gate": dwg1, "ffn1_w_up": dwu1, "ffn1_w_down": dwd1,
        "mix_norm": d_mix_norm, "w_in": _w_in_grad_slabs(dwin_parts), "conv_w": dconv[:CONV_WIDTH].T,
        "a_log": gate_sums[0, :8].reshape(2, GDN_HEADS), "dt_bias": gate_sums[1, :8].reshape(2, GDN_HEADS),
        "gdn_norm_w": d_gnorm, "q_norm_w": dqw.reshape(SWA_HEADS, SWA_HEAD_DIM).sum(0, keepdims=True),
        "k_norm_w": dkw.reshape(SWA_HEADS, SWA_HEAD_DIM).sum(0, keepdims=True), "rel_bias": d_rel[:, :SWA_HEADS],
        "w_out": row_slabs(jnp.concatenate([dwo_a, dwo_b], axis=0)), "ffn2_norm": d_ffn2_norm,
        "ffn2_w_gate": dwg2, "ffn2_w_up": dwu2, "ffn2_w_down": dwd2, "final_norm": d_final,
    }
    return loss_row, dx, grads


MESH_IDS = pl.DeviceIdType.MESH
ANY = pl.BlockSpec(memory_space=pl.ANY)


def _all_gather(v, name):
    m, n = v.shape

    def body(x_ref, out_ref, send_sems, recv_sems, local_sem):
        x, y, c = lax.axis_index("x"), lax.axis_index("y"), lax.axis_index("c")
        me, sibling = (x, y, c), (x, y, 1 - c)
        chips = [(1 - x, y), (x, 1 - y), (1 - x, 1 - y)]

        def rows(px, py, pc):
            return out_ref.at[pl.ds((4 * px + 2 * py + pc) * m, m), :]

        def copy(k, block, to, src=None):
            return pltpu.make_async_remote_copy(
                src_ref=rows(*block) if src is None else src, dst_ref=rows(*block),
                send_sem=send_sems.at[k], recv_sem=recv_sems.at[k], device_id=to, device_id_type=MESH_IDS)

        mine = pltpu.make_async_copy(x_ref, rows(*me), local_sem)
        mine.start()
        first = [copy(0, me, sibling, src=x_ref)]
        first += [copy(1 + j, me, (*chip, c), src=x_ref) for j, chip in enumerate(chips)]
        for cp in first:
            cp.start()
        passed = [copy(4 + j, (*chip, c), sibling) for j, chip in enumerate(chips)]
        for j, chip in enumerate(chips):
            copy(1 + j, (*chip, c), me).wait_recv()
            passed[j].start()
        copy(0, sibling, me).wait_recv()
        for j, chip in enumerate(chips):
            copy(4 + j, (*chip, 1 - c), me).wait_recv()
        for cp in first + passed:
            cp.wait_send()
        mine.wait()

    return pl.pallas_call(
        body, name=name, in_specs=[ANY], out_specs=ANY,
        out_shape=jax.ShapeDtypeStruct((N_DEV * m, n), v.dtype),
        scratch_shapes=[pltpu.SemaphoreType.DMA((7,)), pltpu.SemaphoreType.DMA((7,)), pltpu.SemaphoreType.DMA],
        compiler_params=pltpu.CompilerParams(vmem_limit_bytes=V7X_VMEM_LIMIT_BYTES),
    )(v)


def _sibling_swap(v, name):
    def body(v_ref, out_ref, send_sem, recv_sem):
        x, y, c = lax.axis_index("x"), lax.axis_index("y"), lax.axis_index("c")
        cp = pltpu.make_async_remote_copy(src_ref=v_ref, dst_ref=out_ref, send_sem=send_sem, recv_sem=recv_sem,
                                          device_id=(x, y, 1 - c), device_id_type=MESH_IDS)
        cp.start()
        cp.wait()

    return pl.pallas_call(
        body, name=name, in_specs=[ANY], out_specs=ANY, out_shape=jax.ShapeDtypeStruct(v.shape, v.dtype),
        scratch_shapes=[pltpu.SemaphoreType.DMA, pltpu.SemaphoreType.DMA],
        compiler_params=pltpu.CompilerParams(vmem_limit_bytes=V7X_VMEM_LIMIT_BYTES),
    )(v)


def _chip_exchange(t, name):
    def body(t_ref, out_ref, send_sems, recv_sems, local_sem):
        x, y, c = lax.axis_index("x"), lax.axis_index("y"), lax.axis_index("c")
        mine = 2 * x + y
        chips = [(1 - x, y), (x, 1 - y), (1 - x, 1 - y)]
        own = pltpu.make_async_copy(t_ref.at[mine], out_ref.at[mine], local_sem)
        own.start()
        copies = [pltpu.make_async_remote_copy(
            src_ref=t_ref.at[2 * px + py], dst_ref=out_ref.at[mine], send_sem=send_sems.at[j], recv_sem=recv_sems.at[j],
            device_id=(px, py, c), device_id_type=MESH_IDS) for j, (px, py) in enumerate(chips)]
        for cp in copies:
            cp.start()
        for j, (px, py) in enumerate(chips):
            pltpu.make_async_remote_copy(
                src_ref=t_ref.at[mine], dst_ref=out_ref.at[2 * px + py], send_sem=send_sems.at[j],
                recv_sem=recv_sems.at[j], device_id=(px, py, c), device_id_type=MESH_IDS).wait_recv()
        for cp in copies:
            cp.wait_send()
        own.wait()

    return pl.pallas_call(
        body, name=name, in_specs=[ANY], out_specs=ANY, out_shape=jax.ShapeDtypeStruct(t.shape, t.dtype),
        scratch_shapes=[pltpu.SemaphoreType.DMA((3,)), pltpu.SemaphoreType.DMA((3,)), pltpu.SemaphoreType.DMA],
        compiler_params=pltpu.CompilerParams(vmem_limit_bytes=V7X_VMEM_LIMIT_BYTES),
    )(t)


def _adamw(parts, w, m, v, name):
    nparts, r, n = parts.shape
    tr = r
    for cand in (256, 176, 128, 104, 64, 8):
        if r % cand == 0:
            tr = cand
            break
    bc1 = 1.0 - ADAM_B1 ** ADAM_STEP
    bc2 = 1.0 - ADAM_B2 ** ADAM_STEP

    def body(p_ref, w_ref, m_ref, v_ref, g_ref, d_ref, nm_ref, nv_ref):
        g = p_ref[0].astype(F32)
        for k in range(1, nparts):
            g = g + p_ref[k].astype(F32)
        mn = ADAM_B1 * m_ref[...] + (1.0 - ADAM_B1) * g
        vn = ADAM_B2 * v_ref[...] + (1.0 - ADAM_B2) * (g * g)
        m_hat = mn / bc1
        v_hat = vn / bc2
        g_ref[...] = g
        nm_ref[...] = mn
        nv_ref[...] = vn
        d_ref[...] = -ADAM_LR * (m_hat / (jnp.sqrt(v_hat) + ADAM_EPS) + ADAM_WD * w_ref[...])

    blk = pl.BlockSpec((tr, n), lambda i: (i, 0))
    return pl.pallas_call(
        body, name=name, grid=(r // tr,),
        in_specs=[pl.BlockSpec((nparts, tr, n), lambda i: (0, i, 0)), blk, blk, blk],
        out_specs=[blk] * 4, out_shape=[jax.ShapeDtypeStruct((r, n), F32)] * 4,
        compiler_params=_params("parallel"),
    )(parts, w, m, v)


def _mesh_place():
    x, y, c = lax.axis_index("x"), lax.axis_index("y"), lax.axis_index("c")
    return x, y, c, [(1 - x, y), (x, 1 - y), (1 - x, 1 - y)]


def _all_gather_many(vs, name):
    na = len(vs)

    def body(*refs):
        x_refs, out_refs = refs[:na], refs[na:2 * na]
        send_sems, recv_sems, local_sems = refs[2 * na:]
        x, y, c, chips = _mesh_place()
        me, sibling = (x, y, c), (x, y, 1 - c)

        def slab(i, px, py, pc):
            return out_refs[i].at[4 * px + 2 * py + pc]

        def copy(i, k, block, to, src=None):
            return pltpu.make_async_remote_copy(
                src_ref=slab(i, *block) if src is None else src, dst_ref=slab(i, *block),
                send_sem=send_sems.at[i, k], recv_sem=recv_sems.at[i, k], device_id=to, device_id_type=MESH_IDS)

        mine = [pltpu.make_async_copy(x_refs[i], slab(i, *me), local_sems.at[i]) for i in range(na)]
        first = []
        for i in range(na):
            mine[i].start()
            first.append(copy(i, 0, me, sibling, src=x_refs[i]))
            first += [copy(i, 1 + j, me, (*chip, c), src=x_refs[i]) for j, chip in enumerate(chips)]
        for cp in first:
            cp.start()
        passed = []
        for j, chip in enumerate(chips):
            for i in range(na):
                copy(i, 1 + j, (*chip, c), me).wait_recv()
                passed.append(copy(i, 4 + j, (*chip, c), sibling))
                passed[-1].start()
        for i in range(na):
            copy(i, 0, sibling, me).wait_recv()
        for j, chip in enumerate(chips):
            for i in range(na):
                copy(i, 4 + j, (*chip, 1 - c), me).wait_recv()
        for cp in first + passed:
            cp.wait_send()
        for cp in mine:
            cp.wait()

    return pl.pallas_call(
        body, name=name, in_specs=[ANY] * na, out_specs=[ANY] * na,
        out_shape=[jax.ShapeDtypeStruct((N_DEV,) + v.shape, v.dtype) for v in vs],
        scratch_shapes=[pltpu.SemaphoreType.DMA((na, 7)), pltpu.SemaphoreType.DMA((na, 7)), pltpu.SemaphoreType.DMA((na,))],
        compiler_params=pltpu.CompilerParams(vmem_limit_bytes=V7X_VMEM_LIMIT_BYTES),
    )(*vs)


def _sibling_swap_many(gs, name):
    na = len(gs)

    def body(*refs):
        g_refs, out_refs = refs[:na], refs[na:2 * na]
        send_sems, recv_sems = refs[2 * na:]
        x, y, c, _ = _mesh_place()
        copies = [pltpu.make_async_remote_copy(
            src_ref=g_refs[i].at[2 * k + 1 - c], dst_ref=out_refs[i].at[k], send_sem=send_sems.at[i, k],
            recv_sem=recv_sems.at[i, k], device_id=(x, y, 1 - c), device_id_type=MESH_IDS)
            for i in range(na) for k in range(4)]
        for cp in copies:
            cp.start()
        for cp in copies:
            cp.wait()

    return pl.pallas_call(
        body, name=name, in_specs=[ANY] * na, out_specs=[ANY] * na,
        out_shape=[jax.ShapeDtypeStruct((4,) + g.shape[1:], g.dtype) for g in gs],
        scratch_shapes=[pltpu.SemaphoreType.DMA((na, 4)), pltpu.SemaphoreType.DMA((na, 4))],
        compiler_params=pltpu.CompilerParams(vmem_limit_bytes=V7X_VMEM_LIMIT_BYTES),
    )(*gs)


def _chip_sum(g, got, core, name):
    _, r, n = g.shape

    def body(c_ref, g_ref, got_ref, o_ref):
        o_ref[...] = (g_ref[...] + got_ref[...]).astype(BF16)

    return pl.pallas_call(
        body, name=name,
        grid_spec=pltpu.PrefetchScalarGridSpec(
            num_scalar_prefetch=1, grid=(4,),
            in_specs=[pl.BlockSpec((None, r, n), lambda k, c_ref: (2 * k + c_ref[0], 0, 0)),
                      pl.BlockSpec((None, r, n), lambda k, c_ref: (k, 0, 0))],
            out_specs=pl.BlockSpec((None, r, n), lambda k, c_ref: (k, 0, 0))),
        out_shape=jax.ShapeDtypeStruct((4, r, n), BF16), compiler_params=_params("parallel"),
    )(core, g, got)


def _chip_exchange_many(ts, name):
    na = len(ts)

    def body(*refs):
        t_refs, out_refs = refs[:na], refs[na:2 * na]
        send_sems, recv_sems, local_sems = refs[2 * na:]
        x, y, c, chips = _mesh_place()
        mine = 2 * x + y
        own = [pltpu.make_async_copy(t_refs[i].at[mine], out_refs[i].at[mine], local_sems.at[i]) for i in range(na)]
        for cp in own:
            cp.start()
        copies = [pltpu.make_async_remote_copy(
            src_ref=t_refs[i].at[2 * px + py], dst_ref=out_refs[i].at[mine], send_sem=send_sems.at[i, j],
            recv_sem=recv_sems.at[i, j], device_id=(px, py, c), device_id_type=MESH_IDS)
            for j, (px, py) in enumerate(chips) for i in range(na)]
        for cp in copies:
            cp.start()
        for j, (px, py) in enumerate(chips):
            for i in range(na):
                pltpu.make_async_remote_copy(
                    src_ref=t_refs[i].at[mine], dst_ref=out_refs[i].at[2 * px + py], send_sem=send_sems.at[i, j],
                    recv_sem=recv_sems.at[i, j], device_id=(px, py, c), device_id_type=MESH_IDS).wait_recv()
        for cp in copies:
            cp.wait_send()
        for cp in own:
            cp.wait()

    return pl.pallas_call(
        body, name=name, in_specs=[ANY] * na, out_specs=[ANY] * na,
        out_shape=[jax.ShapeDtypeStruct(t.shape, t.dtype) for t in ts],
        scratch_shapes=[pltpu.SemaphoreType.DMA((na, 3)), pltpu.SemaphoreType.DMA((na, 3)), pltpu.SemaphoreType.DMA((na,))],
        compiler_params=pltpu.CompilerParams(vmem_limit_bytes=V7X_VMEM_LIMIT_BYTES),
    )(*ts)


BIG = ("ffn1_w_gate", "ffn1_w_up", "ffn1_w_down", "w_in", "w_out", "ffn2_w_gate", "ffn2_w_up", "ffn2_w_down")
COL_SHARDED = ("ffn1_w_gate", "ffn1_w_up", "w_in", "ffn2_w_gate", "ffn2_w_up")
SMALL = ("ffn1_norm", "mix_norm", "a_log", "dt_bias", "gdn_norm_w", "q_norm_w", "k_norm_w", "rel_bias",
         "ffn2_norm", "final_norm")
WEIGHTS = ("ffn1_norm", "ffn1_w_gate", "ffn1_w_up", "ffn1_w_down", "mix_norm", "w_in", "conv_w", "a_log", "dt_bias",
           "gdn_norm_w", "q_norm_w", "k_norm_w", "rel_bias", "w_out", "ffn2_norm", "ffn2_w_gate", "ffn2_w_up",
           "ffn2_w_down", "final_norm")
PACK_WIDTH = 1024
PACK_ROW_MULTIPLE = 32


def _pack(arrays, width, row_multiple):
    flat = jnp.concatenate([a.reshape(-1) for a in arrays])
    rows = -(-flat.shape[0] // width)
    rows = -(-rows // row_multiple) * row_multiple
    return jnp.pad(flat, (0, rows * width - flat.shape[0])).reshape(rows, width)


def _unpack(packed, shapes):
    flat = packed.reshape(-1)
    out, pos = [], 0
    for shp in shapes:
        size = int(np.prod(shp))
        out.append(flat[pos:pos + size].reshape(shp))
        pos += size
    return out


def _blocks_of(name, full):
    if name in COL_SHARDED:
        rows, cols = full.shape
        return full.reshape(rows, N_DEV, cols // N_DEV).transpose(1, 0, 2).reshape(N_DEV, -1)
    return full.reshape(N_DEV, -1)


def _full_of(name, blocks, shard_shape):
    rows, cols = shard_shape
    if name in COL_SHARDED:
        return blocks.reshape(N_DEV, rows, cols).transpose(1, 0, 2).reshape(rows, N_DEV * cols)
    return blocks.reshape(N_DEV * rows, cols)


def kernel(x, ffn1_norm, ffn1_w_gate, ffn1_w_up, ffn1_w_down, mix_norm, w_in, conv_w, a_log, dt_bias, gdn_norm_w, q_norm_w, k_norm_w, rel_bias, w_out, ffn2_norm, ffn2_w_gate, ffn2_w_up, ffn2_w_down, final_norm, loss_target, m_ffn1_norm, m_ffn1_w_gate, m_ffn1_w_up, m_ffn1_w_down, m_mix_norm, m_w_in, m_conv_w, m_a_log, m_dt_bias, m_gdn_norm_w, m_q_norm_w, m_k_norm_w, m_rel_bias, m_w_out, m_ffn2_norm, m_ffn2_w_gate, m_ffn2_w_up, m_ffn2_w_down, m_final_norm, v_ffn1_norm, v_ffn1_w_gate, v_ffn1_w_up, v_ffn1_w_down, v_mix_norm, v_w_in, v_conv_w, v_a_log, v_dt_bias, v_gdn_norm_w, v_q_norm_w, v_k_norm_w, v_rel_bias, v_w_out, v_ffn2_norm, v_ffn2_w_gate, v_ffn2_w_up, v_ffn2_w_down, v_final_norm):
    w = dict(ffn1_norm=ffn1_norm, ffn1_w_gate=ffn1_w_gate, ffn1_w_up=ffn1_w_up, ffn1_w_down=ffn1_w_down, mix_norm=mix_norm, w_in=w_in, conv_w=conv_w, a_log=a_log, dt_bias=dt_bias, gdn_norm_w=gdn_norm_w, q_norm_w=q_norm_w, k_norm_w=k_norm_w, rel_bias=rel_bias, w_out=w_out, ffn2_norm=ffn2_norm, ffn2_w_gate=ffn2_w_gate, ffn2_w_up=ffn2_w_up, ffn2_w_down=ffn2_w_down, final_norm=final_norm)
    mom = dict(ffn1_norm=m_ffn1_norm, ffn1_w_gate=m_ffn1_w_gate, ffn1_w_up=m_ffn1_w_up, ffn1_w_down=m_ffn1_w_down, mix_norm=m_mix_norm, w_in=m_w_in, conv_w=m_conv_w, a_log=m_a_log, dt_bias=m_dt_bias, gdn_norm_w=m_gdn_norm_w, q_norm_w=m_q_norm_w, k_norm_w=m_k_norm_w, rel_bias=m_rel_bias, w_out=m_w_out, ffn2_norm=m_ffn2_norm, ffn2_w_gate=m_ffn2_w_gate, ffn2_w_up=m_ffn2_w_up, ffn2_w_down=m_ffn2_w_down, final_norm=m_final_norm)
    var = dict(ffn1_norm=v_ffn1_norm, ffn1_w_gate=v_ffn1_w_gate, ffn1_w_up=v_ffn1_w_up, ffn1_w_down=v_ffn1_w_down, mix_norm=v_mix_norm, w_in=v_w_in, conv_w=v_conv_w, a_log=v_a_log, dt_bias=v_dt_bias, gdn_norm_w=v_gdn_norm_w, q_norm_w=v_q_norm_w, k_norm_w=v_k_norm_w, rel_bias=v_rel_bias, w_out=v_w_out, ffn2_norm=v_ffn2_norm, ffn2_w_gate=v_ffn2_w_gate, ffn2_w_up=v_ffn2_w_up, ffn2_w_down=v_ffn2_w_down, final_norm=v_final_norm)
    ix, iy, ic = lax.axis_index("x"), lax.axis_index("y"), lax.axis_index("c")
    me = 4 * ix + 2 * iy + ic

    shard = {n: w[n][0] for n in BIG}

    conv_shard_shape = w["conv_w"][0].shape
    conv_elems = conv_shard_shape[0] * conv_shard_shape[1]
    gathered = _all_gather_many([shard[n].astype(BF16) for n in BIG] + [_pack([w["conv_w"][0]], LANE, 8)],
                                "gather_weights")
    slabs = dict(zip(BIG, gathered))
    wts = {}
    for n in BIG:
        if n == "w_in":
            wts["w_in_pad"] = _w_in_from_slabs(slabs[n])
        elif n in COL_SHARDED:
            wts[n] = _cols_from_slabs(slabs[n], f"{n}_cols")
        else:
            wts[n] = slabs[n].reshape(N_DEV * slabs[n].shape[1], slabs[n].shape[2])

    small = {n: w[n][0] if n not in ("rel_bias",) else w[n] for n in SMALL}
    small = {n: (a.reshape(1, -1) if n.endswith("norm") else a) for n, a in small.items()}
    conv_all = gathered[-1].reshape(N_DEV, -1)
    small["conv_w"] = conv_all[:, :conv_elems].reshape(N_DEV * conv_shard_shape[0], conv_shard_shape[1])
    loss_row, grad_x, grads = _local_step(x[0], loss_target[0], wts, small)
    loss = lax.psum(loss_row[0, 0], ("x", "y", "c"))

    gots = _sibling_swap_many([grads[n] for n in BIG], "grads_to_sibling")
    core = ic.astype(jnp.int32).reshape(1)
    sums = [_chip_sum(grads[n], got, core, f"{n}_chip_sum") for n, got in zip(BIG, gots)]
    parts = _chip_exchange_many(sums, "grads_to_chips")
    big_out = [[], [], [], []]
    for n, part in zip(BIG, parts):
        for kind, val in enumerate(_adamw(part, shard[n], mom[n][0], var[n][0], f"{n}_adamw")):
            big_out[kind].append(val)

    small_names = SMALL + ("conv_w",)
    small_shapes = [grads[n].shape for n in small_names]
    g_small = _pack([grads[n] for n in small_names], LANE, 8)
    small_rows = g_small.shape[0]
    all_small = _all_gather(g_small, "gather_small_grads").reshape(N_DEV, small_rows, LANE)
    rep_shapes = [grads[n].shape for n in SMALL]
    zero_conv = jnp.zeros(small_shapes[-1], F32)
    ws = _pack([w[n].reshape(grads[n].shape) for n in SMALL] + [zero_conv], LANE, 8)
    ms = _pack([mom[n].reshape(grads[n].shape) for n in SMALL] + [zero_conv], LANE, 8)
    vs = _pack([var[n].reshape(grads[n].shape) for n in SMALL] + [zero_conv], LANE, 8)
    small_out = [_unpack(a, small_shapes) for a in _adamw(all_small, ws, ms, vs, "adamw_small")]
    conv_g = lax.dynamic_slice_in_dim(small_out[0][-1], me * conv_shard_shape[0], conv_shard_shape[0], axis=0)
    conv_out = [_unpack(a, [conv_shard_shape])[0] for a in _adamw(
        _pack([conv_g], LANE, 8)[None], _pack([w["conv_w"][0]], LANE, 8), _pack([mom["conv_w"][0]], LANE, 8),
        _pack([var["conv_w"][0]], LANE, 8), "adamw_conv")]

    def leaf(kind, n):
        if n in BIG:
            val = big_out[kind][BIG.index(n)]
        elif n == "conv_w":
            val = conv_out[kind]
        else:
            val = small_out[kind][SMALL.index(n)]
        return val.reshape(w[n].shape)

    outs = [loss, grad_x[None]]
    for kind in range(4):
        outs += [leaf(kind, n) for n in WEIGHTS]
    return tuple(outs)
```

```python
import functools
import math

import numpy as np
import jax
import jax.numpy as jnp
from jax import lax
from jax.experimental import pallas as pl
from jax.experimental.pallas import tpu as pltpu

F32 = jnp.float32
BF16 = jnp.bfloat16

D_MODEL = 1024
D_FF = 2816
GDN_HEADS = 4
GDN_HEAD_DIM = 128
GDN_WIDTH = 512
CONV_WIDTH = 5
CHUNK = 64
SWA_HEADS = 8
SWA_HEAD_DIM = 64
SWA_WIDTH = 512
DILATION_PATTERNS = ((128, 1), (512, 4), (2048, 16))
REL_BUCKETS = 32
REL_MAX_DISTANCE = 1024
EPS = 1e-6
NEG_BIG = -1e30
N_DEV = 8

ADAM_LR = 0.001
ADAM_B1 = 0.9
ADAM_B2 = 0.999
ADAM_EPS = 1e-08
ADAM_WD = 0.01
ADAM_STEP = 10

QKV_A = 3 * GDN_WIDTH
OFF_Z = QKV_A
OFF_B = OFF_Z + GDN_WIDTH
OFF_AB = OFF_B + 3 * SWA_WIDTH
N_PAD = OFF_AB + 128
N_IN = 3600

V7X_VMEM_LIMIT_BYTES = 56 * 1024 * 1024
LANE = 128
ATT_BQ = 128
ATT_HALO = 64
CONV_ROWS = 256

NN = (((1,), (0,)), ((), ()))
NT = (((1,), (1,)), ((), ()))
TN = (((0,), (0,)), ((), ()))


def _params(*sem):
    return pltpu.CompilerParams(dimension_semantics=sem, vmem_limit_bytes=V7X_VMEM_LIMIT_BYTES)


def _dot(a, b, dn=NN):
    return lax.dot_general(a.astype(BF16), b.astype(BF16), dn, preferred_element_type=F32)


def _dot_hi(a, b, dn=NN):
    return lax.dot_general(a, b, dn, precision=lax.Precision.HIGHEST, preferred_element_type=F32)


def _sigmoid(x):
    return 1.0 / (1.0 + jnp.exp(-x))


def _matmul(pairs, *, ta=False, tb=False, out_dtype=F32, tm, tn, tk, name, res=None, alpha=None, shard_cols=None):
    a0, b0 = pairs[0]
    m = a0.shape[1] if ta else a0.shape[0]
    k = a0.shape[0] if ta else a0.shape[1]
    n = b0.shape[0] if tb else b0.shape[1]
    tm, tn, tk = min(tm, m), min(tn, n), min(tk, k)
    assert m % tm == 0 and n % tn == 0 and k % tk == 0, (name, m, n, k, tm, tn, tk)
    nk = k // tk
    npairs = len(pairs)
    dn = (((0 if ta else 1,), (1 if tb else 0,)), ((), ()))

    def body(*refs):
        ins = refs[:2 * npairs]
        pos = 2 * npairs
        r_ref = None
        if res is not None:
            r_ref = refs[pos]
            pos += 1
        o_ref, acc = refs[pos], refs[pos + 1]
        kk = pl.program_id(2)
        t = None
        for p in range(npairs):
            d = _dot(ins[2 * p][...], ins[2 * p + 1][...], dn)
            t = d if t is None else t + d

        if nk > 1:
            @pl.when(kk == 0)
            def _():
                acc[...] = t

            @pl.when((kk > 0) & (kk < nk - 1))
            def _():
                acc[...] += t

        @pl.when(kk == nk - 1)
        def _():
            r = acc[...] + t if nk > 1 else t
            if alpha is not None:
                r = r * alpha
            if r_ref is not None:
                r = r_ref[...] + r
            if shard_cols is None:
                o_ref[...] = r.astype(out_dtype)
            else:
                for sh in range(tn // shard_cols):
                    o_ref[sh] = r[:, sh * shard_cols:(sh + 1) * shard_cols].astype(out_dtype)

    a_spec = pl.BlockSpec((tk, tm), lambda i, j, kk: (kk, i)) if ta else pl.BlockSpec((tm, tk), lambda i, j, kk: (i, kk))
    b_spec = pl.BlockSpec((tn, tk), lambda i, j, kk: (j, kk)) if tb else pl.BlockSpec((tk, tn), lambda i, j, kk: (kk, j))
    o_spec = pl.BlockSpec((tm, tn), lambda i, j, kk: (i, j))
    in_specs = [a_spec, b_spec] * npairs + ([o_spec] if res is not None else [])
    args = [t for pr in pairs for t in pr] + ([res] if res is not None else [])
    out_spec, out_shape = o_spec, (m, n)
    if shard_cols is not None:
        assert res is None and tn % shard_cols == 0
        out_spec = pl.BlockSpec((tn // shard_cols, tm, shard_cols), lambda i, j, kk: (j, i, 0))
        out_shape = (n // shard_cols, m, shard_cols)
    return pl.pallas_call(
        body, name=name, grid=(m // tm, n // tn, nk), in_specs=in_specs, out_specs=out_spec,
        out_shape=jax.ShapeDtypeStruct(out_shape, out_dtype),
        scratch_shapes=[pltpu.VMEM((tm, tn) if nk > 1 else (8, LANE), F32)],
        compiler_params=_params("parallel", "parallel", "arbitrary"),
    )(*args)


def _rms_fwd(x, w, name):
    s, d = x.shape
    tm = min(512, s)

    def body(x_ref, w_ref, n_ref, r_ref):
        xv = x_ref[...]
        r = lax.rsqrt(jnp.mean(xv * xv, axis=-1, keepdims=True) + EPS)
        n_ref[...] = (xv * r * w_ref[...]).astype(BF16)
        r_ref[...] = r

    return pl.pallas_call(
        body, name=name, grid=(s // tm,),
        in_specs=[pl.BlockSpec((tm, d), lambda i: (i, 0)), pl.BlockSpec((1, d), lambda i: (0, 0))],
        out_specs=[pl.BlockSpec((tm, d), lambda i: (i, 0)), pl.BlockSpec((tm, 1), lambda i: (i, 0))],
        out_shape=[jax.ShapeDtypeStruct((s, d), BF16), jax.ShapeDtypeStruct((s, 1), F32)],
        compiler_params=_params("parallel"),
    )(x, w)


def _rms_bwd(dn, x, r, w, dres, name):
    s, d = x.shape
    tm = min(512, s)

    def body(dn_ref, x_ref, r_ref, w_ref, dres_ref, dx_ref, dw_ref):
        @pl.when(pl.program_id(0) == 0)
        def _():
            dw_ref[...] = jnp.zeros_like(dw_ref)

        rv = r_ref[...]
        xhat = x_ref[...] * rv
        g = dn_ref[...]
        t = g * w_ref[...]
        dx_ref[...] = dres_ref[...] + rv * (t - xhat * jnp.mean(t * xhat, axis=-1, keepdims=True))
        dw_ref[...] += jnp.sum(g * xhat, axis=0, keepdims=True)

    row = pl.BlockSpec((tm, d), lambda i: (i, 0))
    vec = pl.BlockSpec((1, d), lambda i: (0, 0))
    return pl.pallas_call(
        body, name=name, grid=(s // tm,),
        in_specs=[row, row, pl.BlockSpec((tm, 1), lambda i: (i, 0)), vec, row],
        out_specs=[row, vec],
        out_shape=[jax.ShapeDtypeStruct((s, d), F32), jax.ShapeDtypeStruct((1, d), F32)],
        compiler_params=_params("arbitrary"),
    )(dn, x, r, w, dres)


def _final_loss(x3, wf, tgt):
    s, d = x3.shape
    tm = min(512, s)

    def body(x_ref, w_ref, t_ref, loss_ref, dx_ref, dw_ref):
        @pl.when(pl.program_id(0) == 0)
        def _():
            dw_ref[...] = jnp.zeros_like(dw_ref)
            loss_ref[...] = jnp.zeros_like(loss_ref)

        xv = x_ref[...]
        wv = w_ref[...]
        r = lax.rsqrt(jnp.mean(xv * xv, axis=-1, keepdims=True) + EPS)
        xhat = xv * r
        e = xhat * wv - t_ref[...]
        part = 0.5 * jnp.sum(jnp.mean(e * e, axis=-1, keepdims=True), axis=0, keepdims=True)
        loss_ref[...] += jnp.broadcast_to(part, loss_ref.shape)
        dy = e * (1.0 / d)
        dw_ref[...] += jnp.sum(dy * xhat, axis=0, keepdims=True)
        t = dy * wv
        dx_ref[...] = r * (t - xhat * jnp.mean(t * xhat, axis=-1, keepdims=True))

    row = pl.BlockSpec((tm, d), lambda i: (i, 0))
    vec = pl.BlockSpec((1, d), lambda i: (0, 0))
    return pl.pallas_call(
        body, name="final_loss", grid=(s // tm,),
        in_specs=[row, vec, row],
        out_specs=[pl.BlockSpec((1, LANE), lambda i: (0, 0)), row, vec],
        out_shape=[jax.ShapeDtypeStruct((1, LANE), F32), jax.ShapeDtypeStruct((s, d), F32),
                   jax.ShapeDtypeStruct((1, d), F32)],
        compiler_params=_params("arbitrary"),
    )(x3, wf, tgt)


def _ffn_up(n, wg, wu, name):
    s, d = n.shape
    f = wg.shape[1]
    tm, tn = min(512, s), f // 2

    def body(n_ref, wg_ref, wu_ref, g_ref, u_ref, a_ref):
        nv = n_ref[...]
        g = _dot(nv, wg_ref[...])
        u = _dot(nv, wu_ref[...])
        g_ref[...] = g.astype(BF16)
        u_ref[...] = u.astype(BF16)
        a_ref[...] = (g * _sigmoid(g) * u).astype(BF16)

    o = pl.BlockSpec((tm, tn), lambda j, i: (i, j))
    wspec = pl.BlockSpec((d, tn), lambda j, i: (0, j))
    return pl.pallas_call(
        body, name=name, grid=(f // tn, s // tm),
        in_specs=[pl.BlockSpec((tm, d), lambda j, i: (i, 0)), wspec, wspec],
        out_specs=[o, o, o],
        out_shape=[jax.ShapeDtypeStruct((s, f), BF16)] * 3,
        compiler_params=_params("parallel", "parallel"),
    )(n, wg, wu)


def _ffn_dact(dx, wd, g, u, name):
    s, d = dx.shape
    f = wd.shape[0]
    tm, tn = min(512, s), f // 2

    def body(dx_ref, wd_ref, g_ref, u_ref, dg_ref, du_ref):
        da = 0.5 * _dot(dx_ref[...], wd_ref[...], NT)
        gv = g_ref[...].astype(F32)
        sg = _sigmoid(gv)
        du_ref[...] = (da * gv * sg).astype(BF16)
        dg_ref[...] = (da * u_ref[...].astype(F32) * (sg * (1.0 + gv * (1.0 - sg)))).astype(BF16)

    o = pl.BlockSpec((tm, tn), lambda j, i: (i, j))
    return pl.pallas_call(
        body, name=name, grid=(f // tn, s // tm),
        in_specs=[pl.BlockSpec((tm, d), lambda j, i: (i, 0)), pl.BlockSpec((tn, d), lambda j, i: (j, 0)), o, o],
        out_specs=[o, o],
        out_shape=[jax.ShapeDtypeStruct((s, f), BF16), jax.ShapeDtypeStruct((s, f), BF16)],
        compiler_params=_params("parallel", "parallel"),
    )(dx, wd, g, u)


def _ffn_forward(x, norm_w, wg, wu, wd, tag):
    n, r = _rms_fwd(x, norm_w, f"{tag}_norm")
    g, u, a = _ffn_up(n, wg, wu, f"{tag}_up")
    y = _matmul([(a, wd)], tm=512, tn=1024, tk=wd.shape[0], name=f"{tag}_down", res=x, alpha=0.5)
    return y, (n, r, g, u, a)


def _ffn_backward(dy, x, norm_w, wg, wu, wd, saved, tag, dw_dtype=F32):
    n, r, g, u, a = saved
    dwd = _matmul([(a, dy)], ta=True, tm=1408, tn=1024, tk=2048, name=f"{tag}_dwd", alpha=0.5,
                  out_dtype=dw_dtype)
    dg, du = _ffn_dact(dy, wd, g, u, f"{tag}_dact")
    cols = wg.shape[1] // N_DEV
    dwg = _matmul([(n, dg)], ta=True, tm=512, tn=1408, tk=4096, name=f"{tag}_dwg", shard_cols=cols,
                  out_dtype=dw_dtype)
    dwu = _matmul([(n, du)], ta=True, tm=512, tn=1408, tk=4096, name=f"{tag}_dwu", shard_cols=cols,
                  out_dtype=dw_dtype)
    dn = _matmul([(dg, wg), (du, wu)], tb=True, tm=512, tn=1024, tk=wg.shape[1], name=f"{tag}_dn")
    dx, dnorm = _rms_bwd(dn, x, r, norm_w, dy, f"{tag}_dnorm")
    return dx, dnorm, dwg, dwu, dwd


Q_SCALE = GDN_HEAD_DIM ** -0.5
CONV_HALO = 8


def _conv_taps(win, w_ref, rows, sign):
    n = rows + 2 * CONV_HALO
    acc = None
    for t in range(CONV_WIDTH):
        o = sign * (t - CONV_WIDTH // 2)
        sh = win if o == 0 else pltpu.roll(win, (-o) % n, 0)
        term = sh[CONV_HALO:CONV_HALO + rows] * w_ref[t:t + 1, :]
        acc = term if acc is None else acc + term
    return acc


def _gdn_conv_fwd(p_pad, conv_wt):
    s = p_pad.shape[0]
    rows = min(CONV_ROWS, s)
    nblk = QKV_A // LANE

    def body(p_ref, w_ref, c_ref, y_ref, pad):
        j = pl.program_id(0)
        zeros = jnp.zeros((CONV_HALO, LANE), F32)
        pad[0:CONV_HALO, :] = zeros
        pad[CONV_HALO + s:2 * CONV_HALO + s, :] = zeros
        pad[CONV_HALO:CONV_HALO + s, :] = p_ref[...]

        def chunk(ci, carry):
            b = pl.multiple_of(ci * rows, rows)
            win = pad[pl.ds(b, rows + 2 * CONV_HALO), :]
            c = _conv_taps(win, w_ref, rows, 1)
            c_ref[pl.ds(b, rows), :] = c
            act = c * _sigmoid(c)
            nrm = lax.rsqrt(jnp.sum(act * act, axis=-1, keepdims=True) + EPS)
            mult = jnp.where(j < GDN_HEADS, nrm * Q_SCALE, jnp.where(j < 2 * GDN_HEADS, nrm, 1.0))
            y_ref[pl.ds(b, rows), :] = act * mult
            return carry

        lax.fori_loop(0, s // rows, chunk, 0)

    col = pl.BlockSpec((s, LANE), lambda j: (0, j))
    return pl.pallas_call(
        body, name="gdn_conv_fwd", grid=(nblk,),
        in_specs=[col, pl.BlockSpec((8, LANE), lambda j: (0, j))],
        out_specs=[col, col],
        out_shape=[jax.ShapeDtypeStruct((s, QKV_A), F32), jax.ShapeDtypeStruct((s, QKV_A), F32)],
        scratch_shapes=[pltpu.VMEM((s + 2 * CONV_HALO, LANE), F32)],
        compiler_params=_params("parallel"),
    )(p_pad, conv_wt)


def _gdn_conv_bwd(dy_f, dy_r, c_pre, p_pad, conv_wt):
    s = p_pad.shape[0]
    rows = min(CONV_ROWS, s)
    nblk = QKV_A // LANE

    def body(dyf_ref, dyr_ref, c_ref, p_ref, w_ref, dp_ref, dw_ref, ppad, dcpad):
        j = pl.program_id(0)
        zeros = jnp.zeros((CONV_HALO, LANE), F32)
        for buf in (ppad, dcpad):
            buf[0:CONV_HALO, :] = zeros
            buf[CONV_HALO + s:2 * CONV_HALO + s, :] = zeros
        ppad[CONV_HALO:CONV_HALO + s, :] = p_ref[...]

        def act_bwd(ci, carry):
            b = pl.multiple_of(ci * rows, rows)
            c = c_ref[pl.ds(b, rows), :]
            g = dyf_ref[pl.ds(b, rows), :] + dyr_ref[pl.ds(b, rows), :]
            sg = _sigmoid(c)
            act = c * sg
            nrm = lax.rsqrt(jnp.sum(act * act, axis=-1, keepdims=True) + EPS)
            yh = act * nrm
            scale = jnp.where(j < GDN_HEADS, Q_SCALE, 1.0)
            dact_qk = (scale * nrm) * (g - yh * jnp.sum(g * yh, axis=-1, keepdims=True))
            dact = jnp.where(j < 2 * GDN_HEADS, dact_qk, g)
            dcpad[pl.ds(pl.multiple_of(b + CONV_HALO, CONV_HALO), rows), :] = dact * (sg * (1.0 + c * (1.0 - sg)))
            return carry

        lax.fori_loop(0, s // rows, act_bwd, 0)
        tap = lax.broadcasted_iota(jnp.int32, (8, LANE), 0)

        def taps_bwd(ci, dw):
            b = pl.multiple_of(ci * rows, rows)
            dcw = dcpad[pl.ds(b, rows + 2 * CONV_HALO), :]
            dp_ref[pl.ds(b, rows), :] = _conv_taps(dcw, w_ref, rows, -1)
            pw = ppad[pl.ds(b, rows + 2 * CONV_HALO), :]
            dc = dcw[CONV_HALO:CONV_HALO + rows]
            n = rows + 2 * CONV_HALO
            for t in range(CONV_WIDTH):
                o = t - CONV_WIDTH // 2
                sh = pw if o == 0 else pltpu.roll(pw, (-o) % n, 0)
                row = jnp.sum(dc * sh[CONV_HALO:CONV_HALO + rows], axis=0, keepdims=True)
                dw = dw + jnp.where(tap == t, row, 0.0)
            return dw

        dw_ref[...] = lax.fori_loop(0, s // rows, taps_bwd, jnp.zeros((8, LANE), F32))

    col = pl.BlockSpec((s, LANE), lambda j: (0, j))
    wspec = pl.BlockSpec((8, LANE), lambda j: (0, j))
    return pl.pallas_call(
        body, name="gdn_conv_bwd", grid=(nblk,),
        in_specs=[col, col, col, col, wspec],
        out_specs=[col, wspec],
        out_shape=[jax.ShapeDtypeStruct((s, QKV_A), F32), jax.ShapeDtypeStruct((8, QKV_A), F32)],
        scratch_shapes=[pltpu.VMEM((s + 2 * CONV_HALO, LANE), F32), pltpu.VMEM((s + 2 * CONV_HALO, LANE), F32)],
        compiler_params=_params("parallel"),
    )(dy_f, dy_r, c_pre, p_pad, conv_wt)


def _softplus(x):
    return jnp.maximum(x, 0.0) + jnp.log(1.0 + jnp.exp(-jnp.abs(x)))


def _gdn_gates_fwd(p_pad, alog_row, dt_row):
    s = p_pad.shape[0]
    tm = min(1024, s)

    def body(p_ref, al_ref, dt_ref, o_ref):
        x = p_ref[...]
        lane = lax.broadcasted_iota(jnp.int32, x.shape, 1)
        g = -jnp.exp(al_ref[...]) * _softplus(x + dt_ref[...])
        o_ref[...] = jnp.where(lane < 8, g, jnp.where(lane < 16, _sigmoid(x), 0.0))

    vec = pl.BlockSpec((1, LANE), lambda i: (0, 0))
    return pl.pallas_call(
        body, name="gdn_gates_fwd", grid=(s // tm,),
        in_specs=[pl.BlockSpec((tm, LANE), lambda i: (i, OFF_AB // LANE)), vec, vec],
        out_specs=pl.BlockSpec((tm, LANE), lambda i: (i, 0)),
        out_shape=jax.ShapeDtypeStruct((s, LANE), F32),
        compiler_params=_params("parallel"),
    )(p_pad, alog_row, dt_row)


def _gdn_gates_bwd(dgb_f, dgb_r, p_pad, gb, alog_row, dt_row):
    s = p_pad.shape[0]
    tm = min(1024, s)

    def body(df_ref, dr_ref, p_ref, gb_ref, al_ref, dt_ref, dp_ref, sum_ref):
        @pl.when(pl.program_id(0) == 0)
        def _():
            sum_ref[...] = jnp.zeros_like(sum_ref)

        x = p_ref[...]
        gbv = gb_ref[...]
        dgb = df_ref[...] + dr_ref[...]
        lane = lax.broadcasted_iota(jnp.int32, x.shape, 1)
        da = dgb * (-jnp.exp(al_ref[...])) * _sigmoid(x + dt_ref[...])
        db = dgb * gbv * (1.0 - gbv)
        dp_ref[...] = jnp.where(lane < 8, da, jnp.where(lane < 16, db, 0.0))
        row = lax.broadcasted_iota(jnp.int32, (8, LANE), 0)
        lane8 = lax.broadcasted_iota(jnp.int32, (8, LANE), 1)
        d_alog = jnp.sum(dgb * gbv, axis=0, keepdims=True)
        d_dt = jnp.sum(da, axis=0, keepdims=True)
        upd = jnp.where(row == 0, d_alog, jnp.where(row == 1, d_dt, 0.0))
        sum_ref[...] += jnp.where(lane8 < 8, upd, 0.0)

    vec = pl.BlockSpec((1, LANE), lambda i: (0, 0))
    blk = pl.BlockSpec((tm, LANE), lambda i: (i, 0))
    return pl.pallas_call(
        body, name="gdn_gates_bwd", grid=(s // tm,),
        in_specs=[blk, blk, pl.BlockSpec((tm, LANE), lambda i: (i, OFF_AB // LANE)), blk, vec, vec],
        out_specs=[blk, pl.BlockSpec((8, LANE), lambda i: (0, 0))],
        out_shape=[jax.ShapeDtypeStruct((s, LANE), F32), jax.ShapeDtypeStruct((8, LANE), F32)],
        compiler_params=_params("arbitrary"),
    )(dgb_f, dgb_r, p_pad, gb, alog_row, dt_row)


def _chunk_masks(rev):
    row = lax.broadcasted_iota(jnp.int32, (CHUNK, CHUNK), 0)
    col = lax.broadcasted_iota(jnp.int32, (CHUNK, CHUNK), 1)
    le = (col >= row) if rev else (col <= row)
    strict = (col > row) if rev else (col < row)
    return le, strict, row == col


def _chunk_common(q, k, v, g, beta, gc, masks):
    le, strict, eye = masks
    gc_row = _dot_hi(jnp.ones((CHUNK, CHUNK), F32), jnp.where(eye, gc, 0.0))
    decay = jnp.where(le, jnp.exp(jnp.where(le, gc - gc_row, 0.0)), 0.0)
    eg = jnp.exp(gc)
    gl = jnp.sum(g, axis=0, keepdims=True)
    kb = k * beta
    vb = v * beta
    kbeg = kb * eg
    lm = jnp.where(strict, _dot(kb, k, NT) * decay, 0.0)
    intra = _dot(q, k, NT) * decay
    qg = q * eg
    edec = jnp.exp(gl - gc)
    kdec = k * edec
    return dict(decay=decay, eg=eg, gl=gl, kb=kb, vb=vb, kbeg=kbeg, lm=lm, intra=intra, qg=qg, edec=edec, kdec=kdec)


def _unit_lower_inverse(lm, eye):
    x = -lm
    t = eye.astype(F32) + x
    p = x
    for _ in range(5):
        p = _dot_hi(p, p)
        t = t + _dot_hi(t, p)
    return t


def _gate_lanes(rev, h):
    d = 1 if rev else 0
    return d * GDN_HEADS + h, 8 + d * GDN_HEADS + h


def _delta_fwd(y, gb, rev):
    s = y.shape[0]
    nc = s // CHUNK
    hd = GDN_HEAD_DIM

    def chunk_of(n):
        return nc - 1 - n if rev else n

    def body(q_ref, k_ref, v_ref, gb_ref, o_ref, s_all, t_all, state):
        @pl.when(pl.program_id(0) == 0)
        def _():
            state[...] = jnp.zeros_like(state)

        masks = _chunk_masks(rev)
        gbv = gb_ref[...]
        gcm = _dot_hi(masks[0].astype(F32), gbv)
        for h in range(GDN_HEADS):
            gi, bi = _gate_lanes(rev, h)
            sl = slice(h * hd, (h + 1) * hd)
            q, k, v = q_ref[:, sl], k_ref[:, sl], v_ref[:, sl]
            g, beta, gc = gbv[:, gi:gi + 1], gbv[:, bi:bi + 1], gcm[:, gi:gi + 1]
            cm = _chunk_common(q, k, v, g, beta, gc, masks)
            tinv = _unit_lower_inverse(cm["lm"], masks[2])
            u = _dot(tinv, cm["vb"])
            w = _dot(tinv, cm["kbeg"])
            st = state[h]
            v_new = u - _dot(w, st)
            o_ref[:, sl] = _dot(cm["qg"], st) + _dot(cm["intra"], v_new)
            s_all[0, h] = st
            t_all[0, h] = tinv
            state[h] = st * jnp.exp(cm["gl"]) + _dot(cm["kdec"], v_new, TN)

    def col(j):
        return pl.BlockSpec((CHUNK, GDN_WIDTH), lambda n: (chunk_of(n), j))

    return pl.pallas_call(
        body, name="delta_fwd_r" if rev else "delta_fwd_f", grid=(nc,),
        in_specs=[col(0), col(1), col(2), pl.BlockSpec((CHUNK, LANE), lambda n: (chunk_of(n), 0))],
        out_specs=[pl.BlockSpec((CHUNK, GDN_WIDTH), lambda n: (chunk_of(n), 0)),
                   pl.BlockSpec((1, GDN_HEADS, hd, hd), lambda n: (chunk_of(n), 0, 0, 0)),
                   pl.BlockSpec((1, GDN_HEADS, CHUNK, CHUNK), lambda n: (chunk_of(n), 0, 0, 0))],
        out_shape=[jax.ShapeDtypeStruct((s, GDN_WIDTH), F32),
                   jax.ShapeDtypeStruct((nc, GDN_HEADS, hd, hd), F32),
                   jax.ShapeDtypeStruct((nc, GDN_HEADS, CHUNK, CHUNK), F32)],
        scratch_shapes=[pltpu.VMEM((GDN_HEADS, hd, hd), F32)],
        compiler_params=_params("arbitrary"),
    )(y, y, y, gb)


def _delta_bwd(y, gb, do, s_all, t_all, rev):
    s = y.shape[0]
    nc = s // CHUNK
    hd = GDN_HEAD_DIM

    def chunk_of(n):
        return n if rev else nc - 1 - n

    def body(q_ref, k_ref, v_ref, gb_ref, do_ref, s_ref, t_ref, dy_ref, dgb_ref, dstate):
        @pl.when(pl.program_id(0) == 0)
        def _():
            dstate[...] = jnp.zeros_like(dstate)

        masks = _chunk_masks(rev)
        le, strict, _ = masks
        le_t = _chunk_masks(not rev)[0].astype(F32)
        gbv = gb_ref[...]
        gcm = _dot_hi(le.astype(F32), gbv)
        lane = lax.broadcasted_iota(jnp.int32, (CHUNK, LANE), 1)
        ones_cl = jnp.ones((CHUNK, LANE), F32)
        dgc_tile = jnp.zeros((CHUNK, LANE), F32)
        rest_tile = jnp.zeros((CHUNK, LANE), F32)
        for h in range(GDN_HEADS):
            gi, bi = _gate_lanes(rev, h)
            sl = slice(h * hd, (h + 1) * hd)
            q, k, v = q_ref[:, sl], k_ref[:, sl], v_ref[:, sl]
            g, beta, gc = gbv[:, gi:gi + 1], gbv[:, bi:bi + 1], gcm[:, gi:gi + 1]
            cm = _chunk_common(q, k, v, g, beta, gc, masks)
            tinv = t_ref[0, h]
            st = s_ref[0, h]
            ds_out = dstate[h]
            dov = do_ref[:, sl]
            u = _dot(tinv, cm["vb"])
            w = _dot(tinv, cm["kbeg"])
            v_new = u - _dot(w, st)
            egl = jnp.exp(cm["gl"])
            d_qg = _dot(dov, st, NT)
            d_intra = _dot(dov, v_new, NT)
            dv_new = _dot(cm["intra"], dov, TN) + _dot(cm["kdec"], ds_out)
            d_kdec = _dot(v_new, ds_out, NT)
            dstate[h] = _dot(cm["qg"], dov, TN) + egl * ds_out - _dot(w, dv_new, TN)
            dgl = egl * jnp.sum(jnp.sum(st * ds_out, axis=1, keepdims=True), axis=0, keepdims=True)
            dw = -_dot(dv_new, st, NT)
            dvb = _dot(tinv, dv_new, TN)
            dkbeg = _dot(tinv, dw, TN)
            dlm = jnp.where(strict, -(_dot(dvb, u, NT) + _dot(dkbeg, w, NT)), 0.0)
            d_a = dlm * cm["decay"]
            d_qk = d_intra * cm["decay"]
            e = dlm * cm["lm"] + d_intra * cm["intra"]
            dgc = jnp.sum(e, axis=1, keepdims=True) - _dot_hi(e, ones_cl, TN)[:, 0:1]
            dkb = _dot(d_a, k) + dkbeg * cm["eg"]
            dk = _dot(d_a, cm["kb"], TN) + _dot(d_qk, q, TN)
            dq = _dot(d_qk, k) + d_qg * cm["eg"]
            dgc = dgc + jnp.sum(d_qg * cm["qg"], axis=1, keepdims=True)
            dgc = dgc + jnp.sum(dkbeg * cm["kbeg"], axis=1, keepdims=True)
            tdec = jnp.sum(d_kdec * cm["kdec"], axis=1, keepdims=True)
            dk = dk + d_kdec * cm["edec"] + dkb * beta
            dgc = dgc - tdec
            dgl = dgl + jnp.sum(tdec, axis=0, keepdims=True)
            dbeta = jnp.sum(dvb * v, axis=1, keepdims=True) + jnp.sum(dkb * k, axis=1, keepdims=True)
            dy_ref[:, h * hd:(h + 1) * hd] = dq
            dy_ref[:, GDN_WIDTH + h * hd:GDN_WIDTH + (h + 1) * hd] = dk
            dy_ref[:, 2 * GDN_WIDTH + h * hd:2 * GDN_WIDTH + (h + 1) * hd] = dvb * beta
            dgc_tile = dgc_tile + jnp.where(lane == gi, dgc, 0.0)
            rest_tile = rest_tile + jnp.where(lane == gi, dgl, 0.0) + jnp.where(lane == bi, dbeta, 0.0)
        dgb_ref[...] = _dot_hi(le_t, dgc_tile) + rest_tile

    def col(j):
        return pl.BlockSpec((CHUNK, GDN_WIDTH), lambda n: (chunk_of(n), j))

    first = pl.BlockSpec((CHUNK, GDN_WIDTH), lambda n: (chunk_of(n), 0))
    return pl.pallas_call(
        body, name="delta_bwd_r" if rev else "delta_bwd_f", grid=(nc,),
        in_specs=[col(0), col(1), col(2), pl.BlockSpec((CHUNK, LANE), lambda n: (chunk_of(n), 0)), first,
                  pl.BlockSpec((1, GDN_HEADS, hd, hd), lambda n: (chunk_of(n), 0, 0, 0)),
                  pl.BlockSpec((1, GDN_HEADS, CHUNK, CHUNK), lambda n: (chunk_of(n), 0, 0, 0))],
        out_specs=[pl.BlockSpec((CHUNK, QKV_A), lambda n: (chunk_of(n), 0)),
                   pl.BlockSpec((CHUNK, LANE), lambda n: (chunk_of(n), 0))],
        out_shape=[jax.ShapeDtypeStruct((s, QKV_A), F32), jax.ShapeDtypeStruct((s, LANE), F32)],
        scratch_shapes=[pltpu.VMEM((GDN_HEADS, hd, hd), F32)],
        compiler_params=_params("arbitrary"),
    )(y, y, y, gb, do, s_all, t_all)


BNN = (((2,), (1,)), ((0,), (0,)))
BNT = (((2,), (2,)), ((0,), (0,)))
BTN = (((1,), (1,)), ((0,), (0,)))
NB = 2 * GDN_HEADS


def _bdot(a, b, dn=BNN):
    return lax.dot_general(a.astype(BF16), b.astype(BF16), dn, preferred_element_type=F32)


def _dot3(a, b, dn):
    ah = a.astype(BF16)
    al = (a - ah.astype(F32)).astype(BF16)
    bh = b.astype(BF16)
    bl = (b - bh.astype(F32)).astype(BF16)

    def d(x, y):
        return lax.dot_general(x, y, dn, preferred_element_type=F32)

    return d(ah, bh) + d(ah, bl) + d(al, bh)


def _both(f_val, r_val):
    return jnp.stack([f_val] * GDN_HEADS + [r_val] * GDN_HEADS)


def _heads(ref_f, ref_r):
    hd = GDN_HEAD_DIM
    return jnp.stack([ref_f[:, h * hd:(h + 1) * hd] for h in range(GDN_HEADS)]
                     + [ref_r[:, h * hd:(h + 1) * hd] for h in range(GDN_HEADS)])


def _gate_cols(tile_f, tile_r, base):
    return jnp.stack([tile_f[:, base + h:base + h + 1] for h in range(GDN_HEADS)]
                     + [tile_r[:, base + GDN_HEADS + h:base + GDN_HEADS + h + 1] for h in range(GDN_HEADS)])


def _chunk_common2(q, k, v, gbf, gbr):
    mf, mr = _chunk_masks(False), _chunk_masks(True)
    le, strict = _both(mf[0], mr[0]), _both(mf[1], mr[1])
    eye = mf[2]
    gcm_f = _dot3(mf[0].astype(F32), gbf, NN)
    gcm_r = _dot3(mr[0].astype(F32), gbr, NN)
    g, beta, gc = _gate_cols(gbf, gbr, 0), _gate_cols(gbf, gbr, 8), _gate_cols(gcm_f, gcm_r, 0)
    gc_row = _dot3(jnp.ones((NB, CHUNK, CHUNK), F32), jnp.where(eye[None], gc, 0.0), BNN)
    decay = jnp.where(le, jnp.exp(jnp.where(le, gc - gc_row, 0.0)), 0.0)
    eg = jnp.exp(gc)
    gl = jnp.sum(g, axis=1, keepdims=True)
    kb = k * beta
    vb = v * beta
    kbeg = kb * eg
    lm = jnp.where(strict, _bdot(kb, k, BNT) * decay, 0.0)
    intra = _bdot(q, k, BNT) * decay
    edec = jnp.exp(gl - gc)
    return dict(strict=strict, eye=eye, beta=beta, decay=decay, eg=eg, gl=gl, kb=kb, vb=vb, kbeg=kbeg,
                lm=lm, intra=intra, qg=q * eg, edec=edec, kdec=k * edec)


def _unit_triangular_inverse(lm, eye):
    x = -lm
    t = eye[None].astype(F32) + x
    p = x
    for _ in range(5):
        p = _dot3(p, p, BNN)
        t = t + _dot3(t, p, BNN)
    return t


def _delta_fwd2(y, gb, gather=()):
    s = y.shape[0]
    nc = s // CHUNK
    hd = GDN_HEAD_DIM
    na = len(gather)

    def body(*refs):
        qf, kf, vf, gf, qr, kr, vr, gr = refs[:8]
        of_ref, or_ref, sf_all, sr_all, tf_all, tr_all = refs[8 + na:14 + na]
        state = refs[14 + 2 * na]
        step = pl.program_id(0)

        @pl.when(step == 0)
        def _():
            state[...] = jnp.zeros_like(state)

        if na:
            start, forward, finish = _gather_phases(refs[8:8 + na], refs[14 + na:14 + 2 * na], *refs[15 + 2 * na:])
            pl.when(step == 0)(start)
            pl.when(step == nc // 2)(forward)
            pl.when(step == nc - 1)(finish)

        q, k, v = _heads(qf, qr), _heads(kf, kr), _heads(vf, vr)
        cm = _chunk_common2(q, k, v, gf[...], gr[...])
        tinv = _unit_triangular_inverse(cm["lm"], cm["eye"])
        u = _bdot(tinv, cm["vb"])
        w = _bdot(tinv, cm["kbeg"])
        st = state[...]
        v_new = u - _bdot(w, st)
        o = _bdot(cm["qg"], st) + _bdot(cm["intra"], v_new)
        state[...] = st * jnp.exp(cm["gl"]) + _bdot(cm["kdec"], v_new, BTN)
        for h in range(GDN_HEADS):
            of_ref[:, h * hd:(h + 1) * hd] = o[h]
            or_ref[:, h * hd:(h + 1) * hd] = o[GDN_HEADS + h]
        sf_all[0] = st[:GDN_HEADS]
        sr_all[0] = st[GDN_HEADS:]
        tf_all[0] = tinv[:GDN_HEADS]
        tr_all[0] = tinv[GDN_HEADS:]

    def col(j, rev):
        return pl.BlockSpec((CHUNK, GDN_WIDTH), (lambda n: (nc - 1 - n, j)) if rev else (lambda n: (n, j)))

    def gate(rev):
        return pl.BlockSpec((CHUNK, LANE), (lambda n: (nc - 1 - n, 0)) if rev else (lambda n: (n, 0)))

    def per_chunk(d1, d2, rev):
        return pl.BlockSpec((1, GDN_HEADS, d1, d2), (lambda n: (nc - 1 - n, 0, 0, 0)) if rev else (lambda n: (n, 0, 0, 0)))

    assert na == 0 or nc >= 4
    res = pl.pallas_call(
        body, name="delta_fwd", grid=(nc,),
        in_specs=[col(0, False), col(1, False), col(2, False), gate(False), col(0, True), col(1, True), col(2, True), gate(True)]
        + [ANY] * na,
        out_specs=[col(0, False), col(0, True), per_chunk(hd, hd, False), per_chunk(hd, hd, True),
                   per_chunk(CHUNK, CHUNK, False), per_chunk(CHUNK, CHUNK, True)] + [ANY] * na,
        out_shape=[jax.ShapeDtypeStruct((s, GDN_WIDTH), F32)] * 2 + [jax.ShapeDtypeStruct((nc, GDN_HEADS, hd, hd), F32)] * 2
        + [jax.ShapeDtypeStruct((nc, GDN_HEADS, CHUNK, CHUNK), F32)] * 2
        + [jax.ShapeDtypeStruct((N_DEV,) + v.shape, v.dtype) for v in gather],
        scratch_shapes=[pltpu.VMEM((NB, hd, hd), F32)] + (_gather_semaphores(na) if na else []),
        compiler_params=_params("arbitrary"),
    )(y, y, y, gb, y, y, y, gb, *gather)
    return res[:6], res[6:]


def _delta_bwd2(y, gb, do, sf_all, sr_all, tf_all, tr_all, scatter=()):
    s = y.shape[0]
    nc = s // CHUNK
    hd = GDN_HEAD_DIM
    na = len(scatter)

    def body(*refs):
        qf, kf, vf, gf, dof, sf, tf, qr, kr, vr, gr, dor, sr, tr = refs[:14]
        dyf_ref, dyr_ref, dgf_ref, dgr_ref = refs[14 + na:18 + na]
        dstate = refs[18 + 2 * na]
        step = pl.program_id(0)

        @pl.when(step == 0)
        def _():
            dstate[...] = jnp.zeros_like(dstate)

        if na:
            start, finish = _scatter_phases(refs[14:14 + na], refs[18 + na:18 + 2 * na], *refs[19 + 2 * na:])
            pl.when(step == 0)(start)
            pl.when(step == nc - 1)(finish)

        q, k, v, dov = _heads(qf, qr), _heads(kf, kr), _heads(vf, vr), _heads(dof, dor)
        cm = _chunk_common2(q, k, v, gf[...], gr[...])
        tinv = jnp.concatenate([tf[0], tr[0]], axis=0)
        st = jnp.concatenate([sf[0], sr[0]], axis=0)
        ds_out = dstate[...]
        decay, lm, intra, qg, kdec, kbeg, eg, kb, beta = (
            cm[n] for n in ("decay", "lm", "intra", "qg", "kdec", "kbeg", "eg", "kb", "beta"))
        u = _bdot(tinv, cm["vb"])
        w = _bdot(tinv, kbeg)
        v_new = u - _bdot(w, st)
        egl = jnp.exp(cm["gl"])
        d_qg = _bdot(dov, st, BNT)
        d_intra = _bdot(dov, v_new, BNT)
        dv_new = _bdot(intra, dov, BTN) + _bdot(kdec, ds_out)
        d_kdec = _bdot(v_new, ds_out, BNT)
        dstate[...] = _bdot(qg, dov, BTN) + egl * ds_out - _bdot(w, dv_new, BTN)
        dgl = egl * jnp.sum(jnp.sum(st * ds_out, axis=2, keepdims=True), axis=1, keepdims=True)
        dw = -_bdot(dv_new, st, BNT)
        dvb = _bdot(tinv, dv_new, BTN)
        dkbeg = _bdot(tinv, dw, BTN)
        dlm = jnp.where(cm["strict"], -(_bdot(dvb, u, BNT) + _bdot(dkbeg, w, BNT)), 0.0)
        d_a = dlm * decay
        d_qk = d_intra * decay
        e = dlm * lm + d_intra * intra
        colsum = _dot3(e, jnp.ones((NB, CHUNK, LANE), F32), BTN)[:, :, 0:1]
        dgc = jnp.sum(e, axis=2, keepdims=True) - colsum
        dkb = _bdot(d_a, k) + dkbeg * eg
        dk = _bdot(d_a, kb, BTN) + _bdot(d_qk, q, BTN)
        dq = _bdot(d_qk, k) + d_qg * eg
        dgc = dgc + jnp.sum(d_qg * qg, axis=2, keepdims=True) + jnp.sum(dkbeg * kbeg, axis=2, keepdims=True)
        tdec = jnp.sum(d_kdec * kdec, axis=2, keepdims=True)
        dk = dk + d_kdec * cm["edec"] + dkb * beta
        dgc = dgc - tdec
        dgl = dgl + jnp.sum(tdec, axis=1, keepdims=True)
        dbeta = jnp.sum(dvb * v, axis=2, keepdims=True) + jnp.sum(dkb * k, axis=2, keepdims=True)
        dv = dvb * beta
        lane = lax.broadcasted_iota(jnp.int32, (CHUNK, LANE), 1)
        for rev, dy_ref, dg_ref in ((False, dyf_ref, dgf_ref), (True, dyr_ref, dgr_ref)):
            dgc_tile = jnp.zeros((CHUNK, LANE), F32)
            rest = jnp.zeros((CHUNK, LANE), F32)
            for h in range(GDN_HEADS):
                b = (GDN_HEADS if rev else 0) + h
                gi, bi = _gate_lanes(rev, h)
                dgc_tile = dgc_tile + jnp.where(lane == gi, dgc[b], 0.0)
                rest = rest + jnp.where(lane == gi, dgl[b], 0.0) + jnp.where(lane == bi, dbeta[b], 0.0)
                dy_ref[:, h * hd:(h + 1) * hd] = dq[b]
                dy_ref[:, GDN_WIDTH + h * hd:GDN_WIDTH + (h + 1) * hd] = dk[b]
                dy_ref[:, 2 * GDN_WIDTH + h * hd:2 * GDN_WIDTH + (h + 1) * hd] = dv[b]
            le_t = _chunk_masks(not rev)[0].astype(F32)
            dg_ref[...] = _dot3(le_t, dgc_tile, NN) + rest

    def col(j, rev):
        return pl.BlockSpec((CHUNK, GDN_WIDTH), (lambda n: (n, j)) if rev else (lambda n: (nc - 1 - n, j)))

    def wide(width, rev):
        return pl.BlockSpec((CHUNK, width), (lambda n: (n, 0)) if rev else (lambda n: (nc - 1 - n, 0)))

    def per_chunk(d1, d2, rev):
        return pl.BlockSpec((1, GDN_HEADS, d1, d2), (lambda n: (n, 0, 0, 0)) if rev else (lambda n: (nc - 1 - n, 0, 0, 0)))

    def side(rev):
        return [col(0, rev), col(1, rev), col(2, rev), wide(LANE, rev), wide(GDN_WIDTH, rev), per_chunk(hd, hd, rev),
                per_chunk(CHUNK, CHUNK, rev)]

    assert na == 0 or nc >= 2
    res = pl.pallas_call(
        body, name="delta_bwd", grid=(nc,),
        in_specs=side(False) + side(True) + [ANY] * na,
        out_specs=[wide(QKV_A, False), wide(QKV_A, True), wide(LANE, False), wide(LANE, True)] + [ANY] * na,
        out_shape=[jax.ShapeDtypeStruct((s, QKV_A), F32)] * 2 + [jax.ShapeDtypeStruct((s, LANE), F32)] * 2
        + [jax.ShapeDtypeStruct(g.shape, g.dtype) for g in scatter],
        scratch_shapes=[pltpu.VMEM((NB, hd, hd), F32)] + (_gather_semaphores(na) if na else []),
        compiler_params=_params("arbitrary"),
    )(y, y, y, gb, do, sf_all, tf_all, y, y, y, gb, do, sr_all, tr_all, *scatter)
    return res[:4], res[4:]


def _gdn_post_fwd(o_f, o_r, p_pad, norm_row):
    s = o_f.shape[0]
    tm = min(512, s)
    hd = GDN_HEAD_DIM

    def body(of_ref, or_ref, z_ref, w_ref, out_ref, osum_ref):
        o = of_ref[...] + or_ref[...]
        osum_ref[...] = o
        z = z_ref[...]
        gate = z * _sigmoid(z)
        for h in range(GDN_HEADS):
            sl = slice(h * hd, (h + 1) * hd)
            oh = o[:, sl]
            r = lax.rsqrt(jnp.mean(oh * oh, axis=-1, keepdims=True) + EPS)
            out_ref[:, sl] = (oh * r * w_ref[...] * gate[:, sl]).astype(BF16)

    blk = pl.BlockSpec((tm, GDN_WIDTH), lambda i: (i, 0))
    return pl.pallas_call(
        body, name="gdn_post_fwd", grid=(s // tm,),
        in_specs=[blk, blk, pl.BlockSpec((tm, GDN_WIDTH), lambda i: (i, OFF_Z // GDN_WIDTH)),
                  pl.BlockSpec((1, hd), lambda i: (0, 0))],
        out_specs=[blk, blk],
        out_shape=[jax.ShapeDtypeStruct((s, GDN_WIDTH), BF16), jax.ShapeDtypeStruct((s, GDN_WIDTH), F32)],
        compiler_params=_params("parallel"),
    )(o_f, o_r, p_pad, norm_row)


def _gdn_post_bwd(d_out, o_sum, p_pad, norm_row):
    s = o_sum.shape[0]
    tm = min(512, s)
    hd = GDN_HEAD_DIM

    def body(d_ref, o_ref, z_ref, w_ref, do_ref, dz_ref, dw_ref):
        @pl.when(pl.program_id(0) == 0)
        def _():
            dw_ref[...] = jnp.zeros_like(dw_ref)

        z = z_ref[...]
        sg = _sigmoid(z)
        gate = z * sg
        dgate = sg * (1.0 + z * (1.0 - sg))
        wv = w_ref[...]
        dw = jnp.zeros((1, hd), F32)
        for h in range(GDN_HEADS):
            sl = slice(h * hd, (h + 1) * hd)
            oh = o_ref[:, sl]
            dh = d_ref[:, sl]
            r = lax.rsqrt(jnp.mean(oh * oh, axis=-1, keepdims=True) + EPS)
            ohat = oh * r
            dz_ref[:, sl] = dh * ohat * wv * dgate[:, sl]
            drn = dh * gate[:, sl]
            t = drn * wv
            do_ref[:, sl] = r * (t - ohat * jnp.mean(t * ohat, axis=-1, keepdims=True))
            dw = dw + jnp.sum(drn * ohat, axis=0, keepdims=True)
        dw_ref[...] += dw

    blk = pl.BlockSpec((tm, GDN_WIDTH), lambda i: (i, 0))
    vec = pl.BlockSpec((1, hd), lambda i: (0, 0))
    return pl.pallas_call(
        body, name="gdn_post_bwd", grid=(s // tm,),
        in_specs=[blk, blk, pl.BlockSpec((tm, GDN_WIDTH), lambda i: (i, OFF_Z // GDN_WIDTH)), vec],
        out_specs=[blk, blk, vec],
        out_shape=[jax.ShapeDtypeStruct((s, GDN_WIDTH), F32), jax.ShapeDtypeStruct((s, GDN_WIDTH), F32),
                   jax.ShapeDtypeStruct((1, hd), F32)],
        compiler_params=_params("arbitrary"),
    )(d_out, o_sum, p_pad, norm_row)


def _add2(a, b, name):
    s, w = a.shape
    tm = next(t for t in (1024, 640, 512, 256, 128, 64, 8) if s % t == 0)

    def body(a_ref, b_ref, o_ref):
        o_ref[...] = a_ref[...] + b_ref[...]

    blk = pl.BlockSpec((tm, w), lambda i: (i, 0))
    return pl.pallas_call(body, name=name, grid=(s // tm,), in_specs=[blk, blk], out_specs=blk,
                          out_shape=jax.ShapeDtypeStruct((s, w), F32), compiler_params=_params("parallel"))(a, b)


def _gdn_forward(p_pad, conv_wt, alog_row, dt_row, norm_row, gather=()):
    c_pre, y = _gdn_conv_fwd(p_pad, conv_wt)
    gb = _gdn_gates_fwd(p_pad, alog_row, dt_row)
    (o_f, o_r, s_f, s_r, t_f, t_r), gathered = _delta_fwd2(y, gb, gather)
    out, o_sum = _gdn_post_fwd(o_f, o_r, p_pad, norm_row)
    return out, (c_pre, y, gb, s_f, t_f, s_r, t_r, o_sum), gathered


def _gdn_backward(d_out, p_pad, conv_wt, alog_row, dt_row, norm_row, saved, scatter=()):
    c_pre, y, gb, s_f, t_f, s_r, t_r, o_sum = saved
    do, dz, dnorm = _gdn_post_bwd(d_out, o_sum, p_pad, norm_row)
    (dy_f, dy_r, dgb_f, dgb_r), received = _delta_bwd2(y, gb, do, s_f, s_r, t_f, t_r, scatter)
    dp_qkv, dconv = _gdn_conv_bwd(dy_f, dy_r, c_pre, p_pad, conv_wt)
    dp_ab, gate_sums = _gdn_gates_bwd(dgb_f, dgb_r, p_pad, gb, alog_row, dt_row)
    return dp_qkv, dz, dp_ab, dconv, gate_sums, dnorm, received


ATT_BK = ATT_BQ + 2 * ATT_HALO
SWA_SCALE = SWA_HEAD_DIM ** -0.5


def _t5_bucket(rel):
    nb = REL_BUCKETS // 2
    bucket = (rel > 0).astype(np.int32) * nb
    n = np.abs(rel)
    max_exact = nb // 2
    large = max_exact + (np.log(np.maximum(n, 1) / max_exact)
                         / math.log(REL_MAX_DISTANCE / max_exact) * (nb - max_exact)).astype(np.int32)
    large = np.minimum(large, nb - 1)
    return (bucket + np.where(n < max_exact, n, large)).astype(np.int32)


def _band_tables(dilation, queries_are_rows_of_block):
    blk = np.arange(ATT_BQ)
    band = np.arange(ATT_BK) - ATT_HALO
    if queries_are_rows_of_block:
        rel = band[None, :] - blk[:, None]
        band_idx = np.broadcast_to(np.arange(ATT_BK)[None, :], rel.shape)
    else:
        rel = blk[None, :] - band[:, None]
        band_idx = np.broadcast_to(np.arange(ATT_BK)[:, None], rel.shape)
    base = np.abs(rel) <= ATT_HALO
    not_prev = band_idx >= ATT_HALO
    not_next = band_idx < ATT_HALO + ATT_BQ
    valid = np.stack([base & not_prev, base, base & not_next, base & not_prev & not_next])
    return valid, _t5_bucket(rel * dilation)


def _bias_tiles(rel_bias, dilation, queries_are_rows_of_block):
    valid, bucket = _band_tables(dilation, queries_are_rows_of_block)
    onehot = (jnp.asarray(bucket.reshape(-1, 1)) == jnp.arange(REL_BUCKETS, dtype=jnp.int32)[None, :]).astype(F32)
    rb = jnp.dot(onehot, rel_bias.astype(F32), precision=lax.Precision.HIGHEST)
    rb = rb.T.reshape((SWA_HEADS,) + bucket.shape)
    return jnp.where(valid[:, None], rb[None], NEG_BIG).astype(F32)


def _group_sum(x, bd):
    hi = x.astype(BF16)
    lo = (x - hi.astype(F32)).astype(BF16)
    return jnp.dot(hi, bd, preferred_element_type=F32) + jnp.dot(lo, bd, preferred_element_type=F32)


def _head_block_diag():
    idx = np.arange(SWA_WIDTH) // SWA_HEAD_DIM
    return jnp.asarray(idx[:, None] == idx[None, :], BF16)


def _swa_pre_fwd(p_pad, qw_row, kw_row, bd):
    s = p_pad.shape[0]
    tm = min(512, s)
    inv = 1.0 / SWA_HEAD_DIM

    def body(q_ref, k_ref, v_ref, qw_ref, kw_ref, bd_ref, qo_ref, ko_ref, vo_ref):
        bdv = bd_ref[...]
        q = q_ref[...]
        k = k_ref[...]
        rq = lax.rsqrt(_group_sum(q * q, bdv) * inv + EPS)
        rk = lax.rsqrt(_group_sum(k * k, bdv) * inv + EPS)
        qo_ref[...] = (q * rq * qw_ref[...] * SWA_SCALE).astype(BF16)
        ko_ref[...] = (k * rk * kw_ref[...]).astype(BF16)
        vo_ref[...] = v_ref[...].astype(BF16)

    base = OFF_B // SWA_WIDTH
    blk = pl.BlockSpec((tm, SWA_WIDTH), lambda i: (i, 0))
    vec = pl.BlockSpec((1, SWA_WIDTH), lambda i: (0, 0))
    return pl.pallas_call(
        body, name="swa_pre_fwd", grid=(s // tm,),
        in_specs=[pl.BlockSpec((tm, SWA_WIDTH), lambda i: (i, base)), pl.BlockSpec((tm, SWA_WIDTH), lambda i: (i, base + 1)),
                  pl.BlockSpec((tm, SWA_WIDTH), lambda i: (i, base + 2)), vec, vec,
                  pl.BlockSpec((SWA_WIDTH, SWA_WIDTH), lambda i: (0, 0))],
        out_specs=[blk, blk, blk],
        out_shape=[jax.ShapeDtypeStruct((s, SWA_WIDTH), BF16)] * 3,
        compiler_params=_params("parallel"),
    )(p_pad, p_pad, p_pad, qw_row, kw_row, bd)


def _swa_pre_bwd(dqs, dks, dvs, p_pad, qw_row, kw_row, bd):
    s = p_pad.shape[0]
    tm = min(256, s)
    inv = 1.0 / SWA_HEAD_DIM
    npat = len(dqs)

    def body(*refs):
        dq_refs, dk_refs, dv_refs = refs[:npat], refs[npat:2 * npat], refs[2 * npat:3 * npat]
        q_ref, k_ref, qw_ref, kw_ref, bd_ref, dp_ref, dqw_ref, dkw_ref = refs[3 * npat:]

        @pl.when(pl.program_id(0) == 0)
        def _():
            dqw_ref[...] = jnp.zeros_like(dqw_ref)
            dkw_ref[...] = jnp.zeros_like(dkw_ref)

        bdv = bd_ref[...]

        def norm_bwd(x, g, w, scale):
            r = lax.rsqrt(_group_sum(x * x, bdv) * inv + EPS)
            xhat = x * r
            t = g * w * scale
            dx = r * (t - xhat * (_group_sum(t * xhat, bdv) * inv))
            return dx, jnp.sum(g * scale * xhat, axis=0, keepdims=True)

        def total(rs):
            t = rs[0][...].astype(F32)
            for r in rs[1:]:
                t = t + r[...].astype(F32)
            return t

        dq, dqw = norm_bwd(q_ref[...], total(dq_refs), qw_ref[...], SWA_SCALE)
        dk, dkw = norm_bwd(k_ref[...], total(dk_refs), kw_ref[...], 1.0)
        dp_ref[:, 0:SWA_WIDTH] = dq
        dp_ref[:, SWA_WIDTH:2 * SWA_WIDTH] = dk
        dp_ref[:, 2 * SWA_WIDTH:3 * SWA_WIDTH] = total(dv_refs)
        dqw_ref[...] += dqw
        dkw_ref[...] += dkw

    base = OFF_B // SWA_WIDTH
    blk = pl.BlockSpec((tm, SWA_WIDTH), lambda i: (i, 0))
    vec = pl.BlockSpec((1, SWA_WIDTH), lambda i: (0, 0))
    return pl.pallas_call(
        body, name="swa_pre_bwd", grid=(s // tm,),
        in_specs=[blk] * (3 * npat) + [pl.BlockSpec((tm, SWA_WIDTH), lambda i: (i, base)),
                                      pl.BlockSpec((tm, SWA_WIDTH), lambda i: (i, base + 1)), vec, vec,
                                      pl.BlockSpec((SWA_WIDTH, SWA_WIDTH), lambda i: (0, 0))],
        out_specs=[pl.BlockSpec((tm, 3 * SWA_WIDTH), lambda i: (i, 0)), vec, vec],
        out_shape=[jax.ShapeDtypeStruct((s, 3 * SWA_WIDTH), F32), jax.ShapeDtypeStruct((1, SWA_WIDTH), F32),
                   jax.ShapeDtypeStruct((1, SWA_WIDTH), F32)],
        compiler_params=_params("arbitrary"),
    )(*dqs, *dks, *dvs, p_pad, p_pad, qw_row, kw_row, bd)


def _band_specs(length):
    per = ATT_BQ // ATT_HALO
    last = length // ATT_HALO - 1
    prev = pl.BlockSpec((ATT_HALO, SWA_WIDTH), lambda r, t: (jnp.maximum(t * per - 1, 0), r))
    cur = pl.BlockSpec((ATT_BQ, SWA_WIDTH), lambda r, t: (t, r))
    nxt = pl.BlockSpec((ATT_HALO, SWA_WIDTH), lambda r, t: (jnp.minimum((t + 1) * per, last), r))
    return [prev, cur, nxt]


def _tile_variant(t, nb):
    if nb == 1:
        return 3
    return jnp.where(t == 0, 0, jnp.where(t == nb - 1, 2, 1))


def _band(refs):
    return jnp.concatenate([r[...] for r in refs], axis=0)


def _att_fwd(q, k, v, bias, dilation):
    s = q.shape[0]
    length = s // dilation
    nb = length // ATT_BQ
    view = (length, dilation * SWA_WIDTH)
    hd = SWA_HEAD_DIM

    def body(q_ref, kp, kc, kn, vp, vc, vn, b_ref, o_ref, lse_ref):
        kb, vb = _band((kp, kc, kn)), _band((vp, vc, vn))
        qv = q_ref[...]
        for h in range(SWA_HEADS):
            sl = slice(h * hd, (h + 1) * hd)
            sc = _dot(qv[:, sl], kb[:, sl], NT) + b_ref[0, h]
            m = jnp.max(sc, axis=-1, keepdims=True)
            p = jnp.exp(sc - m)
            den = jnp.sum(p, axis=-1, keepdims=True)
            o_ref[:, sl] = _dot(p, vb[:, sl]) / den
            lse_ref[:, sl] = jnp.broadcast_to(m + jnp.log(den), (ATT_BQ, hd))

    cur = pl.BlockSpec((ATT_BQ, SWA_WIDTH), lambda r, t: (t, r))
    bspec = pl.BlockSpec((1, SWA_HEADS, ATT_BQ, ATT_BK), lambda r, t: (_tile_variant(t, nb), 0, 0, 0))
    o, lse = pl.pallas_call(
        body, name=f"att_fwd_d{dilation}", grid=(dilation, nb),
        in_specs=[cur] + _band_specs(length) * 2 + [bspec],
        out_specs=[cur, cur],
        out_shape=[jax.ShapeDtypeStruct(view, F32)] * 2,
        compiler_params=_params("parallel", "parallel"),
    )(q.reshape(view), *([k.reshape(view)] * 3), *([v.reshape(view)] * 3), bias)
    return o.reshape(s, SWA_WIDTH), lse.reshape(s, SWA_WIDTH)


def _att_dq(q, k, v, dop, lse, cp, bias, dilation):
    s = q.shape[0]
    length = s // dilation
    nb = length // ATT_BQ
    view = (length, dilation * SWA_WIDTH)
    hd = SWA_HEAD_DIM

    def body(q_ref, kp, kc, kn, vp, vc, vn, do_ref, lse_ref, cp_ref, b_ref, dq_ref, db_ref):
        @pl.when((pl.program_id(0) == 0) & (pl.program_id(1) == 0))
        def _():
            db_ref[...] = jnp.zeros_like(db_ref)

        var = _tile_variant(pl.program_id(1), nb)
        kb, vb = _band((kp, kc, kn)), _band((vp, vc, vn))
        qv, dov, lsev, cpv = q_ref[...], do_ref[...], lse_ref[...], cp_ref[...]
        for h in range(SWA_HEADS):
            sl = slice(h * hd, (h + 1) * hd)
            sc = _dot(qv[:, sl], kb[:, sl], NT) + b_ref[0, h]
            p = jnp.exp(sc - lsev[:, h * hd:h * hd + 1])
            dp = _dot(dov[:, sl], vb[:, sl], NT)
            ds = p * (dp + cpv[:, h * hd:h * hd + 1])
            dq_ref[:, sl] = _dot(ds, kb[:, sl])
            db_ref[var, h] += ds

    cur = pl.BlockSpec((ATT_BQ, SWA_WIDTH), lambda r, t: (t, r))
    bspec = pl.BlockSpec((1, SWA_HEADS, ATT_BQ, ATT_BK), lambda r, t: (_tile_variant(t, nb), 0, 0, 0))
    dq, db = pl.pallas_call(
        body, name=f"att_dq_d{dilation}", grid=(dilation, nb),
        in_specs=[cur] + _band_specs(length) * 2 + [cur, cur, cur, bspec],
        out_specs=[cur, pl.BlockSpec((4, SWA_HEADS, ATT_BQ, ATT_BK), lambda r, t: (0, 0, 0, 0))],
        out_shape=[jax.ShapeDtypeStruct(view, F32), jax.ShapeDtypeStruct((4, SWA_HEADS, ATT_BQ, ATT_BK), F32)],
        compiler_params=_params("arbitrary", "arbitrary"),
    )(q.reshape(view), *([k.reshape(view)] * 3), *([v.reshape(view)] * 3), dop.reshape(view), lse.reshape(view),
      cp.reshape(view), bias)
    return dq.reshape(s, SWA_WIDTH), db


def _att_dkv(q, k, v, dop, lse, cp, bias_t, dilation):
    s = q.shape[0]
    length = s // dilation
    nb = length // ATT_BQ
    view = (length, dilation * SWA_WIDTH)
    hd = SWA_HEAD_DIM

    def body(k_ref, v_ref, qp, qc, qn, dp_, dc_, dn_, lp, lc, ln, cp_, cc_, cn_, b_ref, dk_ref, dv_ref):
        qb, dob = _band((qp, qc, qn)), _band((dp_, dc_, dn_))
        lseb, cpb = _band((lp, lc, ln)), _band((cp_, cc_, cn_))
        kv, vv = k_ref[...], v_ref[...]
        for h in range(SWA_HEADS):
            sl = slice(h * hd, (h + 1) * hd)
            sc = _dot(qb[:, sl], kv[:, sl], NT) + b_ref[0, h]
            p = jnp.exp(sc - lseb[:, h * hd:h * hd + 1])
            dv_ref[:, sl] = _dot(p, dob[:, sl], TN)
            dp = _dot(dob[:, sl], vv[:, sl], NT)
            ds = p * (dp + cpb[:, h * hd:h * hd + 1])
            dk_ref[:, sl] = _dot(ds, qb[:, sl], TN)

    cur = pl.BlockSpec((ATT_BQ, SWA_WIDTH), lambda r, t: (t, r))
    bspec = pl.BlockSpec((1, SWA_HEADS, ATT_BK, ATT_BQ), lambda r, t: (_tile_variant(t, nb), 0, 0, 0))
    dk, dv = pl.pallas_call(
        body, name=f"att_dkv_d{dilation}", grid=(dilation, nb),
        in_specs=[cur, cur] + _band_specs(length) * 4 + [bspec],
        out_specs=[cur, cur],
        out_shape=[jax.ShapeDtypeStruct(view, F32)] * 2,
        compiler_params=_params("parallel", "parallel"),
    )(k.reshape(view), v.reshape(view), *([q.reshape(view)] * 3), *([dop.reshape(view)] * 3),
      *([lse.reshape(view)] * 3), *([cp.reshape(view)] * 3), bias_t)
    return dk.reshape(s, SWA_WIDTH), dv.reshape(s, SWA_WIDTH)


N_PAIRS = SWA_HEADS // 2


def _pairs(x):
    return jnp.stack([x[:, LANE * p:LANE * (p + 1)] for p in range(N_PAIRS)])


def _per_head_rows(x):
    first = lax.broadcasted_iota(jnp.int32, x.shape, 2) < SWA_HEAD_DIM
    zero = jnp.zeros_like(x)
    return jnp.concatenate([jnp.where(first, x, zero), jnp.where(first, zero, x)], axis=1)


def _per_head_cols(x):
    return jnp.stack([jnp.concatenate([x[:, LANE * p:LANE * p + 1],
                                       x[:, LANE * p + SWA_HEAD_DIM:LANE * p + SWA_HEAD_DIM + 1]], axis=0)
                      for p in range(N_PAIRS)])


def _merge_heads(x, rows):
    first = lax.broadcasted_iota(jnp.int32, (N_PAIRS, rows, LANE), 2) < SWA_HEAD_DIM
    return jnp.where(first, x[:, :rows], x[:, rows:])


def _store_pairs(ref, x):
    for p in range(N_PAIRS):
        ref[:, LANE * p:LANE * (p + 1)] = x[p].astype(ref.dtype)


def _att_fwd2(q, k, v, bias, dilation):
    s = q.shape[0]
    length = s // dilation
    nb = length // ATT_BQ
    view = (length, dilation * SWA_WIDTH)

    def body(q_ref, kp, kc, kn, vp, vc, vn, b_ref, o_ref, lse_ref):
        kb, vb = _pairs(_band((kp, kc, kn))), _pairs(_band((vp, vc, vn)))
        qm = _per_head_rows(_pairs(q_ref[...]))
        sc = _bdot(qm, kb, BNT) + b_ref[0].reshape(N_PAIRS, 2 * ATT_BQ, ATT_BK)
        m = jnp.max(sc, axis=-1, keepdims=True)
        p = jnp.exp(sc - m)
        den = jnp.sum(p, axis=-1, keepdims=True)
        o = _bdot(p, vb) / den
        _store_pairs(o_ref, _merge_heads(o, ATT_BQ))
        lse = jnp.broadcast_to(m + jnp.log(den), (N_PAIRS, 2 * ATT_BQ, LANE))
        _store_pairs(lse_ref, _merge_heads(lse, ATT_BQ))

    cur = pl.BlockSpec((ATT_BQ, SWA_WIDTH), lambda r, t: (t, r))
    bspec = pl.BlockSpec((1, SWA_HEADS, ATT_BQ, ATT_BK), lambda r, t: (_tile_variant(t, nb), 0, 0, 0))
    o, lse = pl.pallas_call(
        body, name=f"att_fwd_d{dilation}", grid=(dilation, nb),
        in_specs=[cur] + _band_specs(length) * 2 + [bspec],
        out_specs=[cur, cur],
        out_shape=[jax.ShapeDtypeStruct(view, BF16), jax.ShapeDtypeStruct(view, F32)],
        compiler_params=_params("parallel", "parallel"),
    )(q.reshape(view), *([k.reshape(view)] * 3), *([v.reshape(view)] * 3), bias)
    return o.reshape(s, SWA_WIDTH), lse.reshape(s, SWA_WIDTH)


def _att_dq2(q, k, v, dop, lse, cp, bias, dilation):
    s = q.shape[0]
    length = s // dilation
    nb = length // ATT_BQ
    view = (length, dilation * SWA_WIDTH)

    def body(q_ref, kp, kc, kn, vp, vc, vn, do_ref, lse_ref, cp_ref, b_ref, dq_ref, db_ref):
        @pl.when((pl.program_id(0) == 0) & (pl.program_id(1) == 0))
        def _():
            db_ref[...] = jnp.zeros_like(db_ref)

        var = _tile_variant(pl.program_id(1), nb)
        kb, vb = _pairs(_band((kp, kc, kn))), _pairs(_band((vp, vc, vn)))
        qm = _per_head_rows(_pairs(q_ref[...]))
        dom = _per_head_rows(_pairs(do_ref[...]))
        sc = _bdot(qm, kb, BNT) + b_ref[0].reshape(N_PAIRS, 2 * ATT_BQ, ATT_BK)
        p = jnp.exp(sc - _per_head_cols(lse_ref[...]))
        ds = p * (_bdot(dom, vb, BNT) + _per_head_cols(cp_ref[...]))
        _store_pairs(dq_ref, _merge_heads(_bdot(ds, kb), ATT_BQ))
        db_ref[var] += ds.reshape(SWA_HEADS, ATT_BQ, ATT_BK)

    cur = pl.BlockSpec((ATT_BQ, SWA_WIDTH), lambda r, t: (t, r))
    bspec = pl.BlockSpec((1, SWA_HEADS, ATT_BQ, ATT_BK), lambda r, t: (_tile_variant(t, nb), 0, 0, 0))
    dq, db = pl.pallas_call(
        body, name=f"att_dq_d{dilation}", grid=(dilation, nb),
        in_specs=[cur] + _band_specs(length) * 2 + [cur, cur, cur, bspec],
        out_specs=[cur, pl.BlockSpec((4, SWA_HEADS, ATT_BQ, ATT_BK), lambda r, t: (0, 0, 0, 0))],
        out_shape=[jax.ShapeDtypeStruct(view, BF16), jax.ShapeDtypeStruct((4, SWA_HEADS, ATT_BQ, ATT_BK), F32)],
        compiler_params=_params("arbitrary", "arbitrary"),
    )(q.reshape(view), *([k.reshape(view)] * 3), *([v.reshape(view)] * 3), dop.reshape(view), lse.reshape(view),
      cp.reshape(view), bias)
    return dq.reshape(s, SWA_WIDTH), db


def _att_dkv2(q, k, v, dop, lse, cp, bias_t, dilation):
    s = q.shape[0]
    length = s // dilation
    nb = length // ATT_BQ
    view = (length, dilation * SWA_WIDTH)

    def body(k_ref, v_ref, qp, qc, qn, dp_, dc_, dn_, lp, lc, ln, cp_, cc_, cn_, b_ref, dk_ref, dv_ref):
        qm = _per_head_rows(_pairs(_band((qp, qc, qn))))
        dom = _per_head_rows(_pairs(_band((dp_, dc_, dn_))))
        lsev = _per_head_cols(_band((lp, lc, ln)))
        cpv = _per_head_cols(_band((cp_, cc_, cn_)))
        kv, vv = _pairs(k_ref[...]), _pairs(v_ref[...])
        sc = _bdot(qm, kv, BNT) + b_ref[0].reshape(N_PAIRS, 2 * ATT_BK, ATT_BQ)
        p = jnp.exp(sc - lsev)
        _store_pairs(dv_ref, _bdot(p, dom, BTN))
        ds = p * (_bdot(dom, vv, BNT) + cpv)
        _store_pairs(dk_ref, _bdot(ds, qm, BTN))

    cur = pl.BlockSpec((ATT_BQ, SWA_WIDTH), lambda r, t: (t, r))
    bspec = pl.BlockSpec((1, SWA_HEADS, ATT_BK, ATT_BQ), lambda r, t: (_tile_variant(t, nb), 0, 0, 0))
    dk, dv = pl.pallas_call(
        body, name=f"att_dkv_d{dilation}", grid=(dilation, nb),
        in_specs=[cur, cur] + _band_specs(length) * 4 + [bspec],
        out_specs=[cur, cur],
        out_shape=[jax.ShapeDtypeStruct(view, BF16)] * 2,
        compiler_params=_params("parallel", "parallel"),
    )(k.reshape(view), v.reshape(view), *([q.reshape(view)] * 3), *([dop.reshape(view)] * 3),
      *([lse.reshape(view)] * 3), *([cp.reshape(view)] * 3), bias_t)
    return dk.reshape(s, SWA_WIDTH), dv.reshape(s, SWA_WIDTH)


def _pattern_weights(lses):
    m = lses[0]
    for l in lses[1:]:
        m = jnp.maximum(m, l)
    es = [jnp.exp(l - m) for l in lses]
    den = es[0]
    for e in es[1:]:
        den = den + e
    return [e / den for e in es]


def _combine_fwd(outs, lses):
    s = outs[0].shape[0]
    tm = min(512, s)
    npat = len(outs)

    def body(*refs):
        ws = _pattern_weights([r[...] for r in refs[npat:2 * npat]])
        o = ws[0] * refs[0][...]
        for p in range(1, npat):
            o = o + ws[p] * refs[p][...]
        refs[2 * npat][...] = o.astype(BF16)

    blk = pl.BlockSpec((tm, SWA_WIDTH), lambda i: (i, 0))
    return pl.pallas_call(
        body, name="swa_combine_fwd", grid=(s // tm,), in_specs=[blk] * (2 * npat), out_specs=blk,
        out_shape=jax.ShapeDtypeStruct((s, SWA_WIDTH), BF16), compiler_params=_params("parallel"),
    )(*outs, *lses)


def _combine_bwd(d_out, outs, lses, bd):
    s = d_out.shape[0]
    tm = min(512, s)
    npat = len(outs)

    def body(*refs):
        d_ref, bd_ref = refs[0], refs[1 + 2 * npat]
        o_refs, l_refs = refs[1:1 + npat], refs[1 + npat:1 + 2 * npat]
        out_refs = refs[2 + 2 * npat:]
        ws = _pattern_weights([r[...] for r in l_refs])
        dov = d_ref[...]
        o = ws[0] * o_refs[0][...]
        for p in range(1, npat):
            o = o + ws[p] * o_refs[p][...]
        rd = _group_sum(dov * o, bd_ref[...])
        for p in range(npat):
            out_refs[p][...] = (ws[p] * dov).astype(BF16)
            out_refs[npat + p][...] = -ws[p] * rd

    blk = pl.BlockSpec((tm, SWA_WIDTH), lambda i: (i, 0))
    res = pl.pallas_call(
        body, name="swa_combine_bwd", grid=(s // tm,),
        in_specs=[blk] * (1 + 2 * npat) + [pl.BlockSpec((SWA_WIDTH, SWA_WIDTH), lambda i: (0, 0))],
        out_specs=[blk] * (2 * npat),
        out_shape=[jax.ShapeDtypeStruct((s, SWA_WIDTH), BF16)] * npat + [jax.ShapeDtypeStruct((s, SWA_WIDTH), F32)] * npat,
        compiler_params=_params("parallel"),
    )(d_out, *outs, *lses, bd)
    return res[:npat], res[npat:]


def _rel_bias_grad(dbs, buckets):
    npat = len(dbs)

    def body(*refs):
        db_refs, bk_refs, o_ref = refs[:npat], refs[npat:2 * npat], refs[2 * npat]
        row = lax.broadcasted_iota(jnp.int32, (REL_BUCKETS, LANE), 0)
        lane = lax.broadcasted_iota(jnp.int32, (REL_BUCKETS, LANE), 1)
        tiles = [[db_refs[p][0, h] + db_refs[p][1, h] + db_refs[p][2, h] + db_refs[p][3, h] for h in range(SWA_HEADS)]
                 for p in range(npat)]
        bks = [r[...] for r in bk_refs]

        def one_bucket(b, acc):
            for h in range(SWA_HEADS):
                tot = jnp.zeros((1, 1), F32)
                for p in range(npat):
                    sel = jnp.where(bks[p] == b, tiles[p][h], 0.0)
                    tot = tot + jnp.sum(jnp.sum(sel, axis=1, keepdims=True), axis=0, keepdims=True)
                acc = acc + jnp.where((row == b) & (lane == h), tot, 0.0)
            return acc

        o_ref[...] = lax.fori_loop(0, REL_BUCKETS, one_bucket, jnp.zeros((REL_BUCKETS, LANE), F32))

    full4 = pl.BlockSpec((4, SWA_HEADS, ATT_BQ, ATT_BK), lambda: (0, 0, 0, 0))
    full2 = pl.BlockSpec((ATT_BQ, ATT_BK), lambda: (0, 0))
    return pl.pallas_call(
        body, name="rel_bias_grad", in_specs=[full4] * npat + [full2] * npat,
        out_specs=pl.BlockSpec((REL_BUCKETS, LANE), lambda: (0, 0)),
        out_shape=jax.ShapeDtypeStruct((REL_BUCKETS, LANE), F32),
        compiler_params=pltpu.CompilerParams(vmem_limit_bytes=V7X_VMEM_LIMIT_BYTES),
    )(*dbs, *buckets)


def _swa_forward(p_pad, qw_row, kw_row, rel_bias, bd):
    q, k, v = _swa_pre_fwd(p_pad, qw_row, kw_row, bd)
    outs, lses = [], []
    for _, dil in DILATION_PATTERNS:
        o, lse = _att_fwd2(q, k, v, _bias_tiles(rel_bias, dil, True), dil)
        outs.append(o)
        lses.append(lse)
    return _combine_fwd(outs, lses), (q, k, v, outs, lses)


def _swa_backward(d_out, p_pad, qw_row, kw_row, rel_bias, bd, saved):
    q, k, v, outs, lses = saved
    dops, cps = _combine_bwd(d_out, outs, lses, bd)
    dqs, dks, dvs, dbs, buckets = [], [], [], [], []
    for p, (_, dil) in enumerate(DILATION_PATTERNS):
        dq, db = _att_dq2(q, k, v, dops[p], lses[p], cps[p], _bias_tiles(rel_bias, dil, True), dil)
        dk, dv = _att_dkv2(q, k, v, dops[p], lses[p], cps[p], _bias_tiles(rel_bias, dil, False), dil)
        dqs.append(dq)
        dks.append(dk)
        dvs.append(dv)
        dbs.append(db)
        buckets.append(jnp.asarray(_band_tables(dil, True)[1]))
    dp, dqw, dkw = _swa_pre_bwd(dqs, dks, dvs, p_pad, qw_row, kw_row, bd)
    return dp, dqw, dkw, _rel_bias_grad(dbs, buckets)


def _lane_row(v):
    flat = v.reshape(-1).astype(F32)
    return jnp.zeros((1, LANE), F32).at[0, :flat.shape[0]].set(flat)


W_IN_SHARD = N_IN // N_DEV
W_IN_RUNS = ((0, QKV_A, 0), (QKV_A, OFF_B, QKV_A), (OFF_B, OFF_B + 16, OFF_AB), (OFF_B + 16, N_IN, OFF_B))
W_IN_SEGMENTS = ((0, QKV_A), (OFF_Z, GDN_WIDTH), (OFF_B, 3 * SWA_WIDTH), (OFF_AB, LANE))


def _w_in_pieces(shard):
    lo, hi = shard * W_IN_SHARD, (shard + 1) * W_IN_SHARD
    out = []
    for first, last, dst in W_IN_RUNS:
        a, b = max(lo, first), min(hi, last)
        if a < b:
            out.append((a - lo, b - a, dst + a - first))
    return out


def _cols_from_slabs(w3, name):
    nd, r, wd = w3.shape
    half = nd // 2

    def body(w_ref, o_ref):
        for sh in range(half):
            o_ref[:, wd * sh:wd * (sh + 1)] = w_ref[sh]

    return pl.pallas_call(
        body, name=name, grid=(2,), in_specs=[pl.BlockSpec((half, r, wd), lambda j: (j, 0, 0))],
        out_specs=pl.BlockSpec((r, half * wd), lambda j: (0, j)),
        out_shape=jax.ShapeDtypeStruct((r, nd * wd), w3.dtype), compiler_params=_params("parallel"),
    )(w3)


def _w_in_from_slabs(w3):
    nd, r, _ = w3.shape

    def body(w_ref, o_ref):
        o_ref[:, OFF_AB:N_PAD] = jnp.zeros((r, N_PAD - OFF_AB), w3.dtype)
        for sh in range(nd):
            for src, length, dst in _w_in_pieces(sh):
                o_ref[:, dst:dst + length] = w_ref[sh, :, src:src + length]

    return pl.pallas_call(
        body, name="w_in_from_slabs", out_shape=jax.ShapeDtypeStruct((r, N_PAD), w3.dtype),
        compiler_params=pltpu.CompilerParams(vmem_limit_bytes=V7X_VMEM_LIMIT_BYTES),
    )(w3)


def _w_in_grad_slabs(parts):
    r = parts[0].shape[0]

    def body(*refs):
        o_ref = refs[len(parts)]
        for sh in range(N_DEV):
            for src, length, dst in _w_in_pieces(sh):
                seg = next(i for i, (off, width) in enumerate(W_IN_SEGMENTS) if off <= dst < off + width)
                at = dst - W_IN_SEGMENTS[seg][0]
                o_ref[sh, :, src:src + length] = refs[seg][:, at:at + length]

    return pl.pallas_call(
        body, name="w_in_grad_slabs", out_shape=jax.ShapeDtypeStruct((N_DEV, r, W_IN_SHARD), F32),
        compiler_params=pltpu.CompilerParams(vmem_limit_bytes=V7X_VMEM_LIMIT_BYTES),
    )(*parts)


LATE = ("w_out", "ffn2_w_gate", "ffn2_w_up", "ffn2_w_down")


def _late_weights(slabs):
    out = {}
    for n, g in zip(LATE, slabs):
        out[n] = _cols_from_slabs(g, f"{n}_cols") if n in COL_SHARDED else g.reshape(N_DEV * g.shape[1], g.shape[2])
    return out


def _local_step(x, tgt, wts, small, late_shards=None):
    bd = _head_block_diag()
    conv_wt = jnp.zeros((8, QKV_A), F32).at[:CONV_WIDTH].set(small["conv_w"].T)
    alog_row, dt_row = _lane_row(small["a_log"]), _lane_row(small["dt_bias"])
    gnorm_row = small["gdn_norm_w"].reshape(1, GDN_HEAD_DIM)
    qw_row = jnp.tile(small["q_norm_w"].reshape(-1), SWA_HEADS).reshape(1, SWA_WIDTH)
    kw_row = jnp.tile(small["k_norm_w"].reshape(-1), SWA_HEADS).reshape(1, SWA_WIDTH)
    rel_bias = small["rel_bias"]
    win_pad = wts["w_in_pad"]
    exchange = late_shards is not None
    dw_dtype = BF16 if exchange else F32

    x1, sv1 = _ffn_forward(x, small["ffn1_norm"], wts["ffn1_w_gate"], wts["ffn1_w_up"], wts["ffn1_w_down"], "ffn1")
    n2, r2 = _rms_fwd(x1, small["mix_norm"], "mix_norm")
    p_pad = _matmul([(n2, win_pad)], tm=256, tn=N_PAD, tk=D_MODEL, name="w_in")
    o_a, sva, gathered = _gdn_forward(p_pad, conv_wt, alog_row, dt_row, gnorm_row,
                                      gather=[late_shards[n] for n in LATE] if exchange else ())
    if exchange:
        wts = {**wts, **_late_weights(gathered)}
    wo_a, wo_b = wts["w_out"][:GDN_WIDTH], wts["w_out"][GDN_WIDTH:]
    o_b, svb = _swa_forward(p_pad, qw_row, kw_row, rel_bias, bd)
    x2 = _matmul([(o_a, wo_a), (o_b, wo_b)], tm=512, tn=D_MODEL, tk=GDN_WIDTH, name="w_out", res=x1)
    x3, sv2 = _ffn_forward(x2, small["ffn2_norm"], wts["ffn2_w_gate"], wts["ffn2_w_up"], wts["ffn2_w_down"], "ffn2")
    loss_row, dx3, d_final = _final_loss(x3, small["final_norm"], tgt)

    dx2, d_ffn2_norm, dwg2, dwu2, dwd2 = _ffn_backward(
        dx3, x2, small["ffn2_norm"], wts["ffn2_w_gate"], wts["ffn2_w_up"], wts["ffn2_w_down"], sv2, "ffn2", dw_dtype)
    d_oa = _matmul([(dx2, wo_a)], tb=True, tm=512, tn=GDN_WIDTH, tk=D_MODEL, name="w_out_da")
    d_ob = _matmul([(dx2, wo_b)], tb=True, tm=512, tn=SWA_WIDTH, tk=D_MODEL, name="w_out_db")
    dwo_a = _matmul([(o_a, dx2)], ta=True, tm=GDN_WIDTH, tn=D_MODEL, tk=2048, name="w_out_dwa", out_dtype=dw_dtype)
    dwo_b = _matmul([(o_b, dx2)], ta=True, tm=SWA_WIDTH, tn=D_MODEL, tk=2048, name="w_out_dwb", out_dtype=dw_dtype)

    def row_slabs(full):
        return full.reshape(N_DEV, full.shape[0] // N_DEV, full.shape[1])

    late_grads = [row_slabs(jnp.concatenate([dwo_a, dwo_b], axis=0)), dwg2, dwu2, row_slabs(dwd2)]
    dp_qkv, dz, dp_ab, dconv, gate_sums, d_gnorm, received = _gdn_backward(
        d_oa, p_pad, conv_wt, alog_row, dt_row, gnorm_row, sva, scatter=late_grads if exchange else ())
    if exchange:
        late_grads = received
    dp_b, dqw, dkw, d_rel = _swa_backward(d_ob, p_pad, qw_row, kw_row, rel_bias, bd, svb)
    segs = [(dp_qkv, 0, QKV_A), (dz, OFF_Z, GDN_WIDTH), (dp_b, OFF_B, 3 * SWA_WIDTH), (dp_ab, OFF_AB, LANE)]
    dn2 = None
    dwin_parts = []
    for i, (dseg, off, width) in enumerate(segs):
        dwin_parts.append(_matmul([(n2, dseg)], ta=True, tm=512, tn=width, tk=2048, name=f"w_in_dw{i}"))
        dn2 = _matmul([(dseg, win_pad[:, off:off + width])], tb=True, tm=512, tn=D_MODEL, tk=width,
                      name=f"w_in_dn{i}", res=dn2)
    dx1, d_mix_norm = _rms_bwd(dn2, x1, r2, small["mix_norm"], dx2, "mix_dnorm")
    dx, d_ffn1_norm, dwg1, dwu1, dwd1 = _ffn_backward(
        dx1, x, small["ffn1_norm"], wts["ffn1_w_gate"], wts["ffn1_w_up"], wts["ffn1_w_down"], sv1, "ffn1")

    grads = {
        "ffn1_norm": d_ffn1_norm, "ffn1_w_gate": dwg1, "ffn1_w_up": dwu1, "ffn1_w_down": row_slabs(dwd1),
        "mix_norm": d_mix_norm, "w_in": _w_in_grad_slabs(dwin_parts), "conv_w": dconv[:CONV_WIDTH].T,
        "a_log": gate_sums[0, :8].reshape(2, GDN_HEADS), "dt_bias": gate_sums[1, :8].reshape(2, GDN_HEADS),
        "gdn_norm_w": d_gnorm, "q_norm_w": dqw.reshape(SWA_HEADS, SWA_HEAD_DIM).sum(0, keepdims=True),
        "k_norm_w": dkw.reshape(SWA_HEADS, SWA_HEAD_DIM).sum(0, keepdims=True), "rel_bias": d_rel[:, :SWA_HEADS],
        "ffn2_norm": d_ffn2_norm, "final_norm": d_final, **dict(zip(LATE, late_grads)),
    }
    return loss_row, dx, grads


MESH_IDS = pl.DeviceIdType.MESH
ANY = pl.BlockSpec(memory_space=pl.ANY)


def _all_gather(v, name):
    m, n = v.shape

    def body(x_ref, out_ref, send_sems, recv_sems, local_sem):
        x, y, c = lax.axis_index("x"), lax.axis_index("y"), lax.axis_index("c")
        me, sibling = (x, y, c), (x, y, 1 - c)
        chips = [(1 - x, y), (x, 1 - y), (1 - x, 1 - y)]

        def rows(px, py, pc):
            return out_ref.at[pl.ds((4 * px + 2 * py + pc) * m, m), :]

        def copy(k, block, to, src=None):
            return pltpu.make_async_remote_copy(
                src_ref=rows(*block) if src is None else src, dst_ref=rows(*block),
                send_sem=send_sems.at[k], recv_sem=recv_sems.at[k], device_id=to, device_id_type=MESH_IDS)

        mine = pltpu.make_async_copy(x_ref, rows(*me), local_sem)
        mine.start()
        first = [copy(0, me, sibling, src=x_ref)]
        first += [copy(1 + j, me, (*chip, c), src=x_ref) for j, chip in enumerate(chips)]
        for cp in first:
            cp.start()
        passed = [copy(4 + j, (*chip, c), sibling) for j, chip in enumerate(chips)]
        for j, chip in enumerate(chips):
            copy(1 + j, (*chip, c), me).wait_recv()
            passed[j].start()
        copy(0, sibling, me).wait_recv()
        for j, chip in enumerate(chips):
            copy(4 + j, (*chip, 1 - c), me).wait_recv()
        for cp in first + passed:
            cp.wait_send()
        mine.wait()

    return pl.pallas_call(
        body, name=name, in_specs=[ANY], out_specs=ANY,
        out_shape=jax.ShapeDtypeStruct((N_DEV * m, n), v.dtype),
        scratch_shapes=[pltpu.SemaphoreType.DMA((7,)), pltpu.SemaphoreType.DMA((7,)), pltpu.SemaphoreType.DMA],
        compiler_params=pltpu.CompilerParams(vmem_limit_bytes=V7X_VMEM_LIMIT_BYTES),
    )(v)


def _sibling_swap(v, name):
    def body(v_ref, out_ref, send_sem, recv_sem):
        x, y, c = lax.axis_index("x"), lax.axis_index("y"), lax.axis_index("c")
        cp = pltpu.make_async_remote_copy(src_ref=v_ref, dst_ref=out_ref, send_sem=send_sem, recv_sem=recv_sem,
                                          device_id=(x, y, 1 - c), device_id_type=MESH_IDS)
        cp.start()
        cp.wait()

    return pl.pallas_call(
        body, name=name, in_specs=[ANY], out_specs=ANY, out_shape=jax.ShapeDtypeStruct(v.shape, v.dtype),
        scratch_shapes=[pltpu.SemaphoreType.DMA, pltpu.SemaphoreType.DMA],
        compiler_params=pltpu.CompilerParams(vmem_limit_bytes=V7X_VMEM_LIMIT_BYTES),
    )(v)


def _chip_exchange(t, name):
    def body(t_ref, out_ref, send_sems, recv_sems, local_sem):
        x, y, c = lax.axis_index("x"), lax.axis_index("y"), lax.axis_index("c")
        mine = 2 * x + y
        chips = [(1 - x, y), (x, 1 - y), (1 - x, 1 - y)]
        own = pltpu.make_async_copy(t_ref.at[mine], out_ref.at[mine], local_sem)
        own.start()
        copies = [pltpu.make_async_remote_copy(
            src_ref=t_ref.at[2 * px + py], dst_ref=out_ref.at[mine], send_sem=send_sems.at[j], recv_sem=recv_sems.at[j],
            device_id=(px, py, c), device_id_type=MESH_IDS) for j, (px, py) in enumerate(chips)]
        for cp in copies:
            cp.start()
        for j, (px, py) in enumerate(chips):
            pltpu.make_async_remote_copy(
                src_ref=t_ref.at[mine], dst_ref=out_ref.at[2 * px + py], send_sem=send_sems.at[j],
                recv_sem=recv_sems.at[j], device_id=(px, py, c), device_id_type=MESH_IDS).wait_recv()
        for cp in copies:
            cp.wait_send()
        own.wait()

    return pl.pallas_call(
        body, name=name, in_specs=[ANY], out_specs=ANY, out_shape=jax.ShapeDtypeStruct(t.shape, t.dtype),
        scratch_shapes=[pltpu.SemaphoreType.DMA((3,)), pltpu.SemaphoreType.DMA((3,)), pltpu.SemaphoreType.DMA],
        compiler_params=pltpu.CompilerParams(vmem_limit_bytes=V7X_VMEM_LIMIT_BYTES),
    )(t)


def _adamw(parts, w, m, v, name):
    nparts, r, n = parts.shape
    tr = r
    for cand in (256, 176, 128, 104, 64, 8):
        if r % cand == 0:
            tr = cand
            break
    bc1 = 1.0 - ADAM_B1 ** ADAM_STEP
    bc2 = 1.0 - ADAM_B2 ** ADAM_STEP

    def body(p_ref, w_ref, m_ref, v_ref, g_ref, d_ref, nm_ref, nv_ref):
        g = p_ref[0].astype(F32)
        for k in range(1, nparts):
            g = g + p_ref[k].astype(F32)
        mn = ADAM_B1 * m_ref[...] + (1.0 - ADAM_B1) * g
        vn = ADAM_B2 * v_ref[...] + (1.0 - ADAM_B2) * (g * g)
        m_hat = mn / bc1
        v_hat = vn / bc2
        g_ref[...] = g
        nm_ref[...] = mn
        nv_ref[...] = vn
        d_ref[...] = -ADAM_LR * (m_hat / (jnp.sqrt(v_hat) + ADAM_EPS) + ADAM_WD * w_ref[...])

    blk = pl.BlockSpec((tr, n), lambda i: (i, 0))
    return pl.pallas_call(
        body, name=name, grid=(r // tr,),
        in_specs=[pl.BlockSpec((nparts, tr, n), lambda i: (0, i, 0)), blk, blk, blk],
        out_specs=[blk] * 4, out_shape=[jax.ShapeDtypeStruct((r, n), F32)] * 4,
        compiler_params=_params("parallel"),
    )(parts, w, m, v)


def _mesh_place():
    x, y, c = lax.axis_index("x"), lax.axis_index("y"), lax.axis_index("c")
    return x, y, c, [(1 - x, y), (x, 1 - y), (1 - x, 1 - y)]


def _gather_phases(x_refs, out_refs, send_sems, recv_sems, local_sems):
    na = len(x_refs)

    def place():
        x, y, c, chips = _mesh_place()
        return (x, y, c), (x, y, 1 - c), chips, c

    def slab(i, px, py, pc):
        return out_refs[i].at[4 * px + 2 * py + pc]

    def copy(i, k, block, to, src=None):
        return pltpu.make_async_remote_copy(
            src_ref=slab(i, *block) if src is None else src, dst_ref=slab(i, *block),
            send_sem=send_sems.at[i, k], recv_sem=recv_sems.at[i, k], device_id=to, device_id_type=MESH_IDS)

    def own(i, me):
        return pltpu.make_async_copy(x_refs[i], slab(i, *me), local_sems.at[i])

    def sends(i, me, sibling, chips, c):
        return [copy(i, 0, me, sibling, src=x_refs[i])] + [copy(i, 1 + j, me, (*chip, c), src=x_refs[i])
                                                          for j, chip in enumerate(chips)]

    def start():
        me, sibling, chips, c = place()
        for i in range(na):
            own(i, me).start()
            for cp in sends(i, me, sibling, chips, c):
                cp.start()

    def forward():
        me, sibling, chips, c = place()
        for j, chip in enumerate(chips):
            for i in range(na):
                copy(i, 1 + j, (*chip, c), me).wait_recv()
                copy(i, 4 + j, (*chip, c), sibling).start()

    def finish():
        me, sibling, chips, c = place()
        for i in range(na):
            copy(i, 0, sibling, me).wait_recv()
        for j, chip in enumerate(chips):
            for i in range(na):
                copy(i, 4 + j, (*chip, 1 - c), me).wait_recv()
        for i in range(na):
            for cp in sends(i, me, sibling, chips, c):
                cp.wait_send()
            for j, chip in enumerate(chips):
                copy(i, 4 + j, (*chip, c), sibling).wait_send()
            own(i, me).wait()

    return start, forward, finish


def _gather_semaphores(na):
    return [pltpu.SemaphoreType.DMA((na, 7)), pltpu.SemaphoreType.DMA((na, 7)), pltpu.SemaphoreType.DMA((na,))]


def _scatter_phases(g_refs, out_refs, send_sems, recv_sems, local_sems):
    na = len(g_refs)

    def place(m):
        x, y, c = lax.axis_index("x"), lax.axis_index("y"), lax.axis_index("c")
        px = 1 - x if m & 4 else x
        py = 1 - y if m & 2 else y
        pc = 1 - c if m & 1 else c
        return 4 * x + 2 * y + c, (px, py, pc), 4 * px + 2 * py + pc

    def own(i):
        me, _, _ = place(0)
        return pltpu.make_async_copy(g_refs[i].at[me], out_refs[i].at[me], local_sems.at[i])

    def start():
        for i in range(na):
            own(i).start()
            for m in range(1, N_DEV):
                me, peer, peer_idx = place(m)
                pltpu.make_async_remote_copy(
                    src_ref=g_refs[i].at[peer_idx], dst_ref=out_refs[i].at[me], send_sem=send_sems.at[i, m - 1],
                    recv_sem=recv_sems.at[i, m - 1], device_id=peer, device_id_type=MESH_IDS).start()

    def finish():
        for i in range(na):
            for m in range(1, N_DEV):
                me, peer, peer_idx = place(m)
                cp = pltpu.make_async_remote_copy(
                    src_ref=g_refs[i].at[peer_idx], dst_ref=out_refs[i].at[peer_idx], send_sem=send_sems.at[i, m - 1],
                    recv_sem=recv_sems.at[i, m - 1], device_id=peer, device_id_type=MESH_IDS)
                cp.wait_recv()
                cp.wait_send()
            own(i).wait()

    return start, finish


def _all_gather_many(vs, name):
    na = len(vs)

    def body(*refs):
        x_refs, out_refs = refs[:na], refs[na:2 * na]
        for step in _gather_phases(x_refs, out_refs, *refs[2 * na:]):
            step()

    return pl.pallas_call(
        body, name=name, in_specs=[ANY] * na, out_specs=[ANY] * na,
        out_shape=[jax.ShapeDtypeStruct((N_DEV,) + v.shape, v.dtype) for v in vs],
        scratch_shapes=_gather_semaphores(na),
        compiler_params=pltpu.CompilerParams(vmem_limit_bytes=V7X_VMEM_LIMIT_BYTES),
    )(*vs)


def _sibling_swap_many(gs, name):
    na = len(gs)

    def body(*refs):
        g_refs, out_refs = refs[:na], refs[na:2 * na]
        send_sems, recv_sems = refs[2 * na:]
        x, y, c, _ = _mesh_place()
        copies = [pltpu.make_async_remote_copy(
            src_ref=g_refs[i].at[2 * k + 1 - c], dst_ref=out_refs[i].at[k], send_sem=send_sems.at[i, k],
            recv_sem=recv_sems.at[i, k], device_id=(x, y, 1 - c), device_id_type=MESH_IDS)
            for i in range(na) for k in range(4)]
        for cp in copies:
            cp.start()
        for cp in copies:
            cp.wait()

    return pl.pallas_call(
        body, name=name, in_specs=[ANY] * na, out_specs=[ANY] * na,
        out_shape=[jax.ShapeDtypeStruct((4,) + g.shape[1:], g.dtype) for g in gs],
        scratch_shapes=[pltpu.SemaphoreType.DMA((na, 4)), pltpu.SemaphoreType.DMA((na, 4))],
        compiler_params=pltpu.CompilerParams(vmem_limit_bytes=V7X_VMEM_LIMIT_BYTES),
    )(*gs)


def _chip_sum(g, got, core, name):
    _, r, n = g.shape

    def body(c_ref, g_ref, got_ref, o_ref):
        o_ref[...] = (g_ref[...] + got_ref[...]).astype(BF16)

    return pl.pallas_call(
        body, name=name,
        grid_spec=pltpu.PrefetchScalarGridSpec(
            num_scalar_prefetch=1, grid=(4,),
            in_specs=[pl.BlockSpec((None, r, n), lambda k, c_ref: (2 * k + c_ref[0], 0, 0)),
                      pl.BlockSpec((None, r, n), lambda k, c_ref: (k, 0, 0))],
            out_specs=pl.BlockSpec((None, r, n), lambda k, c_ref: (k, 0, 0))),
        out_shape=jax.ShapeDtypeStruct((4, r, n), BF16), compiler_params=_params("parallel"),
    )(core, g, got)


def _chip_exchange_many(ts, name):
    na = len(ts)

    def body(*refs):
        t_refs, out_refs = refs[:na], refs[na:2 * na]
        send_sems, recv_sems, local_sems = refs[2 * na:]
        x, y, c, chips = _mesh_place()
        mine = 2 * x + y
        own = [pltpu.make_async_copy(t_refs[i].at[mine], out_refs[i].at[mine], local_sems.at[i]) for i in range(na)]
        for cp in own:
            cp.start()
        copies = [pltpu.make_async_remote_copy(
            src_ref=t_refs[i].at[2 * px + py], dst_ref=out_refs[i].at[mine], send_sem=send_sems.at[i, j],
            recv_sem=recv_sems.at[i, j], device_id=(px, py, c), device_id_type=MESH_IDS)
            for j, (px, py) in enumerate(chips) for i in range(na)]
        for cp in copies:
            cp.start()
        for j, (px, py) in enumerate(chips):
            for i in range(na):
                pltpu.make_async_remote_copy(
                    src_ref=t_refs[i].at[mine], dst_ref=out_refs[i].at[2 * px + py], send_sem=send_sems.at[i, j],
                    recv_sem=recv_sems.at[i, j], device_id=(px, py, c), device_id_type=MESH_IDS).wait_recv()
        for cp in copies:
            cp.wait_send()
        for cp in own:
            cp.wait()

    return pl.pallas_call(
        body, name=name, in_specs=[ANY] * na, out_specs=[ANY] * na,
        out_shape=[jax.ShapeDtypeStruct(t.shape, t.dtype) for t in ts],
        scratch_shapes=[pltpu.SemaphoreType.DMA((na, 3)), pltpu.SemaphoreType.DMA((na, 3)), pltpu.SemaphoreType.DMA((na,))],
        compiler_params=pltpu.CompilerParams(vmem_limit_bytes=V7X_VMEM_LIMIT_BYTES),
    )(*ts)


BIG = ("ffn1_w_gate", "ffn1_w_up", "ffn1_w_down", "w_in", "w_out", "ffn2_w_gate", "ffn2_w_up", "ffn2_w_down")
COL_SHARDED = ("ffn1_w_gate", "ffn1_w_up", "w_in", "ffn2_w_gate", "ffn2_w_up")
SMALL = ("ffn1_norm", "mix_norm", "a_log", "dt_bias", "gdn_norm_w", "q_norm_w", "k_norm_w", "rel_bias",
         "ffn2_norm", "final_norm")
WEIGHTS = ("ffn1_norm", "ffn1_w_gate", "ffn1_w_up", "ffn1_w_down", "mix_norm", "w_in", "conv_w", "a_log", "dt_bias",
           "gdn_norm_w", "q_norm_w", "k_norm_w", "rel_bias", "w_out", "ffn2_norm", "ffn2_w_gate", "ffn2_w_up",
           "ffn2_w_down", "final_norm")
PACK_WIDTH = 1024
PACK_ROW_MULTIPLE = 32


def _pack(arrays, width, row_multiple):
    flat = jnp.concatenate([a.reshape(-1) for a in arrays])
    rows = -(-flat.shape[0] // width)
    rows = -(-rows // row_multiple) * row_multiple
    return jnp.pad(flat, (0, rows * width - flat.shape[0])).reshape(rows, width)


def _unpack(packed, shapes):
    flat = packed.reshape(-1)
    out, pos = [], 0
    for shp in shapes:
        size = int(np.prod(shp))
        out.append(flat[pos:pos + size].reshape(shp))
        pos += size
    return out


def _blocks_of(name, full):
    if name in COL_SHARDED:
        rows, cols = full.shape
        return full.reshape(rows, N_DEV, cols // N_DEV).transpose(1, 0, 2).reshape(N_DEV, -1)
    return full.reshape(N_DEV, -1)


def _full_of(name, blocks, shard_shape):
    rows, cols = shard_shape
    if name in COL_SHARDED:
        return blocks.reshape(N_DEV, rows, cols).transpose(1, 0, 2).reshape(rows, N_DEV * cols)
    return blocks.reshape(N_DEV * rows, cols)


def kernel(x, ffn1_norm, ffn1_w_gate, ffn1_w_up, ffn1_w_down, mix_norm, w_in, conv_w, a_log, dt_bias, gdn_norm_w, q_norm_w, k_norm_w, rel_bias, w_out, ffn2_norm, ffn2_w_gate, ffn2_w_up, ffn2_w_down, final_norm, loss_target, m_ffn1_norm, m_ffn1_w_gate, m_ffn1_w_up, m_ffn1_w_down, m_mix_norm, m_w_in, m_conv_w, m_a_log, m_dt_bias, m_gdn_norm_w, m_q_norm_w, m_k_norm_w, m_rel_bias, m_w_out, m_ffn2_norm, m_ffn2_w_gate, m_ffn2_w_up, m_ffn2_w_down, m_final_norm, v_ffn1_norm, v_ffn1_w_gate, v_ffn1_w_up, v_ffn1_w_down, v_mix_norm, v_w_in, v_conv_w, v_a_log, v_dt_bias, v_gdn_norm_w, v_q_norm_w, v_k_norm_w, v_rel_bias, v_w_out, v_ffn2_norm, v_ffn2_w_gate, v_ffn2_w_up, v_ffn2_w_down, v_final_norm):
    w = dict(ffn1_norm=ffn1_norm, ffn1_w_gate=ffn1_w_gate, ffn1_w_up=ffn1_w_up, ffn1_w_down=ffn1_w_down, mix_norm=mix_norm, w_in=w_in, conv_w=conv_w, a_log=a_log, dt_bias=dt_bias, gdn_norm_w=gdn_norm_w, q_norm_w=q_norm_w, k_norm_w=k_norm_w, rel_bias=rel_bias, w_out=w_out, ffn2_norm=ffn2_norm, ffn2_w_gate=ffn2_w_gate, ffn2_w_up=ffn2_w_up, ffn2_w_down=ffn2_w_down, final_norm=final_norm)
    mom = dict(ffn1_norm=m_ffn1_norm, ffn1_w_gate=m_ffn1_w_gate, ffn1_w_up=m_ffn1_w_up, ffn1_w_down=m_ffn1_w_down, mix_norm=m_mix_norm, w_in=m_w_in, conv_w=m_conv_w, a_log=m_a_log, dt_bias=m_dt_bias, gdn_norm_w=m_gdn_norm_w, q_norm_w=m_q_norm_w, k_norm_w=m_k_norm_w, rel_bias=m_rel_bias, w_out=m_w_out, ffn2_norm=m_ffn2_norm, ffn2_w_gate=m_ffn2_w_gate, ffn2_w_up=m_ffn2_w_up, ffn2_w_down=m_ffn2_w_down, final_norm=m_final_norm)
    var = dict(ffn1_norm=v_ffn1_norm, ffn1_w_gate=v_ffn1_w_gate, ffn1_w_up=v_ffn1_w_up, ffn1_w_down=v_ffn1_w_down, mix_norm=v_mix_norm, w_in=v_w_in, conv_w=v_conv_w, a_log=v_a_log, dt_bias=v_dt_bias, gdn_norm_w=v_gdn_norm_w, q_norm_w=v_q_norm_w, k_norm_w=v_k_norm_w, rel_bias=v_rel_bias, w_out=v_w_out, ffn2_norm=v_ffn2_norm, ffn2_w_gate=v_ffn2_w_gate, ffn2_w_up=v_ffn2_w_up, ffn2_w_down=v_ffn2_w_down, final_norm=v_final_norm)
    ix, iy, ic = lax.axis_index("x"), lax.axis_index("y"), lax.axis_index("c")
    me = 4 * ix + 2 * iy + ic

    shard = {n: w[n][0] for n in BIG}

    conv_shard_shape = w["conv_w"][0].shape
    conv_elems = conv_shard_shape[0] * conv_shard_shape[1]
    early = [n for n in BIG if n not in LATE]
    gathered = _all_gather_many([shard[n].astype(BF16) for n in early] + [_pack([w["conv_w"][0]], LANE, 8)],
                                "gather_weights")
    slabs = dict(zip(early, gathered))
    wts = {}
    for n in early:
        if n == "w_in":
            wts["w_in_pad"] = _w_in_from_slabs(slabs[n])
        elif n in COL_SHARDED:
            wts[n] = _cols_from_slabs(slabs[n], f"{n}_cols")
        else:
            wts[n] = slabs[n].reshape(N_DEV * slabs[n].shape[1], slabs[n].shape[2])

    small = {n: w[n][0] if n not in ("rel_bias",) else w[n] for n in SMALL}
    small = {n: (a.reshape(1, -1) if n.endswith("norm") else a) for n, a in small.items()}
    conv_all = gathered[-1].reshape(N_DEV, -1)
    small["conv_w"] = conv_all[:, :conv_elems].reshape(N_DEV * conv_shard_shape[0], conv_shard_shape[1])
    loss_row, grad_x, grads = _local_step(x[0], loss_target[0], wts, small,
                                          late_shards={n: shard[n].astype(BF16) for n in LATE})
    loss = lax.psum(loss_row[0, 0], ("x", "y", "c"))

    gots = _sibling_swap_many([grads[n] for n in early], "grads_to_sibling")
    core = ic.astype(jnp.int32).reshape(1)
    sums = [_chip_sum(grads[n], got, core, f"{n}_chip_sum") for n, got in zip(early, gots)]
    parts = dict(zip(early, _chip_exchange_many(sums, "grads_to_chips")))
    big_out = [[], [], [], []]
    for n in BIG:
        part = grads[n] if n in LATE else parts[n]
        for kind, val in enumerate(_adamw(part, shard[n], mom[n][0], var[n][0], f"{n}_adamw")):
            big_out[kind].append(val)

    small_names = SMALL + ("conv_w",)
    small_shapes = [grads[n].shape for n in small_names]
    g_small = _pack([grads[n] for n in small_names], LANE, 8)
    small_rows = g_small.shape[0]
    all_small = _all_gather(g_small, "gather_small_grads").reshape(N_DEV, small_rows, LANE)
    rep_shapes = [grads[n].shape for n in SMALL]
    zero_conv = jnp.zeros(small_shapes[-1], F32)
    ws = _pack([w[n].reshape(grads[n].shape) for n in SMALL] + [zero_conv], LANE, 8)
    ms = _pack([mom[n].reshape(grads[n].shape) for n in SMALL] + [zero_conv], LANE, 8)
    vs = _pack([var[n].reshape(grads[n].shape) for n in SMALL] + [zero_conv], LANE, 8)
    small_out = [_unpack(a, small_shapes) for a in _adamw(all_small, ws, ms, vs, "adamw_small")]
    conv_g = lax.dynamic_slice_in_dim(small_out[0][-1], me * conv_shard_shape[0], conv_shard_shape[0], axis=0)
    conv_out = [_unpack(a, [conv_shard_shape])[0] for a in _adamw(
        _pack([conv_g], LANE, 8)[None], _pack([w["conv_w"][0]], LANE, 8), _pack([mom["conv_w"][0]], LANE, 8),
        _pack([var["conv_w"][0]], LANE, 8), "adamw_conv")]

    def leaf(kind, n):
        if n in BIG:
            val = big_out[kind][BIG.index(n)]
        elif n == "conv_w":
            val = conv_out[kind]
        else:
            val = small_out[kind][SMALL.index(n)]
        return val.reshape(w[n].shape)

    outs = [loss, grad_x[None]]
    for kind in range(4):
        outs += [leaf(kind, n) for n in WEIGHTS]
    return tuple(outs)
```

```python
import functools
import math

import numpy as np
import jax
import jax.numpy as jnp
from jax import lax
from jax.experimental import pallas as pl
from jax.experimental.pallas import tpu as pltpu

F32 = jnp.float32
BF16 = jnp.bfloat16

D_MODEL = 1024
D_FF = 2816
GDN_HEADS = 4
GDN_HEAD_DIM = 128
GDN_WIDTH = 512
CONV_WIDTH = 5
CHUNK = 64
SWA_HEADS = 8
SWA_HEAD_DIM = 64
SWA_WIDTH = 512
DILATION_PATTERNS = ((128, 1), (512, 4), (2048, 16))
REL_BUCKETS = 32
REL_MAX_DISTANCE = 1024
EPS = 1e-6
NEG_BIG = -1e30
N_DEV = 8

ADAM_LR = 0.001
ADAM_B1 = 0.9
ADAM_B2 = 0.999
ADAM_EPS = 1e-08
ADAM_WD = 0.01
ADAM_STEP = 10

QKV_A = 3 * GDN_WIDTH
OFF_Z = QKV_A
OFF_B = OFF_Z + GDN_WIDTH
OFF_AB = OFF_B + 3 * SWA_WIDTH
N_PAD = OFF_AB + 128
N_IN = 3600

V7X_VMEM_LIMIT_BYTES = 56 * 1024 * 1024
LANE = 128
ATT_BQ = 128
ATT_HALO = 64
CONV_ROWS = 256

NN = (((1,), (0,)), ((), ()))
NT = (((1,), (1,)), ((), ()))
TN = (((0,), (0,)), ((), ()))


def _params(*sem):
    return pltpu.CompilerParams(dimension_semantics=sem, vmem_limit_bytes=V7X_VMEM_LIMIT_BYTES)


def _dot(a, b, dn=NN):
    return lax.dot_general(a.astype(BF16), b.astype(BF16), dn, preferred_element_type=F32)


def _dot_hi(a, b, dn=NN):
    return lax.dot_general(a, b, dn, precision=lax.Precision.HIGHEST, preferred_element_type=F32)


def _sigmoid(x):
    return 1.0 / (1.0 + jnp.exp(-x))


class _Exchange:
    def __init__(self, kind, arrays):
        self.kind, self.arrays = kind, list(arrays)

    def out_shape(self):
        lead = (N_DEV,) if self.kind == "gather" else ()
        return [jax.ShapeDtypeStruct(lead + v.shape, v.dtype) for v in self.arrays]

    def hooks(self, in_refs, out_refs, sems, grid):
        step = pl.program_id(0)
        for axis in range(1, len(grid)):
            step = step * grid[axis] + pl.program_id(axis)
        total = math.prod(grid)
        if self.kind == "gather":
            assert total >= 4
            start, forward, finish = _gather_phases(in_refs, out_refs, *sems)
            pl.when(step == total // 2)(forward)
        else:
            assert total >= 2
            start, finish = _scatter_phases(in_refs, out_refs, *sems)
        pl.when(step == 0)(start)
        pl.when(step == total - 1)(finish)


def _pallas(body, *, name, grid, in_specs, out_specs, out_shape, args, semantics, scratch_shapes=(), exchange=None):
    n_in, n_out, n_scr = len(in_specs), len(out_specs), len(scratch_shapes)
    if exchange is None:
        res = pl.pallas_call(
            body, name=name, grid=grid, in_specs=list(in_specs), out_specs=list(out_specs), out_shape=list(out_shape),
            scratch_shapes=list(scratch_shapes), compiler_params=_params(*semantics))(*args)
        return list(res), []
    na = len(exchange.arrays)

    def carrying(*refs):
        ins, sent = refs[:n_in], refs[n_in:n_in + na]
        outs = refs[n_in + na:n_in + na + n_out]
        landed = refs[n_in + na + n_out:n_in + 2 * na + n_out]
        rest = refs[n_in + 2 * na + n_out:]
        exchange.hooks(sent, landed, rest[n_scr:], grid)
        body(*ins, *outs, *rest[:n_scr])

    res = pl.pallas_call(
        carrying, name=name, grid=grid, in_specs=list(in_specs) + [ANY] * na, out_specs=list(out_specs) + [ANY] * na,
        out_shape=list(out_shape) + exchange.out_shape(), scratch_shapes=list(scratch_shapes) + _gather_semaphores(na),
        compiler_params=_params(*(["arbitrary"] * len(grid))))(*args, *exchange.arrays)
    return list(res[:n_out]), list(res[n_out:])


def _matmul(pairs, *, ta=False, tb=False, out_dtype=F32, tm, tn, tk, name, res=None, alpha=None, shard_cols=None,
            exchange=None):
    a0, b0 = pairs[0]
    m = a0.shape[1] if ta else a0.shape[0]
    k = a0.shape[0] if ta else a0.shape[1]
    n = b0.shape[0] if tb else b0.shape[1]
    tm, tn, tk = min(tm, m), min(tn, n), min(tk, k)
    assert m % tm == 0 and n % tn == 0 and k % tk == 0, (name, m, n, k, tm, tn, tk)
    nk = k // tk
    npairs = len(pairs)
    dn = (((0 if ta else 1,), (1 if tb else 0,)), ((), ()))

    def body(*refs):
        ins = refs[:2 * npairs]
        pos = 2 * npairs
        r_ref = None
        if res is not None:
            r_ref = refs[pos]
            pos += 1
        o_ref, acc = refs[pos], refs[pos + 1]
        kk = pl.program_id(2)
        t = None
        for p in range(npairs):
            d = _dot(ins[2 * p][...], ins[2 * p + 1][...], dn)
            t = d if t is None else t + d

        if nk > 1:
            @pl.when(kk == 0)
            def _():
                acc[...] = t

            @pl.when((kk > 0) & (kk < nk - 1))
            def _():
                acc[...] += t

        @pl.when(kk == nk - 1)
        def _():
            r = acc[...] + t if nk > 1 else t
            if alpha is not None:
                r = r * alpha
            if r_ref is not None:
                r = r_ref[...] + r
            if shard_cols is None:
                o_ref[...] = r.astype(out_dtype)
            else:
                for sh in range(tn // shard_cols):
                    o_ref[sh] = r[:, sh * shard_cols:(sh + 1) * shard_cols].astype(out_dtype)

    a_spec = pl.BlockSpec((tk, tm), lambda i, j, kk: (kk, i)) if ta else pl.BlockSpec((tm, tk), lambda i, j, kk: (i, kk))
    b_spec = pl.BlockSpec((tn, tk), lambda i, j, kk: (j, kk)) if tb else pl.BlockSpec((tk, tn), lambda i, j, kk: (kk, j))
    o_spec = pl.BlockSpec((tm, tn), lambda i, j, kk: (i, j))
    in_specs = [a_spec, b_spec] * npairs + ([o_spec] if res is not None else [])
    args = [t for pr in pairs for t in pr] + ([res] if res is not None else [])
    out_spec, out_shape = o_spec, (m, n)
    if shard_cols is not None:
        assert res is None and tn % shard_cols == 0
        out_spec = pl.BlockSpec((tn // shard_cols, tm, shard_cols), lambda i, j, kk: (j, i, 0))
        out_shape = (n // shard_cols, m, shard_cols)
    (out,), exchanged = _pallas(
        body, name=name, grid=(m // tm, n // tn, nk), in_specs=in_specs, out_specs=[out_spec],
        out_shape=[jax.ShapeDtypeStruct(out_shape, out_dtype)],
        scratch_shapes=[pltpu.VMEM((tm, tn) if nk > 1 else (8, LANE), F32)],
        semantics=("parallel", "parallel", "arbitrary"), args=args, exchange=exchange)
    return out if exchange is None else (out, exchanged)


def _rms_fwd(x, w, name):
    s, d = x.shape
    tm = min(512, s)

    def body(x_ref, w_ref, n_ref, r_ref):
        xv = x_ref[...]
        r = lax.rsqrt(jnp.mean(xv * xv, axis=-1, keepdims=True) + EPS)
        n_ref[...] = (xv * r * w_ref[...]).astype(BF16)
        r_ref[...] = r

    return pl.pallas_call(
        body, name=name, grid=(s // tm,),
        in_specs=[pl.BlockSpec((tm, d), lambda i: (i, 0)), pl.BlockSpec((1, d), lambda i: (0, 0))],
        out_specs=[pl.BlockSpec((tm, d), lambda i: (i, 0)), pl.BlockSpec((tm, 1), lambda i: (i, 0))],
        out_shape=[jax.ShapeDtypeStruct((s, d), BF16), jax.ShapeDtypeStruct((s, 1), F32)],
        compiler_params=_params("parallel"),
    )(x, w)


def _rms_bwd(dn, x, r, w, dres, name, exchange=None):
    s, d = x.shape
    tm = min(512, s)

    def body(dn_ref, x_ref, r_ref, w_ref, dres_ref, dx_ref, dw_ref):
        @pl.when(pl.program_id(0) == 0)
        def _():
            dw_ref[...] = jnp.zeros_like(dw_ref)

        rv = r_ref[...]
        xhat = x_ref[...] * rv
        g = dn_ref[...]
        t = g * w_ref[...]
        dx_ref[...] = dres_ref[...] + rv * (t - xhat * jnp.mean(t * xhat, axis=-1, keepdims=True))
        dw_ref[...] += jnp.sum(g * xhat, axis=0, keepdims=True)

    row = pl.BlockSpec((tm, d), lambda i: (i, 0))
    vec = pl.BlockSpec((1, d), lambda i: (0, 0))
    (dx, dw), exchanged = _pallas(
        body, name=name, grid=(s // tm,),
        in_specs=[row, row, pl.BlockSpec((tm, 1), lambda i: (i, 0)), vec, row],
        out_specs=[row, vec],
        out_shape=[jax.ShapeDtypeStruct((s, d), F32), jax.ShapeDtypeStruct((1, d), F32)],
        semantics=("arbitrary",), args=(dn, x, r, w, dres), exchange=exchange)
    return (dx, dw) if exchange is None else (dx, dw, exchanged)


def _final_loss(x3, wf, tgt):
    s, d = x3.shape
    tm = min(512, s)

    def body(x_ref, w_ref, t_ref, loss_ref, dx_ref, dw_ref):
        @pl.when(pl.program_id(0) == 0)
        def _():
            dw_ref[...] = jnp.zeros_like(dw_ref)
            loss_ref[...] = jnp.zeros_like(loss_ref)

        xv = x_ref[...]
        wv = w_ref[...]
        r = lax.rsqrt(jnp.mean(xv * xv, axis=-1, keepdims=True) + EPS)
        xhat = xv * r
        e = xhat * wv - t_ref[...]
        part = 0.5 * jnp.sum(jnp.mean(e * e, axis=-1, keepdims=True), axis=0, keepdims=True)
        loss_ref[...] += jnp.broadcast_to(part, loss_ref.shape)
        dy = e * (1.0 / d)
        dw_ref[...] += jnp.sum(dy * xhat, axis=0, keepdims=True)
        t = dy * wv
        dx_ref[...] = r * (t - xhat * jnp.mean(t * xhat, axis=-1, keepdims=True))

    row = pl.BlockSpec((tm, d), lambda i: (i, 0))
    vec = pl.BlockSpec((1, d), lambda i: (0, 0))
    return pl.pallas_call(
        body, name="final_loss", grid=(s // tm,),
        in_specs=[row, vec, row],
        out_specs=[pl.BlockSpec((1, LANE), lambda i: (0, 0)), row, vec],
        out_shape=[jax.ShapeDtypeStruct((1, LANE), F32), jax.ShapeDtypeStruct((s, d), F32),
                   jax.ShapeDtypeStruct((1, d), F32)],
        compiler_params=_params("arbitrary"),
    )(x3, wf, tgt)


def _ffn_up(n, wg, wu, name, exchange=None):
    s, d = n.shape
    f = wg.shape[1]
    tm, tn = min(512, s), f // 2

    def body(n_ref, wg_ref, wu_ref, g_ref, u_ref, a_ref):
        nv = n_ref[...]
        g = _dot(nv, wg_ref[...])
        u = _dot(nv, wu_ref[...])
        g_ref[...] = g.astype(BF16)
        u_ref[...] = u.astype(BF16)
        a_ref[...] = (g * _sigmoid(g) * u).astype(BF16)

    o = pl.BlockSpec((tm, tn), lambda j, i: (i, j))
    wspec = pl.BlockSpec((d, tn), lambda j, i: (0, j))
    return _pallas(
        body, name=name, grid=(f // tn, s // tm),
        in_specs=[pl.BlockSpec((tm, d), lambda j, i: (i, 0)), wspec, wspec],
        out_specs=[o, o, o],
        out_shape=[jax.ShapeDtypeStruct((s, f), BF16)] * 3,
        semantics=("parallel", "parallel"), args=(n, wg, wu), exchange=exchange)


def _ffn_dact(dx, wd, g, u, name, exchange=None):
    s, d = dx.shape
    f = wd.shape[0]
    tm, tn = min(512, s), f // 2

    def body(dx_ref, wd_ref, g_ref, u_ref, dg_ref, du_ref):
        da = 0.5 * _dot(dx_ref[...], wd_ref[...], NT)
        gv = g_ref[...].astype(F32)
        sg = _sigmoid(gv)
        du_ref[...] = (da * gv * sg).astype(BF16)
        dg_ref[...] = (da * u_ref[...].astype(F32) * (sg * (1.0 + gv * (1.0 - sg)))).astype(BF16)

    o = pl.BlockSpec((tm, tn), lambda j, i: (i, j))
    return _pallas(
        body, name=name, grid=(f // tn, s // tm),
        in_specs=[pl.BlockSpec((tm, d), lambda j, i: (i, 0)), pl.BlockSpec((tn, d), lambda j, i: (j, 0)), o, o],
        out_specs=[o, o],
        out_shape=[jax.ShapeDtypeStruct((s, f), BF16), jax.ShapeDtypeStruct((s, f), BF16)],
        semantics=("parallel", "parallel"), args=(dx, wd, g, u), exchange=exchange)


def _row_slabs(full):
    return full.reshape(N_DEV, full.shape[0] // N_DEV, full.shape[1])


def _ffn_forward(x, norm_w, wg, wu, wd, tag, gather=()):
    n, r = _rms_fwd(x, norm_w, f"{tag}_norm")
    (g, u, a), got = _ffn_up(n, wg, wu, f"{tag}_up", _Exchange("gather", gather) if gather else None)
    if wd is None:
        wd, got = got[0].reshape(N_DEV * got[0].shape[1], got[0].shape[2]), got[1:]
    y = _matmul([(a, wd)], tm=512, tn=1024, tk=wd.shape[0], name=f"{tag}_down", res=x, alpha=0.5)
    return y, (n, r, g, u, a), wd, got


def _ffn_backward(dy, x, norm_w, wg, wu, wd, saved, tag, dw_dtype=F32, scatter=None):
    n, r, g, u, a = saved
    cols = wg.shape[1] // N_DEV

    def behind(arrays):
        return _Exchange("scatter", arrays) if scatter is not None else None

    dwd = _row_slabs(_matmul([(a, dy)], ta=True, tm=1408, tn=1024, tk=2048, name=f"{tag}_dwd", alpha=0.5,
                             out_dtype=dw_dtype))
    (dg, du), extras = _ffn_dact(dy, wd, g, u, f"{tag}_dact", behind(scatter))
    dwg = _matmul([(n, dg)], ta=True, tm=512, tn=1408, tk=4096, name=f"{tag}_dwg", shard_cols=cols, out_dtype=dw_dtype)
    dwu = _matmul([(n, du)], ta=True, tm=512, tn=1408, tk=4096, name=f"{tag}_dwu", shard_cols=cols, out_dtype=dw_dtype,
                  exchange=behind([dwd]))
    dn = _matmul([(dg, wg), (du, wu)], tb=True, tm=512, tn=1024, tk=wg.shape[1], name=f"{tag}_dn", exchange=behind([dwg]))
    last = _rms_bwd(dn[0] if scatter is not None else dn, x, r, norm_w, dy, f"{tag}_dnorm",
                    behind([dwu[0]]) if scatter is not None else None)
    if scatter is None:
        return last[0], last[1], dwg, dwu, dwd, []
    return last[0], last[1], dn[1][0], last[2][0], dwu[1][0], extras


Q_SCALE = GDN_HEAD_DIM ** -0.5
CONV_HALO = 8


def _conv_taps(win, w_ref, rows, sign):
    n = rows + 2 * CONV_HALO
    acc = None
    for t in range(CONV_WIDTH):
        o = sign * (t - CONV_WIDTH // 2)
        sh = win if o == 0 else pltpu.roll(win, (-o) % n, 0)
        term = sh[CONV_HALO:CONV_HALO + rows] * w_ref[t:t + 1, :]
        acc = term if acc is None else acc + term
    return acc


def _gdn_conv_fwd(p_pad, conv_wt):
    s = p_pad.shape[0]
    rows = min(CONV_ROWS, s)
    nblk = QKV_A // LANE

    def body(p_ref, w_ref, c_ref, y_ref, pad):
        j = pl.program_id(0)
        zeros = jnp.zeros((CONV_HALO, LANE), F32)
        pad[0:CONV_HALO, :] = zeros
        pad[CONV_HALO + s:2 * CONV_HALO + s, :] = zeros
        pad[CONV_HALO:CONV_HALO + s, :] = p_ref[...]

        def chunk(ci, carry):
            b = pl.multiple_of(ci * rows, rows)
            win = pad[pl.ds(b, rows + 2 * CONV_HALO), :]
            c = _conv_taps(win, w_ref, rows, 1)
            c_ref[pl.ds(b, rows), :] = c
            act = c * _sigmoid(c)
            nrm = lax.rsqrt(jnp.sum(act * act, axis=-1, keepdims=True) + EPS)
            mult = jnp.where(j < GDN_HEADS, nrm * Q_SCALE, jnp.where(j < 2 * GDN_HEADS, nrm, 1.0))
            y_ref[pl.ds(b, rows), :] = act * mult
            return carry

        lax.fori_loop(0, s // rows, chunk, 0)

    col = pl.BlockSpec((s, LANE), lambda j: (0, j))
    return pl.pallas_call(
        body, name="gdn_conv_fwd", grid=(nblk,),
        in_specs=[col, pl.BlockSpec((8, LANE), lambda j: (0, j))],
        out_specs=[col, col],
        out_shape=[jax.ShapeDtypeStruct((s, QKV_A), F32), jax.ShapeDtypeStruct((s, QKV_A), F32)],
        scratch_shapes=[pltpu.VMEM((s + 2 * CONV_HALO, LANE), F32)],
        compiler_params=_params("parallel"),
    )(p_pad, conv_wt)


def _gdn_conv_bwd(dy_f, dy_r, c_pre, p_pad, conv_wt):
    s = p_pad.shape[0]
    rows = min(CONV_ROWS, s)
    nblk = QKV_A // LANE

    def body(dyf_ref, dyr_ref, c_ref, p_ref, w_ref, dp_ref, dw_ref, ppad, dcpad):
        j = pl.program_id(0)
        zeros = jnp.zeros((CONV_HALO, LANE), F32)
        for buf in (ppad, dcpad):
            buf[0:CONV_HALO, :] = zeros
            buf[CONV_HALO + s:2 * CONV_HALO + s, :] = zeros
        ppad[CONV_HALO:CONV_HALO + s, :] = p_ref[...]

        def act_bwd(ci, carry):
            b = pl.multiple_of(ci * rows, rows)
            c = c_ref[pl.ds(b, rows), :]
            g = dyf_ref[pl.ds(b, rows), :] + dyr_ref[pl.ds(b, rows), :]
            sg = _sigmoid(c)
            act = c * sg
            nrm = lax.rsqrt(jnp.sum(act * act, axis=-1, keepdims=True) + EPS)
            yh = act * nrm
            scale = jnp.where(j < GDN_HEADS, Q_SCALE, 1.0)
            dact_qk = (scale * nrm) * (g - yh * jnp.sum(g * yh, axis=-1, keepdims=True))
            dact = jnp.where(j < 2 * GDN_HEADS, dact_qk, g)
            dcpad[pl.ds(pl.multiple_of(b + CONV_HALO, CONV_HALO), rows), :] = dact * (sg * (1.0 + c * (1.0 - sg)))
            return carry

        lax.fori_loop(0, s // rows, act_bwd, 0)
        tap = lax.broadcasted_iota(jnp.int32, (8, LANE), 0)

        def taps_bwd(ci, dw):
            b = pl.multiple_of(ci * rows, rows)
            dcw = dcpad[pl.ds(b, rows + 2 * CONV_HALO), :]
            dp_ref[pl.ds(b, rows), :] = _conv_taps(dcw, w_ref, rows, -1)
            pw = ppad[pl.ds(b, rows + 2 * CONV_HALO), :]
            dc = dcw[CONV_HALO:CONV_HALO + rows]
            n = rows + 2 * CONV_HALO
            for t in range(CONV_WIDTH):
                o = t - CONV_WIDTH // 2
                sh = pw if o == 0 else pltpu.roll(pw, (-o) % n, 0)
                row = jnp.sum(dc * sh[CONV_HALO:CONV_HALO + rows], axis=0, keepdims=True)
                dw = dw + jnp.where(tap == t, row, 0.0)
            return dw

        dw_ref[...] = lax.fori_loop(0, s // rows, taps_bwd, jnp.zeros((8, LANE), F32))

    col = pl.BlockSpec((s, LANE), lambda j: (0, j))
    wspec = pl.BlockSpec((8, LANE), lambda j: (0, j))
    return pl.pallas_call(
        body, name="gdn_conv_bwd", grid=(nblk,),
        in_specs=[col, col, col, col, wspec],
        out_specs=[col, wspec],
        out_shape=[jax.ShapeDtypeStruct((s, QKV_A), F32), jax.ShapeDtypeStruct((8, QKV_A), F32)],
        scratch_shapes=[pltpu.VMEM((s + 2 * CONV_HALO, LANE), F32), pltpu.VMEM((s + 2 * CONV_HALO, LANE), F32)],
        compiler_params=_params("parallel"),
    )(dy_f, dy_r, c_pre, p_pad, conv_wt)


def _softplus(x):
    return jnp.maximum(x, 0.0) + jnp.log(1.0 + jnp.exp(-jnp.abs(x)))


def _gdn_gates_fwd(p_pad, alog_row, dt_row):
    s = p_pad.shape[0]
    tm = min(1024, s)

    def body(p_ref, al_ref, dt_ref, o_ref):
        x = p_ref[...]
        lane = lax.broadcasted_iota(jnp.int32, x.shape, 1)
        g = -jnp.exp(al_ref[...]) * _softplus(x + dt_ref[...])
        o_ref[...] = jnp.where(lane < 8, g, jnp.where(lane < 16, _sigmoid(x), 0.0))

    vec = pl.BlockSpec((1, LANE), lambda i: (0, 0))
    return pl.pallas_call(
        body, name="gdn_gates_fwd", grid=(s // tm,),
        in_specs=[pl.BlockSpec((tm, LANE), lambda i: (i, OFF_AB // LANE)), vec, vec],
        out_specs=pl.BlockSpec((tm, LANE), lambda i: (i, 0)),
        out_shape=jax.ShapeDtypeStruct((s, LANE), F32),
        compiler_params=_params("parallel"),
    )(p_pad, alog_row, dt_row)


def _gdn_gates_bwd(dgb_f, dgb_r, p_pad, gb, alog_row, dt_row):
    s = p_pad.shape[0]
    tm = min(1024, s)

    def body(df_ref, dr_ref, p_ref, gb_ref, al_ref, dt_ref, dp_ref, sum_ref):
        @pl.when(pl.program_id(0) == 0)
        def _():
            sum_ref[...] = jnp.zeros_like(sum_ref)

        x = p_ref[...]
        gbv = gb_ref[...]
        dgb = df_ref[...] + dr_ref[...]
        lane = lax.broadcasted_iota(jnp.int32, x.shape, 1)
        da = dgb * (-jnp.exp(al_ref[...])) * _sigmoid(x + dt_ref[...])
        db = dgb * gbv * (1.0 - gbv)
        dp_ref[...] = jnp.where(lane < 8, da, jnp.where(lane < 16, db, 0.0))
        row = lax.broadcasted_iota(jnp.int32, (8, LANE), 0)
        lane8 = lax.broadcasted_iota(jnp.int32, (8, LANE), 1)
        d_alog = jnp.sum(dgb * gbv, axis=0, keepdims=True)
        d_dt = jnp.sum(da, axis=0, keepdims=True)
        upd = jnp.where(row == 0, d_alog, jnp.where(row == 1, d_dt, 0.0))
        sum_ref[...] += jnp.where(lane8 < 8, upd, 0.0)

    vec = pl.BlockSpec((1, LANE), lambda i: (0, 0))
    blk = pl.BlockSpec((tm, LANE), lambda i: (i, 0))
    return pl.pallas_call(
        body, name="gdn_gates_bwd", grid=(s // tm,),
        in_specs=[blk, blk, pl.BlockSpec((tm, LANE), lambda i: (i, OFF_AB // LANE)), blk, vec, vec],
        out_specs=[blk, pl.BlockSpec((8, LANE), lambda i: (0, 0))],
        out_shape=[jax.ShapeDtypeStruct((s, LANE), F32), jax.ShapeDtypeStruct((8, LANE), F32)],
        compiler_params=_params("arbitrary"),
    )(dgb_f, dgb_r, p_pad, gb, alog_row, dt_row)


def _chunk_masks(rev):
    row = lax.broadcasted_iota(jnp.int32, (CHUNK, CHUNK), 0)
    col = lax.broadcasted_iota(jnp.int32, (CHUNK, CHUNK), 1)
    le = (col >= row) if rev else (col <= row)
    strict = (col > row) if rev else (col < row)
    return le, strict, row == col


def _chunk_common(q, k, v, g, beta, gc, masks):
    le, strict, eye = masks
    gc_row = _dot_hi(jnp.ones((CHUNK, CHUNK), F32), jnp.where(eye, gc, 0.0))
    decay = jnp.where(le, jnp.exp(jnp.where(le, gc - gc_row, 0.0)), 0.0)
    eg = jnp.exp(gc)
    gl = jnp.sum(g, axis=0, keepdims=True)
    kb = k * beta
    vb = v * beta
    kbeg = kb * eg
    lm = jnp.where(strict, _dot(kb, k, NT) * decay, 0.0)
    intra = _dot(q, k, NT) * decay
    qg = q * eg
    edec = jnp.exp(gl - gc)
    kdec = k * edec
    return dict(decay=decay, eg=eg, gl=gl, kb=kb, vb=vb, kbeg=kbeg, lm=lm, intra=intra, qg=qg, edec=edec, kdec=kdec)


def _unit_lower_inverse(lm, eye):
    x = -lm
    t = eye.astype(F32) + x
    p = x
    for _ in range(5):
        p = _dot_hi(p, p)
        t = t + _dot_hi(t, p)
    return t


def _gate_lanes(rev, h):
    d = 1 if rev else 0
    return d * GDN_HEADS + h, 8 + d * GDN_HEADS + h


def _delta_fwd(y, gb, rev):
    s = y.shape[0]
    nc = s // CHUNK
    hd = GDN_HEAD_DIM

    def chunk_of(n):
        return nc - 1 - n if rev else n

    def body(q_ref, k_ref, v_ref, gb_ref, o_ref, s_all, t_all, state):
        @pl.when(pl.program_id(0) == 0)
        def _():
            state[...] = jnp.zeros_like(state)

        masks = _chunk_masks(rev)
        gbv = gb_ref[...]
        gcm = _dot_hi(masks[0].astype(F32), gbv)
        for h in range(GDN_HEADS):
            gi, bi = _gate_lanes(rev, h)
            sl = slice(h * hd, (h + 1) * hd)
            q, k, v = q_ref[:, sl], k_ref[:, sl], v_ref[:, sl]
            g, beta, gc = gbv[:, gi:gi + 1], gbv[:, bi:bi + 1], gcm[:, gi:gi + 1]
            cm = _chunk_common(q, k, v, g, beta, gc, masks)
            tinv = _unit_lower_inverse(cm["lm"], masks[2])
            u = _dot(tinv, cm["vb"])
            w = _dot(tinv, cm["kbeg"])
            st = state[h]
            v_new = u - _dot(w, st)
            o_ref[:, sl] = _dot(cm["qg"], st) + _dot(cm["intra"], v_new)
            s_all[0, h] = st
            t_all[0, h] = tinv
            state[h] = st * jnp.exp(cm["gl"]) + _dot(cm["kdec"], v_new, TN)

    def col(j):
        return pl.BlockSpec((CHUNK, GDN_WIDTH), lambda n: (chunk_of(n), j))

    return pl.pallas_call(
        body, name="delta_fwd_r" if rev else "delta_fwd_f", grid=(nc,),
        in_specs=[col(0), col(1), col(2), pl.BlockSpec((CHUNK, LANE), lambda n: (chunk_of(n), 0))],
        out_specs=[pl.BlockSpec((CHUNK, GDN_WIDTH), lambda n: (chunk_of(n), 0)),
                   pl.BlockSpec((1, GDN_HEADS, hd, hd), lambda n: (chunk_of(n), 0, 0, 0)),
                   pl.BlockSpec((1, GDN_HEADS, CHUNK, CHUNK), lambda n: (chunk_of(n), 0, 0, 0))],
        out_shape=[jax.ShapeDtypeStruct((s, GDN_WIDTH), F32),
                   jax.ShapeDtypeStruct((nc, GDN_HEADS, hd, hd), F32),
                   jax.ShapeDtypeStruct((nc, GDN_HEADS, CHUNK, CHUNK), F32)],
        scratch_shapes=[pltpu.VMEM((GDN_HEADS, hd, hd), F32)],
        compiler_params=_params("arbitrary"),
    )(y, y, y, gb)


def _delta_bwd(y, gb, do, s_all, t_all, rev):
    s = y.shape[0]
    nc = s // CHUNK
    hd = GDN_HEAD_DIM

    def chunk_of(n):
        return n if rev else nc - 1 - n

    def body(q_ref, k_ref, v_ref, gb_ref, do_ref, s_ref, t_ref, dy_ref, dgb_ref, dstate):
        @pl.when(pl.program_id(0) == 0)
        def _():
            dstate[...] = jnp.zeros_like(dstate)

        masks = _chunk_masks(rev)
        le, strict, _ = masks
        le_t = _chunk_masks(not rev)[0].astype(F32)
        gbv = gb_ref[...]
        gcm = _dot_hi(le.astype(F32), gbv)
        lane = lax.broadcasted_iota(jnp.int32, (CHUNK, LANE), 1)
        ones_cl = jnp.ones((CHUNK, LANE), F32)
        dgc_tile = jnp.zeros((CHUNK, LANE), F32)
        rest_tile = jnp.zeros((CHUNK, LANE), F32)
        for h in range(GDN_HEADS):
            gi, bi = _gate_lanes(rev, h)
            sl = slice(h * hd, (h + 1) * hd)
            q, k, v = q_ref[:, sl], k_ref[:, sl], v_ref[:, sl]
            g, beta, gc = gbv[:, gi:gi + 1], gbv[:, bi:bi + 1], gcm[:, gi:gi + 1]
            cm = _chunk_common(q, k, v, g, beta, gc, masks)
            tinv = t_ref[0, h]
            st = s_ref[0, h]
            ds_out = dstate[h]
            dov = do_ref[:, sl]
            u = _dot(tinv, cm["vb"])
            w = _dot(tinv, cm["kbeg"])
            v_new = u - _dot(w, st)
            egl = jnp.exp(cm["gl"])
            d_qg = _dot(dov, st, NT)
            d_intra = _dot(dov, v_new, NT)
            dv_new = _dot(cm["intra"], dov, TN) + _dot(cm["kdec"], ds_out)
            d_kdec = _dot(v_new, ds_out, NT)
            dstate[h] = _dot(cm["qg"], dov, TN) + egl * ds_out - _dot(w, dv_new, TN)
            dgl = egl * jnp.sum(jnp.sum(st * ds_out, axis=1, keepdims=True), axis=0, keepdims=True)
            dw = -_dot(dv_new, st, NT)
            dvb = _dot(tinv, dv_new, TN)
            dkbeg = _dot(tinv, dw, TN)
            dlm = jnp.where(strict, -(_dot(dvb, u, NT) + _dot(dkbeg, w, NT)), 0.0)
            d_a = dlm * cm["decay"]
            d_qk = d_intra * cm["decay"]
            e = dlm * cm["lm"] + d_intra * cm["intra"]
            dgc = jnp.sum(e, axis=1, keepdims=True) - _dot_hi(e, ones_cl, TN)[:, 0:1]
            dkb = _dot(d_a, k) + dkbeg * cm["eg"]
            dk = _dot(d_a, cm["kb"], TN) + _dot(d_qk, q, TN)
            dq = _dot(d_qk, k) + d_qg * cm["eg"]
            dgc = dgc + jnp.sum(d_qg * cm["qg"], axis=1, keepdims=True)
            dgc = dgc + jnp.sum(dkbeg * cm["kbeg"], axis=1, keepdims=True)
            tdec = jnp.sum(d_kdec * cm["kdec"], axis=1, keepdims=True)
            dk = dk + d_kdec * cm["edec"] + dkb * beta
            dgc = dgc - tdec
            dgl = dgl + jnp.sum(tdec, axis=0, keepdims=True)
            dbeta = jnp.sum(dvb * v, axis=1, keepdims=True) + jnp.sum(dkb * k, axis=1, keepdims=True)
            dy_ref[:, h * hd:(h + 1) * hd] = dq
            dy_ref[:, GDN_WIDTH + h * hd:GDN_WIDTH + (h + 1) * hd] = dk
            dy_ref[:, 2 * GDN_WIDTH + h * hd:2 * GDN_WIDTH + (h + 1) * hd] = dvb * beta
            dgc_tile = dgc_tile + jnp.where(lane == gi, dgc, 0.0)
            rest_tile = rest_tile + jnp.where(lane == gi, dgl, 0.0) + jnp.where(lane == bi, dbeta, 0.0)
        dgb_ref[...] = _dot_hi(le_t, dgc_tile) + rest_tile

    def col(j):
        return pl.BlockSpec((CHUNK, GDN_WIDTH), lambda n: (chunk_of(n), j))

    first = pl.BlockSpec((CHUNK, GDN_WIDTH), lambda n: (chunk_of(n), 0))
    return pl.pallas_call(
        body, name="delta_bwd_r" if rev else "delta_bwd_f", grid=(nc,),
        in_specs=[col(0), col(1), col(2), pl.BlockSpec((CHUNK, LANE), lambda n: (chunk_of(n), 0)), first,
                  pl.BlockSpec((1, GDN_HEADS, hd, hd), lambda n: (chunk_of(n), 0, 0, 0)),
                  pl.BlockSpec((1, GDN_HEADS, CHUNK, CHUNK), lambda n: (chunk_of(n), 0, 0, 0))],
        out_specs=[pl.BlockSpec((CHUNK, QKV_A), lambda n: (chunk_of(n), 0)),
                   pl.BlockSpec((CHUNK, LANE), lambda n: (chunk_of(n), 0))],
        out_shape=[jax.ShapeDtypeStruct((s, QKV_A), F32), jax.ShapeDtypeStruct((s, LANE), F32)],
        scratch_shapes=[pltpu.VMEM((GDN_HEADS, hd, hd), F32)],
        compiler_params=_params("arbitrary"),
    )(y, y, y, gb, do, s_all, t_all)


BNN = (((2,), (1,)), ((0,), (0,)))
BNT = (((2,), (2,)), ((0,), (0,)))
BTN = (((1,), (1,)), ((0,), (0,)))
NB = 2 * GDN_HEADS


def _bdot(a, b, dn=BNN):
    return lax.dot_general(a.astype(BF16), b.astype(BF16), dn, preferred_element_type=F32)


def _dot3(a, b, dn):
    ah = a.astype(BF16)
    al = (a - ah.astype(F32)).astype(BF16)
    bh = b.astype(BF16)
    bl = (b - bh.astype(F32)).astype(BF16)

    def d(x, y):
        return lax.dot_general(x, y, dn, preferred_element_type=F32)

    return d(ah, bh) + d(ah, bl) + d(al, bh)


def _both(f_val, r_val):
    return jnp.stack([f_val] * GDN_HEADS + [r_val] * GDN_HEADS)


def _heads(ref_f, ref_r):
    hd = GDN_HEAD_DIM
    return jnp.stack([ref_f[:, h * hd:(h + 1) * hd] for h in range(GDN_HEADS)]
                     + [ref_r[:, h * hd:(h + 1) * hd] for h in range(GDN_HEADS)])


def _gate_cols(tile_f, tile_r, base):
    return jnp.stack([tile_f[:, base + h:base + h + 1] for h in range(GDN_HEADS)]
                     + [tile_r[:, base + GDN_HEADS + h:base + GDN_HEADS + h + 1] for h in range(GDN_HEADS)])


def _chunk_common2(q, k, v, gbf, gbr):
    mf, mr = _chunk_masks(False), _chunk_masks(True)
    le, strict = _both(mf[0], mr[0]), _both(mf[1], mr[1])
    eye = mf[2]
    gcm_f = _dot3(mf[0].astype(F32), gbf, NN)
    gcm_r = _dot3(mr[0].astype(F32), gbr, NN)
    g, beta, gc = _gate_cols(gbf, gbr, 0), _gate_cols(gbf, gbr, 8), _gate_cols(gcm_f, gcm_r, 0)
    gc_row = _dot3(jnp.ones((NB, CHUNK, CHUNK), F32), jnp.where(eye[None], gc, 0.0), BNN)
    decay = jnp.where(le, jnp.exp(jnp.where(le, gc - gc_row, 0.0)), 0.0)
    eg = jnp.exp(gc)
    gl = jnp.sum(g, axis=1, keepdims=True)
    kb = k * beta
    vb = v * beta
    kbeg = kb * eg
    lm = jnp.where(strict, _bdot(kb, k, BNT) * decay, 0.0)
    intra = _bdot(q, k, BNT) * decay
    edec = jnp.exp(gl - gc)
    return dict(strict=strict, eye=eye, beta=beta, decay=decay, eg=eg, gl=gl, kb=kb, vb=vb, kbeg=kbeg,
                lm=lm, intra=intra, qg=q * eg, edec=edec, kdec=k * edec)


def _unit_triangular_inverse(lm, eye):
    x = -lm
    t = eye[None].astype(F32) + x
    p = x
    for _ in range(5):
        p = _dot3(p, p, BNN)
        t = t + _dot3(t, p, BNN)
    return t


def _delta_fwd2(y, gb, gather=()):
    s = y.shape[0]
    nc = s // CHUNK
    hd = GDN_HEAD_DIM
    na = len(gather)

    def body(*refs):
        qf, kf, vf, gf, qr, kr, vr, gr = refs[:8]
        of_ref, or_ref, sf_all, sr_all, tf_all, tr_all = refs[8 + na:14 + na]
        state = refs[14 + 2 * na]
        step = pl.program_id(0)

        @pl.when(step == 0)
        def _():
            state[...] = jnp.zeros_like(state)

        if na:
            start, forward, finish = _gather_phases(refs[8:8 + na], refs[14 + na:14 + 2 * na], *refs[15 + 2 * na:])
            pl.when(step == 0)(start)
            pl.when(step == nc // 2)(forward)
            pl.when(step == nc - 1)(finish)

        q, k, v = _heads(qf, qr), _heads(kf, kr), _heads(vf, vr)
        cm = _chunk_common2(q, k, v, gf[...], gr[...])
        tinv = _unit_triangular_inverse(cm["lm"], cm["eye"])
        u = _bdot(tinv, cm["vb"])
        w = _bdot(tinv, cm["kbeg"])
        st = state[...]
        v_new = u - _bdot(w, st)
        o = _bdot(cm["qg"], st) + _bdot(cm["intra"], v_new)
        state[...] = st * jnp.exp(cm["gl"]) + _bdot(cm["kdec"], v_new, BTN)
        for h in range(GDN_HEADS):
            of_ref[:, h * hd:(h + 1) * hd] = o[h]
            or_ref[:, h * hd:(h + 1) * hd] = o[GDN_HEADS + h]
        sf_all[0] = st[:GDN_HEADS]
        sr_all[0] = st[GDN_HEADS:]
        tf_all[0] = tinv[:GDN_HEADS]
        tr_all[0] = tinv[GDN_HEADS:]

    def col(j, rev):
        return pl.BlockSpec((CHUNK, GDN_WIDTH), (lambda n: (nc - 1 - n, j)) if rev else (lambda n: (n, j)))

    def gate(rev):
        return pl.BlockSpec((CHUNK, LANE), (lambda n: (nc - 1 - n, 0)) if rev else (lambda n: (n, 0)))

    def per_chunk(d1, d2, rev):
        return pl.BlockSpec((1, GDN_HEADS, d1, d2), (lambda n: (nc - 1 - n, 0, 0, 0)) if rev else (lambda n: (n, 0, 0, 0)))

    assert na == 0 or nc >= 4
    res = pl.pallas_call(
        body, name="delta_fwd", grid=(nc,),
        in_specs=[col(0, False), col(1, False), col(2, False), gate(False), col(0, True), col(1, True), col(2, True), gate(True)]
        + [ANY] * na,
        out_specs=[col(0, False), col(0, True), per_chunk(hd, hd, False), per_chunk(hd, hd, True),
                   per_chunk(CHUNK, CHUNK, False), per_chunk(CHUNK, CHUNK, True)] + [ANY] * na,
        out_shape=[jax.ShapeDtypeStruct((s, GDN_WIDTH), F32)] * 2 + [jax.ShapeDtypeStruct((nc, GDN_HEADS, hd, hd), F32)] * 2
        + [jax.ShapeDtypeStruct((nc, GDN_HEADS, CHUNK, CHUNK), F32)] * 2
        + [jax.ShapeDtypeStruct((N_DEV,) + v.shape, v.dtype) for v in gather],
        scratch_shapes=[pltpu.VMEM((NB, hd, hd), F32)] + (_gather_semaphores(na) if na else []),
        compiler_params=_params("arbitrary"),
    )(y, y, y, gb, y, y, y, gb, *gather)
    return res[:6], res[6:]


def _delta_bwd2(y, gb, do, sf_all, sr_all, tf_all, tr_all, scatter=()):
    s = y.shape[0]
    nc = s // CHUNK
    hd = GDN_HEAD_DIM
    na = len(scatter)

    def body(*refs):
        qf, kf, vf, gf, dof, sf, tf, qr, kr, vr, gr, dor, sr, tr = refs[:14]
        dyf_ref, dyr_ref, dgf_ref, dgr_ref = refs[14 + na:18 + na]
        dstate = refs[18 + 2 * na]
        step = pl.program_id(0)

        @pl.when(step == 0)
        def _():
            dstate[...] = jnp.zeros_like(dstate)

        if na:
            start, finish = _scatter_phases(refs[14:14 + na], refs[18 + na:18 + 2 * na], *refs[19 + 2 * na:])
            pl.when(step == 0)(start)
            pl.when(step == nc - 1)(finish)

        q, k, v, dov = _heads(qf, qr), _heads(kf, kr), _heads(vf, vr), _heads(dof, dor)
        cm = _chunk_common2(q, k, v, gf[...], gr[...])
        tinv = jnp.concatenate([tf[0], tr[0]], axis=0)
        st = jnp.concatenate([sf[0], sr[0]], axis=0)
        ds_out = dstate[...]
        decay, lm, intra, qg, kdec, kbeg, eg, kb, beta = (
            cm[n] for n in ("decay", "lm", "intra", "qg", "kdec", "kbeg", "eg", "kb", "beta"))
        u = _bdot(tinv, cm["vb"])
        w = _bdot(tinv, kbeg)
        v_new = u - _bdot(w, st)
        egl = jnp.exp(cm["gl"])
        d_qg = _bdot(dov, st, BNT)
        d_intra = _bdot(dov, v_new, BNT)
        dv_new = _bdot(intra, dov, BTN) + _bdot(kdec, ds_out)
        d_kdec = _bdot(v_new, ds_out, BNT)
        dstate[...] = _bdot(qg, dov, BTN) + egl * ds_out - _bdot(w, dv_new, BTN)
        dgl = egl * jnp.sum(jnp.sum(st * ds_out, axis=2, keepdims=True), axis=1, keepdims=True)
        dw = -_bdot(dv_new, st, BNT)
        dvb = _bdot(tinv, dv_new, BTN)
        dkbeg = _bdot(tinv, dw, BTN)
        dlm = jnp.where(cm["strict"], -(_bdot(dvb, u, BNT) + _bdot(dkbeg, w, BNT)), 0.0)
        d_a = dlm * decay
        d_qk = d_intra * decay
        e = dlm * lm + d_intra * intra
        colsum = _dot3(e, jnp.ones((NB, CHUNK, LANE), F32), BTN)[:, :, 0:1]
        dgc = jnp.sum(e, axis=2, keepdims=True) - colsum
        dkb = _bdot(d_a, k) + dkbeg * eg
        dk = _bdot(d_a, kb, BTN) + _bdot(d_qk, q, BTN)
        dq = _bdot(d_qk, k) + d_qg * eg
        dgc = dgc + jnp.sum(d_qg * qg, axis=2, keepdims=True) + jnp.sum(dkbeg * kbeg, axis=2, keepdims=True)
        tdec = jnp.sum(d_kdec * kdec, axis=2, keepdims=True)
        dk = dk + d_kdec * cm["edec"] + dkb * beta
        dgc = dgc - tdec
        dgl = dgl + jnp.sum(tdec, axis=1, keepdims=True)
        dbeta = jnp.sum(dvb * v, axis=2, keepdims=True) + jnp.sum(dkb * k, axis=2, keepdims=True)
        dv = dvb * beta
        lane = lax.broadcasted_iota(jnp.int32, (CHUNK, LANE), 1)
        for rev, dy_ref, dg_ref in ((False, dyf_ref, dgf_ref), (True, dyr_ref, dgr_ref)):
            dgc_tile = jnp.zeros((CHUNK, LANE), F32)
            rest = jnp.zeros((CHUNK, LANE), F32)
            for h in range(GDN_HEADS):
                b = (GDN_HEADS if rev else 0) + h
                gi, bi = _gate_lanes(rev, h)
                dgc_tile = dgc_tile + jnp.where(lane == gi, dgc[b], 0.0)
                rest = rest + jnp.where(lane == gi, dgl[b], 0.0) + jnp.where(lane == bi, dbeta[b], 0.0)
                dy_ref[:, h * hd:(h + 1) * hd] = dq[b]
                dy_ref[:, GDN_WIDTH + h * hd:GDN_WIDTH + (h + 1) * hd] = dk[b]
                dy_ref[:, 2 * GDN_WIDTH + h * hd:2 * GDN_WIDTH + (h + 1) * hd] = dv[b]
            le_t = _chunk_masks(not rev)[0].astype(F32)
            dg_ref[...] = _dot3(le_t, dgc_tile, NN) + rest

    def col(j, rev):
        return pl.BlockSpec((CHUNK, GDN_WIDTH), (lambda n: (n, j)) if rev else (lambda n: (nc - 1 - n, j)))

    def wide(width, rev):
        return pl.BlockSpec((CHUNK, width), (lambda n: (n, 0)) if rev else (lambda n: (nc - 1 - n, 0)))

    def per_chunk(d1, d2, rev):
        return pl.BlockSpec((1, GDN_HEADS, d1, d2), (lambda n: (n, 0, 0, 0)) if rev else (lambda n: (nc - 1 - n, 0, 0, 0)))

    def side(rev):
        return [col(0, rev), col(1, rev), col(2, rev), wide(LANE, rev), wide(GDN_WIDTH, rev), per_chunk(hd, hd, rev),
                per_chunk(CHUNK, CHUNK, rev)]

    assert na == 0 or nc >= 2
    res = pl.pallas_call(
        body, name="delta_bwd", grid=(nc,),
        in_specs=side(False) + side(True) + [ANY] * na,
        out_specs=[wide(QKV_A, False), wide(QKV_A, True), wide(LANE, False), wide(LANE, True)] + [ANY] * na,
        out_shape=[jax.ShapeDtypeStruct((s, QKV_A), F32)] * 2 + [jax.ShapeDtypeStruct((s, LANE), F32)] * 2
        + [jax.ShapeDtypeStruct(g.shape, g.dtype) for g in scatter],
        scratch_shapes=[pltpu.VMEM((NB, hd, hd), F32)] + (_gather_semaphores(na) if na else []),
        compiler_params=_params("arbitrary"),
    )(y, y, y, gb, do, sf_all, tf_all, y, y, y, gb, do, sr_all, tr_all, *scatter)
    return res[:4], res[4:]


def _gdn_post_fwd(o_f, o_r, p_pad, norm_row):
    s = o_f.shape[0]
    tm = min(512, s)
    hd = GDN_HEAD_DIM

    def body(of_ref, or_ref, z_ref, w_ref, out_ref, osum_ref):
        o = of_ref[...] + or_ref[...]
        osum_ref[...] = o
        z = z_ref[...]
        gate = z * _sigmoid(z)
        for h in range(GDN_HEADS):
            sl = slice(h * hd, (h + 1) * hd)
            oh = o[:, sl]
            r = lax.rsqrt(jnp.mean(oh * oh, axis=-1, keepdims=True) + EPS)
            out_ref[:, sl] = (oh * r * w_ref[...] * gate[:, sl]).astype(BF16)

    blk = pl.BlockSpec((tm, GDN_WIDTH), lambda i: (i, 0))
    return pl.pallas_call(
        body, name="gdn_post_fwd", grid=(s // tm,),
        in_specs=[blk, blk, pl.BlockSpec((tm, GDN_WIDTH), lambda i: (i, OFF_Z // GDN_WIDTH)),
                  pl.BlockSpec((1, hd), lambda i: (0, 0))],
        out_specs=[blk, blk],
        out_shape=[jax.ShapeDtypeStruct((s, GDN_WIDTH), BF16), jax.ShapeDtypeStruct((s, GDN_WIDTH), F32)],
        compiler_params=_params("parallel"),
    )(o_f, o_r, p_pad, norm_row)


def _gdn_post_bwd(d_out, o_sum, p_pad, norm_row):
    s = o_sum.shape[0]
    tm = min(512, s)
    hd = GDN_HEAD_DIM

    def body(d_ref, o_ref, z_ref, w_ref, do_ref, dz_ref, dw_ref):
        @pl.when(pl.program_id(0) == 0)
        def _():
            dw_ref[...] = jnp.zeros_like(dw_ref)

        z = z_ref[...]
        sg = _sigmoid(z)
        gate = z * sg
        dgate = sg * (1.0 + z * (1.0 - sg))
        wv = w_ref[...]
        dw = jnp.zeros((1, hd), F32)
        for h in range(GDN_HEADS):
            sl = slice(h * hd, (h + 1) * hd)
            oh = o_ref[:, sl]
            dh = d_ref[:, sl]
            r = lax.rsqrt(jnp.mean(oh * oh, axis=-1, keepdims=True) + EPS)
            ohat = oh * r
            dz_ref[:, sl] = dh * ohat * wv * dgate[:, sl]
            drn = dh * gate[:, sl]
            t = drn * wv
            do_ref[:, sl] = r * (t - ohat * jnp.mean(t * ohat, axis=-1, keepdims=True))
            dw = dw + jnp.sum(drn * ohat, axis=0, keepdims=True)
        dw_ref[...] += dw

    blk = pl.BlockSpec((tm, GDN_WIDTH), lambda i: (i, 0))
    vec = pl.BlockSpec((1, hd), lambda i: (0, 0))
    return pl.pallas_call(
        body, name="gdn_post_bwd", grid=(s // tm,),
        in_specs=[blk, blk, pl.BlockSpec((tm, GDN_WIDTH), lambda i: (i, OFF_Z // GDN_WIDTH)), vec],
        out_specs=[blk, blk, vec],
        out_shape=[jax.ShapeDtypeStruct((s, GDN_WIDTH), F32), jax.ShapeDtypeStruct((s, GDN_WIDTH), F32),
                   jax.ShapeDtypeStruct((1, hd), F32)],
        compiler_params=_params("arbitrary"),
    )(d_out, o_sum, p_pad, norm_row)


def _add2(a, b, name):
    s, w = a.shape
    tm = next(t for t in (1024, 640, 512, 256, 128, 64, 8) if s % t == 0)

    def body(a_ref, b_ref, o_ref):
        o_ref[...] = a_ref[...] + b_ref[...]

    blk = pl.BlockSpec((tm, w), lambda i: (i, 0))
    return pl.pallas_call(body, name=name, grid=(s // tm,), in_specs=[blk, blk], out_specs=blk,
                          out_shape=jax.ShapeDtypeStruct((s, w), F32), compiler_params=_params("parallel"))(a, b)


def _gdn_forward(p_pad, conv_wt, alog_row, dt_row, norm_row, gather=()):
    c_pre, y = _gdn_conv_fwd(p_pad, conv_wt)
    gb = _gdn_gates_fwd(p_pad, alog_row, dt_row)
    (o_f, o_r, s_f, s_r, t_f, t_r), gathered = _delta_fwd2(y, gb, gather)
    out, o_sum = _gdn_post_fwd(o_f, o_r, p_pad, norm_row)
    return out, (c_pre, y, gb, s_f, t_f, s_r, t_r, o_sum), gathered


def _gdn_backward(d_out, p_pad, conv_wt, alog_row, dt_row, norm_row, saved, scatter=()):
    c_pre, y, gb, s_f, t_f, s_r, t_r, o_sum = saved
    do, dz, dnorm = _gdn_post_bwd(d_out, o_sum, p_pad, norm_row)
    (dy_f, dy_r, dgb_f, dgb_r), received = _delta_bwd2(y, gb, do, s_f, s_r, t_f, t_r, scatter)
    dp_qkv, dconv = _gdn_conv_bwd(dy_f, dy_r, c_pre, p_pad, conv_wt)
    dp_ab, gate_sums = _gdn_gates_bwd(dgb_f, dgb_r, p_pad, gb, alog_row, dt_row)
    return dp_qkv, dz, dp_ab, dconv, gate_sums, dnorm, received


ATT_BK = ATT_BQ + 2 * ATT_HALO
SWA_SCALE = SWA_HEAD_DIM ** -0.5


def _t5_bucket(rel):
    nb = REL_BUCKETS // 2
    bucket = (rel > 0).astype(np.int32) * nb
    n = np.abs(rel)
    max_exact = nb // 2
    large = max_exact + (np.log(np.maximum(n, 1) / max_exact)
                         / math.log(REL_MAX_DISTANCE / max_exact) * (nb - max_exact)).astype(np.int32)
    large = np.minimum(large, nb - 1)
    return (bucket + np.where(n < max_exact, n, large)).astype(np.int32)


def _band_tables(dilation, queries_are_rows_of_block):
    blk = np.arange(ATT_BQ)
    band = np.arange(ATT_BK) - ATT_HALO
    if queries_are_rows_of_block:
        rel = band[None, :] - blk[:, None]
        band_idx = np.broadcast_to(np.arange(ATT_BK)[None, :], rel.shape)
    else:
        rel = blk[None, :] - band[:, None]
        band_idx = np.broadcast_to(np.arange(ATT_BK)[:, None], rel.shape)
    base = np.abs(rel) <= ATT_HALO
    not_prev = band_idx >= ATT_HALO
    not_next = band_idx < ATT_HALO + ATT_BQ
    valid = np.stack([base & not_prev, base, base & not_next, base & not_prev & not_next])
    return valid, _t5_bucket(rel * dilation)


def _bias_tiles(rel_bias, dilation, queries_are_rows_of_block):
    valid, bucket = _band_tables(dilation, queries_are_rows_of_block)
    onehot = (jnp.asarray(bucket.reshape(-1, 1)) == jnp.arange(REL_BUCKETS, dtype=jnp.int32)[None, :]).astype(F32)
    rb = jnp.dot(onehot, rel_bias.astype(F32), precision=lax.Precision.HIGHEST)
    rb = rb.T.reshape((SWA_HEADS,) + bucket.shape)
    return jnp.where(valid[:, None], rb[None], NEG_BIG).astype(F32)


def _group_sum(x, bd):
    hi = x.astype(BF16)
    lo = (x - hi.astype(F32)).astype(BF16)
    return jnp.dot(hi, bd, preferred_element_type=F32) + jnp.dot(lo, bd, preferred_element_type=F32)


def _head_block_diag():
    idx = np.arange(SWA_WIDTH) // SWA_HEAD_DIM
    return jnp.asarray(idx[:, None] == idx[None, :], BF16)


def _swa_pre_fwd(p_pad, qw_row, kw_row, bd):
    s = p_pad.shape[0]
    tm = min(512, s)
    inv = 1.0 / SWA_HEAD_DIM

    def body(q_ref, k_ref, v_ref, qw_ref, kw_ref, bd_ref, qo_ref, ko_ref, vo_ref):
        bdv = bd_ref[...]
        q = q_ref[...]
        k = k_ref[...]
        rq = lax.rsqrt(_group_sum(q * q, bdv) * inv + EPS)
        rk = lax.rsqrt(_group_sum(k * k, bdv) * inv + EPS)
        qo_ref[...] = (q * rq * qw_ref[...] * SWA_SCALE).astype(BF16)
        ko_ref[...] = (k * rk * kw_ref[...]).astype(BF16)
        vo_ref[...] = v_ref[...].astype(BF16)

    base = OFF_B // SWA_WIDTH
    blk = pl.BlockSpec((tm, SWA_WIDTH), lambda i: (i, 0))
    vec = pl.BlockSpec((1, SWA_WIDTH), lambda i: (0, 0))
    return pl.pallas_call(
        body, name="swa_pre_fwd", grid=(s // tm,),
        in_specs=[pl.BlockSpec((tm, SWA_WIDTH), lambda i: (i, base)), pl.BlockSpec((tm, SWA_WIDTH), lambda i: (i, base + 1)),
                  pl.BlockSpec((tm, SWA_WIDTH), lambda i: (i, base + 2)), vec, vec,
                  pl.BlockSpec((SWA_WIDTH, SWA_WIDTH), lambda i: (0, 0))],
        out_specs=[blk, blk, blk],
        out_shape=[jax.ShapeDtypeStruct((s, SWA_WIDTH), BF16)] * 3,
        compiler_params=_params("parallel"),
    )(p_pad, p_pad, p_pad, qw_row, kw_row, bd)


def _swa_pre_bwd(dqs, dks, dvs, p_pad, qw_row, kw_row, bd):
    s = p_pad.shape[0]
    tm = min(256, s)
    inv = 1.0 / SWA_HEAD_DIM
    npat = len(dqs)

    def body(*refs):
        dq_refs, dk_refs, dv_refs = refs[:npat], refs[npat:2 * npat], refs[2 * npat:3 * npat]
        q_ref, k_ref, qw_ref, kw_ref, bd_ref, dp_ref, dqw_ref, dkw_ref = refs[3 * npat:]

        @pl.when(pl.program_id(0) == 0)
        def _():
            dqw_ref[...] = jnp.zeros_like(dqw_ref)
            dkw_ref[...] = jnp.zeros_like(dkw_ref)

        bdv = bd_ref[...]

        def norm_bwd(x, g, w, scale):
            r = lax.rsqrt(_group_sum(x * x, bdv) * inv + EPS)
            xhat = x * r
            t = g * w * scale
            dx = r * (t - xhat * (_group_sum(t * xhat, bdv) * inv))
            return dx, jnp.sum(g * scale * xhat, axis=0, keepdims=True)

        def total(rs):
            t = rs[0][...].astype(F32)
            for r in rs[1:]:
                t = t + r[...].astype(F32)
            return t

        dq, dqw = norm_bwd(q_ref[...], total(dq_refs), qw_ref[...], SWA_SCALE)
        dk, dkw = norm_bwd(k_ref[...], total(dk_refs), kw_ref[...], 1.0)
        dp_ref[:, 0:SWA_WIDTH] = dq
        dp_ref[:, SWA_WIDTH:2 * SWA_WIDTH] = dk
        dp_ref[:, 2 * SWA_WIDTH:3 * SWA_WIDTH] = total(dv_refs)
        dqw_ref[...] += dqw
        dkw_ref[...] += dkw

    base = OFF_B // SWA_WIDTH
    blk = pl.BlockSpec((tm, SWA_WIDTH), lambda i: (i, 0))
    vec = pl.BlockSpec((1, SWA_WIDTH), lambda i: (0, 0))
    return pl.pallas_call(
        body, name="swa_pre_bwd", grid=(s // tm,),
        in_specs=[blk] * (3 * npat) + [pl.BlockSpec((tm, SWA_WIDTH), lambda i: (i, base)),
                                      pl.BlockSpec((tm, SWA_WIDTH), lambda i: (i, base + 1)), vec, vec,
                                      pl.BlockSpec((SWA_WIDTH, SWA_WIDTH), lambda i: (0, 0))],
        out_specs=[pl.BlockSpec((tm, 3 * SWA_WIDTH), lambda i: (i, 0)), vec, vec],
        out_shape=[jax.ShapeDtypeStruct((s, 3 * SWA_WIDTH), F32), jax.ShapeDtypeStruct((1, SWA_WIDTH), F32),
                   jax.ShapeDtypeStruct((1, SWA_WIDTH), F32)],
        compiler_params=_params("arbitrary"),
    )(*dqs, *dks, *dvs, p_pad, p_pad, qw_row, kw_row, bd)


def _band_specs(length):
    per = ATT_BQ // ATT_HALO
    last = length // ATT_HALO - 1
    prev = pl.BlockSpec((ATT_HALO, SWA_WIDTH), lambda r, t: (jnp.maximum(t * per - 1, 0), r))
    cur = pl.BlockSpec((ATT_BQ, SWA_WIDTH), lambda r, t: (t, r))
    nxt = pl.BlockSpec((ATT_HALO, SWA_WIDTH), lambda r, t: (jnp.minimum((t + 1) * per, last), r))
    return [prev, cur, nxt]


def _tile_variant(t, nb):
    if nb == 1:
        return 3
    return jnp.where(t == 0, 0, jnp.where(t == nb - 1, 2, 1))


def _band(refs):
    return jnp.concatenate([r[...] for r in refs], axis=0)


def _att_fwd(q, k, v, bias, dilation):
    s = q.shape[0]
    length = s // dilation
    nb = length // ATT_BQ
    view = (length, dilation * SWA_WIDTH)
    hd = SWA_HEAD_DIM

    def body(q_ref, kp, kc, kn, vp, vc, vn, b_ref, o_ref, lse_ref):
        kb, vb = _band((kp, kc, kn)), _band((vp, vc, vn))
        qv = q_ref[...]
        for h in range(SWA_HEADS):
            sl = slice(h * hd, (h + 1) * hd)
            sc = _dot(qv[:, sl], kb[:, sl], NT) + b_ref[0, h]
            m = jnp.max(sc, axis=-1, keepdims=True)
            p = jnp.exp(sc - m)
            den = jnp.sum(p, axis=-1, keepdims=True)
            o_ref[:, sl] = _dot(p, vb[:, sl]) / den
            lse_ref[:, sl] = jnp.broadcast_to(m + jnp.log(den), (ATT_BQ, hd))

    cur = pl.BlockSpec((ATT_BQ, SWA_WIDTH), lambda r, t: (t, r))
    bspec = pl.BlockSpec((1, SWA_HEADS, ATT_BQ, ATT_BK), lambda r, t: (_tile_variant(t, nb), 0, 0, 0))
    o, lse = pl.pallas_call(
        body, name=f"att_fwd_d{dilation}", grid=(dilation, nb),
        in_specs=[cur] + _band_specs(length) * 2 + [bspec],
        out_specs=[cur, cur],
        out_shape=[jax.ShapeDtypeStruct(view, F32)] * 2,
        compiler_params=_params("parallel", "parallel"),
    )(q.reshape(view), *([k.reshape(view)] * 3), *([v.reshape(view)] * 3), bias)
    return o.reshape(s, SWA_WIDTH), lse.reshape(s, SWA_WIDTH)


def _att_dq(q, k, v, dop, lse, cp, bias, dilation):
    s = q.shape[0]
    length = s // dilation
    nb = length // ATT_BQ
    view = (length, dilation * SWA_WIDTH)
    hd = SWA_HEAD_DIM

    def body(q_ref, kp, kc, kn, vp, vc, vn, do_ref, lse_ref, cp_ref, b_ref, dq_ref, db_ref):
        @pl.when((pl.program_id(0) == 0) & (pl.program_id(1) == 0))
        def _():
            db_ref[...] = jnp.zeros_like(db_ref)

        var = _tile_variant(pl.program_id(1), nb)
        kb, vb = _band((kp, kc, kn)), _band((vp, vc, vn))
        qv, dov, lsev, cpv = q_ref[...], do_ref[...], lse_ref[...], cp_ref[...]
        for h in range(SWA_HEADS):
            sl = slice(h * hd, (h + 1) * hd)
            sc = _dot(qv[:, sl], kb[:, sl], NT) + b_ref[0, h]
            p = jnp.exp(sc - lsev[:, h * hd:h * hd + 1])
            dp = _dot(dov[:, sl], vb[:, sl], NT)
            ds = p * (dp + cpv[:, h * hd:h * hd + 1])
            dq_ref[:, sl] = _dot(ds, kb[:, sl])
            db_ref[var, h] += ds

    cur = pl.BlockSpec((ATT_BQ, SWA_WIDTH), lambda r, t: (t, r))
    bspec = pl.BlockSpec((1, SWA_HEADS, ATT_BQ, ATT_BK), lambda r, t: (_tile_variant(t, nb), 0, 0, 0))
    dq, db = pl.pallas_call(
        body, name=f"att_dq_d{dilation}", grid=(dilation, nb),
        in_specs=[cur] + _band_specs(length) * 2 + [cur, cur, cur, bspec],
        out_specs=[cur, pl.BlockSpec((4, SWA_HEADS, ATT_BQ, ATT_BK), lambda r, t: (0, 0, 0, 0))],
        out_shape=[jax.ShapeDtypeStruct(view, F32), jax.ShapeDtypeStruct((4, SWA_HEADS, ATT_BQ, ATT_BK), F32)],
        compiler_params=_params("arbitrary", "arbitrary"),
    )(q.reshape(view), *([k.reshape(view)] * 3), *([v.reshape(view)] * 3), dop.reshape(view), lse.reshape(view),
      cp.reshape(view), bias)
    return dq.reshape(s, SWA_WIDTH), db


def _att_dkv(q, k, v, dop, lse, cp, bias_t, dilation):
    s = q.shape[0]
    length = s // dilation
    nb = length // ATT_BQ
    view = (length, dilation * SWA_WIDTH)
    hd = SWA_HEAD_DIM

    def body(k_ref, v_ref, qp, qc, qn, dp_, dc_, dn_, lp, lc, ln, cp_, cc_, cn_, b_ref, dk_ref, dv_ref):
        qb, dob = _band((qp, qc, qn)), _band((dp_, dc_, dn_))
        lseb, cpb = _band((lp, lc, ln)), _band((cp_, cc_, cn_))
        kv, vv = k_ref[...], v_ref[...]
        for h in range(SWA_HEADS):
            sl = slice(h * hd, (h + 1) * hd)
            sc = _dot(qb[:, sl], kv[:, sl], NT) + b_ref[0, h]
            p = jnp.exp(sc - lseb[:, h * hd:h * hd + 1])
            dv_ref[:, sl] = _dot(p, dob[:, sl], TN)
            dp = _dot(dob[:, sl], vv[:, sl], NT)
            ds = p * (dp + cpb[:, h * hd:h * hd + 1])
            dk_ref[:, sl] = _dot(ds, qb[:, sl], TN)

    cur = pl.BlockSpec((ATT_BQ, SWA_WIDTH), lambda r, t: (t, r))
    bspec = pl.BlockSpec((1, SWA_HEADS, ATT_BK, ATT_BQ), lambda r, t: (_tile_variant(t, nb), 0, 0, 0))
    dk, dv = pl.pallas_call(
        body, name=f"att_dkv_d{dilation}", grid=(dilation, nb),
        in_specs=[cur, cur] + _band_specs(length) * 4 + [bspec],
        out_specs=[cur, cur],
        out_shape=[jax.ShapeDtypeStruct(view, F32)] * 2,
        compiler_params=_params("parallel", "parallel"),
    )(k.reshape(view), v.reshape(view), *([q.reshape(view)] * 3), *([dop.reshape(view)] * 3),
      *([lse.reshape(view)] * 3), *([cp.reshape(view)] * 3), bias_t)
    return dk.reshape(s, SWA_WIDTH), dv.reshape(s, SWA_WIDTH)


N_PAIRS = SWA_HEADS // 2


def _pairs(x):
    return jnp.stack([x[:, LANE * p:LANE * (p + 1)] for p in range(N_PAIRS)])


def _per_head_rows(x):
    first = lax.broadcasted_iota(jnp.int32, x.shape, 2) < SWA_HEAD_DIM
    zero = jnp.zeros_like(x)
    return jnp.concatenate([jnp.where(first, x, zero), jnp.where(first, zero, x)], axis=1)


def _per_head_cols(x):
    return jnp.stack([jnp.concatenate([x[:, LANE * p:LANE * p + 1],
                                       x[:, LANE * p + SWA_HEAD_DIM:LANE * p + SWA_HEAD_DIM + 1]], axis=0)
                      for p in range(N_PAIRS)])


def _merge_heads(x, rows):
    first = lax.broadcasted_iota(jnp.int32, (N_PAIRS, rows, LANE), 2) < SWA_HEAD_DIM
    return jnp.where(first, x[:, :rows], x[:, rows:])


def _store_pairs(ref, x):
    for p in range(N_PAIRS):
        ref[:, LANE * p:LANE * (p + 1)] = x[p].astype(ref.dtype)


def _att_fwd2(q, k, v, bias, dilation):
    s = q.shape[0]
    length = s // dilation
    nb = length // ATT_BQ
    view = (length, dilation * SWA_WIDTH)

    def body(q_ref, kp, kc, kn, vp, vc, vn, b_ref, o_ref, lse_ref):
        kb, vb = _pairs(_band((kp, kc, kn))), _pairs(_band((vp, vc, vn)))
        qm = _per_head_rows(_pairs(q_ref[...]))
        sc = _bdot(qm, kb, BNT) + b_ref[0].reshape(N_PAIRS, 2 * ATT_BQ, ATT_BK)
        m = jnp.max(sc, axis=-1, keepdims=True)
        p = jnp.exp(sc - m)
        den = jnp.sum(p, axis=-1, keepdims=True)
        o = _bdot(p, vb) / den
        _store_pairs(o_ref, _merge_heads(o, ATT_BQ))
        lse = jnp.broadcast_to(m + jnp.log(den), (N_PAIRS, 2 * ATT_BQ, LANE))
        _store_pairs(lse_ref, _merge_heads(lse, ATT_BQ))

    cur = pl.BlockSpec((ATT_BQ, SWA_WIDTH), lambda r, t: (t, r))
    bspec = pl.BlockSpec((1, SWA_HEADS, ATT_BQ, ATT_BK), lambda r, t: (_tile_variant(t, nb), 0, 0, 0))
    o, lse = pl.pallas_call(
        body, name=f"att_fwd_d{dilation}", grid=(dilation, nb),
        in_specs=[cur] + _band_specs(length) * 2 + [bspec],
        out_specs=[cur, cur],
        out_shape=[jax.ShapeDtypeStruct(view, BF16), jax.ShapeDtypeStruct(view, F32)],
        compiler_params=_params("parallel", "parallel"),
    )(q.reshape(view), *([k.reshape(view)] * 3), *([v.reshape(view)] * 3), bias)
    return o.reshape(s, SWA_WIDTH), lse.reshape(s, SWA_WIDTH)


def _att_dq2(q, k, v, dop, lse, cp, bias, dilation):
    s = q.shape[0]
    length = s // dilation
    nb = length // ATT_BQ
    view = (length, dilation * SWA_WIDTH)

    def body(q_ref, kp, kc, kn, vp, vc, vn, do_ref, lse_ref, cp_ref, b_ref, dq_ref, db_ref):
        @pl.when((pl.program_id(0) == 0) & (pl.program_id(1) == 0))
        def _():
            db_ref[...] = jnp.zeros_like(db_ref)

        var = _tile_variant(pl.program_id(1), nb)
        kb, vb = _pairs(_band((kp, kc, kn))), _pairs(_band((vp, vc, vn)))
        qm = _per_head_rows(_pairs(q_ref[...]))
        dom = _per_head_rows(_pairs(do_ref[...]))
        sc = _bdot(qm, kb, BNT) + b_ref[0].reshape(N_PAIRS, 2 * ATT_BQ, ATT_BK)
        p = jnp.exp(sc - _per_head_cols(lse_ref[...]))
        ds = p * (_bdot(dom, vb, BNT) + _per_head_cols(cp_ref[...]))
        _store_pairs(dq_ref, _merge_heads(_bdot(ds, kb), ATT_BQ))
        db_ref[var] += ds.reshape(SWA_HEADS, ATT_BQ, ATT_BK)

    cur = pl.BlockSpec((ATT_BQ, SWA_WIDTH), lambda r, t: (t, r))
    bspec = pl.BlockSpec((1, SWA_HEADS, ATT_BQ, ATT_BK), lambda r, t: (_tile_variant(t, nb), 0, 0, 0))
    dq, db = pl.pallas_call(
        body, name=f"att_dq_d{dilation}", grid=(dilation, nb),
        in_specs=[cur] + _band_specs(length) * 2 + [cur, cur, cur, bspec],
        out_specs=[cur, pl.BlockSpec((4, SWA_HEADS, ATT_BQ, ATT_BK), lambda r, t: (0, 0, 0, 0))],
        out_shape=[jax.ShapeDtypeStruct(view, BF16), jax.ShapeDtypeStruct((4, SWA_HEADS, ATT_BQ, ATT_BK), F32)],
        compiler_params=_params("arbitrary", "arbitrary"),
    )(q.reshape(view), *([k.reshape(view)] * 3), *([v.reshape(view)] * 3), dop.reshape(view), lse.reshape(view),
      cp.reshape(view), bias)
    return dq.reshape(s, SWA_WIDTH), db


def _att_dkv2(q, k, v, dop, lse, cp, bias_t, dilation):
    s = q.shape[0]
    length = s // dilation
    nb = length // ATT_BQ
    view = (length, dilation * SWA_WIDTH)

    def body(k_ref, v_ref, qp, qc, qn, dp_, dc_, dn_, lp, lc, ln, cp_, cc_, cn_, b_ref, dk_ref, dv_ref):
        qm = _per_head_rows(_pairs(_band((qp, qc, qn))))
        dom = _per_head_rows(_pairs(_band((dp_, dc_, dn_))))
        lsev = _per_head_cols(_band((lp, lc, ln)))
        cpv = _per_head_cols(_band((cp_, cc_, cn_)))
        kv, vv = _pairs(k_ref[...]), _pairs(v_ref[...])
        sc = _bdot(qm, kv, BNT) + b_ref[0].reshape(N_PAIRS, 2 * ATT_BK, ATT_BQ)
        p = jnp.exp(sc - lsev)
        _store_pairs(dv_ref, _bdot(p, dom, BTN))
        ds = p * (_bdot(dom, vv, BNT) + cpv)
        _store_pairs(dk_ref, _bdot(ds, qm, BTN))

    cur = pl.BlockSpec((ATT_BQ, SWA_WIDTH), lambda r, t: (t, r))
    bspec = pl.BlockSpec((1, SWA_HEADS, ATT_BK, ATT_BQ), lambda r, t: (_tile_variant(t, nb), 0, 0, 0))
    dk, dv = pl.pallas_call(
        body, name=f"att_dkv_d{dilation}", grid=(dilation, nb),
        in_specs=[cur, cur] + _band_specs(length) * 4 + [bspec],
        out_specs=[cur, cur],
        out_shape=[jax.ShapeDtypeStruct(view, BF16)] * 2,
        compiler_params=_params("parallel", "parallel"),
    )(k.reshape(view), v.reshape(view), *([q.reshape(view)] * 3), *([dop.reshape(view)] * 3),
      *([lse.reshape(view)] * 3), *([cp.reshape(view)] * 3), bias_t)
    return dk.reshape(s, SWA_WIDTH), dv.reshape(s, SWA_WIDTH)


def _pattern_weights(lses):
    m = lses[0]
    for l in lses[1:]:
        m = jnp.maximum(m, l)
    es = [jnp.exp(l - m) for l in lses]
    den = es[0]
    for e in es[1:]:
        den = den + e
    return [e / den for e in es]


def _combine_fwd(outs, lses):
    s = outs[0].shape[0]
    tm = min(512, s)
    npat = len(outs)

    def body(*refs):
        ws = _pattern_weights([r[...] for r in refs[npat:2 * npat]])
        o = ws[0] * refs[0][...]
        for p in range(1, npat):
            o = o + ws[p] * refs[p][...]
        refs[2 * npat][...] = o.astype(BF16)

    blk = pl.BlockSpec((tm, SWA_WIDTH), lambda i: (i, 0))
    return pl.pallas_call(
        body, name="swa_combine_fwd", grid=(s // tm,), in_specs=[blk] * (2 * npat), out_specs=blk,
        out_shape=jax.ShapeDtypeStruct((s, SWA_WIDTH), BF16), compiler_params=_params("parallel"),
    )(*outs, *lses)


def _combine_bwd(d_out, outs, lses, bd):
    s = d_out.shape[0]
    tm = min(512, s)
    npat = len(outs)

    def body(*refs):
        d_ref, bd_ref = refs[0], refs[1 + 2 * npat]
        o_refs, l_refs = refs[1:1 + npat], refs[1 + npat:1 + 2 * npat]
        out_refs = refs[2 + 2 * npat:]
        ws = _pattern_weights([r[...] for r in l_refs])
        dov = d_ref[...]
        o = ws[0] * o_refs[0][...]
        for p in range(1, npat):
            o = o + ws[p] * o_refs[p][...]
        rd = _group_sum(dov * o, bd_ref[...])
        for p in range(npat):
            out_refs[p][...] = (ws[p] * dov).astype(BF16)
            out_refs[npat + p][...] = -ws[p] * rd

    blk = pl.BlockSpec((tm, SWA_WIDTH), lambda i: (i, 0))
    res = pl.pallas_call(
        body, name="swa_combine_bwd", grid=(s // tm,),
        in_specs=[blk] * (1 + 2 * npat) + [pl.BlockSpec((SWA_WIDTH, SWA_WIDTH), lambda i: (0, 0))],
        out_specs=[blk] * (2 * npat),
        out_shape=[jax.ShapeDtypeStruct((s, SWA_WIDTH), BF16)] * npat + [jax.ShapeDtypeStruct((s, SWA_WIDTH), F32)] * npat,
        compiler_params=_params("parallel"),
    )(d_out, *outs, *lses, bd)
    return res[:npat], res[npat:]


def _rel_bias_grad(dbs, buckets):
    npat = len(dbs)

    def body(*refs):
        db_refs, bk_refs, o_ref = refs[:npat], refs[npat:2 * npat], refs[2 * npat]
        row = lax.broadcasted_iota(jnp.int32, (REL_BUCKETS, LANE), 0)
        lane = lax.broadcasted_iota(jnp.int32, (REL_BUCKETS, LANE), 1)
        tiles = [[db_refs[p][0, h] + db_refs[p][1, h] + db_refs[p][2, h] + db_refs[p][3, h] for h in range(SWA_HEADS)]
                 for p in range(npat)]
        bks = [r[...] for r in bk_refs]

        def one_bucket(b, acc):
            for h in range(SWA_HEADS):
                tot = jnp.zeros((1, 1), F32)
                for p in range(npat):
                    sel = jnp.where(bks[p] == b, tiles[p][h], 0.0)
                    tot = tot + jnp.sum(jnp.sum(sel, axis=1, keepdims=True), axis=0, keepdims=True)
                acc = acc + jnp.where((row == b) & (lane == h), tot, 0.0)
            return acc

        o_ref[...] = lax.fori_loop(0, REL_BUCKETS, one_bucket, jnp.zeros((REL_BUCKETS, LANE), F32))

    full4 = pl.BlockSpec((4, SWA_HEADS, ATT_BQ, ATT_BK), lambda: (0, 0, 0, 0))
    full2 = pl.BlockSpec((ATT_BQ, ATT_BK), lambda: (0, 0))
    return pl.pallas_call(
        body, name="rel_bias_grad", in_specs=[full4] * npat + [full2] * npat,
        out_specs=pl.BlockSpec((REL_BUCKETS, LANE), lambda: (0, 0)),
        out_shape=jax.ShapeDtypeStruct((REL_BUCKETS, LANE), F32),
        compiler_params=pltpu.CompilerParams(vmem_limit_bytes=V7X_VMEM_LIMIT_BYTES),
    )(*dbs, *buckets)


def _swa_forward(p_pad, qw_row, kw_row, rel_bias, bd):
    q, k, v = _swa_pre_fwd(p_pad, qw_row, kw_row, bd)
    outs, lses = [], []
    for _, dil in DILATION_PATTERNS:
        o, lse = _att_fwd2(q, k, v, _bias_tiles(rel_bias, dil, True), dil)
        outs.append(o)
        lses.append(lse)
    return _combine_fwd(outs, lses), (q, k, v, outs, lses)


def _swa_backward(d_out, p_pad, qw_row, kw_row, rel_bias, bd, saved):
    q, k, v, outs, lses = saved
    dops, cps = _combine_bwd(d_out, outs, lses, bd)
    dqs, dks, dvs, dbs, buckets = [], [], [], [], []
    for p, (_, dil) in enumerate(DILATION_PATTERNS):
        dq, db = _att_dq2(q, k, v, dops[p], lses[p], cps[p], _bias_tiles(rel_bias, dil, True), dil)
        dk, dv = _att_dkv2(q, k, v, dops[p], lses[p], cps[p], _bias_tiles(rel_bias, dil, False), dil)
        dqs.append(dq)
        dks.append(dk)
        dvs.append(dv)
        dbs.append(db)
        buckets.append(jnp.asarray(_band_tables(dil, True)[1]))
    dp, dqw, dkw = _swa_pre_bwd(dqs, dks, dvs, p_pad, qw_row, kw_row, bd)
    return dp, dqw, dkw, _rel_bias_grad(dbs, buckets)


def _lane_row(v):
    flat = v.reshape(-1).astype(F32)
    return jnp.zeros((1, LANE), F32).at[0, :flat.shape[0]].set(flat)


W_IN_SHARD = N_IN // N_DEV
W_IN_RUNS = ((0, QKV_A, 0), (QKV_A, OFF_B, QKV_A), (OFF_B, OFF_B + 16, OFF_AB), (OFF_B + 16, N_IN, OFF_B))
W_IN_SEGMENTS = ((0, QKV_A), (OFF_Z, GDN_WIDTH), (OFF_B, 3 * SWA_WIDTH), (OFF_AB, LANE))


def _w_in_pieces(shard):
    lo, hi = shard * W_IN_SHARD, (shard + 1) * W_IN_SHARD
    out = []
    for first, last, dst in W_IN_RUNS:
        a, b = max(lo, first), min(hi, last)
        if a < b:
            out.append((a - lo, b - a, dst + a - first))
    return out


def _cols_from_slabs(w3, name):
    nd, r, wd = w3.shape
    half = nd // 2

    def body(w_ref, o_ref):
        for sh in range(half):
            o_ref[:, wd * sh:wd * (sh + 1)] = w_ref[sh]

    return pl.pallas_call(
        body, name=name, grid=(2,), in_specs=[pl.BlockSpec((half, r, wd), lambda j: (j, 0, 0))],
        out_specs=pl.BlockSpec((r, half * wd), lambda j: (0, j)),
        out_shape=jax.ShapeDtypeStruct((r, nd * wd), w3.dtype), compiler_params=_params("parallel"),
    )(w3)


def _w_in_from_slabs(w3):
    nd, r, _ = w3.shape

    def body(w_ref, o_ref):
        o_ref[:, OFF_AB:N_PAD] = jnp.zeros((r, N_PAD - OFF_AB), w3.dtype)
        for sh in range(nd):
            for src, length, dst in _w_in_pieces(sh):
                o_ref[:, dst:dst + length] = w_ref[sh, :, src:src + length]

    return pl.pallas_call(
        body, name="w_in_from_slabs", out_shape=jax.ShapeDtypeStruct((r, N_PAD), w3.dtype),
        compiler_params=pltpu.CompilerParams(vmem_limit_bytes=V7X_VMEM_LIMIT_BYTES),
    )(w3)


def _w_in_grad_slabs(parts, dtype):
    r = parts[0].shape[0]

    def body(*refs):
        o_ref = refs[len(parts)]
        for sh in range(N_DEV):
            for src, length, dst in _w_in_pieces(sh):
                seg = next(i for i, (off, width) in enumerate(W_IN_SEGMENTS) if off <= dst < off + width)
                at = dst - W_IN_SEGMENTS[seg][0]
                o_ref[sh, :, src:src + length] = refs[seg][:, at:at + length].astype(dtype)

    return pl.pallas_call(
        body, name="w_in_grad_slabs", out_shape=jax.ShapeDtypeStruct((N_DEV, r, W_IN_SHARD), dtype),
        compiler_params=pltpu.CompilerParams(vmem_limit_bytes=V7X_VMEM_LIMIT_BYTES),
    )(*parts)


LATE = ("w_out", "ffn2_w_gate", "ffn2_w_up", "ffn2_w_down")


def _late_weights(slabs):
    out = {}
    for n, g in zip(LATE, slabs):
        out[n] = _cols_from_slabs(g, f"{n}_cols") if n in COL_SHARDED else g.reshape(N_DEV * g.shape[1], g.shape[2])
    return out


def _local_step(x, tgt, wts, small, late_shards=None):
    bd = _head_block_diag()
    conv_wt = jnp.zeros((8, QKV_A), F32).at[:CONV_WIDTH].set(small["conv_w"].T)
    alog_row, dt_row = _lane_row(small["a_log"]), _lane_row(small["dt_bias"])
    gnorm_row = small["gdn_norm_w"].reshape(1, GDN_HEAD_DIM)
    qw_row = jnp.tile(small["q_norm_w"].reshape(-1), SWA_HEADS).reshape(1, SWA_WIDTH)
    kw_row = jnp.tile(small["k_norm_w"].reshape(-1), SWA_HEADS).reshape(1, SWA_WIDTH)
    rel_bias = small["rel_bias"]
    exchange = late_shards is not None
    dw_dtype = BF16 if exchange else F32

    x1, sv1, wd1, got = _ffn_forward(
        x, small["ffn1_norm"], wts["ffn1_w_gate"], wts["ffn1_w_up"], wts.get("ffn1_w_down"), "ffn1",
        gather=[late_shards["ffn1_w_down"], late_shards["w_in"]] if exchange else ())
    win_pad = _w_in_from_slabs(got[0]) if exchange else wts["w_in_pad"]
    n2, r2 = _rms_fwd(x1, small["mix_norm"], "mix_norm")
    p_pad = _matmul([(n2, win_pad)], tm=256, tn=N_PAD, tk=D_MODEL, name="w_in")
    o_a, sva, gathered = _gdn_forward(p_pad, conv_wt, alog_row, dt_row, gnorm_row,
                                      gather=[late_shards[n] for n in LATE] if exchange else ())
    if exchange:
        wts = {**wts, **_late_weights(gathered)}
    wo_a, wo_b = wts["w_out"][:GDN_WIDTH], wts["w_out"][GDN_WIDTH:]
    o_b, svb = _swa_forward(p_pad, qw_row, kw_row, rel_bias, bd)
    x2 = _matmul([(o_a, wo_a), (o_b, wo_b)], tm=512, tn=D_MODEL, tk=GDN_WIDTH, name="w_out", res=x1)
    x3, sv2, _, _ = _ffn_forward(x2, small["ffn2_norm"], wts["ffn2_w_gate"], wts["ffn2_w_up"], wts["ffn2_w_down"], "ffn2")
    loss_row, dx3, d_final = _final_loss(x3, small["final_norm"], tgt)

    dx2, d_ffn2_norm, dwg2, dwu2, dwd2, _ = _ffn_backward(
        dx3, x2, small["ffn2_norm"], wts["ffn2_w_gate"], wts["ffn2_w_up"], wts["ffn2_w_down"], sv2, "ffn2", dw_dtype)
    d_oa = _matmul([(dx2, wo_a)], tb=True, tm=512, tn=GDN_WIDTH, tk=D_MODEL, name="w_out_da")
    d_ob = _matmul([(dx2, wo_b)], tb=True, tm=512, tn=SWA_WIDTH, tk=D_MODEL, name="w_out_db")
    dwo_a = _matmul([(o_a, dx2)], ta=True, tm=GDN_WIDTH, tn=D_MODEL, tk=2048, name="w_out_dwa", out_dtype=dw_dtype)
    dwo_b = _matmul([(o_b, dx2)], ta=True, tm=SWA_WIDTH, tn=D_MODEL, tk=2048, name="w_out_dwb", out_dtype=dw_dtype)

    late_grads = [_row_slabs(jnp.concatenate([dwo_a, dwo_b], axis=0)), dwg2, dwu2, dwd2]
    dp_qkv, dz, dp_ab, dconv, gate_sums, d_gnorm, received = _gdn_backward(
        d_oa, p_pad, conv_wt, alog_row, dt_row, gnorm_row, sva, scatter=late_grads if exchange else ())
    if exchange:
        late_grads = received
    dp_b, dqw, dkw, d_rel = _swa_backward(d_ob, p_pad, qw_row, kw_row, rel_bias, bd, svb)
    segs = [(dp_qkv, 0, QKV_A), (dz, OFF_Z, GDN_WIDTH), (dp_b, OFF_B, 3 * SWA_WIDTH), (dp_ab, OFF_AB, LANE)]
    dn2 = None
    dwin_parts = []
    for i, (dseg, off, width) in enumerate(segs):
        dwin_parts.append(_matmul([(n2, dseg)], ta=True, tm=512, tn=width, tk=2048, name=f"w_in_dw{i}"))
        dn2 = _matmul([(dseg, win_pad[:, off:off + width])], tb=True, tm=512, tn=D_MODEL, tk=width,
                      name=f"w_in_dn{i}", res=dn2)
    dx1, d_mix_norm = _rms_bwd(dn2, x1, r2, small["mix_norm"], dx2, "mix_dnorm")
    d_w_in = _w_in_grad_slabs(dwin_parts, dw_dtype)
    dx, d_ffn1_norm, dwg1, dwu1, dwd1, got = _ffn_backward(
        dx1, x, small["ffn1_norm"], wts["ffn1_w_gate"], wts["ffn1_w_up"], wd1, sv1, "ffn1", dw_dtype,
        scatter=[d_w_in] if exchange else None)
    if exchange:
        d_w_in = got[0]

    grads = {
        "ffn1_norm": d_ffn1_norm, "ffn1_w_gate": dwg1, "ffn1_w_up": dwu1, "ffn1_w_down": dwd1,
        "mix_norm": d_mix_norm, "w_in": d_w_in, "conv_w": dconv[:CONV_WIDTH].T,
        "a_log": gate_sums[0, :8].reshape(2, GDN_HEADS), "dt_bias": gate_sums[1, :8].reshape(2, GDN_HEADS),
        "gdn_norm_w": d_gnorm, "q_norm_w": dqw.reshape(SWA_HEADS, SWA_HEAD_DIM).sum(0, keepdims=True),
        "k_norm_w": dkw.reshape(SWA_HEADS, SWA_HEAD_DIM).sum(0, keepdims=True), "rel_bias": d_rel[:, :SWA_HEADS],
        "ffn2_norm": d_ffn2_norm, "final_norm": d_final, **dict(zip(LATE, late_grads)),
    }
    return loss_row, dx, grads


MESH_IDS = pl.DeviceIdType.MESH
ANY = pl.BlockSpec(memory_space=pl.ANY)


def _all_gather(v, name):
    m, n = v.shape

    def body(x_ref, out_ref, send_sems, recv_sems, local_sem):
        x, y, c = lax.axis_index("x"), lax.axis_index("y"), lax.axis_index("c")
        me, sibling = (x, y, c), (x, y, 1 - c)
        chips = [(1 - x, y), (x, 1 - y), (1 - x, 1 - y)]

        def rows(px, py, pc):
            return out_ref.at[pl.ds((4 * px + 2 * py + pc) * m, m), :]

        def copy(k, block, to, src=None):
            return pltpu.make_async_remote_copy(
                src_ref=rows(*block) if src is None else src, dst_ref=rows(*block),
                send_sem=send_sems.at[k], recv_sem=recv_sems.at[k], device_id=to, device_id_type=MESH_IDS)

        mine = pltpu.make_async_copy(x_ref, rows(*me), local_sem)
        mine.start()
        first = [copy(0, me, sibling, src=x_ref)]
        first += [copy(1 + j, me, (*chip, c), src=x_ref) for j, chip in enumerate(chips)]
        for cp in first:
            cp.start()
        passed = [copy(4 + j, (*chip, c), sibling) for j, chip in enumerate(chips)]
        for j, chip in enumerate(chips):
            copy(1 + j, (*chip, c), me).wait_recv()
            passed[j].start()
        copy(0, sibling, me).wait_recv()
        for j, chip in enumerate(chips):
            copy(4 + j, (*chip, 1 - c), me).wait_recv()
        for cp in first + passed:
            cp.wait_send()
        mine.wait()

    return pl.pallas_call(
        body, name=name, in_specs=[ANY], out_specs=ANY,
        out_shape=jax.ShapeDtypeStruct((N_DEV * m, n), v.dtype),
        scratch_shapes=[pltpu.SemaphoreType.DMA((7,)), pltpu.SemaphoreType.DMA((7,)), pltpu.SemaphoreType.DMA],
        compiler_params=pltpu.CompilerParams(vmem_limit_bytes=V7X_VMEM_LIMIT_BYTES),
    )(v)


def _sibling_swap(v, name):
    def body(v_ref, out_ref, send_sem, recv_sem):
        x, y, c = lax.axis_index("x"), lax.axis_index("y"), lax.axis_index("c")
        cp = pltpu.make_async_remote_copy(src_ref=v_ref, dst_ref=out_ref, send_sem=send_sem, recv_sem=recv_sem,
                                          device_id=(x, y, 1 - c), device_id_type=MESH_IDS)
        cp.start()
        cp.wait()

    return pl.pallas_call(
        body, name=name, in_specs=[ANY], out_specs=ANY, out_shape=jax.ShapeDtypeStruct(v.shape, v.dtype),
        scratch_shapes=[pltpu.SemaphoreType.DMA, pltpu.SemaphoreType.DMA],
        compiler_params=pltpu.CompilerParams(vmem_limit_bytes=V7X_VMEM_LIMIT_BYTES),
    )(v)


def _chip_exchange(t, name):
    def body(t_ref, out_ref, send_sems, recv_sems, local_sem):
        x, y, c = lax.axis_index("x"), lax.axis_index("y"), lax.axis_index("c")
        mine = 2 * x + y
        chips = [(1 - x, y), (x, 1 - y), (1 - x, 1 - y)]
        own = pltpu.make_async_copy(t_ref.at[mine], out_ref.at[mine], local_sem)
        own.start()
        copies = [pltpu.make_async_remote_copy(
            src_ref=t_ref.at[2 * px + py], dst_ref=out_ref.at[mine], send_sem=send_sems.at[j], recv_sem=recv_sems.at[j],
            device_id=(px, py, c), device_id_type=MESH_IDS) for j, (px, py) in enumerate(chips)]
        for cp in copies:
            cp.start()
        for j, (px, py) in enumerate(chips):
            pltpu.make_async_remote_copy(
                src_ref=t_ref.at[mine], dst_ref=out_ref.at[2 * px + py], send_sem=send_sems.at[j],
                recv_sem=recv_sems.at[j], device_id=(px, py, c), device_id_type=MESH_IDS).wait_recv()
        for cp in copies:
            cp.wait_send()
        own.wait()

    return pl.pallas_call(
        body, name=name, in_specs=[ANY], out_specs=ANY, out_shape=jax.ShapeDtypeStruct(t.shape, t.dtype),
        scratch_shapes=[pltpu.SemaphoreType.DMA((3,)), pltpu.SemaphoreType.DMA((3,)), pltpu.SemaphoreType.DMA],
        compiler_params=pltpu.CompilerParams(vmem_limit_bytes=V7X_VMEM_LIMIT_BYTES),
    )(t)


def _adamw(parts, w, m, v, name):
    nparts, r, n = parts.shape
    tr = r
    for cand in (256, 176, 128, 104, 64, 8):
        if r % cand == 0:
            tr = cand
            break
    bc1 = 1.0 - ADAM_B1 ** ADAM_STEP
    bc2 = 1.0 - ADAM_B2 ** ADAM_STEP

    def body(p_ref, w_ref, m_ref, v_ref, g_ref, d_ref, nm_ref, nv_ref):
        g = p_ref[0].astype(F32)
        for k in range(1, nparts):
            g = g + p_ref[k].astype(F32)
        mn = ADAM_B1 * m_ref[...] + (1.0 - ADAM_B1) * g
        vn = ADAM_B2 * v_ref[...] + (1.0 - ADAM_B2) * (g * g)
        m_hat = mn / bc1
        v_hat = vn / bc2
        g_ref[...] = g
        nm_ref[...] = mn
        nv_ref[...] = vn
        d_ref[...] = -ADAM_LR * (m_hat / (jnp.sqrt(v_hat) + ADAM_EPS) + ADAM_WD * w_ref[...])

    blk = pl.BlockSpec((tr, n), lambda i: (i, 0))
    return pl.pallas_call(
        body, name=name, grid=(r // tr,),
        in_specs=[pl.BlockSpec((nparts, tr, n), lambda i: (0, i, 0)), blk, blk, blk],
        out_specs=[blk] * 4, out_shape=[jax.ShapeDtypeStruct((r, n), F32)] * 4,
        compiler_params=_params("parallel"),
    )(parts, w, m, v)


def _mesh_place():
    x, y, c = lax.axis_index("x"), lax.axis_index("y"), lax.axis_index("c")
    return x, y, c, [(1 - x, y), (x, 1 - y), (1 - x, 1 - y)]


def _gather_phases(x_refs, out_refs, send_sems, recv_sems, local_sems):
    na = len(x_refs)

    def place():
        x, y, c, chips = _mesh_place()
        return (x, y, c), (x, y, 1 - c), chips, c

    def slab(i, px, py, pc):
        return out_refs[i].at[4 * px + 2 * py + pc]

    def copy(i, k, block, to, src=None):
        return pltpu.make_async_remote_copy(
            src_ref=slab(i, *block) if src is None else src, dst_ref=slab(i, *block),
            send_sem=send_sems.at[i, k], recv_sem=recv_sems.at[i, k], device_id=to, device_id_type=MESH_IDS)

    def own(i, me):
        return pltpu.make_async_copy(x_refs[i], slab(i, *me), local_sems.at[i])

    def sends(i, me, sibling, chips, c):
        return [copy(i, 0, me, sibling, src=x_refs[i])] + [copy(i, 1 + j, me, (*chip, c), src=x_refs[i])
                                                          for j, chip in enumerate(chips)]

    def start():
        me, sibling, chips, c = place()
        for i in range(na):
            own(i, me).start()
            for cp in sends(i, me, sibling, chips, c):
                cp.start()

    def forward():
        me, sibling, chips, c = place()
        for j, chip in enumerate(chips):
            for i in range(na):
                copy(i, 1 + j, (*chip, c), me).wait_recv()
                copy(i, 4 + j, (*chip, c), sibling).start()

    def finish():
        me, sibling, chips, c = place()
        for i in range(na):
            copy(i, 0, sibling, me).wait_recv()
        for j, chip in enumerate(chips):
            for i in range(na):
                copy(i, 4 + j, (*chip, 1 - c), me).wait_recv()
        for i in range(na):
            for cp in sends(i, me, sibling, chips, c):
                cp.wait_send()
            for j, chip in enumerate(chips):
                copy(i, 4 + j, (*chip, c), sibling).wait_send()
            own(i, me).wait()

    return start, forward, finish


def _gather_semaphores(na):
    return [pltpu.SemaphoreType.DMA((na, 7)), pltpu.SemaphoreType.DMA((na, 7)), pltpu.SemaphoreType.DMA((na,))]


def _scatter_phases(g_refs, out_refs, send_sems, recv_sems, local_sems):
    na = len(g_refs)

    def place(m):
        x, y, c = lax.axis_index("x"), lax.axis_index("y"), lax.axis_index("c")
        px = 1 - x if m & 4 else x
        py = 1 - y if m & 2 else y
        pc = 1 - c if m & 1 else c
        return 4 * x + 2 * y + c, (px, py, pc), 4 * px + 2 * py + pc

    def own(i):
        me, _, _ = place(0)
        return pltpu.make_async_copy(g_refs[i].at[me], out_refs[i].at[me], local_sems.at[i])

    def start():
        for i in range(na):
            own(i).start()
            for m in range(1, N_DEV):
                me, peer, peer_idx = place(m)
                pltpu.make_async_remote_copy(
                    src_ref=g_refs[i].at[peer_idx], dst_ref=out_refs[i].at[me], send_sem=send_sems.at[i, m - 1],
                    recv_sem=recv_sems.at[i, m - 1], device_id=peer, device_id_type=MESH_IDS).start()

    def finish():
        for i in range(na):
            for m in range(1, N_DEV):
                me, peer, peer_idx = place(m)
                cp = pltpu.make_async_remote_copy(
                    src_ref=g_refs[i].at[peer_idx], dst_ref=out_refs[i].at[peer_idx], send_sem=send_sems.at[i, m - 1],
                    recv_sem=recv_sems.at[i, m - 1], device_id=peer, device_id_type=MESH_IDS)
                cp.wait_recv()
                cp.wait_send()
            own(i).wait()

    return start, finish


def _all_gather_many(vs, name):
    na = len(vs)

    def body(*refs):
        x_refs, out_refs = refs[:na], refs[na:2 * na]
        for step in _gather_phases(x_refs, out_refs, *refs[2 * na:]):
            step()

    return pl.pallas_call(
        body, name=name, in_specs=[ANY] * na, out_specs=[ANY] * na,
        out_shape=[jax.ShapeDtypeStruct((N_DEV,) + v.shape, v.dtype) for v in vs],
        scratch_shapes=_gather_semaphores(na),
        compiler_params=pltpu.CompilerParams(vmem_limit_bytes=V7X_VMEM_LIMIT_BYTES),
    )(*vs)


def _sibling_swap_many(gs, name):
    na = len(gs)

    def body(*refs):
        g_refs, out_refs = refs[:na], refs[na:2 * na]
        send_sems, recv_sems = refs[2 * na:]
        x, y, c, _ = _mesh_place()
        copies = [pltpu.make_async_remote_copy(
            src_ref=g_refs[i].at[2 * k + 1 - c], dst_ref=out_refs[i].at[k], send_sem=send_sems.at[i, k],
            recv_sem=recv_sems.at[i, k], device_id=(x, y, 1 - c), device_id_type=MESH_IDS)
            for i in range(na) for k in range(4)]
        for cp in copies:
            cp.start()
        for cp in copies:
            cp.wait()

    return pl.pallas_call(
        body, name=name, in_specs=[ANY] * na, out_specs=[ANY] * na,
        out_shape=[jax.ShapeDtypeStruct((4,) + g.shape[1:], g.dtype) for g in gs],
        scratch_shapes=[pltpu.SemaphoreType.DMA((na, 4)), pltpu.SemaphoreType.DMA((na, 4))],
        compiler_params=pltpu.CompilerParams(vmem_limit_bytes=V7X_VMEM_LIMIT_BYTES),
    )(*gs)


def _chip_sum(g, got, core, name):
    _, r, n = g.shape

    def body(c_ref, g_ref, got_ref, o_ref):
        o_ref[...] = (g_ref[...] + got_ref[...]).astype(BF16)

    return pl.pallas_call(
        body, name=name,
        grid_spec=pltpu.PrefetchScalarGridSpec(
            num_scalar_prefetch=1, grid=(4,),
            in_specs=[pl.BlockSpec((None, r, n), lambda k, c_ref: (2 * k + c_ref[0], 0, 0)),
                      pl.BlockSpec((None, r, n), lambda k, c_ref: (k, 0, 0))],
            out_specs=pl.BlockSpec((None, r, n), lambda k, c_ref: (k, 0, 0))),
        out_shape=jax.ShapeDtypeStruct((4, r, n), BF16), compiler_params=_params("parallel"),
    )(core, g, got)


def _chip_exchange_many(ts, name):
    na = len(ts)

    def body(*refs):
        t_refs, out_refs = refs[:na], refs[na:2 * na]
        send_sems, recv_sems, local_sems = refs[2 * na:]
        x, y, c, chips = _mesh_place()
        mine = 2 * x + y
        own = [pltpu.make_async_copy(t_refs[i].at[mine], out_refs[i].at[mine], local_sems.at[i]) for i in range(na)]
        for cp in own:
            cp.start()
        copies = [pltpu.make_async_remote_copy(
            src_ref=t_refs[i].at[2 * px + py], dst_ref=out_refs[i].at[mine], send_sem=send_sems.at[i, j],
            recv_sem=recv_sems.at[i, j], device_id=(px, py, c), device_id_type=MESH_IDS)
            for j, (px, py) in enumerate(chips) for i in range(na)]
        for cp in copies:
            cp.start()
        for j, (px, py) in enumerate(chips):
            for i in range(na):
                pltpu.make_async_remote_copy(
                    src_ref=t_refs[i].at[mine], dst_ref=out_refs[i].at[2 * px + py], send_sem=send_sems.at[i, j],
                    recv_sem=recv_sems.at[i, j], device_id=(px, py, c), device_id_type=MESH_IDS).wait_recv()
        for cp in copies:
            cp.wait_send()
        for cp in own:
            cp.wait()

    return pl.pallas_call(
        body, name=name, in_specs=[ANY] * na, out_specs=[ANY] * na,
        out_shape=[jax.ShapeDtypeStruct(t.shape, t.dtype) for t in ts],
        scratch_shapes=[pltpu.SemaphoreType.DMA((na, 3)), pltpu.SemaphoreType.DMA((na, 3)), pltpu.SemaphoreType.DMA((na,))],
        compiler_params=pltpu.CompilerParams(vmem_limit_bytes=V7X_VMEM_LIMIT_BYTES),
    )(*ts)


BIG = ("ffn1_w_gate", "ffn1_w_up", "ffn1_w_down", "w_in", "w_out", "ffn2_w_gate", "ffn2_w_up", "ffn2_w_down")
COL_SHARDED = ("ffn1_w_gate", "ffn1_w_up", "w_in", "ffn2_w_gate", "ffn2_w_up")
SMALL = ("ffn1_norm", "mix_norm", "a_log", "dt_bias", "gdn_norm_w", "q_norm_w", "k_norm_w", "rel_bias",
         "ffn2_norm", "final_norm")
WEIGHTS = ("ffn1_norm", "ffn1_w_gate", "ffn1_w_up", "ffn1_w_down", "mix_norm", "w_in", "conv_w", "a_log", "dt_bias",
           "gdn_norm_w", "q_norm_w", "k_norm_w", "rel_bias", "w_out", "ffn2_norm", "ffn2_w_gate", "ffn2_w_up",
           "ffn2_w_down", "final_norm")
PACK_WIDTH = 1024
PACK_ROW_MULTIPLE = 32


def _pack(arrays, width, row_multiple):
    flat = jnp.concatenate([a.reshape(-1) for a in arrays])
    rows = -(-flat.shape[0] // width)
    rows = -(-rows // row_multiple) * row_multiple
    return jnp.pad(flat, (0, rows * width - flat.shape[0])).reshape(rows, width)


def _unpack(packed, shapes):
    flat = packed.reshape(-1)
    out, pos = [], 0
    for shp in shapes:
        size = int(np.prod(shp))
        out.append(flat[pos:pos + size].reshape(shp))
        pos += size
    return out


def _blocks_of(name, full):
    if name in COL_SHARDED:
        rows, cols = full.shape
        return full.reshape(rows, N_DEV, cols // N_DEV).transpose(1, 0, 2).reshape(N_DEV, -1)
    return full.reshape(N_DEV, -1)


def _full_of(name, blocks, shard_shape):
    rows, cols = shard_shape
    if name in COL_SHARDED:
        return blocks.reshape(N_DEV, rows, cols).transpose(1, 0, 2).reshape(rows, N_DEV * cols)
    return blocks.reshape(N_DEV * rows, cols)


def kernel(x, ffn1_norm, ffn1_w_gate, ffn1_w_up, ffn1_w_down, mix_norm, w_in, conv_w, a_log, dt_bias, gdn_norm_w, q_norm_w, k_norm_w, rel_bias, w_out, ffn2_norm, ffn2_w_gate, ffn2_w_up, ffn2_w_down, final_norm, loss_target, m_ffn1_norm, m_ffn1_w_gate, m_ffn1_w_up, m_ffn1_w_down, m_mix_norm, m_w_in, m_conv_w, m_a_log, m_dt_bias, m_gdn_norm_w, m_q_norm_w, m_k_norm_w, m_rel_bias, m_w_out, m_ffn2_norm, m_ffn2_w_gate, m_ffn2_w_up, m_ffn2_w_down, m_final_norm, v_ffn1_norm, v_ffn1_w_gate, v_ffn1_w_up, v_ffn1_w_down, v_mix_norm, v_w_in, v_conv_w, v_a_log, v_dt_bias, v_gdn_norm_w, v_q_norm_w, v_k_norm_w, v_rel_bias, v_w_out, v_ffn2_norm, v_ffn2_w_gate, v_ffn2_w_up, v_ffn2_w_down, v_final_norm):
    w = dict(ffn1_norm=ffn1_norm, ffn1_w_gate=ffn1_w_gate, ffn1_w_up=ffn1_w_up, ffn1_w_down=ffn1_w_down, mix_norm=mix_norm, w_in=w_in, conv_w=conv_w, a_log=a_log, dt_bias=dt_bias, gdn_norm_w=gdn_norm_w, q_norm_w=q_norm_w, k_norm_w=k_norm_w, rel_bias=rel_bias, w_out=w_out, ffn2_norm=ffn2_norm, ffn2_w_gate=ffn2_w_gate, ffn2_w_up=ffn2_w_up, ffn2_w_down=ffn2_w_down, final_norm=final_norm)
    mom = dict(ffn1_norm=m_ffn1_norm, ffn1_w_gate=m_ffn1_w_gate, ffn1_w_up=m_ffn1_w_up, ffn1_w_down=m_ffn1_w_down, mix_norm=m_mix_norm, w_in=m_w_in, conv_w=m_conv_w, a_log=m_a_log, dt_bias=m_dt_bias, gdn_norm_w=m_gdn_norm_w, q_norm_w=m_q_norm_w, k_norm_w=m_k_norm_w, rel_bias=m_rel_bias, w_out=m_w_out, ffn2_norm=m_ffn2_norm, ffn2_w_gate=m_ffn2_w_gate, ffn2_w_up=m_ffn2_w_up, ffn2_w_down=m_ffn2_w_down, final_norm=m_final_norm)
    var = dict(ffn1_norm=v_ffn1_norm, ffn1_w_gate=v_ffn1_w_gate, ffn1_w_up=v_ffn1_w_up, ffn1_w_down=v_ffn1_w_down, mix_norm=v_mix_norm, w_in=v_w_in, conv_w=v_conv_w, a_log=v_a_log, dt_bias=v_dt_bias, gdn_norm_w=v_gdn_norm_w, q_norm_w=v_q_norm_w, k_norm_w=v_k_norm_w, rel_bias=v_rel_bias, w_out=v_w_out, ffn2_norm=v_ffn2_norm, ffn2_w_gate=v_ffn2_w_gate, ffn2_w_up=v_ffn2_w_up, ffn2_w_down=v_ffn2_w_down, final_norm=v_final_norm)
    ix, iy, ic = lax.axis_index("x"), lax.axis_index("y"), lax.axis_index("c")
    me = 4 * ix + 2 * iy + ic

    shard = {n: w[n][0] for n in BIG}

    conv_shard_shape = w["conv_w"][0].shape
    conv_elems = conv_shard_shape[0] * conv_shard_shape[1]
    first = ("ffn1_w_gate", "ffn1_w_up")
    gathered = _all_gather_many([shard[n].astype(BF16) for n in first] + [_pack([w["conv_w"][0]], LANE, 8)],
                                "gather_weights")
    wts = {n: _cols_from_slabs(g, f"{n}_cols") for n, g in zip(first, gathered)}

    small = {n: w[n][0] if n not in ("rel_bias",) else w[n] for n in SMALL}
    small = {n: (a.reshape(1, -1) if n.endswith("norm") else a) for n, a in small.items()}
    conv_all = gathered[-1].reshape(N_DEV, -1)
    small["conv_w"] = conv_all[:, :conv_elems].reshape(N_DEV * conv_shard_shape[0], conv_shard_shape[1])
    loss_row, grad_x, grads = _local_step(x[0], loss_target[0], wts, small,
                                          late_shards={n: shard[n].astype(BF16) for n in BIG if n not in first})
    loss = lax.psum(loss_row[0, 0], ("x", "y", "c"))

    big_out = [[], [], [], []]
    for n in BIG:
        for kind, val in enumerate(_adamw(grads[n], shard[n], mom[n][0], var[n][0], f"{n}_adamw")):
            big_out[kind].append(val)

    small_names = SMALL + ("conv_w",)
    small_shapes = [grads[n].shape for n in small_names]
    g_small = _pack([grads[n] for n in small_names], LANE, 8)
    small_rows = g_small.shape[0]
    all_small = _all_gather(g_small, "gather_small_grads").reshape(N_DEV, small_rows, LANE)
    rep_shapes = [grads[n].shape for n in SMALL]
    zero_conv = jnp.zeros(small_shapes[-1], F32)
    ws = _pack([w[n].reshape(grads[n].shape) for n in SMALL] + [zero_conv], LANE, 8)
    ms = _pack([mom[n].reshape(grads[n].shape) for n in SMALL] + [zero_conv], LANE, 8)
    vs = _pack([var[n].reshape(grads[n].shape) for n in SMALL] + [zero_conv], LANE, 8)
    small_out = [_unpack(a, small_shapes) for a in _adamw(all_small, ws, ms, vs, "adamw_small")]
    conv_g = lax.dynamic_slice_in_dim(small_out[0][-1], me * conv_shard_shape[0], conv_shard_shape[0], axis=0)
    conv_out = [_unpack(a, [conv_shard_shape])[0] for a in _adamw(
        _pack([conv_g], LANE, 8)[None], _pack([w["conv_w"][0]], LANE, 8), _pack([mom["conv_w"][0]], LANE, 8),
        _pack([var["conv_w"][0]], LANE, 8), "adamw_conv")]

    def leaf(kind, n):
        if n in BIG:
            val = big_out[kind][BIG.index(n)]
        elif n == "conv_w":
            val = conv_out[kind]
        else:
            val = small_out[kind][SMALL.index(n)]
        return val.reshape(w[n].shape)

    outs = [loss, grad_x[None]]
    for kind in range(4):
        outs += [leaf(kind, n) for n in WEIGHTS]
    return tuple(outs)
```

```python
import functools
import math

import numpy as np
import jax
import jax.numpy as jnp
from jax import lax
from jax.experimental import pallas as pl
from jax.experimental.pallas import tpu as pltpu

F32 = jnp.float32
BF16 = jnp.bfloat16

D_MODEL = 1024
D_FF = 2816
GDN_HEADS = 4
GDN_HEAD_DIM = 128
GDN_WIDTH = 512
CONV_WIDTH = 5
CHUNK = 64
SWA_HEADS = 8
SWA_HEAD_DIM = 64
SWA_WIDTH = 512
DILATION_PATTERNS = ((128, 1), (512, 4), (2048, 16))
REL_BUCKETS = 32
REL_MAX_DISTANCE = 1024
EPS = 1e-6
NEG_BIG = -1e30
N_DEV = 8

ADAM_LR = 0.001
ADAM_B1 = 0.9
ADAM_B2 = 0.999
ADAM_EPS = 1e-08
ADAM_WD = 0.01
ADAM_STEP = 10

QKV_A = 3 * GDN_WIDTH
OFF_Z = QKV_A
OFF_B = OFF_Z + GDN_WIDTH
OFF_AB = OFF_B + 3 * SWA_WIDTH
N_PAD = OFF_AB + 128
N_IN = 3600

V7X_VMEM_LIMIT_BYTES = 56 * 1024 * 1024
LANE = 128
ATT_BQ = 128
ATT_HALO = 64
CONV_ROWS = 256

NN = (((1,), (0,)), ((), ()))
NT = (((1,), (1,)), ((), ()))
TN = (((0,), (0,)), ((), ()))


def _params(*sem):
    return pltpu.CompilerParams(dimension_semantics=sem, vmem_limit_bytes=V7X_VMEM_LIMIT_BYTES)


def _dot(a, b, dn=NN):
    return lax.dot_general(a.astype(BF16), b.astype(BF16), dn, preferred_element_type=F32)


def _dot_hi(a, b, dn=NN):
    return lax.dot_general(a, b, dn, precision=lax.Precision.HIGHEST, preferred_element_type=F32)


def _sigmoid(x):
    return 1.0 / (1.0 + jnp.exp(-x))


class _Exchange:
    def __init__(self, kind, arrays):
        self.kind, self.arrays = kind, list(arrays)

    def out_shape(self):
        lead = (N_DEV,) if self.kind == "gather" else ()
        return [jax.ShapeDtypeStruct(lead + v.shape, v.dtype) for v in self.arrays]

    def hooks(self, in_refs, out_refs, sems, grid):
        step = pl.program_id(0)
        for axis in range(1, len(grid)):
            step = step * grid[axis] + pl.program_id(axis)
        total = math.prod(grid)
        if self.kind == "gather":
            assert total >= 4
            start, forward, finish = _gather_phases(in_refs, out_refs, *sems)
            pl.when(step == total // 2)(forward)
        else:
            assert total >= 2
            start, finish = _scatter_phases(in_refs, out_refs, *sems)
        pl.when(step == 0)(start)
        pl.when(step == total - 1)(finish)


def _pallas(body, *, name, grid, in_specs, out_specs, out_shape, args, semantics, scratch_shapes=(), exchange=None):
    n_in, n_out, n_scr = len(in_specs), len(out_specs), len(scratch_shapes)
    if exchange is None:
        res = pl.pallas_call(
            body, name=name, grid=grid, in_specs=list(in_specs), out_specs=list(out_specs), out_shape=list(out_shape),
            scratch_shapes=list(scratch_shapes), compiler_params=_params(*semantics))(*args)
        return list(res), []
    na = len(exchange.arrays)

    def carrying(*refs):
        ins, sent = refs[:n_in], refs[n_in:n_in + na]
        outs = refs[n_in + na:n_in + na + n_out]
        landed = refs[n_in + na + n_out:n_in + 2 * na + n_out]
        rest = refs[n_in + 2 * na + n_out:]
        exchange.hooks(sent, landed, rest[n_scr:], grid)
        body(*ins, *outs, *rest[:n_scr])

    res = pl.pallas_call(
        carrying, name=name, grid=grid, in_specs=list(in_specs) + [ANY] * na, out_specs=list(out_specs) + [ANY] * na,
        out_shape=list(out_shape) + exchange.out_shape(), scratch_shapes=list(scratch_shapes) + _gather_semaphores(na),
        compiler_params=_params(*(["arbitrary"] * len(grid))))(*args, *exchange.arrays)
    return list(res[:n_out]), list(res[n_out:])


def _matmul(pairs, *, ta=False, tb=False, out_dtype=F32, tm, tn, tk, name, res=None, alpha=None, shard_cols=None,
            exchange=None):
    a0, b0 = pairs[0]
    m = a0.shape[1] if ta else a0.shape[0]
    k = a0.shape[0] if ta else a0.shape[1]
    n = b0.shape[0] if tb else b0.shape[1]
    tm, tn, tk = min(tm, m), min(tn, n), min(tk, k)
    assert m % tm == 0 and n % tn == 0 and k % tk == 0, (name, m, n, k, tm, tn, tk)
    nk = k // tk
    npairs = len(pairs)
    dn = (((0 if ta else 1,), (1 if tb else 0,)), ((), ()))

    def body(*refs):
        ins = refs[:2 * npairs]
        pos = 2 * npairs
        r_ref = None
        if res is not None:
            r_ref = refs[pos]
            pos += 1
        o_ref, acc = refs[pos], refs[pos + 1]
        kk = pl.program_id(2)
        t = None
        for p in range(npairs):
            d = _dot(ins[2 * p][...], ins[2 * p + 1][...], dn)
            t = d if t is None else t + d

        if nk > 1:
            @pl.when(kk == 0)
            def _():
                acc[...] = t

            @pl.when((kk > 0) & (kk < nk - 1))
            def _():
                acc[...] += t

        @pl.when(kk == nk - 1)
        def _():
            r = acc[...] + t if nk > 1 else t
            if alpha is not None:
                r = r * alpha
            if r_ref is not None:
                r = r_ref[...] + r
            if shard_cols is None:
                o_ref[...] = r.astype(out_dtype)
            else:
                for sh in range(tn // shard_cols):
                    o_ref[sh] = r[:, sh * shard_cols:(sh + 1) * shard_cols].astype(out_dtype)

    a_spec = pl.BlockSpec((tk, tm), lambda i, j, kk: (kk, i)) if ta else pl.BlockSpec((tm, tk), lambda i, j, kk: (i, kk))
    b_spec = pl.BlockSpec((tn, tk), lambda i, j, kk: (j, kk)) if tb else pl.BlockSpec((tk, tn), lambda i, j, kk: (kk, j))
    o_spec = pl.BlockSpec((tm, tn), lambda i, j, kk: (i, j))
    in_specs = [a_spec, b_spec] * npairs + ([o_spec] if res is not None else [])
    args = [t for pr in pairs for t in pr] + ([res] if res is not None else [])
    out_spec, out_shape = o_spec, (m, n)
    if shard_cols is not None:
        assert res is None and tn % shard_cols == 0
        out_spec = pl.BlockSpec((tn // shard_cols, tm, shard_cols), lambda i, j, kk: (j, i, 0))
        out_shape = (n // shard_cols, m, shard_cols)
    (out,), exchanged = _pallas(
        body, name=name, grid=(m // tm, n // tn, nk), in_specs=in_specs, out_specs=[out_spec],
        out_shape=[jax.ShapeDtypeStruct(out_shape, out_dtype)],
        scratch_shapes=[pltpu.VMEM((tm, tn) if nk > 1 else (8, LANE), F32)],
        semantics=("parallel", "parallel", "arbitrary"), args=args, exchange=exchange)
    return out if exchange is None else (out, exchanged)


def _rms_fwd(x, w, name):
    s, d = x.shape
    tm = min(512, s)

    def body(x_ref, w_ref, n_ref, r_ref):
        xv = x_ref[...]
        r = lax.rsqrt(jnp.mean(xv * xv, axis=-1, keepdims=True) + EPS)
        n_ref[...] = (xv * r * w_ref[...]).astype(BF16)
        r_ref[...] = r

    return pl.pallas_call(
        body, name=name, grid=(s // tm,),
        in_specs=[pl.BlockSpec((tm, d), lambda i: (i, 0)), pl.BlockSpec((1, d), lambda i: (0, 0))],
        out_specs=[pl.BlockSpec((tm, d), lambda i: (i, 0)), pl.BlockSpec((tm, 1), lambda i: (i, 0))],
        out_shape=[jax.ShapeDtypeStruct((s, d), BF16), jax.ShapeDtypeStruct((s, 1), F32)],
        compiler_params=_params("parallel"),
    )(x, w)


def _rms_bwd(dn, x, r, w, dres, name, exchange=None):
    s, d = x.shape
    tm = min(512, s)

    def body(dn_ref, x_ref, r_ref, w_ref, dres_ref, dx_ref, dw_ref):
        @pl.when(pl.program_id(0) == 0)
        def _():
            dw_ref[...] = jnp.zeros_like(dw_ref)

        rv = r_ref[...]
        xhat = x_ref[...] * rv
        g = dn_ref[...]
        t = g * w_ref[...]
        dx_ref[...] = dres_ref[...] + rv * (t - xhat * jnp.mean(t * xhat, axis=-1, keepdims=True))
        dw_ref[...] += jnp.sum(g * xhat, axis=0, keepdims=True)

    row = pl.BlockSpec((tm, d), lambda i: (i, 0))
    vec = pl.BlockSpec((1, d), lambda i: (0, 0))
    (dx, dw), exchanged = _pallas(
        body, name=name, grid=(s // tm,),
        in_specs=[row, row, pl.BlockSpec((tm, 1), lambda i: (i, 0)), vec, row],
        out_specs=[row, vec],
        out_shape=[jax.ShapeDtypeStruct((s, d), F32), jax.ShapeDtypeStruct((1, d), F32)],
        semantics=("arbitrary",), args=(dn, x, r, w, dres), exchange=exchange)
    return (dx, dw) if exchange is None else (dx, dw, exchanged)


def _final_loss(x3, wf, tgt):
    s, d = x3.shape
    tm = min(512, s)

    def body(x_ref, w_ref, t_ref, loss_ref, dx_ref, dw_ref):
        @pl.when(pl.program_id(0) == 0)
        def _():
            dw_ref[...] = jnp.zeros_like(dw_ref)
            loss_ref[...] = jnp.zeros_like(loss_ref)

        xv = x_ref[...]
        wv = w_ref[...]
        r = lax.rsqrt(jnp.mean(xv * xv, axis=-1, keepdims=True) + EPS)
        xhat = xv * r
        e = xhat * wv - t_ref[...]
        part = 0.5 * jnp.sum(jnp.mean(e * e, axis=-1, keepdims=True), axis=0, keepdims=True)
        loss_ref[...] += jnp.broadcast_to(part, loss_ref.shape)
        dy = e * (1.0 / d)
        dw_ref[...] += jnp.sum(dy * xhat, axis=0, keepdims=True)
        t = dy * wv
        dx_ref[...] = r * (t - xhat * jnp.mean(t * xhat, axis=-1, keepdims=True))

    row = pl.BlockSpec((tm, d), lambda i: (i, 0))
    vec = pl.BlockSpec((1, d), lambda i: (0, 0))
    return pl.pallas_call(
        body, name="final_loss", grid=(s // tm,),
        in_specs=[row, vec, row],
        out_specs=[pl.BlockSpec((1, LANE), lambda i: (0, 0)), row, vec],
        out_shape=[jax.ShapeDtypeStruct((1, LANE), F32), jax.ShapeDtypeStruct((s, d), F32),
                   jax.ShapeDtypeStruct((1, d), F32)],
        compiler_params=_params("arbitrary"),
    )(x3, wf, tgt)


def _ffn_up(n, wg, wu, name, exchange=None):
    s, d = n.shape
    f = wg.shape[0]
    tm, tn = min(512, s), f // 2

    def body(n_ref, wg_ref, wu_ref, g_ref, u_ref, a_ref):
        nv = n_ref[...]
        g = _dot(nv, wg_ref[...], NT)
        u = _dot(nv, wu_ref[...], NT)
        g_ref[...] = g.astype(BF16)
        u_ref[...] = u.astype(BF16)
        a_ref[...] = (g * _sigmoid(g) * u).astype(BF16)

    o = pl.BlockSpec((tm, tn), lambda j, i: (i, j))
    wspec = pl.BlockSpec((tn, d), lambda j, i: (j, 0))
    return _pallas(
        body, name=name, grid=(f // tn, s // tm),
        in_specs=[pl.BlockSpec((tm, d), lambda j, i: (i, 0)), wspec, wspec],
        out_specs=[o, o, o],
        out_shape=[jax.ShapeDtypeStruct((s, f), BF16)] * 3,
        semantics=("parallel", "parallel"), args=(n, wg, wu), exchange=exchange)


def _ffn_dact(dx, wd, g, u, name, exchange=None):
    s, d = dx.shape
    f = wd.shape[0]
    tm, tn = min(512, s), f // 2

    def body(dx_ref, wd_ref, g_ref, u_ref, dg_ref, du_ref):
        da = 0.5 * _dot(dx_ref[...], wd_ref[...], NT)
        gv = g_ref[...].astype(F32)
        sg = _sigmoid(gv)
        du_ref[...] = (da * gv * sg).astype(BF16)
        dg_ref[...] = (da * u_ref[...].astype(F32) * (sg * (1.0 + gv * (1.0 - sg)))).astype(BF16)

    o = pl.BlockSpec((tm, tn), lambda j, i: (i, j))
    return _pallas(
        body, name=name, grid=(f // tn, s // tm),
        in_specs=[pl.BlockSpec((tm, d), lambda j, i: (i, 0)), pl.BlockSpec((tn, d), lambda j, i: (j, 0)), o, o],
        out_specs=[o, o],
        out_shape=[jax.ShapeDtypeStruct((s, f), BF16), jax.ShapeDtypeStruct((s, f), BF16)],
        semantics=("parallel", "parallel"), args=(dx, wd, g, u), exchange=exchange)


def _row_slabs(full):
    return full.reshape(N_DEV, full.shape[0] // N_DEV, full.shape[1])


def _ffn_forward(x, norm_w, wg, wu, wd, tag, gather=()):
    n, r = _rms_fwd(x, norm_w, f"{tag}_norm")
    (g, u, a), got = _ffn_up(n, wg, wu, f"{tag}_up", _Exchange("gather", gather) if gather else None)
    if wd is None:
        wd, got = got[0].reshape(N_DEV * got[0].shape[1], got[0].shape[2]), got[1:]
    y = _matmul([(a, wd)], tm=512, tn=1024, tk=wd.shape[0], name=f"{tag}_down", res=x, alpha=0.5)
    return y, (n, r, g, u, a), wd, got


def _ffn_backward(dy, x, norm_w, wgt, wut, wd, saved, tag, dw_dtype=F32, scatter=None):
    n, r, g, u, a = saved

    def behind(arrays):
        return _Exchange("scatter", arrays) if scatter is not None else None

    def dw(act, grad, name, alpha=None, exchange=None):
        return _matmul([(act, grad)], ta=True, tm=1408, tn=1024, tk=2048, name=name, alpha=alpha, out_dtype=dw_dtype,
                       exchange=exchange)

    dwd = _row_slabs(dw(a, dy, f"{tag}_dwd", alpha=0.5))
    (dg, du), extras = _ffn_dact(dy, wd, g, u, f"{tag}_dact", behind(scatter))
    if scatter is None:
        dwg, dwu = _row_slabs(dw(dg, n, f"{tag}_dwg")), _row_slabs(dw(du, n, f"{tag}_dwu"))
    else:
        dwg, (dwd,) = dw(dg, n, f"{tag}_dwg", exchange=behind([dwd]))
        dwg, dwu = _row_slabs(dwg), _row_slabs(dw(du, n, f"{tag}_dwu"))
    dn = _matmul([(dg, wgt), (du, wut)], tm=512, tn=1024, tk=wgt.shape[0], name=f"{tag}_dn", exchange=behind([dwg, dwu]))
    if scatter is not None:
        dn, (dwg, dwu) = dn
    dx, dnorm = _rms_bwd(dn, x, r, norm_w, dy, f"{tag}_dnorm")
    return dx, dnorm, dwg, dwu, dwd, extras


Q_SCALE = GDN_HEAD_DIM ** -0.5
CONV_HALO = 8


def _conv_taps(win, w_ref, rows, sign):
    n = rows + 2 * CONV_HALO
    acc = None
    for t in range(CONV_WIDTH):
        o = sign * (t - CONV_WIDTH // 2)
        sh = win if o == 0 else pltpu.roll(win, (-o) % n, 0)
        term = sh[CONV_HALO:CONV_HALO + rows] * w_ref[t:t + 1, :]
        acc = term if acc is None else acc + term
    return acc


def _gdn_conv_fwd(p_pad, conv_wt):
    s = p_pad.shape[0]
    rows = min(CONV_ROWS, s)
    nblk = QKV_A // LANE

    def body(p_ref, w_ref, c_ref, y_ref, pad):
        j = pl.program_id(0)
        zeros = jnp.zeros((CONV_HALO, LANE), F32)
        pad[0:CONV_HALO, :] = zeros
        pad[CONV_HALO + s:2 * CONV_HALO + s, :] = zeros
        pad[CONV_HALO:CONV_HALO + s, :] = p_ref[...]

        def chunk(ci, carry):
            b = pl.multiple_of(ci * rows, rows)
            win = pad[pl.ds(b, rows + 2 * CONV_HALO), :]
            c = _conv_taps(win, w_ref, rows, 1)
            c_ref[pl.ds(b, rows), :] = c
            act = c * _sigmoid(c)
            nrm = lax.rsqrt(jnp.sum(act * act, axis=-1, keepdims=True) + EPS)
            mult = jnp.where(j < GDN_HEADS, nrm * Q_SCALE, jnp.where(j < 2 * GDN_HEADS, nrm, 1.0))
            y_ref[pl.ds(b, rows), :] = act * mult
            return carry

        lax.fori_loop(0, s // rows, chunk, 0)

    col = pl.BlockSpec((s, LANE), lambda j: (0, j))
    return pl.pallas_call(
        body, name="gdn_conv_fwd", grid=(nblk,),
        in_specs=[col, pl.BlockSpec((8, LANE), lambda j: (0, j))],
        out_specs=[col, col],
        out_shape=[jax.ShapeDtypeStruct((s, QKV_A), F32), jax.ShapeDtypeStruct((s, QKV_A), F32)],
        scratch_shapes=[pltpu.VMEM((s + 2 * CONV_HALO, LANE), F32)],
        compiler_params=_params("parallel"),
    )(p_pad, conv_wt)


def _gdn_conv_bwd(dy_f, dy_r, c_pre, p_pad, conv_wt):
    s = p_pad.shape[0]
    rows = min(CONV_ROWS, s)
    nblk = QKV_A // LANE

    def body(dyf_ref, dyr_ref, c_ref, p_ref, w_ref, dp_ref, dw_ref, ppad, dcpad):
        j = pl.program_id(0)
        zeros = jnp.zeros((CONV_HALO, LANE), F32)
        for buf in (ppad, dcpad):
            buf[0:CONV_HALO, :] = zeros
            buf[CONV_HALO + s:2 * CONV_HALO + s, :] = zeros
        ppad[CONV_HALO:CONV_HALO + s, :] = p_ref[...]

        def act_bwd(ci, carry):
            b = pl.multiple_of(ci * rows, rows)
            c = c_ref[pl.ds(b, rows), :]
            g = dyf_ref[pl.ds(b, rows), :] + dyr_ref[pl.ds(b, rows), :]
            sg = _sigmoid(c)
            act = c * sg
            nrm = lax.rsqrt(jnp.sum(act * act, axis=-1, keepdims=True) + EPS)
            yh = act * nrm
            scale = jnp.where(j < GDN_HEADS, Q_SCALE, 1.0)
            dact_qk = (scale * nrm) * (g - yh * jnp.sum(g * yh, axis=-1, keepdims=True))
            dact = jnp.where(j < 2 * GDN_HEADS, dact_qk, g)
            dcpad[pl.ds(pl.multiple_of(b + CONV_HALO, CONV_HALO), rows), :] = dact * (sg * (1.0 + c * (1.0 - sg)))
            return carry

        lax.fori_loop(0, s // rows, act_bwd, 0)
        tap = lax.broadcasted_iota(jnp.int32, (8, LANE), 0)

        def taps_bwd(ci, dw):
            b = pl.multiple_of(ci * rows, rows)
            dcw = dcpad[pl.ds(b, rows + 2 * CONV_HALO), :]
            dp_ref[pl.ds(b, rows), :] = _conv_taps(dcw, w_ref, rows, -1)
            pw = ppad[pl.ds(b, rows + 2 * CONV_HALO), :]
            dc = dcw[CONV_HALO:CONV_HALO + rows]
            n = rows + 2 * CONV_HALO
            for t in range(CONV_WIDTH):
                o = t - CONV_WIDTH // 2
                sh = pw if o == 0 else pltpu.roll(pw, (-o) % n, 0)
                row = jnp.sum(dc * sh[CONV_HALO:CONV_HALO + rows], axis=0, keepdims=True)
                dw = dw + jnp.where(tap == t, row, 0.0)
            return dw

        dw_ref[...] = lax.fori_loop(0, s // rows, taps_bwd, jnp.zeros((8, LANE), F32))

    col = pl.BlockSpec((s, LANE), lambda j: (0, j))
    wspec = pl.BlockSpec((8, LANE), lambda j: (0, j))
    return pl.pallas_call(
        body, name="gdn_conv_bwd", grid=(nblk,),
        in_specs=[col, col, col, col, wspec],
        out_specs=[col, wspec],
        out_shape=[jax.ShapeDtypeStruct((s, QKV_A), F32), jax.ShapeDtypeStruct((8, QKV_A), F32)],
        scratch_shapes=[pltpu.VMEM((s + 2 * CONV_HALO, LANE), F32), pltpu.VMEM((s + 2 * CONV_HALO, LANE), F32)],
        compiler_params=_params("parallel"),
    )(dy_f, dy_r, c_pre, p_pad, conv_wt)


def _softplus(x):
    return jnp.maximum(x, 0.0) + jnp.log(1.0 + jnp.exp(-jnp.abs(x)))


def _gdn_gates_fwd(p_pad, alog_row, dt_row):
    s = p_pad.shape[0]
    tm = min(1024, s)

    def body(p_ref, al_ref, dt_ref, o_ref):
        x = p_ref[...]
        lane = lax.broadcasted_iota(jnp.int32, x.shape, 1)
        g = -jnp.exp(al_ref[...]) * _softplus(x + dt_ref[...])
        o_ref[...] = jnp.where(lane < 8, g, jnp.where(lane < 16, _sigmoid(x), 0.0))

    vec = pl.BlockSpec((1, LANE), lambda i: (0, 0))
    return pl.pallas_call(
        body, name="gdn_gates_fwd", grid=(s // tm,),
        in_specs=[pl.BlockSpec((tm, LANE), lambda i: (i, OFF_AB // LANE)), vec, vec],
        out_specs=pl.BlockSpec((tm, LANE), lambda i: (i, 0)),
        out_shape=jax.ShapeDtypeStruct((s, LANE), F32),
        compiler_params=_params("parallel"),
    )(p_pad, alog_row, dt_row)


def _gdn_gates_bwd(dgb_f, dgb_r, p_pad, gb, alog_row, dt_row):
    s = p_pad.shape[0]
    tm = min(1024, s)

    def body(df_ref, dr_ref, p_ref, gb_ref, al_ref, dt_ref, dp_ref, sum_ref):
        @pl.when(pl.program_id(0) == 0)
        def _():
            sum_ref[...] = jnp.zeros_like(sum_ref)

        x = p_ref[...]
        gbv = gb_ref[...]
        dgb = df_ref[...] + dr_ref[...]
        lane = lax.broadcasted_iota(jnp.int32, x.shape, 1)
        da = dgb * (-jnp.exp(al_ref[...])) * _sigmoid(x + dt_ref[...])
        db = dgb * gbv * (1.0 - gbv)
        dp_ref[...] = jnp.where(lane < 8, da, jnp.where(lane < 16, db, 0.0))
        row = lax.broadcasted_iota(jnp.int32, (8, LANE), 0)
        lane8 = lax.broadcasted_iota(jnp.int32, (8, LANE), 1)
        d_alog = jnp.sum(dgb * gbv, axis=0, keepdims=True)
        d_dt = jnp.sum(da, axis=0, keepdims=True)
        upd = jnp.where(row == 0, d_alog, jnp.where(row == 1, d_dt, 0.0))
        sum_ref[...] += jnp.where(lane8 < 8, upd, 0.0)

    vec = pl.BlockSpec((1, LANE), lambda i: (0, 0))
    blk = pl.BlockSpec((tm, LANE), lambda i: (i, 0))
    return pl.pallas_call(
        body, name="gdn_gates_bwd", grid=(s // tm,),
        in_specs=[blk, blk, pl.BlockSpec((tm, LANE), lambda i: (i, OFF_AB // LANE)), blk, vec, vec],
        out_specs=[blk, pl.BlockSpec((8, LANE), lambda i: (0, 0))],
        out_shape=[jax.ShapeDtypeStruct((s, LANE), F32), jax.ShapeDtypeStruct((8, LANE), F32)],
        compiler_params=_params("arbitrary"),
    )(dgb_f, dgb_r, p_pad, gb, alog_row, dt_row)


def _chunk_masks(rev):
    row = lax.broadcasted_iota(jnp.int32, (CHUNK, CHUNK), 0)
    col = lax.broadcasted_iota(jnp.int32, (CHUNK, CHUNK), 1)
    le = (col >= row) if rev else (col <= row)
    strict = (col > row) if rev else (col < row)
    return le, strict, row == col


def _chunk_common(q, k, v, g, beta, gc, masks):
    le, strict, eye = masks
    gc_row = _dot_hi(jnp.ones((CHUNK, CHUNK), F32), jnp.where(eye, gc, 0.0))
    decay = jnp.where(le, jnp.exp(jnp.where(le, gc - gc_row, 0.0)), 0.0)
    eg = jnp.exp(gc)
    gl = jnp.sum(g, axis=0, keepdims=True)
    kb = k * beta
    vb = v * beta
    kbeg = kb * eg
    lm = jnp.where(strict, _dot(kb, k, NT) * decay, 0.0)
    intra = _dot(q, k, NT) * decay
    qg = q * eg
    edec = jnp.exp(gl - gc)
    kdec = k * edec
    return dict(decay=decay, eg=eg, gl=gl, kb=kb, vb=vb, kbeg=kbeg, lm=lm, intra=intra, qg=qg, edec=edec, kdec=kdec)


def _unit_lower_inverse(lm, eye):
    x = -lm
    t = eye.astype(F32) + x
    p = x
    for _ in range(5):
        p = _dot_hi(p, p)
        t = t + _dot_hi(t, p)
    return t


def _gate_lanes(rev, h):
    d = 1 if rev else 0
    return d * GDN_HEADS + h, 8 + d * GDN_HEADS + h


def _delta_fwd(y, gb, rev):
    s = y.shape[0]
    nc = s // CHUNK
    hd = GDN_HEAD_DIM

    def chunk_of(n):
        return nc - 1 - n if rev else n

    def body(q_ref, k_ref, v_ref, gb_ref, o_ref, s_all, t_all, state):
        @pl.when(pl.program_id(0) == 0)
        def _():
            state[...] = jnp.zeros_like(state)

        masks = _chunk_masks(rev)
        gbv = gb_ref[...]
        gcm = _dot_hi(masks[0].astype(F32), gbv)
        for h in range(GDN_HEADS):
            gi, bi = _gate_lanes(rev, h)
            sl = slice(h * hd, (h + 1) * hd)
            q, k, v = q_ref[:, sl], k_ref[:, sl], v_ref[:, sl]
            g, beta, gc = gbv[:, gi:gi + 1], gbv[:, bi:bi + 1], gcm[:, gi:gi + 1]
            cm = _chunk_common(q, k, v, g, beta, gc, masks)
            tinv = _unit_lower_inverse(cm["lm"], masks[2])
            u = _dot(tinv, cm["vb"])
            w = _dot(tinv, cm["kbeg"])
            st = state[h]
            v_new = u - _dot(w, st)
            o_ref[:, sl] = _dot(cm["qg"], st) + _dot(cm["intra"], v_new)
            s_all[0, h] = st
            t_all[0, h] = tinv
            state[h] = st * jnp.exp(cm["gl"]) + _dot(cm["kdec"], v_new, TN)

    def col(j):
        return pl.BlockSpec((CHUNK, GDN_WIDTH), lambda n: (chunk_of(n), j))

    return pl.pallas_call(
        body, name="delta_fwd_r" if rev else "delta_fwd_f", grid=(nc,),
        in_specs=[col(0), col(1), col(2), pl.BlockSpec((CHUNK, LANE), lambda n: (chunk_of(n), 0))],
        out_specs=[pl.BlockSpec((CHUNK, GDN_WIDTH), lambda n: (chunk_of(n), 0)),
                   pl.BlockSpec((1, GDN_HEADS, hd, hd), lambda n: (chunk_of(n), 0, 0, 0)),
                   pl.BlockSpec((1, GDN_HEADS, CHUNK, CHUNK), lambda n: (chunk_of(n), 0, 0, 0))],
        out_shape=[jax.ShapeDtypeStruct((s, GDN_WIDTH), F32),
                   jax.ShapeDtypeStruct((nc, GDN_HEADS, hd, hd), F32),
                   jax.ShapeDtypeStruct((nc, GDN_HEADS, CHUNK, CHUNK), F32)],
        scratch_shapes=[pltpu.VMEM((GDN_HEADS, hd, hd), F32)],
        compiler_params=_params("arbitrary"),
    )(y, y, y, gb)


def _delta_bwd(y, gb, do, s_all, t_all, rev):
    s = y.shape[0]
    nc = s // CHUNK
    hd = GDN_HEAD_DIM

    def chunk_of(n):
        return n if rev else nc - 1 - n

    def body(q_ref, k_ref, v_ref, gb_ref, do_ref, s_ref, t_ref, dy_ref, dgb_ref, dstate):
        @pl.when(pl.program_id(0) == 0)
        def _():
            dstate[...] = jnp.zeros_like(dstate)

        masks = _chunk_masks(rev)
        le, strict, _ = masks
        le_t = _chunk_masks(not rev)[0].astype(F32)
        gbv = gb_ref[...]
        gcm = _dot_hi(le.astype(F32), gbv)
        lane = lax.broadcasted_iota(jnp.int32, (CHUNK, LANE), 1)
        ones_cl = jnp.ones((CHUNK, LANE), F32)
        dgc_tile = jnp.zeros((CHUNK, LANE), F32)
        rest_tile = jnp.zeros((CHUNK, LANE), F32)
        for h in range(GDN_HEADS):
            gi, bi = _gate_lanes(rev, h)
            sl = slice(h * hd, (h + 1) * hd)
            q, k, v = q_ref[:, sl], k_ref[:, sl], v_ref[:, sl]
            g, beta, gc = gbv[:, gi:gi + 1], gbv[:, bi:bi + 1], gcm[:, gi:gi + 1]
            cm = _chunk_common(q, k, v, g, beta, gc, masks)
            tinv = t_ref[0, h]
            st = s_ref[0, h]
            ds_out = dstate[h]
            dov = do_ref[:, sl]
            u = _dot(tinv, cm["vb"])
            w = _dot(tinv, cm["kbeg"])
            v_new = u - _dot(w, st)
            egl = jnp.exp(cm["gl"])
            d_qg = _dot(dov, st, NT)
            d_intra = _dot(dov, v_new, NT)
            dv_new = _dot(cm["intra"], dov, TN) + _dot(cm["kdec"], ds_out)
            d_kdec = _dot(v_new, ds_out, NT)
            dstate[h] = _dot(cm["qg"], dov, TN) + egl * ds_out - _dot(w, dv_new, TN)
            dgl = egl * jnp.sum(jnp.sum(st * ds_out, axis=1, keepdims=True), axis=0, keepdims=True)
            dw = -_dot(dv_new, st, NT)
            dvb = _dot(tinv, dv_new, TN)
            dkbeg = _dot(tinv, dw, TN)
            dlm = jnp.where(strict, -(_dot(dvb, u, NT) + _dot(dkbeg, w, NT)), 0.0)
            d_a = dlm * cm["decay"]
            d_qk = d_intra * cm["decay"]
            e = dlm * cm["lm"] + d_intra * cm["intra"]
            dgc = jnp.sum(e, axis=1, keepdims=True) - _dot_hi(e, ones_cl, TN)[:, 0:1]
            dkb = _dot(d_a, k) + dkbeg * cm["eg"]
            dk = _dot(d_a, cm["kb"], TN) + _dot(d_qk, q, TN)
            dq = _dot(d_qk, k) + d_qg * cm["eg"]
            dgc = dgc + jnp.sum(d_qg * cm["qg"], axis=1, keepdims=True)
            dgc = dgc + jnp.sum(dkbeg * cm["kbeg"], axis=1, keepdims=True)
            tdec = jnp.sum(d_kdec * cm["kdec"], axis=1, keepdims=True)
            dk = dk + d_kdec * cm["edec"] + dkb * beta
            dgc = dgc - tdec
            dgl = dgl + jnp.sum(tdec, axis=0, keepdims=True)
            dbeta = jnp.sum(dvb * v, axis=1, keepdims=True) + jnp.sum(dkb * k, axis=1, keepdims=True)
            dy_ref[:, h * hd:(h + 1) * hd] = dq
            dy_ref[:, GDN_WIDTH + h * hd:GDN_WIDTH + (h + 1) * hd] = dk
            dy_ref[:, 2 * GDN_WIDTH + h * hd:2 * GDN_WIDTH + (h + 1) * hd] = dvb * beta
            dgc_tile = dgc_tile + jnp.where(lane == gi, dgc, 0.0)
            rest_tile = rest_tile + jnp.where(lane == gi, dgl, 0.0) + jnp.where(lane == bi, dbeta, 0.0)
        dgb_ref[...] = _dot_hi(le_t, dgc_tile) + rest_tile

    def col(j):
        return pl.BlockSpec((CHUNK, GDN_WIDTH), lambda n: (chunk_of(n), j))

    first = pl.BlockSpec((CHUNK, GDN_WIDTH), lambda n: (chunk_of(n), 0))
    return pl.pallas_call(
        body, name="delta_bwd_r" if rev else "delta_bwd_f", grid=(nc,),
        in_specs=[col(0), col(1), col(2), pl.BlockSpec((CHUNK, LANE), lambda n: (chunk_of(n), 0)), first,
                  pl.BlockSpec((1, GDN_HEADS, hd, hd), lambda n: (chunk_of(n), 0, 0, 0)),
                  pl.BlockSpec((1, GDN_HEADS, CHUNK, CHUNK), lambda n: (chunk_of(n), 0, 0, 0))],
        out_specs=[pl.BlockSpec((CHUNK, QKV_A), lambda n: (chunk_of(n), 0)),
                   pl.BlockSpec((CHUNK, LANE), lambda n: (chunk_of(n), 0))],
        out_shape=[jax.ShapeDtypeStruct((s, QKV_A), F32), jax.ShapeDtypeStruct((s, LANE), F32)],
        scratch_shapes=[pltpu.VMEM((GDN_HEADS, hd, hd), F32)],
        compiler_params=_params("arbitrary"),
    )(y, y, y, gb, do, s_all, t_all)


BNN = (((2,), (1,)), ((0,), (0,)))
BNT = (((2,), (2,)), ((0,), (0,)))
BTN = (((1,), (1,)), ((0,), (0,)))
NB = 2 * GDN_HEADS


def _bdot(a, b, dn=BNN):
    return lax.dot_general(a.astype(BF16), b.astype(BF16), dn, preferred_element_type=F32)


def _dot3(a, b, dn):
    ah = a.astype(BF16)
    al = (a - ah.astype(F32)).astype(BF16)
    bh = b.astype(BF16)
    bl = (b - bh.astype(F32)).astype(BF16)

    def d(x, y):
        return lax.dot_general(x, y, dn, preferred_element_type=F32)

    return d(ah, bh) + d(ah, bl) + d(al, bh)


def _both(f_val, r_val):
    return jnp.stack([f_val] * GDN_HEADS + [r_val] * GDN_HEADS)


def _heads(ref_f, ref_r):
    hd = GDN_HEAD_DIM
    return jnp.stack([ref_f[:, h * hd:(h + 1) * hd] for h in range(GDN_HEADS)]
                     + [ref_r[:, h * hd:(h + 1) * hd] for h in range(GDN_HEADS)])


def _gate_cols(tile_f, tile_r, base):
    return jnp.stack([tile_f[:, base + h:base + h + 1] for h in range(GDN_HEADS)]
                     + [tile_r[:, base + GDN_HEADS + h:base + GDN_HEADS + h + 1] for h in range(GDN_HEADS)])


def _chunk_common2(q, k, v, gbf, gbr):
    mf, mr = _chunk_masks(False), _chunk_masks(True)
    le, strict = _both(mf[0], mr[0]), _both(mf[1], mr[1])
    eye = mf[2]
    gcm_f = _dot3(mf[0].astype(F32), gbf, NN)
    gcm_r = _dot3(mr[0].astype(F32), gbr, NN)
    g, beta, gc = _gate_cols(gbf, gbr, 0), _gate_cols(gbf, gbr, 8), _gate_cols(gcm_f, gcm_r, 0)
    gc_row = _dot3(jnp.ones((NB, CHUNK, CHUNK), F32), jnp.where(eye[None], gc, 0.0), BNN)
    decay = jnp.where(le, jnp.exp(jnp.where(le, gc - gc_row, 0.0)), 0.0)
    eg = jnp.exp(gc)
    gl = jnp.sum(g, axis=1, keepdims=True)
    kb = k * beta
    vb = v * beta
    kbeg = kb * eg
    lm = jnp.where(strict, _bdot(kb, k, BNT) * decay, 0.0)
    intra = _bdot(q, k, BNT) * decay
    edec = jnp.exp(gl - gc)
    return dict(strict=strict, eye=eye, beta=beta, decay=decay, eg=eg, gl=gl, kb=kb, vb=vb, kbeg=kbeg,
                lm=lm, intra=intra, qg=q * eg, edec=edec, kdec=k * edec)


def _unit_triangular_inverse(lm, eye):
    x = -lm
    t = eye[None].astype(F32) + x
    p = x
    for _ in range(5):
        p = _dot3(p, p, BNN)
        t = t + _dot3(t, p, BNN)
    return t


def _delta_fwd2(y, gb, gather=()):
    s = y.shape[0]
    nc = s // CHUNK
    hd = GDN_HEAD_DIM
    na = len(gather)

    def body(*refs):
        qf, kf, vf, gf, qr, kr, vr, gr = refs[:8]
        of_ref, or_ref, sf_all, sr_all, tf_all, tr_all = refs[8 + na:14 + na]
        state = refs[14 + 2 * na]
        step = pl.program_id(0)

        @pl.when(step == 0)
        def _():
            state[...] = jnp.zeros_like(state)

        if na:
            start, forward, finish = _gather_phases(refs[8:8 + na], refs[14 + na:14 + 2 * na], *refs[15 + 2 * na:])
            pl.when(step == 0)(start)
            pl.when(step == nc // 2)(forward)
            pl.when(step == nc - 1)(finish)

        q, k, v = _heads(qf, qr), _heads(kf, kr), _heads(vf, vr)
        cm = _chunk_common2(q, k, v, gf[...], gr[...])
        tinv = _unit_triangular_inverse(cm["lm"], cm["eye"])
        u = _bdot(tinv, cm["vb"])
        w = _bdot(tinv, cm["kbeg"])
        st = state[...]
        v_new = u - _bdot(w, st)
        o = _bdot(cm["qg"], st) + _bdot(cm["intra"], v_new)
        state[...] = st * jnp.exp(cm["gl"]) + _bdot(cm["kdec"], v_new, BTN)
        for h in range(GDN_HEADS):
            of_ref[:, h * hd:(h + 1) * hd] = o[h]
            or_ref[:, h * hd:(h + 1) * hd] = o[GDN_HEADS + h]
        sf_all[0] = st[:GDN_HEADS]
        sr_all[0] = st[GDN_HEADS:]
        tf_all[0] = tinv[:GDN_HEADS]
        tr_all[0] = tinv[GDN_HEADS:]

    def col(j, rev):
        return pl.BlockSpec((CHUNK, GDN_WIDTH), (lambda n: (nc - 1 - n, j)) if rev else (lambda n: (n, j)))

    def gate(rev):
        return pl.BlockSpec((CHUNK, LANE), (lambda n: (nc - 1 - n, 0)) if rev else (lambda n: (n, 0)))

    def per_chunk(d1, d2, rev):
        return pl.BlockSpec((1, GDN_HEADS, d1, d2), (lambda n: (nc - 1 - n, 0, 0, 0)) if rev else (lambda n: (n, 0, 0, 0)))

    assert na == 0 or nc >= 4
    res = pl.pallas_call(
        body, name="delta_fwd", grid=(nc,),
        in_specs=[col(0, False), col(1, False), col(2, False), gate(False), col(0, True), col(1, True), col(2, True), gate(True)]
        + [ANY] * na,
        out_specs=[col(0, False), col(0, True), per_chunk(hd, hd, False), per_chunk(hd, hd, True),
                   per_chunk(CHUNK, CHUNK, False), per_chunk(CHUNK, CHUNK, True)] + [ANY] * na,
        out_shape=[jax.ShapeDtypeStruct((s, GDN_WIDTH), F32)] * 2 + [jax.ShapeDtypeStruct((nc, GDN_HEADS, hd, hd), F32)] * 2
        + [jax.ShapeDtypeStruct((nc, GDN_HEADS, CHUNK, CHUNK), F32)] * 2
        + [jax.ShapeDtypeStruct((N_DEV,) + v.shape, v.dtype) for v in gather],
        scratch_shapes=[pltpu.VMEM((NB, hd, hd), F32)] + (_gather_semaphores(na) if na else []),
        compiler_params=_params("arbitrary"),
    )(y, y, y, gb, y, y, y, gb, *gather)
    return res[:6], res[6:]


def _delta_bwd2(y, gb, do, sf_all, sr_all, tf_all, tr_all, scatter=()):
    s = y.shape[0]
    nc = s // CHUNK
    hd = GDN_HEAD_DIM
    na = len(scatter)

    def body(*refs):
        qf, kf, vf, gf, dof, sf, tf, qr, kr, vr, gr, dor, sr, tr = refs[:14]
        dyf_ref, dyr_ref, dgf_ref, dgr_ref = refs[14 + na:18 + na]
        dstate = refs[18 + 2 * na]
        step = pl.program_id(0)

        @pl.when(step == 0)
        def _():
            dstate[...] = jnp.zeros_like(dstate)

        if na:
            start, finish = _scatter_phases(refs[14:14 + na], refs[18 + na:18 + 2 * na], *refs[19 + 2 * na:])
            pl.when(step == 0)(start)
            pl.when(step == nc - 1)(finish)

        q, k, v, dov = _heads(qf, qr), _heads(kf, kr), _heads(vf, vr), _heads(dof, dor)
        cm = _chunk_common2(q, k, v, gf[...], gr[...])
        tinv = jnp.concatenate([tf[0], tr[0]], axis=0)
        st = jnp.concatenate([sf[0], sr[0]], axis=0)
        ds_out = dstate[...]
        decay, lm, intra, qg, kdec, kbeg, eg, kb, beta = (
            cm[n] for n in ("decay", "lm", "intra", "qg", "kdec", "kbeg", "eg", "kb", "beta"))
        u = _bdot(tinv, cm["vb"])
        w = _bdot(tinv, kbeg)
        v_new = u - _bdot(w, st)
        egl = jnp.exp(cm["gl"])
        d_qg = _bdot(dov, st, BNT)
        d_intra = _bdot(dov, v_new, BNT)
        dv_new = _bdot(intra, dov, BTN) + _bdot(kdec, ds_out)
        d_kdec = _bdot(v_new, ds_out, BNT)
        dstate[...] = _bdot(qg, dov, BTN) + egl * ds_out - _bdot(w, dv_new, BTN)
        dgl = egl * jnp.sum(jnp.sum(st * ds_out, axis=2, keepdims=True), axis=1, keepdims=True)
        dw = -_bdot(dv_new, st, BNT)
        dvb = _bdot(tinv, dv_new, BTN)
        dkbeg = _bdot(tinv, dw, BTN)
        dlm = jnp.where(cm["strict"], -(_bdot(dvb, u, BNT) + _bdot(dkbeg, w, BNT)), 0.0)
        d_a = dlm * decay
        d_qk = d_intra * decay
        e = dlm * lm + d_intra * intra
        colsum = _dot3(e, jnp.ones((NB, CHUNK, LANE), F32), BTN)[:, :, 0:1]
        dgc = jnp.sum(e, axis=2, keepdims=True) - colsum
        dkb = _bdot(d_a, k) + dkbeg * eg
        dk = _bdot(d_a, kb, BTN) + _bdot(d_qk, q, BTN)
        dq = _bdot(d_qk, k) + d_qg * eg
        dgc = dgc + jnp.sum(d_qg * qg, axis=2, keepdims=True) + jnp.sum(dkbeg * kbeg, axis=2, keepdims=True)
        tdec = jnp.sum(d_kdec * kdec, axis=2, keepdims=True)
        dk = dk + d_kdec * cm["edec"] + dkb * beta
        dgc = dgc - tdec
        dgl = dgl + jnp.sum(tdec, axis=1, keepdims=True)
        dbeta = jnp.sum(dvb * v, axis=2, keepdims=True) + jnp.sum(dkb * k, axis=2, keepdims=True)
        dv = dvb * beta
        lane = lax.broadcasted_iota(jnp.int32, (CHUNK, LANE), 1)
        for rev, dy_ref, dg_ref in ((False, dyf_ref, dgf_ref), (True, dyr_ref, dgr_ref)):
            dgc_tile = jnp.zeros((CHUNK, LANE), F32)
            rest = jnp.zeros((CHUNK, LANE), F32)
            for h in range(GDN_HEADS):
                b = (GDN_HEADS if rev else 0) + h
                gi, bi = _gate_lanes(rev, h)
                dgc_tile = dgc_tile + jnp.where(lane == gi, dgc[b], 0.0)
                rest = rest + jnp.where(lane == gi, dgl[b], 0.0) + jnp.where(lane == bi, dbeta[b], 0.0)
                dy_ref[:, h * hd:(h + 1) * hd] = dq[b]
                dy_ref[:, GDN_WIDTH + h * hd:GDN_WIDTH + (h + 1) * hd] = dk[b]
                dy_ref[:, 2 * GDN_WIDTH + h * hd:2 * GDN_WIDTH + (h + 1) * hd] = dv[b]
            le_t = _chunk_masks(not rev)[0].astype(F32)
            dg_ref[...] = _dot3(le_t, dgc_tile, NN) + rest

    def col(j, rev):
        return pl.BlockSpec((CHUNK, GDN_WIDTH), (lambda n: (n, j)) if rev else (lambda n: (nc - 1 - n, j)))

    def wide(width, rev):
        return pl.BlockSpec((CHUNK, width), (lambda n: (n, 0)) if rev else (lambda n: (nc - 1 - n, 0)))

    def per_chunk(d1, d2, rev):
        return pl.BlockSpec((1, GDN_HEADS, d1, d2), (lambda n: (n, 0, 0, 0)) if rev else (lambda n: (nc - 1 - n, 0, 0, 0)))

    def side(rev):
        return [col(0, rev), col(1, rev), col(2, rev), wide(LANE, rev), wide(GDN_WIDTH, rev), per_chunk(hd, hd, rev),
                per_chunk(CHUNK, CHUNK, rev)]

    assert na == 0 or nc >= 2
    res = pl.pallas_call(
        body, name="delta_bwd", grid=(nc,),
        in_specs=side(False) + side(True) + [ANY] * na,
        out_specs=[wide(QKV_A, False), wide(QKV_A, True), wide(LANE, False), wide(LANE, True)] + [ANY] * na,
        out_shape=[jax.ShapeDtypeStruct((s, QKV_A), F32)] * 2 + [jax.ShapeDtypeStruct((s, LANE), F32)] * 2
        + [jax.ShapeDtypeStruct(g.shape, g.dtype) for g in scatter],
        scratch_shapes=[pltpu.VMEM((NB, hd, hd), F32)] + (_gather_semaphores(na) if na else []),
        compiler_params=_params("arbitrary"),
    )(y, y, y, gb, do, sf_all, tf_all, y, y, y, gb, do, sr_all, tr_all, *scatter)
    return res[:4], res[4:]


def _gdn_post_fwd(o_f, o_r, p_pad, norm_row):
    s = o_f.shape[0]
    tm = min(512, s)
    hd = GDN_HEAD_DIM

    def body(of_ref, or_ref, z_ref, w_ref, out_ref, osum_ref):
        o = of_ref[...] + or_ref[...]
        osum_ref[...] = o
        z = z_ref[...]
        gate = z * _sigmoid(z)
        for h in range(GDN_HEADS):
            sl = slice(h * hd, (h + 1) * hd)
            oh = o[:, sl]
            r = lax.rsqrt(jnp.mean(oh * oh, axis=-1, keepdims=True) + EPS)
            out_ref[:, sl] = (oh * r * w_ref[...] * gate[:, sl]).astype(BF16)

    blk = pl.BlockSpec((tm, GDN_WIDTH), lambda i: (i, 0))
    return pl.pallas_call(
        body, name="gdn_post_fwd", grid=(s // tm,),
        in_specs=[blk, blk, pl.BlockSpec((tm, GDN_WIDTH), lambda i: (i, OFF_Z // GDN_WIDTH)),
                  pl.BlockSpec((1, hd), lambda i: (0, 0))],
        out_specs=[blk, blk],
        out_shape=[jax.ShapeDtypeStruct((s, GDN_WIDTH), BF16), jax.ShapeDtypeStruct((s, GDN_WIDTH), F32)],
        compiler_params=_params("parallel"),
    )(o_f, o_r, p_pad, norm_row)


def _gdn_post_bwd(d_out, o_sum, p_pad, norm_row):
    s = o_sum.shape[0]
    tm = min(512, s)
    hd = GDN_HEAD_DIM

    def body(d_ref, o_ref, z_ref, w_ref, do_ref, dz_ref, dw_ref):
        @pl.when(pl.program_id(0) == 0)
        def _():
            dw_ref[...] = jnp.zeros_like(dw_ref)

        z = z_ref[...]
        sg = _sigmoid(z)
        gate = z * sg
        dgate = sg * (1.0 + z * (1.0 - sg))
        wv = w_ref[...]
        dw = jnp.zeros((1, hd), F32)
        for h in range(GDN_HEADS):
            sl = slice(h * hd, (h + 1) * hd)
            oh = o_ref[:, sl]
            dh = d_ref[:, sl]
            r = lax.rsqrt(jnp.mean(oh * oh, axis=-1, keepdims=True) + EPS)
            ohat = oh * r
            dz_ref[:, sl] = dh * ohat * wv * dgate[:, sl]
            drn = dh * gate[:, sl]
            t = drn * wv
            do_ref[:, sl] = r * (t - ohat * jnp.mean(t * ohat, axis=-1, keepdims=True))
            dw = dw + jnp.sum(drn * ohat, axis=0, keepdims=True)
        dw_ref[...] += dw

    blk = pl.BlockSpec((tm, GDN_WIDTH), lambda i: (i, 0))
    vec = pl.BlockSpec((1, hd), lambda i: (0, 0))
    return pl.pallas_call(
        body, name="gdn_post_bwd", grid=(s // tm,),
        in_specs=[blk, blk, pl.BlockSpec((tm, GDN_WIDTH), lambda i: (i, OFF_Z // GDN_WIDTH)), vec],
        out_specs=[blk, blk, vec],
        out_shape=[jax.ShapeDtypeStruct((s, GDN_WIDTH), F32), jax.ShapeDtypeStruct((s, GDN_WIDTH), F32),
                   jax.ShapeDtypeStruct((1, hd), F32)],
        compiler_params=_params("arbitrary"),
    )(d_out, o_sum, p_pad, norm_row)


def _add2(a, b, name):
    s, w = a.shape
    tm = next(t for t in (1024, 640, 512, 256, 128, 64, 8) if s % t == 0)

    def body(a_ref, b_ref, o_ref):
        o_ref[...] = a_ref[...] + b_ref[...]

    blk = pl.BlockSpec((tm, w), lambda i: (i, 0))
    return pl.pallas_call(body, name=name, grid=(s // tm,), in_specs=[blk, blk], out_specs=blk,
                          out_shape=jax.ShapeDtypeStruct((s, w), F32), compiler_params=_params("parallel"))(a, b)


def _gdn_forward(p_pad, conv_wt, alog_row, dt_row, norm_row, gather=()):
    c_pre, y = _gdn_conv_fwd(p_pad, conv_wt)
    gb = _gdn_gates_fwd(p_pad, alog_row, dt_row)
    (o_f, o_r, s_f, s_r, t_f, t_r), gathered = _delta_fwd2(y, gb, gather)
    out, o_sum = _gdn_post_fwd(o_f, o_r, p_pad, norm_row)
    return out, (c_pre, y, gb, s_f, t_f, s_r, t_r, o_sum), gathered


def _gdn_backward(d_out, p_pad, conv_wt, alog_row, dt_row, norm_row, saved, scatter=()):
    c_pre, y, gb, s_f, t_f, s_r, t_r, o_sum = saved
    do, dz, dnorm = _gdn_post_bwd(d_out, o_sum, p_pad, norm_row)
    (dy_f, dy_r, dgb_f, dgb_r), received = _delta_bwd2(y, gb, do, s_f, s_r, t_f, t_r, scatter)
    dp_qkv, dconv = _gdn_conv_bwd(dy_f, dy_r, c_pre, p_pad, conv_wt)
    dp_ab, gate_sums = _gdn_gates_bwd(dgb_f, dgb_r, p_pad, gb, alog_row, dt_row)
    return dp_qkv, dz, dp_ab, dconv, gate_sums, dnorm, received


ATT_BK = ATT_BQ + 2 * ATT_HALO
SWA_SCALE = SWA_HEAD_DIM ** -0.5


def _t5_bucket(rel):
    nb = REL_BUCKETS // 2
    bucket = (rel > 0).astype(np.int32) * nb
    n = np.abs(rel)
    max_exact = nb // 2
    large = max_exact + (np.log(np.maximum(n, 1) / max_exact)
                         / math.log(REL_MAX_DISTANCE / max_exact) * (nb - max_exact)).astype(np.int32)
    large = np.minimum(large, nb - 1)
    return (bucket + np.where(n < max_exact, n, large)).astype(np.int32)


def _band_tables(dilation, queries_are_rows_of_block):
    blk = np.arange(ATT_BQ)
    band = np.arange(ATT_BK) - ATT_HALO
    if queries_are_rows_of_block:
        rel = band[None, :] - blk[:, None]
        band_idx = np.broadcast_to(np.arange(ATT_BK)[None, :], rel.shape)
    else:
        rel = blk[None, :] - band[:, None]
        band_idx = np.broadcast_to(np.arange(ATT_BK)[:, None], rel.shape)
    base = np.abs(rel) <= ATT_HALO
    not_prev = band_idx >= ATT_HALO
    not_next = band_idx < ATT_HALO + ATT_BQ
    valid = np.stack([base & not_prev, base, base & not_next, base & not_prev & not_next])
    return valid, _t5_bucket(rel * dilation)


def _bias_tiles(rel_bias, dilation, queries_are_rows_of_block):
    valid, bucket = _band_tables(dilation, queries_are_rows_of_block)
    onehot = (jnp.asarray(bucket.reshape(-1, 1)) == jnp.arange(REL_BUCKETS, dtype=jnp.int32)[None, :]).astype(F32)
    rb = jnp.dot(onehot, rel_bias.astype(F32), precision=lax.Precision.HIGHEST)
    rb = rb.T.reshape((SWA_HEADS,) + bucket.shape)
    return jnp.where(valid[:, None], rb[None], NEG_BIG).astype(F32)


def _group_sum(x, bd):
    hi = x.astype(BF16)
    lo = (x - hi.astype(F32)).astype(BF16)
    return jnp.dot(hi, bd, preferred_element_type=F32) + jnp.dot(lo, bd, preferred_element_type=F32)


def _head_block_diag():
    idx = np.arange(SWA_WIDTH) // SWA_HEAD_DIM
    return jnp.asarray(idx[:, None] == idx[None, :], BF16)


def _swa_pre_fwd(p_pad, qw_row, kw_row, bd):
    s = p_pad.shape[0]
    tm = min(512, s)
    inv = 1.0 / SWA_HEAD_DIM

    def body(q_ref, k_ref, v_ref, qw_ref, kw_ref, bd_ref, qo_ref, ko_ref, vo_ref):
        bdv = bd_ref[...]
        q = q_ref[...]
        k = k_ref[...]
        rq = lax.rsqrt(_group_sum(q * q, bdv) * inv + EPS)
        rk = lax.rsqrt(_group_sum(k * k, bdv) * inv + EPS)
        qo_ref[...] = (q * rq * qw_ref[...] * SWA_SCALE).astype(BF16)
        ko_ref[...] = (k * rk * kw_ref[...]).astype(BF16)
        vo_ref[...] = v_ref[...].astype(BF16)

    base = OFF_B // SWA_WIDTH
    blk = pl.BlockSpec((tm, SWA_WIDTH), lambda i: (i, 0))
    vec = pl.BlockSpec((1, SWA_WIDTH), lambda i: (0, 0))
    return pl.pallas_call(
        body, name="swa_pre_fwd", grid=(s // tm,),
        in_specs=[pl.BlockSpec((tm, SWA_WIDTH), lambda i: (i, base)), pl.BlockSpec((tm, SWA_WIDTH), lambda i: (i, base + 1)),
                  pl.BlockSpec((tm, SWA_WIDTH), lambda i: (i, base + 2)), vec, vec,
                  pl.BlockSpec((SWA_WIDTH, SWA_WIDTH), lambda i: (0, 0))],
        out_specs=[blk, blk, blk],
        out_shape=[jax.ShapeDtypeStruct((s, SWA_WIDTH), BF16)] * 3,
        compiler_params=_params("parallel"),
    )(p_pad, p_pad, p_pad, qw_row, kw_row, bd)


def _swa_pre_bwd(dqs, dks, dvs, p_pad, qw_row, kw_row, bd):
    s = p_pad.shape[0]
    tm = min(256, s)
    inv = 1.0 / SWA_HEAD_DIM
    npat = len(dqs)

    def body(*refs):
        dq_refs, dk_refs, dv_refs = refs[:npat], refs[npat:2 * npat], refs[2 * npat:3 * npat]
        q_ref, k_ref, qw_ref, kw_ref, bd_ref, dp_ref, dqw_ref, dkw_ref = refs[3 * npat:]

        @pl.when(pl.program_id(0) == 0)
        def _():
            dqw_ref[...] = jnp.zeros_like(dqw_ref)
            dkw_ref[...] = jnp.zeros_like(dkw_ref)

        bdv = bd_ref[...]

        def norm_bwd(x, g, w, scale):
            r = lax.rsqrt(_group_sum(x * x, bdv) * inv + EPS)
            xhat = x * r
            t = g * w * scale
            dx = r * (t - xhat * (_group_sum(t * xhat, bdv) * inv))
            return dx, jnp.sum(g * scale * xhat, axis=0, keepdims=True)

        def total(rs):
            t = rs[0][...].astype(F32)
            for r in rs[1:]:
                t = t + r[...].astype(F32)
            return t

        dq, dqw = norm_bwd(q_ref[...], total(dq_refs), qw_ref[...], SWA_SCALE)
        dk, dkw = norm_bwd(k_ref[...], total(dk_refs), kw_ref[...], 1.0)
        dp_ref[:, 0:SWA_WIDTH] = dq
        dp_ref[:, SWA_WIDTH:2 * SWA_WIDTH] = dk
        dp_ref[:, 2 * SWA_WIDTH:3 * SWA_WIDTH] = total(dv_refs)
        dqw_ref[...] += dqw
        dkw_ref[...] += dkw

    base = OFF_B // SWA_WIDTH
    blk = pl.BlockSpec((tm, SWA_WIDTH), lambda i: (i, 0))
    vec = pl.BlockSpec((1, SWA_WIDTH), lambda i: (0, 0))
    return pl.pallas_call(
        body, name="swa_pre_bwd", grid=(s // tm,),
        in_specs=[blk] * (3 * npat) + [pl.BlockSpec((tm, SWA_WIDTH), lambda i: (i, base)),
                                      pl.BlockSpec((tm, SWA_WIDTH), lambda i: (i, base + 1)), vec, vec,
                                      pl.BlockSpec((SWA_WIDTH, SWA_WIDTH), lambda i: (0, 0))],
        out_specs=[pl.BlockSpec((tm, 3 * SWA_WIDTH), lambda i: (i, 0)), vec, vec],
        out_shape=[jax.ShapeDtypeStruct((s, 3 * SWA_WIDTH), F32), jax.ShapeDtypeStruct((1, SWA_WIDTH), F32),
                   jax.ShapeDtypeStruct((1, SWA_WIDTH), F32)],
        compiler_params=_params("arbitrary"),
    )(*dqs, *dks, *dvs, p_pad, p_pad, qw_row, kw_row, bd)


def _band_specs(length):
    per = ATT_BQ // ATT_HALO
    last = length // ATT_HALO - 1
    prev = pl.BlockSpec((ATT_HALO, SWA_WIDTH), lambda r, t: (jnp.maximum(t * per - 1, 0), r))
    cur = pl.BlockSpec((ATT_BQ, SWA_WIDTH), lambda r, t: (t, r))
    nxt = pl.BlockSpec((ATT_HALO, SWA_WIDTH), lambda r, t: (jnp.minimum((t + 1) * per, last), r))
    return [prev, cur, nxt]


def _tile_variant(t, nb):
    if nb == 1:
        return 3
    return jnp.where(t == 0, 0, jnp.where(t == nb - 1, 2, 1))


def _band(refs):
    return jnp.concatenate([r[...] for r in refs], axis=0)


def _att_fwd(q, k, v, bias, dilation):
    s = q.shape[0]
    length = s // dilation
    nb = length // ATT_BQ
    view = (length, dilation * SWA_WIDTH)
    hd = SWA_HEAD_DIM

    def body(q_ref, kp, kc, kn, vp, vc, vn, b_ref, o_ref, lse_ref):
        kb, vb = _band((kp, kc, kn)), _band((vp, vc, vn))
        qv = q_ref[...]
        for h in range(SWA_HEADS):
            sl = slice(h * hd, (h + 1) * hd)
            sc = _dot(qv[:, sl], kb[:, sl], NT) + b_ref[0, h]
            m = jnp.max(sc, axis=-1, keepdims=True)
            p = jnp.exp(sc - m)
            den = jnp.sum(p, axis=-1, keepdims=True)
            o_ref[:, sl] = _dot(p, vb[:, sl]) / den
            lse_ref[:, sl] = jnp.broadcast_to(m + jnp.log(den), (ATT_BQ, hd))

    cur = pl.BlockSpec((ATT_BQ, SWA_WIDTH), lambda r, t: (t, r))
    bspec = pl.BlockSpec((1, SWA_HEADS, ATT_BQ, ATT_BK), lambda r, t: (_tile_variant(t, nb), 0, 0, 0))
    o, lse = pl.pallas_call(
        body, name=f"att_fwd_d{dilation}", grid=(dilation, nb),
        in_specs=[cur] + _band_specs(length) * 2 + [bspec],
        out_specs=[cur, cur],
        out_shape=[jax.ShapeDtypeStruct(view, F32)] * 2,
        compiler_params=_params("parallel", "parallel"),
    )(q.reshape(view), *([k.reshape(view)] * 3), *([v.reshape(view)] * 3), bias)
    return o.reshape(s, SWA_WIDTH), lse.reshape(s, SWA_WIDTH)


def _att_dq(q, k, v, dop, lse, cp, bias, dilation):
    s = q.shape[0]
    length = s // dilation
    nb = length // ATT_BQ
    view = (length, dilation * SWA_WIDTH)
    hd = SWA_HEAD_DIM

    def body(q_ref, kp, kc, kn, vp, vc, vn, do_ref, lse_ref, cp_ref, b_ref, dq_ref, db_ref):
        @pl.when((pl.program_id(0) == 0) & (pl.program_id(1) == 0))
        def _():
            db_ref[...] = jnp.zeros_like(db_ref)

        var = _tile_variant(pl.program_id(1), nb)
        kb, vb = _band((kp, kc, kn)), _band((vp, vc, vn))
        qv, dov, lsev, cpv = q_ref[...], do_ref[...], lse_ref[...], cp_ref[...]
        for h in range(SWA_HEADS):
            sl = slice(h * hd, (h + 1) * hd)
            sc = _dot(qv[:, sl], kb[:, sl], NT) + b_ref[0, h]
            p = jnp.exp(sc - lsev[:, h * hd:h * hd + 1])
            dp = _dot(dov[:, sl], vb[:, sl], NT)
            ds = p * (dp + cpv[:, h * hd:h * hd + 1])
            dq_ref[:, sl] = _dot(ds, kb[:, sl])
            db_ref[var, h] += ds

    cur = pl.BlockSpec((ATT_BQ, SWA_WIDTH), lambda r, t: (t, r))
    bspec = pl.BlockSpec((1, SWA_HEADS, ATT_BQ, ATT_BK), lambda r, t: (_tile_variant(t, nb), 0, 0, 0))
    dq, db = pl.pallas_call(
        body, name=f"att_dq_d{dilation}", grid=(dilation, nb),
        in_specs=[cur] + _band_specs(length) * 2 + [cur, cur, cur, bspec],
        out_specs=[cur, pl.BlockSpec((4, SWA_HEADS, ATT_BQ, ATT_BK), lambda r, t: (0, 0, 0, 0))],
        out_shape=[jax.ShapeDtypeStruct(view, F32), jax.ShapeDtypeStruct((4, SWA_HEADS, ATT_BQ, ATT_BK), F32)],
        compiler_params=_params("arbitrary", "arbitrary"),
    )(q.reshape(view), *([k.reshape(view)] * 3), *([v.reshape(view)] * 3), dop.reshape(view), lse.reshape(view),
      cp.reshape(view), bias)
    return dq.reshape(s, SWA_WIDTH), db


def _att_dkv(q, k, v, dop, lse, cp, bias_t, dilation):
    s = q.shape[0]
    length = s // dilation
    nb = length // ATT_BQ
    view = (length, dilation * SWA_WIDTH)
    hd = SWA_HEAD_DIM

    def body(k_ref, v_ref, qp, qc, qn, dp_, dc_, dn_, lp, lc, ln, cp_, cc_, cn_, b_ref, dk_ref, dv_ref):
        qb, dob = _band((qp, qc, qn)), _band((dp_, dc_, dn_))
        lseb, cpb = _band((lp, lc, ln)), _band((cp_, cc_, cn_))
        kv, vv = k_ref[...], v_ref[...]
        for h in range(SWA_HEADS):
            sl = slice(h * hd, (h + 1) * hd)
            sc = _dot(qb[:, sl], kv[:, sl], NT) + b_ref[0, h]
            p = jnp.exp(sc - lseb[:, h * hd:h * hd + 1])
            dv_ref[:, sl] = _dot(p, dob[:, sl], TN)
            dp = _dot(dob[:, sl], vv[:, sl], NT)
            ds = p * (dp + cpb[:, h * hd:h * hd + 1])
            dk_ref[:, sl] = _dot(ds, qb[:, sl], TN)

    cur = pl.BlockSpec((ATT_BQ, SWA_WIDTH), lambda r, t: (t, r))
    bspec = pl.BlockSpec((1, SWA_HEADS, ATT_BK, ATT_BQ), lambda r, t: (_tile_variant(t, nb), 0, 0, 0))
    dk, dv = pl.pallas_call(
        body, name=f"att_dkv_d{dilation}", grid=(dilation, nb),
        in_specs=[cur, cur] + _band_specs(length) * 4 + [bspec],
        out_specs=[cur, cur],
        out_shape=[jax.ShapeDtypeStruct(view, F32)] * 2,
        compiler_params=_params("parallel", "parallel"),
    )(k.reshape(view), v.reshape(view), *([q.reshape(view)] * 3), *([dop.reshape(view)] * 3),
      *([lse.reshape(view)] * 3), *([cp.reshape(view)] * 3), bias_t)
    return dk.reshape(s, SWA_WIDTH), dv.reshape(s, SWA_WIDTH)


N_PAIRS = SWA_HEADS // 2


def _pairs(x):
    return jnp.stack([x[:, LANE * p:LANE * (p + 1)] for p in range(N_PAIRS)])


def _per_head_rows(x):
    first = lax.broadcasted_iota(jnp.int32, x.shape, 2) < SWA_HEAD_DIM
    zero = jnp.zeros_like(x)
    return jnp.concatenate([jnp.where(first, x, zero), jnp.where(first, zero, x)], axis=1)


def _per_head_cols(x):
    return jnp.stack([jnp.concatenate([x[:, LANE * p:LANE * p + 1],
                                       x[:, LANE * p + SWA_HEAD_DIM:LANE * p + SWA_HEAD_DIM + 1]], axis=0)
                      for p in range(N_PAIRS)])


def _merge_heads(x, rows):
    first = lax.broadcasted_iota(jnp.int32, (N_PAIRS, rows, LANE), 2) < SWA_HEAD_DIM
    return jnp.where(first, x[:, :rows], x[:, rows:])


def _store_pairs(ref, x):
    for p in range(N_PAIRS):
        ref[:, LANE * p:LANE * (p + 1)] = x[p].astype(ref.dtype)


def _att_fwd2(q, k, v, bias, dilation):
    s = q.shape[0]
    length = s // dilation
    nb = length // ATT_BQ
    view = (length, dilation * SWA_WIDTH)

    def body(q_ref, kp, kc, kn, vp, vc, vn, b_ref, o_ref, lse_ref):
        kb, vb = _pairs(_band((kp, kc, kn))), _pairs(_band((vp, vc, vn)))
        qm = _per_head_rows(_pairs(q_ref[...]))
        sc = _bdot(qm, kb, BNT) + b_ref[0].reshape(N_PAIRS, 2 * ATT_BQ, ATT_BK)
        m = jnp.max(sc, axis=-1, keepdims=True)
        p = jnp.exp(sc - m)
        den = jnp.sum(p, axis=-1, keepdims=True)
        o = _bdot(p, vb) / den
        _store_pairs(o_ref, _merge_heads(o, ATT_BQ))
        lse = jnp.broadcast_to(m + jnp.log(den), (N_PAIRS, 2 * ATT_BQ, LANE))
        _store_pairs(lse_ref, _merge_heads(lse, ATT_BQ))

    cur = pl.BlockSpec((ATT_BQ, SWA_WIDTH), lambda r, t: (t, r))
    bspec = pl.BlockSpec((1, SWA_HEADS, ATT_BQ, ATT_BK), lambda r, t: (_tile_variant(t, nb), 0, 0, 0))
    o, lse = pl.pallas_call(
        body, name=f"att_fwd_d{dilation}", grid=(dilation, nb),
        in_specs=[cur] + _band_specs(length) * 2 + [bspec],
        out_specs=[cur, cur],
        out_shape=[jax.ShapeDtypeStruct(view, BF16), jax.ShapeDtypeStruct(view, F32)],
        compiler_params=_params("parallel", "parallel"),
    )(q.reshape(view), *([k.reshape(view)] * 3), *([v.reshape(view)] * 3), bias)
    return o.reshape(s, SWA_WIDTH), lse.reshape(s, SWA_WIDTH)


def _att_dq2(q, k, v, dop, lse, cp, bias, dilation):
    s = q.shape[0]
    length = s // dilation
    nb = length // ATT_BQ
    view = (length, dilation * SWA_WIDTH)

    def body(q_ref, kp, kc, kn, vp, vc, vn, do_ref, lse_ref, cp_ref, b_ref, dq_ref, db_ref):
        @pl.when((pl.program_id(0) == 0) & (pl.program_id(1) == 0))
        def _():
            db_ref[...] = jnp.zeros_like(db_ref)

        var = _tile_variant(pl.program_id(1), nb)
        kb, vb = _pairs(_band((kp, kc, kn))), _pairs(_band((vp, vc, vn)))
        qm = _per_head_rows(_pairs(q_ref[...]))
        dom = _per_head_rows(_pairs(do_ref[...]))
        sc = _bdot(qm, kb, BNT) + b_ref[0].reshape(N_PAIRS, 2 * ATT_BQ, ATT_BK)
        p = jnp.exp(sc - _per_head_cols(lse_ref[...]))
        ds = p * (_bdot(dom, vb, BNT) + _per_head_cols(cp_ref[...]))
        _store_pairs(dq_ref, _merge_heads(_bdot(ds, kb), ATT_BQ))
        db_ref[var] += ds.reshape(SWA_HEADS, ATT_BQ, ATT_BK)

    cur = pl.BlockSpec((ATT_BQ, SWA_WIDTH), lambda r, t: (t, r))
    bspec = pl.BlockSpec((1, SWA_HEADS, ATT_BQ, ATT_BK), lambda r, t: (_tile_variant(t, nb), 0, 0, 0))
    dq, db = pl.pallas_call(
        body, name=f"att_dq_d{dilation}", grid=(dilation, nb),
        in_specs=[cur] + _band_specs(length) * 2 + [cur, cur, cur, bspec],
        out_specs=[cur, pl.BlockSpec((4, SWA_HEADS, ATT_BQ, ATT_BK), lambda r, t: (0, 0, 0, 0))],
        out_shape=[jax.ShapeDtypeStruct(view, BF16), jax.ShapeDtypeStruct((4, SWA_HEADS, ATT_BQ, ATT_BK), F32)],
        compiler_params=_params("arbitrary", "arbitrary"),
    )(q.reshape(view), *([k.reshape(view)] * 3), *([v.reshape(view)] * 3), dop.reshape(view), lse.reshape(view),
      cp.reshape(view), bias)
    return dq.reshape(s, SWA_WIDTH), db


def _att_dkv2(q, k, v, dop, lse, cp, bias_t, dilation):
    s = q.shape[0]
    length = s // dilation
    nb = length // ATT_BQ
    view = (length, dilation * SWA_WIDTH)

    def body(k_ref, v_ref, qp, qc, qn, dp_, dc_, dn_, lp, lc, ln, cp_, cc_, cn_, b_ref, dk_ref, dv_ref):
        qm = _per_head_rows(_pairs(_band((qp, qc, qn))))
        dom = _per_head_rows(_pairs(_band((dp_, dc_, dn_))))
        lsev = _per_head_cols(_band((lp, lc, ln)))
        cpv = _per_head_cols(_band((cp_, cc_, cn_)))
        kv, vv = _pairs(k_ref[...]), _pairs(v_ref[...])
        sc = _bdot(qm, kv, BNT) + b_ref[0].reshape(N_PAIRS, 2 * ATT_BK, ATT_BQ)
        p = jnp.exp(sc - lsev)
        _store_pairs(dv_ref, _bdot(p, dom, BTN))
        ds = p * (_bdot(dom, vv, BNT) + cpv)
        _store_pairs(dk_ref, _bdot(ds, qm, BTN))

    cur = pl.BlockSpec((ATT_BQ, SWA_WIDTH), lambda r, t: (t, r))
    bspec = pl.BlockSpec((1, SWA_HEADS, ATT_BK, ATT_BQ), lambda r, t: (_tile_variant(t, nb), 0, 0, 0))
    dk, dv = pl.pallas_call(
        body, name=f"att_dkv_d{dilation}", grid=(dilation, nb),
        in_specs=[cur, cur] + _band_specs(length) * 4 + [bspec],
        out_specs=[cur, cur],
        out_shape=[jax.ShapeDtypeStruct(view, BF16)] * 2,
        compiler_params=_params("parallel", "parallel"),
    )(k.reshape(view), v.reshape(view), *([q.reshape(view)] * 3), *([dop.reshape(view)] * 3),
      *([lse.reshape(view)] * 3), *([cp.reshape(view)] * 3), bias_t)
    return dk.reshape(s, SWA_WIDTH), dv.reshape(s, SWA_WIDTH)


def _pattern_weights(lses):
    m = lses[0]
    for l in lses[1:]:
        m = jnp.maximum(m, l)
    es = [jnp.exp(l - m) for l in lses]
    den = es[0]
    for e in es[1:]:
        den = den + e
    return [e / den for e in es]


def _combine_fwd(outs, lses):
    s = outs[0].shape[0]
    tm = min(512, s)
    npat = len(outs)

    def body(*refs):
        ws = _pattern_weights([r[...] for r in refs[npat:2 * npat]])
        o = ws[0] * refs[0][...]
        for p in range(1, npat):
            o = o + ws[p] * refs[p][...]
        refs[2 * npat][...] = o.astype(BF16)

    blk = pl.BlockSpec((tm, SWA_WIDTH), lambda i: (i, 0))
    return pl.pallas_call(
        body, name="swa_combine_fwd", grid=(s // tm,), in_specs=[blk] * (2 * npat), out_specs=blk,
        out_shape=jax.ShapeDtypeStruct((s, SWA_WIDTH), BF16), compiler_params=_params("parallel"),
    )(*outs, *lses)


def _combine_bwd(d_out, outs, lses, bd):
    s = d_out.shape[0]
    tm = min(512, s)
    npat = len(outs)

    def body(*refs):
        d_ref, bd_ref = refs[0], refs[1 + 2 * npat]
        o_refs, l_refs = refs[1:1 + npat], refs[1 + npat:1 + 2 * npat]
        out_refs = refs[2 + 2 * npat:]
        ws = _pattern_weights([r[...] for r in l_refs])
        dov = d_ref[...]
        o = ws[0] * o_refs[0][...]
        for p in range(1, npat):
            o = o + ws[p] * o_refs[p][...]
        rd = _group_sum(dov * o, bd_ref[...])
        for p in range(npat):
            out_refs[p][...] = (ws[p] * dov).astype(BF16)
            out_refs[npat + p][...] = -ws[p] * rd

    blk = pl.BlockSpec((tm, SWA_WIDTH), lambda i: (i, 0))
    res = pl.pallas_call(
        body, name="swa_combine_bwd", grid=(s // tm,),
        in_specs=[blk] * (1 + 2 * npat) + [pl.BlockSpec((SWA_WIDTH, SWA_WIDTH), lambda i: (0, 0))],
        out_specs=[blk] * (2 * npat),
        out_shape=[jax.ShapeDtypeStruct((s, SWA_WIDTH), BF16)] * npat + [jax.ShapeDtypeStruct((s, SWA_WIDTH), F32)] * npat,
        compiler_params=_params("parallel"),
    )(d_out, *outs, *lses, bd)
    return res[:npat], res[npat:]


def _rel_bias_grad(dbs, buckets):
    npat = len(dbs)

    def body(*refs):
        db_refs, bk_refs, o_ref = refs[:npat], refs[npat:2 * npat], refs[2 * npat]
        row = lax.broadcasted_iota(jnp.int32, (REL_BUCKETS, LANE), 0)
        lane = lax.broadcasted_iota(jnp.int32, (REL_BUCKETS, LANE), 1)
        tiles = [[db_refs[p][0, h] + db_refs[p][1, h] + db_refs[p][2, h] + db_refs[p][3, h] for h in range(SWA_HEADS)]
                 for p in range(npat)]
        bks = [r[...] for r in bk_refs]

        def one_bucket(b, acc):
            for h in range(SWA_HEADS):
                tot = jnp.zeros((1, 1), F32)
                for p in range(npat):
                    sel = jnp.where(bks[p] == b, tiles[p][h], 0.0)
                    tot = tot + jnp.sum(jnp.sum(sel, axis=1, keepdims=True), axis=0, keepdims=True)
                acc = acc + jnp.where((row == b) & (lane == h), tot, 0.0)
            return acc

        o_ref[...] = lax.fori_loop(0, REL_BUCKETS, one_bucket, jnp.zeros((REL_BUCKETS, LANE), F32))

    full4 = pl.BlockSpec((4, SWA_HEADS, ATT_BQ, ATT_BK), lambda: (0, 0, 0, 0))
    full2 = pl.BlockSpec((ATT_BQ, ATT_BK), lambda: (0, 0))
    return pl.pallas_call(
        body, name="rel_bias_grad", in_specs=[full4] * npat + [full2] * npat,
        out_specs=pl.BlockSpec((REL_BUCKETS, LANE), lambda: (0, 0)),
        out_shape=jax.ShapeDtypeStruct((REL_BUCKETS, LANE), F32),
        compiler_params=pltpu.CompilerParams(vmem_limit_bytes=V7X_VMEM_LIMIT_BYTES),
    )(*dbs, *buckets)


def _swa_forward(p_pad, qw_row, kw_row, rel_bias, bd):
    q, k, v = _swa_pre_fwd(p_pad, qw_row, kw_row, bd)
    outs, lses = [], []
    for _, dil in DILATION_PATTERNS:
        o, lse = _att_fwd2(q, k, v, _bias_tiles(rel_bias, dil, True), dil)
        outs.append(o)
        lses.append(lse)
    return _combine_fwd(outs, lses), (q, k, v, outs, lses)


def _swa_backward(d_out, p_pad, qw_row, kw_row, rel_bias, bd, saved):
    q, k, v, outs, lses = saved
    dops, cps = _combine_bwd(d_out, outs, lses, bd)
    dqs, dks, dvs, dbs, buckets = [], [], [], [], []
    for p, (_, dil) in enumerate(DILATION_PATTERNS):
        dq, db = _att_dq2(q, k, v, dops[p], lses[p], cps[p], _bias_tiles(rel_bias, dil, True), dil)
        dk, dv = _att_dkv2(q, k, v, dops[p], lses[p], cps[p], _bias_tiles(rel_bias, dil, False), dil)
        dqs.append(dq)
        dks.append(dk)
        dvs.append(dv)
        dbs.append(db)
        buckets.append(jnp.asarray(_band_tables(dil, True)[1]))
    dp, dqw, dkw = _swa_pre_bwd(dqs, dks, dvs, p_pad, qw_row, kw_row, bd)
    return dp, dqw, dkw, _rel_bias_grad(dbs, buckets)


def _lane_row(v):
    flat = v.reshape(-1).astype(F32)
    return jnp.zeros((1, LANE), F32).at[0, :flat.shape[0]].set(flat)


W_IN_SHARD = N_IN // N_DEV
W_IN_RUNS = ((0, QKV_A, 0), (QKV_A, OFF_B, QKV_A), (OFF_B, OFF_B + 16, OFF_AB), (OFF_B + 16, N_IN, OFF_B))
W_IN_SEGMENTS = ((0, QKV_A), (OFF_Z, GDN_WIDTH), (OFF_B, 3 * SWA_WIDTH), (OFF_AB, LANE))


def _w_in_pieces(shard):
    lo, hi = shard * W_IN_SHARD, (shard + 1) * W_IN_SHARD
    out = []
    for first, last, dst in W_IN_RUNS:
        a, b = max(lo, first), min(hi, last)
        if a < b:
            out.append((a - lo, b - a, dst + a - first))
    return out


def _cols_from_slabs(w3, name):
    nd, r, wd = w3.shape
    half = nd // 2

    def body(w_ref, o_ref):
        for sh in range(half):
            o_ref[:, wd * sh:wd * (sh + 1)] = w_ref[sh]

    return pl.pallas_call(
        body, name=name, grid=(2,), in_specs=[pl.BlockSpec((half, r, wd), lambda j: (j, 0, 0))],
        out_specs=pl.BlockSpec((r, half * wd), lambda j: (0, j)),
        out_shape=jax.ShapeDtypeStruct((r, nd * wd), w3.dtype), compiler_params=_params("parallel"),
    )(w3)


def _w_in_from_slabs(w3):
    nd, r, _ = w3.shape

    def body(w_ref, o_ref):
        o_ref[:, OFF_AB:N_PAD] = jnp.zeros((r, N_PAD - OFF_AB), w3.dtype)
        for sh in range(nd):
            for src, length, dst in _w_in_pieces(sh):
                o_ref[:, dst:dst + length] = w_ref[sh, :, src:src + length]

    return pl.pallas_call(
        body, name="w_in_from_slabs", out_shape=jax.ShapeDtypeStruct((r, N_PAD), w3.dtype),
        compiler_params=pltpu.CompilerParams(vmem_limit_bytes=V7X_VMEM_LIMIT_BYTES),
    )(w3)


def _w_in_grad_slabs(parts, dtype):
    r = parts[0].shape[0]

    def body(*refs):
        o_ref = refs[len(parts)]
        for sh in range(N_DEV):
            for src, length, dst in _w_in_pieces(sh):
                seg = next(i for i, (off, width) in enumerate(W_IN_SEGMENTS) if off <= dst < off + width)
                at = dst - W_IN_SEGMENTS[seg][0]
                o_ref[sh, :, src:src + length] = refs[seg][:, at:at + length].astype(dtype)

    return pl.pallas_call(
        body, name="w_in_grad_slabs", out_shape=jax.ShapeDtypeStruct((N_DEV, r, W_IN_SHARD), dtype),
        compiler_params=pltpu.CompilerParams(vmem_limit_bytes=V7X_VMEM_LIMIT_BYTES),
    )(*parts)


LATE = ("w_out", "ffn2_w_gate", "ffn2_w_up", "ffn2_w_down")
TRANSPOSED = ("ffn1_w_gate", "ffn1_w_up", "ffn2_w_gate", "ffn2_w_up")


def _late_weights(slabs):
    return {n: g.reshape(N_DEV * g.shape[1], g.shape[2]) for n, g in zip(LATE, slabs)}


def _local_step(x, tgt, wts, small, late_shards=None):
    bd = _head_block_diag()
    conv_wt = jnp.zeros((8, QKV_A), F32).at[:CONV_WIDTH].set(small["conv_w"].T)
    alog_row, dt_row = _lane_row(small["a_log"]), _lane_row(small["dt_bias"])
    gnorm_row = small["gdn_norm_w"].reshape(1, GDN_HEAD_DIM)
    qw_row = jnp.tile(small["q_norm_w"].reshape(-1), SWA_HEADS).reshape(1, SWA_WIDTH)
    kw_row = jnp.tile(small["k_norm_w"].reshape(-1), SWA_HEADS).reshape(1, SWA_WIDTH)
    rel_bias = small["rel_bias"]
    exchange = late_shards is not None
    dw_dtype = BF16 if exchange else F32

    x1, sv1, wd1, got = _ffn_forward(
        x, small["ffn1_norm"], wts["ffn1_w_gate"], wts["ffn1_w_up"], wts.get("ffn1_w_down"), "ffn1",
        gather=[late_shards["ffn1_w_down"], late_shards["w_in"]] if exchange else ())
    win_pad = _w_in_from_slabs(got[0]) if exchange else wts["w_in_pad"]
    n2, r2 = _rms_fwd(x1, small["mix_norm"], "mix_norm")
    p_pad = _matmul([(n2, win_pad)], tm=256, tn=N_PAD, tk=D_MODEL, name="w_in")
    o_a, sva, gathered = _gdn_forward(p_pad, conv_wt, alog_row, dt_row, gnorm_row,
                                      gather=[late_shards[n] for n in LATE] if exchange else ())
    if exchange:
        wts = {**wts, **_late_weights(gathered)}
    wo_a, wo_b = wts["w_out"][:GDN_WIDTH], wts["w_out"][GDN_WIDTH:]
    o_b, svb = _swa_forward(p_pad, qw_row, kw_row, rel_bias, bd)
    x2 = _matmul([(o_a, wo_a), (o_b, wo_b)], tm=512, tn=D_MODEL, tk=GDN_WIDTH, name="w_out", res=x1)
    x3, sv2, _, _ = _ffn_forward(x2, small["ffn2_norm"], wts["ffn2_w_gate"], wts["ffn2_w_up"], wts["ffn2_w_down"], "ffn2")
    loss_row, dx3, d_final = _final_loss(x3, small["final_norm"], tgt)

    dx2, d_ffn2_norm, dwg2, dwu2, dwd2, _ = _ffn_backward(
        dx3, x2, small["ffn2_norm"], wts["ffn2_w_gate"], wts["ffn2_w_up"], wts["ffn2_w_down"], sv2, "ffn2", dw_dtype)
    d_oa = _matmul([(dx2, wo_a)], tb=True, tm=512, tn=GDN_WIDTH, tk=D_MODEL, name="w_out_da")
    d_ob = _matmul([(dx2, wo_b)], tb=True, tm=512, tn=SWA_WIDTH, tk=D_MODEL, name="w_out_db")
    dwo_a = _matmul([(o_a, dx2)], ta=True, tm=GDN_WIDTH, tn=D_MODEL, tk=2048, name="w_out_dwa", out_dtype=dw_dtype)
    dwo_b = _matmul([(o_b, dx2)], ta=True, tm=SWA_WIDTH, tn=D_MODEL, tk=2048, name="w_out_dwb", out_dtype=dw_dtype)

    late_grads = [_row_slabs(jnp.concatenate([dwo_a, dwo_b], axis=0)), dwg2, dwu2, dwd2]
    dp_qkv, dz, dp_ab, dconv, gate_sums, d_gnorm, received = _gdn_backward(
        d_oa, p_pad, conv_wt, alog_row, dt_row, gnorm_row, sva, scatter=late_grads if exchange else ())
    if exchange:
        late_grads = received
    dp_b, dqw, dkw, d_rel = _swa_backward(d_ob, p_pad, qw_row, kw_row, rel_bias, bd, svb)
    segs = [(dp_qkv, 0, QKV_A), (dz, OFF_Z, GDN_WIDTH), (dp_b, OFF_B, 3 * SWA_WIDTH), (dp_ab, OFF_AB, LANE)]
    dn2 = None
    dwin_parts = []
    for i, (dseg, off, width) in enumerate(segs):
        dwin_parts.append(_matmul([(n2, dseg)], ta=True, tm=512, tn=width, tk=2048, name=f"w_in_dw{i}"))
        dn2 = _matmul([(dseg, win_pad[:, off:off + width])], tb=True, tm=512, tn=D_MODEL, tk=width,
                      name=f"w_in_dn{i}", res=dn2)
    dx1, d_mix_norm = _rms_bwd(dn2, x1, r2, small["mix_norm"], dx2, "mix_dnorm")
    d_w_in = _w_in_grad_slabs(dwin_parts, dw_dtype)
    dx, d_ffn1_norm, dwg1, dwu1, dwd1, got = _ffn_backward(
        dx1, x, small["ffn1_norm"], wts["ffn1_w_gate"], wts["ffn1_w_up"], wd1, sv1, "ffn1", dw_dtype,
        scatter=[d_w_in] if exchange else None)
    if exchange:
        d_w_in = got[0]

    grads = {
        "ffn1_norm": d_ffn1_norm, "ffn1_w_gate": dwg1, "ffn1_w_up": dwu1, "ffn1_w_down": dwd1,
        "mix_norm": d_mix_norm, "w_in": d_w_in, "conv_w": dconv[:CONV_WIDTH].T,
        "a_log": gate_sums[0, :8].reshape(2, GDN_HEADS), "dt_bias": gate_sums[1, :8].reshape(2, GDN_HEADS),
        "gdn_norm_w": d_gnorm, "q_norm_w": dqw.reshape(SWA_HEADS, SWA_HEAD_DIM).sum(0, keepdims=True),
        "k_norm_w": dkw.reshape(SWA_HEADS, SWA_HEAD_DIM).sum(0, keepdims=True), "rel_bias": d_rel[:, :SWA_HEADS],
        "ffn2_norm": d_ffn2_norm, "final_norm": d_final, **dict(zip(LATE, late_grads)),
    }
    return loss_row, dx, grads


MESH_IDS = pl.DeviceIdType.MESH
ANY = pl.BlockSpec(memory_space=pl.ANY)


def _all_gather(v, name):
    m, n = v.shape

    def body(x_ref, out_ref, send_sems, recv_sems, local_sem):
        x, y, c = lax.axis_index("x"), lax.axis_index("y"), lax.axis_index("c")
        me, sibling = (x, y, c), (x, y, 1 - c)
        chips = [(1 - x, y), (x, 1 - y), (1 - x, 1 - y)]

        def rows(px, py, pc):
            return out_ref.at[pl.ds((4 * px + 2 * py + pc) * m, m), :]

        def copy(k, block, to, src=None):
            return pltpu.make_async_remote_copy(
                src_ref=rows(*block) if src is None else src, dst_ref=rows(*block),
                send_sem=send_sems.at[k], recv_sem=recv_sems.at[k], device_id=to, device_id_type=MESH_IDS)

        mine = pltpu.make_async_copy(x_ref, rows(*me), local_sem)
        mine.start()
        first = [copy(0, me, sibling, src=x_ref)]
        first += [copy(1 + j, me, (*chip, c), src=x_ref) for j, chip in enumerate(chips)]
        for cp in first:
            cp.start()
        passed = [copy(4 + j, (*chip, c), sibling) for j, chip in enumerate(chips)]
        for j, chip in enumerate(chips):
            copy(1 + j, (*chip, c), me).wait_recv()
            passed[j].start()
        copy(0, sibling, me).wait_recv()
        for j, chip in enumerate(chips):
            copy(4 + j, (*chip, 1 - c), me).wait_recv()
        for cp in first + passed:
            cp.wait_send()
        mine.wait()

    return pl.pallas_call(
        body, name=name, in_specs=[ANY], out_specs=ANY,
        out_shape=jax.ShapeDtypeStruct((N_DEV * m, n), v.dtype),
        scratch_shapes=[pltpu.SemaphoreType.DMA((7,)), pltpu.SemaphoreType.DMA((7,)), pltpu.SemaphoreType.DMA],
        compiler_params=pltpu.CompilerParams(vmem_limit_bytes=V7X_VMEM_LIMIT_BYTES),
    )(v)


def _sibling_swap(v, name):
    def body(v_ref, out_ref, send_sem, recv_sem):
        x, y, c = lax.axis_index("x"), lax.axis_index("y"), lax.axis_index("c")
        cp = pltpu.make_async_remote_copy(src_ref=v_ref, dst_ref=out_ref, send_sem=send_sem, recv_sem=recv_sem,
                                          device_id=(x, y, 1 - c), device_id_type=MESH_IDS)
        cp.start()
        cp.wait()

    return pl.pallas_call(
        body, name=name, in_specs=[ANY], out_specs=ANY, out_shape=jax.ShapeDtypeStruct(v.shape, v.dtype),
        scratch_shapes=[pltpu.SemaphoreType.DMA, pltpu.SemaphoreType.DMA],
        compiler_params=pltpu.CompilerParams(vmem_limit_bytes=V7X_VMEM_LIMIT_BYTES),
    )(v)


def _chip_exchange(t, name):
    def body(t_ref, out_ref, send_sems, recv_sems, local_sem):
        x, y, c = lax.axis_index("x"), lax.axis_index("y"), lax.axis_index("c")
        mine = 2 * x + y
        chips = [(1 - x, y), (x, 1 - y), (1 - x, 1 - y)]
        own = pltpu.make_async_copy(t_ref.at[mine], out_ref.at[mine], local_sem)
        own.start()
        copies = [pltpu.make_async_remote_copy(
            src_ref=t_ref.at[2 * px + py], dst_ref=out_ref.at[mine], send_sem=send_sems.at[j], recv_sem=recv_sems.at[j],
            device_id=(px, py, c), device_id_type=MESH_IDS) for j, (px, py) in enumerate(chips)]
        for cp in copies:
            cp.start()
        for j, (px, py) in enumerate(chips):
            pltpu.make_async_remote_copy(
                src_ref=t_ref.at[mine], dst_ref=out_ref.at[2 * px + py], send_sem=send_sems.at[j],
                recv_sem=recv_sems.at[j], device_id=(px, py, c), device_id_type=MESH_IDS).wait_recv()
        for cp in copies:
            cp.wait_send()
        own.wait()

    return pl.pallas_call(
        body, name=name, in_specs=[ANY], out_specs=ANY, out_shape=jax.ShapeDtypeStruct(t.shape, t.dtype),
        scratch_shapes=[pltpu.SemaphoreType.DMA((3,)), pltpu.SemaphoreType.DMA((3,)), pltpu.SemaphoreType.DMA],
        compiler_params=pltpu.CompilerParams(vmem_limit_bytes=V7X_VMEM_LIMIT_BYTES),
    )(t)


def _adamw(parts, w, m, v, name):
    nparts, r, n = parts.shape
    tr = r
    for cand in (256, 176, 128, 104, 64, 8):
        if r % cand == 0:
            tr = cand
            break
    bc1 = 1.0 - ADAM_B1 ** ADAM_STEP
    bc2 = 1.0 - ADAM_B2 ** ADAM_STEP

    def body(p_ref, w_ref, m_ref, v_ref, g_ref, d_ref, nm_ref, nv_ref):
        g = p_ref[0].astype(F32)
        for k in range(1, nparts):
            g = g + p_ref[k].astype(F32)
        mn = ADAM_B1 * m_ref[...] + (1.0 - ADAM_B1) * g
        vn = ADAM_B2 * v_ref[...] + (1.0 - ADAM_B2) * (g * g)
        m_hat = mn / bc1
        v_hat = vn / bc2
        g_ref[...] = g
        nm_ref[...] = mn
        nv_ref[...] = vn
        d_ref[...] = -ADAM_LR * (m_hat / (jnp.sqrt(v_hat) + ADAM_EPS) + ADAM_WD * w_ref[...])

    blk = pl.BlockSpec((tr, n), lambda i: (i, 0))
    return pl.pallas_call(
        body, name=name, grid=(r // tr,),
        in_specs=[pl.BlockSpec((nparts, tr, n), lambda i: (0, i, 0)), blk, blk, blk],
        out_specs=[blk] * 4, out_shape=[jax.ShapeDtypeStruct((r, n), F32)] * 4,
        compiler_params=_params("parallel"),
    )(parts, w, m, v)


def _mesh_place():
    x, y, c = lax.axis_index("x"), lax.axis_index("y"), lax.axis_index("c")
    return x, y, c, [(1 - x, y), (x, 1 - y), (1 - x, 1 - y)]


def _gather_phases(x_refs, out_refs, send_sems, recv_sems, local_sems):
    na = len(x_refs)

    def place():
        x, y, c, chips = _mesh_place()
        return (x, y, c), (x, y, 1 - c), chips, c

    def slab(i, px, py, pc):
        return out_refs[i].at[4 * px + 2 * py + pc]

    def copy(i, k, block, to, src=None):
        return pltpu.make_async_remote_copy(
            src_ref=slab(i, *block) if src is None else src, dst_ref=slab(i, *block),
            send_sem=send_sems.at[i, k], recv_sem=recv_sems.at[i, k], device_id=to, device_id_type=MESH_IDS)

    def own(i, me):
        return pltpu.make_async_copy(x_refs[i], slab(i, *me), local_sems.at[i])

    def sends(i, me, sibling, chips, c):
        return [copy(i, 0, me, sibling, src=x_refs[i])] + [copy(i, 1 + j, me, (*chip, c), src=x_refs[i])
                                                          for j, chip in enumerate(chips)]

    def start():
        me, sibling, chips, c = place()
        for i in range(na):
            own(i, me).start()
            for cp in sends(i, me, sibling, chips, c):
                cp.start()

    def forward():
        me, sibling, chips, c = place()
        for j, chip in enumerate(chips):
            for i in range(na):
                copy(i, 1 + j, (*chip, c), me).wait_recv()
                copy(i, 4 + j, (*chip, c), sibling).start()

    def finish():
        me, sibling, chips, c = place()
        for i in range(na):
            copy(i, 0, sibling, me).wait_recv()
        for j, chip in enumerate(chips):
            for i in range(na):
                copy(i, 4 + j, (*chip, 1 - c), me).wait_recv()
        for i in range(na):
            for cp in sends(i, me, sibling, chips, c):
                cp.wait_send()
            for j, chip in enumerate(chips):
                copy(i, 4 + j, (*chip, c), sibling).wait_send()
            own(i, me).wait()

    return start, forward, finish


def _gather_semaphores(na):
    return [pltpu.SemaphoreType.DMA((na, 7)), pltpu.SemaphoreType.DMA((na, 7)), pltpu.SemaphoreType.DMA((na,))]


def _scatter_phases(g_refs, out_refs, send_sems, recv_sems, local_sems):
    na = len(g_refs)

    def place(m):
        x, y, c = lax.axis_index("x"), lax.axis_index("y"), lax.axis_index("c")
        px = 1 - x if m & 4 else x
        py = 1 - y if m & 2 else y
        pc = 1 - c if m & 1 else c
        return 4 * x + 2 * y + c, (px, py, pc), 4 * px + 2 * py + pc

    def own(i):
        me, _, _ = place(0)
        return pltpu.make_async_copy(g_refs[i].at[me], out_refs[i].at[me], local_sems.at[i])

    def start():
        for i in range(na):
            own(i).start()
            for m in range(1, N_DEV):
                me, peer, peer_idx = place(m)
                pltpu.make_async_remote_copy(
                    src_ref=g_refs[i].at[peer_idx], dst_ref=out_refs[i].at[me], send_sem=send_sems.at[i, m - 1],
                    recv_sem=recv_sems.at[i, m - 1], device_id=peer, device_id_type=MESH_IDS).start()

    def finish():
        for i in range(na):
            for m in range(1, N_DEV):
                me, peer, peer_idx = place(m)
                cp = pltpu.make_async_remote_copy(
                    src_ref=g_refs[i].at[peer_idx], dst_ref=out_refs[i].at[peer_idx], send_sem=send_sems.at[i, m - 1],
                    recv_sem=recv_sems.at[i, m - 1], device_id=peer, device_id_type=MESH_IDS)
                cp.wait_recv()
                cp.wait_send()
            own(i).wait()

    return start, finish


def _all_gather_many(vs, name):
    na = len(vs)

    def body(*refs):
        x_refs, out_refs = refs[:na], refs[na:2 * na]
        for step in _gather_phases(x_refs, out_refs, *refs[2 * na:]):
            step()

    return pl.pallas_call(
        body, name=name, in_specs=[ANY] * na, out_specs=[ANY] * na,
        out_shape=[jax.ShapeDtypeStruct((N_DEV,) + v.shape, v.dtype) for v in vs],
        scratch_shapes=_gather_semaphores(na),
        compiler_params=pltpu.CompilerParams(vmem_limit_bytes=V7X_VMEM_LIMIT_BYTES),
    )(*vs)


def _sibling_swap_many(gs, name):
    na = len(gs)

    def body(*refs):
        g_refs, out_refs = refs[:na], refs[na:2 * na]
        send_sems, recv_sems = refs[2 * na:]
        x, y, c, _ = _mesh_place()
        copies = [pltpu.make_async_remote_copy(
            src_ref=g_refs[i].at[2 * k + 1 - c], dst_ref=out_refs[i].at[k], send_sem=send_sems.at[i, k],
            recv_sem=recv_sems.at[i, k], device_id=(x, y, 1 - c), device_id_type=MESH_IDS)
            for i in range(na) for k in range(4)]
        for cp in copies:
            cp.start()
        for cp in copies:
            cp.wait()

    return pl.pallas_call(
        body, name=name, in_specs=[ANY] * na, out_specs=[ANY] * na,
        out_shape=[jax.ShapeDtypeStruct((4,) + g.shape[1:], g.dtype) for g in gs],
        scratch_shapes=[pltpu.SemaphoreType.DMA((na, 4)), pltpu.SemaphoreType.DMA((na, 4))],
        compiler_params=pltpu.CompilerParams(vmem_limit_bytes=V7X_VMEM_LIMIT_BYTES),
    )(*gs)


def _chip_sum(g, got, core, name):
    _, r, n = g.shape

    def body(c_ref, g_ref, got_ref, o_ref):
        o_ref[...] = (g_ref[...] + got_ref[...]).astype(BF16)

    return pl.pallas_call(
        body, name=name,
        grid_spec=pltpu.PrefetchScalarGridSpec(
            num_scalar_prefetch=1, grid=(4,),
            in_specs=[pl.BlockSpec((None, r, n), lambda k, c_ref: (2 * k + c_ref[0], 0, 0)),
                      pl.BlockSpec((None, r, n), lambda k, c_ref: (k, 0, 0))],
            out_specs=pl.BlockSpec((None, r, n), lambda k, c_ref: (k, 0, 0))),
        out_shape=jax.ShapeDtypeStruct((4, r, n), BF16), compiler_params=_params("parallel"),
    )(core, g, got)


def _chip_exchange_many(ts, name):
    na = len(ts)

    def body(*refs):
        t_refs, out_refs = refs[:na], refs[na:2 * na]
        send_sems, recv_sems, local_sems = refs[2 * na:]
        x, y, c, chips = _mesh_place()
        mine = 2 * x + y
        own = [pltpu.make_async_copy(t_refs[i].at[mine], out_refs[i].at[mine], local_sems.at[i]) for i in range(na)]
        for cp in own:
            cp.start()
        copies = [pltpu.make_async_remote_copy(
            src_ref=t_refs[i].at[2 * px + py], dst_ref=out_refs[i].at[mine], send_sem=send_sems.at[i, j],
            recv_sem=recv_sems.at[i, j], device_id=(px, py, c), device_id_type=MESH_IDS)
            for j, (px, py) in enumerate(chips) for i in range(na)]
        for cp in copies:
            cp.start()
        for j, (px, py) in enumerate(chips):
            for i in range(na):
                pltpu.make_async_remote_copy(
                    src_ref=t_refs[i].at[mine], dst_ref=out_refs[i].at[2 * px + py], send_sem=send_sems.at[i, j],
                    recv_sem=recv_sems.at[i, j], device_id=(px, py, c), device_id_type=MESH_IDS).wait_recv()
        for cp in copies:
            cp.wait_send()
        for cp in own:
            cp.wait()

    return pl.pallas_call(
        body, name=name, in_specs=[ANY] * na, out_specs=[ANY] * na,
        out_shape=[jax.ShapeDtypeStruct(t.shape, t.dtype) for t in ts],
        scratch_shapes=[pltpu.SemaphoreType.DMA((na, 3)), pltpu.SemaphoreType.DMA((na, 3)), pltpu.SemaphoreType.DMA((na,))],
        compiler_params=pltpu.CompilerParams(vmem_limit_bytes=V7X_VMEM_LIMIT_BYTES),
    )(*ts)


BIG = ("ffn1_w_gate", "ffn1_w_up", "ffn1_w_down", "w_in", "w_out", "ffn2_w_gate", "ffn2_w_up", "ffn2_w_down")
COL_SHARDED = ("ffn1_w_gate", "ffn1_w_up", "w_in", "ffn2_w_gate", "ffn2_w_up")
SMALL = ("ffn1_norm", "mix_norm", "a_log", "dt_bias", "gdn_norm_w", "q_norm_w", "k_norm_w", "rel_bias",
         "ffn2_norm", "final_norm")
WEIGHTS = ("ffn1_norm", "ffn1_w_gate", "ffn1_w_up", "ffn1_w_down", "mix_norm", "w_in", "conv_w", "a_log", "dt_bias",
           "gdn_norm_w", "q_norm_w", "k_norm_w", "rel_bias", "w_out", "ffn2_norm", "ffn2_w_gate", "ffn2_w_up",
           "ffn2_w_down", "final_norm")
PACK_WIDTH = 1024
PACK_ROW_MULTIPLE = 32


def _pack(arrays, width, row_multiple):
    flat = jnp.concatenate([a.reshape(-1) for a in arrays])
    rows = -(-flat.shape[0] // width)
    rows = -(-rows // row_multiple) * row_multiple
    return jnp.pad(flat, (0, rows * width - flat.shape[0])).reshape(rows, width)


def _unpack(packed, shapes):
    flat = packed.reshape(-1)
    out, pos = [], 0
    for shp in shapes:
        size = int(np.prod(shp))
        out.append(flat[pos:pos + size].reshape(shp))
        pos += size
    return out


def _blocks_of(name, full):
    if name in COL_SHARDED:
        rows, cols = full.shape
        return full.reshape(rows, N_DEV, cols // N_DEV).transpose(1, 0, 2).reshape(N_DEV, -1)
    return full.reshape(N_DEV, -1)


def _full_of(name, blocks, shard_shape):
    rows, cols = shard_shape
    if name in COL_SHARDED:
        return blocks.reshape(N_DEV, rows, cols).transpose(1, 0, 2).reshape(rows, N_DEV * cols)
    return blocks.reshape(N_DEV * rows, cols)


def kernel(x, ffn1_norm, ffn1_w_gate, ffn1_w_up, ffn1_w_down, mix_norm, w_in, conv_w, a_log, dt_bias, gdn_norm_w, q_norm_w, k_norm_w, rel_bias, w_out, ffn2_norm, ffn2_w_gate, ffn2_w_up, ffn2_w_down, final_norm, loss_target, m_ffn1_norm, m_ffn1_w_gate, m_ffn1_w_up, m_ffn1_w_down, m_mix_norm, m_w_in, m_conv_w, m_a_log, m_dt_bias, m_gdn_norm_w, m_q_norm_w, m_k_norm_w, m_rel_bias, m_w_out, m_ffn2_norm, m_ffn2_w_gate, m_ffn2_w_up, m_ffn2_w_down, m_final_norm, v_ffn1_norm, v_ffn1_w_gate, v_ffn1_w_up, v_ffn1_w_down, v_mix_norm, v_w_in, v_conv_w, v_a_log, v_dt_bias, v_gdn_norm_w, v_q_norm_w, v_k_norm_w, v_rel_bias, v_w_out, v_ffn2_norm, v_ffn2_w_gate, v_ffn2_w_up, v_ffn2_w_down, v_final_norm):
    w = dict(ffn1_norm=ffn1_norm, ffn1_w_gate=ffn1_w_gate, ffn1_w_up=ffn1_w_up, ffn1_w_down=ffn1_w_down, mix_norm=mix_norm, w_in=w_in, conv_w=conv_w, a_log=a_log, dt_bias=dt_bias, gdn_norm_w=gdn_norm_w, q_norm_w=q_norm_w, k_norm_w=k_norm_w, rel_bias=rel_bias, w_out=w_out, ffn2_norm=ffn2_norm, ffn2_w_gate=ffn2_w_gate, ffn2_w_up=ffn2_w_up, ffn2_w_down=ffn2_w_down, final_norm=final_norm)
    mom = dict(ffn1_norm=m_ffn1_norm, ffn1_w_gate=m_ffn1_w_gate, ffn1_w_up=m_ffn1_w_up, ffn1_w_down=m_ffn1_w_down, mix_norm=m_mix_norm, w_in=m_w_in, conv_w=m_conv_w, a_log=m_a_log, dt_bias=m_dt_bias, gdn_norm_w=m_gdn_norm_w, q_norm_w=m_q_norm_w, k_norm_w=m_k_norm_w, rel_bias=m_rel_bias, w_out=m_w_out, ffn2_norm=m_ffn2_norm, ffn2_w_gate=m_ffn2_w_gate, ffn2_w_up=m_ffn2_w_up, ffn2_w_down=m_ffn2_w_down, final_norm=m_final_norm)
    var = dict(ffn1_norm=v_ffn1_norm, ffn1_w_gate=v_ffn1_w_gate, ffn1_w_up=v_ffn1_w_up, ffn1_w_down=v_ffn1_w_down, mix_norm=v_mix_norm, w_in=v_w_in, conv_w=v_conv_w, a_log=v_a_log, dt_bias=v_dt_bias, gdn_norm_w=v_gdn_norm_w, q_norm_w=v_q_norm_w, k_norm_w=v_k_norm_w, rel_bias=v_rel_bias, w_out=v_w_out, ffn2_norm=v_ffn2_norm, ffn2_w_gate=v_ffn2_w_gate, ffn2_w_up=v_ffn2_w_up, ffn2_w_down=v_ffn2_w_down, final_norm=v_final_norm)
    ix, iy, ic = lax.axis_index("x"), lax.axis_index("y"), lax.axis_index("c")
    me = 4 * ix + 2 * iy + ic

    def local(a, n):
        return jnp.swapaxes(a[0], 0, 1) if n in TRANSPOSED else a[0]

    shard = {n: local(w[n], n) for n in BIG}

    conv_shard_shape = w["conv_w"][0].shape
    conv_elems = conv_shard_shape[0] * conv_shard_shape[1]
    first = ("ffn1_w_gate", "ffn1_w_up")
    gathered = _all_gather_many([shard[n].astype(BF16) for n in first] + [_pack([w["conv_w"][0]], LANE, 8)],
                                "gather_weights")
    wts = {n: g.reshape(N_DEV * g.shape[1], g.shape[2]) for n, g in zip(first, gathered)}

    small = {n: w[n][0] if n not in ("rel_bias",) else w[n] for n in SMALL}
    small = {n: (a.reshape(1, -1) if n.endswith("norm") else a) for n, a in small.items()}
    conv_all = gathered[-1].reshape(N_DEV, -1)
    small["conv_w"] = conv_all[:, :conv_elems].reshape(N_DEV * conv_shard_shape[0], conv_shard_shape[1])
    loss_row, grad_x, grads = _local_step(x[0], loss_target[0], wts, small,
                                          late_shards={n: shard[n].astype(BF16) for n in BIG if n not in first})
    loss = lax.psum(loss_row[0, 0], ("x", "y", "c"))

    big_out = [[], [], [], []]
    for n in BIG:
        for kind, val in enumerate(_adamw(grads[n], shard[n], local(mom[n], n), local(var[n], n), f"{n}_adamw")):
            big_out[kind].append(jnp.swapaxes(val, 0, 1) if n in TRANSPOSED else val)

    small_names = SMALL + ("conv_w",)
    small_shapes = [grads[n].shape for n in small_names]
    g_small = _pack([grads[n] for n in small_names], LANE, 8)
    small_rows = g_small.shape[0]
    all_small = _all_gather(g_small, "gather_small_grads").reshape(N_DEV, small_rows, LANE)
    rep_shapes = [grads[n].shape for n in SMALL]
    zero_conv = jnp.zeros(small_shapes[-1], F32)
    ws = _pack([w[n].reshape(grads[n].shape) for n in SMALL] + [zero_conv], LANE, 8)
    ms = _pack([mom[n].reshape(grads[n].shape) for n in SMALL] + [zero_conv], LANE, 8)
    vs = _pack([var[n].reshape(grads[n].shape) for n in SMALL] + [zero_conv], LANE, 8)
    small_out = [_unpack(a, small_shapes) for a in _adamw(all_small, ws, ms, vs, "adamw_small")]
    conv_g = lax.dynamic_slice_in_dim(small_out[0][-1], me * conv_shard_shape[0], conv_shard_shape[0], axis=0)
    conv_out = [_unpack(a, [conv_shard_shape])[0] for a in _adamw(
        _pack([conv_g], LANE, 8)[None], _pack([w["conv_w"][0]], LANE, 8), _pack([mom["conv_w"][0]], LANE, 8),
        _pack([var["conv_w"][0]], LANE, 8), "adamw_conv")]

    def leaf(kind, n):
        if n in BIG:
            val = big_out[kind][BIG.index(n)]
        elif n == "conv_w":
            val = conv_out[kind]
        else:
            val = small_out[kind][SMALL.index(n)]
        return val.reshape(w[n].shape)

    outs = [loss, grad_x[None]]
    for kind in range(4):
        outs += [leaf(kind, n) for n in WEIGHTS]
    return tuple(outs)
```

```python
import functools
import math

import numpy as np
import jax
import jax.numpy as jnp
from jax import lax
from jax.experimental import pallas as pl
from jax.experimental.pallas import tpu as pltpu

F32 = jnp.float32
BF16 = jnp.bfloat16

D_MODEL = 1024
D_FF = 2816
GDN_HEADS = 4
GDN_HEAD_DIM = 128
GDN_WIDTH = 512
CONV_WIDTH = 5
CHUNK = 64
SWA_HEADS = 8
SWA_HEAD_DIM = 64
SWA_WIDTH = 512
DILATION_PATTERNS = ((128, 1), (512, 4), (2048, 16))
REL_BUCKETS = 32
REL_MAX_DISTANCE = 1024
EPS = 1e-6
NEG_BIG = -1e30
N_DEV = 8

ADAM_LR = 0.001
ADAM_B1 = 0.9
ADAM_B2 = 0.999
ADAM_EPS = 1e-08
ADAM_WD = 0.01
ADAM_STEP = 10

QKV_A = 3 * GDN_WIDTH
OFF_B = QKV_A
OFF_Z = OFF_B + 3 * SWA_WIDTH
OFF_AB = OFF_Z + GDN_WIDTH
N_PAD = OFF_AB + 256
N_IN = 3600
NAT_Z, NAT_AB, NAT_B = QKV_A, QKV_A + GDN_WIDTH, QKV_A + GDN_WIDTH + 16

V7X_VMEM_LIMIT_BYTES = 56 * 1024 * 1024
LANE = 128
ATT_BQ = 128
ATT_HALO = 64
CONV_ROWS = 256

NN = (((1,), (0,)), ((), ()))
NT = (((1,), (1,)), ((), ()))
TN = (((0,), (0,)), ((), ()))


def _params(*sem):
    return pltpu.CompilerParams(dimension_semantics=sem, vmem_limit_bytes=V7X_VMEM_LIMIT_BYTES)


def _dot(a, b, dn=NN):
    return lax.dot_general(a.astype(BF16), b.astype(BF16), dn, preferred_element_type=F32)


def _dot_hi(a, b, dn=NN):
    return lax.dot_general(a, b, dn, precision=lax.Precision.HIGHEST, preferred_element_type=F32)


def _sigmoid(x):
    return 1.0 / (1.0 + jnp.exp(-x))


class _Exchange:
    def __init__(self, kind, arrays):
        self.kind, self.arrays = kind, list(arrays)

    def out_shape(self):
        lead = (N_DEV,) if self.kind == "gather" else ()
        return [jax.ShapeDtypeStruct(lead + v.shape, v.dtype) for v in self.arrays]

    def hooks(self, in_refs, out_refs, sems, grid):
        step = pl.program_id(0)
        for axis in range(1, len(grid)):
            step = step * grid[axis] + pl.program_id(axis)
        total = math.prod(grid)
        if self.kind == "gather":
            assert total >= 4
            start, forward, finish = _gather_phases(in_refs, out_refs, *sems)
            pl.when(step == total // 2)(forward)
        else:
            assert total >= 2
            start, finish = _scatter_phases(in_refs, out_refs, *sems)
        pl.when(step == 0)(start)
        pl.when(step == total - 1)(finish)


def _pallas(body, *, name, grid, in_specs, out_specs, out_shape, args, semantics, scratch_shapes=(), exchange=None):
    n_in, n_out, n_scr = len(in_specs), len(out_specs), len(scratch_shapes)
    if exchange is None:
        res = pl.pallas_call(
            body, name=name, grid=grid, in_specs=list(in_specs), out_specs=list(out_specs), out_shape=list(out_shape),
            scratch_shapes=list(scratch_shapes), compiler_params=_params(*semantics))(*args)
        return list(res), []
    na = len(exchange.arrays)

    def carrying(*refs):
        ins, sent = refs[:n_in], refs[n_in:n_in + na]
        outs = refs[n_in + na:n_in + na + n_out]
        landed = refs[n_in + na + n_out:n_in + 2 * na + n_out]
        rest = refs[n_in + 2 * na + n_out:]
        exchange.hooks(sent, landed, rest[n_scr:], grid)
        body(*ins, *outs, *rest[:n_scr])

    res = pl.pallas_call(
        carrying, name=name, grid=grid, in_specs=list(in_specs) + [ANY] * na, out_specs=list(out_specs) + [ANY] * na,
        out_shape=list(out_shape) + exchange.out_shape(), scratch_shapes=list(scratch_shapes) + _gather_semaphores(na),
        compiler_params=_params(*(["arbitrary"] * len(grid))))(*args, *exchange.arrays)
    return list(res[:n_out]), list(res[n_out:])


def _matmul(pairs, *, ta=False, tb=False, out_dtype=F32, tm, tn, tk, name, res=None, alpha=None, shard_cols=None,
            exchange=None):
    a0, b0 = pairs[0]
    m = a0.shape[1] if ta else a0.shape[0]
    k = a0.shape[0] if ta else a0.shape[1]
    n = b0.shape[0] if tb else b0.shape[1]
    tm, tn, tk = min(tm, m), min(tn, n), min(tk, k)
    assert m % tm == 0 and n % tn == 0 and k % tk == 0, (name, m, n, k, tm, tn, tk)
    nk = k // tk
    npairs = len(pairs)
    dn = (((0 if ta else 1,), (1 if tb else 0,)), ((), ()))

    def body(*refs):
        ins = refs[:2 * npairs]
        pos = 2 * npairs
        r_ref = None
        if res is not None:
            r_ref = refs[pos]
            pos += 1
        o_ref, acc = refs[pos], refs[pos + 1]
        kk = pl.program_id(2)
        t = None
        for p in range(npairs):
            d = _dot(ins[2 * p][...], ins[2 * p + 1][...], dn)
            t = d if t is None else t + d

        if nk > 1:
            @pl.when(kk == 0)
            def _():
                acc[...] = t

            @pl.when((kk > 0) & (kk < nk - 1))
            def _():
                acc[...] += t

        @pl.when(kk == nk - 1)
        def _():
            r = acc[...] + t if nk > 1 else t
            if alpha is not None:
                r = r * alpha
            if r_ref is not None:
                r = r_ref[...] + r
            if shard_cols is None:
                o_ref[...] = r.astype(out_dtype)
            else:
                for sh in range(tn // shard_cols):
                    o_ref[sh] = r[:, sh * shard_cols:(sh + 1) * shard_cols].astype(out_dtype)

    a_spec = pl.BlockSpec((tk, tm), lambda i, j, kk: (kk, i)) if ta else pl.BlockSpec((tm, tk), lambda i, j, kk: (i, kk))
    b_spec = pl.BlockSpec((tn, tk), lambda i, j, kk: (j, kk)) if tb else pl.BlockSpec((tk, tn), lambda i, j, kk: (kk, j))
    o_spec = pl.BlockSpec((tm, tn), lambda i, j, kk: (i, j))
    in_specs = [a_spec, b_spec] * npairs + ([o_spec] if res is not None else [])
    args = [t for pr in pairs for t in pr] + ([res] if res is not None else [])
    out_spec, out_shape = o_spec, (m, n)
    if shard_cols is not None:
        assert res is None and tn % shard_cols == 0
        out_spec = pl.BlockSpec((tn // shard_cols, tm, shard_cols), lambda i, j, kk: (j, i, 0))
        out_shape = (n // shard_cols, m, shard_cols)
    (out,), exchanged = _pallas(
        body, name=name, grid=(m // tm, n // tn, nk), in_specs=in_specs, out_specs=[out_spec],
        out_shape=[jax.ShapeDtypeStruct(out_shape, out_dtype)],
        scratch_shapes=[pltpu.VMEM((tm, tn) if nk > 1 else (8, LANE), F32)],
        semantics=("parallel", "parallel", "arbitrary"), args=args, exchange=exchange)
    return out if exchange is None else (out, exchanged)


def _rms_fwd(x, w, name):
    s, d = x.shape
    tm = min(512, s)

    def body(x_ref, w_ref, n_ref, r_ref):
        xv = x_ref[...]
        r = lax.rsqrt(jnp.mean(xv * xv, axis=-1, keepdims=True) + EPS)
        n_ref[...] = (xv * r * w_ref[...]).astype(BF16)
        r_ref[...] = r

    return pl.pallas_call(
        body, name=name, grid=(s // tm,),
        in_specs=[pl.BlockSpec((tm, d), lambda i: (i, 0)), pl.BlockSpec((1, d), lambda i: (0, 0))],
        out_specs=[pl.BlockSpec((tm, d), lambda i: (i, 0)), pl.BlockSpec((tm, 1), lambda i: (i, 0))],
        out_shape=[jax.ShapeDtypeStruct((s, d), BF16), jax.ShapeDtypeStruct((s, 1), F32)],
        compiler_params=_params("parallel"),
    )(x, w)


def _rms_bwd(dn, x, r, w, dres, name, exchange=None):
    s, d = x.shape
    tm = min(512, s)

    def body(dn_ref, x_ref, r_ref, w_ref, dres_ref, dx_ref, dw_ref):
        @pl.when(pl.program_id(0) == 0)
        def _():
            dw_ref[...] = jnp.zeros_like(dw_ref)

        rv = r_ref[...]
        xhat = x_ref[...] * rv
        g = dn_ref[...]
        t = g * w_ref[...]
        dx_ref[...] = dres_ref[...] + rv * (t - xhat * jnp.mean(t * xhat, axis=-1, keepdims=True))
        dw_ref[...] += jnp.sum(g * xhat, axis=0, keepdims=True)

    row = pl.BlockSpec((tm, d), lambda i: (i, 0))
    vec = pl.BlockSpec((1, d), lambda i: (0, 0))
    (dx, dw), exchanged = _pallas(
        body, name=name, grid=(s // tm,),
        in_specs=[row, row, pl.BlockSpec((tm, 1), lambda i: (i, 0)), vec, row],
        out_specs=[row, vec],
        out_shape=[jax.ShapeDtypeStruct((s, d), F32), jax.ShapeDtypeStruct((1, d), F32)],
        semantics=("arbitrary",), args=(dn, x, r, w, dres), exchange=exchange)
    return (dx, dw) if exchange is None else (dx, dw, exchanged)


def _final_loss(x3, wf, tgt):
    s, d = x3.shape
    tm = min(512, s)

    def body(x_ref, w_ref, t_ref, loss_ref, dx_ref, dw_ref):
        @pl.when(pl.program_id(0) == 0)
        def _():
            dw_ref[...] = jnp.zeros_like(dw_ref)
            loss_ref[...] = jnp.zeros_like(loss_ref)

        xv = x_ref[...]
        wv = w_ref[...]
        r = lax.rsqrt(jnp.mean(xv * xv, axis=-1, keepdims=True) + EPS)
        xhat = xv * r
        e = xhat * wv - t_ref[...]
        part = 0.5 * jnp.sum(jnp.mean(e * e, axis=-1, keepdims=True), axis=0, keepdims=True)
        loss_ref[...] += jnp.broadcast_to(part, loss_ref.shape)
        dy = e * (1.0 / d)
        dw_ref[...] += jnp.sum(dy * xhat, axis=0, keepdims=True)
        t = dy * wv
        dx_ref[...] = r * (t - xhat * jnp.mean(t * xhat, axis=-1, keepdims=True))

    row = pl.BlockSpec((tm, d), lambda i: (i, 0))
    vec = pl.BlockSpec((1, d), lambda i: (0, 0))
    return pl.pallas_call(
        body, name="final_loss", grid=(s // tm,),
        in_specs=[row, vec, row],
        out_specs=[pl.BlockSpec((1, LANE), lambda i: (0, 0)), row, vec],
        out_shape=[jax.ShapeDtypeStruct((1, LANE), F32), jax.ShapeDtypeStruct((s, d), F32),
                   jax.ShapeDtypeStruct((1, d), F32)],
        compiler_params=_params("arbitrary"),
    )(x3, wf, tgt)


def _ffn_up(n, wg, wu, name, exchange=None):
    s, d = n.shape
    f = wg.shape[0]
    tm, tn = min(512, s), f // 2

    def body(n_ref, wg_ref, wu_ref, g_ref, u_ref, a_ref):
        nv = n_ref[...]
        g = _dot(nv, wg_ref[...], NT)
        u = _dot(nv, wu_ref[...], NT)
        g_ref[...] = g.astype(BF16)
        u_ref[...] = u.astype(BF16)
        a_ref[...] = (g * _sigmoid(g) * u).astype(BF16)

    o = pl.BlockSpec((tm, tn), lambda j, i: (i, j))
    wspec = pl.BlockSpec((tn, d), lambda j, i: (j, 0))
    return _pallas(
        body, name=name, grid=(f // tn, s // tm),
        in_specs=[pl.BlockSpec((tm, d), lambda j, i: (i, 0)), wspec, wspec],
        out_specs=[o, o, o],
        out_shape=[jax.ShapeDtypeStruct((s, f), BF16)] * 3,
        semantics=("parallel", "parallel"), args=(n, wg, wu), exchange=exchange)


def _ffn_dact(dx, wd, g, u, name, exchange=None):
    s, d = dx.shape
    f = wd.shape[0]
    tm, tn = min(512, s), f // 2

    def body(dx_ref, wd_ref, g_ref, u_ref, dg_ref, du_ref):
        da = 0.5 * _dot(dx_ref[...], wd_ref[...], NT)
        gv = g_ref[...].astype(F32)
        sg = _sigmoid(gv)
        du_ref[...] = (da * gv * sg).astype(BF16)
        dg_ref[...] = (da * u_ref[...].astype(F32) * (sg * (1.0 + gv * (1.0 - sg)))).astype(BF16)

    o = pl.BlockSpec((tm, tn), lambda j, i: (i, j))
    return _pallas(
        body, name=name, grid=(f // tn, s // tm),
        in_specs=[pl.BlockSpec((tm, d), lambda j, i: (i, 0)), pl.BlockSpec((tn, d), lambda j, i: (j, 0)), o, o],
        out_specs=[o, o],
        out_shape=[jax.ShapeDtypeStruct((s, f), BF16), jax.ShapeDtypeStruct((s, f), BF16)],
        semantics=("parallel", "parallel"), args=(dx, wd, g, u), exchange=exchange)


def _row_slabs(full):
    return full.reshape(N_DEV, full.shape[0] // N_DEV, full.shape[1])


def _ffn_forward(x, norm_w, wg, wu, wd, tag, gather=()):
    n, r = _rms_fwd(x, norm_w, f"{tag}_norm")
    (g, u, a), got = _ffn_up(n, wg, wu, f"{tag}_up", _Exchange("gather", gather) if gather else None)
    if wd is None:
        wd, got = got[0].reshape(N_DEV * got[0].shape[1], got[0].shape[2]), got[1:]
    y = _matmul([(a, wd)], tm=512, tn=1024, tk=wd.shape[0], name=f"{tag}_down", res=x, alpha=0.5)
    return y, (n, r, g, u, a), wd, got


def _ffn_backward(dy, x, norm_w, wgt, wut, wd, saved, tag, dw_dtype=F32, scatter=None):
    n, r, g, u, a = saved

    def behind(arrays):
        return _Exchange("scatter", arrays) if scatter is not None else None

    def dw(act, grad, name, alpha=None, exchange=None):
        return _matmul([(act, grad)], ta=True, tm=1408, tn=1024, tk=2048, name=name, alpha=alpha, out_dtype=dw_dtype,
                       exchange=exchange)

    dwd = _row_slabs(dw(a, dy, f"{tag}_dwd", alpha=0.5))
    (dg, du), extras = _ffn_dact(dy, wd, g, u, f"{tag}_dact", behind(scatter))
    if scatter is None:
        dwg, dwu = _row_slabs(dw(dg, n, f"{tag}_dwg")), _row_slabs(dw(du, n, f"{tag}_dwu"))
    else:
        dwg, (dwd,) = dw(dg, n, f"{tag}_dwg", exchange=behind([dwd]))
        dwg, dwu = _row_slabs(dwg), _row_slabs(dw(du, n, f"{tag}_dwu"))
    dn = _matmul([(dg, wgt), (du, wut)], tm=512, tn=1024, tk=wgt.shape[0], name=f"{tag}_dn", exchange=behind([dwg, dwu]))
    if scatter is not None:
        dn, (dwg, dwu) = dn
    dx, dnorm = _rms_bwd(dn, x, r, norm_w, dy, f"{tag}_dnorm")
    return dx, dnorm, dwg, dwu, dwd, extras


Q_SCALE = GDN_HEAD_DIM ** -0.5
CONV_HALO = 8


def _conv_taps(win, w_ref, rows, sign):
    n = rows + 2 * CONV_HALO
    acc = None
    for t in range(CONV_WIDTH):
        o = sign * (t - CONV_WIDTH // 2)
        sh = win if o == 0 else pltpu.roll(win, (-o) % n, 0)
        term = sh[CONV_HALO:CONV_HALO + rows] * w_ref[t:t + 1, :]
        acc = term if acc is None else acc + term
    return acc


def _gdn_conv_fwd(p_pad, conv_wt):
    s = p_pad.shape[0]
    rows = min(CONV_ROWS, s)
    nblk = QKV_A // LANE

    def body(p_ref, w_ref, c_ref, y_ref, pad):
        j = pl.program_id(0)
        zeros = jnp.zeros((CONV_HALO, LANE), F32)
        pad[0:CONV_HALO, :] = zeros
        pad[CONV_HALO + s:2 * CONV_HALO + s, :] = zeros
        pad[CONV_HALO:CONV_HALO + s, :] = p_ref[...]

        def chunk(ci, carry):
            b = pl.multiple_of(ci * rows, rows)
            win = pad[pl.ds(b, rows + 2 * CONV_HALO), :]
            c = _conv_taps(win, w_ref, rows, 1)
            c_ref[pl.ds(b, rows), :] = c
            act = c * _sigmoid(c)
            nrm = lax.rsqrt(jnp.sum(act * act, axis=-1, keepdims=True) + EPS)
            mult = jnp.where(j < GDN_HEADS, nrm * Q_SCALE, jnp.where(j < 2 * GDN_HEADS, nrm, 1.0))
            y_ref[pl.ds(b, rows), :] = act * mult
            return carry

        lax.fori_loop(0, s // rows, chunk, 0)

    col = pl.BlockSpec((s, LANE), lambda j: (0, j))
    return pl.pallas_call(
        body, name="gdn_conv_fwd", grid=(nblk,),
        in_specs=[col, pl.BlockSpec((8, LANE), lambda j: (0, j))],
        out_specs=[col, col],
        out_shape=[jax.ShapeDtypeStruct((s, QKV_A), F32), jax.ShapeDtypeStruct((s, QKV_A), F32)],
        scratch_shapes=[pltpu.VMEM((s + 2 * CONV_HALO, LANE), F32)],
        compiler_params=_params("parallel"),
    )(p_pad, conv_wt)


def _gdn_conv_bwd(dy_f, dy_r, c_pre, p_pad, conv_wt, dp_all):
    s = p_pad.shape[0]
    rows = min(CONV_ROWS, s)
    nblk = QKV_A // LANE

    def body(dyf_ref, dyr_ref, c_ref, p_ref, w_ref, _, dp_ref, dw_ref, ppad, dcpad):
        j = pl.program_id(0)
        zeros = jnp.zeros((CONV_HALO, LANE), F32)
        for buf in (ppad, dcpad):
            buf[0:CONV_HALO, :] = zeros
            buf[CONV_HALO + s:2 * CONV_HALO + s, :] = zeros
        ppad[CONV_HALO:CONV_HALO + s, :] = p_ref[...]

        def act_bwd(ci, carry):
            b = pl.multiple_of(ci * rows, rows)
            c = c_ref[pl.ds(b, rows), :]
            g = dyf_ref[pl.ds(b, rows), :] + dyr_ref[pl.ds(b, rows), :]
            sg = _sigmoid(c)
            act = c * sg
            nrm = lax.rsqrt(jnp.sum(act * act, axis=-1, keepdims=True) + EPS)
            yh = act * nrm
            scale = jnp.where(j < GDN_HEADS, Q_SCALE, 1.0)
            dact_qk = (scale * nrm) * (g - yh * jnp.sum(g * yh, axis=-1, keepdims=True))
            dact = jnp.where(j < 2 * GDN_HEADS, dact_qk, g)
            dcpad[pl.ds(pl.multiple_of(b + CONV_HALO, CONV_HALO), rows), :] = dact * (sg * (1.0 + c * (1.0 - sg)))
            return carry

        lax.fori_loop(0, s // rows, act_bwd, 0)
        tap = lax.broadcasted_iota(jnp.int32, (8, LANE), 0)

        def taps_bwd(ci, dw):
            b = pl.multiple_of(ci * rows, rows)
            dcw = dcpad[pl.ds(b, rows + 2 * CONV_HALO), :]
            dp_ref[pl.ds(b, rows), :] = _conv_taps(dcw, w_ref, rows, -1).astype(BF16)
            pw = ppad[pl.ds(b, rows + 2 * CONV_HALO), :]
            dc = dcw[CONV_HALO:CONV_HALO + rows]
            n = rows + 2 * CONV_HALO
            for t in range(CONV_WIDTH):
                o = t - CONV_WIDTH // 2
                sh = pw if o == 0 else pltpu.roll(pw, (-o) % n, 0)
                row = jnp.sum(dc * sh[CONV_HALO:CONV_HALO + rows], axis=0, keepdims=True)
                dw = dw + jnp.where(tap == t, row, 0.0)
            return dw

        dw_ref[...] = lax.fori_loop(0, s // rows, taps_bwd, jnp.zeros((8, LANE), F32))

    col = pl.BlockSpec((s, LANE), lambda j: (0, j))
    wspec = pl.BlockSpec((8, LANE), lambda j: (0, j))
    return pl.pallas_call(
        body, name="gdn_conv_bwd", grid=(nblk,),
        in_specs=[col, col, col, col, wspec, ANY],
        out_specs=[col, wspec],
        out_shape=[jax.ShapeDtypeStruct(dp_all.shape, dp_all.dtype), jax.ShapeDtypeStruct((8, QKV_A), F32)],
        scratch_shapes=[pltpu.VMEM((s + 2 * CONV_HALO, LANE), F32), pltpu.VMEM((s + 2 * CONV_HALO, LANE), F32)],
        input_output_aliases={5: 0},
        compiler_params=_params("parallel"),
    )(dy_f, dy_r, c_pre, p_pad, conv_wt, dp_all)


def _softplus(x):
    return jnp.maximum(x, 0.0) + jnp.log(1.0 + jnp.exp(-jnp.abs(x)))


def _gdn_gates_fwd(p_pad, alog_row, dt_row):
    s = p_pad.shape[0]
    tm = min(1024, s)

    def body(p_ref, al_ref, dt_ref, o_ref):
        x = p_ref[...]
        lane = lax.broadcasted_iota(jnp.int32, x.shape, 1)
        g = -jnp.exp(al_ref[...]) * _softplus(x + dt_ref[...])
        o_ref[...] = jnp.where(lane < 8, g, jnp.where(lane < 16, _sigmoid(x), 0.0))

    vec = pl.BlockSpec((1, LANE), lambda i: (0, 0))
    return pl.pallas_call(
        body, name="gdn_gates_fwd", grid=(s // tm,),
        in_specs=[pl.BlockSpec((tm, LANE), lambda i: (i, OFF_AB // LANE)), vec, vec],
        out_specs=pl.BlockSpec((tm, LANE), lambda i: (i, 0)),
        out_shape=jax.ShapeDtypeStruct((s, LANE), F32),
        compiler_params=_params("parallel"),
    )(p_pad, alog_row, dt_row)


def _gdn_gates_bwd(dgb_f, dgb_r, p_pad, gb, alog_row, dt_row, dp_all):
    s = p_pad.shape[0]
    tm = min(1024, s)
    tail = N_PAD - OFF_AB

    def body(df_ref, dr_ref, p_ref, gb_ref, al_ref, dt_ref, _, dp_ref, sum_ref):
        @pl.when(pl.program_id(0) == 0)
        def _():
            sum_ref[...] = jnp.zeros_like(sum_ref)

        x = p_ref[...]
        gbv = gb_ref[...]
        dgb = df_ref[...] + dr_ref[...]
        lane = lax.broadcasted_iota(jnp.int32, x.shape, 1)
        da = dgb * (-jnp.exp(al_ref[...])) * _sigmoid(x + dt_ref[...])
        db = dgb * gbv * (1.0 - gbv)
        dp_ref[:, 0:LANE] = jnp.where(lane < 8, da, jnp.where(lane < 16, db, 0.0)).astype(BF16)
        dp_ref[:, LANE:tail] = jnp.zeros((tm, tail - LANE), BF16)
        row = lax.broadcasted_iota(jnp.int32, (8, LANE), 0)
        lane8 = lax.broadcasted_iota(jnp.int32, (8, LANE), 1)
        d_alog = jnp.sum(dgb * gbv, axis=0, keepdims=True)
        d_dt = jnp.sum(da, axis=0, keepdims=True)
        upd = jnp.where(row == 0, d_alog, jnp.where(row == 1, d_dt, 0.0))
        sum_ref[...] += jnp.where(lane8 < 8, upd, 0.0)

    vec = pl.BlockSpec((1, LANE), lambda i: (0, 0))
    blk = pl.BlockSpec((tm, LANE), lambda i: (i, 0))
    return pl.pallas_call(
        body, name="gdn_gates_bwd", grid=(s // tm,),
        in_specs=[blk, blk, pl.BlockSpec((tm, LANE), lambda i: (i, OFF_AB // LANE)), blk, vec, vec, ANY],
        out_specs=[pl.BlockSpec((tm, tail), lambda i: (i, OFF_AB // tail)), pl.BlockSpec((8, LANE), lambda i: (0, 0))],
        out_shape=[jax.ShapeDtypeStruct(dp_all.shape, dp_all.dtype), jax.ShapeDtypeStruct((8, LANE), F32)],
        input_output_aliases={6: 0},
        compiler_params=_params("arbitrary"),
    )(dgb_f, dgb_r, p_pad, gb, alog_row, dt_row, dp_all)


def _chunk_masks(rev):
    row = lax.broadcasted_iota(jnp.int32, (CHUNK, CHUNK), 0)
    col = lax.broadcasted_iota(jnp.int32, (CHUNK, CHUNK), 1)
    le = (col >= row) if rev else (col <= row)
    strict = (col > row) if rev else (col < row)
    return le, strict, row == col


def _chunk_common(q, k, v, g, beta, gc, masks):
    le, strict, eye = masks
    gc_row = _dot_hi(jnp.ones((CHUNK, CHUNK), F32), jnp.where(eye, gc, 0.0))
    decay = jnp.where(le, jnp.exp(jnp.where(le, gc - gc_row, 0.0)), 0.0)
    eg = jnp.exp(gc)
    gl = jnp.sum(g, axis=0, keepdims=True)
    kb = k * beta
    vb = v * beta
    kbeg = kb * eg
    lm = jnp.where(strict, _dot(kb, k, NT) * decay, 0.0)
    intra = _dot(q, k, NT) * decay
    qg = q * eg
    edec = jnp.exp(gl - gc)
    kdec = k * edec
    return dict(decay=decay, eg=eg, gl=gl, kb=kb, vb=vb, kbeg=kbeg, lm=lm, intra=intra, qg=qg, edec=edec, kdec=kdec)


def _unit_lower_inverse(lm, eye):
    x = -lm
    t = eye.astype(F32) + x
    p = x
    for _ in range(5):
        p = _dot_hi(p, p)
        t = t + _dot_hi(t, p)
    return t


def _gate_lanes(rev, h):
    d = 1 if rev else 0
    return d * GDN_HEADS + h, 8 + d * GDN_HEADS + h


def _delta_fwd(y, gb, rev):
    s = y.shape[0]
    nc = s // CHUNK
    hd = GDN_HEAD_DIM

    def chunk_of(n):
        return nc - 1 - n if rev else n

    def body(q_ref, k_ref, v_ref, gb_ref, o_ref, s_all, t_all, state):
        @pl.when(pl.program_id(0) == 0)
        def _():
            state[...] = jnp.zeros_like(state)

        masks = _chunk_masks(rev)
        gbv = gb_ref[...]
        gcm = _dot_hi(masks[0].astype(F32), gbv)
        for h in range(GDN_HEADS):
            gi, bi = _gate_lanes(rev, h)
            sl = slice(h * hd, (h + 1) * hd)
            q, k, v = q_ref[:, sl], k_ref[:, sl], v_ref[:, sl]
            g, beta, gc = gbv[:, gi:gi + 1], gbv[:, bi:bi + 1], gcm[:, gi:gi + 1]
            cm = _chunk_common(q, k, v, g, beta, gc, masks)
            tinv = _unit_lower_inverse(cm["lm"], masks[2])
            u = _dot(tinv, cm["vb"])
            w = _dot(tinv, cm["kbeg"])
            st = state[h]
            v_new = u - _dot(w, st)
            o_ref[:, sl] = _dot(cm["qg"], st) + _dot(cm["intra"], v_new)
            s_all[0, h] = st
            t_all[0, h] = tinv
            state[h] = st * jnp.exp(cm["gl"]) + _dot(cm["kdec"], v_new, TN)

    def col(j):
        return pl.BlockSpec((CHUNK, GDN_WIDTH), lambda n: (chunk_of(n), j))

    return pl.pallas_call(
        body, name="delta_fwd_r" if rev else "delta_fwd_f", grid=(nc,),
        in_specs=[col(0), col(1), col(2), pl.BlockSpec((CHUNK, LANE), lambda n: (chunk_of(n), 0))],
        out_specs=[pl.BlockSpec((CHUNK, GDN_WIDTH), lambda n: (chunk_of(n), 0)),
                   pl.BlockSpec((1, GDN_HEADS, hd, hd), lambda n: (chunk_of(n), 0, 0, 0)),
                   pl.BlockSpec((1, GDN_HEADS, CHUNK, CHUNK), lambda n: (chunk_of(n), 0, 0, 0))],
        out_shape=[jax.ShapeDtypeStruct((s, GDN_WIDTH), F32),
                   jax.ShapeDtypeStruct((nc, GDN_HEADS, hd, hd), F32),
                   jax.ShapeDtypeStruct((nc, GDN_HEADS, CHUNK, CHUNK), F32)],
        scratch_shapes=[pltpu.VMEM((GDN_HEADS, hd, hd), F32)],
        compiler_params=_params("arbitrary"),
    )(y, y, y, gb)


def _delta_bwd(y, gb, do, s_all, t_all, rev):
    s = y.shape[0]
    nc = s // CHUNK
    hd = GDN_HEAD_DIM

    def chunk_of(n):
        return n if rev else nc - 1 - n

    def body(q_ref, k_ref, v_ref, gb_ref, do_ref, s_ref, t_ref, dy_ref, dgb_ref, dstate):
        @pl.when(pl.program_id(0) == 0)
        def _():
            dstate[...] = jnp.zeros_like(dstate)

        masks = _chunk_masks(rev)
        le, strict, _ = masks
        le_t = _chunk_masks(not rev)[0].astype(F32)
        gbv = gb_ref[...]
        gcm = _dot_hi(le.astype(F32), gbv)
        lane = lax.broadcasted_iota(jnp.int32, (CHUNK, LANE), 1)
        ones_cl = jnp.ones((CHUNK, LANE), F32)
        dgc_tile = jnp.zeros((CHUNK, LANE), F32)
        rest_tile = jnp.zeros((CHUNK, LANE), F32)
        for h in range(GDN_HEADS):
            gi, bi = _gate_lanes(rev, h)
            sl = slice(h * hd, (h + 1) * hd)
            q, k, v = q_ref[:, sl], k_ref[:, sl], v_ref[:, sl]
            g, beta, gc = gbv[:, gi:gi + 1], gbv[:, bi:bi + 1], gcm[:, gi:gi + 1]
            cm = _chunk_common(q, k, v, g, beta, gc, masks)
            tinv = t_ref[0, h]
            st = s_ref[0, h]
            ds_out = dstate[h]
            dov = do_ref[:, sl]
            u = _dot(tinv, cm["vb"])
            w = _dot(tinv, cm["kbeg"])
            v_new = u - _dot(w, st)
            egl = jnp.exp(cm["gl"])
            d_qg = _dot(dov, st, NT)
            d_intra = _dot(dov, v_new, NT)
            dv_new = _dot(cm["intra"], dov, TN) + _dot(cm["kdec"], ds_out)
            d_kdec = _dot(v_new, ds_out, NT)
            dstate[h] = _dot(cm["qg"], dov, TN) + egl * ds_out - _dot(w, dv_new, TN)
            dgl = egl * jnp.sum(jnp.sum(st * ds_out, axis=1, keepdims=True), axis=0, keepdims=True)
            dw = -_dot(dv_new, st, NT)
            dvb = _dot(tinv, dv_new, TN)
            dkbeg = _dot(tinv, dw, TN)
            dlm = jnp.where(strict, -(_dot(dvb, u, NT) + _dot(dkbeg, w, NT)), 0.0)
            d_a = dlm * cm["decay"]
            d_qk = d_intra * cm["decay"]
            e = dlm * cm["lm"] + d_intra * cm["intra"]
            dgc = jnp.sum(e, axis=1, keepdims=True) - _dot_hi(e, ones_cl, TN)[:, 0:1]
            dkb = _dot(d_a, k) + dkbeg * cm["eg"]
            dk = _dot(d_a, cm["kb"], TN) + _dot(d_qk, q, TN)
            dq = _dot(d_qk, k) + d_qg * cm["eg"]
            dgc = dgc + jnp.sum(d_qg * cm["qg"], axis=1, keepdims=True)
            dgc = dgc + jnp.sum(dkbeg * cm["kbeg"], axis=1, keepdims=True)
            tdec = jnp.sum(d_kdec * cm["kdec"], axis=1, keepdims=True)
            dk = dk + d_kdec * cm["edec"] + dkb * beta
            dgc = dgc - tdec
            dgl = dgl + jnp.sum(tdec, axis=0, keepdims=True)
            dbeta = jnp.sum(dvb * v, axis=1, keepdims=True) + jnp.sum(dkb * k, axis=1, keepdims=True)
            dy_ref[:, h * hd:(h + 1) * hd] = dq
            dy_ref[:, GDN_WIDTH + h * hd:GDN_WIDTH + (h + 1) * hd] = dk
            dy_ref[:, 2 * GDN_WIDTH + h * hd:2 * GDN_WIDTH + (h + 1) * hd] = dvb * beta
            dgc_tile = dgc_tile + jnp.where(lane == gi, dgc, 0.0)
            rest_tile = rest_tile + jnp.where(lane == gi, dgl, 0.0) + jnp.where(lane == bi, dbeta, 0.0)
        dgb_ref[...] = _dot_hi(le_t, dgc_tile) + rest_tile

    def col(j):
        return pl.BlockSpec((CHUNK, GDN_WIDTH), lambda n: (chunk_of(n), j))

    first = pl.BlockSpec((CHUNK, GDN_WIDTH), lambda n: (chunk_of(n), 0))
    return pl.pallas_call(
        body, name="delta_bwd_r" if rev else "delta_bwd_f", grid=(nc,),
        in_specs=[col(0), col(1), col(2), pl.BlockSpec((CHUNK, LANE), lambda n: (chunk_of(n), 0)), first,
                  pl.BlockSpec((1, GDN_HEADS, hd, hd), lambda n: (chunk_of(n), 0, 0, 0)),
                  pl.BlockSpec((1, GDN_HEADS, CHUNK, CHUNK), lambda n: (chunk_of(n), 0, 0, 0))],
        out_specs=[pl.BlockSpec((CHUNK, QKV_A), lambda n: (chunk_of(n), 0)),
                   pl.BlockSpec((CHUNK, LANE), lambda n: (chunk_of(n), 0))],
        out_shape=[jax.ShapeDtypeStruct((s, QKV_A), F32), jax.ShapeDtypeStruct((s, LANE), F32)],
        scratch_shapes=[pltpu.VMEM((GDN_HEADS, hd, hd), F32)],
        compiler_params=_params("arbitrary"),
    )(y, y, y, gb, do, s_all, t_all)


BNN = (((2,), (1,)), ((0,), (0,)))
BNT = (((2,), (2,)), ((0,), (0,)))
BTN = (((1,), (1,)), ((0,), (0,)))
NB = 2 * GDN_HEADS


def _bdot(a, b, dn=BNN):
    return lax.dot_general(a.astype(BF16), b.astype(BF16), dn, preferred_element_type=F32)


def _dot3(a, b, dn):
    ah = a.astype(BF16)
    al = (a - ah.astype(F32)).astype(BF16)
    bh = b.astype(BF16)
    bl = (b - bh.astype(F32)).astype(BF16)

    def d(x, y):
        return lax.dot_general(x, y, dn, preferred_element_type=F32)

    return d(ah, bh) + d(ah, bl) + d(al, bh)


def _both(f_val, r_val):
    return jnp.stack([f_val] * GDN_HEADS + [r_val] * GDN_HEADS)


def _heads(ref_f, ref_r):
    hd = GDN_HEAD_DIM
    return jnp.stack([ref_f[:, h * hd:(h + 1) * hd] for h in range(GDN_HEADS)]
                     + [ref_r[:, h * hd:(h + 1) * hd] for h in range(GDN_HEADS)])


def _gate_cols(tile_f, tile_r, base):
    return jnp.stack([tile_f[:, base + h:base + h + 1] for h in range(GDN_HEADS)]
                     + [tile_r[:, base + GDN_HEADS + h:base + GDN_HEADS + h + 1] for h in range(GDN_HEADS)])


def _chunk_common2(q, k, v, gbf, gbr):
    mf, mr = _chunk_masks(False), _chunk_masks(True)
    le, strict = _both(mf[0], mr[0]), _both(mf[1], mr[1])
    eye = mf[2]
    gcm_f = _dot3(mf[0].astype(F32), gbf, NN)
    gcm_r = _dot3(mr[0].astype(F32), gbr, NN)
    g, beta, gc = _gate_cols(gbf, gbr, 0), _gate_cols(gbf, gbr, 8), _gate_cols(gcm_f, gcm_r, 0)
    gc_row = _dot3(jnp.ones((NB, CHUNK, CHUNK), F32), jnp.where(eye[None], gc, 0.0), BNN)
    decay = jnp.where(le, jnp.exp(jnp.where(le, gc - gc_row, 0.0)), 0.0)
    eg = jnp.exp(gc)
    gl = jnp.sum(g, axis=1, keepdims=True)
    kb = k * beta
    vb = v * beta
    kbeg = kb * eg
    lm = jnp.where(strict, _bdot(kb, k, BNT) * decay, 0.0)
    intra = _bdot(q, k, BNT) * decay
    edec = jnp.exp(gl - gc)
    return dict(strict=strict, eye=eye, beta=beta, decay=decay, eg=eg, gl=gl, kb=kb, vb=vb, kbeg=kbeg,
                lm=lm, intra=intra, qg=q * eg, edec=edec, kdec=k * edec)


def _unit_triangular_inverse(lm, eye):
    x = -lm
    t = eye[None].astype(F32) + x
    p = x
    for level in range(5):
        prod = functools.partial(_dot3, dn=BNN) if level < 2 else _bdot
        p = prod(p, p)
        t = t + prod(t, p)
    return t


def _delta_fwd2(y, gb, gather=()):
    s = y.shape[0]
    nc = s // CHUNK
    hd = GDN_HEAD_DIM
    na = len(gather)

    def body(*refs):
        qf, kf, vf, gf, qr, kr, vr, gr = refs[:8]
        of_ref, or_ref, sf_all, sr_all, tf_all, tr_all = refs[8 + na:14 + na]
        state = refs[14 + 2 * na]
        step = pl.program_id(0)

        @pl.when(step == 0)
        def _():
            state[...] = jnp.zeros_like(state)

        if na:
            start, forward, finish = _gather_phases(refs[8:8 + na], refs[14 + na:14 + 2 * na], *refs[15 + 2 * na:])
            pl.when(step == 0)(start)
            pl.when(step == nc // 2)(forward)
            pl.when(step == nc - 1)(finish)

        q, k, v = _heads(qf, qr), _heads(kf, kr), _heads(vf, vr)
        cm = _chunk_common2(q, k, v, gf[...], gr[...])
        tinv = _unit_triangular_inverse(cm["lm"], cm["eye"])
        u = _bdot(tinv, cm["vb"])
        w = _bdot(tinv, cm["kbeg"])
        st = state[...]
        v_new = u - _bdot(w, st)
        o = _bdot(cm["qg"], st) + _bdot(cm["intra"], v_new)
        state[...] = st * jnp.exp(cm["gl"]) + _bdot(cm["kdec"], v_new, BTN)
        for h in range(GDN_HEADS):
            of_ref[:, h * hd:(h + 1) * hd] = o[h]
            or_ref[:, h * hd:(h + 1) * hd] = o[GDN_HEADS + h]
        sf_all[0] = st[:GDN_HEADS]
        sr_all[0] = st[GDN_HEADS:]
        tf_all[0] = tinv[:GDN_HEADS]
        tr_all[0] = tinv[GDN_HEADS:]

    def col(j, rev):
        return pl.BlockSpec((CHUNK, GDN_WIDTH), (lambda n: (nc - 1 - n, j)) if rev else (lambda n: (n, j)))

    def gate(rev):
        return pl.BlockSpec((CHUNK, LANE), (lambda n: (nc - 1 - n, 0)) if rev else (lambda n: (n, 0)))

    def per_chunk(d1, d2, rev):
        return pl.BlockSpec((1, GDN_HEADS, d1, d2), (lambda n: (nc - 1 - n, 0, 0, 0)) if rev else (lambda n: (n, 0, 0, 0)))

    assert na == 0 or nc >= 4
    res = pl.pallas_call(
        body, name="delta_fwd", grid=(nc,),
        in_specs=[col(0, False), col(1, False), col(2, False), gate(False), col(0, True), col(1, True), col(2, True), gate(True)]
        + [ANY] * na,
        out_specs=[col(0, False), col(0, True), per_chunk(hd, hd, False), per_chunk(hd, hd, True),
                   per_chunk(CHUNK, CHUNK, False), per_chunk(CHUNK, CHUNK, True)] + [ANY] * na,
        out_shape=[jax.ShapeDtypeStruct((s, GDN_WIDTH), F32)] * 2 + [jax.ShapeDtypeStruct((nc, GDN_HEADS, hd, hd), F32)] * 2
        + [jax.ShapeDtypeStruct((nc, GDN_HEADS, CHUNK, CHUNK), F32)] * 2
        + [jax.ShapeDtypeStruct((N_DEV,) + v.shape, v.dtype) for v in gather],
        scratch_shapes=[pltpu.VMEM((NB, hd, hd), F32)] + (_gather_semaphores(na) if na else []),
        compiler_params=_params("arbitrary"),
    )(y, y, y, gb, y, y, y, gb, *gather)
    return res[:6], res[6:]


def _delta_bwd2(y, gb, do, sf_all, sr_all, tf_all, tr_all, scatter=()):
    s = y.shape[0]
    nc = s // CHUNK
    hd = GDN_HEAD_DIM
    na = len(scatter)

    def body(*refs):
        qf, kf, vf, gf, dof, sf, tf, qr, kr, vr, gr, dor, sr, tr = refs[:14]
        dyf_ref, dyr_ref, dgf_ref, dgr_ref = refs[14 + na:18 + na]
        dstate = refs[18 + 2 * na]
        step = pl.program_id(0)

        @pl.when(step == 0)
        def _():
            dstate[...] = jnp.zeros_like(dstate)

        if na:
            start, finish = _scatter_phases(refs[14:14 + na], refs[18 + na:18 + 2 * na], *refs[19 + 2 * na:])
            pl.when(step == 0)(start)
            pl.when(step == nc - 1)(finish)

        q, k, v, dov = _heads(qf, qr), _heads(kf, kr), _heads(vf, vr), _heads(dof, dor)
        cm = _chunk_common2(q, k, v, gf[...], gr[...])
        tinv = jnp.concatenate([tf[0], tr[0]], axis=0)
        st = jnp.concatenate([sf[0], sr[0]], axis=0)
        ds_out = dstate[...]
        decay, lm, intra, qg, kdec, kbeg, eg, kb, beta = (
            cm[n] for n in ("decay", "lm", "intra", "qg", "kdec", "kbeg", "eg", "kb", "beta"))
        u = _bdot(tinv, cm["vb"])
        w = _bdot(tinv, kbeg)
        v_new = u - _bdot(w, st)
        egl = jnp.exp(cm["gl"])
        d_qg = _bdot(dov, st, BNT)
        d_intra = _bdot(dov, v_new, BNT)
        dv_new = _bdot(intra, dov, BTN) + _bdot(kdec, ds_out)
        d_kdec = _bdot(v_new, ds_out, BNT)
        dstate[...] = _bdot(qg, dov, BTN) + egl * ds_out - _bdot(w, dv_new, BTN)
        dgl = egl * jnp.sum(jnp.sum(st * ds_out, axis=2, keepdims=True), axis=1, keepdims=True)
        dw = -_bdot(dv_new, st, BNT)
        dvb = _bdot(tinv, dv_new, BTN)
        dkbeg = _bdot(tinv, dw, BTN)
        dlm = jnp.where(cm["strict"], -(_bdot(dvb, u, BNT) + _bdot(dkbeg, w, BNT)), 0.0)
        d_a = dlm * decay
        d_qk = d_intra * decay
        e = dlm * lm + d_intra * intra
        colsum = _dot3(e, jnp.ones((NB, CHUNK, LANE), F32), BTN)[:, :, 0:1]
        dgc = jnp.sum(e, axis=2, keepdims=True) - colsum
        dkb = _bdot(d_a, k) + dkbeg * eg
        dk = _bdot(d_a, kb, BTN) + _bdot(d_qk, q, BTN)
        dq = _bdot(d_qk, k) + d_qg * eg
        dgc = dgc + jnp.sum(d_qg * qg, axis=2, keepdims=True) + jnp.sum(dkbeg * kbeg, axis=2, keepdims=True)
        tdec = jnp.sum(d_kdec * kdec, axis=2, keepdims=True)
        dk = dk + d_kdec * cm["edec"] + dkb * beta
        dgc = dgc - tdec
        dgl = dgl + jnp.sum(tdec, axis=1, keepdims=True)
        dbeta = jnp.sum(dvb * v, axis=2, keepdims=True) + jnp.sum(dkb * k, axis=2, keepdims=True)
        dv = dvb * beta
        lane = lax.broadcasted_iota(jnp.int32, (CHUNK, LANE), 1)
        for rev, dy_ref, dg_ref in ((False, dyf_ref, dgf_ref), (True, dyr_ref, dgr_ref)):
            dgc_tile = jnp.zeros((CHUNK, LANE), F32)
            rest = jnp.zeros((CHUNK, LANE), F32)
            for h in range(GDN_HEADS):
                b = (GDN_HEADS if rev else 0) + h
                gi, bi = _gate_lanes(rev, h)
                dgc_tile = dgc_tile + jnp.where(lane == gi, dgc[b], 0.0)
                rest = rest + jnp.where(lane == gi, dgl[b], 0.0) + jnp.where(lane == bi, dbeta[b], 0.0)
                dy_ref[:, h * hd:(h + 1) * hd] = dq[b]
                dy_ref[:, GDN_WIDTH + h * hd:GDN_WIDTH + (h + 1) * hd] = dk[b]
                dy_ref[:, 2 * GDN_WIDTH + h * hd:2 * GDN_WIDTH + (h + 1) * hd] = dv[b]
            le_t = _chunk_masks(not rev)[0].astype(F32)
            dg_ref[...] = _dot3(le_t, dgc_tile, NN) + rest

    def col(j, rev):
        return pl.BlockSpec((CHUNK, GDN_WIDTH), (lambda n: (n, j)) if rev else (lambda n: (nc - 1 - n, j)))

    def wide(width, rev):
        return pl.BlockSpec((CHUNK, width), (lambda n: (n, 0)) if rev else (lambda n: (nc - 1 - n, 0)))

    def per_chunk(d1, d2, rev):
        return pl.BlockSpec((1, GDN_HEADS, d1, d2), (lambda n: (n, 0, 0, 0)) if rev else (lambda n: (nc - 1 - n, 0, 0, 0)))

    def side(rev):
        return [col(0, rev), col(1, rev), col(2, rev), wide(LANE, rev), wide(GDN_WIDTH, rev), per_chunk(hd, hd, rev),
                per_chunk(CHUNK, CHUNK, rev)]

    assert na == 0 or nc >= 2
    res = pl.pallas_call(
        body, name="delta_bwd", grid=(nc,),
        in_specs=side(False) + side(True) + [ANY] * na,
        out_specs=[wide(QKV_A, False), wide(QKV_A, True), wide(LANE, False), wide(LANE, True)] + [ANY] * na,
        out_shape=[jax.ShapeDtypeStruct((s, QKV_A), F32)] * 2 + [jax.ShapeDtypeStruct((s, LANE), F32)] * 2
        + [jax.ShapeDtypeStruct(g.shape, g.dtype) for g in scatter],
        scratch_shapes=[pltpu.VMEM((NB, hd, hd), F32)] + (_gather_semaphores(na) if na else []),
        compiler_params=_params("arbitrary"),
    )(y, y, y, gb, do, sf_all, tf_all, y, y, y, gb, do, sr_all, tr_all, *scatter)
    return res[:4], res[4:]


def _gdn_post_fwd(o_f, o_r, p_pad, norm_row):
    s = o_f.shape[0]
    tm = min(512, s)
    hd = GDN_HEAD_DIM

    def body(of_ref, or_ref, z_ref, w_ref, out_ref, osum_ref):
        o = of_ref[...] + or_ref[...]
        osum_ref[...] = o
        z = z_ref[...]
        gate = z * _sigmoid(z)
        for h in range(GDN_HEADS):
            sl = slice(h * hd, (h + 1) * hd)
            oh = o[:, sl]
            r = lax.rsqrt(jnp.mean(oh * oh, axis=-1, keepdims=True) + EPS)
            out_ref[:, sl] = (oh * r * w_ref[...] * gate[:, sl]).astype(BF16)

    blk = pl.BlockSpec((tm, GDN_WIDTH), lambda i: (i, 0))
    return pl.pallas_call(
        body, name="gdn_post_fwd", grid=(s // tm,),
        in_specs=[blk, blk, pl.BlockSpec((tm, GDN_WIDTH), lambda i: (i, OFF_Z // GDN_WIDTH)),
                  pl.BlockSpec((1, hd), lambda i: (0, 0))],
        out_specs=[blk, blk],
        out_shape=[jax.ShapeDtypeStruct((s, GDN_WIDTH), BF16), jax.ShapeDtypeStruct((s, GDN_WIDTH), F32)],
        compiler_params=_params("parallel"),
    )(o_f, o_r, p_pad, norm_row)


def _gdn_post_bwd(d_out, o_sum, p_pad, norm_row):
    s = o_sum.shape[0]
    tm = min(512, s)
    hd = GDN_HEAD_DIM

    def body(d_ref, o_ref, z_ref, w_ref, do_ref, dz_ref, dw_ref):
        @pl.when(pl.program_id(0) == 0)
        def _():
            dw_ref[...] = jnp.zeros_like(dw_ref)

        z = z_ref[...]
        sg = _sigmoid(z)
        gate = z * sg
        dgate = sg * (1.0 + z * (1.0 - sg))
        wv = w_ref[...]
        dw = jnp.zeros((1, hd), F32)
        for h in range(GDN_HEADS):
            sl = slice(h * hd, (h + 1) * hd)
            oh = o_ref[:, sl]
            dh = d_ref[:, sl]
            r = lax.rsqrt(jnp.mean(oh * oh, axis=-1, keepdims=True) + EPS)
            ohat = oh * r
            dz_ref[:, sl] = (dh * ohat * wv * dgate[:, sl]).astype(BF16)
            drn = dh * gate[:, sl]
            t = drn * wv
            do_ref[:, sl] = r * (t - ohat * jnp.mean(t * ohat, axis=-1, keepdims=True))
            dw = dw + jnp.sum(drn * ohat, axis=0, keepdims=True)
        dw_ref[...] += dw

    blk = pl.BlockSpec((tm, GDN_WIDTH), lambda i: (i, 0))
    vec = pl.BlockSpec((1, hd), lambda i: (0, 0))
    return pl.pallas_call(
        body, name="gdn_post_bwd", grid=(s // tm,),
        in_specs=[blk, blk, pl.BlockSpec((tm, GDN_WIDTH), lambda i: (i, OFF_Z // GDN_WIDTH)), vec],
        out_specs=[blk, pl.BlockSpec((tm, GDN_WIDTH), lambda i: (i, OFF_Z // GDN_WIDTH)), vec],
        out_shape=[jax.ShapeDtypeStruct((s, GDN_WIDTH), F32), jax.ShapeDtypeStruct((s, N_PAD), BF16),
                   jax.ShapeDtypeStruct((1, hd), F32)],
        compiler_params=_params("arbitrary"),
    )(d_out, o_sum, p_pad, norm_row)


def _add2(a, b, name):
    s, w = a.shape
    tm = next(t for t in (1024, 640, 512, 256, 128, 64, 8) if s % t == 0)

    def body(a_ref, b_ref, o_ref):
        o_ref[...] = a_ref[...] + b_ref[...]

    blk = pl.BlockSpec((tm, w), lambda i: (i, 0))
    return pl.pallas_call(body, name=name, grid=(s // tm,), in_specs=[blk, blk], out_specs=blk,
                          out_shape=jax.ShapeDtypeStruct((s, w), F32), compiler_params=_params("parallel"))(a, b)


def _gdn_forward(p_pad, conv_wt, alog_row, dt_row, norm_row, gather=()):
    c_pre, y = _gdn_conv_fwd(p_pad, conv_wt)
    gb = _gdn_gates_fwd(p_pad, alog_row, dt_row)
    (o_f, o_r, s_f, s_r, t_f, t_r), gathered = _delta_fwd2(y, gb, gather)
    out, o_sum = _gdn_post_fwd(o_f, o_r, p_pad, norm_row)
    return out, (c_pre, y, gb, s_f, t_f, s_r, t_r, o_sum), gathered


def _gdn_backward(d_out, p_pad, conv_wt, alog_row, dt_row, norm_row, saved, scatter=()):
    c_pre, y, gb, s_f, t_f, s_r, t_r, o_sum = saved
    do, dp_all, dnorm = _gdn_post_bwd(d_out, o_sum, p_pad, norm_row)
    (dy_f, dy_r, dgb_f, dgb_r), received = _delta_bwd2(y, gb, do, s_f, s_r, t_f, t_r, scatter)
    dp_all, dconv = _gdn_conv_bwd(dy_f, dy_r, c_pre, p_pad, conv_wt, dp_all)
    dp_all, gate_sums = _gdn_gates_bwd(dgb_f, dgb_r, p_pad, gb, alog_row, dt_row, dp_all)
    return dp_all, dconv, gate_sums, dnorm, received


ATT_BK = ATT_BQ + 2 * ATT_HALO
SWA_SCALE = SWA_HEAD_DIM ** -0.5


def _t5_bucket(rel):
    nb = REL_BUCKETS // 2
    bucket = (rel > 0).astype(np.int32) * nb
    n = np.abs(rel)
    max_exact = nb // 2
    large = max_exact + (np.log(np.maximum(n, 1) / max_exact)
                         / math.log(REL_MAX_DISTANCE / max_exact) * (nb - max_exact)).astype(np.int32)
    large = np.minimum(large, nb - 1)
    return (bucket + np.where(n < max_exact, n, large)).astype(np.int32)


def _band_tables(dilation, queries_are_rows_of_block):
    blk = np.arange(ATT_BQ)
    band = np.arange(ATT_BK) - ATT_HALO
    if queries_are_rows_of_block:
        rel = band[None, :] - blk[:, None]
        band_idx = np.broadcast_to(np.arange(ATT_BK)[None, :], rel.shape)
    else:
        rel = blk[None, :] - band[:, None]
        band_idx = np.broadcast_to(np.arange(ATT_BK)[:, None], rel.shape)
    base = np.abs(rel) <= ATT_HALO
    not_prev = band_idx >= ATT_HALO
    not_next = band_idx < ATT_HALO + ATT_BQ
    valid = np.stack([base & not_prev, base, base & not_next, base & not_prev & not_next])
    return valid, _t5_bucket(rel * dilation)


def _bias_tiles(rel_bias, dilation, queries_are_rows_of_block):
    valid, bucket = _band_tables(dilation, queries_are_rows_of_block)
    onehot = (jnp.asarray(bucket.reshape(-1, 1)) == jnp.arange(REL_BUCKETS, dtype=jnp.int32)[None, :]).astype(F32)
    rb = jnp.dot(onehot, rel_bias.astype(F32), precision=lax.Precision.HIGHEST)
    rb = rb.T.reshape((SWA_HEADS,) + bucket.shape)
    return jnp.where(valid[:, None], rb[None], NEG_BIG).astype(F32)


def _group_sum(x, bd):
    hi = x.astype(BF16)
    lo = (x - hi.astype(F32)).astype(BF16)
    return jnp.dot(hi, bd, preferred_element_type=F32) + jnp.dot(lo, bd, preferred_element_type=F32)


def _head_block_diag():
    idx = np.arange(SWA_WIDTH) // SWA_HEAD_DIM
    return jnp.asarray(idx[:, None] == idx[None, :], BF16)


def _swa_pre_fwd(p_pad, qw_row, kw_row, bd):
    s = p_pad.shape[0]
    tm = min(512, s)
    inv = 1.0 / SWA_HEAD_DIM

    def body(q_ref, k_ref, v_ref, qw_ref, kw_ref, bd_ref, qo_ref, ko_ref, vo_ref):
        bdv = bd_ref[...]
        q = q_ref[...]
        k = k_ref[...]
        rq = lax.rsqrt(_group_sum(q * q, bdv) * inv + EPS)
        rk = lax.rsqrt(_group_sum(k * k, bdv) * inv + EPS)
        qo_ref[...] = (q * rq * qw_ref[...] * SWA_SCALE).astype(BF16)
        ko_ref[...] = (k * rk * kw_ref[...]).astype(BF16)
        vo_ref[...] = v_ref[...].astype(BF16)

    base = OFF_B // SWA_WIDTH
    blk = pl.BlockSpec((tm, SWA_WIDTH), lambda i: (i, 0))
    vec = pl.BlockSpec((1, SWA_WIDTH), lambda i: (0, 0))
    return pl.pallas_call(
        body, name="swa_pre_fwd", grid=(s // tm,),
        in_specs=[pl.BlockSpec((tm, SWA_WIDTH), lambda i: (i, base)), pl.BlockSpec((tm, SWA_WIDTH), lambda i: (i, base + 1)),
                  pl.BlockSpec((tm, SWA_WIDTH), lambda i: (i, base + 2)), vec, vec,
                  pl.BlockSpec((SWA_WIDTH, SWA_WIDTH), lambda i: (0, 0))],
        out_specs=[blk, blk, blk],
        out_shape=[jax.ShapeDtypeStruct((s, SWA_WIDTH), BF16)] * 3,
        compiler_params=_params("parallel"),
    )(p_pad, p_pad, p_pad, qw_row, kw_row, bd)


def _swa_pre_bwd(dqs, dks, dvs, p_pad, qw_row, kw_row, bd, dp_all):
    s = p_pad.shape[0]
    tm = min(256, s)
    inv = 1.0 / SWA_HEAD_DIM
    npat = len(dqs)

    def body(*refs):
        dq_refs, dk_refs, dv_refs = refs[:npat], refs[npat:2 * npat], refs[2 * npat:3 * npat]
        q_ref, k_ref, qw_ref, kw_ref, bd_ref, _, dp_ref, dqw_ref, dkw_ref = refs[3 * npat:]

        @pl.when(pl.program_id(0) == 0)
        def _():
            dqw_ref[...] = jnp.zeros_like(dqw_ref)
            dkw_ref[...] = jnp.zeros_like(dkw_ref)

        bdv = bd_ref[...]

        def norm_bwd(x, g, w, scale):
            r = lax.rsqrt(_group_sum(x * x, bdv) * inv + EPS)
            xhat = x * r
            t = g * w * scale
            dx = r * (t - xhat * (_group_sum(t * xhat, bdv) * inv))
            return dx, jnp.sum(g * scale * xhat, axis=0, keepdims=True)

        def total(rs):
            t = rs[0][...].astype(F32)
            for r in rs[1:]:
                t = t + r[...].astype(F32)
            return t

        dq, dqw = norm_bwd(q_ref[...], total(dq_refs), qw_ref[...], SWA_SCALE)
        dk, dkw = norm_bwd(k_ref[...], total(dk_refs), kw_ref[...], 1.0)
        dp_ref[:, 0:SWA_WIDTH] = dq.astype(BF16)
        dp_ref[:, SWA_WIDTH:2 * SWA_WIDTH] = dk.astype(BF16)
        dp_ref[:, 2 * SWA_WIDTH:3 * SWA_WIDTH] = total(dv_refs).astype(BF16)
        dqw_ref[...] += dqw
        dkw_ref[...] += dkw

    base = OFF_B // SWA_WIDTH
    blk = pl.BlockSpec((tm, SWA_WIDTH), lambda i: (i, 0))
    vec = pl.BlockSpec((1, SWA_WIDTH), lambda i: (0, 0))
    return pl.pallas_call(
        body, name="swa_pre_bwd", grid=(s // tm,),
        in_specs=[blk] * (3 * npat) + [pl.BlockSpec((tm, SWA_WIDTH), lambda i: (i, base)),
                                      pl.BlockSpec((tm, SWA_WIDTH), lambda i: (i, base + 1)), vec, vec,
                                      pl.BlockSpec((SWA_WIDTH, SWA_WIDTH), lambda i: (0, 0)), ANY],
        out_specs=[pl.BlockSpec((tm, 3 * SWA_WIDTH), lambda i: (i, OFF_B // (3 * SWA_WIDTH))), vec, vec],
        out_shape=[jax.ShapeDtypeStruct(dp_all.shape, dp_all.dtype), jax.ShapeDtypeStruct((1, SWA_WIDTH), F32),
                   jax.ShapeDtypeStruct((1, SWA_WIDTH), F32)],
        input_output_aliases={3 * npat + 5: 0},
        compiler_params=_params("arbitrary"),
    )(*dqs, *dks, *dvs, p_pad, p_pad, qw_row, kw_row, bd, dp_all)


def _band_specs(length):
    per = ATT_BQ // ATT_HALO
    last = length // ATT_HALO - 1
    prev = pl.BlockSpec((ATT_HALO, SWA_WIDTH), lambda r, t: (jnp.maximum(t * per - 1, 0), r))
    cur = pl.BlockSpec((ATT_BQ, SWA_WIDTH), lambda r, t: (t, r))
    nxt = pl.BlockSpec((ATT_HALO, SWA_WIDTH), lambda r, t: (jnp.minimum((t + 1) * per, last), r))
    return [prev, cur, nxt]


def _tile_variant(t, nb):
    if nb == 1:
        return 3
    return jnp.where(t == 0, 0, jnp.where(t == nb - 1, 2, 1))


def _band(refs):
    return jnp.concatenate([r[...] for r in refs], axis=0)


def _att_fwd(q, k, v, bias, dilation):
    s = q.shape[0]
    length = s // dilation
    nb = length // ATT_BQ
    view = (length, dilation * SWA_WIDTH)
    hd = SWA_HEAD_DIM

    def body(q_ref, kp, kc, kn, vp, vc, vn, b_ref, o_ref, lse_ref):
        kb, vb = _band((kp, kc, kn)), _band((vp, vc, vn))
        qv = q_ref[...]
        for h in range(SWA_HEADS):
            sl = slice(h * hd, (h + 1) * hd)
            sc = _dot(qv[:, sl], kb[:, sl], NT) + b_ref[0, h]
            m = jnp.max(sc, axis=-1, keepdims=True)
            p = jnp.exp(sc - m)
            den = jnp.sum(p, axis=-1, keepdims=True)
            o_ref[:, sl] = _dot(p, vb[:, sl]) / den
            lse_ref[:, sl] = jnp.broadcast_to(m + jnp.log(den), (ATT_BQ, hd))

    cur = pl.BlockSpec((ATT_BQ, SWA_WIDTH), lambda r, t: (t, r))
    bspec = pl.BlockSpec((1, SWA_HEADS, ATT_BQ, ATT_BK), lambda r, t: (_tile_variant(t, nb), 0, 0, 0))
    o, lse = pl.pallas_call(
        body, name=f"att_fwd_d{dilation}", grid=(dilation, nb),
        in_specs=[cur] + _band_specs(length) * 2 + [bspec],
        out_specs=[cur, cur],
        out_shape=[jax.ShapeDtypeStruct(view, F32)] * 2,
        compiler_params=_params("parallel", "parallel"),
    )(q.reshape(view), *([k.reshape(view)] * 3), *([v.reshape(view)] * 3), bias)
    return o.reshape(s, SWA_WIDTH), lse.reshape(s, SWA_WIDTH)


def _att_dq(q, k, v, dop, lse, cp, bias, dilation):
    s = q.shape[0]
    length = s // dilation
    nb = length // ATT_BQ
    view = (length, dilation * SWA_WIDTH)
    hd = SWA_HEAD_DIM

    def body(q_ref, kp, kc, kn, vp, vc, vn, do_ref, lse_ref, cp_ref, b_ref, dq_ref, db_ref):
        @pl.when((pl.program_id(0) == 0) & (pl.program_id(1) == 0))
        def _():
            db_ref[...] = jnp.zeros_like(db_ref)

        var = _tile_variant(pl.program_id(1), nb)
        kb, vb = _band((kp, kc, kn)), _band((vp, vc, vn))
        qv, dov, lsev, cpv = q_ref[...], do_ref[...], lse_ref[...], cp_ref[...]
        for h in range(SWA_HEADS):
            sl = slice(h * hd, (h + 1) * hd)
            sc = _dot(qv[:, sl], kb[:, sl], NT) + b_ref[0, h]
            p = jnp.exp(sc - lsev[:, h * hd:h * hd + 1])
            dp = _dot(dov[:, sl], vb[:, sl], NT)
            ds = p * (dp + cpv[:, h * hd:h * hd + 1])
            dq_ref[:, sl] = _dot(ds, kb[:, sl])
            db_ref[var, h] += ds

    cur = pl.BlockSpec((ATT_BQ, SWA_WIDTH), lambda r, t: (t, r))
    bspec = pl.BlockSpec((1, SWA_HEADS, ATT_BQ, ATT_BK), lambda r, t: (_tile_variant(t, nb), 0, 0, 0))
    dq, db = pl.pallas_call(
        body, name=f"att_dq_d{dilation}", grid=(dilation, nb),
        in_specs=[cur] + _band_specs(length) * 2 + [cur, cur, cur, bspec],
        out_specs=[cur, pl.BlockSpec((4, SWA_HEADS, ATT_BQ, ATT_BK), lambda r, t: (0, 0, 0, 0))],
        out_shape=[jax.ShapeDtypeStruct(view, F32), jax.ShapeDtypeStruct((4, SWA_HEADS, ATT_BQ, ATT_BK), F32)],
        compiler_params=_params("arbitrary", "arbitrary"),
    )(q.reshape(view), *([k.reshape(view)] * 3), *([v.reshape(view)] * 3), dop.reshape(view), lse.reshape(view),
      cp.reshape(view), bias)
    return dq.reshape(s, SWA_WIDTH), db


def _att_dkv(q, k, v, dop, lse, cp, bias_t, dilation):
    s = q.shape[0]
    length = s // dilation
    nb = length // ATT_BQ
    view = (length, dilation * SWA_WIDTH)
    hd = SWA_HEAD_DIM

    def body(k_ref, v_ref, qp, qc, qn, dp_, dc_, dn_, lp, lc, ln, cp_, cc_, cn_, b_ref, dk_ref, dv_ref):
        qb, dob = _band((qp, qc, qn)), _band((dp_, dc_, dn_))
        lseb, cpb = _band((lp, lc, ln)), _band((cp_, cc_, cn_))
        kv, vv = k_ref[...], v_ref[...]
        for h in range(SWA_HEADS):
            sl = slice(h * hd, (h + 1) * hd)
            sc = _dot(qb[:, sl], kv[:, sl], NT) + b_ref[0, h]
            p = jnp.exp(sc - lseb[:, h * hd:h * hd + 1])
            dv_ref[:, sl] = _dot(p, dob[:, sl], TN)
            dp = _dot(dob[:, sl], vv[:, sl], NT)
            ds = p * (dp + cpb[:, h * hd:h * hd + 1])
            dk_ref[:, sl] = _dot(ds, qb[:, sl], TN)

    cur = pl.BlockSpec((ATT_BQ, SWA_WIDTH), lambda r, t: (t, r))
    bspec = pl.BlockSpec((1, SWA_HEADS, ATT_BK, ATT_BQ), lambda r, t: (_tile_variant(t, nb), 0, 0, 0))
    dk, dv = pl.pallas_call(
        body, name=f"att_dkv_d{dilation}", grid=(dilation, nb),
        in_specs=[cur, cur] + _band_specs(length) * 4 + [bspec],
        out_specs=[cur, cur],
        out_shape=[jax.ShapeDtypeStruct(view, F32)] * 2,
        compiler_params=_params("parallel", "parallel"),
    )(k.reshape(view), v.reshape(view), *([q.reshape(view)] * 3), *([dop.reshape(view)] * 3),
      *([lse.reshape(view)] * 3), *([cp.reshape(view)] * 3), bias_t)
    return dk.reshape(s, SWA_WIDTH), dv.reshape(s, SWA_WIDTH)


N_PAIRS = SWA_HEADS // 2


def _pairs(x):
    return jnp.stack([x[:, LANE * p:LANE * (p + 1)] for p in range(N_PAIRS)])


def _per_head_rows(x):
    first = lax.broadcasted_iota(jnp.int32, x.shape, 2) < SWA_HEAD_DIM
    zero = jnp.zeros_like(x)
    return jnp.concatenate([jnp.where(first, x, zero), jnp.where(first, zero, x)], axis=1)


def _per_head_cols(x):
    return jnp.stack([jnp.concatenate([x[:, LANE * p:LANE * p + 1],
                                       x[:, LANE * p + SWA_HEAD_DIM:LANE * p + SWA_HEAD_DIM + 1]], axis=0)
                      for p in range(N_PAIRS)])


def _merge_heads(x, rows):
    first = lax.broadcasted_iota(jnp.int32, (N_PAIRS, rows, LANE), 2) < SWA_HEAD_DIM
    return jnp.where(first, x[:, :rows], x[:, rows:])


def _store_pairs(ref, x):
    for p in range(N_PAIRS):
        ref[:, LANE * p:LANE * (p + 1)] = x[p].astype(ref.dtype)


def _att_fwd2(q, k, v, bias, dilation):
    s = q.shape[0]
    length = s // dilation
    nb = length // ATT_BQ
    view = (length, dilation * SWA_WIDTH)

    def body(q_ref, kp, kc, kn, vp, vc, vn, b_ref, o_ref, lse_ref):
        kb, vb = _pairs(_band((kp, kc, kn))), _pairs(_band((vp, vc, vn)))
        qm = _per_head_rows(_pairs(q_ref[...]))
        sc = _bdot(qm, kb, BNT) + b_ref[0].reshape(N_PAIRS, 2 * ATT_BQ, ATT_BK)
        m = jnp.max(sc, axis=-1, keepdims=True)
        p = jnp.exp(sc - m)
        den = jnp.sum(p, axis=-1, keepdims=True)
        o = _bdot(p, vb) / den
        _store_pairs(o_ref, _merge_heads(o, ATT_BQ))
        lse = jnp.broadcast_to(m + jnp.log(den), (N_PAIRS, 2 * ATT_BQ, LANE))
        _store_pairs(lse_ref, _merge_heads(lse, ATT_BQ))

    cur = pl.BlockSpec((ATT_BQ, SWA_WIDTH), lambda r, t: (t, r))
    bspec = pl.BlockSpec((1, SWA_HEADS, ATT_BQ, ATT_BK), lambda r, t: (_tile_variant(t, nb), 0, 0, 0))
    o, lse = pl.pallas_call(
        body, name=f"att_fwd_d{dilation}", grid=(dilation, nb),
        in_specs=[cur] + _band_specs(length) * 2 + [bspec],
        out_specs=[cur, cur],
        out_shape=[jax.ShapeDtypeStruct(view, BF16), jax.ShapeDtypeStruct(view, F32)],
        compiler_params=_params("parallel", "parallel"),
    )(q.reshape(view), *([k.reshape(view)] * 3), *([v.reshape(view)] * 3), bias)
    return o.reshape(s, SWA_WIDTH), lse.reshape(s, SWA_WIDTH)


def _att_dq2(q, k, v, dop, lse, cp, bias, dilation):
    s = q.shape[0]
    length = s // dilation
    nb = length // ATT_BQ
    view = (length, dilation * SWA_WIDTH)

    def body(q_ref, kp, kc, kn, vp, vc, vn, do_ref, lse_ref, cp_ref, b_ref, dq_ref, db_ref):
        @pl.when((pl.program_id(0) == 0) & (pl.program_id(1) == 0))
        def _():
            db_ref[...] = jnp.zeros_like(db_ref)

        var = _tile_variant(pl.program_id(1), nb)
        kb, vb = _pairs(_band((kp, kc, kn))), _pairs(_band((vp, vc, vn)))
        qm = _per_head_rows(_pairs(q_ref[...]))
        dom = _per_head_rows(_pairs(do_ref[...]))
        sc = _bdot(qm, kb, BNT) + b_ref[0].reshape(N_PAIRS, 2 * ATT_BQ, ATT_BK)
        p = jnp.exp(sc - _per_head_cols(lse_ref[...]))
        ds = p * (_bdot(dom, vb, BNT) + _per_head_cols(cp_ref[...]))
        _store_pairs(dq_ref, _merge_heads(_bdot(ds, kb), ATT_BQ))
        db_ref[var] += ds.reshape(SWA_HEADS, ATT_BQ, ATT_BK)

    cur = pl.BlockSpec((ATT_BQ, SWA_WIDTH), lambda r, t: (t, r))
    bspec = pl.BlockSpec((1, SWA_HEADS, ATT_BQ, ATT_BK), lambda r, t: (_tile_variant(t, nb), 0, 0, 0))
    dq, db = pl.pallas_call(
        body, name=f"att_dq_d{dilation}", grid=(dilation, nb),
        in_specs=[cur] + _band_specs(length) * 2 + [cur, cur, cur, bspec],
        out_specs=[cur, pl.BlockSpec((4, SWA_HEADS, ATT_BQ, ATT_BK), lambda r, t: (0, 0, 0, 0))],
        out_shape=[jax.ShapeDtypeStruct(view, BF16), jax.ShapeDtypeStruct((4, SWA_HEADS, ATT_BQ, ATT_BK), F32)],
        compiler_params=_params("arbitrary", "arbitrary"),
    )(q.reshape(view), *([k.reshape(view)] * 3), *([v.reshape(view)] * 3), dop.reshape(view), lse.reshape(view),
      cp.reshape(view), bias)
    return dq.reshape(s, SWA_WIDTH), db


def _att_dkv2(q, k, v, dop, lse, cp, bias_t, dilation):
    s = q.shape[0]
    length = s // dilation
    nb = length // ATT_BQ
    view = (length, dilation * SWA_WIDTH)

    def body(k_ref, v_ref, qp, qc, qn, dp_, dc_, dn_, lp, lc, ln, cp_, cc_, cn_, b_ref, dk_ref, dv_ref):
        qm = _per_head_rows(_pairs(_band((qp, qc, qn))))
        dom = _per_head_rows(_pairs(_band((dp_, dc_, dn_))))
        lsev = _per_head_cols(_band((lp, lc, ln)))
        cpv = _per_head_cols(_band((cp_, cc_, cn_)))
        kv, vv = _pairs(k_ref[...]), _pairs(v_ref[...])
        sc = _bdot(qm, kv, BNT) + b_ref[0].reshape(N_PAIRS, 2 * ATT_BK, ATT_BQ)
        p = jnp.exp(sc - lsev)
        _store_pairs(dv_ref, _bdot(p, dom, BTN))
        ds = p * (_bdot(dom, vv, BNT) + cpv)
        _store_pairs(dk_ref, _bdot(ds, qm, BTN))

    cur = pl.BlockSpec((ATT_BQ, SWA_WIDTH), lambda r, t: (t, r))
    bspec = pl.BlockSpec((1, SWA_HEADS, ATT_BK, ATT_BQ), lambda r, t: (_tile_variant(t, nb), 0, 0, 0))
    dk, dv = pl.pallas_call(
        body, name=f"att_dkv_d{dilation}", grid=(dilation, nb),
        in_specs=[cur, cur] + _band_specs(length) * 4 + [bspec],
        out_specs=[cur, cur],
        out_shape=[jax.ShapeDtypeStruct(view, BF16)] * 2,
        compiler_params=_params("parallel", "parallel"),
    )(k.reshape(view), v.reshape(view), *([q.reshape(view)] * 3), *([dop.reshape(view)] * 3),
      *([lse.reshape(view)] * 3), *([cp.reshape(view)] * 3), bias_t)
    return dk.reshape(s, SWA_WIDTH), dv.reshape(s, SWA_WIDTH)


def _pattern_weights(lses):
    m = lses[0]
    for l in lses[1:]:
        m = jnp.maximum(m, l)
    es = [jnp.exp(l - m) for l in lses]
    den = es[0]
    for e in es[1:]:
        den = den + e
    return [e / den for e in es]


def _combine_fwd(outs, lses):
    s = outs[0].shape[0]
    tm = min(512, s)
    npat = len(outs)

    def body(*refs):
        ws = _pattern_weights([r[...] for r in refs[npat:2 * npat]])
        o = ws[0] * refs[0][...]
        for p in range(1, npat):
            o = o + ws[p] * refs[p][...]
        refs[2 * npat][...] = o.astype(BF16)

    blk = pl.BlockSpec((tm, SWA_WIDTH), lambda i: (i, 0))
    return pl.pallas_call(
        body, name="swa_combine_fwd", grid=(s // tm,), in_specs=[blk] * (2 * npat), out_specs=blk,
        out_shape=jax.ShapeDtypeStruct((s, SWA_WIDTH), BF16), compiler_params=_params("parallel"),
    )(*outs, *lses)


def _combine_bwd(d_out, outs, lses, bd):
    s = d_out.shape[0]
    tm = min(512, s)
    npat = len(outs)

    def body(*refs):
        d_ref, bd_ref = refs[0], refs[1 + 2 * npat]
        o_refs, l_refs = refs[1:1 + npat], refs[1 + npat:1 + 2 * npat]
        out_refs = refs[2 + 2 * npat:]
        ws = _pattern_weights([r[...] for r in l_refs])
        dov = d_ref[...]
        o = ws[0] * o_refs[0][...]
        for p in range(1, npat):
            o = o + ws[p] * o_refs[p][...]
        rd = _group_sum(dov * o, bd_ref[...])
        for p in range(npat):
            out_refs[p][...] = (ws[p] * dov).astype(BF16)
            out_refs[npat + p][...] = -ws[p] * rd

    blk = pl.BlockSpec((tm, SWA_WIDTH), lambda i: (i, 0))
    res = pl.pallas_call(
        body, name="swa_combine_bwd", grid=(s // tm,),
        in_specs=[blk] * (1 + 2 * npat) + [pl.BlockSpec((SWA_WIDTH, SWA_WIDTH), lambda i: (0, 0))],
        out_specs=[blk] * (2 * npat),
        out_shape=[jax.ShapeDtypeStruct((s, SWA_WIDTH), BF16)] * npat + [jax.ShapeDtypeStruct((s, SWA_WIDTH), F32)] * npat,
        compiler_params=_params("parallel"),
    )(d_out, *outs, *lses, bd)
    return res[:npat], res[npat:]


def _rel_bias_grad(dbs, buckets):
    npat = len(dbs)

    def body(*refs):
        db_refs, bk_refs, o_ref = refs[:npat], refs[npat:2 * npat], refs[2 * npat]
        row = lax.broadcasted_iota(jnp.int32, (REL_BUCKETS, LANE), 0)
        lane = lax.broadcasted_iota(jnp.int32, (REL_BUCKETS, LANE), 1)
        tiles = [[db_refs[p][0, h] + db_refs[p][1, h] + db_refs[p][2, h] + db_refs[p][3, h] for h in range(SWA_HEADS)]
                 for p in range(npat)]
        bks = [r[...] for r in bk_refs]

        def one_bucket(b, acc):
            for h in range(SWA_HEADS):
                tot = jnp.zeros((1, 1), F32)
                for p in range(npat):
                    sel = jnp.where(bks[p] == b, tiles[p][h], 0.0)
                    tot = tot + jnp.sum(jnp.sum(sel, axis=1, keepdims=True), axis=0, keepdims=True)
                acc = acc + jnp.where((row == b) & (lane == h), tot, 0.0)
            return acc

        o_ref[...] = lax.fori_loop(0, REL_BUCKETS, one_bucket, jnp.zeros((REL_BUCKETS, LANE), F32))

    full4 = pl.BlockSpec((4, SWA_HEADS, ATT_BQ, ATT_BK), lambda: (0, 0, 0, 0))
    full2 = pl.BlockSpec((ATT_BQ, ATT_BK), lambda: (0, 0))
    return pl.pallas_call(
        body, name="rel_bias_grad", in_specs=[full4] * npat + [full2] * npat,
        out_specs=pl.BlockSpec((REL_BUCKETS, LANE), lambda: (0, 0)),
        out_shape=jax.ShapeDtypeStruct((REL_BUCKETS, LANE), F32),
        compiler_params=pltpu.CompilerParams(vmem_limit_bytes=V7X_VMEM_LIMIT_BYTES),
    )(*dbs, *buckets)


def _swa_forward(p_pad, qw_row, kw_row, rel_bias, bd):
    q, k, v = _swa_pre_fwd(p_pad, qw_row, kw_row, bd)
    outs, lses = [], []
    for _, dil in DILATION_PATTERNS:
        o, lse = _att_fwd2(q, k, v, _bias_tiles(rel_bias, dil, True), dil)
        outs.append(o)
        lses.append(lse)
    return _combine_fwd(outs, lses), (q, k, v, outs, lses)


def _swa_backward(d_out, p_pad, qw_row, kw_row, rel_bias, bd, saved, dp_all):
    q, k, v, outs, lses = saved
    dops, cps = _combine_bwd(d_out, outs, lses, bd)
    dqs, dks, dvs, dbs, buckets = [], [], [], [], []
    for p, (_, dil) in enumerate(DILATION_PATTERNS):
        dq, db = _att_dq2(q, k, v, dops[p], lses[p], cps[p], _bias_tiles(rel_bias, dil, True), dil)
        dk, dv = _att_dkv2(q, k, v, dops[p], lses[p], cps[p], _bias_tiles(rel_bias, dil, False), dil)
        dqs.append(dq)
        dks.append(dk)
        dvs.append(dv)
        dbs.append(db)
        buckets.append(jnp.asarray(_band_tables(dil, True)[1]))
    dp, dqw, dkw = _swa_pre_bwd(dqs, dks, dvs, p_pad, qw_row, kw_row, bd, dp_all)
    return dp, dqw, dkw, _rel_bias_grad(dbs, buckets)


def _lane_row(v):
    flat = v.reshape(-1).astype(F32)
    return jnp.zeros((1, LANE), F32).at[0, :flat.shape[0]].set(flat)


W_IN_SHARD = N_IN // N_DEV
W_IN_RUNS = ((0, NAT_Z, 0), (NAT_Z, NAT_AB, OFF_Z), (NAT_AB, NAT_B, OFF_AB), (NAT_B, N_IN, OFF_B))


def _w_in_pieces(shard):
    lo, hi = shard * W_IN_SHARD, (shard + 1) * W_IN_SHARD
    out = []
    for first, last, dst in W_IN_RUNS:
        a, b = max(lo, first), min(hi, last)
        if a < b:
            out.append((a - lo, b - a, dst + a - first))
    return out


def _cols_from_slabs(w3, name):
    nd, r, wd = w3.shape
    half = nd // 2

    def body(w_ref, o_ref):
        for sh in range(half):
            o_ref[:, wd * sh:wd * (sh + 1)] = w_ref[sh]

    return pl.pallas_call(
        body, name=name, grid=(2,), in_specs=[pl.BlockSpec((half, r, wd), lambda j: (j, 0, 0))],
        out_specs=pl.BlockSpec((r, half * wd), lambda j: (0, j)),
        out_shape=jax.ShapeDtypeStruct((r, nd * wd), w3.dtype), compiler_params=_params("parallel"),
    )(w3)


def _w_in_from_slabs(w3):
    nd, r, _ = w3.shape

    def body(w_ref, o_ref):
        o_ref[:, OFF_AB:N_PAD] = jnp.zeros((r, N_PAD - OFF_AB), w3.dtype)
        for sh in range(nd):
            for src, length, dst in _w_in_pieces(sh):
                o_ref[:, dst:dst + length] = w_ref[sh, :, src:src + length]

    return pl.pallas_call(
        body, name="w_in_from_slabs", out_shape=jax.ShapeDtypeStruct((r, N_PAD), w3.dtype),
        compiler_params=pltpu.CompilerParams(vmem_limit_bytes=V7X_VMEM_LIMIT_BYTES),
    )(w3)


def _w_in_grad_slabs(dw_pad, dtype):
    r = dw_pad.shape[0]

    def body(dw_ref, o_ref):
        for sh in range(N_DEV):
            for src, length, dst in _w_in_pieces(sh):
                o_ref[sh, :, src:src + length] = dw_ref[:, dst:dst + length].astype(dtype)

    return pl.pallas_call(
        body, name="w_in_grad_slabs", out_shape=jax.ShapeDtypeStruct((N_DEV, r, W_IN_SHARD), dtype),
        compiler_params=pltpu.CompilerParams(vmem_limit_bytes=V7X_VMEM_LIMIT_BYTES),
    )(dw_pad)


LATE = ("w_out", "ffn2_w_gate", "ffn2_w_up", "ffn2_w_down")
TRANSPOSED = ("ffn1_w_gate", "ffn1_w_up", "ffn2_w_gate", "ffn2_w_up")


def _late_weights(slabs):
    return {n: g.reshape(N_DEV * g.shape[1], g.shape[2]) for n, g in zip(LATE, slabs)}


def _local_step(x, tgt, wts, small, late_shards=None):
    bd = _head_block_diag()
    conv_wt = jnp.zeros((8, QKV_A), F32).at[:CONV_WIDTH].set(small["conv_w"].T)
    alog_row, dt_row = _lane_row(small["a_log"]), _lane_row(small["dt_bias"])
    gnorm_row = small["gdn_norm_w"].reshape(1, GDN_HEAD_DIM)
    qw_row = jnp.tile(small["q_norm_w"].reshape(-1), SWA_HEADS).reshape(1, SWA_WIDTH)
    kw_row = jnp.tile(small["k_norm_w"].reshape(-1), SWA_HEADS).reshape(1, SWA_WIDTH)
    rel_bias = small["rel_bias"]
    exchange = late_shards is not None
    dw_dtype = BF16 if exchange else F32

    x1, sv1, wd1, got = _ffn_forward(
        x, small["ffn1_norm"], wts["ffn1_w_gate"], wts["ffn1_w_up"], wts.get("ffn1_w_down"), "ffn1",
        gather=[late_shards["ffn1_w_down"], late_shards["w_in"]] if exchange else ())
    win_pad = _w_in_from_slabs(got[0]) if exchange else wts["w_in_pad"]
    n2, r2 = _rms_fwd(x1, small["mix_norm"], "mix_norm")
    p_pad = _matmul([(n2, win_pad)], tm=256, tn=N_PAD, tk=D_MODEL, name="w_in")
    o_a, sva, gathered = _gdn_forward(p_pad, conv_wt, alog_row, dt_row, gnorm_row,
                                      gather=[late_shards[n] for n in LATE] if exchange else ())
    if exchange:
        wts = {**wts, **_late_weights(gathered)}
    wo_a, wo_b = wts["w_out"][:GDN_WIDTH], wts["w_out"][GDN_WIDTH:]
    o_b, svb = _swa_forward(p_pad, qw_row, kw_row, rel_bias, bd)
    x2 = _matmul([(o_a, wo_a), (o_b, wo_b)], tm=512, tn=D_MODEL, tk=GDN_WIDTH, name="w_out", res=x1)
    x3, sv2, _, _ = _ffn_forward(x2, small["ffn2_norm"], wts["ffn2_w_gate"], wts["ffn2_w_up"], wts["ffn2_w_down"], "ffn2")
    loss_row, dx3, d_final = _final_loss(x3, small["final_norm"], tgt)

    dx2, d_ffn2_norm, dwg2, dwu2, dwd2, _ = _ffn_backward(
        dx3, x2, small["ffn2_norm"], wts["ffn2_w_gate"], wts["ffn2_w_up"], wts["ffn2_w_down"], sv2, "ffn2", dw_dtype)
    d_oa = _matmul([(dx2, wo_a)], tb=True, tm=512, tn=GDN_WIDTH, tk=D_MODEL, name="w_out_da")
    d_ob = _matmul([(dx2, wo_b)], tb=True, tm=512, tn=SWA_WIDTH, tk=D_MODEL, name="w_out_db")
    dwo_a = _matmul([(o_a, dx2)], ta=True, tm=GDN_WIDTH, tn=D_MODEL, tk=2048, name="w_out_dwa", out_dtype=dw_dtype)
    dwo_b = _matmul([(o_b, dx2)], ta=True, tm=SWA_WIDTH, tn=D_MODEL, tk=2048, name="w_out_dwb", out_dtype=dw_dtype)

    late_grads = [_row_slabs(jnp.concatenate([dwo_a, dwo_b], axis=0)), dwg2, dwu2, dwd2]
    dp_all, dconv, gate_sums, d_gnorm, received = _gdn_backward(
        d_oa, p_pad, conv_wt, alog_row, dt_row, gnorm_row, sva, scatter=late_grads if exchange else ())
    if exchange:
        late_grads = received
    dp_all, dqw, dkw, d_rel = _swa_backward(d_ob, p_pad, qw_row, kw_row, rel_bias, bd, svb, dp_all)
    dw_pad = _matmul([(n2, dp_all)], ta=True, tm=D_MODEL, tn=N_PAD // 3, tk=2048, name="w_in_dw")
    dn2 = _matmul([(dp_all, win_pad)], tb=True, tm=512, tn=D_MODEL, tk=N_PAD, name="w_in_dn")
    dx1, d_mix_norm = _rms_bwd(dn2, x1, r2, small["mix_norm"], dx2, "mix_dnorm")
    d_w_in = _w_in_grad_slabs(dw_pad, dw_dtype)
    dx, d_ffn1_norm, dwg1, dwu1, dwd1, got = _ffn_backward(
        dx1, x, small["ffn1_norm"], wts["ffn1_w_gate"], wts["ffn1_w_up"], wd1, sv1, "ffn1", dw_dtype,
        scatter=[d_w_in] if exchange else None)
    if exchange:
        d_w_in = got[0]

    grads = {
        "ffn1_norm": d_ffn1_norm, "ffn1_w_gate": dwg1, "ffn1_w_up": dwu1, "ffn1_w_down": dwd1,
        "mix_norm": d_mix_norm, "w_in": d_w_in, "conv_w": dconv[:CONV_WIDTH].T,
        "a_log": gate_sums[0, :8].reshape(2, GDN_HEADS), "dt_bias": gate_sums[1, :8].reshape(2, GDN_HEADS),
        "gdn_norm_w": d_gnorm, "q_norm_w": dqw.reshape(SWA_HEADS, SWA_HEAD_DIM).sum(0, keepdims=True),
        "k_norm_w": dkw.reshape(SWA_HEADS, SWA_HEAD_DIM).sum(0, keepdims=True), "rel_bias": d_rel[:, :SWA_HEADS],
        "ffn2_norm": d_ffn2_norm, "final_norm": d_final, **dict(zip(LATE, late_grads)),
    }
    return loss_row, dx, grads


MESH_IDS = pl.DeviceIdType.MESH
ANY = pl.BlockSpec(memory_space=pl.ANY)


def _all_gather(v, name):
    m, n = v.shape

    def body(x_ref, out_ref, send_sems, recv_sems, local_sem):
        x, y, c = lax.axis_index("x"), lax.axis_index("y"), lax.axis_index("c")
        me, sibling = (x, y, c), (x, y, 1 - c)
        chips = [(1 - x, y), (x, 1 - y), (1 - x, 1 - y)]

        def rows(px, py, pc):
            return out_ref.at[pl.ds((4 * px + 2 * py + pc) * m, m), :]

        def copy(k, block, to, src=None):
            return pltpu.make_async_remote_copy(
                src_ref=rows(*block) if src is None else src, dst_ref=rows(*block),
                send_sem=send_sems.at[k], recv_sem=recv_sems.at[k], device_id=to, device_id_type=MESH_IDS)

        mine = pltpu.make_async_copy(x_ref, rows(*me), local_sem)
        mine.start()
        first = [copy(0, me, sibling, src=x_ref)]
        first += [copy(1 + j, me, (*chip, c), src=x_ref) for j, chip in enumerate(chips)]
        for cp in first:
            cp.start()
        passed = [copy(4 + j, (*chip, c), sibling) for j, chip in enumerate(chips)]
        for j, chip in enumerate(chips):
            copy(1 + j, (*chip, c), me).wait_recv()
            passed[j].start()
        copy(0, sibling, me).wait_recv()
        for j, chip in enumerate(chips):
            copy(4 + j, (*chip, 1 - c), me).wait_recv()
        for cp in first + passed:
            cp.wait_send()
        mine.wait()

    return pl.pallas_call(
        body, name=name, in_specs=[ANY], out_specs=ANY,
        out_shape=jax.ShapeDtypeStruct((N_DEV * m, n), v.dtype),
        scratch_shapes=[pltpu.SemaphoreType.DMA((7,)), pltpu.SemaphoreType.DMA((7,)), pltpu.SemaphoreType.DMA],
        compiler_params=pltpu.CompilerParams(vmem_limit_bytes=V7X_VMEM_LIMIT_BYTES),
    )(v)


def _sibling_swap(v, name):
    def body(v_ref, out_ref, send_sem, recv_sem):
        x, y, c = lax.axis_index("x"), lax.axis_index("y"), lax.axis_index("c")
        cp = pltpu.make_async_remote_copy(src_ref=v_ref, dst_ref=out_ref, send_sem=send_sem, recv_sem=recv_sem,
                                          device_id=(x, y, 1 - c), device_id_type=MESH_IDS)
        cp.start()
        cp.wait()

    return pl.pallas_call(
        body, name=name, in_specs=[ANY], out_specs=ANY, out_shape=jax.ShapeDtypeStruct(v.shape, v.dtype),
        scratch_shapes=[pltpu.SemaphoreType.DMA, pltpu.SemaphoreType.DMA],
        compiler_params=pltpu.CompilerParams(vmem_limit_bytes=V7X_VMEM_LIMIT_BYTES),
    )(v)


def _chip_exchange(t, name):
    def body(t_ref, out_ref, send_sems, recv_sems, local_sem):
        x, y, c = lax.axis_index("x"), lax.axis_index("y"), lax.axis_index("c")
        mine = 2 * x + y
        chips = [(1 - x, y), (x, 1 - y), (1 - x, 1 - y)]
        own = pltpu.make_async_copy(t_ref.at[mine], out_ref.at[mine], local_sem)
        own.start()
        copies = [pltpu.make_async_remote_copy(
            src_ref=t_ref.at[2 * px + py], dst_ref=out_ref.at[mine], send_sem=send_sems.at[j], recv_sem=recv_sems.at[j],
            device_id=(px, py, c), device_id_type=MESH_IDS) for j, (px, py) in enumerate(chips)]
        for cp in copies:
            cp.start()
        for j, (px, py) in enumerate(chips):
            pltpu.make_async_remote_copy(
                src_ref=t_ref.at[mine], dst_ref=out_ref.at[2 * px + py], send_sem=send_sems.at[j],
                recv_sem=recv_sems.at[j], device_id=(px, py, c), device_id_type=MESH_IDS).wait_recv()
        for cp in copies:
            cp.wait_send()
        own.wait()

    return pl.pallas_call(
        body, name=name, in_specs=[ANY], out_specs=ANY, out_shape=jax.ShapeDtypeStruct(t.shape, t.dtype),
        scratch_shapes=[pltpu.SemaphoreType.DMA((3,)), pltpu.SemaphoreType.DMA((3,)), pltpu.SemaphoreType.DMA],
        compiler_params=pltpu.CompilerParams(vmem_limit_bytes=V7X_VMEM_LIMIT_BYTES),
    )(t)


def _adamw(parts, w, m, v, name):
    nparts, r, n = parts.shape
    tr = r
    for cand in (256, 176, 128, 104, 64, 8):
        if r % cand == 0:
            tr = cand
            break
    bc1 = 1.0 - ADAM_B1 ** ADAM_STEP
    bc2 = 1.0 - ADAM_B2 ** ADAM_STEP

    def body(p_ref, w_ref, m_ref, v_ref, g_ref, d_ref, nm_ref, nv_ref):
        g = p_ref[0].astype(F32)
        for k in range(1, nparts):
            g = g + p_ref[k].astype(F32)
        mn = ADAM_B1 * m_ref[...] + (1.0 - ADAM_B1) * g
        vn = ADAM_B2 * v_ref[...] + (1.0 - ADAM_B2) * (g * g)
        m_hat = mn / bc1
        v_hat = vn / bc2
        g_ref[...] = g
        nm_ref[...] = mn
        nv_ref[...] = vn
        d_ref[...] = -ADAM_LR * (m_hat / (jnp.sqrt(v_hat) + ADAM_EPS) + ADAM_WD * w_ref[...])

    blk = pl.BlockSpec((tr, n), lambda i: (i, 0))
    return pl.pallas_call(
        body, name=name, grid=(r // tr,),
        in_specs=[pl.BlockSpec((nparts, tr, n), lambda i: (0, i, 0)), blk, blk, blk],
        out_specs=[blk] * 4, out_shape=[jax.ShapeDtypeStruct((r, n), F32)] * 4,
        compiler_params=_params("parallel"),
    )(parts, w, m, v)


def _mesh_place():
    x, y, c = lax.axis_index("x"), lax.axis_index("y"), lax.axis_index("c")
    return x, y, c, [(1 - x, y), (x, 1 - y), (1 - x, 1 - y)]


def _gather_phases(x_refs, out_refs, send_sems, recv_sems, local_sems):
    na = len(x_refs)

    def place():
        x, y, c, chips = _mesh_place()
        return (x, y, c), (x, y, 1 - c), chips, c

    def slab(i, px, py, pc):
        return out_refs[i].at[4 * px + 2 * py + pc]

    def copy(i, k, block, to, src=None):
        return pltpu.make_async_remote_copy(
            src_ref=slab(i, *block) if src is None else src, dst_ref=slab(i, *block),
            send_sem=send_sems.at[i, k], recv_sem=recv_sems.at[i, k], device_id=to, device_id_type=MESH_IDS)

    def own(i, me):
        return pltpu.make_async_copy(x_refs[i], slab(i, *me), local_sems.at[i])

    def sends(i, me, sibling, chips, c):
        return [copy(i, 0, me, sibling, src=x_refs[i])] + [copy(i, 1 + j, me, (*chip, c), src=x_refs[i])
                                                          for j, chip in enumerate(chips)]

    def start():
        me, sibling, chips, c = place()
        for i in range(na):
            own(i, me).start()
            for cp in sends(i, me, sibling, chips, c):
                cp.start()

    def forward():
        me, sibling, chips, c = place()
        for j, chip in enumerate(chips):
            for i in range(na):
                copy(i, 1 + j, (*chip, c), me).wait_recv()
                copy(i, 4 + j, (*chip, c), sibling).start()

    def finish():
        me, sibling, chips, c = place()
        for i in range(na):
            copy(i, 0, sibling, me).wait_recv()
        for j, chip in enumerate(chips):
            for i in range(na):
                copy(i, 4 + j, (*chip, 1 - c), me).wait_recv()
        for i in range(na):
            for cp in sends(i, me, sibling, chips, c):
                cp.wait_send()
            for j, chip in enumerate(chips):
                copy(i, 4 + j, (*chip, c), sibling).wait_send()
            own(i, me).wait()

    return start, forward, finish


def _gather_semaphores(na):
    return [pltpu.SemaphoreType.DMA((na, 7)), pltpu.SemaphoreType.DMA((na, 7)), pltpu.SemaphoreType.DMA((na,))]


def _scatter_phases(g_refs, out_refs, send_sems, recv_sems, local_sems):
    na = len(g_refs)

    def place(m):
        x, y, c = lax.axis_index("x"), lax.axis_index("y"), lax.axis_index("c")
        px = 1 - x if m & 4 else x
        py = 1 - y if m & 2 else y
        pc = 1 - c if m & 1 else c
        return 4 * x + 2 * y + c, (px, py, pc), 4 * px + 2 * py + pc

    def own(i):
        me, _, _ = place(0)
        return pltpu.make_async_copy(g_refs[i].at[me], out_refs[i].at[me], local_sems.at[i])

    def start():
        for i in range(na):
            own(i).start()
            for m in range(1, N_DEV):
                me, peer, peer_idx = place(m)
                pltpu.make_async_remote_copy(
                    src_ref=g_refs[i].at[peer_idx], dst_ref=out_refs[i].at[me], send_sem=send_sems.at[i, m - 1],
                    recv_sem=recv_sems.at[i, m - 1], device_id=peer, device_id_type=MESH_IDS).start()

    def finish():
        for i in range(na):
            for m in range(1, N_DEV):
                me, peer, peer_idx = place(m)
                cp = pltpu.make_async_remote_copy(
                    src_ref=g_refs[i].at[peer_idx], dst_ref=out_refs[i].at[peer_idx], send_sem=send_sems.at[i, m - 1],
                    recv_sem=recv_sems.at[i, m - 1], device_id=peer, device_id_type=MESH_IDS)
                cp.wait_recv()
                cp.wait_send()
            own(i).wait()

    return start, finish


def _all_gather_many(vs, name):
    na = len(vs)

    def body(*refs):
        x_refs, out_refs = refs[:na], refs[na:2 * na]
        for step in _gather_phases(x_refs, out_refs, *refs[2 * na:]):
            step()

    return pl.pallas_call(
        body, name=name, in_specs=[ANY] * na, out_specs=[ANY] * na,
        out_shape=[jax.ShapeDtypeStruct((N_DEV,) + v.shape, v.dtype) for v in vs],
        scratch_shapes=_gather_semaphores(na),
        compiler_params=pltpu.CompilerParams(vmem_limit_bytes=V7X_VMEM_LIMIT_BYTES),
    )(*vs)


def _sibling_swap_many(gs, name):
    na = len(gs)

    def body(*refs):
        g_refs, out_refs = refs[:na], refs[na:2 * na]
        send_sems, recv_sems = refs[2 * na:]
        x, y, c, _ = _mesh_place()
        copies = [pltpu.make_async_remote_copy(
            src_ref=g_refs[i].at[2 * k + 1 - c], dst_ref=out_refs[i].at[k], send_sem=send_sems.at[i, k],
            recv_sem=recv_sems.at[i, k], device_id=(x, y, 1 - c), device_id_type=MESH_IDS)
            for i in range(na) for k in range(4)]
        for cp in copies:
            cp.start()
        for cp in copies:
            cp.wait()

    return pl.pallas_call(
        body, name=name, in_specs=[ANY] * na, out_specs=[ANY] * na,
        out_shape=[jax.ShapeDtypeStruct((4,) + g.shape[1:], g.dtype) for g in gs],
        scratch_shapes=[pltpu.SemaphoreType.DMA((na, 4)), pltpu.SemaphoreType.DMA((na, 4))],
        compiler_params=pltpu.CompilerParams(vmem_limit_bytes=V7X_VMEM_LIMIT_BYTES),
    )(*gs)


def _chip_sum(g, got, core, name):
    _, r, n = g.shape

    def body(c_ref, g_ref, got_ref, o_ref):
        o_ref[...] = (g_ref[...] + got_ref[...]).astype(BF16)

    return pl.pallas_call(
        body, name=name,
        grid_spec=pltpu.PrefetchScalarGridSpec(
            num_scalar_prefetch=1, grid=(4,),
            in_specs=[pl.BlockSpec((None, r, n), lambda k, c_ref: (2 * k + c_ref[0], 0, 0)),
                      pl.BlockSpec((None, r, n), lambda k, c_ref: (k, 0, 0))],
            out_specs=pl.BlockSpec((None, r, n), lambda k, c_ref: (k, 0, 0))),
        out_shape=jax.ShapeDtypeStruct((4, r, n), BF16), compiler_params=_params("parallel"),
    )(core, g, got)


def _chip_exchange_many(ts, name):
    na = len(ts)

    def body(*refs):
        t_refs, out_refs = refs[:na], refs[na:2 * na]
        send_sems, recv_sems, local_sems = refs[2 * na:]
        x, y, c, chips = _mesh_place()
        mine = 2 * x + y
        own = [pltpu.make_async_copy(t_refs[i].at[mine], out_refs[i].at[mine], local_sems.at[i]) for i in range(na)]
        for cp in own:
            cp.start()
        copies = [pltpu.make_async_remote_copy(
            src_ref=t_refs[i].at[2 * px + py], dst_ref=out_refs[i].at[mine], send_sem=send_sems.at[i, j],
            recv_sem=recv_sems.at[i, j], device_id=(px, py, c), device_id_type=MESH_IDS)
            for j, (px, py) in enumerate(chips) for i in range(na)]
        for cp in copies:
            cp.start()
        for j, (px, py) in enumerate(chips):
            for i in range(na):
                pltpu.make_async_remote_copy(
                    src_ref=t_refs[i].at[mine], dst_ref=out_refs[i].at[2 * px + py], send_sem=send_sems.at[i, j],
                    recv_sem=recv_sems.at[i, j], device_id=(px, py, c), device_id_type=MESH_IDS).wait_recv()
        for cp in copies:
            cp.wait_send()
        for cp in own:
            cp.wait()

    return pl.pallas_call(
        body, name=name, in_specs=[ANY] * na, out_specs=[ANY] * na,
        out_shape=[jax.ShapeDtypeStruct(t.shape, t.dtype) for t in ts],
        scratch_shapes=[pltpu.SemaphoreType.DMA((na, 3)), pltpu.SemaphoreType.DMA((na, 3)), pltpu.SemaphoreType.DMA((na,))],
        compiler_params=pltpu.CompilerParams(vmem_limit_bytes=V7X_VMEM_LIMIT_BYTES),
    )(*ts)


BIG = ("ffn1_w_gate", "ffn1_w_up", "ffn1_w_down", "w_in", "w_out", "ffn2_w_gate", "ffn2_w_up", "ffn2_w_down")
COL_SHARDED = ("ffn1_w_gate", "ffn1_w_up", "w_in", "ffn2_w_gate", "ffn2_w_up")
SMALL = ("ffn1_norm", "mix_norm", "a_log", "dt_bias", "gdn_norm_w", "q_norm_w", "k_norm_w", "rel_bias",
         "ffn2_norm", "final_norm")
WEIGHTS = ("ffn1_norm", "ffn1_w_gate", "ffn1_w_up", "ffn1_w_down", "mix_norm", "w_in", "conv_w", "a_log", "dt_bias",
           "gdn_norm_w", "q_norm_w", "k_norm_w", "rel_bias", "w_out", "ffn2_norm", "ffn2_w_gate", "ffn2_w_up",
           "ffn2_w_down", "final_norm")
PACK_WIDTH = 1024
PACK_ROW_MULTIPLE = 32


def _pack(arrays, width, row_multiple):
    flat = jnp.concatenate([a.reshape(-1) for a in arrays])
    rows = -(-flat.shape[0] // width)
    rows = -(-rows // row_multiple) * row_multiple
    return jnp.pad(flat, (0, rows * width - flat.shape[0])).reshape(rows, width)


def _unpack(packed, shapes):
    flat = packed.reshape(-1)
    out, pos = [], 0
    for shp in shapes:
        size = int(np.prod(shp))
        out.append(flat[pos:pos + size].reshape(shp))
        pos += size
    return out


def _blocks_of(name, full):
    if name in COL_SHARDED:
        rows, cols = full.shape
        return full.reshape(rows, N_DEV, cols // N_DEV).transpose(1, 0, 2).reshape(N_DEV, -1)
    return full.reshape(N_DEV, -1)


def _full_of(name, blocks, shard_shape):
    rows, cols = shard_shape
    if name in COL_SHARDED:
        return blocks.reshape(N_DEV, rows, cols).transpose(1, 0, 2).reshape(rows, N_DEV * cols)
    return blocks.reshape(N_DEV * rows, cols)


def kernel(x, ffn1_norm, ffn1_w_gate, ffn1_w_up, ffn1_w_down, mix_norm, w_in, conv_w, a_log, dt_bias, gdn_norm_w, q_norm_w, k_norm_w, rel_bias, w_out, ffn2_norm, ffn2_w_gate, ffn2_w_up, ffn2_w_down, final_norm, loss_target, m_ffn1_norm, m_ffn1_w_gate, m_ffn1_w_up, m_ffn1_w_down, m_mix_norm, m_w_in, m_conv_w, m_a_log, m_dt_bias, m_gdn_norm_w, m_q_norm_w, m_k_norm_w, m_rel_bias, m_w_out, m_ffn2_norm, m_ffn2_w_gate, m_ffn2_w_up, m_ffn2_w_down, m_final_norm, v_ffn1_norm, v_ffn1_w_gate, v_ffn1_w_up, v_ffn1_w_down, v_mix_norm, v_w_in, v_conv_w, v_a_log, v_dt_bias, v_gdn_norm_w, v_q_norm_w, v_k_norm_w, v_rel_bias, v_w_out, v_ffn2_norm, v_ffn2_w_gate, v_ffn2_w_up, v_ffn2_w_down, v_final_norm):
    w = dict(ffn1_norm=ffn1_norm, ffn1_w_gate=ffn1_w_gate, ffn1_w_up=ffn1_w_up, ffn1_w_down=ffn1_w_down, mix_norm=mix_norm, w_in=w_in, conv_w=conv_w, a_log=a_log, dt_bias=dt_bias, gdn_norm_w=gdn_norm_w, q_norm_w=q_norm_w, k_norm_w=k_norm_w, rel_bias=rel_bias, w_out=w_out, ffn2_norm=ffn2_norm, ffn2_w_gate=ffn2_w_gate, ffn2_w_up=ffn2_w_up, ffn2_w_down=ffn2_w_down, final_norm=final_norm)
    mom = dict(ffn1_norm=m_ffn1_norm, ffn1_w_gate=m_ffn1_w_gate, ffn1_w_up=m_ffn1_w_up, ffn1_w_down=m_ffn1_w_down, mix_norm=m_mix_norm, w_in=m_w_in, conv_w=m_conv_w, a_log=m_a_log, dt_bias=m_dt_bias, gdn_norm_w=m_gdn_norm_w, q_norm_w=m_q_norm_w, k_norm_w=m_k_norm_w, rel_bias=m_rel_bias, w_out=m_w_out, ffn2_norm=m_ffn2_norm, ffn2_w_gate=m_ffn2_w_gate, ffn2_w_up=m_ffn2_w_up, ffn2_w_down=m_ffn2_w_down, final_norm=m_final_norm)
    var = dict(ffn1_norm=v_ffn1_norm, ffn1_w_gate=v_ffn1_w_gate, ffn1_w_up=v_ffn1_w_up, ffn1_w_down=v_ffn1_w_down, mix_norm=v_mix_norm, w_in=v_w_in, conv_w=v_conv_w, a_log=v_a_log, dt_bias=v_dt_bias, gdn_norm_w=v_gdn_norm_w, q_norm_w=v_q_norm_w, k_norm_w=v_k_norm_w, rel_bias=v_rel_bias, w_out=v_w_out, ffn2_norm=v_ffn2_norm, ffn2_w_gate=v_ffn2_w_gate, ffn2_w_up=v_ffn2_w_up, ffn2_w_down=v_ffn2_w_down, final_norm=v_final_norm)
    ix, iy, ic = lax.axis_index("x"), lax.axis_index("y"), lax.axis_index("c")
    me = 4 * ix + 2 * iy + ic

    def local(a, n):
        return jnp.swapaxes(a[0], 0, 1) if n in TRANSPOSED else a[0]

    shard = {n: local(w[n], n) for n in BIG}

    conv_shard_shape = w["conv_w"][0].shape
    conv_elems = conv_shard_shape[0] * conv_shard_shape[1]
    first = ("ffn1_w_gate", "ffn1_w_up")
    gathered = _all_gather_many([shard[n].astype(BF16) for n in first] + [_pack([w["conv_w"][0]], LANE, 8)],
                                "gather_weights")
    wts = {n: g.reshape(N_DEV * g.shape[1], g.shape[2]) for n, g in zip(first, gathered)}

    small = {n: w[n][0] if n not in ("rel_bias",) else w[n] for n in SMALL}
    small = {n: (a.reshape(1, -1) if n.endswith("norm") else a) for n, a in small.items()}
    conv_all = gathered[-1].reshape(N_DEV, -1)
    small["conv_w"] = conv_all[:, :conv_elems].reshape(N_DEV * conv_shard_shape[0], conv_shard_shape[1])
    loss_row, grad_x, grads = _local_step(x[0], loss_target[0], wts, small,
                                          late_shards={n: shard[n].astype(BF16) for n in BIG if n not in first})
    loss = lax.psum(loss_row[0, 0], ("x", "y", "c"))

    big_out = [[], [], [], []]
    for n in BIG:
        for kind, val in enumerate(_adamw(grads[n], shard[n], local(mom[n], n), local(var[n], n), f"{n}_adamw")):
            big_out[kind].append(jnp.swapaxes(val, 0, 1) if n in TRANSPOSED else val)

    small_names = SMALL + ("conv_w",)
    small_shapes = [grads[n].shape for n in small_names]
    g_small = _pack([grads[n] for n in small_names], LANE, 8)
    small_rows = g_small.shape[0]
    all_small = _all_gather(g_small, "gather_small_grads").reshape(N_DEV, small_rows, LANE)
    rep_shapes = [grads[n].shape for n in SMALL]
    zero_conv = jnp.zeros(small_shapes[-1], F32)
    ws = _pack([w[n].reshape(grads[n].shape) for n in SMALL] + [zero_conv], LANE, 8)
    ms = _pack([mom[n].reshape(grads[n].shape) for n in SMALL] + [zero_conv], LANE, 8)
    vs = _pack([var[n].reshape(grads[n].shape) for n in SMALL] + [zero_conv], LANE, 8)
    small_out = [_unpack(a, small_shapes) for a in _adamw(all_small, ws, ms, vs, "adamw_small")]
    conv_g = lax.dynamic_slice_in_dim(small_out[0][-1], me * conv_shard_shape[0], conv_shard_shape[0], axis=0)
    conv_out = [_unpack(a, [conv_shard_shape])[0] for a in _adamw(
        _pack([conv_g], LANE, 8)[None], _pack([w["conv_w"][0]], LANE, 8), _pack([mom["conv_w"][0]], LANE, 8),
        _pack([var["conv_w"][0]], LANE, 8), "adamw_conv")]

    def leaf(kind, n):
        if n in BIG:
            val = big_out[kind][BIG.index(n)]
        elif n == "conv_w":
            val = conv_out[kind]
        else:
            val = small_out[kind][SMALL.index(n)]
        return val.reshape(w[n].shape)

    outs = [loss, grad_x[None]]
    for kind in range(4):
        outs += [leaf(kind, n) for n in WEIGHTS]
    return tuple(outs)
```

```python
import functools
import math

import numpy as np
import jax
import jax.numpy as jnp
from jax import lax
from jax.experimental import pallas as pl
from jax.experimental.pallas import tpu as pltpu

F32 = jnp.float32
BF16 = jnp.bfloat16

D_MODEL = 1024
D_FF = 2816
GDN_HEADS = 4
GDN_HEAD_DIM = 128
GDN_WIDTH = 512
CONV_WIDTH = 5
CHUNK = 64
SWA_HEADS = 8
SWA_HEAD_DIM = 64
SWA_WIDTH = 512
DILATION_PATTERNS = ((128, 1), (512, 4), (2048, 16))
REL_BUCKETS = 32
REL_MAX_DISTANCE = 1024
EPS = 1e-6
NEG_BIG = -1e30
N_DEV = 8

ADAM_LR = 0.001
ADAM_B1 = 0.9
ADAM_B2 = 0.999
ADAM_EPS = 1e-08
ADAM_WD = 0.01
ADAM_STEP = 10

QKV_A = 3 * GDN_WIDTH
OFF_B = QKV_A
OFF_Z = OFF_B + 3 * SWA_WIDTH
OFF_AB = OFF_Z + GDN_WIDTH
N_PAD = OFF_AB + 256
N_IN = 3600
NAT_Z, NAT_AB, NAT_B = QKV_A, QKV_A + GDN_WIDTH, QKV_A + GDN_WIDTH + 16

V7X_VMEM_LIMIT_BYTES = 56 * 1024 * 1024
LANE = 128
ATT_BQ = 128
ATT_HALO = 64
CONV_ROWS = 256

NN = (((1,), (0,)), ((), ()))
NT = (((1,), (1,)), ((), ()))
TN = (((0,), (0,)), ((), ()))


def _params(*sem):
    return pltpu.CompilerParams(dimension_semantics=sem, vmem_limit_bytes=V7X_VMEM_LIMIT_BYTES)


def _dot(a, b, dn=NN):
    return lax.dot_general(a.astype(BF16), b.astype(BF16), dn, preferred_element_type=F32)


def _dot_hi(a, b, dn=NN):
    return lax.dot_general(a, b, dn, precision=lax.Precision.HIGHEST, preferred_element_type=F32)


def _sigmoid(x):
    return 1.0 / (1.0 + jnp.exp(-x))


class _Exchange:
    def __init__(self, kind, arrays):
        self.kind, self.arrays = kind, list(arrays)

    def out_shape(self):
        lead = (N_DEV,) if self.kind == "gather" else ()
        return [jax.ShapeDtypeStruct(lead + v.shape, v.dtype) for v in self.arrays]

    def hooks(self, in_refs, out_refs, sems, grid):
        step = pl.program_id(0)
        for axis in range(1, len(grid)):
            step = step * grid[axis] + pl.program_id(axis)
        total = math.prod(grid)
        if self.kind == "gather":
            assert total >= 4
            start, forward, finish = _gather_phases(in_refs, out_refs, *sems)
            pl.when(step == total // 2)(forward)
        else:
            assert total >= 2
            start, finish = _scatter_phases(in_refs, out_refs, *sems)
        pl.when(step == 0)(start)
        pl.when(step == total - 1)(finish)


def _pallas(body, *, name, grid, in_specs, out_specs, out_shape, args, semantics, scratch_shapes=(), exchange=None):
    n_in, n_out, n_scr = len(in_specs), len(out_specs), len(scratch_shapes)
    if exchange is None:
        res = pl.pallas_call(
            body, name=name, grid=grid, in_specs=list(in_specs), out_specs=list(out_specs), out_shape=list(out_shape),
            scratch_shapes=list(scratch_shapes), compiler_params=_params(*semantics))(*args)
        return list(res), []
    na = len(exchange.arrays)

    def carrying(*refs):
        ins, sent = refs[:n_in], refs[n_in:n_in + na]
        outs = refs[n_in + na:n_in + na + n_out]
        landed = refs[n_in + na + n_out:n_in + 2 * na + n_out]
        rest = refs[n_in + 2 * na + n_out:]
        exchange.hooks(sent, landed, rest[n_scr:], grid)
        body(*ins, *outs, *rest[:n_scr])

    res = pl.pallas_call(
        carrying, name=name, grid=grid, in_specs=list(in_specs) + [ANY] * na, out_specs=list(out_specs) + [ANY] * na,
        out_shape=list(out_shape) + exchange.out_shape(), scratch_shapes=list(scratch_shapes) + _gather_semaphores(na),
        compiler_params=_params(*(["arbitrary"] * len(grid))))(*args, *exchange.arrays)
    return list(res[:n_out]), list(res[n_out:])


def _matmul(pairs, *, ta=False, tb=False, out_dtype=F32, tm, tn, tk, name, res=None, alpha=None, shard_cols=None,
            exchange=None):
    a0, b0 = pairs[0]
    m = a0.shape[1] if ta else a0.shape[0]
    k = a0.shape[0] if ta else a0.shape[1]
    n = b0.shape[0] if tb else b0.shape[1]
    tm, tn, tk = min(tm, m), min(tn, n), min(tk, k)
    assert m % tm == 0 and n % tn == 0 and k % tk == 0, (name, m, n, k, tm, tn, tk)
    nk = k // tk
    npairs = len(pairs)
    dn = (((0 if ta else 1,), (1 if tb else 0,)), ((), ()))

    def body(*refs):
        ins = refs[:2 * npairs]
        pos = 2 * npairs
        r_ref = None
        if res is not None:
            r_ref = refs[pos]
            pos += 1
        o_ref, acc = refs[pos], refs[pos + 1]
        kk = pl.program_id(2)
        t = None
        for p in range(npairs):
            d = _dot(ins[2 * p][...], ins[2 * p + 1][...], dn)
            t = d if t is None else t + d

        if nk > 1:
            @pl.when(kk == 0)
            def _():
                acc[...] = t

            @pl.when((kk > 0) & (kk < nk - 1))
            def _():
                acc[...] += t

        @pl.when(kk == nk - 1)
        def _():
            r = acc[...] + t if nk > 1 else t
            if alpha is not None:
                r = r * alpha
            if r_ref is not None:
                r = r_ref[...] + r
            if shard_cols is None:
                o_ref[...] = r.astype(out_dtype)
            else:
                for sh in range(tn // shard_cols):
                    o_ref[sh] = r[:, sh * shard_cols:(sh + 1) * shard_cols].astype(out_dtype)

    a_spec = pl.BlockSpec((tk, tm), lambda i, j, kk: (kk, i)) if ta else pl.BlockSpec((tm, tk), lambda i, j, kk: (i, kk))
    b_spec = pl.BlockSpec((tn, tk), lambda i, j, kk: (j, kk)) if tb else pl.BlockSpec((tk, tn), lambda i, j, kk: (kk, j))
    o_spec = pl.BlockSpec((tm, tn), lambda i, j, kk: (i, j))
    in_specs = [a_spec, b_spec] * npairs + ([o_spec] if res is not None else [])
    args = [t for pr in pairs for t in pr] + ([res] if res is not None else [])
    out_spec, out_shape = o_spec, (m, n)
    if shard_cols is not None:
        assert res is None and tn % shard_cols == 0
        out_spec = pl.BlockSpec((tn // shard_cols, tm, shard_cols), lambda i, j, kk: (j, i, 0))
        out_shape = (n // shard_cols, m, shard_cols)
    (out,), exchanged = _pallas(
        body, name=name, grid=(m // tm, n // tn, nk), in_specs=in_specs, out_specs=[out_spec],
        out_shape=[jax.ShapeDtypeStruct(out_shape, out_dtype)],
        scratch_shapes=[pltpu.VMEM((tm, tn) if nk > 1 else (8, LANE), F32)],
        semantics=("parallel", "parallel", "arbitrary"), args=args, exchange=exchange)
    return out if exchange is None else (out, exchanged)


def _rms_fwd(x, w, name):
    s, d = x.shape
    tm = min(512, s)

    def body(x_ref, w_ref, n_ref, r_ref):
        xv = x_ref[...]
        r = lax.rsqrt(jnp.mean(xv * xv, axis=-1, keepdims=True) + EPS)
        n_ref[...] = (xv * r * w_ref[...]).astype(BF16)
        r_ref[...] = r

    return pl.pallas_call(
        body, name=name, grid=(s // tm,),
        in_specs=[pl.BlockSpec((tm, d), lambda i: (i, 0)), pl.BlockSpec((1, d), lambda i: (0, 0))],
        out_specs=[pl.BlockSpec((tm, d), lambda i: (i, 0)), pl.BlockSpec((tm, 1), lambda i: (i, 0))],
        out_shape=[jax.ShapeDtypeStruct((s, d), BF16), jax.ShapeDtypeStruct((s, 1), F32)],
        compiler_params=_params("parallel"),
    )(x, w)


def _rms_bwd(dn, x, r, w, dres, name, exchange=None):
    s, d = x.shape
    tm = min(512, s)

    def body(dn_ref, x_ref, r_ref, w_ref, dres_ref, dx_ref, dw_ref):
        @pl.when(pl.program_id(0) == 0)
        def _():
            dw_ref[...] = jnp.zeros_like(dw_ref)

        rv = r_ref[...]
        xhat = x_ref[...] * rv
        g = dn_ref[...]
        t = g * w_ref[...]
        dx_ref[...] = dres_ref[...] + rv * (t - xhat * jnp.mean(t * xhat, axis=-1, keepdims=True))
        dw_ref[...] += jnp.sum(g * xhat, axis=0, keepdims=True)

    row = pl.BlockSpec((tm, d), lambda i: (i, 0))
    vec = pl.BlockSpec((1, d), lambda i: (0, 0))
    (dx, dw), exchanged = _pallas(
        body, name=name, grid=(s // tm,),
        in_specs=[row, row, pl.BlockSpec((tm, 1), lambda i: (i, 0)), vec, row],
        out_specs=[row, vec],
        out_shape=[jax.ShapeDtypeStruct((s, d), F32), jax.ShapeDtypeStruct((1, d), F32)],
        semantics=("arbitrary",), args=(dn, x, r, w, dres), exchange=exchange)
    return (dx, dw) if exchange is None else (dx, dw, exchanged)


def _final_loss(x3, wf, tgt):
    s, d = x3.shape
    tm = min(512, s)

    def body(x_ref, w_ref, t_ref, loss_ref, dx_ref, dw_ref):
        @pl.when(pl.program_id(0) == 0)
        def _():
            dw_ref[...] = jnp.zeros_like(dw_ref)
            loss_ref[...] = jnp.zeros_like(loss_ref)

        xv = x_ref[...]
        wv = w_ref[...]
        r = lax.rsqrt(jnp.mean(xv * xv, axis=-1, keepdims=True) + EPS)
        xhat = xv * r
        e = xhat * wv - t_ref[...]
        part = 0.5 * jnp.sum(jnp.mean(e * e, axis=-1, keepdims=True), axis=0, keepdims=True)
        loss_ref[...] += jnp.broadcast_to(part, loss_ref.shape)
        dy = e * (1.0 / d)
        dw_ref[...] += jnp.sum(dy * xhat, axis=0, keepdims=True)
        t = dy * wv
        dx_ref[...] = r * (t - xhat * jnp.mean(t * xhat, axis=-1, keepdims=True))

    row = pl.BlockSpec((tm, d), lambda i: (i, 0))
    vec = pl.BlockSpec((1, d), lambda i: (0, 0))
    return pl.pallas_call(
        body, name="final_loss", grid=(s // tm,),
        in_specs=[row, vec, row],
        out_specs=[pl.BlockSpec((1, LANE), lambda i: (0, 0)), row, vec],
        out_shape=[jax.ShapeDtypeStruct((1, LANE), F32), jax.ShapeDtypeStruct((s, d), F32),
                   jax.ShapeDtypeStruct((1, d), F32)],
        compiler_params=_params("arbitrary"),
    )(x3, wf, tgt)


def _ffn_up(n, wg, wu, name, exchange=None):
    s, d = n.shape
    f = wg.shape[0]
    tm, tn = min(512, s), f // 2

    def body(n_ref, wg_ref, wu_ref, g_ref, u_ref, a_ref):
        nv = n_ref[...]
        g = _dot(nv, wg_ref[...], NT)
        u = _dot(nv, wu_ref[...], NT)
        g_ref[...] = g.astype(BF16)
        u_ref[...] = u.astype(BF16)
        a_ref[...] = (g * _sigmoid(g) * u).astype(BF16)

    o = pl.BlockSpec((tm, tn), lambda j, i: (i, j))
    wspec = pl.BlockSpec((tn, d), lambda j, i: (j, 0))
    return _pallas(
        body, name=name, grid=(f // tn, s // tm),
        in_specs=[pl.BlockSpec((tm, d), lambda j, i: (i, 0)), wspec, wspec],
        out_specs=[o, o, o],
        out_shape=[jax.ShapeDtypeStruct((s, f), BF16)] * 3,
        semantics=("parallel", "parallel"), args=(n, wg, wu), exchange=exchange)


def _ffn_dact(dx, wd, g, u, name, exchange=None):
    s, d = dx.shape
    f = wd.shape[0]
    tm, tn = min(512, s), f // 2

    def body(dx_ref, wd_ref, g_ref, u_ref, dg_ref, du_ref):
        da = 0.5 * _dot(dx_ref[...], wd_ref[...], NT)
        gv = g_ref[...].astype(F32)
        sg = _sigmoid(gv)
        du_ref[...] = (da * gv * sg).astype(BF16)
        dg_ref[...] = (da * u_ref[...].astype(F32) * (sg * (1.0 + gv * (1.0 - sg)))).astype(BF16)

    o = pl.BlockSpec((tm, tn), lambda j, i: (i, j))
    return _pallas(
        body, name=name, grid=(f // tn, s // tm),
        in_specs=[pl.BlockSpec((tm, d), lambda j, i: (i, 0)), pl.BlockSpec((tn, d), lambda j, i: (j, 0)), o, o],
        out_specs=[o, o],
        out_shape=[jax.ShapeDtypeStruct((s, f), BF16), jax.ShapeDtypeStruct((s, f), BF16)],
        semantics=("parallel", "parallel"), args=(dx, wd, g, u), exchange=exchange)


def _row_slabs(full):
    return full.reshape(N_DEV, full.shape[0] // N_DEV, full.shape[1])


def _ffn_forward(x, norm_w, wg, wu, wd, tag, gather=()):
    n, r = _rms_fwd(x, norm_w, f"{tag}_norm")
    (g, u, a), got = _ffn_up(n, wg, wu, f"{tag}_up", _Exchange("gather", gather) if gather else None)
    if wd is None:
        wd, got = got[0].reshape(N_DEV * got[0].shape[1], got[0].shape[2]), got[1:]
    y = _matmul([(a, wd)], tm=512, tn=1024, tk=wd.shape[0], name=f"{tag}_down", res=x, alpha=0.5)
    return y, (n, r, g, u, a), wd, got


def _ffn_backward(dy, x, norm_w, wgt, wut, wd, saved, tag, dw_dtype=F32, scatter=None):
    n, r, g, u, a = saved

    def behind(arrays):
        return _Exchange("scatter", arrays) if scatter is not None else None

    def dw(act, grad, name, alpha=None, exchange=None):
        return _matmul([(act, grad)], ta=True, tm=1408, tn=1024, tk=2048, name=name, alpha=alpha, out_dtype=dw_dtype,
                       exchange=exchange)

    dwd = _row_slabs(dw(a, dy, f"{tag}_dwd", alpha=0.5))
    (dg, du), extras = _ffn_dact(dy, wd, g, u, f"{tag}_dact", behind(scatter))
    if scatter is None:
        dwg, dwu = _row_slabs(dw(dg, n, f"{tag}_dwg")), _row_slabs(dw(du, n, f"{tag}_dwu"))
    else:
        dwg, (dwd,) = dw(dg, n, f"{tag}_dwg", exchange=behind([dwd]))
        dwu, (dwg,) = dw(du, n, f"{tag}_dwu", exchange=behind([_row_slabs(dwg)]))
        dwu = _row_slabs(dwu)
    dn = _matmul([(dg, wgt), (du, wut)], tm=512, tn=1024, tk=wgt.shape[0], name=f"{tag}_dn", exchange=behind([dwu]))
    if scatter is not None:
        dn, (dwu,) = dn
    dx, dnorm = _rms_bwd(dn, x, r, norm_w, dy, f"{tag}_dnorm")
    return dx, dnorm, dwg, dwu, dwd, extras


Q_SCALE = GDN_HEAD_DIM ** -0.5
CONV_HALO = 8


def _lane_block(s):
    return pl.BlockSpec((None, s, LANE), lambda j: (j, 0, 0))


def _conv_taps(win, w_ref, rows, sign):
    n = rows + 2 * CONV_HALO
    acc = None
    for t in range(CONV_WIDTH):
        o = sign * (t - CONV_WIDTH // 2)
        sh = win if o == 0 else pltpu.roll(win, (-o) % n, 0)
        term = sh[CONV_HALO:CONV_HALO + rows] * w_ref[t:t + 1, :]
        acc = term if acc is None else acc + term
    return acc


def _gdn_conv_fwd(p_pad, conv_wt):
    s = p_pad.shape[0]
    rows = min(CONV_ROWS, s)
    nblk = QKV_A // LANE

    def body(p_ref, w_ref, c_ref, y_ref, pad):
        j = pl.program_id(0)
        zeros = jnp.zeros((CONV_HALO, LANE), F32)
        pad[0:CONV_HALO, :] = zeros
        pad[CONV_HALO + s:2 * CONV_HALO + s, :] = zeros
        pad[CONV_HALO:CONV_HALO + s, :] = p_ref[...]

        def chunk(ci, carry):
            b = pl.multiple_of(ci * rows, rows)
            win = pad[pl.ds(b, rows + 2 * CONV_HALO), :]
            c = _conv_taps(win, w_ref, rows, 1)
            c_ref[pl.ds(b, rows), :] = c
            act = c * _sigmoid(c)
            nrm = lax.rsqrt(jnp.sum(act * act, axis=-1, keepdims=True) + EPS)
            mult = jnp.where(j < GDN_HEADS, nrm * Q_SCALE, jnp.where(j < 2 * GDN_HEADS, nrm, 1.0))
            y_ref[pl.ds(b, rows), :] = act * mult
            return carry

        lax.fori_loop(0, s // rows, chunk, 0)

    col = pl.BlockSpec((s, LANE), lambda j: (0, j))
    return pl.pallas_call(
        body, name="gdn_conv_fwd", grid=(nblk,),
        in_specs=[col, pl.BlockSpec((8, LANE), lambda j: (0, j))],
        out_specs=[_lane_block(s), _lane_block(s)],
        out_shape=[jax.ShapeDtypeStruct((nblk, s, LANE), F32), jax.ShapeDtypeStruct((nblk, s, LANE), F32)],
        scratch_shapes=[pltpu.VMEM((s + 2 * CONV_HALO, LANE), F32)],
        compiler_params=_params("parallel"),
    )(p_pad, conv_wt)


def _gdn_conv_bwd(dy_f, dy_r, c_pre, p_pad, conv_wt, dp_all):
    s = p_pad.shape[0]
    rows = min(CONV_ROWS, s)
    nblk = QKV_A // LANE

    def body(dyf_ref, dyr_ref, c_ref, p_ref, w_ref, _, dp_ref, dw_ref, ppad, dcpad):
        j = pl.program_id(0)
        zeros = jnp.zeros((CONV_HALO, LANE), F32)
        for buf in (ppad, dcpad):
            buf[0:CONV_HALO, :] = zeros
            buf[CONV_HALO + s:2 * CONV_HALO + s, :] = zeros
        ppad[CONV_HALO:CONV_HALO + s, :] = p_ref[...]

        def act_bwd(ci, carry):
            b = pl.multiple_of(ci * rows, rows)
            c = c_ref[pl.ds(b, rows), :]
            g = dyf_ref[pl.ds(b, rows), :] + dyr_ref[pl.ds(b, rows), :]
            sg = _sigmoid(c)
            act = c * sg
            nrm = lax.rsqrt(jnp.sum(act * act, axis=-1, keepdims=True) + EPS)
            yh = act * nrm
            scale = jnp.where(j < GDN_HEADS, Q_SCALE, 1.0)
            dact_qk = (scale * nrm) * (g - yh * jnp.sum(g * yh, axis=-1, keepdims=True))
            dact = jnp.where(j < 2 * GDN_HEADS, dact_qk, g)
            dcpad[pl.ds(pl.multiple_of(b + CONV_HALO, CONV_HALO), rows), :] = dact * (sg * (1.0 + c * (1.0 - sg)))
            return carry

        lax.fori_loop(0, s // rows, act_bwd, 0)
        tap = lax.broadcasted_iota(jnp.int32, (8, LANE), 0)

        def taps_bwd(ci, dw):
            b = pl.multiple_of(ci * rows, rows)
            dcw = dcpad[pl.ds(b, rows + 2 * CONV_HALO), :]
            dp_ref[pl.ds(b, rows), :] = _conv_taps(dcw, w_ref, rows, -1).astype(BF16)
            pw = ppad[pl.ds(b, rows + 2 * CONV_HALO), :]
            dc = dcw[CONV_HALO:CONV_HALO + rows]
            n = rows + 2 * CONV_HALO
            for t in range(CONV_WIDTH):
                o = t - CONV_WIDTH // 2
                sh = pw if o == 0 else pltpu.roll(pw, (-o) % n, 0)
                row = jnp.sum(dc * sh[CONV_HALO:CONV_HALO + rows], axis=0, keepdims=True)
                dw = dw + jnp.where(tap == t, row, 0.0)
            return dw

        dw_ref[...] = lax.fori_loop(0, s // rows, taps_bwd, jnp.zeros((8, LANE), F32))

    col = pl.BlockSpec((s, LANE), lambda j: (0, j))
    wspec = pl.BlockSpec((8, LANE), lambda j: (0, j))
    return pl.pallas_call(
        body, name="gdn_conv_bwd", grid=(nblk,),
        in_specs=[_lane_block(s), _lane_block(s), _lane_block(s), col, wspec, ANY],
        out_specs=[col, wspec],
        out_shape=[jax.ShapeDtypeStruct(dp_all.shape, dp_all.dtype), jax.ShapeDtypeStruct((8, QKV_A), F32)],
        scratch_shapes=[pltpu.VMEM((s + 2 * CONV_HALO, LANE), F32), pltpu.VMEM((s + 2 * CONV_HALO, LANE), F32)],
        input_output_aliases={5: 0},
        compiler_params=_params("parallel"),
    )(dy_f, dy_r, c_pre, p_pad, conv_wt, dp_all)


def _softplus(x):
    return jnp.maximum(x, 0.0) + jnp.log(1.0 + jnp.exp(-jnp.abs(x)))


def _gdn_gates_fwd(p_pad, alog_row, dt_row):
    s = p_pad.shape[0]
    tm = min(1024, s)

    def body(p_ref, al_ref, dt_ref, o_ref):
        x = p_ref[...]
        lane = lax.broadcasted_iota(jnp.int32, x.shape, 1)
        g = -jnp.exp(al_ref[...]) * _softplus(x + dt_ref[...])
        o_ref[...] = jnp.where(lane < 8, g, jnp.where(lane < 16, _sigmoid(x), 0.0))

    vec = pl.BlockSpec((1, LANE), lambda i: (0, 0))
    return pl.pallas_call(
        body, name="gdn_gates_fwd", grid=(s // tm,),
        in_specs=[pl.BlockSpec((tm, LANE), lambda i: (i, OFF_AB // LANE)), vec, vec],
        out_specs=pl.BlockSpec((tm, LANE), lambda i: (i, 0)),
        out_shape=jax.ShapeDtypeStruct((s, LANE), F32),
        compiler_params=_params("parallel"),
    )(p_pad, alog_row, dt_row)


def _gdn_gates_bwd(dgb_f, dgb_r, p_pad, gb, alog_row, dt_row, dp_all):
    s = p_pad.shape[0]
    tm = min(1024, s)
    tail = N_PAD - OFF_AB

    def body(df_ref, dr_ref, p_ref, gb_ref, al_ref, dt_ref, _, dp_ref, sum_ref):
        @pl.when(pl.program_id(0) == 0)
        def _():
            sum_ref[...] = jnp.zeros_like(sum_ref)

        x = p_ref[...]
        gbv = gb_ref[...]
        dgb = df_ref[...] + dr_ref[...]
        lane = lax.broadcasted_iota(jnp.int32, x.shape, 1)
        da = dgb * (-jnp.exp(al_ref[...])) * _sigmoid(x + dt_ref[...])
        db = dgb * gbv * (1.0 - gbv)
        dp_ref[:, 0:LANE] = jnp.where(lane < 8, da, jnp.where(lane < 16, db, 0.0)).astype(BF16)
        dp_ref[:, LANE:tail] = jnp.zeros((tm, tail - LANE), BF16)
        row = lax.broadcasted_iota(jnp.int32, (8, LANE), 0)
        lane8 = lax.broadcasted_iota(jnp.int32, (8, LANE), 1)
        d_alog = jnp.sum(dgb * gbv, axis=0, keepdims=True)
        d_dt = jnp.sum(da, axis=0, keepdims=True)
        upd = jnp.where(row == 0, d_alog, jnp.where(row == 1, d_dt, 0.0))
        sum_ref[...] += jnp.where(lane8 < 8, upd, 0.0)

    vec = pl.BlockSpec((1, LANE), lambda i: (0, 0))
    blk = pl.BlockSpec((tm, LANE), lambda i: (i, 0))
    return pl.pallas_call(
        body, name="gdn_gates_bwd", grid=(s // tm,),
        in_specs=[blk, blk, pl.BlockSpec((tm, LANE), lambda i: (i, OFF_AB // LANE)), blk, vec, vec, ANY],
        out_specs=[pl.BlockSpec((tm, tail), lambda i: (i, OFF_AB // tail)), pl.BlockSpec((8, LANE), lambda i: (0, 0))],
        out_shape=[jax.ShapeDtypeStruct(dp_all.shape, dp_all.dtype), jax.ShapeDtypeStruct((8, LANE), F32)],
        input_output_aliases={6: 0},
        compiler_params=_params("arbitrary"),
    )(dgb_f, dgb_r, p_pad, gb, alog_row, dt_row, dp_all)


def _chunk_masks(rev):
    row = lax.broadcasted_iota(jnp.int32, (CHUNK, CHUNK), 0)
    col = lax.broadcasted_iota(jnp.int32, (CHUNK, CHUNK), 1)
    le = (col >= row) if rev else (col <= row)
    strict = (col > row) if rev else (col < row)
    return le, strict, row == col


def _chunk_common(q, k, v, g, beta, gc, masks):
    le, strict, eye = masks
    gc_row = _dot_hi(jnp.ones((CHUNK, CHUNK), F32), jnp.where(eye, gc, 0.0))
    decay = jnp.where(le, jnp.exp(jnp.where(le, gc - gc_row, 0.0)), 0.0)
    eg = jnp.exp(gc)
    gl = jnp.sum(g, axis=0, keepdims=True)
    kb = k * beta
    vb = v * beta
    kbeg = kb * eg
    lm = jnp.where(strict, _dot(kb, k, NT) * decay, 0.0)
    intra = _dot(q, k, NT) * decay
    qg = q * eg
    edec = jnp.exp(gl - gc)
    kdec = k * edec
    return dict(decay=decay, eg=eg, gl=gl, kb=kb, vb=vb, kbeg=kbeg, lm=lm, intra=intra, qg=qg, edec=edec, kdec=kdec)


def _unit_lower_inverse(lm, eye):
    x = -lm
    t = eye.astype(F32) + x
    p = x
    for _ in range(5):
        p = _dot_hi(p, p)
        t = t + _dot_hi(t, p)
    return t


def _gate_lanes(rev, h):
    d = 1 if rev else 0
    return d * GDN_HEADS + h, 8 + d * GDN_HEADS + h


def _delta_fwd(y, gb, rev):
    s = y.shape[0]
    nc = s // CHUNK
    hd = GDN_HEAD_DIM

    def chunk_of(n):
        return nc - 1 - n if rev else n

    def body(q_ref, k_ref, v_ref, gb_ref, o_ref, s_all, t_all, state):
        @pl.when(pl.program_id(0) == 0)
        def _():
            state[...] = jnp.zeros_like(state)

        masks = _chunk_masks(rev)
        gbv = gb_ref[...]
        gcm = _dot_hi(masks[0].astype(F32), gbv)
        for h in range(GDN_HEADS):
            gi, bi = _gate_lanes(rev, h)
            sl = slice(h * hd, (h + 1) * hd)
            q, k, v = q_ref[:, sl], k_ref[:, sl], v_ref[:, sl]
            g, beta, gc = gbv[:, gi:gi + 1], gbv[:, bi:bi + 1], gcm[:, gi:gi + 1]
            cm = _chunk_common(q, k, v, g, beta, gc, masks)
            tinv = _unit_lower_inverse(cm["lm"], masks[2])
            u = _dot(tinv, cm["vb"])
            w = _dot(tinv, cm["kbeg"])
            st = state[h]
            v_new = u - _dot(w, st)
            o_ref[:, sl] = _dot(cm["qg"], st) + _dot(cm["intra"], v_new)
            s_all[0, h] = st
            t_all[0, h] = tinv
            state[h] = st * jnp.exp(cm["gl"]) + _dot(cm["kdec"], v_new, TN)

    def col(j):
        return pl.BlockSpec((CHUNK, GDN_WIDTH), lambda n: (chunk_of(n), j))

    return pl.pallas_call(
        body, name="delta_fwd_r" if rev else "delta_fwd_f", grid=(nc,),
        in_specs=[col(0), col(1), col(2), pl.BlockSpec((CHUNK, LANE), lambda n: (chunk_of(n), 0))],
        out_specs=[pl.BlockSpec((CHUNK, GDN_WIDTH), lambda n: (chunk_of(n), 0)),
                   pl.BlockSpec((1, GDN_HEADS, hd, hd), lambda n: (chunk_of(n), 0, 0, 0)),
                   pl.BlockSpec((1, GDN_HEADS, CHUNK, CHUNK), lambda n: (chunk_of(n), 0, 0, 0))],
        out_shape=[jax.ShapeDtypeStruct((s, GDN_WIDTH), F32),
                   jax.ShapeDtypeStruct((nc, GDN_HEADS, hd, hd), F32),
                   jax.ShapeDtypeStruct((nc, GDN_HEADS, CHUNK, CHUNK), F32)],
        scratch_shapes=[pltpu.VMEM((GDN_HEADS, hd, hd), F32)],
        compiler_params=_params("arbitrary"),
    )(y, y, y, gb)


def _delta_bwd(y, gb, do, s_all, t_all, rev):
    s = y.shape[0]
    nc = s // CHUNK
    hd = GDN_HEAD_DIM

    def chunk_of(n):
        return n if rev else nc - 1 - n

    def body(q_ref, k_ref, v_ref, gb_ref, do_ref, s_ref, t_ref, dy_ref, dgb_ref, dstate):
        @pl.when(pl.program_id(0) == 0)
        def _():
            dstate[...] = jnp.zeros_like(dstate)

        masks = _chunk_masks(rev)
        le, strict, _ = masks
        le_t = _chunk_masks(not rev)[0].astype(F32)
        gbv = gb_ref[...]
        gcm = _dot_hi(le.astype(F32), gbv)
        lane = lax.broadcasted_iota(jnp.int32, (CHUNK, LANE), 1)
        ones_cl = jnp.ones((CHUNK, LANE), F32)
        dgc_tile = jnp.zeros((CHUNK, LANE), F32)
        rest_tile = jnp.zeros((CHUNK, LANE), F32)
        for h in range(GDN_HEADS):
            gi, bi = _gate_lanes(rev, h)
            sl = slice(h * hd, (h + 1) * hd)
            q, k, v = q_ref[:, sl], k_ref[:, sl], v_ref[:, sl]
            g, beta, gc = gbv[:, gi:gi + 1], gbv[:, bi:bi + 1], gcm[:, gi:gi + 1]
            cm = _chunk_common(q, k, v, g, beta, gc, masks)
            tinv = t_ref[0, h]
            st = s_ref[0, h]
            ds_out = dstate[h]
            dov = do_ref[:, sl]
            u = _dot(tinv, cm["vb"])
            w = _dot(tinv, cm["kbeg"])
            v_new = u - _dot(w, st)
            egl = jnp.exp(cm["gl"])
            d_qg = _dot(dov, st, NT)
            d_intra = _dot(dov, v_new, NT)
            dv_new = _dot(cm["intra"], dov, TN) + _dot(cm["kdec"], ds_out)
            d_kdec = _dot(v_new, ds_out, NT)
            dstate[h] = _dot(cm["qg"], dov, TN) + egl * ds_out - _dot(w, dv_new, TN)
            dgl = egl * jnp.sum(jnp.sum(st * ds_out, axis=1, keepdims=True), axis=0, keepdims=True)
            dw = -_dot(dv_new, st, NT)
            dvb = _dot(tinv, dv_new, TN)
            dkbeg = _dot(tinv, dw, TN)
            dlm = jnp.where(strict, -(_dot(dvb, u, NT) + _dot(dkbeg, w, NT)), 0.0)
            d_a = dlm * cm["decay"]
            d_qk = d_intra * cm["decay"]
            e = dlm * cm["lm"] + d_intra * cm["intra"]
            dgc = jnp.sum(e, axis=1, keepdims=True) - _dot_hi(e, ones_cl, TN)[:, 0:1]
            dkb = _dot(d_a, k) + dkbeg * cm["eg"]
            dk = _dot(d_a, cm["kb"], TN) + _dot(d_qk, q, TN)
            dq = _dot(d_qk, k) + d_qg * cm["eg"]
            dgc = dgc + jnp.sum(d_qg * cm["qg"], axis=1, keepdims=True)
            dgc = dgc + jnp.sum(dkbeg * cm["kbeg"], axis=1, keepdims=True)
            tdec = jnp.sum(d_kdec * cm["kdec"], axis=1, keepdims=True)
            dk = dk + d_kdec * cm["edec"] + dkb * beta
            dgc = dgc - tdec
            dgl = dgl + jnp.sum(tdec, axis=0, keepdims=True)
            dbeta = jnp.sum(dvb * v, axis=1, keepdims=True) + jnp.sum(dkb * k, axis=1, keepdims=True)
            dy_ref[:, h * hd:(h + 1) * hd] = dq
            dy_ref[:, GDN_WIDTH + h * hd:GDN_WIDTH + (h + 1) * hd] = dk
            dy_ref[:, 2 * GDN_WIDTH + h * hd:2 * GDN_WIDTH + (h + 1) * hd] = dvb * beta
            dgc_tile = dgc_tile + jnp.where(lane == gi, dgc, 0.0)
            rest_tile = rest_tile + jnp.where(lane == gi, dgl, 0.0) + jnp.where(lane == bi, dbeta, 0.0)
        dgb_ref[...] = _dot_hi(le_t, dgc_tile) + rest_tile

    def col(j):
        return pl.BlockSpec((CHUNK, GDN_WIDTH), lambda n: (chunk_of(n), j))

    first = pl.BlockSpec((CHUNK, GDN_WIDTH), lambda n: (chunk_of(n), 0))
    return pl.pallas_call(
        body, name="delta_bwd_r" if rev else "delta_bwd_f", grid=(nc,),
        in_specs=[col(0), col(1), col(2), pl.BlockSpec((CHUNK, LANE), lambda n: (chunk_of(n), 0)), first,
                  pl.BlockSpec((1, GDN_HEADS, hd, hd), lambda n: (chunk_of(n), 0, 0, 0)),
                  pl.BlockSpec((1, GDN_HEADS, CHUNK, CHUNK), lambda n: (chunk_of(n), 0, 0, 0))],
        out_specs=[pl.BlockSpec((CHUNK, QKV_A), lambda n: (chunk_of(n), 0)),
                   pl.BlockSpec((CHUNK, LANE), lambda n: (chunk_of(n), 0))],
        out_shape=[jax.ShapeDtypeStruct((s, QKV_A), F32), jax.ShapeDtypeStruct((s, LANE), F32)],
        scratch_shapes=[pltpu.VMEM((GDN_HEADS, hd, hd), F32)],
        compiler_params=_params("arbitrary"),
    )(y, y, y, gb, do, s_all, t_all)


BNN = (((2,), (1,)), ((0,), (0,)))
BNT = (((2,), (2,)), ((0,), (0,)))
BTN = (((1,), (1,)), ((0,), (0,)))
NB = 2 * GDN_HEADS


def _bdot(a, b, dn=BNN):
    return lax.dot_general(a.astype(BF16), b.astype(BF16), dn, preferred_element_type=F32)


def _dot3(a, b, dn):
    ah = a.astype(BF16)
    al = (a - ah.astype(F32)).astype(BF16)
    bh = b.astype(BF16)
    bl = (b - bh.astype(F32)).astype(BF16)

    def d(x, y):
        return lax.dot_general(x, y, dn, preferred_element_type=F32)

    return d(ah, bh) + d(ah, bl) + d(al, bh)


def _both(f_val, r_val):
    return jnp.stack([f_val] * GDN_HEADS + [r_val] * GDN_HEADS)


def _head_blocks(ref_f, ref_r):
    return jnp.concatenate([ref_f[...], ref_r[...]], axis=0)


def _heads(ref_f, ref_r):
    hd = GDN_HEAD_DIM
    return jnp.stack([ref_f[:, h * hd:(h + 1) * hd] for h in range(GDN_HEADS)]
                     + [ref_r[:, h * hd:(h + 1) * hd] for h in range(GDN_HEADS)])


def _gate_cols(tile_f, tile_r, base):
    return jnp.stack([tile_f[:, base + h:base + h + 1] for h in range(GDN_HEADS)]
                     + [tile_r[:, base + GDN_HEADS + h:base + GDN_HEADS + h + 1] for h in range(GDN_HEADS)])


def _chunk_common2(q, k, v, gbf, gbr):
    mf, mr = _chunk_masks(False), _chunk_masks(True)
    le, strict = _both(mf[0], mr[0]), _both(mf[1], mr[1])
    eye = mf[2]
    gcm_f = _dot3(mf[0].astype(F32), gbf, NN)
    gcm_r = _dot3(mr[0].astype(F32), gbr, NN)
    g, beta, gc = _gate_cols(gbf, gbr, 0), _gate_cols(gbf, gbr, 8), _gate_cols(gcm_f, gcm_r, 0)
    gc_row = _dot3(jnp.ones((NB, CHUNK, CHUNK), F32), jnp.where(eye[None], gc, 0.0), BNN)
    decay = jnp.where(le, jnp.exp(jnp.where(le, gc - gc_row, 0.0)), 0.0)
    eg = jnp.exp(gc)
    gl = jnp.sum(g, axis=1, keepdims=True)
    kb = k * beta
    vb = v * beta
    kbeg = kb * eg
    lm = jnp.where(strict, _bdot(kb, k, BNT) * decay, 0.0)
    intra = _bdot(q, k, BNT) * decay
    edec = jnp.exp(gl - gc)
    return dict(strict=strict, eye=eye, beta=beta, decay=decay, eg=eg, gl=gl, kb=kb, vb=vb, kbeg=kbeg,
                lm=lm, intra=intra, qg=q * eg, edec=edec, kdec=k * edec)


def _unit_triangular_inverse(lm, eye):
    x = -lm
    t = eye[None].astype(F32) + x
    p = x
    for level in range(5):
        prod = functools.partial(_dot3, dn=BNN) if level < 2 else _bdot
        p = prod(p, p)
        t = t + prod(t, p)
    return t


def _delta_fwd2(y, gb, gather=()):
    s = y.shape[1]
    nc = s // CHUNK
    hd = GDN_HEAD_DIM
    na = len(gather)

    def body(*refs):
        qf, kf, vf, gf, qr, kr, vr, gr = refs[:8]
        of_ref, or_ref, sf_all, sr_all, tf_all, tr_all = refs[8 + na:14 + na]
        state = refs[14 + 2 * na]
        step = pl.program_id(0)

        @pl.when(step == 0)
        def _():
            state[...] = jnp.zeros_like(state)

        if na:
            start, forward, finish = _gather_phases(refs[8:8 + na], refs[14 + na:14 + 2 * na], *refs[15 + 2 * na:])
            pl.when(step == 0)(start)
            pl.when(step == nc // 2)(forward)
            pl.when(step == nc - 1)(finish)

        q, k, v = _head_blocks(qf, qr), _head_blocks(kf, kr), _head_blocks(vf, vr)
        cm = _chunk_common2(q, k, v, gf[...], gr[...])
        tinv = _unit_triangular_inverse(cm["lm"], cm["eye"])
        u = _bdot(tinv, cm["vb"])
        w = _bdot(tinv, cm["kbeg"])
        st = state[...]
        v_new = u - _bdot(w, st)
        o = _bdot(cm["qg"], st) + _bdot(cm["intra"], v_new)
        state[...] = st * jnp.exp(cm["gl"]) + _bdot(cm["kdec"], v_new, BTN)
        for h in range(GDN_HEADS):
            of_ref[:, h * hd:(h + 1) * hd] = o[h]
            or_ref[:, h * hd:(h + 1) * hd] = o[GDN_HEADS + h]
        sf_all[0] = st[:GDN_HEADS]
        sr_all[0] = st[GDN_HEADS:]
        tf_all[0] = tinv[:GDN_HEADS]
        tr_all[0] = tinv[GDN_HEADS:]

    def col(j, rev):
        return pl.BlockSpec((GDN_HEADS, CHUNK, hd), (lambda n: (j, nc - 1 - n, 0)) if rev else (lambda n: (j, n, 0)))

    def out(rev):
        return pl.BlockSpec((CHUNK, GDN_WIDTH), (lambda n: (nc - 1 - n, 0)) if rev else (lambda n: (n, 0)))

    def gate(rev):
        return pl.BlockSpec((CHUNK, LANE), (lambda n: (nc - 1 - n, 0)) if rev else (lambda n: (n, 0)))

    def per_chunk(d1, d2, rev):
        return pl.BlockSpec((1, GDN_HEADS, d1, d2), (lambda n: (nc - 1 - n, 0, 0, 0)) if rev else (lambda n: (n, 0, 0, 0)))

    assert na == 0 or nc >= 4
    res = pl.pallas_call(
        body, name="delta_fwd", grid=(nc,),
        in_specs=[col(0, False), col(1, False), col(2, False), gate(False), col(0, True), col(1, True), col(2, True), gate(True)]
        + [ANY] * na,
        out_specs=[out(False), out(True), per_chunk(hd, hd, False), per_chunk(hd, hd, True),
                   per_chunk(CHUNK, CHUNK, False), per_chunk(CHUNK, CHUNK, True)] + [ANY] * na,
        out_shape=[jax.ShapeDtypeStruct((s, GDN_WIDTH), F32)] * 2 + [jax.ShapeDtypeStruct((nc, GDN_HEADS, hd, hd), F32)] * 2
        + [jax.ShapeDtypeStruct((nc, GDN_HEADS, CHUNK, CHUNK), F32)] * 2
        + [jax.ShapeDtypeStruct((N_DEV,) + v.shape, v.dtype) for v in gather],
        scratch_shapes=[pltpu.VMEM((NB, hd, hd), F32)] + (_gather_semaphores(na) if na else []),
        compiler_params=_params("arbitrary"),
    )(y, y, y, gb, y, y, y, gb, *gather)
    return res[:6], res[6:]


def _delta_bwd2(y, gb, do, sf_all, sr_all, tf_all, tr_all, scatter=()):
    s = y.shape[1]
    nc = s // CHUNK
    hd = GDN_HEAD_DIM
    na = len(scatter)

    def body(*refs):
        qf, kf, vf, gf, dof, sf, tf, qr, kr, vr, gr, dor, sr, tr = refs[:14]
        dyf_ref, dyr_ref, dgf_ref, dgr_ref = refs[14 + na:18 + na]
        dstate = refs[18 + 2 * na]
        step = pl.program_id(0)

        @pl.when(step == 0)
        def _():
            dstate[...] = jnp.zeros_like(dstate)

        if na:
            start, finish = _scatter_phases(refs[14:14 + na], refs[18 + na:18 + 2 * na], *refs[19 + 2 * na:])
            pl.when(step == 0)(start)
            pl.when(step == nc - 1)(finish)

        q, k, v, dov = _head_blocks(qf, qr), _head_blocks(kf, kr), _head_blocks(vf, vr), _heads(dof, dor)
        cm = _chunk_common2(q, k, v, gf[...], gr[...])
        tinv = jnp.concatenate([tf[0], tr[0]], axis=0)
        st = jnp.concatenate([sf[0], sr[0]], axis=0)
        ds_out = dstate[...]
        decay, lm, intra, qg, kdec, kbeg, eg, kb, beta = (
            cm[n] for n in ("decay", "lm", "intra", "qg", "kdec", "kbeg", "eg", "kb", "beta"))
        u = _bdot(tinv, cm["vb"])
        w = _bdot(tinv, kbeg)
        v_new = u - _bdot(w, st)
        egl = jnp.exp(cm["gl"])
        d_qg = _bdot(dov, st, BNT)
        d_intra = _bdot(dov, v_new, BNT)
        dv_new = _bdot(intra, dov, BTN) + _bdot(kdec, ds_out)
        d_kdec = _bdot(v_new, ds_out, BNT)
        dstate[...] = _bdot(qg, dov, BTN) + egl * ds_out - _bdot(w, dv_new, BTN)
        dgl = egl * jnp.sum(jnp.sum(st * ds_out, axis=2, keepdims=True), axis=1, keepdims=True)
        dw = -_bdot(dv_new, st, BNT)
        dvb = _bdot(tinv, dv_new, BTN)
        dkbeg = _bdot(tinv, dw, BTN)
        dlm = jnp.where(cm["strict"], -(_bdot(dvb, u, BNT) + _bdot(dkbeg, w, BNT)), 0.0)
        d_a = dlm * decay
        d_qk = d_intra * decay
        e = dlm * lm + d_intra * intra
        colsum = _dot3(e, jnp.ones((NB, CHUNK, LANE), F32), BTN)[:, :, 0:1]
        dgc = jnp.sum(e, axis=2, keepdims=True) - colsum
        dkb = _bdot(d_a, k) + dkbeg * eg
        dk = _bdot(d_a, kb, BTN) + _bdot(d_qk, q, BTN)
        dq = _bdot(d_qk, k) + d_qg * eg
        dgc = dgc + jnp.sum(d_qg * qg, axis=2, keepdims=True) + jnp.sum(dkbeg * kbeg, axis=2, keepdims=True)
        tdec = jnp.sum(d_kdec * kdec, axis=2, keepdims=True)
        dk = dk + d_kdec * cm["edec"] + dkb * beta
        dgc = dgc - tdec
        dgl = dgl + jnp.sum(tdec, axis=1, keepdims=True)
        dbeta = jnp.sum(dvb * v, axis=2, keepdims=True) + jnp.sum(dkb * k, axis=2, keepdims=True)
        dv = dvb * beta
        lane = lax.broadcasted_iota(jnp.int32, (CHUNK, LANE), 1)
        for rev, dy_ref, dg_ref in ((False, dyf_ref, dgf_ref), (True, dyr_ref, dgr_ref)):
            dgc_tile = jnp.zeros((CHUNK, LANE), F32)
            rest = jnp.zeros((CHUNK, LANE), F32)
            for h in range(GDN_HEADS):
                b = (GDN_HEADS if rev else 0) + h
                gi, bi = _gate_lanes(rev, h)
                dgc_tile = dgc_tile + jnp.where(lane == gi, dgc[b], 0.0)
                rest = rest + jnp.where(lane == gi, dgl[b], 0.0) + jnp.where(lane == bi, dbeta[b], 0.0)
                dy_ref[h] = dq[b]
                dy_ref[GDN_HEADS + h] = dk[b]
                dy_ref[2 * GDN_HEADS + h] = dv[b]
            le_t = _chunk_masks(not rev)[0].astype(F32)
            dg_ref[...] = _dot3(le_t, dgc_tile, NN) + rest

    def col(j, rev, blocks=GDN_HEADS):
        return pl.BlockSpec((blocks, CHUNK, hd), (lambda n: (j, n, 0)) if rev else (lambda n: (j, nc - 1 - n, 0)))

    def wide(width, rev):
        return pl.BlockSpec((CHUNK, width), (lambda n: (n, 0)) if rev else (lambda n: (nc - 1 - n, 0)))

    def per_chunk(d1, d2, rev):
        return pl.BlockSpec((1, GDN_HEADS, d1, d2), (lambda n: (n, 0, 0, 0)) if rev else (lambda n: (nc - 1 - n, 0, 0, 0)))

    def side(rev):
        return [col(0, rev), col(1, rev), col(2, rev), wide(LANE, rev), wide(GDN_WIDTH, rev), per_chunk(hd, hd, rev),
                per_chunk(CHUNK, CHUNK, rev)]

    assert na == 0 or nc >= 2
    res = pl.pallas_call(
        body, name="delta_bwd", grid=(nc,),
        in_specs=side(False) + side(True) + [ANY] * na,
        out_specs=[col(0, False, 3 * GDN_HEADS), col(0, True, 3 * GDN_HEADS), wide(LANE, False), wide(LANE, True)]
        + [ANY] * na,
        out_shape=[jax.ShapeDtypeStruct((3 * GDN_HEADS, s, hd), F32)] * 2 + [jax.ShapeDtypeStruct((s, LANE), F32)] * 2
        + [jax.ShapeDtypeStruct(g.shape, g.dtype) for g in scatter],
        scratch_shapes=[pltpu.VMEM((NB, hd, hd), F32)] + (_gather_semaphores(na) if na else []),
        compiler_params=_params("arbitrary"),
    )(y, y, y, gb, do, sf_all, tf_all, y, y, y, gb, do, sr_all, tr_all, *scatter)
    return res[:4], res[4:]


def _gdn_post_fwd(o_f, o_r, p_pad, norm_row):
    s = o_f.shape[0]
    tm = min(512, s)
    hd = GDN_HEAD_DIM

    def body(of_ref, or_ref, z_ref, w_ref, out_ref, osum_ref):
        o = of_ref[...] + or_ref[...]
        osum_ref[...] = o
        z = z_ref[...]
        gate = z * _sigmoid(z)
        for h in range(GDN_HEADS):
            sl = slice(h * hd, (h + 1) * hd)
            oh = o[:, sl]
            r = lax.rsqrt(jnp.mean(oh * oh, axis=-1, keepdims=True) + EPS)
            out_ref[:, sl] = (oh * r * w_ref[...] * gate[:, sl]).astype(BF16)

    blk = pl.BlockSpec((tm, GDN_WIDTH), lambda i: (i, 0))
    return pl.pallas_call(
        body, name="gdn_post_fwd", grid=(s // tm,),
        in_specs=[blk, blk, pl.BlockSpec((tm, GDN_WIDTH), lambda i: (i, OFF_Z // GDN_WIDTH)),
                  pl.BlockSpec((1, hd), lambda i: (0, 0))],
        out_specs=[blk, blk],
        out_shape=[jax.ShapeDtypeStruct((s, GDN_WIDTH), BF16), jax.ShapeDtypeStruct((s, GDN_WIDTH), F32)],
        compiler_params=_params("parallel"),
    )(o_f, o_r, p_pad, norm_row)


def _gdn_post_bwd(d_out, o_sum, p_pad, norm_row):
    s = o_sum.shape[0]
    tm = min(512, s)
    hd = GDN_HEAD_DIM

    def body(d_ref, o_ref, z_ref, w_ref, do_ref, dz_ref, dw_ref):
        @pl.when(pl.program_id(0) == 0)
        def _():
            dw_ref[...] = jnp.zeros_like(dw_ref)

        z = z_ref[...]
        sg = _sigmoid(z)
        gate = z * sg
        dgate = sg * (1.0 + z * (1.0 - sg))
        wv = w_ref[...]
        dw = jnp.zeros((1, hd), F32)
        for h in range(GDN_HEADS):
            sl = slice(h * hd, (h + 1) * hd)
            oh = o_ref[:, sl]
            dh = d_ref[:, sl]
            r = lax.rsqrt(jnp.mean(oh * oh, axis=-1, keepdims=True) + EPS)
            ohat = oh * r
            dz_ref[:, sl] = (dh * ohat * wv * dgate[:, sl]).astype(BF16)
            drn = dh * gate[:, sl]
            t = drn * wv
            do_ref[:, sl] = r * (t - ohat * jnp.mean(t * ohat, axis=-1, keepdims=True))
            dw = dw + jnp.sum(drn * ohat, axis=0, keepdims=True)
        dw_ref[...] += dw

    blk = pl.BlockSpec((tm, GDN_WIDTH), lambda i: (i, 0))
    vec = pl.BlockSpec((1, hd), lambda i: (0, 0))
    return pl.pallas_call(
        body, name="gdn_post_bwd", grid=(s // tm,),
        in_specs=[blk, blk, pl.BlockSpec((tm, GDN_WIDTH), lambda i: (i, OFF_Z // GDN_WIDTH)), vec],
        out_specs=[blk, pl.BlockSpec((tm, GDN_WIDTH), lambda i: (i, OFF_Z // GDN_WIDTH)), vec],
        out_shape=[jax.ShapeDtypeStruct((s, GDN_WIDTH), F32), jax.ShapeDtypeStruct((s, N_PAD), BF16),
                   jax.ShapeDtypeStruct((1, hd), F32)],
        compiler_params=_params("arbitrary"),
    )(d_out, o_sum, p_pad, norm_row)


def _add2(a, b, name):
    s, w = a.shape
    tm = next(t for t in (1024, 640, 512, 256, 128, 64, 8) if s % t == 0)

    def body(a_ref, b_ref, o_ref):
        o_ref[...] = a_ref[...] + b_ref[...]

    blk = pl.BlockSpec((tm, w), lambda i: (i, 0))
    return pl.pallas_call(body, name=name, grid=(s // tm,), in_specs=[blk, blk], out_specs=blk,
                          out_shape=jax.ShapeDtypeStruct((s, w), F32), compiler_params=_params("parallel"))(a, b)


def _gdn_forward(p_pad, conv_wt, alog_row, dt_row, norm_row, gather=()):
    c_pre, y = _gdn_conv_fwd(p_pad, conv_wt)
    gb = _gdn_gates_fwd(p_pad, alog_row, dt_row)
    (o_f, o_r, s_f, s_r, t_f, t_r), gathered = _delta_fwd2(y, gb, gather)
    out, o_sum = _gdn_post_fwd(o_f, o_r, p_pad, norm_row)
    return out, (c_pre, y, gb, s_f, t_f, s_r, t_r, o_sum), gathered


def _gdn_backward(d_out, p_pad, conv_wt, alog_row, dt_row, norm_row, saved, scatter=()):
    c_pre, y, gb, s_f, t_f, s_r, t_r, o_sum = saved
    do, dp_all, dnorm = _gdn_post_bwd(d_out, o_sum, p_pad, norm_row)
    (dy_f, dy_r, dgb_f, dgb_r), received = _delta_bwd2(y, gb, do, s_f, s_r, t_f, t_r, scatter)
    dp_all, dconv = _gdn_conv_bwd(dy_f, dy_r, c_pre, p_pad, conv_wt, dp_all)
    dp_all, gate_sums = _gdn_gates_bwd(dgb_f, dgb_r, p_pad, gb, alog_row, dt_row, dp_all)
    return dp_all, dconv, gate_sums, dnorm, received


ATT_BK = ATT_BQ + 2 * ATT_HALO
SWA_SCALE = SWA_HEAD_DIM ** -0.5


def _t5_bucket(rel):
    nb = REL_BUCKETS // 2
    bucket = (rel > 0).astype(np.int32) * nb
    n = np.abs(rel)
    max_exact = nb // 2
    large = max_exact + (np.log(np.maximum(n, 1) / max_exact)
                         / math.log(REL_MAX_DISTANCE / max_exact) * (nb - max_exact)).astype(np.int32)
    large = np.minimum(large, nb - 1)
    return (bucket + np.where(n < max_exact, n, large)).astype(np.int32)


def _band_tables(dilation, queries_are_rows_of_block):
    blk = np.arange(ATT_BQ)
    band = np.arange(ATT_BK) - ATT_HALO
    if queries_are_rows_of_block:
        rel = band[None, :] - blk[:, None]
        band_idx = np.broadcast_to(np.arange(ATT_BK)[None, :], rel.shape)
    else:
        rel = blk[None, :] - band[:, None]
        band_idx = np.broadcast_to(np.arange(ATT_BK)[:, None], rel.shape)
    base = np.abs(rel) <= ATT_HALO
    not_prev = band_idx >= ATT_HALO
    not_next = band_idx < ATT_HALO + ATT_BQ
    valid = np.stack([base & not_prev, base, base & not_next, base & not_prev & not_next])
    return valid, _t5_bucket(rel * dilation)


def _bias_tiles(rel_bias, dilation, queries_are_rows_of_block):
    valid, bucket = _band_tables(dilation, queries_are_rows_of_block)
    onehot = (jnp.asarray(bucket.reshape(-1, 1)) == jnp.arange(REL_BUCKETS, dtype=jnp.int32)[None, :]).astype(F32)
    rb = jnp.dot(onehot, rel_bias.astype(F32), precision=lax.Precision.HIGHEST)
    rb = rb.T.reshape((SWA_HEADS,) + bucket.shape)
    return jnp.where(valid[:, None], rb[None], NEG_BIG).astype(F32)


def _group_sum(x, bd):
    hi = x.astype(BF16)
    lo = (x - hi.astype(F32)).astype(BF16)
    return jnp.dot(hi, bd, preferred_element_type=F32) + jnp.dot(lo, bd, preferred_element_type=F32)


def _head_block_diag():
    idx = np.arange(SWA_WIDTH) // SWA_HEAD_DIM
    return jnp.asarray(idx[:, None] == idx[None, :], BF16)


def _swa_pre_fwd(p_pad, qw_row, kw_row, bd):
    s = p_pad.shape[0]
    tm = min(512, s)
    inv = 1.0 / SWA_HEAD_DIM

    def body(q_ref, k_ref, v_ref, qw_ref, kw_ref, bd_ref, qo_ref, ko_ref, vo_ref):
        bdv = bd_ref[...]
        q = q_ref[...]
        k = k_ref[...]
        rq = lax.rsqrt(_group_sum(q * q, bdv) * inv + EPS)
        rk = lax.rsqrt(_group_sum(k * k, bdv) * inv + EPS)
        qo_ref[...] = (q * rq * qw_ref[...] * SWA_SCALE).astype(BF16)
        ko_ref[...] = (k * rk * kw_ref[...]).astype(BF16)
        vo_ref[...] = v_ref[...].astype(BF16)

    base = OFF_B // SWA_WIDTH
    blk = pl.BlockSpec((tm, SWA_WIDTH), lambda i: (i, 0))
    vec = pl.BlockSpec((1, SWA_WIDTH), lambda i: (0, 0))
    return pl.pallas_call(
        body, name="swa_pre_fwd", grid=(s // tm,),
        in_specs=[pl.BlockSpec((tm, SWA_WIDTH), lambda i: (i, base)), pl.BlockSpec((tm, SWA_WIDTH), lambda i: (i, base + 1)),
                  pl.BlockSpec((tm, SWA_WIDTH), lambda i: (i, base + 2)), vec, vec,
                  pl.BlockSpec((SWA_WIDTH, SWA_WIDTH), lambda i: (0, 0))],
        out_specs=[blk, blk, blk],
        out_shape=[jax.ShapeDtypeStruct((s, SWA_WIDTH), BF16)] * 3,
        compiler_params=_params("parallel"),
    )(p_pad, p_pad, p_pad, qw_row, kw_row, bd)


def _swa_pre_bwd(dqs, dks, dvs, p_pad, qw_row, kw_row, bd, dp_all):
    s = p_pad.shape[0]
    tm = min(256, s)
    inv = 1.0 / SWA_HEAD_DIM
    npat = len(dqs)

    def body(*refs):
        dq_refs, dk_refs, dv_refs = refs[:npat], refs[npat:2 * npat], refs[2 * npat:3 * npat]
        q_ref, k_ref, qw_ref, kw_ref, bd_ref, _, dp_ref, dqw_ref, dkw_ref = refs[3 * npat:]

        @pl.when(pl.program_id(0) == 0)
        def _():
            dqw_ref[...] = jnp.zeros_like(dqw_ref)
            dkw_ref[...] = jnp.zeros_like(dkw_ref)

        bdv = bd_ref[...]

        def norm_bwd(x, g, w, scale):
            r = lax.rsqrt(_group_sum(x * x, bdv) * inv + EPS)
            xhat = x * r
            t = g * w * scale
            dx = r * (t - xhat * (_group_sum(t * xhat, bdv) * inv))
            return dx, jnp.sum(g * scale * xhat, axis=0, keepdims=True)

        def total(rs):
            t = rs[0][...].astype(F32)
            for r in rs[1:]:
                t = t + r[...].astype(F32)
            return t

        dq, dqw = norm_bwd(q_ref[...], total(dq_refs), qw_ref[...], SWA_SCALE)
        dk, dkw = norm_bwd(k_ref[...], total(dk_refs), kw_ref[...], 1.0)
        dp_ref[:, 0:SWA_WIDTH] = dq.astype(BF16)
        dp_ref[:, SWA_WIDTH:2 * SWA_WIDTH] = dk.astype(BF16)
        dp_ref[:, 2 * SWA_WIDTH:3 * SWA_WIDTH] = total(dv_refs).astype(BF16)
        dqw_ref[...] += dqw
        dkw_ref[...] += dkw

    base = OFF_B // SWA_WIDTH
    blk = pl.BlockSpec((tm, SWA_WIDTH), lambda i: (i, 0))
    vec = pl.BlockSpec((1, SWA_WIDTH), lambda i: (0, 0))
    return pl.pallas_call(
        body, name="swa_pre_bwd", grid=(s // tm,),
        in_specs=[blk] * (3 * npat) + [pl.BlockSpec((tm, SWA_WIDTH), lambda i: (i, base)),
                                      pl.BlockSpec((tm, SWA_WIDTH), lambda i: (i, base + 1)), vec, vec,
                                      pl.BlockSpec((SWA_WIDTH, SWA_WIDTH), lambda i: (0, 0)), ANY],
        out_specs=[pl.BlockSpec((tm, 3 * SWA_WIDTH), lambda i: (i, OFF_B // (3 * SWA_WIDTH))), vec, vec],
        out_shape=[jax.ShapeDtypeStruct(dp_all.shape, dp_all.dtype), jax.ShapeDtypeStruct((1, SWA_WIDTH), F32),
                   jax.ShapeDtypeStruct((1, SWA_WIDTH), F32)],
        input_output_aliases={3 * npat + 5: 0},
        compiler_params=_params("arbitrary"),
    )(*dqs, *dks, *dvs, p_pad, p_pad, qw_row, kw_row, bd, dp_all)


def _band_specs(length):
    per = ATT_BQ // ATT_HALO
    last = length // ATT_HALO - 1
    prev = pl.BlockSpec((ATT_HALO, SWA_WIDTH), lambda r, t: (jnp.maximum(t * per - 1, 0), r))
    cur = pl.BlockSpec((ATT_BQ, SWA_WIDTH), lambda r, t: (t, r))
    nxt = pl.BlockSpec((ATT_HALO, SWA_WIDTH), lambda r, t: (jnp.minimum((t + 1) * per, last), r))
    return [prev, cur, nxt]


def _tile_variant(t, nb):
    if nb == 1:
        return 3
    return jnp.where(t == 0, 0, jnp.where(t == nb - 1, 2, 1))


def _band(refs):
    return jnp.concatenate([r[...] for r in refs], axis=0)


def _att_fwd(q, k, v, bias, dilation):
    s = q.shape[0]
    length = s // dilation
    nb = length // ATT_BQ
    view = (length, dilation * SWA_WIDTH)
    hd = SWA_HEAD_DIM

    def body(q_ref, kp, kc, kn, vp, vc, vn, b_ref, o_ref, lse_ref):
        kb, vb = _band((kp, kc, kn)), _band((vp, vc, vn))
        qv = q_ref[...]
        for h in range(SWA_HEADS):
            sl = slice(h * hd, (h + 1) * hd)
            sc = _dot(qv[:, sl], kb[:, sl], NT) + b_ref[0, h]
            m = jnp.max(sc, axis=-1, keepdims=True)
            p = jnp.exp(sc - m)
            den = jnp.sum(p, axis=-1, keepdims=True)
            o_ref[:, sl] = _dot(p, vb[:, sl]) / den
            lse_ref[:, sl] = jnp.broadcast_to(m + jnp.log(den), (ATT_BQ, hd))

    cur = pl.BlockSpec((ATT_BQ, SWA_WIDTH), lambda r, t: (t, r))
    bspec = pl.BlockSpec((1, SWA_HEADS, ATT_BQ, ATT_BK), lambda r, t: (_tile_variant(t, nb), 0, 0, 0))
    o, lse = pl.pallas_call(
        body, name=f"att_fwd_d{dilation}", grid=(dilation, nb),
        in_specs=[cur] + _band_specs(length) * 2 + [bspec],
        out_specs=[cur, cur],
        out_shape=[jax.ShapeDtypeStruct(view, F32)] * 2,
        compiler_params=_params("parallel", "parallel"),
    )(q.reshape(view), *([k.reshape(view)] * 3), *([v.reshape(view)] * 3), bias)
    return o.reshape(s, SWA_WIDTH), lse.reshape(s, SWA_WIDTH)


def _att_dq(q, k, v, dop, lse, cp, bias, dilation):
    s = q.shape[0]
    length = s // dilation
    nb = length // ATT_BQ
    view = (length, dilation * SWA_WIDTH)
    hd = SWA_HEAD_DIM

    def body(q_ref, kp, kc, kn, vp, vc, vn, do_ref, lse_ref, cp_ref, b_ref, dq_ref, db_ref):
        @pl.when((pl.program_id(0) == 0) & (pl.program_id(1) == 0))
        def _():
            db_ref[...] = jnp.zeros_like(db_ref)

        var = _tile_variant(pl.program_id(1), nb)
        kb, vb = _band((kp, kc, kn)), _band((vp, vc, vn))
        qv, dov, lsev, cpv = q_ref[...], do_ref[...], lse_ref[...], cp_ref[...]
        for h in range(SWA_HEADS):
            sl = slice(h * hd, (h + 1) * hd)
            sc = _dot(qv[:, sl], kb[:, sl], NT) + b_ref[0, h]
            p = jnp.exp(sc - lsev[:, h * hd:h * hd + 1])
            dp = _dot(dov[:, sl], vb[:, sl], NT)
            ds = p * (dp + cpv[:, h * hd:h * hd + 1])
            dq_ref[:, sl] = _dot(ds, kb[:, sl])
            db_ref[var, h] += ds

    cur = pl.BlockSpec((ATT_BQ, SWA_WIDTH), lambda r, t: (t, r))
    bspec = pl.BlockSpec((1, SWA_HEADS, ATT_BQ, ATT_BK), lambda r, t: (_tile_variant(t, nb), 0, 0, 0))
    dq, db = pl.pallas_call(
        body, name=f"att_dq_d{dilation}", grid=(dilation, nb),
        in_specs=[cur] + _band_specs(length) * 2 + [cur, cur, cur, bspec],
        out_specs=[cur, pl.BlockSpec((4, SWA_HEADS, ATT_BQ, ATT_BK), lambda r, t: (0, 0, 0, 0))],
        out_shape=[jax.ShapeDtypeStruct(view, F32), jax.ShapeDtypeStruct((4, SWA_HEADS, ATT_BQ, ATT_BK), F32)],
        compiler_params=_params("arbitrary", "arbitrary"),
    )(q.reshape(view), *([k.reshape(view)] * 3), *([v.reshape(view)] * 3), dop.reshape(view), lse.reshape(view),
      cp.reshape(view), bias)
    return dq.reshape(s, SWA_WIDTH), db


def _att_dkv(q, k, v, dop, lse, cp, bias_t, dilation):
    s = q.shape[0]
    length = s // dilation
    nb = length // ATT_BQ
    view = (length, dilation * SWA_WIDTH)
    hd = SWA_HEAD_DIM

    def body(k_ref, v_ref, qp, qc, qn, dp_, dc_, dn_, lp, lc, ln, cp_, cc_, cn_, b_ref, dk_ref, dv_ref):
        qb, dob = _band((qp, qc, qn)), _band((dp_, dc_, dn_))
        lseb, cpb = _band((lp, lc, ln)), _band((cp_, cc_, cn_))
        kv, vv = k_ref[...], v_ref[...]
        for h in range(SWA_HEADS):
            sl = slice(h * hd, (h + 1) * hd)
            sc = _dot(qb[:, sl], kv[:, sl], NT) + b_ref[0, h]
            p = jnp.exp(sc - lseb[:, h * hd:h * hd + 1])
            dv_ref[:, sl] = _dot(p, dob[:, sl], TN)
            dp = _dot(dob[:, sl], vv[:, sl], NT)
            ds = p * (dp + cpb[:, h * hd:h * hd + 1])
            dk_ref[:, sl] = _dot(ds, qb[:, sl], TN)

    cur = pl.BlockSpec((ATT_BQ, SWA_WIDTH), lambda r, t: (t, r))
    bspec = pl.BlockSpec((1, SWA_HEADS, ATT_BK, ATT_BQ), lambda r, t: (_tile_variant(t, nb), 0, 0, 0))
    dk, dv = pl.pallas_call(
        body, name=f"att_dkv_d{dilation}", grid=(dilation, nb),
        in_specs=[cur, cur] + _band_specs(length) * 4 + [bspec],
        out_specs=[cur, cur],
        out_shape=[jax.ShapeDtypeStruct(view, F32)] * 2,
        compiler_params=_params("parallel", "parallel"),
    )(k.reshape(view), v.reshape(view), *([q.reshape(view)] * 3), *([dop.reshape(view)] * 3),
      *([lse.reshape(view)] * 3), *([cp.reshape(view)] * 3), bias_t)
    return dk.reshape(s, SWA_WIDTH), dv.reshape(s, SWA_WIDTH)


N_PAIRS = SWA_HEADS // 2


def _pairs(x):
    return jnp.stack([x[:, LANE * p:LANE * (p + 1)] for p in range(N_PAIRS)])


def _per_head_rows(x):
    first = lax.broadcasted_iota(jnp.int32, x.shape, 2) < SWA_HEAD_DIM
    zero = jnp.zeros_like(x)
    return jnp.concatenate([jnp.where(first, x, zero), jnp.where(first, zero, x)], axis=1)


def _per_head_cols(x):
    return jnp.stack([jnp.concatenate([x[:, LANE * p:LANE * p + 1],
                                       x[:, LANE * p + SWA_HEAD_DIM:LANE * p + SWA_HEAD_DIM + 1]], axis=0)
                      for p in range(N_PAIRS)])


def _merge_heads(x, rows):
    first = lax.broadcasted_iota(jnp.int32, (N_PAIRS, rows, LANE), 2) < SWA_HEAD_DIM
    return jnp.where(first, x[:, :rows], x[:, rows:])


def _store_pairs(ref, x):
    for p in range(N_PAIRS):
        ref[:, LANE * p:LANE * (p + 1)] = x[p].astype(ref.dtype)


def _att_fwd2(q, k, v, bias, dilation):
    s = q.shape[0]
    length = s // dilation
    nb = length // ATT_BQ
    view = (length, dilation * SWA_WIDTH)

    def body(q_ref, kp, kc, kn, vp, vc, vn, b_ref, o_ref, lse_ref):
        kb, vb = _pairs(_band((kp, kc, kn))), _pairs(_band((vp, vc, vn)))
        qm = _per_head_rows(_pairs(q_ref[...]))
        sc = _bdot(qm, kb, BNT) + b_ref[0].reshape(N_PAIRS, 2 * ATT_BQ, ATT_BK)
        m = jnp.max(sc, axis=-1, keepdims=True)
        p = jnp.exp(sc - m)
        den = jnp.sum(p, axis=-1, keepdims=True)
        o = _bdot(p, vb) / den
        _store_pairs(o_ref, _merge_heads(o, ATT_BQ))
        lse = jnp.broadcast_to(m + jnp.log(den), (N_PAIRS, 2 * ATT_BQ, LANE))
        _store_pairs(lse_ref, _merge_heads(lse, ATT_BQ))

    cur = pl.BlockSpec((ATT_BQ, SWA_WIDTH), lambda r, t: (t, r))
    bspec = pl.BlockSpec((1, SWA_HEADS, ATT_BQ, ATT_BK), lambda r, t: (_tile_variant(t, nb), 0, 0, 0))
    o, lse = pl.pallas_call(
        body, name=f"att_fwd_d{dilation}", grid=(dilation, nb),
        in_specs=[cur] + _band_specs(length) * 2 + [bspec],
        out_specs=[cur, cur],
        out_shape=[jax.ShapeDtypeStruct(view, BF16), jax.ShapeDtypeStruct(view, F32)],
        compiler_params=_params("parallel", "parallel"),
    )(q.reshape(view), *([k.reshape(view)] * 3), *([v.reshape(view)] * 3), bias)
    return o.reshape(s, SWA_WIDTH), lse.reshape(s, SWA_WIDTH)


def _att_dq2(q, k, v, dop, lse, cp, bias, dilation):
    s = q.shape[0]
    length = s // dilation
    nb = length // ATT_BQ
    view = (length, dilation * SWA_WIDTH)

    def body(q_ref, kp, kc, kn, vp, vc, vn, do_ref, lse_ref, cp_ref, b_ref, dq_ref, db_ref):
        @pl.when((pl.program_id(0) == 0) & (pl.program_id(1) == 0))
        def _():
            db_ref[...] = jnp.zeros_like(db_ref)

        var = _tile_variant(pl.program_id(1), nb)
        kb, vb = _pairs(_band((kp, kc, kn))), _pairs(_band((vp, vc, vn)))
        qm = _per_head_rows(_pairs(q_ref[...]))
        dom = _per_head_rows(_pairs(do_ref[...]))
        sc = _bdot(qm, kb, BNT) + b_ref[0].reshape(N_PAIRS, 2 * ATT_BQ, ATT_BK)
        p = jnp.exp(sc - _per_head_cols(lse_ref[...]))
        ds = p * (_bdot(dom, vb, BNT) + _per_head_cols(cp_ref[...]))
        _store_pairs(dq_ref, _merge_heads(_bdot(ds, kb), ATT_BQ))
        db_ref[var] += ds.reshape(SWA_HEADS, ATT_BQ, ATT_BK)

    cur = pl.BlockSpec((ATT_BQ, SWA_WIDTH), lambda r, t: (t, r))
    bspec = pl.BlockSpec((1, SWA_HEADS, ATT_BQ, ATT_BK), lambda r, t: (_tile_variant(t, nb), 0, 0, 0))
    dq, db = pl.pallas_call(
        body, name=f"att_dq_d{dilation}", grid=(dilation, nb),
        in_specs=[cur] + _band_specs(length) * 2 + [cur, cur, cur, bspec],
        out_specs=[cur, pl.BlockSpec((4, SWA_HEADS, ATT_BQ, ATT_BK), lambda r, t: (0, 0, 0, 0))],
        out_shape=[jax.ShapeDtypeStruct(view, BF16), jax.ShapeDtypeStruct((4, SWA_HEADS, ATT_BQ, ATT_BK), F32)],
        compiler_params=_params("arbitrary", "arbitrary"),
    )(q.reshape(view), *([k.reshape(view)] * 3), *([v.reshape(view)] * 3), dop.reshape(view), lse.reshape(view),
      cp.reshape(view), bias)
    return dq.reshape(s, SWA_WIDTH), db


def _att_dkv2(q, k, v, dop, lse, cp, bias_t, dilation):
    s = q.shape[0]
    length = s // dilation
    nb = length // ATT_BQ
    view = (length, dilation * SWA_WIDTH)

    def body(k_ref, v_ref, qp, qc, qn, dp_, dc_, dn_, lp, lc, ln, cp_, cc_, cn_, b_ref, dk_ref, dv_ref):
        qm = _per_head_rows(_pairs(_band((qp, qc, qn))))
        dom = _per_head_rows(_pairs(_band((dp_, dc_, dn_))))
        lsev = _per_head_cols(_band((lp, lc, ln)))
        cpv = _per_head_cols(_band((cp_, cc_, cn_)))
        kv, vv = _pairs(k_ref[...]), _pairs(v_ref[...])
        sc = _bdot(qm, kv, BNT) + b_ref[0].reshape(N_PAIRS, 2 * ATT_BK, ATT_BQ)
        p = jnp.exp(sc - lsev)
        _store_pairs(dv_ref, _bdot(p, dom, BTN))
        ds = p * (_bdot(dom, vv, BNT) + cpv)
        _store_pairs(dk_ref, _bdot(ds, qm, BTN))

    cur = pl.BlockSpec((ATT_BQ, SWA_WIDTH), lambda r, t: (t, r))
    bspec = pl.BlockSpec((1, SWA_HEADS, ATT_BK, ATT_BQ), lambda r, t: (_tile_variant(t, nb), 0, 0, 0))
    dk, dv = pl.pallas_call(
        body, name=f"att_dkv_d{dilation}", grid=(dilation, nb),
        in_specs=[cur, cur] + _band_specs(length) * 4 + [bspec],
        out_specs=[cur, cur],
        out_shape=[jax.ShapeDtypeStruct(view, BF16)] * 2,
        compiler_params=_params("parallel", "parallel"),
    )(k.reshape(view), v.reshape(view), *([q.reshape(view)] * 3), *([dop.reshape(view)] * 3),
      *([lse.reshape(view)] * 3), *([cp.reshape(view)] * 3), bias_t)
    return dk.reshape(s, SWA_WIDTH), dv.reshape(s, SWA_WIDTH)


def _pattern_weights(lses):
    m = lses[0]
    for l in lses[1:]:
        m = jnp.maximum(m, l)
    es = [jnp.exp(l - m) for l in lses]
    den = es[0]
    for e in es[1:]:
        den = den + e
    return [e / den for e in es]


def _combine_fwd(outs, lses):
    s = outs[0].shape[0]
    tm = min(512, s)
    npat = len(outs)

    def body(*refs):
        ws = _pattern_weights([r[...] for r in refs[npat:2 * npat]])
        o = ws[0] * refs[0][...]
        for p in range(1, npat):
            o = o + ws[p] * refs[p][...]
        refs[2 * npat][...] = o.astype(BF16)

    blk = pl.BlockSpec((tm, SWA_WIDTH), lambda i: (i, 0))
    return pl.pallas_call(
        body, name="swa_combine_fwd", grid=(s // tm,), in_specs=[blk] * (2 * npat), out_specs=blk,
        out_shape=jax.ShapeDtypeStruct((s, SWA_WIDTH), BF16), compiler_params=_params("parallel"),
    )(*outs, *lses)


def _combine_bwd(d_out, outs, lses, bd):
    s = d_out.shape[0]
    tm = min(512, s)
    npat = len(outs)

    def body(*refs):
        d_ref, bd_ref = refs[0], refs[1 + 2 * npat]
        o_refs, l_refs = refs[1:1 + npat], refs[1 + npat:1 + 2 * npat]
        out_refs = refs[2 + 2 * npat:]
        ws = _pattern_weights([r[...] for r in l_refs])
        dov = d_ref[...]
        o = ws[0] * o_refs[0][...]
        for p in range(1, npat):
            o = o + ws[p] * o_refs[p][...]
        rd = _group_sum(dov * o, bd_ref[...])
        for p in range(npat):
            out_refs[p][...] = (ws[p] * dov).astype(BF16)
            out_refs[npat + p][...] = -ws[p] * rd

    blk = pl.BlockSpec((tm, SWA_WIDTH), lambda i: (i, 0))
    res = pl.pallas_call(
        body, name="swa_combine_bwd", grid=(s // tm,),
        in_specs=[blk] * (1 + 2 * npat) + [pl.BlockSpec((SWA_WIDTH, SWA_WIDTH), lambda i: (0, 0))],
        out_specs=[blk] * (2 * npat),
        out_shape=[jax.ShapeDtypeStruct((s, SWA_WIDTH), BF16)] * npat + [jax.ShapeDtypeStruct((s, SWA_WIDTH), F32)] * npat,
        compiler_params=_params("parallel"),
    )(d_out, *outs, *lses, bd)
    return res[:npat], res[npat:]


def _rel_bias_grad(dbs, buckets):
    npat = len(dbs)

    def body(*refs):
        db_refs, bk_refs, o_ref = refs[:npat], refs[npat:2 * npat], refs[2 * npat]
        row = lax.broadcasted_iota(jnp.int32, (REL_BUCKETS, LANE), 0)
        lane = lax.broadcasted_iota(jnp.int32, (REL_BUCKETS, LANE), 1)
        tiles = [[db_refs[p][0, h] + db_refs[p][1, h] + db_refs[p][2, h] + db_refs[p][3, h] for h in range(SWA_HEADS)]
                 for p in range(npat)]
        bks = [r[...] for r in bk_refs]

        def one_bucket(b, acc):
            for h in range(SWA_HEADS):
                tot = jnp.zeros((1, 1), F32)
                for p in range(npat):
                    sel = jnp.where(bks[p] == b, tiles[p][h], 0.0)
                    tot = tot + jnp.sum(jnp.sum(sel, axis=1, keepdims=True), axis=0, keepdims=True)
                acc = acc + jnp.where((row == b) & (lane == h), tot, 0.0)
            return acc

        o_ref[...] = lax.fori_loop(0, REL_BUCKETS, one_bucket, jnp.zeros((REL_BUCKETS, LANE), F32))

    full4 = pl.BlockSpec((4, SWA_HEADS, ATT_BQ, ATT_BK), lambda: (0, 0, 0, 0))
    full2 = pl.BlockSpec((ATT_BQ, ATT_BK), lambda: (0, 0))
    return pl.pallas_call(
        body, name="rel_bias_grad", in_specs=[full4] * npat + [full2] * npat,
        out_specs=pl.BlockSpec((REL_BUCKETS, LANE), lambda: (0, 0)),
        out_shape=jax.ShapeDtypeStruct((REL_BUCKETS, LANE), F32),
        compiler_params=pltpu.CompilerParams(vmem_limit_bytes=V7X_VMEM_LIMIT_BYTES),
    )(*dbs, *buckets)


def _swa_forward(p_pad, qw_row, kw_row, rel_bias, bd):
    q, k, v = _swa_pre_fwd(p_pad, qw_row, kw_row, bd)
    outs, lses = [], []
    for _, dil in DILATION_PATTERNS:
        o, lse = _att_fwd2(q, k, v, _bias_tiles(rel_bias, dil, True), dil)
        outs.append(o)
        lses.append(lse)
    return _combine_fwd(outs, lses), (q, k, v, outs, lses)


def _swa_backward(d_out, p_pad, qw_row, kw_row, rel_bias, bd, saved, dp_all):
    q, k, v, outs, lses = saved
    dops, cps = _combine_bwd(d_out, outs, lses, bd)
    dqs, dks, dvs, dbs, buckets = [], [], [], [], []
    for p, (_, dil) in enumerate(DILATION_PATTERNS):
        dq, db = _att_dq2(q, k, v, dops[p], lses[p], cps[p], _bias_tiles(rel_bias, dil, True), dil)
        dk, dv = _att_dkv2(q, k, v, dops[p], lses[p], cps[p], _bias_tiles(rel_bias, dil, False), dil)
        dqs.append(dq)
        dks.append(dk)
        dvs.append(dv)
        dbs.append(db)
        buckets.append(jnp.asarray(_band_tables(dil, True)[1]))
    dp, dqw, dkw = _swa_pre_bwd(dqs, dks, dvs, p_pad, qw_row, kw_row, bd, dp_all)
    return dp, dqw, dkw, _rel_bias_grad(dbs, buckets)


def _lane_row(v):
    flat = v.reshape(-1).astype(F32)
    return jnp.zeros((1, LANE), F32).at[0, :flat.shape[0]].set(flat)


W_IN_SHARD = N_IN // N_DEV
W_IN_RUNS = ((0, NAT_Z, 0), (NAT_Z, NAT_AB, OFF_Z), (NAT_AB, NAT_B, OFF_AB), (NAT_B, N_IN, OFF_B))


def _w_in_pieces(shard):
    lo, hi = shard * W_IN_SHARD, (shard + 1) * W_IN_SHARD
    out = []
    for first, last, dst in W_IN_RUNS:
        a, b = max(lo, first), min(hi, last)
        if a < b:
            out.append((a - lo, b - a, dst + a - first))
    return out


def _cols_from_slabs(w3, name):
    nd, r, wd = w3.shape
    half = nd // 2

    def body(w_ref, o_ref):
        for sh in range(half):
            o_ref[:, wd * sh:wd * (sh + 1)] = w_ref[sh]

    return pl.pallas_call(
        body, name=name, grid=(2,), in_specs=[pl.BlockSpec((half, r, wd), lambda j: (j, 0, 0))],
        out_specs=pl.BlockSpec((r, half * wd), lambda j: (0, j)),
        out_shape=jax.ShapeDtypeStruct((r, nd * wd), w3.dtype), compiler_params=_params("parallel"),
    )(w3)


def _w_in_from_slabs(w3):
    nd, r, _ = w3.shape

    def body(w_ref, o_ref):
        o_ref[:, OFF_AB:N_PAD] = jnp.zeros((r, N_PAD - OFF_AB), w3.dtype)
        for sh in range(nd):
            for src, length, dst in _w_in_pieces(sh):
                o_ref[:, dst:dst + length] = w_ref[sh, :, src:src + length]

    return pl.pallas_call(
        body, name="w_in_from_slabs", out_shape=jax.ShapeDtypeStruct((r, N_PAD), w3.dtype),
        compiler_params=pltpu.CompilerParams(vmem_limit_bytes=V7X_VMEM_LIMIT_BYTES),
    )(w3)


def _w_in_grad_slabs(dw_pad, dtype):
    r = dw_pad.shape[0]

    def body(dw_ref, o_ref):
        for sh in range(N_DEV):
            for src, length, dst in _w_in_pieces(sh):
                o_ref[sh, :, src:src + length] = dw_ref[:, dst:dst + length].astype(dtype)

    return pl.pallas_call(
        body, name="w_in_grad_slabs", out_shape=jax.ShapeDtypeStruct((N_DEV, r, W_IN_SHARD), dtype),
        compiler_params=pltpu.CompilerParams(vmem_limit_bytes=V7X_VMEM_LIMIT_BYTES),
    )(dw_pad)


LATE = ("w_out", "ffn2_w_gate", "ffn2_w_up", "ffn2_w_down")
TRANSPOSED = ("ffn1_w_gate", "ffn1_w_up", "ffn2_w_gate", "ffn2_w_up")


def _late_weights(slabs):
    return {n: g.reshape(N_DEV * g.shape[1], g.shape[2]) for n, g in zip(LATE, slabs)}


def _local_step(x, tgt, wts, small, late_shards=None):
    bd = _head_block_diag()
    conv_wt = jnp.zeros((8, QKV_A), F32).at[:CONV_WIDTH].set(small["conv_w"].T)
    alog_row, dt_row = _lane_row(small["a_log"]), _lane_row(small["dt_bias"])
    gnorm_row = small["gdn_norm_w"].reshape(1, GDN_HEAD_DIM)
    qw_row = jnp.tile(small["q_norm_w"].reshape(-1), SWA_HEADS).reshape(1, SWA_WIDTH)
    kw_row = jnp.tile(small["k_norm_w"].reshape(-1), SWA_HEADS).reshape(1, SWA_WIDTH)
    rel_bias = small["rel_bias"]
    exchange = late_shards is not None
    dw_dtype = BF16 if exchange else F32

    x1, sv1, wd1, got = _ffn_forward(
        x, small["ffn1_norm"], wts["ffn1_w_gate"], wts["ffn1_w_up"], wts.get("ffn1_w_down"), "ffn1",
        gather=[late_shards["ffn1_w_down"], late_shards["w_in"]] if exchange else ())
    win_pad = _w_in_from_slabs(got[0]) if exchange else wts["w_in_pad"]
    n2, r2 = _rms_fwd(x1, small["mix_norm"], "mix_norm")
    p_pad = _matmul([(n2, win_pad)], tm=256, tn=N_PAD, tk=D_MODEL, name="w_in")
    o_a, sva, gathered = _gdn_forward(p_pad, conv_wt, alog_row, dt_row, gnorm_row,
                                      gather=[late_shards[n] for n in LATE] if exchange else ())
    if exchange:
        wts = {**wts, **_late_weights(gathered)}
    wo_a, wo_b = wts["w_out"][:GDN_WIDTH], wts["w_out"][GDN_WIDTH:]
    o_b, svb = _swa_forward(p_pad, qw_row, kw_row, rel_bias, bd)
    x2 = _matmul([(o_a, wo_a), (o_b, wo_b)], tm=512, tn=D_MODEL, tk=GDN_WIDTH, name="w_out", res=x1)
    x3, sv2, _, _ = _ffn_forward(x2, small["ffn2_norm"], wts["ffn2_w_gate"], wts["ffn2_w_up"], wts["ffn2_w_down"], "ffn2")
    loss_row, dx3, d_final = _final_loss(x3, small["final_norm"], tgt)

    dx2, d_ffn2_norm, dwg2, dwu2, dwd2, _ = _ffn_backward(
        dx3, x2, small["ffn2_norm"], wts["ffn2_w_gate"], wts["ffn2_w_up"], wts["ffn2_w_down"], sv2, "ffn2", dw_dtype)
    d_oa = _matmul([(dx2, wo_a)], tb=True, tm=512, tn=GDN_WIDTH, tk=D_MODEL, name="w_out_da")
    d_ob = _matmul([(dx2, wo_b)], tb=True, tm=512, tn=SWA_WIDTH, tk=D_MODEL, name="w_out_db")
    dwo_a = _matmul([(o_a, dx2)], ta=True, tm=GDN_WIDTH, tn=D_MODEL, tk=2048, name="w_out_dwa", out_dtype=dw_dtype)
    dwo_b = _matmul([(o_b, dx2)], ta=True, tm=SWA_WIDTH, tn=D_MODEL, tk=2048, name="w_out_dwb", out_dtype=dw_dtype)

    late_grads = [_row_slabs(jnp.concatenate([dwo_a, dwo_b], axis=0)), dwg2, dwu2, dwd2]
    dp_all, dconv, gate_sums, d_gnorm, received = _gdn_backward(
        d_oa, p_pad, conv_wt, alog_row, dt_row, gnorm_row, sva, scatter=late_grads if exchange else ())
    if exchange:
        late_grads = received
    dp_all, dqw, dkw, d_rel = _swa_backward(d_ob, p_pad, qw_row, kw_row, rel_bias, bd, svb, dp_all)
    dw_pad = _matmul([(n2, dp_all)], ta=True, tm=D_MODEL, tn=N_PAD // 3, tk=2048, name="w_in_dw")
    dn2 = _matmul([(dp_all, win_pad)], tb=True, tm=512, tn=D_MODEL, tk=N_PAD, name="w_in_dn")
    dx1, d_mix_norm = _rms_bwd(dn2, x1, r2, small["mix_norm"], dx2, "mix_dnorm")
    d_w_in = _w_in_grad_slabs(dw_pad, dw_dtype)
    dx, d_ffn1_norm, dwg1, dwu1, dwd1, got = _ffn_backward(
        dx1, x, small["ffn1_norm"], wts["ffn1_w_gate"], wts["ffn1_w_up"], wd1, sv1, "ffn1", dw_dtype,
        scatter=[d_w_in] if exchange else None)
    if exchange:
        d_w_in = got[0]

    grads = {
        "ffn1_norm": d_ffn1_norm, "ffn1_w_gate": dwg1, "ffn1_w_up": dwu1, "ffn1_w_down": dwd1,
        "mix_norm": d_mix_norm, "w_in": d_w_in, "conv_w": dconv[:CONV_WIDTH].T,
        "a_log": gate_sums[0, :8].reshape(2, GDN_HEADS), "dt_bias": gate_sums[1, :8].reshape(2, GDN_HEADS),
        "gdn_norm_w": d_gnorm, "q_norm_w": dqw.reshape(SWA_HEADS, SWA_HEAD_DIM).sum(0, keepdims=True),
        "k_norm_w": dkw.reshape(SWA_HEADS, SWA_HEAD_DIM).sum(0, keepdims=True), "rel_bias": d_rel[:, :SWA_HEADS],
        "ffn2_norm": d_ffn2_norm, "final_norm": d_final, **dict(zip(LATE, late_grads)),
    }
    return loss_row, dx, grads


MESH_IDS = pl.DeviceIdType.MESH
ANY = pl.BlockSpec(memory_space=pl.ANY)


def _all_gather(v, name):
    m, n = v.shape

    def body(x_ref, out_ref, send_sems, recv_sems, local_sem):
        x, y, c = lax.axis_index("x"), lax.axis_index("y"), lax.axis_index("c")
        me, sibling = (x, y, c), (x, y, 1 - c)
        chips = [(1 - x, y), (x, 1 - y), (1 - x, 1 - y)]

        def rows(px, py, pc):
            return out_ref.at[pl.ds((4 * px + 2 * py + pc) * m, m), :]

        def copy(k, block, to, src=None):
            return pltpu.make_async_remote_copy(
                src_ref=rows(*block) if src is None else src, dst_ref=rows(*block),
                send_sem=send_sems.at[k], recv_sem=recv_sems.at[k], device_id=to, device_id_type=MESH_IDS)

        mine = pltpu.make_async_copy(x_ref, rows(*me), local_sem)
        mine.start()
        first = [copy(0, me, sibling, src=x_ref)]
        first += [copy(1 + j, me, (*chip, c), src=x_ref) for j, chip in enumerate(chips)]
        for cp in first:
            cp.start()
        passed = [copy(4 + j, (*chip, c), sibling) for j, chip in enumerate(chips)]
        for j, chip in enumerate(chips):
            copy(1 + j, (*chip, c), me).wait_recv()
            passed[j].start()
        copy(0, sibling, me).wait_recv()
        for j, chip in enumerate(chips):
            copy(4 + j, (*chip, 1 - c), me).wait_recv()
        for cp in first + passed:
            cp.wait_send()
        mine.wait()

    return pl.pallas_call(
        body, name=name, in_specs=[ANY], out_specs=ANY,
        out_shape=jax.ShapeDtypeStruct((N_DEV * m, n), v.dtype),
        scratch_shapes=[pltpu.SemaphoreType.DMA((7,)), pltpu.SemaphoreType.DMA((7,)), pltpu.SemaphoreType.DMA],
        compiler_params=pltpu.CompilerParams(vmem_limit_bytes=V7X_VMEM_LIMIT_BYTES),
    )(v)


def _sibling_swap(v, name):
    def body(v_ref, out_ref, send_sem, recv_sem):
        x, y, c = lax.axis_index("x"), lax.axis_index("y"), lax.axis_index("c")
        cp = pltpu.make_async_remote_copy(src_ref=v_ref, dst_ref=out_ref, send_sem=send_sem, recv_sem=recv_sem,
                                          device_id=(x, y, 1 - c), device_id_type=MESH_IDS)
        cp.start()
        cp.wait()

    return pl.pallas_call(
        body, name=name, in_specs=[ANY], out_specs=ANY, out_shape=jax.ShapeDtypeStruct(v.shape, v.dtype),
        scratch_shapes=[pltpu.SemaphoreType.DMA, pltpu.SemaphoreType.DMA],
        compiler_params=pltpu.CompilerParams(vmem_limit_bytes=V7X_VMEM_LIMIT_BYTES),
    )(v)


def _chip_exchange(t, name):
    def body(t_ref, out_ref, send_sems, recv_sems, local_sem):
        x, y, c = lax.axis_index("x"), lax.axis_index("y"), lax.axis_index("c")
        mine = 2 * x + y
        chips = [(1 - x, y), (x, 1 - y), (1 - x, 1 - y)]
        own = pltpu.make_async_copy(t_ref.at[mine], out_ref.at[mine], local_sem)
        own.start()
        copies = [pltpu.make_async_remote_copy(
            src_ref=t_ref.at[2 * px + py], dst_ref=out_ref.at[mine], send_sem=send_sems.at[j], recv_sem=recv_sems.at[j],
            device_id=(px, py, c), device_id_type=MESH_IDS) for j, (px, py) in enumerate(chips)]
        for cp in copies:
            cp.start()
        for j, (px, py) in enumerate(chips):
            pltpu.make_async_remote_copy(
                src_ref=t_ref.at[mine], dst_ref=out_ref.at[2 * px + py], send_sem=send_sems.at[j],
                recv_sem=recv_sems.at[j], device_id=(px, py, c), device_id_type=MESH_IDS).wait_recv()
        for cp in copies:
            cp.wait_send()
        own.wait()

    return pl.pallas_call(
        body, name=name, in_specs=[ANY], out_specs=ANY, out_shape=jax.ShapeDtypeStruct(t.shape, t.dtype),
        scratch_shapes=[pltpu.SemaphoreType.DMA((3,)), pltpu.SemaphoreType.DMA((3,)), pltpu.SemaphoreType.DMA],
        compiler_params=pltpu.CompilerParams(vmem_limit_bytes=V7X_VMEM_LIMIT_BYTES),
    )(t)


def _adamw(parts, w, m, v, name):
    nparts, r, n = parts.shape
    tr = r
    for cand in (256, 176, 128, 104, 64, 8):
        if r % cand == 0:
            tr = cand
            break
    bc1 = 1.0 - ADAM_B1 ** ADAM_STEP
    bc2 = 1.0 - ADAM_B2 ** ADAM_STEP

    def body(p_ref, w_ref, m_ref, v_ref, g_ref, d_ref, nm_ref, nv_ref):
        g = p_ref[0].astype(F32)
        for k in range(1, nparts):
            g = g + p_ref[k].astype(F32)
        mn = ADAM_B1 * m_ref[...] + (1.0 - ADAM_B1) * g
        vn = ADAM_B2 * v_ref[...] + (1.0 - ADAM_B2) * (g * g)
        m_hat = mn / bc1
        v_hat = vn / bc2
        g_ref[...] = g
        nm_ref[...] = mn
        nv_ref[...] = vn
        d_ref[...] = -ADAM_LR * (m_hat / (jnp.sqrt(v_hat) + ADAM_EPS) + ADAM_WD * w_ref[...])

    blk = pl.BlockSpec((tr, n), lambda i: (i, 0))
    return pl.pallas_call(
        body, name=name, grid=(r // tr,),
        in_specs=[pl.BlockSpec((nparts, tr, n), lambda i: (0, i, 0)), blk, blk, blk],
        out_specs=[blk] * 4, out_shape=[jax.ShapeDtypeStruct((r, n), F32)] * 4,
        compiler_params=_params("parallel"),
    )(parts, w, m, v)


def _mesh_place():
    x, y, c = lax.axis_index("x"), lax.axis_index("y"), lax.axis_index("c")
    return x, y, c, [(1 - x, y), (x, 1 - y), (1 - x, 1 - y)]


def _gather_phases(x_refs, out_refs, send_sems, recv_sems, local_sems):
    na = len(x_refs)

    def place():
        x, y, c, chips = _mesh_place()
        return (x, y, c), (x, y, 1 - c), chips, c

    def slab(i, px, py, pc):
        return out_refs[i].at[4 * px + 2 * py + pc]

    def copy(i, k, block, to, src=None):
        return pltpu.make_async_remote_copy(
            src_ref=slab(i, *block) if src is None else src, dst_ref=slab(i, *block),
            send_sem=send_sems.at[i, k], recv_sem=recv_sems.at[i, k], device_id=to, device_id_type=MESH_IDS)

    def own(i, me):
        return pltpu.make_async_copy(x_refs[i], slab(i, *me), local_sems.at[i])

    def sends(i, me, sibling, chips, c):
        return [copy(i, 0, me, sibling, src=x_refs[i])] + [copy(i, 1 + j, me, (*chip, c), src=x_refs[i])
                                                          for j, chip in enumerate(chips)]

    def start():
        me, sibling, chips, c = place()
        for i in range(na):
            own(i, me).start()
            for cp in sends(i, me, sibling, chips, c):
                cp.start()

    def forward():
        me, sibling, chips, c = place()
        for j, chip in enumerate(chips):
            for i in range(na):
                copy(i, 1 + j, (*chip, c), me).wait_recv()
                copy(i, 4 + j, (*chip, c), sibling).start()

    def finish():
        me, sibling, chips, c = place()
        for i in range(na):
            copy(i, 0, sibling, me).wait_recv()
        for j, chip in enumerate(chips):
            for i in range(na):
                copy(i, 4 + j, (*chip, 1 - c), me).wait_recv()
        for i in range(na):
            for cp in sends(i, me, sibling, chips, c):
                cp.wait_send()
            for j, chip in enumerate(chips):
                copy(i, 4 + j, (*chip, c), sibling).wait_send()
            own(i, me).wait()

    return start, forward, finish


def _gather_semaphores(na):
    return [pltpu.SemaphoreType.DMA((na, 7)), pltpu.SemaphoreType.DMA((na, 7)), pltpu.SemaphoreType.DMA((na,))]


def _scatter_phases(g_refs, out_refs, send_sems, recv_sems, local_sems):
    na = len(g_refs)

    def place(m):
        x, y, c = lax.axis_index("x"), lax.axis_index("y"), lax.axis_index("c")
        px = 1 - x if m & 4 else x
        py = 1 - y if m & 2 else y
        pc = 1 - c if m & 1 else c
        return 4 * x + 2 * y + c, (px, py, pc), 4 * px + 2 * py + pc

    def own(i):
        me, _, _ = place(0)
        return pltpu.make_async_copy(g_refs[i].at[me], out_refs[i].at[me], local_sems.at[i])

    def start():
        for i in range(na):
            own(i).start()
            for m in range(1, N_DEV):
                me, peer, peer_idx = place(m)
                pltpu.make_async_remote_copy(
                    src_ref=g_refs[i].at[peer_idx], dst_ref=out_refs[i].at[me], send_sem=send_sems.at[i, m - 1],
                    recv_sem=recv_sems.at[i, m - 1], device_id=peer, device_id_type=MESH_IDS).start()

    def finish():
        for i in range(na):
            for m in range(1, N_DEV):
                me, peer, peer_idx = place(m)
                cp = pltpu.make_async_remote_copy(
                    src_ref=g_refs[i].at[peer_idx], dst_ref=out_refs[i].at[peer_idx], send_sem=send_sems.at[i, m - 1],
                    recv_sem=recv_sems.at[i, m - 1], device_id=peer, device_id_type=MESH_IDS)
                cp.wait_recv()
                cp.wait_send()
            own(i).wait()

    return start, finish


def _all_gather_many(vs, name):
    na = len(vs)

    def body(*refs):
        x_refs, out_refs = refs[:na], refs[na:2 * na]
        for step in _gather_phases(x_refs, out_refs, *refs[2 * na:]):
            step()

    return pl.pallas_call(
        body, name=name, in_specs=[ANY] * na, out_specs=[ANY] * na,
        out_shape=[jax.ShapeDtypeStruct((N_DEV,) + v.shape, v.dtype) for v in vs],
        scratch_shapes=_gather_semaphores(na),
        compiler_params=pltpu.CompilerParams(vmem_limit_bytes=V7X_VMEM_LIMIT_BYTES),
    )(*vs)


def _sibling_swap_many(gs, name):
    na = len(gs)

    def body(*refs):
        g_refs, out_refs = refs[:na], refs[na:2 * na]
        send_sems, recv_sems = refs[2 * na:]
        x, y, c, _ = _mesh_place()
        copies = [pltpu.make_async_remote_copy(
            src_ref=g_refs[i].at[2 * k + 1 - c], dst_ref=out_refs[i].at[k], send_sem=send_sems.at[i, k],
            recv_sem=recv_sems.at[i, k], device_id=(x, y, 1 - c), device_id_type=MESH_IDS)
            for i in range(na) for k in range(4)]
        for cp in copies:
            cp.start()
        for cp in copies:
            cp.wait()

    return pl.pallas_call(
        body, name=name, in_specs=[ANY] * na, out_specs=[ANY] * na,
        out_shape=[jax.ShapeDtypeStruct((4,) + g.shape[1:], g.dtype) for g in gs],
        scratch_shapes=[pltpu.SemaphoreType.DMA((na, 4)), pltpu.SemaphoreType.DMA((na, 4))],
        compiler_params=pltpu.CompilerParams(vmem_limit_bytes=V7X_VMEM_LIMIT_BYTES),
    )(*gs)


def _chip_sum(g, got, core, name):
    _, r, n = g.shape

    def body(c_ref, g_ref, got_ref, o_ref):
        o_ref[...] = (g_ref[...] + got_ref[...]).astype(BF16)

    return pl.pallas_call(
        body, name=name,
        grid_spec=pltpu.PrefetchScalarGridSpec(
            num_scalar_prefetch=1, grid=(4,),
            in_specs=[pl.BlockSpec((None, r, n), lambda k, c_ref: (2 * k + c_ref[0], 0, 0)),
                      pl.BlockSpec((None, r, n), lambda k, c_ref: (k, 0, 0))],
            out_specs=pl.BlockSpec((None, r, n), lambda k, c_ref: (k, 0, 0))),
        out_shape=jax.ShapeDtypeStruct((4, r, n), BF16), compiler_params=_params("parallel"),
    )(core, g, got)


def _chip_exchange_many(ts, name):
    na = len(ts)

    def body(*refs):
        t_refs, out_refs = refs[:na], refs[na:2 * na]
        send_sems, recv_sems, local_sems = refs[2 * na:]
        x, y, c, chips = _mesh_place()
        mine = 2 * x + y
        own = [pltpu.make_async_copy(t_refs[i].at[mine], out_refs[i].at[mine], local_sems.at[i]) for i in range(na)]
        for cp in own:
            cp.start()
        copies = [pltpu.make_async_remote_copy(
            src_ref=t_refs[i].at[2 * px + py], dst_ref=out_refs[i].at[mine], send_sem=send_sems.at[i, j],
            recv_sem=recv_sems.at[i, j], device_id=(px, py, c), device_id_type=MESH_IDS)
            for j, (px, py) in enumerate(chips) for i in range(na)]
        for cp in copies:
            cp.start()
        for j, (px, py) in enumerate(chips):
            for i in range(na):
                pltpu.make_async_remote_copy(
                    src_ref=t_refs[i].at[mine], dst_ref=out_refs[i].at[2 * px + py], send_sem=send_sems.at[i, j],
                    recv_sem=recv_sems.at[i, j], device_id=(px, py, c), device_id_type=MESH_IDS).wait_recv()
        for cp in copies:
            cp.wait_send()
        for cp in own:
            cp.wait()

    return pl.pallas_call(
        body, name=name, in_specs=[ANY] * na, out_specs=[ANY] * na,
        out_shape=[jax.ShapeDtypeStruct(t.shape, t.dtype) for t in ts],
        scratch_shapes=[pltpu.SemaphoreType.DMA((na, 3)), pltpu.SemaphoreType.DMA((na, 3)), pltpu.SemaphoreType.DMA((na,))],
        compiler_params=pltpu.CompilerParams(vmem_limit_bytes=V7X_VMEM_LIMIT_BYTES),
    )(*ts)


BIG = ("ffn1_w_gate", "ffn1_w_up", "ffn1_w_down", "w_in", "w_out", "ffn2_w_gate", "ffn2_w_up", "ffn2_w_down")
COL_SHARDED = ("ffn1_w_gate", "ffn1_w_up", "w_in", "ffn2_w_gate", "ffn2_w_up")
SMALL = ("ffn1_norm", "mix_norm", "a_log", "dt_bias", "gdn_norm_w", "q_norm_w", "k_norm_w", "rel_bias",
         "ffn2_norm", "final_norm")
WEIGHTS = ("ffn1_norm", "ffn1_w_gate", "ffn1_w_up", "ffn1_w_down", "mix_norm", "w_in", "conv_w", "a_log", "dt_bias",
           "gdn_norm_w", "q_norm_w", "k_norm_w", "rel_bias", "w_out", "ffn2_norm", "ffn2_w_gate", "ffn2_w_up",
           "ffn2_w_down", "final_norm")
PACK_WIDTH = 1024
PACK_ROW_MULTIPLE = 32


def _pack(arrays, width, row_multiple):
    flat = jnp.concatenate([a.reshape(-1) for a in arrays])
    rows = -(-flat.shape[0] // width)
    rows = -(-rows // row_multiple) * row_multiple
    return jnp.pad(flat, (0, rows * width - flat.shape[0])).reshape(rows, width)


def _unpack(packed, shapes):
    flat = packed.reshape(-1)
    out, pos = [], 0
    for shp in shapes:
        size = int(np.prod(shp))
        out.append(flat[pos:pos + size].reshape(shp))
        pos += size
    return out


def _blocks_of(name, full):
    if name in COL_SHARDED:
        rows, cols = full.shape
        return full.reshape(rows, N_DEV, cols // N_DEV).transpose(1, 0, 2).reshape(N_DEV, -1)
    return full.reshape(N_DEV, -1)


def _full_of(name, blocks, shard_shape):
    rows, cols = shard_shape
    if name in COL_SHARDED:
        return blocks.reshape(N_DEV, rows, cols).transpose(1, 0, 2).reshape(rows, N_DEV * cols)
    return blocks.reshape(N_DEV * rows, cols)


def kernel(x, ffn1_norm, ffn1_w_gate, ffn1_w_up, ffn1_w_down, mix_norm, w_in, conv_w, a_log, dt_bias, gdn_norm_w, q_norm_w, k_norm_w, rel_bias, w_out, ffn2_norm, ffn2_w_gate, ffn2_w_up, ffn2_w_down, final_norm, loss_target, m_ffn1_norm, m_ffn1_w_gate, m_ffn1_w_up, m_ffn1_w_down, m_mix_norm, m_w_in, m_conv_w, m_a_log, m_dt_bias, m_gdn_norm_w, m_q_norm_w, m_k_norm_w, m_rel_bias, m_w_out, m_ffn2_norm, m_ffn2_w_gate, m_ffn2_w_up, m_ffn2_w_down, m_final_norm, v_ffn1_norm, v_ffn1_w_gate, v_ffn1_w_up, v_ffn1_w_down, v_mix_norm, v_w_in, v_conv_w, v_a_log, v_dt_bias, v_gdn_norm_w, v_q_norm_w, v_k_norm_w, v_rel_bias, v_w_out, v_ffn2_norm, v_ffn2_w_gate, v_ffn2_w_up, v_ffn2_w_down, v_final_norm):
    w = dict(ffn1_norm=ffn1_norm, ffn1_w_gate=ffn1_w_gate, ffn1_w_up=ffn1_w_up, ffn1_w_down=ffn1_w_down, mix_norm=mix_norm, w_in=w_in, conv_w=conv_w, a_log=a_log, dt_bias=dt_bias, gdn_norm_w=gdn_norm_w, q_norm_w=q_norm_w, k_norm_w=k_norm_w, rel_bias=rel_bias, w_out=w_out, ffn2_norm=ffn2_norm, ffn2_w_gate=ffn2_w_gate, ffn2_w_up=ffn2_w_up, ffn2_w_down=ffn2_w_down, final_norm=final_norm)
    mom = dict(ffn1_norm=m_ffn1_norm, ffn1_w_gate=m_ffn1_w_gate, ffn1_w_up=m_ffn1_w_up, ffn1_w_down=m_ffn1_w_down, mix_norm=m_mix_norm, w_in=m_w_in, conv_w=m_conv_w, a_log=m_a_log, dt_bias=m_dt_bias, gdn_norm_w=m_gdn_norm_w, q_norm_w=m_q_norm_w, k_norm_w=m_k_norm_w, rel_bias=m_rel_bias, w_out=m_w_out, ffn2_norm=m_ffn2_norm, ffn2_w_gate=m_ffn2_w_gate, ffn2_w_up=m_ffn2_w_up, ffn2_w_down=m_ffn2_w_down, final_norm=m_final_norm)
    var = dict(ffn1_norm=v_ffn1_norm, ffn1_w_gate=v_ffn1_w_gate, ffn1_w_up=v_ffn1_w_up, ffn1_w_down=v_ffn1_w_down, mix_norm=v_mix_norm, w_in=v_w_in, conv_w=v_conv_w, a_log=v_a_log, dt_bias=v_dt_bias, gdn_norm_w=v_gdn_norm_w, q_norm_w=v_q_norm_w, k_norm_w=v_k_norm_w, rel_bias=v_rel_bias, w_out=v_w_out, ffn2_norm=v_ffn2_norm, ffn2_w_gate=v_ffn2_w_gate, ffn2_w_up=v_ffn2_w_up, ffn2_w_down=v_ffn2_w_down, final_norm=v_final_norm)
    ix, iy, ic = lax.axis_index("x"), lax.axis_index("y"), lax.axis_index("c")
    me = 4 * ix + 2 * iy + ic

    def local(a, n):
        return jnp.swapaxes(a[0], 0, 1) if n in TRANSPOSED else a[0]

    shard = {n: local(w[n], n) for n in BIG}

    conv_shard_shape = w["conv_w"][0].shape
    conv_elems = conv_shard_shape[0] * conv_shard_shape[1]
    first = ("ffn1_w_gate", "ffn1_w_up")
    gathered = _all_gather_many([shard[n].astype(BF16) for n in first] + [_pack([w["conv_w"][0]], LANE, 8)],
                                "gather_weights")
    wts = {n: g.reshape(N_DEV * g.shape[1], g.shape[2]) for n, g in zip(first, gathered)}

    small = {n: w[n][0] if n not in ("rel_bias",) else w[n] for n in SMALL}
    small = {n: (a.reshape(1, -1) if n.endswith("norm") else a) for n, a in small.items()}
    conv_all = gathered[-1].reshape(N_DEV, -1)
    small["conv_w"] = conv_all[:, :conv_elems].reshape(N_DEV * conv_shard_shape[0], conv_shard_shape[1])
    loss_row, grad_x, grads = _local_step(x[0], loss_target[0], wts, small,
                                          late_shards={n: shard[n].astype(BF16) for n in BIG if n not in first})
    loss = lax.psum(loss_row[0, 0], ("x", "y", "c"))

    big_out = [[], [], [], []]
    for n in BIG:
        for kind, val in enumerate(_adamw(grads[n], shard[n], local(mom[n], n), local(var[n], n), f"{n}_adamw")):
            big_out[kind].append(jnp.swapaxes(val, 0, 1) if n in TRANSPOSED else val)

    small_names = SMALL + ("conv_w",)
    small_shapes = [grads[n].shape for n in small_names]
    g_small = _pack([grads[n] for n in small_names], LANE, 8)
    small_rows = g_small.shape[0]
    all_small = _all_gather(g_small, "gather_small_grads").reshape(N_DEV, small_rows, LANE)
    rep_shapes = [grads[n].shape for n in SMALL]
    zero_conv = jnp.zeros(small_shapes[-1], F32)
    ws = _pack([w[n].reshape(grads[n].shape) for n in SMALL] + [zero_conv], LANE, 8)
    ms = _pack([mom[n].reshape(grads[n].shape) for n in SMALL] + [zero_conv], LANE, 8)
    vs = _pack([var[n].reshape(grads[n].shape) for n in SMALL] + [zero_conv], LANE, 8)
    small_out = [_unpack(a, small_shapes) for a in _adamw(all_small, ws, ms, vs, "adamw_small")]
    conv_g = lax.dynamic_slice_in_dim(small_out[0][-1], me * conv_shard_shape[0], conv_shard_shape[0], axis=0)
    conv_out = [_unpack(a, [conv_shard_shape])[0] for a in _adamw(
        _pack([conv_g], LANE, 8)[None], _pack([w["conv_w"][0]], LANE, 8), _pack([mom["conv_w"][0]], LANE, 8),
        _pack([var["conv_w"][0]], LANE, 8), "adamw_conv")]

    def leaf(kind, n):
        if n in BIG:
            val = big_out[kind][BIG.index(n)]
        elif n == "conv_w":
            val = conv_out[kind]
        else:
            val = small_out[kind][SMALL.index(n)]
        return val.reshape(w[n].shape)

    outs = [loss, grad_x[None]]
    for kind in range(4):
        outs += [leaf(kind, n) for n in WEIGHTS]
    return tuple(outs)
```

```python
import functools
import math

import numpy as np
import jax
import jax.numpy as jnp
from jax import lax
from jax.experimental import pallas as pl
from jax.experimental.pallas import tpu as pltpu

F32 = jnp.float32
BF16 = jnp.bfloat16

D_MODEL = 1024
D_FF = 2816
GDN_HEADS = 4
GDN_HEAD_DIM = 128
GDN_WIDTH = 512
CONV_WIDTH = 5
CHUNK = 64
SWA_HEADS = 8
SWA_HEAD_DIM = 64
SWA_WIDTH = 512
DILATION_PATTERNS = ((128, 1), (512, 4), (2048, 16))
REL_BUCKETS = 32
REL_MAX_DISTANCE = 1024
EPS = 1e-6
NEG_BIG = -1e30
N_DEV = 8

ADAM_LR = 0.001
ADAM_B1 = 0.9
ADAM_B2 = 0.999
ADAM_EPS = 1e-08
ADAM_WD = 0.01
ADAM_STEP = 10

QKV_A = 3 * GDN_WIDTH
OFF_B = QKV_A
OFF_Z = OFF_B + 3 * SWA_WIDTH
OFF_AB = OFF_Z + GDN_WIDTH
N_PAD = OFF_AB + 256
N_IN = 3600
NAT_Z, NAT_AB, NAT_B = QKV_A, QKV_A + GDN_WIDTH, QKV_A + GDN_WIDTH + 16

V7X_VMEM_LIMIT_BYTES = 56 * 1024 * 1024
LANE = 128
ATT_BQ = 128
ATT_HALO = 64
CONV_ROWS = 256

NN = (((1,), (0,)), ((), ()))
NT = (((1,), (1,)), ((), ()))
TN = (((0,), (0,)), ((), ()))


def _params(*sem):
    return pltpu.CompilerParams(dimension_semantics=sem, vmem_limit_bytes=V7X_VMEM_LIMIT_BYTES)


def _dot(a, b, dn=NN):
    return lax.dot_general(a.astype(BF16), b.astype(BF16), dn, preferred_element_type=F32)


def _sigmoid(x):
    return 1.0 / (1.0 + jnp.exp(-x))


class _Exchange:
    def __init__(self, kind, arrays):
        self.kind, self.arrays = kind, list(arrays)

    def out_shape(self):
        lead = (N_DEV,) if self.kind == "gather" else ()
        return [jax.ShapeDtypeStruct(lead + v.shape, v.dtype) for v in self.arrays]

    def hooks(self, in_refs, out_refs, sems, grid):
        step = pl.program_id(0)
        for axis in range(1, len(grid)):
            step = step * grid[axis] + pl.program_id(axis)
        total = math.prod(grid)
        if self.kind == "gather":
            assert total >= 4
            start, forward, finish = _gather_phases(in_refs, out_refs, *sems)
            pl.when(step == total // 2)(forward)
        else:
            assert total >= 2
            start, finish = _scatter_phases(in_refs, out_refs, *sems)
        pl.when(step == 0)(start)
        pl.when(step == total - 1)(finish)


def _pallas(body, *, name, grid, in_specs, out_specs, out_shape, args, semantics, scratch_shapes=(), exchange=None):
    n_in, n_out, n_scr = len(in_specs), len(out_specs), len(scratch_shapes)
    if exchange is None:
        res = pl.pallas_call(
            body, name=name, grid=grid, in_specs=list(in_specs), out_specs=list(out_specs), out_shape=list(out_shape),
            scratch_shapes=list(scratch_shapes), compiler_params=_params(*semantics))(*args)
        return list(res), []
    na = len(exchange.arrays)

    def carrying(*refs):
        ins, sent = refs[:n_in], refs[n_in:n_in + na]
        outs = refs[n_in + na:n_in + na + n_out]
        landed = refs[n_in + na + n_out:n_in + 2 * na + n_out]
        rest = refs[n_in + 2 * na + n_out:]
        exchange.hooks(sent, landed, rest[n_scr:], grid)
        body(*ins, *outs, *rest[:n_scr])

    res = pl.pallas_call(
        carrying, name=name, grid=grid, in_specs=list(in_specs) + [ANY] * na, out_specs=list(out_specs) + [ANY] * na,
        out_shape=list(out_shape) + exchange.out_shape(), scratch_shapes=list(scratch_shapes) + _gather_semaphores(na),
        compiler_params=_params(*(["arbitrary"] * len(grid))))(*args, *exchange.arrays)
    return list(res[:n_out]), list(res[n_out:])


def _matmul(pairs, *, ta=False, tb=False, out_dtype=F32, tm, tn, tk, name, res=None, alpha=None, shard_cols=None,
            exchange=None):
    a0, b0 = pairs[0]
    m = a0.shape[1] if ta else a0.shape[0]
    k = a0.shape[0] if ta else a0.shape[1]
    n = b0.shape[0] if tb else b0.shape[1]
    tm, tn, tk = min(tm, m), min(tn, n), min(tk, k)
    assert m % tm == 0 and n % tn == 0 and k % tk == 0, (name, m, n, k, tm, tn, tk)
    nk = k // tk
    npairs = len(pairs)
    dn = (((0 if ta else 1,), (1 if tb else 0,)), ((), ()))

    def body(*refs):
        ins = refs[:2 * npairs]
        pos = 2 * npairs
        r_ref = None
        if res is not None:
            r_ref = refs[pos]
            pos += 1
        o_ref, acc = refs[pos], refs[pos + 1]
        kk = pl.program_id(2)
        t = None
        for p in range(npairs):
            d = _dot(ins[2 * p][...], ins[2 * p + 1][...], dn)
            t = d if t is None else t + d

        if nk > 1:
            @pl.when(kk == 0)
            def _():
                acc[...] = t

            @pl.when((kk > 0) & (kk < nk - 1))
            def _():
                acc[...] += t

        @pl.when(kk == nk - 1)
        def _():
            r = acc[...] + t if nk > 1 else t
            if alpha is not None:
                r = r * alpha
            if r_ref is not None:
                r = r_ref[...] + r
            if shard_cols is None:
                o_ref[...] = r.astype(out_dtype)
            else:
                for sh in range(tn // shard_cols):
                    o_ref[sh] = r[:, sh * shard_cols:(sh + 1) * shard_cols].astype(out_dtype)

    a_spec = pl.BlockSpec((tk, tm), lambda i, j, kk: (kk, i)) if ta else pl.BlockSpec((tm, tk), lambda i, j, kk: (i, kk))
    b_spec = pl.BlockSpec((tn, tk), lambda i, j, kk: (j, kk)) if tb else pl.BlockSpec((tk, tn), lambda i, j, kk: (kk, j))
    o_spec = pl.BlockSpec((tm, tn), lambda i, j, kk: (i, j))
    in_specs = [a_spec, b_spec] * npairs + ([o_spec] if res is not None else [])
    args = [t for pr in pairs for t in pr] + ([res] if res is not None else [])
    out_spec, out_shape = o_spec, (m, n)
    if shard_cols is not None:
        assert res is None and tn % shard_cols == 0
        out_spec = pl.BlockSpec((tn // shard_cols, tm, shard_cols), lambda i, j, kk: (j, i, 0))
        out_shape = (n // shard_cols, m, shard_cols)
    (out,), exchanged = _pallas(
        body, name=name, grid=(m // tm, n // tn, nk), in_specs=in_specs, out_specs=[out_spec],
        out_shape=[jax.ShapeDtypeStruct(out_shape, out_dtype)],
        scratch_shapes=[pltpu.VMEM((tm, tn) if nk > 1 else (8, LANE), F32)],
        semantics=("parallel", "parallel", "arbitrary"), args=args, exchange=exchange)
    return out if exchange is None else (out, exchanged)


def _rms_fwd(x, w, name):
    s, d = x.shape
    tm = min(512, s)

    def body(x_ref, w_ref, n_ref, r_ref):
        xv = x_ref[...]
        r = lax.rsqrt(jnp.mean(xv * xv, axis=-1, keepdims=True) + EPS)
        n_ref[...] = (xv * r * w_ref[...]).astype(BF16)
        r_ref[...] = r

    return pl.pallas_call(
        body, name=name, grid=(s // tm,),
        in_specs=[pl.BlockSpec((tm, d), lambda i: (i, 0)), pl.BlockSpec((1, d), lambda i: (0, 0))],
        out_specs=[pl.BlockSpec((tm, d), lambda i: (i, 0)), pl.BlockSpec((tm, 1), lambda i: (i, 0))],
        out_shape=[jax.ShapeDtypeStruct((s, d), BF16), jax.ShapeDtypeStruct((s, 1), F32)],
        compiler_params=_params("parallel"),
    )(x, w)


def _rms_bwd(dn, x, r, w, dres, name, exchange=None):
    s, d = x.shape
    tm = min(512, s)

    def body(dn_ref, x_ref, r_ref, w_ref, dres_ref, dx_ref, dw_ref):
        @pl.when(pl.program_id(0) == 0)
        def _():
            dw_ref[...] = jnp.zeros_like(dw_ref)

        rv = r_ref[...]
        xhat = x_ref[...] * rv
        g = dn_ref[...]
        t = g * w_ref[...]
        dx_ref[...] = dres_ref[...] + rv * (t - xhat * jnp.mean(t * xhat, axis=-1, keepdims=True))
        dw_ref[...] += jnp.sum(g * xhat, axis=0, keepdims=True)

    row = pl.BlockSpec((tm, d), lambda i: (i, 0))
    vec = pl.BlockSpec((1, d), lambda i: (0, 0))
    (dx, dw), exchanged = _pallas(
        body, name=name, grid=(s // tm,),
        in_specs=[row, row, pl.BlockSpec((tm, 1), lambda i: (i, 0)), vec, row],
        out_specs=[row, vec],
        out_shape=[jax.ShapeDtypeStruct((s, d), F32), jax.ShapeDtypeStruct((1, d), F32)],
        semantics=("arbitrary",), args=(dn, x, r, w, dres), exchange=exchange)
    return (dx, dw) if exchange is None else (dx, dw, exchanged)


def _final_loss(x3, wf, tgt):
    s, d = x3.shape
    tm = min(512, s)

    def body(x_ref, w_ref, t_ref, loss_ref, dx_ref, dw_ref):
        @pl.when(pl.program_id(0) == 0)
        def _():
            dw_ref[...] = jnp.zeros_like(dw_ref)
            loss_ref[...] = jnp.zeros_like(loss_ref)

        xv = x_ref[...]
        wv = w_ref[...]
        r = lax.rsqrt(jnp.mean(xv * xv, axis=-1, keepdims=True) + EPS)
        xhat = xv * r
        e = xhat * wv - t_ref[...]
        part = 0.5 * jnp.sum(jnp.mean(e * e, axis=-1, keepdims=True), axis=0, keepdims=True)
        loss_ref[...] += jnp.broadcast_to(part, loss_ref.shape)
        dy = e * (1.0 / d)
        dw_ref[...] += jnp.sum(dy * xhat, axis=0, keepdims=True)
        t = dy * wv
        dx_ref[...] = r * (t - xhat * jnp.mean(t * xhat, axis=-1, keepdims=True))

    row = pl.BlockSpec((tm, d), lambda i: (i, 0))
    vec = pl.BlockSpec((1, d), lambda i: (0, 0))
    return pl.pallas_call(
        body, name="final_loss", grid=(s // tm,),
        in_specs=[row, vec, row],
        out_specs=[pl.BlockSpec((1, LANE), lambda i: (0, 0)), row, vec],
        out_shape=[jax.ShapeDtypeStruct((1, LANE), F32), jax.ShapeDtypeStruct((s, d), F32),
                   jax.ShapeDtypeStruct((1, d), F32)],
        compiler_params=_params("arbitrary"),
    )(x3, wf, tgt)


def _ffn_up(n, wg, wu, name, exchange=None):
    s, d = n.shape
    f = wg.shape[0]
    tm, tn = min(512, s), f // 2

    def body(n_ref, wg_ref, wu_ref, g_ref, u_ref, a_ref):
        nv = n_ref[...]
        g = _dot(nv, wg_ref[...], NT)
        u = _dot(nv, wu_ref[...], NT)
        g_ref[...] = g.astype(BF16)
        u_ref[...] = u.astype(BF16)
        a_ref[...] = (g * _sigmoid(g) * u).astype(BF16)

    o = pl.BlockSpec((tm, tn), lambda j, i: (i, j))
    wspec = pl.BlockSpec((tn, d), lambda j, i: (j, 0))
    return _pallas(
        body, name=name, grid=(f // tn, s // tm),
        in_specs=[pl.BlockSpec((tm, d), lambda j, i: (i, 0)), wspec, wspec],
        out_specs=[o, o, o],
        out_shape=[jax.ShapeDtypeStruct((s, f), BF16)] * 3,
        semantics=("parallel", "parallel"), args=(n, wg, wu), exchange=exchange)


def _ffn_dact(dx, wd, g, u, name, exchange=None):
    s, d = dx.shape
    f = wd.shape[0]
    tm, tn = min(512, s), f // 2

    def body(dx_ref, wd_ref, g_ref, u_ref, dg_ref, du_ref):
        da = 0.5 * _dot(dx_ref[...], wd_ref[...], NT)
        gv = g_ref[...].astype(F32)
        sg = _sigmoid(gv)
        du_ref[...] = (da * gv * sg).astype(BF16)
        dg_ref[...] = (da * u_ref[...].astype(F32) * (sg * (1.0 + gv * (1.0 - sg)))).astype(BF16)

    o = pl.BlockSpec((tm, tn), lambda j, i: (i, j))
    return _pallas(
        body, name=name, grid=(f // tn, s // tm),
        in_specs=[pl.BlockSpec((tm, d), lambda j, i: (i, 0)), pl.BlockSpec((tn, d), lambda j, i: (j, 0)), o, o],
        out_specs=[o, o],
        out_shape=[jax.ShapeDtypeStruct((s, f), BF16), jax.ShapeDtypeStruct((s, f), BF16)],
        semantics=("parallel", "parallel"), args=(dx, wd, g, u), exchange=exchange)


def _row_slabs(full):
    return full.reshape(N_DEV, full.shape[0] // N_DEV, full.shape[1])


def _ffn_forward(x, norm_w, wg, wu, wd, tag, gather=()):
    n, r = _rms_fwd(x, norm_w, f"{tag}_norm")
    (g, u, a), got = _ffn_up(n, wg, wu, f"{tag}_up", _Exchange("gather", gather) if gather else None)
    if wd is None:
        wd, got = got[0].reshape(N_DEV * got[0].shape[1], got[0].shape[2]), got[1:]
    y = _matmul([(a, wd)], tm=512, tn=1024, tk=wd.shape[0], name=f"{tag}_down", res=x, alpha=0.5)
    return y, (n, r, g, u, a), wd, got


def _ffn_backward(dy, x, norm_w, wgt, wut, wd, saved, tag, dw_dtype=F32, scatter=None):
    n, r, g, u, a = saved

    def behind(arrays):
        return _Exchange("scatter", arrays) if scatter is not None else None

    def dw(act, grad, name, alpha=None, exchange=None):
        return _matmul([(act, grad)], ta=True, tm=1408, tn=1024, tk=2048, name=name, alpha=alpha, out_dtype=dw_dtype,
                       exchange=exchange)

    dwd = _row_slabs(dw(a, dy, f"{tag}_dwd", alpha=0.5))
    (dg, du), extras = _ffn_dact(dy, wd, g, u, f"{tag}_dact", behind(scatter))
    if scatter is None:
        dwg, dwu = _row_slabs(dw(dg, n, f"{tag}_dwg")), _row_slabs(dw(du, n, f"{tag}_dwu"))
    else:
        dwg, (dwd,) = dw(dg, n, f"{tag}_dwg", exchange=behind([dwd]))
        dwu, (dwg,) = dw(du, n, f"{tag}_dwu", exchange=behind([_row_slabs(dwg)]))
        dwu = _row_slabs(dwu)
    dn = _matmul([(dg, wgt), (du, wut)], tm=512, tn=1024, tk=wgt.shape[0], name=f"{tag}_dn", exchange=behind([dwu]))
    if scatter is not None:
        dn, (dwu,) = dn
    dx, dnorm = _rms_bwd(dn, x, r, norm_w, dy, f"{tag}_dnorm")
    return dx, dnorm, dwg, dwu, dwd, extras


Q_SCALE = GDN_HEAD_DIM ** -0.5
CONV_HALO = 8


def _lane_block(s):
    return pl.BlockSpec((None, s, LANE), lambda j: (j, 0, 0))


def _conv_taps(win, w_ref, rows, sign):
    n = rows + 2 * CONV_HALO
    acc = None
    for t in range(CONV_WIDTH):
        o = sign * (t - CONV_WIDTH // 2)
        sh = win if o == 0 else pltpu.roll(win, (-o) % n, 0)
        term = sh[CONV_HALO:CONV_HALO + rows] * w_ref[t:t + 1, :]
        acc = term if acc is None else acc + term
    return acc


def _gdn_conv_fwd(p_pad, conv_wt):
    s = p_pad.shape[0]
    rows = min(CONV_ROWS, s)
    nblk = QKV_A // LANE

    def body(p_ref, w_ref, c_ref, y_ref, pad):
        j = pl.program_id(0)
        zeros = jnp.zeros((CONV_HALO, LANE), F32)
        pad[0:CONV_HALO, :] = zeros
        pad[CONV_HALO + s:2 * CONV_HALO + s, :] = zeros
        pad[CONV_HALO:CONV_HALO + s, :] = p_ref[...]

        def chunk(ci, carry):
            b = pl.multiple_of(ci * rows, rows)
            win = pad[pl.ds(b, rows + 2 * CONV_HALO), :]
            c = _conv_taps(win, w_ref, rows, 1)
            c_ref[pl.ds(b, rows), :] = c
            act = c * _sigmoid(c)
            nrm = lax.rsqrt(jnp.sum(act * act, axis=-1, keepdims=True) + EPS)
            mult = jnp.where(j < GDN_HEADS, nrm * Q_SCALE, jnp.where(j < 2 * GDN_HEADS, nrm, 1.0))
            y_ref[pl.ds(b, rows), :] = act * mult
            return carry

        lax.fori_loop(0, s // rows, chunk, 0)

    col = pl.BlockSpec((s, LANE), lambda j: (0, j))
    return pl.pallas_call(
        body, name="gdn_conv_fwd", grid=(nblk,),
        in_specs=[col, pl.BlockSpec((8, LANE), lambda j: (0, j))],
        out_specs=[_lane_block(s), _lane_block(s)],
        out_shape=[jax.ShapeDtypeStruct((nblk, s, LANE), F32), jax.ShapeDtypeStruct((nblk, s, LANE), F32)],
        scratch_shapes=[pltpu.VMEM((s + 2 * CONV_HALO, LANE), F32)],
        compiler_params=_params("parallel"),
    )(p_pad, conv_wt)


def _gdn_conv_bwd(dy_f, dy_r, c_pre, p_pad, conv_wt, dp_all):
    s = p_pad.shape[0]
    rows = min(CONV_ROWS, s)
    nblk = QKV_A // LANE

    def body(dyf_ref, dyr_ref, c_ref, p_ref, w_ref, _, dp_ref, dw_ref, ppad, dcpad):
        j = pl.program_id(0)
        zeros = jnp.zeros((CONV_HALO, LANE), F32)
        for buf in (ppad, dcpad):
            buf[0:CONV_HALO, :] = zeros
            buf[CONV_HALO + s:2 * CONV_HALO + s, :] = zeros
        ppad[CONV_HALO:CONV_HALO + s, :] = p_ref[...]

        def act_bwd(ci, carry):
            b = pl.multiple_of(ci * rows, rows)
            c = c_ref[pl.ds(b, rows), :]
            g = dyf_ref[pl.ds(b, rows), :] + dyr_ref[pl.ds(b, rows), :]
            sg = _sigmoid(c)
            act = c * sg
            nrm = lax.rsqrt(jnp.sum(act * act, axis=-1, keepdims=True) + EPS)
            yh = act * nrm
            scale = jnp.where(j < GDN_HEADS, Q_SCALE, 1.0)
            dact_qk = (scale * nrm) * (g - yh * jnp.sum(g * yh, axis=-1, keepdims=True))
            dact = jnp.where(j < 2 * GDN_HEADS, dact_qk, g)
            dcpad[pl.ds(pl.multiple_of(b + CONV_HALO, CONV_HALO), rows), :] = dact * (sg * (1.0 + c * (1.0 - sg)))
            return carry

        lax.fori_loop(0, s // rows, act_bwd, 0)
        tap = lax.broadcasted_iota(jnp.int32, (8, LANE), 0)

        def taps_bwd(ci, dw):
            b = pl.multiple_of(ci * rows, rows)
            dcw = dcpad[pl.ds(b, rows + 2 * CONV_HALO), :]
            dp_ref[pl.ds(b, rows), :] = _conv_taps(dcw, w_ref, rows, -1).astype(BF16)
            pw = ppad[pl.ds(b, rows + 2 * CONV_HALO), :]
            dc = dcw[CONV_HALO:CONV_HALO + rows]
            n = rows + 2 * CONV_HALO
            for t in range(CONV_WIDTH):
                o = t - CONV_WIDTH // 2
                sh = pw if o == 0 else pltpu.roll(pw, (-o) % n, 0)
                row = jnp.sum(dc * sh[CONV_HALO:CONV_HALO + rows], axis=0, keepdims=True)
                dw = dw + jnp.where(tap == t, row, 0.0)
            return dw

        dw_ref[...] = lax.fori_loop(0, s // rows, taps_bwd, jnp.zeros((8, LANE), F32))

    col = pl.BlockSpec((s, LANE), lambda j: (0, j))
    wspec = pl.BlockSpec((8, LANE), lambda j: (0, j))
    return pl.pallas_call(
        body, name="gdn_conv_bwd", grid=(nblk,),
        in_specs=[_lane_block(s), _lane_block(s), _lane_block(s), col, wspec, ANY],
        out_specs=[col, wspec],
        out_shape=[jax.ShapeDtypeStruct(dp_all.shape, dp_all.dtype), jax.ShapeDtypeStruct((8, QKV_A), F32)],
        scratch_shapes=[pltpu.VMEM((s + 2 * CONV_HALO, LANE), F32), pltpu.VMEM((s + 2 * CONV_HALO, LANE), F32)],
        input_output_aliases={5: 0},
        compiler_params=_params("parallel"),
    )(dy_f, dy_r, c_pre, p_pad, conv_wt, dp_all)


def _softplus(x):
    return jnp.maximum(x, 0.0) + jnp.log(1.0 + jnp.exp(-jnp.abs(x)))


def _gdn_gates_fwd(p_pad, alog_row, dt_row):
    s = p_pad.shape[0]
    tm = min(1024, s)

    def body(p_ref, al_ref, dt_ref, o_ref):
        x = p_ref[...]
        lane = lax.broadcasted_iota(jnp.int32, x.shape, 1)
        g = -jnp.exp(al_ref[...]) * _softplus(x + dt_ref[...])
        o_ref[...] = jnp.where(lane < 8, g, jnp.where(lane < 16, _sigmoid(x), 0.0))

    vec = pl.BlockSpec((1, LANE), lambda i: (0, 0))
    return pl.pallas_call(
        body, name="gdn_gates_fwd", grid=(s // tm,),
        in_specs=[pl.BlockSpec((tm, LANE), lambda i: (i, OFF_AB // LANE)), vec, vec],
        out_specs=pl.BlockSpec((tm, LANE), lambda i: (i, 0)),
        out_shape=jax.ShapeDtypeStruct((s, LANE), F32),
        compiler_params=_params("parallel"),
    )(p_pad, alog_row, dt_row)


def _gdn_gates_bwd(dgb_f, dgb_r, p_pad, gb, alog_row, dt_row, dp_all):
    s = p_pad.shape[0]
    tm = min(1024, s)
    tail = N_PAD - OFF_AB

    def body(df_ref, dr_ref, p_ref, gb_ref, al_ref, dt_ref, _, dp_ref, sum_ref):
        @pl.when(pl.program_id(0) == 0)
        def _():
            sum_ref[...] = jnp.zeros_like(sum_ref)

        x = p_ref[...]
        gbv = gb_ref[...]
        dgb = df_ref[...] + dr_ref[...]
        lane = lax.broadcasted_iota(jnp.int32, x.shape, 1)
        da = dgb * (-jnp.exp(al_ref[...])) * _sigmoid(x + dt_ref[...])
        db = dgb * gbv * (1.0 - gbv)
        dp_ref[:, 0:LANE] = jnp.where(lane < 8, da, jnp.where(lane < 16, db, 0.0)).astype(BF16)
        dp_ref[:, LANE:tail] = jnp.zeros((tm, tail - LANE), BF16)
        row = lax.broadcasted_iota(jnp.int32, (8, LANE), 0)
        lane8 = lax.broadcasted_iota(jnp.int32, (8, LANE), 1)
        d_alog = jnp.sum(dgb * gbv, axis=0, keepdims=True)
        d_dt = jnp.sum(da, axis=0, keepdims=True)
        upd = jnp.where(row == 0, d_alog, jnp.where(row == 1, d_dt, 0.0))
        sum_ref[...] += jnp.where(lane8 < 8, upd, 0.0)

    vec = pl.BlockSpec((1, LANE), lambda i: (0, 0))
    blk = pl.BlockSpec((tm, LANE), lambda i: (i, 0))
    return pl.pallas_call(
        body, name="gdn_gates_bwd", grid=(s // tm,),
        in_specs=[blk, blk, pl.BlockSpec((tm, LANE), lambda i: (i, OFF_AB // LANE)), blk, vec, vec, ANY],
        out_specs=[pl.BlockSpec((tm, tail), lambda i: (i, OFF_AB // tail)), pl.BlockSpec((8, LANE), lambda i: (0, 0))],
        out_shape=[jax.ShapeDtypeStruct(dp_all.shape, dp_all.dtype), jax.ShapeDtypeStruct((8, LANE), F32)],
        input_output_aliases={6: 0},
        compiler_params=_params("arbitrary"),
    )(dgb_f, dgb_r, p_pad, gb, alog_row, dt_row, dp_all)


def _chunk_masks(rev):
    row = lax.broadcasted_iota(jnp.int32, (CHUNK, CHUNK), 0)
    col = lax.broadcasted_iota(jnp.int32, (CHUNK, CHUNK), 1)
    le = (col >= row) if rev else (col <= row)
    strict = (col > row) if rev else (col < row)
    return le, strict, row == col


def _gate_lanes(rev, h):
    d = 1 if rev else 0
    return d * GDN_HEADS + h, 8 + d * GDN_HEADS + h


BNN = (((2,), (1,)), ((0,), (0,)))
BNT = (((2,), (2,)), ((0,), (0,)))
BTN = (((1,), (1,)), ((0,), (0,)))
NB = 2 * GDN_HEADS


def _bdot(a, b, dn=BNN):
    return lax.dot_general(a.astype(BF16), b.astype(BF16), dn, preferred_element_type=F32)


def _dot3(a, b, dn):
    ah = a.astype(BF16)
    al = (a - ah.astype(F32)).astype(BF16)
    bh = b.astype(BF16)
    bl = (b - bh.astype(F32)).astype(BF16)

    def d(x, y):
        return lax.dot_general(x, y, dn, preferred_element_type=F32)

    return d(ah, bh) + d(ah, bl) + d(al, bh)


def _both(f_val, r_val):
    return jnp.stack([f_val] * GDN_HEADS + [r_val] * GDN_HEADS)


def _head_blocks(ref_f, ref_r):
    return jnp.concatenate([ref_f[...], ref_r[...]], axis=0)


def _heads(ref_f, ref_r):
    hd = GDN_HEAD_DIM
    return jnp.stack([ref_f[:, h * hd:(h + 1) * hd] for h in range(GDN_HEADS)]
                     + [ref_r[:, h * hd:(h + 1) * hd] for h in range(GDN_HEADS)])


def _gate_cols(tile_f, tile_r, base):
    return jnp.stack([tile_f[:, base + h:base + h + 1] for h in range(GDN_HEADS)]
                     + [tile_r[:, base + GDN_HEADS + h:base + GDN_HEADS + h + 1] for h in range(GDN_HEADS)])


def _chunk_common2(q, k, v, gbf, gbr):
    mf, mr = _chunk_masks(False), _chunk_masks(True)
    le, strict = _both(mf[0], mr[0]), _both(mf[1], mr[1])
    eye = mf[2]
    gcm_f = _dot3(mf[0].astype(F32), gbf, NN)
    gcm_r = _dot3(mr[0].astype(F32), gbr, NN)
    g, beta, gc = _gate_cols(gbf, gbr, 0), _gate_cols(gbf, gbr, 8), _gate_cols(gcm_f, gcm_r, 0)
    gc_row = _dot3(jnp.ones((NB, CHUNK, CHUNK), F32), jnp.where(eye[None], gc, 0.0), BNN)
    decay = jnp.where(le, jnp.exp(jnp.where(le, gc - gc_row, 0.0)), 0.0)
    eg = jnp.exp(gc)
    gl = jnp.sum(g, axis=1, keepdims=True)
    kb = k * beta
    vb = v * beta
    kbeg = kb * eg
    lm = jnp.where(strict, _bdot(kb, k, BNT) * decay, 0.0)
    intra = _bdot(q, k, BNT) * decay
    edec = jnp.exp(gl - gc)
    return dict(strict=strict, eye=eye, beta=beta, decay=decay, eg=eg, gl=gl, kb=kb, vb=vb, kbeg=kbeg,
                lm=lm, intra=intra, qg=q * eg, edec=edec, kdec=k * edec)


def _unit_triangular_inverse(lm, eye):
    x = -lm
    t = eye[None].astype(F32) + x
    p = x
    for level in range(5):
        prod = functools.partial(_dot3, dn=BNN) if level < 2 else _bdot
        p = prod(p, p)
        t = t + prod(t, p)
    return t


def _delta_fwd2(y, gb, gather=()):
    s = y.shape[1]
    nc = s // CHUNK
    hd = GDN_HEAD_DIM
    na = len(gather)

    def body(*refs):
        qf, kf, vf, gf, qr, kr, vr, gr = refs[:8]
        of_ref, or_ref, sf_all, sr_all, tf_all, tr_all = refs[8 + na:14 + na]
        state = refs[14 + 2 * na]
        step = pl.program_id(0)

        @pl.when(step == 0)
        def _():
            state[...] = jnp.zeros_like(state)

        if na:
            start, forward, finish = _gather_phases(refs[8:8 + na], refs[14 + na:14 + 2 * na], *refs[15 + 2 * na:])
            pl.when(step == 0)(start)
            pl.when(step == nc // 2)(forward)
            pl.when(step == nc - 1)(finish)

        q, k, v = _head_blocks(qf, qr), _head_blocks(kf, kr), _head_blocks(vf, vr)
        cm = _chunk_common2(q, k, v, gf[...], gr[...])
        tinv = _unit_triangular_inverse(cm["lm"], cm["eye"])
        u = _bdot(tinv, cm["vb"])
        w = _bdot(tinv, cm["kbeg"])
        st = state[...]
        v_new = u - _bdot(w, st)
        o = _bdot(cm["qg"], st) + _bdot(cm["intra"], v_new)
        state[...] = st * jnp.exp(cm["gl"]) + _bdot(cm["kdec"], v_new, BTN)
        for h in range(GDN_HEADS):
            of_ref[:, h * hd:(h + 1) * hd] = o[h]
            or_ref[:, h * hd:(h + 1) * hd] = o[GDN_HEADS + h]
        sf_all[0] = st[:GDN_HEADS]
        sr_all[0] = st[GDN_HEADS:]
        tf_all[0] = tinv[:GDN_HEADS]
        tr_all[0] = tinv[GDN_HEADS:]

    def col(j, rev):
        return pl.BlockSpec((GDN_HEADS, CHUNK, hd), (lambda n: (j, nc - 1 - n, 0)) if rev else (lambda n: (j, n, 0)))

    def out(rev):
        return pl.BlockSpec((CHUNK, GDN_WIDTH), (lambda n: (nc - 1 - n, 0)) if rev else (lambda n: (n, 0)))

    def gate(rev):
        return pl.BlockSpec((CHUNK, LANE), (lambda n: (nc - 1 - n, 0)) if rev else (lambda n: (n, 0)))

    def per_chunk(d1, d2, rev):
        return pl.BlockSpec((1, GDN_HEADS, d1, d2), (lambda n: (nc - 1 - n, 0, 0, 0)) if rev else (lambda n: (n, 0, 0, 0)))

    assert na == 0 or nc >= 4
    res = pl.pallas_call(
        body, name="delta_fwd", grid=(nc,),
        in_specs=[col(0, False), col(1, False), col(2, False), gate(False), col(0, True), col(1, True), col(2, True), gate(True)]
        + [ANY] * na,
        out_specs=[out(False), out(True), per_chunk(hd, hd, False), per_chunk(hd, hd, True),
                   per_chunk(CHUNK, CHUNK, False), per_chunk(CHUNK, CHUNK, True)] + [ANY] * na,
        out_shape=[jax.ShapeDtypeStruct((s, GDN_WIDTH), F32)] * 2 + [jax.ShapeDtypeStruct((nc, GDN_HEADS, hd, hd), F32)] * 2
        + [jax.ShapeDtypeStruct((nc, GDN_HEADS, CHUNK, CHUNK), F32)] * 2
        + [jax.ShapeDtypeStruct((N_DEV,) + v.shape, v.dtype) for v in gather],
        scratch_shapes=[pltpu.VMEM((NB, hd, hd), F32)] + (_gather_semaphores(na) if na else []),
        compiler_params=_params("arbitrary"),
    )(y, y, y, gb, y, y, y, gb, *gather)
    return res[:6], res[6:]


def _delta_bwd2(y, gb, do, sf_all, sr_all, tf_all, tr_all, scatter=()):
    s = y.shape[1]
    nc = s // CHUNK
    hd = GDN_HEAD_DIM
    na = len(scatter)

    def body(*refs):
        qf, kf, vf, gf, dof, sf, tf, qr, kr, vr, gr, dor, sr, tr = refs[:14]
        dyf_ref, dyr_ref, dgf_ref, dgr_ref = refs[14 + na:18 + na]
        dstate = refs[18 + 2 * na]
        step = pl.program_id(0)

        @pl.when(step == 0)
        def _():
            dstate[...] = jnp.zeros_like(dstate)

        if na:
            start, finish = _scatter_phases(refs[14:14 + na], refs[18 + na:18 + 2 * na], *refs[19 + 2 * na:])
            pl.when(step == 0)(start)
            pl.when(step == nc - 1)(finish)

        q, k, v, dov = _head_blocks(qf, qr), _head_blocks(kf, kr), _head_blocks(vf, vr), _heads(dof, dor)
        cm = _chunk_common2(q, k, v, gf[...], gr[...])
        tinv = jnp.concatenate([tf[0], tr[0]], axis=0)
        st = jnp.concatenate([sf[0], sr[0]], axis=0)
        ds_out = dstate[...]
        decay, lm, intra, qg, kdec, kbeg, eg, kb, beta = (
            cm[n] for n in ("decay", "lm", "intra", "qg", "kdec", "kbeg", "eg", "kb", "beta"))
        u = _bdot(tinv, cm["vb"])
        w = _bdot(tinv, kbeg)
        v_new = u - _bdot(w, st)
        egl = jnp.exp(cm["gl"])
        d_qg = _bdot(dov, st, BNT)
        d_intra = _bdot(dov, v_new, BNT)
        dv_new = _bdot(intra, dov, BTN) + _bdot(kdec, ds_out)
        d_kdec = _bdot(v_new, ds_out, BNT)
        dstate[...] = _bdot(qg, dov, BTN) + egl * ds_out - _bdot(w, dv_new, BTN)
        dgl = egl * jnp.sum(jnp.sum(st * ds_out, axis=2, keepdims=True), axis=1, keepdims=True)
        dw = -_bdot(dv_new, st, BNT)
        dvb = _bdot(tinv, dv_new, BTN)
        dkbeg = _bdot(tinv, dw, BTN)
        dlm = jnp.where(cm["strict"], -(_bdot(dvb, u, BNT) + _bdot(dkbeg, w, BNT)), 0.0)
        d_a = dlm * decay
        d_qk = d_intra * decay
        e = dlm * lm + d_intra * intra
        colsum = _dot3(e, jnp.ones((NB, CHUNK, LANE), F32), BTN)[:, :, 0:1]
        dgc = jnp.sum(e, axis=2, keepdims=True) - colsum
        dkb = _bdot(d_a, k) + dkbeg * eg
        dk = _bdot(d_a, kb, BTN) + _bdot(d_qk, q, BTN)
        dq = _bdot(d_qk, k) + d_qg * eg
        dgc = dgc + jnp.sum(d_qg * qg, axis=2, keepdims=True) + jnp.sum(dkbeg * kbeg, axis=2, keepdims=True)
        tdec = jnp.sum(d_kdec * kdec, axis=2, keepdims=True)
        dk = dk + d_kdec * cm["edec"] + dkb * beta
        dgc = dgc - tdec
        dgl = dgl + jnp.sum(tdec, axis=1, keepdims=True)
        dbeta = jnp.sum(dvb * v, axis=2, keepdims=True) + jnp.sum(dkb * k, axis=2, keepdims=True)
        dv = dvb * beta
        lane = lax.broadcasted_iota(jnp.int32, (CHUNK, LANE), 1)
        for rev, dy_ref, dg_ref in ((False, dyf_ref, dgf_ref), (True, dyr_ref, dgr_ref)):
            dgc_tile = jnp.zeros((CHUNK, LANE), F32)
            rest = jnp.zeros((CHUNK, LANE), F32)
            for h in range(GDN_HEADS):
                b = (GDN_HEADS if rev else 0) + h
                gi, bi = _gate_lanes(rev, h)
                dgc_tile = dgc_tile + jnp.where(lane == gi, dgc[b], 0.0)
                rest = rest + jnp.where(lane == gi, dgl[b], 0.0) + jnp.where(lane == bi, dbeta[b], 0.0)
                dy_ref[h] = dq[b]
                dy_ref[GDN_HEADS + h] = dk[b]
                dy_ref[2 * GDN_HEADS + h] = dv[b]
            le_t = _chunk_masks(not rev)[0].astype(F32)
            dg_ref[...] = _dot3(le_t, dgc_tile, NN) + rest

    def col(j, rev, blocks=GDN_HEADS):
        return pl.BlockSpec((blocks, CHUNK, hd), (lambda n: (j, n, 0)) if rev else (lambda n: (j, nc - 1 - n, 0)))

    def wide(width, rev):
        return pl.BlockSpec((CHUNK, width), (lambda n: (n, 0)) if rev else (lambda n: (nc - 1 - n, 0)))

    def per_chunk(d1, d2, rev):
        return pl.BlockSpec((1, GDN_HEADS, d1, d2), (lambda n: (n, 0, 0, 0)) if rev else (lambda n: (nc - 1 - n, 0, 0, 0)))

    def side(rev):
        return [col(0, rev), col(1, rev), col(2, rev), wide(LANE, rev), wide(GDN_WIDTH, rev), per_chunk(hd, hd, rev),
                per_chunk(CHUNK, CHUNK, rev)]

    assert na == 0 or nc >= 2
    res = pl.pallas_call(
        body, name="delta_bwd", grid=(nc,),
        in_specs=side(False) + side(True) + [ANY] * na,
        out_specs=[col(0, False, 3 * GDN_HEADS), col(0, True, 3 * GDN_HEADS), wide(LANE, False), wide(LANE, True)]
        + [ANY] * na,
        out_shape=[jax.ShapeDtypeStruct((3 * GDN_HEADS, s, hd), F32)] * 2 + [jax.ShapeDtypeStruct((s, LANE), F32)] * 2
        + [jax.ShapeDtypeStruct(g.shape, g.dtype) for g in scatter],
        scratch_shapes=[pltpu.VMEM((NB, hd, hd), F32)] + (_gather_semaphores(na) if na else []),
        compiler_params=_params("arbitrary"),
    )(y, y, y, gb, do, sf_all, tf_all, y, y, y, gb, do, sr_all, tr_all, *scatter)
    return res[:4], res[4:]


def _gdn_post_fwd(o_f, o_r, p_pad, norm_row):
    s = o_f.shape[0]
    tm = min(512, s)
    hd = GDN_HEAD_DIM

    def body(of_ref, or_ref, z_ref, w_ref, out_ref, osum_ref):
        o = of_ref[...] + or_ref[...]
        osum_ref[...] = o
        z = z_ref[...]
        gate = z * _sigmoid(z)
        for h in range(GDN_HEADS):
            sl = slice(h * hd, (h + 1) * hd)
            oh = o[:, sl]
            r = lax.rsqrt(jnp.mean(oh * oh, axis=-1, keepdims=True) + EPS)
            out_ref[:, sl] = (oh * r * w_ref[...] * gate[:, sl]).astype(BF16)

    blk = pl.BlockSpec((tm, GDN_WIDTH), lambda i: (i, 0))
    return pl.pallas_call(
        body, name="gdn_post_fwd", grid=(s // tm,),
        in_specs=[blk, blk, pl.BlockSpec((tm, GDN_WIDTH), lambda i: (i, OFF_Z // GDN_WIDTH)),
                  pl.BlockSpec((1, hd), lambda i: (0, 0))],
        out_specs=[blk, blk],
        out_shape=[jax.ShapeDtypeStruct((s, GDN_WIDTH), BF16), jax.ShapeDtypeStruct((s, GDN_WIDTH), F32)],
        compiler_params=_params("parallel"),
    )(o_f, o_r, p_pad, norm_row)


def _gdn_post_bwd(d_out, o_sum, p_pad, norm_row):
    s = o_sum.shape[0]
    tm = min(512, s)
    hd = GDN_HEAD_DIM

    def body(d_ref, o_ref, z_ref, w_ref, do_ref, dz_ref, dw_ref):
        @pl.when(pl.program_id(0) == 0)
        def _():
            dw_ref[...] = jnp.zeros_like(dw_ref)

        z = z_ref[...]
        sg = _sigmoid(z)
        gate = z * sg
        dgate = sg * (1.0 + z * (1.0 - sg))
        wv = w_ref[...]
        dw = jnp.zeros((1, hd), F32)
        for h in range(GDN_HEADS):
            sl = slice(h * hd, (h + 1) * hd)
            oh = o_ref[:, sl]
            dh = d_ref[:, sl]
            r = lax.rsqrt(jnp.mean(oh * oh, axis=-1, keepdims=True) + EPS)
            ohat = oh * r
            dz_ref[:, sl] = (dh * ohat * wv * dgate[:, sl]).astype(BF16)
            drn = dh * gate[:, sl]
            t = drn * wv
            do_ref[:, sl] = r * (t - ohat * jnp.mean(t * ohat, axis=-1, keepdims=True))
            dw = dw + jnp.sum(drn * ohat, axis=0, keepdims=True)
        dw_ref[...] += dw

    blk = pl.BlockSpec((tm, GDN_WIDTH), lambda i: (i, 0))
    vec = pl.BlockSpec((1, hd), lambda i: (0, 0))
    return pl.pallas_call(
        body, name="gdn_post_bwd", grid=(s // tm,),
        in_specs=[blk, blk, pl.BlockSpec((tm, GDN_WIDTH), lambda i: (i, OFF_Z // GDN_WIDTH)), vec],
        out_specs=[blk, pl.BlockSpec((tm, GDN_WIDTH), lambda i: (i, OFF_Z // GDN_WIDTH)), vec],
        out_shape=[jax.ShapeDtypeStruct((s, GDN_WIDTH), F32), jax.ShapeDtypeStruct((s, N_PAD), BF16),
                   jax.ShapeDtypeStruct((1, hd), F32)],
        compiler_params=_params("arbitrary"),
    )(d_out, o_sum, p_pad, norm_row)


def _gdn_forward(p_pad, conv_wt, alog_row, dt_row, norm_row, gather=()):
    c_pre, y = _gdn_conv_fwd(p_pad, conv_wt)
    gb = _gdn_gates_fwd(p_pad, alog_row, dt_row)
    (o_f, o_r, s_f, s_r, t_f, t_r), gathered = _delta_fwd2(y, gb, gather)
    out, o_sum = _gdn_post_fwd(o_f, o_r, p_pad, norm_row)
    return out, (c_pre, y, gb, s_f, t_f, s_r, t_r, o_sum), gathered


def _gdn_backward(d_out, p_pad, conv_wt, alog_row, dt_row, norm_row, saved, scatter=()):
    c_pre, y, gb, s_f, t_f, s_r, t_r, o_sum = saved
    do, dp_all, dnorm = _gdn_post_bwd(d_out, o_sum, p_pad, norm_row)
    (dy_f, dy_r, dgb_f, dgb_r), received = _delta_bwd2(y, gb, do, s_f, s_r, t_f, t_r, scatter)
    dp_all, dconv = _gdn_conv_bwd(dy_f, dy_r, c_pre, p_pad, conv_wt, dp_all)
    dp_all, gate_sums = _gdn_gates_bwd(dgb_f, dgb_r, p_pad, gb, alog_row, dt_row, dp_all)
    return dp_all, dconv, gate_sums, dnorm, received


ATT_BK = ATT_BQ + 2 * ATT_HALO
SWA_SCALE = SWA_HEAD_DIM ** -0.5


def _t5_bucket(rel):
    nb = REL_BUCKETS // 2
    bucket = (rel > 0).astype(np.int32) * nb
    n = np.abs(rel)
    max_exact = nb // 2
    large = max_exact + (np.log(np.maximum(n, 1) / max_exact)
                         / math.log(REL_MAX_DISTANCE / max_exact) * (nb - max_exact)).astype(np.int32)
    large = np.minimum(large, nb - 1)
    return (bucket + np.where(n < max_exact, n, large)).astype(np.int32)


def _band_tables(dilation, queries_are_rows_of_block):
    blk = np.arange(ATT_BQ)
    band = np.arange(ATT_BK) - ATT_HALO
    if queries_are_rows_of_block:
        rel = band[None, :] - blk[:, None]
        band_idx = np.broadcast_to(np.arange(ATT_BK)[None, :], rel.shape)
    else:
        rel = blk[None, :] - band[:, None]
        band_idx = np.broadcast_to(np.arange(ATT_BK)[:, None], rel.shape)
    base = np.abs(rel) <= ATT_HALO
    not_prev = band_idx >= ATT_HALO
    not_next = band_idx < ATT_HALO + ATT_BQ
    valid = np.stack([base & not_prev, base, base & not_next, base & not_prev & not_next])
    return valid, _t5_bucket(rel * dilation)


def _bias_tiles(rel_bias, dilation, queries_are_rows_of_block):
    valid, bucket = _band_tables(dilation, queries_are_rows_of_block)
    onehot = (jnp.asarray(bucket.reshape(-1, 1)) == jnp.arange(REL_BUCKETS, dtype=jnp.int32)[None, :]).astype(F32)
    rb = jnp.dot(onehot, rel_bias.astype(F32), precision=lax.Precision.HIGHEST)
    rb = rb.T.reshape((SWA_HEADS,) + bucket.shape)
    return jnp.where(valid[:, None], rb[None], NEG_BIG).astype(F32)


def _group_sum(x, bd):
    hi = x.astype(BF16)
    lo = (x - hi.astype(F32)).astype(BF16)
    return jnp.dot(hi, bd, preferred_element_type=F32) + jnp.dot(lo, bd, preferred_element_type=F32)


def _head_block_diag():
    idx = np.arange(SWA_WIDTH) // SWA_HEAD_DIM
    return jnp.asarray(idx[:, None] == idx[None, :], BF16)


def _swa_pre_fwd(p_pad, qw_row, kw_row, bd):
    s = p_pad.shape[0]
    tm = min(512, s)
    inv = 1.0 / SWA_HEAD_DIM

    def body(q_ref, k_ref, v_ref, qw_ref, kw_ref, bd_ref, qo_ref, ko_ref, vo_ref):
        bdv = bd_ref[...]
        q = q_ref[...]
        k = k_ref[...]
        rq = lax.rsqrt(_group_sum(q * q, bdv) * inv + EPS)
        rk = lax.rsqrt(_group_sum(k * k, bdv) * inv + EPS)
        qo_ref[...] = (q * rq * qw_ref[...] * SWA_SCALE).astype(BF16)
        ko_ref[...] = (k * rk * kw_ref[...]).astype(BF16)
        vo_ref[...] = v_ref[...].astype(BF16)

    base = OFF_B // SWA_WIDTH
    blk = pl.BlockSpec((tm, SWA_WIDTH), lambda i: (i, 0))
    vec = pl.BlockSpec((1, SWA_WIDTH), lambda i: (0, 0))
    return pl.pallas_call(
        body, name="swa_pre_fwd", grid=(s // tm,),
        in_specs=[pl.BlockSpec((tm, SWA_WIDTH), lambda i: (i, base)), pl.BlockSpec((tm, SWA_WIDTH), lambda i: (i, base + 1)),
                  pl.BlockSpec((tm, SWA_WIDTH), lambda i: (i, base + 2)), vec, vec,
                  pl.BlockSpec((SWA_WIDTH, SWA_WIDTH), lambda i: (0, 0))],
        out_specs=[blk, blk, blk],
        out_shape=[jax.ShapeDtypeStruct((s, SWA_WIDTH), BF16)] * 3,
        compiler_params=_params("parallel"),
    )(p_pad, p_pad, p_pad, qw_row, kw_row, bd)


def _swa_pre_bwd(dqs, dks, dvs, p_pad, qw_row, kw_row, bd, dp_all):
    s = p_pad.shape[0]
    tm = min(256, s)
    inv = 1.0 / SWA_HEAD_DIM
    npat = len(dqs)

    def body(*refs):
        dq_refs, dk_refs, dv_refs = refs[:npat], refs[npat:2 * npat], refs[2 * npat:3 * npat]
        q_ref, k_ref, qw_ref, kw_ref, bd_ref, _, dp_ref, dqw_ref, dkw_ref = refs[3 * npat:]

        @pl.when(pl.program_id(0) == 0)
        def _():
            dqw_ref[...] = jnp.zeros_like(dqw_ref)
            dkw_ref[...] = jnp.zeros_like(dkw_ref)

        bdv = bd_ref[...]

        def norm_bwd(x, g, w, scale):
            r = lax.rsqrt(_group_sum(x * x, bdv) * inv + EPS)
            xhat = x * r
            t = g * w * scale
            dx = r * (t - xhat * (_group_sum(t * xhat, bdv) * inv))
            return dx, jnp.sum(g * scale * xhat, axis=0, keepdims=True)

        def total(rs):
            t = rs[0][...].astype(F32)
            for r in rs[1:]:
                t = t + r[...].astype(F32)
            return t

        dq, dqw = norm_bwd(q_ref[...], total(dq_refs), qw_ref[...], SWA_SCALE)
        dk, dkw = norm_bwd(k_ref[...], total(dk_refs), kw_ref[...], 1.0)
        dp_ref[:, 0:SWA_WIDTH] = dq.astype(BF16)
        dp_ref[:, SWA_WIDTH:2 * SWA_WIDTH] = dk.astype(BF16)
        dp_ref[:, 2 * SWA_WIDTH:3 * SWA_WIDTH] = total(dv_refs).astype(BF16)
        dqw_ref[...] += dqw
        dkw_ref[...] += dkw

    base = OFF_B // SWA_WIDTH
    blk = pl.BlockSpec((tm, SWA_WIDTH), lambda i: (i, 0))
    vec = pl.BlockSpec((1, SWA_WIDTH), lambda i: (0, 0))
    return pl.pallas_call(
        body, name="swa_pre_bwd", grid=(s // tm,),
        in_specs=[blk] * (3 * npat) + [pl.BlockSpec((tm, SWA_WIDTH), lambda i: (i, base)),
                                      pl.BlockSpec((tm, SWA_WIDTH), lambda i: (i, base + 1)), vec, vec,
                                      pl.BlockSpec((SWA_WIDTH, SWA_WIDTH), lambda i: (0, 0)), ANY],
        out_specs=[pl.BlockSpec((tm, 3 * SWA_WIDTH), lambda i: (i, OFF_B // (3 * SWA_WIDTH))), vec, vec],
        out_shape=[jax.ShapeDtypeStruct(dp_all.shape, dp_all.dtype), jax.ShapeDtypeStruct((1, SWA_WIDTH), F32),
                   jax.ShapeDtypeStruct((1, SWA_WIDTH), F32)],
        input_output_aliases={3 * npat + 5: 0},
        compiler_params=_params("arbitrary"),
    )(*dqs, *dks, *dvs, p_pad, p_pad, qw_row, kw_row, bd, dp_all)


def _band_specs(length):
    per = ATT_BQ // ATT_HALO
    last = length // ATT_HALO - 1
    prev = pl.BlockSpec((ATT_HALO, SWA_WIDTH), lambda r, t: (jnp.maximum(t * per - 1, 0), r))
    cur = pl.BlockSpec((ATT_BQ, SWA_WIDTH), lambda r, t: (t, r))
    nxt = pl.BlockSpec((ATT_HALO, SWA_WIDTH), lambda r, t: (jnp.minimum((t + 1) * per, last), r))
    return [prev, cur, nxt]


def _tile_variant(t, nb):
    if nb == 1:
        return 3
    return jnp.where(t == 0, 0, jnp.where(t == nb - 1, 2, 1))


def _band(refs):
    return jnp.concatenate([r[...] for r in refs], axis=0)


N_PAIRS = SWA_HEADS // 2


def _pairs(x):
    return jnp.stack([x[:, LANE * p:LANE * (p + 1)] for p in range(N_PAIRS)])


def _per_head_rows(x):
    first = lax.broadcasted_iota(jnp.int32, x.shape, 2) < SWA_HEAD_DIM
    zero = jnp.zeros_like(x)
    return jnp.concatenate([jnp.where(first, x, zero), jnp.where(first, zero, x)], axis=1)


def _per_head_cols(x):
    return jnp.stack([jnp.concatenate([x[:, LANE * p:LANE * p + 1],
                                       x[:, LANE * p + SWA_HEAD_DIM:LANE * p + SWA_HEAD_DIM + 1]], axis=0)
                      for p in range(N_PAIRS)])


def _merge_heads(x, rows):
    first = lax.broadcasted_iota(jnp.int32, (N_PAIRS, rows, LANE), 2) < SWA_HEAD_DIM
    return jnp.where(first, x[:, :rows], x[:, rows:])


def _store_pairs(ref, x):
    for p in range(N_PAIRS):
        ref[:, LANE * p:LANE * (p + 1)] = x[p].astype(ref.dtype)


def _att_fwd2(q, k, v, bias, dilation):
    s = q.shape[0]
    length = s // dilation
    nb = length // ATT_BQ
    view = (length, dilation * SWA_WIDTH)

    def body(q_ref, kp, kc, kn, vp, vc, vn, b_ref, o_ref, lse_ref):
        kb, vb = _pairs(_band((kp, kc, kn))), _pairs(_band((vp, vc, vn)))
        qm = _per_head_rows(_pairs(q_ref[...]))
        sc = _bdot(qm, kb, BNT) + b_ref[0].reshape(N_PAIRS, 2 * ATT_BQ, ATT_BK)
        m = jnp.max(sc, axis=-1, keepdims=True)
        p = jnp.exp(sc - m)
        den = jnp.sum(p, axis=-1, keepdims=True)
        o = _bdot(p, vb) / den
        _store_pairs(o_ref, _merge_heads(o, ATT_BQ))
        lse = jnp.broadcast_to(m + jnp.log(den), (N_PAIRS, 2 * ATT_BQ, LANE))
        _store_pairs(lse_ref, _merge_heads(lse, ATT_BQ))

    cur = pl.BlockSpec((ATT_BQ, SWA_WIDTH), lambda r, t: (t, r))
    bspec = pl.BlockSpec((1, SWA_HEADS, ATT_BQ, ATT_BK), lambda r, t: (_tile_variant(t, nb), 0, 0, 0))
    o, lse = pl.pallas_call(
        body, name=f"att_fwd_d{dilation}", grid=(dilation, nb),
        in_specs=[cur] + _band_specs(length) * 2 + [bspec],
        out_specs=[cur, cur],
        out_shape=[jax.ShapeDtypeStruct(view, BF16), jax.ShapeDtypeStruct(view, F32)],
        compiler_params=_params("parallel", "parallel"),
    )(q.reshape(view), *([k.reshape(view)] * 3), *([v.reshape(view)] * 3), bias)
    return o.reshape(s, SWA_WIDTH), lse.reshape(s, SWA_WIDTH)


def _att_dq2(q, k, v, dop, lse, cp, bias, dilation):
    s = q.shape[0]
    length = s // dilation
    nb = length // ATT_BQ
    view = (length, dilation * SWA_WIDTH)

    def body(q_ref, kp, kc, kn, vp, vc, vn, do_ref, lse_ref, cp_ref, b_ref, dq_ref, db_ref):
        @pl.when((pl.program_id(0) == 0) & (pl.program_id(1) == 0))
        def _():
            db_ref[...] = jnp.zeros_like(db_ref)

        var = _tile_variant(pl.program_id(1), nb)
        kb, vb = _pairs(_band((kp, kc, kn))), _pairs(_band((vp, vc, vn)))
        qm = _per_head_rows(_pairs(q_ref[...]))
        dom = _per_head_rows(_pairs(do_ref[...]))
        sc = _bdot(qm, kb, BNT) + b_ref[0].reshape(N_PAIRS, 2 * ATT_BQ, ATT_BK)
        p = jnp.exp(sc - _per_head_cols(lse_ref[...]))
        ds = p * (_bdot(dom, vb, BNT) + _per_head_cols(cp_ref[...]))
        _store_pairs(dq_ref, _merge_heads(_bdot(ds, kb), ATT_BQ))
        db_ref[var] += ds.reshape(SWA_HEADS, ATT_BQ, ATT_BK)

    cur = pl.BlockSpec((ATT_BQ, SWA_WIDTH), lambda r, t: (t, r))
    bspec = pl.BlockSpec((1, SWA_HEADS, ATT_BQ, ATT_BK), lambda r, t: (_tile_variant(t, nb), 0, 0, 0))
    dq, db = pl.pallas_call(
        body, name=f"att_dq_d{dilation}", grid=(dilation, nb),
        in_specs=[cur] + _band_specs(length) * 2 + [cur, cur, cur, bspec],
        out_specs=[cur, pl.BlockSpec((4, SWA_HEADS, ATT_BQ, ATT_BK), lambda r, t: (0, 0, 0, 0))],
        out_shape=[jax.ShapeDtypeStruct(view, BF16), jax.ShapeDtypeStruct((4, SWA_HEADS, ATT_BQ, ATT_BK), F32)],
        compiler_params=_params("arbitrary", "arbitrary"),
    )(q.reshape(view), *([k.reshape(view)] * 3), *([v.reshape(view)] * 3), dop.reshape(view), lse.reshape(view),
      cp.reshape(view), bias)
    return dq.reshape(s, SWA_WIDTH), db


def _att_dkv2(q, k, v, dop, lse, cp, bias_t, dilation):
    s = q.shape[0]
    length = s // dilation
    nb = length // ATT_BQ
    view = (length, dilation * SWA_WIDTH)

    def body(k_ref, v_ref, qp, qc, qn, dp_, dc_, dn_, lp, lc, ln, cp_, cc_, cn_, b_ref, dk_ref, dv_ref):
        qm = _per_head_rows(_pairs(_band((qp, qc, qn))))
        dom = _per_head_rows(_pairs(_band((dp_, dc_, dn_))))
        lsev = _per_head_cols(_band((lp, lc, ln)))
        cpv = _per_head_cols(_band((cp_, cc_, cn_)))
        kv, vv = _pairs(k_ref[...]), _pairs(v_ref[...])
        sc = _bdot(qm, kv, BNT) + b_ref[0].reshape(N_PAIRS, 2 * ATT_BK, ATT_BQ)
        p = jnp.exp(sc - lsev)
        _store_pairs(dv_ref, _bdot(p, dom, BTN))
        ds = p * (_bdot(dom, vv, BNT) + cpv)
        _store_pairs(dk_ref, _bdot(ds, qm, BTN))

    cur = pl.BlockSpec((ATT_BQ, SWA_WIDTH), lambda r, t: (t, r))
    bspec = pl.BlockSpec((1, SWA_HEADS, ATT_BK, ATT_BQ), lambda r, t: (_tile_variant(t, nb), 0, 0, 0))
    dk, dv = pl.pallas_call(
        body, name=f"att_dkv_d{dilation}", grid=(dilation, nb),
        in_specs=[cur, cur] + _band_specs(length) * 4 + [bspec],
        out_specs=[cur, cur],
        out_shape=[jax.ShapeDtypeStruct(view, BF16)] * 2,
        compiler_params=_params("parallel", "parallel"),
    )(k.reshape(view), v.reshape(view), *([q.reshape(view)] * 3), *([dop.reshape(view)] * 3),
      *([lse.reshape(view)] * 3), *([cp.reshape(view)] * 3), bias_t)
    return dk.reshape(s, SWA_WIDTH), dv.reshape(s, SWA_WIDTH)


def _pattern_weights(lses):
    m = lses[0]
    for l in lses[1:]:
        m = jnp.maximum(m, l)
    es = [jnp.exp(l - m) for l in lses]
    den = es[0]
    for e in es[1:]:
        den = den + e
    return [e / den for e in es]


def _combine_fwd(outs, lses):
    s = outs[0].shape[0]
    tm = min(512, s)
    npat = len(outs)

    def body(*refs):
        ws = _pattern_weights([r[...] for r in refs[npat:2 * npat]])
        o = ws[0] * refs[0][...]
        for p in range(1, npat):
            o = o + ws[p] * refs[p][...]
        refs[2 * npat][...] = o.astype(BF16)

    blk = pl.BlockSpec((tm, SWA_WIDTH), lambda i: (i, 0))
    return pl.pallas_call(
        body, name="swa_combine_fwd", grid=(s // tm,), in_specs=[blk] * (2 * npat), out_specs=blk,
        out_shape=jax.ShapeDtypeStruct((s, SWA_WIDTH), BF16), compiler_params=_params("parallel"),
    )(*outs, *lses)


def _combine_bwd(d_out, outs, lses, bd):
    s = d_out.shape[0]
    tm = min(512, s)
    npat = len(outs)

    def body(*refs):
        d_ref, bd_ref = refs[0], refs[1 + 2 * npat]
        o_refs, l_refs = refs[1:1 + npat], refs[1 + npat:1 + 2 * npat]
        out_refs = refs[2 + 2 * npat:]
        ws = _pattern_weights([r[...] for r in l_refs])
        dov = d_ref[...]
        o = ws[0] * o_refs[0][...]
        for p in range(1, npat):
            o = o + ws[p] * o_refs[p][...]
        rd = _group_sum(dov * o, bd_ref[...])
        for p in range(npat):
            out_refs[p][...] = (ws[p] * dov).astype(BF16)
            out_refs[npat + p][...] = -ws[p] * rd

    blk = pl.BlockSpec((tm, SWA_WIDTH), lambda i: (i, 0))
    res = pl.pallas_call(
        body, name="swa_combine_bwd", grid=(s // tm,),
        in_specs=[blk] * (1 + 2 * npat) + [pl.BlockSpec((SWA_WIDTH, SWA_WIDTH), lambda i: (0, 0))],
        out_specs=[blk] * (2 * npat),
        out_shape=[jax.ShapeDtypeStruct((s, SWA_WIDTH), BF16)] * npat + [jax.ShapeDtypeStruct((s, SWA_WIDTH), F32)] * npat,
        compiler_params=_params("parallel"),
    )(d_out, *outs, *lses, bd)
    return res[:npat], res[npat:]


def _rel_bias_grad(dbs, buckets):
    npat = len(dbs)

    def body(*refs):
        db_refs, bk_refs, o_ref = refs[:npat], refs[npat:2 * npat], refs[2 * npat]
        row = lax.broadcasted_iota(jnp.int32, (REL_BUCKETS, LANE), 0)
        lane = lax.broadcasted_iota(jnp.int32, (REL_BUCKETS, LANE), 1)
        tiles = [[db_refs[p][0, h] + db_refs[p][1, h] + db_refs[p][2, h] + db_refs[p][3, h] for h in range(SWA_HEADS)]
                 for p in range(npat)]
        bks = [r[...] for r in bk_refs]

        def one_bucket(b, acc):
            for h in range(SWA_HEADS):
                tot = jnp.zeros((1, 1), F32)
                for p in range(npat):
                    sel = jnp.where(bks[p] == b, tiles[p][h], 0.0)
                    tot = tot + jnp.sum(jnp.sum(sel, axis=1, keepdims=True), axis=0, keepdims=True)
                acc = acc + jnp.where((row == b) & (lane == h), tot, 0.0)
            return acc

        o_ref[...] = lax.fori_loop(0, REL_BUCKETS, one_bucket, jnp.zeros((REL_BUCKETS, LANE), F32))

    full4 = pl.BlockSpec((4, SWA_HEADS, ATT_BQ, ATT_BK), lambda: (0, 0, 0, 0))
    full2 = pl.BlockSpec((ATT_BQ, ATT_BK), lambda: (0, 0))
    return pl.pallas_call(
        body, name="rel_bias_grad", in_specs=[full4] * npat + [full2] * npat,
        out_specs=pl.BlockSpec((REL_BUCKETS, LANE), lambda: (0, 0)),
        out_shape=jax.ShapeDtypeStruct((REL_BUCKETS, LANE), F32),
        compiler_params=pltpu.CompilerParams(vmem_limit_bytes=V7X_VMEM_LIMIT_BYTES),
    )(*dbs, *buckets)


def _swa_forward(p_pad, qw_row, kw_row, rel_bias, bd):
    q, k, v = _swa_pre_fwd(p_pad, qw_row, kw_row, bd)
    outs, lses = [], []
    for _, dil in DILATION_PATTERNS:
        o, lse = _att_fwd2(q, k, v, _bias_tiles(rel_bias, dil, True), dil)
        outs.append(o)
        lses.append(lse)
    return _combine_fwd(outs, lses), (q, k, v, outs, lses)


def _swa_backward(d_out, p_pad, qw_row, kw_row, rel_bias, bd, saved, dp_all):
    q, k, v, outs, lses = saved
    dops, cps = _combine_bwd(d_out, outs, lses, bd)
    dqs, dks, dvs, dbs, buckets = [], [], [], [], []
    for p, (_, dil) in enumerate(DILATION_PATTERNS):
        dq, db = _att_dq2(q, k, v, dops[p], lses[p], cps[p], _bias_tiles(rel_bias, dil, True), dil)
        dk, dv = _att_dkv2(q, k, v, dops[p], lses[p], cps[p], _bias_tiles(rel_bias, dil, False), dil)
        dqs.append(dq)
        dks.append(dk)
        dvs.append(dv)
        dbs.append(db)
        buckets.append(jnp.asarray(_band_tables(dil, True)[1]))
    dp, dqw, dkw = _swa_pre_bwd(dqs, dks, dvs, p_pad, qw_row, kw_row, bd, dp_all)
    return dp, dqw, dkw, _rel_bias_grad(dbs, buckets)


def _lane_row(v):
    flat = v.reshape(-1).astype(F32)
    return jnp.zeros((1, LANE), F32).at[0, :flat.shape[0]].set(flat)


W_IN_SHARD = N_IN // N_DEV
W_IN_RUNS = ((0, NAT_Z, 0), (NAT_Z, NAT_AB, OFF_Z), (NAT_AB, NAT_B, OFF_AB), (NAT_B, N_IN, OFF_B))


def _w_in_pieces(shard):
    lo, hi = shard * W_IN_SHARD, (shard + 1) * W_IN_SHARD
    out = []
    for first, last, dst in W_IN_RUNS:
        a, b = max(lo, first), min(hi, last)
        if a < b:
            out.append((a - lo, b - a, dst + a - first))
    return out


def _w_in_from_slabs(w3):
    nd, r, _ = w3.shape

    def body(w_ref, o_ref):
        o_ref[:, OFF_AB:N_PAD] = jnp.zeros((r, N_PAD - OFF_AB), w3.dtype)
        for sh in range(nd):
            for src, length, dst in _w_in_pieces(sh):
                o_ref[:, dst:dst + length] = w_ref[sh, :, src:src + length]

    return pl.pallas_call(
        body, name="w_in_from_slabs", out_shape=jax.ShapeDtypeStruct((r, N_PAD), w3.dtype),
        compiler_params=pltpu.CompilerParams(vmem_limit_bytes=V7X_VMEM_LIMIT_BYTES),
    )(w3)


def _w_in_grad_slabs(dw_pad, dtype):
    r = dw_pad.shape[0]

    def body(dw_ref, o_ref):
        for sh in range(N_DEV):
            for src, length, dst in _w_in_pieces(sh):
                o_ref[sh, :, src:src + length] = dw_ref[:, dst:dst + length].astype(dtype)

    return pl.pallas_call(
        body, name="w_in_grad_slabs", out_shape=jax.ShapeDtypeStruct((N_DEV, r, W_IN_SHARD), dtype),
        compiler_params=pltpu.CompilerParams(vmem_limit_bytes=V7X_VMEM_LIMIT_BYTES),
    )(dw_pad)


LATE = ("w_out", "ffn2_w_gate", "ffn2_w_up", "ffn2_w_down")
TRANSPOSED = ("ffn1_w_gate", "ffn1_w_up", "ffn2_w_gate", "ffn2_w_up")


def _late_weights(slabs):
    return {n: g.reshape(N_DEV * g.shape[1], g.shape[2]) for n, g in zip(LATE, slabs)}


def _local_step(x, tgt, wts, small, late_shards=None):
    bd = _head_block_diag()
    conv_wt = jnp.zeros((8, QKV_A), F32).at[:CONV_WIDTH].set(small["conv_w"].T)
    alog_row, dt_row = _lane_row(small["a_log"]), _lane_row(small["dt_bias"])
    gnorm_row = small["gdn_norm_w"].reshape(1, GDN_HEAD_DIM)
    qw_row = jnp.tile(small["q_norm_w"].reshape(-1), SWA_HEADS).reshape(1, SWA_WIDTH)
    kw_row = jnp.tile(small["k_norm_w"].reshape(-1), SWA_HEADS).reshape(1, SWA_WIDTH)
    rel_bias = small["rel_bias"]
    exchange = late_shards is not None
    dw_dtype = BF16 if exchange else F32

    x1, sv1, wd1, got = _ffn_forward(
        x, small["ffn1_norm"], wts["ffn1_w_gate"], wts["ffn1_w_up"], wts.get("ffn1_w_down"), "ffn1",
        gather=[late_shards["ffn1_w_down"], late_shards["w_in"]] if exchange else ())
    win_pad = _w_in_from_slabs(got[0]) if exchange else wts["w_in_pad"]
    n2, r2 = _rms_fwd(x1, small["mix_norm"], "mix_norm")
    p_pad = _matmul([(n2, win_pad)], tm=256, tn=N_PAD, tk=D_MODEL, name="w_in")
    o_a, sva, gathered = _gdn_forward(p_pad, conv_wt, alog_row, dt_row, gnorm_row,
                                      gather=[late_shards[n] for n in LATE] if exchange else ())
    if exchange:
        wts = {**wts, **_late_weights(gathered)}
    wo_a, wo_b = wts["w_out"][:GDN_WIDTH], wts["w_out"][GDN_WIDTH:]
    o_b, svb = _swa_forward(p_pad, qw_row, kw_row, rel_bias, bd)
    x2 = _matmul([(o_a, wo_a), (o_b, wo_b)], tm=512, tn=D_MODEL, tk=GDN_WIDTH, name="w_out", res=x1)
    x3, sv2, _, _ = _ffn_forward(x2, small["ffn2_norm"], wts["ffn2_w_gate"], wts["ffn2_w_up"], wts["ffn2_w_down"], "ffn2")
    loss_row, dx3, d_final = _final_loss(x3, small["final_norm"], tgt)

    dx2, d_ffn2_norm, dwg2, dwu2, dwd2, _ = _ffn_backward(
        dx3, x2, small["ffn2_norm"], wts["ffn2_w_gate"], wts["ffn2_w_up"], wts["ffn2_w_down"], sv2, "ffn2", dw_dtype)
    d_oa = _matmul([(dx2, wo_a)], tb=True, tm=512, tn=GDN_WIDTH, tk=D_MODEL, name="w_out_da")
    d_ob = _matmul([(dx2, wo_b)], tb=True, tm=512, tn=SWA_WIDTH, tk=D_MODEL, name="w_out_db")
    dwo_a = _matmul([(o_a, dx2)], ta=True, tm=GDN_WIDTH, tn=D_MODEL, tk=2048, name="w_out_dwa", out_dtype=dw_dtype)
    dwo_b = _matmul([(o_b, dx2)], ta=True, tm=SWA_WIDTH, tn=D_MODEL, tk=2048, name="w_out_dwb", out_dtype=dw_dtype)

    late_grads = [_row_slabs(jnp.concatenate([dwo_a, dwo_b], axis=0)), dwg2, dwu2, dwd2]
    dp_all, dconv, gate_sums, d_gnorm, received = _gdn_backward(
        d_oa, p_pad, conv_wt, alog_row, dt_row, gnorm_row, sva, scatter=late_grads if exchange else ())
    if exchange:
        late_grads = received
    dp_all, dqw, dkw, d_rel = _swa_backward(d_ob, p_pad, qw_row, kw_row, rel_bias, bd, svb, dp_all)
    dw_pad = _matmul([(n2, dp_all)], ta=True, tm=D_MODEL, tn=N_PAD // 3, tk=2048, name="w_in_dw")
    dn2 = _matmul([(dp_all, win_pad)], tb=True, tm=512, tn=D_MODEL, tk=N_PAD, name="w_in_dn")
    dx1, d_mix_norm = _rms_bwd(dn2, x1, r2, small["mix_norm"], dx2, "mix_dnorm")
    d_w_in = _w_in_grad_slabs(dw_pad, dw_dtype)
    dx, d_ffn1_norm, dwg1, dwu1, dwd1, got = _ffn_backward(
        dx1, x, small["ffn1_norm"], wts["ffn1_w_gate"], wts["ffn1_w_up"], wd1, sv1, "ffn1", dw_dtype,
        scatter=[d_w_in] if exchange else None)
    if exchange:
        d_w_in = got[0]

    grads = {
        "ffn1_norm": d_ffn1_norm, "ffn1_w_gate": dwg1, "ffn1_w_up": dwu1, "ffn1_w_down": dwd1,
        "mix_norm": d_mix_norm, "w_in": d_w_in, "conv_w": dconv[:CONV_WIDTH].T,
        "a_log": gate_sums[0, :8].reshape(2, GDN_HEADS), "dt_bias": gate_sums[1, :8].reshape(2, GDN_HEADS),
        "gdn_norm_w": d_gnorm, "q_norm_w": dqw.reshape(SWA_HEADS, SWA_HEAD_DIM).sum(0, keepdims=True),
        "k_norm_w": dkw.reshape(SWA_HEADS, SWA_HEAD_DIM).sum(0, keepdims=True), "rel_bias": d_rel[:, :SWA_HEADS],
        "ffn2_norm": d_ffn2_norm, "final_norm": d_final, **dict(zip(LATE, late_grads)),
    }
    return loss_row, dx, grads


MESH_IDS = pl.DeviceIdType.MESH
ANY = pl.BlockSpec(memory_space=pl.ANY)


def _adamw(parts, w, m, v, name):
    nparts, r, n = parts.shape
    tr = r
    for cand in (256, 176, 128, 104, 64, 8):
        if r % cand == 0:
            tr = cand
            break
    bc1 = 1.0 - ADAM_B1 ** ADAM_STEP
    bc2 = 1.0 - ADAM_B2 ** ADAM_STEP

    def body(p_ref, w_ref, m_ref, v_ref, g_ref, d_ref, nm_ref, nv_ref):
        g = p_ref[0].astype(F32)
        for k in range(1, nparts):
            g = g + p_ref[k].astype(F32)
        mn = ADAM_B1 * m_ref[...] + (1.0 - ADAM_B1) * g
        vn = ADAM_B2 * v_ref[...] + (1.0 - ADAM_B2) * (g * g)
        m_hat = mn / bc1
        v_hat = vn / bc2
        g_ref[...] = g
        nm_ref[...] = mn
        nv_ref[...] = vn
        d_ref[...] = -ADAM_LR * (m_hat / (jnp.sqrt(v_hat) + ADAM_EPS) + ADAM_WD * w_ref[...])

    blk = pl.BlockSpec((tr, n), lambda i: (i, 0))
    return pl.pallas_call(
        body, name=name, grid=(r // tr,),
        in_specs=[pl.BlockSpec((nparts, tr, n), lambda i: (0, i, 0)), blk, blk, blk],
        out_specs=[blk] * 4, out_shape=[jax.ShapeDtypeStruct((r, n), F32)] * 4,
        compiler_params=_params("parallel"),
    )(parts, w, m, v)


def _mesh_place():
    x, y, c = lax.axis_index("x"), lax.axis_index("y"), lax.axis_index("c")
    return x, y, c, [(1 - x, y), (x, 1 - y), (1 - x, 1 - y)]


def _gather_phases(x_refs, out_refs, send_sems, recv_sems, local_sems):
    na = len(x_refs)

    def place():
        x, y, c, chips = _mesh_place()
        return (x, y, c), (x, y, 1 - c), chips, c

    def slab(i, px, py, pc):
        return out_refs[i].at[4 * px + 2 * py + pc]

    def copy(i, k, block, to, src=None):
        return pltpu.make_async_remote_copy(
            src_ref=slab(i, *block) if src is None else src, dst_ref=slab(i, *block),
            send_sem=send_sems.at[i, k], recv_sem=recv_sems.at[i, k], device_id=to, device_id_type=MESH_IDS)

    def own(i, me):
        return pltpu.make_async_copy(x_refs[i], slab(i, *me), local_sems.at[i])

    def sends(i, me, sibling, chips, c):
        return [copy(i, 0, me, sibling, src=x_refs[i])] + [copy(i, 1 + j, me, (*chip, c), src=x_refs[i])
                                                          for j, chip in enumerate(chips)]

    def start():
        me, sibling, chips, c = place()
        for i in range(na):
            own(i, me).start()
            for cp in sends(i, me, sibling, chips, c):
                cp.start()

    def forward():
        me, sibling, chips, c = place()
        for j, chip in enumerate(chips):
            for i in range(na):
                copy(i, 1 + j, (*chip, c), me).wait_recv()
                copy(i, 4 + j, (*chip, c), sibling).start()

    def finish():
        me, sibling, chips, c = place()
        for i in range(na):
            copy(i, 0, sibling, me).wait_recv()
        for j, chip in enumerate(chips):
            for i in range(na):
                copy(i, 4 + j, (*chip, 1 - c), me).wait_recv()
        for i in range(na):
            for cp in sends(i, me, sibling, chips, c):
                cp.wait_send()
            for j, chip in enumerate(chips):
                copy(i, 4 + j, (*chip, c), sibling).wait_send()
            own(i, me).wait()

    return start, forward, finish


def _gather_semaphores(na):
    return [pltpu.SemaphoreType.DMA((na, 7)), pltpu.SemaphoreType.DMA((na, 7)), pltpu.SemaphoreType.DMA((na,))]


def _scatter_phases(g_refs, out_refs, send_sems, recv_sems, local_sems):
    na = len(g_refs)

    def place(m):
        x, y, c = lax.axis_index("x"), lax.axis_index("y"), lax.axis_index("c")
        px = 1 - x if m & 4 else x
        py = 1 - y if m & 2 else y
        pc = 1 - c if m & 1 else c
        return 4 * x + 2 * y + c, (px, py, pc), 4 * px + 2 * py + pc

    def own(i):
        me, _, _ = place(0)
        return pltpu.make_async_copy(g_refs[i].at[me], out_refs[i].at[me], local_sems.at[i])

    def start():
        for i in range(na):
            own(i).start()
            for m in range(1, N_DEV):
                me, peer, peer_idx = place(m)
                pltpu.make_async_remote_copy(
                    src_ref=g_refs[i].at[peer_idx], dst_ref=out_refs[i].at[me], send_sem=send_sems.at[i, m - 1],
                    recv_sem=recv_sems.at[i, m - 1], device_id=peer, device_id_type=MESH_IDS).start()

    def finish():
        for i in range(na):
            for m in range(1, N_DEV):
                me, peer, peer_idx = place(m)
                cp = pltpu.make_async_remote_copy(
                    src_ref=g_refs[i].at[peer_idx], dst_ref=out_refs[i].at[peer_idx], send_sem=send_sems.at[i, m - 1],
                    recv_sem=recv_sems.at[i, m - 1], device_id=peer, device_id_type=MESH_IDS)
                cp.wait_recv()
                cp.wait_send()
            own(i).wait()

    return start, finish


def _all_gather_many(vs, name):
    na = len(vs)

    def body(*refs):
        x_refs, out_refs = refs[:na], refs[na:2 * na]
        for step in _gather_phases(x_refs, out_refs, *refs[2 * na:]):
            step()

    return pl.pallas_call(
        body, name=name, in_specs=[ANY] * na, out_specs=[ANY] * na,
        out_shape=[jax.ShapeDtypeStruct((N_DEV,) + v.shape, v.dtype) for v in vs],
        scratch_shapes=_gather_semaphores(na),
        compiler_params=pltpu.CompilerParams(vmem_limit_bytes=V7X_VMEM_LIMIT_BYTES),
    )(*vs)


BIG = ("ffn1_w_gate", "ffn1_w_up", "ffn1_w_down", "w_in", "w_out", "ffn2_w_gate", "ffn2_w_up", "ffn2_w_down")
SMALL = ("ffn1_norm", "mix_norm", "a_log", "dt_bias", "gdn_norm_w", "q_norm_w", "k_norm_w", "rel_bias",
         "ffn2_norm", "final_norm")
WEIGHTS = ("ffn1_norm", "ffn1_w_gate", "ffn1_w_up", "ffn1_w_down", "mix_norm", "w_in", "conv_w", "a_log", "dt_bias",
           "gdn_norm_w", "q_norm_w", "k_norm_w", "rel_bias", "w_out", "ffn2_norm", "ffn2_w_gate", "ffn2_w_up",
           "ffn2_w_down", "final_norm")


def _pack(arrays, width, row_multiple):
    flat = jnp.concatenate([a.reshape(-1) for a in arrays])
    rows = -(-flat.shape[0] // width)
    rows = -(-rows // row_multiple) * row_multiple
    return jnp.pad(flat, (0, rows * width - flat.shape[0])).reshape(rows, width)


def _unpack(packed, shapes):
    flat = packed.reshape(-1)
    out, pos = [], 0
    for shp in shapes:
        size = int(np.prod(shp))
        out.append(flat[pos:pos + size].reshape(shp))
        pos += size
    return out


def kernel(x, ffn1_norm, ffn1_w_gate, ffn1_w_up, ffn1_w_down, mix_norm, w_in, conv_w, a_log, dt_bias, gdn_norm_w, q_norm_w, k_norm_w, rel_bias, w_out, ffn2_norm, ffn2_w_gate, ffn2_w_up, ffn2_w_down, final_norm, loss_target, m_ffn1_norm, m_ffn1_w_gate, m_ffn1_w_up, m_ffn1_w_down, m_mix_norm, m_w_in, m_conv_w, m_a_log, m_dt_bias, m_gdn_norm_w, m_q_norm_w, m_k_norm_w, m_rel_bias, m_w_out, m_ffn2_norm, m_ffn2_w_gate, m_ffn2_w_up, m_ffn2_w_down, m_final_norm, v_ffn1_norm, v_ffn1_w_gate, v_ffn1_w_up, v_ffn1_w_down, v_mix_norm, v_w_in, v_conv_w, v_a_log, v_dt_bias, v_gdn_norm_w, v_q_norm_w, v_k_norm_w, v_rel_bias, v_w_out, v_ffn2_norm, v_ffn2_w_gate, v_ffn2_w_up, v_ffn2_w_down, v_final_norm):
    w = dict(ffn1_norm=ffn1_norm, ffn1_w_gate=ffn1_w_gate, ffn1_w_up=ffn1_w_up, ffn1_w_down=ffn1_w_down, mix_norm=mix_norm, w_in=w_in, conv_w=conv_w, a_log=a_log, dt_bias=dt_bias, gdn_norm_w=gdn_norm_w, q_norm_w=q_norm_w, k_norm_w=k_norm_w, rel_bias=rel_bias, w_out=w_out, ffn2_norm=ffn2_norm, ffn2_w_gate=ffn2_w_gate, ffn2_w_up=ffn2_w_up, ffn2_w_down=ffn2_w_down, final_norm=final_norm)
    mom = dict(ffn1_norm=m_ffn1_norm, ffn1_w_gate=m_ffn1_w_gate, ffn1_w_up=m_ffn1_w_up, ffn1_w_down=m_ffn1_w_down, mix_norm=m_mix_norm, w_in=m_w_in, conv_w=m_conv_w, a_log=m_a_log, dt_bias=m_dt_bias, gdn_norm_w=m_gdn_norm_w, q_norm_w=m_q_norm_w, k_norm_w=m_k_norm_w, rel_bias=m_rel_bias, w_out=m_w_out, ffn2_norm=m_ffn2_norm, ffn2_w_gate=m_ffn2_w_gate, ffn2_w_up=m_ffn2_w_up, ffn2_w_down=m_ffn2_w_down, final_norm=m_final_norm)
    var = dict(ffn1_norm=v_ffn1_norm, ffn1_w_gate=v_ffn1_w_gate, ffn1_w_up=v_ffn1_w_up, ffn1_w_down=v_ffn1_w_down, mix_norm=v_mix_norm, w_in=v_w_in, conv_w=v_conv_w, a_log=v_a_log, dt_bias=v_dt_bias, gdn_norm_w=v_gdn_norm_w, q_norm_w=v_q_norm_w, k_norm_w=v_k_norm_w, rel_bias=v_rel_bias, w_out=v_w_out, ffn2_norm=v_ffn2_norm, ffn2_w_gate=v_ffn2_w_gate, ffn2_w_up=v_ffn2_w_up, ffn2_w_down=v_ffn2_w_down, final_norm=v_final_norm)
    ix, iy, ic = lax.axis_index("x"), lax.axis_index("y"), lax.axis_index("c")
    me = 4 * ix + 2 * iy + ic

    def local(a, n):
        return jnp.swapaxes(a[0], 0, 1) if n in TRANSPOSED else a[0]

    shard = {n: local(w[n], n) for n in BIG}

    conv_shard_shape = w["conv_w"][0].shape
    conv_elems = conv_shard_shape[0] * conv_shard_shape[1]
    first = ("ffn1_w_gate", "ffn1_w_up")
    gathered = _all_gather_many([shard[n].astype(BF16) for n in first] + [_pack([w["conv_w"][0]], LANE, 8)],
                                "gather_weights")
    wts = {n: g.reshape(N_DEV * g.shape[1], g.shape[2]) for n, g in zip(first, gathered)}

    small = {n: w[n][0] if n not in ("rel_bias",) else w[n] for n in SMALL}
    small = {n: (a.reshape(1, -1) if n.endswith("norm") else a) for n, a in small.items()}
    conv_all = gathered[-1].reshape(N_DEV, -1)
    small["conv_w"] = conv_all[:, :conv_elems].reshape(N_DEV * conv_shard_shape[0], conv_shard_shape[1])
    loss_row, grad_x, grads = _local_step(x[0], loss_target[0], wts, small,
                                          late_shards={n: shard[n].astype(BF16) for n in BIG if n not in first})

    big_out = [[], [], [], []]
    for n in BIG:
        for kind, val in enumerate(_adamw(grads[n], shard[n], local(mom[n], n), local(var[n], n), f"{n}_adamw")):
            big_out[kind].append(jnp.swapaxes(val, 0, 1) if n in TRANSPOSED else val)

    small_names = SMALL + ("conv_w",)
    small_shapes = [grads[n].shape for n in small_names] + [(1, 1)]
    g_small = _pack([grads[n] for n in small_names] + [loss_row[:, :1]], LANE, 8)
    all_small = _all_gather_many([g_small], "gather_small_grads")[0]
    riders = [jnp.zeros(shp, F32) for shp in small_shapes[len(SMALL):]]
    ws = _pack([w[n].reshape(grads[n].shape) for n in SMALL] + riders, LANE, 8)
    ms = _pack([mom[n].reshape(grads[n].shape) for n in SMALL] + riders, LANE, 8)
    vs = _pack([var[n].reshape(grads[n].shape) for n in SMALL] + riders, LANE, 8)
    small_out = [_unpack(a, small_shapes) for a in _adamw(all_small, ws, ms, vs, "adamw_small")]
    loss = small_out[0][-1][0, 0]
    conv_g = lax.dynamic_slice_in_dim(small_out[0][len(SMALL)], me * conv_shard_shape[0], conv_shard_shape[0], axis=0)
    conv_out = [_unpack(a, [conv_shard_shape])[0] for a in _adamw(
        _pack([conv_g], LANE, 8)[None], _pack([w["conv_w"][0]], LANE, 8), _pack([mom["conv_w"][0]], LANE, 8),
        _pack([var["conv_w"][0]], LANE, 8), "adamw_conv")]

    def leaf(kind, n):
        if n in BIG:
            val = big_out[kind][BIG.index(n)]
        elif n == "conv_w":
            val = conv_out[kind]
        else:
            val = small_out[kind][SMALL.index(n)]
        return val.reshape(w[n].shape)

    outs = [loss, grad_x[None]]
    for kind in range(4):
        outs += [leaf(kind, n) for n in WEIGHTS]
    return tuple(outs)
```

```python
import functools
import math

import numpy as np
import jax
import jax.numpy as jnp
from jax import lax
from jax.experimental import pallas as pl
from jax.experimental.pallas import tpu as pltpu

F32 = jnp.float32
BF16 = jnp.bfloat16

D_MODEL = 1024
D_FF = 2816
GDN_HEADS = 4
GDN_HEAD_DIM = 128
GDN_WIDTH = 512
CONV_WIDTH = 5
CHUNK = 64
SWA_HEADS = 8
SWA_HEAD_DIM = 64
SWA_WIDTH = 512
DILATION_PATTERNS = ((128, 1), (512, 4), (2048, 16))
REL_BUCKETS = 32
REL_MAX_DISTANCE = 1024
EPS = 1e-6
NEG_BIG = -1e30
N_DEV = 8

ADAM_LR = 0.001
ADAM_B1 = 0.9
ADAM_B2 = 0.999
ADAM_EPS = 1e-08
ADAM_WD = 0.01
ADAM_STEP = 10

QKV_A = 3 * GDN_WIDTH
OFF_B = QKV_A
OFF_Z = OFF_B + 3 * SWA_WIDTH
OFF_AB = OFF_Z + GDN_WIDTH
N_PAD = OFF_AB + 256
N_IN = 3600
NAT_Z, NAT_AB, NAT_B = QKV_A, QKV_A + GDN_WIDTH, QKV_A + GDN_WIDTH + 16

V7X_VMEM_LIMIT_BYTES = 56 * 1024 * 1024
LANE = 128
ATT_BQ = 128
ATT_HALO = 64
CONV_ROWS = 256

NN = (((1,), (0,)), ((), ()))
NT = (((1,), (1,)), ((), ()))
TN = (((0,), (0,)), ((), ()))


def _params(*sem):
    return pltpu.CompilerParams(dimension_semantics=sem, vmem_limit_bytes=V7X_VMEM_LIMIT_BYTES)


def _dot(a, b, dn=NN):
    return lax.dot_general(a.astype(BF16), b.astype(BF16), dn, preferred_element_type=F32)


def _sigmoid(x):
    return 1.0 / (1.0 + jnp.exp(-x))


class _Exchange:
    def __init__(self, kind, arrays):
        self.kind, self.arrays = kind, list(arrays)

    def out_shape(self):
        lead = (N_DEV,) if self.kind == "gather" else ()
        return [jax.ShapeDtypeStruct(lead + v.shape, v.dtype) for v in self.arrays]

    def hooks(self, in_refs, out_refs, sems, grid):
        step = pl.program_id(0)
        for axis in range(1, len(grid)):
            step = step * grid[axis] + pl.program_id(axis)
        total = math.prod(grid)
        if self.kind == "gather":
            assert total >= 4
            start, forward, finish = _gather_phases(in_refs, out_refs, *sems)
            pl.when(step == total // 2)(forward)
        else:
            assert total >= 2
            start, finish = _scatter_phases(in_refs, out_refs, *sems)
        pl.when(step == 0)(start)
        pl.when(step == total - 1)(finish)


def _pallas(body, *, name, grid, in_specs, out_specs, out_shape, args, semantics, scratch_shapes=(), exchange=None):
    n_in, n_out, n_scr = len(in_specs), len(out_specs), len(scratch_shapes)
    if exchange is None:
        res = pl.pallas_call(
            body, name=name, grid=grid, in_specs=list(in_specs), out_specs=list(out_specs), out_shape=list(out_shape),
            scratch_shapes=list(scratch_shapes), compiler_params=_params(*semantics))(*args)
        return list(res), []
    na = len(exchange.arrays)

    def carrying(*refs):
        ins, sent = refs[:n_in], refs[n_in:n_in + na]
        outs = refs[n_in + na:n_in + na + n_out]
        landed = refs[n_in + na + n_out:n_in + 2 * na + n_out]
        rest = refs[n_in + 2 * na + n_out:]
        exchange.hooks(sent, landed, rest[n_scr:], grid)
        body(*ins, *outs, *rest[:n_scr])

    res = pl.pallas_call(
        carrying, name=name, grid=grid, in_specs=list(in_specs) + [ANY] * na, out_specs=list(out_specs) + [ANY] * na,
        out_shape=list(out_shape) + exchange.out_shape(), scratch_shapes=list(scratch_shapes) + _gather_semaphores(na),
        compiler_params=_params(*(["arbitrary"] * len(grid))))(*args, *exchange.arrays)
    return list(res[:n_out]), list(res[n_out:])


def _matmul(pairs, *, ta=False, tb=False, out_dtype=F32, tm, tn, tk, name, res=None, alpha=None, shard_cols=None,
            exchange=None):
    a0, b0 = pairs[0]
    m = a0.shape[1] if ta else a0.shape[0]
    k = a0.shape[0] if ta else a0.shape[1]
    n = b0.shape[0] if tb else b0.shape[1]
    tm, tn, tk = min(tm, m), min(tn, n), min(tk, k)
    assert m % tm == 0 and n % tn == 0 and k % tk == 0, (name, m, n, k, tm, tn, tk)
    nk = k // tk
    npairs = len(pairs)
    dn = (((0 if ta else 1,), (1 if tb else 0,)), ((), ()))

    def body(*refs):
        ins = refs[:2 * npairs]
        pos = 2 * npairs
        r_ref = None
        if res is not None:
            r_ref = refs[pos]
            pos += 1
        o_ref, acc = refs[pos], refs[pos + 1]
        kk = pl.program_id(2)
        t = None
        for p in range(npairs):
            d = _dot(ins[2 * p][...], ins[2 * p + 1][...], dn)
            t = d if t is None else t + d

        if nk > 1:
            @pl.when(kk == 0)
            def _():
                acc[...] = t

            @pl.when((kk > 0) & (kk < nk - 1))
            def _():
                acc[...] += t

        @pl.when(kk == nk - 1)
        def _():
            r = acc[...] + t if nk > 1 else t
            if alpha is not None:
                r = r * alpha
            if r_ref is not None:
                r = r_ref[...] + r
            if shard_cols is None:
                o_ref[...] = r.astype(out_dtype)
            else:
                for sh in range(tn // shard_cols):
                    o_ref[sh] = r[:, sh * shard_cols:(sh + 1) * shard_cols].astype(out_dtype)

    a_spec = pl.BlockSpec((tk, tm), lambda i, j, kk: (kk, i)) if ta else pl.BlockSpec((tm, tk), lambda i, j, kk: (i, kk))
    b_spec = pl.BlockSpec((tn, tk), lambda i, j, kk: (j, kk)) if tb else pl.BlockSpec((tk, tn), lambda i, j, kk: (kk, j))
    o_spec = pl.BlockSpec((tm, tn), lambda i, j, kk: (i, j))
    in_specs = [a_spec, b_spec] * npairs + ([o_spec] if res is not None else [])
    args = [t for pr in pairs for t in pr] + ([res] if res is not None else [])
    out_spec, out_shape = o_spec, (m, n)
    if shard_cols is not None:
        assert res is None and tn % shard_cols == 0
        out_spec = pl.BlockSpec((tn // shard_cols, tm, shard_cols), lambda i, j, kk: (j, i, 0))
        out_shape = (n // shard_cols, m, shard_cols)
    (out,), exchanged = _pallas(
        body, name=name, grid=(m // tm, n // tn, nk), in_specs=in_specs, out_specs=[out_spec],
        out_shape=[jax.ShapeDtypeStruct(out_shape, out_dtype)],
        scratch_shapes=[pltpu.VMEM((tm, tn) if nk > 1 else (8, LANE), F32)],
        semantics=("parallel", "parallel", "arbitrary"), args=args, exchange=exchange)
    return out if exchange is None else (out, exchanged)


def _rms_fwd(x, w, name):
    s, d = x.shape
    tm = min(512, s)

    def body(x_ref, w_ref, n_ref, r_ref):
        xv = x_ref[...]
        r = lax.rsqrt(jnp.mean(xv * xv, axis=-1, keepdims=True) + EPS)
        n_ref[...] = (xv * r * w_ref[...]).astype(BF16)
        r_ref[...] = r

    return pl.pallas_call(
        body, name=name, grid=(s // tm,),
        in_specs=[pl.BlockSpec((tm, d), lambda i: (i, 0)), pl.BlockSpec((1, d), lambda i: (0, 0))],
        out_specs=[pl.BlockSpec((tm, d), lambda i: (i, 0)), pl.BlockSpec((tm, 1), lambda i: (i, 0))],
        out_shape=[jax.ShapeDtypeStruct((s, d), BF16), jax.ShapeDtypeStruct((s, 1), F32)],
        compiler_params=_params("parallel"),
    )(x, w)


def _rms_bwd(dn, x, r, w, dres, name, exchange=None):
    s, d = x.shape
    tm = min(512, s)

    def body(dn_ref, x_ref, r_ref, w_ref, dres_ref, dx_ref, dw_ref):
        @pl.when(pl.program_id(0) == 0)
        def _():
            dw_ref[...] = jnp.zeros_like(dw_ref)

        rv = r_ref[...]
        xhat = x_ref[...] * rv
        g = dn_ref[...]
        t = g * w_ref[...]
        dx_ref[...] = dres_ref[...] + rv * (t - xhat * jnp.mean(t * xhat, axis=-1, keepdims=True))
        dw_ref[...] += jnp.sum(g * xhat, axis=0, keepdims=True)

    row = pl.BlockSpec((tm, d), lambda i: (i, 0))
    vec = pl.BlockSpec((1, d), lambda i: (0, 0))
    (dx, dw), exchanged = _pallas(
        body, name=name, grid=(s // tm,),
        in_specs=[row, row, pl.BlockSpec((tm, 1), lambda i: (i, 0)), vec, row],
        out_specs=[row, vec],
        out_shape=[jax.ShapeDtypeStruct((s, d), F32), jax.ShapeDtypeStruct((1, d), F32)],
        semantics=("arbitrary",), args=(dn, x, r, w, dres), exchange=exchange)
    return (dx, dw) if exchange is None else (dx, dw, exchanged)


def _final_loss(x3, wf, tgt):
    s, d = x3.shape
    tm = min(512, s)

    def body(x_ref, w_ref, t_ref, loss_ref, dx_ref, dw_ref):
        @pl.when(pl.program_id(0) == 0)
        def _():
            dw_ref[...] = jnp.zeros_like(dw_ref)
            loss_ref[...] = jnp.zeros_like(loss_ref)

        xv = x_ref[...]
        wv = w_ref[...]
        r = lax.rsqrt(jnp.mean(xv * xv, axis=-1, keepdims=True) + EPS)
        xhat = xv * r
        e = xhat * wv - t_ref[...]
        part = 0.5 * jnp.sum(jnp.mean(e * e, axis=-1, keepdims=True), axis=0, keepdims=True)
        loss_ref[...] += jnp.broadcast_to(part, loss_ref.shape)
        dy = e * (1.0 / d)
        dw_ref[...] += jnp.sum(dy * xhat, axis=0, keepdims=True)
        t = dy * wv
        dx_ref[...] = r * (t - xhat * jnp.mean(t * xhat, axis=-1, keepdims=True))

    row = pl.BlockSpec((tm, d), lambda i: (i, 0))
    vec = pl.BlockSpec((1, d), lambda i: (0, 0))
    return pl.pallas_call(
        body, name="final_loss", grid=(s // tm,),
        in_specs=[row, vec, row],
        out_specs=[pl.BlockSpec((1, LANE), lambda i: (0, 0)), row, vec],
        out_shape=[jax.ShapeDtypeStruct((1, LANE), F32), jax.ShapeDtypeStruct((s, d), F32),
                   jax.ShapeDtypeStruct((1, d), F32)],
        compiler_params=_params("arbitrary"),
    )(x3, wf, tgt)


def _ffn_up(n, wg, wu, name, exchange=None):
    s, d = n.shape
    f = wg.shape[0]
    tm, tn = min(512, s), f // 2

    def body(n_ref, wg_ref, wu_ref, g_ref, u_ref, a_ref):
        nv = n_ref[...]
        g = _dot(nv, wg_ref[...], NT)
        u = _dot(nv, wu_ref[...], NT)
        g_ref[...] = g.astype(BF16)
        u_ref[...] = u.astype(BF16)
        a_ref[...] = (g * _sigmoid(g) * u).astype(BF16)

    o = pl.BlockSpec((tm, tn), lambda j, i: (i, j))
    wspec = pl.BlockSpec((tn, d), lambda j, i: (j, 0))
    return _pallas(
        body, name=name, grid=(f // tn, s // tm),
        in_specs=[pl.BlockSpec((tm, d), lambda j, i: (i, 0)), wspec, wspec],
        out_specs=[o, o, o],
        out_shape=[jax.ShapeDtypeStruct((s, f), BF16)] * 3,
        semantics=("parallel", "parallel"), args=(n, wg, wu), exchange=exchange)


def _ffn_dact(dx, wd, g, u, name, exchange=None):
    s, d = dx.shape
    f = wd.shape[0]
    tm, tn = min(512, s), f // 2

    def body(dx_ref, wd_ref, g_ref, u_ref, dg_ref, du_ref):
        da = 0.5 * _dot(dx_ref[...], wd_ref[...], NT)
        gv = g_ref[...].astype(F32)
        sg = _sigmoid(gv)
        du_ref[...] = (da * gv * sg).astype(BF16)
        dg_ref[...] = (da * u_ref[...].astype(F32) * (sg * (1.0 + gv * (1.0 - sg)))).astype(BF16)

    o = pl.BlockSpec((tm, tn), lambda j, i: (i, j))
    return _pallas(
        body, name=name, grid=(f // tn, s // tm),
        in_specs=[pl.BlockSpec((tm, d), lambda j, i: (i, 0)), pl.BlockSpec((tn, d), lambda j, i: (j, 0)), o, o],
        out_specs=[o, o],
        out_shape=[jax.ShapeDtypeStruct((s, f), BF16), jax.ShapeDtypeStruct((s, f), BF16)],
        semantics=("parallel", "parallel"), args=(dx, wd, g, u), exchange=exchange)


def _row_slabs(full):
    return full.reshape(N_DEV, full.shape[0] // N_DEV, full.shape[1])


def _ffn_forward(x, norm_w, wg, wu, wd, tag, gather=()):
    n, r = _rms_fwd(x, norm_w, f"{tag}_norm")
    (g, u, a), got = _ffn_up(n, wg, wu, f"{tag}_up", _Exchange("gather", gather) if gather else None)
    if wd is None:
        wd, got = got[0].reshape(N_DEV * got[0].shape[1], got[0].shape[2]), got[1:]
    y = _matmul([(a, wd)], tm=512, tn=1024, tk=wd.shape[0], name=f"{tag}_down", res=x, alpha=0.5)
    return y, (n, r, g, u, a), wd, got


def _ffn_backward(dy, x, norm_w, wgt, wut, wd, saved, tag, dw_dtype=F32, scatter=None):
    n, r, g, u, a = saved

    def behind(arrays):
        return _Exchange("scatter", arrays) if scatter is not None else None

    def dw(act, grad, name, alpha=None, exchange=None):
        return _matmul([(act, grad)], ta=True, tm=1408, tn=1024, tk=2048, name=name, alpha=alpha, out_dtype=dw_dtype,
                       exchange=exchange)

    dwd = _row_slabs(dw(a, dy, f"{tag}_dwd", alpha=0.5))
    (dg, du), extras = _ffn_dact(dy, wd, g, u, f"{tag}_dact", behind(scatter))
    if scatter is None:
        dwg, dwu = _row_slabs(dw(dg, n, f"{tag}_dwg")), _row_slabs(dw(du, n, f"{tag}_dwu"))
    else:
        dwg, (dwd,) = dw(dg, n, f"{tag}_dwg", exchange=behind([dwd]))
        dwu, (dwg,) = dw(du, n, f"{tag}_dwu", exchange=behind([_row_slabs(dwg)]))
        dwu = _row_slabs(dwu)
    dn = _matmul([(dg, wgt), (du, wut)], tm=512, tn=1024, tk=wgt.shape[0], name=f"{tag}_dn", exchange=behind([dwu]))
    if scatter is not None:
        dn, (dwu,) = dn
    dx, dnorm = _rms_bwd(dn, x, r, norm_w, dy, f"{tag}_dnorm")
    return dx, dnorm, dwg, dwu, dwd, extras


Q_SCALE = GDN_HEAD_DIM ** -0.5
CONV_HALO = 8


def _lane_block(s):
    return pl.BlockSpec((None, s, LANE), lambda j: (j, 0, 0))


def _conv_taps(win, w_ref, rows, sign):
    n = rows + 2 * CONV_HALO
    acc = None
    for t in range(CONV_WIDTH):
        o = sign * (t - CONV_WIDTH // 2)
        sh = win if o == 0 else pltpu.roll(win, (-o) % n, 0)
        term = sh[CONV_HALO:CONV_HALO + rows] * w_ref[t:t + 1, :]
        acc = term if acc is None else acc + term
    return acc


def _gdn_conv_fwd(p_pad, conv_wt):
    s = p_pad.shape[0]
    rows = min(CONV_ROWS, s)
    nblk = QKV_A // LANE

    def body(p_ref, w_ref, c_ref, y_ref, pad):
        j = pl.program_id(0)
        zeros = jnp.zeros((CONV_HALO, LANE), F32)
        pad[0:CONV_HALO, :] = zeros
        pad[CONV_HALO + s:2 * CONV_HALO + s, :] = zeros
        pad[CONV_HALO:CONV_HALO + s, :] = p_ref[...]

        def chunk(ci, carry):
            b = pl.multiple_of(ci * rows, rows)
            win = pad[pl.ds(b, rows + 2 * CONV_HALO), :]
            c = _conv_taps(win, w_ref, rows, 1)
            c_ref[pl.ds(b, rows), :] = c
            act = c * _sigmoid(c)
            nrm = lax.rsqrt(jnp.sum(act * act, axis=-1, keepdims=True) + EPS)
            mult = jnp.where(j < GDN_HEADS, nrm * Q_SCALE, jnp.where(j < 2 * GDN_HEADS, nrm, 1.0))
            y_ref[pl.ds(b, rows), :] = act * mult
            return carry

        lax.fori_loop(0, s // rows, chunk, 0)

    col = pl.BlockSpec((s, LANE), lambda j: (0, j))
    return pl.pallas_call(
        body, name="gdn_conv_fwd", grid=(nblk,),
        in_specs=[col, pl.BlockSpec((8, LANE), lambda j: (0, j))],
        out_specs=[_lane_block(s), _lane_block(s)],
        out_shape=[jax.ShapeDtypeStruct((nblk, s, LANE), F32), jax.ShapeDtypeStruct((nblk, s, LANE), F32)],
        scratch_shapes=[pltpu.VMEM((s + 2 * CONV_HALO, LANE), F32)],
        compiler_params=_params("parallel"),
    )(p_pad, conv_wt)


def _gdn_conv_bwd(dy_f, dy_r, c_pre, p_pad, conv_wt, dp_all):
    s = p_pad.shape[0]
    rows = min(CONV_ROWS, s)
    nblk = QKV_A // LANE

    def body(dyf_ref, dyr_ref, c_ref, p_ref, w_ref, _, dp_ref, dw_ref, ppad, dcpad):
        j = pl.program_id(0)
        zeros = jnp.zeros((CONV_HALO, LANE), F32)
        for buf in (ppad, dcpad):
            buf[0:CONV_HALO, :] = zeros
            buf[CONV_HALO + s:2 * CONV_HALO + s, :] = zeros
        ppad[CONV_HALO:CONV_HALO + s, :] = p_ref[...]

        def act_bwd(ci, carry):
            b = pl.multiple_of(ci * rows, rows)
            c = c_ref[pl.ds(b, rows), :]
            g = dyf_ref[pl.ds(b, rows), :] + dyr_ref[pl.ds(b, rows), :]
            sg = _sigmoid(c)
            act = c * sg
            nrm = lax.rsqrt(jnp.sum(act * act, axis=-1, keepdims=True) + EPS)
            yh = act * nrm
            scale = jnp.where(j < GDN_HEADS, Q_SCALE, 1.0)
            dact_qk = (scale * nrm) * (g - yh * jnp.sum(g * yh, axis=-1, keepdims=True))
            dact = jnp.where(j < 2 * GDN_HEADS, dact_qk, g)
            dcpad[pl.ds(pl.multiple_of(b + CONV_HALO, CONV_HALO), rows), :] = dact * (sg * (1.0 + c * (1.0 - sg)))
            return carry

        lax.fori_loop(0, s // rows, act_bwd, 0)
        tap = lax.broadcasted_iota(jnp.int32, (8, LANE), 0)

        def taps_bwd(ci, dw):
            b = pl.multiple_of(ci * rows, rows)
            dcw = dcpad[pl.ds(b, rows + 2 * CONV_HALO), :]
            dp_ref[pl.ds(b, rows), :] = _conv_taps(dcw, w_ref, rows, -1).astype(BF16)
            pw = ppad[pl.ds(b, rows + 2 * CONV_HALO), :]
            dc = dcw[CONV_HALO:CONV_HALO + rows]
            n = rows + 2 * CONV_HALO
            for t in range(CONV_WIDTH):
                o = t - CONV_WIDTH // 2
                sh = pw if o == 0 else pltpu.roll(pw, (-o) % n, 0)
                row = jnp.sum(dc * sh[CONV_HALO:CONV_HALO + rows], axis=0, keepdims=True)
                dw = dw + jnp.where(tap == t, row, 0.0)
            return dw

        dw_ref[...] = lax.fori_loop(0, s // rows, taps_bwd, jnp.zeros((8, LANE), F32))

    col = pl.BlockSpec((s, LANE), lambda j: (0, j))
    wspec = pl.BlockSpec((8, LANE), lambda j: (0, j))
    return pl.pallas_call(
        body, name="gdn_conv_bwd", grid=(nblk,),
        in_specs=[_lane_block(s), _lane_block(s), _lane_block(s), col, wspec, ANY],
        out_specs=[col, wspec],
        out_shape=[jax.ShapeDtypeStruct(dp_all.shape, dp_all.dtype), jax.ShapeDtypeStruct((8, QKV_A), F32)],
        scratch_shapes=[pltpu.VMEM((s + 2 * CONV_HALO, LANE), F32), pltpu.VMEM((s + 2 * CONV_HALO, LANE), F32)],
        input_output_aliases={5: 0},
        compiler_params=_params("parallel"),
    )(dy_f, dy_r, c_pre, p_pad, conv_wt, dp_all)


def _softplus(x):
    return jnp.maximum(x, 0.0) + jnp.log(1.0 + jnp.exp(-jnp.abs(x)))


def _gdn_gates_fwd(p_pad, alog_row, dt_row):
    s = p_pad.shape[0]
    tm = min(1024, s)

    def body(p_ref, al_ref, dt_ref, o_ref):
        x = p_ref[...]
        lane = lax.broadcasted_iota(jnp.int32, x.shape, 1)
        g = -jnp.exp(al_ref[...]) * _softplus(x + dt_ref[...])
        o_ref[...] = jnp.where(lane < 8, g, jnp.where(lane < 16, _sigmoid(x), 0.0))

    vec = pl.BlockSpec((1, LANE), lambda i: (0, 0))
    return pl.pallas_call(
        body, name="gdn_gates_fwd", grid=(s // tm,),
        in_specs=[pl.BlockSpec((tm, LANE), lambda i: (i, OFF_AB // LANE)), vec, vec],
        out_specs=pl.BlockSpec((tm, LANE), lambda i: (i, 0)),
        out_shape=jax.ShapeDtypeStruct((s, LANE), F32),
        compiler_params=_params("parallel"),
    )(p_pad, alog_row, dt_row)


def _gdn_gates_bwd(dgb_f, dgb_r, p_pad, gb, alog_row, dt_row, dp_all):
    s = p_pad.shape[0]
    tm = min(1024, s)
    tail = N_PAD - OFF_AB

    def body(df_ref, dr_ref, p_ref, gb_ref, al_ref, dt_ref, _, dp_ref, sum_ref):
        @pl.when(pl.program_id(0) == 0)
        def _():
            sum_ref[...] = jnp.zeros_like(sum_ref)

        x = p_ref[...]
        gbv = gb_ref[...]
        dgb = df_ref[...] + dr_ref[...]
        lane = lax.broadcasted_iota(jnp.int32, x.shape, 1)
        da = dgb * (-jnp.exp(al_ref[...])) * _sigmoid(x + dt_ref[...])
        db = dgb * gbv * (1.0 - gbv)
        dp_ref[:, 0:LANE] = jnp.where(lane < 8, da, jnp.where(lane < 16, db, 0.0)).astype(BF16)
        dp_ref[:, LANE:tail] = jnp.zeros((tm, tail - LANE), BF16)
        row = lax.broadcasted_iota(jnp.int32, (8, LANE), 0)
        lane8 = lax.broadcasted_iota(jnp.int32, (8, LANE), 1)
        d_alog = jnp.sum(dgb * gbv, axis=0, keepdims=True)
        d_dt = jnp.sum(da, axis=0, keepdims=True)
        upd = jnp.where(row == 0, d_alog, jnp.where(row == 1, d_dt, 0.0))
        sum_ref[...] += jnp.where(lane8 < 8, upd, 0.0)

    vec = pl.BlockSpec((1, LANE), lambda i: (0, 0))
    blk = pl.BlockSpec((tm, LANE), lambda i: (i, 0))
    return pl.pallas_call(
        body, name="gdn_gates_bwd", grid=(s // tm,),
        in_specs=[blk, blk, pl.BlockSpec((tm, LANE), lambda i: (i, OFF_AB // LANE)), blk, vec, vec, ANY],
        out_specs=[pl.BlockSpec((tm, tail), lambda i: (i, OFF_AB // tail)), pl.BlockSpec((8, LANE), lambda i: (0, 0))],
        out_shape=[jax.ShapeDtypeStruct(dp_all.shape, dp_all.dtype), jax.ShapeDtypeStruct((8, LANE), F32)],
        input_output_aliases={6: 0},
        compiler_params=_params("arbitrary"),
    )(dgb_f, dgb_r, p_pad, gb, alog_row, dt_row, dp_all)


def _chunk_masks(rev):
    row = lax.broadcasted_iota(jnp.int32, (CHUNK, CHUNK), 0)
    col = lax.broadcasted_iota(jnp.int32, (CHUNK, CHUNK), 1)
    le = (col >= row) if rev else (col <= row)
    strict = (col > row) if rev else (col < row)
    return le, strict, row == col


def _gate_lanes(rev, h):
    d = 1 if rev else 0
    return d * GDN_HEADS + h, 8 + d * GDN_HEADS + h


BNN = (((2,), (1,)), ((0,), (0,)))
BNT = (((2,), (2,)), ((0,), (0,)))
BTN = (((1,), (1,)), ((0,), (0,)))
NB = 2 * GDN_HEADS
DELTA_CHUNKS = 4


def _bdot(a, b, dn=BNN):
    return lax.dot_general(a.astype(BF16), b.astype(BF16), dn, preferred_element_type=F32)


def _dot3(a, b, dn, exact_a=False, exact_b=False):
    def d(x, y):
        return lax.dot_general(x, y, dn, preferred_element_type=F32)

    ah = a.astype(BF16)
    bh = b.astype(BF16)
    out = d(ah, bh)
    if not exact_b:
        out = out + d(ah, (b - bh.astype(F32)).astype(BF16))
    if not exact_a:
        out = out + d((a - ah.astype(F32)).astype(BF16), bh)
    return out


def _both(f_val, r_val):
    return jnp.stack([f_val] * GDN_HEADS + [r_val] * GDN_HEADS)


def _head_blocks(ref_f, ref_r, rows_f, rows_r):
    return jnp.concatenate([ref_f[:, rows_f, :], ref_r[:, rows_r, :]], axis=0)


def _chunk_rows(c):
    return slice(c * CHUNK, (c + 1) * CHUNK), slice((DELTA_CHUNKS - 1 - c) * CHUNK, (DELTA_CHUNKS - c) * CHUNK)


def _heads(ref_f, ref_r, rows_f, rows_r):
    hd = GDN_HEAD_DIM
    return jnp.stack([ref_f[rows_f, h * hd:(h + 1) * hd] for h in range(GDN_HEADS)]
                     + [ref_r[rows_r, h * hd:(h + 1) * hd] for h in range(GDN_HEADS)])


def _gate_cols(tile_f, tile_r, base):
    return jnp.stack([tile_f[:, base + h:base + h + 1] for h in range(GDN_HEADS)]
                     + [tile_r[:, base + GDN_HEADS + h:base + GDN_HEADS + h + 1] for h in range(GDN_HEADS)])


def _chunk_common2(q, k, v, gbf, gbr):
    mf, mr = _chunk_masks(False), _chunk_masks(True)
    le, strict = _both(mf[0], mr[0]), _both(mf[1], mr[1])
    eye = mf[2]
    gcm_f = _dot3(mf[0].astype(F32), gbf, NN, exact_a=True)
    gcm_r = _dot3(mr[0].astype(F32), gbr, NN, exact_a=True)
    g, beta, gc = _gate_cols(gbf, gbr, 0), _gate_cols(gbf, gbr, 8), _gate_cols(gcm_f, gcm_r, 0)
    gc_row = _dot3(jnp.ones((NB, CHUNK, CHUNK), F32), jnp.where(eye[None], gc, 0.0), BNN, exact_a=True)
    decay = jnp.where(le, jnp.exp(jnp.where(le, gc - gc_row, 0.0)), 0.0)
    eg = jnp.exp(gc)
    gl = jnp.sum(g, axis=1, keepdims=True)
    kb = k * beta
    vb = v * beta
    kbeg = kb * eg
    lm = jnp.where(strict, _bdot(kb, k, BNT) * decay, 0.0)
    intra = _bdot(q, k, BNT) * decay
    edec = jnp.exp(gl - gc)
    return dict(strict=strict, eye=eye, beta=beta, decay=decay, eg=eg, gl=gl, kb=kb, vb=vb, kbeg=kbeg,
                lm=lm, intra=intra, qg=q * eg, edec=edec, kdec=k * edec)


def _unit_triangular_inverse(lm, eye):
    x = -lm
    t = eye[None].astype(F32) + x
    p = x
    for level in range(5):
        prod = functools.partial(_dot3, dn=BNN) if level < 2 else _bdot
        p = prod(p, p)
        t = t + prod(t, p)
    return t


def _delta_fwd2(y, gb, gather=()):
    s = y.shape[1]
    nc = s // CHUNK
    hd = GDN_HEAD_DIM
    na = len(gather)

    def body(*refs):
        qf, kf, vf, gf, qr, kr, vr, gr = refs[:8]
        of_ref, or_ref, sf_all, sr_all, tf_all, tr_all = refs[8 + na:14 + na]
        state = refs[14 + 2 * na]
        step = pl.program_id(0)

        @pl.when(step == 0)
        def _():
            state[...] = jnp.zeros_like(state)

        if na:
            start, forward, finish = _gather_phases(refs[8:8 + na], refs[14 + na:14 + 2 * na], *refs[15 + 2 * na:])
            pl.when(step == 0)(start)
            pl.when(step == ns // 2)(forward)
            pl.when(step == ns - 1)(finish)

        st = state[...]
        for c in range(DELTA_CHUNKS):
            rf, rr = _chunk_rows(c)
            q, k, v = _head_blocks(qf, qr, rf, rr), _head_blocks(kf, kr, rf, rr), _head_blocks(vf, vr, rf, rr)
            cm = _chunk_common2(q, k, v, gf[rf, :], gr[rr, :])
            tinv = _unit_triangular_inverse(cm["lm"], cm["eye"])
            u = _bdot(tinv, cm["vb"])
            w = _bdot(tinv, cm["kbeg"])
            v_new = u - _bdot(w, st)
            o = _bdot(cm["qg"], st) + _bdot(cm["intra"], v_new)
            for h in range(GDN_HEADS):
                of_ref[rf, h * hd:(h + 1) * hd] = o[h]
                or_ref[rr, h * hd:(h + 1) * hd] = o[GDN_HEADS + h]
            sf_all[c] = st[:GDN_HEADS]
            sr_all[DELTA_CHUNKS - 1 - c] = st[GDN_HEADS:]
            tf_all[c] = tinv[:GDN_HEADS]
            tr_all[DELTA_CHUNKS - 1 - c] = tinv[GDN_HEADS:]
            st = st * jnp.exp(cm["gl"]) + _bdot(cm["kdec"], v_new, BTN)
        state[...] = st

    rows = DELTA_CHUNKS * CHUNK
    ns = nc // DELTA_CHUNKS

    def col(j, rev):
        return pl.BlockSpec((GDN_HEADS, rows, hd), (lambda n: (j, ns - 1 - n, 0)) if rev else (lambda n: (j, n, 0)))

    def out(rev):
        return pl.BlockSpec((rows, GDN_WIDTH), (lambda n: (ns - 1 - n, 0)) if rev else (lambda n: (n, 0)))

    def gate(rev):
        return pl.BlockSpec((rows, LANE), (lambda n: (ns - 1 - n, 0)) if rev else (lambda n: (n, 0)))

    def per_chunk(d1, d2, rev):
        return pl.BlockSpec((DELTA_CHUNKS, GDN_HEADS, d1, d2),
                            (lambda n: (ns - 1 - n, 0, 0, 0)) if rev else (lambda n: (n, 0, 0, 0)))

    assert nc % DELTA_CHUNKS == 0 and (na == 0 or ns >= 4)
    res = pl.pallas_call(
        body, name="delta_fwd", grid=(ns,),
        in_specs=[col(0, False), col(1, False), col(2, False), gate(False), col(0, True), col(1, True), col(2, True), gate(True)]
        + [ANY] * na,
        out_specs=[out(False), out(True), per_chunk(hd, hd, False), per_chunk(hd, hd, True),
                   per_chunk(CHUNK, CHUNK, False), per_chunk(CHUNK, CHUNK, True)] + [ANY] * na,
        out_shape=[jax.ShapeDtypeStruct((s, GDN_WIDTH), F32)] * 2 + [jax.ShapeDtypeStruct((nc, GDN_HEADS, hd, hd), F32)] * 2
        + [jax.ShapeDtypeStruct((nc, GDN_HEADS, CHUNK, CHUNK), F32)] * 2
        + [jax.ShapeDtypeStruct((N_DEV,) + v.shape, v.dtype) for v in gather],
        scratch_shapes=[pltpu.VMEM((NB, hd, hd), F32)] + (_gather_semaphores(na) if na else []),
        compiler_params=_params("arbitrary"),
    )(y, y, y, gb, y, y, y, gb, *gather)
    return res[:6], res[6:]


def _delta_bwd2(y, gb, do, sf_all, sr_all, tf_all, tr_all, scatter=()):
    s = y.shape[1]
    nc = s // CHUNK
    hd = GDN_HEAD_DIM
    na = len(scatter)

    def body(*refs):
        qf, kf, vf, gf, dof, sf, tf, qr, kr, vr, gr, dor, sr, tr = refs[:14]
        dyf_ref, dyr_ref, dgf_ref, dgr_ref = refs[14 + na:18 + na]
        dstate = refs[18 + 2 * na]
        step = pl.program_id(0)

        @pl.when(step == 0)
        def _():
            dstate[...] = jnp.zeros_like(dstate)

        if na:
            start, finish = _scatter_phases(refs[14:14 + na], refs[18 + na:18 + 2 * na], *refs[19 + 2 * na:])
            pl.when(step == 0)(start)
            pl.when(step == ns - 1)(finish)

        def one_chunk(c, ds_out):
            rr, rf = _chunk_rows(c)
            cf, cr = DELTA_CHUNKS - 1 - c, c
            q, k, v = _head_blocks(qf, qr, rf, rr), _head_blocks(kf, kr, rf, rr), _head_blocks(vf, vr, rf, rr)
            dov = _heads(dof, dor, rf, rr)
            cm = _chunk_common2(q, k, v, gf[rf, :], gr[rr, :])
            tinv = jnp.concatenate([tf[cf], tr[cr]], axis=0)
            st = jnp.concatenate([sf[cf], sr[cr]], axis=0)
            decay, lm, intra, qg, kdec, kbeg, eg, kb, beta = (
                cm[n] for n in ("decay", "lm", "intra", "qg", "kdec", "kbeg", "eg", "kb", "beta"))
            u = _bdot(tinv, cm["vb"])
            w = _bdot(tinv, kbeg)
            v_new = u - _bdot(w, st)
            egl = jnp.exp(cm["gl"])
            d_qg = _bdot(dov, st, BNT)
            d_intra = _bdot(dov, v_new, BNT)
            dv_new = _bdot(intra, dov, BTN) + _bdot(kdec, ds_out)
            d_kdec = _bdot(v_new, ds_out, BNT)
            ds_in = _bdot(qg, dov, BTN) + egl * ds_out - _bdot(w, dv_new, BTN)
            dgl = egl * jnp.sum(jnp.sum(st * ds_out, axis=2, keepdims=True), axis=1, keepdims=True)
            dw = -_bdot(dv_new, st, BNT)
            dvb = _bdot(tinv, dv_new, BTN)
            dkbeg = _bdot(tinv, dw, BTN)
            dlm = jnp.where(cm["strict"], -(_bdot(dvb, u, BNT) + _bdot(dkbeg, w, BNT)), 0.0)
            d_a = dlm * decay
            d_qk = d_intra * decay
            e = dlm * lm + d_intra * intra
            colsum = _dot3(e, jnp.ones((NB, CHUNK, LANE), F32), BTN, exact_b=True)[:, :, 0:1]
            dgc = jnp.sum(e, axis=2, keepdims=True) - colsum
            dkb = _bdot(d_a, k) + dkbeg * eg
            dk = _bdot(d_a, kb, BTN) + _bdot(d_qk, q, BTN)
            dq = _bdot(d_qk, k) + d_qg * eg
            dgc = dgc + jnp.sum(d_qg * qg, axis=2, keepdims=True) + jnp.sum(dkbeg * kbeg, axis=2, keepdims=True)
            tdec = jnp.sum(d_kdec * kdec, axis=2, keepdims=True)
            dk = dk + d_kdec * cm["edec"] + dkb * beta
            dgc = dgc - tdec
            dgl = dgl + jnp.sum(tdec, axis=1, keepdims=True)
            dbeta = jnp.sum(dvb * v, axis=2, keepdims=True) + jnp.sum(dkb * k, axis=2, keepdims=True)
            dv = dvb * beta
            lane = lax.broadcasted_iota(jnp.int32, (CHUNK, LANE), 1)
            for rev, dy_ref, dg_ref, rows in ((False, dyf_ref, dgf_ref, rf), (True, dyr_ref, dgr_ref, rr)):
                dgc_tile = jnp.zeros((CHUNK, LANE), F32)
                rest = jnp.zeros((CHUNK, LANE), F32)
                for h in range(GDN_HEADS):
                    b = (GDN_HEADS if rev else 0) + h
                    gi, bi = _gate_lanes(rev, h)
                    dgc_tile = dgc_tile + jnp.where(lane == gi, dgc[b], 0.0)
                    rest = rest + jnp.where(lane == gi, dgl[b], 0.0) + jnp.where(lane == bi, dbeta[b], 0.0)
                    dy_ref[h, rows, :] = dq[b]
                    dy_ref[GDN_HEADS + h, rows, :] = dk[b]
                    dy_ref[2 * GDN_HEADS + h, rows, :] = dv[b]
                le_t = _chunk_masks(not rev)[0].astype(F32)
                dg_ref[rows, :] = _dot3(le_t, dgc_tile, NN, exact_a=True) + rest
            return ds_in

        ds = dstate[...]
        for c in range(DELTA_CHUNKS):
            ds = one_chunk(c, ds)
        dstate[...] = ds

    rows_per_step = DELTA_CHUNKS * CHUNK
    ns = nc // DELTA_CHUNKS

    def col(j, rev, blocks=GDN_HEADS):
        return pl.BlockSpec((blocks, rows_per_step, hd), (lambda n: (j, n, 0)) if rev else (lambda n: (j, ns - 1 - n, 0)))

    def wide(width, rev):
        return pl.BlockSpec((rows_per_step, width), (lambda n: (n, 0)) if rev else (lambda n: (ns - 1 - n, 0)))

    def per_chunk(d1, d2, rev):
        return pl.BlockSpec((DELTA_CHUNKS, GDN_HEADS, d1, d2),
                            (lambda n: (n, 0, 0, 0)) if rev else (lambda n: (ns - 1 - n, 0, 0, 0)))

    def side(rev):
        return [col(0, rev), col(1, rev), col(2, rev), wide(LANE, rev), wide(GDN_WIDTH, rev), per_chunk(hd, hd, rev),
                per_chunk(CHUNK, CHUNK, rev)]

    assert nc % DELTA_CHUNKS == 0 and (na == 0 or ns >= 2)
    res = pl.pallas_call(
        body, name="delta_bwd", grid=(ns,),
        in_specs=side(False) + side(True) + [ANY] * na,
        out_specs=[col(0, False, 3 * GDN_HEADS), col(0, True, 3 * GDN_HEADS), wide(LANE, False), wide(LANE, True)]
        + [ANY] * na,
        out_shape=[jax.ShapeDtypeStruct((3 * GDN_HEADS, s, hd), F32)] * 2 + [jax.ShapeDtypeStruct((s, LANE), F32)] * 2
        + [jax.ShapeDtypeStruct(g.shape, g.dtype) for g in scatter],
        scratch_shapes=[pltpu.VMEM((NB, hd, hd), F32)] + (_gather_semaphores(na) if na else []),
        compiler_params=_params("arbitrary"),
    )(y, y, y, gb, do, sf_all, tf_all, y, y, y, gb, do, sr_all, tr_all, *scatter)
    return res[:4], res[4:]


def _gdn_post_fwd(o_f, o_r, p_pad, norm_row):
    s = o_f.shape[0]
    tm = min(512, s)
    hd = GDN_HEAD_DIM

    def body(of_ref, or_ref, z_ref, w_ref, out_ref, osum_ref):
        o = of_ref[...] + or_ref[...]
        osum_ref[...] = o
        z = z_ref[...]
        gate = z * _sigmoid(z)
        for h in range(GDN_HEADS):
            sl = slice(h * hd, (h + 1) * hd)
            oh = o[:, sl]
            r = lax.rsqrt(jnp.mean(oh * oh, axis=-1, keepdims=True) + EPS)
            out_ref[:, sl] = (oh * r * w_ref[...] * gate[:, sl]).astype(BF16)

    blk = pl.BlockSpec((tm, GDN_WIDTH), lambda i: (i, 0))
    return pl.pallas_call(
        body, name="gdn_post_fwd", grid=(s // tm,),
        in_specs=[blk, blk, pl.BlockSpec((tm, GDN_WIDTH), lambda i: (i, OFF_Z // GDN_WIDTH)),
                  pl.BlockSpec((1, hd), lambda i: (0, 0))],
        out_specs=[blk, blk],
        out_shape=[jax.ShapeDtypeStruct((s, GDN_WIDTH), BF16), jax.ShapeDtypeStruct((s, GDN_WIDTH), F32)],
        compiler_params=_params("parallel"),
    )(o_f, o_r, p_pad, norm_row)


def _gdn_post_bwd(d_out, o_sum, p_pad, norm_row):
    s = o_sum.shape[0]
    tm = min(512, s)
    hd = GDN_HEAD_DIM

    def body(d_ref, o_ref, z_ref, w_ref, do_ref, dz_ref, dw_ref):
        @pl.when(pl.program_id(0) == 0)
        def _():
            dw_ref[...] = jnp.zeros_like(dw_ref)

        z = z_ref[...]
        sg = _sigmoid(z)
        gate = z * sg
        dgate = sg * (1.0 + z * (1.0 - sg))
        wv = w_ref[...]
        dw = jnp.zeros((1, hd), F32)
        for h in range(GDN_HEADS):
            sl = slice(h * hd, (h + 1) * hd)
            oh = o_ref[:, sl]
            dh = d_ref[:, sl]
            r = lax.rsqrt(jnp.mean(oh * oh, axis=-1, keepdims=True) + EPS)
            ohat = oh * r
            dz_ref[:, sl] = (dh * ohat * wv * dgate[:, sl]).astype(BF16)
            drn = dh * gate[:, sl]
            t = drn * wv
            do_ref[:, sl] = r * (t - ohat * jnp.mean(t * ohat, axis=-1, keepdims=True))
            dw = dw + jnp.sum(drn * ohat, axis=0, keepdims=True)
        dw_ref[...] += dw

    blk = pl.BlockSpec((tm, GDN_WIDTH), lambda i: (i, 0))
    vec = pl.BlockSpec((1, hd), lambda i: (0, 0))
    return pl.pallas_call(
        body, name="gdn_post_bwd", grid=(s // tm,),
        in_specs=[blk, blk, pl.BlockSpec((tm, GDN_WIDTH), lambda i: (i, OFF_Z // GDN_WIDTH)), vec],
        out_specs=[blk, pl.BlockSpec((tm, GDN_WIDTH), lambda i: (i, OFF_Z // GDN_WIDTH)), vec],
        out_shape=[jax.ShapeDtypeStruct((s, GDN_WIDTH), F32), jax.ShapeDtypeStruct((s, N_PAD), BF16),
                   jax.ShapeDtypeStruct((1, hd), F32)],
        compiler_params=_params("arbitrary"),
    )(d_out, o_sum, p_pad, norm_row)


def _gdn_forward(p_pad, conv_wt, alog_row, dt_row, norm_row, gather=()):
    c_pre, y = _gdn_conv_fwd(p_pad, conv_wt)
    gb = _gdn_gates_fwd(p_pad, alog_row, dt_row)
    (o_f, o_r, s_f, s_r, t_f, t_r), gathered = _delta_fwd2(y, gb, gather)
    out, o_sum = _gdn_post_fwd(o_f, o_r, p_pad, norm_row)
    return out, (c_pre, y, gb, s_f, t_f, s_r, t_r, o_sum), gathered


def _gdn_backward(d_out, p_pad, conv_wt, alog_row, dt_row, norm_row, saved, scatter=()):
    c_pre, y, gb, s_f, t_f, s_r, t_r, o_sum = saved
    do, dp_all, dnorm = _gdn_post_bwd(d_out, o_sum, p_pad, norm_row)
    (dy_f, dy_r, dgb_f, dgb_r), received = _delta_bwd2(y, gb, do, s_f, s_r, t_f, t_r, scatter)
    dp_all, dconv = _gdn_conv_bwd(dy_f, dy_r, c_pre, p_pad, conv_wt, dp_all)
    dp_all, gate_sums = _gdn_gates_bwd(dgb_f, dgb_r, p_pad, gb, alog_row, dt_row, dp_all)
    return dp_all, dconv, gate_sums, dnorm, received


ATT_BK = ATT_BQ + 2 * ATT_HALO
SWA_SCALE = SWA_HEAD_DIM ** -0.5


def _t5_bucket(rel):
    nb = REL_BUCKETS // 2
    bucket = (rel > 0).astype(np.int32) * nb
    n = np.abs(rel)
    max_exact = nb // 2
    large = max_exact + (np.log(np.maximum(n, 1) / max_exact)
                         / math.log(REL_MAX_DISTANCE / max_exact) * (nb - max_exact)).astype(np.int32)
    large = np.minimum(large, nb - 1)
    return (bucket + np.where(n < max_exact, n, large)).astype(np.int32)


def _band_tables(dilation, queries_are_rows_of_block):
    blk = np.arange(ATT_BQ)
    band = np.arange(ATT_BK) - ATT_HALO
    if queries_are_rows_of_block:
        rel = band[None, :] - blk[:, None]
        band_idx = np.broadcast_to(np.arange(ATT_BK)[None, :], rel.shape)
    else:
        rel = blk[None, :] - band[:, None]
        band_idx = np.broadcast_to(np.arange(ATT_BK)[:, None], rel.shape)
    base = np.abs(rel) <= ATT_HALO
    not_prev = band_idx >= ATT_HALO
    not_next = band_idx < ATT_HALO + ATT_BQ
    valid = np.stack([base & not_prev, base, base & not_next, base & not_prev & not_next])
    return valid, _t5_bucket(rel * dilation)


def _bias_tiles(rel_bias, dilation, queries_are_rows_of_block):
    valid, bucket = _band_tables(dilation, queries_are_rows_of_block)
    onehot = (jnp.asarray(bucket.reshape(-1, 1)) == jnp.arange(REL_BUCKETS, dtype=jnp.int32)[None, :]).astype(F32)
    rb = jnp.dot(onehot, rel_bias.astype(F32), precision=lax.Precision.HIGHEST)
    rb = rb.T.reshape((SWA_HEADS,) + bucket.shape)
    return jnp.where(valid[:, None], rb[None], NEG_BIG).astype(F32)


def _group_sum(x, bd):
    hi = x.astype(BF16)
    lo = (x - hi.astype(F32)).astype(BF16)
    return jnp.dot(hi, bd, preferred_element_type=F32) + jnp.dot(lo, bd, preferred_element_type=F32)


def _head_block_diag():
    idx = np.arange(SWA_WIDTH) // SWA_HEAD_DIM
    return jnp.asarray(idx[:, None] == idx[None, :], BF16)


def _swa_pre_fwd(p_pad, qw_row, kw_row, bd):
    s = p_pad.shape[0]
    tm = min(512, s)
    inv = 1.0 / SWA_HEAD_DIM

    def body(q_ref, k_ref, v_ref, qw_ref, kw_ref, bd_ref, qo_ref, ko_ref, vo_ref):
        bdv = bd_ref[...]
        q = q_ref[...]
        k = k_ref[...]
        rq = lax.rsqrt(_group_sum(q * q, bdv) * inv + EPS)
        rk = lax.rsqrt(_group_sum(k * k, bdv) * inv + EPS)
        qo_ref[...] = (q * rq * qw_ref[...] * SWA_SCALE).astype(BF16)
        ko_ref[...] = (k * rk * kw_ref[...]).astype(BF16)
        vo_ref[...] = v_ref[...].astype(BF16)

    base = OFF_B // SWA_WIDTH
    blk = pl.BlockSpec((tm, SWA_WIDTH), lambda i: (i, 0))
    vec = pl.BlockSpec((1, SWA_WIDTH), lambda i: (0, 0))
    return pl.pallas_call(
        body, name="swa_pre_fwd", grid=(s // tm,),
        in_specs=[pl.BlockSpec((tm, SWA_WIDTH), lambda i: (i, base)), pl.BlockSpec((tm, SWA_WIDTH), lambda i: (i, base + 1)),
                  pl.BlockSpec((tm, SWA_WIDTH), lambda i: (i, base + 2)), vec, vec,
                  pl.BlockSpec((SWA_WIDTH, SWA_WIDTH), lambda i: (0, 0))],
        out_specs=[blk, blk, blk],
        out_shape=[jax.ShapeDtypeStruct((s, SWA_WIDTH), BF16)] * 3,
        compiler_params=_params("parallel"),
    )(p_pad, p_pad, p_pad, qw_row, kw_row, bd)


def _swa_pre_bwd(dqs, dks, dvs, p_pad, qw_row, kw_row, bd, dp_all):
    s = p_pad.shape[0]
    tm = min(256, s)
    inv = 1.0 / SWA_HEAD_DIM
    npat = len(dqs)

    def body(*refs):
        dq_refs, dk_refs, dv_refs = refs[:npat], refs[npat:2 * npat], refs[2 * npat:3 * npat]
        q_ref, k_ref, qw_ref, kw_ref, bd_ref, _, dp_ref, dqw_ref, dkw_ref = refs[3 * npat:]

        @pl.when(pl.program_id(0) == 0)
        def _():
            dqw_ref[...] = jnp.zeros_like(dqw_ref)
            dkw_ref[...] = jnp.zeros_like(dkw_ref)

        bdv = bd_ref[...]

        def norm_bwd(x, g, w, scale):
            r = lax.rsqrt(_group_sum(x * x, bdv) * inv + EPS)
            xhat = x * r
            t = g * w * scale
            dx = r * (t - xhat * (_group_sum(t * xhat, bdv) * inv))
            return dx, jnp.sum(g * scale * xhat, axis=0, keepdims=True)

        def total(rs):
            t = rs[0][...].astype(F32)
            for r in rs[1:]:
                t = t + r[...].astype(F32)
            return t

        dq, dqw = norm_bwd(q_ref[...], total(dq_refs), qw_ref[...], SWA_SCALE)
        dk, dkw = norm_bwd(k_ref[...], total(dk_refs), kw_ref[...], 1.0)
        dp_ref[:, 0:SWA_WIDTH] = dq.astype(BF16)
        dp_ref[:, SWA_WIDTH:2 * SWA_WIDTH] = dk.astype(BF16)
        dp_ref[:, 2 * SWA_WIDTH:3 * SWA_WIDTH] = total(dv_refs).astype(BF16)
        dqw_ref[...] += dqw
        dkw_ref[...] += dkw

    base = OFF_B // SWA_WIDTH
    blk = pl.BlockSpec((tm, SWA_WIDTH), lambda i: (i, 0))
    vec = pl.BlockSpec((1, SWA_WIDTH), lambda i: (0, 0))
    return pl.pallas_call(
        body, name="swa_pre_bwd", grid=(s // tm,),
        in_specs=[blk] * (3 * npat) + [pl.BlockSpec((tm, SWA_WIDTH), lambda i: (i, base)),
                                      pl.BlockSpec((tm, SWA_WIDTH), lambda i: (i, base + 1)), vec, vec,
                                      pl.BlockSpec((SWA_WIDTH, SWA_WIDTH), lambda i: (0, 0)), ANY],
        out_specs=[pl.BlockSpec((tm, 3 * SWA_WIDTH), lambda i: (i, OFF_B // (3 * SWA_WIDTH))), vec, vec],
        out_shape=[jax.ShapeDtypeStruct(dp_all.shape, dp_all.dtype), jax.ShapeDtypeStruct((1, SWA_WIDTH), F32),
                   jax.ShapeDtypeStruct((1, SWA_WIDTH), F32)],
        input_output_aliases={3 * npat + 5: 0},
        compiler_params=_params("arbitrary"),
    )(*dqs, *dks, *dvs, p_pad, p_pad, qw_row, kw_row, bd, dp_all)


def _band_specs(length):
    per = ATT_BQ // ATT_HALO
    last = length // ATT_HALO - 1
    prev = pl.BlockSpec((ATT_HALO, SWA_WIDTH), lambda r, t: (jnp.maximum(t * per - 1, 0), r))
    cur = pl.BlockSpec((ATT_BQ, SWA_WIDTH), lambda r, t: (t, r))
    nxt = pl.BlockSpec((ATT_HALO, SWA_WIDTH), lambda r, t: (jnp.minimum((t + 1) * per, last), r))
    return [prev, cur, nxt]


def _tile_variant(t, nb):
    if nb == 1:
        return 3
    return jnp.where(t == 0, 0, jnp.where(t == nb - 1, 2, 1))


def _band(refs):
    return jnp.concatenate([r[...] for r in refs], axis=0)


N_PAIRS = SWA_HEADS // 2


def _pairs(x):
    return jnp.stack([x[:, LANE * p:LANE * (p + 1)] for p in range(N_PAIRS)])


def _per_head_rows(x):
    first = lax.broadcasted_iota(jnp.int32, x.shape, 2) < SWA_HEAD_DIM
    zero = jnp.zeros_like(x)
    return jnp.concatenate([jnp.where(first, x, zero), jnp.where(first, zero, x)], axis=1)


def _per_head_cols(x):
    return jnp.stack([jnp.concatenate([x[:, LANE * p:LANE * p + 1],
                                       x[:, LANE * p + SWA_HEAD_DIM:LANE * p + SWA_HEAD_DIM + 1]], axis=0)
                      for p in range(N_PAIRS)])


def _merge_heads(x, rows):
    first = lax.broadcasted_iota(jnp.int32, (N_PAIRS, rows, LANE), 2) < SWA_HEAD_DIM
    return jnp.where(first, x[:, :rows], x[:, rows:])


def _store_pairs(ref, x):
    for p in range(N_PAIRS):
        ref[:, LANE * p:LANE * (p + 1)] = x[p].astype(ref.dtype)


def _att_fwd2(q, k, v, bias, dilation):
    s = q.shape[0]
    length = s // dilation
    nb = length // ATT_BQ
    view = (length, dilation * SWA_WIDTH)

    def body(q_ref, kp, kc, kn, vp, vc, vn, b_ref, o_ref, lse_ref):
        kb, vb = _pairs(_band((kp, kc, kn))), _pairs(_band((vp, vc, vn)))
        qm = _per_head_rows(_pairs(q_ref[...]))
        sc = _bdot(qm, kb, BNT) + b_ref[0].reshape(N_PAIRS, 2 * ATT_BQ, ATT_BK)
        m = jnp.max(sc, axis=-1, keepdims=True)
        p = jnp.exp(sc - m)
        den = jnp.sum(p, axis=-1, keepdims=True)
        o = _bdot(p, vb) / den
        _store_pairs(o_ref, _merge_heads(o, ATT_BQ))
        lse = jnp.broadcast_to(m + jnp.log(den), (N_PAIRS, 2 * ATT_BQ, LANE))
        _store_pairs(lse_ref, _merge_heads(lse, ATT_BQ))

    cur = pl.BlockSpec((ATT_BQ, SWA_WIDTH), lambda r, t: (t, r))
    bspec = pl.BlockSpec((1, SWA_HEADS, ATT_BQ, ATT_BK), lambda r, t: (_tile_variant(t, nb), 0, 0, 0))
    o, lse = pl.pallas_call(
        body, name=f"att_fwd_d{dilation}", grid=(dilation, nb),
        in_specs=[cur] + _band_specs(length) * 2 + [bspec],
        out_specs=[cur, cur],
        out_shape=[jax.ShapeDtypeStruct(view, BF16), jax.ShapeDtypeStruct(view, F32)],
        compiler_params=_params("parallel", "parallel"),
    )(q.reshape(view), *([k.reshape(view)] * 3), *([v.reshape(view)] * 3), bias)
    return o.reshape(s, SWA_WIDTH), lse.reshape(s, SWA_WIDTH)


def _att_dq2(q, k, v, dop, lse, cp, bias, dilation):
    s = q.shape[0]
    length = s // dilation
    nb = length // ATT_BQ
    view = (length, dilation * SWA_WIDTH)

    def body(q_ref, kp, kc, kn, vp, vc, vn, do_ref, lse_ref, cp_ref, b_ref, dq_ref, db_ref):
        @pl.when((pl.program_id(0) == 0) & (pl.program_id(1) == 0))
        def _():
            db_ref[...] = jnp.zeros_like(db_ref)

        var = _tile_variant(pl.program_id(1), nb)
        kb, vb = _pairs(_band((kp, kc, kn))), _pairs(_band((vp, vc, vn)))
        qm = _per_head_rows(_pairs(q_ref[...]))
        dom = _per_head_rows(_pairs(do_ref[...]))
        sc = _bdot(qm, kb, BNT) + b_ref[0].reshape(N_PAIRS, 2 * ATT_BQ, ATT_BK)
        p = jnp.exp(sc - _per_head_cols(lse_ref[...]))
        ds = p * (_bdot(dom, vb, BNT) + _per_head_cols(cp_ref[...]))
        _store_pairs(dq_ref, _merge_heads(_bdot(ds, kb), ATT_BQ))
        db_ref[var] += ds.reshape(SWA_HEADS, ATT_BQ, ATT_BK)

    cur = pl.BlockSpec((ATT_BQ, SWA_WIDTH), lambda r, t: (t, r))
    bspec = pl.BlockSpec((1, SWA_HEADS, ATT_BQ, ATT_BK), lambda r, t: (_tile_variant(t, nb), 0, 0, 0))
    dq, db = pl.pallas_call(
        body, name=f"att_dq_d{dilation}", grid=(dilation, nb),
        in_specs=[cur] + _band_specs(length) * 2 + [cur, cur, cur, bspec],
        out_specs=[cur, pl.BlockSpec((4, SWA_HEADS, ATT_BQ, ATT_BK), lambda r, t: (0, 0, 0, 0))],
        out_shape=[jax.ShapeDtypeStruct(view, BF16), jax.ShapeDtypeStruct((4, SWA_HEADS, ATT_BQ, ATT_BK), F32)],
        compiler_params=_params("arbitrary", "arbitrary"),
    )(q.reshape(view), *([k.reshape(view)] * 3), *([v.reshape(view)] * 3), dop.reshape(view), lse.reshape(view),
      cp.reshape(view), bias)
    return dq.reshape(s, SWA_WIDTH), db


def _att_dkv2(q, k, v, dop, lse, cp, bias_t, dilation):
    s = q.shape[0]
    length = s // dilation
    nb = length // ATT_BQ
    view = (length, dilation * SWA_WIDTH)

    def body(k_ref, v_ref, qp, qc, qn, dp_, dc_, dn_, lp, lc, ln, cp_, cc_, cn_, b_ref, dk_ref, dv_ref):
        qm = _per_head_rows(_pairs(_band((qp, qc, qn))))
        dom = _per_head_rows(_pairs(_band((dp_, dc_, dn_))))
        lsev = _per_head_cols(_band((lp, lc, ln)))
        cpv = _per_head_cols(_band((cp_, cc_, cn_)))
        kv, vv = _pairs(k_ref[...]), _pairs(v_ref[...])
        sc = _bdot(qm, kv, BNT) + b_ref[0].reshape(N_PAIRS, 2 * ATT_BK, ATT_BQ)
        p = jnp.exp(sc - lsev)
        _store_pairs(dv_ref, _bdot(p, dom, BTN))
        ds = p * (_bdot(dom, vv, BNT) + cpv)
        _store_pairs(dk_ref, _bdot(ds, qm, BTN))

    cur = pl.BlockSpec((ATT_BQ, SWA_WIDTH), lambda r, t: (t, r))
    bspec = pl.BlockSpec((1, SWA_HEADS, ATT_BK, ATT_BQ), lambda r, t: (_tile_variant(t, nb), 0, 0, 0))
    dk, dv = pl.pallas_call(
        body, name=f"att_dkv_d{dilation}", grid=(dilation, nb),
        in_specs=[cur, cur] + _band_specs(length) * 4 + [bspec],
        out_specs=[cur, cur],
        out_shape=[jax.ShapeDtypeStruct(view, BF16)] * 2,
        compiler_params=_params("parallel", "parallel"),
    )(k.reshape(view), v.reshape(view), *([q.reshape(view)] * 3), *([dop.reshape(view)] * 3),
      *([lse.reshape(view)] * 3), *([cp.reshape(view)] * 3), bias_t)
    return dk.reshape(s, SWA_WIDTH), dv.reshape(s, SWA_WIDTH)


def _pattern_weights(lses):
    m = lses[0]
    for l in lses[1:]:
        m = jnp.maximum(m, l)
    es = [jnp.exp(l - m) for l in lses]
    den = es[0]
    for e in es[1:]:
        den = den + e
    return [e / den for e in es]


def _combine_fwd(outs, lses):
    s = outs[0].shape[0]
    tm = min(512, s)
    npat = len(outs)

    def body(*refs):
        ws = _pattern_weights([r[...] for r in refs[npat:2 * npat]])
        o = ws[0] * refs[0][...]
        for p in range(1, npat):
            o = o + ws[p] * refs[p][...]
        refs[2 * npat][...] = o.astype(BF16)

    blk = pl.BlockSpec((tm, SWA_WIDTH), lambda i: (i, 0))
    return pl.pallas_call(
        body, name="swa_combine_fwd", grid=(s // tm,), in_specs=[blk] * (2 * npat), out_specs=blk,
        out_shape=jax.ShapeDtypeStruct((s, SWA_WIDTH), BF16), compiler_params=_params("parallel"),
    )(*outs, *lses)


def _combine_bwd(d_out, outs, lses, bd):
    s = d_out.shape[0]
    tm = min(512, s)
    npat = len(outs)

    def body(*refs):
        d_ref, bd_ref = refs[0], refs[1 + 2 * npat]
        o_refs, l_refs = refs[1:1 + npat], refs[1 + npat:1 + 2 * npat]
        out_refs = refs[2 + 2 * npat:]
        ws = _pattern_weights([r[...] for r in l_refs])
        dov = d_ref[...]
        o = ws[0] * o_refs[0][...]
        for p in range(1, npat):
            o = o + ws[p] * o_refs[p][...]
        rd = _group_sum(dov * o, bd_ref[...])
        for p in range(npat):
            out_refs[p][...] = (ws[p] * dov).astype(BF16)
            out_refs[npat + p][...] = -ws[p] * rd

    blk = pl.BlockSpec((tm, SWA_WIDTH), lambda i: (i, 0))
    res = pl.pallas_call(
        body, name="swa_combine_bwd", grid=(s // tm,),
        in_specs=[blk] * (1 + 2 * npat) + [pl.BlockSpec((SWA_WIDTH, SWA_WIDTH), lambda i: (0, 0))],
        out_specs=[blk] * (2 * npat),
        out_shape=[jax.ShapeDtypeStruct((s, SWA_WIDTH), BF16)] * npat + [jax.ShapeDtypeStruct((s, SWA_WIDTH), F32)] * npat,
        compiler_params=_params("parallel"),
    )(d_out, *outs, *lses, bd)
    return res[:npat], res[npat:]


def _rel_bias_grad(dbs, buckets):
    npat = len(dbs)

    def body(*refs):
        db_refs, bk_refs, o_ref = refs[:npat], refs[npat:2 * npat], refs[2 * npat]
        row = lax.broadcasted_iota(jnp.int32, (REL_BUCKETS, LANE), 0)
        lane = lax.broadcasted_iota(jnp.int32, (REL_BUCKETS, LANE), 1)
        tiles = [[db_refs[p][0, h] + db_refs[p][1, h] + db_refs[p][2, h] + db_refs[p][3, h] for h in range(SWA_HEADS)]
                 for p in range(npat)]
        bks = [r[...] for r in bk_refs]

        def one_bucket(b, acc):
            for h in range(SWA_HEADS):
                tot = jnp.zeros((1, 1), F32)
                for p in range(npat):
                    sel = jnp.where(bks[p] == b, tiles[p][h], 0.0)
                    tot = tot + jnp.sum(jnp.sum(sel, axis=1, keepdims=True), axis=0, keepdims=True)
                acc = acc + jnp.where((row == b) & (lane == h), tot, 0.0)
            return acc

        o_ref[...] = lax.fori_loop(0, REL_BUCKETS, one_bucket, jnp.zeros((REL_BUCKETS, LANE), F32))

    full4 = pl.BlockSpec((4, SWA_HEADS, ATT_BQ, ATT_BK), lambda: (0, 0, 0, 0))
    full2 = pl.BlockSpec((ATT_BQ, ATT_BK), lambda: (0, 0))
    return pl.pallas_call(
        body, name="rel_bias_grad", in_specs=[full4] * npat + [full2] * npat,
        out_specs=pl.BlockSpec((REL_BUCKETS, LANE), lambda: (0, 0)),
        out_shape=jax.ShapeDtypeStruct((REL_BUCKETS, LANE), F32),
        compiler_params=pltpu.CompilerParams(vmem_limit_bytes=V7X_VMEM_LIMIT_BYTES),
    )(*dbs, *buckets)


def _swa_forward(p_pad, qw_row, kw_row, rel_bias, bd):
    q, k, v = _swa_pre_fwd(p_pad, qw_row, kw_row, bd)
    outs, lses = [], []
    for _, dil in DILATION_PATTERNS:
        o, lse = _att_fwd2(q, k, v, _bias_tiles(rel_bias, dil, True), dil)
        outs.append(o)
        lses.append(lse)
    return _combine_fwd(outs, lses), (q, k, v, outs, lses)


def _swa_backward(d_out, p_pad, qw_row, kw_row, rel_bias, bd, saved, dp_all):
    q, k, v, outs, lses = saved
    dops, cps = _combine_bwd(d_out, outs, lses, bd)
    dqs, dks, dvs, dbs, buckets = [], [], [], [], []
    for p, (_, dil) in enumerate(DILATION_PATTERNS):
        dq, db = _att_dq2(q, k, v, dops[p], lses[p], cps[p], _bias_tiles(rel_bias, dil, True), dil)
        dk, dv = _att_dkv2(q, k, v, dops[p], lses[p], cps[p], _bias_tiles(rel_bias, dil, False), dil)
        dqs.append(dq)
        dks.append(dk)
        dvs.append(dv)
        dbs.append(db)
        buckets.append(jnp.asarray(_band_tables(dil, True)[1]))
    dp, dqw, dkw = _swa_pre_bwd(dqs, dks, dvs, p_pad, qw_row, kw_row, bd, dp_all)
    return dp, dqw, dkw, _rel_bias_grad(dbs, buckets)


def _lane_row(v):
    flat = v.reshape(-1).astype(F32)
    return jnp.zeros((1, LANE), F32).at[0, :flat.shape[0]].set(flat)


W_IN_SHARD = N_IN // N_DEV
W_IN_RUNS = ((0, NAT_Z, 0), (NAT_Z, NAT_AB, OFF_Z), (NAT_AB, NAT_B, OFF_AB), (NAT_B, N_IN, OFF_B))


def _w_in_pieces(shard):
    lo, hi = shard * W_IN_SHARD, (shard + 1) * W_IN_SHARD
    out = []
    for first, last, dst in W_IN_RUNS:
        a, b = max(lo, first), min(hi, last)
        if a < b:
            out.append((a - lo, b - a, dst + a - first))
    return out


def _w_in_from_slabs(w3):
    nd, r, _ = w3.shape

    def body(w_ref, o_ref):
        o_ref[:, OFF_AB:N_PAD] = jnp.zeros((r, N_PAD - OFF_AB), w3.dtype)
        for sh in range(nd):
            for src, length, dst in _w_in_pieces(sh):
                o_ref[:, dst:dst + length] = w_ref[sh, :, src:src + length]

    return pl.pallas_call(
        body, name="w_in_from_slabs", out_shape=jax.ShapeDtypeStruct((r, N_PAD), w3.dtype),
        compiler_params=pltpu.CompilerParams(vmem_limit_bytes=V7X_VMEM_LIMIT_BYTES),
    )(w3)


def _w_in_grad_slabs(dw_pad, dtype):
    r = dw_pad.shape[0]

    def body(dw_ref, o_ref):
        for sh in range(N_DEV):
            for src, length, dst in _w_in_pieces(sh):
                o_ref[sh, :, src:src + length] = dw_ref[:, dst:dst + length].astype(dtype)

    return pl.pallas_call(
        body, name="w_in_grad_slabs", out_shape=jax.ShapeDtypeStruct((N_DEV, r, W_IN_SHARD), dtype),
        compiler_params=pltpu.CompilerParams(vmem_limit_bytes=V7X_VMEM_LIMIT_BYTES),
    )(dw_pad)


LATE = ("w_out", "ffn2_w_gate", "ffn2_w_up", "ffn2_w_down")
TRANSPOSED = ("ffn1_w_gate", "ffn1_w_up", "ffn2_w_gate", "ffn2_w_up")


def _late_weights(slabs):
    return {n: g.reshape(N_DEV * g.shape[1], g.shape[2]) for n, g in zip(LATE, slabs)}


def _local_step(x, tgt, wts, small, late_shards=None):
    bd = _head_block_diag()
    conv_wt = jnp.zeros((8, QKV_A), F32).at[:CONV_WIDTH].set(small["conv_w"].T)
    alog_row, dt_row = _lane_row(small["a_log"]), _lane_row(small["dt_bias"])
    gnorm_row = small["gdn_norm_w"].reshape(1, GDN_HEAD_DIM)
    qw_row = jnp.tile(small["q_norm_w"].reshape(-1), SWA_HEADS).reshape(1, SWA_WIDTH)
    kw_row = jnp.tile(small["k_norm_w"].reshape(-1), SWA_HEADS).reshape(1, SWA_WIDTH)
    rel_bias = small["rel_bias"]
    exchange = late_shards is not None
    dw_dtype = BF16 if exchange else F32

    x1, sv1, wd1, got = _ffn_forward(
        x, small["ffn1_norm"], wts["ffn1_w_gate"], wts["ffn1_w_up"], wts.get("ffn1_w_down"), "ffn1",
        gather=[late_shards["ffn1_w_down"], late_shards["w_in"]] if exchange else ())
    win_pad = _w_in_from_slabs(got[0]) if exchange else wts["w_in_pad"]
    n2, r2 = _rms_fwd(x1, small["mix_norm"], "mix_norm")
    p_pad = _matmul([(n2, win_pad)], tm=256, tn=N_PAD, tk=D_MODEL, name="w_in")
    o_a, sva, gathered = _gdn_forward(p_pad, conv_wt, alog_row, dt_row, gnorm_row,
                                      gather=[late_shards[n] for n in LATE] if exchange else ())
    if exchange:
        wts = {**wts, **_late_weights(gathered)}
    wo_a, wo_b = wts["w_out"][:GDN_WIDTH], wts["w_out"][GDN_WIDTH:]
    o_b, svb = _swa_forward(p_pad, qw_row, kw_row, rel_bias, bd)
    x2 = _matmul([(o_a, wo_a), (o_b, wo_b)], tm=512, tn=D_MODEL, tk=GDN_WIDTH, name="w_out", res=x1)
    x3, sv2, _, _ = _ffn_forward(x2, small["ffn2_norm"], wts["ffn2_w_gate"], wts["ffn2_w_up"], wts["ffn2_w_down"], "ffn2")
    loss_row, dx3, d_final = _final_loss(x3, small["final_norm"], tgt)

    dx2, d_ffn2_norm, dwg2, dwu2, dwd2, _ = _ffn_backward(
        dx3, x2, small["ffn2_norm"], wts["ffn2_w_gate"], wts["ffn2_w_up"], wts["ffn2_w_down"], sv2, "ffn2", dw_dtype)
    d_oa = _matmul([(dx2, wo_a)], tb=True, tm=512, tn=GDN_WIDTH, tk=D_MODEL, name="w_out_da")
    d_ob = _matmul([(dx2, wo_b)], tb=True, tm=512, tn=SWA_WIDTH, tk=D_MODEL, name="w_out_db")
    dwo_a = _matmul([(o_a, dx2)], ta=True, tm=GDN_WIDTH, tn=D_MODEL, tk=2048, name="w_out_dwa", out_dtype=dw_dtype)
    dwo_b = _matmul([(o_b, dx2)], ta=True, tm=SWA_WIDTH, tn=D_MODEL, tk=2048, name="w_out_dwb", out_dtype=dw_dtype)

    late_grads = [_row_slabs(jnp.concatenate([dwo_a, dwo_b], axis=0)), dwg2, dwu2, dwd2]
    dp_all, dconv, gate_sums, d_gnorm, received = _gdn_backward(
        d_oa, p_pad, conv_wt, alog_row, dt_row, gnorm_row, sva, scatter=late_grads if exchange else ())
    if exchange:
        late_grads = received
    dp_all, dqw, dkw, d_rel = _swa_backward(d_ob, p_pad, qw_row, kw_row, rel_bias, bd, svb, dp_all)
    dw_pad = _matmul([(n2, dp_all)], ta=True, tm=D_MODEL, tn=N_PAD // 3, tk=2048, name="w_in_dw")
    dn2 = _matmul([(dp_all, win_pad)], tb=True, tm=512, tn=D_MODEL, tk=N_PAD, name="w_in_dn")
    dx1, d_mix_norm = _rms_bwd(dn2, x1, r2, small["mix_norm"], dx2, "mix_dnorm")
    d_w_in = _w_in_grad_slabs(dw_pad, dw_dtype)
    dx, d_ffn1_norm, dwg1, dwu1, dwd1, got = _ffn_backward(
        dx1, x, small["ffn1_norm"], wts["ffn1_w_gate"], wts["ffn1_w_up"], wd1, sv1, "ffn1", dw_dtype,
        scatter=[d_w_in] if exchange else None)
    if exchange:
        d_w_in = got[0]

    grads = {
        "ffn1_norm": d_ffn1_norm, "ffn1_w_gate": dwg1, "ffn1_w_up": dwu1, "ffn1_w_down": dwd1,
        "mix_norm": d_mix_norm, "w_in": d_w_in, "conv_w": dconv[:CONV_WIDTH].T,
        "a_log": gate_sums[0, :8].reshape(2, GDN_HEADS), "dt_bias": gate_sums[1, :8].reshape(2, GDN_HEADS),
        "gdn_norm_w": d_gnorm, "q_norm_w": dqw.reshape(SWA_HEADS, SWA_HEAD_DIM).sum(0, keepdims=True),
        "k_norm_w": dkw.reshape(SWA_HEADS, SWA_HEAD_DIM).sum(0, keepdims=True), "rel_bias": d_rel[:, :SWA_HEADS],
        "ffn2_norm": d_ffn2_norm, "final_norm": d_final, **dict(zip(LATE, late_grads)),
    }
    return loss_row, dx, grads


MESH_IDS = pl.DeviceIdType.MESH
ANY = pl.BlockSpec(memory_space=pl.ANY)


def _adamw(parts, w, m, v, name):
    nparts, r, n = parts.shape
    tr = r
    for cand in (256, 176, 128, 104, 64, 8):
        if r % cand == 0:
            tr = cand
            break
    bc1 = 1.0 - ADAM_B1 ** ADAM_STEP
    bc2 = 1.0 - ADAM_B2 ** ADAM_STEP

    def body(p_ref, w_ref, m_ref, v_ref, g_ref, d_ref, nm_ref, nv_ref):
        g = p_ref[0].astype(F32)
        for k in range(1, nparts):
            g = g + p_ref[k].astype(F32)
        mn = ADAM_B1 * m_ref[...] + (1.0 - ADAM_B1) * g
        vn = ADAM_B2 * v_ref[...] + (1.0 - ADAM_B2) * (g * g)
        m_hat = mn / bc1
        v_hat = vn / bc2
        g_ref[...] = g
        nm_ref[...] = mn
        nv_ref[...] = vn
        d_ref[...] = -ADAM_LR * (m_hat / (jnp.sqrt(v_hat) + ADAM_EPS) + ADAM_WD * w_ref[...])

    blk = pl.BlockSpec((tr, n), lambda i: (i, 0))
    return pl.pallas_call(
        body, name=name, grid=(r // tr,),
        in_specs=[pl.BlockSpec((nparts, tr, n), lambda i: (0, i, 0)), blk, blk, blk],
        out_specs=[blk] * 4, out_shape=[jax.ShapeDtypeStruct((r, n), F32)] * 4,
        compiler_params=_params("parallel"),
    )(parts, w, m, v)


def _mesh_place():
    x, y, c = lax.axis_index("x"), lax.axis_index("y"), lax.axis_index("c")
    return x, y, c, [(1 - x, y), (x, 1 - y), (1 - x, 1 - y)]


def _gather_phases(x_refs, out_refs, send_sems, recv_sems, local_sems):
    na = len(x_refs)

    def place():
        x, y, c, chips = _mesh_place()
        return (x, y, c), (x, y, 1 - c), chips, c

    def slab(i, px, py, pc):
        return out_refs[i].at[4 * px + 2 * py + pc]

    def copy(i, k, block, to, src=None):
        return pltpu.make_async_remote_copy(
            src_ref=slab(i, *block) if src is None else src, dst_ref=slab(i, *block),
            send_sem=send_sems.at[i, k], recv_sem=recv_sems.at[i, k], device_id=to, device_id_type=MESH_IDS)

    def own(i, me):
        return pltpu.make_async_copy(x_refs[i], slab(i, *me), local_sems.at[i])

    def sends(i, me, sibling, chips, c):
        return [copy(i, 0, me, sibling, src=x_refs[i])] + [copy(i, 1 + j, me, (*chip, c), src=x_refs[i])
                                                          for j, chip in enumerate(chips)]

    def start():
        me, sibling, chips, c = place()
        for i in range(na):
            own(i, me).start()
            for cp in sends(i, me, sibling, chips, c):
                cp.start()

    def forward():
        me, sibling, chips, c = place()
        for j, chip in enumerate(chips):
            for i in range(na):
                copy(i, 1 + j, (*chip, c), me).wait_recv()
                copy(i, 4 + j, (*chip, c), sibling).start()

    def finish():
        me, sibling, chips, c = place()
        for i in range(na):
            copy(i, 0, sibling, me).wait_recv()
        for j, chip in enumerate(chips):
            for i in range(na):
                copy(i, 4 + j, (*chip, 1 - c), me).wait_recv()
        for i in range(na):
            for cp in sends(i, me, sibling, chips, c):
                cp.wait_send()
            for j, chip in enumerate(chips):
                copy(i, 4 + j, (*chip, c), sibling).wait_send()
            own(i, me).wait()

    return start, forward, finish


def _gather_semaphores(na):
    return [pltpu.SemaphoreType.DMA((na, 7)), pltpu.SemaphoreType.DMA((na, 7)), pltpu.SemaphoreType.DMA((na,))]


def _scatter_phases(g_refs, out_refs, send_sems, recv_sems, local_sems):
    na = len(g_refs)

    def place(m):
        x, y, c = lax.axis_index("x"), lax.axis_index("y"), lax.axis_index("c")
        px = 1 - x if m & 4 else x
        py = 1 - y if m & 2 else y
        pc = 1 - c if m & 1 else c
        return 4 * x + 2 * y + c, (px, py, pc), 4 * px + 2 * py + pc

    def own(i):
        me, _, _ = place(0)
        return pltpu.make_async_copy(g_refs[i].at[me], out_refs[i].at[me], local_sems.at[i])

    def start():
        for i in range(na):
            own(i).start()
            for m in range(1, N_DEV):
                me, peer, peer_idx = place(m)
                pltpu.make_async_remote_copy(
                    src_ref=g_refs[i].at[peer_idx], dst_ref=out_refs[i].at[me], send_sem=send_sems.at[i, m - 1],
                    recv_sem=recv_sems.at[i, m - 1], device_id=peer, device_id_type=MESH_IDS).start()

    def finish():
        for i in range(na):
            for m in range(1, N_DEV):
                me, peer, peer_idx = place(m)
                cp = pltpu.make_async_remote_copy(
                    src_ref=g_refs[i].at[peer_idx], dst_ref=out_refs[i].at[peer_idx], send_sem=send_sems.at[i, m - 1],
                    recv_sem=recv_sems.at[i, m - 1], device_id=peer, device_id_type=MESH_IDS)
                cp.wait_recv()
                cp.wait_send()
            own(i).wait()

    return start, finish


def _all_gather_many(vs, name):
    na = len(vs)

    def body(*refs):
        x_refs, out_refs = refs[:na], refs[na:2 * na]
        for step in _gather_phases(x_refs, out_refs, *refs[2 * na:]):
            step()

    return pl.pallas_call(
        body, name=name, in_specs=[ANY] * na, out_specs=[ANY] * na,
        out_shape=[jax.ShapeDtypeStruct((N_DEV,) + v.shape, v.dtype) for v in vs],
        scratch_shapes=_gather_semaphores(na),
        compiler_params=pltpu.CompilerParams(vmem_limit_bytes=V7X_VMEM_LIMIT_BYTES),
    )(*vs)


BIG = ("ffn1_w_gate", "ffn1_w_up", "ffn1_w_down", "w_in", "w_out", "ffn2_w_gate", "ffn2_w_up", "ffn2_w_down")
SMALL = ("ffn1_norm", "mix_norm", "a_log", "dt_bias", "gdn_norm_w", "q_norm_w", "k_norm_w", "rel_bias",
         "ffn2_norm", "final_norm")
WEIGHTS = ("ffn1_norm", "ffn1_w_gate", "ffn1_w_up", "ffn1_w_down", "mix_norm", "w_in", "conv_w", "a_log", "dt_bias",
           "gdn_norm_w", "q_norm_w", "k_norm_w", "rel_bias", "w_out", "ffn2_norm", "ffn2_w_gate", "ffn2_w_up",
           "ffn2_w_down", "final_norm")


def _pack(arrays, width, row_multiple):
    flat = jnp.concatenate([a.reshape(-1) for a in arrays])
    rows = -(-flat.shape[0] // width)
    rows = -(-rows // row_multiple) * row_multiple
    return jnp.pad(flat, (0, rows * width - flat.shape[0])).reshape(rows, width)


def _unpack(packed, shapes):
    flat = packed.reshape(-1)
    out, pos = [], 0
    for shp in shapes:
        size = int(np.prod(shp))
        out.append(flat[pos:pos + size].reshape(shp))
        pos += size
    return out


def kernel(x, ffn1_norm, ffn1_w_gate, ffn1_w_up, ffn1_w_down, mix_norm, w_in, conv_w, a_log, dt_bias, gdn_norm_w, q_norm_w, k_norm_w, rel_bias, w_out, ffn2_norm, ffn2_w_gate, ffn2_w_up, ffn2_w_down, final_norm, loss_target, m_ffn1_norm, m_ffn1_w_gate, m_ffn1_w_up, m_ffn1_w_down, m_mix_norm, m_w_in, m_conv_w, m_a_log, m_dt_bias, m_gdn_norm_w, m_q_norm_w, m_k_norm_w, m_rel_bias, m_w_out, m_ffn2_norm, m_ffn2_w_gate, m_ffn2_w_up, m_ffn2_w_down, m_final_norm, v_ffn1_norm, v_ffn1_w_gate, v_ffn1_w_up, v_ffn1_w_down, v_mix_norm, v_w_in, v_conv_w, v_a_log, v_dt_bias, v_gdn_norm_w, v_q_norm_w, v_k_norm_w, v_rel_bias, v_w_out, v_ffn2_norm, v_ffn2_w_gate, v_ffn2_w_up, v_ffn2_w_down, v_final_norm):
    w = dict(ffn1_norm=ffn1_norm, ffn1_w_gate=ffn1_w_gate, ffn1_w_up=ffn1_w_up, ffn1_w_down=ffn1_w_down, mix_norm=mix_norm, w_in=w_in, conv_w=conv_w, a_log=a_log, dt_bias=dt_bias, gdn_norm_w=gdn_norm_w, q_norm_w=q_norm_w, k_norm_w=k_norm_w, rel_bias=rel_bias, w_out=w_out, ffn2_norm=ffn2_norm, ffn2_w_gate=ffn2_w_gate, ffn2_w_up=ffn2_w_up, ffn2_w_down=ffn2_w_down, final_norm=final_norm)
    mom = dict(ffn1_norm=m_ffn1_norm, ffn1_w_gate=m_ffn1_w_gate, ffn1_w_up=m_ffn1_w_up, ffn1_w_down=m_ffn1_w_down, mix_norm=m_mix_norm, w_in=m_w_in, conv_w=m_conv_w, a_log=m_a_log, dt_bias=m_dt_bias, gdn_norm_w=m_gdn_norm_w, q_norm_w=m_q_norm_w, k_norm_w=m_k_norm_w, rel_bias=m_rel_bias, w_out=m_w_out, ffn2_norm=m_ffn2_norm, ffn2_w_gate=m_ffn2_w_gate, ffn2_w_up=m_ffn2_w_up, ffn2_w_down=m_ffn2_w_down, final_norm=m_final_norm)
    var = dict(ffn1_norm=v_ffn1_norm, ffn1_w_gate=v_ffn1_w_gate, ffn1_w_up=v_ffn1_w_up, ffn1_w_down=v_ffn1_w_down, mix_norm=v_mix_norm, w_in=v_w_in, conv_w=v_conv_w, a_log=v_a_log, dt_bias=v_dt_bias, gdn_norm_w=v_gdn_norm_w, q_norm_w=v_q_norm_w, k_norm_w=v_k_norm_w, rel_bias=v_rel_bias, w_out=v_w_out, ffn2_norm=v_ffn2_norm, ffn2_w_gate=v_ffn2_w_gate, ffn2_w_up=v_ffn2_w_up, ffn2_w_down=v_ffn2_w_down, final_norm=v_final_norm)
    ix, iy, ic = lax.axis_index("x"), lax.axis_index("y"), lax.axis_index("c")
    me = 4 * ix + 2 * iy + ic

    def local(a, n):
        return jnp.swapaxes(a[0], 0, 1) if n in TRANSPOSED else a[0]

    shard = {n: local(w[n], n) for n in BIG}

    conv_shard_shape = w["conv_w"][0].shape
    conv_elems = conv_shard_shape[0] * conv_shard_shape[1]
    first = ("ffn1_w_gate", "ffn1_w_up")
    gathered = _all_gather_many([shard[n].astype(BF16) for n in first] + [_pack([w["conv_w"][0]], LANE, 8)],
                                "gather_weights")
    wts = {n: g.reshape(N_DEV * g.shape[1], g.shape[2]) for n, g in zip(first, gathered)}

    small = {n: w[n][0] if n not in ("rel_bias",) else w[n] for n in SMALL}
    small = {n: (a.reshape(1, -1) if n.endswith("norm") else a) for n, a in small.items()}
    conv_all = gathered[-1].reshape(N_DEV, -1)
    small["conv_w"] = conv_all[:, :conv_elems].reshape(N_DEV * conv_shard_shape[0], conv_shard_shape[1])
    loss_row, grad_x, grads = _local_step(x[0], loss_target[0], wts, small,
                                          late_shards={n: shard[n].astype(BF16) for n in BIG if n not in first})

    big_out = [[], [], [], []]
    for n in BIG:
        for kind, val in enumerate(_adamw(grads[n], shard[n], local(mom[n], n), local(var[n], n), f"{n}_adamw")):
            big_out[kind].append(jnp.swapaxes(val, 0, 1) if n in TRANSPOSED else val)

    small_names = SMALL + ("conv_w",)
    small_shapes = [grads[n].shape for n in small_names] + [(1, 1)]
    g_small = _pack([grads[n] for n in small_names] + [loss_row[:, :1]], LANE, 8)
    all_small = _all_gather_many([g_small], "gather_small_grads")[0]
    riders = [jnp.zeros(shp, F32) for shp in small_shapes[len(SMALL):]]
    ws = _pack([w[n].reshape(grads[n].shape) for n in SMALL] + riders, LANE, 8)
    ms = _pack([mom[n].reshape(grads[n].shape) for n in SMALL] + riders, LANE, 8)
    vs = _pack([var[n].reshape(grads[n].shape) for n in SMALL] + riders, LANE, 8)
    small_out = [_unpack(a, small_shapes) for a in _adamw(all_small, ws, ms, vs, "adamw_small")]
    loss = small_out[0][-1][0, 0]
    conv_g = lax.dynamic_slice_in_dim(small_out[0][len(SMALL)], me * conv_shard_shape[0], conv_shard_shape[0], axis=0)
    conv_out = [_unpack(a, [conv_shard_shape])[0] for a in _adamw(
        _pack([conv_g], LANE, 8)[None], _pack([w["conv_w"][0]], LANE, 8), _pack([mom["conv_w"][0]], LANE, 8),
        _pack([var["conv_w"][0]], LANE, 8), "adamw_conv")]

    def leaf(kind, n):
        if n in BIG:
            val = big_out[kind][BIG.index(n)]
        elif n == "conv_w":
            val = conv_out[kind]
        else:
            val = small_out[kind][SMALL.index(n)]
        return val.reshape(w[n].shape)

    outs = [loss, grad_x[None]]
    for kind in range(4):
        outs += [leaf(kind, n) for n in WEIGHTS]
    return tuple(outs)
```

```python
import functools
import math

import numpy as np
import jax
import jax.numpy as jnp
from jax import lax
from jax.experimental import pallas as pl
from jax.experimental.pallas import tpu as pltpu

F32 = jnp.float32
BF16 = jnp.bfloat16

D_MODEL = 1024
D_FF = 2816
GDN_HEADS = 4
GDN_HEAD_DIM = 128
GDN_WIDTH = 512
CONV_WIDTH = 5
CHUNK = 64
SWA_HEADS = 8
SWA_HEAD_DIM = 64
SWA_WIDTH = 512
DILATION_PATTERNS = ((128, 1), (512, 4), (2048, 16))
REL_BUCKETS = 32
REL_MAX_DISTANCE = 1024
EPS = 1e-6
NEG_BIG = -1e30
N_DEV = 8

ADAM_LR = 0.001
ADAM_B1 = 0.9
ADAM_B2 = 0.999
ADAM_EPS = 1e-08
ADAM_WD = 0.01
ADAM_STEP = 10

QKV_A = 3 * GDN_WIDTH
OFF_B = QKV_A
OFF_Z = OFF_B + 3 * SWA_WIDTH
OFF_AB = OFF_Z + GDN_WIDTH
N_PAD = OFF_AB + 256
N_IN = 3600
NAT_Z, NAT_AB, NAT_B = QKV_A, QKV_A + GDN_WIDTH, QKV_A + GDN_WIDTH + 16

V7X_VMEM_LIMIT_BYTES = 56 * 1024 * 1024
LANE = 128
ATT_BQ = 128
ATT_HALO = 64
CONV_ROWS = 256

NN = (((1,), (0,)), ((), ()))
NT = (((1,), (1,)), ((), ()))
TN = (((0,), (0,)), ((), ()))


def _params(*sem):
    return pltpu.CompilerParams(dimension_semantics=sem, vmem_limit_bytes=V7X_VMEM_LIMIT_BYTES)


def _dot(a, b, dn=NN):
    return lax.dot_general(a.astype(BF16), b.astype(BF16), dn, preferred_element_type=F32)


def _sigmoid(x):
    return 1.0 / (1.0 + jnp.exp(-x))


class _Exchange:
    def __init__(self, kind, arrays):
        self.kind, self.arrays = kind, list(arrays)

    def out_shape(self):
        lead = (N_DEV,) if self.kind == "gather" else ()
        return [jax.ShapeDtypeStruct(lead + v.shape, v.dtype) for v in self.arrays]

    def hooks(self, in_refs, out_refs, sems, grid):
        step = pl.program_id(0)
        for axis in range(1, len(grid)):
            step = step * grid[axis] + pl.program_id(axis)
        total = math.prod(grid)
        if self.kind == "gather":
            assert total >= 4
            start, forward, finish = _gather_phases(in_refs, out_refs, *sems)
            pl.when(step == total // 2)(forward)
        else:
            assert total >= 2
            start, finish = _scatter_phases(in_refs, out_refs, *sems)
        pl.when(step == 0)(start)
        pl.when(step == total - 1)(finish)


def _pallas(body, *, name, grid, in_specs, out_specs, out_shape, args, semantics, scratch_shapes=(), exchange=None):
    n_in, n_out, n_scr = len(in_specs), len(out_specs), len(scratch_shapes)
    if exchange is None:
        res = pl.pallas_call(
            body, name=name, grid=grid, in_specs=list(in_specs), out_specs=list(out_specs), out_shape=list(out_shape),
            scratch_shapes=list(scratch_shapes), compiler_params=_params(*semantics))(*args)
        return list(res), []
    na = len(exchange.arrays)

    def carrying(*refs):
        ins, sent = refs[:n_in], refs[n_in:n_in + na]
        outs = refs[n_in + na:n_in + na + n_out]
        landed = refs[n_in + na + n_out:n_in + 2 * na + n_out]
        rest = refs[n_in + 2 * na + n_out:]
        exchange.hooks(sent, landed, rest[n_scr:], grid)
        body(*ins, *outs, *rest[:n_scr])

    res = pl.pallas_call(
        carrying, name=name, grid=grid, in_specs=list(in_specs) + [ANY] * na, out_specs=list(out_specs) + [ANY] * na,
        out_shape=list(out_shape) + exchange.out_shape(), scratch_shapes=list(scratch_shapes) + _gather_semaphores(na),
        compiler_params=_params(*(["arbitrary"] * len(grid))))(*args, *exchange.arrays)
    return list(res[:n_out]), list(res[n_out:])


def _matmul(pairs, *, ta=False, tb=False, out_dtype=F32, tm, tn, tk, name, res=None, alpha=None, shard_cols=None,
            exchange=None):
    a0, b0 = pairs[0]
    m = a0.shape[1] if ta else a0.shape[0]
    k = a0.shape[0] if ta else a0.shape[1]
    n = b0.shape[0] if tb else b0.shape[1]
    tm, tn, tk = min(tm, m), min(tn, n), min(tk, k)
    assert m % tm == 0 and n % tn == 0 and k % tk == 0, (name, m, n, k, tm, tn, tk)
    nk = k // tk
    npairs = len(pairs)
    dn = (((0 if ta else 1,), (1 if tb else 0,)), ((), ()))

    def body(*refs):
        ins = refs[:2 * npairs]
        pos = 2 * npairs
        r_ref = None
        if res is not None:
            r_ref = refs[pos]
            pos += 1
        o_ref, acc = refs[pos], refs[pos + 1]
        kk = pl.program_id(2)
        t = None
        for p in range(npairs):
            d = _dot(ins[2 * p][...], ins[2 * p + 1][...], dn)
            t = d if t is None else t + d

        if nk > 1:
            @pl.when(kk == 0)
            def _():
                acc[...] = t

            @pl.when((kk > 0) & (kk < nk - 1))
            def _():
                acc[...] += t

        @pl.when(kk == nk - 1)
        def _():
            r = acc[...] + t if nk > 1 else t
            if alpha is not None:
                r = r * alpha
            if r_ref is not None:
                r = r_ref[...] + r
            if shard_cols is None:
                o_ref[...] = r.astype(out_dtype)
            else:
                for sh in range(tn // shard_cols):
                    o_ref[sh] = r[:, sh * shard_cols:(sh + 1) * shard_cols].astype(out_dtype)

    a_spec = pl.BlockSpec((tk, tm), lambda i, j, kk: (kk, i)) if ta else pl.BlockSpec((tm, tk), lambda i, j, kk: (i, kk))
    b_spec = pl.BlockSpec((tn, tk), lambda i, j, kk: (j, kk)) if tb else pl.BlockSpec((tk, tn), lambda i, j, kk: (kk, j))
    o_spec = pl.BlockSpec((tm, tn), lambda i, j, kk: (i, j))
    in_specs = [a_spec, b_spec] * npairs + ([o_spec] if res is not None else [])
    args = [t for pr in pairs for t in pr] + ([res] if res is not None else [])
    out_spec, out_shape = o_spec, (m, n)
    if shard_cols is not None:
        assert res is None and tn % shard_cols == 0
        out_spec = pl.BlockSpec((tn // shard_cols, tm, shard_cols), lambda i, j, kk: (j, i, 0))
        out_shape = (n // shard_cols, m, shard_cols)
    (out,), exchanged = _pallas(
        body, name=name, grid=(m // tm, n // tn, nk), in_specs=in_specs, out_specs=[out_spec],
        out_shape=[jax.ShapeDtypeStruct(out_shape, out_dtype)],
        scratch_shapes=[pltpu.VMEM((tm, tn) if nk > 1 else (8, LANE), F32)],
        semantics=("parallel", "parallel", "arbitrary"), args=args, exchange=exchange)
    return out if exchange is None else (out, exchanged)


def _rms_fwd(x, w, name):
    s, d = x.shape
    tm = min(512, s)

    def body(x_ref, w_ref, n_ref, r_ref):
        xv = x_ref[...]
        r = lax.rsqrt(jnp.mean(xv * xv, axis=-1, keepdims=True) + EPS)
        n_ref[...] = (xv * r * w_ref[...]).astype(BF16)
        r_ref[...] = r

    return pl.pallas_call(
        body, name=name, grid=(s // tm,),
        in_specs=[pl.BlockSpec((tm, d), lambda i: (i, 0)), pl.BlockSpec((1, d), lambda i: (0, 0))],
        out_specs=[pl.BlockSpec((tm, d), lambda i: (i, 0)), pl.BlockSpec((tm, 1), lambda i: (i, 0))],
        out_shape=[jax.ShapeDtypeStruct((s, d), BF16), jax.ShapeDtypeStruct((s, 1), F32)],
        compiler_params=_params("parallel"),
    )(x, w)


def _rms_bwd(dn, x, r, w, dres, name, exchange=None):
    s, d = x.shape
    tm = min(512, s)

    def body(dn_ref, x_ref, r_ref, w_ref, dres_ref, dx_ref, dw_ref):
        @pl.when(pl.program_id(0) == 0)
        def _():
            dw_ref[...] = jnp.zeros_like(dw_ref)

        rv = r_ref[...]
        xhat = x_ref[...] * rv
        g = dn_ref[...]
        t = g * w_ref[...]
        dx_ref[...] = dres_ref[...] + rv * (t - xhat * jnp.mean(t * xhat, axis=-1, keepdims=True))
        dw_ref[...] += jnp.sum(g * xhat, axis=0, keepdims=True)

    row = pl.BlockSpec((tm, d), lambda i: (i, 0))
    vec = pl.BlockSpec((1, d), lambda i: (0, 0))
    (dx, dw), exchanged = _pallas(
        body, name=name, grid=(s // tm,),
        in_specs=[row, row, pl.BlockSpec((tm, 1), lambda i: (i, 0)), vec, row],
        out_specs=[row, vec],
        out_shape=[jax.ShapeDtypeStruct((s, d), F32), jax.ShapeDtypeStruct((1, d), F32)],
        semantics=("arbitrary",), args=(dn, x, r, w, dres), exchange=exchange)
    return (dx, dw) if exchange is None else (dx, dw, exchanged)


def _final_loss(x3, wf, tgt):
    s, d = x3.shape
    tm = min(512, s)

    def body(x_ref, w_ref, t_ref, loss_ref, dx_ref, dw_ref):
        @pl.when(pl.program_id(0) == 0)
        def _():
            dw_ref[...] = jnp.zeros_like(dw_ref)
            loss_ref[...] = jnp.zeros_like(loss_ref)

        xv = x_ref[...]
        wv = w_ref[...]
        r = lax.rsqrt(jnp.mean(xv * xv, axis=-1, keepdims=True) + EPS)
        xhat = xv * r
        e = xhat * wv - t_ref[...]
        part = 0.5 * jnp.sum(jnp.mean(e * e, axis=-1, keepdims=True), axis=0, keepdims=True)
        loss_ref[...] += jnp.broadcast_to(part, loss_ref.shape)
        dy = e * (1.0 / d)
        dw_ref[...] += jnp.sum(dy * xhat, axis=0, keepdims=True)
        t = dy * wv
        dx_ref[...] = r * (t - xhat * jnp.mean(t * xhat, axis=-1, keepdims=True))

    row = pl.BlockSpec((tm, d), lambda i: (i, 0))
    vec = pl.BlockSpec((1, d), lambda i: (0, 0))
    return pl.pallas_call(
        body, name="final_loss", grid=(s // tm,),
        in_specs=[row, vec, row],
        out_specs=[pl.BlockSpec((1, LANE), lambda i: (0, 0)), row, vec],
        out_shape=[jax.ShapeDtypeStruct((1, LANE), F32), jax.ShapeDtypeStruct((s, d), F32),
                   jax.ShapeDtypeStruct((1, d), F32)],
        compiler_params=_params("arbitrary"),
    )(x3, wf, tgt)


def _ffn_up(n, wg, wu, name, exchange=None):
    s, d = n.shape
    f = wg.shape[0]
    tm, tn = min(512, s), f // 2

    def body(n_ref, wg_ref, wu_ref, g_ref, u_ref, a_ref):
        nv = n_ref[...]
        g = _dot(nv, wg_ref[...], NT)
        u = _dot(nv, wu_ref[...], NT)
        g_ref[...] = g.astype(BF16)
        u_ref[...] = u.astype(BF16)
        a_ref[...] = (g * _sigmoid(g) * u).astype(BF16)

    o = pl.BlockSpec((tm, tn), lambda j, i: (i, j))
    wspec = pl.BlockSpec((tn, d), lambda j, i: (j, 0))
    return _pallas(
        body, name=name, grid=(f // tn, s // tm),
        in_specs=[pl.BlockSpec((tm, d), lambda j, i: (i, 0)), wspec, wspec],
        out_specs=[o, o, o],
        out_shape=[jax.ShapeDtypeStruct((s, f), BF16)] * 3,
        semantics=("parallel", "parallel"), args=(n, wg, wu), exchange=exchange)


def _ffn_dact(dx, wd, g, u, name, exchange=None):
    s, d = dx.shape
    f = wd.shape[0]
    tm, tn = min(512, s), f // 2

    def body(dx_ref, wd_ref, g_ref, u_ref, dg_ref, du_ref):
        da = 0.5 * _dot(dx_ref[...], wd_ref[...], NT)
        gv = g_ref[...].astype(F32)
        sg = _sigmoid(gv)
        du_ref[...] = (da * gv * sg).astype(BF16)
        dg_ref[...] = (da * u_ref[...].astype(F32) * (sg * (1.0 + gv * (1.0 - sg)))).astype(BF16)

    o = pl.BlockSpec((tm, tn), lambda j, i: (i, j))
    return _pallas(
        body, name=name, grid=(f // tn, s // tm),
        in_specs=[pl.BlockSpec((tm, d), lambda j, i: (i, 0)), pl.BlockSpec((tn, d), lambda j, i: (j, 0)), o, o],
        out_specs=[o, o],
        out_shape=[jax.ShapeDtypeStruct((s, f), BF16), jax.ShapeDtypeStruct((s, f), BF16)],
        semantics=("parallel", "parallel"), args=(dx, wd, g, u), exchange=exchange)


def _row_slabs(full):
    return full.reshape(N_DEV, full.shape[0] // N_DEV, full.shape[1])


def _ffn_forward(x, norm_w, wg, wu, wd, tag, gather=()):
    n, r = _rms_fwd(x, norm_w, f"{tag}_norm")
    (g, u, a), got = _ffn_up(n, wg, wu, f"{tag}_up", _Exchange("gather", gather) if gather else None)
    if wd is None:
        wd, got = got[0].reshape(N_DEV * got[0].shape[1], got[0].shape[2]), got[1:]
    y = _matmul([(a, wd)], tm=512, tn=1024, tk=wd.shape[0], name=f"{tag}_down", res=x, alpha=0.5)
    return y, (n, r, g, u, a), wd, got


def _ffn_backward(dy, x, norm_w, wgt, wut, wd, saved, tag, dw_dtype=F32, scatter=None):
    n, r, g, u, a = saved

    def behind(arrays):
        return _Exchange("scatter", arrays) if scatter is not None else None

    def dw(act, grad, name, alpha=None, exchange=None):
        return _matmul([(act, grad)], ta=True, tm=1408, tn=1024, tk=2048, name=name, alpha=alpha, out_dtype=dw_dtype,
                       exchange=exchange)

    dwd = _row_slabs(dw(a, dy, f"{tag}_dwd", alpha=0.5))
    (dg, du), extras = _ffn_dact(dy, wd, g, u, f"{tag}_dact", behind(scatter))
    if scatter is None:
        dwg, dwu = _row_slabs(dw(dg, n, f"{tag}_dwg")), _row_slabs(dw(du, n, f"{tag}_dwu"))
    else:
        dwg, (dwd,) = dw(dg, n, f"{tag}_dwg", exchange=behind([dwd]))
        dwu, (dwg,) = dw(du, n, f"{tag}_dwu", exchange=behind([_row_slabs(dwg)]))
        dwu = _row_slabs(dwu)
    dn = _matmul([(dg, wgt), (du, wut)], tm=512, tn=1024, tk=wgt.shape[0], name=f"{tag}_dn", exchange=behind([dwu]))
    if scatter is not None:
        dn, (dwu,) = dn
    dx, dnorm = _rms_bwd(dn, x, r, norm_w, dy, f"{tag}_dnorm")
    return dx, dnorm, dwg, dwu, dwd, extras


Q_SCALE = GDN_HEAD_DIM ** -0.5
CONV_HALO = 8


def _lane_block(s):
    return pl.BlockSpec((None, s, LANE), lambda j: (j, 0, 0))


def _conv_taps(win, w_ref, rows, sign):
    n = rows + 2 * CONV_HALO
    acc = None
    for t in range(CONV_WIDTH):
        o = sign * (t - CONV_WIDTH // 2)
        sh = win if o == 0 else pltpu.roll(win, (-o) % n, 0)
        term = sh[CONV_HALO:CONV_HALO + rows] * w_ref[t:t + 1, :]
        acc = term if acc is None else acc + term
    return acc


def _gdn_conv_fwd(p_pad, conv_wt):
    s = p_pad.shape[0]
    rows = min(CONV_ROWS, s)
    nblk = QKV_A // LANE

    def body(p_ref, w_ref, c_ref, y_ref, pad):
        j = pl.program_id(0)
        zeros = jnp.zeros((CONV_HALO, LANE), F32)
        pad[0:CONV_HALO, :] = zeros
        pad[CONV_HALO + s:2 * CONV_HALO + s, :] = zeros
        pad[CONV_HALO:CONV_HALO + s, :] = p_ref[...]

        def chunk(ci, carry):
            b = pl.multiple_of(ci * rows, rows)
            win = pad[pl.ds(b, rows + 2 * CONV_HALO), :]
            c = _conv_taps(win, w_ref, rows, 1)
            c_ref[pl.ds(b, rows), :] = c
            act = c * _sigmoid(c)
            nrm = lax.rsqrt(jnp.sum(act * act, axis=-1, keepdims=True) + EPS)
            mult = jnp.where(j < GDN_HEADS, nrm * Q_SCALE, jnp.where(j < 2 * GDN_HEADS, nrm, 1.0))
            y_ref[pl.ds(b, rows), :] = act * mult
            return carry

        lax.fori_loop(0, s // rows, chunk, 0)

    col = pl.BlockSpec((s, LANE), lambda j: (0, j))
    return pl.pallas_call(
        body, name="gdn_conv_fwd", grid=(nblk,),
        in_specs=[col, pl.BlockSpec((8, LANE), lambda j: (0, j))],
        out_specs=[_lane_block(s), _lane_block(s)],
        out_shape=[jax.ShapeDtypeStruct((nblk, s, LANE), F32), jax.ShapeDtypeStruct((nblk, s, LANE), F32)],
        scratch_shapes=[pltpu.VMEM((s + 2 * CONV_HALO, LANE), F32)],
        compiler_params=_params("parallel"),
    )(p_pad, conv_wt)


def _gdn_conv_bwd(dy_f, dy_r, c_pre, p_pad, conv_wt, dp_all):
    s = p_pad.shape[0]
    rows = min(CONV_ROWS, s)
    nblk = QKV_A // LANE

    def body(dyf_ref, dyr_ref, c_ref, p_ref, w_ref, _, dp_ref, dw_ref, ppad, dcpad):
        j = pl.program_id(0)
        zeros = jnp.zeros((CONV_HALO, LANE), F32)
        for buf in (ppad, dcpad):
            buf[0:CONV_HALO, :] = zeros
            buf[CONV_HALO + s:2 * CONV_HALO + s, :] = zeros
        ppad[CONV_HALO:CONV_HALO + s, :] = p_ref[...]

        def act_bwd(ci, carry):
            b = pl.multiple_of(ci * rows, rows)
            c = c_ref[pl.ds(b, rows), :]
            g = dyf_ref[pl.ds(b, rows), :] + dyr_ref[pl.ds(b, rows), :]
            sg = _sigmoid(c)
            act = c * sg
            nrm = lax.rsqrt(jnp.sum(act * act, axis=-1, keepdims=True) + EPS)
            yh = act * nrm
            scale = jnp.where(j < GDN_HEADS, Q_SCALE, 1.0)
            dact_qk = (scale * nrm) * (g - yh * jnp.sum(g * yh, axis=-1, keepdims=True))
            dact = jnp.where(j < 2 * GDN_HEADS, dact_qk, g)
            dcpad[pl.ds(pl.multiple_of(b + CONV_HALO, CONV_HALO), rows), :] = dact * (sg * (1.0 + c * (1.0 - sg)))
            return carry

        lax.fori_loop(0, s // rows, act_bwd, 0)
        tap = lax.broadcasted_iota(jnp.int32, (8, LANE), 0)

        def taps_bwd(ci, dw):
            b = pl.multiple_of(ci * rows, rows)
            dcw = dcpad[pl.ds(b, rows + 2 * CONV_HALO), :]
            dp_ref[pl.ds(b, rows), :] = _conv_taps(dcw, w_ref, rows, -1).astype(BF16)
            pw = ppad[pl.ds(b, rows + 2 * CONV_HALO), :]
            dc = dcw[CONV_HALO:CONV_HALO + rows]
            n = rows + 2 * CONV_HALO
            for t in range(CONV_WIDTH):
                o = t - CONV_WIDTH // 2
                sh = pw if o == 0 else pltpu.roll(pw, (-o) % n, 0)
                row = jnp.sum(dc * sh[CONV_HALO:CONV_HALO + rows], axis=0, keepdims=True)
                dw = dw + jnp.where(tap == t, row, 0.0)
            return dw

        dw_ref[...] = lax.fori_loop(0, s // rows, taps_bwd, jnp.zeros((8, LANE), F32))

    col = pl.BlockSpec((s, LANE), lambda j: (0, j))
    wspec = pl.BlockSpec((8, LANE), lambda j: (0, j))
    return pl.pallas_call(
        body, name="gdn_conv_bwd", grid=(nblk,),
        in_specs=[_lane_block(s), _lane_block(s), _lane_block(s), col, wspec, ANY],
        out_specs=[col, wspec],
        out_shape=[jax.ShapeDtypeStruct(dp_all.shape, dp_all.dtype), jax.ShapeDtypeStruct((8, QKV_A), F32)],
        scratch_shapes=[pltpu.VMEM((s + 2 * CONV_HALO, LANE), F32), pltpu.VMEM((s + 2 * CONV_HALO, LANE), F32)],
        input_output_aliases={5: 0},
        compiler_params=_params("parallel"),
    )(dy_f, dy_r, c_pre, p_pad, conv_wt, dp_all)


def _softplus(x):
    return jnp.maximum(x, 0.0) + jnp.log(1.0 + jnp.exp(-jnp.abs(x)))


def _gdn_gates_fwd(p_pad, alog_row, dt_row):
    s = p_pad.shape[0]
    tm = min(1024, s)

    def body(p_ref, al_ref, dt_ref, o_ref):
        x = p_ref[...]
        lane = lax.broadcasted_iota(jnp.int32, x.shape, 1)
        g = -jnp.exp(al_ref[...]) * _softplus(x + dt_ref[...])
        o_ref[...] = jnp.where(lane < 8, g, jnp.where(lane < 16, _sigmoid(x), 0.0))

    vec = pl.BlockSpec((1, LANE), lambda i: (0, 0))
    return pl.pallas_call(
        body, name="gdn_gates_fwd", grid=(s // tm,),
        in_specs=[pl.BlockSpec((tm, LANE), lambda i: (i, OFF_AB // LANE)), vec, vec],
        out_specs=pl.BlockSpec((tm, LANE), lambda i: (i, 0)),
        out_shape=jax.ShapeDtypeStruct((s, LANE), F32),
        compiler_params=_params("parallel"),
    )(p_pad, alog_row, dt_row)


def _gdn_gates_bwd(dgb_f, dgb_r, p_pad, gb, alog_row, dt_row, dp_all):
    s = p_pad.shape[0]
    tm = min(1024, s)
    tail = N_PAD - OFF_AB

    def body(df_ref, dr_ref, p_ref, gb_ref, al_ref, dt_ref, _, dp_ref, sum_ref):
        @pl.when(pl.program_id(0) == 0)
        def _():
            sum_ref[...] = jnp.zeros_like(sum_ref)

        x = p_ref[...]
        gbv = gb_ref[...]
        dgb = df_ref[...] + dr_ref[...]
        lane = lax.broadcasted_iota(jnp.int32, x.shape, 1)
        da = dgb * (-jnp.exp(al_ref[...])) * _sigmoid(x + dt_ref[...])
        db = dgb * gbv * (1.0 - gbv)
        dp_ref[:, 0:LANE] = jnp.where(lane < 8, da, jnp.where(lane < 16, db, 0.0)).astype(BF16)
        dp_ref[:, LANE:tail] = jnp.zeros((tm, tail - LANE), BF16)
        row = lax.broadcasted_iota(jnp.int32, (8, LANE), 0)
        lane8 = lax.broadcasted_iota(jnp.int32, (8, LANE), 1)
        d_alog = jnp.sum(dgb * gbv, axis=0, keepdims=True)
        d_dt = jnp.sum(da, axis=0, keepdims=True)
        upd = jnp.where(row == 0, d_alog, jnp.where(row == 1, d_dt, 0.0))
        sum_ref[...] += jnp.where(lane8 < 8, upd, 0.0)

    vec = pl.BlockSpec((1, LANE), lambda i: (0, 0))
    blk = pl.BlockSpec((tm, LANE), lambda i: (i, 0))
    return pl.pallas_call(
        body, name="gdn_gates_bwd", grid=(s // tm,),
        in_specs=[blk, blk, pl.BlockSpec((tm, LANE), lambda i: (i, OFF_AB // LANE)), blk, vec, vec, ANY],
        out_specs=[pl.BlockSpec((tm, tail), lambda i: (i, OFF_AB // tail)), pl.BlockSpec((8, LANE), lambda i: (0, 0))],
        out_shape=[jax.ShapeDtypeStruct(dp_all.shape, dp_all.dtype), jax.ShapeDtypeStruct((8, LANE), F32)],
        input_output_aliases={6: 0},
        compiler_params=_params("arbitrary"),
    )(dgb_f, dgb_r, p_pad, gb, alog_row, dt_row, dp_all)


def _chunk_masks(rev):
    row = lax.broadcasted_iota(jnp.int32, (CHUNK, CHUNK), 0)
    col = lax.broadcasted_iota(jnp.int32, (CHUNK, CHUNK), 1)
    le = (col >= row) if rev else (col <= row)
    strict = (col > row) if rev else (col < row)
    return le, strict, row == col


def _gate_lanes(rev, h):
    d = 1 if rev else 0
    return d * GDN_HEADS + h, 8 + d * GDN_HEADS + h


BNN = (((2,), (1,)), ((0,), (0,)))
BNT = (((2,), (2,)), ((0,), (0,)))
BTN = (((1,), (1,)), ((0,), (0,)))
NB = 2 * GDN_HEADS
DELTA_CHUNKS = 4


def _bdot(a, b, dn=BNN):
    return lax.dot_general(a.astype(BF16), b.astype(BF16), dn, preferred_element_type=F32)


def _dot3(a, b, dn, exact_a=False, exact_b=False):
    def d(x, y):
        return lax.dot_general(x, y, dn, preferred_element_type=F32)

    ah = a.astype(BF16)
    bh = b.astype(BF16)
    out = d(ah, bh)
    if not exact_b:
        out = out + d(ah, (b - bh.astype(F32)).astype(BF16))
    if not exact_a:
        out = out + d((a - ah.astype(F32)).astype(BF16), bh)
    return out


def _both(f_val, r_val):
    return jnp.stack([f_val] * GDN_HEADS + [r_val] * GDN_HEADS)


def _head_blocks(ref_f, ref_r, rows_f, rows_r):
    return jnp.concatenate([ref_f[:, rows_f, :], ref_r[:, rows_r, :]], axis=0)


def _chunk_rows(c):
    return slice(c * CHUNK, (c + 1) * CHUNK), slice((DELTA_CHUNKS - 1 - c) * CHUNK, (DELTA_CHUNKS - c) * CHUNK)


def _heads(ref_f, ref_r, rows_f, rows_r):
    hd = GDN_HEAD_DIM
    return jnp.stack([ref_f[rows_f, h * hd:(h + 1) * hd] for h in range(GDN_HEADS)]
                     + [ref_r[rows_r, h * hd:(h + 1) * hd] for h in range(GDN_HEADS)])


def _gate_cols(tile_f, tile_r, base):
    return jnp.stack([tile_f[:, base + h:base + h + 1] for h in range(GDN_HEADS)]
                     + [tile_r[:, base + GDN_HEADS + h:base + GDN_HEADS + h + 1] for h in range(GDN_HEADS)])


def _chunk_common2(q, k, v, gbf, gbr):
    mf, mr = _chunk_masks(False), _chunk_masks(True)
    le, strict = _both(mf[0], mr[0]), _both(mf[1], mr[1])
    eye = mf[2]
    gcm_f = _dot3(mf[0].astype(F32), gbf, NN, exact_a=True)
    gcm_r = _dot3(mr[0].astype(F32), gbr, NN, exact_a=True)
    g, beta, gc = _gate_cols(gbf, gbr, 0), _gate_cols(gbf, gbr, 8), _gate_cols(gcm_f, gcm_r, 0)
    gc_row = _dot3(jnp.ones((NB, CHUNK, CHUNK), F32), jnp.where(eye[None], gc, 0.0), BNN, exact_a=True)
    decay = jnp.where(le, jnp.exp(jnp.where(le, gc - gc_row, 0.0)), 0.0)
    eg = jnp.exp(gc)
    gl = jnp.sum(g, axis=1, keepdims=True)
    kb = k * beta
    vb = v * beta
    kbeg = kb * eg
    lm = jnp.where(strict, _bdot(kb, k, BNT) * decay, 0.0)
    intra = _bdot(q, k, BNT) * decay
    edec = jnp.exp(gl - gc)
    return dict(strict=strict, eye=eye, beta=beta, decay=decay, eg=eg, gl=gl, kb=kb, vb=vb, kbeg=kbeg,
                lm=lm, intra=intra, qg=q * eg, edec=edec, kdec=k * edec)


def _unit_triangular_inverse(lm, eye):
    x = -lm
    t = eye[None].astype(F32) + x
    p = x
    for level in range(5):
        prod = functools.partial(_dot3, dn=BNN) if level < 2 else _bdot
        p = prod(p, p)
        t = t + prod(t, p)
    return t


def _delta_fwd2(y, gb, gather=()):
    s = y.shape[1]
    nc = s // CHUNK
    hd = GDN_HEAD_DIM
    na = len(gather)

    def body(*refs):
        qf, kf, vf, gf, qr, kr, vr, gr = refs[:8]
        of_ref, or_ref, sf_all, sr_all, tf_all, tr_all = refs[8 + na:14 + na]
        state = refs[14 + 2 * na]
        step = pl.program_id(0)

        @pl.when(step == 0)
        def _():
            state[...] = jnp.zeros_like(state)

        if na:
            start, forward, finish = _gather_phases(refs[8:8 + na], refs[14 + na:14 + 2 * na], *refs[15 + 2 * na:])
            pl.when(step == 0)(start)
            pl.when(step == ns // 2)(forward)
            pl.when(step == ns - 1)(finish)

        st = state[...]
        for c in range(DELTA_CHUNKS):
            rf, rr = _chunk_rows(c)
            q, k, v = _head_blocks(qf, qr, rf, rr), _head_blocks(kf, kr, rf, rr), _head_blocks(vf, vr, rf, rr)
            cm = _chunk_common2(q, k, v, gf[rf, :], gr[rr, :])
            tinv = _unit_triangular_inverse(cm["lm"], cm["eye"])
            u = _bdot(tinv, cm["vb"])
            w = _bdot(tinv, cm["kbeg"])
            v_new = u - _bdot(w, st)
            o = _bdot(cm["qg"], st) + _bdot(cm["intra"], v_new)
            for h in range(GDN_HEADS):
                of_ref[rf, h * hd:(h + 1) * hd] = o[h]
                or_ref[rr, h * hd:(h + 1) * hd] = o[GDN_HEADS + h]
            sf_all[c] = st[:GDN_HEADS]
            sr_all[DELTA_CHUNKS - 1 - c] = st[GDN_HEADS:]
            tf_all[c] = tinv[:GDN_HEADS]
            tr_all[DELTA_CHUNKS - 1 - c] = tinv[GDN_HEADS:]
            st = st * jnp.exp(cm["gl"]) + _bdot(cm["kdec"], v_new, BTN)
        state[...] = st

    rows = DELTA_CHUNKS * CHUNK
    ns = nc // DELTA_CHUNKS

    def col(j, rev):
        return pl.BlockSpec((GDN_HEADS, rows, hd), (lambda n: (j, ns - 1 - n, 0)) if rev else (lambda n: (j, n, 0)))

    def out(rev):
        return pl.BlockSpec((rows, GDN_WIDTH), (lambda n: (ns - 1 - n, 0)) if rev else (lambda n: (n, 0)))

    def gate(rev):
        return pl.BlockSpec((rows, LANE), (lambda n: (ns - 1 - n, 0)) if rev else (lambda n: (n, 0)))

    def per_chunk(d1, d2, rev):
        return pl.BlockSpec((DELTA_CHUNKS, GDN_HEADS, d1, d2),
                            (lambda n: (ns - 1 - n, 0, 0, 0)) if rev else (lambda n: (n, 0, 0, 0)))

    assert nc % DELTA_CHUNKS == 0 and (na == 0 or ns >= 4)
    res = pl.pallas_call(
        body, name="delta_fwd", grid=(ns,),
        in_specs=[col(0, False), col(1, False), col(2, False), gate(False), col(0, True), col(1, True), col(2, True), gate(True)]
        + [ANY] * na,
        out_specs=[out(False), out(True), per_chunk(hd, hd, False), per_chunk(hd, hd, True),
                   per_chunk(CHUNK, CHUNK, False), per_chunk(CHUNK, CHUNK, True)] + [ANY] * na,
        out_shape=[jax.ShapeDtypeStruct((s, GDN_WIDTH), F32)] * 2 + [jax.ShapeDtypeStruct((nc, GDN_HEADS, hd, hd), F32)] * 2
        + [jax.ShapeDtypeStruct((nc, GDN_HEADS, CHUNK, CHUNK), F32)] * 2
        + [jax.ShapeDtypeStruct((N_DEV,) + v.shape, v.dtype) for v in gather],
        scratch_shapes=[pltpu.VMEM((NB, hd, hd), F32)] + (_gather_semaphores(na) if na else []),
        compiler_params=_params("arbitrary"),
    )(y, y, y, gb, y, y, y, gb, *gather)
    return res[:6], res[6:]


def _delta_bwd2(y, gb, do, sf_all, sr_all, tf_all, tr_all, scatter=()):
    s = y.shape[1]
    nc = s // CHUNK
    hd = GDN_HEAD_DIM
    na = len(scatter)

    def body(*refs):
        qf, kf, vf, gf, dof, sf, tf, qr, kr, vr, gr, dor, sr, tr = refs[:14]
        dyf_ref, dyr_ref, dgf_ref, dgr_ref = refs[14 + na:18 + na]
        dstate = refs[18 + 2 * na]
        step = pl.program_id(0)

        @pl.when(step == 0)
        def _():
            dstate[...] = jnp.zeros_like(dstate)

        if na:
            start, finish = _scatter_phases(refs[14:14 + na], refs[18 + na:18 + 2 * na], *refs[19 + 2 * na:])
            pl.when(step == 0)(start)
            pl.when(step == ns - 1)(finish)

        def one_chunk(c, ds_out):
            rr, rf = _chunk_rows(c)
            cf, cr = DELTA_CHUNKS - 1 - c, c
            q, k, v = _head_blocks(qf, qr, rf, rr), _head_blocks(kf, kr, rf, rr), _head_blocks(vf, vr, rf, rr)
            dov = _heads(dof, dor, rf, rr)
            cm = _chunk_common2(q, k, v, gf[rf, :], gr[rr, :])
            tinv = jnp.concatenate([tf[cf], tr[cr]], axis=0)
            st = jnp.concatenate([sf[cf], sr[cr]], axis=0)
            decay, lm, intra, qg, kdec, kbeg, eg, kb, beta = (
                cm[n] for n in ("decay", "lm", "intra", "qg", "kdec", "kbeg", "eg", "kb", "beta"))
            u = _bdot(tinv, cm["vb"])
            w = _bdot(tinv, kbeg)
            v_new = u - _bdot(w, st)
            egl = jnp.exp(cm["gl"])
            d_qg = _bdot(dov, st, BNT)
            d_intra = _bdot(dov, v_new, BNT)
            dv_new = _bdot(intra, dov, BTN) + _bdot(kdec, ds_out)
            d_kdec = _bdot(v_new, ds_out, BNT)
            ds_in = _bdot(qg, dov, BTN) + egl * ds_out - _bdot(w, dv_new, BTN)
            dgl = egl * jnp.sum(jnp.sum(st * ds_out, axis=2, keepdims=True), axis=1, keepdims=True)
            dw = -_bdot(dv_new, st, BNT)
            dvb = _bdot(tinv, dv_new, BTN)
            dkbeg = _bdot(tinv, dw, BTN)
            dlm = jnp.where(cm["strict"], -(_bdot(dvb, u, BNT) + _bdot(dkbeg, w, BNT)), 0.0)
            d_a = dlm * decay
            d_qk = d_intra * decay
            e = dlm * lm + d_intra * intra
            colsum = _dot3(e, jnp.ones((NB, CHUNK, LANE), F32), BTN, exact_b=True)[:, :, 0:1]
            dgc = jnp.sum(e, axis=2, keepdims=True) - colsum
            dkb = _bdot(d_a, k) + dkbeg * eg
            dk = _bdot(d_a, kb, BTN) + _bdot(d_qk, q, BTN)
            dq = _bdot(d_qk, k) + d_qg * eg
            dgc = dgc + jnp.sum(d_qg * qg, axis=2, keepdims=True) + jnp.sum(dkbeg * kbeg, axis=2, keepdims=True)
            tdec = jnp.sum(d_kdec * kdec, axis=2, keepdims=True)
            dk = dk + d_kdec * cm["edec"] + dkb * beta
            dgc = dgc - tdec
            dgl = dgl + jnp.sum(tdec, axis=1, keepdims=True)
            dbeta = jnp.sum(dvb * v, axis=2, keepdims=True) + jnp.sum(dkb * k, axis=2, keepdims=True)
            dv = dvb * beta
            lane = lax.broadcasted_iota(jnp.int32, (CHUNK, LANE), 1)
            for rev, dy_ref, dg_ref, rows in ((False, dyf_ref, dgf_ref, rf), (True, dyr_ref, dgr_ref, rr)):
                dgc_tile = jnp.zeros((CHUNK, LANE), F32)
                rest = jnp.zeros((CHUNK, LANE), F32)
                for h in range(GDN_HEADS):
                    b = (GDN_HEADS if rev else 0) + h
                    gi, bi = _gate_lanes(rev, h)
                    dgc_tile = dgc_tile + jnp.where(lane == gi, dgc[b], 0.0)
                    rest = rest + jnp.where(lane == gi, dgl[b], 0.0) + jnp.where(lane == bi, dbeta[b], 0.0)
                    dy_ref[h, rows, :] = dq[b]
                    dy_ref[GDN_HEADS + h, rows, :] = dk[b]
                    dy_ref[2 * GDN_HEADS + h, rows, :] = dv[b]
                le_t = _chunk_masks(not rev)[0].astype(F32)
                dg_ref[rows, :] = _dot3(le_t, dgc_tile, NN, exact_a=True) + rest
            return ds_in

        ds = dstate[...]
        for c in range(DELTA_CHUNKS):
            ds = one_chunk(c, ds)
        dstate[...] = ds

    rows_per_step = DELTA_CHUNKS * CHUNK
    ns = nc // DELTA_CHUNKS

    def col(j, rev, blocks=GDN_HEADS):
        return pl.BlockSpec((blocks, rows_per_step, hd), (lambda n: (j, n, 0)) if rev else (lambda n: (j, ns - 1 - n, 0)))

    def wide(width, rev):
        return pl.BlockSpec((rows_per_step, width), (lambda n: (n, 0)) if rev else (lambda n: (ns - 1 - n, 0)))

    def per_chunk(d1, d2, rev):
        return pl.BlockSpec((DELTA_CHUNKS, GDN_HEADS, d1, d2),
                            (lambda n: (n, 0, 0, 0)) if rev else (lambda n: (ns - 1 - n, 0, 0, 0)))

    def side(rev):
        return [col(0, rev), col(1, rev), col(2, rev), wide(LANE, rev), wide(GDN_WIDTH, rev), per_chunk(hd, hd, rev),
                per_chunk(CHUNK, CHUNK, rev)]

    assert nc % DELTA_CHUNKS == 0 and (na == 0 or ns >= 2)
    res = pl.pallas_call(
        body, name="delta_bwd", grid=(ns,),
        in_specs=side(False) + side(True) + [ANY] * na,
        out_specs=[col(0, False, 3 * GDN_HEADS), col(0, True, 3 * GDN_HEADS), wide(LANE, False), wide(LANE, True)]
        + [ANY] * na,
        out_shape=[jax.ShapeDtypeStruct((3 * GDN_HEADS, s, hd), F32)] * 2 + [jax.ShapeDtypeStruct((s, LANE), F32)] * 2
        + [jax.ShapeDtypeStruct(g.shape, g.dtype) for g in scatter],
        scratch_shapes=[pltpu.VMEM((NB, hd, hd), F32)] + (_gather_semaphores(na) if na else []),
        compiler_params=_params("arbitrary"),
    )(y, y, y, gb, do, sf_all, tf_all, y, y, y, gb, do, sr_all, tr_all, *scatter)
    return res[:4], res[4:]


def _gdn_post_fwd(o_f, o_r, p_pad, norm_row):
    s = o_f.shape[0]
    tm = min(512, s)
    hd = GDN_HEAD_DIM

    def body(of_ref, or_ref, z_ref, w_ref, out_ref, osum_ref):
        o = of_ref[...] + or_ref[...]
        osum_ref[...] = o
        z = z_ref[...]
        gate = z * _sigmoid(z)
        for h in range(GDN_HEADS):
            sl = slice(h * hd, (h + 1) * hd)
            oh = o[:, sl]
            r = lax.rsqrt(jnp.mean(oh * oh, axis=-1, keepdims=True) + EPS)
            out_ref[:, sl] = (oh * r * w_ref[...] * gate[:, sl]).astype(BF16)

    blk = pl.BlockSpec((tm, GDN_WIDTH), lambda i: (i, 0))
    return pl.pallas_call(
        body, name="gdn_post_fwd", grid=(s // tm,),
        in_specs=[blk, blk, pl.BlockSpec((tm, GDN_WIDTH), lambda i: (i, OFF_Z // GDN_WIDTH)),
                  pl.BlockSpec((1, hd), lambda i: (0, 0))],
        out_specs=[blk, blk],
        out_shape=[jax.ShapeDtypeStruct((s, GDN_WIDTH), BF16), jax.ShapeDtypeStruct((s, GDN_WIDTH), F32)],
        compiler_params=_params("parallel"),
    )(o_f, o_r, p_pad, norm_row)


def _gdn_post_bwd(d_out, o_sum, p_pad, norm_row):
    s = o_sum.shape[0]
    tm = min(512, s)
    hd = GDN_HEAD_DIM

    def body(d_ref, o_ref, z_ref, w_ref, do_ref, dz_ref, dw_ref):
        @pl.when(pl.program_id(0) == 0)
        def _():
            dw_ref[...] = jnp.zeros_like(dw_ref)

        z = z_ref[...]
        sg = _sigmoid(z)
        gate = z * sg
        dgate = sg * (1.0 + z * (1.0 - sg))
        wv = w_ref[...]
        dw = jnp.zeros((1, hd), F32)
        for h in range(GDN_HEADS):
            sl = slice(h * hd, (h + 1) * hd)
            oh = o_ref[:, sl]
            dh = d_ref[:, sl]
            r = lax.rsqrt(jnp.mean(oh * oh, axis=-1, keepdims=True) + EPS)
            ohat = oh * r
            dz_ref[:, sl] = (dh * ohat * wv * dgate[:, sl]).astype(BF16)
            drn = dh * gate[:, sl]
            t = drn * wv
            do_ref[:, sl] = r * (t - ohat * jnp.mean(t * ohat, axis=-1, keepdims=True))
            dw = dw + jnp.sum(drn * ohat, axis=0, keepdims=True)
        dw_ref[...] += dw

    blk = pl.BlockSpec((tm, GDN_WIDTH), lambda i: (i, 0))
    vec = pl.BlockSpec((1, hd), lambda i: (0, 0))
    return pl.pallas_call(
        body, name="gdn_post_bwd", grid=(s // tm,),
        in_specs=[blk, blk, pl.BlockSpec((tm, GDN_WIDTH), lambda i: (i, OFF_Z // GDN_WIDTH)), vec],
        out_specs=[blk, pl.BlockSpec((tm, GDN_WIDTH), lambda i: (i, OFF_Z // GDN_WIDTH)), vec],
        out_shape=[jax.ShapeDtypeStruct((s, GDN_WIDTH), F32), jax.ShapeDtypeStruct((s, N_PAD), BF16),
                   jax.ShapeDtypeStruct((1, hd), F32)],
        compiler_params=_params("arbitrary"),
    )(d_out, o_sum, p_pad, norm_row)


def _gdn_forward(p_pad, conv_wt, alog_row, dt_row, norm_row, gather=()):
    c_pre, y = _gdn_conv_fwd(p_pad, conv_wt)
    gb = _gdn_gates_fwd(p_pad, alog_row, dt_row)
    (o_f, o_r, s_f, s_r, t_f, t_r), gathered = _delta_fwd2(y, gb, gather)
    out, o_sum = _gdn_post_fwd(o_f, o_r, p_pad, norm_row)
    return out, (c_pre, y, gb, s_f, t_f, s_r, t_r, o_sum), gathered


def _gdn_backward(d_out, p_pad, conv_wt, alog_row, dt_row, norm_row, saved, scatter=()):
    c_pre, y, gb, s_f, t_f, s_r, t_r, o_sum = saved
    do, dp_all, dnorm = _gdn_post_bwd(d_out, o_sum, p_pad, norm_row)
    (dy_f, dy_r, dgb_f, dgb_r), received = _delta_bwd2(y, gb, do, s_f, s_r, t_f, t_r, scatter)
    dp_all, dconv = _gdn_conv_bwd(dy_f, dy_r, c_pre, p_pad, conv_wt, dp_all)
    dp_all, gate_sums = _gdn_gates_bwd(dgb_f, dgb_r, p_pad, gb, alog_row, dt_row, dp_all)
    return dp_all, dconv, gate_sums, dnorm, received


ATT_BK = ATT_BQ + 2 * ATT_HALO
ATT_SUB = 4
SWA_SCALE = SWA_HEAD_DIM ** -0.5


def _t5_bucket(rel):
    nb = REL_BUCKETS // 2
    bucket = (rel > 0).astype(np.int32) * nb
    n = np.abs(rel)
    max_exact = nb // 2
    large = max_exact + (np.log(np.maximum(n, 1) / max_exact)
                         / math.log(REL_MAX_DISTANCE / max_exact) * (nb - max_exact)).astype(np.int32)
    large = np.minimum(large, nb - 1)
    return (bucket + np.where(n < max_exact, n, large)).astype(np.int32)


def _band_tables(dilation, queries_are_rows_of_block):
    blk = np.arange(ATT_BQ)
    band = np.arange(ATT_BK) - ATT_HALO
    if queries_are_rows_of_block:
        rel = band[None, :] - blk[:, None]
        band_idx = np.broadcast_to(np.arange(ATT_BK)[None, :], rel.shape)
    else:
        rel = blk[None, :] - band[:, None]
        band_idx = np.broadcast_to(np.arange(ATT_BK)[:, None], rel.shape)
    base = np.abs(rel) <= ATT_HALO
    not_prev = band_idx >= ATT_HALO
    not_next = band_idx < ATT_HALO + ATT_BQ
    valid = np.stack([base & not_prev, base, base & not_next, base & not_prev & not_next])
    return valid, _t5_bucket(rel * dilation)


def _bias_tiles(rel_bias, dilation, queries_are_rows_of_block):
    valid, bucket = _band_tables(dilation, queries_are_rows_of_block)
    onehot = (jnp.asarray(bucket.reshape(-1, 1)) == jnp.arange(REL_BUCKETS, dtype=jnp.int32)[None, :]).astype(F32)
    rb = jnp.dot(onehot, rel_bias.astype(F32), precision=lax.Precision.HIGHEST)
    rb = rb.T.reshape((SWA_HEADS,) + bucket.shape)
    return jnp.where(valid[:, None], rb[None], NEG_BIG).astype(F32)


def _group_sum(x, bd):
    hi = x.astype(BF16)
    lo = (x - hi.astype(F32)).astype(BF16)
    return jnp.dot(hi, bd, preferred_element_type=F32) + jnp.dot(lo, bd, preferred_element_type=F32)


def _head_block_diag():
    idx = np.arange(SWA_WIDTH) // SWA_HEAD_DIM
    return jnp.asarray(idx[:, None] == idx[None, :], BF16)


def _swa_pre_fwd(p_pad, qw_row, kw_row, bd):
    s = p_pad.shape[0]
    tm = min(512, s)
    inv = 1.0 / SWA_HEAD_DIM

    def body(q_ref, k_ref, v_ref, qw_ref, kw_ref, bd_ref, qo_ref, ko_ref, vo_ref):
        bdv = bd_ref[...]
        q = q_ref[...]
        k = k_ref[...]
        rq = lax.rsqrt(_group_sum(q * q, bdv) * inv + EPS)
        rk = lax.rsqrt(_group_sum(k * k, bdv) * inv + EPS)
        qo_ref[...] = (q * rq * qw_ref[...] * SWA_SCALE).astype(BF16)
        ko_ref[...] = (k * rk * kw_ref[...]).astype(BF16)
        vo_ref[...] = v_ref[...].astype(BF16)

    base = OFF_B // SWA_WIDTH
    blk = pl.BlockSpec((tm, SWA_WIDTH), lambda i: (i, 0))
    vec = pl.BlockSpec((1, SWA_WIDTH), lambda i: (0, 0))
    return pl.pallas_call(
        body, name="swa_pre_fwd", grid=(s // tm,),
        in_specs=[pl.BlockSpec((tm, SWA_WIDTH), lambda i: (i, base)), pl.BlockSpec((tm, SWA_WIDTH), lambda i: (i, base + 1)),
                  pl.BlockSpec((tm, SWA_WIDTH), lambda i: (i, base + 2)), vec, vec,
                  pl.BlockSpec((SWA_WIDTH, SWA_WIDTH), lambda i: (0, 0))],
        out_specs=[blk, blk, blk],
        out_shape=[jax.ShapeDtypeStruct((s, SWA_WIDTH), BF16)] * 3,
        compiler_params=_params("parallel"),
    )(p_pad, p_pad, p_pad, qw_row, kw_row, bd)


def _swa_pre_bwd(dqs, dks, dvs, p_pad, qw_row, kw_row, bd, dp_all):
    s = p_pad.shape[0]
    tm = min(256, s)
    inv = 1.0 / SWA_HEAD_DIM
    npat = len(dqs)

    def body(*refs):
        dq_refs, dk_refs, dv_refs = refs[:npat], refs[npat:2 * npat], refs[2 * npat:3 * npat]
        q_ref, k_ref, qw_ref, kw_ref, bd_ref, _, dp_ref, dqw_ref, dkw_ref = refs[3 * npat:]

        @pl.when(pl.program_id(0) == 0)
        def _():
            dqw_ref[...] = jnp.zeros_like(dqw_ref)
            dkw_ref[...] = jnp.zeros_like(dkw_ref)

        bdv = bd_ref[...]

        def norm_bwd(x, g, w, scale):
            r = lax.rsqrt(_group_sum(x * x, bdv) * inv + EPS)
            xhat = x * r
            t = g * w * scale
            dx = r * (t - xhat * (_group_sum(t * xhat, bdv) * inv))
            return dx, jnp.sum(g * scale * xhat, axis=0, keepdims=True)

        def total(rs):
            t = rs[0][...].astype(F32)
            for r in rs[1:]:
                t = t + r[...].astype(F32)
            return t

        dq, dqw = norm_bwd(q_ref[...], total(dq_refs), qw_ref[...], SWA_SCALE)
        dk, dkw = norm_bwd(k_ref[...], total(dk_refs), kw_ref[...], 1.0)
        dp_ref[:, 0:SWA_WIDTH] = dq.astype(BF16)
        dp_ref[:, SWA_WIDTH:2 * SWA_WIDTH] = dk.astype(BF16)
        dp_ref[:, 2 * SWA_WIDTH:3 * SWA_WIDTH] = total(dv_refs).astype(BF16)
        dqw_ref[...] += dqw
        dkw_ref[...] += dkw

    base = OFF_B // SWA_WIDTH
    blk = pl.BlockSpec((tm, SWA_WIDTH), lambda i: (i, 0))
    vec = pl.BlockSpec((1, SWA_WIDTH), lambda i: (0, 0))
    return pl.pallas_call(
        body, name="swa_pre_bwd", grid=(s // tm,),
        in_specs=[blk] * (3 * npat) + [pl.BlockSpec((tm, SWA_WIDTH), lambda i: (i, base)),
                                      pl.BlockSpec((tm, SWA_WIDTH), lambda i: (i, base + 1)), vec, vec,
                                      pl.BlockSpec((SWA_WIDTH, SWA_WIDTH), lambda i: (0, 0)), ANY],
        out_specs=[pl.BlockSpec((tm, 3 * SWA_WIDTH), lambda i: (i, OFF_B // (3 * SWA_WIDTH))), vec, vec],
        out_shape=[jax.ShapeDtypeStruct(dp_all.shape, dp_all.dtype), jax.ShapeDtypeStruct((1, SWA_WIDTH), F32),
                   jax.ShapeDtypeStruct((1, SWA_WIDTH), F32)],
        input_output_aliases={3 * npat + 5: 0},
        compiler_params=_params("arbitrary"),
    )(*dqs, *dks, *dvs, p_pad, p_pad, qw_row, kw_row, bd, dp_all)


def _band_specs(length, rows):
    per = rows // ATT_HALO
    last = length // ATT_HALO - 1
    prev = pl.BlockSpec((ATT_HALO, SWA_WIDTH), lambda r, t: (jnp.maximum(t * per - 1, 0), r))
    cur = pl.BlockSpec((rows, SWA_WIDTH), lambda r, t: (t, r))
    nxt = pl.BlockSpec((ATT_HALO, SWA_WIDTH), lambda r, t: (jnp.minimum((t + 1) * per, last), r))
    return [prev, cur, nxt]


def _tile_variant(t, nb, u, sub):
    first, last = u == 0, u == sub - 1
    if first and last:
        return 3 if nb == 1 else jnp.where(t == 0, 0, jnp.where(t == nb - 1, 2, 1))
    if first:
        return jnp.where(t == 0, 0, 1)
    if last:
        return jnp.where(t == nb - 1, 2, 1)
    return 1


def _bias_specs(nb, sub, rows, cols):
    return [pl.BlockSpec((1, SWA_HEADS, rows, cols),
                         functools.partial(lambda r, t, u: (_tile_variant(t, nb, u, sub), 0, 0, 0), u=u))
            for u in range(sub)]


def _band(refs):
    return jnp.concatenate([r[...] for r in refs], axis=0)


def _sub(u, width=ATT_BQ):
    return slice(u * ATT_BQ, u * ATT_BQ + width)


N_PAIRS = SWA_HEADS // 2


def _pairs(x):
    return jnp.stack([x[:, LANE * p:LANE * (p + 1)] for p in range(N_PAIRS)])


def _per_head_rows(x):
    first = lax.broadcasted_iota(jnp.int32, x.shape, 2) < SWA_HEAD_DIM
    zero = jnp.zeros_like(x)
    return jnp.concatenate([jnp.where(first, x, zero), jnp.where(first, zero, x)], axis=1)


def _per_head_cols(x):
    return jnp.stack([jnp.concatenate([x[:, LANE * p:LANE * p + 1],
                                       x[:, LANE * p + SWA_HEAD_DIM:LANE * p + SWA_HEAD_DIM + 1]], axis=0)
                      for p in range(N_PAIRS)])


def _merge_heads(x, rows):
    first = lax.broadcasted_iota(jnp.int32, (N_PAIRS, rows, LANE), 2) < SWA_HEAD_DIM
    return jnp.where(first, x[:, :rows], x[:, rows:])


def _store_pairs(ref, x, rows):
    for p in range(N_PAIRS):
        ref[rows, LANE * p:LANE * (p + 1)] = x[p].astype(ref.dtype)


def _att_fwd2(q, k, v, bias, dilation):
    s = q.shape[0]
    length = s // dilation
    sub = min(ATT_SUB, length // ATT_BQ)
    rows = sub * ATT_BQ
    nb = length // rows
    view = (length, dilation * SWA_WIDTH)

    def body(q_ref, kp, kc, kn, vp, vc, vn, *rest):
        b_refs, (o_ref, lse_ref) = rest[:sub], rest[sub:]
        kwin, vwin = _band((kp, kc, kn)), _band((vp, vc, vn))
        for u in range(sub):
            kb, vb = _pairs(kwin[_sub(u, ATT_BK)]), _pairs(vwin[_sub(u, ATT_BK)])
            qm = _per_head_rows(_pairs(q_ref[_sub(u), :]))
            sc = _bdot(qm, kb, BNT) + b_refs[u][0].reshape(N_PAIRS, 2 * ATT_BQ, ATT_BK)
            m = jnp.max(sc, axis=-1, keepdims=True)
            p = jnp.exp(sc - m)
            den = jnp.sum(p, axis=-1, keepdims=True)
            o = _bdot(p, vb) / den
            _store_pairs(o_ref, _merge_heads(o, ATT_BQ), _sub(u))
            lse = jnp.broadcast_to(m + jnp.log(den), (N_PAIRS, 2 * ATT_BQ, LANE))
            _store_pairs(lse_ref, _merge_heads(lse, ATT_BQ), _sub(u))

    cur = pl.BlockSpec((rows, SWA_WIDTH), lambda r, t: (t, r))
    o, lse = pl.pallas_call(
        body, name=f"att_fwd_d{dilation}", grid=(dilation, nb),
        in_specs=[cur] + _band_specs(length, rows) * 2 + _bias_specs(nb, sub,ATT_BQ, ATT_BK),
        out_specs=[cur, cur],
        out_shape=[jax.ShapeDtypeStruct(view, BF16), jax.ShapeDtypeStruct(view, F32)],
        compiler_params=_params("parallel", "parallel"),
    )(q.reshape(view), *([k.reshape(view)] * 3), *([v.reshape(view)] * 3), *([bias] * sub))
    return o.reshape(s, SWA_WIDTH), lse.reshape(s, SWA_WIDTH)


def _att_dq2(q, k, v, dop, lse, cp, bias, dilation):
    s = q.shape[0]
    length = s // dilation
    sub = min(ATT_SUB, length // ATT_BQ)
    rows = sub * ATT_BQ
    nb = length // rows
    view = (length, dilation * SWA_WIDTH)

    def body(q_ref, kp, kc, kn, vp, vc, vn, do_ref, lse_ref, cp_ref, *rest):
        b_refs, (dq_ref, db_ref) = rest[:sub], rest[sub:]

        @pl.when((pl.program_id(0) == 0) & (pl.program_id(1) == 0))
        def _():
            db_ref[...] = jnp.zeros_like(db_ref)

        kwin, vwin = _band((kp, kc, kn)), _band((vp, vc, vn))
        for u in range(sub):
            kb, vb = _pairs(kwin[_sub(u, ATT_BK)]), _pairs(vwin[_sub(u, ATT_BK)])
            qm = _per_head_rows(_pairs(q_ref[_sub(u), :]))
            dom = _per_head_rows(_pairs(do_ref[_sub(u), :]))
            sc = _bdot(qm, kb, BNT) + b_refs[u][0].reshape(N_PAIRS, 2 * ATT_BQ, ATT_BK)
            p = jnp.exp(sc - _per_head_cols(lse_ref[_sub(u), :]))
            ds = p * (_bdot(dom, vb, BNT) + _per_head_cols(cp_ref[_sub(u), :]))
            _store_pairs(dq_ref, _merge_heads(_bdot(ds, kb), ATT_BQ), _sub(u))
            db_ref[_tile_variant(pl.program_id(1), nb, u, sub)] += ds.reshape(SWA_HEADS, ATT_BQ, ATT_BK)

    cur = pl.BlockSpec((rows, SWA_WIDTH), lambda r, t: (t, r))
    dq, db = pl.pallas_call(
        body, name=f"att_dq_d{dilation}", grid=(dilation, nb),
        in_specs=[cur] + _band_specs(length, rows) * 2 + [cur, cur, cur] + _bias_specs(nb, sub,ATT_BQ, ATT_BK),
        out_specs=[cur, pl.BlockSpec((4, SWA_HEADS, ATT_BQ, ATT_BK), lambda r, t: (0, 0, 0, 0))],
        out_shape=[jax.ShapeDtypeStruct(view, BF16), jax.ShapeDtypeStruct((4, SWA_HEADS, ATT_BQ, ATT_BK), F32)],
        compiler_params=_params("arbitrary", "arbitrary"),
    )(q.reshape(view), *([k.reshape(view)] * 3), *([v.reshape(view)] * 3), dop.reshape(view), lse.reshape(view),
      cp.reshape(view), *([bias] * sub))
    return dq.reshape(s, SWA_WIDTH), db


def _att_dkv2(q, k, v, dop, lse, cp, bias_t, dilation):
    s = q.shape[0]
    length = s // dilation
    sub = min(ATT_SUB, length // ATT_BQ)
    rows = sub * ATT_BQ
    nb = length // rows
    view = (length, dilation * SWA_WIDTH)

    def body(k_ref, v_ref, qp, qc, qn, dp_, dc_, dn_, lp, lc, ln, cp_, cc_, cn_, *rest):
        b_refs, (dk_ref, dv_ref) = rest[:sub], rest[sub:]
        qwin, dowin = _band((qp, qc, qn)), _band((dp_, dc_, dn_))
        lsewin, cpwin = _band((lp, lc, ln)), _band((cp_, cc_, cn_))
        for u in range(sub):
            band = _sub(u, ATT_BK)
            qm = _per_head_rows(_pairs(qwin[band]))
            dom = _per_head_rows(_pairs(dowin[band]))
            kv, vv = _pairs(k_ref[_sub(u), :]), _pairs(v_ref[_sub(u), :])
            sc = _bdot(qm, kv, BNT) + b_refs[u][0].reshape(N_PAIRS, 2 * ATT_BK, ATT_BQ)
            p = jnp.exp(sc - _per_head_cols(lsewin[band]))
            _store_pairs(dv_ref, _bdot(p, dom, BTN), _sub(u))
            ds = p * (_bdot(dom, vv, BNT) + _per_head_cols(cpwin[band]))
            _store_pairs(dk_ref, _bdot(ds, qm, BTN), _sub(u))

    cur = pl.BlockSpec((rows, SWA_WIDTH), lambda r, t: (t, r))
    dk, dv = pl.pallas_call(
        body, name=f"att_dkv_d{dilation}", grid=(dilation, nb),
        in_specs=[cur, cur] + _band_specs(length, rows) * 4 + _bias_specs(nb, sub,ATT_BK, ATT_BQ),
        out_specs=[cur, cur],
        out_shape=[jax.ShapeDtypeStruct(view, BF16)] * 2,
        compiler_params=_params("parallel", "parallel"),
    )(k.reshape(view), v.reshape(view), *([q.reshape(view)] * 3), *([dop.reshape(view)] * 3),
      *([lse.reshape(view)] * 3), *([cp.reshape(view)] * 3), *([bias_t] * sub))
    return dk.reshape(s, SWA_WIDTH), dv.reshape(s, SWA_WIDTH)


def _pattern_weights(lses):
    m = lses[0]
    for l in lses[1:]:
        m = jnp.maximum(m, l)
    es = [jnp.exp(l - m) for l in lses]
    den = es[0]
    for e in es[1:]:
        den = den + e
    return [e / den for e in es]


def _combine_fwd(outs, lses):
    s = outs[0].shape[0]
    tm = min(512, s)
    npat = len(outs)

    def body(*refs):
        ws = _pattern_weights([r[...] for r in refs[npat:2 * npat]])
        o = ws[0] * refs[0][...]
        for p in range(1, npat):
            o = o + ws[p] * refs[p][...]
        refs[2 * npat][...] = o.astype(BF16)

    blk = pl.BlockSpec((tm, SWA_WIDTH), lambda i: (i, 0))
    return pl.pallas_call(
        body, name="swa_combine_fwd", grid=(s // tm,), in_specs=[blk] * (2 * npat), out_specs=blk,
        out_shape=jax.ShapeDtypeStruct((s, SWA_WIDTH), BF16), compiler_params=_params("parallel"),
    )(*outs, *lses)


def _combine_bwd(d_out, outs, lses, bd):
    s = d_out.shape[0]
    tm = min(512, s)
    npat = len(outs)

    def body(*refs):
        d_ref, bd_ref = refs[0], refs[1 + 2 * npat]
        o_refs, l_refs = refs[1:1 + npat], refs[1 + npat:1 + 2 * npat]
        out_refs = refs[2 + 2 * npat:]
        ws = _pattern_weights([r[...] for r in l_refs])
        dov = d_ref[...]
        o = ws[0] * o_refs[0][...]
        for p in range(1, npat):
            o = o + ws[p] * o_refs[p][...]
        rd = _group_sum(dov * o, bd_ref[...])
        for p in range(npat):
            out_refs[p][...] = (ws[p] * dov).astype(BF16)
            out_refs[npat + p][...] = -ws[p] * rd

    blk = pl.BlockSpec((tm, SWA_WIDTH), lambda i: (i, 0))
    res = pl.pallas_call(
        body, name="swa_combine_bwd", grid=(s // tm,),
        in_specs=[blk] * (1 + 2 * npat) + [pl.BlockSpec((SWA_WIDTH, SWA_WIDTH), lambda i: (0, 0))],
        out_specs=[blk] * (2 * npat),
        out_shape=[jax.ShapeDtypeStruct((s, SWA_WIDTH), BF16)] * npat + [jax.ShapeDtypeStruct((s, SWA_WIDTH), F32)] * npat,
        compiler_params=_params("parallel"),
    )(d_out, *outs, *lses, bd)
    return res[:npat], res[npat:]


def _rel_bias_grad(dbs, buckets):
    npat = len(dbs)

    def body(*refs):
        db_refs, bk_refs, o_ref = refs[:npat], refs[npat:2 * npat], refs[2 * npat]
        row = lax.broadcasted_iota(jnp.int32, (REL_BUCKETS, LANE), 0)
        lane = lax.broadcasted_iota(jnp.int32, (REL_BUCKETS, LANE), 1)
        tiles = [[db_refs[p][0, h] + db_refs[p][1, h] + db_refs[p][2, h] + db_refs[p][3, h] for h in range(SWA_HEADS)]
                 for p in range(npat)]
        bks = [r[...] for r in bk_refs]

        def one_bucket(b, acc):
            for h in range(SWA_HEADS):
                tot = jnp.zeros((1, 1), F32)
                for p in range(npat):
                    sel = jnp.where(bks[p] == b, tiles[p][h], 0.0)
                    tot = tot + jnp.sum(jnp.sum(sel, axis=1, keepdims=True), axis=0, keepdims=True)
                acc = acc + jnp.where((row == b) & (lane == h), tot, 0.0)
            return acc

        o_ref[...] = lax.fori_loop(0, REL_BUCKETS, one_bucket, jnp.zeros((REL_BUCKETS, LANE), F32))

    full4 = pl.BlockSpec((4, SWA_HEADS, ATT_BQ, ATT_BK), lambda: (0, 0, 0, 0))
    full2 = pl.BlockSpec((ATT_BQ, ATT_BK), lambda: (0, 0))
    return pl.pallas_call(
        body, name="rel_bias_grad", in_specs=[full4] * npat + [full2] * npat,
        out_specs=pl.BlockSpec((REL_BUCKETS, LANE), lambda: (0, 0)),
        out_shape=jax.ShapeDtypeStruct((REL_BUCKETS, LANE), F32),
        compiler_params=pltpu.CompilerParams(vmem_limit_bytes=V7X_VMEM_LIMIT_BYTES),
    )(*dbs, *buckets)


def _swa_forward(p_pad, qw_row, kw_row, rel_bias, bd):
    q, k, v = _swa_pre_fwd(p_pad, qw_row, kw_row, bd)
    outs, lses = [], []
    for _, dil in DILATION_PATTERNS:
        o, lse = _att_fwd2(q, k, v, _bias_tiles(rel_bias, dil, True), dil)
        outs.append(o)
        lses.append(lse)
    return _combine_fwd(outs, lses), (q, k, v, outs, lses)


def _swa_backward(d_out, p_pad, qw_row, kw_row, rel_bias, bd, saved, dp_all):
    q, k, v, outs, lses = saved
    dops, cps = _combine_bwd(d_out, outs, lses, bd)
    dqs, dks, dvs, dbs, buckets = [], [], [], [], []
    for p, (_, dil) in enumerate(DILATION_PATTERNS):
        dq, db = _att_dq2(q, k, v, dops[p], lses[p], cps[p], _bias_tiles(rel_bias, dil, True), dil)
        dk, dv = _att_dkv2(q, k, v, dops[p], lses[p], cps[p], _bias_tiles(rel_bias, dil, False), dil)
        dqs.append(dq)
        dks.append(dk)
        dvs.append(dv)
        dbs.append(db)
        buckets.append(jnp.asarray(_band_tables(dil, True)[1]))
    dp, dqw, dkw = _swa_pre_bwd(dqs, dks, dvs, p_pad, qw_row, kw_row, bd, dp_all)
    return dp, dqw, dkw, _rel_bias_grad(dbs, buckets)


def _lane_row(v):
    flat = v.reshape(-1).astype(F32)
    return jnp.zeros((1, LANE), F32).at[0, :flat.shape[0]].set(flat)


W_IN_SHARD = N_IN // N_DEV
W_IN_RUNS = ((0, NAT_Z, 0), (NAT_Z, NAT_AB, OFF_Z), (NAT_AB, NAT_B, OFF_AB), (NAT_B, N_IN, OFF_B))


def _w_in_pieces(shard):
    lo, hi = shard * W_IN_SHARD, (shard + 1) * W_IN_SHARD
    out = []
    for first, last, dst in W_IN_RUNS:
        a, b = max(lo, first), min(hi, last)
        if a < b:
            out.append((a - lo, b - a, dst + a - first))
    return out


def _w_in_from_slabs(w3):
    nd, r, _ = w3.shape

    def body(w_ref, o_ref):
        o_ref[:, OFF_AB:N_PAD] = jnp.zeros((r, N_PAD - OFF_AB), w3.dtype)
        for sh in range(nd):
            for src, length, dst in _w_in_pieces(sh):
                o_ref[:, dst:dst + length] = w_ref[sh, :, src:src + length]

    return pl.pallas_call(
        body, name="w_in_from_slabs", out_shape=jax.ShapeDtypeStruct((r, N_PAD), w3.dtype),
        compiler_params=pltpu.CompilerParams(vmem_limit_bytes=V7X_VMEM_LIMIT_BYTES),
    )(w3)


def _w_in_grad_slabs(dw_pad, dtype):
    r = dw_pad.shape[0]

    def body(dw_ref, o_ref):
        for sh in range(N_DEV):
            for src, length, dst in _w_in_pieces(sh):
                o_ref[sh, :, src:src + length] = dw_ref[:, dst:dst + length].astype(dtype)

    return pl.pallas_call(
        body, name="w_in_grad_slabs", out_shape=jax.ShapeDtypeStruct((N_DEV, r, W_IN_SHARD), dtype),
        compiler_params=pltpu.CompilerParams(vmem_limit_bytes=V7X_VMEM_LIMIT_BYTES),
    )(dw_pad)


LATE = ("w_out", "ffn2_w_gate", "ffn2_w_up", "ffn2_w_down")
TRANSPOSED = ("ffn1_w_gate", "ffn1_w_up", "ffn2_w_gate", "ffn2_w_up")


def _late_weights(slabs):
    return {n: g.reshape(N_DEV * g.shape[1], g.shape[2]) for n, g in zip(LATE, slabs)}


def _local_step(x, tgt, wts, small, late_shards=None):
    bd = _head_block_diag()
    conv_wt = jnp.zeros((8, QKV_A), F32).at[:CONV_WIDTH].set(small["conv_w"].T)
    alog_row, dt_row = _lane_row(small["a_log"]), _lane_row(small["dt_bias"])
    gnorm_row = small["gdn_norm_w"].reshape(1, GDN_HEAD_DIM)
    qw_row = jnp.tile(small["q_norm_w"].reshape(-1), SWA_HEADS).reshape(1, SWA_WIDTH)
    kw_row = jnp.tile(small["k_norm_w"].reshape(-1), SWA_HEADS).reshape(1, SWA_WIDTH)
    rel_bias = small["rel_bias"]
    exchange = late_shards is not None
    dw_dtype = BF16 if exchange else F32

    x1, sv1, wd1, got = _ffn_forward(
        x, small["ffn1_norm"], wts["ffn1_w_gate"], wts["ffn1_w_up"], wts.get("ffn1_w_down"), "ffn1",
        gather=[late_shards["ffn1_w_down"], late_shards["w_in"]] if exchange else ())
    win_pad = _w_in_from_slabs(got[0]) if exchange else wts["w_in_pad"]
    n2, r2 = _rms_fwd(x1, small["mix_norm"], "mix_norm")
    p_pad = _matmul([(n2, win_pad)], tm=256, tn=N_PAD, tk=D_MODEL, name="w_in")
    o_a, sva, gathered = _gdn_forward(p_pad, conv_wt, alog_row, dt_row, gnorm_row,
                                      gather=[late_shards[n] for n in LATE] if exchange else ())
    if exchange:
        wts = {**wts, **_late_weights(gathered)}
    wo_a, wo_b = wts["w_out"][:GDN_WIDTH], wts["w_out"][GDN_WIDTH:]
    o_b, svb = _swa_forward(p_pad, qw_row, kw_row, rel_bias, bd)
    x2 = _matmul([(o_a, wo_a), (o_b, wo_b)], tm=512, tn=D_MODEL, tk=GDN_WIDTH, name="w_out", res=x1)
    x3, sv2, _, _ = _ffn_forward(x2, small["ffn2_norm"], wts["ffn2_w_gate"], wts["ffn2_w_up"], wts["ffn2_w_down"], "ffn2")
    loss_row, dx3, d_final = _final_loss(x3, small["final_norm"], tgt)

    dx2, d_ffn2_norm, dwg2, dwu2, dwd2, _ = _ffn_backward(
        dx3, x2, small["ffn2_norm"], wts["ffn2_w_gate"], wts["ffn2_w_up"], wts["ffn2_w_down"], sv2, "ffn2", dw_dtype)
    d_oa = _matmul([(dx2, wo_a)], tb=True, tm=512, tn=GDN_WIDTH, tk=D_MODEL, name="w_out_da")
    d_ob = _matmul([(dx2, wo_b)], tb=True, tm=512, tn=SWA_WIDTH, tk=D_MODEL, name="w_out_db")
    dwo_a = _matmul([(o_a, dx2)], ta=True, tm=GDN_WIDTH, tn=D_MODEL, tk=2048, name="w_out_dwa", out_dtype=dw_dtype)
    dwo_b = _matmul([(o_b, dx2)], ta=True, tm=SWA_WIDTH, tn=D_MODEL, tk=2048, name="w_out_dwb", out_dtype=dw_dtype)

    late_grads = [_row_slabs(jnp.concatenate([dwo_a, dwo_b], axis=0)), dwg2, dwu2, dwd2]
    dp_all, dconv, gate_sums, d_gnorm, received = _gdn_backward(
        d_oa, p_pad, conv_wt, alog_row, dt_row, gnorm_row, sva, scatter=late_grads if exchange else ())
    if exchange:
        late_grads = received
    dp_all, dqw, dkw, d_rel = _swa_backward(d_ob, p_pad, qw_row, kw_row, rel_bias, bd, svb, dp_all)
    dw_pad = _matmul([(n2, dp_all)], ta=True, tm=D_MODEL, tn=N_PAD // 3, tk=2048, name="w_in_dw")
    dn2 = _matmul([(dp_all, win_pad)], tb=True, tm=512, tn=D_MODEL, tk=N_PAD, name="w_in_dn")
    dx1, d_mix_norm = _rms_bwd(dn2, x1, r2, small["mix_norm"], dx2, "mix_dnorm")
    d_w_in = _w_in_grad_slabs(dw_pad, dw_dtype)
    dx, d_ffn1_norm, dwg1, dwu1, dwd1, got = _ffn_backward(
        dx1, x, small["ffn1_norm"], wts["ffn1_w_gate"], wts["ffn1_w_up"], wd1, sv1, "ffn1", dw_dtype,
        scatter=[d_w_in] if exchange else None)
    if exchange:
        d_w_in = got[0]

    grads = {
        "ffn1_norm": d_ffn1_norm, "ffn1_w_gate": dwg1, "ffn1_w_up": dwu1, "ffn1_w_down": dwd1,
        "mix_norm": d_mix_norm, "w_in": d_w_in, "conv_w": dconv[:CONV_WIDTH].T,
        "a_log": gate_sums[0, :8].reshape(2, GDN_HEADS), "dt_bias": gate_sums[1, :8].reshape(2, GDN_HEADS),
        "gdn_norm_w": d_gnorm, "q_norm_w": dqw.reshape(SWA_HEADS, SWA_HEAD_DIM).sum(0, keepdims=True),
        "k_norm_w": dkw.reshape(SWA_HEADS, SWA_HEAD_DIM).sum(0, keepdims=True), "rel_bias": d_rel[:, :SWA_HEADS],
        "ffn2_norm": d_ffn2_norm, "final_norm": d_final, **dict(zip(LATE, late_grads)),
    }
    return loss_row, dx, grads


MESH_IDS = pl.DeviceIdType.MESH
ANY = pl.BlockSpec(memory_space=pl.ANY)


def _adamw(parts, w, m, v, name):
    nparts, r, n = parts.shape
    tr = r
    for cand in (256, 176, 128, 104, 64, 8):
        if r % cand == 0:
            tr = cand
            break
    bc1 = 1.0 - ADAM_B1 ** ADAM_STEP
    bc2 = 1.0 - ADAM_B2 ** ADAM_STEP

    def body(p_ref, w_ref, m_ref, v_ref, g_ref, d_ref, nm_ref, nv_ref):
        g = p_ref[0].astype(F32)
        for k in range(1, nparts):
            g = g + p_ref[k].astype(F32)
        mn = ADAM_B1 * m_ref[...] + (1.0 - ADAM_B1) * g
        vn = ADAM_B2 * v_ref[...] + (1.0 - ADAM_B2) * (g * g)
        m_hat = mn / bc1
        v_hat = vn / bc2
        g_ref[...] = g
        nm_ref[...] = mn
        nv_ref[...] = vn
        d_ref[...] = -ADAM_LR * (m_hat / (jnp.sqrt(v_hat) + ADAM_EPS) + ADAM_WD * w_ref[...])

    blk = pl.BlockSpec((tr, n), lambda i: (i, 0))
    return pl.pallas_call(
        body, name=name, grid=(r // tr,),
        in_specs=[pl.BlockSpec((nparts, tr, n), lambda i: (0, i, 0)), blk, blk, blk],
        out_specs=[blk] * 4, out_shape=[jax.ShapeDtypeStruct((r, n), F32)] * 4,
        compiler_params=_params("parallel"),
    )(parts, w, m, v)


def _mesh_place():
    x, y, c = lax.axis_index("x"), lax.axis_index("y"), lax.axis_index("c")
    return x, y, c, [(1 - x, y), (x, 1 - y), (1 - x, 1 - y)]


def _gather_phases(x_refs, out_refs, send_sems, recv_sems, local_sems):
    na = len(x_refs)

    def place():
        x, y, c, chips = _mesh_place()
        return (x, y, c), (x, y, 1 - c), chips, c

    def slab(i, px, py, pc):
        return out_refs[i].at[4 * px + 2 * py + pc]

    def copy(i, k, block, to, src=None):
        return pltpu.make_async_remote_copy(
            src_ref=slab(i, *block) if src is None else src, dst_ref=slab(i, *block),
            send_sem=send_sems.at[i, k], recv_sem=recv_sems.at[i, k], device_id=to, device_id_type=MESH_IDS)

    def own(i, me):
        return pltpu.make_async_copy(x_refs[i], slab(i, *me), local_sems.at[i])

    def sends(i, me, sibling, chips, c):
        return [copy(i, 0, me, sibling, src=x_refs[i])] + [copy(i, 1 + j, me, (*chip, c), src=x_refs[i])
                                                          for j, chip in enumerate(chips)]

    def start():
        me, sibling, chips, c = place()
        for i in range(na):
            own(i, me).start()
            for cp in sends(i, me, sibling, chips, c):
                cp.start()

    def forward():
        me, sibling, chips, c = place()
        for j, chip in enumerate(chips):
            for i in range(na):
                copy(i, 1 + j, (*chip, c), me).wait_recv()
                copy(i, 4 + j, (*chip, c), sibling).start()

    def finish():
        me, sibling, chips, c = place()
        for i in range(na):
            copy(i, 0, sibling, me).wait_recv()
        for j, chip in enumerate(chips):
            for i in range(na):
                copy(i, 4 + j, (*chip, 1 - c), me).wait_recv()
        for i in range(na):
            for cp in sends(i, me, sibling, chips, c):
                cp.wait_send()
            for j, chip in enumerate(chips):
                copy(i, 4 + j, (*chip, c), sibling).wait_send()
            own(i, me).wait()

    return start, forward, finish


def _gather_semaphores(na):
    return [pltpu.SemaphoreType.DMA((na, 7)), pltpu.SemaphoreType.DMA((na, 7)), pltpu.SemaphoreType.DMA((na,))]


def _scatter_phases(g_refs, out_refs, send_sems, recv_sems, local_sems):
    na = len(g_refs)

    def place(m):
        x, y, c = lax.axis_index("x"), lax.axis_index("y"), lax.axis_index("c")
        px = 1 - x if m & 4 else x
        py = 1 - y if m & 2 else y
        pc = 1 - c if m & 1 else c
        return 4 * x + 2 * y + c, (px, py, pc), 4 * px + 2 * py + pc

    def own(i):
        me, _, _ = place(0)
        return pltpu.make_async_copy(g_refs[i].at[me], out_refs[i].at[me], local_sems.at[i])

    def start():
        for i in range(na):
            own(i).start()
            for m in range(1, N_DEV):
                me, peer, peer_idx = place(m)
                pltpu.make_async_remote_copy(
                    src_ref=g_refs[i].at[peer_idx], dst_ref=out_refs[i].at[me], send_sem=send_sems.at[i, m - 1],
                    recv_sem=recv_sems.at[i, m - 1], device_id=peer, device_id_type=MESH_IDS).start()

    def finish():
        for i in range(na):
            for m in range(1, N_DEV):
                me, peer, peer_idx = place(m)
                cp = pltpu.make_async_remote_copy(
                    src_ref=g_refs[i].at[peer_idx], dst_ref=out_refs[i].at[peer_idx], send_sem=send_sems.at[i, m - 1],
                    recv_sem=recv_sems.at[i, m - 1], device_id=peer, device_id_type=MESH_IDS)
                cp.wait_recv()
                cp.wait_send()
            own(i).wait()

    return start, finish


def _all_gather_many(vs, name):
    na = len(vs)

    def body(*refs):
        x_refs, out_refs = refs[:na], refs[na:2 * na]
        for step in _gather_phases(x_refs, out_refs, *refs[2 * na:]):
            step()

    return pl.pallas_call(
        body, name=name, in_specs=[ANY] * na, out_specs=[ANY] * na,
        out_shape=[jax.ShapeDtypeStruct((N_DEV,) + v.shape, v.dtype) for v in vs],
        scratch_shapes=_gather_semaphores(na),
        compiler_params=pltpu.CompilerParams(vmem_limit_bytes=V7X_VMEM_LIMIT_BYTES),
    )(*vs)


BIG = ("ffn1_w_gate", "ffn1_w_up", "ffn1_w_down", "w_in", "w_out", "ffn2_w_gate", "ffn2_w_up", "ffn2_w_down")
SMALL = ("ffn1_norm", "mix_norm", "a_log", "dt_bias", "gdn_norm_w", "q_norm_w", "k_norm_w", "rel_bias",
         "ffn2_norm", "final_norm")
WEIGHTS = ("ffn1_norm", "ffn1_w_gate", "ffn1_w_up", "ffn1_w_down", "mix_norm", "w_in", "conv_w", "a_log", "dt_bias",
           "gdn_norm_w", "q_norm_w", "k_norm_w", "rel_bias", "w_out", "ffn2_norm", "ffn2_w_gate", "ffn2_w_up",
           "ffn2_w_down", "final_norm")


def _pack(arrays, width, row_multiple):
    flat = jnp.concatenate([a.reshape(-1) for a in arrays])
    rows = -(-flat.shape[0] // width)
    rows = -(-rows // row_multiple) * row_multiple
    return jnp.pad(flat, (0, rows * width - flat.shape[0])).reshape(rows, width)


def _unpack(packed, shapes):
    flat = packed.reshape(-1)
    out, pos = [], 0
    for shp in shapes:
        size = int(np.prod(shp))
        out.append(flat[pos:pos + size].reshape(shp))
        pos += size
    return out


def kernel(x, ffn1_norm, ffn1_w_gate, ffn1_w_up, ffn1_w_down, mix_norm, w_in, conv_w, a_log, dt_bias, gdn_norm_w, q_norm_w, k_norm_w, rel_bias, w_out, ffn2_norm, ffn2_w_gate, ffn2_w_up, ffn2_w_down, final_norm, loss_target, m_ffn1_norm, m_ffn1_w_gate, m_ffn1_w_up, m_ffn1_w_down, m_mix_norm, m_w_in, m_conv_w, m_a_log, m_dt_bias, m_gdn_norm_w, m_q_norm_w, m_k_norm_w, m_rel_bias, m_w_out, m_ffn2_norm, m_ffn2_w_gate, m_ffn2_w_up, m_ffn2_w_down, m_final_norm, v_ffn1_norm, v_ffn1_w_gate, v_ffn1_w_up, v_ffn1_w_down, v_mix_norm, v_w_in, v_conv_w, v_a_log, v_dt_bias, v_gdn_norm_w, v_q_norm_w, v_k_norm_w, v_rel_bias, v_w_out, v_ffn2_norm, v_ffn2_w_gate, v_ffn2_w_up, v_ffn2_w_down, v_final_norm):
    w = dict(ffn1_norm=ffn1_norm, ffn1_w_gate=ffn1_w_gate, ffn1_w_up=ffn1_w_up, ffn1_w_down=ffn1_w_down, mix_norm=mix_norm, w_in=w_in, conv_w=conv_w, a_log=a_log, dt_bias=dt_bias, gdn_norm_w=gdn_norm_w, q_norm_w=q_norm_w, k_norm_w=k_norm_w, rel_bias=rel_bias, w_out=w_out, ffn2_norm=ffn2_norm, ffn2_w_gate=ffn2_w_gate, ffn2_w_up=ffn2_w_up, ffn2_w_down=ffn2_w_down, final_norm=final_norm)
    mom = dict(ffn1_norm=m_ffn1_norm, ffn1_w_gate=m_ffn1_w_gate, ffn1_w_up=m_ffn1_w_up, ffn1_w_down=m_ffn1_w_down, mix_norm=m_mix_norm, w_in=m_w_in, conv_w=m_conv_w, a_log=m_a_log, dt_bias=m_dt_bias, gdn_norm_w=m_gdn_norm_w, q_norm_w=m_q_norm_w, k_norm_w=m_k_norm_w, rel_bias=m_rel_bias, w_out=m_w_out, ffn2_norm=m_ffn2_norm, ffn2_w_gate=m_ffn2_w_gate, ffn2_w_up=m_ffn2_w_up, ffn2_w_down=m_ffn2_w_down, final_norm=m_final_norm)
    var = dict(ffn1_norm=v_ffn1_norm, ffn1_w_gate=v_ffn1_w_gate, ffn1_w_up=v_ffn1_w_up, ffn1_w_down=v_ffn1_w_down, mix_norm=v_mix_norm, w_in=v_w_in, conv_w=v_conv_w, a_log=v_a_log, dt_bias=v_dt_bias, gdn_norm_w=v_gdn_norm_w, q_norm_w=v_q_norm_w, k_norm_w=v_k_norm_w, rel_bias=v_rel_bias, w_out=v_w_out, ffn2_norm=v_ffn2_norm, ffn2_w_gate=v_ffn2_w_gate, ffn2_w_up=v_ffn2_w_up, ffn2_w_down=v_ffn2_w_down, final_norm=v_final_norm)
    ix, iy, ic = lax.axis_index("x"), lax.axis_index("y"), lax.axis_index("c")
    me = 4 * ix + 2 * iy + ic

    def local(a, n):
        return jnp.swapaxes(a[0], 0, 1) if n in TRANSPOSED else a[0]

    shard = {n: local(w[n], n) for n in BIG}

    conv_shard_shape = w["conv_w"][0].shape
    conv_elems = conv_shard_shape[0] * conv_shard_shape[1]
    first = ("ffn1_w_gate", "ffn1_w_up")
    gathered = _all_gather_many([shard[n].astype(BF16) for n in first] + [_pack([w["conv_w"][0]], LANE, 8)],
                                "gather_weights")
    wts = {n: g.reshape(N_DEV * g.shape[1], g.shape[2]) for n, g in zip(first, gathered)}

    small = {n: w[n][0] if n not in ("rel_bias",) else w[n] for n in SMALL}
    small = {n: (a.reshape(1, -1) if n.endswith("norm") else a) for n, a in small.items()}
    conv_all = gathered[-1].reshape(N_DEV, -1)
    small["conv_w"] = conv_all[:, :conv_elems].reshape(N_DEV * conv_shard_shape[0], conv_shard_shape[1])
    loss_row, grad_x, grads = _local_step(x[0], loss_target[0], wts, small,
                                          late_shards={n: shard[n].astype(BF16) for n in BIG if n not in first})

    big_out = [[], [], [], []]
    for n in BIG:
        for kind, val in enumerate(_adamw(grads[n], shard[n], local(mom[n], n), local(var[n], n), f"{n}_adamw")):
            big_out[kind].append(jnp.swapaxes(val, 0, 1) if n in TRANSPOSED else val)

    small_names = SMALL + ("conv_w",)
    small_shapes = [grads[n].shape for n in small_names] + [(1, 1)]
    g_small = _pack([grads[n] for n in small_names] + [loss_row[:, :1]], LANE, 8)
    all_small = _all_gather_many([g_small], "gather_small_grads")[0]
    riders = [jnp.zeros(shp, F32) for shp in small_shapes[len(SMALL):]]
    ws = _pack([w[n].reshape(grads[n].shape) for n in SMALL] + riders, LANE, 8)
    ms = _pack([mom[n].reshape(grads[n].shape) for n in SMALL] + riders, LANE, 8)
    vs = _pack([var[n].reshape(grads[n].shape) for n in SMALL] + riders, LANE, 8)
    small_out = [_unpack(a, small_shapes) for a in _adamw(all_small, ws, ms, vs, "adamw_small")]
    loss = small_out[0][-1][0, 0]
    conv_g = lax.dynamic_slice_in_dim(small_out[0][len(SMALL)], me * conv_shard_shape[0], conv_shard_shape[0], axis=0)
    conv_out = [_unpack(a, [conv_shard_shape])[0] for a in _adamw(
        _pack([conv_g], LANE, 8)[None], _pack([w["conv_w"][0]], LANE, 8), _pack([mom["conv_w"][0]], LANE, 8),
        _pack([var["conv_w"][0]], LANE, 8), "adamw_conv")]

    def leaf(kind, n):
        if n in BIG:
            val = big_out[kind][BIG.index(n)]
        elif n == "conv_w":
            val = conv_out[kind]
        else:
            val = small_out[kind][SMALL.index(n)]
        return val.reshape(w[n].shape)

    outs = [loss, grad_x[None]]
    for kind in range(4):
        outs += [leaf(kind, n) for n in WEIGHTS]
    return tuple(outs)
```

```python
import functools
import math

import numpy as np
import jax
import jax.numpy as jnp
from jax import lax
from jax.experimental import pallas as pl
from jax.experimental.pallas import tpu as pltpu

F32 = jnp.float32
BF16 = jnp.bfloat16

D_MODEL = 1024
D_FF = 2816
GDN_HEADS = 4
GDN_HEAD_DIM = 128
GDN_WIDTH = 512
CONV_WIDTH = 5
CHUNK = 64
SWA_HEADS = 8
SWA_HEAD_DIM = 64
SWA_WIDTH = 512
DILATION_PATTERNS = ((128, 1), (512, 4), (2048, 16))
REL_BUCKETS = 32
REL_MAX_DISTANCE = 1024
EPS = 1e-6
NEG_BIG = -1e30
N_DEV = 8

ADAM_LR = 0.001
ADAM_B1 = 0.9
ADAM_B2 = 0.999
ADAM_EPS = 1e-08
ADAM_WD = 0.01
ADAM_STEP = 10

QKV_A = 3 * GDN_WIDTH
OFF_B = QKV_A
OFF_Z = OFF_B + 3 * SWA_WIDTH
OFF_AB = OFF_Z + GDN_WIDTH
N_PAD = OFF_AB + 256
N_IN = 3600
NAT_Z, NAT_AB, NAT_B = QKV_A, QKV_A + GDN_WIDTH, QKV_A + GDN_WIDTH + 16

V7X_VMEM_LIMIT_BYTES = 56 * 1024 * 1024
LANE = 128
ATT_BQ = 128
ATT_HALO = 64
CONV_ROWS = 256

NN = (((1,), (0,)), ((), ()))
NT = (((1,), (1,)), ((), ()))
TN = (((0,), (0,)), ((), ()))


def _params(*sem):
    return pltpu.CompilerParams(dimension_semantics=sem, vmem_limit_bytes=V7X_VMEM_LIMIT_BYTES)


def _dot(a, b, dn=NN):
    return lax.dot_general(a.astype(BF16), b.astype(BF16), dn, preferred_element_type=F32)


def _sigmoid(x):
    return 1.0 / (1.0 + jnp.exp(-x))


class _Exchange:
    def __init__(self, kind, arrays):
        self.kind, self.arrays = kind, list(arrays)

    def out_shape(self):
        lead = (N_DEV,) if self.kind == "gather" else ()
        return [jax.ShapeDtypeStruct(lead + v.shape, v.dtype) for v in self.arrays]

    def hooks(self, in_refs, out_refs, sems, grid):
        step = pl.program_id(0)
        for axis in range(1, len(grid)):
            step = step * grid[axis] + pl.program_id(axis)
        total = math.prod(grid)
        if self.kind == "gather":
            assert total >= 4
            start, forward, finish = _gather_phases(in_refs, out_refs, *sems)
            pl.when(step == total // 2)(forward)
        else:
            assert total >= 2
            start, finish = _scatter_phases(in_refs, out_refs, *sems)
        pl.when(step == 0)(start)
        pl.when(step == total - 1)(finish)


def _pallas(body, *, name, grid, in_specs, out_specs, out_shape, args, semantics, scratch_shapes=(), exchange=None):
    n_in, n_out, n_scr = len(in_specs), len(out_specs), len(scratch_shapes)
    if exchange is None:
        res = pl.pallas_call(
            body, name=name, grid=grid, in_specs=list(in_specs), out_specs=list(out_specs), out_shape=list(out_shape),
            scratch_shapes=list(scratch_shapes), compiler_params=_params(*semantics))(*args)
        return list(res), []
    na = len(exchange.arrays)

    def carrying(*refs):
        ins, sent = refs[:n_in], refs[n_in:n_in + na]
        outs = refs[n_in + na:n_in + na + n_out]
        landed = refs[n_in + na + n_out:n_in + 2 * na + n_out]
        rest = refs[n_in + 2 * na + n_out:]
        exchange.hooks(sent, landed, rest[n_scr:], grid)
        body(*ins, *outs, *rest[:n_scr])

    res = pl.pallas_call(
        carrying, name=name, grid=grid, in_specs=list(in_specs) + [ANY] * na, out_specs=list(out_specs) + [ANY] * na,
        out_shape=list(out_shape) + exchange.out_shape(), scratch_shapes=list(scratch_shapes) + _gather_semaphores(na),
        compiler_params=_params(*(["arbitrary"] * len(grid))))(*args, *exchange.arrays)
    return list(res[:n_out]), list(res[n_out:])


def _matmul(pairs, *, ta=False, tb=False, out_dtype=F32, tm, tn, tk, name, res=None, alpha=None, shard_cols=None,
            exchange=None):
    a0, b0 = pairs[0]
    m = a0.shape[1] if ta else a0.shape[0]
    k = a0.shape[0] if ta else a0.shape[1]
    n = b0.shape[0] if tb else b0.shape[1]
    tm, tn, tk = min(tm, m), min(tn, n), min(tk, k)
    assert m % tm == 0 and n % tn == 0 and k % tk == 0, (name, m, n, k, tm, tn, tk)
    nk = k // tk
    npairs = len(pairs)
    dn = (((0 if ta else 1,), (1 if tb else 0,)), ((), ()))

    def body(*refs):
        ins = refs[:2 * npairs]
        pos = 2 * npairs
        r_ref = None
        if res is not None:
            r_ref = refs[pos]
            pos += 1
        o_ref, acc = refs[pos], refs[pos + 1]
        kk = pl.program_id(2)
        t = None
        for p in range(npairs):
            d = _dot(ins[2 * p][...], ins[2 * p + 1][...], dn)
            t = d if t is None else t + d

        if nk > 1:
            @pl.when(kk == 0)
            def _():
                acc[...] = t

            @pl.when((kk > 0) & (kk < nk - 1))
            def _():
                acc[...] += t

        @pl.when(kk == nk - 1)
        def _():
            r = acc[...] + t if nk > 1 else t
            if alpha is not None:
                r = r * alpha
            if r_ref is not None:
                r = r_ref[...] + r
            if shard_cols is None:
                o_ref[...] = r.astype(out_dtype)
            else:
                for sh in range(tn // shard_cols):
                    o_ref[sh] = r[:, sh * shard_cols:(sh + 1) * shard_cols].astype(out_dtype)

    a_spec = pl.BlockSpec((tk, tm), lambda i, j, kk: (kk, i)) if ta else pl.BlockSpec((tm, tk), lambda i, j, kk: (i, kk))
    b_spec = pl.BlockSpec((tn, tk), lambda i, j, kk: (j, kk)) if tb else pl.BlockSpec((tk, tn), lambda i, j, kk: (kk, j))
    o_spec = pl.BlockSpec((tm, tn), lambda i, j, kk: (i, j))
    in_specs = [a_spec, b_spec] * npairs + ([o_spec] if res is not None else [])
    args = [t for pr in pairs for t in pr] + ([res] if res is not None else [])
    out_spec, out_shape = o_spec, (m, n)
    if shard_cols is not None:
        assert res is None and tn % shard_cols == 0
        out_spec = pl.BlockSpec((tn // shard_cols, tm, shard_cols), lambda i, j, kk: (j, i, 0))
        out_shape = (n // shard_cols, m, shard_cols)
    (out,), exchanged = _pallas(
        body, name=name, grid=(m // tm, n // tn, nk), in_specs=in_specs, out_specs=[out_spec],
        out_shape=[jax.ShapeDtypeStruct(out_shape, out_dtype)],
        scratch_shapes=[pltpu.VMEM((tm, tn) if nk > 1 else (8, LANE), F32)],
        semantics=("parallel", "parallel", "arbitrary"), args=args, exchange=exchange)
    return out if exchange is None else (out, exchanged)


def _rms_fwd(x, w, name, exchange=None):
    s, d = x.shape
    tm = min(512, s)

    def body(x_ref, w_ref, n_ref, r_ref):
        xv = x_ref[...]
        r = lax.rsqrt(jnp.mean(xv * xv, axis=-1, keepdims=True) + EPS)
        n_ref[...] = (xv * r * w_ref[...]).astype(BF16)
        r_ref[...] = r

    (n, r), exchanged = _pallas(
        body, name=name, grid=(s // tm,),
        in_specs=[pl.BlockSpec((tm, d), lambda i: (i, 0)), pl.BlockSpec((1, d), lambda i: (0, 0))],
        out_specs=[pl.BlockSpec((tm, d), lambda i: (i, 0)), pl.BlockSpec((tm, 1), lambda i: (i, 0))],
        out_shape=[jax.ShapeDtypeStruct((s, d), BF16), jax.ShapeDtypeStruct((s, 1), F32)],
        semantics=("parallel",), args=(x, w), exchange=exchange)
    return (n, r) if exchange is None else (n, r, exchanged)


def _rms_bwd(dn, x, r, w, dres, name, exchange=None):
    s, d = x.shape
    tm = min(512, s)

    def body(dn_ref, x_ref, r_ref, w_ref, dres_ref, dx_ref, dw_ref):
        @pl.when(pl.program_id(0) == 0)
        def _():
            dw_ref[...] = jnp.zeros_like(dw_ref)

        rv = r_ref[...]
        xhat = x_ref[...] * rv
        g = dn_ref[...]
        t = g * w_ref[...]
        dx_ref[...] = dres_ref[...] + rv * (t - xhat * jnp.mean(t * xhat, axis=-1, keepdims=True))
        dw_ref[...] += jnp.sum(g * xhat, axis=0, keepdims=True)

    row = pl.BlockSpec((tm, d), lambda i: (i, 0))
    vec = pl.BlockSpec((1, d), lambda i: (0, 0))
    (dx, dw), exchanged = _pallas(
        body, name=name, grid=(s // tm,),
        in_specs=[row, row, pl.BlockSpec((tm, 1), lambda i: (i, 0)), vec, row],
        out_specs=[row, vec],
        out_shape=[jax.ShapeDtypeStruct((s, d), F32), jax.ShapeDtypeStruct((1, d), F32)],
        semantics=("arbitrary",), args=(dn, x, r, w, dres), exchange=exchange)
    return (dx, dw) if exchange is None else (dx, dw, exchanged)


def _final_loss(x3, wf, tgt):
    s, d = x3.shape
    tm = min(512, s)

    def body(x_ref, w_ref, t_ref, loss_ref, dx_ref, dw_ref):
        @pl.when(pl.program_id(0) == 0)
        def _():
            dw_ref[...] = jnp.zeros_like(dw_ref)
            loss_ref[...] = jnp.zeros_like(loss_ref)

        xv = x_ref[...]
        wv = w_ref[...]
        r = lax.rsqrt(jnp.mean(xv * xv, axis=-1, keepdims=True) + EPS)
        xhat = xv * r
        e = xhat * wv - t_ref[...]
        part = 0.5 * jnp.sum(jnp.mean(e * e, axis=-1, keepdims=True), axis=0, keepdims=True)
        loss_ref[...] += jnp.broadcast_to(part, loss_ref.shape)
        dy = e * (1.0 / d)
        dw_ref[...] += jnp.sum(dy * xhat, axis=0, keepdims=True)
        t = dy * wv
        dx_ref[...] = r * (t - xhat * jnp.mean(t * xhat, axis=-1, keepdims=True))

    row = pl.BlockSpec((tm, d), lambda i: (i, 0))
    vec = pl.BlockSpec((1, d), lambda i: (0, 0))
    return pl.pallas_call(
        body, name="final_loss", grid=(s // tm,),
        in_specs=[row, vec, row],
        out_specs=[pl.BlockSpec((1, LANE), lambda i: (0, 0)), row, vec],
        out_shape=[jax.ShapeDtypeStruct((1, LANE), F32), jax.ShapeDtypeStruct((s, d), F32),
                   jax.ShapeDtypeStruct((1, d), F32)],
        compiler_params=_params("arbitrary"),
    )(x3, wf, tgt)


def _ffn_up(n, wg, wu, name, exchange=None):
    s, d = n.shape
    f = wg.shape[0]
    tm, tn = min(512, s), f // 2

    def body(n_ref, wg_ref, wu_ref, g_ref, u_ref, a_ref):
        nv = n_ref[...]
        g = _dot(nv, wg_ref[...], NT)
        u = _dot(nv, wu_ref[...], NT)
        g_ref[...] = g.astype(BF16)
        u_ref[...] = u.astype(BF16)
        a_ref[...] = (g * _sigmoid(g) * u).astype(BF16)

    o = pl.BlockSpec((tm, tn), lambda j, i: (i, j))
    wspec = pl.BlockSpec((tn, d), lambda j, i: (j, 0))
    return _pallas(
        body, name=name, grid=(f // tn, s // tm),
        in_specs=[pl.BlockSpec((tm, d), lambda j, i: (i, 0)), wspec, wspec],
        out_specs=[o, o, o],
        out_shape=[jax.ShapeDtypeStruct((s, f), BF16)] * 3,
        semantics=("parallel", "parallel"), args=(n, wg, wu), exchange=exchange)


def _ffn_dact(dx, wd, g, u, name, exchange=None):
    s, d = dx.shape
    f = wd.shape[0]
    tm, tn = min(512, s), f // 2

    def body(dx_ref, wd_ref, g_ref, u_ref, dg_ref, du_ref):
        da = 0.5 * _dot(dx_ref[...], wd_ref[...], NT)
        gv = g_ref[...].astype(F32)
        sg = _sigmoid(gv)
        du_ref[...] = (da * gv * sg).astype(BF16)
        dg_ref[...] = (da * u_ref[...].astype(F32) * (sg * (1.0 + gv * (1.0 - sg)))).astype(BF16)

    o = pl.BlockSpec((tm, tn), lambda j, i: (i, j))
    return _pallas(
        body, name=name, grid=(f // tn, s // tm),
        in_specs=[pl.BlockSpec((tm, d), lambda j, i: (i, 0)), pl.BlockSpec((tn, d), lambda j, i: (j, 0)), o, o],
        out_specs=[o, o],
        out_shape=[jax.ShapeDtypeStruct((s, f), BF16), jax.ShapeDtypeStruct((s, f), BF16)],
        semantics=("parallel", "parallel"), args=(dx, wd, g, u), exchange=exchange)


def _row_slabs(full):
    return full.reshape(N_DEV, full.shape[0] // N_DEV, full.shape[1])


def _rows_of(slabs):
    return slabs.reshape(N_DEV * slabs.shape[1], slabs.shape[2])


def _ffn_forward(x, norm_w, wg, wu, wd, tag, gather=(), head=()):
    if head:
        n, r, first = _rms_fwd(x, norm_w, f"{tag}_norm", _Exchange("gather", head))
    else:
        (n, r), first = _rms_fwd(x, norm_w, f"{tag}_norm"), []
    if wg is None:
        wg, wu, first = _rows_of(first[0]), _rows_of(first[1]), first[2:]
    (g, u, a), got = _ffn_up(n, wg, wu, f"{tag}_up", _Exchange("gather", gather) if gather else None)
    if wd is None:
        wd, got = _rows_of(got[0]), got[1:]
    y = _matmul([(a, wd)], tm=512, tn=1024, tk=wd.shape[0], name=f"{tag}_down", res=x, alpha=0.5)
    return y, (n, r, g, u, a), (wg, wu, wd), got, first


def _ffn_backward(dy, x, norm_w, wgt, wut, wd, saved, tag, dw_dtype=F32, scatter=None):
    n, r, g, u, a = saved

    def behind(arrays):
        return _Exchange("scatter", arrays) if scatter is not None else None

    def dw(act, grad, name, alpha=None, exchange=None):
        return _matmul([(act, grad)], ta=True, tm=1408, tn=1024, tk=2048, name=name, alpha=alpha, out_dtype=dw_dtype,
                       exchange=exchange)

    dwd = _row_slabs(dw(a, dy, f"{tag}_dwd", alpha=0.5))
    (dg, du), extras = _ffn_dact(dy, wd, g, u, f"{tag}_dact", behind(scatter))
    if scatter is None:
        dwg, dwu = _row_slabs(dw(dg, n, f"{tag}_dwg")), _row_slabs(dw(du, n, f"{tag}_dwu"))
    else:
        dwg, (dwd,) = dw(dg, n, f"{tag}_dwg", exchange=behind([dwd]))
        dwu, (dwg,) = dw(du, n, f"{tag}_dwu", exchange=behind([_row_slabs(dwg)]))
        dwu = _row_slabs(dwu)
    dn = _matmul([(dg, wgt), (du, wut)], tm=512, tn=1024, tk=wgt.shape[0], name=f"{tag}_dn", exchange=behind([dwu]))
    if scatter is not None:
        dn, (dwu,) = dn
    dx, dnorm = _rms_bwd(dn, x, r, norm_w, dy, f"{tag}_dnorm")
    return dx, dnorm, dwg, dwu, dwd, extras


Q_SCALE = GDN_HEAD_DIM ** -0.5
CONV_HALO = 8


def _lane_block(s):
    return pl.BlockSpec((None, s, LANE), lambda j: (j, 0, 0))


def _conv_taps(win, w_ref, rows, sign):
    n = rows + 2 * CONV_HALO
    acc = None
    for t in range(CONV_WIDTH):
        o = sign * (t - CONV_WIDTH // 2)
        sh = win if o == 0 else pltpu.roll(win, (-o) % n, 0)
        term = sh[CONV_HALO:CONV_HALO + rows] * w_ref[t:t + 1, :]
        acc = term if acc is None else acc + term
    return acc


def _gdn_conv_fwd(p_pad, conv_wt):
    s = p_pad.shape[0]
    rows = min(CONV_ROWS, s)
    nblk = QKV_A // LANE

    def body(p_ref, w_ref, c_ref, y_ref, pad):
        j = pl.program_id(0)
        zeros = jnp.zeros((CONV_HALO, LANE), F32)
        pad[0:CONV_HALO, :] = zeros
        pad[CONV_HALO + s:2 * CONV_HALO + s, :] = zeros
        pad[CONV_HALO:CONV_HALO + s, :] = p_ref[...]

        def chunk(ci, carry):
            b = pl.multiple_of(ci * rows, rows)
            win = pad[pl.ds(b, rows + 2 * CONV_HALO), :]
            c = _conv_taps(win, w_ref, rows, 1)
            c_ref[pl.ds(b, rows), :] = c
            act = c * _sigmoid(c)
            nrm = lax.rsqrt(jnp.sum(act * act, axis=-1, keepdims=True) + EPS)
            mult = jnp.where(j < GDN_HEADS, nrm * Q_SCALE, jnp.where(j < 2 * GDN_HEADS, nrm, 1.0))
            y_ref[pl.ds(b, rows), :] = act * mult
            return carry

        lax.fori_loop(0, s // rows, chunk, 0)

    col = pl.BlockSpec((s, LANE), lambda j: (0, j))
    return pl.pallas_call(
        body, name="gdn_conv_fwd", grid=(nblk,),
        in_specs=[col, pl.BlockSpec((8, LANE), lambda j: (0, j))],
        out_specs=[_lane_block(s), _lane_block(s)],
        out_shape=[jax.ShapeDtypeStruct((nblk, s, LANE), F32), jax.ShapeDtypeStruct((nblk, s, LANE), F32)],
        scratch_shapes=[pltpu.VMEM((s + 2 * CONV_HALO, LANE), F32)],
        compiler_params=_params("parallel"),
    )(p_pad, conv_wt)


def _gdn_conv_bwd(dy_f, dy_r, c_pre, p_pad, conv_wt, dp_all):
    s = p_pad.shape[0]
    rows = min(CONV_ROWS, s)
    nblk = QKV_A // LANE

    def body(dyf_ref, dyr_ref, c_ref, p_ref, w_ref, _, dp_ref, dw_ref, ppad, dcpad):
        j = pl.program_id(0)
        zeros = jnp.zeros((CONV_HALO, LANE), F32)
        for buf in (ppad, dcpad):
            buf[0:CONV_HALO, :] = zeros
            buf[CONV_HALO + s:2 * CONV_HALO + s, :] = zeros
        ppad[CONV_HALO:CONV_HALO + s, :] = p_ref[...]

        def act_bwd(ci, carry):
            b = pl.multiple_of(ci * rows, rows)
            c = c_ref[pl.ds(b, rows), :]
            g = dyf_ref[pl.ds(b, rows), :] + dyr_ref[pl.ds(b, rows), :]
            sg = _sigmoid(c)
            act = c * sg
            nrm = lax.rsqrt(jnp.sum(act * act, axis=-1, keepdims=True) + EPS)
            yh = act * nrm
            scale = jnp.where(j < GDN_HEADS, Q_SCALE, 1.0)
            dact_qk = (scale * nrm) * (g - yh * jnp.sum(g * yh, axis=-1, keepdims=True))
            dact = jnp.where(j < 2 * GDN_HEADS, dact_qk, g)
            dcpad[pl.ds(pl.multiple_of(b + CONV_HALO, CONV_HALO), rows), :] = dact * (sg * (1.0 + c * (1.0 - sg)))
            return carry

        lax.fori_loop(0, s // rows, act_bwd, 0)
        tap = lax.broadcasted_iota(jnp.int32, (8, LANE), 0)

        def taps_bwd(ci, dw):
            b = pl.multiple_of(ci * rows, rows)
            dcw = dcpad[pl.ds(b, rows + 2 * CONV_HALO), :]
            dp_ref[pl.ds(b, rows), :] = _conv_taps(dcw, w_ref, rows, -1).astype(BF16)
            pw = ppad[pl.ds(b, rows + 2 * CONV_HALO), :]
            dc = dcw[CONV_HALO:CONV_HALO + rows]
            n = rows + 2 * CONV_HALO
            for t in range(CONV_WIDTH):
                o = t - CONV_WIDTH // 2
                sh = pw if o == 0 else pltpu.roll(pw, (-o) % n, 0)
                row = jnp.sum(dc * sh[CONV_HALO:CONV_HALO + rows], axis=0, keepdims=True)
                dw = dw + jnp.where(tap == t, row, 0.0)
            return dw

        dw_ref[...] = lax.fori_loop(0, s // rows, taps_bwd, jnp.zeros((8, LANE), F32))

    col = pl.BlockSpec((s, LANE), lambda j: (0, j))
    wspec = pl.BlockSpec((8, LANE), lambda j: (0, j))
    return pl.pallas_call(
        body, name="gdn_conv_bwd", grid=(nblk,),
        in_specs=[_lane_block(s), _lane_block(s), _lane_block(s), col, wspec, ANY],
        out_specs=[col, wspec],
        out_shape=[jax.ShapeDtypeStruct(dp_all.shape, dp_all.dtype), jax.ShapeDtypeStruct((8, QKV_A), F32)],
        scratch_shapes=[pltpu.VMEM((s + 2 * CONV_HALO, LANE), F32), pltpu.VMEM((s + 2 * CONV_HALO, LANE), F32)],
        input_output_aliases={5: 0},
        compiler_params=_params("parallel"),
    )(dy_f, dy_r, c_pre, p_pad, conv_wt, dp_all)


def _softplus(x):
    return jnp.maximum(x, 0.0) + jnp.log(1.0 + jnp.exp(-jnp.abs(x)))


def _gdn_gates_fwd(p_pad, alog_row, dt_row):
    s = p_pad.shape[0]
    tm = min(1024, s)

    def body(p_ref, al_ref, dt_ref, o_ref):
        x = p_ref[...]
        lane = lax.broadcasted_iota(jnp.int32, x.shape, 1)
        g = -jnp.exp(al_ref[...]) * _softplus(x + dt_ref[...])
        o_ref[...] = jnp.where(lane < 8, g, jnp.where(lane < 16, _sigmoid(x), 0.0))

    vec = pl.BlockSpec((1, LANE), lambda i: (0, 0))
    return pl.pallas_call(
        body, name="gdn_gates_fwd", grid=(s // tm,),
        in_specs=[pl.BlockSpec((tm, LANE), lambda i: (i, OFF_AB // LANE)), vec, vec],
        out_specs=pl.BlockSpec((tm, LANE), lambda i: (i, 0)),
        out_shape=jax.ShapeDtypeStruct((s, LANE), F32),
        compiler_params=_params("parallel"),
    )(p_pad, alog_row, dt_row)


def _gdn_gates_bwd(dgb_f, dgb_r, p_pad, gb, alog_row, dt_row, dp_all):
    s = p_pad.shape[0]
    tm = min(1024, s)
    tail = N_PAD - OFF_AB

    def body(df_ref, dr_ref, p_ref, gb_ref, al_ref, dt_ref, _, dp_ref, sum_ref):
        @pl.when(pl.program_id(0) == 0)
        def _():
            sum_ref[...] = jnp.zeros_like(sum_ref)

        x = p_ref[...]
        gbv = gb_ref[...]
        dgb = df_ref[...] + dr_ref[...]
        lane = lax.broadcasted_iota(jnp.int32, x.shape, 1)
        da = dgb * (-jnp.exp(al_ref[...])) * _sigmoid(x + dt_ref[...])
        db = dgb * gbv * (1.0 - gbv)
        dp_ref[:, 0:LANE] = jnp.where(lane < 8, da, jnp.where(lane < 16, db, 0.0)).astype(BF16)
        dp_ref[:, LANE:tail] = jnp.zeros((tm, tail - LANE), BF16)
        row = lax.broadcasted_iota(jnp.int32, (8, LANE), 0)
        lane8 = lax.broadcasted_iota(jnp.int32, (8, LANE), 1)
        d_alog = jnp.sum(dgb * gbv, axis=0, keepdims=True)
        d_dt = jnp.sum(da, axis=0, keepdims=True)
        upd = jnp.where(row == 0, d_alog, jnp.where(row == 1, d_dt, 0.0))
        sum_ref[...] += jnp.where(lane8 < 8, upd, 0.0)

    vec = pl.BlockSpec((1, LANE), lambda i: (0, 0))
    blk = pl.BlockSpec((tm, LANE), lambda i: (i, 0))
    return pl.pallas_call(
        body, name="gdn_gates_bwd", grid=(s // tm,),
        in_specs=[blk, blk, pl.BlockSpec((tm, LANE), lambda i: (i, OFF_AB // LANE)), blk, vec, vec, ANY],
        out_specs=[pl.BlockSpec((tm, tail), lambda i: (i, OFF_AB // tail)), pl.BlockSpec((8, LANE), lambda i: (0, 0))],
        out_shape=[jax.ShapeDtypeStruct(dp_all.shape, dp_all.dtype), jax.ShapeDtypeStruct((8, LANE), F32)],
        input_output_aliases={6: 0},
        compiler_params=_params("arbitrary"),
    )(dgb_f, dgb_r, p_pad, gb, alog_row, dt_row, dp_all)


def _chunk_masks(rev):
    row = lax.broadcasted_iota(jnp.int32, (CHUNK, CHUNK), 0)
    col = lax.broadcasted_iota(jnp.int32, (CHUNK, CHUNK), 1)
    le = (col >= row) if rev else (col <= row)
    strict = (col > row) if rev else (col < row)
    return le, strict, row == col


def _gate_lanes(rev, h):
    d = 1 if rev else 0
    return d * GDN_HEADS + h, 8 + d * GDN_HEADS + h


BNN = (((2,), (1,)), ((0,), (0,)))
BNT = (((2,), (2,)), ((0,), (0,)))
BTN = (((1,), (1,)), ((0,), (0,)))
NB = 2 * GDN_HEADS
DELTA_CHUNKS = 8


def _bdot(a, b, dn=BNN):
    return lax.dot_general(a.astype(BF16), b.astype(BF16), dn, preferred_element_type=F32)


def _dot3(a, b, dn, exact_a=False, exact_b=False):
    def d(x, y):
        return lax.dot_general(x, y, dn, preferred_element_type=F32)

    ah = a.astype(BF16)
    bh = b.astype(BF16)
    out = d(ah, bh)
    if not exact_b:
        out = out + d(ah, (b - bh.astype(F32)).astype(BF16))
    if not exact_a:
        out = out + d((a - ah.astype(F32)).astype(BF16), bh)
    return out


def _both(f_val, r_val):
    return jnp.stack([f_val] * GDN_HEADS + [r_val] * GDN_HEADS)


def _head_blocks(ref_f, ref_r, rows_f, rows_r):
    return jnp.concatenate([ref_f[:, rows_f, :], ref_r[:, rows_r, :]], axis=0)


def _chunk_rows(c):
    return slice(c * CHUNK, (c + 1) * CHUNK), slice((DELTA_CHUNKS - 1 - c) * CHUNK, (DELTA_CHUNKS - c) * CHUNK)


def _heads(ref_f, ref_r, rows_f, rows_r):
    hd = GDN_HEAD_DIM
    return jnp.stack([ref_f[rows_f, h * hd:(h + 1) * hd] for h in range(GDN_HEADS)]
                     + [ref_r[rows_r, h * hd:(h + 1) * hd] for h in range(GDN_HEADS)])


def _gate_cols(tile_f, tile_r, base):
    return jnp.stack([tile_f[:, base + h:base + h + 1] for h in range(GDN_HEADS)]
                     + [tile_r[:, base + GDN_HEADS + h:base + GDN_HEADS + h + 1] for h in range(GDN_HEADS)])


def _chunk_common2(q, k, v, gbf, gbr):
    mf, mr = _chunk_masks(False), _chunk_masks(True)
    le, strict = _both(mf[0], mr[0]), _both(mf[1], mr[1])
    eye = mf[2]
    gcm_f = _dot3(mf[0].astype(F32), gbf, NN, exact_a=True)
    gcm_r = _dot3(mr[0].astype(F32), gbr, NN, exact_a=True)
    g, beta, gc = _gate_cols(gbf, gbr, 0), _gate_cols(gbf, gbr, 8), _gate_cols(gcm_f, gcm_r, 0)
    gc_row = _dot3(jnp.ones((NB, CHUNK, CHUNK), F32), jnp.where(eye[None], gc, 0.0), BNN, exact_a=True)
    decay = jnp.where(le, jnp.exp(jnp.where(le, gc - gc_row, 0.0)), 0.0)
    eg = jnp.exp(gc)
    gl = jnp.sum(g, axis=1, keepdims=True)
    kb = k * beta
    vb = v * beta
    kbeg = kb * eg
    lm = jnp.where(strict, _bdot(kb, k, BNT) * decay, 0.0)
    intra = _bdot(q, k, BNT) * decay
    edec = jnp.exp(gl - gc)
    return dict(strict=strict, eye=eye, beta=beta, decay=decay, eg=eg, gl=gl, kb=kb, vb=vb, kbeg=kbeg,
                lm=lm, intra=intra, qg=q * eg, edec=edec, kdec=k * edec)


def _unit_triangular_inverse(lm, eye):
    x = -lm
    t = eye[None].astype(F32) + x
    p = x
    for level in range(5):
        prod = functools.partial(_dot3, dn=BNN) if level < 2 else _bdot
        p = prod(p, p)
        t = t + prod(t, p)
    return t


def _delta_fwd2(y, gb, gather=()):
    s = y.shape[1]
    nc = s // CHUNK
    hd = GDN_HEAD_DIM
    na = len(gather)

    def body(*refs):
        qf, kf, vf, gf, qr, kr, vr, gr = refs[:8]
        of_ref, or_ref, sf_all, sr_all, tf_all, tr_all = refs[8 + na:14 + na]
        state = refs[14 + 2 * na]
        step = pl.program_id(0)

        @pl.when(step == 0)
        def _():
            state[...] = jnp.zeros_like(state)

        if na:
            start, forward, finish = _gather_phases(refs[8:8 + na], refs[14 + na:14 + 2 * na], *refs[15 + 2 * na:])
            pl.when(step == 0)(start)
            pl.when(step == ns // 2)(forward)
            pl.when(step == ns - 1)(finish)

        st = state[...]
        for c in range(DELTA_CHUNKS):
            rf, rr = _chunk_rows(c)
            q, k, v = _head_blocks(qf, qr, rf, rr), _head_blocks(kf, kr, rf, rr), _head_blocks(vf, vr, rf, rr)
            cm = _chunk_common2(q, k, v, gf[rf, :], gr[rr, :])
            tinv = _unit_triangular_inverse(cm["lm"], cm["eye"])
            u = _bdot(tinv, cm["vb"])
            w = _bdot(tinv, cm["kbeg"])
            v_new = u - _bdot(w, st)
            o = _bdot(cm["qg"], st) + _bdot(cm["intra"], v_new)
            for h in range(GDN_HEADS):
                of_ref[rf, h * hd:(h + 1) * hd] = o[h]
                or_ref[rr, h * hd:(h + 1) * hd] = o[GDN_HEADS + h]
            sf_all[c] = st[:GDN_HEADS]
            sr_all[DELTA_CHUNKS - 1 - c] = st[GDN_HEADS:]
            tf_all[c] = tinv[:GDN_HEADS]
            tr_all[DELTA_CHUNKS - 1 - c] = tinv[GDN_HEADS:]
            st = st * jnp.exp(cm["gl"]) + _bdot(cm["kdec"], v_new, BTN)
        state[...] = st

    rows = DELTA_CHUNKS * CHUNK
    ns = nc // DELTA_CHUNKS

    def col(j, rev):
        return pl.BlockSpec((GDN_HEADS, rows, hd), (lambda n: (j, ns - 1 - n, 0)) if rev else (lambda n: (j, n, 0)))

    def out(rev):
        return pl.BlockSpec((rows, GDN_WIDTH), (lambda n: (ns - 1 - n, 0)) if rev else (lambda n: (n, 0)))

    def gate(rev):
        return pl.BlockSpec((rows, LANE), (lambda n: (ns - 1 - n, 0)) if rev else (lambda n: (n, 0)))

    def per_chunk(d1, d2, rev):
        return pl.BlockSpec((DELTA_CHUNKS, GDN_HEADS, d1, d2),
                            (lambda n: (ns - 1 - n, 0, 0, 0)) if rev else (lambda n: (n, 0, 0, 0)))

    assert nc % DELTA_CHUNKS == 0 and (na == 0 or ns >= 4)
    res = pl.pallas_call(
        body, name="delta_fwd", grid=(ns,),
        in_specs=[col(0, False), col(1, False), col(2, False), gate(False), col(0, True), col(1, True), col(2, True), gate(True)]
        + [ANY] * na,
        out_specs=[out(False), out(True), per_chunk(hd, hd, False), per_chunk(hd, hd, True),
                   per_chunk(CHUNK, CHUNK, False), per_chunk(CHUNK, CHUNK, True)] + [ANY] * na,
        out_shape=[jax.ShapeDtypeStruct((s, GDN_WIDTH), F32)] * 2 + [jax.ShapeDtypeStruct((nc, GDN_HEADS, hd, hd), F32)] * 2
        + [jax.ShapeDtypeStruct((nc, GDN_HEADS, CHUNK, CHUNK), F32)] * 2
        + [jax.ShapeDtypeStruct((N_DEV,) + v.shape, v.dtype) for v in gather],
        scratch_shapes=[pltpu.VMEM((NB, hd, hd), F32)] + (_gather_semaphores(na) if na else []),
        compiler_params=_params("arbitrary"),
    )(y, y, y, gb, y, y, y, gb, *gather)
    return res[:6], res[6:]


def _delta_bwd2(y, gb, do, sf_all, sr_all, tf_all, tr_all, scatter=()):
    s = y.shape[1]
    nc = s // CHUNK
    hd = GDN_HEAD_DIM
    na = len(scatter)

    def body(*refs):
        qf, kf, vf, gf, dof, sf, tf, qr, kr, vr, gr, dor, sr, tr = refs[:14]
        dyf_ref, dyr_ref, dgf_ref, dgr_ref = refs[14 + na:18 + na]
        dstate = refs[18 + 2 * na]
        step = pl.program_id(0)

        @pl.when(step == 0)
        def _():
            dstate[...] = jnp.zeros_like(dstate)

        if na:
            start, finish = _scatter_phases(refs[14:14 + na], refs[18 + na:18 + 2 * na], *refs[19 + 2 * na:])
            pl.when(step == 0)(start)
            pl.when(step == ns - 1)(finish)

        def one_chunk(c, ds_out):
            rr, rf = _chunk_rows(c)
            cf, cr = DELTA_CHUNKS - 1 - c, c
            q, k, v = _head_blocks(qf, qr, rf, rr), _head_blocks(kf, kr, rf, rr), _head_blocks(vf, vr, rf, rr)
            dov = _heads(dof, dor, rf, rr)
            cm = _chunk_common2(q, k, v, gf[rf, :], gr[rr, :])
            tinv = jnp.concatenate([tf[cf], tr[cr]], axis=0)
            st = jnp.concatenate([sf[cf], sr[cr]], axis=0)
            decay, lm, intra, qg, kdec, kbeg, eg, kb, beta = (
                cm[n] for n in ("decay", "lm", "intra", "qg", "kdec", "kbeg", "eg", "kb", "beta"))
            u = _bdot(tinv, cm["vb"])
            w = _bdot(tinv, kbeg)
            v_new = u - _bdot(w, st)
            egl = jnp.exp(cm["gl"])
            d_qg = _bdot(dov, st, BNT)
            d_intra = _bdot(dov, v_new, BNT)
            dv_new = _bdot(intra, dov, BTN) + _bdot(kdec, ds_out)
            d_kdec = _bdot(v_new, ds_out, BNT)
            ds_in = _bdot(qg, dov, BTN) + egl * ds_out - _bdot(w, dv_new, BTN)
            dgl = egl * jnp.sum(jnp.sum(st * ds_out, axis=2, keepdims=True), axis=1, keepdims=True)
            dw = -_bdot(dv_new, st, BNT)
            dvb = _bdot(tinv, dv_new, BTN)
            dkbeg = _bdot(tinv, dw, BTN)
            dlm = jnp.where(cm["strict"], -(_bdot(dvb, u, BNT) + _bdot(dkbeg, w, BNT)), 0.0)
            d_a = dlm * decay
            d_qk = d_intra * decay
            e = dlm * lm + d_intra * intra
            colsum = _dot3(e, jnp.ones((NB, CHUNK, LANE), F32), BTN, exact_b=True)[:, :, 0:1]
            dgc = jnp.sum(e, axis=2, keepdims=True) - colsum
            dkb = _bdot(d_a, k) + dkbeg * eg
            dk = _bdot(d_a, kb, BTN) + _bdot(d_qk, q, BTN)
            dq = _bdot(d_qk, k) + d_qg * eg
            dgc = dgc + jnp.sum(d_qg * qg, axis=2, keepdims=True) + jnp.sum(dkbeg * kbeg, axis=2, keepdims=True)
            tdec = jnp.sum(d_kdec * kdec, axis=2, keepdims=True)
            dk = dk + d_kdec * cm["edec"] + dkb * beta
            dgc = dgc - tdec
            dgl = dgl + jnp.sum(tdec, axis=1, keepdims=True)
            dbeta = jnp.sum(dvb * v, axis=2, keepdims=True) + jnp.sum(dkb * k, axis=2, keepdims=True)
            dv = dvb * beta
            lane = lax.broadcasted_iota(jnp.int32, (CHUNK, LANE), 1)
            for rev, dy_ref, dg_ref, rows in ((False, dyf_ref, dgf_ref, rf), (True, dyr_ref, dgr_ref, rr)):
                dgc_tile = jnp.zeros((CHUNK, LANE), F32)
                rest = jnp.zeros((CHUNK, LANE), F32)
                for h in range(GDN_HEADS):
                    b = (GDN_HEADS if rev else 0) + h
                    gi, bi = _gate_lanes(rev, h)
                    dgc_tile = dgc_tile + jnp.where(lane == gi, dgc[b], 0.0)
                    rest = rest + jnp.where(lane == gi, dgl[b], 0.0) + jnp.where(lane == bi, dbeta[b], 0.0)
                    dy_ref[h, rows, :] = dq[b]
                    dy_ref[GDN_HEADS + h, rows, :] = dk[b]
                    dy_ref[2 * GDN_HEADS + h, rows, :] = dv[b]
                le_t = _chunk_masks(not rev)[0].astype(F32)
                dg_ref[rows, :] = _dot3(le_t, dgc_tile, NN, exact_a=True) + rest
            return ds_in

        ds = dstate[...]
        for c in range(DELTA_CHUNKS):
            ds = one_chunk(c, ds)
        dstate[...] = ds

    rows_per_step = DELTA_CHUNKS * CHUNK
    ns = nc // DELTA_CHUNKS

    def col(j, rev, blocks=GDN_HEADS):
        return pl.BlockSpec((blocks, rows_per_step, hd), (lambda n: (j, n, 0)) if rev else (lambda n: (j, ns - 1 - n, 0)))

    def wide(width, rev):
        return pl.BlockSpec((rows_per_step, width), (lambda n: (n, 0)) if rev else (lambda n: (ns - 1 - n, 0)))

    def per_chunk(d1, d2, rev):
        return pl.BlockSpec((DELTA_CHUNKS, GDN_HEADS, d1, d2),
                            (lambda n: (n, 0, 0, 0)) if rev else (lambda n: (ns - 1 - n, 0, 0, 0)))

    def side(rev):
        return [col(0, rev), col(1, rev), col(2, rev), wide(LANE, rev), wide(GDN_WIDTH, rev), per_chunk(hd, hd, rev),
                per_chunk(CHUNK, CHUNK, rev)]

    assert nc % DELTA_CHUNKS == 0 and (na == 0 or ns >= 2)
    res = pl.pallas_call(
        body, name="delta_bwd", grid=(ns,),
        in_specs=side(False) + side(True) + [ANY] * na,
        out_specs=[col(0, False, 3 * GDN_HEADS), col(0, True, 3 * GDN_HEADS), wide(LANE, False), wide(LANE, True)]
        + [ANY] * na,
        out_shape=[jax.ShapeDtypeStruct((3 * GDN_HEADS, s, hd), F32)] * 2 + [jax.ShapeDtypeStruct((s, LANE), F32)] * 2
        + [jax.ShapeDtypeStruct(g.shape, g.dtype) for g in scatter],
        scratch_shapes=[pltpu.VMEM((NB, hd, hd), F32)] + (_gather_semaphores(na) if na else []),
        compiler_params=_params("arbitrary"),
    )(y, y, y, gb, do, sf_all, tf_all, y, y, y, gb, do, sr_all, tr_all, *scatter)
    return res[:4], res[4:]


def _gdn_post_fwd(o_f, o_r, p_pad, norm_row):
    s = o_f.shape[0]
    tm = min(512, s)
    hd = GDN_HEAD_DIM

    def body(of_ref, or_ref, z_ref, w_ref, out_ref, osum_ref):
        o = of_ref[...] + or_ref[...]
        osum_ref[...] = o
        z = z_ref[...]
        gate = z * _sigmoid(z)
        for h in range(GDN_HEADS):
            sl = slice(h * hd, (h + 1) * hd)
            oh = o[:, sl]
            r = lax.rsqrt(jnp.mean(oh * oh, axis=-1, keepdims=True) + EPS)
            out_ref[:, sl] = (oh * r * w_ref[...] * gate[:, sl]).astype(BF16)

    blk = pl.BlockSpec((tm, GDN_WIDTH), lambda i: (i, 0))
    return pl.pallas_call(
        body, name="gdn_post_fwd", grid=(s // tm,),
        in_specs=[blk, blk, pl.BlockSpec((tm, GDN_WIDTH), lambda i: (i, OFF_Z // GDN_WIDTH)),
                  pl.BlockSpec((1, hd), lambda i: (0, 0))],
        out_specs=[blk, blk],
        out_shape=[jax.ShapeDtypeStruct((s, GDN_WIDTH), BF16), jax.ShapeDtypeStruct((s, GDN_WIDTH), F32)],
        compiler_params=_params("parallel"),
    )(o_f, o_r, p_pad, norm_row)


def _gdn_post_bwd(d_out, o_sum, p_pad, norm_row):
    s = o_sum.shape[0]
    tm = min(512, s)
    hd = GDN_HEAD_DIM

    def body(d_ref, o_ref, z_ref, w_ref, do_ref, dz_ref, dw_ref):
        @pl.when(pl.program_id(0) == 0)
        def _():
            dw_ref[...] = jnp.zeros_like(dw_ref)

        z = z_ref[...]
        sg = _sigmoid(z)
        gate = z * sg
        dgate = sg * (1.0 + z * (1.0 - sg))
        wv = w_ref[...]
        dw = jnp.zeros((1, hd), F32)
        for h in range(GDN_HEADS):
            sl = slice(h * hd, (h + 1) * hd)
            oh = o_ref[:, sl]
            dh = d_ref[:, sl]
            r = lax.rsqrt(jnp.mean(oh * oh, axis=-1, keepdims=True) + EPS)
            ohat = oh * r
            dz_ref[:, sl] = (dh * ohat * wv * dgate[:, sl]).astype(BF16)
            drn = dh * gate[:, sl]
            t = drn * wv
            do_ref[:, sl] = r * (t - ohat * jnp.mean(t * ohat, axis=-1, keepdims=True))
            dw = dw + jnp.sum(drn * ohat, axis=0, keepdims=True)
        dw_ref[...] += dw

    blk = pl.BlockSpec((tm, GDN_WIDTH), lambda i: (i, 0))
    vec = pl.BlockSpec((1, hd), lambda i: (0, 0))
    return pl.pallas_call(
        body, name="gdn_post_bwd", grid=(s // tm,),
        in_specs=[blk, blk, pl.BlockSpec((tm, GDN_WIDTH), lambda i: (i, OFF_Z // GDN_WIDTH)), vec],
        out_specs=[blk, pl.BlockSpec((tm, GDN_WIDTH), lambda i: (i, OFF_Z // GDN_WIDTH)), vec],
        out_shape=[jax.ShapeDtypeStruct((s, GDN_WIDTH), F32), jax.ShapeDtypeStruct((s, N_PAD), BF16),
                   jax.ShapeDtypeStruct((1, hd), F32)],
        compiler_params=_params("arbitrary"),
    )(d_out, o_sum, p_pad, norm_row)


def _gdn_forward(p_pad, conv_wt, alog_row, dt_row, norm_row, gather=()):
    c_pre, y = _gdn_conv_fwd(p_pad, conv_wt)
    gb = _gdn_gates_fwd(p_pad, alog_row, dt_row)
    (o_f, o_r, s_f, s_r, t_f, t_r), gathered = _delta_fwd2(y, gb, gather)
    out, o_sum = _gdn_post_fwd(o_f, o_r, p_pad, norm_row)
    return out, (c_pre, y, gb, s_f, t_f, s_r, t_r, o_sum), gathered


def _gdn_backward(d_out, p_pad, conv_wt, alog_row, dt_row, norm_row, saved, scatter=()):
    c_pre, y, gb, s_f, t_f, s_r, t_r, o_sum = saved
    do, dp_all, dnorm = _gdn_post_bwd(d_out, o_sum, p_pad, norm_row)
    (dy_f, dy_r, dgb_f, dgb_r), received = _delta_bwd2(y, gb, do, s_f, s_r, t_f, t_r, scatter)
    dp_all, dconv = _gdn_conv_bwd(dy_f, dy_r, c_pre, p_pad, conv_wt, dp_all)
    dp_all, gate_sums = _gdn_gates_bwd(dgb_f, dgb_r, p_pad, gb, alog_row, dt_row, dp_all)
    return dp_all, dconv, gate_sums, dnorm, received


ATT_BK = ATT_BQ + 2 * ATT_HALO
ATT_SUB = 8
SWA_SCALE = SWA_HEAD_DIM ** -0.5


def _t5_bucket(rel):
    nb = REL_BUCKETS // 2
    bucket = (rel > 0).astype(np.int32) * nb
    n = np.abs(rel)
    max_exact = nb // 2
    large = max_exact + (np.log(np.maximum(n, 1) / max_exact)
                         / math.log(REL_MAX_DISTANCE / max_exact) * (nb - max_exact)).astype(np.int32)
    large = np.minimum(large, nb - 1)
    return (bucket + np.where(n < max_exact, n, large)).astype(np.int32)


def _band_tables(dilation, queries_are_rows_of_block):
    blk = np.arange(ATT_BQ)
    band = np.arange(ATT_BK) - ATT_HALO
    if queries_are_rows_of_block:
        rel = band[None, :] - blk[:, None]
        band_idx = np.broadcast_to(np.arange(ATT_BK)[None, :], rel.shape)
    else:
        rel = blk[None, :] - band[:, None]
        band_idx = np.broadcast_to(np.arange(ATT_BK)[:, None], rel.shape)
    base = np.abs(rel) <= ATT_HALO
    not_prev = band_idx >= ATT_HALO
    not_next = band_idx < ATT_HALO + ATT_BQ
    valid = np.stack([base & not_prev, base, base & not_next, base & not_prev & not_next])
    return valid, _t5_bucket(rel * dilation)


def _bias_tiles(rel_bias, dilation, queries_are_rows_of_block):
    valid, bucket = _band_tables(dilation, queries_are_rows_of_block)
    onehot = (jnp.asarray(bucket.reshape(-1, 1)) == jnp.arange(REL_BUCKETS, dtype=jnp.int32)[None, :]).astype(F32)
    rb = jnp.dot(onehot, rel_bias.astype(F32), precision=lax.Precision.HIGHEST)
    rb = rb.T.reshape((SWA_HEADS,) + bucket.shape)
    return jnp.where(valid[:, None], rb[None], NEG_BIG).astype(F32)


def _group_sum(x, bd):
    hi = x.astype(BF16)
    lo = (x - hi.astype(F32)).astype(BF16)
    return jnp.dot(hi, bd, preferred_element_type=F32) + jnp.dot(lo, bd, preferred_element_type=F32)


def _head_block_diag():
    idx = np.arange(SWA_WIDTH) // SWA_HEAD_DIM
    return jnp.asarray(idx[:, None] == idx[None, :], BF16)


def _swa_pre_fwd(p_pad, qw_row, kw_row, bd):
    s = p_pad.shape[0]
    tm = min(512, s)
    inv = 1.0 / SWA_HEAD_DIM

    def body(q_ref, k_ref, v_ref, qw_ref, kw_ref, bd_ref, qo_ref, ko_ref, vo_ref):
        bdv = bd_ref[...]
        q = q_ref[...]
        k = k_ref[...]
        rq = lax.rsqrt(_group_sum(q * q, bdv) * inv + EPS)
        rk = lax.rsqrt(_group_sum(k * k, bdv) * inv + EPS)
        qo_ref[...] = (q * rq * qw_ref[...] * SWA_SCALE).astype(BF16)
        ko_ref[...] = (k * rk * kw_ref[...]).astype(BF16)
        vo_ref[...] = v_ref[...].astype(BF16)

    base = OFF_B // SWA_WIDTH
    blk = pl.BlockSpec((tm, SWA_WIDTH), lambda i: (i, 0))
    vec = pl.BlockSpec((1, SWA_WIDTH), lambda i: (0, 0))
    return pl.pallas_call(
        body, name="swa_pre_fwd", grid=(s // tm,),
        in_specs=[pl.BlockSpec((tm, SWA_WIDTH), lambda i: (i, base)), pl.BlockSpec((tm, SWA_WIDTH), lambda i: (i, base + 1)),
                  pl.BlockSpec((tm, SWA_WIDTH), lambda i: (i, base + 2)), vec, vec,
                  pl.BlockSpec((SWA_WIDTH, SWA_WIDTH), lambda i: (0, 0))],
        out_specs=[blk, blk, blk],
        out_shape=[jax.ShapeDtypeStruct((s, SWA_WIDTH), BF16)] * 3,
        compiler_params=_params("parallel"),
    )(p_pad, p_pad, p_pad, qw_row, kw_row, bd)


def _swa_pre_bwd(dqs, dks, dvs, p_pad, qw_row, kw_row, bd, dp_all):
    s = p_pad.shape[0]
    tm = min(256, s)
    inv = 1.0 / SWA_HEAD_DIM
    npat = len(dqs)

    def body(*refs):
        dq_refs, dk_refs, dv_refs = refs[:npat], refs[npat:2 * npat], refs[2 * npat:3 * npat]
        q_ref, k_ref, qw_ref, kw_ref, bd_ref, _, dp_ref, dqw_ref, dkw_ref = refs[3 * npat:]

        @pl.when(pl.program_id(0) == 0)
        def _():
            dqw_ref[...] = jnp.zeros_like(dqw_ref)
            dkw_ref[...] = jnp.zeros_like(dkw_ref)

        bdv = bd_ref[...]

        def norm_bwd(x, g, w, scale):
            r = lax.rsqrt(_group_sum(x * x, bdv) * inv + EPS)
            xhat = x * r
            t = g * w * scale
            dx = r * (t - xhat * (_group_sum(t * xhat, bdv) * inv))
            return dx, jnp.sum(g * scale * xhat, axis=0, keepdims=True)

        def total(rs):
            t = rs[0][...].astype(F32)
            for r in rs[1:]:
                t = t + r[...].astype(F32)
            return t

        dq, dqw = norm_bwd(q_ref[...], total(dq_refs), qw_ref[...], SWA_SCALE)
        dk, dkw = norm_bwd(k_ref[...], total(dk_refs), kw_ref[...], 1.0)
        dp_ref[:, 0:SWA_WIDTH] = dq.astype(BF16)
        dp_ref[:, SWA_WIDTH:2 * SWA_WIDTH] = dk.astype(BF16)
        dp_ref[:, 2 * SWA_WIDTH:3 * SWA_WIDTH] = total(dv_refs).astype(BF16)
        dqw_ref[...] += dqw
        dkw_ref[...] += dkw

    base = OFF_B // SWA_WIDTH
    blk = pl.BlockSpec((tm, SWA_WIDTH), lambda i: (i, 0))
    vec = pl.BlockSpec((1, SWA_WIDTH), lambda i: (0, 0))
    return pl.pallas_call(
        body, name="swa_pre_bwd", grid=(s // tm,),
        in_specs=[blk] * (3 * npat) + [pl.BlockSpec((tm, SWA_WIDTH), lambda i: (i, base)),
                                      pl.BlockSpec((tm, SWA_WIDTH), lambda i: (i, base + 1)), vec, vec,
                                      pl.BlockSpec((SWA_WIDTH, SWA_WIDTH), lambda i: (0, 0)), ANY],
        out_specs=[pl.BlockSpec((tm, 3 * SWA_WIDTH), lambda i: (i, OFF_B // (3 * SWA_WIDTH))), vec, vec],
        out_shape=[jax.ShapeDtypeStruct(dp_all.shape, dp_all.dtype), jax.ShapeDtypeStruct((1, SWA_WIDTH), F32),
                   jax.ShapeDtypeStruct((1, SWA_WIDTH), F32)],
        input_output_aliases={3 * npat + 5: 0},
        compiler_params=_params("arbitrary"),
    )(*dqs, *dks, *dvs, p_pad, p_pad, qw_row, kw_row, bd, dp_all)


def _band_specs(length, rows):
    per = rows // ATT_HALO
    last = length // ATT_HALO - 1
    prev = pl.BlockSpec((ATT_HALO, SWA_WIDTH), lambda r, t: (jnp.maximum(t * per - 1, 0), r))
    cur = pl.BlockSpec((rows, SWA_WIDTH), lambda r, t: (t, r))
    nxt = pl.BlockSpec((ATT_HALO, SWA_WIDTH), lambda r, t: (jnp.minimum((t + 1) * per, last), r))
    return [prev, cur, nxt]


def _tile_variant(t, nb, u, sub):
    first, last = u == 0, u == sub - 1
    if first and last:
        return 3 if nb == 1 else jnp.where(t == 0, 0, jnp.where(t == nb - 1, 2, 1))
    if first:
        return jnp.where(t == 0, 0, 1)
    if last:
        return jnp.where(t == nb - 1, 2, 1)
    return 1


def _bias_specs(nb, sub, rows, cols):
    return [pl.BlockSpec((1, SWA_HEADS, rows, cols),
                         functools.partial(lambda r, t, u: (_tile_variant(t, nb, u, sub), 0, 0, 0), u=u))
            for u in range(sub)]


def _band(refs):
    return jnp.concatenate([r[...] for r in refs], axis=0)


def _sub(u, width=ATT_BQ):
    return slice(u * ATT_BQ, u * ATT_BQ + width)


N_PAIRS = SWA_HEADS // 2


def _pairs(x):
    return jnp.stack([x[:, LANE * p:LANE * (p + 1)] for p in range(N_PAIRS)])


def _per_head_rows(x):
    first = lax.broadcasted_iota(jnp.int32, x.shape, 2) < SWA_HEAD_DIM
    zero = jnp.zeros_like(x)
    return jnp.concatenate([jnp.where(first, x, zero), jnp.where(first, zero, x)], axis=1)


def _per_head_cols(x):
    return jnp.stack([jnp.concatenate([x[:, LANE * p:LANE * p + 1],
                                       x[:, LANE * p + SWA_HEAD_DIM:LANE * p + SWA_HEAD_DIM + 1]], axis=0)
                      for p in range(N_PAIRS)])


def _merge_heads(x, rows):
    first = lax.broadcasted_iota(jnp.int32, (N_PAIRS, rows, LANE), 2) < SWA_HEAD_DIM
    return jnp.where(first, x[:, :rows], x[:, rows:])


def _store_pairs(ref, x, rows):
    for p in range(N_PAIRS):
        ref[rows, LANE * p:LANE * (p + 1)] = x[p].astype(ref.dtype)


def _att_fwd2(q, k, v, bias, dilation):
    s = q.shape[0]
    length = s // dilation
    sub = min(ATT_SUB, length // ATT_BQ)
    rows = sub * ATT_BQ
    nb = length // rows
    view = (length, dilation * SWA_WIDTH)

    def body(q_ref, kp, kc, kn, vp, vc, vn, *rest):
        b_refs, (o_ref, lse_ref) = rest[:sub], rest[sub:]
        kwin, vwin = _band((kp, kc, kn)), _band((vp, vc, vn))
        for u in range(sub):
            kb, vb = _pairs(kwin[_sub(u, ATT_BK)]), _pairs(vwin[_sub(u, ATT_BK)])
            qm = _per_head_rows(_pairs(q_ref[_sub(u), :]))
            sc = _bdot(qm, kb, BNT) + b_refs[u][0].reshape(N_PAIRS, 2 * ATT_BQ, ATT_BK)
            m = jnp.max(sc, axis=-1, keepdims=True)
            p = jnp.exp(sc - m)
            den = jnp.sum(p, axis=-1, keepdims=True)
            o = _bdot(p, vb) / den
            _store_pairs(o_ref, _merge_heads(o, ATT_BQ), _sub(u))
            lse = jnp.broadcast_to(m + jnp.log(den), (N_PAIRS, 2 * ATT_BQ, LANE))
            _store_pairs(lse_ref, _merge_heads(lse, ATT_BQ), _sub(u))

    cur = pl.BlockSpec((rows, SWA_WIDTH), lambda r, t: (t, r))
    o, lse = pl.pallas_call(
        body, name=f"att_fwd_d{dilation}", grid=(dilation, nb),
        in_specs=[cur] + _band_specs(length, rows) * 2 + _bias_specs(nb, sub,ATT_BQ, ATT_BK),
        out_specs=[cur, cur],
        out_shape=[jax.ShapeDtypeStruct(view, BF16), jax.ShapeDtypeStruct(view, F32)],
        compiler_params=_params("parallel", "parallel"),
    )(q.reshape(view), *([k.reshape(view)] * 3), *([v.reshape(view)] * 3), *([bias] * sub))
    return o.reshape(s, SWA_WIDTH), lse.reshape(s, SWA_WIDTH)


def _att_dq2(q, k, v, dop, lse, cp, bias, dilation):
    s = q.shape[0]
    length = s // dilation
    sub = min(ATT_SUB, length // ATT_BQ)
    rows = sub * ATT_BQ
    nb = length // rows
    view = (length, dilation * SWA_WIDTH)

    def body(q_ref, kp, kc, kn, vp, vc, vn, do_ref, lse_ref, cp_ref, *rest):
        b_refs, (dq_ref, db_ref) = rest[:sub], rest[sub:]

        @pl.when((pl.program_id(0) == 0) & (pl.program_id(1) == 0))
        def _():
            db_ref[...] = jnp.zeros_like(db_ref)

        kwin, vwin = _band((kp, kc, kn)), _band((vp, vc, vn))
        for u in range(sub):
            kb, vb = _pairs(kwin[_sub(u, ATT_BK)]), _pairs(vwin[_sub(u, ATT_BK)])
            qm = _per_head_rows(_pairs(q_ref[_sub(u), :]))
            dom = _per_head_rows(_pairs(do_ref[_sub(u), :]))
            sc = _bdot(qm, kb, BNT) + b_refs[u][0].reshape(N_PAIRS, 2 * ATT_BQ, ATT_BK)
            p = jnp.exp(sc - _per_head_cols(lse_ref[_sub(u), :]))
            ds = p * (_bdot(dom, vb, BNT) + _per_head_cols(cp_ref[_sub(u), :]))
            _store_pairs(dq_ref, _merge_heads(_bdot(ds, kb), ATT_BQ), _sub(u))
            db_ref[_tile_variant(pl.program_id(1), nb, u, sub)] += ds.reshape(SWA_HEADS, ATT_BQ, ATT_BK)

    cur = pl.BlockSpec((rows, SWA_WIDTH), lambda r, t: (t, r))
    dq, db = pl.pallas_call(
        body, name=f"att_dq_d{dilation}", grid=(dilation, nb),
        in_specs=[cur] + _band_specs(length, rows) * 2 + [cur, cur, cur] + _bias_specs(nb, sub,ATT_BQ, ATT_BK),
        out_specs=[cur, pl.BlockSpec((4, SWA_HEADS, ATT_BQ, ATT_BK), lambda r, t: (0, 0, 0, 0))],
        out_shape=[jax.ShapeDtypeStruct(view, BF16), jax.ShapeDtypeStruct((4, SWA_HEADS, ATT_BQ, ATT_BK), F32)],
        compiler_params=_params("arbitrary", "arbitrary"),
    )(q.reshape(view), *([k.reshape(view)] * 3), *([v.reshape(view)] * 3), dop.reshape(view), lse.reshape(view),
      cp.reshape(view), *([bias] * sub))
    return dq.reshape(s, SWA_WIDTH), db


def _att_dkv2(q, k, v, dop, lse, cp, bias_t, dilation):
    s = q.shape[0]
    length = s // dilation
    sub = min(ATT_SUB, length // ATT_BQ)
    rows = sub * ATT_BQ
    nb = length // rows
    view = (length, dilation * SWA_WIDTH)

    def body(k_ref, v_ref, qp, qc, qn, dp_, dc_, dn_, lp, lc, ln, cp_, cc_, cn_, *rest):
        b_refs, (dk_ref, dv_ref) = rest[:sub], rest[sub:]
        qwin, dowin = _band((qp, qc, qn)), _band((dp_, dc_, dn_))
        lsewin, cpwin = _band((lp, lc, ln)), _band((cp_, cc_, cn_))
        for u in range(sub):
            band = _sub(u, ATT_BK)
            qm = _per_head_rows(_pairs(qwin[band]))
            dom = _per_head_rows(_pairs(dowin[band]))
            kv, vv = _pairs(k_ref[_sub(u), :]), _pairs(v_ref[_sub(u), :])
            sc = _bdot(qm, kv, BNT) + b_refs[u][0].reshape(N_PAIRS, 2 * ATT_BK, ATT_BQ)
            p = jnp.exp(sc - _per_head_cols(lsewin[band]))
            _store_pairs(dv_ref, _bdot(p, dom, BTN), _sub(u))
            ds = p * (_bdot(dom, vv, BNT) + _per_head_cols(cpwin[band]))
            _store_pairs(dk_ref, _bdot(ds, qm, BTN), _sub(u))

    cur = pl.BlockSpec((rows, SWA_WIDTH), lambda r, t: (t, r))
    dk, dv = pl.pallas_call(
        body, name=f"att_dkv_d{dilation}", grid=(dilation, nb),
        in_specs=[cur, cur] + _band_specs(length, rows) * 4 + _bias_specs(nb, sub,ATT_BK, ATT_BQ),
        out_specs=[cur, cur],
        out_shape=[jax.ShapeDtypeStruct(view, BF16)] * 2,
        compiler_params=_params("parallel", "parallel"),
    )(k.reshape(view), v.reshape(view), *([q.reshape(view)] * 3), *([dop.reshape(view)] * 3),
      *([lse.reshape(view)] * 3), *([cp.reshape(view)] * 3), *([bias_t] * sub))
    return dk.reshape(s, SWA_WIDTH), dv.reshape(s, SWA_WIDTH)


def _pattern_weights(lses):
    m = lses[0]
    for l in lses[1:]:
        m = jnp.maximum(m, l)
    es = [jnp.exp(l - m) for l in lses]
    den = es[0]
    for e in es[1:]:
        den = den + e
    return [e / den for e in es]


def _combine_fwd(outs, lses):
    s = outs[0].shape[0]
    tm = min(512, s)
    npat = len(outs)

    def body(*refs):
        ws = _pattern_weights([r[...] for r in refs[npat:2 * npat]])
        o = ws[0] * refs[0][...]
        for p in range(1, npat):
            o = o + ws[p] * refs[p][...]
        refs[2 * npat][...] = o.astype(BF16)

    blk = pl.BlockSpec((tm, SWA_WIDTH), lambda i: (i, 0))
    return pl.pallas_call(
        body, name="swa_combine_fwd", grid=(s // tm,), in_specs=[blk] * (2 * npat), out_specs=blk,
        out_shape=jax.ShapeDtypeStruct((s, SWA_WIDTH), BF16), compiler_params=_params("parallel"),
    )(*outs, *lses)


def _combine_bwd(d_out, outs, lses, bd):
    s = d_out.shape[0]
    tm = min(512, s)
    npat = len(outs)

    def body(*refs):
        d_ref, bd_ref = refs[0], refs[1 + 2 * npat]
        o_refs, l_refs = refs[1:1 + npat], refs[1 + npat:1 + 2 * npat]
        out_refs = refs[2 + 2 * npat:]
        ws = _pattern_weights([r[...] for r in l_refs])
        dov = d_ref[...]
        o = ws[0] * o_refs[0][...]
        for p in range(1, npat):
            o = o + ws[p] * o_refs[p][...]
        rd = _group_sum(dov * o, bd_ref[...])
        for p in range(npat):
            out_refs[p][...] = (ws[p] * dov).astype(BF16)
            out_refs[npat + p][...] = -ws[p] * rd

    blk = pl.BlockSpec((tm, SWA_WIDTH), lambda i: (i, 0))
    res = pl.pallas_call(
        body, name="swa_combine_bwd", grid=(s // tm,),
        in_specs=[blk] * (1 + 2 * npat) + [pl.BlockSpec((SWA_WIDTH, SWA_WIDTH), lambda i: (0, 0))],
        out_specs=[blk] * (2 * npat),
        out_shape=[jax.ShapeDtypeStruct((s, SWA_WIDTH), BF16)] * npat + [jax.ShapeDtypeStruct((s, SWA_WIDTH), F32)] * npat,
        compiler_params=_params("parallel"),
    )(d_out, *outs, *lses, bd)
    return res[:npat], res[npat:]


def _rel_bias_grad(dbs, buckets):
    npat = len(dbs)

    def body(*refs):
        db_refs, bk_refs, o_ref = refs[:npat], refs[npat:2 * npat], refs[2 * npat]
        row = lax.broadcasted_iota(jnp.int32, (REL_BUCKETS, LANE), 0)
        lane = lax.broadcasted_iota(jnp.int32, (REL_BUCKETS, LANE), 1)
        tiles = [[db_refs[p][0, h] + db_refs[p][1, h] + db_refs[p][2, h] + db_refs[p][3, h] for h in range(SWA_HEADS)]
                 for p in range(npat)]
        bks = [r[...] for r in bk_refs]

        def one_bucket(b, acc):
            for h in range(SWA_HEADS):
                tot = jnp.zeros((1, 1), F32)
                for p in range(npat):
                    sel = jnp.where(bks[p] == b, tiles[p][h], 0.0)
                    tot = tot + jnp.sum(jnp.sum(sel, axis=1, keepdims=True), axis=0, keepdims=True)
                acc = acc + jnp.where((row == b) & (lane == h), tot, 0.0)
            return acc

        o_ref[...] = lax.fori_loop(0, REL_BUCKETS, one_bucket, jnp.zeros((REL_BUCKETS, LANE), F32))

    full4 = pl.BlockSpec((4, SWA_HEADS, ATT_BQ, ATT_BK), lambda: (0, 0, 0, 0))
    full2 = pl.BlockSpec((ATT_BQ, ATT_BK), lambda: (0, 0))
    return pl.pallas_call(
        body, name="rel_bias_grad", in_specs=[full4] * npat + [full2] * npat,
        out_specs=pl.BlockSpec((REL_BUCKETS, LANE), lambda: (0, 0)),
        out_shape=jax.ShapeDtypeStruct((REL_BUCKETS, LANE), F32),
        compiler_params=pltpu.CompilerParams(vmem_limit_bytes=V7X_VMEM_LIMIT_BYTES),
    )(*dbs, *buckets)


def _swa_forward(p_pad, qw_row, kw_row, rel_bias, bd):
    q, k, v = _swa_pre_fwd(p_pad, qw_row, kw_row, bd)
    outs, lses = [], []
    for _, dil in DILATION_PATTERNS:
        o, lse = _att_fwd2(q, k, v, _bias_tiles(rel_bias, dil, True), dil)
        outs.append(o)
        lses.append(lse)
    return _combine_fwd(outs, lses), (q, k, v, outs, lses)


def _swa_backward(d_out, p_pad, qw_row, kw_row, rel_bias, bd, saved, dp_all):
    q, k, v, outs, lses = saved
    dops, cps = _combine_bwd(d_out, outs, lses, bd)
    dqs, dks, dvs, dbs, buckets = [], [], [], [], []
    for p, (_, dil) in enumerate(DILATION_PATTERNS):
        dq, db = _att_dq2(q, k, v, dops[p], lses[p], cps[p], _bias_tiles(rel_bias, dil, True), dil)
        dk, dv = _att_dkv2(q, k, v, dops[p], lses[p], cps[p], _bias_tiles(rel_bias, dil, False), dil)
        dqs.append(dq)
        dks.append(dk)
        dvs.append(dv)
        dbs.append(db)
        buckets.append(jnp.asarray(_band_tables(dil, True)[1]))
    dp, dqw, dkw = _swa_pre_bwd(dqs, dks, dvs, p_pad, qw_row, kw_row, bd, dp_all)
    return dp, dqw, dkw, _rel_bias_grad(dbs, buckets)


def _lane_row(v):
    flat = v.reshape(-1).astype(F32)
    return jnp.zeros((1, LANE), F32).at[0, :flat.shape[0]].set(flat)


W_IN_SHARD = N_IN // N_DEV
W_IN_RUNS = ((0, NAT_Z, 0), (NAT_Z, NAT_AB, OFF_Z), (NAT_AB, NAT_B, OFF_AB), (NAT_B, N_IN, OFF_B))


def _w_in_pieces(shard):
    lo, hi = shard * W_IN_SHARD, (shard + 1) * W_IN_SHARD
    out = []
    for first, last, dst in W_IN_RUNS:
        a, b = max(lo, first), min(hi, last)
        if a < b:
            out.append((a - lo, b - a, dst + a - first))
    return out


def _w_in_from_slabs(w3):
    nd, r, _ = w3.shape

    def body(w_ref, o_ref):
        o_ref[:, OFF_AB:N_PAD] = jnp.zeros((r, N_PAD - OFF_AB), w3.dtype)
        for sh in range(nd):
            for src, length, dst in _w_in_pieces(sh):
                o_ref[:, dst:dst + length] = w_ref[sh, :, src:src + length]

    return pl.pallas_call(
        body, name="w_in_from_slabs", out_shape=jax.ShapeDtypeStruct((r, N_PAD), w3.dtype),
        compiler_params=pltpu.CompilerParams(vmem_limit_bytes=V7X_VMEM_LIMIT_BYTES),
    )(w3)


def _w_in_grad_slabs(dw_pad, dtype):
    r = dw_pad.shape[0]

    def body(dw_ref, o_ref):
        for sh in range(N_DEV):
            for src, length, dst in _w_in_pieces(sh):
                o_ref[sh, :, src:src + length] = dw_ref[:, dst:dst + length].astype(dtype)

    return pl.pallas_call(
        body, name="w_in_grad_slabs", out_shape=jax.ShapeDtypeStruct((N_DEV, r, W_IN_SHARD), dtype),
        compiler_params=pltpu.CompilerParams(vmem_limit_bytes=V7X_VMEM_LIMIT_BYTES),
    )(dw_pad)


LATE = ("w_out", "ffn2_w_gate", "ffn2_w_up", "ffn2_w_down")
TRANSPOSED = ("ffn1_w_gate", "ffn1_w_up", "ffn2_w_gate", "ffn2_w_up")


def _late_weights(slabs):
    return {n: g.reshape(N_DEV * g.shape[1], g.shape[2]) for n, g in zip(LATE, slabs)}


def _local_step(x, tgt, wts, small, late_shards=None):
    bd = _head_block_diag()
    alog_row, dt_row = _lane_row(small["a_log"]), _lane_row(small["dt_bias"])
    gnorm_row = small["gdn_norm_w"].reshape(1, GDN_HEAD_DIM)
    qw_row = jnp.tile(small["q_norm_w"].reshape(-1), SWA_HEADS).reshape(1, SWA_WIDTH)
    kw_row = jnp.tile(small["k_norm_w"].reshape(-1), SWA_HEADS).reshape(1, SWA_WIDTH)
    rel_bias = small["rel_bias"]
    exchange = late_shards is not None
    dw_dtype = BF16 if exchange else F32

    x1, sv1, (wg1, wu1, wd1), got, first = _ffn_forward(
        x, small["ffn1_norm"], wts.get("ffn1_w_gate"), wts.get("ffn1_w_up"), wts.get("ffn1_w_down"), "ffn1",
        gather=[late_shards["ffn1_w_down"], late_shards["w_in"]] if exchange else (),
        head=[late_shards["ffn1_w_gate"], late_shards["ffn1_w_up"], late_shards["conv_w"]] if exchange else ())
    win_pad = _w_in_from_slabs(got[0]) if exchange else wts["w_in_pad"]
    conv_w = first[0].reshape(N_DEV, -1)[:, :QKV_A // N_DEV * CONV_WIDTH].reshape(QKV_A, CONV_WIDTH) if exchange \
        else small["conv_w"]
    conv_wt = jnp.zeros((8, QKV_A), F32).at[:CONV_WIDTH].set(conv_w.T)
    n2, r2 = _rms_fwd(x1, small["mix_norm"], "mix_norm")
    p_pad = _matmul([(n2, win_pad)], tm=256, tn=N_PAD, tk=D_MODEL, name="w_in")
    o_a, sva, gathered = _gdn_forward(p_pad, conv_wt, alog_row, dt_row, gnorm_row,
                                      gather=[late_shards[n] for n in LATE] if exchange else ())
    if exchange:
        wts = {**wts, **_late_weights(gathered)}
    wo_a, wo_b = wts["w_out"][:GDN_WIDTH], wts["w_out"][GDN_WIDTH:]
    o_b, svb = _swa_forward(p_pad, qw_row, kw_row, rel_bias, bd)
    x2 = _matmul([(o_a, wo_a), (o_b, wo_b)], tm=512, tn=D_MODEL, tk=GDN_WIDTH, name="w_out", res=x1)
    x3, sv2, _, _, _ = _ffn_forward(x2, small["ffn2_norm"], wts["ffn2_w_gate"], wts["ffn2_w_up"], wts["ffn2_w_down"],
                                    "ffn2")
    loss_row, dx3, d_final = _final_loss(x3, small["final_norm"], tgt)

    dx2, d_ffn2_norm, dwg2, dwu2, dwd2, _ = _ffn_backward(
        dx3, x2, small["ffn2_norm"], wts["ffn2_w_gate"], wts["ffn2_w_up"], wts["ffn2_w_down"], sv2, "ffn2", dw_dtype)
    d_oa = _matmul([(dx2, wo_a)], tb=True, tm=512, tn=GDN_WIDTH, tk=D_MODEL, name="w_out_da")
    d_ob = _matmul([(dx2, wo_b)], tb=True, tm=512, tn=SWA_WIDTH, tk=D_MODEL, name="w_out_db")
    dwo_a = _matmul([(o_a, dx2)], ta=True, tm=GDN_WIDTH, tn=D_MODEL, tk=2048, name="w_out_dwa", out_dtype=dw_dtype)
    dwo_b = _matmul([(o_b, dx2)], ta=True, tm=SWA_WIDTH, tn=D_MODEL, tk=2048, name="w_out_dwb", out_dtype=dw_dtype)

    late_grads = [_row_slabs(jnp.concatenate([dwo_a, dwo_b], axis=0)), dwg2, dwu2, dwd2]
    dp_all, dconv, gate_sums, d_gnorm, received = _gdn_backward(
        d_oa, p_pad, conv_wt, alog_row, dt_row, gnorm_row, sva, scatter=late_grads if exchange else ())
    if exchange:
        late_grads = received
    dp_all, dqw, dkw, d_rel = _swa_backward(d_ob, p_pad, qw_row, kw_row, rel_bias, bd, svb, dp_all)
    dw_pad = _matmul([(n2, dp_all)], ta=True, tm=D_MODEL, tn=N_PAD // 3, tk=2048, name="w_in_dw")
    dn2 = _matmul([(dp_all, win_pad)], tb=True, tm=512, tn=D_MODEL, tk=N_PAD, name="w_in_dn")
    dx1, d_mix_norm = _rms_bwd(dn2, x1, r2, small["mix_norm"], dx2, "mix_dnorm")
    d_w_in = _w_in_grad_slabs(dw_pad, dw_dtype)
    dx, d_ffn1_norm, dwg1, dwu1, dwd1, got = _ffn_backward(
        dx1, x, small["ffn1_norm"], wg1, wu1, wd1, sv1, "ffn1", dw_dtype,
        scatter=[d_w_in] if exchange else None)
    if exchange:
        d_w_in = got[0]

    grads = {
        "ffn1_norm": d_ffn1_norm, "ffn1_w_gate": dwg1, "ffn1_w_up": dwu1, "ffn1_w_down": dwd1,
        "mix_norm": d_mix_norm, "w_in": d_w_in, "conv_w": dconv[:CONV_WIDTH].T,
        "a_log": gate_sums[0, :8].reshape(2, GDN_HEADS), "dt_bias": gate_sums[1, :8].reshape(2, GDN_HEADS),
        "gdn_norm_w": d_gnorm, "q_norm_w": dqw.reshape(SWA_HEADS, SWA_HEAD_DIM).sum(0, keepdims=True),
        "k_norm_w": dkw.reshape(SWA_HEADS, SWA_HEAD_DIM).sum(0, keepdims=True), "rel_bias": d_rel[:, :SWA_HEADS],
        "ffn2_norm": d_ffn2_norm, "final_norm": d_final, **dict(zip(LATE, late_grads)),
    }
    return loss_row, dx, grads


MESH_IDS = pl.DeviceIdType.MESH
ANY = pl.BlockSpec(memory_space=pl.ANY)


def _adamw(parts, w, m, v, name):
    nparts, r, n = parts.shape
    tr = r
    for cand in (256, 176, 128, 104, 64, 8):
        if r % cand == 0:
            tr = cand
            break
    bc1 = 1.0 - ADAM_B1 ** ADAM_STEP
    bc2 = 1.0 - ADAM_B2 ** ADAM_STEP

    def body(p_ref, w_ref, m_ref, v_ref, g_ref, d_ref, nm_ref, nv_ref):
        g = p_ref[0].astype(F32)
        for k in range(1, nparts):
            g = g + p_ref[k].astype(F32)
        mn = ADAM_B1 * m_ref[...] + (1.0 - ADAM_B1) * g
        vn = ADAM_B2 * v_ref[...] + (1.0 - ADAM_B2) * (g * g)
        m_hat = mn / bc1
        v_hat = vn / bc2
        g_ref[...] = g
        nm_ref[...] = mn
        nv_ref[...] = vn
        d_ref[...] = -ADAM_LR * (m_hat / (jnp.sqrt(v_hat) + ADAM_EPS) + ADAM_WD * w_ref[...])

    blk = pl.BlockSpec((tr, n), lambda i: (i, 0))
    return pl.pallas_call(
        body, name=name, grid=(r // tr,),
        in_specs=[pl.BlockSpec((nparts, tr, n), lambda i: (0, i, 0)), blk, blk, blk],
        out_specs=[blk] * 4, out_shape=[jax.ShapeDtypeStruct((r, n), F32)] * 4,
        compiler_params=_params("parallel"),
    )(parts, w, m, v)


def _mesh_place():
    x, y, c = lax.axis_index("x"), lax.axis_index("y"), lax.axis_index("c")
    return x, y, c, [(1 - x, y), (x, 1 - y), (1 - x, 1 - y)]


def _gather_phases(x_refs, out_refs, send_sems, recv_sems, local_sems):
    na = len(x_refs)

    def place():
        x, y, c, chips = _mesh_place()
        return (x, y, c), (x, y, 1 - c), chips, c

    def slab(i, px, py, pc):
        return out_refs[i].at[4 * px + 2 * py + pc]

    def copy(i, k, block, to, src=None):
        return pltpu.make_async_remote_copy(
            src_ref=slab(i, *block) if src is None else src, dst_ref=slab(i, *block),
            send_sem=send_sems.at[i, k], recv_sem=recv_sems.at[i, k], device_id=to, device_id_type=MESH_IDS)

    def own(i, me):
        return pltpu.make_async_copy(x_refs[i], slab(i, *me), local_sems.at[i])

    def sends(i, me, sibling, chips, c):
        return [copy(i, 0, me, sibling, src=x_refs[i])] + [copy(i, 1 + j, me, (*chip, c), src=x_refs[i])
                                                          for j, chip in enumerate(chips)]

    def start():
        me, sibling, chips, c = place()
        for i in range(na):
            own(i, me).start()
            for cp in sends(i, me, sibling, chips, c):
                cp.start()

    def forward():
        me, sibling, chips, c = place()
        for j, chip in enumerate(chips):
            for i in range(na):
                copy(i, 1 + j, (*chip, c), me).wait_recv()
                copy(i, 4 + j, (*chip, c), sibling).start()

    def finish():
        me, sibling, chips, c = place()
        for i in range(na):
            copy(i, 0, sibling, me).wait_recv()
        for j, chip in enumerate(chips):
            for i in range(na):
                copy(i, 4 + j, (*chip, 1 - c), me).wait_recv()
        for i in range(na):
            for cp in sends(i, me, sibling, chips, c):
                cp.wait_send()
            for j, chip in enumerate(chips):
                copy(i, 4 + j, (*chip, c), sibling).wait_send()
            own(i, me).wait()

    return start, forward, finish


def _gather_semaphores(na):
    return [pltpu.SemaphoreType.DMA((na, 7)), pltpu.SemaphoreType.DMA((na, 7)), pltpu.SemaphoreType.DMA((na,))]


def _scatter_phases(g_refs, out_refs, send_sems, recv_sems, local_sems):
    na = len(g_refs)

    def place(m):
        x, y, c = lax.axis_index("x"), lax.axis_index("y"), lax.axis_index("c")
        px = 1 - x if m & 4 else x
        py = 1 - y if m & 2 else y
        pc = 1 - c if m & 1 else c
        return 4 * x + 2 * y + c, (px, py, pc), 4 * px + 2 * py + pc

    def own(i):
        me, _, _ = place(0)
        return pltpu.make_async_copy(g_refs[i].at[me], out_refs[i].at[me], local_sems.at[i])

    def start():
        for i in range(na):
            own(i).start()
            for m in range(1, N_DEV):
                me, peer, peer_idx = place(m)
                pltpu.make_async_remote_copy(
                    src_ref=g_refs[i].at[peer_idx], dst_ref=out_refs[i].at[me], send_sem=send_sems.at[i, m - 1],
                    recv_sem=recv_sems.at[i, m - 1], device_id=peer, device_id_type=MESH_IDS).start()

    def finish():
        for i in range(na):
            for m in range(1, N_DEV):
                me, peer, peer_idx = place(m)
                cp = pltpu.make_async_remote_copy(
                    src_ref=g_refs[i].at[peer_idx], dst_ref=out_refs[i].at[peer_idx], send_sem=send_sems.at[i, m - 1],
                    recv_sem=recv_sems.at[i, m - 1], device_id=peer, device_id_type=MESH_IDS)
                cp.wait_recv()
                cp.wait_send()
            own(i).wait()

    return start, finish


def _all_gather_many(vs, name):
    na = len(vs)

    def body(*refs):
        x_refs, out_refs = refs[:na], refs[na:2 * na]
        for step in _gather_phases(x_refs, out_refs, *refs[2 * na:]):
            step()

    return pl.pallas_call(
        body, name=name, in_specs=[ANY] * na, out_specs=[ANY] * na,
        out_shape=[jax.ShapeDtypeStruct((N_DEV,) + v.shape, v.dtype) for v in vs],
        scratch_shapes=_gather_semaphores(na),
        compiler_params=pltpu.CompilerParams(vmem_limit_bytes=V7X_VMEM_LIMIT_BYTES),
    )(*vs)


BIG = ("ffn1_w_gate", "ffn1_w_up", "ffn1_w_down", "w_in", "w_out", "ffn2_w_gate", "ffn2_w_up", "ffn2_w_down")
SMALL = ("ffn1_norm", "mix_norm", "a_log", "dt_bias", "gdn_norm_w", "q_norm_w", "k_norm_w", "rel_bias",
         "ffn2_norm", "final_norm")
WEIGHTS = ("ffn1_norm", "ffn1_w_gate", "ffn1_w_up", "ffn1_w_down", "mix_norm", "w_in", "conv_w", "a_log", "dt_bias",
           "gdn_norm_w", "q_norm_w", "k_norm_w", "rel_bias", "w_out", "ffn2_norm", "ffn2_w_gate", "ffn2_w_up",
           "ffn2_w_down", "final_norm")


def _pack(arrays, width, row_multiple):
    flat = jnp.concatenate([a.reshape(-1) for a in arrays])
    rows = -(-flat.shape[0] // width)
    rows = -(-rows // row_multiple) * row_multiple
    return jnp.pad(flat, (0, rows * width - flat.shape[0])).reshape(rows, width)


def _unpack(packed, shapes):
    flat = packed.reshape(-1)
    out, pos = [], 0
    for shp in shapes:
        size = int(np.prod(shp))
        out.append(flat[pos:pos + size].reshape(shp))
        pos += size
    return out


def kernel(x, ffn1_norm, ffn1_w_gate, ffn1_w_up, ffn1_w_down, mix_norm, w_in, conv_w, a_log, dt_bias, gdn_norm_w, q_norm_w, k_norm_w, rel_bias, w_out, ffn2_norm, ffn2_w_gate, ffn2_w_up, ffn2_w_down, final_norm, loss_target, m_ffn1_norm, m_ffn1_w_gate, m_ffn1_w_up, m_ffn1_w_down, m_mix_norm, m_w_in, m_conv_w, m_a_log, m_dt_bias, m_gdn_norm_w, m_q_norm_w, m_k_norm_w, m_rel_bias, m_w_out, m_ffn2_norm, m_ffn2_w_gate, m_ffn2_w_up, m_ffn2_w_down, m_final_norm, v_ffn1_norm, v_ffn1_w_gate, v_ffn1_w_up, v_ffn1_w_down, v_mix_norm, v_w_in, v_conv_w, v_a_log, v_dt_bias, v_gdn_norm_w, v_q_norm_w, v_k_norm_w, v_rel_bias, v_w_out, v_ffn2_norm, v_ffn2_w_gate, v_ffn2_w_up, v_ffn2_w_down, v_final_norm):
    w = dict(ffn1_norm=ffn1_norm, ffn1_w_gate=ffn1_w_gate, ffn1_w_up=ffn1_w_up, ffn1_w_down=ffn1_w_down, mix_norm=mix_norm, w_in=w_in, conv_w=conv_w, a_log=a_log, dt_bias=dt_bias, gdn_norm_w=gdn_norm_w, q_norm_w=q_norm_w, k_norm_w=k_norm_w, rel_bias=rel_bias, w_out=w_out, ffn2_norm=ffn2_norm, ffn2_w_gate=ffn2_w_gate, ffn2_w_up=ffn2_w_up, ffn2_w_down=ffn2_w_down, final_norm=final_norm)
    mom = dict(ffn1_norm=m_ffn1_norm, ffn1_w_gate=m_ffn1_w_gate, ffn1_w_up=m_ffn1_w_up, ffn1_w_down=m_ffn1_w_down, mix_norm=m_mix_norm, w_in=m_w_in, conv_w=m_conv_w, a_log=m_a_log, dt_bias=m_dt_bias, gdn_norm_w=m_gdn_norm_w, q_norm_w=m_q_norm_w, k_norm_w=m_k_norm_w, rel_bias=m_rel_bias, w_out=m_w_out, ffn2_norm=m_ffn2_norm, ffn2_w_gate=m_ffn2_w_gate, ffn2_w_up=m_ffn2_w_up, ffn2_w_down=m_ffn2_w_down, final_norm=m_final_norm)
    var = dict(ffn1_norm=v_ffn1_norm, ffn1_w_gate=v_ffn1_w_gate, ffn1_w_up=v_ffn1_w_up, ffn1_w_down=v_ffn1_w_down, mix_norm=v_mix_norm, w_in=v_w_in, conv_w=v_conv_w, a_log=v_a_log, dt_bias=v_dt_bias, gdn_norm_w=v_gdn_norm_w, q_norm_w=v_q_norm_w, k_norm_w=v_k_norm_w, rel_bias=v_rel_bias, w_out=v_w_out, ffn2_norm=v_ffn2_norm, ffn2_w_gate=v_ffn2_w_gate, ffn2_w_up=v_ffn2_w_up, ffn2_w_down=v_ffn2_w_down, final_norm=v_final_norm)
    ix, iy, ic = lax.axis_index("x"), lax.axis_index("y"), lax.axis_index("c")
    me = 4 * ix + 2 * iy + ic

    def local(a, n):
        return jnp.swapaxes(a[0], 0, 1) if n in TRANSPOSED else a[0]

    shard = {n: local(w[n], n) for n in BIG}

    conv_shard_shape = w["conv_w"][0].shape
    small = {n: w[n][0] if n not in ("rel_bias",) else w[n] for n in SMALL}
    small = {n: (a.reshape(1, -1) if n.endswith("norm") else a) for n, a in small.items()}
    shards = {n: shard[n].astype(BF16) for n in BIG}
    shards["conv_w"] = _pack([w["conv_w"][0]], LANE, 8)
    loss_row, grad_x, grads = _local_step(x[0], loss_target[0], {}, small, late_shards=shards)

    big_out = [[], [], [], []]
    for n in BIG:
        for kind, val in enumerate(_adamw(grads[n], shard[n], local(mom[n], n), local(var[n], n), f"{n}_adamw")):
            big_out[kind].append(jnp.swapaxes(val, 0, 1) if n in TRANSPOSED else val)

    small_names = SMALL + ("conv_w",)
    small_shapes = [grads[n].shape for n in small_names] + [(1, 1)]
    g_small = _pack([grads[n] for n in small_names] + [loss_row[:, :1]], LANE, 8)
    all_small = _all_gather_many([g_small], "gather_small_grads")[0]
    riders = [jnp.zeros(shp, F32) for shp in small_shapes[len(SMALL):]]
    ws = _pack([w[n].reshape(grads[n].shape) for n in SMALL] + riders, LANE, 8)
    ms = _pack([mom[n].reshape(grads[n].shape) for n in SMALL] + riders, LANE, 8)
    vs = _pack([var[n].reshape(grads[n].shape) for n in SMALL] + riders, LANE, 8)
    small_out = [_unpack(a, small_shapes) for a in _adamw(all_small, ws, ms, vs, "adamw_small")]
    loss = small_out[0][-1][0, 0]
    conv_g = lax.dynamic_slice_in_dim(small_out[0][len(SMALL)], me * conv_shard_shape[0], conv_shard_shape[0], axis=0)
    conv_out = [_unpack(a, [conv_shard_shape])[0] for a in _adamw(
        _pack([conv_g], LANE, 8)[None], _pack([w["conv_w"][0]], LANE, 8), _pack([mom["conv_w"][0]], LANE, 8),
        _pack([var["conv_w"][0]], LANE, 8), "adamw_conv")]

    def leaf(kind, n):
        if n in BIG:
            val = big_out[kind][BIG.index(n)]
        elif n == "conv_w":
            val = conv_out[kind]
        else:
            val = small_out[kind][SMALL.index(n)]
        return val.reshape(w[n].shape)

    outs = [loss, grad_x[None]]
    for kind in range(4):
        outs += [leaf(kind, n) for n in WEIGHTS]
    return tuple(outs)
```

```python
import functools
import math

import numpy as np
import jax
import jax.numpy as jnp
from jax import lax
from jax.experimental import pallas as pl
from jax.experimental.pallas import tpu as pltpu

F32 = jnp.float32
BF16 = jnp.bfloat16

D_MODEL = 1024
D_FF = 2816
GDN_HEADS = 4
GDN_HEAD_DIM = 128
GDN_WIDTH = 512
CONV_WIDTH = 5
CHUNK = 64
SWA_HEADS = 8
SWA_HEAD_DIM = 64
SWA_WIDTH = 512
DILATION_PATTERNS = ((128, 1), (512, 4), (2048, 16))
REL_BUCKETS = 32
REL_MAX_DISTANCE = 1024
EPS = 1e-6
NEG_BIG = -1e30
N_DEV = 8

ADAM_LR = 0.001
ADAM_B1 = 0.9
ADAM_B2 = 0.999
ADAM_EPS = 1e-08
ADAM_WD = 0.01
ADAM_STEP = 10

QKV_A = 3 * GDN_WIDTH
OFF_B = QKV_A
OFF_Z = OFF_B + 3 * SWA_WIDTH
OFF_AB = OFF_Z + GDN_WIDTH
N_PAD = OFF_AB + 256
N_IN = 3600
NAT_Z, NAT_AB, NAT_B = QKV_A, QKV_A + GDN_WIDTH, QKV_A + GDN_WIDTH + 16

V7X_VMEM_LIMIT_BYTES = 56 * 1024 * 1024
LANE = 128
ATT_BQ = 128
ATT_HALO = 64
CONV_ROWS = 256

NN = (((1,), (0,)), ((), ()))
NT = (((1,), (1,)), ((), ()))
TN = (((0,), (0,)), ((), ()))


def _params(*sem):
    return pltpu.CompilerParams(dimension_semantics=sem, vmem_limit_bytes=V7X_VMEM_LIMIT_BYTES)


def _dot(a, b, dn=NN):
    return lax.dot_general(a.astype(BF16), b.astype(BF16), dn, preferred_element_type=F32)


def _sigmoid(x):
    return 1.0 / (1.0 + jnp.exp(-x))


class _Exchange:
    def __init__(self, kind, arrays):
        self.kind, self.arrays = kind, list(arrays)

    def out_shape(self):
        lead = (N_DEV,) if self.kind == "gather" else ()
        return [jax.ShapeDtypeStruct(lead + v.shape, v.dtype) for v in self.arrays]

    def hooks(self, in_refs, out_refs, sems, grid):
        step = pl.program_id(0)
        for axis in range(1, len(grid)):
            step = step * grid[axis] + pl.program_id(axis)
        total = math.prod(grid)
        if self.kind == "gather":
            assert total >= 4
            start, forward, finish = _gather_phases(in_refs, out_refs, *sems)
            pl.when(step == total // 2)(forward)
        else:
            assert total >= 2
            start, finish = _scatter_phases(in_refs, out_refs, *sems)
        pl.when(step == 0)(start)
        pl.when(step == total - 1)(finish)


def _pallas(body, *, name, grid, in_specs, out_specs, out_shape, args, semantics, scratch_shapes=(), exchange=None):
    n_in, n_out, n_scr = len(in_specs), len(out_specs), len(scratch_shapes)
    if exchange is None:
        res = pl.pallas_call(
            body, name=name, grid=grid, in_specs=list(in_specs), out_specs=list(out_specs), out_shape=list(out_shape),
            scratch_shapes=list(scratch_shapes), compiler_params=_params(*semantics))(*args)
        return list(res), []
    na = len(exchange.arrays)

    def carrying(*refs):
        ins, sent = refs[:n_in], refs[n_in:n_in + na]
        outs = refs[n_in + na:n_in + na + n_out]
        landed = refs[n_in + na + n_out:n_in + 2 * na + n_out]
        rest = refs[n_in + 2 * na + n_out:]
        exchange.hooks(sent, landed, rest[n_scr:], grid)
        body(*ins, *outs, *rest[:n_scr])

    res = pl.pallas_call(
        carrying, name=name, grid=grid, in_specs=list(in_specs) + [ANY] * na, out_specs=list(out_specs) + [ANY] * na,
        out_shape=list(out_shape) + exchange.out_shape(), scratch_shapes=list(scratch_shapes) + _gather_semaphores(na),
        compiler_params=_params(*(["arbitrary"] * len(grid))))(*args, *exchange.arrays)
    return list(res[:n_out]), list(res[n_out:])


def _matmul(pairs, *, ta=False, tb=False, out_dtype=F32, tm, tn, tk, name, res=None, alpha=None, norm_bwd=None,
            exchange=None):
    a0, b0 = pairs[0]
    m = a0.shape[1] if ta else a0.shape[0]
    k = a0.shape[0] if ta else a0.shape[1]
    n = b0.shape[0] if tb else b0.shape[1]
    tm, tn, tk = min(tm, m), min(tn, n), min(tk, k)
    assert m % tm == 0 and n % tn == 0 and k % tk == 0, (name, m, n, k, tm, tn, tk)
    nk = k // tk
    npairs = len(pairs)
    dn = (((0 if ta else 1,), (1 if tb else 0,)), ((), ()))
    assert norm_bwd is None or (tn == n and res is None and alpha is None)

    def body(*refs):
        ins = refs[:2 * npairs]
        pos = 2 * npairs
        r_ref = None
        if res is not None:
            r_ref = refs[pos]
            pos += 1
        if norm_bwd is not None:
            x_ref, rs_ref, w_ref, dres_ref = refs[pos:pos + 4]
            o_ref, dw_ref, acc = refs[pos + 4:pos + 7]

            @pl.when((pl.program_id(0) == 0) & (pl.program_id(2) == 0))
            def _():
                dw_ref[...] = jnp.zeros_like(dw_ref)
        else:
            o_ref, acc = refs[pos], refs[pos + 1]
        kk = pl.program_id(2)
        t = None
        for p in range(npairs):
            d = _dot(ins[2 * p][...], ins[2 * p + 1][...], dn)
            t = d if t is None else t + d

        if nk > 1:
            @pl.when(kk == 0)
            def _():
                acc[...] = t

            @pl.when((kk > 0) & (kk < nk - 1))
            def _():
                acc[...] += t

        @pl.when(kk == nk - 1)
        def _():
            r = acc[...] + t if nk > 1 else t
            if alpha is not None:
                r = r * alpha
            if r_ref is not None:
                r = r_ref[...] + r
            if norm_bwd is not None:
                rs = rs_ref[...]
                xhat = x_ref[...] * rs
                dw_ref[...] += jnp.sum(r * xhat, axis=0, keepdims=True)
                t_w = r * w_ref[...]
                r = dres_ref[...] + rs * (t_w - xhat * jnp.mean(t_w * xhat, axis=-1, keepdims=True))
            o_ref[...] = r.astype(out_dtype)

    a_spec = pl.BlockSpec((tk, tm), lambda i, j, kk: (kk, i)) if ta else pl.BlockSpec((tm, tk), lambda i, j, kk: (i, kk))
    b_spec = pl.BlockSpec((tn, tk), lambda i, j, kk: (j, kk)) if tb else pl.BlockSpec((tk, tn), lambda i, j, kk: (kk, j))
    o_spec = pl.BlockSpec((tm, tn), lambda i, j, kk: (i, j))
    in_specs = [a_spec, b_spec] * npairs + ([o_spec] if res is not None else [])
    args = [t for pr in pairs for t in pr] + ([res] if res is not None else [])
    out_specs, out_shape = [o_spec], [jax.ShapeDtypeStruct((m, n), out_dtype)]
    if norm_bwd is not None:
        vec = pl.BlockSpec((1, n), lambda i, j, kk: (0, 0))
        in_specs += [o_spec, pl.BlockSpec((tm, 1), lambda i, j, kk: (i, 0)), vec, o_spec]
        args += list(norm_bwd)
        out_specs.append(vec)
        out_shape.append(jax.ShapeDtypeStruct((1, n), F32))
    outs, exchanged = _pallas(
        body, name=name, grid=(m // tm, n // tn, nk), in_specs=in_specs, out_specs=out_specs, out_shape=out_shape,
        scratch_shapes=[pltpu.VMEM((tm, tn) if nk > 1 else (8, LANE), F32)],
        semantics=("arbitrary",) * 3 if norm_bwd is not None else ("parallel", "parallel", "arbitrary"), args=args,
        exchange=exchange)
    out = outs[0] if norm_bwd is None else tuple(outs)
    return out if exchange is None else (out, exchanged)


def _rms_fwd(x, w, name, exchange=None):
    s, d = x.shape
    tm = min(512, s)

    def body(x_ref, w_ref, n_ref, r_ref):
        xv = x_ref[...]
        r = lax.rsqrt(jnp.mean(xv * xv, axis=-1, keepdims=True) + EPS)
        n_ref[...] = (xv * r * w_ref[...]).astype(BF16)
        r_ref[...] = r

    (n, r), exchanged = _pallas(
        body, name=name, grid=(s // tm,),
        in_specs=[pl.BlockSpec((tm, d), lambda i: (i, 0)), pl.BlockSpec((1, d), lambda i: (0, 0))],
        out_specs=[pl.BlockSpec((tm, d), lambda i: (i, 0)), pl.BlockSpec((tm, 1), lambda i: (i, 0))],
        out_shape=[jax.ShapeDtypeStruct((s, d), BF16), jax.ShapeDtypeStruct((s, 1), F32)],
        semantics=("parallel",), args=(x, w), exchange=exchange)
    return (n, r) if exchange is None else (n, r, exchanged)


def _rms_bwd(dn, x, r, w, dres, name, exchange=None):
    s, d = x.shape
    tm = min(512, s)

    def body(dn_ref, x_ref, r_ref, w_ref, dres_ref, dx_ref, dw_ref):
        @pl.when(pl.program_id(0) == 0)
        def _():
            dw_ref[...] = jnp.zeros_like(dw_ref)

        rv = r_ref[...]
        xhat = x_ref[...] * rv
        g = dn_ref[...]
        t = g * w_ref[...]
        dx_ref[...] = dres_ref[...] + rv * (t - xhat * jnp.mean(t * xhat, axis=-1, keepdims=True))
        dw_ref[...] += jnp.sum(g * xhat, axis=0, keepdims=True)

    row = pl.BlockSpec((tm, d), lambda i: (i, 0))
    vec = pl.BlockSpec((1, d), lambda i: (0, 0))
    (dx, dw), exchanged = _pallas(
        body, name=name, grid=(s // tm,),
        in_specs=[row, row, pl.BlockSpec((tm, 1), lambda i: (i, 0)), vec, row],
        out_specs=[row, vec],
        out_shape=[jax.ShapeDtypeStruct((s, d), F32), jax.ShapeDtypeStruct((1, d), F32)],
        semantics=("arbitrary",), args=(dn, x, r, w, dres), exchange=exchange)
    return (dx, dw) if exchange is None else (dx, dw, exchanged)


def _final_loss(x3, wf, tgt):
    s, d = x3.shape
    tm = min(512, s)

    def body(x_ref, w_ref, t_ref, loss_ref, dx_ref, dw_ref):
        @pl.when(pl.program_id(0) == 0)
        def _():
            dw_ref[...] = jnp.zeros_like(dw_ref)
            loss_ref[...] = jnp.zeros_like(loss_ref)

        xv = x_ref[...]
        wv = w_ref[...]
        r = lax.rsqrt(jnp.mean(xv * xv, axis=-1, keepdims=True) + EPS)
        xhat = xv * r
        e = xhat * wv - t_ref[...]
        part = 0.5 * jnp.sum(jnp.mean(e * e, axis=-1, keepdims=True), axis=0, keepdims=True)
        loss_ref[...] += jnp.broadcast_to(part, loss_ref.shape)
        dy = e * (1.0 / d)
        dw_ref[...] += jnp.sum(dy * xhat, axis=0, keepdims=True)
        t = dy * wv
        dx_ref[...] = r * (t - xhat * jnp.mean(t * xhat, axis=-1, keepdims=True))

    row = pl.BlockSpec((tm, d), lambda i: (i, 0))
    vec = pl.BlockSpec((1, d), lambda i: (0, 0))
    return pl.pallas_call(
        body, name="final_loss", grid=(s // tm,),
        in_specs=[row, vec, row],
        out_specs=[pl.BlockSpec((1, LANE), lambda i: (0, 0)), row, vec],
        out_shape=[jax.ShapeDtypeStruct((1, LANE), F32), jax.ShapeDtypeStruct((s, d), F32),
                   jax.ShapeDtypeStruct((1, d), F32)],
        compiler_params=_params("arbitrary"),
    )(x3, wf, tgt)


def _ffn_up(n, wg, wu, name, exchange=None):
    s, d = n.shape
    f = wg.shape[0]
    tm, tn = min(512, s), f // 2

    def body(n_ref, wg_ref, wu_ref, g_ref, u_ref, a_ref):
        nv = n_ref[...]
        g = _dot(nv, wg_ref[...], NT)
        u = _dot(nv, wu_ref[...], NT)
        g_ref[...] = g.astype(BF16)
        u_ref[...] = u.astype(BF16)
        a_ref[...] = (g * _sigmoid(g) * u).astype(BF16)

    o = pl.BlockSpec((tm, tn), lambda j, i: (i, j))
    wspec = pl.BlockSpec((tn, d), lambda j, i: (j, 0))
    return _pallas(
        body, name=name, grid=(f // tn, s // tm),
        in_specs=[pl.BlockSpec((tm, d), lambda j, i: (i, 0)), wspec, wspec],
        out_specs=[o, o, o],
        out_shape=[jax.ShapeDtypeStruct((s, f), BF16)] * 3,
        semantics=("parallel", "parallel"), args=(n, wg, wu), exchange=exchange)


def _ffn_dact(dx, wd, g, u, name, exchange=None):
    s, d = dx.shape
    f = wd.shape[0]
    tm, tn = min(512, s), f // 2

    def body(dx_ref, wd_ref, g_ref, u_ref, dg_ref, du_ref):
        da = 0.5 * _dot(dx_ref[...], wd_ref[...], NT)
        gv = g_ref[...].astype(F32)
        sg = _sigmoid(gv)
        du_ref[...] = (da * gv * sg).astype(BF16)
        dg_ref[...] = (da * u_ref[...].astype(F32) * (sg * (1.0 + gv * (1.0 - sg)))).astype(BF16)

    o = pl.BlockSpec((tm, tn), lambda j, i: (i, j))
    return _pallas(
        body, name=name, grid=(f // tn, s // tm),
        in_specs=[pl.BlockSpec((tm, d), lambda j, i: (i, 0)), pl.BlockSpec((tn, d), lambda j, i: (j, 0)), o, o],
        out_specs=[o, o],
        out_shape=[jax.ShapeDtypeStruct((s, f), BF16), jax.ShapeDtypeStruct((s, f), BF16)],
        semantics=("parallel", "parallel"), args=(dx, wd, g, u), exchange=exchange)


def _row_slabs(full):
    return full.reshape(N_DEV, full.shape[0] // N_DEV, full.shape[1])


def _rows_of(slabs):
    return slabs.reshape(N_DEV * slabs.shape[1], slabs.shape[2])


def _ffn_forward(x, norm_w, wg, wu, wd, tag, gather=(), head=()):
    if head:
        n, r, first = _rms_fwd(x, norm_w, f"{tag}_norm", _Exchange("gather", head))
    else:
        (n, r), first = _rms_fwd(x, norm_w, f"{tag}_norm"), []
    if wg is None:
        wg, wu, first = _rows_of(first[0]), _rows_of(first[1]), first[2:]
    (g, u, a), got = _ffn_up(n, wg, wu, f"{tag}_up", _Exchange("gather", gather) if gather else None)
    if wd is None:
        wd, got = _rows_of(got[0]), got[1:]
    y = _matmul([(a, wd)], tm=512, tn=1024, tk=wd.shape[0], name=f"{tag}_down", res=x, alpha=0.5)
    return y, (n, r, g, u, a), (wg, wu, wd), got, first


def _ffn_backward(dy, x, norm_w, wgt, wut, wd, saved, tag, dw_dtype=F32, scatter=None):
    n, r, g, u, a = saved

    def behind(arrays):
        return _Exchange("scatter", arrays) if scatter is not None else None

    def dw(act, grad, name, alpha=None, exchange=None):
        return _matmul([(act, grad)], ta=True, tm=1408, tn=1024, tk=2048, name=name, alpha=alpha, out_dtype=dw_dtype,
                       exchange=exchange)

    dwd = _row_slabs(dw(a, dy, f"{tag}_dwd", alpha=0.5))
    (dg, du), extras = _ffn_dact(dy, wd, g, u, f"{tag}_dact", behind(scatter))
    if scatter is None:
        dwg, dwu = _row_slabs(dw(dg, n, f"{tag}_dwg")), _row_slabs(dw(du, n, f"{tag}_dwu"))
    else:
        dwg, (dwd,) = dw(dg, n, f"{tag}_dwg", exchange=behind([dwd]))
        dwu, (dwg,) = dw(du, n, f"{tag}_dwu", exchange=behind([_row_slabs(dwg)]))
        dwu = _row_slabs(dwu)
    if scatter is None:
        dx, dnorm = _matmul([(dg, wgt), (du, wut)], tm=512, tn=1024, tk=wgt.shape[0], name=f"{tag}_dn",
                            norm_bwd=(x, r, norm_w, dy))
    else:
        dn, (dwu,) = _matmul([(dg, wgt), (du, wut)], tm=512, tn=1024, tk=wgt.shape[0], name=f"{tag}_dn",
                             exchange=behind([dwu]))
        dx, dnorm = _rms_bwd(dn, x, r, norm_w, dy, f"{tag}_dnorm")
    return dx, dnorm, dwg, dwu, dwd, extras


Q_SCALE = GDN_HEAD_DIM ** -0.5
CONV_HALO = 8


def _lane_block(s):
    return pl.BlockSpec((None, s, LANE), lambda j: (j, 0, 0))


def _conv_taps(win, w_ref, rows, sign):
    n = rows + 2 * CONV_HALO
    acc = None
    for t in range(CONV_WIDTH):
        o = sign * (t - CONV_WIDTH // 2)
        sh = win if o == 0 else pltpu.roll(win, (-o) % n, 0)
        term = sh[CONV_HALO:CONV_HALO + rows] * w_ref[t:t + 1, :]
        acc = term if acc is None else acc + term
    return acc


def _gdn_conv_fwd(p_pad, conv_wt):
    s = p_pad.shape[0]
    rows = min(CONV_ROWS, s)
    nblk = QKV_A // LANE

    def body(p_ref, w_ref, c_ref, y_ref, pad):
        j = pl.program_id(0)
        zeros = jnp.zeros((CONV_HALO, LANE), F32)
        pad[0:CONV_HALO, :] = zeros
        pad[CONV_HALO + s:2 * CONV_HALO + s, :] = zeros
        pad[CONV_HALO:CONV_HALO + s, :] = p_ref[...]

        def chunk(ci, carry):
            b = pl.multiple_of(ci * rows, rows)
            win = pad[pl.ds(b, rows + 2 * CONV_HALO), :]
            c = _conv_taps(win, w_ref, rows, 1)
            c_ref[pl.ds(b, rows), :] = c
            act = c * _sigmoid(c)
            nrm = lax.rsqrt(jnp.sum(act * act, axis=-1, keepdims=True) + EPS)
            mult = jnp.where(j < GDN_HEADS, nrm * Q_SCALE, jnp.where(j < 2 * GDN_HEADS, nrm, 1.0))
            y_ref[pl.ds(b, rows), :] = act * mult
            return carry

        lax.fori_loop(0, s // rows, chunk, 0)

    col = pl.BlockSpec((s, LANE), lambda j: (0, j))
    return pl.pallas_call(
        body, name="gdn_conv_fwd", grid=(nblk,),
        in_specs=[col, pl.BlockSpec((8, LANE), lambda j: (0, j))],
        out_specs=[_lane_block(s), _lane_block(s)],
        out_shape=[jax.ShapeDtypeStruct((nblk, s, LANE), F32), jax.ShapeDtypeStruct((nblk, s, LANE), F32)],
        scratch_shapes=[pltpu.VMEM((s + 2 * CONV_HALO, LANE), F32)],
        compiler_params=_params("parallel"),
    )(p_pad, conv_wt)


def _gdn_conv_bwd(dy_f, dy_r, c_pre, p_pad, conv_wt, dp_all):
    s = p_pad.shape[0]
    rows = min(CONV_ROWS, s)
    nblk = QKV_A // LANE

    def body(dyf_ref, dyr_ref, c_ref, p_ref, w_ref, _, dp_ref, dw_ref, ppad, dcpad):
        j = pl.program_id(0)
        zeros = jnp.zeros((CONV_HALO, LANE), F32)
        for buf in (ppad, dcpad):
            buf[0:CONV_HALO, :] = zeros
            buf[CONV_HALO + s:2 * CONV_HALO + s, :] = zeros
        ppad[CONV_HALO:CONV_HALO + s, :] = p_ref[...]

        def act_bwd(ci, carry):
            b = pl.multiple_of(ci * rows, rows)
            c = c_ref[pl.ds(b, rows), :]
            g = dyf_ref[pl.ds(b, rows), :] + dyr_ref[pl.ds(b, rows), :]
            sg = _sigmoid(c)
            act = c * sg
            nrm = lax.rsqrt(jnp.sum(act * act, axis=-1, keepdims=True) + EPS)
            yh = act * nrm
            scale = jnp.where(j < GDN_HEADS, Q_SCALE, 1.0)
            dact_qk = (scale * nrm) * (g - yh * jnp.sum(g * yh, axis=-1, keepdims=True))
            dact = jnp.where(j < 2 * GDN_HEADS, dact_qk, g)
            dcpad[pl.ds(pl.multiple_of(b + CONV_HALO, CONV_HALO), rows), :] = dact * (sg * (1.0 + c * (1.0 - sg)))
            return carry

        lax.fori_loop(0, s // rows, act_bwd, 0)
        tap = lax.broadcasted_iota(jnp.int32, (8, LANE), 0)

        def taps_bwd(ci, dw):
            b = pl.multiple_of(ci * rows, rows)
            dcw = dcpad[pl.ds(b, rows + 2 * CONV_HALO), :]
            dp_ref[pl.ds(b, rows), :] = _conv_taps(dcw, w_ref, rows, -1).astype(BF16)
            pw = ppad[pl.ds(b, rows + 2 * CONV_HALO), :]
            dc = dcw[CONV_HALO:CONV_HALO + rows]
            n = rows + 2 * CONV_HALO
            for t in range(CONV_WIDTH):
                o = t - CONV_WIDTH // 2
                sh = pw if o == 0 else pltpu.roll(pw, (-o) % n, 0)
                row = jnp.sum(dc * sh[CONV_HALO:CONV_HALO + rows], axis=0, keepdims=True)
                dw = dw + jnp.where(tap == t, row, 0.0)
            return dw

        dw_ref[...] = lax.fori_loop(0, s // rows, taps_bwd, jnp.zeros((8, LANE), F32))

    col = pl.BlockSpec((s, LANE), lambda j: (0, j))
    wspec = pl.BlockSpec((8, LANE), lambda j: (0, j))
    return pl.pallas_call(
        body, name="gdn_conv_bwd", grid=(nblk,),
        in_specs=[_lane_block(s), _lane_block(s), _lane_block(s), col, wspec, ANY],
        out_specs=[col, wspec],
        out_shape=[jax.ShapeDtypeStruct(dp_all.shape, dp_all.dtype), jax.ShapeDtypeStruct((8, QKV_A), F32)],
        scratch_shapes=[pltpu.VMEM((s + 2 * CONV_HALO, LANE), F32), pltpu.VMEM((s + 2 * CONV_HALO, LANE), F32)],
        input_output_aliases={5: 0},
        compiler_params=_params("parallel"),
    )(dy_f, dy_r, c_pre, p_pad, conv_wt, dp_all)


def _softplus(x):
    return jnp.maximum(x, 0.0) + jnp.log(1.0 + jnp.exp(-jnp.abs(x)))


def _gdn_gates_fwd(p_pad, alog_row, dt_row):
    s = p_pad.shape[0]
    tm = min(1024, s)

    def body(p_ref, al_ref, dt_ref, o_ref):
        x = p_ref[...]
        lane = lax.broadcasted_iota(jnp.int32, x.shape, 1)
        g = -jnp.exp(al_ref[...]) * _softplus(x + dt_ref[...])
        o_ref[...] = jnp.where(lane < 8, g, jnp.where(lane < 16, _sigmoid(x), 0.0))

    vec = pl.BlockSpec((1, LANE), lambda i: (0, 0))
    return pl.pallas_call(
        body, name="gdn_gates_fwd", grid=(s // tm,),
        in_specs=[pl.BlockSpec((tm, LANE), lambda i: (i, OFF_AB // LANE)), vec, vec],
        out_specs=pl.BlockSpec((tm, LANE), lambda i: (i, 0)),
        out_shape=jax.ShapeDtypeStruct((s, LANE), F32),
        compiler_params=_params("parallel"),
    )(p_pad, alog_row, dt_row)


def _gdn_gates_bwd(dgb_f, dgb_r, p_pad, gb, alog_row, dt_row, dp_all):
    s = p_pad.shape[0]
    tm = min(1024, s)
    tail = N_PAD - OFF_AB

    def body(df_ref, dr_ref, p_ref, gb_ref, al_ref, dt_ref, _, dp_ref, sum_ref):
        @pl.when(pl.program_id(0) == 0)
        def _():
            sum_ref[...] = jnp.zeros_like(sum_ref)

        x = p_ref[...]
        gbv = gb_ref[...]
        dgb = df_ref[...] + dr_ref[...]
        lane = lax.broadcasted_iota(jnp.int32, x.shape, 1)
        da = dgb * (-jnp.exp(al_ref[...])) * _sigmoid(x + dt_ref[...])
        db = dgb * gbv * (1.0 - gbv)
        dp_ref[:, 0:LANE] = jnp.where(lane < 8, da, jnp.where(lane < 16, db, 0.0)).astype(BF16)
        dp_ref[:, LANE:tail] = jnp.zeros((tm, tail - LANE), BF16)
        row = lax.broadcasted_iota(jnp.int32, (8, LANE), 0)
        lane8 = lax.broadcasted_iota(jnp.int32, (8, LANE), 1)
        d_alog = jnp.sum(dgb * gbv, axis=0, keepdims=True)
        d_dt = jnp.sum(da, axis=0, keepdims=True)
        upd = jnp.where(row == 0, d_alog, jnp.where(row == 1, d_dt, 0.0))
        sum_ref[...] += jnp.where(lane8 < 8, upd, 0.0)

    vec = pl.BlockSpec((1, LANE), lambda i: (0, 0))
    blk = pl.BlockSpec((tm, LANE), lambda i: (i, 0))
    return pl.pallas_call(
        body, name="gdn_gates_bwd", grid=(s // tm,),
        in_specs=[blk, blk, pl.BlockSpec((tm, LANE), lambda i: (i, OFF_AB // LANE)), blk, vec, vec, ANY],
        out_specs=[pl.BlockSpec((tm, tail), lambda i: (i, OFF_AB // tail)), pl.BlockSpec((8, LANE), lambda i: (0, 0))],
        out_shape=[jax.ShapeDtypeStruct(dp_all.shape, dp_all.dtype), jax.ShapeDtypeStruct((8, LANE), F32)],
        input_output_aliases={6: 0},
        compiler_params=_params("arbitrary"),
    )(dgb_f, dgb_r, p_pad, gb, alog_row, dt_row, dp_all)


def _chunk_masks(rev):
    row = lax.broadcasted_iota(jnp.int32, (CHUNK, CHUNK), 0)
    col = lax.broadcasted_iota(jnp.int32, (CHUNK, CHUNK), 1)
    le = (col >= row) if rev else (col <= row)
    strict = (col > row) if rev else (col < row)
    return le, strict, row == col


def _gate_lanes(rev, h):
    d = 1 if rev else 0
    return d * GDN_HEADS + h, 8 + d * GDN_HEADS + h


BNN = (((2,), (1,)), ((0,), (0,)))
BNT = (((2,), (2,)), ((0,), (0,)))
BTN = (((1,), (1,)), ((0,), (0,)))
NB = 2 * GDN_HEADS
DELTA_CHUNKS = 8


def _bdot(a, b, dn=BNN):
    return lax.dot_general(a.astype(BF16), b.astype(BF16), dn, preferred_element_type=F32)


def _dot3(a, b, dn, exact_a=False, exact_b=False):
    def d(x, y):
        return lax.dot_general(x, y, dn, preferred_element_type=F32)

    ah = a.astype(BF16)
    bh = b.astype(BF16)
    out = d(ah, bh)
    if not exact_b:
        out = out + d(ah, (b - bh.astype(F32)).astype(BF16))
    if not exact_a:
        out = out + d((a - ah.astype(F32)).astype(BF16), bh)
    return out


def _both(f_val, r_val):
    return jnp.stack([f_val] * GDN_HEADS + [r_val] * GDN_HEADS)


def _head_blocks(ref_f, ref_r, rows_f, rows_r):
    return jnp.concatenate([ref_f[:, rows_f, :], ref_r[:, rows_r, :]], axis=0)


def _chunk_rows(c):
    return slice(c * CHUNK, (c + 1) * CHUNK), slice((DELTA_CHUNKS - 1 - c) * CHUNK, (DELTA_CHUNKS - c) * CHUNK)


def _heads(ref_f, ref_r, rows_f, rows_r):
    hd = GDN_HEAD_DIM
    return jnp.stack([ref_f[rows_f, h * hd:(h + 1) * hd] for h in range(GDN_HEADS)]
                     + [ref_r[rows_r, h * hd:(h + 1) * hd] for h in range(GDN_HEADS)])


def _gate_cols(tile_f, tile_r, base):
    return jnp.stack([tile_f[:, base + h:base + h + 1] for h in range(GDN_HEADS)]
                     + [tile_r[:, base + GDN_HEADS + h:base + GDN_HEADS + h + 1] for h in range(GDN_HEADS)])


def _chunk_common2(q, k, v, gbf, gbr):
    mf, mr = _chunk_masks(False), _chunk_masks(True)
    le, strict = _both(mf[0], mr[0]), _both(mf[1], mr[1])
    eye = mf[2]
    gcm_f = _dot3(mf[0].astype(F32), gbf, NN, exact_a=True)
    gcm_r = _dot3(mr[0].astype(F32), gbr, NN, exact_a=True)
    g, beta, gc = _gate_cols(gbf, gbr, 0), _gate_cols(gbf, gbr, 8), _gate_cols(gcm_f, gcm_r, 0)
    gc_row = _dot3(jnp.ones((NB, CHUNK, CHUNK), F32), jnp.where(eye[None], gc, 0.0), BNN, exact_a=True)
    decay = jnp.where(le, jnp.exp(jnp.where(le, gc - gc_row, 0.0)), 0.0)
    eg = jnp.exp(gc)
    gl = jnp.sum(g, axis=1, keepdims=True)
    kb = k * beta
    vb = v * beta
    kbeg = kb * eg
    lm = jnp.where(strict, _bdot(kb, k, BNT) * decay, 0.0)
    intra = _bdot(q, k, BNT) * decay
    edec = jnp.exp(gl - gc)
    return dict(strict=strict, eye=eye, beta=beta, decay=decay, eg=eg, gl=gl, kb=kb, vb=vb, kbeg=kbeg,
                lm=lm, intra=intra, qg=q * eg, edec=edec, kdec=k * edec)


def _unit_triangular_inverse(lm, eye):
    x = -lm
    t = eye[None].astype(F32) + x
    p = x
    for level in range(5):
        prod = functools.partial(_dot3, dn=BNN) if level < 2 else _bdot
        p = prod(p, p)
        t = t + prod(t, p)
    return t


def _delta_fwd2(y, gb, gather=()):
    s = y.shape[1]
    nc = s // CHUNK
    hd = GDN_HEAD_DIM
    na = len(gather)

    def body(*refs):
        qf, kf, vf, gf, qr, kr, vr, gr = refs[:8]
        of_ref, or_ref, sf_all, sr_all, tf_all, tr_all = refs[8 + na:14 + na]
        state = refs[14 + 2 * na]
        step = pl.program_id(0)

        @pl.when(step == 0)
        def _():
            state[...] = jnp.zeros_like(state)

        if na:
            start, forward, finish = _gather_phases(refs[8:8 + na], refs[14 + na:14 + 2 * na], *refs[15 + 2 * na:])
            pl.when(step == 0)(start)
            pl.when(step == ns // 2)(forward)
            pl.when(step == ns - 1)(finish)

        st = state[...]
        for c in range(DELTA_CHUNKS):
            rf, rr = _chunk_rows(c)
            q, k, v = _head_blocks(qf, qr, rf, rr), _head_blocks(kf, kr, rf, rr), _head_blocks(vf, vr, rf, rr)
            cm = _chunk_common2(q, k, v, gf[rf, :], gr[rr, :])
            tinv = _unit_triangular_inverse(cm["lm"], cm["eye"])
            u = _bdot(tinv, cm["vb"])
            w = _bdot(tinv, cm["kbeg"])
            v_new = u - _bdot(w, st)
            o = _bdot(cm["qg"], st) + _bdot(cm["intra"], v_new)
            for h in range(GDN_HEADS):
                of_ref[rf, h * hd:(h + 1) * hd] = o[h]
                or_ref[rr, h * hd:(h + 1) * hd] = o[GDN_HEADS + h]
            sf_all[c] = st[:GDN_HEADS]
            sr_all[DELTA_CHUNKS - 1 - c] = st[GDN_HEADS:]
            tf_all[c] = tinv[:GDN_HEADS]
            tr_all[DELTA_CHUNKS - 1 - c] = tinv[GDN_HEADS:]
            st = st * jnp.exp(cm["gl"]) + _bdot(cm["kdec"], v_new, BTN)
        state[...] = st

    rows = DELTA_CHUNKS * CHUNK
    ns = nc // DELTA_CHUNKS

    def col(j, rev):
        return pl.BlockSpec((GDN_HEADS, rows, hd), (lambda n: (j, ns - 1 - n, 0)) if rev else (lambda n: (j, n, 0)))

    def out(rev):
        return pl.BlockSpec((rows, GDN_WIDTH), (lambda n: (ns - 1 - n, 0)) if rev else (lambda n: (n, 0)))

    def gate(rev):
        return pl.BlockSpec((rows, LANE), (lambda n: (ns - 1 - n, 0)) if rev else (lambda n: (n, 0)))

    def per_chunk(d1, d2, rev):
        return pl.BlockSpec((DELTA_CHUNKS, GDN_HEADS, d1, d2),
                            (lambda n: (ns - 1 - n, 0, 0, 0)) if rev else (lambda n: (n, 0, 0, 0)))

    assert nc % DELTA_CHUNKS == 0 and (na == 0 or ns >= 4)
    res = pl.pallas_call(
        body, name="delta_fwd", grid=(ns,),
        in_specs=[col(0, False), col(1, False), col(2, False), gate(False), col(0, True), col(1, True), col(2, True), gate(True)]
        + [ANY] * na,
        out_specs=[out(False), out(True), per_chunk(hd, hd, False), per_chunk(hd, hd, True),
                   per_chunk(CHUNK, CHUNK, False), per_chunk(CHUNK, CHUNK, True)] + [ANY] * na,
        out_shape=[jax.ShapeDtypeStruct((s, GDN_WIDTH), F32)] * 2 + [jax.ShapeDtypeStruct((nc, GDN_HEADS, hd, hd), F32)] * 2
        + [jax.ShapeDtypeStruct((nc, GDN_HEADS, CHUNK, CHUNK), F32)] * 2
        + [jax.ShapeDtypeStruct((N_DEV,) + v.shape, v.dtype) for v in gather],
        scratch_shapes=[pltpu.VMEM((NB, hd, hd), F32)] + (_gather_semaphores(na) if na else []),
        compiler_params=_params("arbitrary"),
    )(y, y, y, gb, y, y, y, gb, *gather)
    return res[:6], res[6:]


def _delta_bwd2(y, gb, do, sf_all, sr_all, tf_all, tr_all, scatter=()):
    s = y.shape[1]
    nc = s // CHUNK
    hd = GDN_HEAD_DIM
    na = len(scatter)

    def body(*refs):
        qf, kf, vf, gf, dof, sf, tf, qr, kr, vr, gr, dor, sr, tr = refs[:14]
        dyf_ref, dyr_ref, dgf_ref, dgr_ref = refs[14 + na:18 + na]
        dstate = refs[18 + 2 * na]
        step = pl.program_id(0)

        @pl.when(step == 0)
        def _():
            dstate[...] = jnp.zeros_like(dstate)

        if na:
            start, finish = _scatter_phases(refs[14:14 + na], refs[18 + na:18 + 2 * na], *refs[19 + 2 * na:])
            pl.when(step == 0)(start)
            pl.when(step == ns - 1)(finish)

        def one_chunk(c, ds_out):
            rr, rf = _chunk_rows(c)
            cf, cr = DELTA_CHUNKS - 1 - c, c
            q, k, v = _head_blocks(qf, qr, rf, rr), _head_blocks(kf, kr, rf, rr), _head_blocks(vf, vr, rf, rr)
            dov = _heads(dof, dor, rf, rr)
            cm = _chunk_common2(q, k, v, gf[rf, :], gr[rr, :])
            tinv = jnp.concatenate([tf[cf], tr[cr]], axis=0)
            st = jnp.concatenate([sf[cf], sr[cr]], axis=0)
            decay, lm, intra, qg, kdec, kbeg, eg, kb, beta = (
                cm[n] for n in ("decay", "lm", "intra", "qg", "kdec", "kbeg", "eg", "kb", "beta"))
            u = _bdot(tinv, cm["vb"])
            w = _bdot(tinv, kbeg)
            v_new = u - _bdot(w, st)
            egl = jnp.exp(cm["gl"])
            d_qg = _bdot(dov, st, BNT)
            d_intra = _bdot(dov, v_new, BNT)
            dv_new = _bdot(intra, dov, BTN) + _bdot(kdec, ds_out)
            d_kdec = _bdot(v_new, ds_out, BNT)
            ds_in = _bdot(qg, dov, BTN) + egl * ds_out - _bdot(w, dv_new, BTN)
            dgl = egl * jnp.sum(jnp.sum(st * ds_out, axis=2, keepdims=True), axis=1, keepdims=True)
            dw = -_bdot(dv_new, st, BNT)
            dvb = _bdot(tinv, dv_new, BTN)
            dkbeg = _bdot(tinv, dw, BTN)
            dlm = jnp.where(cm["strict"], -(_bdot(dvb, u, BNT) + _bdot(dkbeg, w, BNT)), 0.0)
            d_a = dlm * decay
            d_qk = d_intra * decay
            e = dlm * lm + d_intra * intra
            colsum = _dot3(e, jnp.ones((NB, CHUNK, LANE), F32), BTN, exact_b=True)[:, :, 0:1]
            dgc = jnp.sum(e, axis=2, keepdims=True) - colsum
            dkb = _bdot(d_a, k) + dkbeg * eg
            dk = _bdot(d_a, kb, BTN) + _bdot(d_qk, q, BTN)
            dq = _bdot(d_qk, k) + d_qg * eg
            dgc = dgc + jnp.sum(d_qg * qg, axis=2, keepdims=True) + jnp.sum(dkbeg * kbeg, axis=2, keepdims=True)
            tdec = jnp.sum(d_kdec * kdec, axis=2, keepdims=True)
            dk = dk + d_kdec * cm["edec"] + dkb * beta
            dgc = dgc - tdec
            dgl = dgl + jnp.sum(tdec, axis=1, keepdims=True)
            dbeta = jnp.sum(dvb * v, axis=2, keepdims=True) + jnp.sum(dkb * k, axis=2, keepdims=True)
            dv = dvb * beta
            lane = lax.broadcasted_iota(jnp.int32, (CHUNK, LANE), 1)
            for rev, dy_ref, dg_ref, rows in ((False, dyf_ref, dgf_ref, rf), (True, dyr_ref, dgr_ref, rr)):
                dgc_tile = jnp.zeros((CHUNK, LANE), F32)
                rest = jnp.zeros((CHUNK, LANE), F32)
                for h in range(GDN_HEADS):
                    b = (GDN_HEADS if rev else 0) + h
                    gi, bi = _gate_lanes(rev, h)
                    dgc_tile = dgc_tile + jnp.where(lane == gi, dgc[b], 0.0)
                    rest = rest + jnp.where(lane == gi, dgl[b], 0.0) + jnp.where(lane == bi, dbeta[b], 0.0)
                    dy_ref[h, rows, :] = dq[b]
                    dy_ref[GDN_HEADS + h, rows, :] = dk[b]
                    dy_ref[2 * GDN_HEADS + h, rows, :] = dv[b]
                le_t = _chunk_masks(not rev)[0].astype(F32)
                dg_ref[rows, :] = _dot3(le_t, dgc_tile, NN, exact_a=True) + rest
            return ds_in

        ds = dstate[...]
        for c in range(DELTA_CHUNKS):
            ds = one_chunk(c, ds)
        dstate[...] = ds

    rows_per_step = DELTA_CHUNKS * CHUNK
    ns = nc // DELTA_CHUNKS

    def col(j, rev, blocks=GDN_HEADS):
        return pl.BlockSpec((blocks, rows_per_step, hd), (lambda n: (j, n, 0)) if rev else (lambda n: (j, ns - 1 - n, 0)))

    def wide(width, rev):
        return pl.BlockSpec((rows_per_step, width), (lambda n: (n, 0)) if rev else (lambda n: (ns - 1 - n, 0)))

    def per_chunk(d1, d2, rev):
        return pl.BlockSpec((DELTA_CHUNKS, GDN_HEADS, d1, d2),
                            (lambda n: (n, 0, 0, 0)) if rev else (lambda n: (ns - 1 - n, 0, 0, 0)))

    def side(rev):
        return [col(0, rev), col(1, rev), col(2, rev), wide(LANE, rev), wide(GDN_WIDTH, rev), per_chunk(hd, hd, rev),
                per_chunk(CHUNK, CHUNK, rev)]

    assert nc % DELTA_CHUNKS == 0 and (na == 0 or ns >= 2)
    res = pl.pallas_call(
        body, name="delta_bwd", grid=(ns,),
        in_specs=side(False) + side(True) + [ANY] * na,
        out_specs=[col(0, False, 3 * GDN_HEADS), col(0, True, 3 * GDN_HEADS), wide(LANE, False), wide(LANE, True)]
        + [ANY] * na,
        out_shape=[jax.ShapeDtypeStruct((3 * GDN_HEADS, s, hd), F32)] * 2 + [jax.ShapeDtypeStruct((s, LANE), F32)] * 2
        + [jax.ShapeDtypeStruct(g.shape, g.dtype) for g in scatter],
        scratch_shapes=[pltpu.VMEM((NB, hd, hd), F32)] + (_gather_semaphores(na) if na else []),
        compiler_params=_params("arbitrary"),
    )(y, y, y, gb, do, sf_all, tf_all, y, y, y, gb, do, sr_all, tr_all, *scatter)
    return res[:4], res[4:]


def _gdn_post_fwd(o_f, o_r, p_pad, norm_row):
    s = o_f.shape[0]
    tm = min(512, s)
    hd = GDN_HEAD_DIM

    def body(of_ref, or_ref, z_ref, w_ref, out_ref, osum_ref):
        o = of_ref[...] + or_ref[...]
        osum_ref[...] = o
        z = z_ref[...]
        gate = z * _sigmoid(z)
        for h in range(GDN_HEADS):
            sl = slice(h * hd, (h + 1) * hd)
            oh = o[:, sl]
            r = lax.rsqrt(jnp.mean(oh * oh, axis=-1, keepdims=True) + EPS)
            out_ref[:, sl] = (oh * r * w_ref[...] * gate[:, sl]).astype(BF16)

    blk = pl.BlockSpec((tm, GDN_WIDTH), lambda i: (i, 0))
    return pl.pallas_call(
        body, name="gdn_post_fwd", grid=(s // tm,),
        in_specs=[blk, blk, pl.BlockSpec((tm, GDN_WIDTH), lambda i: (i, OFF_Z // GDN_WIDTH)),
                  pl.BlockSpec((1, hd), lambda i: (0, 0))],
        out_specs=[blk, blk],
        out_shape=[jax.ShapeDtypeStruct((s, GDN_WIDTH), BF16), jax.ShapeDtypeStruct((s, GDN_WIDTH), F32)],
        compiler_params=_params("parallel"),
    )(o_f, o_r, p_pad, norm_row)


def _gdn_post_bwd(d_out, o_sum, p_pad, norm_row):
    s = o_sum.shape[0]
    tm = min(512, s)
    hd = GDN_HEAD_DIM

    def body(d_ref, o_ref, z_ref, w_ref, do_ref, dz_ref, dw_ref):
        @pl.when(pl.program_id(0) == 0)
        def _():
            dw_ref[...] = jnp.zeros_like(dw_ref)

        z = z_ref[...]
        sg = _sigmoid(z)
        gate = z * sg
        dgate = sg * (1.0 + z * (1.0 - sg))
        wv = w_ref[...]
        dw = jnp.zeros((1, hd), F32)
        for h in range(GDN_HEADS):
            sl = slice(h * hd, (h + 1) * hd)
            oh = o_ref[:, sl]
            dh = d_ref[:, sl]
            r = lax.rsqrt(jnp.mean(oh * oh, axis=-1, keepdims=True) + EPS)
            ohat = oh * r
            dz_ref[:, sl] = (dh * ohat * wv * dgate[:, sl]).astype(BF16)
            drn = dh * gate[:, sl]
            t = drn * wv
            do_ref[:, sl] = r * (t - ohat * jnp.mean(t * ohat, axis=-1, keepdims=True))
            dw = dw + jnp.sum(drn * ohat, axis=0, keepdims=True)
        dw_ref[...] += dw

    blk = pl.BlockSpec((tm, GDN_WIDTH), lambda i: (i, 0))
    vec = pl.BlockSpec((1, hd), lambda i: (0, 0))
    return pl.pallas_call(
        body, name="gdn_post_bwd", grid=(s // tm,),
        in_specs=[blk, blk, pl.BlockSpec((tm, GDN_WIDTH), lambda i: (i, OFF_Z // GDN_WIDTH)), vec],
        out_specs=[blk, pl.BlockSpec((tm, GDN_WIDTH), lambda i: (i, OFF_Z // GDN_WIDTH)), vec],
        out_shape=[jax.ShapeDtypeStruct((s, GDN_WIDTH), F32), jax.ShapeDtypeStruct((s, N_PAD), BF16),
                   jax.ShapeDtypeStruct((1, hd), F32)],
        compiler_params=_params("arbitrary"),
    )(d_out, o_sum, p_pad, norm_row)


def _gdn_forward(p_pad, conv_wt, alog_row, dt_row, norm_row, gather=()):
    c_pre, y = _gdn_conv_fwd(p_pad, conv_wt)
    gb = _gdn_gates_fwd(p_pad, alog_row, dt_row)
    (o_f, o_r, s_f, s_r, t_f, t_r), gathered = _delta_fwd2(y, gb, gather)
    out, o_sum = _gdn_post_fwd(o_f, o_r, p_pad, norm_row)
    return out, (c_pre, y, gb, s_f, t_f, s_r, t_r, o_sum), gathered


def _gdn_backward(d_out, p_pad, conv_wt, alog_row, dt_row, norm_row, saved, scatter=()):
    c_pre, y, gb, s_f, t_f, s_r, t_r, o_sum = saved
    do, dp_all, dnorm = _gdn_post_bwd(d_out, o_sum, p_pad, norm_row)
    (dy_f, dy_r, dgb_f, dgb_r), received = _delta_bwd2(y, gb, do, s_f, s_r, t_f, t_r, scatter)
    dp_all, dconv = _gdn_conv_bwd(dy_f, dy_r, c_pre, p_pad, conv_wt, dp_all)
    dp_all, gate_sums = _gdn_gates_bwd(dgb_f, dgb_r, p_pad, gb, alog_row, dt_row, dp_all)
    return dp_all, dconv, gate_sums, dnorm, received


ATT_BK = ATT_BQ + 2 * ATT_HALO
ATT_SUB = 8
SWA_SCALE = SWA_HEAD_DIM ** -0.5


def _t5_bucket(rel):
    nb = REL_BUCKETS // 2
    bucket = (rel > 0).astype(np.int32) * nb
    n = np.abs(rel)
    max_exact = nb // 2
    large = max_exact + (np.log(np.maximum(n, 1) / max_exact)
                         / math.log(REL_MAX_DISTANCE / max_exact) * (nb - max_exact)).astype(np.int32)
    large = np.minimum(large, nb - 1)
    return (bucket + np.where(n < max_exact, n, large)).astype(np.int32)


def _band_tables(dilation, queries_are_rows_of_block):
    blk = np.arange(ATT_BQ)
    band = np.arange(ATT_BK) - ATT_HALO
    if queries_are_rows_of_block:
        rel = band[None, :] - blk[:, None]
        band_idx = np.broadcast_to(np.arange(ATT_BK)[None, :], rel.shape)
    else:
        rel = blk[None, :] - band[:, None]
        band_idx = np.broadcast_to(np.arange(ATT_BK)[:, None], rel.shape)
    base = np.abs(rel) <= ATT_HALO
    not_prev = band_idx >= ATT_HALO
    not_next = band_idx < ATT_HALO + ATT_BQ
    valid = np.stack([base & not_prev, base, base & not_next, base & not_prev & not_next])
    return valid, _t5_bucket(rel * dilation)


def _bias_tiles(rel_bias, dilation, queries_are_rows_of_block):
    valid, bucket = _band_tables(dilation, queries_are_rows_of_block)
    onehot = (jnp.asarray(bucket.reshape(-1, 1)) == jnp.arange(REL_BUCKETS, dtype=jnp.int32)[None, :]).astype(F32)
    rb = jnp.dot(onehot, rel_bias.astype(F32), precision=lax.Precision.HIGHEST)
    rb = rb.T.reshape((SWA_HEADS,) + bucket.shape)
    return jnp.where(valid[:, None], rb[None], NEG_BIG).astype(F32)


def _group_sum(x, bd):
    hi = x.astype(BF16)
    lo = (x - hi.astype(F32)).astype(BF16)
    return jnp.dot(hi, bd, preferred_element_type=F32) + jnp.dot(lo, bd, preferred_element_type=F32)


def _head_block_diag():
    idx = np.arange(SWA_WIDTH) // SWA_HEAD_DIM
    return jnp.asarray(idx[:, None] == idx[None, :], BF16)


def _swa_pre_fwd(p_pad, qw_row, kw_row, bd):
    s = p_pad.shape[0]
    tm = min(512, s)
    inv = 1.0 / SWA_HEAD_DIM

    def body(q_ref, k_ref, v_ref, qw_ref, kw_ref, bd_ref, qo_ref, ko_ref, vo_ref):
        bdv = bd_ref[...]
        q = q_ref[...]
        k = k_ref[...]
        rq = lax.rsqrt(_group_sum(q * q, bdv) * inv + EPS)
        rk = lax.rsqrt(_group_sum(k * k, bdv) * inv + EPS)
        qo_ref[...] = (q * rq * qw_ref[...] * SWA_SCALE).astype(BF16)
        ko_ref[...] = (k * rk * kw_ref[...]).astype(BF16)
        vo_ref[...] = v_ref[...].astype(BF16)

    base = OFF_B // SWA_WIDTH
    blk = pl.BlockSpec((tm, SWA_WIDTH), lambda i: (i, 0))
    vec = pl.BlockSpec((1, SWA_WIDTH), lambda i: (0, 0))
    return pl.pallas_call(
        body, name="swa_pre_fwd", grid=(s // tm,),
        in_specs=[pl.BlockSpec((tm, SWA_WIDTH), lambda i: (i, base)), pl.BlockSpec((tm, SWA_WIDTH), lambda i: (i, base + 1)),
                  pl.BlockSpec((tm, SWA_WIDTH), lambda i: (i, base + 2)), vec, vec,
                  pl.BlockSpec((SWA_WIDTH, SWA_WIDTH), lambda i: (0, 0))],
        out_specs=[blk, blk, blk],
        out_shape=[jax.ShapeDtypeStruct((s, SWA_WIDTH), BF16)] * 3,
        compiler_params=_params("parallel"),
    )(p_pad, p_pad, p_pad, qw_row, kw_row, bd)


def _swa_pre_bwd(dqs, dks, dvs, p_pad, qw_row, kw_row, bd, dp_all):
    s = p_pad.shape[0]
    tm = min(256, s)
    inv = 1.0 / SWA_HEAD_DIM
    npat = len(dqs)

    def body(*refs):
        dq_refs, dk_refs, dv_refs = refs[:npat], refs[npat:2 * npat], refs[2 * npat:3 * npat]
        q_ref, k_ref, qw_ref, kw_ref, bd_ref, _, dp_ref, dqw_ref, dkw_ref = refs[3 * npat:]

        @pl.when(pl.program_id(0) == 0)
        def _():
            dqw_ref[...] = jnp.zeros_like(dqw_ref)
            dkw_ref[...] = jnp.zeros_like(dkw_ref)

        bdv = bd_ref[...]

        def norm_bwd(x, g, w, scale):
            r = lax.rsqrt(_group_sum(x * x, bdv) * inv + EPS)
            xhat = x * r
            t = g * w * scale
            dx = r * (t - xhat * (_group_sum(t * xhat, bdv) * inv))
            return dx, jnp.sum(g * scale * xhat, axis=0, keepdims=True)

        def total(rs):
            t = rs[0][...].astype(F32)
            for r in rs[1:]:
                t = t + r[...].astype(F32)
            return t

        dq, dqw = norm_bwd(q_ref[...], total(dq_refs), qw_ref[...], SWA_SCALE)
        dk, dkw = norm_bwd(k_ref[...], total(dk_refs), kw_ref[...], 1.0)
        dp_ref[:, 0:SWA_WIDTH] = dq.astype(BF16)
        dp_ref[:, SWA_WIDTH:2 * SWA_WIDTH] = dk.astype(BF16)
        dp_ref[:, 2 * SWA_WIDTH:3 * SWA_WIDTH] = total(dv_refs).astype(BF16)
        dqw_ref[...] += dqw
        dkw_ref[...] += dkw

    base = OFF_B // SWA_WIDTH
    blk = pl.BlockSpec((tm, SWA_WIDTH), lambda i: (i, 0))
    vec = pl.BlockSpec((1, SWA_WIDTH), lambda i: (0, 0))
    return pl.pallas_call(
        body, name="swa_pre_bwd", grid=(s // tm,),
        in_specs=[blk] * (3 * npat) + [pl.BlockSpec((tm, SWA_WIDTH), lambda i: (i, base)),
                                      pl.BlockSpec((tm, SWA_WIDTH), lambda i: (i, base + 1)), vec, vec,
                                      pl.BlockSpec((SWA_WIDTH, SWA_WIDTH), lambda i: (0, 0)), ANY],
        out_specs=[pl.BlockSpec((tm, 3 * SWA_WIDTH), lambda i: (i, OFF_B // (3 * SWA_WIDTH))), vec, vec],
        out_shape=[jax.ShapeDtypeStruct(dp_all.shape, dp_all.dtype), jax.ShapeDtypeStruct((1, SWA_WIDTH), F32),
                   jax.ShapeDtypeStruct((1, SWA_WIDTH), F32)],
        input_output_aliases={3 * npat + 5: 0},
        compiler_params=_params("arbitrary"),
    )(*dqs, *dks, *dvs, p_pad, p_pad, qw_row, kw_row, bd, dp_all)


def _band_specs(length, rows):
    per = rows // ATT_HALO
    last = length // ATT_HALO - 1
    prev = pl.BlockSpec((ATT_HALO, SWA_WIDTH), lambda r, t: (jnp.maximum(t * per - 1, 0), r))
    cur = pl.BlockSpec((rows, SWA_WIDTH), lambda r, t: (t, r))
    nxt = pl.BlockSpec((ATT_HALO, SWA_WIDTH), lambda r, t: (jnp.minimum((t + 1) * per, last), r))
    return [prev, cur, nxt]


def _tile_variant(t, nb, u, sub):
    first, last = u == 0, u == sub - 1
    if first and last:
        return 3 if nb == 1 else jnp.where(t == 0, 0, jnp.where(t == nb - 1, 2, 1))
    if first:
        return jnp.where(t == 0, 0, 1)
    if last:
        return jnp.where(t == nb - 1, 2, 1)
    return 1


def _bias_specs(nb, sub, rows, cols):
    return [pl.BlockSpec((1, SWA_HEADS, rows, cols),
                         functools.partial(lambda r, t, u: (_tile_variant(t, nb, u, sub), 0, 0, 0), u=u))
            for u in range(sub)]


def _band(refs):
    return jnp.concatenate([r[...] for r in refs], axis=0)


def _sub(u, width=ATT_BQ):
    return slice(u * ATT_BQ, u * ATT_BQ + width)


N_PAIRS = SWA_HEADS // 2


def _pairs(x):
    return jnp.stack([x[:, LANE * p:LANE * (p + 1)] for p in range(N_PAIRS)])


def _per_head_rows(x):
    first = lax.broadcasted_iota(jnp.int32, x.shape, 2) < SWA_HEAD_DIM
    zero = jnp.zeros_like(x)
    return jnp.concatenate([jnp.where(first, x, zero), jnp.where(first, zero, x)], axis=1)


def _per_head_cols(x):
    return jnp.stack([jnp.concatenate([x[:, LANE * p:LANE * p + 1],
                                       x[:, LANE * p + SWA_HEAD_DIM:LANE * p + SWA_HEAD_DIM + 1]], axis=0)
                      for p in range(N_PAIRS)])


def _merge_heads(x, rows):
    first = lax.broadcasted_iota(jnp.int32, (N_PAIRS, rows, LANE), 2) < SWA_HEAD_DIM
    return jnp.where(first, x[:, :rows], x[:, rows:])


def _store_pairs(ref, x, rows):
    for p in range(N_PAIRS):
        ref[rows, LANE * p:LANE * (p + 1)] = x[p].astype(ref.dtype)


def _att_fwd2(q, k, v, bias, dilation):
    s = q.shape[0]
    length = s // dilation
    sub = min(ATT_SUB, length // ATT_BQ)
    rows = sub * ATT_BQ
    nb = length // rows
    view = (length, dilation * SWA_WIDTH)

    def body(q_ref, kp, kc, kn, vp, vc, vn, *rest):
        b_refs, (o_ref, lse_ref) = rest[:sub], rest[sub:]
        kwin, vwin = _band((kp, kc, kn)), _band((vp, vc, vn))
        for u in range(sub):
            kb, vb = _pairs(kwin[_sub(u, ATT_BK)]), _pairs(vwin[_sub(u, ATT_BK)])
            qm = _per_head_rows(_pairs(q_ref[_sub(u), :]))
            sc = _bdot(qm, kb, BNT) + b_refs[u][0].reshape(N_PAIRS, 2 * ATT_BQ, ATT_BK)
            m = jnp.max(sc, axis=-1, keepdims=True)
            p = jnp.exp(sc - m)
            den = jnp.sum(p, axis=-1, keepdims=True)
            o = _bdot(p, vb) / den
            _store_pairs(o_ref, _merge_heads(o, ATT_BQ), _sub(u))
            lse = jnp.broadcast_to(m + jnp.log(den), (N_PAIRS, 2 * ATT_BQ, LANE))
            _store_pairs(lse_ref, _merge_heads(lse, ATT_BQ), _sub(u))

    cur = pl.BlockSpec((rows, SWA_WIDTH), lambda r, t: (t, r))
    o, lse = pl.pallas_call(
        body, name=f"att_fwd_d{dilation}", grid=(dilation, nb),
        in_specs=[cur] + _band_specs(length, rows) * 2 + _bias_specs(nb, sub,ATT_BQ, ATT_BK),
        out_specs=[cur, cur],
        out_shape=[jax.ShapeDtypeStruct(view, BF16), jax.ShapeDtypeStruct(view, F32)],
        compiler_params=_params("parallel", "parallel"),
    )(q.reshape(view), *([k.reshape(view)] * 3), *([v.reshape(view)] * 3), *([bias] * sub))
    return o.reshape(s, SWA_WIDTH), lse.reshape(s, SWA_WIDTH)


def _att_dq2(q, k, v, dop, lse, cp, bias, dilation):
    s = q.shape[0]
    length = s // dilation
    sub = min(ATT_SUB, length // ATT_BQ)
    rows = sub * ATT_BQ
    nb = length // rows
    view = (length, dilation * SWA_WIDTH)

    def body(q_ref, kp, kc, kn, vp, vc, vn, do_ref, lse_ref, cp_ref, *rest):
        b_refs, (dq_ref, db_ref) = rest[:sub], rest[sub:]

        @pl.when((pl.program_id(0) == 0) & (pl.program_id(1) == 0))
        def _():
            db_ref[...] = jnp.zeros_like(db_ref)

        kwin, vwin = _band((kp, kc, kn)), _band((vp, vc, vn))
        for u in range(sub):
            kb, vb = _pairs(kwin[_sub(u, ATT_BK)]), _pairs(vwin[_sub(u, ATT_BK)])
            qm = _per_head_rows(_pairs(q_ref[_sub(u), :]))
            dom = _per_head_rows(_pairs(do_ref[_sub(u), :]))
            sc = _bdot(qm, kb, BNT) + b_refs[u][0].reshape(N_PAIRS, 2 * ATT_BQ, ATT_BK)
            p = jnp.exp(sc - _per_head_cols(lse_ref[_sub(u), :]))
            ds = p * (_bdot(dom, vb, BNT) + _per_head_cols(cp_ref[_sub(u), :]))
            _store_pairs(dq_ref, _merge_heads(_bdot(ds, kb), ATT_BQ), _sub(u))
            db_ref[_tile_variant(pl.program_id(1), nb, u, sub)] += ds.reshape(SWA_HEADS, ATT_BQ, ATT_BK)

    cur = pl.BlockSpec((rows, SWA_WIDTH), lambda r, t: (t, r))
    dq, db = pl.pallas_call(
        body, name=f"att_dq_d{dilation}", grid=(dilation, nb),
        in_specs=[cur] + _band_specs(length, rows) * 2 + [cur, cur, cur] + _bias_specs(nb, sub,ATT_BQ, ATT_BK),
        out_specs=[cur, pl.BlockSpec((4, SWA_HEADS, ATT_BQ, ATT_BK), lambda r, t: (0, 0, 0, 0))],
        out_shape=[jax.ShapeDtypeStruct(view, BF16), jax.ShapeDtypeStruct((4, SWA_HEADS, ATT_BQ, ATT_BK), F32)],
        compiler_params=_params("arbitrary", "arbitrary"),
    )(q.reshape(view), *([k.reshape(view)] * 3), *([v.reshape(view)] * 3), dop.reshape(view), lse.reshape(view),
      cp.reshape(view), *([bias] * sub))
    return dq.reshape(s, SWA_WIDTH), db


def _att_dkv2(q, k, v, dop, lse, cp, bias_t, dilation):
    s = q.shape[0]
    length = s // dilation
    sub = min(ATT_SUB, length // ATT_BQ)
    rows = sub * ATT_BQ
    nb = length // rows
    view = (length, dilation * SWA_WIDTH)

    def body(k_ref, v_ref, qp, qc, qn, dp_, dc_, dn_, lp, lc, ln, cp_, cc_, cn_, *rest):
        b_refs, (dk_ref, dv_ref) = rest[:sub], rest[sub:]
        qwin, dowin = _band((qp, qc, qn)), _band((dp_, dc_, dn_))
        lsewin, cpwin = _band((lp, lc, ln)), _band((cp_, cc_, cn_))
        for u in range(sub):
            band = _sub(u, ATT_BK)
            qm = _per_head_rows(_pairs(qwin[band]))
            dom = _per_head_rows(_pairs(dowin[band]))
            kv, vv = _pairs(k_ref[_sub(u), :]), _pairs(v_ref[_sub(u), :])
            sc = _bdot(qm, kv, BNT) + b_refs[u][0].reshape(N_PAIRS, 2 * ATT_BK, ATT_BQ)
            p = jnp.exp(sc - _per_head_cols(lsewin[band]))
            _store_pairs(dv_ref, _bdot(p, dom, BTN), _sub(u))
            ds = p * (_bdot(dom, vv, BNT) + _per_head_cols(cpwin[band]))
            _store_pairs(dk_ref, _bdot(ds, qm, BTN), _sub(u))

    cur = pl.BlockSpec((rows, SWA_WIDTH), lambda r, t: (t, r))
    dk, dv = pl.pallas_call(
        body, name=f"att_dkv_d{dilation}", grid=(dilation, nb),
        in_specs=[cur, cur] + _band_specs(length, rows) * 4 + _bias_specs(nb, sub,ATT_BK, ATT_BQ),
        out_specs=[cur, cur],
        out_shape=[jax.ShapeDtypeStruct(view, BF16)] * 2,
        compiler_params=_params("parallel", "parallel"),
    )(k.reshape(view), v.reshape(view), *([q.reshape(view)] * 3), *([dop.reshape(view)] * 3),
      *([lse.reshape(view)] * 3), *([cp.reshape(view)] * 3), *([bias_t] * sub))
    return dk.reshape(s, SWA_WIDTH), dv.reshape(s, SWA_WIDTH)


def _pattern_weights(lses):
    m = lses[0]
    for l in lses[1:]:
        m = jnp.maximum(m, l)
    es = [jnp.exp(l - m) for l in lses]
    den = es[0]
    for e in es[1:]:
        den = den + e
    return [e / den for e in es]


def _combine_fwd(outs, lses):
    s = outs[0].shape[0]
    tm = min(512, s)
    npat = len(outs)

    def body(*refs):
        ws = _pattern_weights([r[...] for r in refs[npat:2 * npat]])
        o = ws[0] * refs[0][...]
        for p in range(1, npat):
            o = o + ws[p] * refs[p][...]
        refs[2 * npat][...] = o.astype(BF16)

    blk = pl.BlockSpec((tm, SWA_WIDTH), lambda i: (i, 0))
    return pl.pallas_call(
        body, name="swa_combine_fwd", grid=(s // tm,), in_specs=[blk] * (2 * npat), out_specs=blk,
        out_shape=jax.ShapeDtypeStruct((s, SWA_WIDTH), BF16), compiler_params=_params("parallel"),
    )(*outs, *lses)


def _combine_bwd(d_out, outs, lses, bd):
    s = d_out.shape[0]
    tm = min(512, s)
    npat = len(outs)

    def body(*refs):
        d_ref, bd_ref = refs[0], refs[1 + 2 * npat]
        o_refs, l_refs = refs[1:1 + npat], refs[1 + npat:1 + 2 * npat]
        out_refs = refs[2 + 2 * npat:]
        ws = _pattern_weights([r[...] for r in l_refs])
        dov = d_ref[...]
        o = ws[0] * o_refs[0][...]
        for p in range(1, npat):
            o = o + ws[p] * o_refs[p][...]
        rd = _group_sum(dov * o, bd_ref[...])
        for p in range(npat):
            out_refs[p][...] = (ws[p] * dov).astype(BF16)
            out_refs[npat + p][...] = -ws[p] * rd

    blk = pl.BlockSpec((tm, SWA_WIDTH), lambda i: (i, 0))
    res = pl.pallas_call(
        body, name="swa_combine_bwd", grid=(s // tm,),
        in_specs=[blk] * (1 + 2 * npat) + [pl.BlockSpec((SWA_WIDTH, SWA_WIDTH), lambda i: (0, 0))],
        out_specs=[blk] * (2 * npat),
        out_shape=[jax.ShapeDtypeStruct((s, SWA_WIDTH), BF16)] * npat + [jax.ShapeDtypeStruct((s, SWA_WIDTH), F32)] * npat,
        compiler_params=_params("parallel"),
    )(d_out, *outs, *lses, bd)
    return res[:npat], res[npat:]


def _rel_bias_grad(dbs, buckets):
    npat = len(dbs)

    def body(*refs):
        db_refs, bk_refs, o_ref = refs[:npat], refs[npat:2 * npat], refs[2 * npat]
        row = lax.broadcasted_iota(jnp.int32, (REL_BUCKETS, LANE), 0)
        lane = lax.broadcasted_iota(jnp.int32, (REL_BUCKETS, LANE), 1)
        tiles = [[db_refs[p][0, h] + db_refs[p][1, h] + db_refs[p][2, h] + db_refs[p][3, h] for h in range(SWA_HEADS)]
                 for p in range(npat)]
        bks = [r[...] for r in bk_refs]

        def one_bucket(b, acc):
            for h in range(SWA_HEADS):
                tot = jnp.zeros((1, 1), F32)
                for p in range(npat):
                    sel = jnp.where(bks[p] == b, tiles[p][h], 0.0)
                    tot = tot + jnp.sum(jnp.sum(sel, axis=1, keepdims=True), axis=0, keepdims=True)
                acc = acc + jnp.where((row == b) & (lane == h), tot, 0.0)
            return acc

        o_ref[...] = lax.fori_loop(0, REL_BUCKETS, one_bucket, jnp.zeros((REL_BUCKETS, LANE), F32))

    full4 = pl.BlockSpec((4, SWA_HEADS, ATT_BQ, ATT_BK), lambda: (0, 0, 0, 0))
    full2 = pl.BlockSpec((ATT_BQ, ATT_BK), lambda: (0, 0))
    return pl.pallas_call(
        body, name="rel_bias_grad", in_specs=[full4] * npat + [full2] * npat,
        out_specs=pl.BlockSpec((REL_BUCKETS, LANE), lambda: (0, 0)),
        out_shape=jax.ShapeDtypeStruct((REL_BUCKETS, LANE), F32),
        compiler_params=pltpu.CompilerParams(vmem_limit_bytes=V7X_VMEM_LIMIT_BYTES),
    )(*dbs, *buckets)


def _swa_forward(p_pad, qw_row, kw_row, rel_bias, bd):
    q, k, v = _swa_pre_fwd(p_pad, qw_row, kw_row, bd)
    outs, lses = [], []
    for _, dil in DILATION_PATTERNS:
        o, lse = _att_fwd2(q, k, v, _bias_tiles(rel_bias, dil, True), dil)
        outs.append(o)
        lses.append(lse)
    return _combine_fwd(outs, lses), (q, k, v, outs, lses)


def _swa_backward(d_out, p_pad, qw_row, kw_row, rel_bias, bd, saved, dp_all):
    q, k, v, outs, lses = saved
    dops, cps = _combine_bwd(d_out, outs, lses, bd)
    dqs, dks, dvs, dbs, buckets = [], [], [], [], []
    for p, (_, dil) in enumerate(DILATION_PATTERNS):
        dq, db = _att_dq2(q, k, v, dops[p], lses[p], cps[p], _bias_tiles(rel_bias, dil, True), dil)
        dk, dv = _att_dkv2(q, k, v, dops[p], lses[p], cps[p], _bias_tiles(rel_bias, dil, False), dil)
        dqs.append(dq)
        dks.append(dk)
        dvs.append(dv)
        dbs.append(db)
        buckets.append(jnp.asarray(_band_tables(dil, True)[1]))
    dp, dqw, dkw = _swa_pre_bwd(dqs, dks, dvs, p_pad, qw_row, kw_row, bd, dp_all)
    return dp, dqw, dkw, _rel_bias_grad(dbs, buckets)


def _lane_row(v):
    flat = v.reshape(-1).astype(F32)
    return jnp.zeros((1, LANE), F32).at[0, :flat.shape[0]].set(flat)


W_IN_SHARD = N_IN // N_DEV
W_IN_RUNS = ((0, NAT_Z, 0), (NAT_Z, NAT_AB, OFF_Z), (NAT_AB, NAT_B, OFF_AB), (NAT_B, N_IN, OFF_B))


def _w_in_pieces(shard):
    lo, hi = shard * W_IN_SHARD, (shard + 1) * W_IN_SHARD
    out = []
    for first, last, dst in W_IN_RUNS:
        a, b = max(lo, first), min(hi, last)
        if a < b:
            out.append((a - lo, b - a, dst + a - first))
    return out


def _w_in_from_slabs(w3):
    nd, r, _ = w3.shape

    def body(w_ref, o_ref):
        o_ref[:, OFF_AB:N_PAD] = jnp.zeros((r, N_PAD - OFF_AB), w3.dtype)
        for sh in range(nd):
            for src, length, dst in _w_in_pieces(sh):
                o_ref[:, dst:dst + length] = w_ref[sh, :, src:src + length]

    return pl.pallas_call(
        body, name="w_in_from_slabs", out_shape=jax.ShapeDtypeStruct((r, N_PAD), w3.dtype),
        compiler_params=pltpu.CompilerParams(vmem_limit_bytes=V7X_VMEM_LIMIT_BYTES),
    )(w3)


def _w_in_grad_slabs(dw_pad, dtype):
    r = dw_pad.shape[0]

    def body(dw_ref, o_ref):
        for sh in range(N_DEV):
            for src, length, dst in _w_in_pieces(sh):
                o_ref[sh, :, src:src + length] = dw_ref[:, dst:dst + length].astype(dtype)

    return pl.pallas_call(
        body, name="w_in_grad_slabs", out_shape=jax.ShapeDtypeStruct((N_DEV, r, W_IN_SHARD), dtype),
        compiler_params=pltpu.CompilerParams(vmem_limit_bytes=V7X_VMEM_LIMIT_BYTES),
    )(dw_pad)


LATE = ("w_out", "ffn2_w_gate", "ffn2_w_up", "ffn2_w_down")
TRANSPOSED = ("ffn1_w_gate", "ffn1_w_up", "ffn2_w_gate", "ffn2_w_up")


def _late_weights(slabs):
    return {n: g.reshape(N_DEV * g.shape[1], g.shape[2]) for n, g in zip(LATE, slabs)}


def _local_step(x, tgt, wts, small, late_shards=None):
    bd = _head_block_diag()
    alog_row, dt_row = _lane_row(small["a_log"]), _lane_row(small["dt_bias"])
    gnorm_row = small["gdn_norm_w"].reshape(1, GDN_HEAD_DIM)
    qw_row = jnp.tile(small["q_norm_w"].reshape(-1), SWA_HEADS).reshape(1, SWA_WIDTH)
    kw_row = jnp.tile(small["k_norm_w"].reshape(-1), SWA_HEADS).reshape(1, SWA_WIDTH)
    rel_bias = small["rel_bias"]
    exchange = late_shards is not None
    dw_dtype = BF16 if exchange else F32

    x1, sv1, (wg1, wu1, wd1), got, first = _ffn_forward(
        x, small["ffn1_norm"], wts.get("ffn1_w_gate"), wts.get("ffn1_w_up"), wts.get("ffn1_w_down"), "ffn1",
        gather=[late_shards["ffn1_w_down"], late_shards["w_in"]] if exchange else (),
        head=[late_shards["ffn1_w_gate"], late_shards["ffn1_w_up"], late_shards["conv_w"]] if exchange else ())
    win_pad = _w_in_from_slabs(got[0]) if exchange else wts["w_in_pad"]
    conv_w = first[0].reshape(N_DEV, -1)[:, :QKV_A // N_DEV * CONV_WIDTH].reshape(QKV_A, CONV_WIDTH) if exchange \
        else small["conv_w"]
    conv_wt = jnp.zeros((8, QKV_A), F32).at[:CONV_WIDTH].set(conv_w.T)
    n2, r2 = _rms_fwd(x1, small["mix_norm"], "mix_norm")
    p_pad = _matmul([(n2, win_pad)], tm=256, tn=N_PAD, tk=D_MODEL, name="w_in")
    o_a, sva, gathered = _gdn_forward(p_pad, conv_wt, alog_row, dt_row, gnorm_row,
                                      gather=[late_shards[n] for n in LATE] if exchange else ())
    if exchange:
        wts = {**wts, **_late_weights(gathered)}
    wo_a, wo_b = wts["w_out"][:GDN_WIDTH], wts["w_out"][GDN_WIDTH:]
    o_b, svb = _swa_forward(p_pad, qw_row, kw_row, rel_bias, bd)
    x2 = _matmul([(o_a, wo_a), (o_b, wo_b)], tm=512, tn=D_MODEL, tk=GDN_WIDTH, name="w_out", res=x1)
    x3, sv2, _, _, _ = _ffn_forward(x2, small["ffn2_norm"], wts["ffn2_w_gate"], wts["ffn2_w_up"], wts["ffn2_w_down"],
                                    "ffn2")
    loss_row, dx3, d_final = _final_loss(x3, small["final_norm"], tgt)

    dx2, d_ffn2_norm, dwg2, dwu2, dwd2, _ = _ffn_backward(
        dx3, x2, small["ffn2_norm"], wts["ffn2_w_gate"], wts["ffn2_w_up"], wts["ffn2_w_down"], sv2, "ffn2", dw_dtype)
    d_oa = _matmul([(dx2, wo_a)], tb=True, tm=512, tn=GDN_WIDTH, tk=D_MODEL, name="w_out_da")
    d_ob = _matmul([(dx2, wo_b)], tb=True, tm=512, tn=SWA_WIDTH, tk=D_MODEL, name="w_out_db")
    dwo_a = _matmul([(o_a, dx2)], ta=True, tm=GDN_WIDTH, tn=D_MODEL, tk=2048, name="w_out_dwa", out_dtype=dw_dtype)
    dwo_b = _matmul([(o_b, dx2)], ta=True, tm=SWA_WIDTH, tn=D_MODEL, tk=2048, name="w_out_dwb", out_dtype=dw_dtype)

    late_grads = [_row_slabs(jnp.concatenate([dwo_a, dwo_b], axis=0)), dwg2, dwu2, dwd2]
    dp_all, dconv, gate_sums, d_gnorm, received = _gdn_backward(
        d_oa, p_pad, conv_wt, alog_row, dt_row, gnorm_row, sva, scatter=late_grads if exchange else ())
    if exchange:
        late_grads = received
    dp_all, dqw, dkw, d_rel = _swa_backward(d_ob, p_pad, qw_row, kw_row, rel_bias, bd, svb, dp_all)
    dw_pad = _matmul([(n2, dp_all)], ta=True, tm=D_MODEL, tn=N_PAD // 3, tk=2048, name="w_in_dw")
    dx1, d_mix_norm = _matmul([(dp_all, win_pad)], tb=True, tm=512, tn=D_MODEL, tk=N_PAD, name="w_in_dn",
                              norm_bwd=(x1, r2, small["mix_norm"], dx2))
    d_w_in = _w_in_grad_slabs(dw_pad, dw_dtype)
    dx, d_ffn1_norm, dwg1, dwu1, dwd1, got = _ffn_backward(
        dx1, x, small["ffn1_norm"], wg1, wu1, wd1, sv1, "ffn1", dw_dtype,
        scatter=[d_w_in] if exchange else None)
    if exchange:
        d_w_in = got[0]

    grads = {
        "ffn1_norm": d_ffn1_norm, "ffn1_w_gate": dwg1, "ffn1_w_up": dwu1, "ffn1_w_down": dwd1,
        "mix_norm": d_mix_norm, "w_in": d_w_in, "conv_w": dconv[:CONV_WIDTH].T,
        "a_log": gate_sums[0, :8].reshape(2, GDN_HEADS), "dt_bias": gate_sums[1, :8].reshape(2, GDN_HEADS),
        "gdn_norm_w": d_gnorm, "q_norm_w": dqw.reshape(SWA_HEADS, SWA_HEAD_DIM).sum(0, keepdims=True),
        "k_norm_w": dkw.reshape(SWA_HEADS, SWA_HEAD_DIM).sum(0, keepdims=True), "rel_bias": d_rel[:, :SWA_HEADS],
        "ffn2_norm": d_ffn2_norm, "final_norm": d_final, **dict(zip(LATE, late_grads)),
    }
    return loss_row, dx, grads


MESH_IDS = pl.DeviceIdType.MESH
ANY = pl.BlockSpec(memory_space=pl.ANY)


def _adamw(parts, w, m, v, name):
    nparts, r, n = parts.shape
    tr = r
    for cand in (256, 176, 128, 104, 64, 8):
        if r % cand == 0:
            tr = cand
            break
    bc1 = 1.0 - ADAM_B1 ** ADAM_STEP
    bc2 = 1.0 - ADAM_B2 ** ADAM_STEP

    def body(p_ref, w_ref, m_ref, v_ref, g_ref, d_ref, nm_ref, nv_ref):
        g = p_ref[0].astype(F32)
        for k in range(1, nparts):
            g = g + p_ref[k].astype(F32)
        mn = ADAM_B1 * m_ref[...] + (1.0 - ADAM_B1) * g
        vn = ADAM_B2 * v_ref[...] + (1.0 - ADAM_B2) * (g * g)
        m_hat = mn / bc1
        v_hat = vn / bc2
        g_ref[...] = g
        nm_ref[...] = mn
        nv_ref[...] = vn
        d_ref[...] = -ADAM_LR * (m_hat / (jnp.sqrt(v_hat) + ADAM_EPS) + ADAM_WD * w_ref[...])

    blk = pl.BlockSpec((tr, n), lambda i: (i, 0))
    return pl.pallas_call(
        body, name=name, grid=(r // tr,),
        in_specs=[pl.BlockSpec((nparts, tr, n), lambda i: (0, i, 0)), blk, blk, blk],
        out_specs=[blk] * 4, out_shape=[jax.ShapeDtypeStruct((r, n), F32)] * 4,
        compiler_params=_params("parallel"),
    )(parts, w, m, v)


def _mesh_place():
    x, y, c = lax.axis_index("x"), lax.axis_index("y"), lax.axis_index("c")
    return x, y, c, [(1 - x, y), (x, 1 - y), (1 - x, 1 - y)]


def _gather_phases(x_refs, out_refs, send_sems, recv_sems, local_sems):
    na = len(x_refs)

    def place():
        x, y, c, chips = _mesh_place()
        return (x, y, c), (x, y, 1 - c), chips, c

    def slab(i, px, py, pc):
        return out_refs[i].at[4 * px + 2 * py + pc]

    def copy(i, k, block, to, src=None):
        return pltpu.make_async_remote_copy(
            src_ref=slab(i, *block) if src is None else src, dst_ref=slab(i, *block),
            send_sem=send_sems.at[i, k], recv_sem=recv_sems.at[i, k], device_id=to, device_id_type=MESH_IDS)

    def own(i, me):
        return pltpu.make_async_copy(x_refs[i], slab(i, *me), local_sems.at[i])

    def sends(i, me, sibling, chips, c):
        return [copy(i, 0, me, sibling, src=x_refs[i])] + [copy(i, 1 + j, me, (*chip, c), src=x_refs[i])
                                                          for j, chip in enumerate(chips)]

    def start():
        me, sibling, chips, c = place()
        for i in range(na):
            own(i, me).start()
            for cp in sends(i, me, sibling, chips, c):
                cp.start()

    def forward():
        me, sibling, chips, c = place()
        for j, chip in enumerate(chips):
            for i in range(na):
                copy(i, 1 + j, (*chip, c), me).wait_recv()
                copy(i, 4 + j, (*chip, c), sibling).start()

    def finish():
        me, sibling, chips, c = place()
        for i in range(na):
            copy(i, 0, sibling, me).wait_recv()
        for j, chip in enumerate(chips):
            for i in range(na):
                copy(i, 4 + j, (*chip, 1 - c), me).wait_recv()
        for i in range(na):
            for cp in sends(i, me, sibling, chips, c):
                cp.wait_send()
            for j, chip in enumerate(chips):
                copy(i, 4 + j, (*chip, c), sibling).wait_send()
            own(i, me).wait()

    return start, forward, finish


def _gather_semaphores(na):
    return [pltpu.SemaphoreType.DMA((na, 7)), pltpu.SemaphoreType.DMA((na, 7)), pltpu.SemaphoreType.DMA((na,))]


def _scatter_phases(g_refs, out_refs, send_sems, recv_sems, local_sems):
    na = len(g_refs)

    def place(m):
        x, y, c = lax.axis_index("x"), lax.axis_index("y"), lax.axis_index("c")
        px = 1 - x if m & 4 else x
        py = 1 - y if m & 2 else y
        pc = 1 - c if m & 1 else c
        return 4 * x + 2 * y + c, (px, py, pc), 4 * px + 2 * py + pc

    def own(i):
        me, _, _ = place(0)
        return pltpu.make_async_copy(g_refs[i].at[me], out_refs[i].at[me], local_sems.at[i])

    def start():
        for i in range(na):
            own(i).start()
            for m in range(1, N_DEV):
                me, peer, peer_idx = place(m)
                pltpu.make_async_remote_copy(
                    src_ref=g_refs[i].at[peer_idx], dst_ref=out_refs[i].at[me], send_sem=send_sems.at[i, m - 1],
                    recv_sem=recv_sems.at[i, m - 1], device_id=peer, device_id_type=MESH_IDS).start()

    def finish():
        for i in range(na):
            for m in range(1, N_DEV):
                me, peer, peer_idx = place(m)
                cp = pltpu.make_async_remote_copy(
                    src_ref=g_refs[i].at[peer_idx], dst_ref=out_refs[i].at[peer_idx], send_sem=send_sems.at[i, m - 1],
                    recv_sem=recv_sems.at[i, m - 1], device_id=peer, device_id_type=MESH_IDS)
                cp.wait_recv()
                cp.wait_send()
            own(i).wait()

    return start, finish


def _all_gather_many(vs, name):
    na = len(vs)

    def body(*refs):
        x_refs, out_refs = refs[:na], refs[na:2 * na]
        for step in _gather_phases(x_refs, out_refs, *refs[2 * na:]):
            step()

    return pl.pallas_call(
        body, name=name, in_specs=[ANY] * na, out_specs=[ANY] * na,
        out_shape=[jax.ShapeDtypeStruct((N_DEV,) + v.shape, v.dtype) for v in vs],
        scratch_shapes=_gather_semaphores(na),
        compiler_params=pltpu.CompilerParams(vmem_limit_bytes=V7X_VMEM_LIMIT_BYTES),
    )(*vs)


BIG = ("ffn1_w_gate", "ffn1_w_up", "ffn1_w_down", "w_in", "w_out", "ffn2_w_gate", "ffn2_w_up", "ffn2_w_down")
SMALL = ("ffn1_norm", "mix_norm", "a_log", "dt_bias", "gdn_norm_w", "q_norm_w", "k_norm_w", "rel_bias",
         "ffn2_norm", "final_norm")
WEIGHTS = ("ffn1_norm", "ffn1_w_gate", "ffn1_w_up", "ffn1_w_down", "mix_norm", "w_in", "conv_w", "a_log", "dt_bias",
           "gdn_norm_w", "q_norm_w", "k_norm_w", "rel_bias", "w_out", "ffn2_norm", "ffn2_w_gate", "ffn2_w_up",
           "ffn2_w_down", "final_norm")


def _pack(arrays, width, row_multiple):
    flat = jnp.concatenate([a.reshape(-1) for a in arrays])
    rows = -(-flat.shape[0] // width)
    rows = -(-rows // row_multiple) * row_multiple
    return jnp.pad(flat, (0, rows * width - flat.shape[0])).reshape(rows, width)


def _unpack(packed, shapes):
    flat = packed.reshape(-1)
    out, pos = [], 0
    for shp in shapes:
        size = int(np.prod(shp))
        out.append(flat[pos:pos + size].reshape(shp))
        pos += size
    return out


def kernel(x, ffn1_norm, ffn1_w_gate, ffn1_w_up, ffn1_w_down, mix_norm, w_in, conv_w, a_log, dt_bias, gdn_norm_w, q_norm_w, k_norm_w, rel_bias, w_out, ffn2_norm, ffn2_w_gate, ffn2_w_up, ffn2_w_down, final_norm, loss_target, m_ffn1_norm, m_ffn1_w_gate, m_ffn1_w_up, m_ffn1_w_down, m_mix_norm, m_w_in, m_conv_w, m_a_log, m_dt_bias, m_gdn_norm_w, m_q_norm_w, m_k_norm_w, m_rel_bias, m_w_out, m_ffn2_norm, m_ffn2_w_gate, m_ffn2_w_up, m_ffn2_w_down, m_final_norm, v_ffn1_norm, v_ffn1_w_gate, v_ffn1_w_up, v_ffn1_w_down, v_mix_norm, v_w_in, v_conv_w, v_a_log, v_dt_bias, v_gdn_norm_w, v_q_norm_w, v_k_norm_w, v_rel_bias, v_w_out, v_ffn2_norm, v_ffn2_w_gate, v_ffn2_w_up, v_ffn2_w_down, v_final_norm):
    w = dict(ffn1_norm=ffn1_norm, ffn1_w_gate=ffn1_w_gate, ffn1_w_up=ffn1_w_up, ffn1_w_down=ffn1_w_down, mix_norm=mix_norm, w_in=w_in, conv_w=conv_w, a_log=a_log, dt_bias=dt_bias, gdn_norm_w=gdn_norm_w, q_norm_w=q_norm_w, k_norm_w=k_norm_w, rel_bias=rel_bias, w_out=w_out, ffn2_norm=ffn2_norm, ffn2_w_gate=ffn2_w_gate, ffn2_w_up=ffn2_w_up, ffn2_w_down=ffn2_w_down, final_norm=final_norm)
    mom = dict(ffn1_norm=m_ffn1_norm, ffn1_w_gate=m_ffn1_w_gate, ffn1_w_up=m_ffn1_w_up, ffn1_w_down=m_ffn1_w_down, mix_norm=m_mix_norm, w_in=m_w_in, conv_w=m_conv_w, a_log=m_a_log, dt_bias=m_dt_bias, gdn_norm_w=m_gdn_norm_w, q_norm_w=m_q_norm_w, k_norm_w=m_k_norm_w, rel_bias=m_rel_bias, w_out=m_w_out, ffn2_norm=m_ffn2_norm, ffn2_w_gate=m_ffn2_w_gate, ffn2_w_up=m_ffn2_w_up, ffn2_w_down=m_ffn2_w_down, final_norm=m_final_norm)
    var = dict(ffn1_norm=v_ffn1_norm, ffn1_w_gate=v_ffn1_w_gate, ffn1_w_up=v_ffn1_w_up, ffn1_w_down=v_ffn1_w_down, mix_norm=v_mix_norm, w_in=v_w_in, conv_w=v_conv_w, a_log=v_a_log, dt_bias=v_dt_bias, gdn_norm_w=v_gdn_norm_w, q_norm_w=v_q_norm_w, k_norm_w=v_k_norm_w, rel_bias=v_rel_bias, w_out=v_w_out, ffn2_norm=v_ffn2_norm, ffn2_w_gate=v_ffn2_w_gate, ffn2_w_up=v_ffn2_w_up, ffn2_w_down=v_ffn2_w_down, final_norm=v_final_norm)
    ix, iy, ic = lax.axis_index("x"), lax.axis_index("y"), lax.axis_index("c")
    me = 4 * ix + 2 * iy + ic

    def local(a, n):
        return jnp.swapaxes(a[0], 0, 1) if n in TRANSPOSED else a[0]

    shard = {n: local(w[n], n) for n in BIG}

    conv_shard_shape = w["conv_w"][0].shape
    small = {n: w[n][0] if n not in ("rel_bias",) else w[n] for n in SMALL}
    small = {n: (a.reshape(1, -1) if n.endswith("norm") else a) for n, a in small.items()}
    shards = {n: shard[n].astype(BF16) for n in BIG}
    shards["conv_w"] = _pack([w["conv_w"][0]], LANE, 8)
    loss_row, grad_x, grads = _local_step(x[0], loss_target[0], {}, small, late_shards=shards)

    big_out = [[], [], [], []]
    for n in BIG:
        for kind, val in enumerate(_adamw(grads[n], shard[n], local(mom[n], n), local(var[n], n), f"{n}_adamw")):
            big_out[kind].append(jnp.swapaxes(val, 0, 1) if n in TRANSPOSED else val)

    small_names = SMALL + ("conv_w",)
    small_shapes = [grads[n].shape for n in small_names] + [(1, 1)]
    g_small = _pack([grads[n] for n in small_names] + [loss_row[:, :1]], LANE, 8)
    all_small = _all_gather_many([g_small], "gather_small_grads")[0]
    riders = [jnp.zeros(shp, F32) for shp in small_shapes[len(SMALL):]]
    ws = _pack([w[n].reshape(grads[n].shape) for n in SMALL] + riders, LANE, 8)
    ms = _pack([mom[n].reshape(grads[n].shape) for n in SMALL] + riders, LANE, 8)
    vs = _pack([var[n].reshape(grads[n].shape) for n in SMALL] + riders, LANE, 8)
    small_out = [_unpack(a, small_shapes) for a in _adamw(all_small, ws, ms, vs, "adamw_small")]
    loss = small_out[0][-1][0, 0]
    conv_g = lax.dynamic_slice_in_dim(small_out[0][len(SMALL)], me * conv_shard_shape[0], conv_shard_shape[0], axis=0)
    conv_out = [_unpack(a, [conv_shard_shape])[0] for a in _adamw(
        _pack([conv_g], LANE, 8)[None], _pack([w["conv_w"][0]], LANE, 8), _pack([mom["conv_w"][0]], LANE, 8),
        _pack([var["conv_w"][0]], LANE, 8), "adamw_conv")]

    def leaf(kind, n):
        if n in BIG:
            val = big_out[kind][BIG.index(n)]
        elif n == "conv_w":
            val = conv_out[kind]
        else:
            val = small_out[kind][SMALL.index(n)]
        return val.reshape(w[n].shape)

    outs = [loss, grad_x[None]]
    for kind in range(4):
        outs += [leaf(kind, n) for n in WEIGHTS]
    return tuple(outs)
```

```python
import functools
import math

import numpy as np
import jax
import jax.numpy as jnp
from jax import lax
from jax.experimental import pallas as pl
from jax.experimental.pallas import tpu as pltpu

F32 = jnp.float32
BF16 = jnp.bfloat16

D_MODEL = 1024
D_FF = 2816
GDN_HEADS = 4
GDN_HEAD_DIM = 128
GDN_WIDTH = 512
CONV_WIDTH = 5
CHUNK = 64
SWA_HEADS = 8
SWA_HEAD_DIM = 64
SWA_WIDTH = 512
DILATION_PATTERNS = ((128, 1), (512, 4), (2048, 16))
REL_BUCKETS = 32
REL_MAX_DISTANCE = 1024
EPS = 1e-6
NEG_BIG = -1e30
N_DEV = 8

ADAM_LR = 0.001
ADAM_B1 = 0.9
ADAM_B2 = 0.999
ADAM_EPS = 1e-08
ADAM_WD = 0.01
ADAM_STEP = 10

QKV_A = 3 * GDN_WIDTH
OFF_B = QKV_A
OFF_Z = OFF_B + 3 * SWA_WIDTH
OFF_AB = OFF_Z + GDN_WIDTH
N_PAD = OFF_AB + 256
N_IN = 3600
NAT_Z, NAT_AB, NAT_B = QKV_A, QKV_A + GDN_WIDTH, QKV_A + GDN_WIDTH + 16

V7X_VMEM_LIMIT_BYTES = 56 * 1024 * 1024
LANE = 128
ATT_BQ = 128
ATT_HALO = 64
CONV_ROWS = 256

NN = (((1,), (0,)), ((), ()))
NT = (((1,), (1,)), ((), ()))
TN = (((0,), (0,)), ((), ()))


def _params(*sem):
    return pltpu.CompilerParams(dimension_semantics=sem, vmem_limit_bytes=V7X_VMEM_LIMIT_BYTES)


def _dot(a, b, dn=NN):
    return lax.dot_general(a.astype(BF16), b.astype(BF16), dn, preferred_element_type=F32)


def _sigmoid(x):
    return 1.0 / (1.0 + jnp.exp(-x))


class _Exchange:
    def __init__(self, kind, arrays):
        self.kind, self.arrays = kind, list(arrays)

    def out_shape(self):
        lead = (N_DEV,) if self.kind == "gather" else ()
        return [jax.ShapeDtypeStruct(lead + v.shape, v.dtype) for v in self.arrays]

    def hooks(self, in_refs, out_refs, sems, grid):
        step = pl.program_id(0)
        for axis in range(1, len(grid)):
            step = step * grid[axis] + pl.program_id(axis)
        total = math.prod(grid)
        if self.kind == "gather":
            assert total >= 4
            start, forward, finish = _gather_phases(in_refs, out_refs, *sems)
            pl.when(step == total // 2)(forward)
        else:
            assert total >= 2
            start, finish = _scatter_phases(in_refs, out_refs, *sems)
        pl.when(step == 0)(start)
        pl.when(step == total - 1)(finish)


def _pallas(body, *, name, grid, in_specs, out_specs, out_shape, args, semantics, scratch_shapes=(), exchange=None):
    n_in, n_out, n_scr = len(in_specs), len(out_specs), len(scratch_shapes)
    if exchange is None:
        res = pl.pallas_call(
            body, name=name, grid=grid, in_specs=list(in_specs), out_specs=list(out_specs), out_shape=list(out_shape),
            scratch_shapes=list(scratch_shapes), compiler_params=_params(*semantics))(*args)
        return list(res), []
    na = len(exchange.arrays)

    def carrying(*refs):
        ins, sent = refs[:n_in], refs[n_in:n_in + na]
        outs = refs[n_in + na:n_in + na + n_out]
        landed = refs[n_in + na + n_out:n_in + 2 * na + n_out]
        rest = refs[n_in + 2 * na + n_out:]
        exchange.hooks(sent, landed, rest[n_scr:], grid)
        body(*ins, *outs, *rest[:n_scr])

    res = pl.pallas_call(
        carrying, name=name, grid=grid, in_specs=list(in_specs) + [ANY] * na, out_specs=list(out_specs) + [ANY] * na,
        out_shape=list(out_shape) + exchange.out_shape(), scratch_shapes=list(scratch_shapes) + _gather_semaphores(na),
        compiler_params=_params(*(["arbitrary"] * len(grid))))(*args, *exchange.arrays)
    return list(res[:n_out]), list(res[n_out:])


def _matmul(pairs, *, ta=False, tb=False, out_dtype=F32, tm, tn, tk, name, res=None, alpha=None, norm_bwd=None,
            norm_fwd=None, exchange=None):
    a0, b0 = pairs[0]
    m = a0.shape[1] if ta else a0.shape[0]
    k = a0.shape[0] if ta else a0.shape[1]
    n = b0.shape[0] if tb else b0.shape[1]
    tm, tn, tk = min(tm, m), min(tn, n), min(tk, k)
    assert m % tm == 0 and n % tn == 0 and k % tk == 0, (name, m, n, k, tm, tn, tk)
    nk = k // tk
    npairs = len(pairs)
    dn = (((0 if ta else 1,), (1 if tb else 0,)), ((), ()))
    assert norm_bwd is None or (tn == n and res is None and alpha is None)

    def body(*refs):
        ins = refs[:2 * npairs]
        pos = 2 * npairs
        r_ref = None
        if res is not None:
            r_ref = refs[pos]
            pos += 1
        if norm_bwd is not None:
            x_ref, rs_ref, w_ref, dres_ref = refs[pos:pos + 4]
            o_ref, dw_ref, acc = refs[pos + 4:pos + 7]

            @pl.when((pl.program_id(0) == 0) & (pl.program_id(2) == 0))
            def _():
                dw_ref[...] = jnp.zeros_like(dw_ref)
        elif norm_fwd is not None:
            wn_ref, o_ref, n_ref, rs_out, acc = refs[pos:pos + 5]
        else:
            o_ref, acc = refs[pos], refs[pos + 1]
        kk = pl.program_id(2)
        t = None
        for p in range(npairs):
            d = _dot(ins[2 * p][...], ins[2 * p + 1][...], dn)
            t = d if t is None else t + d

        if nk > 1:
            @pl.when(kk == 0)
            def _():
                acc[...] = t

            @pl.when((kk > 0) & (kk < nk - 1))
            def _():
                acc[...] += t

        @pl.when(kk == nk - 1)
        def _():
            r = acc[...] + t if nk > 1 else t
            if alpha is not None:
                r = r * alpha
            if r_ref is not None:
                r = r_ref[...] + r
            if norm_bwd is not None:
                rs = rs_ref[...]
                xhat = x_ref[...] * rs
                dw_ref[...] += jnp.sum(r * xhat, axis=0, keepdims=True)
                t_w = r * w_ref[...]
                r = dres_ref[...] + rs * (t_w - xhat * jnp.mean(t_w * xhat, axis=-1, keepdims=True))
            if norm_fwd is not None:
                rs = lax.rsqrt(jnp.mean(r * r, axis=-1, keepdims=True) + EPS)
                n_ref[...] = (r * rs * wn_ref[...]).astype(BF16)
                rs_out[...] = rs
            o_ref[...] = r.astype(out_dtype)

    a_spec = pl.BlockSpec((tk, tm), lambda i, j, kk: (kk, i)) if ta else pl.BlockSpec((tm, tk), lambda i, j, kk: (i, kk))
    b_spec = pl.BlockSpec((tn, tk), lambda i, j, kk: (j, kk)) if tb else pl.BlockSpec((tk, tn), lambda i, j, kk: (kk, j))
    o_spec = pl.BlockSpec((tm, tn), lambda i, j, kk: (i, j))
    in_specs = [a_spec, b_spec] * npairs + ([o_spec] if res is not None else [])
    args = [t for pr in pairs for t in pr] + ([res] if res is not None else [])
    out_specs, out_shape = [o_spec], [jax.ShapeDtypeStruct((m, n), out_dtype)]
    if norm_bwd is not None:
        vec = pl.BlockSpec((1, n), lambda i, j, kk: (0, 0))
        in_specs += [o_spec, pl.BlockSpec((tm, 1), lambda i, j, kk: (i, 0)), vec, o_spec]
        args += list(norm_bwd)
        out_specs.append(vec)
        out_shape.append(jax.ShapeDtypeStruct((1, n), F32))
    if norm_fwd is not None:
        assert tn == n and norm_bwd is None
        in_specs.append(pl.BlockSpec((1, n), lambda i, j, kk: (0, 0)))
        args.append(norm_fwd)
        out_specs += [o_spec, pl.BlockSpec((tm, 1), lambda i, j, kk: (i, 0))]
        out_shape += [jax.ShapeDtypeStruct((m, n), BF16), jax.ShapeDtypeStruct((m, 1), F32)]
    outs, exchanged = _pallas(
        body, name=name, grid=(m // tm, n // tn, nk), in_specs=in_specs, out_specs=out_specs, out_shape=out_shape,
        scratch_shapes=[pltpu.VMEM((tm, tn) if nk > 1 else (8, LANE), F32)],
        semantics=("arbitrary",) * 3 if norm_bwd is not None else ("parallel", "parallel", "arbitrary"), args=args,
        exchange=exchange)
    out = outs[0] if len(outs) == 1 else tuple(outs)
    return out if exchange is None else (out, exchanged)


def _rms_fwd(x, w, name, exchange=None):
    s, d = x.shape
    tm = min(512, s)

    def body(x_ref, w_ref, n_ref, r_ref):
        xv = x_ref[...]
        r = lax.rsqrt(jnp.mean(xv * xv, axis=-1, keepdims=True) + EPS)
        n_ref[...] = (xv * r * w_ref[...]).astype(BF16)
        r_ref[...] = r

    (n, r), exchanged = _pallas(
        body, name=name, grid=(s // tm,),
        in_specs=[pl.BlockSpec((tm, d), lambda i: (i, 0)), pl.BlockSpec((1, d), lambda i: (0, 0))],
        out_specs=[pl.BlockSpec((tm, d), lambda i: (i, 0)), pl.BlockSpec((tm, 1), lambda i: (i, 0))],
        out_shape=[jax.ShapeDtypeStruct((s, d), BF16), jax.ShapeDtypeStruct((s, 1), F32)],
        semantics=("parallel",), args=(x, w), exchange=exchange)
    return (n, r) if exchange is None else (n, r, exchanged)


def _rms_bwd(dn, x, r, w, dres, name, exchange=None):
    s, d = x.shape
    tm = min(512, s)

    def body(dn_ref, x_ref, r_ref, w_ref, dres_ref, dx_ref, dw_ref):
        @pl.when(pl.program_id(0) == 0)
        def _():
            dw_ref[...] = jnp.zeros_like(dw_ref)

        rv = r_ref[...]
        xhat = x_ref[...] * rv
        g = dn_ref[...]
        t = g * w_ref[...]
        dx_ref[...] = dres_ref[...] + rv * (t - xhat * jnp.mean(t * xhat, axis=-1, keepdims=True))
        dw_ref[...] += jnp.sum(g * xhat, axis=0, keepdims=True)

    row = pl.BlockSpec((tm, d), lambda i: (i, 0))
    vec = pl.BlockSpec((1, d), lambda i: (0, 0))
    (dx, dw), exchanged = _pallas(
        body, name=name, grid=(s // tm,),
        in_specs=[row, row, pl.BlockSpec((tm, 1), lambda i: (i, 0)), vec, row],
        out_specs=[row, vec],
        out_shape=[jax.ShapeDtypeStruct((s, d), F32), jax.ShapeDtypeStruct((1, d), F32)],
        semantics=("arbitrary",), args=(dn, x, r, w, dres), exchange=exchange)
    return (dx, dw) if exchange is None else (dx, dw, exchanged)


def _final_loss(x3, wf, tgt):
    s, d = x3.shape
    tm = min(512, s)

    def body(x_ref, w_ref, t_ref, loss_ref, dx_ref, dw_ref):
        @pl.when(pl.program_id(0) == 0)
        def _():
            dw_ref[...] = jnp.zeros_like(dw_ref)
            loss_ref[...] = jnp.zeros_like(loss_ref)

        xv = x_ref[...]
        wv = w_ref[...]
        r = lax.rsqrt(jnp.mean(xv * xv, axis=-1, keepdims=True) + EPS)
        xhat = xv * r
        e = xhat * wv - t_ref[...]
        part = 0.5 * jnp.sum(jnp.mean(e * e, axis=-1, keepdims=True), axis=0, keepdims=True)
        loss_ref[...] += jnp.broadcast_to(part, loss_ref.shape)
        dy = e * (1.0 / d)
        dw_ref[...] += jnp.sum(dy * xhat, axis=0, keepdims=True)
        t = dy * wv
        dx_ref[...] = r * (t - xhat * jnp.mean(t * xhat, axis=-1, keepdims=True))

    row = pl.BlockSpec((tm, d), lambda i: (i, 0))
    vec = pl.BlockSpec((1, d), lambda i: (0, 0))
    return pl.pallas_call(
        body, name="final_loss", grid=(s // tm,),
        in_specs=[row, vec, row],
        out_specs=[pl.BlockSpec((1, LANE), lambda i: (0, 0)), row, vec],
        out_shape=[jax.ShapeDtypeStruct((1, LANE), F32), jax.ShapeDtypeStruct((s, d), F32),
                   jax.ShapeDtypeStruct((1, d), F32)],
        compiler_params=_params("arbitrary"),
    )(x3, wf, tgt)


def _ffn_up(n, wg, wu, name, exchange=None):
    s, d = n.shape
    f = wg.shape[0]
    tm, tn = min(512, s), f // 2

    def body(n_ref, wg_ref, wu_ref, g_ref, u_ref, a_ref):
        nv = n_ref[...]
        g = _dot(nv, wg_ref[...], NT)
        u = _dot(nv, wu_ref[...], NT)
        g_ref[...] = g.astype(BF16)
        u_ref[...] = u.astype(BF16)
        a_ref[...] = (g * _sigmoid(g) * u).astype(BF16)

    o = pl.BlockSpec((tm, tn), lambda j, i: (i, j))
    wspec = pl.BlockSpec((tn, d), lambda j, i: (j, 0))
    return _pallas(
        body, name=name, grid=(f // tn, s // tm),
        in_specs=[pl.BlockSpec((tm, d), lambda j, i: (i, 0)), wspec, wspec],
        out_specs=[o, o, o],
        out_shape=[jax.ShapeDtypeStruct((s, f), BF16)] * 3,
        semantics=("parallel", "parallel"), args=(n, wg, wu), exchange=exchange)


def _ffn_dact(dx, wd, g, u, name, exchange=None):
    s, d = dx.shape
    f = wd.shape[0]
    tm, tn = min(512, s), f // 2

    def body(dx_ref, wd_ref, g_ref, u_ref, dg_ref, du_ref):
        da = 0.5 * _dot(dx_ref[...], wd_ref[...], NT)
        gv = g_ref[...].astype(F32)
        sg = _sigmoid(gv)
        du_ref[...] = (da * gv * sg).astype(BF16)
        dg_ref[...] = (da * u_ref[...].astype(F32) * (sg * (1.0 + gv * (1.0 - sg)))).astype(BF16)

    o = pl.BlockSpec((tm, tn), lambda j, i: (i, j))
    return _pallas(
        body, name=name, grid=(f // tn, s // tm),
        in_specs=[pl.BlockSpec((tm, d), lambda j, i: (i, 0)), pl.BlockSpec((tn, d), lambda j, i: (j, 0)), o, o],
        out_specs=[o, o],
        out_shape=[jax.ShapeDtypeStruct((s, f), BF16), jax.ShapeDtypeStruct((s, f), BF16)],
        semantics=("parallel", "parallel"), args=(dx, wd, g, u), exchange=exchange)


def _row_slabs(full):
    return full.reshape(N_DEV, full.shape[0] // N_DEV, full.shape[1])


def _rows_of(slabs):
    return slabs.reshape(N_DEV * slabs.shape[1], slabs.shape[2])


def _ffn_forward(x, norm_w, wg, wu, wd, tag, gather=(), head=(), normed=None, next_norm=None):
    if normed is not None:
        (n, r), first = normed, []
    elif head:
        n, r, first = _rms_fwd(x, norm_w, f"{tag}_norm", _Exchange("gather", head))
    else:
        (n, r), first = _rms_fwd(x, norm_w, f"{tag}_norm"), []
    if wg is None:
        wg, wu, first = _rows_of(first[0]), _rows_of(first[1]), first[2:]
    (g, u, a), got = _ffn_up(n, wg, wu, f"{tag}_up", _Exchange("gather", gather) if gather else None)
    if wd is None:
        wd, got = _rows_of(got[0]), got[1:]
    y = _matmul([(a, wd)], tm=512, tn=1024, tk=wd.shape[0], name=f"{tag}_down", res=x, alpha=0.5, norm_fwd=next_norm)
    y, nxt = (y[0], y[1:]) if next_norm is not None else (y, None)
    return y, (n, r, g, u, a), (wg, wu, wd), got, first, nxt


def _ffn_backward(dy, x, norm_w, wgt, wut, wd, saved, tag, dw_dtype=F32, scatter=None):
    n, r, g, u, a = saved

    def behind(arrays):
        return _Exchange("scatter", arrays) if scatter is not None else None

    def dw(act, grad, name, alpha=None, exchange=None):
        return _matmul([(act, grad)], ta=True, tm=1408, tn=1024, tk=2048, name=name, alpha=alpha, out_dtype=dw_dtype,
                       exchange=exchange)

    dwd = _row_slabs(dw(a, dy, f"{tag}_dwd", alpha=0.5))
    (dg, du), extras = _ffn_dact(dy, wd, g, u, f"{tag}_dact", behind(scatter))
    if scatter is None:
        dwg, dwu = _row_slabs(dw(dg, n, f"{tag}_dwg")), _row_slabs(dw(du, n, f"{tag}_dwu"))
    else:
        dwg, (dwd,) = dw(dg, n, f"{tag}_dwg", exchange=behind([dwd]))
        dwu, (dwg,) = dw(du, n, f"{tag}_dwu", exchange=behind([_row_slabs(dwg)]))
        dwu = _row_slabs(dwu)
    if scatter is None:
        dx, dnorm = _matmul([(dg, wgt), (du, wut)], tm=512, tn=1024, tk=wgt.shape[0], name=f"{tag}_dn",
                            norm_bwd=(x, r, norm_w, dy))
    else:
        dn, (dwu,) = _matmul([(dg, wgt), (du, wut)], tm=512, tn=1024, tk=wgt.shape[0], name=f"{tag}_dn",
                             exchange=behind([dwu]))
        dx, dnorm = _rms_bwd(dn, x, r, norm_w, dy, f"{tag}_dnorm")
    return dx, dnorm, dwg, dwu, dwd, extras


Q_SCALE = GDN_HEAD_DIM ** -0.5
CONV_HALO = 8


def _lane_block(s):
    return pl.BlockSpec((None, s, LANE), lambda j: (j, 0, 0))


def _conv_taps(win, w_ref, rows, sign):
    n = rows + 2 * CONV_HALO
    acc = None
    for t in range(CONV_WIDTH):
        o = sign * (t - CONV_WIDTH // 2)
        sh = win if o == 0 else pltpu.roll(win, (-o) % n, 0)
        term = sh[CONV_HALO:CONV_HALO + rows] * w_ref[t:t + 1, :]
        acc = term if acc is None else acc + term
    return acc


def _gdn_conv_fwd(p_pad, conv_wt):
    s = p_pad.shape[0]
    rows = min(CONV_ROWS, s)
    nblk = QKV_A // LANE

    def body(p_ref, w_ref, c_ref, y_ref, pad):
        j = pl.program_id(0)
        zeros = jnp.zeros((CONV_HALO, LANE), F32)
        pad[0:CONV_HALO, :] = zeros
        pad[CONV_HALO + s:2 * CONV_HALO + s, :] = zeros
        pad[CONV_HALO:CONV_HALO + s, :] = p_ref[...]

        def chunk(ci, carry):
            b = pl.multiple_of(ci * rows, rows)
            win = pad[pl.ds(b, rows + 2 * CONV_HALO), :]
            c = _conv_taps(win, w_ref, rows, 1)
            c_ref[pl.ds(b, rows), :] = c
            act = c * _sigmoid(c)
            nrm = lax.rsqrt(jnp.sum(act * act, axis=-1, keepdims=True) + EPS)
            mult = jnp.where(j < GDN_HEADS, nrm * Q_SCALE, jnp.where(j < 2 * GDN_HEADS, nrm, 1.0))
            y_ref[pl.ds(b, rows), :] = act * mult
            return carry

        lax.fori_loop(0, s // rows, chunk, 0)

    col = pl.BlockSpec((s, LANE), lambda j: (0, j))
    return pl.pallas_call(
        body, name="gdn_conv_fwd", grid=(nblk,),
        in_specs=[col, pl.BlockSpec((8, LANE), lambda j: (0, j))],
        out_specs=[_lane_block(s), _lane_block(s)],
        out_shape=[jax.ShapeDtypeStruct((nblk, s, LANE), F32), jax.ShapeDtypeStruct((nblk, s, LANE), F32)],
        scratch_shapes=[pltpu.VMEM((s + 2 * CONV_HALO, LANE), F32)],
        compiler_params=_params("parallel"),
    )(p_pad, conv_wt)


def _gdn_conv_bwd(dy_f, dy_r, c_pre, p_pad, conv_wt, dp_all):
    s = p_pad.shape[0]
    rows = min(CONV_ROWS, s)
    nblk = QKV_A // LANE

    def body(dyf_ref, dyr_ref, c_ref, p_ref, w_ref, _, dp_ref, dw_ref, ppad, dcpad):
        j = pl.program_id(0)
        zeros = jnp.zeros((CONV_HALO, LANE), F32)
        for buf in (ppad, dcpad):
            buf[0:CONV_HALO, :] = zeros
            buf[CONV_HALO + s:2 * CONV_HALO + s, :] = zeros
        ppad[CONV_HALO:CONV_HALO + s, :] = p_ref[...]

        def act_bwd(ci, carry):
            b = pl.multiple_of(ci * rows, rows)
            c = c_ref[pl.ds(b, rows), :]
            g = dyf_ref[pl.ds(b, rows), :] + dyr_ref[pl.ds(b, rows), :]
            sg = _sigmoid(c)
            act = c * sg
            nrm = lax.rsqrt(jnp.sum(act * act, axis=-1, keepdims=True) + EPS)
            yh = act * nrm
            scale = jnp.where(j < GDN_HEADS, Q_SCALE, 1.0)
            dact_qk = (scale * nrm) * (g - yh * jnp.sum(g * yh, axis=-1, keepdims=True))
            dact = jnp.where(j < 2 * GDN_HEADS, dact_qk, g)
            dcpad[pl.ds(pl.multiple_of(b + CONV_HALO, CONV_HALO), rows), :] = dact * (sg * (1.0 + c * (1.0 - sg)))
            return carry

        lax.fori_loop(0, s // rows, act_bwd, 0)
        tap = lax.broadcasted_iota(jnp.int32, (8, LANE), 0)

        def taps_bwd(ci, dw):
            b = pl.multiple_of(ci * rows, rows)
            dcw = dcpad[pl.ds(b, rows + 2 * CONV_HALO), :]
            dp_ref[pl.ds(b, rows), :] = _conv_taps(dcw, w_ref, rows, -1).astype(BF16)
            pw = ppad[pl.ds(b, rows + 2 * CONV_HALO), :]
            dc = dcw[CONV_HALO:CONV_HALO + rows]
            n = rows + 2 * CONV_HALO
            for t in range(CONV_WIDTH):
                o = t - CONV_WIDTH // 2
                sh = pw if o == 0 else pltpu.roll(pw, (-o) % n, 0)
                row = jnp.sum(dc * sh[CONV_HALO:CONV_HALO + rows], axis=0, keepdims=True)
                dw = dw + jnp.where(tap == t, row, 0.0)
            return dw

        dw_ref[...] = lax.fori_loop(0, s // rows, taps_bwd, jnp.zeros((8, LANE), F32))

    col = pl.BlockSpec((s, LANE), lambda j: (0, j))
    wspec = pl.BlockSpec((8, LANE), lambda j: (0, j))
    return pl.pallas_call(
        body, name="gdn_conv_bwd", grid=(nblk,),
        in_specs=[_lane_block(s), _lane_block(s), _lane_block(s), col, wspec, ANY],
        out_specs=[col, wspec],
        out_shape=[jax.ShapeDtypeStruct(dp_all.shape, dp_all.dtype), jax.ShapeDtypeStruct((8, QKV_A), F32)],
        scratch_shapes=[pltpu.VMEM((s + 2 * CONV_HALO, LANE), F32), pltpu.VMEM((s + 2 * CONV_HALO, LANE), F32)],
        input_output_aliases={5: 0},
        compiler_params=_params("parallel"),
    )(dy_f, dy_r, c_pre, p_pad, conv_wt, dp_all)


def _softplus(x):
    return jnp.maximum(x, 0.0) + jnp.log(1.0 + jnp.exp(-jnp.abs(x)))


def _gdn_gates_fwd(p_pad, alog_row, dt_row):
    s = p_pad.shape[0]
    tm = min(1024, s)

    def body(p_ref, al_ref, dt_ref, o_ref):
        x = p_ref[...]
        lane = lax.broadcasted_iota(jnp.int32, x.shape, 1)
        g = -jnp.exp(al_ref[...]) * _softplus(x + dt_ref[...])
        o_ref[...] = jnp.where(lane < 8, g, jnp.where(lane < 16, _sigmoid(x), 0.0))

    vec = pl.BlockSpec((1, LANE), lambda i: (0, 0))
    return pl.pallas_call(
        body, name="gdn_gates_fwd", grid=(s // tm,),
        in_specs=[pl.BlockSpec((tm, LANE), lambda i: (i, OFF_AB // LANE)), vec, vec],
        out_specs=pl.BlockSpec((tm, LANE), lambda i: (i, 0)),
        out_shape=jax.ShapeDtypeStruct((s, LANE), F32),
        compiler_params=_params("parallel"),
    )(p_pad, alog_row, dt_row)


def _gdn_gates_bwd(dgb_f, dgb_r, p_pad, gb, alog_row, dt_row, dp_all):
    s = p_pad.shape[0]
    tm = min(1024, s)
    tail = N_PAD - OFF_AB

    def body(df_ref, dr_ref, p_ref, gb_ref, al_ref, dt_ref, _, dp_ref, sum_ref):
        @pl.when(pl.program_id(0) == 0)
        def _():
            sum_ref[...] = jnp.zeros_like(sum_ref)

        x = p_ref[...]
        gbv = gb_ref[...]
        dgb = df_ref[...] + dr_ref[...]
        lane = lax.broadcasted_iota(jnp.int32, x.shape, 1)
        da = dgb * (-jnp.exp(al_ref[...])) * _sigmoid(x + dt_ref[...])
        db = dgb * gbv * (1.0 - gbv)
        dp_ref[:, 0:LANE] = jnp.where(lane < 8, da, jnp.where(lane < 16, db, 0.0)).astype(BF16)
        dp_ref[:, LANE:tail] = jnp.zeros((tm, tail - LANE), BF16)
        row = lax.broadcasted_iota(jnp.int32, (8, LANE), 0)
        lane8 = lax.broadcasted_iota(jnp.int32, (8, LANE), 1)
        d_alog = jnp.sum(dgb * gbv, axis=0, keepdims=True)
        d_dt = jnp.sum(da, axis=0, keepdims=True)
        upd = jnp.where(row == 0, d_alog, jnp.where(row == 1, d_dt, 0.0))
        sum_ref[...] += jnp.where(lane8 < 8, upd, 0.0)

    vec = pl.BlockSpec((1, LANE), lambda i: (0, 0))
    blk = pl.BlockSpec((tm, LANE), lambda i: (i, 0))
    return pl.pallas_call(
        body, name="gdn_gates_bwd", grid=(s // tm,),
        in_specs=[blk, blk, pl.BlockSpec((tm, LANE), lambda i: (i, OFF_AB // LANE)), blk, vec, vec, ANY],
        out_specs=[pl.BlockSpec((tm, tail), lambda i: (i, OFF_AB // tail)), pl.BlockSpec((8, LANE), lambda i: (0, 0))],
        out_shape=[jax.ShapeDtypeStruct(dp_all.shape, dp_all.dtype), jax.ShapeDtypeStruct((8, LANE), F32)],
        input_output_aliases={6: 0},
        compiler_params=_params("arbitrary"),
    )(dgb_f, dgb_r, p_pad, gb, alog_row, dt_row, dp_all)


def _chunk_masks(rev):
    row = lax.broadcasted_iota(jnp.int32, (CHUNK, CHUNK), 0)
    col = lax.broadcasted_iota(jnp.int32, (CHUNK, CHUNK), 1)
    le = (col >= row) if rev else (col <= row)
    strict = (col > row) if rev else (col < row)
    return le, strict, row == col


def _gate_lanes(rev, h):
    d = 1 if rev else 0
    return d * GDN_HEADS + h, 8 + d * GDN_HEADS + h


BNN = (((2,), (1,)), ((0,), (0,)))
BNT = (((2,), (2,)), ((0,), (0,)))
BTN = (((1,), (1,)), ((0,), (0,)))
NB = 2 * GDN_HEADS
DELTA_CHUNKS = 8


def _bdot(a, b, dn=BNN):
    return lax.dot_general(a.astype(BF16), b.astype(BF16), dn, preferred_element_type=F32)


def _dot3(a, b, dn, exact_a=False, exact_b=False):
    def d(x, y):
        return lax.dot_general(x, y, dn, preferred_element_type=F32)

    ah = a.astype(BF16)
    bh = b.astype(BF16)
    out = d(ah, bh)
    if not exact_b:
        out = out + d(ah, (b - bh.astype(F32)).astype(BF16))
    if not exact_a:
        out = out + d((a - ah.astype(F32)).astype(BF16), bh)
    return out


def _both(f_val, r_val):
    return jnp.stack([f_val] * GDN_HEADS + [r_val] * GDN_HEADS)


def _head_blocks(ref_f, ref_r, rows_f, rows_r):
    return jnp.concatenate([ref_f[:, rows_f, :], ref_r[:, rows_r, :]], axis=0)


def _chunk_rows(c):
    return slice(c * CHUNK, (c + 1) * CHUNK), slice((DELTA_CHUNKS - 1 - c) * CHUNK, (DELTA_CHUNKS - c) * CHUNK)


def _heads(ref_f, ref_r, rows_f, rows_r):
    hd = GDN_HEAD_DIM
    return jnp.stack([ref_f[rows_f, h * hd:(h + 1) * hd] for h in range(GDN_HEADS)]
                     + [ref_r[rows_r, h * hd:(h + 1) * hd] for h in range(GDN_HEADS)])


def _gate_cols(tile_f, tile_r, base):
    return jnp.stack([tile_f[:, base + h:base + h + 1] for h in range(GDN_HEADS)]
                     + [tile_r[:, base + GDN_HEADS + h:base + GDN_HEADS + h + 1] for h in range(GDN_HEADS)])


def _chunk_common2(q, k, v, gbf, gbr):
    mf, mr = _chunk_masks(False), _chunk_masks(True)
    le, strict = _both(mf[0], mr[0]), _both(mf[1], mr[1])
    eye = mf[2]
    gcm_f = _dot3(mf[0].astype(F32), gbf, NN, exact_a=True)
    gcm_r = _dot3(mr[0].astype(F32), gbr, NN, exact_a=True)
    g, beta, gc = _gate_cols(gbf, gbr, 0), _gate_cols(gbf, gbr, 8), _gate_cols(gcm_f, gcm_r, 0)
    gc_row = _dot3(jnp.ones((NB, CHUNK, CHUNK), F32), jnp.where(eye[None], gc, 0.0), BNN, exact_a=True)
    decay = jnp.where(le, jnp.exp(jnp.where(le, gc - gc_row, 0.0)), 0.0)
    eg = jnp.exp(gc)
    gl = jnp.sum(g, axis=1, keepdims=True)
    kb = k * beta
    vb = v * beta
    kbeg = kb * eg
    lm = jnp.where(strict, _bdot(kb, k, BNT) * decay, 0.0)
    intra = _bdot(q, k, BNT) * decay
    edec = jnp.exp(gl - gc)
    return dict(strict=strict, eye=eye, beta=beta, decay=decay, eg=eg, gl=gl, kb=kb, vb=vb, kbeg=kbeg,
                lm=lm, intra=intra, qg=q * eg, edec=edec, kdec=k * edec)


def _unit_triangular_inverse(lm, eye):
    x = -lm
    t = eye[None].astype(F32) + x
    p = x
    for level in range(5):
        prod = functools.partial(_dot3, dn=BNN) if level < 2 else _bdot
        p = prod(p, p)
        t = t + prod(t, p)
    return t


def _delta_fwd2(y, gb, gather=()):
    s = y.shape[1]
    nc = s // CHUNK
    hd = GDN_HEAD_DIM
    na = len(gather)

    def body(*refs):
        qf, kf, vf, gf, qr, kr, vr, gr = refs[:8]
        of_ref, or_ref, sf_all, sr_all, tf_all, tr_all = refs[8 + na:14 + na]
        state = refs[14 + 2 * na]
        step = pl.program_id(0)

        @pl.when(step == 0)
        def _():
            state[...] = jnp.zeros_like(state)

        if na:
            start, forward, finish = _gather_phases(refs[8:8 + na], refs[14 + na:14 + 2 * na], *refs[15 + 2 * na:])
            pl.when(step == 0)(start)
            pl.when(step == ns // 2)(forward)
            pl.when(step == ns - 1)(finish)

        st = state[...]
        for c in range(DELTA_CHUNKS):
            rf, rr = _chunk_rows(c)
            q, k, v = _head_blocks(qf, qr, rf, rr), _head_blocks(kf, kr, rf, rr), _head_blocks(vf, vr, rf, rr)
            cm = _chunk_common2(q, k, v, gf[rf, :], gr[rr, :])
            tinv = _unit_triangular_inverse(cm["lm"], cm["eye"])
            u = _bdot(tinv, cm["vb"])
            w = _bdot(tinv, cm["kbeg"])
            v_new = u - _bdot(w, st)
            o = _bdot(cm["qg"], st) + _bdot(cm["intra"], v_new)
            for h in range(GDN_HEADS):
                of_ref[rf, h * hd:(h + 1) * hd] = o[h]
                or_ref[rr, h * hd:(h + 1) * hd] = o[GDN_HEADS + h]
            sf_all[c] = st[:GDN_HEADS]
            sr_all[DELTA_CHUNKS - 1 - c] = st[GDN_HEADS:]
            tf_all[c] = tinv[:GDN_HEADS]
            tr_all[DELTA_CHUNKS - 1 - c] = tinv[GDN_HEADS:]
            st = st * jnp.exp(cm["gl"]) + _bdot(cm["kdec"], v_new, BTN)
        state[...] = st

    rows = DELTA_CHUNKS * CHUNK
    ns = nc // DELTA_CHUNKS

    def col(j, rev):
        return pl.BlockSpec((GDN_HEADS, rows, hd), (lambda n: (j, ns - 1 - n, 0)) if rev else (lambda n: (j, n, 0)))

    def out(rev):
        return pl.BlockSpec((rows, GDN_WIDTH), (lambda n: (ns - 1 - n, 0)) if rev else (lambda n: (n, 0)))

    def gate(rev):
        return pl.BlockSpec((rows, LANE), (lambda n: (ns - 1 - n, 0)) if rev else (lambda n: (n, 0)))

    def per_chunk(d1, d2, rev):
        return pl.BlockSpec((DELTA_CHUNKS, GDN_HEADS, d1, d2),
                            (lambda n: (ns - 1 - n, 0, 0, 0)) if rev else (lambda n: (n, 0, 0, 0)))

    assert nc % DELTA_CHUNKS == 0 and (na == 0 or ns >= 4)
    res = pl.pallas_call(
        body, name="delta_fwd", grid=(ns,),
        in_specs=[col(0, False), col(1, False), col(2, False), gate(False), col(0, True), col(1, True), col(2, True), gate(True)]
        + [ANY] * na,
        out_specs=[out(False), out(True), per_chunk(hd, hd, False), per_chunk(hd, hd, True),
                   per_chunk(CHUNK, CHUNK, False), per_chunk(CHUNK, CHUNK, True)] + [ANY] * na,
        out_shape=[jax.ShapeDtypeStruct((s, GDN_WIDTH), F32)] * 2 + [jax.ShapeDtypeStruct((nc, GDN_HEADS, hd, hd), F32)] * 2
        + [jax.ShapeDtypeStruct((nc, GDN_HEADS, CHUNK, CHUNK), F32)] * 2
        + [jax.ShapeDtypeStruct((N_DEV,) + v.shape, v.dtype) for v in gather],
        scratch_shapes=[pltpu.VMEM((NB, hd, hd), F32)] + (_gather_semaphores(na) if na else []),
        compiler_params=_params("arbitrary"),
    )(y, y, y, gb, y, y, y, gb, *gather)
    return res[:6], res[6:]


def _delta_bwd2(y, gb, do, sf_all, sr_all, tf_all, tr_all, scatter=()):
    s = y.shape[1]
    nc = s // CHUNK
    hd = GDN_HEAD_DIM
    na = len(scatter)

    def body(*refs):
        qf, kf, vf, gf, dof, sf, tf, qr, kr, vr, gr, dor, sr, tr = refs[:14]
        dyf_ref, dyr_ref, dgf_ref, dgr_ref = refs[14 + na:18 + na]
        dstate = refs[18 + 2 * na]
        step = pl.program_id(0)

        @pl.when(step == 0)
        def _():
            dstate[...] = jnp.zeros_like(dstate)

        if na:
            start, finish = _scatter_phases(refs[14:14 + na], refs[18 + na:18 + 2 * na], *refs[19 + 2 * na:])
            pl.when(step == 0)(start)
            pl.when(step == ns - 1)(finish)

        def one_chunk(c, ds_out):
            rr, rf = _chunk_rows(c)
            cf, cr = DELTA_CHUNKS - 1 - c, c
            q, k, v = _head_blocks(qf, qr, rf, rr), _head_blocks(kf, kr, rf, rr), _head_blocks(vf, vr, rf, rr)
            dov = _heads(dof, dor, rf, rr)
            cm = _chunk_common2(q, k, v, gf[rf, :], gr[rr, :])
            tinv = jnp.concatenate([tf[cf], tr[cr]], axis=0)
            st = jnp.concatenate([sf[cf], sr[cr]], axis=0)
            decay, lm, intra, qg, kdec, kbeg, eg, kb, beta = (
                cm[n] for n in ("decay", "lm", "intra", "qg", "kdec", "kbeg", "eg", "kb", "beta"))
            u = _bdot(tinv, cm["vb"])
            w = _bdot(tinv, kbeg)
            v_new = u - _bdot(w, st)
            egl = jnp.exp(cm["gl"])
            d_qg = _bdot(dov, st, BNT)
            d_intra = _bdot(dov, v_new, BNT)
            dv_new = _bdot(intra, dov, BTN) + _bdot(kdec, ds_out)
            d_kdec = _bdot(v_new, ds_out, BNT)
            ds_in = _bdot(qg, dov, BTN) + egl * ds_out - _bdot(w, dv_new, BTN)
            dgl = egl * jnp.sum(jnp.sum(st * ds_out, axis=2, keepdims=True), axis=1, keepdims=True)
            dw = -_bdot(dv_new, st, BNT)
            dvb = _bdot(tinv, dv_new, BTN)
            dkbeg = _bdot(tinv, dw, BTN)
            dlm = jnp.where(cm["strict"], -(_bdot(dvb, u, BNT) + _bdot(dkbeg, w, BNT)), 0.0)
            d_a = dlm * decay
            d_qk = d_intra * decay
            e = dlm * lm + d_intra * intra
            colsum = _dot3(e, jnp.ones((NB, CHUNK, LANE), F32), BTN, exact_b=True)[:, :, 0:1]
            dgc = jnp.sum(e, axis=2, keepdims=True) - colsum
            dkb = _bdot(d_a, k) + dkbeg * eg
            dk = _bdot(d_a, kb, BTN) + _bdot(d_qk, q, BTN)
            dq = _bdot(d_qk, k) + d_qg * eg
            dgc = dgc + jnp.sum(d_qg * qg, axis=2, keepdims=True) + jnp.sum(dkbeg * kbeg, axis=2, keepdims=True)
            tdec = jnp.sum(d_kdec * kdec, axis=2, keepdims=True)
            dk = dk + d_kdec * cm["edec"] + dkb * beta
            dgc = dgc - tdec
            dgl = dgl + jnp.sum(tdec, axis=1, keepdims=True)
            dbeta = jnp.sum(dvb * v, axis=2, keepdims=True) + jnp.sum(dkb * k, axis=2, keepdims=True)
            dv = dvb * beta
            lane = lax.broadcasted_iota(jnp.int32, (CHUNK, LANE), 1)
            for rev, dy_ref, dg_ref, rows in ((False, dyf_ref, dgf_ref, rf), (True, dyr_ref, dgr_ref, rr)):
                dgc_tile = jnp.zeros((CHUNK, LANE), F32)
                rest = jnp.zeros((CHUNK, LANE), F32)
                for h in range(GDN_HEADS):
                    b = (GDN_HEADS if rev else 0) + h
                    gi, bi = _gate_lanes(rev, h)
                    dgc_tile = dgc_tile + jnp.where(lane == gi, dgc[b], 0.0)
                    rest = rest + jnp.where(lane == gi, dgl[b], 0.0) + jnp.where(lane == bi, dbeta[b], 0.0)
                    dy_ref[h, rows, :] = dq[b]
                    dy_ref[GDN_HEADS + h, rows, :] = dk[b]
                    dy_ref[2 * GDN_HEADS + h, rows, :] = dv[b]
                le_t = _chunk_masks(not rev)[0].astype(F32)
                dg_ref[rows, :] = _dot3(le_t, dgc_tile, NN, exact_a=True) + rest
            return ds_in

        ds = dstate[...]
        for c in range(DELTA_CHUNKS):
            ds = one_chunk(c, ds)
        dstate[...] = ds

    rows_per_step = DELTA_CHUNKS * CHUNK
    ns = nc // DELTA_CHUNKS

    def col(j, rev, blocks=GDN_HEADS):
        return pl.BlockSpec((blocks, rows_per_step, hd), (lambda n: (j, n, 0)) if rev else (lambda n: (j, ns - 1 - n, 0)))

    def wide(width, rev):
        return pl.BlockSpec((rows_per_step, width), (lambda n: (n, 0)) if rev else (lambda n: (ns - 1 - n, 0)))

    def per_chunk(d1, d2, rev):
        return pl.BlockSpec((DELTA_CHUNKS, GDN_HEADS, d1, d2),
                            (lambda n: (n, 0, 0, 0)) if rev else (lambda n: (ns - 1 - n, 0, 0, 0)))

    def side(rev):
        return [col(0, rev), col(1, rev), col(2, rev), wide(LANE, rev), wide(GDN_WIDTH, rev), per_chunk(hd, hd, rev),
                per_chunk(CHUNK, CHUNK, rev)]

    assert nc % DELTA_CHUNKS == 0 and (na == 0 or ns >= 2)
    res = pl.pallas_call(
        body, name="delta_bwd", grid=(ns,),
        in_specs=side(False) + side(True) + [ANY] * na,
        out_specs=[col(0, False, 3 * GDN_HEADS), col(0, True, 3 * GDN_HEADS), wide(LANE, False), wide(LANE, True)]
        + [ANY] * na,
        out_shape=[jax.ShapeDtypeStruct((3 * GDN_HEADS, s, hd), F32)] * 2 + [jax.ShapeDtypeStruct((s, LANE), F32)] * 2
        + [jax.ShapeDtypeStruct(g.shape, g.dtype) for g in scatter],
        scratch_shapes=[pltpu.VMEM((NB, hd, hd), F32)] + (_gather_semaphores(na) if na else []),
        compiler_params=_params("arbitrary"),
    )(y, y, y, gb, do, sf_all, tf_all, y, y, y, gb, do, sr_all, tr_all, *scatter)
    return res[:4], res[4:]


def _gdn_post_fwd(o_f, o_r, p_pad, norm_row):
    s = o_f.shape[0]
    tm = min(512, s)
    hd = GDN_HEAD_DIM

    def body(of_ref, or_ref, z_ref, w_ref, out_ref, osum_ref):
        o = of_ref[...] + or_ref[...]
        osum_ref[...] = o
        z = z_ref[...]
        gate = z * _sigmoid(z)
        for h in range(GDN_HEADS):
            sl = slice(h * hd, (h + 1) * hd)
            oh = o[:, sl]
            r = lax.rsqrt(jnp.mean(oh * oh, axis=-1, keepdims=True) + EPS)
            out_ref[:, sl] = (oh * r * w_ref[...] * gate[:, sl]).astype(BF16)

    blk = pl.BlockSpec((tm, GDN_WIDTH), lambda i: (i, 0))
    return pl.pallas_call(
        body, name="gdn_post_fwd", grid=(s // tm,),
        in_specs=[blk, blk, pl.BlockSpec((tm, GDN_WIDTH), lambda i: (i, OFF_Z // GDN_WIDTH)),
                  pl.BlockSpec((1, hd), lambda i: (0, 0))],
        out_specs=[blk, blk],
        out_shape=[jax.ShapeDtypeStruct((s, GDN_WIDTH), BF16), jax.ShapeDtypeStruct((s, GDN_WIDTH), F32)],
        compiler_params=_params("parallel"),
    )(o_f, o_r, p_pad, norm_row)


def _gdn_post_bwd(d_out, o_sum, p_pad, norm_row):
    s = o_sum.shape[0]
    tm = min(512, s)
    hd = GDN_HEAD_DIM

    def body(d_ref, o_ref, z_ref, w_ref, do_ref, dz_ref, dw_ref):
        @pl.when(pl.program_id(0) == 0)
        def _():
            dw_ref[...] = jnp.zeros_like(dw_ref)

        z = z_ref[...]
        sg = _sigmoid(z)
        gate = z * sg
        dgate = sg * (1.0 + z * (1.0 - sg))
        wv = w_ref[...]
        dw = jnp.zeros((1, hd), F32)
        for h in range(GDN_HEADS):
            sl = slice(h * hd, (h + 1) * hd)
            oh = o_ref[:, sl]
            dh = d_ref[:, sl]
            r = lax.rsqrt(jnp.mean(oh * oh, axis=-1, keepdims=True) + EPS)
            ohat = oh * r
            dz_ref[:, sl] = (dh * ohat * wv * dgate[:, sl]).astype(BF16)
            drn = dh * gate[:, sl]
            t = drn * wv
            do_ref[:, sl] = r * (t - ohat * jnp.mean(t * ohat, axis=-1, keepdims=True))
            dw = dw + jnp.sum(drn * ohat, axis=0, keepdims=True)
        dw_ref[...] += dw

    blk = pl.BlockSpec((tm, GDN_WIDTH), lambda i: (i, 0))
    vec = pl.BlockSpec((1, hd), lambda i: (0, 0))
    return pl.pallas_call(
        body, name="gdn_post_bwd", grid=(s // tm,),
        in_specs=[blk, blk, pl.BlockSpec((tm, GDN_WIDTH), lambda i: (i, OFF_Z // GDN_WIDTH)), vec],
        out_specs=[blk, pl.BlockSpec((tm, GDN_WIDTH), lambda i: (i, OFF_Z // GDN_WIDTH)), vec],
        out_shape=[jax.ShapeDtypeStruct((s, GDN_WIDTH), F32), jax.ShapeDtypeStruct((s, N_PAD), BF16),
                   jax.ShapeDtypeStruct((1, hd), F32)],
        compiler_params=_params("arbitrary"),
    )(d_out, o_sum, p_pad, norm_row)


def _gdn_forward(p_pad, conv_wt, alog_row, dt_row, norm_row, gather=()):
    c_pre, y = _gdn_conv_fwd(p_pad, conv_wt)
    gb = _gdn_gates_fwd(p_pad, alog_row, dt_row)
    (o_f, o_r, s_f, s_r, t_f, t_r), gathered = _delta_fwd2(y, gb, gather)
    out, o_sum = _gdn_post_fwd(o_f, o_r, p_pad, norm_row)
    return out, (c_pre, y, gb, s_f, t_f, s_r, t_r, o_sum), gathered


def _gdn_backward(d_out, p_pad, conv_wt, alog_row, dt_row, norm_row, saved, scatter=()):
    c_pre, y, gb, s_f, t_f, s_r, t_r, o_sum = saved
    do, dp_all, dnorm = _gdn_post_bwd(d_out, o_sum, p_pad, norm_row)
    (dy_f, dy_r, dgb_f, dgb_r), received = _delta_bwd2(y, gb, do, s_f, s_r, t_f, t_r, scatter)
    dp_all, dconv = _gdn_conv_bwd(dy_f, dy_r, c_pre, p_pad, conv_wt, dp_all)
    dp_all, gate_sums = _gdn_gates_bwd(dgb_f, dgb_r, p_pad, gb, alog_row, dt_row, dp_all)
    return dp_all, dconv, gate_sums, dnorm, received


ATT_BK = ATT_BQ + 2 * ATT_HALO
ATT_SUB = 8
SWA_SCALE = SWA_HEAD_DIM ** -0.5


def _t5_bucket(rel):
    nb = REL_BUCKETS // 2
    bucket = (rel > 0).astype(np.int32) * nb
    n = np.abs(rel)
    max_exact = nb // 2
    large = max_exact + (np.log(np.maximum(n, 1) / max_exact)
                         / math.log(REL_MAX_DISTANCE / max_exact) * (nb - max_exact)).astype(np.int32)
    large = np.minimum(large, nb - 1)
    return (bucket + np.where(n < max_exact, n, large)).astype(np.int32)


def _band_tables(dilation, queries_are_rows_of_block):
    blk = np.arange(ATT_BQ)
    band = np.arange(ATT_BK) - ATT_HALO
    if queries_are_rows_of_block:
        rel = band[None, :] - blk[:, None]
        band_idx = np.broadcast_to(np.arange(ATT_BK)[None, :], rel.shape)
    else:
        rel = blk[None, :] - band[:, None]
        band_idx = np.broadcast_to(np.arange(ATT_BK)[:, None], rel.shape)
    base = np.abs(rel) <= ATT_HALO
    not_prev = band_idx >= ATT_HALO
    not_next = band_idx < ATT_HALO + ATT_BQ
    valid = np.stack([base & not_prev, base, base & not_next, base & not_prev & not_next])
    return valid, _t5_bucket(rel * dilation)


def _bias_tiles(rel_bias, dilation, queries_are_rows_of_block):
    valid, bucket = _band_tables(dilation, queries_are_rows_of_block)
    onehot = (jnp.asarray(bucket.reshape(-1, 1)) == jnp.arange(REL_BUCKETS, dtype=jnp.int32)[None, :]).astype(F32)
    rb = jnp.dot(onehot, rel_bias.astype(F32), precision=lax.Precision.HIGHEST)
    rb = rb.T.reshape((SWA_HEADS,) + bucket.shape)
    return jnp.where(valid[:, None], rb[None], NEG_BIG).astype(F32)


def _group_sum(x, bd):
    hi = x.astype(BF16)
    lo = (x - hi.astype(F32)).astype(BF16)
    return jnp.dot(hi, bd, preferred_element_type=F32) + jnp.dot(lo, bd, preferred_element_type=F32)


def _head_block_diag():
    idx = np.arange(SWA_WIDTH) // SWA_HEAD_DIM
    return jnp.asarray(idx[:, None] == idx[None, :], BF16)


def _swa_pre_fwd(p_pad, qw_row, kw_row, bd):
    s = p_pad.shape[0]
    tm = min(512, s)
    inv = 1.0 / SWA_HEAD_DIM

    def body(q_ref, k_ref, v_ref, qw_ref, kw_ref, bd_ref, qo_ref, ko_ref, vo_ref):
        bdv = bd_ref[...]
        q = q_ref[...]
        k = k_ref[...]
        rq = lax.rsqrt(_group_sum(q * q, bdv) * inv + EPS)
        rk = lax.rsqrt(_group_sum(k * k, bdv) * inv + EPS)
        qo_ref[...] = (q * rq * qw_ref[...] * SWA_SCALE).astype(BF16)
        ko_ref[...] = (k * rk * kw_ref[...]).astype(BF16)
        vo_ref[...] = v_ref[...].astype(BF16)

    base = OFF_B // SWA_WIDTH
    blk = pl.BlockSpec((tm, SWA_WIDTH), lambda i: (i, 0))
    vec = pl.BlockSpec((1, SWA_WIDTH), lambda i: (0, 0))
    return pl.pallas_call(
        body, name="swa_pre_fwd", grid=(s // tm,),
        in_specs=[pl.BlockSpec((tm, SWA_WIDTH), lambda i: (i, base)), pl.BlockSpec((tm, SWA_WIDTH), lambda i: (i, base + 1)),
                  pl.BlockSpec((tm, SWA_WIDTH), lambda i: (i, base + 2)), vec, vec,
                  pl.BlockSpec((SWA_WIDTH, SWA_WIDTH), lambda i: (0, 0))],
        out_specs=[blk, blk, blk],
        out_shape=[jax.ShapeDtypeStruct((s, SWA_WIDTH), BF16)] * 3,
        compiler_params=_params("parallel"),
    )(p_pad, p_pad, p_pad, qw_row, kw_row, bd)


def _swa_pre_bwd(dqs, dks, dvs, p_pad, qw_row, kw_row, bd, dp_all):
    s = p_pad.shape[0]
    tm = min(256, s)
    inv = 1.0 / SWA_HEAD_DIM
    npat = len(dqs)

    def body(*refs):
        dq_refs, dk_refs, dv_refs = refs[:npat], refs[npat:2 * npat], refs[2 * npat:3 * npat]
        q_ref, k_ref, qw_ref, kw_ref, bd_ref, _, dp_ref, dqw_ref, dkw_ref = refs[3 * npat:]

        @pl.when(pl.program_id(0) == 0)
        def _():
            dqw_ref[...] = jnp.zeros_like(dqw_ref)
            dkw_ref[...] = jnp.zeros_like(dkw_ref)

        bdv = bd_ref[...]

        def norm_bwd(x, g, w, scale):
            r = lax.rsqrt(_group_sum(x * x, bdv) * inv + EPS)
            xhat = x * r
            t = g * w * scale
            dx = r * (t - xhat * (_group_sum(t * xhat, bdv) * inv))
            return dx, jnp.sum(g * scale * xhat, axis=0, keepdims=True)

        def total(rs):
            t = rs[0][...].astype(F32)
            for r in rs[1:]:
                t = t + r[...].astype(F32)
            return t

        dq, dqw = norm_bwd(q_ref[...], total(dq_refs), qw_ref[...], SWA_SCALE)
        dk, dkw = norm_bwd(k_ref[...], total(dk_refs), kw_ref[...], 1.0)
        dp_ref[:, 0:SWA_WIDTH] = dq.astype(BF16)
        dp_ref[:, SWA_WIDTH:2 * SWA_WIDTH] = dk.astype(BF16)
        dp_ref[:, 2 * SWA_WIDTH:3 * SWA_WIDTH] = total(dv_refs).astype(BF16)
        dqw_ref[...] += dqw
        dkw_ref[...] += dkw

    base = OFF_B // SWA_WIDTH
    blk = pl.BlockSpec((tm, SWA_WIDTH), lambda i: (i, 0))
    vec = pl.BlockSpec((1, SWA_WIDTH), lambda i: (0, 0))
    return pl.pallas_call(
        body, name="swa_pre_bwd", grid=(s // tm,),
        in_specs=[blk] * (3 * npat) + [pl.BlockSpec((tm, SWA_WIDTH), lambda i: (i, base)),
                                      pl.BlockSpec((tm, SWA_WIDTH), lambda i: (i, base + 1)), vec, vec,
                                      pl.BlockSpec((SWA_WIDTH, SWA_WIDTH), lambda i: (0, 0)), ANY],
        out_specs=[pl.BlockSpec((tm, 3 * SWA_WIDTH), lambda i: (i, OFF_B // (3 * SWA_WIDTH))), vec, vec],
        out_shape=[jax.ShapeDtypeStruct(dp_all.shape, dp_all.dtype), jax.ShapeDtypeStruct((1, SWA_WIDTH), F32),
                   jax.ShapeDtypeStruct((1, SWA_WIDTH), F32)],
        input_output_aliases={3 * npat + 5: 0},
        compiler_params=_params("arbitrary"),
    )(*dqs, *dks, *dvs, p_pad, p_pad, qw_row, kw_row, bd, dp_all)


def _band_specs(length, rows):
    per = rows // ATT_HALO
    last = length // ATT_HALO - 1
    prev = pl.BlockSpec((ATT_HALO, SWA_WIDTH), lambda r, t: (jnp.maximum(t * per - 1, 0), r))
    cur = pl.BlockSpec((rows, SWA_WIDTH), lambda r, t: (t, r))
    nxt = pl.BlockSpec((ATT_HALO, SWA_WIDTH), lambda r, t: (jnp.minimum((t + 1) * per, last), r))
    return [prev, cur, nxt]


def _tile_variant(t, nb, u, sub):
    first, last = u == 0, u == sub - 1
    if first and last:
        return 3 if nb == 1 else jnp.where(t == 0, 0, jnp.where(t == nb - 1, 2, 1))
    if first:
        return jnp.where(t == 0, 0, 1)
    if last:
        return jnp.where(t == nb - 1, 2, 1)
    return 1


def _bias_specs(nb, sub, rows, cols):
    return [pl.BlockSpec((1, SWA_HEADS, rows, cols),
                         functools.partial(lambda r, t, u: (_tile_variant(t, nb, u, sub), 0, 0, 0), u=u))
            for u in range(sub)]


def _band(refs):
    return jnp.concatenate([r[...] for r in refs], axis=0)


def _sub(u, width=ATT_BQ):
    return slice(u * ATT_BQ, u * ATT_BQ + width)


N_PAIRS = SWA_HEADS // 2


def _pairs(x):
    return jnp.stack([x[:, LANE * p:LANE * (p + 1)] for p in range(N_PAIRS)])


def _per_head_rows(x):
    first = lax.broadcasted_iota(jnp.int32, x.shape, 2) < SWA_HEAD_DIM
    zero = jnp.zeros_like(x)
    return jnp.concatenate([jnp.where(first, x, zero), jnp.where(first, zero, x)], axis=1)


def _per_head_cols(x):
    return jnp.stack([jnp.concatenate([x[:, LANE * p:LANE * p + 1],
                                       x[:, LANE * p + SWA_HEAD_DIM:LANE * p + SWA_HEAD_DIM + 1]], axis=0)
                      for p in range(N_PAIRS)])


def _merge_heads(x, rows):
    first = lax.broadcasted_iota(jnp.int32, (N_PAIRS, rows, LANE), 2) < SWA_HEAD_DIM
    return jnp.where(first, x[:, :rows], x[:, rows:])


def _store_pairs(ref, x, rows):
    for p in range(N_PAIRS):
        ref[rows, LANE * p:LANE * (p + 1)] = x[p].astype(ref.dtype)


def _att_fwd2(q, k, v, bias, dilation):
    s = q.shape[0]
    length = s // dilation
    sub = min(ATT_SUB, length // ATT_BQ)
    rows = sub * ATT_BQ
    nb = length // rows
    view = (length, dilation * SWA_WIDTH)

    def body(q_ref, kp, kc, kn, vp, vc, vn, *rest):
        b_refs, (o_ref, lse_ref) = rest[:sub], rest[sub:]
        kwin, vwin = _band((kp, kc, kn)), _band((vp, vc, vn))
        for u in range(sub):
            kb, vb = _pairs(kwin[_sub(u, ATT_BK)]), _pairs(vwin[_sub(u, ATT_BK)])
            qm = _per_head_rows(_pairs(q_ref[_sub(u), :]))
            sc = _bdot(qm, kb, BNT) + b_refs[u][0].reshape(N_PAIRS, 2 * ATT_BQ, ATT_BK)
            m = jnp.max(sc, axis=-1, keepdims=True)
            p = jnp.exp(sc - m)
            den = jnp.sum(p, axis=-1, keepdims=True)
            o = _bdot(p, vb) / den
            _store_pairs(o_ref, _merge_heads(o, ATT_BQ), _sub(u))
            lse = jnp.broadcast_to(m + jnp.log(den), (N_PAIRS, 2 * ATT_BQ, LANE))
            _store_pairs(lse_ref, _merge_heads(lse, ATT_BQ), _sub(u))

    cur = pl.BlockSpec((rows, SWA_WIDTH), lambda r, t: (t, r))
    o, lse = pl.pallas_call(
        body, name=f"att_fwd_d{dilation}", grid=(dilation, nb),
        in_specs=[cur] + _band_specs(length, rows) * 2 + _bias_specs(nb, sub,ATT_BQ, ATT_BK),
        out_specs=[cur, cur],
        out_shape=[jax.ShapeDtypeStruct(view, BF16), jax.ShapeDtypeStruct(view, F32)],
        compiler_params=_params("parallel", "parallel"),
    )(q.reshape(view), *([k.reshape(view)] * 3), *([v.reshape(view)] * 3), *([bias] * sub))
    return o.reshape(s, SWA_WIDTH), lse.reshape(s, SWA_WIDTH)


def _att_dq2(q, k, v, dop, lse, cp, bias, dilation):
    s = q.shape[0]
    length = s // dilation
    sub = min(ATT_SUB, length // ATT_BQ)
    rows = sub * ATT_BQ
    nb = length // rows
    view = (length, dilation * SWA_WIDTH)

    def body(q_ref, kp, kc, kn, vp, vc, vn, do_ref, lse_ref, cp_ref, *rest):
        b_refs, (dq_ref, db_ref) = rest[:sub], rest[sub:]

        @pl.when((pl.program_id(0) == 0) & (pl.program_id(1) == 0))
        def _():
            db_ref[...] = jnp.zeros_like(db_ref)

        kwin, vwin = _band((kp, kc, kn)), _band((vp, vc, vn))
        for u in range(sub):
            kb, vb = _pairs(kwin[_sub(u, ATT_BK)]), _pairs(vwin[_sub(u, ATT_BK)])
            qm = _per_head_rows(_pairs(q_ref[_sub(u), :]))
            dom = _per_head_rows(_pairs(do_ref[_sub(u), :]))
            sc = _bdot(qm, kb, BNT) + b_refs[u][0].reshape(N_PAIRS, 2 * ATT_BQ, ATT_BK)
            p = jnp.exp(sc - _per_head_cols(lse_ref[_sub(u), :]))
            ds = p * (_bdot(dom, vb, BNT) + _per_head_cols(cp_ref[_sub(u), :]))
            _store_pairs(dq_ref, _merge_heads(_bdot(ds, kb), ATT_BQ), _sub(u))
            db_ref[_tile_variant(pl.program_id(1), nb, u, sub)] += ds.reshape(SWA_HEADS, ATT_BQ, ATT_BK)

    cur = pl.BlockSpec((rows, SWA_WIDTH), lambda r, t: (t, r))
    dq, db = pl.pallas_call(
        body, name=f"att_dq_d{dilation}", grid=(dilation, nb),
        in_specs=[cur] + _band_specs(length, rows) * 2 + [cur, cur, cur] + _bias_specs(nb, sub,ATT_BQ, ATT_BK),
        out_specs=[cur, pl.BlockSpec((4, SWA_HEADS, ATT_BQ, ATT_BK), lambda r, t: (0, 0, 0, 0))],
        out_shape=[jax.ShapeDtypeStruct(view, BF16), jax.ShapeDtypeStruct((4, SWA_HEADS, ATT_BQ, ATT_BK), F32)],
        compiler_params=_params("arbitrary", "arbitrary"),
    )(q.reshape(view), *([k.reshape(view)] * 3), *([v.reshape(view)] * 3), dop.reshape(view), lse.reshape(view),
      cp.reshape(view), *([bias] * sub))
    return dq.reshape(s, SWA_WIDTH), db


def _att_dkv2(q, k, v, dop, lse, cp, bias_t, dilation):
    s = q.shape[0]
    length = s // dilation
    sub = min(ATT_SUB, length // ATT_BQ)
    rows = sub * ATT_BQ
    nb = length // rows
    view = (length, dilation * SWA_WIDTH)

    def body(k_ref, v_ref, qp, qc, qn, dp_, dc_, dn_, lp, lc, ln, cp_, cc_, cn_, *rest):
        b_refs, (dk_ref, dv_ref) = rest[:sub], rest[sub:]
        qwin, dowin = _band((qp, qc, qn)), _band((dp_, dc_, dn_))
        lsewin, cpwin = _band((lp, lc, ln)), _band((cp_, cc_, cn_))
        for u in range(sub):
            band = _sub(u, ATT_BK)
            qm = _per_head_rows(_pairs(qwin[band]))
            dom = _per_head_rows(_pairs(dowin[band]))
            kv, vv = _pairs(k_ref[_sub(u), :]), _pairs(v_ref[_sub(u), :])
            sc = _bdot(qm, kv, BNT) + b_refs[u][0].reshape(N_PAIRS, 2 * ATT_BK, ATT_BQ)
            p = jnp.exp(sc - _per_head_cols(lsewin[band]))
            _store_pairs(dv_ref, _bdot(p, dom, BTN), _sub(u))
            ds = p * (_bdot(dom, vv, BNT) + _per_head_cols(cpwin[band]))
            _store_pairs(dk_ref, _bdot(ds, qm, BTN), _sub(u))

    cur = pl.BlockSpec((rows, SWA_WIDTH), lambda r, t: (t, r))
    dk, dv = pl.pallas_call(
        body, name=f"att_dkv_d{dilation}", grid=(dilation, nb),
        in_specs=[cur, cur] + _band_specs(length, rows) * 4 + _bias_specs(nb, sub,ATT_BK, ATT_BQ),
        out_specs=[cur, cur],
        out_shape=[jax.ShapeDtypeStruct(view, BF16)] * 2,
        compiler_params=_params("parallel", "parallel"),
    )(k.reshape(view), v.reshape(view), *([q.reshape(view)] * 3), *([dop.reshape(view)] * 3),
      *([lse.reshape(view)] * 3), *([cp.reshape(view)] * 3), *([bias_t] * sub))
    return dk.reshape(s, SWA_WIDTH), dv.reshape(s, SWA_WIDTH)


def _pattern_weights(lses):
    m = lses[0]
    for l in lses[1:]:
        m = jnp.maximum(m, l)
    es = [jnp.exp(l - m) for l in lses]
    den = es[0]
    for e in es[1:]:
        den = den + e
    return [e / den for e in es]


def _combine_fwd(outs, lses):
    s = outs[0].shape[0]
    tm = min(512, s)
    npat = len(outs)

    def body(*refs):
        ws = _pattern_weights([r[...] for r in refs[npat:2 * npat]])
        o = ws[0] * refs[0][...]
        for p in range(1, npat):
            o = o + ws[p] * refs[p][...]
        refs[2 * npat][...] = o.astype(BF16)

    blk = pl.BlockSpec((tm, SWA_WIDTH), lambda i: (i, 0))
    return pl.pallas_call(
        body, name="swa_combine_fwd", grid=(s // tm,), in_specs=[blk] * (2 * npat), out_specs=blk,
        out_shape=jax.ShapeDtypeStruct((s, SWA_WIDTH), BF16), compiler_params=_params("parallel"),
    )(*outs, *lses)


def _combine_bwd(d_out, outs, lses, bd):
    s = d_out.shape[0]
    tm = min(512, s)
    npat = len(outs)

    def body(*refs):
        d_ref, bd_ref = refs[0], refs[1 + 2 * npat]
        o_refs, l_refs = refs[1:1 + npat], refs[1 + npat:1 + 2 * npat]
        out_refs = refs[2 + 2 * npat:]
        ws = _pattern_weights([r[...] for r in l_refs])
        dov = d_ref[...]
        o = ws[0] * o_refs[0][...]
        for p in range(1, npat):
            o = o + ws[p] * o_refs[p][...]
        rd = _group_sum(dov * o, bd_ref[...])
        for p in range(npat):
            out_refs[p][...] = (ws[p] * dov).astype(BF16)
            out_refs[npat + p][...] = -ws[p] * rd

    blk = pl.BlockSpec((tm, SWA_WIDTH), lambda i: (i, 0))
    res = pl.pallas_call(
        body, name="swa_combine_bwd", grid=(s // tm,),
        in_specs=[blk] * (1 + 2 * npat) + [pl.BlockSpec((SWA_WIDTH, SWA_WIDTH), lambda i: (0, 0))],
        out_specs=[blk] * (2 * npat),
        out_shape=[jax.ShapeDtypeStruct((s, SWA_WIDTH), BF16)] * npat + [jax.ShapeDtypeStruct((s, SWA_WIDTH), F32)] * npat,
        compiler_params=_params("parallel"),
    )(d_out, *outs, *lses, bd)
    return res[:npat], res[npat:]


def _rel_bias_grad(dbs, buckets):
    npat = len(dbs)

    def body(*refs):
        db_refs, bk_refs, o_ref = refs[:npat], refs[npat:2 * npat], refs[2 * npat]
        row = lax.broadcasted_iota(jnp.int32, (REL_BUCKETS, LANE), 0)
        lane = lax.broadcasted_iota(jnp.int32, (REL_BUCKETS, LANE), 1)
        tiles = [[db_refs[p][0, h] + db_refs[p][1, h] + db_refs[p][2, h] + db_refs[p][3, h] for h in range(SWA_HEADS)]
                 for p in range(npat)]
        bks = [r[...] for r in bk_refs]

        def one_bucket(b, acc):
            for h in range(SWA_HEADS):
                tot = jnp.zeros((1, 1), F32)
                for p in range(npat):
                    sel = jnp.where(bks[p] == b, tiles[p][h], 0.0)
                    tot = tot + jnp.sum(jnp.sum(sel, axis=1, keepdims=True), axis=0, keepdims=True)
                acc = acc + jnp.where((row == b) & (lane == h), tot, 0.0)
            return acc

        o_ref[...] = lax.fori_loop(0, REL_BUCKETS, one_bucket, jnp.zeros((REL_BUCKETS, LANE), F32))

    full4 = pl.BlockSpec((4, SWA_HEADS, ATT_BQ, ATT_BK), lambda: (0, 0, 0, 0))
    full2 = pl.BlockSpec((ATT_BQ, ATT_BK), lambda: (0, 0))
    return pl.pallas_call(
        body, name="rel_bias_grad", in_specs=[full4] * npat + [full2] * npat,
        out_specs=pl.BlockSpec((REL_BUCKETS, LANE), lambda: (0, 0)),
        out_shape=jax.ShapeDtypeStruct((REL_BUCKETS, LANE), F32),
        compiler_params=pltpu.CompilerParams(vmem_limit_bytes=V7X_VMEM_LIMIT_BYTES),
    )(*dbs, *buckets)


def _swa_forward(p_pad, qw_row, kw_row, rel_bias, bd):
    q, k, v = _swa_pre_fwd(p_pad, qw_row, kw_row, bd)
    outs, lses = [], []
    for _, dil in DILATION_PATTERNS:
        o, lse = _att_fwd2(q, k, v, _bias_tiles(rel_bias, dil, True), dil)
        outs.append(o)
        lses.append(lse)
    return _combine_fwd(outs, lses), (q, k, v, outs, lses)


def _swa_backward(d_out, p_pad, qw_row, kw_row, rel_bias, bd, saved, dp_all):
    q, k, v, outs, lses = saved
    dops, cps = _combine_bwd(d_out, outs, lses, bd)
    dqs, dks, dvs, dbs, buckets = [], [], [], [], []
    for p, (_, dil) in enumerate(DILATION_PATTERNS):
        dq, db = _att_dq2(q, k, v, dops[p], lses[p], cps[p], _bias_tiles(rel_bias, dil, True), dil)
        dk, dv = _att_dkv2(q, k, v, dops[p], lses[p], cps[p], _bias_tiles(rel_bias, dil, False), dil)
        dqs.append(dq)
        dks.append(dk)
        dvs.append(dv)
        dbs.append(db)
        buckets.append(jnp.asarray(_band_tables(dil, True)[1]))
    dp, dqw, dkw = _swa_pre_bwd(dqs, dks, dvs, p_pad, qw_row, kw_row, bd, dp_all)
    return dp, dqw, dkw, _rel_bias_grad(dbs, buckets)


def _lane_row(v):
    flat = v.reshape(-1).astype(F32)
    return jnp.zeros((1, LANE), F32).at[0, :flat.shape[0]].set(flat)


W_IN_SHARD = N_IN // N_DEV
W_IN_RUNS = ((0, NAT_Z, 0), (NAT_Z, NAT_AB, OFF_Z), (NAT_AB, NAT_B, OFF_AB), (NAT_B, N_IN, OFF_B))


def _w_in_pieces(shard):
    lo, hi = shard * W_IN_SHARD, (shard + 1) * W_IN_SHARD
    out = []
    for first, last, dst in W_IN_RUNS:
        a, b = max(lo, first), min(hi, last)
        if a < b:
            out.append((a - lo, b - a, dst + a - first))
    return out


def _w_in_from_slabs(w3):
    nd, r, _ = w3.shape

    def body(w_ref, o_ref):
        o_ref[:, OFF_AB:N_PAD] = jnp.zeros((r, N_PAD - OFF_AB), w3.dtype)
        for sh in range(nd):
            for src, length, dst in _w_in_pieces(sh):
                o_ref[:, dst:dst + length] = w_ref[sh, :, src:src + length]

    return pl.pallas_call(
        body, name="w_in_from_slabs", out_shape=jax.ShapeDtypeStruct((r, N_PAD), w3.dtype),
        compiler_params=pltpu.CompilerParams(vmem_limit_bytes=V7X_VMEM_LIMIT_BYTES),
    )(w3)


def _w_in_grad_slabs(dw_pad, dtype):
    r = dw_pad.shape[0]

    def body(dw_ref, o_ref):
        for sh in range(N_DEV):
            for src, length, dst in _w_in_pieces(sh):
                o_ref[sh, :, src:src + length] = dw_ref[:, dst:dst + length].astype(dtype)

    return pl.pallas_call(
        body, name="w_in_grad_slabs", out_shape=jax.ShapeDtypeStruct((N_DEV, r, W_IN_SHARD), dtype),
        compiler_params=pltpu.CompilerParams(vmem_limit_bytes=V7X_VMEM_LIMIT_BYTES),
    )(dw_pad)


LATE = ("w_out", "ffn2_w_gate", "ffn2_w_up", "ffn2_w_down")
TRANSPOSED = ("ffn1_w_gate", "ffn1_w_up", "ffn2_w_gate", "ffn2_w_up")


def _late_weights(slabs):
    return {n: g.reshape(N_DEV * g.shape[1], g.shape[2]) for n, g in zip(LATE, slabs)}


def _local_step(x, tgt, wts, small, late_shards=None):
    bd = _head_block_diag()
    alog_row, dt_row = _lane_row(small["a_log"]), _lane_row(small["dt_bias"])
    gnorm_row = small["gdn_norm_w"].reshape(1, GDN_HEAD_DIM)
    qw_row = jnp.tile(small["q_norm_w"].reshape(-1), SWA_HEADS).reshape(1, SWA_WIDTH)
    kw_row = jnp.tile(small["k_norm_w"].reshape(-1), SWA_HEADS).reshape(1, SWA_WIDTH)
    rel_bias = small["rel_bias"]
    exchange = late_shards is not None
    dw_dtype = BF16 if exchange else F32

    x1, sv1, (wg1, wu1, wd1), got, first, (n2, r2) = _ffn_forward(
        x, small["ffn1_norm"], wts.get("ffn1_w_gate"), wts.get("ffn1_w_up"), wts.get("ffn1_w_down"), "ffn1",
        gather=[late_shards["ffn1_w_down"], late_shards["w_in"]] if exchange else (),
        head=[late_shards["ffn1_w_gate"], late_shards["ffn1_w_up"], late_shards["conv_w"]] if exchange else (),
        next_norm=small["mix_norm"])
    win_pad = _w_in_from_slabs(got[0]) if exchange else wts["w_in_pad"]
    conv_w = first[0].reshape(N_DEV, -1)[:, :QKV_A // N_DEV * CONV_WIDTH].reshape(QKV_A, CONV_WIDTH) if exchange \
        else small["conv_w"]
    conv_wt = jnp.zeros((8, QKV_A), F32).at[:CONV_WIDTH].set(conv_w.T)
    p_pad = _matmul([(n2, win_pad)], tm=256, tn=N_PAD, tk=D_MODEL, name="w_in")
    o_a, sva, gathered = _gdn_forward(p_pad, conv_wt, alog_row, dt_row, gnorm_row,
                                      gather=[late_shards[n] for n in LATE] if exchange else ())
    if exchange:
        wts = {**wts, **_late_weights(gathered)}
    wo_a, wo_b = wts["w_out"][:GDN_WIDTH], wts["w_out"][GDN_WIDTH:]
    o_b, svb = _swa_forward(p_pad, qw_row, kw_row, rel_bias, bd)
    x2, n3, r3 = _matmul([(o_a, wo_a), (o_b, wo_b)], tm=512, tn=D_MODEL, tk=GDN_WIDTH, name="w_out", res=x1,
                         norm_fwd=small["ffn2_norm"])
    x3, sv2, _, _, _, _ = _ffn_forward(x2, small["ffn2_norm"], wts["ffn2_w_gate"], wts["ffn2_w_up"], wts["ffn2_w_down"],
                                       "ffn2", normed=(n3, r3))
    loss_row, dx3, d_final = _final_loss(x3, small["final_norm"], tgt)

    dx2, d_ffn2_norm, dwg2, dwu2, dwd2, _ = _ffn_backward(
        dx3, x2, small["ffn2_norm"], wts["ffn2_w_gate"], wts["ffn2_w_up"], wts["ffn2_w_down"], sv2, "ffn2", dw_dtype)
    d_oa = _matmul([(dx2, wo_a)], tb=True, tm=512, tn=GDN_WIDTH, tk=D_MODEL, name="w_out_da")
    d_ob = _matmul([(dx2, wo_b)], tb=True, tm=512, tn=SWA_WIDTH, tk=D_MODEL, name="w_out_db")
    dwo_a = _matmul([(o_a, dx2)], ta=True, tm=GDN_WIDTH, tn=D_MODEL, tk=2048, name="w_out_dwa", out_dtype=dw_dtype)
    dwo_b = _matmul([(o_b, dx2)], ta=True, tm=SWA_WIDTH, tn=D_MODEL, tk=2048, name="w_out_dwb", out_dtype=dw_dtype)

    late_grads = [_row_slabs(jnp.concatenate([dwo_a, dwo_b], axis=0)), dwg2, dwu2, dwd2]
    dp_all, dconv, gate_sums, d_gnorm, received = _gdn_backward(
        d_oa, p_pad, conv_wt, alog_row, dt_row, gnorm_row, sva, scatter=late_grads if exchange else ())
    if exchange:
        late_grads = received
    dp_all, dqw, dkw, d_rel = _swa_backward(d_ob, p_pad, qw_row, kw_row, rel_bias, bd, svb, dp_all)
    dw_pad = _matmul([(n2, dp_all)], ta=True, tm=D_MODEL, tn=N_PAD // 3, tk=2048, name="w_in_dw")
    dx1, d_mix_norm = _matmul([(dp_all, win_pad)], tb=True, tm=512, tn=D_MODEL, tk=N_PAD, name="w_in_dn",
                              norm_bwd=(x1, r2, small["mix_norm"], dx2))
    d_w_in = _w_in_grad_slabs(dw_pad, dw_dtype)
    dx, d_ffn1_norm, dwg1, dwu1, dwd1, got = _ffn_backward(
        dx1, x, small["ffn1_norm"], wg1, wu1, wd1, sv1, "ffn1", dw_dtype,
        scatter=[d_w_in] if exchange else None)
    if exchange:
        d_w_in = got[0]

    grads = {
        "ffn1_norm": d_ffn1_norm, "ffn1_w_gate": dwg1, "ffn1_w_up": dwu1, "ffn1_w_down": dwd1,
        "mix_norm": d_mix_norm, "w_in": d_w_in, "conv_w": dconv[:CONV_WIDTH].T,
        "a_log": gate_sums[0, :8].reshape(2, GDN_HEADS), "dt_bias": gate_sums[1, :8].reshape(2, GDN_HEADS),
        "gdn_norm_w": d_gnorm, "q_norm_w": dqw.reshape(SWA_HEADS, SWA_HEAD_DIM).sum(0, keepdims=True),
        "k_norm_w": dkw.reshape(SWA_HEADS, SWA_HEAD_DIM).sum(0, keepdims=True), "rel_bias": d_rel[:, :SWA_HEADS],
        "ffn2_norm": d_ffn2_norm, "final_norm": d_final, **dict(zip(LATE, late_grads)),
    }
    return loss_row, dx, grads


MESH_IDS = pl.DeviceIdType.MESH
ANY = pl.BlockSpec(memory_space=pl.ANY)


def _adamw(parts, w, m, v, name):
    nparts, r, n = parts.shape
    tr = r
    for cand in (256, 176, 128, 104, 64, 8):
        if r % cand == 0:
            tr = cand
            break
    bc1 = 1.0 - ADAM_B1 ** ADAM_STEP
    bc2 = 1.0 - ADAM_B2 ** ADAM_STEP

    def body(p_ref, w_ref, m_ref, v_ref, g_ref, d_ref, nm_ref, nv_ref):
        g = p_ref[0].astype(F32)
        for k in range(1, nparts):
            g = g + p_ref[k].astype(F32)
        mn = ADAM_B1 * m_ref[...] + (1.0 - ADAM_B1) * g
        vn = ADAM_B2 * v_ref[...] + (1.0 - ADAM_B2) * (g * g)
        m_hat = mn / bc1
        v_hat = vn / bc2
        g_ref[...] = g
        nm_ref[...] = mn
        nv_ref[...] = vn
        d_ref[...] = -ADAM_LR * (m_hat / (jnp.sqrt(v_hat) + ADAM_EPS) + ADAM_WD * w_ref[...])

    blk = pl.BlockSpec((tr, n), lambda i: (i, 0))
    return pl.pallas_call(
        body, name=name, grid=(r // tr,),
        in_specs=[pl.BlockSpec((nparts, tr, n), lambda i: (0, i, 0)), blk, blk, blk],
        out_specs=[blk] * 4, out_shape=[jax.ShapeDtypeStruct((r, n), F32)] * 4,
        compiler_params=_params("parallel"),
    )(parts, w, m, v)


def _mesh_place():
    x, y, c = lax.axis_index("x"), lax.axis_index("y"), lax.axis_index("c")
    return x, y, c, [(1 - x, y), (x, 1 - y), (1 - x, 1 - y)]


def _gather_phases(x_refs, out_refs, send_sems, recv_sems, local_sems):
    na = len(x_refs)

    def place():
        x, y, c, chips = _mesh_place()
        return (x, y, c), (x, y, 1 - c), chips, c

    def slab(i, px, py, pc):
        return out_refs[i].at[4 * px + 2 * py + pc]

    def copy(i, k, block, to, src=None):
        return pltpu.make_async_remote_copy(
            src_ref=slab(i, *block) if src is None else src, dst_ref=slab(i, *block),
            send_sem=send_sems.at[i, k], recv_sem=recv_sems.at[i, k], device_id=to, device_id_type=MESH_IDS)

    def own(i, me):
        return pltpu.make_async_copy(x_refs[i], slab(i, *me), local_sems.at[i])

    def sends(i, me, sibling, chips, c):
        return [copy(i, 0, me, sibling, src=x_refs[i])] + [copy(i, 1 + j, me, (*chip, c), src=x_refs[i])
                                                          for j, chip in enumerate(chips)]

    def start():
        me, sibling, chips, c = place()
        for i in range(na):
            own(i, me).start()
            for cp in sends(i, me, sibling, chips, c):
                cp.start()

    def forward():
        me, sibling, chips, c = place()
        for j, chip in enumerate(chips):
            for i in range(na):
                copy(i, 1 + j, (*chip, c), me).wait_recv()
                copy(i, 4 + j, (*chip, c), sibling).start()

    def finish():
        me, sibling, chips, c = place()
        for i in range(na):
            copy(i, 0, sibling, me).wait_recv()
        for j, chip in enumerate(chips):
            for i in range(na):
                copy(i, 4 + j, (*chip, 1 - c), me).wait_recv()
        for i in range(na):
            for cp in sends(i, me, sibling, chips, c):
                cp.wait_send()
            for j, chip in enumerate(chips):
                copy(i, 4 + j, (*chip, c), sibling).wait_send()
            own(i, me).wait()

    return start, forward, finish


def _gather_semaphores(na):
    return [pltpu.SemaphoreType.DMA((na, 7)), pltpu.SemaphoreType.DMA((na, 7)), pltpu.SemaphoreType.DMA((na,))]


def _scatter_phases(g_refs, out_refs, send_sems, recv_sems, local_sems):
    na = len(g_refs)

    def place(m):
        x, y, c = lax.axis_index("x"), lax.axis_index("y"), lax.axis_index("c")
        px = 1 - x if m & 4 else x
        py = 1 - y if m & 2 else y
        pc = 1 - c if m & 1 else c
        return 4 * x + 2 * y + c, (px, py, pc), 4 * px + 2 * py + pc

    def own(i):
        me, _, _ = place(0)
        return pltpu.make_async_copy(g_refs[i].at[me], out_refs[i].at[me], local_sems.at[i])

    def start():
        for i in range(na):
            own(i).start()
            for m in range(1, N_DEV):
                me, peer, peer_idx = place(m)
                pltpu.make_async_remote_copy(
                    src_ref=g_refs[i].at[peer_idx], dst_ref=out_refs[i].at[me], send_sem=send_sems.at[i, m - 1],
                    recv_sem=recv_sems.at[i, m - 1], device_id=peer, device_id_type=MESH_IDS).start()

    def finish():
        for i in range(na):
            for m in range(1, N_DEV):
                me, peer, peer_idx = place(m)
                cp = pltpu.make_async_remote_copy(
                    src_ref=g_refs[i].at[peer_idx], dst_ref=out_refs[i].at[peer_idx], send_sem=send_sems.at[i, m - 1],
                    recv_sem=recv_sems.at[i, m - 1], device_id=peer, device_id_type=MESH_IDS)
                cp.wait_recv()
                cp.wait_send()
            own(i).wait()

    return start, finish


def _all_gather_many(vs, name):
    na = len(vs)

    def body(*refs):
        x_refs, out_refs = refs[:na], refs[na:2 * na]
        for step in _gather_phases(x_refs, out_refs, *refs[2 * na:]):
            step()

    return pl.pallas_call(
        body, name=name, in_specs=[ANY] * na, out_specs=[ANY] * na,
        out_shape=[jax.ShapeDtypeStruct((N_DEV,) + v.shape, v.dtype) for v in vs],
        scratch_shapes=_gather_semaphores(na),
        compiler_params=pltpu.CompilerParams(vmem_limit_bytes=V7X_VMEM_LIMIT_BYTES),
    )(*vs)


BIG = ("ffn1_w_gate", "ffn1_w_up", "ffn1_w_down", "w_in", "w_out", "ffn2_w_gate", "ffn2_w_up", "ffn2_w_down")
SMALL = ("ffn1_norm", "mix_norm", "a_log", "dt_bias", "gdn_norm_w", "q_norm_w", "k_norm_w", "rel_bias",
         "ffn2_norm", "final_norm")
WEIGHTS = ("ffn1_norm", "ffn1_w_gate", "ffn1_w_up", "ffn1_w_down", "mix_norm", "w_in", "conv_w", "a_log", "dt_bias",
           "gdn_norm_w", "q_norm_w", "k_norm_w", "rel_bias", "w_out", "ffn2_norm", "ffn2_w_gate", "ffn2_w_up",
           "ffn2_w_down", "final_norm")


def _pack(arrays, width, row_multiple):
    flat = jnp.concatenate([a.reshape(-1) for a in arrays])
    rows = -(-flat.shape[0] // width)
    rows = -(-rows // row_multiple) * row_multiple
    return jnp.pad(flat, (0, rows * width - flat.shape[0])).reshape(rows, width)


def _unpack(packed, shapes):
    flat = packed.reshape(-1)
    out, pos = [], 0
    for shp in shapes:
        size = int(np.prod(shp))
        out.append(flat[pos:pos + size].reshape(shp))
        pos += size
    return out


def kernel(x, ffn1_norm, ffn1_w_gate, ffn1_w_up, ffn1_w_down, mix_norm, w_in, conv_w, a_log, dt_bias, gdn_norm_w, q_norm_w, k_norm_w, rel_bias, w_out, ffn2_norm, ffn2_w_gate, ffn2_w_up, ffn2_w_down, final_norm, loss_target, m_ffn1_norm, m_ffn1_w_gate, m_ffn1_w_up, m_ffn1_w_down, m_mix_norm, m_w_in, m_conv_w, m_a_log, m_dt_bias, m_gdn_norm_w, m_q_norm_w, m_k_norm_w, m_rel_bias, m_w_out, m_ffn2_norm, m_ffn2_w_gate, m_ffn2_w_up, m_ffn2_w_down, m_final_norm, v_ffn1_norm, v_ffn1_w_gate, v_ffn1_w_up, v_ffn1_w_down, v_mix_norm, v_w_in, v_conv_w, v_a_log, v_dt_bias, v_gdn_norm_w, v_q_norm_w, v_k_norm_w, v_rel_bias, v_w_out, v_ffn2_norm, v_ffn2_w_gate, v_ffn2_w_up, v_ffn2_w_down, v_final_norm):
    w = dict(ffn1_norm=ffn1_norm, ffn1_w_gate=ffn1_w_gate, ffn1_w_up=ffn1_w_up, ffn1_w_down=ffn1_w_down, mix_norm=mix_norm, w_in=w_in, conv_w=conv_w, a_log=a_log, dt_bias=dt_bias, gdn_norm_w=gdn_norm_w, q_norm_w=q_norm_w, k_norm_w=k_norm_w, rel_bias=rel_bias, w_out=w_out, ffn2_norm=ffn2_norm, ffn2_w_gate=ffn2_w_gate, ffn2_w_up=ffn2_w_up, ffn2_w_down=ffn2_w_down, final_norm=final_norm)
    mom = dict(ffn1_norm=m_ffn1_norm, ffn1_w_gate=m_ffn1_w_gate, ffn1_w_up=m_ffn1_w_up, ffn1_w_down=m_ffn1_w_down, mix_norm=m_mix_norm, w_in=m_w_in, conv_w=m_conv_w, a_log=m_a_log, dt_bias=m_dt_bias, gdn_norm_w=m_gdn_norm_w, q_norm_w=m_q_norm_w, k_norm_w=m_k_norm_w, rel_bias=m_rel_bias, w_out=m_w_out, ffn2_norm=m_ffn2_norm, ffn2_w_gate=m_ffn2_w_gate, ffn2_w_up=m_ffn2_w_up, ffn2_w_down=m_ffn2_w_down, final_norm=m_final_norm)
    var = dict(ffn1_norm=v_ffn1_norm, ffn1_w_gate=v_ffn1_w_gate, ffn1_w_up=v_ffn1_w_up, ffn1_w_down=v_ffn1_w_down, mix_norm=v_mix_norm, w_in=v_w_in, conv_w=v_conv_w, a_log=v_a_log, dt_bias=v_dt_bias, gdn_norm_w=v_gdn_norm_w, q_norm_w=v_q_norm_w, k_norm_w=v_k_norm_w, rel_bias=v_rel_bias, w_out=v_w_out, ffn2_norm=v_ffn2_norm, ffn2_w_gate=v_ffn2_w_gate, ffn2_w_up=v_ffn2_w_up, ffn2_w_down=v_ffn2_w_down, final_norm=v_final_norm)
    ix, iy, ic = lax.axis_index("x"), lax.axis_index("y"), lax.axis_index("c")
    me = 4 * ix + 2 * iy + ic

    def local(a, n):
        return jnp.swapaxes(a[0], 0, 1) if n in TRANSPOSED else a[0]

    shard = {n: local(w[n], n) for n in BIG}

    conv_shard_shape = w["conv_w"][0].shape
    small = {n: w[n][0] if n not in ("rel_bias",) else w[n] for n in SMALL}
    small = {n: (a.reshape(1, -1) if n.endswith("norm") else a) for n, a in small.items()}
    shards = {n: shard[n].astype(BF16) for n in BIG}
    shards["conv_w"] = _pack([w["conv_w"][0]], LANE, 8)
    loss_row, grad_x, grads = _local_step(x[0], loss_target[0], {}, small, late_shards=shards)

    big_out = [[], [], [], []]
    for n in BIG:
        for kind, val in enumerate(_adamw(grads[n], shard[n], local(mom[n], n), local(var[n], n), f"{n}_adamw")):
            big_out[kind].append(jnp.swapaxes(val, 0, 1) if n in TRANSPOSED else val)

    small_names = SMALL + ("conv_w",)
    small_shapes = [grads[n].shape for n in small_names] + [(1, 1)]
    g_small = _pack([grads[n] for n in small_names] + [loss_row[:, :1]], LANE, 8)
    all_small = _all_gather_many([g_small], "gather_small_grads")[0]
    riders = [jnp.zeros(shp, F32) for shp in small_shapes[len(SMALL):]]
    ws = _pack([w[n].reshape(grads[n].shape) for n in SMALL] + riders, LANE, 8)
    ms = _pack([mom[n].reshape(grads[n].shape) for n in SMALL] + riders, LANE, 8)
    vs = _pack([var[n].reshape(grads[n].shape) for n in SMALL] + riders, LANE, 8)
    small_out = [_unpack(a, small_shapes) for a in _adamw(all_small, ws, ms, vs, "adamw_small")]
    loss = small_out[0][-1][0, 0]
    conv_g = lax.dynamic_slice_in_dim(small_out[0][len(SMALL)], me * conv_shard_shape[0], conv_shard_shape[0], axis=0)
    conv_out = [_unpack(a, [conv_shard_shape])[0] for a in _adamw(
        _pack([conv_g], LANE, 8)[None], _pack([w["conv_w"][0]], LANE, 8), _pack([mom["conv_w"][0]], LANE, 8),
        _pack([var["conv_w"][0]], LANE, 8), "adamw_conv")]

    def leaf(kind, n):
        if n in BIG:
            val = big_out[kind][BIG.index(n)]
        elif n == "conv_w":
            val = conv_out[kind]
        else:
            val = small_out[kind][SMALL.index(n)]
        return val.reshape(w[n].shape)

    outs = [loss, grad_x[None]]
    for kind in range(4):
        outs += [leaf(kind, n) for n in WEIGHTS]
    return tuple(outs)
```

```python
import functools
import math

import numpy as np
import jax
import jax.numpy as jnp
from jax import lax
from jax.experimental import pallas as pl
from jax.experimental.pallas import tpu as pltpu

F32 = jnp.float32
BF16 = jnp.bfloat16

D_MODEL = 1024
D_FF = 2816
GDN_HEADS = 4
GDN_HEAD_DIM = 128
GDN_WIDTH = 512
CONV_WIDTH = 5
CHUNK = 64
SWA_HEADS = 8
SWA_HEAD_DIM = 64
SWA_WIDTH = 512
DILATION_PATTERNS = ((128, 1), (512, 4), (2048, 16))
REL_BUCKETS = 32
REL_MAX_DISTANCE = 1024
EPS = 1e-6
NEG_BIG = -1e30
N_DEV = 8

ADAM_LR = 0.001
ADAM_B1 = 0.9
ADAM_B2 = 0.999
ADAM_EPS = 1e-08
ADAM_WD = 0.01
ADAM_STEP = 10

QKV_A = 3 * GDN_WIDTH
OFF_B = QKV_A
OFF_Z = OFF_B + 3 * SWA_WIDTH
OFF_AB = OFF_Z + GDN_WIDTH
N_PAD = OFF_AB + 256
N_IN = 3600
NAT_Z, NAT_AB, NAT_B = QKV_A, QKV_A + GDN_WIDTH, QKV_A + GDN_WIDTH + 16

V7X_VMEM_LIMIT_BYTES = 56 * 1024 * 1024
LANE = 128
ATT_BQ = 128
ATT_HALO = 64
CONV_ROWS = 256

NN = (((1,), (0,)), ((), ()))
NT = (((1,), (1,)), ((), ()))
TN = (((0,), (0,)), ((), ()))


def _params(*sem):
    return pltpu.CompilerParams(dimension_semantics=sem, vmem_limit_bytes=V7X_VMEM_LIMIT_BYTES)


def _dot(a, b, dn=NN):
    return lax.dot_general(a.astype(BF16), b.astype(BF16), dn, preferred_element_type=F32)


def _sigmoid(x):
    return 1.0 / (1.0 + jnp.exp(-x))


class _Exchange:
    def __init__(self, kind, arrays):
        self.kind, self.arrays = kind, list(arrays)

    def out_shape(self):
        lead = (N_DEV,) if self.kind == "gather" else ()
        return [jax.ShapeDtypeStruct(lead + v.shape, v.dtype) for v in self.arrays]

    def hooks(self, in_refs, out_refs, sems, grid):
        step = pl.program_id(0)
        for axis in range(1, len(grid)):
            step = step * grid[axis] + pl.program_id(axis)
        total = math.prod(grid)
        if self.kind == "gather":
            assert total >= 4
            start, forward, finish = _gather_phases(in_refs, out_refs, *sems)
            pl.when(step == total // 2)(forward)
        else:
            assert total >= 2
            start, finish = _scatter_phases(in_refs, out_refs, *sems)
        pl.when(step == 0)(start)
        pl.when(step == total - 1)(finish)


def _pallas(body, *, name, grid, in_specs, out_specs, out_shape, args, semantics, scratch_shapes=(), exchange=None):
    n_in, n_out, n_scr = len(in_specs), len(out_specs), len(scratch_shapes)
    if exchange is None:
        res = pl.pallas_call(
            body, name=name, grid=grid, in_specs=list(in_specs), out_specs=list(out_specs), out_shape=list(out_shape),
            scratch_shapes=list(scratch_shapes), compiler_params=_params(*semantics))(*args)
        return list(res), []
    na = len(exchange.arrays)

    def carrying(*refs):
        ins, sent = refs[:n_in], refs[n_in:n_in + na]
        outs = refs[n_in + na:n_in + na + n_out]
        landed = refs[n_in + na + n_out:n_in + 2 * na + n_out]
        rest = refs[n_in + 2 * na + n_out:]
        exchange.hooks(sent, landed, rest[n_scr:], grid)
        body(*ins, *outs, *rest[:n_scr])

    res = pl.pallas_call(
        carrying, name=name, grid=grid, in_specs=list(in_specs) + [ANY] * na, out_specs=list(out_specs) + [ANY] * na,
        out_shape=list(out_shape) + exchange.out_shape(), scratch_shapes=list(scratch_shapes) + _gather_semaphores(na),
        compiler_params=_params(*(["arbitrary"] * len(grid))))(*args, *exchange.arrays)
    return list(res[:n_out]), list(res[n_out:])


def _matmul(pairs, *, ta=False, tb=False, out_dtype=F32, tm, tn, tk, name, res=None, alpha=None, norm_bwd=None,
            norm_fwd=None, loss=None, exchange=None):
    a0, b0 = pairs[0]
    m = a0.shape[1] if ta else a0.shape[0]
    k = a0.shape[0] if ta else a0.shape[1]
    n = b0.shape[0] if tb else b0.shape[1]
    tm, tn, tk = min(tm, m), min(tn, n), min(tk, k)
    assert m % tm == 0 and n % tn == 0 and k % tk == 0, (name, m, n, k, tm, tn, tk)
    nk = k // tk
    npairs = len(pairs)
    dn = (((0 if ta else 1,), (1 if tb else 0,)), ((), ()))
    assert norm_bwd is None or (tn == n and res is None and alpha is None)

    def body(*refs):
        ins = refs[:2 * npairs]
        pos = 2 * npairs
        r_ref = None
        if res is not None:
            r_ref = refs[pos]
            pos += 1
        if norm_bwd is not None:
            x_ref, rs_ref, w_ref, dres_ref = refs[pos:pos + 4]
            o_ref, dw_ref, acc = refs[pos + 4:pos + 7]

            @pl.when((pl.program_id(0) == 0) & (pl.program_id(2) == 0))
            def _():
                dw_ref[...] = jnp.zeros_like(dw_ref)
        elif norm_fwd is not None:
            wn_ref, o_ref, n_ref, rs_out, acc = refs[pos:pos + 5]
        elif loss is not None:
            wf_ref, tg_ref, loss_ref, o_ref, dwf_ref, acc = refs[pos:pos + 6]

            @pl.when((pl.program_id(0) == 0) & (pl.program_id(2) == 0))
            def _():
                dwf_ref[...] = jnp.zeros_like(dwf_ref)
                loss_ref[...] = jnp.zeros_like(loss_ref)
        else:
            o_ref, acc = refs[pos], refs[pos + 1]
        kk = pl.program_id(2)
        t = None
        for p in range(npairs):
            d = _dot(ins[2 * p][...], ins[2 * p + 1][...], dn)
            t = d if t is None else t + d

        if nk > 1:
            @pl.when(kk == 0)
            def _():
                acc[...] = t

            @pl.when((kk > 0) & (kk < nk - 1))
            def _():
                acc[...] += t

        @pl.when(kk == nk - 1)
        def _():
            r = acc[...] + t if nk > 1 else t
            if alpha is not None:
                r = r * alpha
            if r_ref is not None:
                r = r_ref[...] + r
            if norm_bwd is not None:
                rs = rs_ref[...]
                xhat = x_ref[...] * rs
                dw_ref[...] += jnp.sum(r * xhat, axis=0, keepdims=True)
                t_w = r * w_ref[...]
                r = dres_ref[...] + rs * (t_w - xhat * jnp.mean(t_w * xhat, axis=-1, keepdims=True))
            if norm_fwd is not None:
                rs = lax.rsqrt(jnp.mean(r * r, axis=-1, keepdims=True) + EPS)
                n_ref[...] = (r * rs * wn_ref[...]).astype(BF16)
                rs_out[...] = rs
            if loss is not None:
                wv = wf_ref[...]
                rs = lax.rsqrt(jnp.mean(r * r, axis=-1, keepdims=True) + EPS)
                xhat = r * rs
                e = xhat * wv - tg_ref[...]
                part = 0.5 * jnp.sum(jnp.mean(e * e, axis=-1, keepdims=True), axis=0, keepdims=True)
                loss_ref[...] += jnp.broadcast_to(part, loss_ref.shape)
                dy = e * (1.0 / n)
                dwf_ref[...] += jnp.sum(dy * xhat, axis=0, keepdims=True)
                t_w = dy * wv
                r = rs * (t_w - xhat * jnp.mean(t_w * xhat, axis=-1, keepdims=True))
            o_ref[...] = r.astype(out_dtype)

    a_spec = pl.BlockSpec((tk, tm), lambda i, j, kk: (kk, i)) if ta else pl.BlockSpec((tm, tk), lambda i, j, kk: (i, kk))
    b_spec = pl.BlockSpec((tn, tk), lambda i, j, kk: (j, kk)) if tb else pl.BlockSpec((tk, tn), lambda i, j, kk: (kk, j))
    o_spec = pl.BlockSpec((tm, tn), lambda i, j, kk: (i, j))
    in_specs = [a_spec, b_spec] * npairs + ([o_spec] if res is not None else [])
    args = [t for pr in pairs for t in pr] + ([res] if res is not None else [])
    out_specs, out_shape = [o_spec], [jax.ShapeDtypeStruct((m, n), out_dtype)]
    if norm_bwd is not None:
        vec = pl.BlockSpec((1, n), lambda i, j, kk: (0, 0))
        in_specs += [o_spec, pl.BlockSpec((tm, 1), lambda i, j, kk: (i, 0)), vec, o_spec]
        args += list(norm_bwd)
        out_specs.append(vec)
        out_shape.append(jax.ShapeDtypeStruct((1, n), F32))
    if norm_fwd is not None:
        assert tn == n and norm_bwd is None
        in_specs.append(pl.BlockSpec((1, n), lambda i, j, kk: (0, 0)))
        args.append(norm_fwd)
        out_specs += [o_spec, pl.BlockSpec((tm, 1), lambda i, j, kk: (i, 0))]
        out_shape += [jax.ShapeDtypeStruct((m, n), BF16), jax.ShapeDtypeStruct((m, 1), F32)]
    if loss is not None:
        assert tn == n and norm_bwd is None and norm_fwd is None
        vec = pl.BlockSpec((1, n), lambda i, j, kk: (0, 0))
        in_specs += [vec, o_spec]
        args += list(loss)
        out_specs = [pl.BlockSpec((1, LANE), lambda i, j, kk: (0, 0))] + out_specs + [vec]
        out_shape = [jax.ShapeDtypeStruct((1, LANE), F32)] + out_shape + [jax.ShapeDtypeStruct((1, n), F32)]
    sequential = norm_bwd is not None or loss is not None
    outs, exchanged = _pallas(
        body, name=name, grid=(m // tm, n // tn, nk), in_specs=in_specs, out_specs=out_specs, out_shape=out_shape,
        scratch_shapes=[pltpu.VMEM((tm, tn) if nk > 1 else (8, LANE), F32)],
        semantics=("arbitrary",) * 3 if sequential else ("parallel", "parallel", "arbitrary"), args=args,
        exchange=exchange)
    out = outs[0] if len(outs) == 1 else tuple(outs)
    return out if exchange is None else (out, exchanged)


def _rms_fwd(x, w, name, exchange=None):
    s, d = x.shape
    tm = min(512, s)

    def body(x_ref, w_ref, n_ref, r_ref):
        xv = x_ref[...]
        r = lax.rsqrt(jnp.mean(xv * xv, axis=-1, keepdims=True) + EPS)
        n_ref[...] = (xv * r * w_ref[...]).astype(BF16)
        r_ref[...] = r

    (n, r), exchanged = _pallas(
        body, name=name, grid=(s // tm,),
        in_specs=[pl.BlockSpec((tm, d), lambda i: (i, 0)), pl.BlockSpec((1, d), lambda i: (0, 0))],
        out_specs=[pl.BlockSpec((tm, d), lambda i: (i, 0)), pl.BlockSpec((tm, 1), lambda i: (i, 0))],
        out_shape=[jax.ShapeDtypeStruct((s, d), BF16), jax.ShapeDtypeStruct((s, 1), F32)],
        semantics=("parallel",), args=(x, w), exchange=exchange)
    return (n, r) if exchange is None else (n, r, exchanged)


def _rms_bwd(dn, x, r, w, dres, name, exchange=None):
    s, d = x.shape
    tm = min(512, s)

    def body(dn_ref, x_ref, r_ref, w_ref, dres_ref, dx_ref, dw_ref):
        @pl.when(pl.program_id(0) == 0)
        def _():
            dw_ref[...] = jnp.zeros_like(dw_ref)

        rv = r_ref[...]
        xhat = x_ref[...] * rv
        g = dn_ref[...]
        t = g * w_ref[...]
        dx_ref[...] = dres_ref[...] + rv * (t - xhat * jnp.mean(t * xhat, axis=-1, keepdims=True))
        dw_ref[...] += jnp.sum(g * xhat, axis=0, keepdims=True)

    row = pl.BlockSpec((tm, d), lambda i: (i, 0))
    vec = pl.BlockSpec((1, d), lambda i: (0, 0))
    (dx, dw), exchanged = _pallas(
        body, name=name, grid=(s // tm,),
        in_specs=[row, row, pl.BlockSpec((tm, 1), lambda i: (i, 0)), vec, row],
        out_specs=[row, vec],
        out_shape=[jax.ShapeDtypeStruct((s, d), F32), jax.ShapeDtypeStruct((1, d), F32)],
        semantics=("arbitrary",), args=(dn, x, r, w, dres), exchange=exchange)
    return (dx, dw) if exchange is None else (dx, dw, exchanged)


def _ffn_up(n, wg, wu, name, exchange=None):
    s, d = n.shape
    f = wg.shape[0]
    tm, tn = min(512, s), f // 2

    def body(n_ref, wg_ref, wu_ref, g_ref, u_ref, a_ref):
        nv = n_ref[...]
        g = _dot(nv, wg_ref[...], NT)
        u = _dot(nv, wu_ref[...], NT)
        g_ref[...] = g.astype(BF16)
        u_ref[...] = u.astype(BF16)
        a_ref[...] = (g * _sigmoid(g) * u).astype(BF16)

    o = pl.BlockSpec((tm, tn), lambda j, i: (i, j))
    wspec = pl.BlockSpec((tn, d), lambda j, i: (j, 0))
    return _pallas(
        body, name=name, grid=(f // tn, s // tm),
        in_specs=[pl.BlockSpec((tm, d), lambda j, i: (i, 0)), wspec, wspec],
        out_specs=[o, o, o],
        out_shape=[jax.ShapeDtypeStruct((s, f), BF16)] * 3,
        semantics=("parallel", "parallel"), args=(n, wg, wu), exchange=exchange)


def _ffn_dact(dx, wd, g, u, name, exchange=None):
    s, d = dx.shape
    f = wd.shape[0]
    tm, tn = min(512, s), f // 2

    def body(dx_ref, wd_ref, g_ref, u_ref, dg_ref, du_ref):
        da = 0.5 * _dot(dx_ref[...], wd_ref[...], NT)
        gv = g_ref[...].astype(F32)
        sg = _sigmoid(gv)
        du_ref[...] = (da * gv * sg).astype(BF16)
        dg_ref[...] = (da * u_ref[...].astype(F32) * (sg * (1.0 + gv * (1.0 - sg)))).astype(BF16)

    o = pl.BlockSpec((tm, tn), lambda j, i: (i, j))
    return _pallas(
        body, name=name, grid=(f // tn, s // tm),
        in_specs=[pl.BlockSpec((tm, d), lambda j, i: (i, 0)), pl.BlockSpec((tn, d), lambda j, i: (j, 0)), o, o],
        out_specs=[o, o],
        out_shape=[jax.ShapeDtypeStruct((s, f), BF16), jax.ShapeDtypeStruct((s, f), BF16)],
        semantics=("parallel", "parallel"), args=(dx, wd, g, u), exchange=exchange)


def _row_slabs(full):
    return full.reshape(N_DEV, full.shape[0] // N_DEV, full.shape[1])


def _rows_of(slabs):
    return slabs.reshape(N_DEV * slabs.shape[1], slabs.shape[2])


def _ffn_forward(x, norm_w, wg, wu, wd, tag, gather=(), head=(), normed=None, next_norm=None, loss=None):
    if normed is not None:
        (n, r), first = normed, []
    elif head:
        n, r, first = _rms_fwd(x, norm_w, f"{tag}_norm", _Exchange("gather", head))
    else:
        (n, r), first = _rms_fwd(x, norm_w, f"{tag}_norm"), []
    if wg is None:
        wg, wu, first = _rows_of(first[0]), _rows_of(first[1]), first[2:]
    (g, u, a), got = _ffn_up(n, wg, wu, f"{tag}_up", _Exchange("gather", gather) if gather else None)
    if wd is None:
        wd, got = _rows_of(got[0]), got[1:]
    y = _matmul([(a, wd)], tm=512, tn=1024, tk=wd.shape[0], name=f"{tag}_down", res=x, alpha=0.5, norm_fwd=next_norm,
                loss=loss)
    y, nxt = (y[0], y[1:]) if next_norm is not None else (y, None)
    return y, (n, r, g, u, a), (wg, wu, wd), got, first, nxt


def _ffn_backward(dy, x, norm_w, wgt, wut, wd, saved, tag, dw_dtype=F32, scatter=None):
    n, r, g, u, a = saved

    def behind(arrays):
        return _Exchange("scatter", arrays) if scatter is not None else None

    def dw(act, grad, name, alpha=None, exchange=None):
        return _matmul([(act, grad)], ta=True, tm=1408, tn=1024, tk=2048, name=name, alpha=alpha, out_dtype=dw_dtype,
                       exchange=exchange)

    dwd = _row_slabs(dw(a, dy, f"{tag}_dwd", alpha=0.5))
    (dg, du), extras = _ffn_dact(dy, wd, g, u, f"{tag}_dact", behind(scatter))
    if scatter is None:
        dwg, dwu = _row_slabs(dw(dg, n, f"{tag}_dwg")), _row_slabs(dw(du, n, f"{tag}_dwu"))
    else:
        dwg, (dwd,) = dw(dg, n, f"{tag}_dwg", exchange=behind([dwd]))
        dwu, (dwg,) = dw(du, n, f"{tag}_dwu", exchange=behind([_row_slabs(dwg)]))
        dwu = _row_slabs(dwu)
    if scatter is None:
        dx, dnorm = _matmul([(dg, wgt), (du, wut)], tm=512, tn=1024, tk=wgt.shape[0], name=f"{tag}_dn",
                            norm_bwd=(x, r, norm_w, dy))
    else:
        dn, (dwu,) = _matmul([(dg, wgt), (du, wut)], tm=512, tn=1024, tk=wgt.shape[0], name=f"{tag}_dn",
                             exchange=behind([dwu]))
        dx, dnorm = _rms_bwd(dn, x, r, norm_w, dy, f"{tag}_dnorm")
    return dx, dnorm, dwg, dwu, dwd, extras


Q_SCALE = GDN_HEAD_DIM ** -0.5
CONV_HALO = 8


def _lane_block(s):
    return pl.BlockSpec((None, s, LANE), lambda j: (j, 0, 0))


def _conv_taps(win, w_ref, rows, sign):
    n = rows + 2 * CONV_HALO
    acc = None
    for t in range(CONV_WIDTH):
        o = sign * (t - CONV_WIDTH // 2)
        sh = win if o == 0 else pltpu.roll(win, (-o) % n, 0)
        term = sh[CONV_HALO:CONV_HALO + rows] * w_ref[t:t + 1, :]
        acc = term if acc is None else acc + term
    return acc


def _gdn_conv_fwd(p_pad, conv_wt):
    s = p_pad.shape[0]
    rows = min(CONV_ROWS, s)
    nblk = QKV_A // LANE

    def body(p_ref, w_ref, c_ref, y_ref, pad):
        j = pl.program_id(0)
        zeros = jnp.zeros((CONV_HALO, LANE), F32)
        pad[0:CONV_HALO, :] = zeros
        pad[CONV_HALO + s:2 * CONV_HALO + s, :] = zeros
        pad[CONV_HALO:CONV_HALO + s, :] = p_ref[...]

        def chunk(ci, carry):
            b = pl.multiple_of(ci * rows, rows)
            win = pad[pl.ds(b, rows + 2 * CONV_HALO), :]
            c = _conv_taps(win, w_ref, rows, 1)
            c_ref[pl.ds(b, rows), :] = c
            act = c * _sigmoid(c)
            nrm = lax.rsqrt(jnp.sum(act * act, axis=-1, keepdims=True) + EPS)
            mult = jnp.where(j < GDN_HEADS, nrm * Q_SCALE, jnp.where(j < 2 * GDN_HEADS, nrm, 1.0))
            y_ref[pl.ds(b, rows), :] = act * mult
            return carry

        lax.fori_loop(0, s // rows, chunk, 0)

    col = pl.BlockSpec((s, LANE), lambda j: (0, j))
    return pl.pallas_call(
        body, name="gdn_conv_fwd", grid=(nblk,),
        in_specs=[col, pl.BlockSpec((8, LANE), lambda j: (0, j))],
        out_specs=[_lane_block(s), _lane_block(s)],
        out_shape=[jax.ShapeDtypeStruct((nblk, s, LANE), F32), jax.ShapeDtypeStruct((nblk, s, LANE), F32)],
        scratch_shapes=[pltpu.VMEM((s + 2 * CONV_HALO, LANE), F32)],
        compiler_params=_params("parallel"),
    )(p_pad, conv_wt)


def _gdn_conv_bwd(dy_f, dy_r, c_pre, p_pad, conv_wt, dp_all):
    s = p_pad.shape[0]
    rows = min(CONV_ROWS, s)
    nblk = QKV_A // LANE

    def body(dyf_ref, dyr_ref, c_ref, p_ref, w_ref, _, dp_ref, dw_ref, ppad, dcpad):
        j = pl.program_id(0)
        zeros = jnp.zeros((CONV_HALO, LANE), F32)
        for buf in (ppad, dcpad):
            buf[0:CONV_HALO, :] = zeros
            buf[CONV_HALO + s:2 * CONV_HALO + s, :] = zeros
        ppad[CONV_HALO:CONV_HALO + s, :] = p_ref[...]

        def act_bwd(ci, carry):
            b = pl.multiple_of(ci * rows, rows)
            c = c_ref[pl.ds(b, rows), :]
            g = dyf_ref[pl.ds(b, rows), :] + dyr_ref[pl.ds(b, rows), :]
            sg = _sigmoid(c)
            act = c * sg
            nrm = lax.rsqrt(jnp.sum(act * act, axis=-1, keepdims=True) + EPS)
            yh = act * nrm
            scale = jnp.where(j < GDN_HEADS, Q_SCALE, 1.0)
            dact_qk = (scale * nrm) * (g - yh * jnp.sum(g * yh, axis=-1, keepdims=True))
            dact = jnp.where(j < 2 * GDN_HEADS, dact_qk, g)
            dcpad[pl.ds(pl.multiple_of(b + CONV_HALO, CONV_HALO), rows), :] = dact * (sg * (1.0 + c * (1.0 - sg)))
            return carry

        lax.fori_loop(0, s // rows, act_bwd, 0)
        tap = lax.broadcasted_iota(jnp.int32, (8, LANE), 0)

        def taps_bwd(ci, dw):
            b = pl.multiple_of(ci * rows, rows)
            dcw = dcpad[pl.ds(b, rows + 2 * CONV_HALO), :]
            dp_ref[pl.ds(b, rows), :] = _conv_taps(dcw, w_ref, rows, -1).astype(BF16)
            pw = ppad[pl.ds(b, rows + 2 * CONV_HALO), :]
            dc = dcw[CONV_HALO:CONV_HALO + rows]
            n = rows + 2 * CONV_HALO
            for t in range(CONV_WIDTH):
                o = t - CONV_WIDTH // 2
                sh = pw if o == 0 else pltpu.roll(pw, (-o) % n, 0)
                row = jnp.sum(dc * sh[CONV_HALO:CONV_HALO + rows], axis=0, keepdims=True)
                dw = dw + jnp.where(tap == t, row, 0.0)
            return dw

        dw_ref[...] = lax.fori_loop(0, s // rows, taps_bwd, jnp.zeros((8, LANE), F32))

    col = pl.BlockSpec((s, LANE), lambda j: (0, j))
    wspec = pl.BlockSpec((8, LANE), lambda j: (0, j))
    return pl.pallas_call(
        body, name="gdn_conv_bwd", grid=(nblk,),
        in_specs=[_lane_block(s), _lane_block(s), _lane_block(s), col, wspec, ANY],
        out_specs=[col, wspec],
        out_shape=[jax.ShapeDtypeStruct(dp_all.shape, dp_all.dtype), jax.ShapeDtypeStruct((8, QKV_A), F32)],
        scratch_shapes=[pltpu.VMEM((s + 2 * CONV_HALO, LANE), F32), pltpu.VMEM((s + 2 * CONV_HALO, LANE), F32)],
        input_output_aliases={5: 0},
        compiler_params=_params("parallel"),
    )(dy_f, dy_r, c_pre, p_pad, conv_wt, dp_all)


def _softplus(x):
    return jnp.maximum(x, 0.0) + jnp.log(1.0 + jnp.exp(-jnp.abs(x)))


def _gdn_gates_fwd(p_pad, alog_row, dt_row):
    s = p_pad.shape[0]
    tm = min(1024, s)

    def body(p_ref, al_ref, dt_ref, o_ref):
        x = p_ref[...]
        lane = lax.broadcasted_iota(jnp.int32, x.shape, 1)
        g = -jnp.exp(al_ref[...]) * _softplus(x + dt_ref[...])
        o_ref[...] = jnp.where(lane < 8, g, jnp.where(lane < 16, _sigmoid(x), 0.0))

    vec = pl.BlockSpec((1, LANE), lambda i: (0, 0))
    return pl.pallas_call(
        body, name="gdn_gates_fwd", grid=(s // tm,),
        in_specs=[pl.BlockSpec((tm, LANE), lambda i: (i, OFF_AB // LANE)), vec, vec],
        out_specs=pl.BlockSpec((tm, LANE), lambda i: (i, 0)),
        out_shape=jax.ShapeDtypeStruct((s, LANE), F32),
        compiler_params=_params("parallel"),
    )(p_pad, alog_row, dt_row)


def _gdn_gates_bwd(dgb_f, dgb_r, p_pad, gb, alog_row, dt_row, dp_all):
    s = p_pad.shape[0]
    tm = min(1024, s)
    tail = N_PAD - OFF_AB

    def body(df_ref, dr_ref, p_ref, gb_ref, al_ref, dt_ref, _, dp_ref, sum_ref):
        @pl.when(pl.program_id(0) == 0)
        def _():
            sum_ref[...] = jnp.zeros_like(sum_ref)

        x = p_ref[...]
        gbv = gb_ref[...]
        dgb = df_ref[...] + dr_ref[...]
        lane = lax.broadcasted_iota(jnp.int32, x.shape, 1)
        da = dgb * (-jnp.exp(al_ref[...])) * _sigmoid(x + dt_ref[...])
        db = dgb * gbv * (1.0 - gbv)
        dp_ref[:, 0:LANE] = jnp.where(lane < 8, da, jnp.where(lane < 16, db, 0.0)).astype(BF16)
        dp_ref[:, LANE:tail] = jnp.zeros((tm, tail - LANE), BF16)
        row = lax.broadcasted_iota(jnp.int32, (8, LANE), 0)
        lane8 = lax.broadcasted_iota(jnp.int32, (8, LANE), 1)
        d_alog = jnp.sum(dgb * gbv, axis=0, keepdims=True)
        d_dt = jnp.sum(da, axis=0, keepdims=True)
        upd = jnp.where(row == 0, d_alog, jnp.where(row == 1, d_dt, 0.0))
        sum_ref[...] += jnp.where(lane8 < 8, upd, 0.0)

    vec = pl.BlockSpec((1, LANE), lambda i: (0, 0))
    blk = pl.BlockSpec((tm, LANE), lambda i: (i, 0))
    return pl.pallas_call(
        body, name="gdn_gates_bwd", grid=(s // tm,),
        in_specs=[blk, blk, pl.BlockSpec((tm, LANE), lambda i: (i, OFF_AB // LANE)), blk, vec, vec, ANY],
        out_specs=[pl.BlockSpec((tm, tail), lambda i: (i, OFF_AB // tail)), pl.BlockSpec((8, LANE), lambda i: (0, 0))],
        out_shape=[jax.ShapeDtypeStruct(dp_all.shape, dp_all.dtype), jax.ShapeDtypeStruct((8, LANE), F32)],
        input_output_aliases={6: 0},
        compiler_params=_params("arbitrary"),
    )(dgb_f, dgb_r, p_pad, gb, alog_row, dt_row, dp_all)


def _chunk_masks(rev):
    row = lax.broadcasted_iota(jnp.int32, (CHUNK, CHUNK), 0)
    col = lax.broadcasted_iota(jnp.int32, (CHUNK, CHUNK), 1)
    le = (col >= row) if rev else (col <= row)
    strict = (col > row) if rev else (col < row)
    return le, strict, row == col


def _gate_lanes(rev, h):
    d = 1 if rev else 0
    return d * GDN_HEADS + h, 8 + d * GDN_HEADS + h


BNN = (((2,), (1,)), ((0,), (0,)))
BNT = (((2,), (2,)), ((0,), (0,)))
BTN = (((1,), (1,)), ((0,), (0,)))
NB = 2 * GDN_HEADS
DELTA_CHUNKS = 8


def _bdot(a, b, dn=BNN):
    return lax.dot_general(a.astype(BF16), b.astype(BF16), dn, preferred_element_type=F32)


def _dot3(a, b, dn, exact_a=False, exact_b=False):
    def d(x, y):
        return lax.dot_general(x, y, dn, preferred_element_type=F32)

    ah = a.astype(BF16)
    bh = b.astype(BF16)
    out = d(ah, bh)
    if not exact_b:
        out = out + d(ah, (b - bh.astype(F32)).astype(BF16))
    if not exact_a:
        out = out + d((a - ah.astype(F32)).astype(BF16), bh)
    return out


def _both(f_val, r_val):
    return jnp.stack([f_val] * GDN_HEADS + [r_val] * GDN_HEADS)


def _head_blocks(ref_f, ref_r, rows_f, rows_r):
    return jnp.concatenate([ref_f[:, rows_f, :], ref_r[:, rows_r, :]], axis=0)


def _chunk_rows(c):
    return slice(c * CHUNK, (c + 1) * CHUNK), slice((DELTA_CHUNKS - 1 - c) * CHUNK, (DELTA_CHUNKS - c) * CHUNK)


def _heads(ref_f, ref_r, rows_f, rows_r):
    hd = GDN_HEAD_DIM
    return jnp.stack([ref_f[rows_f, h * hd:(h + 1) * hd] for h in range(GDN_HEADS)]
                     + [ref_r[rows_r, h * hd:(h + 1) * hd] for h in range(GDN_HEADS)])


def _gate_cols(tile_f, tile_r, base):
    return jnp.stack([tile_f[:, base + h:base + h + 1] for h in range(GDN_HEADS)]
                     + [tile_r[:, base + GDN_HEADS + h:base + GDN_HEADS + h + 1] for h in range(GDN_HEADS)])


def _chunk_common2(q, k, v, gbf, gbr):
    mf, mr = _chunk_masks(False), _chunk_masks(True)
    le, strict = _both(mf[0], mr[0]), _both(mf[1], mr[1])
    eye = mf[2]
    gcm_f = _dot3(mf[0].astype(F32), gbf, NN, exact_a=True)
    gcm_r = _dot3(mr[0].astype(F32), gbr, NN, exact_a=True)
    g, beta, gc = _gate_cols(gbf, gbr, 0), _gate_cols(gbf, gbr, 8), _gate_cols(gcm_f, gcm_r, 0)
    gc_row = _dot3(jnp.ones((NB, CHUNK, CHUNK), F32), jnp.where(eye[None], gc, 0.0), BNN, exact_a=True)
    decay = jnp.where(le, jnp.exp(jnp.where(le, gc - gc_row, 0.0)), 0.0)
    eg = jnp.exp(gc)
    gl = jnp.sum(g, axis=1, keepdims=True)
    kb = k * beta
    vb = v * beta
    kbeg = kb * eg
    lm = jnp.where(strict, _bdot(kb, k, BNT) * decay, 0.0)
    intra = _bdot(q, k, BNT) * decay
    edec = jnp.exp(gl - gc)
    return dict(strict=strict, eye=eye, beta=beta, decay=decay, eg=eg, gl=gl, kb=kb, vb=vb, kbeg=kbeg,
                lm=lm, intra=intra, qg=q * eg, edec=edec, kdec=k * edec)


def _unit_triangular_inverse(lm, eye):
    x = -lm
    t = eye[None].astype(F32) + x
    p = x
    for level in range(5):
        prod = functools.partial(_dot3, dn=BNN) if level < 2 else _bdot
        p = prod(p, p)
        t = t + prod(t, p)
    return t


def _delta_fwd2(y, gb, gather=()):
    s = y.shape[1]
    nc = s // CHUNK
    hd = GDN_HEAD_DIM
    na = len(gather)

    def body(*refs):
        qf, kf, vf, gf, qr, kr, vr, gr = refs[:8]
        of_ref, or_ref, sf_all, sr_all, tf_all, tr_all = refs[8 + na:14 + na]
        state = refs[14 + 2 * na]
        step = pl.program_id(0)

        @pl.when(step == 0)
        def _():
            state[...] = jnp.zeros_like(state)

        if na:
            start, forward, finish = _gather_phases(refs[8:8 + na], refs[14 + na:14 + 2 * na], *refs[15 + 2 * na:])
            pl.when(step == 0)(start)
            pl.when(step == ns // 2)(forward)
            pl.when(step == ns - 1)(finish)

        st = state[...]
        for c in range(DELTA_CHUNKS):
            rf, rr = _chunk_rows(c)
            q, k, v = _head_blocks(qf, qr, rf, rr), _head_blocks(kf, kr, rf, rr), _head_blocks(vf, vr, rf, rr)
            cm = _chunk_common2(q, k, v, gf[rf, :], gr[rr, :])
            tinv = _unit_triangular_inverse(cm["lm"], cm["eye"])
            u = _bdot(tinv, cm["vb"])
            w = _bdot(tinv, cm["kbeg"])
            v_new = u - _bdot(w, st)
            o = _bdot(cm["qg"], st) + _bdot(cm["intra"], v_new)
            for h in range(GDN_HEADS):
                of_ref[rf, h * hd:(h + 1) * hd] = o[h]
                or_ref[rr, h * hd:(h + 1) * hd] = o[GDN_HEADS + h]
            sf_all[c] = st[:GDN_HEADS]
            sr_all[DELTA_CHUNKS - 1 - c] = st[GDN_HEADS:]
            tf_all[c] = tinv[:GDN_HEADS]
            tr_all[DELTA_CHUNKS - 1 - c] = tinv[GDN_HEADS:]
            st = st * jnp.exp(cm["gl"]) + _bdot(cm["kdec"], v_new, BTN)
        state[...] = st

    rows = DELTA_CHUNKS * CHUNK
    ns = nc // DELTA_CHUNKS

    def col(j, rev):
        return pl.BlockSpec((GDN_HEADS, rows, hd), (lambda n: (j, ns - 1 - n, 0)) if rev else (lambda n: (j, n, 0)))

    def out(rev):
        return pl.BlockSpec((rows, GDN_WIDTH), (lambda n: (ns - 1 - n, 0)) if rev else (lambda n: (n, 0)))

    def gate(rev):
        return pl.BlockSpec((rows, LANE), (lambda n: (ns - 1 - n, 0)) if rev else (lambda n: (n, 0)))

    def per_chunk(d1, d2, rev):
        return pl.BlockSpec((DELTA_CHUNKS, GDN_HEADS, d1, d2),
                            (lambda n: (ns - 1 - n, 0, 0, 0)) if rev else (lambda n: (n, 0, 0, 0)))

    assert nc % DELTA_CHUNKS == 0 and (na == 0 or ns >= 4)
    res = pl.pallas_call(
        body, name="delta_fwd", grid=(ns,),
        in_specs=[col(0, False), col(1, False), col(2, False), gate(False), col(0, True), col(1, True), col(2, True), gate(True)]
        + [ANY] * na,
        out_specs=[out(False), out(True), per_chunk(hd, hd, False), per_chunk(hd, hd, True),
                   per_chunk(CHUNK, CHUNK, False), per_chunk(CHUNK, CHUNK, True)] + [ANY] * na,
        out_shape=[jax.ShapeDtypeStruct((s, GDN_WIDTH), F32)] * 2 + [jax.ShapeDtypeStruct((nc, GDN_HEADS, hd, hd), F32)] * 2
        + [jax.ShapeDtypeStruct((nc, GDN_HEADS, CHUNK, CHUNK), F32)] * 2
        + [jax.ShapeDtypeStruct((N_DEV,) + v.shape, v.dtype) for v in gather],
        scratch_shapes=[pltpu.VMEM((NB, hd, hd), F32)] + (_gather_semaphores(na) if na else []),
        compiler_params=_params("arbitrary"),
    )(y, y, y, gb, y, y, y, gb, *gather)
    return res[:6], res[6:]


def _delta_bwd2(y, gb, do, sf_all, sr_all, tf_all, tr_all, scatter=()):
    s = y.shape[1]
    nc = s // CHUNK
    hd = GDN_HEAD_DIM
    na = len(scatter)

    def body(*refs):
        qf, kf, vf, gf, dof, sf, tf, qr, kr, vr, gr, dor, sr, tr = refs[:14]
        dyf_ref, dyr_ref, dgf_ref, dgr_ref = refs[14 + na:18 + na]
        dstate = refs[18 + 2 * na]
        step = pl.program_id(0)

        @pl.when(step == 0)
        def _():
            dstate[...] = jnp.zeros_like(dstate)

        if na:
            start, finish = _scatter_phases(refs[14:14 + na], refs[18 + na:18 + 2 * na], *refs[19 + 2 * na:])
            pl.when(step == 0)(start)
            pl.when(step == ns - 1)(finish)

        def one_chunk(c, ds_out):
            rr, rf = _chunk_rows(c)
            cf, cr = DELTA_CHUNKS - 1 - c, c
            q, k, v = _head_blocks(qf, qr, rf, rr), _head_blocks(kf, kr, rf, rr), _head_blocks(vf, vr, rf, rr)
            dov = _heads(dof, dor, rf, rr)
            cm = _chunk_common2(q, k, v, gf[rf, :], gr[rr, :])
            tinv = jnp.concatenate([tf[cf], tr[cr]], axis=0)
            st = jnp.concatenate([sf[cf], sr[cr]], axis=0)
            decay, lm, intra, qg, kdec, kbeg, eg, kb, beta = (
                cm[n] for n in ("decay", "lm", "intra", "qg", "kdec", "kbeg", "eg", "kb", "beta"))
            u = _bdot(tinv, cm["vb"])
            w = _bdot(tinv, kbeg)
            v_new = u - _bdot(w, st)
            egl = jnp.exp(cm["gl"])
            d_qg = _bdot(dov, st, BNT)
            d_intra = _bdot(dov, v_new, BNT)
            dv_new = _bdot(intra, dov, BTN) + _bdot(kdec, ds_out)
            d_kdec = _bdot(v_new, ds_out, BNT)
            ds_in = _bdot(qg, dov, BTN) + egl * ds_out - _bdot(w, dv_new, BTN)
            dgl = egl * jnp.sum(jnp.sum(st * ds_out, axis=2, keepdims=True), axis=1, keepdims=True)
            dw = -_bdot(dv_new, st, BNT)
            dvb = _bdot(tinv, dv_new, BTN)
            dkbeg = _bdot(tinv, dw, BTN)
            dlm = jnp.where(cm["strict"], -(_bdot(dvb, u, BNT) + _bdot(dkbeg, w, BNT)), 0.0)
            d_a = dlm * decay
            d_qk = d_intra * decay
            e = dlm * lm + d_intra * intra
            colsum = _dot3(e, jnp.ones((NB, CHUNK, LANE), F32), BTN, exact_b=True)[:, :, 0:1]
            dgc = jnp.sum(e, axis=2, keepdims=True) - colsum
            dkb = _bdot(d_a, k) + dkbeg * eg
            dk = _bdot(d_a, kb, BTN) + _bdot(d_qk, q, BTN)
            dq = _bdot(d_qk, k) + d_qg * eg
            dgc = dgc + jnp.sum(d_qg * qg, axis=2, keepdims=True) + jnp.sum(dkbeg * kbeg, axis=2, keepdims=True)
            tdec = jnp.sum(d_kdec * kdec, axis=2, keepdims=True)
            dk = dk + d_kdec * cm["edec"] + dkb * beta
            dgc = dgc - tdec
            dgl = dgl + jnp.sum(tdec, axis=1, keepdims=True)
            dbeta = jnp.sum(dvb * v, axis=2, keepdims=True) + jnp.sum(dkb * k, axis=2, keepdims=True)
            dv = dvb * beta
            lane = lax.broadcasted_iota(jnp.int32, (CHUNK, LANE), 1)
            for rev, dy_ref, dg_ref, rows in ((False, dyf_ref, dgf_ref, rf), (True, dyr_ref, dgr_ref, rr)):
                dgc_tile = jnp.zeros((CHUNK, LANE), F32)
                rest = jnp.zeros((CHUNK, LANE), F32)
                for h in range(GDN_HEADS):
                    b = (GDN_HEADS if rev else 0) + h
                    gi, bi = _gate_lanes(rev, h)
                    dgc_tile = dgc_tile + jnp.where(lane == gi, dgc[b], 0.0)
                    rest = rest + jnp.where(lane == gi, dgl[b], 0.0) + jnp.where(lane == bi, dbeta[b], 0.0)
                    dy_ref[h, rows, :] = dq[b]
                    dy_ref[GDN_HEADS + h, rows, :] = dk[b]
                    dy_ref[2 * GDN_HEADS + h, rows, :] = dv[b]
                le_t = _chunk_masks(not rev)[0].astype(F32)
                dg_ref[rows, :] = _dot3(le_t, dgc_tile, NN, exact_a=True) + rest
            return ds_in

        ds = dstate[...]
        for c in range(DELTA_CHUNKS):
            ds = one_chunk(c, ds)
        dstate[...] = ds

    rows_per_step = DELTA_CHUNKS * CHUNK
    ns = nc // DELTA_CHUNKS

    def col(j, rev, blocks=GDN_HEADS):
        return pl.BlockSpec((blocks, rows_per_step, hd), (lambda n: (j, n, 0)) if rev else (lambda n: (j, ns - 1 - n, 0)))

    def wide(width, rev):
        return pl.BlockSpec((rows_per_step, width), (lambda n: (n, 0)) if rev else (lambda n: (ns - 1 - n, 0)))

    def per_chunk(d1, d2, rev):
        return pl.BlockSpec((DELTA_CHUNKS, GDN_HEADS, d1, d2),
                            (lambda n: (n, 0, 0, 0)) if rev else (lambda n: (ns - 1 - n, 0, 0, 0)))

    def side(rev):
        return [col(0, rev), col(1, rev), col(2, rev), wide(LANE, rev), wide(GDN_WIDTH, rev), per_chunk(hd, hd, rev),
                per_chunk(CHUNK, CHUNK, rev)]

    assert nc % DELTA_CHUNKS == 0 and (na == 0 or ns >= 2)
    res = pl.pallas_call(
        body, name="delta_bwd", grid=(ns,),
        in_specs=side(False) + side(True) + [ANY] * na,
        out_specs=[col(0, False, 3 * GDN_HEADS), col(0, True, 3 * GDN_HEADS), wide(LANE, False), wide(LANE, True)]
        + [ANY] * na,
        out_shape=[jax.ShapeDtypeStruct((3 * GDN_HEADS, s, hd), F32)] * 2 + [jax.ShapeDtypeStruct((s, LANE), F32)] * 2
        + [jax.ShapeDtypeStruct(g.shape, g.dtype) for g in scatter],
        scratch_shapes=[pltpu.VMEM((NB, hd, hd), F32)] + (_gather_semaphores(na) if na else []),
        compiler_params=_params("arbitrary"),
    )(y, y, y, gb, do, sf_all, tf_all, y, y, y, gb, do, sr_all, tr_all, *scatter)
    return res[:4], res[4:]


def _gdn_post_fwd(o_f, o_r, p_pad, norm_row):
    s = o_f.shape[0]
    tm = min(512, s)
    hd = GDN_HEAD_DIM

    def body(of_ref, or_ref, z_ref, w_ref, out_ref, osum_ref):
        o = of_ref[...] + or_ref[...]
        osum_ref[...] = o
        z = z_ref[...]
        gate = z * _sigmoid(z)
        for h in range(GDN_HEADS):
            sl = slice(h * hd, (h + 1) * hd)
            oh = o[:, sl]
            r = lax.rsqrt(jnp.mean(oh * oh, axis=-1, keepdims=True) + EPS)
            out_ref[:, sl] = (oh * r * w_ref[...] * gate[:, sl]).astype(BF16)

    blk = pl.BlockSpec((tm, GDN_WIDTH), lambda i: (i, 0))
    return pl.pallas_call(
        body, name="gdn_post_fwd", grid=(s // tm,),
        in_specs=[blk, blk, pl.BlockSpec((tm, GDN_WIDTH), lambda i: (i, OFF_Z // GDN_WIDTH)),
                  pl.BlockSpec((1, hd), lambda i: (0, 0))],
        out_specs=[blk, blk],
        out_shape=[jax.ShapeDtypeStruct((s, GDN_WIDTH), BF16), jax.ShapeDtypeStruct((s, GDN_WIDTH), F32)],
        compiler_params=_params("parallel"),
    )(o_f, o_r, p_pad, norm_row)


def _gdn_post_bwd(d_out, o_sum, p_pad, norm_row):
    s = o_sum.shape[0]
    tm = min(512, s)
    hd = GDN_HEAD_DIM

    def body(d_ref, o_ref, z_ref, w_ref, do_ref, dz_ref, dw_ref):
        @pl.when(pl.program_id(0) == 0)
        def _():
            dw_ref[...] = jnp.zeros_like(dw_ref)

        z = z_ref[...]
        sg = _sigmoid(z)
        gate = z * sg
        dgate = sg * (1.0 + z * (1.0 - sg))
        wv = w_ref[...]
        dw = jnp.zeros((1, hd), F32)
        for h in range(GDN_HEADS):
            sl = slice(h * hd, (h + 1) * hd)
            oh = o_ref[:, sl]
            dh = d_ref[:, sl]
            r = lax.rsqrt(jnp.mean(oh * oh, axis=-1, keepdims=True) + EPS)
            ohat = oh * r
            dz_ref[:, sl] = (dh * ohat * wv * dgate[:, sl]).astype(BF16)
            drn = dh * gate[:, sl]
            t = drn * wv
            do_ref[:, sl] = r * (t - ohat * jnp.mean(t * ohat, axis=-1, keepdims=True))
            dw = dw + jnp.sum(drn * ohat, axis=0, keepdims=True)
        dw_ref[...] += dw

    blk = pl.BlockSpec((tm, GDN_WIDTH), lambda i: (i, 0))
    vec = pl.BlockSpec((1, hd), lambda i: (0, 0))
    return pl.pallas_call(
        body, name="gdn_post_bwd", grid=(s // tm,),
        in_specs=[blk, blk, pl.BlockSpec((tm, GDN_WIDTH), lambda i: (i, OFF_Z // GDN_WIDTH)), vec],
        out_specs=[blk, pl.BlockSpec((tm, GDN_WIDTH), lambda i: (i, OFF_Z // GDN_WIDTH)), vec],
        out_shape=[jax.ShapeDtypeStruct((s, GDN_WIDTH), F32), jax.ShapeDtypeStruct((s, N_PAD), BF16),
                   jax.ShapeDtypeStruct((1, hd), F32)],
        compiler_params=_params("arbitrary"),
    )(d_out, o_sum, p_pad, norm_row)


def _gdn_forward(p_pad, conv_wt, alog_row, dt_row, norm_row, gather=()):
    c_pre, y = _gdn_conv_fwd(p_pad, conv_wt)
    gb = _gdn_gates_fwd(p_pad, alog_row, dt_row)
    (o_f, o_r, s_f, s_r, t_f, t_r), gathered = _delta_fwd2(y, gb, gather)
    out, o_sum = _gdn_post_fwd(o_f, o_r, p_pad, norm_row)
    return out, (c_pre, y, gb, s_f, t_f, s_r, t_r, o_sum), gathered


def _gdn_backward(d_out, p_pad, conv_wt, alog_row, dt_row, norm_row, saved, scatter=()):
    c_pre, y, gb, s_f, t_f, s_r, t_r, o_sum = saved
    do, dp_all, dnorm = _gdn_post_bwd(d_out, o_sum, p_pad, norm_row)
    (dy_f, dy_r, dgb_f, dgb_r), received = _delta_bwd2(y, gb, do, s_f, s_r, t_f, t_r, scatter)
    dp_all, dconv = _gdn_conv_bwd(dy_f, dy_r, c_pre, p_pad, conv_wt, dp_all)
    dp_all, gate_sums = _gdn_gates_bwd(dgb_f, dgb_r, p_pad, gb, alog_row, dt_row, dp_all)
    return dp_all, dconv, gate_sums, dnorm, received


ATT_BK = ATT_BQ + 2 * ATT_HALO
ATT_SUB = 8
SWA_SCALE = SWA_HEAD_DIM ** -0.5


def _t5_bucket(rel):
    nb = REL_BUCKETS // 2
    bucket = (rel > 0).astype(np.int32) * nb
    n = np.abs(rel)
    max_exact = nb // 2
    large = max_exact + (np.log(np.maximum(n, 1) / max_exact)
                         / math.log(REL_MAX_DISTANCE / max_exact) * (nb - max_exact)).astype(np.int32)
    large = np.minimum(large, nb - 1)
    return (bucket + np.where(n < max_exact, n, large)).astype(np.int32)


def _band_tables(dilation, queries_are_rows_of_block):
    blk = np.arange(ATT_BQ)
    band = np.arange(ATT_BK) - ATT_HALO
    if queries_are_rows_of_block:
        rel = band[None, :] - blk[:, None]
        band_idx = np.broadcast_to(np.arange(ATT_BK)[None, :], rel.shape)
    else:
        rel = blk[None, :] - band[:, None]
        band_idx = np.broadcast_to(np.arange(ATT_BK)[:, None], rel.shape)
    base = np.abs(rel) <= ATT_HALO
    not_prev = band_idx >= ATT_HALO
    not_next = band_idx < ATT_HALO + ATT_BQ
    valid = np.stack([base & not_prev, base, base & not_next, base & not_prev & not_next])
    return valid, _t5_bucket(rel * dilation)


def _bias_tiles(rel_bias, dilation, queries_are_rows_of_block):
    valid, bucket = _band_tables(dilation, queries_are_rows_of_block)
    onehot = (jnp.asarray(bucket.reshape(-1, 1)) == jnp.arange(REL_BUCKETS, dtype=jnp.int32)[None, :]).astype(F32)
    rb = jnp.dot(onehot, rel_bias.astype(F32), precision=lax.Precision.HIGHEST)
    rb = rb.T.reshape((SWA_HEADS,) + bucket.shape)
    return jnp.where(valid[:, None], rb[None], NEG_BIG).astype(F32)


def _group_sum(x, bd):
    hi = x.astype(BF16)
    lo = (x - hi.astype(F32)).astype(BF16)
    return jnp.dot(hi, bd, preferred_element_type=F32) + jnp.dot(lo, bd, preferred_element_type=F32)


def _head_block_diag():
    idx = np.arange(SWA_WIDTH) // SWA_HEAD_DIM
    return jnp.asarray(idx[:, None] == idx[None, :], BF16)


def _swa_pre_fwd(p_pad, qw_row, kw_row, bd):
    s = p_pad.shape[0]
    tm = min(512, s)
    inv = 1.0 / SWA_HEAD_DIM

    def body(q_ref, k_ref, v_ref, qw_ref, kw_ref, bd_ref, qo_ref, ko_ref, vo_ref):
        bdv = bd_ref[...]
        q = q_ref[...]
        k = k_ref[...]
        rq = lax.rsqrt(_group_sum(q * q, bdv) * inv + EPS)
        rk = lax.rsqrt(_group_sum(k * k, bdv) * inv + EPS)
        qo_ref[...] = (q * rq * qw_ref[...] * SWA_SCALE).astype(BF16)
        ko_ref[...] = (k * rk * kw_ref[...]).astype(BF16)
        vo_ref[...] = v_ref[...].astype(BF16)

    base = OFF_B // SWA_WIDTH
    blk = pl.BlockSpec((tm, SWA_WIDTH), lambda i: (i, 0))
    vec = pl.BlockSpec((1, SWA_WIDTH), lambda i: (0, 0))
    return pl.pallas_call(
        body, name="swa_pre_fwd", grid=(s // tm,),
        in_specs=[pl.BlockSpec((tm, SWA_WIDTH), lambda i: (i, base)), pl.BlockSpec((tm, SWA_WIDTH), lambda i: (i, base + 1)),
                  pl.BlockSpec((tm, SWA_WIDTH), lambda i: (i, base + 2)), vec, vec,
                  pl.BlockSpec((SWA_WIDTH, SWA_WIDTH), lambda i: (0, 0))],
        out_specs=[blk, blk, blk],
        out_shape=[jax.ShapeDtypeStruct((s, SWA_WIDTH), BF16)] * 3,
        compiler_params=_params("parallel"),
    )(p_pad, p_pad, p_pad, qw_row, kw_row, bd)


def _swa_pre_bwd(dqs, dks, dvs, p_pad, qw_row, kw_row, bd, dp_all):
    s = p_pad.shape[0]
    tm = min(256, s)
    inv = 1.0 / SWA_HEAD_DIM
    npat = len(dqs)

    def body(*refs):
        dq_refs, dk_refs, dv_refs = refs[:npat], refs[npat:2 * npat], refs[2 * npat:3 * npat]
        q_ref, k_ref, qw_ref, kw_ref, bd_ref, _, dp_ref, dqw_ref, dkw_ref = refs[3 * npat:]

        @pl.when(pl.program_id(0) == 0)
        def _():
            dqw_ref[...] = jnp.zeros_like(dqw_ref)
            dkw_ref[...] = jnp.zeros_like(dkw_ref)

        bdv = bd_ref[...]

        def norm_bwd(x, g, w, scale):
            r = lax.rsqrt(_group_sum(x * x, bdv) * inv + EPS)
            xhat = x * r
            t = g * w * scale
            dx = r * (t - xhat * (_group_sum(t * xhat, bdv) * inv))
            return dx, jnp.sum(g * scale * xhat, axis=0, keepdims=True)

        def total(rs):
            t = rs[0][...].astype(F32)
            for r in rs[1:]:
                t = t + r[...].astype(F32)
            return t

        dq, dqw = norm_bwd(q_ref[...], total(dq_refs), qw_ref[...], SWA_SCALE)
        dk, dkw = norm_bwd(k_ref[...], total(dk_refs), kw_ref[...], 1.0)
        dp_ref[:, 0:SWA_WIDTH] = dq.astype(BF16)
        dp_ref[:, SWA_WIDTH:2 * SWA_WIDTH] = dk.astype(BF16)
        dp_ref[:, 2 * SWA_WIDTH:3 * SWA_WIDTH] = total(dv_refs).astype(BF16)
        dqw_ref[...] += dqw
        dkw_ref[...] += dkw

    base = OFF_B // SWA_WIDTH
    blk = pl.BlockSpec((tm, SWA_WIDTH), lambda i: (i, 0))
    vec = pl.BlockSpec((1, SWA_WIDTH), lambda i: (0, 0))
    return pl.pallas_call(
        body, name="swa_pre_bwd", grid=(s // tm,),
        in_specs=[blk] * (3 * npat) + [pl.BlockSpec((tm, SWA_WIDTH), lambda i: (i, base)),
                                      pl.BlockSpec((tm, SWA_WIDTH), lambda i: (i, base + 1)), vec, vec,
                                      pl.BlockSpec((SWA_WIDTH, SWA_WIDTH), lambda i: (0, 0)), ANY],
        out_specs=[pl.BlockSpec((tm, 3 * SWA_WIDTH), lambda i: (i, OFF_B // (3 * SWA_WIDTH))), vec, vec],
        out_shape=[jax.ShapeDtypeStruct(dp_all.shape, dp_all.dtype), jax.ShapeDtypeStruct((1, SWA_WIDTH), F32),
                   jax.ShapeDtypeStruct((1, SWA_WIDTH), F32)],
        input_output_aliases={3 * npat + 5: 0},
        compiler_params=_params("arbitrary"),
    )(*dqs, *dks, *dvs, p_pad, p_pad, qw_row, kw_row, bd, dp_all)


def _band_specs(length, rows):
    per = rows // ATT_HALO
    last = length // ATT_HALO - 1
    prev = pl.BlockSpec((ATT_HALO, SWA_WIDTH), lambda r, t: (jnp.maximum(t * per - 1, 0), r))
    cur = pl.BlockSpec((rows, SWA_WIDTH), lambda r, t: (t, r))
    nxt = pl.BlockSpec((ATT_HALO, SWA_WIDTH), lambda r, t: (jnp.minimum((t + 1) * per, last), r))
    return [prev, cur, nxt]


def _tile_variant(t, nb, u, sub):
    first, last = u == 0, u == sub - 1
    if first and last:
        return 3 if nb == 1 else jnp.where(t == 0, 0, jnp.where(t == nb - 1, 2, 1))
    if first:
        return jnp.where(t == 0, 0, 1)
    if last:
        return jnp.where(t == nb - 1, 2, 1)
    return 1


def _bias_specs(nb, sub, rows, cols):
    return [pl.BlockSpec((1, SWA_HEADS, rows, cols),
                         functools.partial(lambda r, t, u: (_tile_variant(t, nb, u, sub), 0, 0, 0), u=u))
            for u in range(sub)]


def _band(refs):
    return jnp.concatenate([r[...] for r in refs], axis=0)


def _sub(u, width=ATT_BQ):
    return slice(u * ATT_BQ, u * ATT_BQ + width)


N_PAIRS = SWA_HEADS // 2


def _pairs(x):
    return jnp.stack([x[:, LANE * p:LANE * (p + 1)] for p in range(N_PAIRS)])


def _per_head_rows(x):
    first = lax.broadcasted_iota(jnp.int32, x.shape, 2) < SWA_HEAD_DIM
    zero = jnp.zeros_like(x)
    return jnp.concatenate([jnp.where(first, x, zero), jnp.where(first, zero, x)], axis=1)


def _per_head_cols(x):
    return jnp.stack([jnp.concatenate([x[:, LANE * p:LANE * p + 1],
                                       x[:, LANE * p + SWA_HEAD_DIM:LANE * p + SWA_HEAD_DIM + 1]], axis=0)
                      for p in range(N_PAIRS)])


def _merge_heads(x, rows):
    first = lax.broadcasted_iota(jnp.int32, (N_PAIRS, rows, LANE), 2) < SWA_HEAD_DIM
    return jnp.where(first, x[:, :rows], x[:, rows:])


def _store_pairs(ref, x, rows):
    for p in range(N_PAIRS):
        ref[rows, LANE * p:LANE * (p + 1)] = x[p].astype(ref.dtype)


def _att_fwd2(q, k, v, bias, dilation):
    s = q.shape[0]
    length = s // dilation
    sub = min(ATT_SUB, length // ATT_BQ)
    rows = sub * ATT_BQ
    nb = length // rows
    view = (length, dilation * SWA_WIDTH)

    def body(q_ref, kp, kc, kn, vp, vc, vn, *rest):
        b_refs, (o_ref, lse_ref) = rest[:sub], rest[sub:]
        kwin, vwin = _band((kp, kc, kn)), _band((vp, vc, vn))
        for u in range(sub):
            kb, vb = _pairs(kwin[_sub(u, ATT_BK)]), _pairs(vwin[_sub(u, ATT_BK)])
            qm = _per_head_rows(_pairs(q_ref[_sub(u), :]))
            sc = _bdot(qm, kb, BNT) + b_refs[u][0].reshape(N_PAIRS, 2 * ATT_BQ, ATT_BK)
            m = jnp.max(sc, axis=-1, keepdims=True)
            p = jnp.exp(sc - m)
            den = jnp.sum(p, axis=-1, keepdims=True)
            o = _bdot(p, vb) / den
            _store_pairs(o_ref, _merge_heads(o, ATT_BQ), _sub(u))
            lse = jnp.broadcast_to(m + jnp.log(den), (N_PAIRS, 2 * ATT_BQ, LANE))
            _store_pairs(lse_ref, _merge_heads(lse, ATT_BQ), _sub(u))

    cur = pl.BlockSpec((rows, SWA_WIDTH), lambda r, t: (t, r))
    o, lse = pl.pallas_call(
        body, name=f"att_fwd_d{dilation}", grid=(dilation, nb),
        in_specs=[cur] + _band_specs(length, rows) * 2 + _bias_specs(nb, sub,ATT_BQ, ATT_BK),
        out_specs=[cur, cur],
        out_shape=[jax.ShapeDtypeStruct(view, BF16), jax.ShapeDtypeStruct(view, F32)],
        compiler_params=_params("parallel", "parallel"),
    )(q.reshape(view), *([k.reshape(view)] * 3), *([v.reshape(view)] * 3), *([bias] * sub))
    return o.reshape(s, SWA_WIDTH), lse.reshape(s, SWA_WIDTH)


def _att_dq2(q, k, v, dop, lse, cp, bias, dilation):
    s = q.shape[0]
    length = s // dilation
    sub = min(ATT_SUB, length // ATT_BQ)
    rows = sub * ATT_BQ
    nb = length // rows
    view = (length, dilation * SWA_WIDTH)

    def body(q_ref, kp, kc, kn, vp, vc, vn, do_ref, lse_ref, cp_ref, *rest):
        b_refs, (dq_ref, db_ref) = rest[:sub], rest[sub:]

        @pl.when((pl.program_id(0) == 0) & (pl.program_id(1) == 0))
        def _():
            db_ref[...] = jnp.zeros_like(db_ref)

        kwin, vwin = _band((kp, kc, kn)), _band((vp, vc, vn))
        for u in range(sub):
            kb, vb = _pairs(kwin[_sub(u, ATT_BK)]), _pairs(vwin[_sub(u, ATT_BK)])
            qm = _per_head_rows(_pairs(q_ref[_sub(u), :]))
            dom = _per_head_rows(_pairs(do_ref[_sub(u), :]))
            sc = _bdot(qm, kb, BNT) + b_refs[u][0].reshape(N_PAIRS, 2 * ATT_BQ, ATT_BK)
            p = jnp.exp(sc - _per_head_cols(lse_ref[_sub(u), :]))
            ds = p * (_bdot(dom, vb, BNT) + _per_head_cols(cp_ref[_sub(u), :]))
            _store_pairs(dq_ref, _merge_heads(_bdot(ds, kb), ATT_BQ), _sub(u))
            db_ref[_tile_variant(pl.program_id(1), nb, u, sub)] += ds.reshape(SWA_HEADS, ATT_BQ, ATT_BK)

    cur = pl.BlockSpec((rows, SWA_WIDTH), lambda r, t: (t, r))
    dq, db = pl.pallas_call(
        body, name=f"att_dq_d{dilation}", grid=(dilation, nb),
        in_specs=[cur] + _band_specs(length, rows) * 2 + [cur, cur, cur] + _bias_specs(nb, sub,ATT_BQ, ATT_BK),
        out_specs=[cur, pl.BlockSpec((4, SWA_HEADS, ATT_BQ, ATT_BK), lambda r, t: (0, 0, 0, 0))],
        out_shape=[jax.ShapeDtypeStruct(view, BF16), jax.ShapeDtypeStruct((4, SWA_HEADS, ATT_BQ, ATT_BK), F32)],
        compiler_params=_params("arbitrary", "arbitrary"),
    )(q.reshape(view), *([k.reshape(view)] * 3), *([v.reshape(view)] * 3), dop.reshape(view), lse.reshape(view),
      cp.reshape(view), *([bias] * sub))
    return dq.reshape(s, SWA_WIDTH), db


def _att_dkv2(q, k, v, dop, lse, cp, bias_t, dilation):
    s = q.shape[0]
    length = s // dilation
    sub = min(ATT_SUB, length // ATT_BQ)
    rows = sub * ATT_BQ
    nb = length // rows
    view = (length, dilation * SWA_WIDTH)

    def body(k_ref, v_ref, qp, qc, qn, dp_, dc_, dn_, lp, lc, ln, cp_, cc_, cn_, *rest):
        b_refs, (dk_ref, dv_ref) = rest[:sub], rest[sub:]
        qwin, dowin = _band((qp, qc, qn)), _band((dp_, dc_, dn_))
        lsewin, cpwin = _band((lp, lc, ln)), _band((cp_, cc_, cn_))
        for u in range(sub):
            band = _sub(u, ATT_BK)
            qm = _per_head_rows(_pairs(qwin[band]))
            dom = _per_head_rows(_pairs(dowin[band]))
            kv, vv = _pairs(k_ref[_sub(u), :]), _pairs(v_ref[_sub(u), :])
            sc = _bdot(qm, kv, BNT) + b_refs[u][0].reshape(N_PAIRS, 2 * ATT_BK, ATT_BQ)
            p = jnp.exp(sc - _per_head_cols(lsewin[band]))
            _store_pairs(dv_ref, _bdot(p, dom, BTN), _sub(u))
            ds = p * (_bdot(dom, vv, BNT) + _per_head_cols(cpwin[band]))
            _store_pairs(dk_ref, _bdot(ds, qm, BTN), _sub(u))

    cur = pl.BlockSpec((rows, SWA_WIDTH), lambda r, t: (t, r))
    dk, dv = pl.pallas_call(
        body, name=f"att_dkv_d{dilation}", grid=(dilation, nb),
        in_specs=[cur, cur] + _band_specs(length, rows) * 4 + _bias_specs(nb, sub,ATT_BK, ATT_BQ),
        out_specs=[cur, cur],
        out_shape=[jax.ShapeDtypeStruct(view, BF16)] * 2,
        compiler_params=_params("parallel", "parallel"),
    )(k.reshape(view), v.reshape(view), *([q.reshape(view)] * 3), *([dop.reshape(view)] * 3),
      *([lse.reshape(view)] * 3), *([cp.reshape(view)] * 3), *([bias_t] * sub))
    return dk.reshape(s, SWA_WIDTH), dv.reshape(s, SWA_WIDTH)


def _pattern_weights(lses):
    m = lses[0]
    for l in lses[1:]:
        m = jnp.maximum(m, l)
    es = [jnp.exp(l - m) for l in lses]
    den = es[0]
    for e in es[1:]:
        den = den + e
    return [e / den for e in es]


def _combine_fwd(outs, lses):
    s = outs[0].shape[0]
    tm = min(512, s)
    npat = len(outs)

    def body(*refs):
        ws = _pattern_weights([r[...] for r in refs[npat:2 * npat]])
        o = ws[0] * refs[0][...]
        for p in range(1, npat):
            o = o + ws[p] * refs[p][...]
        refs[2 * npat][...] = o.astype(BF16)

    blk = pl.BlockSpec((tm, SWA_WIDTH), lambda i: (i, 0))
    return pl.pallas_call(
        body, name="swa_combine_fwd", grid=(s // tm,), in_specs=[blk] * (2 * npat), out_specs=blk,
        out_shape=jax.ShapeDtypeStruct((s, SWA_WIDTH), BF16), compiler_params=_params("parallel"),
    )(*outs, *lses)


def _combine_bwd(d_out, outs, lses, bd):
    s = d_out.shape[0]
    tm = min(512, s)
    npat = len(outs)

    def body(*refs):
        d_ref, bd_ref = refs[0], refs[1 + 2 * npat]
        o_refs, l_refs = refs[1:1 + npat], refs[1 + npat:1 + 2 * npat]
        out_refs = refs[2 + 2 * npat:]
        ws = _pattern_weights([r[...] for r in l_refs])
        dov = d_ref[...]
        o = ws[0] * o_refs[0][...]
        for p in range(1, npat):
            o = o + ws[p] * o_refs[p][...]
        rd = _group_sum(dov * o, bd_ref[...])
        for p in range(npat):
            out_refs[p][...] = (ws[p] * dov).astype(BF16)
            out_refs[npat + p][...] = -ws[p] * rd

    blk = pl.BlockSpec((tm, SWA_WIDTH), lambda i: (i, 0))
    res = pl.pallas_call(
        body, name="swa_combine_bwd", grid=(s // tm,),
        in_specs=[blk] * (1 + 2 * npat) + [pl.BlockSpec((SWA_WIDTH, SWA_WIDTH), lambda i: (0, 0))],
        out_specs=[blk] * (2 * npat),
        out_shape=[jax.ShapeDtypeStruct((s, SWA_WIDTH), BF16)] * npat + [jax.ShapeDtypeStruct((s, SWA_WIDTH), F32)] * npat,
        compiler_params=_params("parallel"),
    )(d_out, *outs, *lses, bd)
    return res[:npat], res[npat:]


def _rel_bias_grad(dbs, buckets):
    npat = len(dbs)

    def body(*refs):
        db_refs, bk_refs, o_ref = refs[:npat], refs[npat:2 * npat], refs[2 * npat]
        row = lax.broadcasted_iota(jnp.int32, (REL_BUCKETS, LANE), 0)
        lane = lax.broadcasted_iota(jnp.int32, (REL_BUCKETS, LANE), 1)
        tiles = [[db_refs[p][0, h] + db_refs[p][1, h] + db_refs[p][2, h] + db_refs[p][3, h] for h in range(SWA_HEADS)]
                 for p in range(npat)]
        bks = [r[...] for r in bk_refs]

        def one_bucket(b, acc):
            for h in range(SWA_HEADS):
                tot = jnp.zeros((1, 1), F32)
                for p in range(npat):
                    sel = jnp.where(bks[p] == b, tiles[p][h], 0.0)
                    tot = tot + jnp.sum(jnp.sum(sel, axis=1, keepdims=True), axis=0, keepdims=True)
                acc = acc + jnp.where((row == b) & (lane == h), tot, 0.0)
            return acc

        o_ref[...] = lax.fori_loop(0, REL_BUCKETS, one_bucket, jnp.zeros((REL_BUCKETS, LANE), F32))

    full4 = pl.BlockSpec((4, SWA_HEADS, ATT_BQ, ATT_BK), lambda: (0, 0, 0, 0))
    full2 = pl.BlockSpec((ATT_BQ, ATT_BK), lambda: (0, 0))
    return pl.pallas_call(
        body, name="rel_bias_grad", in_specs=[full4] * npat + [full2] * npat,
        out_specs=pl.BlockSpec((REL_BUCKETS, LANE), lambda: (0, 0)),
        out_shape=jax.ShapeDtypeStruct((REL_BUCKETS, LANE), F32),
        compiler_params=pltpu.CompilerParams(vmem_limit_bytes=V7X_VMEM_LIMIT_BYTES),
    )(*dbs, *buckets)


def _swa_forward(p_pad, qw_row, kw_row, rel_bias, bd):
    q, k, v = _swa_pre_fwd(p_pad, qw_row, kw_row, bd)
    outs, lses = [], []
    for _, dil in DILATION_PATTERNS:
        o, lse = _att_fwd2(q, k, v, _bias_tiles(rel_bias, dil, True), dil)
        outs.append(o)
        lses.append(lse)
    return _combine_fwd(outs, lses), (q, k, v, outs, lses)


def _swa_backward(d_out, p_pad, qw_row, kw_row, rel_bias, bd, saved, dp_all):
    q, k, v, outs, lses = saved
    dops, cps = _combine_bwd(d_out, outs, lses, bd)
    dqs, dks, dvs, dbs, buckets = [], [], [], [], []
    for p, (_, dil) in enumerate(DILATION_PATTERNS):
        dq, db = _att_dq2(q, k, v, dops[p], lses[p], cps[p], _bias_tiles(rel_bias, dil, True), dil)
        dk, dv = _att_dkv2(q, k, v, dops[p], lses[p], cps[p], _bias_tiles(rel_bias, dil, False), dil)
        dqs.append(dq)
        dks.append(dk)
        dvs.append(dv)
        dbs.append(db)
        buckets.append(jnp.asarray(_band_tables(dil, True)[1]))
    dp, dqw, dkw = _swa_pre_bwd(dqs, dks, dvs, p_pad, qw_row, kw_row, bd, dp_all)
    return dp, dqw, dkw, _rel_bias_grad(dbs, buckets)


def _lane_row(v):
    flat = v.reshape(-1).astype(F32)
    return jnp.zeros((1, LANE), F32).at[0, :flat.shape[0]].set(flat)


W_IN_SHARD = N_IN // N_DEV
W_IN_RUNS = ((0, NAT_Z, 0), (NAT_Z, NAT_AB, OFF_Z), (NAT_AB, NAT_B, OFF_AB), (NAT_B, N_IN, OFF_B))


def _w_in_pieces(shard):
    lo, hi = shard * W_IN_SHARD, (shard + 1) * W_IN_SHARD
    out = []
    for first, last, dst in W_IN_RUNS:
        a, b = max(lo, first), min(hi, last)
        if a < b:
            out.append((a - lo, b - a, dst + a - first))
    return out


def _w_in_from_slabs(w3):
    nd, r, _ = w3.shape

    def body(w_ref, o_ref):
        o_ref[:, OFF_AB:N_PAD] = jnp.zeros((r, N_PAD - OFF_AB), w3.dtype)
        for sh in range(nd):
            for src, length, dst in _w_in_pieces(sh):
                o_ref[:, dst:dst + length] = w_ref[sh, :, src:src + length]

    return pl.pallas_call(
        body, name="w_in_from_slabs", out_shape=jax.ShapeDtypeStruct((r, N_PAD), w3.dtype),
        compiler_params=pltpu.CompilerParams(vmem_limit_bytes=V7X_VMEM_LIMIT_BYTES),
    )(w3)


def _w_in_grad_slabs(dw_pad, dtype):
    r = dw_pad.shape[0]

    def body(dw_ref, o_ref):
        for sh in range(N_DEV):
            for src, length, dst in _w_in_pieces(sh):
                o_ref[sh, :, src:src + length] = dw_ref[:, dst:dst + length].astype(dtype)

    return pl.pallas_call(
        body, name="w_in_grad_slabs", out_shape=jax.ShapeDtypeStruct((N_DEV, r, W_IN_SHARD), dtype),
        compiler_params=pltpu.CompilerParams(vmem_limit_bytes=V7X_VMEM_LIMIT_BYTES),
    )(dw_pad)


LATE = ("w_out", "ffn2_w_gate", "ffn2_w_up", "ffn2_w_down")
TRANSPOSED = ("ffn1_w_gate", "ffn1_w_up", "ffn2_w_gate", "ffn2_w_up")


def _late_weights(slabs):
    return {n: g.reshape(N_DEV * g.shape[1], g.shape[2]) for n, g in zip(LATE, slabs)}


def _local_step(x, tgt, wts, small, late_shards=None):
    bd = _head_block_diag()
    alog_row, dt_row = _lane_row(small["a_log"]), _lane_row(small["dt_bias"])
    gnorm_row = small["gdn_norm_w"].reshape(1, GDN_HEAD_DIM)
    qw_row = jnp.tile(small["q_norm_w"].reshape(-1), SWA_HEADS).reshape(1, SWA_WIDTH)
    kw_row = jnp.tile(small["k_norm_w"].reshape(-1), SWA_HEADS).reshape(1, SWA_WIDTH)
    rel_bias = small["rel_bias"]
    exchange = late_shards is not None
    dw_dtype = BF16 if exchange else F32

    x1, sv1, (wg1, wu1, wd1), got, first, (n2, r2) = _ffn_forward(
        x, small["ffn1_norm"], wts.get("ffn1_w_gate"), wts.get("ffn1_w_up"), wts.get("ffn1_w_down"), "ffn1",
        gather=[late_shards["ffn1_w_down"], late_shards["w_in"]] if exchange else (),
        head=[late_shards["ffn1_w_gate"], late_shards["ffn1_w_up"], late_shards["conv_w"]] if exchange else (),
        next_norm=small["mix_norm"])
    win_pad = _w_in_from_slabs(got[0]) if exchange else wts["w_in_pad"]
    conv_w = first[0].reshape(N_DEV, -1)[:, :QKV_A // N_DEV * CONV_WIDTH].reshape(QKV_A, CONV_WIDTH) if exchange \
        else small["conv_w"]
    conv_wt = jnp.zeros((8, QKV_A), F32).at[:CONV_WIDTH].set(conv_w.T)
    p_pad = _matmul([(n2, win_pad)], tm=256, tn=N_PAD, tk=D_MODEL, name="w_in")
    o_a, sva, gathered = _gdn_forward(p_pad, conv_wt, alog_row, dt_row, gnorm_row,
                                      gather=[late_shards[n] for n in LATE] if exchange else ())
    if exchange:
        wts = {**wts, **_late_weights(gathered)}
    wo_a, wo_b = wts["w_out"][:GDN_WIDTH], wts["w_out"][GDN_WIDTH:]
    o_b, svb = _swa_forward(p_pad, qw_row, kw_row, rel_bias, bd)
    x2, n3, r3 = _matmul([(o_a, wo_a), (o_b, wo_b)], tm=512, tn=D_MODEL, tk=GDN_WIDTH, name="w_out", res=x1,
                         norm_fwd=small["ffn2_norm"])
    (loss_row, dx3, d_final), sv2, _, _, _, _ = _ffn_forward(
        x2, small["ffn2_norm"], wts["ffn2_w_gate"], wts["ffn2_w_up"], wts["ffn2_w_down"], "ffn2", normed=(n3, r3),
        loss=(small["final_norm"], tgt))

    dx2, d_ffn2_norm, dwg2, dwu2, dwd2, _ = _ffn_backward(
        dx3, x2, small["ffn2_norm"], wts["ffn2_w_gate"], wts["ffn2_w_up"], wts["ffn2_w_down"], sv2, "ffn2", dw_dtype)
    d_oa = _matmul([(dx2, wo_a)], tb=True, tm=512, tn=GDN_WIDTH, tk=D_MODEL, name="w_out_da")
    d_ob = _matmul([(dx2, wo_b)], tb=True, tm=512, tn=SWA_WIDTH, tk=D_MODEL, name="w_out_db")
    dwo_a = _matmul([(o_a, dx2)], ta=True, tm=GDN_WIDTH, tn=D_MODEL, tk=2048, name="w_out_dwa", out_dtype=dw_dtype)
    dwo_b = _matmul([(o_b, dx2)], ta=True, tm=SWA_WIDTH, tn=D_MODEL, tk=2048, name="w_out_dwb", out_dtype=dw_dtype)

    late_grads = [_row_slabs(jnp.concatenate([dwo_a, dwo_b], axis=0)), dwg2, dwu2, dwd2]
    dp_all, dconv, gate_sums, d_gnorm, received = _gdn_backward(
        d_oa, p_pad, conv_wt, alog_row, dt_row, gnorm_row, sva, scatter=late_grads if exchange else ())
    if exchange:
        late_grads = received
    dp_all, dqw, dkw, d_rel = _swa_backward(d_ob, p_pad, qw_row, kw_row, rel_bias, bd, svb, dp_all)
    dw_pad = _matmul([(n2, dp_all)], ta=True, tm=D_MODEL, tn=N_PAD // 3, tk=2048, name="w_in_dw")
    dx1, d_mix_norm = _matmul([(dp_all, win_pad)], tb=True, tm=512, tn=D_MODEL, tk=N_PAD, name="w_in_dn",
                              norm_bwd=(x1, r2, small["mix_norm"], dx2))
    d_w_in = _w_in_grad_slabs(dw_pad, dw_dtype)
    dx, d_ffn1_norm, dwg1, dwu1, dwd1, got = _ffn_backward(
        dx1, x, small["ffn1_norm"], wg1, wu1, wd1, sv1, "ffn1", dw_dtype,
        scatter=[d_w_in] if exchange else None)
    if exchange:
        d_w_in = got[0]

    grads = {
        "ffn1_norm": d_ffn1_norm, "ffn1_w_gate": dwg1, "ffn1_w_up": dwu1, "ffn1_w_down": dwd1,
        "mix_norm": d_mix_norm, "w_in": d_w_in, "conv_w": dconv[:CONV_WIDTH].T,
        "a_log": gate_sums[0, :8].reshape(2, GDN_HEADS), "dt_bias": gate_sums[1, :8].reshape(2, GDN_HEADS),
        "gdn_norm_w": d_gnorm, "q_norm_w": dqw.reshape(SWA_HEADS, SWA_HEAD_DIM).sum(0, keepdims=True),
        "k_norm_w": dkw.reshape(SWA_HEADS, SWA_HEAD_DIM).sum(0, keepdims=True), "rel_bias": d_rel[:, :SWA_HEADS],
        "ffn2_norm": d_ffn2_norm, "final_norm": d_final, **dict(zip(LATE, late_grads)),
    }
    return loss_row, dx, grads


MESH_IDS = pl.DeviceIdType.MESH
ANY = pl.BlockSpec(memory_space=pl.ANY)


def _adamw(parts, w, m, v, name):
    nparts, r, n = parts.shape
    tr = r
    for cand in (256, 176, 128, 104, 64, 8):
        if r % cand == 0:
            tr = cand
            break
    bc1 = 1.0 - ADAM_B1 ** ADAM_STEP
    bc2 = 1.0 - ADAM_B2 ** ADAM_STEP

    def body(p_ref, w_ref, m_ref, v_ref, g_ref, d_ref, nm_ref, nv_ref):
        g = p_ref[0].astype(F32)
        for k in range(1, nparts):
            g = g + p_ref[k].astype(F32)
        mn = ADAM_B1 * m_ref[...] + (1.0 - ADAM_B1) * g
        vn = ADAM_B2 * v_ref[...] + (1.0 - ADAM_B2) * (g * g)
        m_hat = mn / bc1
        v_hat = vn / bc2
        g_ref[...] = g
        nm_ref[...] = mn
        nv_ref[...] = vn
        d_ref[...] = -ADAM_LR * (m_hat / (jnp.sqrt(v_hat) + ADAM_EPS) + ADAM_WD * w_ref[...])

    blk = pl.BlockSpec((tr, n), lambda i: (i, 0))
    return pl.pallas_call(
        body, name=name, grid=(r // tr,),
        in_specs=[pl.BlockSpec((nparts, tr, n), lambda i: (0, i, 0)), blk, blk, blk],
        out_specs=[blk] * 4, out_shape=[jax.ShapeDtypeStruct((r, n), F32)] * 4,
        compiler_params=_params("parallel"),
    )(parts, w, m, v)


def _mesh_place():
    x, y, c = lax.axis_index("x"), lax.axis_index("y"), lax.axis_index("c")
    return x, y, c, [(1 - x, y), (x, 1 - y), (1 - x, 1 - y)]


def _gather_phases(x_refs, out_refs, send_sems, recv_sems, local_sems):
    na = len(x_refs)

    def place():
        x, y, c, chips = _mesh_place()
        return (x, y, c), (x, y, 1 - c), chips, c

    def slab(i, px, py, pc):
        return out_refs[i].at[4 * px + 2 * py + pc]

    def copy(i, k, block, to, src=None):
        return pltpu.make_async_remote_copy(
            src_ref=slab(i, *block) if src is None else src, dst_ref=slab(i, *block),
            send_sem=send_sems.at[i, k], recv_sem=recv_sems.at[i, k], device_id=to, device_id_type=MESH_IDS)

    def own(i, me):
        return pltpu.make_async_copy(x_refs[i], slab(i, *me), local_sems.at[i])

    def sends(i, me, sibling, chips, c):
        return [copy(i, 0, me, sibling, src=x_refs[i])] + [copy(i, 1 + j, me, (*chip, c), src=x_refs[i])
                                                          for j, chip in enumerate(chips)]

    def start():
        me, sibling, chips, c = place()
        for i in range(na):
            own(i, me).start()
            for cp in sends(i, me, sibling, chips, c):
                cp.start()

    def forward():
        me, sibling, chips, c = place()
        for j, chip in enumerate(chips):
            for i in range(na):
                copy(i, 1 + j, (*chip, c), me).wait_recv()
                copy(i, 4 + j, (*chip, c), sibling).start()

    def finish():
        me, sibling, chips, c = place()
        for i in range(na):
            copy(i, 0, sibling, me).wait_recv()
        for j, chip in enumerate(chips):
            for i in range(na):
                copy(i, 4 + j, (*chip, 1 - c), me).wait_recv()
        for i in range(na):
            for cp in sends(i, me, sibling, chips, c):
                cp.wait_send()
            for j, chip in enumerate(chips):
                copy(i, 4 + j, (*chip, c), sibling).wait_send()
            own(i, me).wait()

    return start, forward, finish


def _gather_semaphores(na):
    return [pltpu.SemaphoreType.DMA((na, 7)), pltpu.SemaphoreType.DMA((na, 7)), pltpu.SemaphoreType.DMA((na,))]


def _scatter_phases(g_refs, out_refs, send_sems, recv_sems, local_sems):
    na = len(g_refs)

    def place(m):
        x, y, c = lax.axis_index("x"), lax.axis_index("y"), lax.axis_index("c")
        px = 1 - x if m & 4 else x
        py = 1 - y if m & 2 else y
        pc = 1 - c if m & 1 else c
        return 4 * x + 2 * y + c, (px, py, pc), 4 * px + 2 * py + pc

    def own(i):
        me, _, _ = place(0)
        return pltpu.make_async_copy(g_refs[i].at[me], out_refs[i].at[me], local_sems.at[i])

    def start():
        for i in range(na):
            own(i).start()
            for m in range(1, N_DEV):
                me, peer, peer_idx = place(m)
                pltpu.make_async_remote_copy(
                    src_ref=g_refs[i].at[peer_idx], dst_ref=out_refs[i].at[me], send_sem=send_sems.at[i, m - 1],
                    recv_sem=recv_sems.at[i, m - 1], device_id=peer, device_id_type=MESH_IDS).start()

    def finish():
        for i in range(na):
            for m in range(1, N_DEV):
                me, peer, peer_idx = place(m)
                cp = pltpu.make_async_remote_copy(
                    src_ref=g_refs[i].at[peer_idx], dst_ref=out_refs[i].at[peer_idx], send_sem=send_sems.at[i, m - 1],
                    recv_sem=recv_sems.at[i, m - 1], device_id=peer, device_id_type=MESH_IDS)
                cp.wait_recv()
                cp.wait_send()
            own(i).wait()

    return start, finish


def _all_gather_many(vs, name):
    na = len(vs)

    def body(*refs):
        x_refs, out_refs = refs[:na], refs[na:2 * na]
        for step in _gather_phases(x_refs, out_refs, *refs[2 * na:]):
            step()

    return pl.pallas_call(
        body, name=name, in_specs=[ANY] * na, out_specs=[ANY] * na,
        out_shape=[jax.ShapeDtypeStruct((N_DEV,) + v.shape, v.dtype) for v in vs],
        scratch_shapes=_gather_semaphores(na),
        compiler_params=pltpu.CompilerParams(vmem_limit_bytes=V7X_VMEM_LIMIT_BYTES),
    )(*vs)


BIG = ("ffn1_w_gate", "ffn1_w_up", "ffn1_w_down", "w_in", "w_out", "ffn2_w_gate", "ffn2_w_up", "ffn2_w_down")
SMALL = ("ffn1_norm", "mix_norm", "a_log", "dt_bias", "gdn_norm_w", "q_norm_w", "k_norm_w", "rel_bias",
         "ffn2_norm", "final_norm")
WEIGHTS = ("ffn1_norm", "ffn1_w_gate", "ffn1_w_up", "ffn1_w_down", "mix_norm", "w_in", "conv_w", "a_log", "dt_bias",
           "gdn_norm_w", "q_norm_w", "k_norm_w", "rel_bias", "w_out", "ffn2_norm", "ffn2_w_gate", "ffn2_w_up",
           "ffn2_w_down", "final_norm")


def _pack(arrays, width, row_multiple):
    flat = jnp.concatenate([a.reshape(-1) for a in arrays])
    rows = -(-flat.shape[0] // width)
    rows = -(-rows // row_multiple) * row_multiple
    return jnp.pad(flat, (0, rows * width - flat.shape[0])).reshape(rows, width)


def _unpack(packed, shapes):
    flat = packed.reshape(-1)
    out, pos = [], 0
    for shp in shapes:
        size = int(np.prod(shp))
        out.append(flat[pos:pos + size].reshape(shp))
        pos += size
    return out


def kernel(x, ffn1_norm, ffn1_w_gate, ffn1_w_up, ffn1_w_down, mix_norm, w_in, conv_w, a_log, dt_bias, gdn_norm_w, q_norm_w, k_norm_w, rel_bias, w_out, ffn2_norm, ffn2_w_gate, ffn2_w_up, ffn2_w_down, final_norm, loss_target, m_ffn1_norm, m_ffn1_w_gate, m_ffn1_w_up, m_ffn1_w_down, m_mix_norm, m_w_in, m_conv_w, m_a_log, m_dt_bias, m_gdn_norm_w, m_q_norm_w, m_k_norm_w, m_rel_bias, m_w_out, m_ffn2_norm, m_ffn2_w_gate, m_ffn2_w_up, m_ffn2_w_down, m_final_norm, v_ffn1_norm, v_ffn1_w_gate, v_ffn1_w_up, v_ffn1_w_down, v_mix_norm, v_w_in, v_conv_w, v_a_log, v_dt_bias, v_gdn_norm_w, v_q_norm_w, v_k_norm_w, v_rel_bias, v_w_out, v_ffn2_norm, v_ffn2_w_gate, v_ffn2_w_up, v_ffn2_w_down, v_final_norm):
    w = dict(ffn1_norm=ffn1_norm, ffn1_w_gate=ffn1_w_gate, ffn1_w_up=ffn1_w_up, ffn1_w_down=ffn1_w_down, mix_norm=mix_norm, w_in=w_in, conv_w=conv_w, a_log=a_log, dt_bias=dt_bias, gdn_norm_w=gdn_norm_w, q_norm_w=q_norm_w, k_norm_w=k_norm_w, rel_bias=rel_bias, w_out=w_out, ffn2_norm=ffn2_norm, ffn2_w_gate=ffn2_w_gate, ffn2_w_up=ffn2_w_up, ffn2_w_down=ffn2_w_down, final_norm=final_norm)
    mom = dict(ffn1_norm=m_ffn1_norm, ffn1_w_gate=m_ffn1_w_gate, ffn1_w_up=m_ffn1_w_up, ffn1_w_down=m_ffn1_w_down, mix_norm=m_mix_norm, w_in=m_w_in, conv_w=m_conv_w, a_log=m_a_log, dt_bias=m_dt_bias, gdn_norm_w=m_gdn_norm_w, q_norm_w=m_q_norm_w, k_norm_w=m_k_norm_w, rel_bias=m_rel_bias, w_out=m_w_out, ffn2_norm=m_ffn2_norm, ffn2_w_gate=m_ffn2_w_gate, ffn2_w_up=m_ffn2_w_up, ffn2_w_down=m_ffn2_w_down, final_norm=m_final_norm)
    var = dict(ffn1_norm=v_ffn1_norm, ffn1_w_gate=v_ffn1_w_gate, ffn1_w_up=v_ffn1_w_up, ffn1_w_down=v_ffn1_w_down, mix_norm=v_mix_norm, w_in=v_w_in, conv_w=v_conv_w, a_log=v_a_log, dt_bias=v_dt_bias, gdn_norm_w=v_gdn_norm_w, q_norm_w=v_q_norm_w, k_norm_w=v_k_norm_w, rel_bias=v_rel_bias, w_out=v_w_out, ffn2_norm=v_ffn2_norm, ffn2_w_gate=v_ffn2_w_gate, ffn2_w_up=v_ffn2_w_up, ffn2_w_down=v_ffn2_w_down, final_norm=v_final_norm)
    ix, iy, ic = lax.axis_index("x"), lax.axis_index("y"), lax.axis_index("c")
    me = 4 * ix + 2 * iy + ic

    def local(a, n):
        return jnp.swapaxes(a[0], 0, 1) if n in TRANSPOSED else a[0]

    shard = {n: local(w[n], n) for n in BIG}

    conv_shard_shape = w["conv_w"][0].shape
    small = {n: w[n][0] if n not in ("rel_bias",) else w[n] for n in SMALL}
    small = {n: (a.reshape(1, -1) if n.endswith("norm") else a) for n, a in small.items()}
    shards = {n: shard[n].astype(BF16) for n in BIG}
    shards["conv_w"] = _pack([w["conv_w"][0]], LANE, 8)
    loss_row, grad_x, grads = _local_step(x[0], loss_target[0], {}, small, late_shards=shards)

    big_out = [[], [], [], []]
    for n in BIG:
        for kind, val in enumerate(_adamw(grads[n], shard[n], local(mom[n], n), local(var[n], n), f"{n}_adamw")):
            big_out[kind].append(jnp.swapaxes(val, 0, 1) if n in TRANSPOSED else val)

    small_names = SMALL + ("conv_w",)
    small_shapes = [grads[n].shape for n in small_names] + [(1, 1)]
    g_small = _pack([grads[n] for n in small_names] + [loss_row[:, :1]], LANE, 8)
    all_small = _all_gather_many([g_small], "gather_small_grads")[0]
    riders = [jnp.zeros(shp, F32) for shp in small_shapes[len(SMALL):]]
    ws = _pack([w[n].reshape(grads[n].shape) for n in SMALL] + riders, LANE, 8)
    ms = _pack([mom[n].reshape(grads[n].shape) for n in SMALL] + riders, LANE, 8)
    vs = _pack([var[n].reshape(grads[n].shape) for n in SMALL] + riders, LANE, 8)
    small_out = [_unpack(a, small_shapes) for a in _adamw(all_small, ws, ms, vs, "adamw_small")]
    loss = small_out[0][-1][0, 0]
    conv_g = lax.dynamic_slice_in_dim(small_out[0][len(SMALL)], me * conv_shard_shape[0], conv_shard_shape[0], axis=0)
    conv_out = [_unpack(a, [conv_shard_shape])[0] for a in _adamw(
        _pack([conv_g], LANE, 8)[None], _pack([w["conv_w"][0]], LANE, 8), _pack([mom["conv_w"][0]], LANE, 8),
        _pack([var["conv_w"][0]], LANE, 8), "adamw_conv")]

    def leaf(kind, n):
        if n in BIG:
            val = big_out[kind][BIG.index(n)]
        elif n == "conv_w":
            val = conv_out[kind]
        else:
            val = small_out[kind][SMALL.index(n)]
        return val.reshape(w[n].shape)

    outs = [loss, grad_x[None]]
    for kind in range(4):
        outs += [leaf(kind, n) for n in WEIGHTS]
    return tuple(outs)
```

```python
import functools
import math

import numpy as np
import jax
import jax.numpy as jnp
from jax import lax
from jax.experimental import pallas as pl
from jax.experimental.pallas import tpu as pltpu

F32 = jnp.float32
BF16 = jnp.bfloat16

D_MODEL = 1024
D_FF = 2816
GDN_HEADS = 4
GDN_HEAD_DIM = 128
GDN_WIDTH = 512
CONV_WIDTH = 5
CHUNK = 64
SWA_HEADS = 8
SWA_HEAD_DIM = 64
SWA_WIDTH = 512
DILATION_PATTERNS = ((128, 1), (512, 4), (2048, 16))
REL_BUCKETS = 32
REL_MAX_DISTANCE = 1024
EPS = 1e-6
NEG_BIG = -1e30
N_DEV = 8

ADAM_LR = 0.001
ADAM_B1 = 0.9
ADAM_B2 = 0.999
ADAM_EPS = 1e-08
ADAM_WD = 0.01
ADAM_STEP = 10

QKV_A = 3 * GDN_WIDTH
OFF_B = QKV_A
OFF_Z = OFF_B + 3 * SWA_WIDTH
OFF_AB = OFF_Z + GDN_WIDTH
N_PAD = OFF_AB + 256
N_IN = 3600
NAT_Z, NAT_AB, NAT_B = QKV_A, QKV_A + GDN_WIDTH, QKV_A + GDN_WIDTH + 16

V7X_VMEM_LIMIT_BYTES = 56 * 1024 * 1024
LANE = 128
ATT_BQ = 128
ATT_HALO = 64
CONV_ROWS = 256

NN = (((1,), (0,)), ((), ()))
NT = (((1,), (1,)), ((), ()))
TN = (((0,), (0,)), ((), ()))


def _params(*sem):
    return pltpu.CompilerParams(dimension_semantics=sem, vmem_limit_bytes=V7X_VMEM_LIMIT_BYTES)


def _dot(a, b, dn=NN):
    return lax.dot_general(a.astype(BF16), b.astype(BF16), dn, preferred_element_type=F32)


def _sigmoid(x):
    return 1.0 / (1.0 + jnp.exp(-x))


class _Exchange:
    def __init__(self, kind, arrays):
        self.kind, self.arrays = kind, list(arrays)

    def out_shape(self):
        lead = (N_DEV,) if self.kind == "gather" else ()
        return [jax.ShapeDtypeStruct(lead + v.shape, v.dtype) for v in self.arrays]

    def hooks(self, in_refs, out_refs, sems, grid):
        step = pl.program_id(0)
        for axis in range(1, len(grid)):
            step = step * grid[axis] + pl.program_id(axis)
        total = math.prod(grid)
        if self.kind == "gather":
            assert total >= 4
            start, forward, finish = _gather_phases(in_refs, out_refs, *sems)
            pl.when(step == total // 2)(forward)
        else:
            assert total >= 2
            start, finish = _scatter_phases(in_refs, out_refs, *sems)
        pl.when(step == 0)(start)
        pl.when(step == total - 1)(finish)


def _pallas(body, *, name, grid, in_specs, out_specs, out_shape, args, semantics, scratch_shapes=(), exchange=None):
    n_in, n_out, n_scr = len(in_specs), len(out_specs), len(scratch_shapes)
    if exchange is None:
        res = pl.pallas_call(
            body, name=name, grid=grid, in_specs=list(in_specs), out_specs=list(out_specs), out_shape=list(out_shape),
            scratch_shapes=list(scratch_shapes), compiler_params=_params(*semantics))(*args)
        return list(res), []
    na = len(exchange.arrays)

    def carrying(*refs):
        ins, sent = refs[:n_in], refs[n_in:n_in + na]
        outs = refs[n_in + na:n_in + na + n_out]
        landed = refs[n_in + na + n_out:n_in + 2 * na + n_out]
        rest = refs[n_in + 2 * na + n_out:]
        exchange.hooks(sent, landed, rest[n_scr:], grid)
        body(*ins, *outs, *rest[:n_scr])

    res = pl.pallas_call(
        carrying, name=name, grid=grid, in_specs=list(in_specs) + [ANY] * na, out_specs=list(out_specs) + [ANY] * na,
        out_shape=list(out_shape) + exchange.out_shape(), scratch_shapes=list(scratch_shapes) + _gather_semaphores(na),
        compiler_params=_params(*(["arbitrary"] * len(grid))))(*args, *exchange.arrays)
    return list(res[:n_out]), list(res[n_out:])


def _matmul(pairs, *, ta=False, tb=False, out_dtype=F32, tm, tn, tk, name, res=None, alpha=None, norm_bwd=None,
            norm_fwd=None, loss=None, exchange=None):
    a0, b0 = pairs[0]
    m = a0.shape[1] if ta else a0.shape[0]
    k = a0.shape[0] if ta else a0.shape[1]
    n = b0.shape[0] if tb else b0.shape[1]
    tm, tn, tk = min(tm, m), min(tn, n), min(tk, k)
    assert m % tm == 0 and n % tn == 0 and k % tk == 0, (name, m, n, k, tm, tn, tk)
    nk = k // tk
    npairs = len(pairs)
    dn = (((0 if ta else 1,), (1 if tb else 0,)), ((), ()))
    assert norm_bwd is None or (tn == n and res is None and alpha is None)

    def body(*refs):
        ins = refs[:2 * npairs]
        pos = 2 * npairs
        r_ref = None
        if res is not None:
            r_ref = refs[pos]
            pos += 1
        if norm_bwd is not None:
            x_ref, rs_ref, w_ref, dres_ref = refs[pos:pos + 4]
            o_ref, dw_ref, acc = refs[pos + 4:pos + 7]

            @pl.when((pl.program_id(0) == 0) & (pl.program_id(2) == 0))
            def _():
                dw_ref[...] = jnp.zeros_like(dw_ref)
        elif norm_fwd is not None:
            wn_ref, o_ref, n_ref, rs_out, acc = refs[pos:pos + 5]
        elif loss is not None:
            wf_ref, tg_ref, loss_ref, o_ref, dwf_ref, acc = refs[pos:pos + 6]

            @pl.when((pl.program_id(0) == 0) & (pl.program_id(2) == 0))
            def _():
                dwf_ref[...] = jnp.zeros_like(dwf_ref)
                loss_ref[...] = jnp.zeros_like(loss_ref)
        else:
            o_ref, acc = refs[pos], refs[pos + 1]
        kk = pl.program_id(2)
        t = None
        for p in range(npairs):
            d = _dot(ins[2 * p][...], ins[2 * p + 1][...], dn)
            t = d if t is None else t + d

        if nk > 1:
            @pl.when(kk == 0)
            def _():
                acc[...] = t

            @pl.when((kk > 0) & (kk < nk - 1))
            def _():
                acc[...] += t

        @pl.when(kk == nk - 1)
        def _():
            r = acc[...] + t if nk > 1 else t
            if alpha is not None:
                r = r * alpha
            if r_ref is not None:
                r = r_ref[...] + r
            if norm_bwd is not None:
                rs = rs_ref[...]
                xhat = x_ref[...] * rs
                dw_ref[...] += jnp.sum(r * xhat, axis=0, keepdims=True)
                t_w = r * w_ref[...]
                r = dres_ref[...] + rs * (t_w - xhat * jnp.mean(t_w * xhat, axis=-1, keepdims=True))
            if norm_fwd is not None:
                rs = lax.rsqrt(jnp.mean(r * r, axis=-1, keepdims=True) + EPS)
                n_ref[...] = (r * rs * wn_ref[...]).astype(BF16)
                rs_out[...] = rs
            if loss is not None:
                wv = wf_ref[...]
                rs = lax.rsqrt(jnp.mean(r * r, axis=-1, keepdims=True) + EPS)
                xhat = r * rs
                e = xhat * wv - tg_ref[...]
                part = 0.5 * jnp.sum(jnp.mean(e * e, axis=-1, keepdims=True), axis=0, keepdims=True)
                loss_ref[...] += jnp.broadcast_to(part, loss_ref.shape)
                dy = e * (1.0 / n)
                dwf_ref[...] += jnp.sum(dy * xhat, axis=0, keepdims=True)
                t_w = dy * wv
                r = rs * (t_w - xhat * jnp.mean(t_w * xhat, axis=-1, keepdims=True))
            o_ref[...] = r.astype(out_dtype)

    a_spec = pl.BlockSpec((tk, tm), lambda i, j, kk: (kk, i)) if ta else pl.BlockSpec((tm, tk), lambda i, j, kk: (i, kk))
    b_spec = pl.BlockSpec((tn, tk), lambda i, j, kk: (j, kk)) if tb else pl.BlockSpec((tk, tn), lambda i, j, kk: (kk, j))
    o_spec = pl.BlockSpec((tm, tn), lambda i, j, kk: (i, j))
    in_specs = [a_spec, b_spec] * npairs + ([o_spec] if res is not None else [])
    args = [t for pr in pairs for t in pr] + ([res] if res is not None else [])
    out_specs, out_shape = [o_spec], [jax.ShapeDtypeStruct((m, n), out_dtype)]
    if norm_bwd is not None:
        vec = pl.BlockSpec((1, n), lambda i, j, kk: (0, 0))
        in_specs += [o_spec, pl.BlockSpec((tm, 1), lambda i, j, kk: (i, 0)), vec, o_spec]
        args += list(norm_bwd)
        out_specs.append(vec)
        out_shape.append(jax.ShapeDtypeStruct((1, n), F32))
    if norm_fwd is not None:
        assert tn == n and norm_bwd is None
        in_specs.append(pl.BlockSpec((1, n), lambda i, j, kk: (0, 0)))
        args.append(norm_fwd)
        out_specs += [o_spec, pl.BlockSpec((tm, 1), lambda i, j, kk: (i, 0))]
        out_shape += [jax.ShapeDtypeStruct((m, n), BF16), jax.ShapeDtypeStruct((m, 1), F32)]
    if loss is not None:
        assert tn == n and norm_bwd is None and norm_fwd is None
        vec = pl.BlockSpec((1, n), lambda i, j, kk: (0, 0))
        in_specs += [vec, o_spec]
        args += list(loss)
        out_specs = [pl.BlockSpec((1, LANE), lambda i, j, kk: (0, 0))] + out_specs + [vec]
        out_shape = [jax.ShapeDtypeStruct((1, LANE), F32)] + out_shape + [jax.ShapeDtypeStruct((1, n), F32)]
    sequential = norm_bwd is not None or loss is not None
    outs, exchanged = _pallas(
        body, name=name, grid=(m // tm, n // tn, nk), in_specs=in_specs, out_specs=out_specs, out_shape=out_shape,
        scratch_shapes=[pltpu.VMEM((tm, tn) if nk > 1 else (8, LANE), F32)],
        semantics=("arbitrary",) * 3 if sequential else ("parallel", "parallel", "arbitrary"), args=args,
        exchange=exchange)
    out = outs[0] if len(outs) == 1 else tuple(outs)
    return out if exchange is None else (out, exchanged)


def _rms_fwd(x, w, name, exchange=None):
    s, d = x.shape
    tm = min(512, s)

    def body(x_ref, w_ref, n_ref, r_ref):
        xv = x_ref[...]
        r = lax.rsqrt(jnp.mean(xv * xv, axis=-1, keepdims=True) + EPS)
        n_ref[...] = (xv * r * w_ref[...]).astype(BF16)
        r_ref[...] = r

    (n, r), exchanged = _pallas(
        body, name=name, grid=(s // tm,),
        in_specs=[pl.BlockSpec((tm, d), lambda i: (i, 0)), pl.BlockSpec((1, d), lambda i: (0, 0))],
        out_specs=[pl.BlockSpec((tm, d), lambda i: (i, 0)), pl.BlockSpec((tm, 1), lambda i: (i, 0))],
        out_shape=[jax.ShapeDtypeStruct((s, d), BF16), jax.ShapeDtypeStruct((s, 1), F32)],
        semantics=("parallel",), args=(x, w), exchange=exchange)
    return (n, r) if exchange is None else (n, r, exchanged)


def _rms_bwd(dn, x, r, w, dres, name, exchange=None):
    s, d = x.shape
    tm = min(512, s)

    def body(dn_ref, x_ref, r_ref, w_ref, dres_ref, dx_ref, dw_ref):
        @pl.when(pl.program_id(0) == 0)
        def _():
            dw_ref[...] = jnp.zeros_like(dw_ref)

        rv = r_ref[...]
        xhat = x_ref[...] * rv
        g = dn_ref[...]
        t = g * w_ref[...]
        dx_ref[...] = dres_ref[...] + rv * (t - xhat * jnp.mean(t * xhat, axis=-1, keepdims=True))
        dw_ref[...] += jnp.sum(g * xhat, axis=0, keepdims=True)

    row = pl.BlockSpec((tm, d), lambda i: (i, 0))
    vec = pl.BlockSpec((1, d), lambda i: (0, 0))
    (dx, dw), exchanged = _pallas(
        body, name=name, grid=(s // tm,),
        in_specs=[row, row, pl.BlockSpec((tm, 1), lambda i: (i, 0)), vec, row],
        out_specs=[row, vec],
        out_shape=[jax.ShapeDtypeStruct((s, d), F32), jax.ShapeDtypeStruct((1, d), F32)],
        semantics=("arbitrary",), args=(dn, x, r, w, dres), exchange=exchange)
    return (dx, dw) if exchange is None else (dx, dw, exchanged)


def _ffn_up(n, wg, wu, name, exchange=None):
    s, d = n.shape
    f = wg.shape[0]
    tm, tn = min(512, s), f // 2

    def body(n_ref, wg_ref, wu_ref, g_ref, u_ref, a_ref):
        nv = n_ref[...]
        g = _dot(nv, wg_ref[...], NT)
        u = _dot(nv, wu_ref[...], NT)
        g_ref[...] = g.astype(BF16)
        u_ref[...] = u.astype(BF16)
        a_ref[...] = (g * _sigmoid(g) * u).astype(BF16)

    o = pl.BlockSpec((tm, tn), lambda j, i: (i, j))
    wspec = pl.BlockSpec((tn, d), lambda j, i: (j, 0))
    return _pallas(
        body, name=name, grid=(f // tn, s // tm),
        in_specs=[pl.BlockSpec((tm, d), lambda j, i: (i, 0)), wspec, wspec],
        out_specs=[o, o, o],
        out_shape=[jax.ShapeDtypeStruct((s, f), BF16)] * 3,
        semantics=("parallel", "parallel"), args=(n, wg, wu), exchange=exchange)


def _ffn_dact(dx, wd, g, u, name, exchange=None):
    s, d = dx.shape
    f = wd.shape[0]
    tm, tn = min(512, s), f // 2

    def body(dx_ref, wd_ref, g_ref, u_ref, dg_ref, du_ref):
        da = 0.5 * _dot(dx_ref[...], wd_ref[...], NT)
        gv = g_ref[...].astype(F32)
        sg = _sigmoid(gv)
        du_ref[...] = (da * gv * sg).astype(BF16)
        dg_ref[...] = (da * u_ref[...].astype(F32) * (sg * (1.0 + gv * (1.0 - sg)))).astype(BF16)

    o = pl.BlockSpec((tm, tn), lambda j, i: (i, j))
    return _pallas(
        body, name=name, grid=(f // tn, s // tm),
        in_specs=[pl.BlockSpec((tm, d), lambda j, i: (i, 0)), pl.BlockSpec((tn, d), lambda j, i: (j, 0)), o, o],
        out_specs=[o, o],
        out_shape=[jax.ShapeDtypeStruct((s, f), BF16), jax.ShapeDtypeStruct((s, f), BF16)],
        semantics=("parallel", "parallel"), args=(dx, wd, g, u), exchange=exchange)


def _row_slabs(full):
    return full.reshape(N_DEV, full.shape[0] // N_DEV, full.shape[1])


def _rows_of(slabs):
    return slabs.reshape(N_DEV * slabs.shape[1], slabs.shape[2])


def _ffn_forward(x, norm_w, wg, wu, wd, tag, gather=(), head=(), normed=None, next_norm=None, loss=None):
    if normed is not None:
        (n, r), first = normed, []
    elif head:
        n, r, first = _rms_fwd(x, norm_w, f"{tag}_norm", _Exchange("gather", head))
    else:
        (n, r), first = _rms_fwd(x, norm_w, f"{tag}_norm"), []
    if wg is None:
        wg, wu, first = _rows_of(first[0]), _rows_of(first[1]), first[2:]
    (g, u, a), got = _ffn_up(n, wg, wu, f"{tag}_up", _Exchange("gather", gather) if gather else None)
    if wd is None:
        wd, got = _rows_of(got[0]), got[1:]
    y = _matmul([(a, wd)], tm=512, tn=1024, tk=wd.shape[0], name=f"{tag}_down", res=x, alpha=0.5, norm_fwd=next_norm,
                loss=loss)
    y, nxt = (y[0], y[1:]) if next_norm is not None else (y, None)
    return y, (n, r, g, u, a), (wg, wu, wd), got, first, nxt


def _ffn_backward(dy, x, norm_w, wgt, wut, wd, saved, tag, dw_dtype=F32, scatter=None):
    n, r, g, u, a = saved

    def behind(arrays):
        return _Exchange("scatter", arrays) if scatter is not None else None

    def dw(act, grad, name, alpha=None, exchange=None):
        return _matmul([(act, grad)], ta=True, tm=1408, tn=1024, tk=2048, name=name, alpha=alpha, out_dtype=dw_dtype,
                       exchange=exchange)

    dwd = _row_slabs(dw(a, dy, f"{tag}_dwd", alpha=0.5))
    (dg, du), extras = _ffn_dact(dy, wd, g, u, f"{tag}_dact", behind(scatter))
    if scatter is None:
        dwg, dwu = _row_slabs(dw(dg, n, f"{tag}_dwg")), _row_slabs(dw(du, n, f"{tag}_dwu"))
    else:
        dwg, (dwd,) = dw(dg, n, f"{tag}_dwg", exchange=behind([dwd]))
        dwu, (dwg,) = dw(du, n, f"{tag}_dwu", exchange=behind([_row_slabs(dwg)]))
        dwu = _row_slabs(dwu)
    if scatter is None:
        dx, dnorm = _matmul([(dg, wgt), (du, wut)], tm=512, tn=1024, tk=wgt.shape[0], name=f"{tag}_dn",
                            norm_bwd=(x, r, norm_w, dy))
    else:
        dn, (dwu,) = _matmul([(dg, wgt), (du, wut)], tm=512, tn=1024, tk=wgt.shape[0], name=f"{tag}_dn",
                             exchange=behind([dwu]))
        dx, dnorm = _rms_bwd(dn, x, r, norm_w, dy, f"{tag}_dnorm")
    return dx, dnorm, dwg, dwu, dwd, extras


Q_SCALE = GDN_HEAD_DIM ** -0.5
CONV_HALO = 8


def _lane_block(s):
    return pl.BlockSpec((None, s, LANE), lambda j: (j, 0, 0))


def _conv_taps(win, w_ref, rows, sign):
    n = rows + 2 * CONV_HALO
    acc = None
    for t in range(CONV_WIDTH):
        o = sign * (t - CONV_WIDTH // 2)
        sh = win if o == 0 else pltpu.roll(win, (-o) % n, 0)
        term = sh[CONV_HALO:CONV_HALO + rows] * w_ref[t:t + 1, :]
        acc = term if acc is None else acc + term
    return acc


def _gdn_conv_fwd(p_pad, conv_wt):
    s = p_pad.shape[0]
    rows = min(CONV_ROWS, s)
    nblk = QKV_A // LANE

    def body(p_ref, w_ref, c_ref, y_ref, pad):
        j = pl.program_id(0)
        zeros = jnp.zeros((CONV_HALO, LANE), F32)
        pad[0:CONV_HALO, :] = zeros
        pad[CONV_HALO + s:2 * CONV_HALO + s, :] = zeros
        pad[CONV_HALO:CONV_HALO + s, :] = p_ref[...]

        def chunk(ci, carry):
            b = pl.multiple_of(ci * rows, rows)
            win = pad[pl.ds(b, rows + 2 * CONV_HALO), :]
            c = _conv_taps(win, w_ref, rows, 1)
            c_ref[pl.ds(b, rows), :] = c
            act = c * _sigmoid(c)
            nrm = lax.rsqrt(jnp.sum(act * act, axis=-1, keepdims=True) + EPS)
            mult = jnp.where(j < GDN_HEADS, nrm * Q_SCALE, jnp.where(j < 2 * GDN_HEADS, nrm, 1.0))
            y_ref[pl.ds(b, rows), :] = act * mult
            return carry

        lax.fori_loop(0, s // rows, chunk, 0)

    col = pl.BlockSpec((s, LANE), lambda j: (0, j))
    return pl.pallas_call(
        body, name="gdn_conv_fwd", grid=(nblk,),
        in_specs=[col, pl.BlockSpec((8, LANE), lambda j: (0, j))],
        out_specs=[_lane_block(s), _lane_block(s)],
        out_shape=[jax.ShapeDtypeStruct((nblk, s, LANE), F32), jax.ShapeDtypeStruct((nblk, s, LANE), F32)],
        scratch_shapes=[pltpu.VMEM((s + 2 * CONV_HALO, LANE), F32)],
        compiler_params=_params("parallel"),
    )(p_pad, conv_wt)


def _gdn_conv_bwd(dy_f, dy_r, c_pre, p_pad, conv_wt, dp_all):
    s = p_pad.shape[0]
    rows = min(CONV_ROWS, s)
    nblk = QKV_A // LANE

    def body(dyf_ref, dyr_ref, c_ref, p_ref, w_ref, _, dp_ref, dw_ref, ppad, dcpad):
        j = pl.program_id(0)
        zeros = jnp.zeros((CONV_HALO, LANE), F32)
        for buf in (ppad, dcpad):
            buf[0:CONV_HALO, :] = zeros
            buf[CONV_HALO + s:2 * CONV_HALO + s, :] = zeros
        ppad[CONV_HALO:CONV_HALO + s, :] = p_ref[...]

        def act_bwd(ci, carry):
            b = pl.multiple_of(ci * rows, rows)
            c = c_ref[pl.ds(b, rows), :]
            g = dyf_ref[pl.ds(b, rows), :] + dyr_ref[pl.ds(b, rows), :]
            sg = _sigmoid(c)
            act = c * sg
            nrm = lax.rsqrt(jnp.sum(act * act, axis=-1, keepdims=True) + EPS)
            yh = act * nrm
            scale = jnp.where(j < GDN_HEADS, Q_SCALE, 1.0)
            dact_qk = (scale * nrm) * (g - yh * jnp.sum(g * yh, axis=-1, keepdims=True))
            dact = jnp.where(j < 2 * GDN_HEADS, dact_qk, g)
            dcpad[pl.ds(pl.multiple_of(b + CONV_HALO, CONV_HALO), rows), :] = dact * (sg * (1.0 + c * (1.0 - sg)))
            return carry

        lax.fori_loop(0, s // rows, act_bwd, 0)
        tap = lax.broadcasted_iota(jnp.int32, (8, LANE), 0)

        def taps_bwd(ci, dw):
            b = pl.multiple_of(ci * rows, rows)
            dcw = dcpad[pl.ds(b, rows + 2 * CONV_HALO), :]
            dp_ref[pl.ds(b, rows), :] = _conv_taps(dcw, w_ref, rows, -1).astype(BF16)
            pw = ppad[pl.ds(b, rows + 2 * CONV_HALO), :]
            dc = dcw[CONV_HALO:CONV_HALO + rows]
            n = rows + 2 * CONV_HALO
            for t in range(CONV_WIDTH):
                o = t - CONV_WIDTH // 2
                sh = pw if o == 0 else pltpu.roll(pw, (-o) % n, 0)
                row = jnp.sum(dc * sh[CONV_HALO:CONV_HALO + rows], axis=0, keepdims=True)
                dw = dw + jnp.where(tap == t, row, 0.0)
            return dw

        dw_ref[...] = lax.fori_loop(0, s // rows, taps_bwd, jnp.zeros((8, LANE), F32))

    col = pl.BlockSpec((s, LANE), lambda j: (0, j))
    wspec = pl.BlockSpec((8, LANE), lambda j: (0, j))
    return pl.pallas_call(
        body, name="gdn_conv_bwd", grid=(nblk,),
        in_specs=[_lane_block(s), _lane_block(s), _lane_block(s), col, wspec, ANY],
        out_specs=[col, wspec],
        out_shape=[jax.ShapeDtypeStruct(dp_all.shape, dp_all.dtype), jax.ShapeDtypeStruct((8, QKV_A), F32)],
        scratch_shapes=[pltpu.VMEM((s + 2 * CONV_HALO, LANE), F32), pltpu.VMEM((s + 2 * CONV_HALO, LANE), F32)],
        input_output_aliases={5: 0},
        compiler_params=_params("parallel"),
    )(dy_f, dy_r, c_pre, p_pad, conv_wt, dp_all)


def _softplus(x):
    return jnp.maximum(x, 0.0) + jnp.log(1.0 + jnp.exp(-jnp.abs(x)))


def _gdn_gates_fwd(p_pad, alog_row, dt_row):
    s = p_pad.shape[0]
    tm = min(1024, s)

    def body(p_ref, al_ref, dt_ref, o_ref):
        x = p_ref[...]
        lane = lax.broadcasted_iota(jnp.int32, x.shape, 1)
        g = -jnp.exp(al_ref[...]) * _softplus(x + dt_ref[...])
        o_ref[...] = jnp.where(lane < 8, g, jnp.where(lane < 16, _sigmoid(x), 0.0))

    vec = pl.BlockSpec((1, LANE), lambda i: (0, 0))
    return pl.pallas_call(
        body, name="gdn_gates_fwd", grid=(s // tm,),
        in_specs=[pl.BlockSpec((tm, LANE), lambda i: (i, OFF_AB // LANE)), vec, vec],
        out_specs=pl.BlockSpec((tm, LANE), lambda i: (i, 0)),
        out_shape=jax.ShapeDtypeStruct((s, LANE), F32),
        compiler_params=_params("parallel"),
    )(p_pad, alog_row, dt_row)


def _gdn_gates_bwd(dgb_f, dgb_r, p_pad, gb, alog_row, dt_row, dp_all):
    s = p_pad.shape[0]
    tm = min(1024, s)
    tail = N_PAD - OFF_AB

    def body(df_ref, dr_ref, p_ref, gb_ref, al_ref, dt_ref, _, dp_ref, sum_ref):
        @pl.when(pl.program_id(0) == 0)
        def _():
            sum_ref[...] = jnp.zeros_like(sum_ref)

        x = p_ref[...]
        gbv = gb_ref[...]
        dgb = df_ref[...] + dr_ref[...]
        lane = lax.broadcasted_iota(jnp.int32, x.shape, 1)
        da = dgb * (-jnp.exp(al_ref[...])) * _sigmoid(x + dt_ref[...])
        db = dgb * gbv * (1.0 - gbv)
        dp_ref[:, 0:LANE] = jnp.where(lane < 8, da, jnp.where(lane < 16, db, 0.0)).astype(BF16)
        dp_ref[:, LANE:tail] = jnp.zeros((tm, tail - LANE), BF16)
        row = lax.broadcasted_iota(jnp.int32, (8, LANE), 0)
        lane8 = lax.broadcasted_iota(jnp.int32, (8, LANE), 1)
        d_alog = jnp.sum(dgb * gbv, axis=0, keepdims=True)
        d_dt = jnp.sum(da, axis=0, keepdims=True)
        upd = jnp.where(row == 0, d_alog, jnp.where(row == 1, d_dt, 0.0))
        sum_ref[...] += jnp.where(lane8 < 8, upd, 0.0)

    vec = pl.BlockSpec((1, LANE), lambda i: (0, 0))
    blk = pl.BlockSpec((tm, LANE), lambda i: (i, 0))
    return pl.pallas_call(
        body, name="gdn_gates_bwd", grid=(s // tm,),
        in_specs=[blk, blk, pl.BlockSpec((tm, LANE), lambda i: (i, OFF_AB // LANE)), blk, vec, vec, ANY],
        out_specs=[pl.BlockSpec((tm, tail), lambda i: (i, OFF_AB // tail)), pl.BlockSpec((8, LANE), lambda i: (0, 0))],
        out_shape=[jax.ShapeDtypeStruct(dp_all.shape, dp_all.dtype), jax.ShapeDtypeStruct((8, LANE), F32)],
        input_output_aliases={6: 0},
        compiler_params=_params("arbitrary"),
    )(dgb_f, dgb_r, p_pad, gb, alog_row, dt_row, dp_all)


def _chunk_masks(rev):
    row = lax.broadcasted_iota(jnp.int32, (CHUNK, CHUNK), 0)
    col = lax.broadcasted_iota(jnp.int32, (CHUNK, CHUNK), 1)
    le = (col >= row) if rev else (col <= row)
    strict = (col > row) if rev else (col < row)
    return le, strict, row == col


def _gate_lanes(rev, h):
    d = 1 if rev else 0
    return d * GDN_HEADS + h, 8 + d * GDN_HEADS + h


BNN = (((2,), (1,)), ((0,), (0,)))
BNT = (((2,), (2,)), ((0,), (0,)))
BTN = (((1,), (1,)), ((0,), (0,)))
NB = 2 * GDN_HEADS
DELTA_CHUNKS = 8


def _bdot(a, b, dn=BNN):
    return lax.dot_general(a.astype(BF16), b.astype(BF16), dn, preferred_element_type=F32)


def _dot3(a, b, dn, exact_a=False, exact_b=False):
    def d(x, y):
        return lax.dot_general(x, y, dn, preferred_element_type=F32)

    ah = a.astype(BF16)
    bh = b.astype(BF16)
    out = d(ah, bh)
    if not exact_b:
        out = out + d(ah, (b - bh.astype(F32)).astype(BF16))
    if not exact_a:
        out = out + d((a - ah.astype(F32)).astype(BF16), bh)
    return out


def _both(f_val, r_val):
    return jnp.stack([f_val] * GDN_HEADS + [r_val] * GDN_HEADS)


def _head_blocks(ref_f, ref_r, rows_f, rows_r):
    return jnp.concatenate([ref_f[:, rows_f, :], ref_r[:, rows_r, :]], axis=0)


def _chunk_rows(c):
    return slice(c * CHUNK, (c + 1) * CHUNK), slice((DELTA_CHUNKS - 1 - c) * CHUNK, (DELTA_CHUNKS - c) * CHUNK)


def _heads(ref_f, ref_r, rows_f, rows_r):
    hd = GDN_HEAD_DIM
    return jnp.stack([ref_f[rows_f, h * hd:(h + 1) * hd] for h in range(GDN_HEADS)]
                     + [ref_r[rows_r, h * hd:(h + 1) * hd] for h in range(GDN_HEADS)])


def _gate_cols(tile_f, tile_r, base):
    return jnp.stack([tile_f[:, base + h:base + h + 1] for h in range(GDN_HEADS)]
                     + [tile_r[:, base + GDN_HEADS + h:base + GDN_HEADS + h + 1] for h in range(GDN_HEADS)])


def _chunk_common2(q, k, v, gbf, gbr):
    mf, mr = _chunk_masks(False), _chunk_masks(True)
    le, strict = _both(mf[0], mr[0]), _both(mf[1], mr[1])
    eye = mf[2]
    gcm_f = _dot3(mf[0].astype(F32), gbf, NN, exact_a=True)
    gcm_r = _dot3(mr[0].astype(F32), gbr, NN, exact_a=True)
    g, beta, gc = _gate_cols(gbf, gbr, 0), _gate_cols(gbf, gbr, 8), _gate_cols(gcm_f, gcm_r, 0)
    gc_row = _dot3(jnp.ones((NB, CHUNK, CHUNK), F32), jnp.where(eye[None], gc, 0.0), BNN, exact_a=True)
    decay = jnp.where(le, jnp.exp(jnp.where(le, gc - gc_row, 0.0)), 0.0)
    eg = jnp.exp(gc)
    gl = jnp.sum(g, axis=1, keepdims=True)
    kb = k * beta
    vb = v * beta
    kbeg = kb * eg
    lm = jnp.where(strict, _bdot(kb, k, BNT) * decay, 0.0)
    intra = _bdot(q, k, BNT) * decay
    edec = jnp.exp(gl - gc)
    return dict(strict=strict, eye=eye, beta=beta, decay=decay, eg=eg, gl=gl, kb=kb, vb=vb, kbeg=kbeg,
                lm=lm, intra=intra, qg=q * eg, edec=edec, kdec=k * edec)


def _unit_triangular_inverse(lm, eye):
    x = -lm
    t = eye[None].astype(F32) + x
    p = x
    for level in range(5):
        prod = functools.partial(_dot3, dn=BNN) if level < 1 else _bdot
        p = prod(p, p)
        t = t + prod(t, p)
    return t


def _delta_fwd2(y, gb, gather=()):
    s = y.shape[1]
    nc = s // CHUNK
    hd = GDN_HEAD_DIM
    na = len(gather)

    def body(*refs):
        qf, kf, vf, gf, qr, kr, vr, gr = refs[:8]
        of_ref, or_ref, sf_all, sr_all, tf_all, tr_all = refs[8 + na:14 + na]
        state = refs[14 + 2 * na]
        step = pl.program_id(0)

        @pl.when(step == 0)
        def _():
            state[...] = jnp.zeros_like(state)

        if na:
            start, forward, finish = _gather_phases(refs[8:8 + na], refs[14 + na:14 + 2 * na], *refs[15 + 2 * na:])
            pl.when(step == 0)(start)
            pl.when(step == ns // 2)(forward)
            pl.when(step == ns - 1)(finish)

        st = state[...]
        for c in range(DELTA_CHUNKS):
            rf, rr = _chunk_rows(c)
            q, k, v = _head_blocks(qf, qr, rf, rr), _head_blocks(kf, kr, rf, rr), _head_blocks(vf, vr, rf, rr)
            cm = _chunk_common2(q, k, v, gf[rf, :], gr[rr, :])
            tinv = _unit_triangular_inverse(cm["lm"], cm["eye"])
            u = _bdot(tinv, cm["vb"])
            w = _bdot(tinv, cm["kbeg"])
            v_new = u - _bdot(w, st)
            o = _bdot(cm["qg"], st) + _bdot(cm["intra"], v_new)
            for h in range(GDN_HEADS):
                of_ref[rf, h * hd:(h + 1) * hd] = o[h]
                or_ref[rr, h * hd:(h + 1) * hd] = o[GDN_HEADS + h]
            sf_all[c] = st[:GDN_HEADS]
            sr_all[DELTA_CHUNKS - 1 - c] = st[GDN_HEADS:]
            tf_all[c] = tinv[:GDN_HEADS]
            tr_all[DELTA_CHUNKS - 1 - c] = tinv[GDN_HEADS:]
            st = st * jnp.exp(cm["gl"]) + _bdot(cm["kdec"], v_new, BTN)
        state[...] = st

    rows = DELTA_CHUNKS * CHUNK
    ns = nc // DELTA_CHUNKS

    def col(j, rev):
        return pl.BlockSpec((GDN_HEADS, rows, hd), (lambda n: (j, ns - 1 - n, 0)) if rev else (lambda n: (j, n, 0)))

    def out(rev):
        return pl.BlockSpec((rows, GDN_WIDTH), (lambda n: (ns - 1 - n, 0)) if rev else (lambda n: (n, 0)))

    def gate(rev):
        return pl.BlockSpec((rows, LANE), (lambda n: (ns - 1 - n, 0)) if rev else (lambda n: (n, 0)))

    def per_chunk(d1, d2, rev):
        return pl.BlockSpec((DELTA_CHUNKS, GDN_HEADS, d1, d2),
                            (lambda n: (ns - 1 - n, 0, 0, 0)) if rev else (lambda n: (n, 0, 0, 0)))

    assert nc % DELTA_CHUNKS == 0 and (na == 0 or ns >= 4)
    res = pl.pallas_call(
        body, name="delta_fwd", grid=(ns,),
        in_specs=[col(0, False), col(1, False), col(2, False), gate(False), col(0, True), col(1, True), col(2, True), gate(True)]
        + [ANY] * na,
        out_specs=[out(False), out(True), per_chunk(hd, hd, False), per_chunk(hd, hd, True),
                   per_chunk(CHUNK, CHUNK, False), per_chunk(CHUNK, CHUNK, True)] + [ANY] * na,
        out_shape=[jax.ShapeDtypeStruct((s, GDN_WIDTH), F32)] * 2 + [jax.ShapeDtypeStruct((nc, GDN_HEADS, hd, hd), F32)] * 2
        + [jax.ShapeDtypeStruct((nc, GDN_HEADS, CHUNK, CHUNK), F32)] * 2
        + [jax.ShapeDtypeStruct((N_DEV,) + v.shape, v.dtype) for v in gather],
        scratch_shapes=[pltpu.VMEM((NB, hd, hd), F32)] + (_gather_semaphores(na) if na else []),
        compiler_params=_params("arbitrary"),
    )(y, y, y, gb, y, y, y, gb, *gather)
    return res[:6], res[6:]


def _delta_bwd2(y, gb, do, sf_all, sr_all, tf_all, tr_all, scatter=()):
    s = y.shape[1]
    nc = s // CHUNK
    hd = GDN_HEAD_DIM
    na = len(scatter)

    def body(*refs):
        qf, kf, vf, gf, dof, sf, tf, qr, kr, vr, gr, dor, sr, tr = refs[:14]
        dyf_ref, dyr_ref, dgf_ref, dgr_ref = refs[14 + na:18 + na]
        dstate = refs[18 + 2 * na]
        step = pl.program_id(0)

        @pl.when(step == 0)
        def _():
            dstate[...] = jnp.zeros_like(dstate)

        if na:
            start, finish = _scatter_phases(refs[14:14 + na], refs[18 + na:18 + 2 * na], *refs[19 + 2 * na:])
            pl.when(step == 0)(start)
            pl.when(step == ns - 1)(finish)

        def one_chunk(c, ds_out):
            rr, rf = _chunk_rows(c)
            cf, cr = DELTA_CHUNKS - 1 - c, c
            q, k, v = _head_blocks(qf, qr, rf, rr), _head_blocks(kf, kr, rf, rr), _head_blocks(vf, vr, rf, rr)
            dov = _heads(dof, dor, rf, rr)
            cm = _chunk_common2(q, k, v, gf[rf, :], gr[rr, :])
            tinv = jnp.concatenate([tf[cf], tr[cr]], axis=0)
            st = jnp.concatenate([sf[cf], sr[cr]], axis=0)
            decay, lm, intra, qg, kdec, kbeg, eg, kb, beta = (
                cm[n] for n in ("decay", "lm", "intra", "qg", "kdec", "kbeg", "eg", "kb", "beta"))
            u = _bdot(tinv, cm["vb"])
            w = _bdot(tinv, kbeg)
            v_new = u - _bdot(w, st)
            egl = jnp.exp(cm["gl"])
            d_qg = _bdot(dov, st, BNT)
            d_intra = _bdot(dov, v_new, BNT)
            dv_new = _bdot(intra, dov, BTN) + _bdot(kdec, ds_out)
            d_kdec = _bdot(v_new, ds_out, BNT)
            ds_in = _bdot(qg, dov, BTN) + egl * ds_out - _bdot(w, dv_new, BTN)
            dgl = egl * jnp.sum(jnp.sum(st * ds_out, axis=2, keepdims=True), axis=1, keepdims=True)
            dw = -_bdot(dv_new, st, BNT)
            dvb = _bdot(tinv, dv_new, BTN)
            dkbeg = _bdot(tinv, dw, BTN)
            dlm = jnp.where(cm["strict"], -(_bdot(dvb, u, BNT) + _bdot(dkbeg, w, BNT)), 0.0)
            d_a = dlm * decay
            d_qk = d_intra * decay
            e = dlm * lm + d_intra * intra
            colsum = _dot3(e, jnp.ones((NB, CHUNK, LANE), F32), BTN, exact_b=True)[:, :, 0:1]
            dgc = jnp.sum(e, axis=2, keepdims=True) - colsum
            dkb = _bdot(d_a, k) + dkbeg * eg
            dk = _bdot(d_a, kb, BTN) + _bdot(d_qk, q, BTN)
            dq = _bdot(d_qk, k) + d_qg * eg
            dgc = dgc + jnp.sum(d_qg * qg, axis=2, keepdims=True) + jnp.sum(dkbeg * kbeg, axis=2, keepdims=True)
            tdec = jnp.sum(d_kdec * kdec, axis=2, keepdims=True)
            dk = dk + d_kdec * cm["edec"] + dkb * beta
            dgc = dgc - tdec
            dgl = dgl + jnp.sum(tdec, axis=1, keepdims=True)
            dbeta = jnp.sum(dvb * v, axis=2, keepdims=True) + jnp.sum(dkb * k, axis=2, keepdims=True)
            dv = dvb * beta
            lane = lax.broadcasted_iota(jnp.int32, (CHUNK, LANE), 1)
            for rev, dy_ref, dg_ref, rows in ((False, dyf_ref, dgf_ref, rf), (True, dyr_ref, dgr_ref, rr)):
                dgc_tile = jnp.zeros((CHUNK, LANE), F32)
                rest = jnp.zeros((CHUNK, LANE), F32)
                for h in range(GDN_HEADS):
                    b = (GDN_HEADS if rev else 0) + h
                    gi, bi = _gate_lanes(rev, h)
                    dgc_tile = dgc_tile + jnp.where(lane == gi, dgc[b], 0.0)
                    rest = rest + jnp.where(lane == gi, dgl[b], 0.0) + jnp.where(lane == bi, dbeta[b], 0.0)
                    dy_ref[h, rows, :] = dq[b]
                    dy_ref[GDN_HEADS + h, rows, :] = dk[b]
                    dy_ref[2 * GDN_HEADS + h, rows, :] = dv[b]
                le_t = _chunk_masks(not rev)[0].astype(F32)
                dg_ref[rows, :] = _dot3(le_t, dgc_tile, NN, exact_a=True) + rest
            return ds_in

        ds = dstate[...]
        for c in range(DELTA_CHUNKS):
            ds = one_chunk(c, ds)
        dstate[...] = ds

    rows_per_step = DELTA_CHUNKS * CHUNK
    ns = nc // DELTA_CHUNKS

    def col(j, rev, blocks=GDN_HEADS):
        return pl.BlockSpec((blocks, rows_per_step, hd), (lambda n: (j, n, 0)) if rev else (lambda n: (j, ns - 1 - n, 0)))

    def wide(width, rev):
        return pl.BlockSpec((rows_per_step, width), (lambda n: (n, 0)) if rev else (lambda n: (ns - 1 - n, 0)))

    def per_chunk(d1, d2, rev):
        return pl.BlockSpec((DELTA_CHUNKS, GDN_HEADS, d1, d2),
                            (lambda n: (n, 0, 0, 0)) if rev else (lambda n: (ns - 1 - n, 0, 0, 0)))

    def side(rev):
        return [col(0, rev), col(1, rev), col(2, rev), wide(LANE, rev), wide(GDN_WIDTH, rev), per_chunk(hd, hd, rev),
                per_chunk(CHUNK, CHUNK, rev)]

    assert nc % DELTA_CHUNKS == 0 and (na == 0 or ns >= 2)
    res = pl.pallas_call(
        body, name="delta_bwd", grid=(ns,),
        in_specs=side(False) + side(True) + [ANY] * na,
        out_specs=[col(0, False, 3 * GDN_HEADS), col(0, True, 3 * GDN_HEADS), wide(LANE, False), wide(LANE, True)]
        + [ANY] * na,
        out_shape=[jax.ShapeDtypeStruct((3 * GDN_HEADS, s, hd), F32)] * 2 + [jax.ShapeDtypeStruct((s, LANE), F32)] * 2
        + [jax.ShapeDtypeStruct(g.shape, g.dtype) for g in scatter],
        scratch_shapes=[pltpu.VMEM((NB, hd, hd), F32)] + (_gather_semaphores(na) if na else []),
        compiler_params=_params("arbitrary"),
    )(y, y, y, gb, do, sf_all, tf_all, y, y, y, gb, do, sr_all, tr_all, *scatter)
    return res[:4], res[4:]


def _gdn_post_fwd(o_f, o_r, p_pad, norm_row):
    s = o_f.shape[0]
    tm = min(512, s)
    hd = GDN_HEAD_DIM

    def body(of_ref, or_ref, z_ref, w_ref, out_ref, osum_ref):
        o = of_ref[...] + or_ref[...]
        osum_ref[...] = o
        z = z_ref[...]
        gate = z * _sigmoid(z)
        for h in range(GDN_HEADS):
            sl = slice(h * hd, (h + 1) * hd)
            oh = o[:, sl]
            r = lax.rsqrt(jnp.mean(oh * oh, axis=-1, keepdims=True) + EPS)
            out_ref[:, sl] = (oh * r * w_ref[...] * gate[:, sl]).astype(BF16)

    blk = pl.BlockSpec((tm, GDN_WIDTH), lambda i: (i, 0))
    return pl.pallas_call(
        body, name="gdn_post_fwd", grid=(s // tm,),
        in_specs=[blk, blk, pl.BlockSpec((tm, GDN_WIDTH), lambda i: (i, OFF_Z // GDN_WIDTH)),
                  pl.BlockSpec((1, hd), lambda i: (0, 0))],
        out_specs=[blk, blk],
        out_shape=[jax.ShapeDtypeStruct((s, GDN_WIDTH), BF16), jax.ShapeDtypeStruct((s, GDN_WIDTH), F32)],
        compiler_params=_params("parallel"),
    )(o_f, o_r, p_pad, norm_row)


def _gdn_post_bwd(d_out, o_sum, p_pad, norm_row):
    s = o_sum.shape[0]
    tm = min(512, s)
    hd = GDN_HEAD_DIM

    def body(d_ref, o_ref, z_ref, w_ref, do_ref, dz_ref, dw_ref):
        @pl.when(pl.program_id(0) == 0)
        def _():
            dw_ref[...] = jnp.zeros_like(dw_ref)

        z = z_ref[...]
        sg = _sigmoid(z)
        gate = z * sg
        dgate = sg * (1.0 + z * (1.0 - sg))
        wv = w_ref[...]
        dw = jnp.zeros((1, hd), F32)
        for h in range(GDN_HEADS):
            sl = slice(h * hd, (h + 1) * hd)
            oh = o_ref[:, sl]
            dh = d_ref[:, sl]
            r = lax.rsqrt(jnp.mean(oh * oh, axis=-1, keepdims=True) + EPS)
            ohat = oh * r
            dz_ref[:, sl] = (dh * ohat * wv * dgate[:, sl]).astype(BF16)
            drn = dh * gate[:, sl]
            t = drn * wv
            do_ref[:, sl] = r * (t - ohat * jnp.mean(t * ohat, axis=-1, keepdims=True))
            dw = dw + jnp.sum(drn * ohat, axis=0, keepdims=True)
        dw_ref[...] += dw

    blk = pl.BlockSpec((tm, GDN_WIDTH), lambda i: (i, 0))
    vec = pl.BlockSpec((1, hd), lambda i: (0, 0))
    return pl.pallas_call(
        body, name="gdn_post_bwd", grid=(s // tm,),
        in_specs=[blk, blk, pl.BlockSpec((tm, GDN_WIDTH), lambda i: (i, OFF_Z // GDN_WIDTH)), vec],
        out_specs=[blk, pl.BlockSpec((tm, GDN_WIDTH), lambda i: (i, OFF_Z // GDN_WIDTH)), vec],
        out_shape=[jax.ShapeDtypeStruct((s, GDN_WIDTH), F32), jax.ShapeDtypeStruct((s, N_PAD), BF16),
                   jax.ShapeDtypeStruct((1, hd), F32)],
        compiler_params=_params("arbitrary"),
    )(d_out, o_sum, p_pad, norm_row)


def _gdn_forward(p_pad, conv_wt, alog_row, dt_row, norm_row, gather=()):
    c_pre, y = _gdn_conv_fwd(p_pad, conv_wt)
    gb = _gdn_gates_fwd(p_pad, alog_row, dt_row)
    (o_f, o_r, s_f, s_r, t_f, t_r), gathered = _delta_fwd2(y, gb, gather)
    out, o_sum = _gdn_post_fwd(o_f, o_r, p_pad, norm_row)
    return out, (c_pre, y, gb, s_f, t_f, s_r, t_r, o_sum), gathered


def _gdn_backward(d_out, p_pad, conv_wt, alog_row, dt_row, norm_row, saved, scatter=()):
    c_pre, y, gb, s_f, t_f, s_r, t_r, o_sum = saved
    do, dp_all, dnorm = _gdn_post_bwd(d_out, o_sum, p_pad, norm_row)
    (dy_f, dy_r, dgb_f, dgb_r), received = _delta_bwd2(y, gb, do, s_f, s_r, t_f, t_r, scatter)
    dp_all, dconv = _gdn_conv_bwd(dy_f, dy_r, c_pre, p_pad, conv_wt, dp_all)
    dp_all, gate_sums = _gdn_gates_bwd(dgb_f, dgb_r, p_pad, gb, alog_row, dt_row, dp_all)
    return dp_all, dconv, gate_sums, dnorm, received


ATT_BK = ATT_BQ + 2 * ATT_HALO
ATT_SUB = 8
SWA_SCALE = SWA_HEAD_DIM ** -0.5


def _t5_bucket(rel):
    nb = REL_BUCKETS // 2
    bucket = (rel > 0).astype(np.int32) * nb
    n = np.abs(rel)
    max_exact = nb // 2
    large = max_exact + (np.log(np.maximum(n, 1) / max_exact)
                         / math.log(REL_MAX_DISTANCE / max_exact) * (nb - max_exact)).astype(np.int32)
    large = np.minimum(large, nb - 1)
    return (bucket + np.where(n < max_exact, n, large)).astype(np.int32)


def _band_tables(dilation, queries_are_rows_of_block):
    blk = np.arange(ATT_BQ)
    band = np.arange(ATT_BK) - ATT_HALO
    if queries_are_rows_of_block:
        rel = band[None, :] - blk[:, None]
        band_idx = np.broadcast_to(np.arange(ATT_BK)[None, :], rel.shape)
    else:
        rel = blk[None, :] - band[:, None]
        band_idx = np.broadcast_to(np.arange(ATT_BK)[:, None], rel.shape)
    base = np.abs(rel) <= ATT_HALO
    not_prev = band_idx >= ATT_HALO
    not_next = band_idx < ATT_HALO + ATT_BQ
    valid = np.stack([base & not_prev, base, base & not_next, base & not_prev & not_next])
    return valid, _t5_bucket(rel * dilation)


def _bias_tiles(rel_bias, dilation, queries_are_rows_of_block):
    valid, bucket = _band_tables(dilation, queries_are_rows_of_block)
    onehot = (jnp.asarray(bucket.reshape(-1, 1)) == jnp.arange(REL_BUCKETS, dtype=jnp.int32)[None, :]).astype(F32)
    rb = jnp.dot(onehot, rel_bias.astype(F32), precision=lax.Precision.HIGHEST)
    rb = rb.T.reshape((SWA_HEADS,) + bucket.shape)
    return jnp.where(valid[:, None], rb[None], NEG_BIG).astype(F32)


def _group_sum(x, bd):
    hi = x.astype(BF16)
    lo = (x - hi.astype(F32)).astype(BF16)
    return jnp.dot(hi, bd, preferred_element_type=F32) + jnp.dot(lo, bd, preferred_element_type=F32)


def _head_block_diag():
    idx = np.arange(SWA_WIDTH) // SWA_HEAD_DIM
    return jnp.asarray(idx[:, None] == idx[None, :], BF16)


def _swa_pre_fwd(p_pad, qw_row, kw_row, bd):
    s = p_pad.shape[0]
    tm = min(512, s)
    inv = 1.0 / SWA_HEAD_DIM

    def body(q_ref, k_ref, v_ref, qw_ref, kw_ref, bd_ref, qo_ref, ko_ref, vo_ref):
        bdv = bd_ref[...]
        q = q_ref[...]
        k = k_ref[...]
        rq = lax.rsqrt(_group_sum(q * q, bdv) * inv + EPS)
        rk = lax.rsqrt(_group_sum(k * k, bdv) * inv + EPS)
        qo_ref[...] = (q * rq * qw_ref[...] * SWA_SCALE).astype(BF16)
        ko_ref[...] = (k * rk * kw_ref[...]).astype(BF16)
        vo_ref[...] = v_ref[...].astype(BF16)

    base = OFF_B // SWA_WIDTH
    blk = pl.BlockSpec((tm, SWA_WIDTH), lambda i: (i, 0))
    vec = pl.BlockSpec((1, SWA_WIDTH), lambda i: (0, 0))
    return pl.pallas_call(
        body, name="swa_pre_fwd", grid=(s // tm,),
        in_specs=[pl.BlockSpec((tm, SWA_WIDTH), lambda i: (i, base)), pl.BlockSpec((tm, SWA_WIDTH), lambda i: (i, base + 1)),
                  pl.BlockSpec((tm, SWA_WIDTH), lambda i: (i, base + 2)), vec, vec,
                  pl.BlockSpec((SWA_WIDTH, SWA_WIDTH), lambda i: (0, 0))],
        out_specs=[blk, blk, blk],
        out_shape=[jax.ShapeDtypeStruct((s, SWA_WIDTH), BF16)] * 3,
        compiler_params=_params("parallel"),
    )(p_pad, p_pad, p_pad, qw_row, kw_row, bd)


def _swa_pre_bwd(dqs, dks, dvs, p_pad, qw_row, kw_row, bd, dp_all):
    s = p_pad.shape[0]
    tm = min(256, s)
    inv = 1.0 / SWA_HEAD_DIM
    npat = len(dqs)

    def body(*refs):
        dq_refs, dk_refs, dv_refs = refs[:npat], refs[npat:2 * npat], refs[2 * npat:3 * npat]
        q_ref, k_ref, qw_ref, kw_ref, bd_ref, _, dp_ref, dqw_ref, dkw_ref = refs[3 * npat:]

        @pl.when(pl.program_id(0) == 0)
        def _():
            dqw_ref[...] = jnp.zeros_like(dqw_ref)
            dkw_ref[...] = jnp.zeros_like(dkw_ref)

        bdv = bd_ref[...]

        def norm_bwd(x, g, w, scale):
            r = lax.rsqrt(_group_sum(x * x, bdv) * inv + EPS)
            xhat = x * r
            t = g * w * scale
            dx = r * (t - xhat * (_group_sum(t * xhat, bdv) * inv))
            return dx, jnp.sum(g * scale * xhat, axis=0, keepdims=True)

        def total(rs):
            t = rs[0][...].astype(F32)
            for r in rs[1:]:
                t = t + r[...].astype(F32)
            return t

        dq, dqw = norm_bwd(q_ref[...], total(dq_refs), qw_ref[...], SWA_SCALE)
        dk, dkw = norm_bwd(k_ref[...], total(dk_refs), kw_ref[...], 1.0)
        dp_ref[:, 0:SWA_WIDTH] = dq.astype(BF16)
        dp_ref[:, SWA_WIDTH:2 * SWA_WIDTH] = dk.astype(BF16)
        dp_ref[:, 2 * SWA_WIDTH:3 * SWA_WIDTH] = total(dv_refs).astype(BF16)
        dqw_ref[...] += dqw
        dkw_ref[...] += dkw

    base = OFF_B // SWA_WIDTH
    blk = pl.BlockSpec((tm, SWA_WIDTH), lambda i: (i, 0))
    vec = pl.BlockSpec((1, SWA_WIDTH), lambda i: (0, 0))
    return pl.pallas_call(
        body, name="swa_pre_bwd", grid=(s // tm,),
        in_specs=[blk] * (3 * npat) + [pl.BlockSpec((tm, SWA_WIDTH), lambda i: (i, base)),
                                      pl.BlockSpec((tm, SWA_WIDTH), lambda i: (i, base + 1)), vec, vec,
                                      pl.BlockSpec((SWA_WIDTH, SWA_WIDTH), lambda i: (0, 0)), ANY],
        out_specs=[pl.BlockSpec((tm, 3 * SWA_WIDTH), lambda i: (i, OFF_B // (3 * SWA_WIDTH))), vec, vec],
        out_shape=[jax.ShapeDtypeStruct(dp_all.shape, dp_all.dtype), jax.ShapeDtypeStruct((1, SWA_WIDTH), F32),
                   jax.ShapeDtypeStruct((1, SWA_WIDTH), F32)],
        input_output_aliases={3 * npat + 5: 0},
        compiler_params=_params("arbitrary"),
    )(*dqs, *dks, *dvs, p_pad, p_pad, qw_row, kw_row, bd, dp_all)


def _band_specs(length, rows):
    per = rows // ATT_HALO
    last = length // ATT_HALO - 1
    prev = pl.BlockSpec((ATT_HALO, SWA_WIDTH), lambda r, t: (jnp.maximum(t * per - 1, 0), r))
    cur = pl.BlockSpec((rows, SWA_WIDTH), lambda r, t: (t, r))
    nxt = pl.BlockSpec((ATT_HALO, SWA_WIDTH), lambda r, t: (jnp.minimum((t + 1) * per, last), r))
    return [prev, cur, nxt]


def _tile_variant(t, nb, u, sub):
    first, last = u == 0, u == sub - 1
    if first and last:
        return 3 if nb == 1 else jnp.where(t == 0, 0, jnp.where(t == nb - 1, 2, 1))
    if first:
        return jnp.where(t == 0, 0, 1)
    if last:
        return jnp.where(t == nb - 1, 2, 1)
    return 1


def _bias_specs(nb, sub, rows, cols):
    return [pl.BlockSpec((1, SWA_HEADS, rows, cols),
                         functools.partial(lambda r, t, u: (_tile_variant(t, nb, u, sub), 0, 0, 0), u=u))
            for u in range(sub)]


def _band(refs):
    return jnp.concatenate([r[...] for r in refs], axis=0)


def _sub(u, width=ATT_BQ):
    return slice(u * ATT_BQ, u * ATT_BQ + width)


N_PAIRS = SWA_HEADS // 2


def _pairs(x):
    return jnp.stack([x[:, LANE * p:LANE * (p + 1)] for p in range(N_PAIRS)])


def _per_head_rows(x):
    first = lax.broadcasted_iota(jnp.int32, x.shape, 2) < SWA_HEAD_DIM
    zero = jnp.zeros_like(x)
    return jnp.concatenate([jnp.where(first, x, zero), jnp.where(first, zero, x)], axis=1)


def _per_head_cols(x):
    return jnp.stack([jnp.concatenate([x[:, LANE * p:LANE * p + 1],
                                       x[:, LANE * p + SWA_HEAD_DIM:LANE * p + SWA_HEAD_DIM + 1]], axis=0)
                      for p in range(N_PAIRS)])


def _merge_heads(x, rows):
    first = lax.broadcasted_iota(jnp.int32, (N_PAIRS, rows, LANE), 2) < SWA_HEAD_DIM
    return jnp.where(first, x[:, :rows], x[:, rows:])


def _store_pairs(ref, x, rows):
    for p in range(N_PAIRS):
        ref[rows, LANE * p:LANE * (p + 1)] = x[p].astype(ref.dtype)


def _att_fwd2(q, k, v, bias, dilation):
    s = q.shape[0]
    length = s // dilation
    sub = min(ATT_SUB, length // ATT_BQ)
    rows = sub * ATT_BQ
    nb = length // rows
    view = (length, dilation * SWA_WIDTH)

    def body(q_ref, kp, kc, kn, vp, vc, vn, *rest):
        b_refs, (o_ref, lse_ref) = rest[:sub], rest[sub:]
        kwin, vwin = _band((kp, kc, kn)), _band((vp, vc, vn))
        for u in range(sub):
            kb, vb = _pairs(kwin[_sub(u, ATT_BK)]), _pairs(vwin[_sub(u, ATT_BK)])
            qm = _per_head_rows(_pairs(q_ref[_sub(u), :]))
            sc = _bdot(qm, kb, BNT) + b_refs[u][0].reshape(N_PAIRS, 2 * ATT_BQ, ATT_BK)
            m = jnp.max(sc, axis=-1, keepdims=True)
            p = jnp.exp(sc - m)
            den = jnp.sum(p, axis=-1, keepdims=True)
            o = _bdot(p, vb) / den
            _store_pairs(o_ref, _merge_heads(o, ATT_BQ), _sub(u))
            lse = jnp.broadcast_to(m + jnp.log(den), (N_PAIRS, 2 * ATT_BQ, LANE))
            _store_pairs(lse_ref, _merge_heads(lse, ATT_BQ), _sub(u))

    cur = pl.BlockSpec((rows, SWA_WIDTH), lambda r, t: (t, r))
    o, lse = pl.pallas_call(
        body, name=f"att_fwd_d{dilation}", grid=(dilation, nb),
        in_specs=[cur] + _band_specs(length, rows) * 2 + _bias_specs(nb, sub,ATT_BQ, ATT_BK),
        out_specs=[cur, cur],
        out_shape=[jax.ShapeDtypeStruct(view, BF16), jax.ShapeDtypeStruct(view, F32)],
        compiler_params=_params("parallel", "parallel"),
    )(q.reshape(view), *([k.reshape(view)] * 3), *([v.reshape(view)] * 3), *([bias] * sub))
    return o.reshape(s, SWA_WIDTH), lse.reshape(s, SWA_WIDTH)


def _att_dq2(q, k, v, dop, lse, cp, bias, dilation):
    s = q.shape[0]
    length = s // dilation
    sub = min(ATT_SUB, length // ATT_BQ)
    rows = sub * ATT_BQ
    nb = length // rows
    view = (length, dilation * SWA_WIDTH)

    def body(q_ref, kp, kc, kn, vp, vc, vn, do_ref, lse_ref, cp_ref, *rest):
        b_refs, (dq_ref, db_ref) = rest[:sub], rest[sub:]

        @pl.when((pl.program_id(0) == 0) & (pl.program_id(1) == 0))
        def _():
            db_ref[...] = jnp.zeros_like(db_ref)

        kwin, vwin = _band((kp, kc, kn)), _band((vp, vc, vn))
        for u in range(sub):
            kb, vb = _pairs(kwin[_sub(u, ATT_BK)]), _pairs(vwin[_sub(u, ATT_BK)])
            qm = _per_head_rows(_pairs(q_ref[_sub(u), :]))
            dom = _per_head_rows(_pairs(do_ref[_sub(u), :]))
            sc = _bdot(qm, kb, BNT) + b_refs[u][0].reshape(N_PAIRS, 2 * ATT_BQ, ATT_BK)
            p = jnp.exp(sc - _per_head_cols(lse_ref[_sub(u), :]))
            ds = p * (_bdot(dom, vb, BNT) + _per_head_cols(cp_ref[_sub(u), :]))
            _store_pairs(dq_ref, _merge_heads(_bdot(ds, kb), ATT_BQ), _sub(u))
            db_ref[_tile_variant(pl.program_id(1), nb, u, sub)] += ds.reshape(SWA_HEADS, ATT_BQ, ATT_BK)

    cur = pl.BlockSpec((rows, SWA_WIDTH), lambda r, t: (t, r))
    dq, db = pl.pallas_call(
        body, name=f"att_dq_d{dilation}", grid=(dilation, nb),
        in_specs=[cur] + _band_specs(length, rows) * 2 + [cur, cur, cur] + _bias_specs(nb, sub,ATT_BQ, ATT_BK),
        out_specs=[cur, pl.BlockSpec((4, SWA_HEADS, ATT_BQ, ATT_BK), lambda r, t: (0, 0, 0, 0))],
        out_shape=[jax.ShapeDtypeStruct(view, BF16), jax.ShapeDtypeStruct((4, SWA_HEADS, ATT_BQ, ATT_BK), F32)],
        compiler_params=_params("arbitrary", "arbitrary"),
    )(q.reshape(view), *([k.reshape(view)] * 3), *([v.reshape(view)] * 3), dop.reshape(view), lse.reshape(view),
      cp.reshape(view), *([bias] * sub))
    return dq.reshape(s, SWA_WIDTH), db


def _att_dkv2(q, k, v, dop, lse, cp, bias_t, dilation):
    s = q.shape[0]
    length = s // dilation
    sub = min(ATT_SUB, length // ATT_BQ)
    rows = sub * ATT_BQ
    nb = length // rows
    view = (length, dilation * SWA_WIDTH)

    def body(k_ref, v_ref, qp, qc, qn, dp_, dc_, dn_, lp, lc, ln, cp_, cc_, cn_, *rest):
        b_refs, (dk_ref, dv_ref) = rest[:sub], rest[sub:]
        qwin, dowin = _band((qp, qc, qn)), _band((dp_, dc_, dn_))
        lsewin, cpwin = _band((lp, lc, ln)), _band((cp_, cc_, cn_))
        for u in range(sub):
            band = _sub(u, ATT_BK)
            qm = _per_head_rows(_pairs(qwin[band]))
            dom = _per_head_rows(_pairs(dowin[band]))
            kv, vv = _pairs(k_ref[_sub(u), :]), _pairs(v_ref[_sub(u), :])
            sc = _bdot(qm, kv, BNT) + b_refs[u][0].reshape(N_PAIRS, 2 * ATT_BK, ATT_BQ)
            p = jnp.exp(sc - _per_head_cols(lsewin[band]))
            _store_pairs(dv_ref, _bdot(p, dom, BTN), _sub(u))
            ds = p * (_bdot(dom, vv, BNT) + _per_head_cols(cpwin[band]))
            _store_pairs(dk_ref, _bdot(ds, qm, BTN), _sub(u))

    cur = pl.BlockSpec((rows, SWA_WIDTH), lambda r, t: (t, r))
    dk, dv = pl.pallas_call(
        body, name=f"att_dkv_d{dilation}", grid=(dilation, nb),
        in_specs=[cur, cur] + _band_specs(length, rows) * 4 + _bias_specs(nb, sub,ATT_BK, ATT_BQ),
        out_specs=[cur, cur],
        out_shape=[jax.ShapeDtypeStruct(view, BF16)] * 2,
        compiler_params=_params("parallel", "parallel"),
    )(k.reshape(view), v.reshape(view), *([q.reshape(view)] * 3), *([dop.reshape(view)] * 3),
      *([lse.reshape(view)] * 3), *([cp.reshape(view)] * 3), *([bias_t] * sub))
    return dk.reshape(s, SWA_WIDTH), dv.reshape(s, SWA_WIDTH)


def _pattern_weights(lses):
    m = lses[0]
    for l in lses[1:]:
        m = jnp.maximum(m, l)
    es = [jnp.exp(l - m) for l in lses]
    den = es[0]
    for e in es[1:]:
        den = den + e
    return [e / den for e in es]


def _combine_fwd(outs, lses):
    s = outs[0].shape[0]
    tm = min(512, s)
    npat = len(outs)

    def body(*refs):
        ws = _pattern_weights([r[...] for r in refs[npat:2 * npat]])
        o = ws[0] * refs[0][...]
        for p in range(1, npat):
            o = o + ws[p] * refs[p][...]
        refs[2 * npat][...] = o.astype(BF16)

    blk = pl.BlockSpec((tm, SWA_WIDTH), lambda i: (i, 0))
    return pl.pallas_call(
        body, name="swa_combine_fwd", grid=(s // tm,), in_specs=[blk] * (2 * npat), out_specs=blk,
        out_shape=jax.ShapeDtypeStruct((s, SWA_WIDTH), BF16), compiler_params=_params("parallel"),
    )(*outs, *lses)


def _combine_bwd(d_out, outs, lses, bd):
    s = d_out.shape[0]
    tm = min(512, s)
    npat = len(outs)

    def body(*refs):
        d_ref, bd_ref = refs[0], refs[1 + 2 * npat]
        o_refs, l_refs = refs[1:1 + npat], refs[1 + npat:1 + 2 * npat]
        out_refs = refs[2 + 2 * npat:]
        ws = _pattern_weights([r[...] for r in l_refs])
        dov = d_ref[...]
        o = ws[0] * o_refs[0][...]
        for p in range(1, npat):
            o = o + ws[p] * o_refs[p][...]
        rd = _group_sum(dov * o, bd_ref[...])
        for p in range(npat):
            out_refs[p][...] = (ws[p] * dov).astype(BF16)
            out_refs[npat + p][...] = -ws[p] * rd

    blk = pl.BlockSpec((tm, SWA_WIDTH), lambda i: (i, 0))
    res = pl.pallas_call(
        body, name="swa_combine_bwd", grid=(s // tm,),
        in_specs=[blk] * (1 + 2 * npat) + [pl.BlockSpec((SWA_WIDTH, SWA_WIDTH), lambda i: (0, 0))],
        out_specs=[blk] * (2 * npat),
        out_shape=[jax.ShapeDtypeStruct((s, SWA_WIDTH), BF16)] * npat + [jax.ShapeDtypeStruct((s, SWA_WIDTH), F32)] * npat,
        compiler_params=_params("parallel"),
    )(d_out, *outs, *lses, bd)
    return res[:npat], res[npat:]


def _rel_bias_grad(dbs, buckets):
    npat = len(dbs)

    def body(*refs):
        db_refs, bk_refs, o_ref = refs[:npat], refs[npat:2 * npat], refs[2 * npat]
        row = lax.broadcasted_iota(jnp.int32, (REL_BUCKETS, LANE), 0)
        lane = lax.broadcasted_iota(jnp.int32, (REL_BUCKETS, LANE), 1)
        tiles = [[db_refs[p][0, h] + db_refs[p][1, h] + db_refs[p][2, h] + db_refs[p][3, h] for h in range(SWA_HEADS)]
                 for p in range(npat)]
        bks = [r[...] for r in bk_refs]

        def one_bucket(b, acc):
            for h in range(SWA_HEADS):
                tot = jnp.zeros((1, 1), F32)
                for p in range(npat):
                    sel = jnp.where(bks[p] == b, tiles[p][h], 0.0)
                    tot = tot + jnp.sum(jnp.sum(sel, axis=1, keepdims=True), axis=0, keepdims=True)
                acc = acc + jnp.where((row == b) & (lane == h), tot, 0.0)
            return acc

        o_ref[...] = lax.fori_loop(0, REL_BUCKETS, one_bucket, jnp.zeros((REL_BUCKETS, LANE), F32))

    full4 = pl.BlockSpec((4, SWA_HEADS, ATT_BQ, ATT_BK), lambda: (0, 0, 0, 0))
    full2 = pl.BlockSpec((ATT_BQ, ATT_BK), lambda: (0, 0))
    return pl.pallas_call(
        body, name="rel_bias_grad", in_specs=[full4] * npat + [full2] * npat,
        out_specs=pl.BlockSpec((REL_BUCKETS, LANE), lambda: (0, 0)),
        out_shape=jax.ShapeDtypeStruct((REL_BUCKETS, LANE), F32),
        compiler_params=pltpu.CompilerParams(vmem_limit_bytes=V7X_VMEM_LIMIT_BYTES),
    )(*dbs, *buckets)


def _swa_forward(p_pad, qw_row, kw_row, rel_bias, bd):
    q, k, v = _swa_pre_fwd(p_pad, qw_row, kw_row, bd)
    outs, lses = [], []
    for _, dil in DILATION_PATTERNS:
        o, lse = _att_fwd2(q, k, v, _bias_tiles(rel_bias, dil, True), dil)
        outs.append(o)
        lses.append(lse)
    return _combine_fwd(outs, lses), (q, k, v, outs, lses)


def _swa_backward(d_out, p_pad, qw_row, kw_row, rel_bias, bd, saved, dp_all):
    q, k, v, outs, lses = saved
    dops, cps = _combine_bwd(d_out, outs, lses, bd)
    dqs, dks, dvs, dbs, buckets = [], [], [], [], []
    for p, (_, dil) in enumerate(DILATION_PATTERNS):
        dq, db = _att_dq2(q, k, v, dops[p], lses[p], cps[p], _bias_tiles(rel_bias, dil, True), dil)
        dk, dv = _att_dkv2(q, k, v, dops[p], lses[p], cps[p], _bias_tiles(rel_bias, dil, False), dil)
        dqs.append(dq)
        dks.append(dk)
        dvs.append(dv)
        dbs.append(db)
        buckets.append(jnp.asarray(_band_tables(dil, True)[1]))
    dp, dqw, dkw = _swa_pre_bwd(dqs, dks, dvs, p_pad, qw_row, kw_row, bd, dp_all)
    return dp, dqw, dkw, _rel_bias_grad(dbs, buckets)


def _lane_row(v):
    flat = v.reshape(-1).astype(F32)
    return jnp.zeros((1, LANE), F32).at[0, :flat.shape[0]].set(flat)


W_IN_SHARD = N_IN // N_DEV
W_IN_RUNS = ((0, NAT_Z, 0), (NAT_Z, NAT_AB, OFF_Z), (NAT_AB, NAT_B, OFF_AB), (NAT_B, N_IN, OFF_B))


def _w_in_pieces(shard):
    lo, hi = shard * W_IN_SHARD, (shard + 1) * W_IN_SHARD
    out = []
    for first, last, dst in W_IN_RUNS:
        a, b = max(lo, first), min(hi, last)
        if a < b:
            out.append((a - lo, b - a, dst + a - first))
    return out


def _w_in_from_slabs(w3):
    nd, r, _ = w3.shape

    def body(w_ref, o_ref):
        o_ref[:, OFF_AB:N_PAD] = jnp.zeros((r, N_PAD - OFF_AB), w3.dtype)
        for sh in range(nd):
            for src, length, dst in _w_in_pieces(sh):
                o_ref[:, dst:dst + length] = w_ref[sh, :, src:src + length]

    return pl.pallas_call(
        body, name="w_in_from_slabs", out_shape=jax.ShapeDtypeStruct((r, N_PAD), w3.dtype),
        compiler_params=pltpu.CompilerParams(vmem_limit_bytes=V7X_VMEM_LIMIT_BYTES),
    )(w3)


def _w_in_grad_slabs(dw_pad, dtype):
    r = dw_pad.shape[0]

    def body(dw_ref, o_ref):
        for sh in range(N_DEV):
            for src, length, dst in _w_in_pieces(sh):
                o_ref[sh, :, src:src + length] = dw_ref[:, dst:dst + length].astype(dtype)

    return pl.pallas_call(
        body, name="w_in_grad_slabs", out_shape=jax.ShapeDtypeStruct((N_DEV, r, W_IN_SHARD), dtype),
        compiler_params=pltpu.CompilerParams(vmem_limit_bytes=V7X_VMEM_LIMIT_BYTES),
    )(dw_pad)


LATE = ("w_out", "ffn2_w_gate", "ffn2_w_up", "ffn2_w_down")
TRANSPOSED = ("ffn1_w_gate", "ffn1_w_up", "ffn2_w_gate", "ffn2_w_up")


def _late_weights(slabs):
    return {n: g.reshape(N_DEV * g.shape[1], g.shape[2]) for n, g in zip(LATE, slabs)}


def _local_step(x, tgt, wts, small, late_shards=None):
    bd = _head_block_diag()
    alog_row, dt_row = _lane_row(small["a_log"]), _lane_row(small["dt_bias"])
    gnorm_row = small["gdn_norm_w"].reshape(1, GDN_HEAD_DIM)
    qw_row = jnp.tile(small["q_norm_w"].reshape(-1), SWA_HEADS).reshape(1, SWA_WIDTH)
    kw_row = jnp.tile(small["k_norm_w"].reshape(-1), SWA_HEADS).reshape(1, SWA_WIDTH)
    rel_bias = small["rel_bias"]
    exchange = late_shards is not None
    dw_dtype = BF16 if exchange else F32

    x1, sv1, (wg1, wu1, wd1), got, first, (n2, r2) = _ffn_forward(
        x, small["ffn1_norm"], wts.get("ffn1_w_gate"), wts.get("ffn1_w_up"), wts.get("ffn1_w_down"), "ffn1",
        gather=[late_shards["ffn1_w_down"], late_shards["w_in"]] if exchange else (),
        head=[late_shards["ffn1_w_gate"], late_shards["ffn1_w_up"], late_shards["conv_w"]] if exchange else (),
        next_norm=small["mix_norm"])
    win_pad = _w_in_from_slabs(got[0]) if exchange else wts["w_in_pad"]
    conv_w = first[0].reshape(N_DEV, -1)[:, :QKV_A // N_DEV * CONV_WIDTH].reshape(QKV_A, CONV_WIDTH) if exchange \
        else small["conv_w"]
    conv_wt = jnp.zeros((8, QKV_A), F32).at[:CONV_WIDTH].set(conv_w.T)
    p_pad = _matmul([(n2, win_pad)], tm=256, tn=N_PAD, tk=D_MODEL, name="w_in")
    o_a, sva, gathered = _gdn_forward(p_pad, conv_wt, alog_row, dt_row, gnorm_row,
                                      gather=[late_shards[n] for n in LATE] if exchange else ())
    if exchange:
        wts = {**wts, **_late_weights(gathered)}
    wo_a, wo_b = wts["w_out"][:GDN_WIDTH], wts["w_out"][GDN_WIDTH:]
    o_b, svb = _swa_forward(p_pad, qw_row, kw_row, rel_bias, bd)
    x2, n3, r3 = _matmul([(o_a, wo_a), (o_b, wo_b)], tm=512, tn=D_MODEL, tk=GDN_WIDTH, name="w_out", res=x1,
                         norm_fwd=small["ffn2_norm"])
    (loss_row, dx3, d_final), sv2, _, _, _, _ = _ffn_forward(
        x2, small["ffn2_norm"], wts["ffn2_w_gate"], wts["ffn2_w_up"], wts["ffn2_w_down"], "ffn2", normed=(n3, r3),
        loss=(small["final_norm"], tgt))

    dx2, d_ffn2_norm, dwg2, dwu2, dwd2, _ = _ffn_backward(
        dx3, x2, small["ffn2_norm"], wts["ffn2_w_gate"], wts["ffn2_w_up"], wts["ffn2_w_down"], sv2, "ffn2", dw_dtype)
    d_oa = _matmul([(dx2, wo_a)], tb=True, tm=512, tn=GDN_WIDTH, tk=D_MODEL, name="w_out_da")
    d_ob = _matmul([(dx2, wo_b)], tb=True, tm=512, tn=SWA_WIDTH, tk=D_MODEL, name="w_out_db")
    dwo_a = _matmul([(o_a, dx2)], ta=True, tm=GDN_WIDTH, tn=D_MODEL, tk=2048, name="w_out_dwa", out_dtype=dw_dtype)
    dwo_b = _matmul([(o_b, dx2)], ta=True, tm=SWA_WIDTH, tn=D_MODEL, tk=2048, name="w_out_dwb", out_dtype=dw_dtype)

    late_grads = [_row_slabs(jnp.concatenate([dwo_a, dwo_b], axis=0)), dwg2, dwu2, dwd2]
    dp_all, dconv, gate_sums, d_gnorm, received = _gdn_backward(
        d_oa, p_pad, conv_wt, alog_row, dt_row, gnorm_row, sva, scatter=late_grads if exchange else ())
    if exchange:
        late_grads = received
    dp_all, dqw, dkw, d_rel = _swa_backward(d_ob, p_pad, qw_row, kw_row, rel_bias, bd, svb, dp_all)
    dw_pad = _matmul([(n2, dp_all)], ta=True, tm=D_MODEL, tn=N_PAD // 3, tk=2048, name="w_in_dw")
    dx1, d_mix_norm = _matmul([(dp_all, win_pad)], tb=True, tm=512, tn=D_MODEL, tk=N_PAD, name="w_in_dn",
                              norm_bwd=(x1, r2, small["mix_norm"], dx2))
    d_w_in = _w_in_grad_slabs(dw_pad, dw_dtype)
    dx, d_ffn1_norm, dwg1, dwu1, dwd1, got = _ffn_backward(
        dx1, x, small["ffn1_norm"], wg1, wu1, wd1, sv1, "ffn1", dw_dtype,
        scatter=[d_w_in] if exchange else None)
    if exchange:
        d_w_in = got[0]

    grads = {
        "ffn1_norm": d_ffn1_norm, "ffn1_w_gate": dwg1, "ffn1_w_up": dwu1, "ffn1_w_down": dwd1,
        "mix_norm": d_mix_norm, "w_in": d_w_in, "conv_w": dconv[:CONV_WIDTH].T,
        "a_log": gate_sums[0, :8].reshape(2, GDN_HEADS), "dt_bias": gate_sums[1, :8].reshape(2, GDN_HEADS),
        "gdn_norm_w": d_gnorm, "q_norm_w": dqw.reshape(SWA_HEADS, SWA_HEAD_DIM).sum(0, keepdims=True),
        "k_norm_w": dkw.reshape(SWA_HEADS, SWA_HEAD_DIM).sum(0, keepdims=True), "rel_bias": d_rel[:, :SWA_HEADS],
        "ffn2_norm": d_ffn2_norm, "final_norm": d_final, **dict(zip(LATE, late_grads)),
    }
    return loss_row, dx, grads


MESH_IDS = pl.DeviceIdType.MESH
ANY = pl.BlockSpec(memory_space=pl.ANY)


def _adamw(parts, w, m, v, name):
    nparts, r, n = parts.shape
    tr = r
    for cand in (256, 176, 128, 104, 64, 8):
        if r % cand == 0:
            tr = cand
            break
    bc1 = 1.0 - ADAM_B1 ** ADAM_STEP
    bc2 = 1.0 - ADAM_B2 ** ADAM_STEP

    def body(p_ref, w_ref, m_ref, v_ref, g_ref, d_ref, nm_ref, nv_ref):
        g = p_ref[0].astype(F32)
        for k in range(1, nparts):
            g = g + p_ref[k].astype(F32)
        mn = ADAM_B1 * m_ref[...] + (1.0 - ADAM_B1) * g
        vn = ADAM_B2 * v_ref[...] + (1.0 - ADAM_B2) * (g * g)
        m_hat = mn / bc1
        v_hat = vn / bc2
        g_ref[...] = g
        nm_ref[...] = mn
        nv_ref[...] = vn
        d_ref[...] = -ADAM_LR * (m_hat / (jnp.sqrt(v_hat) + ADAM_EPS) + ADAM_WD * w_ref[...])

    blk = pl.BlockSpec((tr, n), lambda i: (i, 0))
    return pl.pallas_call(
        body, name=name, grid=(r // tr,),
        in_specs=[pl.BlockSpec((nparts, tr, n), lambda i: (0, i, 0)), blk, blk, blk],
        out_specs=[blk] * 4, out_shape=[jax.ShapeDtypeStruct((r, n), F32)] * 4,
        compiler_params=_params("parallel"),
    )(parts, w, m, v)


def _mesh_place():
    x, y, c = lax.axis_index("x"), lax.axis_index("y"), lax.axis_index("c")
    return x, y, c, [(1 - x, y), (x, 1 - y), (1 - x, 1 - y)]


def _gather_phases(x_refs, out_refs, send_sems, recv_sems, local_sems):
    na = len(x_refs)

    def place():
        x, y, c, chips = _mesh_place()
        return (x, y, c), (x, y, 1 - c), chips, c

    def slab(i, px, py, pc):
        return out_refs[i].at[4 * px + 2 * py + pc]

    def copy(i, k, block, to, src=None):
        return pltpu.make_async_remote_copy(
            src_ref=slab(i, *block) if src is None else src, dst_ref=slab(i, *block),
            send_sem=send_sems.at[i, k], recv_sem=recv_sems.at[i, k], device_id=to, device_id_type=MESH_IDS)

    def own(i, me):
        return pltpu.make_async_copy(x_refs[i], slab(i, *me), local_sems.at[i])

    def sends(i, me, sibling, chips, c):
        return [copy(i, 0, me, sibling, src=x_refs[i])] + [copy(i, 1 + j, me, (*chip, c), src=x_refs[i])
                                                          for j, chip in enumerate(chips)]

    def start():
        me, sibling, chips, c = place()
        for i in range(na):
            own(i, me).start()
            for cp in sends(i, me, sibling, chips, c):
                cp.start()

    def forward():
        me, sibling, chips, c = place()
        for j, chip in enumerate(chips):
            for i in range(na):
                copy(i, 1 + j, (*chip, c), me).wait_recv()
                copy(i, 4 + j, (*chip, c), sibling).start()

    def finish():
        me, sibling, chips, c = place()
        for i in range(na):
            copy(i, 0, sibling, me).wait_recv()
        for j, chip in enumerate(chips):
            for i in range(na):
                copy(i, 4 + j, (*chip, 1 - c), me).wait_recv()
        for i in range(na):
            for cp in sends(i, me, sibling, chips, c):
                cp.wait_send()
            for j, chip in enumerate(chips):
                copy(i, 4 + j, (*chip, c), sibling).wait_send()
            own(i, me).wait()

    return start, forward, finish


def _gather_semaphores(na):
    return [pltpu.SemaphoreType.DMA((na, 7)), pltpu.SemaphoreType.DMA((na, 7)), pltpu.SemaphoreType.DMA((na,))]


def _scatter_phases(g_refs, out_refs, send_sems, recv_sems, local_sems):
    na = len(g_refs)

    def place(m):
        x, y, c = lax.axis_index("x"), lax.axis_index("y"), lax.axis_index("c")
        px = 1 - x if m & 4 else x
        py = 1 - y if m & 2 else y
        pc = 1 - c if m & 1 else c
        return 4 * x + 2 * y + c, (px, py, pc), 4 * px + 2 * py + pc

    def own(i):
        me, _, _ = place(0)
        return pltpu.make_async_copy(g_refs[i].at[me], out_refs[i].at[me], local_sems.at[i])

    def start():
        for i in range(na):
            own(i).start()
            for m in range(1, N_DEV):
                me, peer, peer_idx = place(m)
                pltpu.make_async_remote_copy(
                    src_ref=g_refs[i].at[peer_idx], dst_ref=out_refs[i].at[me], send_sem=send_sems.at[i, m - 1],
                    recv_sem=recv_sems.at[i, m - 1], device_id=peer, device_id_type=MESH_IDS).start()

    def finish():
        for i in range(na):
            for m in range(1, N_DEV):
                me, peer, peer_idx = place(m)
                cp = pltpu.make_async_remote_copy(
                    src_ref=g_refs[i].at[peer_idx], dst_ref=out_refs[i].at[peer_idx], send_sem=send_sems.at[i, m - 1],
                    recv_sem=recv_sems.at[i, m - 1], device_id=peer, device_id_type=MESH_IDS)
                cp.wait_recv()
                cp.wait_send()
            own(i).wait()

    return start, finish


def _all_gather_many(vs, name):
    na = len(vs)

    def body(*refs):
        x_refs, out_refs = refs[:na], refs[na:2 * na]
        for step in _gather_phases(x_refs, out_refs, *refs[2 * na:]):
            step()

    return pl.pallas_call(
        body, name=name, in_specs=[ANY] * na, out_specs=[ANY] * na,
        out_shape=[jax.ShapeDtypeStruct((N_DEV,) + v.shape, v.dtype) for v in vs],
        scratch_shapes=_gather_semaphores(na),
        compiler_params=pltpu.CompilerParams(vmem_limit_bytes=V7X_VMEM_LIMIT_BYTES),
    )(*vs)


BIG = ("ffn1_w_gate", "ffn1_w_up", "ffn1_w_down", "w_in", "w_out", "ffn2_w_gate", "ffn2_w_up", "ffn2_w_down")
SMALL = ("ffn1_norm", "mix_norm", "a_log", "dt_bias", "gdn_norm_w", "q_norm_w", "k_norm_w", "rel_bias",
         "ffn2_norm", "final_norm")
WEIGHTS = ("ffn1_norm", "ffn1_w_gate", "ffn1_w_up", "ffn1_w_down", "mix_norm", "w_in", "conv_w", "a_log", "dt_bias",
           "gdn_norm_w", "q_norm_w", "k_norm_w", "rel_bias", "w_out", "ffn2_norm", "ffn2_w_gate", "ffn2_w_up",
           "ffn2_w_down", "final_norm")


def _pack(arrays, width, row_multiple):
    flat = jnp.concatenate([a.reshape(-1) for a in arrays])
    rows = -(-flat.shape[0] // width)
    rows = -(-rows // row_multiple) * row_multiple
    return jnp.pad(flat, (0, rows * width - flat.shape[0])).reshape(rows, width)


def _unpack(packed, shapes):
    flat = packed.reshape(-1)
    out, pos = [], 0
    for shp in shapes:
        size = int(np.prod(shp))
        out.append(flat[pos:pos + size].reshape(shp))
        pos += size
    return out


def kernel(x, ffn1_norm, ffn1_w_gate, ffn1_w_up, ffn1_w_down, mix_norm, w_in, conv_w, a_log, dt_bias, gdn_norm_w, q_norm_w, k_norm_w, rel_bias, w_out, ffn2_norm, ffn2_w_gate, ffn2_w_up, ffn2_w_down, final_norm, loss_target, m_ffn1_norm, m_ffn1_w_gate, m_ffn1_w_up, m_ffn1_w_down, m_mix_norm, m_w_in, m_conv_w, m_a_log, m_dt_bias, m_gdn_norm_w, m_q_norm_w, m_k_norm_w, m_rel_bias, m_w_out, m_ffn2_norm, m_ffn2_w_gate, m_ffn2_w_up, m_ffn2_w_down, m_final_norm, v_ffn1_norm, v_ffn1_w_gate, v_ffn1_w_up, v_ffn1_w_down, v_mix_norm, v_w_in, v_conv_w, v_a_log, v_dt_bias, v_gdn_norm_w, v_q_norm_w, v_k_norm_w, v_rel_bias, v_w_out, v_ffn2_norm, v_ffn2_w_gate, v_ffn2_w_up, v_ffn2_w_down, v_final_norm):
    w = dict(ffn1_norm=ffn1_norm, ffn1_w_gate=ffn1_w_gate, ffn1_w_up=ffn1_w_up, ffn1_w_down=ffn1_w_down, mix_norm=mix_norm, w_in=w_in, conv_w=conv_w, a_log=a_log, dt_bias=dt_bias, gdn_norm_w=gdn_norm_w, q_norm_w=q_norm_w, k_norm_w=k_norm_w, rel_bias=rel_bias, w_out=w_out, ffn2_norm=ffn2_norm, ffn2_w_gate=ffn2_w_gate, ffn2_w_up=ffn2_w_up, ffn2_w_down=ffn2_w_down, final_norm=final_norm)
    mom = dict(ffn1_norm=m_ffn1_norm, ffn1_w_gate=m_ffn1_w_gate, ffn1_w_up=m_ffn1_w_up, ffn1_w_down=m_ffn1_w_down, mix_norm=m_mix_norm, w_in=m_w_in, conv_w=m_conv_w, a_log=m_a_log, dt_bias=m_dt_bias, gdn_norm_w=m_gdn_norm_w, q_norm_w=m_q_norm_w, k_norm_w=m_k_norm_w, rel_bias=m_rel_bias, w_out=m_w_out, ffn2_norm=m_ffn2_norm, ffn2_w_gate=m_ffn2_w_gate, ffn2_w_up=m_ffn2_w_up, ffn2_w_down=m_ffn2_w_down, final_norm=m_final_norm)
    var = dict(ffn1_norm=v_ffn1_norm, ffn1_w_gate=v_ffn1_w_gate, ffn1_w_up=v_ffn1_w_up, ffn1_w_down=v_ffn1_w_down, mix_norm=v_mix_norm, w_in=v_w_in, conv_w=v_conv_w, a_log=v_a_log, dt_bias=v_dt_bias, gdn_norm_w=v_gdn_norm_w, q_norm_w=v_q_norm_w, k_norm_w=v_k_norm_w, rel_bias=v_rel_bias, w_out=v_w_out, ffn2_norm=v_ffn2_norm, ffn2_w_gate=v_ffn2_w_gate, ffn2_w_up=v_ffn2_w_up, ffn2_w_down=v_ffn2_w_down, final_norm=v_final_norm)
    ix, iy, ic = lax.axis_index("x"), lax.axis_index("y"), lax.axis_index("c")
    me = 4 * ix + 2 * iy + ic

    def local(a, n):
        return jnp.swapaxes(a[0], 0, 1) if n in TRANSPOSED else a[0]

    shard = {n: local(w[n], n) for n in BIG}

    conv_shard_shape = w["conv_w"][0].shape
    small = {n: w[n][0] if n not in ("rel_bias",) else w[n] for n in SMALL}
    small = {n: (a.reshape(1, -1) if n.endswith("norm") else a) for n, a in small.items()}
    shards = {n: shard[n].astype(BF16) for n in BIG}
    shards["conv_w"] = _pack([w["conv_w"][0]], LANE, 8)
    loss_row, grad_x, grads = _local_step(x[0], loss_target[0], {}, small, late_shards=shards)

    big_out = [[], [], [], []]
    for n in BIG:
        for kind, val in enumerate(_adamw(grads[n], shard[n], local(mom[n], n), local(var[n], n), f"{n}_adamw")):
            big_out[kind].append(jnp.swapaxes(val, 0, 1) if n in TRANSPOSED else val)

    small_names = SMALL + ("conv_w",)
    small_shapes = [grads[n].shape for n in small_names] + [(1, 1)]
    g_small = _pack([grads[n] for n in small_names] + [loss_row[:, :1]], LANE, 8)
    all_small = _all_gather_many([g_small], "gather_small_grads")[0]
    riders = [jnp.zeros(shp, F32) for shp in small_shapes[len(SMALL):]]
    ws = _pack([w[n].reshape(grads[n].shape) for n in SMALL] + riders, LANE, 8)
    ms = _pack([mom[n].reshape(grads[n].shape) for n in SMALL] + riders, LANE, 8)
    vs = _pack([var[n].reshape(grads[n].shape) for n in SMALL] + riders, LANE, 8)
    small_out = [_unpack(a, small_shapes) for a in _adamw(all_small, ws, ms, vs, "adamw_small")]
    loss = small_out[0][-1][0, 0]
    conv_g = lax.dynamic_slice_in_dim(small_out[0][len(SMALL)], me * conv_shard_shape[0], conv_shard_shape[0], axis=0)
    conv_out = [_unpack(a, [conv_shard_shape])[0] for a in _adamw(
        _pack([conv_g], LANE, 8)[None], _pack([w["conv_w"][0]], LANE, 8), _pack([mom["conv_w"][0]], LANE, 8),
        _pack([var["conv_w"][0]], LANE, 8), "adamw_conv")]

    def leaf(kind, n):
        if n in BIG:
            val = big_out[kind][BIG.index(n)]
        elif n == "conv_w":
            val = conv_out[kind]
        else:
            val = small_out[kind][SMALL.index(n)]
        return val.reshape(w[n].shape)

    outs = [loss, grad_x[None]]
    for kind in range(4):
        outs += [leaf(kind, n) for n in WEIGHTS]
    return tuple(outs)
```

```python
import functools
import math

import numpy as np
import jax
import jax.numpy as jnp
from jax import lax
from jax.experimental import pallas as pl
from jax.experimental.pallas import tpu as pltpu

F32 = jnp.float32
BF16 = jnp.bfloat16

D_MODEL = 1024
D_FF = 2816
GDN_HEADS = 4
GDN_HEAD_DIM = 128
GDN_WIDTH = 512
CONV_WIDTH = 5
CHUNK = 64
SWA_HEADS = 8
SWA_HEAD_DIM = 64
SWA_WIDTH = 512
DILATION_PATTERNS = ((128, 1), (512, 4), (2048, 16))
REL_BUCKETS = 32
REL_MAX_DISTANCE = 1024
EPS = 1e-6
NEG_BIG = -1e30
N_DEV = 8

ADAM_LR = 0.001
ADAM_B1 = 0.9
ADAM_B2 = 0.999
ADAM_EPS = 1e-08
ADAM_WD = 0.01
ADAM_STEP = 10

QKV_A = 3 * GDN_WIDTH
OFF_B = QKV_A
OFF_Z = OFF_B + 3 * SWA_WIDTH
OFF_AB = OFF_Z + GDN_WIDTH
N_PAD = OFF_AB + 256
N_IN = 3600
NAT_Z, NAT_AB, NAT_B = QKV_A, QKV_A + GDN_WIDTH, QKV_A + GDN_WIDTH + 16

V7X_VMEM_LIMIT_BYTES = 56 * 1024 * 1024
LANE = 128
ATT_BQ = 128
ATT_HALO = 64
CONV_ROWS = 256

NN = (((1,), (0,)), ((), ()))
NT = (((1,), (1,)), ((), ()))
TN = (((0,), (0,)), ((), ()))


def _params(*sem):
    return pltpu.CompilerParams(dimension_semantics=sem, vmem_limit_bytes=V7X_VMEM_LIMIT_BYTES)


def _dot(a, b, dn=NN):
    return lax.dot_general(a.astype(BF16), b.astype(BF16), dn, preferred_element_type=F32)


def _sigmoid(x):
    return 1.0 / (1.0 + jnp.exp(-x))


class _Exchange:
    def __init__(self, kind, arrays):
        self.kind, self.arrays = kind, list(arrays)

    def out_shape(self):
        lead = (N_DEV,) if self.kind == "gather" else ()
        return [jax.ShapeDtypeStruct(lead + v.shape, v.dtype) for v in self.arrays]

    def hooks(self, in_refs, out_refs, sems, grid):
        step = pl.program_id(0)
        for axis in range(1, len(grid)):
            step = step * grid[axis] + pl.program_id(axis)
        total = math.prod(grid)
        if self.kind == "gather":
            assert total >= 4
            start, forward, finish = _gather_phases(in_refs, out_refs, *sems)
            pl.when(step == total // 2)(forward)
        else:
            assert total >= 2
            start, finish = _scatter_phases(in_refs, out_refs, *sems)
        pl.when(step == 0)(start)
        pl.when(step == total - 1)(finish)


def _pallas(body, *, name, grid, in_specs, out_specs, out_shape, args, semantics, scratch_shapes=(), exchange=None):
    n_in, n_out, n_scr = len(in_specs), len(out_specs), len(scratch_shapes)
    if exchange is None:
        res = pl.pallas_call(
            body, name=name, grid=grid, in_specs=list(in_specs), out_specs=list(out_specs), out_shape=list(out_shape),
            scratch_shapes=list(scratch_shapes), compiler_params=_params(*semantics))(*args)
        return list(res), []
    na = len(exchange.arrays)

    def carrying(*refs):
        ins, sent = refs[:n_in], refs[n_in:n_in + na]
        outs = refs[n_in + na:n_in + na + n_out]
        landed = refs[n_in + na + n_out:n_in + 2 * na + n_out]
        rest = refs[n_in + 2 * na + n_out:]
        exchange.hooks(sent, landed, rest[n_scr:], grid)
        body(*ins, *outs, *rest[:n_scr])

    res = pl.pallas_call(
        carrying, name=name, grid=grid, in_specs=list(in_specs) + [ANY] * na, out_specs=list(out_specs) + [ANY] * na,
        out_shape=list(out_shape) + exchange.out_shape(), scratch_shapes=list(scratch_shapes) + _gather_semaphores(na),
        compiler_params=_params(*(["arbitrary"] * len(grid))))(*args, *exchange.arrays)
    return list(res[:n_out]), list(res[n_out:])


def _matmul(pairs, *, ta=False, tb=False, out_dtype=F32, tm, tn, tk, name, res=None, alpha=None, norm_bwd=None,
            norm_fwd=None, loss=None, exchange=None):
    a0, b0 = pairs[0]
    m = a0.shape[1] if ta else a0.shape[0]
    k = a0.shape[0] if ta else a0.shape[1]
    n = b0.shape[0] if tb else b0.shape[1]
    tm, tn, tk = min(tm, m), min(tn, n), min(tk, k)
    assert m % tm == 0 and n % tn == 0 and k % tk == 0, (name, m, n, k, tm, tn, tk)
    nk = k // tk
    npairs = len(pairs)
    dn = (((0 if ta else 1,), (1 if tb else 0,)), ((), ()))
    assert norm_bwd is None or (tn == n and res is None and alpha is None)

    def body(*refs):
        ins = refs[:2 * npairs]
        pos = 2 * npairs
        r_ref = None
        if res is not None:
            r_ref = refs[pos]
            pos += 1
        if norm_bwd is not None:
            x_ref, rs_ref, w_ref, dres_ref = refs[pos:pos + 4]
            o_ref, dw_ref, acc = refs[pos + 4:pos + 7]

            @pl.when((pl.program_id(0) == 0) & (pl.program_id(2) == 0))
            def _():
                dw_ref[...] = jnp.zeros_like(dw_ref)
        elif norm_fwd is not None:
            wn_ref, o_ref, n_ref, rs_out, acc = refs[pos:pos + 5]
        elif loss is not None:
            wf_ref, tg_ref, loss_ref, o_ref, dwf_ref, acc = refs[pos:pos + 6]

            @pl.when((pl.program_id(0) == 0) & (pl.program_id(2) == 0))
            def _():
                dwf_ref[...] = jnp.zeros_like(dwf_ref)
                loss_ref[...] = jnp.zeros_like(loss_ref)
        else:
            o_ref, acc = refs[pos], refs[pos + 1]
        kk = pl.program_id(2)
        t = None
        for p in range(npairs):
            d = _dot(ins[2 * p][...], ins[2 * p + 1][...], dn)
            t = d if t is None else t + d

        if nk > 1:
            @pl.when(kk == 0)
            def _():
                acc[...] = t

            @pl.when((kk > 0) & (kk < nk - 1))
            def _():
                acc[...] += t

        @pl.when(kk == nk - 1)
        def _():
            r = acc[...] + t if nk > 1 else t
            if alpha is not None:
                r = r * alpha
            if r_ref is not None:
                r = r_ref[...] + r
            if norm_bwd is not None:
                rs = rs_ref[...]
                xhat = x_ref[...] * rs
                dw_ref[...] += jnp.sum(r * xhat, axis=0, keepdims=True)
                t_w = r * w_ref[...]
                r = dres_ref[...] + rs * (t_w - xhat * jnp.mean(t_w * xhat, axis=-1, keepdims=True))
            if norm_fwd is not None:
                rs = lax.rsqrt(jnp.mean(r * r, axis=-1, keepdims=True) + EPS)
                n_ref[...] = (r * rs * wn_ref[...]).astype(BF16)
                rs_out[...] = rs
            if loss is not None:
                wv = wf_ref[...]
                rs = lax.rsqrt(jnp.mean(r * r, axis=-1, keepdims=True) + EPS)
                xhat = r * rs
                e = xhat * wv - tg_ref[...]
                part = 0.5 * jnp.sum(jnp.mean(e * e, axis=-1, keepdims=True), axis=0, keepdims=True)
                loss_ref[...] += jnp.broadcast_to(part, loss_ref.shape)
                dy = e * (1.0 / n)
                dwf_ref[...] += jnp.sum(dy * xhat, axis=0, keepdims=True)
                t_w = dy * wv
                r = rs * (t_w - xhat * jnp.mean(t_w * xhat, axis=-1, keepdims=True))
            o_ref[...] = r.astype(out_dtype)

    a_spec = pl.BlockSpec((tk, tm), lambda i, j, kk: (kk, i)) if ta else pl.BlockSpec((tm, tk), lambda i, j, kk: (i, kk))
    b_spec = pl.BlockSpec((tn, tk), lambda i, j, kk: (j, kk)) if tb else pl.BlockSpec((tk, tn), lambda i, j, kk: (kk, j))
    o_spec = pl.BlockSpec((tm, tn), lambda i, j, kk: (i, j))
    in_specs = [a_spec, b_spec] * npairs + ([o_spec] if res is not None else [])
    args = [t for pr in pairs for t in pr] + ([res] if res is not None else [])
    out_specs, out_shape = [o_spec], [jax.ShapeDtypeStruct((m, n), out_dtype)]
    if norm_bwd is not None:
        vec = pl.BlockSpec((1, n), lambda i, j, kk: (0, 0))
        in_specs += [o_spec, pl.BlockSpec((tm, 1), lambda i, j, kk: (i, 0)), vec, o_spec]
        args += list(norm_bwd)
        out_specs.append(vec)
        out_shape.append(jax.ShapeDtypeStruct((1, n), F32))
    if norm_fwd is not None:
        assert tn == n and norm_bwd is None
        in_specs.append(pl.BlockSpec((1, n), lambda i, j, kk: (0, 0)))
        args.append(norm_fwd)
        out_specs += [o_spec, pl.BlockSpec((tm, 1), lambda i, j, kk: (i, 0))]
        out_shape += [jax.ShapeDtypeStruct((m, n), BF16), jax.ShapeDtypeStruct((m, 1), F32)]
    if loss is not None:
        assert tn == n and norm_bwd is None and norm_fwd is None
        vec = pl.BlockSpec((1, n), lambda i, j, kk: (0, 0))
        in_specs += [vec, o_spec]
        args += list(loss)
        out_specs = [pl.BlockSpec((1, LANE), lambda i, j, kk: (0, 0))] + out_specs + [vec]
        out_shape = [jax.ShapeDtypeStruct((1, LANE), F32)] + out_shape + [jax.ShapeDtypeStruct((1, n), F32)]
    sequential = norm_bwd is not None or loss is not None
    outs, exchanged = _pallas(
        body, name=name, grid=(m // tm, n // tn, nk), in_specs=in_specs, out_specs=out_specs, out_shape=out_shape,
        scratch_shapes=[pltpu.VMEM((tm, tn) if nk > 1 else (8, LANE), F32)],
        semantics=("arbitrary",) * 3 if sequential else ("parallel", "parallel", "arbitrary"), args=args,
        exchange=exchange)
    out = outs[0] if len(outs) == 1 else tuple(outs)
    return out if exchange is None else (out, exchanged)


def _rms_fwd(x, w, name, exchange=None):
    s, d = x.shape
    tm = min(512, s)

    def body(x_ref, w_ref, n_ref, r_ref):
        xv = x_ref[...]
        r = lax.rsqrt(jnp.mean(xv * xv, axis=-1, keepdims=True) + EPS)
        n_ref[...] = (xv * r * w_ref[...]).astype(BF16)
        r_ref[...] = r

    (n, r), exchanged = _pallas(
        body, name=name, grid=(s // tm,),
        in_specs=[pl.BlockSpec((tm, d), lambda i: (i, 0)), pl.BlockSpec((1, d), lambda i: (0, 0))],
        out_specs=[pl.BlockSpec((tm, d), lambda i: (i, 0)), pl.BlockSpec((tm, 1), lambda i: (i, 0))],
        out_shape=[jax.ShapeDtypeStruct((s, d), BF16), jax.ShapeDtypeStruct((s, 1), F32)],
        semantics=("parallel",), args=(x, w), exchange=exchange)
    return (n, r) if exchange is None else (n, r, exchanged)


def _rms_bwd(dn, x, r, w, dres, name, exchange=None):
    s, d = x.shape
    tm = min(512, s)

    def body(dn_ref, x_ref, r_ref, w_ref, dres_ref, dx_ref, dw_ref):
        @pl.when(pl.program_id(0) == 0)
        def _():
            dw_ref[...] = jnp.zeros_like(dw_ref)

        rv = r_ref[...]
        xhat = x_ref[...] * rv
        g = dn_ref[...]
        t = g * w_ref[...]
        dx_ref[...] = dres_ref[...] + rv * (t - xhat * jnp.mean(t * xhat, axis=-1, keepdims=True))
        dw_ref[...] += jnp.sum(g * xhat, axis=0, keepdims=True)

    row = pl.BlockSpec((tm, d), lambda i: (i, 0))
    vec = pl.BlockSpec((1, d), lambda i: (0, 0))
    (dx, dw), exchanged = _pallas(
        body, name=name, grid=(s // tm,),
        in_specs=[row, row, pl.BlockSpec((tm, 1), lambda i: (i, 0)), vec, row],
        out_specs=[row, vec],
        out_shape=[jax.ShapeDtypeStruct((s, d), F32), jax.ShapeDtypeStruct((1, d), F32)],
        semantics=("arbitrary",), args=(dn, x, r, w, dres), exchange=exchange)
    return (dx, dw) if exchange is None else (dx, dw, exchanged)


def _ffn_up(n, wg, wu, name, exchange=None):
    s, d = n.shape
    f = wg.shape[0]
    tm, tn = min(512, s), f // 2

    def body(n_ref, wg_ref, wu_ref, g_ref, u_ref, a_ref):
        nv = n_ref[...]
        g = _dot(nv, wg_ref[...], NT)
        u = _dot(nv, wu_ref[...], NT)
        g_ref[...] = g.astype(BF16)
        u_ref[...] = u.astype(BF16)
        a_ref[...] = (g * _sigmoid(g) * u).astype(BF16)

    o = pl.BlockSpec((tm, tn), lambda j, i: (i, j))
    wspec = pl.BlockSpec((tn, d), lambda j, i: (j, 0))
    return _pallas(
        body, name=name, grid=(f // tn, s // tm),
        in_specs=[pl.BlockSpec((tm, d), lambda j, i: (i, 0)), wspec, wspec],
        out_specs=[o, o, o],
        out_shape=[jax.ShapeDtypeStruct((s, f), BF16)] * 3,
        semantics=("parallel", "parallel"), args=(n, wg, wu), exchange=exchange)


def _ffn_dact(dx, wd, g, u, name, exchange=None):
    s, d = dx.shape
    f = wd.shape[0]
    tm, tn = min(512, s), f // 2

    def body(dx_ref, wd_ref, g_ref, u_ref, dg_ref, du_ref):
        da = 0.5 * _dot(dx_ref[...], wd_ref[...], NT)
        gv = g_ref[...].astype(F32)
        sg = _sigmoid(gv)
        du_ref[...] = (da * gv * sg).astype(BF16)
        dg_ref[...] = (da * u_ref[...].astype(F32) * (sg * (1.0 + gv * (1.0 - sg)))).astype(BF16)

    o = pl.BlockSpec((tm, tn), lambda j, i: (i, j))
    return _pallas(
        body, name=name, grid=(f // tn, s // tm),
        in_specs=[pl.BlockSpec((tm, d), lambda j, i: (i, 0)), pl.BlockSpec((tn, d), lambda j, i: (j, 0)), o, o],
        out_specs=[o, o],
        out_shape=[jax.ShapeDtypeStruct((s, f), BF16), jax.ShapeDtypeStruct((s, f), BF16)],
        semantics=("parallel", "parallel"), args=(dx, wd, g, u), exchange=exchange)


def _row_slabs(full):
    return full.reshape(N_DEV, full.shape[0] // N_DEV, full.shape[1])


def _rows_of(slabs):
    return slabs.reshape(N_DEV * slabs.shape[1], slabs.shape[2])


def _ffn_forward(x, norm_w, wg, wu, wd, tag, gather=(), head=(), normed=None, next_norm=None, loss=None):
    if normed is not None:
        (n, r), first = normed, []
    elif head:
        n, r, first = _rms_fwd(x, norm_w, f"{tag}_norm", _Exchange("gather", head))
    else:
        (n, r), first = _rms_fwd(x, norm_w, f"{tag}_norm"), []
    if wg is None:
        wg, wu, first = _rows_of(first[0]), _rows_of(first[1]), first[2:]
    (g, u, a), got = _ffn_up(n, wg, wu, f"{tag}_up", _Exchange("gather", gather) if gather else None)
    if wd is None:
        wd, got = _rows_of(got[0]), got[1:]
    y = _matmul([(a, wd)], tm=512, tn=1024, tk=wd.shape[0], name=f"{tag}_down", res=x, alpha=0.5, norm_fwd=next_norm,
                loss=loss)
    y, nxt = (y[0], y[1:]) if next_norm is not None else (y, None)
    return y, (n, r, g, u, a), (wg, wu, wd), got, first, nxt


def _ffn_backward(dy, x, norm_w, wgt, wut, wd, saved, tag, dw_dtype=F32, scatter=None):
    n, r, g, u, a = saved

    def behind(arrays):
        return _Exchange("scatter", arrays) if scatter is not None else None

    def dw(act, grad, name, alpha=None, exchange=None):
        return _matmul([(act, grad)], ta=True, tm=1408, tn=1024, tk=2048, name=name, alpha=alpha, out_dtype=dw_dtype,
                       exchange=exchange)

    dwd = _row_slabs(dw(a, dy, f"{tag}_dwd", alpha=0.5))
    (dg, du), extras = _ffn_dact(dy, wd, g, u, f"{tag}_dact", behind(scatter))
    if scatter is None:
        dwg, dwu = _row_slabs(dw(dg, n, f"{tag}_dwg")), _row_slabs(dw(du, n, f"{tag}_dwu"))
    else:
        dwg, (dwd,) = dw(dg, n, f"{tag}_dwg", exchange=behind([dwd]))
        dwu, (dwg,) = dw(du, n, f"{tag}_dwu", exchange=behind([_row_slabs(dwg)]))
        dwu = _row_slabs(dwu)
    if scatter is None:
        dx, dnorm = _matmul([(dg, wgt), (du, wut)], tm=512, tn=1024, tk=wgt.shape[0], name=f"{tag}_dn",
                            norm_bwd=(x, r, norm_w, dy))
    else:
        dn, (dwu,) = _matmul([(dg, wgt), (du, wut)], tm=512, tn=1024, tk=wgt.shape[0], name=f"{tag}_dn",
                             exchange=behind([dwu]))
        dx, dnorm = _rms_bwd(dn, x, r, norm_w, dy, f"{tag}_dnorm")
    return dx, dnorm, dwg, dwu, dwd, extras


Q_SCALE = GDN_HEAD_DIM ** -0.5
CONV_HALO = 8


def _lane_block(s):
    return pl.BlockSpec((None, s, LANE), lambda j: (j, 0, 0))


def _conv_taps(win, w_ref, rows, sign):
    n = rows + 2 * CONV_HALO
    acc = None
    for t in range(CONV_WIDTH):
        o = sign * (t - CONV_WIDTH // 2)
        sh = win if o == 0 else pltpu.roll(win, (-o) % n, 0)
        term = sh[CONV_HALO:CONV_HALO + rows] * w_ref[t:t + 1, :]
        acc = term if acc is None else acc + term
    return acc


def _gdn_conv_fwd(p_pad, conv_wt):
    s = p_pad.shape[0]
    rows = min(CONV_ROWS, s)
    nblk = QKV_A // LANE

    def body(p_ref, w_ref, c_ref, y_ref, pad):
        j = pl.program_id(0)
        zeros = jnp.zeros((CONV_HALO, LANE), F32)
        pad[0:CONV_HALO, :] = zeros
        pad[CONV_HALO + s:2 * CONV_HALO + s, :] = zeros
        pad[CONV_HALO:CONV_HALO + s, :] = p_ref[...]

        def chunk(ci, carry):
            b = pl.multiple_of(ci * rows, rows)
            win = pad[pl.ds(b, rows + 2 * CONV_HALO), :]
            c = _conv_taps(win, w_ref, rows, 1)
            c_ref[pl.ds(b, rows), :] = c
            act = c * _sigmoid(c)
            nrm = lax.rsqrt(jnp.sum(act * act, axis=-1, keepdims=True) + EPS)
            mult = jnp.where(j < GDN_HEADS, nrm * Q_SCALE, jnp.where(j < 2 * GDN_HEADS, nrm, 1.0))
            y_ref[pl.ds(b, rows), :] = act * mult
            return carry

        lax.fori_loop(0, s // rows, chunk, 0)

    col = pl.BlockSpec((s, LANE), lambda j: (0, j))
    return pl.pallas_call(
        body, name="gdn_conv_fwd", grid=(nblk,),
        in_specs=[col, pl.BlockSpec((8, LANE), lambda j: (0, j))],
        out_specs=[_lane_block(s), _lane_block(s)],
        out_shape=[jax.ShapeDtypeStruct((nblk, s, LANE), F32), jax.ShapeDtypeStruct((nblk, s, LANE), F32)],
        scratch_shapes=[pltpu.VMEM((s + 2 * CONV_HALO, LANE), F32)],
        compiler_params=_params("parallel"),
    )(p_pad, conv_wt)


def _gdn_conv_bwd(dy_f, dy_r, c_pre, p_pad, conv_wt, dp_all):
    s = p_pad.shape[0]
    rows = min(CONV_ROWS, s)
    nblk = QKV_A // LANE

    def body(dyf_ref, dyr_ref, c_ref, p_ref, w_ref, _, dp_ref, dw_ref, ppad, dcpad):
        j = pl.program_id(0)
        zeros = jnp.zeros((CONV_HALO, LANE), F32)
        for buf in (ppad, dcpad):
            buf[0:CONV_HALO, :] = zeros
            buf[CONV_HALO + s:2 * CONV_HALO + s, :] = zeros
        ppad[CONV_HALO:CONV_HALO + s, :] = p_ref[...]

        def act_bwd(ci, carry):
            b = pl.multiple_of(ci * rows, rows)
            c = c_ref[pl.ds(b, rows), :]
            g = dyf_ref[pl.ds(b, rows), :] + dyr_ref[pl.ds(b, rows), :]
            sg = _sigmoid(c)
            act = c * sg
            nrm = lax.rsqrt(jnp.sum(act * act, axis=-1, keepdims=True) + EPS)
            yh = act * nrm
            scale = jnp.where(j < GDN_HEADS, Q_SCALE, 1.0)
            dact_qk = (scale * nrm) * (g - yh * jnp.sum(g * yh, axis=-1, keepdims=True))
            dact = jnp.where(j < 2 * GDN_HEADS, dact_qk, g)
            dcpad[pl.ds(pl.multiple_of(b + CONV_HALO, CONV_HALO), rows), :] = dact * (sg * (1.0 + c * (1.0 - sg)))
            return carry

        lax.fori_loop(0, s // rows, act_bwd, 0)
        tap = lax.broadcasted_iota(jnp.int32, (8, LANE), 0)

        def taps_bwd(ci, dw):
            b = pl.multiple_of(ci * rows, rows)
            dcw = dcpad[pl.ds(b, rows + 2 * CONV_HALO), :]
            dp_ref[pl.ds(b, rows), :] = _conv_taps(dcw, w_ref, rows, -1).astype(BF16)
            pw = ppad[pl.ds(b, rows + 2 * CONV_HALO), :]
            dc = dcw[CONV_HALO:CONV_HALO + rows]
            n = rows + 2 * CONV_HALO
            for t in range(CONV_WIDTH):
                o = t - CONV_WIDTH // 2
                sh = pw if o == 0 else pltpu.roll(pw, (-o) % n, 0)
                row = jnp.sum(dc * sh[CONV_HALO:CONV_HALO + rows], axis=0, keepdims=True)
                dw = dw + jnp.where(tap == t, row, 0.0)
            return dw

        dw_ref[...] = lax.fori_loop(0, s // rows, taps_bwd, jnp.zeros((8, LANE), F32))

    col = pl.BlockSpec((s, LANE), lambda j: (0, j))
    wspec = pl.BlockSpec((8, LANE), lambda j: (0, j))
    return pl.pallas_call(
        body, name="gdn_conv_bwd", grid=(nblk,),
        in_specs=[_lane_block(s), _lane_block(s), _lane_block(s), col, wspec, ANY],
        out_specs=[col, wspec],
        out_shape=[jax.ShapeDtypeStruct(dp_all.shape, dp_all.dtype), jax.ShapeDtypeStruct((8, QKV_A), F32)],
        scratch_shapes=[pltpu.VMEM((s + 2 * CONV_HALO, LANE), F32), pltpu.VMEM((s + 2 * CONV_HALO, LANE), F32)],
        input_output_aliases={5: 0},
        compiler_params=_params("parallel"),
    )(dy_f, dy_r, c_pre, p_pad, conv_wt, dp_all)


def _softplus(x):
    return jnp.maximum(x, 0.0) + jnp.log(1.0 + jnp.exp(-jnp.abs(x)))


def _gdn_gates_fwd(p_pad, alog_row, dt_row):
    s = p_pad.shape[0]
    tm = min(1024, s)

    def body(p_ref, al_ref, dt_ref, o_ref):
        x = p_ref[...]
        lane = lax.broadcasted_iota(jnp.int32, x.shape, 1)
        g = -jnp.exp(al_ref[...]) * _softplus(x + dt_ref[...])
        o_ref[...] = jnp.where(lane < 8, g, jnp.where(lane < 16, _sigmoid(x), 0.0))

    vec = pl.BlockSpec((1, LANE), lambda i: (0, 0))
    return pl.pallas_call(
        body, name="gdn_gates_fwd", grid=(s // tm,),
        in_specs=[pl.BlockSpec((tm, LANE), lambda i: (i, OFF_AB // LANE)), vec, vec],
        out_specs=pl.BlockSpec((tm, LANE), lambda i: (i, 0)),
        out_shape=jax.ShapeDtypeStruct((s, LANE), F32),
        compiler_params=_params("parallel"),
    )(p_pad, alog_row, dt_row)


def _gdn_gates_bwd(dgb_f, dgb_r, p_pad, gb, alog_row, dt_row, dp_all):
    s = p_pad.shape[0]
    tm = min(1024, s)
    tail = N_PAD - OFF_AB

    def body(df_ref, dr_ref, p_ref, gb_ref, al_ref, dt_ref, _, dp_ref, sum_ref):
        @pl.when(pl.program_id(0) == 0)
        def _():
            sum_ref[...] = jnp.zeros_like(sum_ref)

        x = p_ref[...]
        gbv = gb_ref[...]
        dgb = df_ref[...] + dr_ref[...]
        lane = lax.broadcasted_iota(jnp.int32, x.shape, 1)
        da = dgb * (-jnp.exp(al_ref[...])) * _sigmoid(x + dt_ref[...])
        db = dgb * gbv * (1.0 - gbv)
        dp_ref[:, 0:LANE] = jnp.where(lane < 8, da, jnp.where(lane < 16, db, 0.0)).astype(BF16)
        dp_ref[:, LANE:tail] = jnp.zeros((tm, tail - LANE), BF16)
        row = lax.broadcasted_iota(jnp.int32, (8, LANE), 0)
        lane8 = lax.broadcasted_iota(jnp.int32, (8, LANE), 1)
        d_alog = jnp.sum(dgb * gbv, axis=0, keepdims=True)
        d_dt = jnp.sum(da, axis=0, keepdims=True)
        upd = jnp.where(row == 0, d_alog, jnp.where(row == 1, d_dt, 0.0))
        sum_ref[...] += jnp.where(lane8 < 8, upd, 0.0)

    vec = pl.BlockSpec((1, LANE), lambda i: (0, 0))
    blk = pl.BlockSpec((tm, LANE), lambda i: (i, 0))
    return pl.pallas_call(
        body, name="gdn_gates_bwd", grid=(s // tm,),
        in_specs=[blk, blk, pl.BlockSpec((tm, LANE), lambda i: (i, OFF_AB // LANE)), blk, vec, vec, ANY],
        out_specs=[pl.BlockSpec((tm, tail), lambda i: (i, OFF_AB // tail)), pl.BlockSpec((8, LANE), lambda i: (0, 0))],
        out_shape=[jax.ShapeDtypeStruct(dp_all.shape, dp_all.dtype), jax.ShapeDtypeStruct((8, LANE), F32)],
        input_output_aliases={6: 0},
        compiler_params=_params("arbitrary"),
    )(dgb_f, dgb_r, p_pad, gb, alog_row, dt_row, dp_all)


def _chunk_masks(rev):
    row = lax.broadcasted_iota(jnp.int32, (CHUNK, CHUNK), 0)
    col = lax.broadcasted_iota(jnp.int32, (CHUNK, CHUNK), 1)
    le = (col >= row) if rev else (col <= row)
    strict = (col > row) if rev else (col < row)
    return le, strict, row == col


def _gate_lanes(rev, h):
    d = 1 if rev else 0
    return d * GDN_HEADS + h, 8 + d * GDN_HEADS + h


BNN = (((2,), (1,)), ((0,), (0,)))
BNT = (((2,), (2,)), ((0,), (0,)))
BTN = (((1,), (1,)), ((0,), (0,)))
NB = 2 * GDN_HEADS
DELTA_CHUNKS = 8


def _bdot(a, b, dn=BNN):
    return lax.dot_general(a.astype(BF16), b.astype(BF16), dn, preferred_element_type=F32)


def _dot3(a, b, dn, exact_a=False, exact_b=False):
    def d(x, y):
        return lax.dot_general(x, y, dn, preferred_element_type=F32)

    ah = a.astype(BF16)
    bh = b.astype(BF16)
    out = d(ah, bh)
    if not exact_b:
        out = out + d(ah, (b - bh.astype(F32)).astype(BF16))
    if not exact_a:
        out = out + d((a - ah.astype(F32)).astype(BF16), bh)
    return out


def _both(f_val, r_val):
    return jnp.stack([f_val] * GDN_HEADS + [r_val] * GDN_HEADS)


def _head_blocks(ref_f, ref_r, rows_f, rows_r):
    return jnp.concatenate([ref_f[:, rows_f, :], ref_r[:, rows_r, :]], axis=0)


def _chunk_rows(c):
    return slice(c * CHUNK, (c + 1) * CHUNK), slice((DELTA_CHUNKS - 1 - c) * CHUNK, (DELTA_CHUNKS - c) * CHUNK)


def _heads(ref_f, ref_r, rows_f, rows_r):
    hd = GDN_HEAD_DIM
    return jnp.stack([ref_f[rows_f, h * hd:(h + 1) * hd] for h in range(GDN_HEADS)]
                     + [ref_r[rows_r, h * hd:(h + 1) * hd] for h in range(GDN_HEADS)])


def _gate_cols(tile_f, tile_r, base):
    return jnp.stack([tile_f[:, base + h:base + h + 1] for h in range(GDN_HEADS)]
                     + [tile_r[:, base + GDN_HEADS + h:base + GDN_HEADS + h + 1] for h in range(GDN_HEADS)])


def _chunk_common2(q, k, v, gbf, gbr):
    mf, mr = _chunk_masks(False), _chunk_masks(True)
    le, strict = _both(mf[0], mr[0]), _both(mf[1], mr[1])
    eye = mf[2]
    gcm_f = _dot3(mf[0].astype(F32), gbf, NN, exact_a=True)
    gcm_r = _dot3(mr[0].astype(F32), gbr, NN, exact_a=True)
    g, beta, gc = _gate_cols(gbf, gbr, 0), _gate_cols(gbf, gbr, 8), _gate_cols(gcm_f, gcm_r, 0)
    gc_row = _dot3(jnp.ones((NB, CHUNK, CHUNK), F32), jnp.where(eye[None], gc, 0.0), BNN, exact_a=True)
    decay = jnp.where(le, jnp.exp(jnp.where(le, gc - gc_row, 0.0)), 0.0)
    eg = jnp.exp(gc)
    gl = jnp.sum(g, axis=1, keepdims=True)
    kb = k * beta
    vb = v * beta
    kbeg = kb * eg
    lm = jnp.where(strict, _bdot(kb, k, BNT) * decay, 0.0)
    intra = _bdot(q, k, BNT) * decay
    edec = jnp.exp(gl - gc)
    return dict(strict=strict, eye=eye, beta=beta, decay=decay, eg=eg, gl=gl, kb=kb, vb=vb, kbeg=kbeg,
                lm=lm, intra=intra, qg=q * eg, edec=edec, kdec=k * edec)


def _unit_triangular_inverse(lm, eye):
    x = -lm
    t = eye[None].astype(F32) + x
    p = x
    for _ in range(5):
        p = _bdot(p, p)
        t = t + _bdot(t, p)
    return t


def _delta_fwd2(y, gb, gather=()):
    s = y.shape[1]
    nc = s // CHUNK
    hd = GDN_HEAD_DIM
    na = len(gather)

    def body(*refs):
        qf, kf, vf, gf, qr, kr, vr, gr = refs[:8]
        of_ref, or_ref, sf_all, sr_all, tf_all, tr_all = refs[8 + na:14 + na]
        state = refs[14 + 2 * na]
        step = pl.program_id(0)

        @pl.when(step == 0)
        def _():
            state[...] = jnp.zeros_like(state)

        if na:
            start, forward, finish = _gather_phases(refs[8:8 + na], refs[14 + na:14 + 2 * na], *refs[15 + 2 * na:])
            pl.when(step == 0)(start)
            pl.when(step == ns // 2)(forward)
            pl.when(step == ns - 1)(finish)

        st = state[...]
        for c in range(DELTA_CHUNKS):
            rf, rr = _chunk_rows(c)
            q, k, v = _head_blocks(qf, qr, rf, rr), _head_blocks(kf, kr, rf, rr), _head_blocks(vf, vr, rf, rr)
            cm = _chunk_common2(q, k, v, gf[rf, :], gr[rr, :])
            tinv = _unit_triangular_inverse(cm["lm"], cm["eye"])
            u = _bdot(tinv, cm["vb"])
            w = _bdot(tinv, cm["kbeg"])
            v_new = u - _bdot(w, st)
            o = _bdot(cm["qg"], st) + _bdot(cm["intra"], v_new)
            for h in range(GDN_HEADS):
                of_ref[rf, h * hd:(h + 1) * hd] = o[h]
                or_ref[rr, h * hd:(h + 1) * hd] = o[GDN_HEADS + h]
            sf_all[c] = st[:GDN_HEADS]
            sr_all[DELTA_CHUNKS - 1 - c] = st[GDN_HEADS:]
            tf_all[c] = tinv[:GDN_HEADS]
            tr_all[DELTA_CHUNKS - 1 - c] = tinv[GDN_HEADS:]
            st = st * jnp.exp(cm["gl"]) + _bdot(cm["kdec"], v_new, BTN)
        state[...] = st

    rows = DELTA_CHUNKS * CHUNK
    ns = nc // DELTA_CHUNKS

    def col(j, rev):
        return pl.BlockSpec((GDN_HEADS, rows, hd), (lambda n: (j, ns - 1 - n, 0)) if rev else (lambda n: (j, n, 0)))

    def out(rev):
        return pl.BlockSpec((rows, GDN_WIDTH), (lambda n: (ns - 1 - n, 0)) if rev else (lambda n: (n, 0)))

    def gate(rev):
        return pl.BlockSpec((rows, LANE), (lambda n: (ns - 1 - n, 0)) if rev else (lambda n: (n, 0)))

    def per_chunk(d1, d2, rev):
        return pl.BlockSpec((DELTA_CHUNKS, GDN_HEADS, d1, d2),
                            (lambda n: (ns - 1 - n, 0, 0, 0)) if rev else (lambda n: (n, 0, 0, 0)))

    assert nc % DELTA_CHUNKS == 0 and (na == 0 or ns >= 4)
    res = pl.pallas_call(
        body, name="delta_fwd", grid=(ns,),
        in_specs=[col(0, False), col(1, False), col(2, False), gate(False), col(0, True), col(1, True), col(2, True), gate(True)]
        + [ANY] * na,
        out_specs=[out(False), out(True), per_chunk(hd, hd, False), per_chunk(hd, hd, True),
                   per_chunk(CHUNK, CHUNK, False), per_chunk(CHUNK, CHUNK, True)] + [ANY] * na,
        out_shape=[jax.ShapeDtypeStruct((s, GDN_WIDTH), F32)] * 2 + [jax.ShapeDtypeStruct((nc, GDN_HEADS, hd, hd), F32)] * 2
        + [jax.ShapeDtypeStruct((nc, GDN_HEADS, CHUNK, CHUNK), F32)] * 2
        + [jax.ShapeDtypeStruct((N_DEV,) + v.shape, v.dtype) for v in gather],
        scratch_shapes=[pltpu.VMEM((NB, hd, hd), F32)] + (_gather_semaphores(na) if na else []),
        compiler_params=_params("arbitrary"),
    )(y, y, y, gb, y, y, y, gb, *gather)
    return res[:6], res[6:]


def _delta_bwd2(y, gb, do, sf_all, sr_all, tf_all, tr_all, scatter=()):
    s = y.shape[1]
    nc = s // CHUNK
    hd = GDN_HEAD_DIM
    na = len(scatter)

    def body(*refs):
        qf, kf, vf, gf, dof, sf, tf, qr, kr, vr, gr, dor, sr, tr = refs[:14]
        dyf_ref, dyr_ref, dgf_ref, dgr_ref = refs[14 + na:18 + na]
        dstate = refs[18 + 2 * na]
        step = pl.program_id(0)

        @pl.when(step == 0)
        def _():
            dstate[...] = jnp.zeros_like(dstate)

        if na:
            start, finish = _scatter_phases(refs[14:14 + na], refs[18 + na:18 + 2 * na], *refs[19 + 2 * na:])
            pl.when(step == 0)(start)
            pl.when(step == ns - 1)(finish)

        def one_chunk(c, ds_out):
            rr, rf = _chunk_rows(c)
            cf, cr = DELTA_CHUNKS - 1 - c, c
            q, k, v = _head_blocks(qf, qr, rf, rr), _head_blocks(kf, kr, rf, rr), _head_blocks(vf, vr, rf, rr)
            dov = _heads(dof, dor, rf, rr)
            cm = _chunk_common2(q, k, v, gf[rf, :], gr[rr, :])
            tinv = jnp.concatenate([tf[cf], tr[cr]], axis=0)
            st = jnp.concatenate([sf[cf], sr[cr]], axis=0)
            decay, lm, intra, qg, kdec, kbeg, eg, kb, beta = (
                cm[n] for n in ("decay", "lm", "intra", "qg", "kdec", "kbeg", "eg", "kb", "beta"))
            u = _bdot(tinv, cm["vb"])
            w = _bdot(tinv, kbeg)
            v_new = u - _bdot(w, st)
            egl = jnp.exp(cm["gl"])
            d_qg = _bdot(dov, st, BNT)
            d_intra = _bdot(dov, v_new, BNT)
            dv_new = _bdot(intra, dov, BTN) + _bdot(kdec, ds_out)
            d_kdec = _bdot(v_new, ds_out, BNT)
            ds_in = _bdot(qg, dov, BTN) + egl * ds_out - _bdot(w, dv_new, BTN)
            dgl = egl * jnp.sum(jnp.sum(st * ds_out, axis=2, keepdims=True), axis=1, keepdims=True)
            dw = -_bdot(dv_new, st, BNT)
            dvb = _bdot(tinv, dv_new, BTN)
            dkbeg = _bdot(tinv, dw, BTN)
            dlm = jnp.where(cm["strict"], -(_bdot(dvb, u, BNT) + _bdot(dkbeg, w, BNT)), 0.0)
            d_a = dlm * decay
            d_qk = d_intra * decay
            e = dlm * lm + d_intra * intra
            colsum = _dot3(e, jnp.ones((NB, CHUNK, LANE), F32), BTN, exact_b=True)[:, :, 0:1]
            dgc = jnp.sum(e, axis=2, keepdims=True) - colsum
            dkb = _bdot(d_a, k) + dkbeg * eg
            dk = _bdot(d_a, kb, BTN) + _bdot(d_qk, q, BTN)
            dq = _bdot(d_qk, k) + d_qg * eg
            dgc = dgc + jnp.sum(d_qg * qg, axis=2, keepdims=True) + jnp.sum(dkbeg * kbeg, axis=2, keepdims=True)
            tdec = jnp.sum(d_kdec * kdec, axis=2, keepdims=True)
            dk = dk + d_kdec * cm["edec"] + dkb * beta
            dgc = dgc - tdec
            dgl = dgl + jnp.sum(tdec, axis=1, keepdims=True)
            dbeta = jnp.sum(dvb * v, axis=2, keepdims=True) + jnp.sum(dkb * k, axis=2, keepdims=True)
            dv = dvb * beta
            lane = lax.broadcasted_iota(jnp.int32, (CHUNK, LANE), 1)
            for rev, dy_ref, dg_ref, rows in ((False, dyf_ref, dgf_ref, rf), (True, dyr_ref, dgr_ref, rr)):
                dgc_tile = jnp.zeros((CHUNK, LANE), F32)
                rest = jnp.zeros((CHUNK, LANE), F32)
                for h in range(GDN_HEADS):
                    b = (GDN_HEADS if rev else 0) + h
                    gi, bi = _gate_lanes(rev, h)
                    dgc_tile = dgc_tile + jnp.where(lane == gi, dgc[b], 0.0)
                    rest = rest + jnp.where(lane == gi, dgl[b], 0.0) + jnp.where(lane == bi, dbeta[b], 0.0)
                    dy_ref[h, rows, :] = dq[b]
                    dy_ref[GDN_HEADS + h, rows, :] = dk[b]
                    dy_ref[2 * GDN_HEADS + h, rows, :] = dv[b]
                le_t = _chunk_masks(not rev)[0].astype(F32)
                dg_ref[rows, :] = _dot3(le_t, dgc_tile, NN, exact_a=True) + rest
            return ds_in

        ds = dstate[...]
        for c in range(DELTA_CHUNKS):
            ds = one_chunk(c, ds)
        dstate[...] = ds

    rows_per_step = DELTA_CHUNKS * CHUNK
    ns = nc // DELTA_CHUNKS

    def col(j, rev, blocks=GDN_HEADS):
        return pl.BlockSpec((blocks, rows_per_step, hd), (lambda n: (j, n, 0)) if rev else (lambda n: (j, ns - 1 - n, 0)))

    def wide(width, rev):
        return pl.BlockSpec((rows_per_step, width), (lambda n: (n, 0)) if rev else (lambda n: (ns - 1 - n, 0)))

    def per_chunk(d1, d2, rev):
        return pl.BlockSpec((DELTA_CHUNKS, GDN_HEADS, d1, d2),
                            (lambda n: (n, 0, 0, 0)) if rev else (lambda n: (ns - 1 - n, 0, 0, 0)))

    def side(rev):
        return [col(0, rev), col(1, rev), col(2, rev), wide(LANE, rev), wide(GDN_WIDTH, rev), per_chunk(hd, hd, rev),
                per_chunk(CHUNK, CHUNK, rev)]

    assert nc % DELTA_CHUNKS == 0 and (na == 0 or ns >= 2)
    res = pl.pallas_call(
        body, name="delta_bwd", grid=(ns,),
        in_specs=side(False) + side(True) + [ANY] * na,
        out_specs=[col(0, False, 3 * GDN_HEADS), col(0, True, 3 * GDN_HEADS), wide(LANE, False), wide(LANE, True)]
        + [ANY] * na,
        out_shape=[jax.ShapeDtypeStruct((3 * GDN_HEADS, s, hd), F32)] * 2 + [jax.ShapeDtypeStruct((s, LANE), F32)] * 2
        + [jax.ShapeDtypeStruct(g.shape, g.dtype) for g in scatter],
        scratch_shapes=[pltpu.VMEM((NB, hd, hd), F32)] + (_gather_semaphores(na) if na else []),
        compiler_params=_params("arbitrary"),
    )(y, y, y, gb, do, sf_all, tf_all, y, y, y, gb, do, sr_all, tr_all, *scatter)
    return res[:4], res[4:]


def _gdn_post_fwd(o_f, o_r, p_pad, norm_row):
    s = o_f.shape[0]
    tm = min(512, s)
    hd = GDN_HEAD_DIM

    def body(of_ref, or_ref, z_ref, w_ref, out_ref, osum_ref):
        o = of_ref[...] + or_ref[...]
        osum_ref[...] = o
        z = z_ref[...]
        gate = z * _sigmoid(z)
        for h in range(GDN_HEADS):
            sl = slice(h * hd, (h + 1) * hd)
            oh = o[:, sl]
            r = lax.rsqrt(jnp.mean(oh * oh, axis=-1, keepdims=True) + EPS)
            out_ref[:, sl] = (oh * r * w_ref[...] * gate[:, sl]).astype(BF16)

    blk = pl.BlockSpec((tm, GDN_WIDTH), lambda i: (i, 0))
    return pl.pallas_call(
        body, name="gdn_post_fwd", grid=(s // tm,),
        in_specs=[blk, blk, pl.BlockSpec((tm, GDN_WIDTH), lambda i: (i, OFF_Z // GDN_WIDTH)),
                  pl.BlockSpec((1, hd), lambda i: (0, 0))],
        out_specs=[blk, blk],
        out_shape=[jax.ShapeDtypeStruct((s, GDN_WIDTH), BF16), jax.ShapeDtypeStruct((s, GDN_WIDTH), F32)],
        compiler_params=_params("parallel"),
    )(o_f, o_r, p_pad, norm_row)


def _gdn_post_bwd(d_out, o_sum, p_pad, norm_row):
    s = o_sum.shape[0]
    tm = min(512, s)
    hd = GDN_HEAD_DIM

    def body(d_ref, o_ref, z_ref, w_ref, do_ref, dz_ref, dw_ref):
        @pl.when(pl.program_id(0) == 0)
        def _():
            dw_ref[...] = jnp.zeros_like(dw_ref)

        z = z_ref[...]
        sg = _sigmoid(z)
        gate = z * sg
        dgate = sg * (1.0 + z * (1.0 - sg))
        wv = w_ref[...]
        dw = jnp.zeros((1, hd), F32)
        for h in range(GDN_HEADS):
            sl = slice(h * hd, (h + 1) * hd)
            oh = o_ref[:, sl]
            dh = d_ref[:, sl]
            r = lax.rsqrt(jnp.mean(oh * oh, axis=-1, keepdims=True) + EPS)
            ohat = oh * r
            dz_ref[:, sl] = (dh * ohat * wv * dgate[:, sl]).astype(BF16)
            drn = dh * gate[:, sl]
            t = drn * wv
            do_ref[:, sl] = r * (t - ohat * jnp.mean(t * ohat, axis=-1, keepdims=True))
            dw = dw + jnp.sum(drn * ohat, axis=0, keepdims=True)
        dw_ref[...] += dw

    blk = pl.BlockSpec((tm, GDN_WIDTH), lambda i: (i, 0))
    vec = pl.BlockSpec((1, hd), lambda i: (0, 0))
    return pl.pallas_call(
        body, name="gdn_post_bwd", grid=(s // tm,),
        in_specs=[blk, blk, pl.BlockSpec((tm, GDN_WIDTH), lambda i: (i, OFF_Z // GDN_WIDTH)), vec],
        out_specs=[blk, pl.BlockSpec((tm, GDN_WIDTH), lambda i: (i, OFF_Z // GDN_WIDTH)), vec],
        out_shape=[jax.ShapeDtypeStruct((s, GDN_WIDTH), F32), jax.ShapeDtypeStruct((s, N_PAD), BF16),
                   jax.ShapeDtypeStruct((1, hd), F32)],
        compiler_params=_params("arbitrary"),
    )(d_out, o_sum, p_pad, norm_row)


def _gdn_forward(p_pad, conv_wt, alog_row, dt_row, norm_row, gather=()):
    c_pre, y = _gdn_conv_fwd(p_pad, conv_wt)
    gb = _gdn_gates_fwd(p_pad, alog_row, dt_row)
    (o_f, o_r, s_f, s_r, t_f, t_r), gathered = _delta_fwd2(y, gb, gather)
    out, o_sum = _gdn_post_fwd(o_f, o_r, p_pad, norm_row)
    return out, (c_pre, y, gb, s_f, t_f, s_r, t_r, o_sum), gathered


def _gdn_backward(d_out, p_pad, conv_wt, alog_row, dt_row, norm_row, saved, scatter=()):
    c_pre, y, gb, s_f, t_f, s_r, t_r, o_sum = saved
    do, dp_all, dnorm = _gdn_post_bwd(d_out, o_sum, p_pad, norm_row)
    (dy_f, dy_r, dgb_f, dgb_r), received = _delta_bwd2(y, gb, do, s_f, s_r, t_f, t_r, scatter)
    dp_all, dconv = _gdn_conv_bwd(dy_f, dy_r, c_pre, p_pad, conv_wt, dp_all)
    dp_all, gate_sums = _gdn_gates_bwd(dgb_f, dgb_r, p_pad, gb, alog_row, dt_row, dp_all)
    return dp_all, dconv, gate_sums, dnorm, received


ATT_BK = ATT_BQ + 2 * ATT_HALO
ATT_SUB = 8
SWA_SCALE = SWA_HEAD_DIM ** -0.5


def _t5_bucket(rel):
    nb = REL_BUCKETS // 2
    bucket = (rel > 0).astype(np.int32) * nb
    n = np.abs(rel)
    max_exact = nb // 2
    large = max_exact + (np.log(np.maximum(n, 1) / max_exact)
                         / math.log(REL_MAX_DISTANCE / max_exact) * (nb - max_exact)).astype(np.int32)
    large = np.minimum(large, nb - 1)
    return (bucket + np.where(n < max_exact, n, large)).astype(np.int32)


def _band_tables(dilation, queries_are_rows_of_block):
    blk = np.arange(ATT_BQ)
    band = np.arange(ATT_BK) - ATT_HALO
    if queries_are_rows_of_block:
        rel = band[None, :] - blk[:, None]
        band_idx = np.broadcast_to(np.arange(ATT_BK)[None, :], rel.shape)
    else:
        rel = blk[None, :] - band[:, None]
        band_idx = np.broadcast_to(np.arange(ATT_BK)[:, None], rel.shape)
    base = np.abs(rel) <= ATT_HALO
    not_prev = band_idx >= ATT_HALO
    not_next = band_idx < ATT_HALO + ATT_BQ
    valid = np.stack([base & not_prev, base, base & not_next, base & not_prev & not_next])
    return valid, _t5_bucket(rel * dilation)


def _bias_tiles(rel_bias, dilation, queries_are_rows_of_block):
    valid, bucket = _band_tables(dilation, queries_are_rows_of_block)
    onehot = (jnp.asarray(bucket.reshape(-1, 1)) == jnp.arange(REL_BUCKETS, dtype=jnp.int32)[None, :]).astype(F32)
    rb = jnp.dot(onehot, rel_bias.astype(F32), precision=lax.Precision.HIGHEST)
    rb = rb.T.reshape((SWA_HEADS,) + bucket.shape)
    return jnp.where(valid[:, None], rb[None], NEG_BIG).astype(F32)


def _group_sum(x, bd):
    hi = x.astype(BF16)
    lo = (x - hi.astype(F32)).astype(BF16)
    return jnp.dot(hi, bd, preferred_element_type=F32) + jnp.dot(lo, bd, preferred_element_type=F32)


def _head_block_diag():
    idx = np.arange(SWA_WIDTH) // SWA_HEAD_DIM
    return jnp.asarray(idx[:, None] == idx[None, :], BF16)


def _swa_pre_fwd(p_pad, qw_row, kw_row, bd):
    s = p_pad.shape[0]
    tm = min(512, s)
    inv = 1.0 / SWA_HEAD_DIM

    def body(q_ref, k_ref, v_ref, qw_ref, kw_ref, bd_ref, qo_ref, ko_ref, vo_ref):
        bdv = bd_ref[...]
        q = q_ref[...]
        k = k_ref[...]
        rq = lax.rsqrt(_group_sum(q * q, bdv) * inv + EPS)
        rk = lax.rsqrt(_group_sum(k * k, bdv) * inv + EPS)
        qo_ref[...] = (q * rq * qw_ref[...] * SWA_SCALE).astype(BF16)
        ko_ref[...] = (k * rk * kw_ref[...]).astype(BF16)
        vo_ref[...] = v_ref[...].astype(BF16)

    base = OFF_B // SWA_WIDTH
    blk = pl.BlockSpec((tm, SWA_WIDTH), lambda i: (i, 0))
    vec = pl.BlockSpec((1, SWA_WIDTH), lambda i: (0, 0))
    return pl.pallas_call(
        body, name="swa_pre_fwd", grid=(s // tm,),
        in_specs=[pl.BlockSpec((tm, SWA_WIDTH), lambda i: (i, base)), pl.BlockSpec((tm, SWA_WIDTH), lambda i: (i, base + 1)),
                  pl.BlockSpec((tm, SWA_WIDTH), lambda i: (i, base + 2)), vec, vec,
                  pl.BlockSpec((SWA_WIDTH, SWA_WIDTH), lambda i: (0, 0))],
        out_specs=[blk, blk, blk],
        out_shape=[jax.ShapeDtypeStruct((s, SWA_WIDTH), BF16)] * 3,
        compiler_params=_params("parallel"),
    )(p_pad, p_pad, p_pad, qw_row, kw_row, bd)


def _swa_pre_bwd(dqs, dks, dvs, p_pad, qw_row, kw_row, bd, dp_all):
    s = p_pad.shape[0]
    tm = min(256, s)
    inv = 1.0 / SWA_HEAD_DIM
    npat = len(dqs)

    def body(*refs):
        dq_refs, dk_refs, dv_refs = refs[:npat], refs[npat:2 * npat], refs[2 * npat:3 * npat]
        q_ref, k_ref, qw_ref, kw_ref, bd_ref, _, dp_ref, dqw_ref, dkw_ref = refs[3 * npat:]

        @pl.when(pl.program_id(0) == 0)
        def _():
            dqw_ref[...] = jnp.zeros_like(dqw_ref)
            dkw_ref[...] = jnp.zeros_like(dkw_ref)

        bdv = bd_ref[...]

        def norm_bwd(x, g, w, scale):
            r = lax.rsqrt(_group_sum(x * x, bdv) * inv + EPS)
            xhat = x * r
            t = g * w * scale
            dx = r * (t - xhat * (_group_sum(t * xhat, bdv) * inv))
            return dx, jnp.sum(g * scale * xhat, axis=0, keepdims=True)

        def total(rs):
            t = rs[0][...].astype(F32)
            for r in rs[1:]:
                t = t + r[...].astype(F32)
            return t

        dq, dqw = norm_bwd(q_ref[...], total(dq_refs), qw_ref[...], SWA_SCALE)
        dk, dkw = norm_bwd(k_ref[...], total(dk_refs), kw_ref[...], 1.0)
        dp_ref[:, 0:SWA_WIDTH] = dq.astype(BF16)
        dp_ref[:, SWA_WIDTH:2 * SWA_WIDTH] = dk.astype(BF16)
        dp_ref[:, 2 * SWA_WIDTH:3 * SWA_WIDTH] = total(dv_refs).astype(BF16)
        dqw_ref[...] += dqw
        dkw_ref[...] += dkw

    base = OFF_B // SWA_WIDTH
    blk = pl.BlockSpec((tm, SWA_WIDTH), lambda i: (i, 0))
    vec = pl.BlockSpec((1, SWA_WIDTH), lambda i: (0, 0))
    return pl.pallas_call(
        body, name="swa_pre_bwd", grid=(s // tm,),
        in_specs=[blk] * (3 * npat) + [pl.BlockSpec((tm, SWA_WIDTH), lambda i: (i, base)),
                                      pl.BlockSpec((tm, SWA_WIDTH), lambda i: (i, base + 1)), vec, vec,
                                      pl.BlockSpec((SWA_WIDTH, SWA_WIDTH), lambda i: (0, 0)), ANY],
        out_specs=[pl.BlockSpec((tm, 3 * SWA_WIDTH), lambda i: (i, OFF_B // (3 * SWA_WIDTH))), vec, vec],
        out_shape=[jax.ShapeDtypeStruct(dp_all.shape, dp_all.dtype), jax.ShapeDtypeStruct((1, SWA_WIDTH), F32),
                   jax.ShapeDtypeStruct((1, SWA_WIDTH), F32)],
        input_output_aliases={3 * npat + 5: 0},
        compiler_params=_params("arbitrary"),
    )(*dqs, *dks, *dvs, p_pad, p_pad, qw_row, kw_row, bd, dp_all)


def _band_specs(length, rows):
    per = rows // ATT_HALO
    last = length // ATT_HALO - 1
    prev = pl.BlockSpec((ATT_HALO, SWA_WIDTH), lambda r, t: (jnp.maximum(t * per - 1, 0), r))
    cur = pl.BlockSpec((rows, SWA_WIDTH), lambda r, t: (t, r))
    nxt = pl.BlockSpec((ATT_HALO, SWA_WIDTH), lambda r, t: (jnp.minimum((t + 1) * per, last), r))
    return [prev, cur, nxt]


def _tile_variant(t, nb, u, sub):
    first, last = u == 0, u == sub - 1
    if first and last:
        return 3 if nb == 1 else jnp.where(t == 0, 0, jnp.where(t == nb - 1, 2, 1))
    if first:
        return jnp.where(t == 0, 0, 1)
    if last:
        return jnp.where(t == nb - 1, 2, 1)
    return 1


def _bias_specs(nb, sub, rows, cols):
    return [pl.BlockSpec((1, SWA_HEADS, rows, cols),
                         functools.partial(lambda r, t, u: (_tile_variant(t, nb, u, sub), 0, 0, 0), u=u))
            for u in range(sub)]


def _band(refs):
    return jnp.concatenate([r[...] for r in refs], axis=0)


def _sub(u, width=ATT_BQ):
    return slice(u * ATT_BQ, u * ATT_BQ + width)


N_PAIRS = SWA_HEADS // 2


def _pairs(x):
    return jnp.stack([x[:, LANE * p:LANE * (p + 1)] for p in range(N_PAIRS)])


def _per_head_rows(x):
    first = lax.broadcasted_iota(jnp.int32, x.shape, 2) < SWA_HEAD_DIM
    zero = jnp.zeros_like(x)
    return jnp.concatenate([jnp.where(first, x, zero), jnp.where(first, zero, x)], axis=1)


def _per_head_cols(x):
    return jnp.stack([jnp.concatenate([x[:, LANE * p:LANE * p + 1],
                                       x[:, LANE * p + SWA_HEAD_DIM:LANE * p + SWA_HEAD_DIM + 1]], axis=0)
                      for p in range(N_PAIRS)])


def _merge_heads(x, rows):
    first = lax.broadcasted_iota(jnp.int32, (N_PAIRS, rows, LANE), 2) < SWA_HEAD_DIM
    return jnp.where(first, x[:, :rows], x[:, rows:])


def _store_pairs(ref, x, rows):
    for p in range(N_PAIRS):
        ref[rows, LANE * p:LANE * (p + 1)] = x[p].astype(ref.dtype)


def _att_fwd2(q, k, v, bias, dilation):
    s = q.shape[0]
    length = s // dilation
    sub = min(ATT_SUB, length // ATT_BQ)
    rows = sub * ATT_BQ
    nb = length // rows
    view = (length, dilation * SWA_WIDTH)

    def body(q_ref, kp, kc, kn, vp, vc, vn, *rest):
        b_refs, (o_ref, lse_ref) = rest[:sub], rest[sub:]
        kwin, vwin = _band((kp, kc, kn)), _band((vp, vc, vn))
        for u in range(sub):
            kb, vb = _pairs(kwin[_sub(u, ATT_BK)]), _pairs(vwin[_sub(u, ATT_BK)])
            qm = _per_head_rows(_pairs(q_ref[_sub(u), :]))
            sc = _bdot(qm, kb, BNT) + b_refs[u][0].reshape(N_PAIRS, 2 * ATT_BQ, ATT_BK)
            m = jnp.max(sc, axis=-1, keepdims=True)
            p = jnp.exp(sc - m)
            den = jnp.sum(p, axis=-1, keepdims=True)
            o = _bdot(p, vb) / den
            _store_pairs(o_ref, _merge_heads(o, ATT_BQ), _sub(u))
            lse = jnp.broadcast_to(m + jnp.log(den), (N_PAIRS, 2 * ATT_BQ, LANE))
            _store_pairs(lse_ref, _merge_heads(lse, ATT_BQ), _sub(u))

    cur = pl.BlockSpec((rows, SWA_WIDTH), lambda r, t: (t, r))
    o, lse = pl.pallas_call(
        body, name=f"att_fwd_d{dilation}", grid=(dilation, nb),
        in_specs=[cur] + _band_specs(length, rows) * 2 + _bias_specs(nb, sub,ATT_BQ, ATT_BK),
        out_specs=[cur, cur],
        out_shape=[jax.ShapeDtypeStruct(view, BF16), jax.ShapeDtypeStruct(view, F32)],
        compiler_params=_params("parallel", "parallel"),
    )(q.reshape(view), *([k.reshape(view)] * 3), *([v.reshape(view)] * 3), *([bias] * sub))
    return o.reshape(s, SWA_WIDTH), lse.reshape(s, SWA_WIDTH)


def _att_dq2(q, k, v, dop, lse, cp, bias, dilation):
    s = q.shape[0]
    length = s // dilation
    sub = min(ATT_SUB, length // ATT_BQ)
    rows = sub * ATT_BQ
    nb = length // rows
    view = (length, dilation * SWA_WIDTH)

    def body(q_ref, kp, kc, kn, vp, vc, vn, do_ref, lse_ref, cp_ref, *rest):
        b_refs, (dq_ref, db_ref) = rest[:sub], rest[sub:]

        @pl.when((pl.program_id(0) == 0) & (pl.program_id(1) == 0))
        def _():
            db_ref[...] = jnp.zeros_like(db_ref)

        kwin, vwin = _band((kp, kc, kn)), _band((vp, vc, vn))
        for u in range(sub):
            kb, vb = _pairs(kwin[_sub(u, ATT_BK)]), _pairs(vwin[_sub(u, ATT_BK)])
            qm = _per_head_rows(_pairs(q_ref[_sub(u), :]))
            dom = _per_head_rows(_pairs(do_ref[_sub(u), :]))
            sc = _bdot(qm, kb, BNT) + b_refs[u][0].reshape(N_PAIRS, 2 * ATT_BQ, ATT_BK)
            p = jnp.exp(sc - _per_head_cols(lse_ref[_sub(u), :]))
            ds = p * (_bdot(dom, vb, BNT) + _per_head_cols(cp_ref[_sub(u), :]))
            _store_pairs(dq_ref, _merge_heads(_bdot(ds, kb), ATT_BQ), _sub(u))
            db_ref[_tile_variant(pl.program_id(1), nb, u, sub)] += ds.reshape(SWA_HEADS, ATT_BQ, ATT_BK)

    cur = pl.BlockSpec((rows, SWA_WIDTH), lambda r, t: (t, r))
    dq, db = pl.pallas_call(
        body, name=f"att_dq_d{dilation}", grid=(dilation, nb),
        in_specs=[cur] + _band_specs(length, rows) * 2 + [cur, cur, cur] + _bias_specs(nb, sub,ATT_BQ, ATT_BK),
        out_specs=[cur, pl.BlockSpec((4, SWA_HEADS, ATT_BQ, ATT_BK), lambda r, t: (0, 0, 0, 0))],
        out_shape=[jax.ShapeDtypeStruct(view, BF16), jax.ShapeDtypeStruct((4, SWA_HEADS, ATT_BQ, ATT_BK), F32)],
        compiler_params=_params("arbitrary", "arbitrary"),
    )(q.reshape(view), *([k.reshape(view)] * 3), *([v.reshape(view)] * 3), dop.reshape(view), lse.reshape(view),
      cp.reshape(view), *([bias] * sub))
    return dq.reshape(s, SWA_WIDTH), db


def _att_dkv2(q, k, v, dop, lse, cp, bias_t, dilation):
    s = q.shape[0]
    length = s // dilation
    sub = min(ATT_SUB, length // ATT_BQ)
    rows = sub * ATT_BQ
    nb = length // rows
    view = (length, dilation * SWA_WIDTH)

    def body(k_ref, v_ref, qp, qc, qn, dp_, dc_, dn_, lp, lc, ln, cp_, cc_, cn_, *rest):
        b_refs, (dk_ref, dv_ref) = rest[:sub], rest[sub:]
        qwin, dowin = _band((qp, qc, qn)), _band((dp_, dc_, dn_))
        lsewin, cpwin = _band((lp, lc, ln)), _band((cp_, cc_, cn_))
        for u in range(sub):
            band = _sub(u, ATT_BK)
            qm = _per_head_rows(_pairs(qwin[band]))
            dom = _per_head_rows(_pairs(dowin[band]))
            kv, vv = _pairs(k_ref[_sub(u), :]), _pairs(v_ref[_sub(u), :])
            sc = _bdot(qm, kv, BNT) + b_refs[u][0].reshape(N_PAIRS, 2 * ATT_BK, ATT_BQ)
            p = jnp.exp(sc - _per_head_cols(lsewin[band]))
            _store_pairs(dv_ref, _bdot(p, dom, BTN), _sub(u))
            ds = p * (_bdot(dom, vv, BNT) + _per_head_cols(cpwin[band]))
            _store_pairs(dk_ref, _bdot(ds, qm, BTN), _sub(u))

    cur = pl.BlockSpec((rows, SWA_WIDTH), lambda r, t: (t, r))
    dk, dv = pl.pallas_call(
        body, name=f"att_dkv_d{dilation}", grid=(dilation, nb),
        in_specs=[cur, cur] + _band_specs(length, rows) * 4 + _bias_specs(nb, sub,ATT_BK, ATT_BQ),
        out_specs=[cur, cur],
        out_shape=[jax.ShapeDtypeStruct(view, BF16)] * 2,
        compiler_params=_params("parallel", "parallel"),
    )(k.reshape(view), v.reshape(view), *([q.reshape(view)] * 3), *([dop.reshape(view)] * 3),
      *([lse.reshape(view)] * 3), *([cp.reshape(view)] * 3), *([bias_t] * sub))
    return dk.reshape(s, SWA_WIDTH), dv.reshape(s, SWA_WIDTH)


def _pattern_weights(lses):
    m = lses[0]
    for l in lses[1:]:
        m = jnp.maximum(m, l)
    es = [jnp.exp(l - m) for l in lses]
    den = es[0]
    for e in es[1:]:
        den = den + e
    return [e / den for e in es]


def _combine_fwd(outs, lses):
    s = outs[0].shape[0]
    tm = min(512, s)
    npat = len(outs)

    def body(*refs):
        ws = _pattern_weights([r[...] for r in refs[npat:2 * npat]])
        o = ws[0] * refs[0][...]
        for p in range(1, npat):
            o = o + ws[p] * refs[p][...]
        refs[2 * npat][...] = o.astype(BF16)

    blk = pl.BlockSpec((tm, SWA_WIDTH), lambda i: (i, 0))
    return pl.pallas_call(
        body, name="swa_combine_fwd", grid=(s // tm,), in_specs=[blk] * (2 * npat), out_specs=blk,
        out_shape=jax.ShapeDtypeStruct((s, SWA_WIDTH), BF16), compiler_params=_params("parallel"),
    )(*outs, *lses)


def _combine_bwd(d_out, outs, lses, bd):
    s = d_out.shape[0]
    tm = min(512, s)
    npat = len(outs)

    def body(*refs):
        d_ref, bd_ref = refs[0], refs[1 + 2 * npat]
        o_refs, l_refs = refs[1:1 + npat], refs[1 + npat:1 + 2 * npat]
        out_refs = refs[2 + 2 * npat:]
        ws = _pattern_weights([r[...] for r in l_refs])
        dov = d_ref[...]
        o = ws[0] * o_refs[0][...]
        for p in range(1, npat):
            o = o + ws[p] * o_refs[p][...]
        rd = _group_sum(dov * o, bd_ref[...])
        for p in range(npat):
            out_refs[p][...] = (ws[p] * dov).astype(BF16)
            out_refs[npat + p][...] = -ws[p] * rd

    blk = pl.BlockSpec((tm, SWA_WIDTH), lambda i: (i, 0))
    res = pl.pallas_call(
        body, name="swa_combine_bwd", grid=(s // tm,),
        in_specs=[blk] * (1 + 2 * npat) + [pl.BlockSpec((SWA_WIDTH, SWA_WIDTH), lambda i: (0, 0))],
        out_specs=[blk] * (2 * npat),
        out_shape=[jax.ShapeDtypeStruct((s, SWA_WIDTH), BF16)] * npat + [jax.ShapeDtypeStruct((s, SWA_WIDTH), F32)] * npat,
        compiler_params=_params("parallel"),
    )(d_out, *outs, *lses, bd)
    return res[:npat], res[npat:]


def _rel_bias_grad(dbs, buckets):
    npat = len(dbs)

    def body(*refs):
        db_refs, bk_refs, o_ref = refs[:npat], refs[npat:2 * npat], refs[2 * npat]
        row = lax.broadcasted_iota(jnp.int32, (REL_BUCKETS, LANE), 0)
        lane = lax.broadcasted_iota(jnp.int32, (REL_BUCKETS, LANE), 1)
        tiles = [[db_refs[p][0, h] + db_refs[p][1, h] + db_refs[p][2, h] + db_refs[p][3, h] for h in range(SWA_HEADS)]
                 for p in range(npat)]
        bks = [r[...] for r in bk_refs]

        def one_bucket(b, acc):
            for h in range(SWA_HEADS):
                tot = jnp.zeros((1, 1), F32)
                for p in range(npat):
                    sel = jnp.where(bks[p] == b, tiles[p][h], 0.0)
                    tot = tot + jnp.sum(jnp.sum(sel, axis=1, keepdims=True), axis=0, keepdims=True)
                acc = acc + jnp.where((row == b) & (lane == h), tot, 0.0)
            return acc

        o_ref[...] = lax.fori_loop(0, REL_BUCKETS, one_bucket, jnp.zeros((REL_BUCKETS, LANE), F32))

    full4 = pl.BlockSpec((4, SWA_HEADS, ATT_BQ, ATT_BK), lambda: (0, 0, 0, 0))
    full2 = pl.BlockSpec((ATT_BQ, ATT_BK), lambda: (0, 0))
    return pl.pallas_call(
        body, name="rel_bias_grad", in_specs=[full4] * npat + [full2] * npat,
        out_specs=pl.BlockSpec((REL_BUCKETS, LANE), lambda: (0, 0)),
        out_shape=jax.ShapeDtypeStruct((REL_BUCKETS, LANE), F32),
        compiler_params=pltpu.CompilerParams(vmem_limit_bytes=V7X_VMEM_LIMIT_BYTES),
    )(*dbs, *buckets)


def _swa_forward(p_pad, qw_row, kw_row, rel_bias, bd):
    q, k, v = _swa_pre_fwd(p_pad, qw_row, kw_row, bd)
    outs, lses = [], []
    for _, dil in DILATION_PATTERNS:
        o, lse = _att_fwd2(q, k, v, _bias_tiles(rel_bias, dil, True), dil)
        outs.append(o)
        lses.append(lse)
    return _combine_fwd(outs, lses), (q, k, v, outs, lses)


def _swa_backward(d_out, p_pad, qw_row, kw_row, rel_bias, bd, saved, dp_all):
    q, k, v, outs, lses = saved
    dops, cps = _combine_bwd(d_out, outs, lses, bd)
    dqs, dks, dvs, dbs, buckets = [], [], [], [], []
    for p, (_, dil) in enumerate(DILATION_PATTERNS):
        dq, db = _att_dq2(q, k, v, dops[p], lses[p], cps[p], _bias_tiles(rel_bias, dil, True), dil)
        dk, dv = _att_dkv2(q, k, v, dops[p], lses[p], cps[p], _bias_tiles(rel_bias, dil, False), dil)
        dqs.append(dq)
        dks.append(dk)
        dvs.append(dv)
        dbs.append(db)
        buckets.append(jnp.asarray(_band_tables(dil, True)[1]))
    dp, dqw, dkw = _swa_pre_bwd(dqs, dks, dvs, p_pad, qw_row, kw_row, bd, dp_all)
    return dp, dqw, dkw, _rel_bias_grad(dbs, buckets)


def _lane_row(v):
    flat = v.reshape(-1).astype(F32)
    return jnp.zeros((1, LANE), F32).at[0, :flat.shape[0]].set(flat)


W_IN_SHARD = N_IN // N_DEV
W_IN_RUNS = ((0, NAT_Z, 0), (NAT_Z, NAT_AB, OFF_Z), (NAT_AB, NAT_B, OFF_AB), (NAT_B, N_IN, OFF_B))


def _w_in_pieces(shard):
    lo, hi = shard * W_IN_SHARD, (shard + 1) * W_IN_SHARD
    out = []
    for first, last, dst in W_IN_RUNS:
        a, b = max(lo, first), min(hi, last)
        if a < b:
            out.append((a - lo, b - a, dst + a - first))
    return out


def _w_in_from_slabs(w3):
    nd, r, _ = w3.shape

    def body(w_ref, o_ref):
        o_ref[:, OFF_AB:N_PAD] = jnp.zeros((r, N_PAD - OFF_AB), w3.dtype)
        for sh in range(nd):
            for src, length, dst in _w_in_pieces(sh):
                o_ref[:, dst:dst + length] = w_ref[sh, :, src:src + length]

    return pl.pallas_call(
        body, name="w_in_from_slabs", out_shape=jax.ShapeDtypeStruct((r, N_PAD), w3.dtype),
        compiler_params=pltpu.CompilerParams(vmem_limit_bytes=V7X_VMEM_LIMIT_BYTES),
    )(w3)


def _w_in_grad_slabs(dw_pad, dtype):
    r = dw_pad.shape[0]

    def body(dw_ref, o_ref):
        for sh in range(N_DEV):
            for src, length, dst in _w_in_pieces(sh):
                o_ref[sh, :, src:src + length] = dw_ref[:, dst:dst + length].astype(dtype)

    return pl.pallas_call(
        body, name="w_in_grad_slabs", out_shape=jax.ShapeDtypeStruct((N_DEV, r, W_IN_SHARD), dtype),
        compiler_params=pltpu.CompilerParams(vmem_limit_bytes=V7X_VMEM_LIMIT_BYTES),
    )(dw_pad)


LATE = ("w_out", "ffn2_w_gate", "ffn2_w_up", "ffn2_w_down")
TRANSPOSED = ("ffn1_w_gate", "ffn1_w_up", "ffn2_w_gate", "ffn2_w_up")


def _late_weights(slabs):
    return {n: g.reshape(N_DEV * g.shape[1], g.shape[2]) for n, g in zip(LATE, slabs)}


def _local_step(x, tgt, wts, small, late_shards=None):
    bd = _head_block_diag()
    alog_row, dt_row = _lane_row(small["a_log"]), _lane_row(small["dt_bias"])
    gnorm_row = small["gdn_norm_w"].reshape(1, GDN_HEAD_DIM)
    qw_row = jnp.tile(small["q_norm_w"].reshape(-1), SWA_HEADS).reshape(1, SWA_WIDTH)
    kw_row = jnp.tile(small["k_norm_w"].reshape(-1), SWA_HEADS).reshape(1, SWA_WIDTH)
    rel_bias = small["rel_bias"]
    exchange = late_shards is not None
    dw_dtype = BF16 if exchange else F32

    x1, sv1, (wg1, wu1, wd1), got, first, (n2, r2) = _ffn_forward(
        x, small["ffn1_norm"], wts.get("ffn1_w_gate"), wts.get("ffn1_w_up"), wts.get("ffn1_w_down"), "ffn1",
        gather=[late_shards["ffn1_w_down"], late_shards["w_in"]] if exchange else (),
        head=[late_shards["ffn1_w_gate"], late_shards["ffn1_w_up"], late_shards["conv_w"]] if exchange else (),
        next_norm=small["mix_norm"])
    win_pad = _w_in_from_slabs(got[0]) if exchange else wts["w_in_pad"]
    conv_w = first[0].reshape(N_DEV, -1)[:, :QKV_A // N_DEV * CONV_WIDTH].reshape(QKV_A, CONV_WIDTH) if exchange \
        else small["conv_w"]
    conv_wt = jnp.zeros((8, QKV_A), F32).at[:CONV_WIDTH].set(conv_w.T)
    p_pad = _matmul([(n2, win_pad)], tm=256, tn=N_PAD, tk=D_MODEL, name="w_in")
    o_a, sva, gathered = _gdn_forward(p_pad, conv_wt, alog_row, dt_row, gnorm_row,
                                      gather=[late_shards[n] for n in LATE] if exchange else ())
    if exchange:
        wts = {**wts, **_late_weights(gathered)}
    wo_a, wo_b = wts["w_out"][:GDN_WIDTH], wts["w_out"][GDN_WIDTH:]
    o_b, svb = _swa_forward(p_pad, qw_row, kw_row, rel_bias, bd)
    x2, n3, r3 = _matmul([(o_a, wo_a), (o_b, wo_b)], tm=512, tn=D_MODEL, tk=GDN_WIDTH, name="w_out", res=x1,
                         norm_fwd=small["ffn2_norm"])
    (loss_row, dx3, d_final), sv2, _, _, _, _ = _ffn_forward(
        x2, small["ffn2_norm"], wts["ffn2_w_gate"], wts["ffn2_w_up"], wts["ffn2_w_down"], "ffn2", normed=(n3, r3),
        loss=(small["final_norm"], tgt))

    dx2, d_ffn2_norm, dwg2, dwu2, dwd2, _ = _ffn_backward(
        dx3, x2, small["ffn2_norm"], wts["ffn2_w_gate"], wts["ffn2_w_up"], wts["ffn2_w_down"], sv2, "ffn2", dw_dtype)
    d_oa = _matmul([(dx2, wo_a)], tb=True, tm=512, tn=GDN_WIDTH, tk=D_MODEL, name="w_out_da")
    d_ob = _matmul([(dx2, wo_b)], tb=True, tm=512, tn=SWA_WIDTH, tk=D_MODEL, name="w_out_db")
    dwo_a = _matmul([(o_a, dx2)], ta=True, tm=GDN_WIDTH, tn=D_MODEL, tk=2048, name="w_out_dwa", out_dtype=dw_dtype)
    dwo_b = _matmul([(o_b, dx2)], ta=True, tm=SWA_WIDTH, tn=D_MODEL, tk=2048, name="w_out_dwb", out_dtype=dw_dtype)

    late_grads = [_row_slabs(jnp.concatenate([dwo_a, dwo_b], axis=0)), dwg2, dwu2, dwd2]
    dp_all, dconv, gate_sums, d_gnorm, received = _gdn_backward(
        d_oa, p_pad, conv_wt, alog_row, dt_row, gnorm_row, sva, scatter=late_grads if exchange else ())
    if exchange:
        late_grads = received
    dp_all, dqw, dkw, d_rel = _swa_backward(d_ob, p_pad, qw_row, kw_row, rel_bias, bd, svb, dp_all)
    dw_pad = _matmul([(n2, dp_all)], ta=True, tm=D_MODEL, tn=N_PAD // 3, tk=2048, name="w_in_dw")
    dx1, d_mix_norm = _matmul([(dp_all, win_pad)], tb=True, tm=512, tn=D_MODEL, tk=N_PAD, name="w_in_dn",
                              norm_bwd=(x1, r2, small["mix_norm"], dx2))
    d_w_in = _w_in_grad_slabs(dw_pad, dw_dtype)
    dx, d_ffn1_norm, dwg1, dwu1, dwd1, got = _ffn_backward(
        dx1, x, small["ffn1_norm"], wg1, wu1, wd1, sv1, "ffn1", dw_dtype,
        scatter=[d_w_in] if exchange else None)
    if exchange:
        d_w_in = got[0]

    grads = {
        "ffn1_norm": d_ffn1_norm, "ffn1_w_gate": dwg1, "ffn1_w_up": dwu1, "ffn1_w_down": dwd1,
        "mix_norm": d_mix_norm, "w_in": d_w_in, "conv_w": dconv[:CONV_WIDTH].T,
        "a_log": gate_sums[0, :8].reshape(2, GDN_HEADS), "dt_bias": gate_sums[1, :8].reshape(2, GDN_HEADS),
        "gdn_norm_w": d_gnorm, "q_norm_w": dqw.reshape(SWA_HEADS, SWA_HEAD_DIM).sum(0, keepdims=True),
        "k_norm_w": dkw.reshape(SWA_HEADS, SWA_HEAD_DIM).sum(0, keepdims=True), "rel_bias": d_rel[:, :SWA_HEADS],
        "ffn2_norm": d_ffn2_norm, "final_norm": d_final, **dict(zip(LATE, late_grads)),
    }
    return loss_row, dx, grads


MESH_IDS = pl.DeviceIdType.MESH
ANY = pl.BlockSpec(memory_space=pl.ANY)


def _adamw(parts, w, m, v, name):
    nparts, r, n = parts.shape
    tr = r
    for cand in (256, 176, 128, 104, 64, 8):
        if r % cand == 0:
            tr = cand
            break
    bc1 = 1.0 - ADAM_B1 ** ADAM_STEP
    bc2 = 1.0 - ADAM_B2 ** ADAM_STEP

    def body(p_ref, w_ref, m_ref, v_ref, g_ref, d_ref, nm_ref, nv_ref):
        g = p_ref[0].astype(F32)
        for k in range(1, nparts):
            g = g + p_ref[k].astype(F32)
        mn = ADAM_B1 * m_ref[...] + (1.0 - ADAM_B1) * g
        vn = ADAM_B2 * v_ref[...] + (1.0 - ADAM_B2) * (g * g)
        m_hat = mn / bc1
        v_hat = vn / bc2
        g_ref[...] = g
        nm_ref[...] = mn
        nv_ref[...] = vn
        d_ref[...] = -ADAM_LR * (m_hat / (jnp.sqrt(v_hat) + ADAM_EPS) + ADAM_WD * w_ref[...])

    blk = pl.BlockSpec((tr, n), lambda i: (i, 0))
    return pl.pallas_call(
        body, name=name, grid=(r // tr,),
        in_specs=[pl.BlockSpec((nparts, tr, n), lambda i: (0, i, 0)), blk, blk, blk],
        out_specs=[blk] * 4, out_shape=[jax.ShapeDtypeStruct((r, n), F32)] * 4,
        compiler_params=_params("parallel"),
    )(parts, w, m, v)


def _mesh_place():
    x, y, c = lax.axis_index("x"), lax.axis_index("y"), lax.axis_index("c")
    return x, y, c, [(1 - x, y), (x, 1 - y), (1 - x, 1 - y)]


def _gather_phases(x_refs, out_refs, send_sems, recv_sems, local_sems):
    na = len(x_refs)

    def place():
        x, y, c, chips = _mesh_place()
        return (x, y, c), (x, y, 1 - c), chips, c

    def slab(i, px, py, pc):
        return out_refs[i].at[4 * px + 2 * py + pc]

    def copy(i, k, block, to, src=None):
        return pltpu.make_async_remote_copy(
            src_ref=slab(i, *block) if src is None else src, dst_ref=slab(i, *block),
            send_sem=send_sems.at[i, k], recv_sem=recv_sems.at[i, k], device_id=to, device_id_type=MESH_IDS)

    def own(i, me):
        return pltpu.make_async_copy(x_refs[i], slab(i, *me), local_sems.at[i])

    def sends(i, me, sibling, chips, c):
        return [copy(i, 0, me, sibling, src=x_refs[i])] + [copy(i, 1 + j, me, (*chip, c), src=x_refs[i])
                                                          for j, chip in enumerate(chips)]

    def start():
        me, sibling, chips, c = place()
        for i in range(na):
            own(i, me).start()
            for cp in sends(i, me, sibling, chips, c):
                cp.start()

    def forward():
        me, sibling, chips, c = place()
        for j, chip in enumerate(chips):
            for i in range(na):
                copy(i, 1 + j, (*chip, c), me).wait_recv()
                copy(i, 4 + j, (*chip, c), sibling).start()

    def finish():
        me, sibling, chips, c = place()
        for i in range(na):
            copy(i, 0, sibling, me).wait_recv()
        for j, chip in enumerate(chips):
            for i in range(na):
                copy(i, 4 + j, (*chip, 1 - c), me).wait_recv()
        for i in range(na):
            for cp in sends(i, me, sibling, chips, c):
                cp.wait_send()
            for j, chip in enumerate(chips):
                copy(i, 4 + j, (*chip, c), sibling).wait_send()
            own(i, me).wait()

    return start, forward, finish


def _gather_semaphores(na):
    return [pltpu.SemaphoreType.DMA((na, 7)), pltpu.SemaphoreType.DMA((na, 7)), pltpu.SemaphoreType.DMA((na,))]


def _scatter_phases(g_refs, out_refs, send_sems, recv_sems, local_sems):
    na = len(g_refs)

    def place(m):
        x, y, c = lax.axis_index("x"), lax.axis_index("y"), lax.axis_index("c")
        px = 1 - x if m & 4 else x
        py = 1 - y if m & 2 else y
        pc = 1 - c if m & 1 else c
        return 4 * x + 2 * y + c, (px, py, pc), 4 * px + 2 * py + pc

    def own(i):
        me, _, _ = place(0)
        return pltpu.make_async_copy(g_refs[i].at[me], out_refs[i].at[me], local_sems.at[i])

    def start():
        for i in range(na):
            own(i).start()
            for m in range(1, N_DEV):
                me, peer, peer_idx = place(m)
                pltpu.make_async_remote_copy(
                    src_ref=g_refs[i].at[peer_idx], dst_ref=out_refs[i].at[me], send_sem=send_sems.at[i, m - 1],
                    recv_sem=recv_sems.at[i, m - 1], device_id=peer, device_id_type=MESH_IDS).start()

    def finish():
        for i in range(na):
            for m in range(1, N_DEV):
                me, peer, peer_idx = place(m)
                cp = pltpu.make_async_remote_copy(
                    src_ref=g_refs[i].at[peer_idx], dst_ref=out_refs[i].at[peer_idx], send_sem=send_sems.at[i, m - 1],
                    recv_sem=recv_sems.at[i, m - 1], device_id=peer, device_id_type=MESH_IDS)
                cp.wait_recv()
                cp.wait_send()
            own(i).wait()

    return start, finish


def _all_gather_many(vs, name):
    na = len(vs)

    def body(*refs):
        x_refs, out_refs = refs[:na], refs[na:2 * na]
        for step in _gather_phases(x_refs, out_refs, *refs[2 * na:]):
            step()

    return pl.pallas_call(
        body, name=name, in_specs=[ANY] * na, out_specs=[ANY] * na,
        out_shape=[jax.ShapeDtypeStruct((N_DEV,) + v.shape, v.dtype) for v in vs],
        scratch_shapes=_gather_semaphores(na),
        compiler_params=pltpu.CompilerParams(vmem_limit_bytes=V7X_VMEM_LIMIT_BYTES),
    )(*vs)


BIG = ("ffn1_w_gate", "ffn1_w_up", "ffn1_w_down", "w_in", "w_out", "ffn2_w_gate", "ffn2_w_up", "ffn2_w_down")
SMALL = ("ffn1_norm", "mix_norm", "a_log", "dt_bias", "gdn_norm_w", "q_norm_w", "k_norm_w", "rel_bias",
         "ffn2_norm", "final_norm")
WEIGHTS = ("ffn1_norm", "ffn1_w_gate", "ffn1_w_up", "ffn1_w_down", "mix_norm", "w_in", "conv_w", "a_log", "dt_bias",
           "gdn_norm_w", "q_norm_w", "k_norm_w", "rel_bias", "w_out", "ffn2_norm", "ffn2_w_gate", "ffn2_w_up",
           "ffn2_w_down", "final_norm")


def _pack(arrays, width, row_multiple):
    flat = jnp.concatenate([a.reshape(-1) for a in arrays])
    rows = -(-flat.shape[0] // width)
    rows = -(-rows // row_multiple) * row_multiple
    return jnp.pad(flat, (0, rows * width - flat.shape[0])).reshape(rows, width)


def _unpack(packed, shapes):
    flat = packed.reshape(-1)
    out, pos = [], 0
    for shp in shapes:
        size = int(np.prod(shp))
        out.append(flat[pos:pos + size].reshape(shp))
        pos += size
    return out


def kernel(x, ffn1_norm, ffn1_w_gate, ffn1_w_up, ffn1_w_down, mix_norm, w_in, conv_w, a_log, dt_bias, gdn_norm_w, q_norm_w, k_norm_w, rel_bias, w_out, ffn2_norm, ffn2_w_gate, ffn2_w_up, ffn2_w_down, final_norm, loss_target, m_ffn1_norm, m_ffn1_w_gate, m_ffn1_w_up, m_ffn1_w_down, m_mix_norm, m_w_in, m_conv_w, m_a_log, m_dt_bias, m_gdn_norm_w, m_q_norm_w, m_k_norm_w, m_rel_bias, m_w_out, m_ffn2_norm, m_ffn2_w_gate, m_ffn2_w_up, m_ffn2_w_down, m_final_norm, v_ffn1_norm, v_ffn1_w_gate, v_ffn1_w_up, v_ffn1_w_down, v_mix_norm, v_w_in, v_conv_w, v_a_log, v_dt_bias, v_gdn_norm_w, v_q_norm_w, v_k_norm_w, v_rel_bias, v_w_out, v_ffn2_norm, v_ffn2_w_gate, v_ffn2_w_up, v_ffn2_w_down, v_final_norm):
    w = dict(ffn1_norm=ffn1_norm, ffn1_w_gate=ffn1_w_gate, ffn1_w_up=ffn1_w_up, ffn1_w_down=ffn1_w_down, mix_norm=mix_norm, w_in=w_in, conv_w=conv_w, a_log=a_log, dt_bias=dt_bias, gdn_norm_w=gdn_norm_w, q_norm_w=q_norm_w, k_norm_w=k_norm_w, rel_bias=rel_bias, w_out=w_out, ffn2_norm=ffn2_norm, ffn2_w_gate=ffn2_w_gate, ffn2_w_up=ffn2_w_up, ffn2_w_down=ffn2_w_down, final_norm=final_norm)
    mom = dict(ffn1_norm=m_ffn1_norm, ffn1_w_gate=m_ffn1_w_gate, ffn1_w_up=m_ffn1_w_up, ffn1_w_down=m_ffn1_w_down, mix_norm=m_mix_norm, w_in=m_w_in, conv_w=m_conv_w, a_log=m_a_log, dt_bias=m_dt_bias, gdn_norm_w=m_gdn_norm_w, q_norm_w=m_q_norm_w, k_norm_w=m_k_norm_w, rel_bias=m_rel_bias, w_out=m_w_out, ffn2_norm=m_ffn2_norm, ffn2_w_gate=m_ffn2_w_gate, ffn2_w_up=m_ffn2_w_up, ffn2_w_down=m_ffn2_w_down, final_norm=m_final_norm)
    var = dict(ffn1_norm=v_ffn1_norm, ffn1_w_gate=v_ffn1_w_gate, ffn1_w_up=v_ffn1_w_up, ffn1_w_down=v_ffn1_w_down, mix_norm=v_mix_norm, w_in=v_w_in, conv_w=v_conv_w, a_log=v_a_log, dt_bias=v_dt_bias, gdn_norm_w=v_gdn_norm_w, q_norm_w=v_q_norm_w, k_norm_w=v_k_norm_w, rel_bias=v_rel_bias, w_out=v_w_out, ffn2_norm=v_ffn2_norm, ffn2_w_gate=v_ffn2_w_gate, ffn2_w_up=v_ffn2_w_up, ffn2_w_down=v_ffn2_w_down, final_norm=v_final_norm)
    ix, iy, ic = lax.axis_index("x"), lax.axis_index("y"), lax.axis_index("c")
    me = 4 * ix + 2 * iy + ic

    def local(a, n):
        return jnp.swapaxes(a[0], 0, 1) if n in TRANSPOSED else a[0]

    shard = {n: local(w[n], n) for n in BIG}

    conv_shard_shape = w["conv_w"][0].shape
    small = {n: w[n][0] if n not in ("rel_bias",) else w[n] for n in SMALL}
    small = {n: (a.reshape(1, -1) if n.endswith("norm") else a) for n, a in small.items()}
    shards = {n: shard[n].astype(BF16) for n in BIG}
    shards["conv_w"] = _pack([w["conv_w"][0]], LANE, 8)
    loss_row, grad_x, grads = _local_step(x[0], loss_target[0], {}, small, late_shards=shards)

    big_out = [[], [], [], []]
    for n in BIG:
        for kind, val in enumerate(_adamw(grads[n], shard[n], local(mom[n], n), local(var[n], n), f"{n}_adamw")):
            big_out[kind].append(jnp.swapaxes(val, 0, 1) if n in TRANSPOSED else val)

    small_names = SMALL + ("conv_w",)
    small_shapes = [grads[n].shape for n in small_names] + [(1, 1)]
    g_small = _pack([grads[n] for n in small_names] + [loss_row[:, :1]], LANE, 8)
    all_small = _all_gather_many([g_small], "gather_small_grads")[0]
    riders = [jnp.zeros(shp, F32) for shp in small_shapes[len(SMALL):]]
    ws = _pack([w[n].reshape(grads[n].shape) for n in SMALL] + riders, LANE, 8)
    ms = _pack([mom[n].reshape(grads[n].shape) for n in SMALL] + riders, LANE, 8)
    vs = _pack([var[n].reshape(grads[n].shape) for n in SMALL] + riders, LANE, 8)
    small_out = [_unpack(a, small_shapes) for a in _adamw(all_small, ws, ms, vs, "adamw_small")]
    loss = small_out[0][-1][0, 0]
    conv_g = lax.dynamic_slice_in_dim(small_out[0][len(SMALL)], me * conv_shard_shape[0], conv_shard_shape[0], axis=0)
    conv_out = [_unpack(a, [conv_shard_shape])[0] for a in _adamw(
        _pack([conv_g], LANE, 8)[None], _pack([w["conv_w"][0]], LANE, 8), _pack([mom["conv_w"][0]], LANE, 8),
        _pack([var["conv_w"][0]], LANE, 8), "adamw_conv")]

    def leaf(kind, n):
        if n in BIG:
            val = big_out[kind][BIG.index(n)]
        elif n == "conv_w":
            val = conv_out[kind]
        else:
            val = small_out[kind][SMALL.index(n)]
        return val.reshape(w[n].shape)

    outs = [loss, grad_x[None]]
    for kind in range(4):
        outs += [leaf(kind, n) for n in WEIGHTS]
    return tuple(outs)
```

```python
import functools
import math

import numpy as np
import jax
import jax.numpy as jnp
from jax import lax
from jax.experimental import pallas as pl
from jax.experimental.pallas import tpu as pltpu

F32 = jnp.float32
BF16 = jnp.bfloat16

D_MODEL = 1024
D_FF = 2816
GDN_HEADS = 4
GDN_HEAD_DIM = 128
GDN_WIDTH = 512
CONV_WIDTH = 5
CHUNK = 64
SWA_HEADS = 8
SWA_HEAD_DIM = 64
SWA_WIDTH = 512
DILATION_PATTERNS = ((128, 1), (512, 4), (2048, 16))
REL_BUCKETS = 32
REL_MAX_DISTANCE = 1024
EPS = 1e-6
NEG_BIG = -1e30
N_DEV = 8

ADAM_LR = 0.001
ADAM_B1 = 0.9
ADAM_B2 = 0.999
ADAM_EPS = 1e-08
ADAM_WD = 0.01
ADAM_STEP = 10

QKV_A = 3 * GDN_WIDTH
OFF_B = QKV_A
OFF_Z = OFF_B + 3 * SWA_WIDTH
OFF_AB = OFF_Z + GDN_WIDTH
N_PAD = OFF_AB + 256
N_IN = 3600
NAT_Z, NAT_AB, NAT_B = QKV_A, QKV_A + GDN_WIDTH, QKV_A + GDN_WIDTH + 16

V7X_VMEM_LIMIT_BYTES = 56 * 1024 * 1024
LANE = 128
ATT_BQ = 128
ATT_HALO = 64
CONV_ROWS = 256

NN = (((1,), (0,)), ((), ()))
NT = (((1,), (1,)), ((), ()))
TN = (((0,), (0,)), ((), ()))


def _params(*sem):
    return pltpu.CompilerParams(dimension_semantics=sem, vmem_limit_bytes=V7X_VMEM_LIMIT_BYTES)


def _dot(a, b, dn=NN):
    return lax.dot_general(a.astype(BF16), b.astype(BF16), dn, preferred_element_type=F32)


def _sigmoid(x):
    return 1.0 / (1.0 + jnp.exp(-x))


class _Exchange:
    def __init__(self, kind, arrays):
        self.kind, self.arrays = kind, list(arrays)

    def out_shape(self):
        lead = (N_DEV,) if self.kind == "gather" else ()
        return [jax.ShapeDtypeStruct(lead + v.shape, v.dtype) for v in self.arrays]

    def hooks(self, in_refs, out_refs, sems, grid):
        step = pl.program_id(0)
        for axis in range(1, len(grid)):
            step = step * grid[axis] + pl.program_id(axis)
        total = math.prod(grid)
        if self.kind == "gather":
            assert total >= 4
            start, forward, finish = _gather_phases(in_refs, out_refs, *sems)
            pl.when(step == total // 2)(forward)
        else:
            assert total >= 2
            start, finish = _scatter_phases(in_refs, out_refs, *sems)
        pl.when(step == 0)(start)
        pl.when(step == total - 1)(finish)


def _pallas(body, *, name, grid, in_specs, out_specs, out_shape, args, semantics, scratch_shapes=(), exchange=None):
    n_in, n_out, n_scr = len(in_specs), len(out_specs), len(scratch_shapes)
    if exchange is None:
        res = pl.pallas_call(
            body, name=name, grid=grid, in_specs=list(in_specs), out_specs=list(out_specs), out_shape=list(out_shape),
            scratch_shapes=list(scratch_shapes), compiler_params=_params(*semantics))(*args)
        return list(res), []
    na = len(exchange.arrays)

    def carrying(*refs):
        ins, sent = refs[:n_in], refs[n_in:n_in + na]
        outs = refs[n_in + na:n_in + na + n_out]
        landed = refs[n_in + na + n_out:n_in + 2 * na + n_out]
        rest = refs[n_in + 2 * na + n_out:]
        exchange.hooks(sent, landed, rest[n_scr:], grid)
        body(*ins, *outs, *rest[:n_scr])

    res = pl.pallas_call(
        carrying, name=name, grid=grid, in_specs=list(in_specs) + [ANY] * na, out_specs=list(out_specs) + [ANY] * na,
        out_shape=list(out_shape) + exchange.out_shape(), scratch_shapes=list(scratch_shapes) + _gather_semaphores(na),
        compiler_params=_params(*(["arbitrary"] * len(grid))))(*args, *exchange.arrays)
    return list(res[:n_out]), list(res[n_out:])


def _matmul(pairs, *, ta=False, tb=False, out_dtype=F32, tm, tn, tk, name, res=None, alpha=None, norm_bwd=None,
            norm_fwd=None, loss=None, exchange=None):
    a0, b0 = pairs[0]
    m = a0.shape[1] if ta else a0.shape[0]
    k = a0.shape[0] if ta else a0.shape[1]
    n = b0.shape[0] if tb else b0.shape[1]
    tm, tn, tk = min(tm, m), min(tn, n), min(tk, k)
    assert m % tm == 0 and n % tn == 0 and k % tk == 0, (name, m, n, k, tm, tn, tk)
    nk = k // tk
    npairs = len(pairs)
    dn = (((0 if ta else 1,), (1 if tb else 0,)), ((), ()))
    assert norm_bwd is None or (tn == n and res is None and alpha is None)

    def body(*refs):
        ins = refs[:2 * npairs]
        pos = 2 * npairs
        r_ref = None
        if res is not None:
            r_ref = refs[pos]
            pos += 1
        if norm_bwd is not None:
            x_ref, rs_ref, w_ref, dres_ref = refs[pos:pos + 4]
            o_ref, dw_ref, acc = refs[pos + 4:pos + 7]

            @pl.when((pl.program_id(0) == 0) & (pl.program_id(2) == 0))
            def _():
                dw_ref[...] = jnp.zeros_like(dw_ref)
        elif norm_fwd is not None:
            wn_ref, o_ref, n_ref, rs_out, acc = refs[pos:pos + 5]
        elif loss is not None:
            wf_ref, tg_ref, loss_ref, o_ref, dwf_ref, acc = refs[pos:pos + 6]

            @pl.when((pl.program_id(0) == 0) & (pl.program_id(2) == 0))
            def _():
                dwf_ref[...] = jnp.zeros_like(dwf_ref)
                loss_ref[...] = jnp.zeros_like(loss_ref)
        else:
            o_ref, acc = refs[pos], refs[pos + 1]
        kk = pl.program_id(2)
        t = None
        for p in range(npairs):
            d = _dot(ins[2 * p][...], ins[2 * p + 1][...], dn)
            t = d if t is None else t + d

        if nk > 1:
            @pl.when(kk == 0)
            def _():
                acc[...] = t

            @pl.when((kk > 0) & (kk < nk - 1))
            def _():
                acc[...] += t

        @pl.when(kk == nk - 1)
        def _():
            r = acc[...] + t if nk > 1 else t
            if alpha is not None:
                r = r * alpha
            if r_ref is not None:
                r = r_ref[...] + r
            if norm_bwd is not None:
                rs = rs_ref[...]
                xhat = x_ref[...] * rs
                dw_ref[...] += jnp.sum(r * xhat, axis=0, keepdims=True)
                t_w = r * w_ref[...]
                r = dres_ref[...] + rs * (t_w - xhat * jnp.mean(t_w * xhat, axis=-1, keepdims=True))
            if norm_fwd is not None:
                rs = lax.rsqrt(jnp.mean(r * r, axis=-1, keepdims=True) + EPS)
                n_ref[...] = (r * rs * wn_ref[...]).astype(BF16)
                rs_out[...] = rs
            if loss is not None:
                wv = wf_ref[...]
                rs = lax.rsqrt(jnp.mean(r * r, axis=-1, keepdims=True) + EPS)
                xhat = r * rs
                e = xhat * wv - tg_ref[...]
                part = 0.5 * jnp.sum(jnp.mean(e * e, axis=-1, keepdims=True), axis=0, keepdims=True)
                loss_ref[...] += jnp.broadcast_to(part, loss_ref.shape)
                dy = e * (1.0 / n)
                dwf_ref[...] += jnp.sum(dy * xhat, axis=0, keepdims=True)
                t_w = dy * wv
                r = rs * (t_w - xhat * jnp.mean(t_w * xhat, axis=-1, keepdims=True))
            o_ref[...] = r.astype(out_dtype)

    a_spec = pl.BlockSpec((tk, tm), lambda i, j, kk: (kk, i)) if ta else pl.BlockSpec((tm, tk), lambda i, j, kk: (i, kk))
    b_spec = pl.BlockSpec((tn, tk), lambda i, j, kk: (j, kk)) if tb else pl.BlockSpec((tk, tn), lambda i, j, kk: (kk, j))
    o_spec = pl.BlockSpec((tm, tn), lambda i, j, kk: (i, j))
    in_specs = [a_spec, b_spec] * npairs + ([o_spec] if res is not None else [])
    args = [t for pr in pairs for t in pr] + ([res] if res is not None else [])
    out_specs, out_shape = [o_spec], [jax.ShapeDtypeStruct((m, n), out_dtype)]
    if norm_bwd is not None:
        vec = pl.BlockSpec((1, n), lambda i, j, kk: (0, 0))
        in_specs += [o_spec, pl.BlockSpec((tm, 1), lambda i, j, kk: (i, 0)), vec, o_spec]
        args += list(norm_bwd)
        out_specs.append(vec)
        out_shape.append(jax.ShapeDtypeStruct((1, n), F32))
    if norm_fwd is not None:
        assert tn == n and norm_bwd is None
        in_specs.append(pl.BlockSpec((1, n), lambda i, j, kk: (0, 0)))
        args.append(norm_fwd)
        out_specs += [o_spec, pl.BlockSpec((tm, 1), lambda i, j, kk: (i, 0))]
        out_shape += [jax.ShapeDtypeStruct((m, n), BF16), jax.ShapeDtypeStruct((m, 1), F32)]
    if loss is not None:
        assert tn == n and norm_bwd is None and norm_fwd is None
        vec = pl.BlockSpec((1, n), lambda i, j, kk: (0, 0))
        in_specs += [vec, o_spec]
        args += list(loss)
        out_specs = [pl.BlockSpec((1, LANE), lambda i, j, kk: (0, 0))] + out_specs + [vec]
        out_shape = [jax.ShapeDtypeStruct((1, LANE), F32)] + out_shape + [jax.ShapeDtypeStruct((1, n), F32)]
    sequential = norm_bwd is not None or loss is not None
    outs, exchanged = _pallas(
        body, name=name, grid=(m // tm, n // tn, nk), in_specs=in_specs, out_specs=out_specs, out_shape=out_shape,
        scratch_shapes=[pltpu.VMEM((tm, tn) if nk > 1 else (8, LANE), F32)],
        semantics=("arbitrary",) * 3 if sequential else ("parallel", "parallel", "arbitrary"), args=args,
        exchange=exchange)
    out = outs[0] if len(outs) == 1 else tuple(outs)
    return out if exchange is None else (out, exchanged)


def _rms_fwd(x, w, name, exchange=None):
    s, d = x.shape
    tm = min(512, s)

    def body(x_ref, w_ref, n_ref, r_ref):
        xv = x_ref[...]
        r = lax.rsqrt(jnp.mean(xv * xv, axis=-1, keepdims=True) + EPS)
        n_ref[...] = (xv * r * w_ref[...]).astype(BF16)
        r_ref[...] = r

    (n, r), exchanged = _pallas(
        body, name=name, grid=(s // tm,),
        in_specs=[pl.BlockSpec((tm, d), lambda i: (i, 0)), pl.BlockSpec((1, d), lambda i: (0, 0))],
        out_specs=[pl.BlockSpec((tm, d), lambda i: (i, 0)), pl.BlockSpec((tm, 1), lambda i: (i, 0))],
        out_shape=[jax.ShapeDtypeStruct((s, d), BF16), jax.ShapeDtypeStruct((s, 1), F32)],
        semantics=("parallel",), args=(x, w), exchange=exchange)
    return (n, r) if exchange is None else (n, r, exchanged)


def _rms_bwd(dn, x, r, w, dres, name, exchange=None):
    s, d = x.shape
    tm = min(512, s)

    def body(dn_ref, x_ref, r_ref, w_ref, dres_ref, dx_ref, dw_ref):
        @pl.when(pl.program_id(0) == 0)
        def _():
            dw_ref[...] = jnp.zeros_like(dw_ref)

        rv = r_ref[...]
        xhat = x_ref[...] * rv
        g = dn_ref[...]
        t = g * w_ref[...]
        dx_ref[...] = dres_ref[...] + rv * (t - xhat * jnp.mean(t * xhat, axis=-1, keepdims=True))
        dw_ref[...] += jnp.sum(g * xhat, axis=0, keepdims=True)

    row = pl.BlockSpec((tm, d), lambda i: (i, 0))
    vec = pl.BlockSpec((1, d), lambda i: (0, 0))
    (dx, dw), exchanged = _pallas(
        body, name=name, grid=(s // tm,),
        in_specs=[row, row, pl.BlockSpec((tm, 1), lambda i: (i, 0)), vec, row],
        out_specs=[row, vec],
        out_shape=[jax.ShapeDtypeStruct((s, d), F32), jax.ShapeDtypeStruct((1, d), F32)],
        semantics=("arbitrary",), args=(dn, x, r, w, dres), exchange=exchange)
    return (dx, dw) if exchange is None else (dx, dw, exchanged)


def _ffn_up(n, wg, wu, name, exchange=None):
    s, d = n.shape
    f = wg.shape[0]
    tm, tn = min(512, s), f // 2

    def body(n_ref, wg_ref, wu_ref, g_ref, u_ref, a_ref):
        nv = n_ref[...]
        g = _dot(nv, wg_ref[...], NT)
        u = _dot(nv, wu_ref[...], NT)
        g_ref[...] = g.astype(BF16)
        u_ref[...] = u.astype(BF16)
        a_ref[...] = (g * _sigmoid(g) * u).astype(BF16)

    o = pl.BlockSpec((tm, tn), lambda j, i: (i, j))
    wspec = pl.BlockSpec((tn, d), lambda j, i: (j, 0))
    return _pallas(
        body, name=name, grid=(f // tn, s // tm),
        in_specs=[pl.BlockSpec((tm, d), lambda j, i: (i, 0)), wspec, wspec],
        out_specs=[o, o, o],
        out_shape=[jax.ShapeDtypeStruct((s, f), BF16)] * 3,
        semantics=("parallel", "parallel"), args=(n, wg, wu), exchange=exchange)


def _ffn_dact(dx, wd, g, u, name, exchange=None):
    s, d = dx.shape
    f = wd.shape[0]
    tm, tn = min(512, s), f // 2

    def body(dx_ref, wd_ref, g_ref, u_ref, dg_ref, du_ref):
        da = 0.5 * _dot(dx_ref[...], wd_ref[...], NT)
        gv = g_ref[...].astype(F32)
        sg = _sigmoid(gv)
        du_ref[...] = (da * gv * sg).astype(BF16)
        dg_ref[...] = (da * u_ref[...].astype(F32) * (sg * (1.0 + gv * (1.0 - sg)))).astype(BF16)

    o = pl.BlockSpec((tm, tn), lambda j, i: (i, j))
    return _pallas(
        body, name=name, grid=(f // tn, s // tm),
        in_specs=[pl.BlockSpec((tm, d), lambda j, i: (i, 0)), pl.BlockSpec((tn, d), lambda j, i: (j, 0)), o, o],
        out_specs=[o, o],
        out_shape=[jax.ShapeDtypeStruct((s, f), BF16), jax.ShapeDtypeStruct((s, f), BF16)],
        semantics=("parallel", "parallel"), args=(dx, wd, g, u), exchange=exchange)


def _row_slabs(full):
    return full.reshape(N_DEV, full.shape[0] // N_DEV, full.shape[1])


def _rows_of(slabs):
    return slabs.reshape(N_DEV * slabs.shape[1], slabs.shape[2])


def _ffn_forward(x, norm_w, wg, wu, wd, tag, gather=(), head=(), normed=None, next_norm=None, loss=None):
    if normed is not None:
        (n, r), first = normed, []
    elif head:
        n, r, first = _rms_fwd(x, norm_w, f"{tag}_norm", _Exchange("gather", head))
    else:
        (n, r), first = _rms_fwd(x, norm_w, f"{tag}_norm"), []
    if wg is None:
        wg, wu, first = _rows_of(first[0]), _rows_of(first[1]), first[2:]
    (g, u, a), got = _ffn_up(n, wg, wu, f"{tag}_up", _Exchange("gather", gather) if gather else None)
    if wd is None:
        wd, got = _rows_of(got[0]), got[1:]
    y = _matmul([(a, wd)], tm=512, tn=1024, tk=wd.shape[0], name=f"{tag}_down", res=x, alpha=0.5, norm_fwd=next_norm,
                loss=loss)
    y, nxt = (y[0], y[1:]) if next_norm is not None else (y, None)
    return y, (n, r, g, u, a), (wg, wu, wd), got, first, nxt


def _ffn_backward(dy, x, norm_w, wgt, wut, wd, saved, tag, dw_dtype=F32, scatter=None):
    n, r, g, u, a = saved

    def behind(arrays):
        return _Exchange("scatter", arrays) if scatter is not None else None

    def dw(act, grad, name, alpha=None, exchange=None):
        return _matmul([(act, grad)], ta=True, tm=1408, tn=1024, tk=2048, name=name, alpha=alpha, out_dtype=dw_dtype,
                       exchange=exchange)

    dwd = _row_slabs(dw(a, dy, f"{tag}_dwd", alpha=0.5))
    (dg, du), extras = _ffn_dact(dy, wd, g, u, f"{tag}_dact", behind(scatter))
    if scatter is None:
        dwg, dwu = _row_slabs(dw(dg, n, f"{tag}_dwg")), _row_slabs(dw(du, n, f"{tag}_dwu"))
    else:
        dwg, (dwd,) = dw(dg, n, f"{tag}_dwg", exchange=behind([dwd]))
        dwu, (dwg,) = dw(du, n, f"{tag}_dwu", exchange=behind([_row_slabs(dwg)]))
        dwu = _row_slabs(dwu)
    if scatter is None:
        dx, dnorm = _matmul([(dg, wgt), (du, wut)], tm=512, tn=1024, tk=wgt.shape[0], name=f"{tag}_dn",
                            norm_bwd=(x, r, norm_w, dy))
    else:
        dn, (dwu,) = _matmul([(dg, wgt), (du, wut)], tm=512, tn=1024, tk=wgt.shape[0], name=f"{tag}_dn",
                             exchange=behind([dwu]))
        dx, dnorm = _rms_bwd(dn, x, r, norm_w, dy, f"{tag}_dnorm")
    return dx, dnorm, dwg, dwu, dwd, extras


Q_SCALE = GDN_HEAD_DIM ** -0.5
CONV_HALO = 8


def _lane_block(s):
    return pl.BlockSpec((None, s, LANE), lambda j: (j, 0, 0))


def _halo_window(ref, ci, rows, s):
    b = pl.multiple_of(ci * rows, rows)
    before = ref[pl.ds(pl.multiple_of(jnp.maximum(b - CONV_HALO, 0), CONV_HALO), CONV_HALO), :]
    after = ref[pl.ds(pl.multiple_of(jnp.minimum(b + rows, s - CONV_HALO), CONV_HALO), CONV_HALO), :]
    return jnp.concatenate([jnp.where(ci > 0, before, 0.0), ref[pl.ds(b, rows), :],
                            jnp.where(ci < s // rows - 1, after, 0.0)], axis=0)


def _conv_taps(win, w_ref, rows, sign):
    n = rows + 2 * CONV_HALO
    acc = None
    for t in range(CONV_WIDTH):
        o = sign * (t - CONV_WIDTH // 2)
        sh = win if o == 0 else pltpu.roll(win, (-o) % n, 0)
        term = sh[CONV_HALO:CONV_HALO + rows] * w_ref[t:t + 1, :]
        acc = term if acc is None else acc + term
    return acc


def _gdn_conv_fwd(p_pad, conv_wt):
    s = p_pad.shape[0]
    rows = min(CONV_ROWS, s)
    nblk = QKV_A // LANE

    def body(p_ref, w_ref, c_ref, y_ref):
        j = pl.program_id(0)

        def chunk(ci, carry):
            b = pl.multiple_of(ci * rows, rows)
            c = _conv_taps(_halo_window(p_ref, ci, rows, s), w_ref, rows, 1)
            c_ref[pl.ds(b, rows), :] = c
            act = c * _sigmoid(c)
            nrm = lax.rsqrt(jnp.sum(act * act, axis=-1, keepdims=True) + EPS)
            mult = jnp.where(j < GDN_HEADS, nrm * Q_SCALE, jnp.where(j < 2 * GDN_HEADS, nrm, 1.0))
            y_ref[pl.ds(b, rows), :] = act * mult
            return carry

        lax.fori_loop(0, s // rows, chunk, 0)

    col = pl.BlockSpec((s, LANE), lambda j: (0, j))
    return pl.pallas_call(
        body, name="gdn_conv_fwd", grid=(nblk,),
        in_specs=[col, pl.BlockSpec((8, LANE), lambda j: (0, j))],
        out_specs=[_lane_block(s), _lane_block(s)],
        out_shape=[jax.ShapeDtypeStruct((nblk, s, LANE), F32), jax.ShapeDtypeStruct((nblk, s, LANE), F32)],
        compiler_params=_params("parallel"),
    )(p_pad, conv_wt)


def _gdn_conv_bwd(dy_f, dy_r, c_pre, p_pad, conv_wt, dp_all):
    s = p_pad.shape[0]
    rows = min(CONV_ROWS, s)
    nblk = QKV_A // LANE

    def body(dyf_ref, dyr_ref, c_ref, p_ref, w_ref, _, dp_ref, dw_ref, dcpad):
        j = pl.program_id(0)
        zeros = jnp.zeros((CONV_HALO, LANE), F32)
        dcpad[0:CONV_HALO, :] = zeros
        dcpad[CONV_HALO + s:2 * CONV_HALO + s, :] = zeros

        def act_bwd(ci, carry):
            b = pl.multiple_of(ci * rows, rows)
            c = c_ref[pl.ds(b, rows), :]
            g = dyf_ref[pl.ds(b, rows), :] + dyr_ref[pl.ds(b, rows), :]
            sg = _sigmoid(c)
            act = c * sg
            nrm = lax.rsqrt(jnp.sum(act * act, axis=-1, keepdims=True) + EPS)
            yh = act * nrm
            scale = jnp.where(j < GDN_HEADS, Q_SCALE, 1.0)
            dact_qk = (scale * nrm) * (g - yh * jnp.sum(g * yh, axis=-1, keepdims=True))
            dact = jnp.where(j < 2 * GDN_HEADS, dact_qk, g)
            dcpad[pl.ds(pl.multiple_of(b + CONV_HALO, CONV_HALO), rows), :] = dact * (sg * (1.0 + c * (1.0 - sg)))
            return carry

        lax.fori_loop(0, s // rows, act_bwd, 0)
        tap = lax.broadcasted_iota(jnp.int32, (8, LANE), 0)

        def taps_bwd(ci, dw):
            b = pl.multiple_of(ci * rows, rows)
            dcw = dcpad[pl.ds(b, rows + 2 * CONV_HALO), :]
            dp_ref[pl.ds(b, rows), :] = _conv_taps(dcw, w_ref, rows, -1).astype(BF16)
            pw = _halo_window(p_ref, ci, rows, s)
            dc = dcw[CONV_HALO:CONV_HALO + rows]
            n = rows + 2 * CONV_HALO
            for t in range(CONV_WIDTH):
                o = t - CONV_WIDTH // 2
                sh = pw if o == 0 else pltpu.roll(pw, (-o) % n, 0)
                row = jnp.sum(dc * sh[CONV_HALO:CONV_HALO + rows], axis=0, keepdims=True)
                dw = dw + jnp.where(tap == t, row, 0.0)
            return dw

        dw_ref[...] = lax.fori_loop(0, s // rows, taps_bwd, jnp.zeros((8, LANE), F32))

    col = pl.BlockSpec((s, LANE), lambda j: (0, j))
    wspec = pl.BlockSpec((8, LANE), lambda j: (0, j))
    return pl.pallas_call(
        body, name="gdn_conv_bwd", grid=(nblk,),
        in_specs=[_lane_block(s), _lane_block(s), _lane_block(s), col, wspec, ANY],
        out_specs=[col, wspec],
        out_shape=[jax.ShapeDtypeStruct(dp_all.shape, dp_all.dtype), jax.ShapeDtypeStruct((8, QKV_A), F32)],
        scratch_shapes=[pltpu.VMEM((s + 2 * CONV_HALO, LANE), F32)],
        input_output_aliases={5: 0},
        compiler_params=_params("parallel"),
    )(dy_f, dy_r, c_pre, p_pad, conv_wt, dp_all)


def _softplus(x):
    return jnp.maximum(x, 0.0) + jnp.log(1.0 + jnp.exp(-jnp.abs(x)))


def _gdn_gates_fwd(p_pad, alog_row, dt_row):
    s = p_pad.shape[0]
    tm = min(1024, s)

    def body(p_ref, al_ref, dt_ref, o_ref):
        x = p_ref[...]
        lane = lax.broadcasted_iota(jnp.int32, x.shape, 1)
        g = -jnp.exp(al_ref[...]) * _softplus(x + dt_ref[...])
        o_ref[...] = jnp.where(lane < 8, g, jnp.where(lane < 16, _sigmoid(x), 0.0))

    vec = pl.BlockSpec((1, LANE), lambda i: (0, 0))
    return pl.pallas_call(
        body, name="gdn_gates_fwd", grid=(s // tm,),
        in_specs=[pl.BlockSpec((tm, LANE), lambda i: (i, OFF_AB // LANE)), vec, vec],
        out_specs=pl.BlockSpec((tm, LANE), lambda i: (i, 0)),
        out_shape=jax.ShapeDtypeStruct((s, LANE), F32),
        compiler_params=_params("parallel"),
    )(p_pad, alog_row, dt_row)


def _gdn_gates_bwd(dgb_f, dgb_r, p_pad, gb, alog_row, dt_row, dp_all):
    s = p_pad.shape[0]
    tm = min(1024, s)
    tail = N_PAD - OFF_AB

    def body(df_ref, dr_ref, p_ref, gb_ref, al_ref, dt_ref, _, dp_ref, sum_ref):
        @pl.when(pl.program_id(0) == 0)
        def _():
            sum_ref[...] = jnp.zeros_like(sum_ref)

        x = p_ref[...]
        gbv = gb_ref[...]
        dgb = df_ref[...] + dr_ref[...]
        lane = lax.broadcasted_iota(jnp.int32, x.shape, 1)
        da = dgb * (-jnp.exp(al_ref[...])) * _sigmoid(x + dt_ref[...])
        db = dgb * gbv * (1.0 - gbv)
        dp_ref[:, 0:LANE] = jnp.where(lane < 8, da, jnp.where(lane < 16, db, 0.0)).astype(BF16)
        dp_ref[:, LANE:tail] = jnp.zeros((tm, tail - LANE), BF16)
        row = lax.broadcasted_iota(jnp.int32, (8, LANE), 0)
        lane8 = lax.broadcasted_iota(jnp.int32, (8, LANE), 1)
        d_alog = jnp.sum(dgb * gbv, axis=0, keepdims=True)
        d_dt = jnp.sum(da, axis=0, keepdims=True)
        upd = jnp.where(row == 0, d_alog, jnp.where(row == 1, d_dt, 0.0))
        sum_ref[...] += jnp.where(lane8 < 8, upd, 0.0)

    vec = pl.BlockSpec((1, LANE), lambda i: (0, 0))
    blk = pl.BlockSpec((tm, LANE), lambda i: (i, 0))
    return pl.pallas_call(
        body, name="gdn_gates_bwd", grid=(s // tm,),
        in_specs=[blk, blk, pl.BlockSpec((tm, LANE), lambda i: (i, OFF_AB // LANE)), blk, vec, vec, ANY],
        out_specs=[pl.BlockSpec((tm, tail), lambda i: (i, OFF_AB // tail)), pl.BlockSpec((8, LANE), lambda i: (0, 0))],
        out_shape=[jax.ShapeDtypeStruct(dp_all.shape, dp_all.dtype), jax.ShapeDtypeStruct((8, LANE), F32)],
        input_output_aliases={6: 0},
        compiler_params=_params("arbitrary"),
    )(dgb_f, dgb_r, p_pad, gb, alog_row, dt_row, dp_all)


def _chunk_masks(rev):
    row = lax.broadcasted_iota(jnp.int32, (CHUNK, CHUNK), 0)
    col = lax.broadcasted_iota(jnp.int32, (CHUNK, CHUNK), 1)
    le = (col >= row) if rev else (col <= row)
    strict = (col > row) if rev else (col < row)
    return le, strict, row == col


def _gate_lanes(rev, h):
    d = 1 if rev else 0
    return d * GDN_HEADS + h, 8 + d * GDN_HEADS + h


BNN = (((2,), (1,)), ((0,), (0,)))
BNT = (((2,), (2,)), ((0,), (0,)))
BTN = (((1,), (1,)), ((0,), (0,)))
NB = 2 * GDN_HEADS
DELTA_CHUNKS = 8


def _bdot(a, b, dn=BNN):
    return lax.dot_general(a.astype(BF16), b.astype(BF16), dn, preferred_element_type=F32)


def _dot3(a, b, dn, exact_a=False, exact_b=False):
    def d(x, y):
        return lax.dot_general(x, y, dn, preferred_element_type=F32)

    ah = a.astype(BF16)
    bh = b.astype(BF16)
    out = d(ah, bh)
    if not exact_b:
        out = out + d(ah, (b - bh.astype(F32)).astype(BF16))
    if not exact_a:
        out = out + d((a - ah.astype(F32)).astype(BF16), bh)
    return out


def _both(f_val, r_val):
    return jnp.stack([f_val] * GDN_HEADS + [r_val] * GDN_HEADS)


def _head_blocks(ref_f, ref_r, rows_f, rows_r):
    return jnp.concatenate([ref_f[:, rows_f, :], ref_r[:, rows_r, :]], axis=0)


def _chunk_rows(c):
    return slice(c * CHUNK, (c + 1) * CHUNK), slice((DELTA_CHUNKS - 1 - c) * CHUNK, (DELTA_CHUNKS - c) * CHUNK)


def _heads(ref_f, ref_r, rows_f, rows_r):
    hd = GDN_HEAD_DIM
    return jnp.stack([ref_f[rows_f, h * hd:(h + 1) * hd] for h in range(GDN_HEADS)]
                     + [ref_r[rows_r, h * hd:(h + 1) * hd] for h in range(GDN_HEADS)])


def _gate_cols(tile_f, tile_r, base):
    return jnp.stack([tile_f[:, base + h:base + h + 1] for h in range(GDN_HEADS)]
                     + [tile_r[:, base + GDN_HEADS + h:base + GDN_HEADS + h + 1] for h in range(GDN_HEADS)])


def _chunk_common2(q, k, v, gbf, gbr):
    mf, mr = _chunk_masks(False), _chunk_masks(True)
    le, strict = _both(mf[0], mr[0]), _both(mf[1], mr[1])
    eye = mf[2]
    gcm_f = _dot3(mf[0].astype(F32), gbf, NN, exact_a=True)
    gcm_r = _dot3(mr[0].astype(F32), gbr, NN, exact_a=True)
    g, beta, gc = _gate_cols(gbf, gbr, 0), _gate_cols(gbf, gbr, 8), _gate_cols(gcm_f, gcm_r, 0)
    gc_row = _dot3(jnp.ones((NB, CHUNK, CHUNK), F32), jnp.where(eye[None], gc, 0.0), BNN, exact_a=True)
    decay = jnp.where(le, jnp.exp(jnp.where(le, gc - gc_row, 0.0)), 0.0)
    eg = jnp.exp(gc)
    gl = jnp.sum(g, axis=1, keepdims=True)
    kb = k * beta
    vb = v * beta
    kbeg = kb * eg
    lm = jnp.where(strict, _bdot(kb, k, BNT) * decay, 0.0)
    intra = _bdot(q, k, BNT) * decay
    edec = jnp.exp(gl - gc)
    return dict(strict=strict, eye=eye, beta=beta, decay=decay, eg=eg, gl=gl, kb=kb, vb=vb, kbeg=kbeg,
                lm=lm, intra=intra, qg=q * eg, edec=edec, kdec=k * edec)


def _unit_triangular_inverse(lm, eye):
    x = -lm
    t = eye[None].astype(F32) + x
    p = x
    for _ in range(5):
        p = _bdot(p, p)
        t = t + _bdot(t, p)
    return t


def _delta_fwd2(y, gb, gather=()):
    s = y.shape[1]
    nc = s // CHUNK
    hd = GDN_HEAD_DIM
    na = len(gather)

    def body(*refs):
        qf, kf, vf, gf, qr, kr, vr, gr = refs[:8]
        of_ref, or_ref, sf_all, sr_all, tf_all, tr_all = refs[8 + na:14 + na]
        state = refs[14 + 2 * na]
        step = pl.program_id(0)

        @pl.when(step == 0)
        def _():
            state[...] = jnp.zeros_like(state)

        if na:
            start, forward, finish = _gather_phases(refs[8:8 + na], refs[14 + na:14 + 2 * na], *refs[15 + 2 * na:])
            pl.when(step == 0)(start)
            pl.when(step == ns // 2)(forward)
            pl.when(step == ns - 1)(finish)

        st = state[...]
        for c in range(DELTA_CHUNKS):
            rf, rr = _chunk_rows(c)
            q, k, v = _head_blocks(qf, qr, rf, rr), _head_blocks(kf, kr, rf, rr), _head_blocks(vf, vr, rf, rr)
            cm = _chunk_common2(q, k, v, gf[rf, :], gr[rr, :])
            tinv = _unit_triangular_inverse(cm["lm"], cm["eye"])
            u = _bdot(tinv, cm["vb"])
            w = _bdot(tinv, cm["kbeg"])
            v_new = u - _bdot(w, st)
            o = _bdot(cm["qg"], st) + _bdot(cm["intra"], v_new)
            for h in range(GDN_HEADS):
                of_ref[rf, h * hd:(h + 1) * hd] = o[h]
                or_ref[rr, h * hd:(h + 1) * hd] = o[GDN_HEADS + h]
            sf_all[c] = st[:GDN_HEADS]
            sr_all[DELTA_CHUNKS - 1 - c] = st[GDN_HEADS:]
            tf_all[c] = tinv[:GDN_HEADS]
            tr_all[DELTA_CHUNKS - 1 - c] = tinv[GDN_HEADS:]
            st = st * jnp.exp(cm["gl"]) + _bdot(cm["kdec"], v_new, BTN)
        state[...] = st

    rows = DELTA_CHUNKS * CHUNK
    ns = nc // DELTA_CHUNKS

    def col(j, rev):
        return pl.BlockSpec((GDN_HEADS, rows, hd), (lambda n: (j, ns - 1 - n, 0)) if rev else (lambda n: (j, n, 0)))

    def out(rev):
        return pl.BlockSpec((rows, GDN_WIDTH), (lambda n: (ns - 1 - n, 0)) if rev else (lambda n: (n, 0)))

    def gate(rev):
        return pl.BlockSpec((rows, LANE), (lambda n: (ns - 1 - n, 0)) if rev else (lambda n: (n, 0)))

    def per_chunk(d1, d2, rev):
        return pl.BlockSpec((DELTA_CHUNKS, GDN_HEADS, d1, d2),
                            (lambda n: (ns - 1 - n, 0, 0, 0)) if rev else (lambda n: (n, 0, 0, 0)))

    assert nc % DELTA_CHUNKS == 0 and (na == 0 or ns >= 4)
    res = pl.pallas_call(
        body, name="delta_fwd", grid=(ns,),
        in_specs=[col(0, False), col(1, False), col(2, False), gate(False), col(0, True), col(1, True), col(2, True), gate(True)]
        + [ANY] * na,
        out_specs=[out(False), out(True), per_chunk(hd, hd, False), per_chunk(hd, hd, True),
                   per_chunk(CHUNK, CHUNK, False), per_chunk(CHUNK, CHUNK, True)] + [ANY] * na,
        out_shape=[jax.ShapeDtypeStruct((s, GDN_WIDTH), F32)] * 2 + [jax.ShapeDtypeStruct((nc, GDN_HEADS, hd, hd), F32)] * 2
        + [jax.ShapeDtypeStruct((nc, GDN_HEADS, CHUNK, CHUNK), F32)] * 2
        + [jax.ShapeDtypeStruct((N_DEV,) + v.shape, v.dtype) for v in gather],
        scratch_shapes=[pltpu.VMEM((NB, hd, hd), F32)] + (_gather_semaphores(na) if na else []),
        compiler_params=_params("arbitrary"),
    )(y, y, y, gb, y, y, y, gb, *gather)
    return res[:6], res[6:]


def _delta_bwd2(y, gb, do, sf_all, sr_all, tf_all, tr_all, scatter=()):
    s = y.shape[1]
    nc = s // CHUNK
    hd = GDN_HEAD_DIM
    na = len(scatter)

    def body(*refs):
        qf, kf, vf, gf, dof, sf, tf, qr, kr, vr, gr, dor, sr, tr = refs[:14]
        dyf_ref, dyr_ref, dgf_ref, dgr_ref = refs[14 + na:18 + na]
        dstate = refs[18 + 2 * na]
        step = pl.program_id(0)

        @pl.when(step == 0)
        def _():
            dstate[...] = jnp.zeros_like(dstate)

        if na:
            start, finish = _scatter_phases(refs[14:14 + na], refs[18 + na:18 + 2 * na], *refs[19 + 2 * na:])
            pl.when(step == 0)(start)
            pl.when(step == ns - 1)(finish)

        def one_chunk(c, ds_out):
            rr, rf = _chunk_rows(c)
            cf, cr = DELTA_CHUNKS - 1 - c, c
            q, k, v = _head_blocks(qf, qr, rf, rr), _head_blocks(kf, kr, rf, rr), _head_blocks(vf, vr, rf, rr)
            dov = _heads(dof, dor, rf, rr)
            cm = _chunk_common2(q, k, v, gf[rf, :], gr[rr, :])
            tinv = jnp.concatenate([tf[cf], tr[cr]], axis=0)
            st = jnp.concatenate([sf[cf], sr[cr]], axis=0)
            decay, lm, intra, qg, kdec, kbeg, eg, kb, beta = (
                cm[n] for n in ("decay", "lm", "intra", "qg", "kdec", "kbeg", "eg", "kb", "beta"))
            u = _bdot(tinv, cm["vb"])
            w = _bdot(tinv, kbeg)
            v_new = u - _bdot(w, st)
            egl = jnp.exp(cm["gl"])
            d_qg = _bdot(dov, st, BNT)
            d_intra = _bdot(dov, v_new, BNT)
            dv_new = _bdot(intra, dov, BTN) + _bdot(kdec, ds_out)
            d_kdec = _bdot(v_new, ds_out, BNT)
            ds_in = _bdot(qg, dov, BTN) + egl * ds_out - _bdot(w, dv_new, BTN)
            dgl = egl * jnp.sum(jnp.sum(st * ds_out, axis=2, keepdims=True), axis=1, keepdims=True)
            dw = -_bdot(dv_new, st, BNT)
            dvb = _bdot(tinv, dv_new, BTN)
            dkbeg = _bdot(tinv, dw, BTN)
            dlm = jnp.where(cm["strict"], -(_bdot(dvb, u, BNT) + _bdot(dkbeg, w, BNT)), 0.0)
            d_a = dlm * decay
            d_qk = d_intra * decay
            e = dlm * lm + d_intra * intra
            colsum = _dot3(e, jnp.ones((NB, CHUNK, LANE), F32), BTN, exact_b=True)[:, :, 0:1]
            dgc = jnp.sum(e, axis=2, keepdims=True) - colsum
            dkb = _bdot(d_a, k) + dkbeg * eg
            dk = _bdot(d_a, kb, BTN) + _bdot(d_qk, q, BTN)
            dq = _bdot(d_qk, k) + d_qg * eg
            dgc = dgc + jnp.sum(d_qg * qg, axis=2, keepdims=True) + jnp.sum(dkbeg * kbeg, axis=2, keepdims=True)
            tdec = jnp.sum(d_kdec * kdec, axis=2, keepdims=True)
            dk = dk + d_kdec * cm["edec"] + dkb * beta
            dgc = dgc - tdec
            dgl = dgl + jnp.sum(tdec, axis=1, keepdims=True)
            dbeta = jnp.sum(dvb * v, axis=2, keepdims=True) + jnp.sum(dkb * k, axis=2, keepdims=True)
            dv = dvb * beta
            lane = lax.broadcasted_iota(jnp.int32, (CHUNK, LANE), 1)
            for rev, dy_ref, dg_ref, rows in ((False, dyf_ref, dgf_ref, rf), (True, dyr_ref, dgr_ref, rr)):
                dgc_tile = jnp.zeros((CHUNK, LANE), F32)
                rest = jnp.zeros((CHUNK, LANE), F32)
                for h in range(GDN_HEADS):
                    b = (GDN_HEADS if rev else 0) + h
                    gi, bi = _gate_lanes(rev, h)
                    dgc_tile = dgc_tile + jnp.where(lane == gi, dgc[b], 0.0)
                    rest = rest + jnp.where(lane == gi, dgl[b], 0.0) + jnp.where(lane == bi, dbeta[b], 0.0)
                    dy_ref[h, rows, :] = dq[b]
                    dy_ref[GDN_HEADS + h, rows, :] = dk[b]
                    dy_ref[2 * GDN_HEADS + h, rows, :] = dv[b]
                le_t = _chunk_masks(not rev)[0].astype(F32)
                dg_ref[rows, :] = _dot3(le_t, dgc_tile, NN, exact_a=True) + rest
            return ds_in

        ds = dstate[...]
        for c in range(DELTA_CHUNKS):
            ds = one_chunk(c, ds)
        dstate[...] = ds

    rows_per_step = DELTA_CHUNKS * CHUNK
    ns = nc // DELTA_CHUNKS

    def col(j, rev, blocks=GDN_HEADS):
        return pl.BlockSpec((blocks, rows_per_step, hd), (lambda n: (j, n, 0)) if rev else (lambda n: (j, ns - 1 - n, 0)))

    def wide(width, rev):
        return pl.BlockSpec((rows_per_step, width), (lambda n: (n, 0)) if rev else (lambda n: (ns - 1 - n, 0)))

    def per_chunk(d1, d2, rev):
        return pl.BlockSpec((DELTA_CHUNKS, GDN_HEADS, d1, d2),
                            (lambda n: (n, 0, 0, 0)) if rev else (lambda n: (ns - 1 - n, 0, 0, 0)))

    def side(rev):
        return [col(0, rev), col(1, rev), col(2, rev), wide(LANE, rev), wide(GDN_WIDTH, rev), per_chunk(hd, hd, rev),
                per_chunk(CHUNK, CHUNK, rev)]

    assert nc % DELTA_CHUNKS == 0 and (na == 0 or ns >= 2)
    res = pl.pallas_call(
        body, name="delta_bwd", grid=(ns,),
        in_specs=side(False) + side(True) + [ANY] * na,
        out_specs=[col(0, False, 3 * GDN_HEADS), col(0, True, 3 * GDN_HEADS), wide(LANE, False), wide(LANE, True)]
        + [ANY] * na,
        out_shape=[jax.ShapeDtypeStruct((3 * GDN_HEADS, s, hd), F32)] * 2 + [jax.ShapeDtypeStruct((s, LANE), F32)] * 2
        + [jax.ShapeDtypeStruct(g.shape, g.dtype) for g in scatter],
        scratch_shapes=[pltpu.VMEM((NB, hd, hd), F32)] + (_gather_semaphores(na) if na else []),
        compiler_params=_params("arbitrary"),
    )(y, y, y, gb, do, sf_all, tf_all, y, y, y, gb, do, sr_all, tr_all, *scatter)
    return res[:4], res[4:]


def _gdn_post_fwd(o_f, o_r, p_pad, norm_row):
    s = o_f.shape[0]
    tm = min(512, s)
    hd = GDN_HEAD_DIM

    def body(of_ref, or_ref, z_ref, w_ref, out_ref, osum_ref):
        o = of_ref[...] + or_ref[...]
        osum_ref[...] = o
        z = z_ref[...]
        gate = z * _sigmoid(z)
        for h in range(GDN_HEADS):
            sl = slice(h * hd, (h + 1) * hd)
            oh = o[:, sl]
            r = lax.rsqrt(jnp.mean(oh * oh, axis=-1, keepdims=True) + EPS)
            out_ref[:, sl] = (oh * r * w_ref[...] * gate[:, sl]).astype(BF16)

    blk = pl.BlockSpec((tm, GDN_WIDTH), lambda i: (i, 0))
    return pl.pallas_call(
        body, name="gdn_post_fwd", grid=(s // tm,),
        in_specs=[blk, blk, pl.BlockSpec((tm, GDN_WIDTH), lambda i: (i, OFF_Z // GDN_WIDTH)),
                  pl.BlockSpec((1, hd), lambda i: (0, 0))],
        out_specs=[blk, blk],
        out_shape=[jax.ShapeDtypeStruct((s, GDN_WIDTH), BF16), jax.ShapeDtypeStruct((s, GDN_WIDTH), F32)],
        compiler_params=_params("parallel"),
    )(o_f, o_r, p_pad, norm_row)


def _gdn_post_bwd(d_out, o_sum, p_pad, norm_row):
    s = o_sum.shape[0]
    tm = min(512, s)
    hd = GDN_HEAD_DIM

    def body(d_ref, o_ref, z_ref, w_ref, do_ref, dz_ref, dw_ref):
        @pl.when(pl.program_id(0) == 0)
        def _():
            dw_ref[...] = jnp.zeros_like(dw_ref)

        z = z_ref[...]
        sg = _sigmoid(z)
        gate = z * sg
        dgate = sg * (1.0 + z * (1.0 - sg))
        wv = w_ref[...]
        dw = jnp.zeros((1, hd), F32)
        for h in range(GDN_HEADS):
            sl = slice(h * hd, (h + 1) * hd)
            oh = o_ref[:, sl]
            dh = d_ref[:, sl]
            r = lax.rsqrt(jnp.mean(oh * oh, axis=-1, keepdims=True) + EPS)
            ohat = oh * r
            dz_ref[:, sl] = (dh * ohat * wv * dgate[:, sl]).astype(BF16)
            drn = dh * gate[:, sl]
            t = drn * wv
            do_ref[:, sl] = r * (t - ohat * jnp.mean(t * ohat, axis=-1, keepdims=True))
            dw = dw + jnp.sum(drn * ohat, axis=0, keepdims=True)
        dw_ref[...] += dw

    blk = pl.BlockSpec((tm, GDN_WIDTH), lambda i: (i, 0))
    vec = pl.BlockSpec((1, hd), lambda i: (0, 0))
    return pl.pallas_call(
        body, name="gdn_post_bwd", grid=(s // tm,),
        in_specs=[blk, blk, pl.BlockSpec((tm, GDN_WIDTH), lambda i: (i, OFF_Z // GDN_WIDTH)), vec],
        out_specs=[blk, pl.BlockSpec((tm, GDN_WIDTH), lambda i: (i, OFF_Z // GDN_WIDTH)), vec],
        out_shape=[jax.ShapeDtypeStruct((s, GDN_WIDTH), F32), jax.ShapeDtypeStruct((s, N_PAD), BF16),
                   jax.ShapeDtypeStruct((1, hd), F32)],
        compiler_params=_params("arbitrary"),
    )(d_out, o_sum, p_pad, norm_row)


def _gdn_forward(p_pad, conv_wt, alog_row, dt_row, norm_row, gather=()):
    c_pre, y = _gdn_conv_fwd(p_pad, conv_wt)
    gb = _gdn_gates_fwd(p_pad, alog_row, dt_row)
    (o_f, o_r, s_f, s_r, t_f, t_r), gathered = _delta_fwd2(y, gb, gather)
    out, o_sum = _gdn_post_fwd(o_f, o_r, p_pad, norm_row)
    return out, (c_pre, y, gb, s_f, t_f, s_r, t_r, o_sum), gathered


def _gdn_backward(d_out, p_pad, conv_wt, alog_row, dt_row, norm_row, saved, scatter=()):
    c_pre, y, gb, s_f, t_f, s_r, t_r, o_sum = saved
    do, dp_all, dnorm = _gdn_post_bwd(d_out, o_sum, p_pad, norm_row)
    (dy_f, dy_r, dgb_f, dgb_r), received = _delta_bwd2(y, gb, do, s_f, s_r, t_f, t_r, scatter)
    dp_all, dconv = _gdn_conv_bwd(dy_f, dy_r, c_pre, p_pad, conv_wt, dp_all)
    dp_all, gate_sums = _gdn_gates_bwd(dgb_f, dgb_r, p_pad, gb, alog_row, dt_row, dp_all)
    return dp_all, dconv, gate_sums, dnorm, received


ATT_BK = ATT_BQ + 2 * ATT_HALO
ATT_SUB = 8
SWA_SCALE = SWA_HEAD_DIM ** -0.5


def _t5_bucket(rel):
    nb = REL_BUCKETS // 2
    bucket = (rel > 0).astype(np.int32) * nb
    n = np.abs(rel)
    max_exact = nb // 2
    large = max_exact + (np.log(np.maximum(n, 1) / max_exact)
                         / math.log(REL_MAX_DISTANCE / max_exact) * (nb - max_exact)).astype(np.int32)
    large = np.minimum(large, nb - 1)
    return (bucket + np.where(n < max_exact, n, large)).astype(np.int32)


def _band_tables(dilation, queries_are_rows_of_block):
    blk = np.arange(ATT_BQ)
    band = np.arange(ATT_BK) - ATT_HALO
    if queries_are_rows_of_block:
        rel = band[None, :] - blk[:, None]
        band_idx = np.broadcast_to(np.arange(ATT_BK)[None, :], rel.shape)
    else:
        rel = blk[None, :] - band[:, None]
        band_idx = np.broadcast_to(np.arange(ATT_BK)[:, None], rel.shape)
    base = np.abs(rel) <= ATT_HALO
    not_prev = band_idx >= ATT_HALO
    not_next = band_idx < ATT_HALO + ATT_BQ
    valid = np.stack([base & not_prev, base, base & not_next, base & not_prev & not_next])
    return valid, _t5_bucket(rel * dilation)


def _bias_tiles(rel_bias, dilation, queries_are_rows_of_block):
    valid, bucket = _band_tables(dilation, queries_are_rows_of_block)
    onehot = (jnp.asarray(bucket.reshape(-1, 1)) == jnp.arange(REL_BUCKETS, dtype=jnp.int32)[None, :]).astype(F32)
    rb = jnp.dot(onehot, rel_bias.astype(F32), precision=lax.Precision.HIGHEST)
    rb = rb.T.reshape((SWA_HEADS,) + bucket.shape)
    return jnp.where(valid[:, None], rb[None], NEG_BIG).astype(F32)


def _group_sum(x, bd):
    hi = x.astype(BF16)
    lo = (x - hi.astype(F32)).astype(BF16)
    return jnp.dot(hi, bd, preferred_element_type=F32) + jnp.dot(lo, bd, preferred_element_type=F32)


def _head_block_diag():
    idx = np.arange(SWA_WIDTH) // SWA_HEAD_DIM
    return jnp.asarray(idx[:, None] == idx[None, :], BF16)


def _swa_pre_fwd(p_pad, qw_row, kw_row, bd):
    s = p_pad.shape[0]
    tm = min(512, s)
    inv = 1.0 / SWA_HEAD_DIM

    def body(q_ref, k_ref, v_ref, qw_ref, kw_ref, bd_ref, qo_ref, ko_ref, vo_ref):
        bdv = bd_ref[...]
        q = q_ref[...]
        k = k_ref[...]
        rq = lax.rsqrt(_group_sum(q * q, bdv) * inv + EPS)
        rk = lax.rsqrt(_group_sum(k * k, bdv) * inv + EPS)
        qo_ref[...] = (q * rq * qw_ref[...] * SWA_SCALE).astype(BF16)
        ko_ref[...] = (k * rk * kw_ref[...]).astype(BF16)
        vo_ref[...] = v_ref[...].astype(BF16)

    base = OFF_B // SWA_WIDTH
    blk = pl.BlockSpec((tm, SWA_WIDTH), lambda i: (i, 0))
    vec = pl.BlockSpec((1, SWA_WIDTH), lambda i: (0, 0))
    return pl.pallas_call(
        body, name="swa_pre_fwd", grid=(s // tm,),
        in_specs=[pl.BlockSpec((tm, SWA_WIDTH), lambda i: (i, base)), pl.BlockSpec((tm, SWA_WIDTH), lambda i: (i, base + 1)),
                  pl.BlockSpec((tm, SWA_WIDTH), lambda i: (i, base + 2)), vec, vec,
                  pl.BlockSpec((SWA_WIDTH, SWA_WIDTH), lambda i: (0, 0))],
        out_specs=[blk, blk, blk],
        out_shape=[jax.ShapeDtypeStruct((s, SWA_WIDTH), BF16)] * 3,
        compiler_params=_params("parallel"),
    )(p_pad, p_pad, p_pad, qw_row, kw_row, bd)


def _swa_pre_bwd(dqs, dks, dvs, p_pad, qw_row, kw_row, bd, dp_all):
    s = p_pad.shape[0]
    tm = min(256, s)
    inv = 1.0 / SWA_HEAD_DIM
    npat = len(dqs)

    def body(*refs):
        dq_refs, dk_refs, dv_refs = refs[:npat], refs[npat:2 * npat], refs[2 * npat:3 * npat]
        q_ref, k_ref, qw_ref, kw_ref, bd_ref, _, dp_ref, dqw_ref, dkw_ref = refs[3 * npat:]

        @pl.when(pl.program_id(0) == 0)
        def _():
            dqw_ref[...] = jnp.zeros_like(dqw_ref)
            dkw_ref[...] = jnp.zeros_like(dkw_ref)

        bdv = bd_ref[...]

        def norm_bwd(x, g, w, scale):
            r = lax.rsqrt(_group_sum(x * x, bdv) * inv + EPS)
            xhat = x * r
            t = g * w * scale
            dx = r * (t - xhat * (_group_sum(t * xhat, bdv) * inv))
            return dx, jnp.sum(g * scale * xhat, axis=0, keepdims=True)

        def total(rs):
            t = rs[0][...].astype(F32)
            for r in rs[1:]:
                t = t + r[...].astype(F32)
            return t

        dq, dqw = norm_bwd(q_ref[...], total(dq_refs), qw_ref[...], SWA_SCALE)
        dk, dkw = norm_bwd(k_ref[...], total(dk_refs), kw_ref[...], 1.0)
        dp_ref[:, 0:SWA_WIDTH] = dq.astype(BF16)
        dp_ref[:, SWA_WIDTH:2 * SWA_WIDTH] = dk.astype(BF16)
        dp_ref[:, 2 * SWA_WIDTH:3 * SWA_WIDTH] = total(dv_refs).astype(BF16)
        dqw_ref[...] += dqw
        dkw_ref[...] += dkw

    base = OFF_B // SWA_WIDTH
    blk = pl.BlockSpec((tm, SWA_WIDTH), lambda i: (i, 0))
    vec = pl.BlockSpec((1, SWA_WIDTH), lambda i: (0, 0))
    return pl.pallas_call(
        body, name="swa_pre_bwd", grid=(s // tm,),
        in_specs=[blk] * (3 * npat) + [pl.BlockSpec((tm, SWA_WIDTH), lambda i: (i, base)),
                                      pl.BlockSpec((tm, SWA_WIDTH), lambda i: (i, base + 1)), vec, vec,
                                      pl.BlockSpec((SWA_WIDTH, SWA_WIDTH), lambda i: (0, 0)), ANY],
        out_specs=[pl.BlockSpec((tm, 3 * SWA_WIDTH), lambda i: (i, OFF_B // (3 * SWA_WIDTH))), vec, vec],
        out_shape=[jax.ShapeDtypeStruct(dp_all.shape, dp_all.dtype), jax.ShapeDtypeStruct((1, SWA_WIDTH), F32),
                   jax.ShapeDtypeStruct((1, SWA_WIDTH), F32)],
        input_output_aliases={3 * npat + 5: 0},
        compiler_params=_params("arbitrary"),
    )(*dqs, *dks, *dvs, p_pad, p_pad, qw_row, kw_row, bd, dp_all)


def _band_specs(length, rows):
    per = rows // ATT_HALO
    last = length // ATT_HALO - 1
    prev = pl.BlockSpec((ATT_HALO, SWA_WIDTH), lambda r, t: (jnp.maximum(t * per - 1, 0), r))
    cur = pl.BlockSpec((rows, SWA_WIDTH), lambda r, t: (t, r))
    nxt = pl.BlockSpec((ATT_HALO, SWA_WIDTH), lambda r, t: (jnp.minimum((t + 1) * per, last), r))
    return [prev, cur, nxt]


def _tile_variant(t, nb, u, sub):
    first, last = u == 0, u == sub - 1
    if first and last:
        return 3 if nb == 1 else jnp.where(t == 0, 0, jnp.where(t == nb - 1, 2, 1))
    if first:
        return jnp.where(t == 0, 0, 1)
    if last:
        return jnp.where(t == nb - 1, 2, 1)
    return 1


def _bias_specs(nb, sub, rows, cols):
    return [pl.BlockSpec((1, SWA_HEADS, rows, cols),
                         functools.partial(lambda r, t, u: (_tile_variant(t, nb, u, sub), 0, 0, 0), u=u))
            for u in range(sub)]


def _band(refs):
    return jnp.concatenate([r[...] for r in refs], axis=0)


def _sub(u, width=ATT_BQ):
    return slice(u * ATT_BQ, u * ATT_BQ + width)


N_PAIRS = SWA_HEADS // 2


def _pairs(x):
    return jnp.stack([x[:, LANE * p:LANE * (p + 1)] for p in range(N_PAIRS)])


def _per_head_rows(x):
    first = lax.broadcasted_iota(jnp.int32, x.shape, 2) < SWA_HEAD_DIM
    zero = jnp.zeros_like(x)
    return jnp.concatenate([jnp.where(first, x, zero), jnp.where(first, zero, x)], axis=1)


def _per_head_cols(x):
    return jnp.stack([jnp.concatenate([x[:, LANE * p:LANE * p + 1],
                                       x[:, LANE * p + SWA_HEAD_DIM:LANE * p + SWA_HEAD_DIM + 1]], axis=0)
                      for p in range(N_PAIRS)])


def _merge_heads(x, rows):
    first = lax.broadcasted_iota(jnp.int32, (N_PAIRS, rows, LANE), 2) < SWA_HEAD_DIM
    return jnp.where(first, x[:, :rows], x[:, rows:])


def _store_pairs(ref, x, rows):
    for p in range(N_PAIRS):
        ref[rows, LANE * p:LANE * (p + 1)] = x[p].astype(ref.dtype)


def _att_fwd2(q, k, v, bias, dilation):
    s = q.shape[0]
    length = s // dilation
    sub = min(ATT_SUB, length // ATT_BQ)
    rows = sub * ATT_BQ
    nb = length // rows
    view = (length, dilation * SWA_WIDTH)

    def body(q_ref, kp, kc, kn, vp, vc, vn, *rest):
        b_refs, (o_ref, lse_ref) = rest[:sub], rest[sub:]
        kwin, vwin = _band((kp, kc, kn)), _band((vp, vc, vn))
        for u in range(sub):
            kb, vb = _pairs(kwin[_sub(u, ATT_BK)]), _pairs(vwin[_sub(u, ATT_BK)])
            qm = _per_head_rows(_pairs(q_ref[_sub(u), :]))
            sc = _bdot(qm, kb, BNT) + b_refs[u][0].reshape(N_PAIRS, 2 * ATT_BQ, ATT_BK)
            m = jnp.max(sc, axis=-1, keepdims=True)
            p = jnp.exp(sc - m)
            den = jnp.sum(p, axis=-1, keepdims=True)
            o = _bdot(p, vb) / den
            _store_pairs(o_ref, _merge_heads(o, ATT_BQ), _sub(u))
            lse = jnp.broadcast_to(m + jnp.log(den), (N_PAIRS, 2 * ATT_BQ, LANE))
            _store_pairs(lse_ref, _merge_heads(lse, ATT_BQ), _sub(u))

    cur = pl.BlockSpec((rows, SWA_WIDTH), lambda r, t: (t, r))
    o, lse = pl.pallas_call(
        body, name=f"att_fwd_d{dilation}", grid=(dilation, nb),
        in_specs=[cur] + _band_specs(length, rows) * 2 + _bias_specs(nb, sub,ATT_BQ, ATT_BK),
        out_specs=[cur, cur],
        out_shape=[jax.ShapeDtypeStruct(view, BF16), jax.ShapeDtypeStruct(view, F32)],
        compiler_params=_params("parallel", "parallel"),
    )(q.reshape(view), *([k.reshape(view)] * 3), *([v.reshape(view)] * 3), *([bias] * sub))
    return o.reshape(s, SWA_WIDTH), lse.reshape(s, SWA_WIDTH)


def _att_dq2(q, k, v, dop, lse, cp, bias, dilation):
    s = q.shape[0]
    length = s // dilation
    sub = min(ATT_SUB, length // ATT_BQ)
    rows = sub * ATT_BQ
    nb = length // rows
    view = (length, dilation * SWA_WIDTH)

    def body(q_ref, kp, kc, kn, vp, vc, vn, do_ref, lse_ref, cp_ref, *rest):
        b_refs, (dq_ref, db_ref) = rest[:sub], rest[sub:]

        @pl.when((pl.program_id(0) == 0) & (pl.program_id(1) == 0))
        def _():
            db_ref[...] = jnp.zeros_like(db_ref)

        kwin, vwin = _band((kp, kc, kn)), _band((vp, vc, vn))
        for u in range(sub):
            kb, vb = _pairs(kwin[_sub(u, ATT_BK)]), _pairs(vwin[_sub(u, ATT_BK)])
            qm = _per_head_rows(_pairs(q_ref[_sub(u), :]))
            dom = _per_head_rows(_pairs(do_ref[_sub(u), :]))
            sc = _bdot(qm, kb, BNT) + b_refs[u][0].reshape(N_PAIRS, 2 * ATT_BQ, ATT_BK)
            p = jnp.exp(sc - _per_head_cols(lse_ref[_sub(u), :]))
            ds = p * (_bdot(dom, vb, BNT) + _per_head_cols(cp_ref[_sub(u), :]))
            _store_pairs(dq_ref, _merge_heads(_bdot(ds, kb), ATT_BQ), _sub(u))
            db_ref[_tile_variant(pl.program_id(1), nb, u, sub)] += ds.reshape(SWA_HEADS, ATT_BQ, ATT_BK)

    cur = pl.BlockSpec((rows, SWA_WIDTH), lambda r, t: (t, r))
    dq, db = pl.pallas_call(
        body, name=f"att_dq_d{dilation}", grid=(dilation, nb),
        in_specs=[cur] + _band_specs(length, rows) * 2 + [cur, cur, cur] + _bias_specs(nb, sub,ATT_BQ, ATT_BK),
        out_specs=[cur, pl.BlockSpec((4, SWA_HEADS, ATT_BQ, ATT_BK), lambda r, t: (0, 0, 0, 0))],
        out_shape=[jax.ShapeDtypeStruct(view, BF16), jax.ShapeDtypeStruct((4, SWA_HEADS, ATT_BQ, ATT_BK), F32)],
        compiler_params=_params("arbitrary", "arbitrary"),
    )(q.reshape(view), *([k.reshape(view)] * 3), *([v.reshape(view)] * 3), dop.reshape(view), lse.reshape(view),
      cp.reshape(view), *([bias] * sub))
    return dq.reshape(s, SWA_WIDTH), db


def _att_dkv2(q, k, v, dop, lse, cp, bias_t, dilation):
    s = q.shape[0]
    length = s // dilation
    sub = min(ATT_SUB, length // ATT_BQ)
    rows = sub * ATT_BQ
    nb = length // rows
    view = (length, dilation * SWA_WIDTH)

    def body(k_ref, v_ref, qp, qc, qn, dp_, dc_, dn_, lp, lc, ln, cp_, cc_, cn_, *rest):
        b_refs, (dk_ref, dv_ref) = rest[:sub], rest[sub:]
        qwin, dowin = _band((qp, qc, qn)), _band((dp_, dc_, dn_))
        lsewin, cpwin = _band((lp, lc, ln)), _band((cp_, cc_, cn_))
        for u in range(sub):
            band = _sub(u, ATT_BK)
            qm = _per_head_rows(_pairs(qwin[band]))
            dom = _per_head_rows(_pairs(dowin[band]))
            kv, vv = _pairs(k_ref[_sub(u), :]), _pairs(v_ref[_sub(u), :])
            sc = _bdot(qm, kv, BNT) + b_refs[u][0].reshape(N_PAIRS, 2 * ATT_BK, ATT_BQ)
            p = jnp.exp(sc - _per_head_cols(lsewin[band]))
            _store_pairs(dv_ref, _bdot(p, dom, BTN), _sub(u))
            ds = p * (_bdot(dom, vv, BNT) + _per_head_cols(cpwin[band]))
            _store_pairs(dk_ref, _bdot(ds, qm, BTN), _sub(u))

    cur = pl.BlockSpec((rows, SWA_WIDTH), lambda r, t: (t, r))
    dk, dv = pl.pallas_call(
        body, name=f"att_dkv_d{dilation}", grid=(dilation, nb),
        in_specs=[cur, cur] + _band_specs(length, rows) * 4 + _bias_specs(nb, sub,ATT_BK, ATT_BQ),
        out_specs=[cur, cur],
        out_shape=[jax.ShapeDtypeStruct(view, BF16)] * 2,
        compiler_params=_params("parallel", "parallel"),
    )(k.reshape(view), v.reshape(view), *([q.reshape(view)] * 3), *([dop.reshape(view)] * 3),
      *([lse.reshape(view)] * 3), *([cp.reshape(view)] * 3), *([bias_t] * sub))
    return dk.reshape(s, SWA_WIDTH), dv.reshape(s, SWA_WIDTH)


def _pattern_weights(lses):
    m = lses[0]
    for l in lses[1:]:
        m = jnp.maximum(m, l)
    es = [jnp.exp(l - m) for l in lses]
    den = es[0]
    for e in es[1:]:
        den = den + e
    return [e / den for e in es]


def _combine_fwd(outs, lses):
    s = outs[0].shape[0]
    tm = min(512, s)
    npat = len(outs)

    def body(*refs):
        ws = _pattern_weights([r[...] for r in refs[npat:2 * npat]])
        o = ws[0] * refs[0][...]
        for p in range(1, npat):
            o = o + ws[p] * refs[p][...]
        refs[2 * npat][...] = o.astype(BF16)

    blk = pl.BlockSpec((tm, SWA_WIDTH), lambda i: (i, 0))
    return pl.pallas_call(
        body, name="swa_combine_fwd", grid=(s // tm,), in_specs=[blk] * (2 * npat), out_specs=blk,
        out_shape=jax.ShapeDtypeStruct((s, SWA_WIDTH), BF16), compiler_params=_params("parallel"),
    )(*outs, *lses)


def _combine_bwd(d_out, outs, lses, bd):
    s = d_out.shape[0]
    tm = min(512, s)
    npat = len(outs)

    def body(*refs):
        d_ref, bd_ref = refs[0], refs[1 + 2 * npat]
        o_refs, l_refs = refs[1:1 + npat], refs[1 + npat:1 + 2 * npat]
        out_refs = refs[2 + 2 * npat:]
        ws = _pattern_weights([r[...] for r in l_refs])
        dov = d_ref[...]
        o = ws[0] * o_refs[0][...]
        for p in range(1, npat):
            o = o + ws[p] * o_refs[p][...]
        rd = _group_sum(dov * o, bd_ref[...])
        for p in range(npat):
            out_refs[p][...] = (ws[p] * dov).astype(BF16)
            out_refs[npat + p][...] = -ws[p] * rd

    blk = pl.BlockSpec((tm, SWA_WIDTH), lambda i: (i, 0))
    res = pl.pallas_call(
        body, name="swa_combine_bwd", grid=(s // tm,),
        in_specs=[blk] * (1 + 2 * npat) + [pl.BlockSpec((SWA_WIDTH, SWA_WIDTH), lambda i: (0, 0))],
        out_specs=[blk] * (2 * npat),
        out_shape=[jax.ShapeDtypeStruct((s, SWA_WIDTH), BF16)] * npat + [jax.ShapeDtypeStruct((s, SWA_WIDTH), F32)] * npat,
        compiler_params=_params("parallel"),
    )(d_out, *outs, *lses, bd)
    return res[:npat], res[npat:]


def _rel_bias_grad(dbs, buckets):
    npat = len(dbs)

    def body(*refs):
        db_refs, bk_refs, o_ref = refs[:npat], refs[npat:2 * npat], refs[2 * npat]
        row = lax.broadcasted_iota(jnp.int32, (REL_BUCKETS, LANE), 0)
        lane = lax.broadcasted_iota(jnp.int32, (REL_BUCKETS, LANE), 1)
        tiles = [[db_refs[p][0, h] + db_refs[p][1, h] + db_refs[p][2, h] + db_refs[p][3, h] for h in range(SWA_HEADS)]
                 for p in range(npat)]
        bks = [r[...] for r in bk_refs]

        def one_bucket(b, acc):
            for h in range(SWA_HEADS):
                tot = jnp.zeros((1, 1), F32)
                for p in range(npat):
                    sel = jnp.where(bks[p] == b, tiles[p][h], 0.0)
                    tot = tot + jnp.sum(jnp.sum(sel, axis=1, keepdims=True), axis=0, keepdims=True)
                acc = acc + jnp.where((row == b) & (lane == h), tot, 0.0)
            return acc

        o_ref[...] = lax.fori_loop(0, REL_BUCKETS, one_bucket, jnp.zeros((REL_BUCKETS, LANE), F32))

    full4 = pl.BlockSpec((4, SWA_HEADS, ATT_BQ, ATT_BK), lambda: (0, 0, 0, 0))
    full2 = pl.BlockSpec((ATT_BQ, ATT_BK), lambda: (0, 0))
    return pl.pallas_call(
        body, name="rel_bias_grad", in_specs=[full4] * npat + [full2] * npat,
        out_specs=pl.BlockSpec((REL_BUCKETS, LANE), lambda: (0, 0)),
        out_shape=jax.ShapeDtypeStruct((REL_BUCKETS, LANE), F32),
        compiler_params=pltpu.CompilerParams(vmem_limit_bytes=V7X_VMEM_LIMIT_BYTES),
    )(*dbs, *buckets)


def _swa_forward(p_pad, qw_row, kw_row, rel_bias, bd):
    q, k, v = _swa_pre_fwd(p_pad, qw_row, kw_row, bd)
    outs, lses = [], []
    for _, dil in DILATION_PATTERNS:
        o, lse = _att_fwd2(q, k, v, _bias_tiles(rel_bias, dil, True), dil)
        outs.append(o)
        lses.append(lse)
    return _combine_fwd(outs, lses), (q, k, v, outs, lses)


def _swa_backward(d_out, p_pad, qw_row, kw_row, rel_bias, bd, saved, dp_all):
    q, k, v, outs, lses = saved
    dops, cps = _combine_bwd(d_out, outs, lses, bd)
    dqs, dks, dvs, dbs, buckets = [], [], [], [], []
    for p, (_, dil) in enumerate(DILATION_PATTERNS):
        dq, db = _att_dq2(q, k, v, dops[p], lses[p], cps[p], _bias_tiles(rel_bias, dil, True), dil)
        dk, dv = _att_dkv2(q, k, v, dops[p], lses[p], cps[p], _bias_tiles(rel_bias, dil, False), dil)
        dqs.append(dq)
        dks.append(dk)
        dvs.append(dv)
        dbs.append(db)
        buckets.append(jnp.asarray(_band_tables(dil, True)[1]))
    dp, dqw, dkw = _swa_pre_bwd(dqs, dks, dvs, p_pad, qw_row, kw_row, bd, dp_all)
    return dp, dqw, dkw, _rel_bias_grad(dbs, buckets)


def _lane_row(v):
    flat = v.reshape(-1).astype(F32)
    return jnp.zeros((1, LANE), F32).at[0, :flat.shape[0]].set(flat)


W_IN_SHARD = N_IN // N_DEV
W_IN_RUNS = ((0, NAT_Z, 0), (NAT_Z, NAT_AB, OFF_Z), (NAT_AB, NAT_B, OFF_AB), (NAT_B, N_IN, OFF_B))


def _w_in_pieces(shard):
    lo, hi = shard * W_IN_SHARD, (shard + 1) * W_IN_SHARD
    out = []
    for first, last, dst in W_IN_RUNS:
        a, b = max(lo, first), min(hi, last)
        if a < b:
            out.append((a - lo, b - a, dst + a - first))
    return out


def _w_in_from_slabs(w3):
    nd, r, _ = w3.shape

    def body(w_ref, o_ref):
        o_ref[:, OFF_AB:N_PAD] = jnp.zeros((r, N_PAD - OFF_AB), w3.dtype)
        for sh in range(nd):
            for src, length, dst in _w_in_pieces(sh):
                o_ref[:, dst:dst + length] = w_ref[sh, :, src:src + length]

    return pl.pallas_call(
        body, name="w_in_from_slabs", out_shape=jax.ShapeDtypeStruct((r, N_PAD), w3.dtype),
        compiler_params=pltpu.CompilerParams(vmem_limit_bytes=V7X_VMEM_LIMIT_BYTES),
    )(w3)


def _w_in_grad_slabs(dw_pad, dtype):
    r = dw_pad.shape[0]

    def body(dw_ref, o_ref):
        for sh in range(N_DEV):
            for src, length, dst in _w_in_pieces(sh):
                o_ref[sh, :, src:src + length] = dw_ref[:, dst:dst + length].astype(dtype)

    return pl.pallas_call(
        body, name="w_in_grad_slabs", out_shape=jax.ShapeDtypeStruct((N_DEV, r, W_IN_SHARD), dtype),
        compiler_params=pltpu.CompilerParams(vmem_limit_bytes=V7X_VMEM_LIMIT_BYTES),
    )(dw_pad)


LATE = ("w_out", "ffn2_w_gate", "ffn2_w_up", "ffn2_w_down")
TRANSPOSED = ("ffn1_w_gate", "ffn1_w_up", "ffn2_w_gate", "ffn2_w_up")


def _late_weights(slabs):
    return {n: g.reshape(N_DEV * g.shape[1], g.shape[2]) for n, g in zip(LATE, slabs)}


def _local_step(x, tgt, wts, small, late_shards=None):
    bd = _head_block_diag()
    alog_row, dt_row = _lane_row(small["a_log"]), _lane_row(small["dt_bias"])
    gnorm_row = small["gdn_norm_w"].reshape(1, GDN_HEAD_DIM)
    qw_row = jnp.tile(small["q_norm_w"].reshape(-1), SWA_HEADS).reshape(1, SWA_WIDTH)
    kw_row = jnp.tile(small["k_norm_w"].reshape(-1), SWA_HEADS).reshape(1, SWA_WIDTH)
    rel_bias = small["rel_bias"]
    exchange = late_shards is not None
    dw_dtype = BF16 if exchange else F32

    x1, sv1, (wg1, wu1, wd1), got, first, (n2, r2) = _ffn_forward(
        x, small["ffn1_norm"], wts.get("ffn1_w_gate"), wts.get("ffn1_w_up"), wts.get("ffn1_w_down"), "ffn1",
        gather=[late_shards["ffn1_w_down"], late_shards["w_in"]] if exchange else (),
        head=[late_shards["ffn1_w_gate"], late_shards["ffn1_w_up"], late_shards["conv_w"]] if exchange else (),
        next_norm=small["mix_norm"])
    win_pad = _w_in_from_slabs(got[0]) if exchange else wts["w_in_pad"]
    conv_w = first[0].reshape(N_DEV, -1)[:, :QKV_A // N_DEV * CONV_WIDTH].reshape(QKV_A, CONV_WIDTH) if exchange \
        else small["conv_w"]
    conv_wt = jnp.zeros((8, QKV_A), F32).at[:CONV_WIDTH].set(conv_w.T)
    p_pad = _matmul([(n2, win_pad)], tm=256, tn=N_PAD, tk=D_MODEL, name="w_in")
    o_a, sva, gathered = _gdn_forward(p_pad, conv_wt, alog_row, dt_row, gnorm_row,
                                      gather=[late_shards[n] for n in LATE] if exchange else ())
    if exchange:
        wts = {**wts, **_late_weights(gathered)}
    wo_a, wo_b = wts["w_out"][:GDN_WIDTH], wts["w_out"][GDN_WIDTH:]
    o_b, svb = _swa_forward(p_pad, qw_row, kw_row, rel_bias, bd)
    x2, n3, r3 = _matmul([(o_a, wo_a), (o_b, wo_b)], tm=512, tn=D_MODEL, tk=GDN_WIDTH, name="w_out", res=x1,
                         norm_fwd=small["ffn2_norm"])
    (loss_row, dx3, d_final), sv2, _, _, _, _ = _ffn_forward(
        x2, small["ffn2_norm"], wts["ffn2_w_gate"], wts["ffn2_w_up"], wts["ffn2_w_down"], "ffn2", normed=(n3, r3),
        loss=(small["final_norm"], tgt))

    dx2, d_ffn2_norm, dwg2, dwu2, dwd2, _ = _ffn_backward(
        dx3, x2, small["ffn2_norm"], wts["ffn2_w_gate"], wts["ffn2_w_up"], wts["ffn2_w_down"], sv2, "ffn2", dw_dtype)
    d_oa = _matmul([(dx2, wo_a)], tb=True, tm=512, tn=GDN_WIDTH, tk=D_MODEL, name="w_out_da")
    d_ob = _matmul([(dx2, wo_b)], tb=True, tm=512, tn=SWA_WIDTH, tk=D_MODEL, name="w_out_db")
    dwo_a = _matmul([(o_a, dx2)], ta=True, tm=GDN_WIDTH, tn=D_MODEL, tk=2048, name="w_out_dwa", out_dtype=dw_dtype)
    dwo_b = _matmul([(o_b, dx2)], ta=True, tm=SWA_WIDTH, tn=D_MODEL, tk=2048, name="w_out_dwb", out_dtype=dw_dtype)

    late_grads = [_row_slabs(jnp.concatenate([dwo_a, dwo_b], axis=0)), dwg2, dwu2, dwd2]
    dp_all, dconv, gate_sums, d_gnorm, received = _gdn_backward(
        d_oa, p_pad, conv_wt, alog_row, dt_row, gnorm_row, sva, scatter=late_grads if exchange else ())
    if exchange:
        late_grads = received
    dp_all, dqw, dkw, d_rel = _swa_backward(d_ob, p_pad, qw_row, kw_row, rel_bias, bd, svb, dp_all)
    dw_pad = _matmul([(n2, dp_all)], ta=True, tm=D_MODEL, tn=N_PAD // 3, tk=2048, name="w_in_dw")
    dx1, d_mix_norm = _matmul([(dp_all, win_pad)], tb=True, tm=512, tn=D_MODEL, tk=N_PAD, name="w_in_dn",
                              norm_bwd=(x1, r2, small["mix_norm"], dx2))
    d_w_in = _w_in_grad_slabs(dw_pad, dw_dtype)
    dx, d_ffn1_norm, dwg1, dwu1, dwd1, got = _ffn_backward(
        dx1, x, small["ffn1_norm"], wg1, wu1, wd1, sv1, "ffn1", dw_dtype,
        scatter=[d_w_in] if exchange else None)
    if exchange:
        d_w_in = got[0]

    grads = {
        "ffn1_norm": d_ffn1_norm, "ffn1_w_gate": dwg1, "ffn1_w_up": dwu1, "ffn1_w_down": dwd1,
        "mix_norm": d_mix_norm, "w_in": d_w_in, "conv_w": dconv[:CONV_WIDTH].T,
        "a_log": gate_sums[0, :8].reshape(2, GDN_HEADS), "dt_bias": gate_sums[1, :8].reshape(2, GDN_HEADS),
        "gdn_norm_w": d_gnorm, "q_norm_w": dqw.reshape(SWA_HEADS, SWA_HEAD_DIM).sum(0, keepdims=True),
        "k_norm_w": dkw.reshape(SWA_HEADS, SWA_HEAD_DIM).sum(0, keepdims=True), "rel_bias": d_rel[:, :SWA_HEADS],
        "ffn2_norm": d_ffn2_norm, "final_norm": d_final, **dict(zip(LATE, late_grads)),
    }
    return loss_row, dx, grads


MESH_IDS = pl.DeviceIdType.MESH
ANY = pl.BlockSpec(memory_space=pl.ANY)


def _adamw(parts, w, m, v, name):
    nparts, r, n = parts.shape
    tr = r
    for cand in (256, 176, 128, 104, 64, 8):
        if r % cand == 0:
            tr = cand
            break
    bc1 = 1.0 - ADAM_B1 ** ADAM_STEP
    bc2 = 1.0 - ADAM_B2 ** ADAM_STEP

    def body(p_ref, w_ref, m_ref, v_ref, g_ref, d_ref, nm_ref, nv_ref):
        g = p_ref[0].astype(F32)
        for k in range(1, nparts):
            g = g + p_ref[k].astype(F32)
        mn = ADAM_B1 * m_ref[...] + (1.0 - ADAM_B1) * g
        vn = ADAM_B2 * v_ref[...] + (1.0 - ADAM_B2) * (g * g)
        m_hat = mn / bc1
        v_hat = vn / bc2
        g_ref[...] = g
        nm_ref[...] = mn
        nv_ref[...] = vn
        d_ref[...] = -ADAM_LR * (m_hat / (jnp.sqrt(v_hat) + ADAM_EPS) + ADAM_WD * w_ref[...])

    blk = pl.BlockSpec((tr, n), lambda i: (i, 0))
    return pl.pallas_call(
        body, name=name, grid=(r // tr,),
        in_specs=[pl.BlockSpec((nparts, tr, n), lambda i: (0, i, 0)), blk, blk, blk],
        out_specs=[blk] * 4, out_shape=[jax.ShapeDtypeStruct((r, n), F32)] * 4,
        compiler_params=_params("parallel"),
    )(parts, w, m, v)


def _mesh_place():
    x, y, c = lax.axis_index("x"), lax.axis_index("y"), lax.axis_index("c")
    return x, y, c, [(1 - x, y), (x, 1 - y), (1 - x, 1 - y)]


def _gather_phases(x_refs, out_refs, send_sems, recv_sems, local_sems):
    na = len(x_refs)

    def place():
        x, y, c, chips = _mesh_place()
        return (x, y, c), (x, y, 1 - c), chips, c

    def slab(i, px, py, pc):
        return out_refs[i].at[4 * px + 2 * py + pc]

    def copy(i, k, block, to, src=None):
        return pltpu.make_async_remote_copy(
            src_ref=slab(i, *block) if src is None else src, dst_ref=slab(i, *block),
            send_sem=send_sems.at[i, k], recv_sem=recv_sems.at[i, k], device_id=to, device_id_type=MESH_IDS)

    def own(i, me):
        return pltpu.make_async_copy(x_refs[i], slab(i, *me), local_sems.at[i])

    def sends(i, me, sibling, chips, c):
        return [copy(i, 0, me, sibling, src=x_refs[i])] + [copy(i, 1 + j, me, (*chip, c), src=x_refs[i])
                                                          for j, chip in enumerate(chips)]

    def start():
        me, sibling, chips, c = place()
        for i in range(na):
            own(i, me).start()
            for cp in sends(i, me, sibling, chips, c):
                cp.start()

    def forward():
        me, sibling, chips, c = place()
        for j, chip in enumerate(chips):
            for i in range(na):
                copy(i, 1 + j, (*chip, c), me).wait_recv()
                copy(i, 4 + j, (*chip, c), sibling).start()

    def finish():
        me, sibling, chips, c = place()
        for i in range(na):
            copy(i, 0, sibling, me).wait_recv()
        for j, chip in enumerate(chips):
            for i in range(na):
                copy(i, 4 + j, (*chip, 1 - c), me).wait_recv()
        for i in range(na):
            for cp in sends(i, me, sibling, chips, c):
                cp.wait_send()
            for j, chip in enumerate(chips):
                copy(i, 4 + j, (*chip, c), sibling).wait_send()
            own(i, me).wait()

    return start, forward, finish


def _gather_semaphores(na):
    return [pltpu.SemaphoreType.DMA((na, 7)), pltpu.SemaphoreType.DMA((na, 7)), pltpu.SemaphoreType.DMA((na,))]


def _scatter_phases(g_refs, out_refs, send_sems, recv_sems, local_sems):
    na = len(g_refs)

    def place(m):
        x, y, c = lax.axis_index("x"), lax.axis_index("y"), lax.axis_index("c")
        px = 1 - x if m & 4 else x
        py = 1 - y if m & 2 else y
        pc = 1 - c if m & 1 else c
        return 4 * x + 2 * y + c, (px, py, pc), 4 * px + 2 * py + pc

    def own(i):
        me, _, _ = place(0)
        return pltpu.make_async_copy(g_refs[i].at[me], out_refs[i].at[me], local_sems.at[i])

    def start():
        for i in range(na):
            own(i).start()
            for m in range(1, N_DEV):
                me, peer, peer_idx = place(m)
                pltpu.make_async_remote_copy(
                    src_ref=g_refs[i].at[peer_idx], dst_ref=out_refs[i].at[me], send_sem=send_sems.at[i, m - 1],
                    recv_sem=recv_sems.at[i, m - 1], device_id=peer, device_id_type=MESH_IDS).start()

    def finish():
        for i in range(na):
            for m in range(1, N_DEV):
                me, peer, peer_idx = place(m)
                cp = pltpu.make_async_remote_copy(
                    src_ref=g_refs[i].at[peer_idx], dst_ref=out_refs[i].at[peer_idx], send_sem=send_sems.at[i, m - 1],
                    recv_sem=recv_sems.at[i, m - 1], device_id=peer, device_id_type=MESH_IDS)
                cp.wait_recv()
                cp.wait_send()
            own(i).wait()

    return start, finish


def _all_gather_many(vs, name):
    na = len(vs)

    def body(*refs):
        x_refs, out_refs = refs[:na], refs[na:2 * na]
        for step in _gather_phases(x_refs, out_refs, *refs[2 * na:]):
            step()

    return pl.pallas_call(
        body, name=name, in_specs=[ANY] * na, out_specs=[ANY] * na,
        out_shape=[jax.ShapeDtypeStruct((N_DEV,) + v.shape, v.dtype) for v in vs],
        scratch_shapes=_gather_semaphores(na),
        compiler_params=pltpu.CompilerParams(vmem_limit_bytes=V7X_VMEM_LIMIT_BYTES),
    )(*vs)


BIG = ("ffn1_w_gate", "ffn1_w_up", "ffn1_w_down", "w_in", "w_out", "ffn2_w_gate", "ffn2_w_up", "ffn2_w_down")
SMALL = ("ffn1_norm", "mix_norm", "a_log", "dt_bias", "gdn_norm_w", "q_norm_w", "k_norm_w", "rel_bias",
         "ffn2_norm", "final_norm")
WEIGHTS = ("ffn1_norm", "ffn1_w_gate", "ffn1_w_up", "ffn1_w_down", "mix_norm", "w_in", "conv_w", "a_log", "dt_bias",
           "gdn_norm_w", "q_norm_w", "k_norm_w", "rel_bias", "w_out", "ffn2_norm", "ffn2_w_gate", "ffn2_w_up",
           "ffn2_w_down", "final_norm")


def _pack(arrays, width, row_multiple):
    flat = jnp.concatenate([a.reshape(-1) for a in arrays])
    rows = -(-flat.shape[0] // width)
    rows = -(-rows // row_multiple) * row_multiple
    return jnp.pad(flat, (0, rows * width - flat.shape[0])).reshape(rows, width)


def _unpack(packed, shapes):
    flat = packed.reshape(-1)
    out, pos = [], 0
    for shp in shapes:
        size = int(np.prod(shp))
        out.append(flat[pos:pos + size].reshape(shp))
        pos += size
    return out


def kernel(x, ffn1_norm, ffn1_w_gate, ffn1_w_up, ffn1_w_down, mix_norm, w_in, conv_w, a_log, dt_bias, gdn_norm_w, q_norm_w, k_norm_w, rel_bias, w_out, ffn2_norm, ffn2_w_gate, ffn2_w_up, ffn2_w_down, final_norm, loss_target, m_ffn1_norm, m_ffn1_w_gate, m_ffn1_w_up, m_ffn1_w_down, m_mix_norm, m_w_in, m_conv_w, m_a_log, m_dt_bias, m_gdn_norm_w, m_q_norm_w, m_k_norm_w, m_rel_bias, m_w_out, m_ffn2_norm, m_ffn2_w_gate, m_ffn2_w_up, m_ffn2_w_down, m_final_norm, v_ffn1_norm, v_ffn1_w_gate, v_ffn1_w_up, v_ffn1_w_down, v_mix_norm, v_w_in, v_conv_w, v_a_log, v_dt_bias, v_gdn_norm_w, v_q_norm_w, v_k_norm_w, v_rel_bias, v_w_out, v_ffn2_norm, v_ffn2_w_gate, v_ffn2_w_up, v_ffn2_w_down, v_final_norm):
    w = dict(ffn1_norm=ffn1_norm, ffn1_w_gate=ffn1_w_gate, ffn1_w_up=ffn1_w_up, ffn1_w_down=ffn1_w_down, mix_norm=mix_norm, w_in=w_in, conv_w=conv_w, a_log=a_log, dt_bias=dt_bias, gdn_norm_w=gdn_norm_w, q_norm_w=q_norm_w, k_norm_w=k_norm_w, rel_bias=rel_bias, w_out=w_out, ffn2_norm=ffn2_norm, ffn2_w_gate=ffn2_w_gate, ffn2_w_up=ffn2_w_up, ffn2_w_down=ffn2_w_down, final_norm=final_norm)
    mom = dict(ffn1_norm=m_ffn1_norm, ffn1_w_gate=m_ffn1_w_gate, ffn1_w_up=m_ffn1_w_up, ffn1_w_down=m_ffn1_w_down, mix_norm=m_mix_norm, w_in=m_w_in, conv_w=m_conv_w, a_log=m_a_log, dt_bias=m_dt_bias, gdn_norm_w=m_gdn_norm_w, q_norm_w=m_q_norm_w, k_norm_w=m_k_norm_w, rel_bias=m_rel_bias, w_out=m_w_out, ffn2_norm=m_ffn2_norm, ffn2_w_gate=m_ffn2_w_gate, ffn2_w_up=m_ffn2_w_up, ffn2_w_down=m_ffn2_w_down, final_norm=m_final_norm)
    var = dict(ffn1_norm=v_ffn1_norm, ffn1_w_gate=v_ffn1_w_gate, ffn1_w_up=v_ffn1_w_up, ffn1_w_down=v_ffn1_w_down, mix_norm=v_mix_norm, w_in=v_w_in, conv_w=v_conv_w, a_log=v_a_log, dt_bias=v_dt_bias, gdn_norm_w=v_gdn_norm_w, q_norm_w=v_q_norm_w, k_norm_w=v_k_norm_w, rel_bias=v_rel_bias, w_out=v_w_out, ffn2_norm=v_ffn2_norm, ffn2_w_gate=v_ffn2_w_gate, ffn2_w_up=v_ffn2_w_up, ffn2_w_down=v_ffn2_w_down, final_norm=v_final_norm)
    ix, iy, ic = lax.axis_index("x"), lax.axis_index("y"), lax.axis_index("c")
    me = 4 * ix + 2 * iy + ic

    def local(a, n):
        return jnp.swapaxes(a[0], 0, 1) if n in TRANSPOSED else a[0]

    shard = {n: local(w[n], n) for n in BIG}

    conv_shard_shape = w["conv_w"][0].shape
    small = {n: w[n][0] if n not in ("rel_bias",) else w[n] for n in SMALL}
    small = {n: (a.reshape(1, -1) if n.endswith("norm") else a) for n, a in small.items()}
    shards = {n: shard[n].astype(BF16) for n in BIG}
    shards["conv_w"] = _pack([w["conv_w"][0]], LANE, 8)
    loss_row, grad_x, grads = _local_step(x[0], loss_target[0], {}, small, late_shards=shards)

    big_out = [[], [], [], []]
    for n in BIG:
        for kind, val in enumerate(_adamw(grads[n], shard[n], local(mom[n], n), local(var[n], n), f"{n}_adamw")):
            big_out[kind].append(jnp.swapaxes(val, 0, 1) if n in TRANSPOSED else val)

    small_names = SMALL + ("conv_w",)
    small_shapes = [grads[n].shape for n in small_names] + [(1, 1)]
    g_small = _pack([grads[n] for n in small_names] + [loss_row[:, :1]], LANE, 8)
    all_small = _all_gather_many([g_small], "gather_small_grads")[0]
    riders = [jnp.zeros(shp, F32) for shp in small_shapes[len(SMALL):]]
    ws = _pack([w[n].reshape(grads[n].shape) for n in SMALL] + riders, LANE, 8)
    ms = _pack([mom[n].reshape(grads[n].shape) for n in SMALL] + riders, LANE, 8)
    vs = _pack([var[n].reshape(grads[n].shape) for n in SMALL] + riders, LANE, 8)
    small_out = [_unpack(a, small_shapes) for a in _adamw(all_small, ws, ms, vs, "adamw_small")]
    loss = small_out[0][-1][0, 0]
    conv_g = lax.dynamic_slice_in_dim(small_out[0][len(SMALL)], me * conv_shard_shape[0], conv_shard_shape[0], axis=0)
    conv_out = [_unpack(a, [conv_shard_shape])[0] for a in _adamw(
        _pack([conv_g], LANE, 8)[None], _pack([w["conv_w"][0]], LANE, 8), _pack([mom["conv_w"][0]], LANE, 8),
        _pack([var["conv_w"][0]], LANE, 8), "adamw_conv")]

    def leaf(kind, n):
        if n in BIG:
            val = big_out[kind][BIG.index(n)]
        elif n == "conv_w":
            val = conv_out[kind]
        else:
            val = small_out[kind][SMALL.index(n)]
        return val.reshape(w[n].shape)

    outs = [loss, grad_x[None]]
    for kind in range(4):
        outs += [leaf(kind, n) for n in WEIGHTS]
    return tuple(outs)
```

```python
import functools
import math

import numpy as np
import jax
import jax.numpy as jnp
from jax import lax
from jax.experimental import pallas as pl
from jax.experimental.pallas import tpu as pltpu

F32 = jnp.float32
BF16 = jnp.bfloat16

D_MODEL = 1024
D_FF = 2816
GDN_HEADS = 4
GDN_HEAD_DIM = 128
GDN_WIDTH = 512
CONV_WIDTH = 5
CHUNK = 64
SWA_HEADS = 8
SWA_HEAD_DIM = 64
SWA_WIDTH = 512
DILATION_PATTERNS = ((128, 1), (512, 4), (2048, 16))
REL_BUCKETS = 32
REL_MAX_DISTANCE = 1024
EPS = 1e-6
NEG_BIG = -1e30
N_DEV = 8

ADAM_LR = 0.001
ADAM_B1 = 0.9
ADAM_B2 = 0.999
ADAM_EPS = 1e-08
ADAM_WD = 0.01
ADAM_STEP = 10

QKV_A = 3 * GDN_WIDTH
OFF_B = QKV_A
OFF_Z = OFF_B + 3 * SWA_WIDTH
OFF_AB = OFF_Z + GDN_WIDTH
N_PAD = OFF_AB + 256
N_IN = 3600
NAT_Z, NAT_AB, NAT_B = QKV_A, QKV_A + GDN_WIDTH, QKV_A + GDN_WIDTH + 16

V7X_VMEM_LIMIT_BYTES = 56 * 1024 * 1024
LANE = 128
ATT_BQ = 128
ATT_HALO = 64
CONV_ROWS = 256

NN = (((1,), (0,)), ((), ()))
NT = (((1,), (1,)), ((), ()))
TN = (((0,), (0,)), ((), ()))


def _params(*sem):
    return pltpu.CompilerParams(dimension_semantics=sem, vmem_limit_bytes=V7X_VMEM_LIMIT_BYTES)


def _dot(a, b, dn=NN):
    return lax.dot_general(a.astype(BF16), b.astype(BF16), dn, preferred_element_type=F32)


def _sigmoid(x):
    return 1.0 / (1.0 + jnp.exp(-x))


class _Exchange:
    def __init__(self, kind, arrays):
        self.kind, self.arrays = kind, list(arrays)

    def out_shape(self):
        lead = (N_DEV,) if self.kind == "gather" else ()
        return [jax.ShapeDtypeStruct(lead + v.shape, v.dtype) for v in self.arrays]

    def hooks(self, in_refs, out_refs, sems, grid):
        step = pl.program_id(0)
        for axis in range(1, len(grid)):
            step = step * grid[axis] + pl.program_id(axis)
        total = math.prod(grid)
        if self.kind == "gather":
            assert total >= 4
            start, forward, finish = _gather_phases(in_refs, out_refs, *sems)
            pl.when(step == total // 2)(forward)
        else:
            assert total >= 2
            start, finish = _scatter_phases(in_refs, out_refs, *sems)
        pl.when(step == 0)(start)
        pl.when(step == total - 1)(finish)


def _pallas(body, *, name, grid, in_specs, out_specs, out_shape, args, semantics, scratch_shapes=(), exchange=None):
    n_in, n_out, n_scr = len(in_specs), len(out_specs), len(scratch_shapes)
    if exchange is None:
        res = pl.pallas_call(
            body, name=name, grid=grid, in_specs=list(in_specs), out_specs=list(out_specs), out_shape=list(out_shape),
            scratch_shapes=list(scratch_shapes), compiler_params=_params(*semantics))(*args)
        return list(res), []
    na = len(exchange.arrays)

    def carrying(*refs):
        ins, sent = refs[:n_in], refs[n_in:n_in + na]
        outs = refs[n_in + na:n_in + na + n_out]
        landed = refs[n_in + na + n_out:n_in + 2 * na + n_out]
        rest = refs[n_in + 2 * na + n_out:]
        exchange.hooks(sent, landed, rest[n_scr:], grid)
        body(*ins, *outs, *rest[:n_scr])

    res = pl.pallas_call(
        carrying, name=name, grid=grid, in_specs=list(in_specs) + [ANY] * na, out_specs=list(out_specs) + [ANY] * na,
        out_shape=list(out_shape) + exchange.out_shape(), scratch_shapes=list(scratch_shapes) + _gather_semaphores(na),
        compiler_params=_params(*(["arbitrary"] * len(grid))))(*args, *exchange.arrays)
    return list(res[:n_out]), list(res[n_out:])


def _matmul(pairs, *, ta=False, tb=False, out_dtype=F32, tm, tn, tk, name, res=None, alpha=None, norm_bwd=None,
            norm_fwd=None, loss=None, exchange=None):
    a0, b0 = pairs[0]
    m = a0.shape[1] if ta else a0.shape[0]
    k = a0.shape[0] if ta else a0.shape[1]
    n = b0.shape[0] if tb else b0.shape[1]
    tm, tn, tk = min(tm, m), min(tn, n), min(tk, k)
    assert m % tm == 0 and n % tn == 0 and k % tk == 0, (name, m, n, k, tm, tn, tk)
    nk = k // tk
    npairs = len(pairs)
    dn = (((0 if ta else 1,), (1 if tb else 0,)), ((), ()))
    assert norm_bwd is None or (tn == n and res is None and alpha is None)

    def body(*refs):
        ins = refs[:2 * npairs]
        pos = 2 * npairs
        r_ref = None
        if res is not None:
            r_ref = refs[pos]
            pos += 1
        if norm_bwd is not None:
            x_ref, rs_ref, w_ref, dres_ref = refs[pos:pos + 4]
            o_ref, dw_ref, acc = refs[pos + 4:pos + 7]

            @pl.when((pl.program_id(0) == 0) & (pl.program_id(2) == 0))
            def _():
                dw_ref[...] = jnp.zeros_like(dw_ref)
        elif norm_fwd is not None:
            wn_ref, o_ref, n_ref, rs_out, acc = refs[pos:pos + 5]
        elif loss is not None:
            wf_ref, tg_ref, loss_ref, o_ref, dwf_ref, acc = refs[pos:pos + 6]

            @pl.when((pl.program_id(0) == 0) & (pl.program_id(2) == 0))
            def _():
                dwf_ref[...] = jnp.zeros_like(dwf_ref)
                loss_ref[...] = jnp.zeros_like(loss_ref)
        else:
            o_ref, acc = refs[pos], refs[pos + 1]
        kk = pl.program_id(2)
        t = None
        for p in range(npairs):
            d = _dot(ins[2 * p][...], ins[2 * p + 1][...], dn)
            t = d if t is None else t + d

        if nk > 1:
            @pl.when(kk == 0)
            def _():
                acc[...] = t

            @pl.when((kk > 0) & (kk < nk - 1))
            def _():
                acc[...] += t

        @pl.when(kk == nk - 1)
        def _():
            r = acc[...] + t if nk > 1 else t
            if alpha is not None:
                r = r * alpha
            if r_ref is not None:
                r = r_ref[...] + r
            if norm_bwd is not None:
                rs = rs_ref[...]
                xhat = x_ref[...] * rs
                dw_ref[...] += jnp.sum(r * xhat, axis=0, keepdims=True)
                t_w = r * w_ref[...]
                r = dres_ref[...] + rs * (t_w - xhat * jnp.mean(t_w * xhat, axis=-1, keepdims=True))
            if norm_fwd is not None:
                rs = lax.rsqrt(jnp.mean(r * r, axis=-1, keepdims=True) + EPS)
                n_ref[...] = (r * rs * wn_ref[...]).astype(BF16)
                rs_out[...] = rs
            if loss is not None:
                wv = wf_ref[...]
                rs = lax.rsqrt(jnp.mean(r * r, axis=-1, keepdims=True) + EPS)
                xhat = r * rs
                e = xhat * wv - tg_ref[...]
                part = 0.5 * jnp.sum(jnp.mean(e * e, axis=-1, keepdims=True), axis=0, keepdims=True)
                loss_ref[...] += jnp.broadcast_to(part, loss_ref.shape)
                dy = e * (1.0 / n)
                dwf_ref[...] += jnp.sum(dy * xhat, axis=0, keepdims=True)
                t_w = dy * wv
                r = rs * (t_w - xhat * jnp.mean(t_w * xhat, axis=-1, keepdims=True))
            o_ref[...] = r.astype(out_dtype)

    a_spec = pl.BlockSpec((tk, tm), lambda i, j, kk: (kk, i)) if ta else pl.BlockSpec((tm, tk), lambda i, j, kk: (i, kk))
    b_spec = pl.BlockSpec((tn, tk), lambda i, j, kk: (j, kk)) if tb else pl.BlockSpec((tk, tn), lambda i, j, kk: (kk, j))
    o_spec = pl.BlockSpec((tm, tn), lambda i, j, kk: (i, j))
    in_specs = [a_spec, b_spec] * npairs + ([o_spec] if res is not None else [])
    args = [t for pr in pairs for t in pr] + ([res] if res is not None else [])
    out_specs, out_shape = [o_spec], [jax.ShapeDtypeStruct((m, n), out_dtype)]
    if norm_bwd is not None:
        vec = pl.BlockSpec((1, n), lambda i, j, kk: (0, 0))
        in_specs += [o_spec, pl.BlockSpec((tm, 1), lambda i, j, kk: (i, 0)), vec, o_spec]
        args += list(norm_bwd)
        out_specs.append(vec)
        out_shape.append(jax.ShapeDtypeStruct((1, n), F32))
    if norm_fwd is not None:
        assert tn == n and norm_bwd is None
        in_specs.append(pl.BlockSpec((1, n), lambda i, j, kk: (0, 0)))
        args.append(norm_fwd)
        out_specs += [o_spec, pl.BlockSpec((tm, 1), lambda i, j, kk: (i, 0))]
        out_shape += [jax.ShapeDtypeStruct((m, n), BF16), jax.ShapeDtypeStruct((m, 1), F32)]
    if loss is not None:
        assert tn == n and norm_bwd is None and norm_fwd is None
        vec = pl.BlockSpec((1, n), lambda i, j, kk: (0, 0))
        in_specs += [vec, o_spec]
        args += list(loss)
        out_specs = [pl.BlockSpec((1, LANE), lambda i, j, kk: (0, 0))] + out_specs + [vec]
        out_shape = [jax.ShapeDtypeStruct((1, LANE), F32)] + out_shape + [jax.ShapeDtypeStruct((1, n), F32)]
    sequential = norm_bwd is not None or loss is not None
    outs, exchanged = _pallas(
        body, name=name, grid=(m // tm, n // tn, nk), in_specs=in_specs, out_specs=out_specs, out_shape=out_shape,
        scratch_shapes=[pltpu.VMEM((tm, tn) if nk > 1 else (8, LANE), F32)],
        semantics=("arbitrary",) * 3 if sequential else ("parallel", "parallel", "arbitrary"), args=args,
        exchange=exchange)
    out = outs[0] if len(outs) == 1 else tuple(outs)
    return out if exchange is None else (out, exchanged)


def _rms_fwd(x, w, name, exchange=None):
    s, d = x.shape
    tm = min(512, s)

    def body(x_ref, w_ref, n_ref, r_ref):
        xv = x_ref[...]
        r = lax.rsqrt(jnp.mean(xv * xv, axis=-1, keepdims=True) + EPS)
        n_ref[...] = (xv * r * w_ref[...]).astype(BF16)
        r_ref[...] = r

    (n, r), exchanged = _pallas(
        body, name=name, grid=(s // tm,),
        in_specs=[pl.BlockSpec((tm, d), lambda i: (i, 0)), pl.BlockSpec((1, d), lambda i: (0, 0))],
        out_specs=[pl.BlockSpec((tm, d), lambda i: (i, 0)), pl.BlockSpec((tm, 1), lambda i: (i, 0))],
        out_shape=[jax.ShapeDtypeStruct((s, d), BF16), jax.ShapeDtypeStruct((s, 1), F32)],
        semantics=("parallel",), args=(x, w), exchange=exchange)
    return (n, r) if exchange is None else (n, r, exchanged)


def _rms_bwd(dn, x, r, w, dres, name, exchange=None):
    s, d = x.shape
    tm = min(512, s)

    def body(dn_ref, x_ref, r_ref, w_ref, dres_ref, dx_ref, dw_ref):
        @pl.when(pl.program_id(0) == 0)
        def _():
            dw_ref[...] = jnp.zeros_like(dw_ref)

        rv = r_ref[...]
        xhat = x_ref[...] * rv
        g = dn_ref[...]
        t = g * w_ref[...]
        dx_ref[...] = dres_ref[...] + rv * (t - xhat * jnp.mean(t * xhat, axis=-1, keepdims=True))
        dw_ref[...] += jnp.sum(g * xhat, axis=0, keepdims=True)

    row = pl.BlockSpec((tm, d), lambda i: (i, 0))
    vec = pl.BlockSpec((1, d), lambda i: (0, 0))
    (dx, dw), exchanged = _pallas(
        body, name=name, grid=(s // tm,),
        in_specs=[row, row, pl.BlockSpec((tm, 1), lambda i: (i, 0)), vec, row],
        out_specs=[row, vec],
        out_shape=[jax.ShapeDtypeStruct((s, d), F32), jax.ShapeDtypeStruct((1, d), F32)],
        semantics=("arbitrary",), args=(dn, x, r, w, dres), exchange=exchange)
    return (dx, dw) if exchange is None else (dx, dw, exchanged)


def _ffn_up(n, wg, wu, name, exchange=None):
    s, d = n.shape
    f = wg.shape[0]
    tm, tn = min(512, s), f // 2

    def body(n_ref, wg_ref, wu_ref, g_ref, u_ref, a_ref):
        nv = n_ref[...]
        g = _dot(nv, wg_ref[...], NT)
        u = _dot(nv, wu_ref[...], NT)
        g_ref[...] = g.astype(BF16)
        u_ref[...] = u.astype(BF16)
        a_ref[...] = (g * _sigmoid(g) * u).astype(BF16)

    o = pl.BlockSpec((tm, tn), lambda j, i: (i, j))
    wspec = pl.BlockSpec((tn, d), lambda j, i: (j, 0))
    return _pallas(
        body, name=name, grid=(f // tn, s // tm),
        in_specs=[pl.BlockSpec((tm, d), lambda j, i: (i, 0)), wspec, wspec],
        out_specs=[o, o, o],
        out_shape=[jax.ShapeDtypeStruct((s, f), BF16)] * 3,
        semantics=("parallel", "parallel"), args=(n, wg, wu), exchange=exchange)


def _ffn_dact(dx, wd, g, u, name, exchange=None):
    s, d = dx.shape
    f = wd.shape[0]
    tm, tn = min(512, s), f // 2

    def body(dx_ref, wd_ref, g_ref, u_ref, dg_ref, du_ref):
        da = 0.5 * _dot(dx_ref[...], wd_ref[...], NT)
        gv = g_ref[...].astype(F32)
        sg = _sigmoid(gv)
        du_ref[...] = (da * gv * sg).astype(BF16)
        dg_ref[...] = (da * u_ref[...].astype(F32) * (sg * (1.0 + gv * (1.0 - sg)))).astype(BF16)

    o = pl.BlockSpec((tm, tn), lambda j, i: (i, j))
    return _pallas(
        body, name=name, grid=(f // tn, s // tm),
        in_specs=[pl.BlockSpec((tm, d), lambda j, i: (i, 0)), pl.BlockSpec((tn, d), lambda j, i: (j, 0)), o, o],
        out_specs=[o, o],
        out_shape=[jax.ShapeDtypeStruct((s, f), BF16), jax.ShapeDtypeStruct((s, f), BF16)],
        semantics=("parallel", "parallel"), args=(dx, wd, g, u), exchange=exchange)


def _row_slabs(full):
    return full.reshape(N_DEV, full.shape[0] // N_DEV, full.shape[1])


def _rows_of(slabs):
    return slabs.reshape(N_DEV * slabs.shape[1], slabs.shape[2])


def _ffn_forward(x, norm_w, wg, wu, wd, tag, gather=(), head=(), normed=None, next_norm=None, loss=None):
    if normed is not None:
        (n, r), first = normed, []
    elif head:
        n, r, first = _rms_fwd(x, norm_w, f"{tag}_norm", _Exchange("gather", head))
    else:
        (n, r), first = _rms_fwd(x, norm_w, f"{tag}_norm"), []
    if wg is None:
        wg, wu, first = _rows_of(first[0]), _rows_of(first[1]), first[2:]
    (g, u, a), got = _ffn_up(n, wg, wu, f"{tag}_up", _Exchange("gather", gather) if gather else None)
    if wd is None:
        wd, got = _rows_of(got[0]), got[1:]
    y = _matmul([(a, wd)], tm=512, tn=1024, tk=wd.shape[0], name=f"{tag}_down", res=x, alpha=0.5, norm_fwd=next_norm,
                loss=loss)
    y, nxt = (y[0], y[1:]) if next_norm is not None else (y, None)
    return y, (n, r, g, u, a), (wg, wu, wd), got, first, nxt


def _ffn_backward(dy, x, norm_w, wgt, wut, wd, saved, tag, dw_dtype=F32, scatter=None):
    n, r, g, u, a = saved

    def behind(arrays):
        return _Exchange("scatter", arrays) if scatter is not None else None

    def dw(act, grad, name, alpha=None, exchange=None):
        return _matmul([(act, grad)], ta=True, tm=1408, tn=1024, tk=2048, name=name, alpha=alpha, out_dtype=dw_dtype,
                       exchange=exchange)

    dwd = _row_slabs(dw(a, dy, f"{tag}_dwd", alpha=0.5))
    (dg, du), extras = _ffn_dact(dy, wd, g, u, f"{tag}_dact", behind(scatter))
    if scatter is None:
        dwg, dwu = _row_slabs(dw(dg, n, f"{tag}_dwg")), _row_slabs(dw(du, n, f"{tag}_dwu"))
    else:
        dwg, (dwd,) = dw(dg, n, f"{tag}_dwg", exchange=behind([dwd]))
        dwu, (dwg,) = dw(du, n, f"{tag}_dwu", exchange=behind([_row_slabs(dwg)]))
        dwu = _row_slabs(dwu)
    if scatter is None:
        dx, dnorm = _matmul([(dg, wgt), (du, wut)], tm=512, tn=1024, tk=wgt.shape[0], name=f"{tag}_dn",
                            norm_bwd=(x, r, norm_w, dy))
    else:
        dn, (dwu,) = _matmul([(dg, wgt), (du, wut)], tm=512, tn=1024, tk=wgt.shape[0], name=f"{tag}_dn",
                             exchange=behind([dwu]))
        dx, dnorm = _rms_bwd(dn, x, r, norm_w, dy, f"{tag}_dnorm")
    return dx, dnorm, dwg, dwu, dwd, extras


Q_SCALE = GDN_HEAD_DIM ** -0.5
CONV_HALO = 8


def _lane_block(s):
    return pl.BlockSpec((None, s, LANE), lambda j: (j, 0, 0))


def _halo_window(ref, ci, rows, s):
    b = pl.multiple_of(ci * rows, rows)
    before = ref[pl.ds(pl.multiple_of(jnp.maximum(b - CONV_HALO, 0), CONV_HALO), CONV_HALO), :]
    after = ref[pl.ds(pl.multiple_of(jnp.minimum(b + rows, s - CONV_HALO), CONV_HALO), CONV_HALO), :]
    return jnp.concatenate([jnp.where(ci > 0, before, 0.0), ref[pl.ds(b, rows), :],
                            jnp.where(ci < s // rows - 1, after, 0.0)], axis=0)


def _conv_taps(win, w_ref, rows, sign):
    n = rows + 2 * CONV_HALO
    acc = None
    for t in range(CONV_WIDTH):
        o = sign * (t - CONV_WIDTH // 2)
        sh = win if o == 0 else pltpu.roll(win, (-o) % n, 0)
        term = sh[CONV_HALO:CONV_HALO + rows] * w_ref[t:t + 1, :]
        acc = term if acc is None else acc + term
    return acc


def _gdn_conv_fwd(p_pad, conv_wt):
    s = p_pad.shape[0]
    rows = min(CONV_ROWS, s)
    nblk = QKV_A // LANE

    def body(p_ref, w_ref, c_ref, y_ref):
        j = pl.program_id(0)

        def chunk(ci, carry):
            b = pl.multiple_of(ci * rows, rows)
            c = _conv_taps(_halo_window(p_ref, ci, rows, s), w_ref, rows, 1)
            c_ref[pl.ds(b, rows), :] = c
            act = c * _sigmoid(c)
            nrm = lax.rsqrt(jnp.sum(act * act, axis=-1, keepdims=True) + EPS)
            mult = jnp.where(j < GDN_HEADS, nrm * Q_SCALE, jnp.where(j < 2 * GDN_HEADS, nrm, 1.0))
            y_ref[pl.ds(b, rows), :] = act * mult
            return carry

        lax.fori_loop(0, s // rows, chunk, 0)

    col = pl.BlockSpec((s, LANE), lambda j: (0, j))
    return pl.pallas_call(
        body, name="gdn_conv_fwd", grid=(nblk,),
        in_specs=[col, pl.BlockSpec((8, LANE), lambda j: (0, j))],
        out_specs=[_lane_block(s), _lane_block(s)],
        out_shape=[jax.ShapeDtypeStruct((nblk, s, LANE), F32), jax.ShapeDtypeStruct((nblk, s, LANE), F32)],
        compiler_params=_params("parallel"),
    )(p_pad, conv_wt)


def _gdn_conv_bwd(dy_f, dy_r, c_pre, p_pad, conv_wt, dp_all):
    s = p_pad.shape[0]
    rows = min(CONV_ROWS, s)
    nblk = QKV_A // LANE

    def body(dyf_ref, dyr_ref, c_ref, p_ref, w_ref, _, dp_ref, dw_ref, dcpad):
        j = pl.program_id(0)
        zeros = jnp.zeros((CONV_HALO, LANE), F32)
        dcpad[0:CONV_HALO, :] = zeros
        dcpad[CONV_HALO + s:2 * CONV_HALO + s, :] = zeros

        def act_bwd(ci, carry):
            b = pl.multiple_of(ci * rows, rows)
            c = c_ref[pl.ds(b, rows), :]
            g = dyf_ref[pl.ds(b, rows), :] + dyr_ref[pl.ds(b, rows), :]
            sg = _sigmoid(c)
            act = c * sg
            nrm = lax.rsqrt(jnp.sum(act * act, axis=-1, keepdims=True) + EPS)
            yh = act * nrm
            scale = jnp.where(j < GDN_HEADS, Q_SCALE, 1.0)
            dact_qk = (scale * nrm) * (g - yh * jnp.sum(g * yh, axis=-1, keepdims=True))
            dact = jnp.where(j < 2 * GDN_HEADS, dact_qk, g)
            dcpad[pl.ds(pl.multiple_of(b + CONV_HALO, CONV_HALO), rows), :] = dact * (sg * (1.0 + c * (1.0 - sg)))
            return carry

        lax.fori_loop(0, s // rows, act_bwd, 0)
        tap = lax.broadcasted_iota(jnp.int32, (8, LANE), 0)

        def taps_bwd(ci, dw):
            b = pl.multiple_of(ci * rows, rows)
            dcw = dcpad[pl.ds(b, rows + 2 * CONV_HALO), :]
            dp_ref[pl.ds(b, rows), :] = _conv_taps(dcw, w_ref, rows, -1).astype(BF16)
            pw = _halo_window(p_ref, ci, rows, s)
            dc = dcw[CONV_HALO:CONV_HALO + rows]
            n = rows + 2 * CONV_HALO
            for t in range(CONV_WIDTH):
                o = t - CONV_WIDTH // 2
                sh = pw if o == 0 else pltpu.roll(pw, (-o) % n, 0)
                row = jnp.sum(dc * sh[CONV_HALO:CONV_HALO + rows], axis=0, keepdims=True)
                dw = dw + jnp.where(tap == t, row, 0.0)
            return dw

        dw_ref[...] = lax.fori_loop(0, s // rows, taps_bwd, jnp.zeros((8, LANE), F32))

    col = pl.BlockSpec((s, LANE), lambda j: (0, j))
    wspec = pl.BlockSpec((8, LANE), lambda j: (0, j))
    return pl.pallas_call(
        body, name="gdn_conv_bwd", grid=(nblk,),
        in_specs=[_lane_block(s), _lane_block(s), _lane_block(s), col, wspec, ANY],
        out_specs=[col, wspec],
        out_shape=[jax.ShapeDtypeStruct(dp_all.shape, dp_all.dtype), jax.ShapeDtypeStruct((8, QKV_A), F32)],
        scratch_shapes=[pltpu.VMEM((s + 2 * CONV_HALO, LANE), F32)],
        input_output_aliases={5: 0},
        compiler_params=_params("parallel"),
    )(dy_f, dy_r, c_pre, p_pad, conv_wt, dp_all)


def _softplus(x):
    return jnp.maximum(x, 0.0) + jnp.log(1.0 + jnp.exp(-jnp.abs(x)))


def _gdn_gates_fwd(p_pad, alog_row, dt_row):
    s = p_pad.shape[0]
    tm = min(1024, s)

    def body(p_ref, al_ref, dt_ref, o_ref):
        x = p_ref[...]
        lane = lax.broadcasted_iota(jnp.int32, x.shape, 1)
        g = -jnp.exp(al_ref[...]) * _softplus(x + dt_ref[...])
        o_ref[...] = jnp.where(lane < 8, g, jnp.where(lane < 16, _sigmoid(x), 0.0))

    vec = pl.BlockSpec((1, LANE), lambda i: (0, 0))
    return pl.pallas_call(
        body, name="gdn_gates_fwd", grid=(s // tm,),
        in_specs=[pl.BlockSpec((tm, LANE), lambda i: (i, OFF_AB // LANE)), vec, vec],
        out_specs=pl.BlockSpec((tm, LANE), lambda i: (i, 0)),
        out_shape=jax.ShapeDtypeStruct((s, LANE), F32),
        compiler_params=_params("parallel"),
    )(p_pad, alog_row, dt_row)


def _gdn_gates_bwd(dgb_f, dgb_r, p_pad, gb, alog_row, dt_row, dp_all):
    s = p_pad.shape[0]
    tm = min(1024, s)
    tail = N_PAD - OFF_AB

    def body(df_ref, dr_ref, p_ref, gb_ref, al_ref, dt_ref, _, dp_ref, sum_ref):
        @pl.when(pl.program_id(0) == 0)
        def _():
            sum_ref[...] = jnp.zeros_like(sum_ref)

        x = p_ref[...]
        gbv = gb_ref[...]
        dgb = df_ref[...] + dr_ref[...]
        lane = lax.broadcasted_iota(jnp.int32, x.shape, 1)
        da = dgb * (-jnp.exp(al_ref[...])) * _sigmoid(x + dt_ref[...])
        db = dgb * gbv * (1.0 - gbv)
        dp_ref[:, 0:LANE] = jnp.where(lane < 8, da, jnp.where(lane < 16, db, 0.0)).astype(BF16)
        dp_ref[:, LANE:tail] = jnp.zeros((tm, tail - LANE), BF16)
        row = lax.broadcasted_iota(jnp.int32, (8, LANE), 0)
        lane8 = lax.broadcasted_iota(jnp.int32, (8, LANE), 1)
        d_alog = jnp.sum(dgb * gbv, axis=0, keepdims=True)
        d_dt = jnp.sum(da, axis=0, keepdims=True)
        upd = jnp.where(row == 0, d_alog, jnp.where(row == 1, d_dt, 0.0))
        sum_ref[...] += jnp.where(lane8 < 8, upd, 0.0)

    vec = pl.BlockSpec((1, LANE), lambda i: (0, 0))
    blk = pl.BlockSpec((tm, LANE), lambda i: (i, 0))
    return pl.pallas_call(
        body, name="gdn_gates_bwd", grid=(s // tm,),
        in_specs=[blk, blk, pl.BlockSpec((tm, LANE), lambda i: (i, OFF_AB // LANE)), blk, vec, vec, ANY],
        out_specs=[pl.BlockSpec((tm, tail), lambda i: (i, OFF_AB // tail)), pl.BlockSpec((8, LANE), lambda i: (0, 0))],
        out_shape=[jax.ShapeDtypeStruct(dp_all.shape, dp_all.dtype), jax.ShapeDtypeStruct((8, LANE), F32)],
        input_output_aliases={6: 0},
        compiler_params=_params("arbitrary"),
    )(dgb_f, dgb_r, p_pad, gb, alog_row, dt_row, dp_all)


def _chunk_masks(rev):
    row = lax.broadcasted_iota(jnp.int32, (CHUNK, CHUNK), 0)
    col = lax.broadcasted_iota(jnp.int32, (CHUNK, CHUNK), 1)
    le = (col >= row) if rev else (col <= row)
    strict = (col > row) if rev else (col < row)
    return le, strict, row == col


def _gate_lanes(rev, h):
    d = 1 if rev else 0
    return d * GDN_HEADS + h, 8 + d * GDN_HEADS + h


BNN = (((2,), (1,)), ((0,), (0,)))
BNT = (((2,), (2,)), ((0,), (0,)))
BTN = (((1,), (1,)), ((0,), (0,)))
NB = 2 * GDN_HEADS
DELTA_CHUNKS = 8


def _bdot(a, b, dn=BNN):
    return lax.dot_general(a.astype(BF16), b.astype(BF16), dn, preferred_element_type=F32)


def _dot3(a, b, dn, exact_a=False, exact_b=False):
    def d(x, y):
        return lax.dot_general(x, y, dn, preferred_element_type=F32)

    ah = a.astype(BF16)
    bh = b.astype(BF16)
    out = d(ah, bh)
    if not exact_b:
        out = out + d(ah, (b - bh.astype(F32)).astype(BF16))
    if not exact_a:
        out = out + d((a - ah.astype(F32)).astype(BF16), bh)
    return out


def _both(f_val, r_val):
    return jnp.stack([f_val] * GDN_HEADS + [r_val] * GDN_HEADS)


def _head_blocks(ref_f, ref_r, rows_f, rows_r):
    return jnp.concatenate([ref_f[:, rows_f, :], ref_r[:, rows_r, :]], axis=0)


def _chunk_rows(c):
    return slice(c * CHUNK, (c + 1) * CHUNK), slice((DELTA_CHUNKS - 1 - c) * CHUNK, (DELTA_CHUNKS - c) * CHUNK)


def _heads(ref_f, ref_r, rows_f, rows_r):
    hd = GDN_HEAD_DIM
    return jnp.stack([ref_f[rows_f, h * hd:(h + 1) * hd] for h in range(GDN_HEADS)]
                     + [ref_r[rows_r, h * hd:(h + 1) * hd] for h in range(GDN_HEADS)])


def _gate_cols(tile_f, tile_r, base):
    return jnp.stack([tile_f[:, base + h:base + h + 1] for h in range(GDN_HEADS)]
                     + [tile_r[:, base + GDN_HEADS + h:base + GDN_HEADS + h + 1] for h in range(GDN_HEADS)])


def _chunk_common2(q, k, v, gbf, gbr):
    mf, mr = _chunk_masks(False), _chunk_masks(True)
    le, strict = _both(mf[0], mr[0]), _both(mf[1], mr[1])
    eye = mf[2]
    gcm_f = _dot3(mf[0].astype(F32), gbf, NN, exact_a=True)
    gcm_r = _dot3(mr[0].astype(F32), gbr, NN, exact_a=True)
    g, beta, gc = _gate_cols(gbf, gbr, 0), _gate_cols(gbf, gbr, 8), _gate_cols(gcm_f, gcm_r, 0)
    diag = jnp.where(eye[None], gc, 0.0)
    diag_hi = diag.astype(BF16)
    diag_lo = (diag - diag_hi.astype(F32)).astype(BF16)
    gc_row = lax.dot_general(jnp.ones((NB, CHUNK, 2 * CHUNK), BF16), jnp.concatenate([diag_hi, diag_lo], axis=1), BNN,
                             preferred_element_type=F32)
    decay = jnp.where(le, jnp.exp(jnp.where(le, gc - gc_row, 0.0)), 0.0)
    eg = jnp.exp(gc)
    gl = jnp.sum(g, axis=1, keepdims=True)
    kb = k * beta
    vb = v * beta
    kbeg = kb * eg
    lm = jnp.where(strict, _bdot(kb, k, BNT) * decay, 0.0)
    intra = _bdot(q, k, BNT) * decay
    edec = jnp.exp(gl - gc)
    return dict(strict=strict, eye=eye, beta=beta, decay=decay, eg=eg, gl=gl, kb=kb, vb=vb, kbeg=kbeg,
                lm=lm, intra=intra, qg=q * eg, edec=edec, kdec=k * edec)


def _unit_triangular_inverse(lm, eye):
    x = -lm
    t = eye[None].astype(F32) + x
    p = x
    for _ in range(5):
        p = _bdot(p, p)
        t = t + _bdot(t, p)
    return t


def _delta_fwd2(y, gb, gather=()):
    s = y.shape[1]
    nc = s // CHUNK
    hd = GDN_HEAD_DIM
    na = len(gather)

    def body(*refs):
        qf, kf, vf, gf, qr, kr, vr, gr = refs[:8]
        of_ref, or_ref, sf_all, sr_all, tf_all, tr_all = refs[8 + na:14 + na]
        state = refs[14 + 2 * na]
        step = pl.program_id(0)

        @pl.when(step == 0)
        def _():
            state[...] = jnp.zeros_like(state)

        if na:
            start, forward, finish = _gather_phases(refs[8:8 + na], refs[14 + na:14 + 2 * na], *refs[15 + 2 * na:])
            pl.when(step == 0)(start)
            pl.when(step == ns // 2)(forward)
            pl.when(step == ns - 1)(finish)

        st = state[...]
        for c in range(DELTA_CHUNKS):
            rf, rr = _chunk_rows(c)
            q, k, v = _head_blocks(qf, qr, rf, rr), _head_blocks(kf, kr, rf, rr), _head_blocks(vf, vr, rf, rr)
            cm = _chunk_common2(q, k, v, gf[rf, :], gr[rr, :])
            tinv = _unit_triangular_inverse(cm["lm"], cm["eye"])
            u = _bdot(tinv, cm["vb"])
            w = _bdot(tinv, cm["kbeg"])
            v_new = u - _bdot(w, st)
            o = _bdot(cm["qg"], st) + _bdot(cm["intra"], v_new)
            for h in range(GDN_HEADS):
                of_ref[rf, h * hd:(h + 1) * hd] = o[h]
                or_ref[rr, h * hd:(h + 1) * hd] = o[GDN_HEADS + h]
            sf_all[c] = st[:GDN_HEADS]
            sr_all[DELTA_CHUNKS - 1 - c] = st[GDN_HEADS:]
            tf_all[c] = tinv[:GDN_HEADS]
            tr_all[DELTA_CHUNKS - 1 - c] = tinv[GDN_HEADS:]
            st = st * jnp.exp(cm["gl"]) + _bdot(cm["kdec"], v_new, BTN)
        state[...] = st

    rows = DELTA_CHUNKS * CHUNK
    ns = nc // DELTA_CHUNKS

    def col(j, rev):
        return pl.BlockSpec((GDN_HEADS, rows, hd), (lambda n: (j, ns - 1 - n, 0)) if rev else (lambda n: (j, n, 0)))

    def out(rev):
        return pl.BlockSpec((rows, GDN_WIDTH), (lambda n: (ns - 1 - n, 0)) if rev else (lambda n: (n, 0)))

    def gate(rev):
        return pl.BlockSpec((rows, LANE), (lambda n: (ns - 1 - n, 0)) if rev else (lambda n: (n, 0)))

    def per_chunk(d1, d2, rev):
        return pl.BlockSpec((DELTA_CHUNKS, GDN_HEADS, d1, d2),
                            (lambda n: (ns - 1 - n, 0, 0, 0)) if rev else (lambda n: (n, 0, 0, 0)))

    assert nc % DELTA_CHUNKS == 0 and (na == 0 or ns >= 4)
    res = pl.pallas_call(
        body, name="delta_fwd", grid=(ns,),
        in_specs=[col(0, False), col(1, False), col(2, False), gate(False), col(0, True), col(1, True), col(2, True), gate(True)]
        + [ANY] * na,
        out_specs=[out(False), out(True), per_chunk(hd, hd, False), per_chunk(hd, hd, True),
                   per_chunk(CHUNK, CHUNK, False), per_chunk(CHUNK, CHUNK, True)] + [ANY] * na,
        out_shape=[jax.ShapeDtypeStruct((s, GDN_WIDTH), F32)] * 2 + [jax.ShapeDtypeStruct((nc, GDN_HEADS, hd, hd), F32)] * 2
        + [jax.ShapeDtypeStruct((nc, GDN_HEADS, CHUNK, CHUNK), F32)] * 2
        + [jax.ShapeDtypeStruct((N_DEV,) + v.shape, v.dtype) for v in gather],
        scratch_shapes=[pltpu.VMEM((NB, hd, hd), F32)] + (_gather_semaphores(na) if na else []),
        compiler_params=_params("arbitrary"),
    )(y, y, y, gb, y, y, y, gb, *gather)
    return res[:6], res[6:]


def _delta_bwd2(y, gb, do, sf_all, sr_all, tf_all, tr_all, scatter=()):
    s = y.shape[1]
    nc = s // CHUNK
    hd = GDN_HEAD_DIM
    na = len(scatter)

    def body(*refs):
        qf, kf, vf, gf, dof, sf, tf, qr, kr, vr, gr, dor, sr, tr = refs[:14]
        dyf_ref, dyr_ref, dgf_ref, dgr_ref = refs[14 + na:18 + na]
        dstate = refs[18 + 2 * na]
        step = pl.program_id(0)

        @pl.when(step == 0)
        def _():
            dstate[...] = jnp.zeros_like(dstate)

        if na:
            start, finish = _scatter_phases(refs[14:14 + na], refs[18 + na:18 + 2 * na], *refs[19 + 2 * na:])
            pl.when(step == 0)(start)
            pl.when(step == ns - 1)(finish)

        def one_chunk(c, ds_out):
            rr, rf = _chunk_rows(c)
            cf, cr = DELTA_CHUNKS - 1 - c, c
            q, k, v = _head_blocks(qf, qr, rf, rr), _head_blocks(kf, kr, rf, rr), _head_blocks(vf, vr, rf, rr)
            dov = _heads(dof, dor, rf, rr)
            cm = _chunk_common2(q, k, v, gf[rf, :], gr[rr, :])
            tinv = jnp.concatenate([tf[cf], tr[cr]], axis=0)
            st = jnp.concatenate([sf[cf], sr[cr]], axis=0)
            decay, lm, intra, qg, kdec, kbeg, eg, kb, beta = (
                cm[n] for n in ("decay", "lm", "intra", "qg", "kdec", "kbeg", "eg", "kb", "beta"))
            u = _bdot(tinv, cm["vb"])
            w = _bdot(tinv, kbeg)
            v_new = u - _bdot(w, st)
            egl = jnp.exp(cm["gl"])
            d_qg = _bdot(dov, st, BNT)
            d_intra = _bdot(dov, v_new, BNT)
            dv_new = _bdot(intra, dov, BTN) + _bdot(kdec, ds_out)
            d_kdec = _bdot(v_new, ds_out, BNT)
            ds_in = _bdot(qg, dov, BTN) + egl * ds_out - _bdot(w, dv_new, BTN)
            dgl = egl * jnp.sum(jnp.sum(st * ds_out, axis=2, keepdims=True), axis=1, keepdims=True)
            dw = -_bdot(dv_new, st, BNT)
            dvb = _bdot(tinv, dv_new, BTN)
            dkbeg = _bdot(tinv, dw, BTN)
            dlm = jnp.where(cm["strict"], -(_bdot(dvb, u, BNT) + _bdot(dkbeg, w, BNT)), 0.0)
            d_a = dlm * decay
            d_qk = d_intra * decay
            e = dlm * lm + d_intra * intra
            colsum = _dot3(e, jnp.ones((NB, CHUNK, LANE), F32), BTN, exact_b=True)[:, :, 0:1]
            dgc = jnp.sum(e, axis=2, keepdims=True) - colsum
            dkb = _bdot(d_a, k) + dkbeg * eg
            dk = _bdot(d_a, kb, BTN) + _bdot(d_qk, q, BTN)
            dq = _bdot(d_qk, k) + d_qg * eg
            dgc = dgc + jnp.sum(d_qg * qg, axis=2, keepdims=True) + jnp.sum(dkbeg * kbeg, axis=2, keepdims=True)
            tdec = jnp.sum(d_kdec * kdec, axis=2, keepdims=True)
            dk = dk + d_kdec * cm["edec"] + dkb * beta
            dgc = dgc - tdec
            dgl = dgl + jnp.sum(tdec, axis=1, keepdims=True)
            dbeta = jnp.sum(dvb * v, axis=2, keepdims=True) + jnp.sum(dkb * k, axis=2, keepdims=True)
            dv = dvb * beta
            lane = lax.broadcasted_iota(jnp.int32, (CHUNK, LANE), 1)
            for rev, dy_ref, dg_ref, rows in ((False, dyf_ref, dgf_ref, rf), (True, dyr_ref, dgr_ref, rr)):
                dgc_tile = jnp.zeros((CHUNK, LANE), F32)
                rest = jnp.zeros((CHUNK, LANE), F32)
                for h in range(GDN_HEADS):
                    b = (GDN_HEADS if rev else 0) + h
                    gi, bi = _gate_lanes(rev, h)
                    dgc_tile = dgc_tile + jnp.where(lane == gi, dgc[b], 0.0)
                    rest = rest + jnp.where(lane == gi, dgl[b], 0.0) + jnp.where(lane == bi, dbeta[b], 0.0)
                    dy_ref[h, rows, :] = dq[b]
                    dy_ref[GDN_HEADS + h, rows, :] = dk[b]
                    dy_ref[2 * GDN_HEADS + h, rows, :] = dv[b]
                le_t = _chunk_masks(not rev)[0].astype(F32)
                dg_ref[rows, :] = _dot3(le_t, dgc_tile, NN, exact_a=True) + rest
            return ds_in

        ds = dstate[...]
        for c in range(DELTA_CHUNKS):
            ds = one_chunk(c, ds)
        dstate[...] = ds

    rows_per_step = DELTA_CHUNKS * CHUNK
    ns = nc // DELTA_CHUNKS

    def col(j, rev, blocks=GDN_HEADS):
        return pl.BlockSpec((blocks, rows_per_step, hd), (lambda n: (j, n, 0)) if rev else (lambda n: (j, ns - 1 - n, 0)))

    def wide(width, rev):
        return pl.BlockSpec((rows_per_step, width), (lambda n: (n, 0)) if rev else (lambda n: (ns - 1 - n, 0)))

    def per_chunk(d1, d2, rev):
        return pl.BlockSpec((DELTA_CHUNKS, GDN_HEADS, d1, d2),
                            (lambda n: (n, 0, 0, 0)) if rev else (lambda n: (ns - 1 - n, 0, 0, 0)))

    def side(rev):
        return [col(0, rev), col(1, rev), col(2, rev), wide(LANE, rev), wide(GDN_WIDTH, rev), per_chunk(hd, hd, rev),
                per_chunk(CHUNK, CHUNK, rev)]

    assert nc % DELTA_CHUNKS == 0 and (na == 0 or ns >= 2)
    res = pl.pallas_call(
        body, name="delta_bwd", grid=(ns,),
        in_specs=side(False) + side(True) + [ANY] * na,
        out_specs=[col(0, False, 3 * GDN_HEADS), col(0, True, 3 * GDN_HEADS), wide(LANE, False), wide(LANE, True)]
        + [ANY] * na,
        out_shape=[jax.ShapeDtypeStruct((3 * GDN_HEADS, s, hd), F32)] * 2 + [jax.ShapeDtypeStruct((s, LANE), F32)] * 2
        + [jax.ShapeDtypeStruct(g.shape, g.dtype) for g in scatter],
        scratch_shapes=[pltpu.VMEM((NB, hd, hd), F32)] + (_gather_semaphores(na) if na else []),
        compiler_params=_params("arbitrary"),
    )(y, y, y, gb, do, sf_all, tf_all, y, y, y, gb, do, sr_all, tr_all, *scatter)
    return res[:4], res[4:]


def _gdn_post_fwd(o_f, o_r, p_pad, norm_row):
    s = o_f.shape[0]
    tm = min(512, s)
    hd = GDN_HEAD_DIM

    def body(of_ref, or_ref, z_ref, w_ref, out_ref, osum_ref):
        o = of_ref[...] + or_ref[...]
        osum_ref[...] = o
        z = z_ref[...]
        gate = z * _sigmoid(z)
        for h in range(GDN_HEADS):
            sl = slice(h * hd, (h + 1) * hd)
            oh = o[:, sl]
            r = lax.rsqrt(jnp.mean(oh * oh, axis=-1, keepdims=True) + EPS)
            out_ref[:, sl] = (oh * r * w_ref[...] * gate[:, sl]).astype(BF16)

    blk = pl.BlockSpec((tm, GDN_WIDTH), lambda i: (i, 0))
    return pl.pallas_call(
        body, name="gdn_post_fwd", grid=(s // tm,),
        in_specs=[blk, blk, pl.BlockSpec((tm, GDN_WIDTH), lambda i: (i, OFF_Z // GDN_WIDTH)),
                  pl.BlockSpec((1, hd), lambda i: (0, 0))],
        out_specs=[blk, blk],
        out_shape=[jax.ShapeDtypeStruct((s, GDN_WIDTH), BF16), jax.ShapeDtypeStruct((s, GDN_WIDTH), F32)],
        compiler_params=_params("parallel"),
    )(o_f, o_r, p_pad, norm_row)


def _gdn_post_bwd(d_out, o_sum, p_pad, norm_row):
    s = o_sum.shape[0]
    tm = min(512, s)
    hd = GDN_HEAD_DIM

    def body(d_ref, o_ref, z_ref, w_ref, do_ref, dz_ref, dw_ref):
        @pl.when(pl.program_id(0) == 0)
        def _():
            dw_ref[...] = jnp.zeros_like(dw_ref)

        z = z_ref[...]
        sg = _sigmoid(z)
        gate = z * sg
        dgate = sg * (1.0 + z * (1.0 - sg))
        wv = w_ref[...]
        dw = jnp.zeros((1, hd), F32)
        for h in range(GDN_HEADS):
            sl = slice(h * hd, (h + 1) * hd)
            oh = o_ref[:, sl]
            dh = d_ref[:, sl]
            r = lax.rsqrt(jnp.mean(oh * oh, axis=-1, keepdims=True) + EPS)
            ohat = oh * r
            dz_ref[:, sl] = (dh * ohat * wv * dgate[:, sl]).astype(BF16)
            drn = dh * gate[:, sl]
            t = drn * wv
            do_ref[:, sl] = r * (t - ohat * jnp.mean(t * ohat, axis=-1, keepdims=True))
            dw = dw + jnp.sum(drn * ohat, axis=0, keepdims=True)
        dw_ref[...] += dw

    blk = pl.BlockSpec((tm, GDN_WIDTH), lambda i: (i, 0))
    vec = pl.BlockSpec((1, hd), lambda i: (0, 0))
    return pl.pallas_call(
        body, name="gdn_post_bwd", grid=(s // tm,),
        in_specs=[blk, blk, pl.BlockSpec((tm, GDN_WIDTH), lambda i: (i, OFF_Z // GDN_WIDTH)), vec],
        out_specs=[blk, pl.BlockSpec((tm, GDN_WIDTH), lambda i: (i, OFF_Z // GDN_WIDTH)), vec],
        out_shape=[jax.ShapeDtypeStruct((s, GDN_WIDTH), F32), jax.ShapeDtypeStruct((s, N_PAD), BF16),
                   jax.ShapeDtypeStruct((1, hd), F32)],
        compiler_params=_params("arbitrary"),
    )(d_out, o_sum, p_pad, norm_row)


def _gdn_forward(p_pad, conv_wt, alog_row, dt_row, norm_row, gather=()):
    c_pre, y = _gdn_conv_fwd(p_pad, conv_wt)
    gb = _gdn_gates_fwd(p_pad, alog_row, dt_row)
    (o_f, o_r, s_f, s_r, t_f, t_r), gathered = _delta_fwd2(y, gb, gather)
    out, o_sum = _gdn_post_fwd(o_f, o_r, p_pad, norm_row)
    return out, (c_pre, y, gb, s_f, t_f, s_r, t_r, o_sum), gathered


def _gdn_backward(d_out, p_pad, conv_wt, alog_row, dt_row, norm_row, saved, scatter=()):
    c_pre, y, gb, s_f, t_f, s_r, t_r, o_sum = saved
    do, dp_all, dnorm = _gdn_post_bwd(d_out, o_sum, p_pad, norm_row)
    (dy_f, dy_r, dgb_f, dgb_r), received = _delta_bwd2(y, gb, do, s_f, s_r, t_f, t_r, scatter)
    dp_all, dconv = _gdn_conv_bwd(dy_f, dy_r, c_pre, p_pad, conv_wt, dp_all)
    dp_all, gate_sums = _gdn_gates_bwd(dgb_f, dgb_r, p_pad, gb, alog_row, dt_row, dp_all)
    return dp_all, dconv, gate_sums, dnorm, received


ATT_BK = ATT_BQ + 2 * ATT_HALO
ATT_SUB = 8
SWA_SCALE = SWA_HEAD_DIM ** -0.5


def _t5_bucket(rel):
    nb = REL_BUCKETS // 2
    bucket = (rel > 0).astype(np.int32) * nb
    n = np.abs(rel)
    max_exact = nb // 2
    large = max_exact + (np.log(np.maximum(n, 1) / max_exact)
                         / math.log(REL_MAX_DISTANCE / max_exact) * (nb - max_exact)).astype(np.int32)
    large = np.minimum(large, nb - 1)
    return (bucket + np.where(n < max_exact, n, large)).astype(np.int32)


def _band_tables(dilation, queries_are_rows_of_block):
    blk = np.arange(ATT_BQ)
    band = np.arange(ATT_BK) - ATT_HALO
    if queries_are_rows_of_block:
        rel = band[None, :] - blk[:, None]
        band_idx = np.broadcast_to(np.arange(ATT_BK)[None, :], rel.shape)
    else:
        rel = blk[None, :] - band[:, None]
        band_idx = np.broadcast_to(np.arange(ATT_BK)[:, None], rel.shape)
    base = np.abs(rel) <= ATT_HALO
    not_prev = band_idx >= ATT_HALO
    not_next = band_idx < ATT_HALO + ATT_BQ
    valid = np.stack([base & not_prev, base, base & not_next, base & not_prev & not_next])
    return valid, _t5_bucket(rel * dilation)


def _bias_tiles(rel_bias, dilation, queries_are_rows_of_block):
    valid, bucket = _band_tables(dilation, queries_are_rows_of_block)
    onehot = (jnp.asarray(bucket.reshape(-1, 1)) == jnp.arange(REL_BUCKETS, dtype=jnp.int32)[None, :]).astype(F32)
    rb = jnp.dot(onehot, rel_bias.astype(F32), precision=lax.Precision.HIGHEST)
    rb = rb.T.reshape((SWA_HEADS,) + bucket.shape)
    return jnp.where(valid[:, None], rb[None], NEG_BIG).astype(F32)


def _group_sum(x, bd):
    hi = x.astype(BF16)
    lo = (x - hi.astype(F32)).astype(BF16)
    return jnp.dot(hi, bd, preferred_element_type=F32) + jnp.dot(lo, bd, preferred_element_type=F32)


def _head_block_diag():
    idx = np.arange(SWA_WIDTH) // SWA_HEAD_DIM
    return jnp.asarray(idx[:, None] == idx[None, :], BF16)


def _swa_pre_fwd(p_pad, qw_row, kw_row, bd):
    s = p_pad.shape[0]
    tm = min(512, s)
    inv = 1.0 / SWA_HEAD_DIM

    def body(q_ref, k_ref, v_ref, qw_ref, kw_ref, bd_ref, qo_ref, ko_ref, vo_ref):
        bdv = bd_ref[...]
        q = q_ref[...]
        k = k_ref[...]
        rq = lax.rsqrt(_group_sum(q * q, bdv) * inv + EPS)
        rk = lax.rsqrt(_group_sum(k * k, bdv) * inv + EPS)
        qo_ref[...] = (q * rq * qw_ref[...] * SWA_SCALE).astype(BF16)
        ko_ref[...] = (k * rk * kw_ref[...]).astype(BF16)
        vo_ref[...] = v_ref[...].astype(BF16)

    base = OFF_B // SWA_WIDTH
    blk = pl.BlockSpec((tm, SWA_WIDTH), lambda i: (i, 0))
    vec = pl.BlockSpec((1, SWA_WIDTH), lambda i: (0, 0))
    return pl.pallas_call(
        body, name="swa_pre_fwd", grid=(s // tm,),
        in_specs=[pl.BlockSpec((tm, SWA_WIDTH), lambda i: (i, base)), pl.BlockSpec((tm, SWA_WIDTH), lambda i: (i, base + 1)),
                  pl.BlockSpec((tm, SWA_WIDTH), lambda i: (i, base + 2)), vec, vec,
                  pl.BlockSpec((SWA_WIDTH, SWA_WIDTH), lambda i: (0, 0))],
        out_specs=[blk, blk, blk],
        out_shape=[jax.ShapeDtypeStruct((s, SWA_WIDTH), BF16)] * 3,
        compiler_params=_params("parallel"),
    )(p_pad, p_pad, p_pad, qw_row, kw_row, bd)


def _swa_pre_bwd(dqs, dks, dvs, p_pad, qw_row, kw_row, bd, dp_all):
    s = p_pad.shape[0]
    tm = min(256, s)
    inv = 1.0 / SWA_HEAD_DIM
    npat = len(dqs)

    def body(*refs):
        dq_refs, dk_refs, dv_refs = refs[:npat], refs[npat:2 * npat], refs[2 * npat:3 * npat]
        q_ref, k_ref, qw_ref, kw_ref, bd_ref, _, dp_ref, dqw_ref, dkw_ref = refs[3 * npat:]

        @pl.when(pl.program_id(0) == 0)
        def _():
            dqw_ref[...] = jnp.zeros_like(dqw_ref)
            dkw_ref[...] = jnp.zeros_like(dkw_ref)

        bdv = bd_ref[...]

        def norm_bwd(x, g, w, scale):
            r = lax.rsqrt(_group_sum(x * x, bdv) * inv + EPS)
            xhat = x * r
            t = g * w * scale
            dx = r * (t - xhat * (_group_sum(t * xhat, bdv) * inv))
            return dx, jnp.sum(g * scale * xhat, axis=0, keepdims=True)

        def total(rs):
            t = rs[0][...].astype(F32)
            for r in rs[1:]:
                t = t + r[...].astype(F32)
            return t

        dq, dqw = norm_bwd(q_ref[...], total(dq_refs), qw_ref[...], SWA_SCALE)
        dk, dkw = norm_bwd(k_ref[...], total(dk_refs), kw_ref[...], 1.0)
        dp_ref[:, 0:SWA_WIDTH] = dq.astype(BF16)
        dp_ref[:, SWA_WIDTH:2 * SWA_WIDTH] = dk.astype(BF16)
        dp_ref[:, 2 * SWA_WIDTH:3 * SWA_WIDTH] = total(dv_refs).astype(BF16)
        dqw_ref[...] += dqw
        dkw_ref[...] += dkw

    base = OFF_B // SWA_WIDTH
    blk = pl.BlockSpec((tm, SWA_WIDTH), lambda i: (i, 0))
    vec = pl.BlockSpec((1, SWA_WIDTH), lambda i: (0, 0))
    return pl.pallas_call(
        body, name="swa_pre_bwd", grid=(s // tm,),
        in_specs=[blk] * (3 * npat) + [pl.BlockSpec((tm, SWA_WIDTH), lambda i: (i, base)),
                                      pl.BlockSpec((tm, SWA_WIDTH), lambda i: (i, base + 1)), vec, vec,
                                      pl.BlockSpec((SWA_WIDTH, SWA_WIDTH), lambda i: (0, 0)), ANY],
        out_specs=[pl.BlockSpec((tm, 3 * SWA_WIDTH), lambda i: (i, OFF_B // (3 * SWA_WIDTH))), vec, vec],
        out_shape=[jax.ShapeDtypeStruct(dp_all.shape, dp_all.dtype), jax.ShapeDtypeStruct((1, SWA_WIDTH), F32),
                   jax.ShapeDtypeStruct((1, SWA_WIDTH), F32)],
        input_output_aliases={3 * npat + 5: 0},
        compiler_params=_params("arbitrary"),
    )(*dqs, *dks, *dvs, p_pad, p_pad, qw_row, kw_row, bd, dp_all)


def _band_specs(length, rows):
    per = rows // ATT_HALO
    last = length // ATT_HALO - 1
    prev = pl.BlockSpec((ATT_HALO, SWA_WIDTH), lambda r, t: (jnp.maximum(t * per - 1, 0), r))
    cur = pl.BlockSpec((rows, SWA_WIDTH), lambda r, t: (t, r))
    nxt = pl.BlockSpec((ATT_HALO, SWA_WIDTH), lambda r, t: (jnp.minimum((t + 1) * per, last), r))
    return [prev, cur, nxt]


def _tile_variant(t, nb, u, sub):
    first, last = u == 0, u == sub - 1
    if first and last:
        return 3 if nb == 1 else jnp.where(t == 0, 0, jnp.where(t == nb - 1, 2, 1))
    if first:
        return jnp.where(t == 0, 0, 1)
    if last:
        return jnp.where(t == nb - 1, 2, 1)
    return 1


def _bias_specs(nb, sub, rows, cols):
    return [pl.BlockSpec((1, SWA_HEADS, rows, cols),
                         functools.partial(lambda r, t, u: (_tile_variant(t, nb, u, sub), 0, 0, 0), u=u))
            for u in range(sub)]


def _band(refs):
    return jnp.concatenate([r[...] for r in refs], axis=0)


def _sub(u, width=ATT_BQ):
    return slice(u * ATT_BQ, u * ATT_BQ + width)


N_PAIRS = SWA_HEADS // 2


def _pairs(x):
    return jnp.stack([x[:, LANE * p:LANE * (p + 1)] for p in range(N_PAIRS)])


def _per_head_rows(x):
    first = lax.broadcasted_iota(jnp.int32, x.shape, 2) < SWA_HEAD_DIM
    zero = jnp.zeros_like(x)
    return jnp.concatenate([jnp.where(first, x, zero), jnp.where(first, zero, x)], axis=1)


def _per_head_cols(x):
    return jnp.stack([jnp.concatenate([x[:, LANE * p:LANE * p + 1],
                                       x[:, LANE * p + SWA_HEAD_DIM:LANE * p + SWA_HEAD_DIM + 1]], axis=0)
                      for p in range(N_PAIRS)])


def _merge_heads(x, rows):
    first = lax.broadcasted_iota(jnp.int32, (N_PAIRS, rows, LANE), 2) < SWA_HEAD_DIM
    return jnp.where(first, x[:, :rows], x[:, rows:])


def _store_pairs(ref, x, rows):
    for p in range(N_PAIRS):
        ref[rows, LANE * p:LANE * (p + 1)] = x[p].astype(ref.dtype)


def _att_fwd2(q, k, v, bias, dilation):
    s = q.shape[0]
    length = s // dilation
    sub = min(ATT_SUB, length // ATT_BQ)
    rows = sub * ATT_BQ
    nb = length // rows
    view = (length, dilation * SWA_WIDTH)

    def body(q_ref, kp, kc, kn, vp, vc, vn, *rest):
        b_refs, (o_ref, lse_ref) = rest[:sub], rest[sub:]
        kwin, vwin = _band((kp, kc, kn)), _band((vp, vc, vn))
        for u in range(sub):
            kb, vb = _pairs(kwin[_sub(u, ATT_BK)]), _pairs(vwin[_sub(u, ATT_BK)])
            qm = _per_head_rows(_pairs(q_ref[_sub(u), :]))
            sc = _bdot(qm, kb, BNT) + b_refs[u][0].reshape(N_PAIRS, 2 * ATT_BQ, ATT_BK)
            m = jnp.max(sc, axis=-1, keepdims=True)
            p = jnp.exp(sc - m)
            den = jnp.sum(p, axis=-1, keepdims=True)
            o = _bdot(p, vb) / den
            _store_pairs(o_ref, _merge_heads(o, ATT_BQ), _sub(u))
            lse = jnp.broadcast_to(m + jnp.log(den), (N_PAIRS, 2 * ATT_BQ, LANE))
            _store_pairs(lse_ref, _merge_heads(lse, ATT_BQ), _sub(u))

    cur = pl.BlockSpec((rows, SWA_WIDTH), lambda r, t: (t, r))
    o, lse = pl.pallas_call(
        body, name=f"att_fwd_d{dilation}", grid=(dilation, nb),
        in_specs=[cur] + _band_specs(length, rows) * 2 + _bias_specs(nb, sub,ATT_BQ, ATT_BK),
        out_specs=[cur, cur],
        out_shape=[jax.ShapeDtypeStruct(view, BF16), jax.ShapeDtypeStruct(view, F32)],
        compiler_params=_params("parallel", "parallel"),
    )(q.reshape(view), *([k.reshape(view)] * 3), *([v.reshape(view)] * 3), *([bias] * sub))
    return o.reshape(s, SWA_WIDTH), lse.reshape(s, SWA_WIDTH)


def _att_dq2(q, k, v, dop, lse, cp, bias, dilation):
    s = q.shape[0]
    length = s // dilation
    sub = min(ATT_SUB, length // ATT_BQ)
    rows = sub * ATT_BQ
    nb = length // rows
    view = (length, dilation * SWA_WIDTH)

    def body(q_ref, kp, kc, kn, vp, vc, vn, do_ref, lse_ref, cp_ref, *rest):
        b_refs, (dq_ref, db_ref) = rest[:sub], rest[sub:]

        @pl.when((pl.program_id(0) == 0) & (pl.program_id(1) == 0))
        def _():
            db_ref[...] = jnp.zeros_like(db_ref)

        kwin, vwin = _band((kp, kc, kn)), _band((vp, vc, vn))
        for u in range(sub):
            kb, vb = _pairs(kwin[_sub(u, ATT_BK)]), _pairs(vwin[_sub(u, ATT_BK)])
            qm = _per_head_rows(_pairs(q_ref[_sub(u), :]))
            dom = _per_head_rows(_pairs(do_ref[_sub(u), :]))
            sc = _bdot(qm, kb, BNT) + b_refs[u][0].reshape(N_PAIRS, 2 * ATT_BQ, ATT_BK)
            p = jnp.exp(sc - _per_head_cols(lse_ref[_sub(u), :]))
            ds = p * (_bdot(dom, vb, BNT) + _per_head_cols(cp_ref[_sub(u), :]))
            _store_pairs(dq_ref, _merge_heads(_bdot(ds, kb), ATT_BQ), _sub(u))
            db_ref[_tile_variant(pl.program_id(1), nb, u, sub)] += ds.reshape(SWA_HEADS, ATT_BQ, ATT_BK)

    cur = pl.BlockSpec((rows, SWA_WIDTH), lambda r, t: (t, r))
    dq, db = pl.pallas_call(
        body, name=f"att_dq_d{dilation}", grid=(dilation, nb),
        in_specs=[cur] + _band_specs(length, rows) * 2 + [cur, cur, cur] + _bias_specs(nb, sub,ATT_BQ, ATT_BK),
        out_specs=[cur, pl.BlockSpec((4, SWA_HEADS, ATT_BQ, ATT_BK), lambda r, t: (0, 0, 0, 0))],
        out_shape=[jax.ShapeDtypeStruct(view, BF16), jax.ShapeDtypeStruct((4, SWA_HEADS, ATT_BQ, ATT_BK), F32)],
        compiler_params=_params("arbitrary", "arbitrary"),
    )(q.reshape(view), *([k.reshape(view)] * 3), *([v.reshape(view)] * 3), dop.reshape(view), lse.reshape(view),
      cp.reshape(view), *([bias] * sub))
    return dq.reshape(s, SWA_WIDTH), db


def _att_dkv2(q, k, v, dop, lse, cp, bias_t, dilation):
    s = q.shape[0]
    length = s // dilation
    sub = min(ATT_SUB, length // ATT_BQ)
    rows = sub * ATT_BQ
    nb = length // rows
    view = (length, dilation * SWA_WIDTH)

    def body(k_ref, v_ref, qp, qc, qn, dp_, dc_, dn_, lp, lc, ln, cp_, cc_, cn_, *rest):
        b_refs, (dk_ref, dv_ref) = rest[:sub], rest[sub:]
        qwin, dowin = _band((qp, qc, qn)), _band((dp_, dc_, dn_))
        lsewin, cpwin = _band((lp, lc, ln)), _band((cp_, cc_, cn_))
        for u in range(sub):
            band = _sub(u, ATT_BK)
            qm = _per_head_rows(_pairs(qwin[band]))
            dom = _per_head_rows(_pairs(dowin[band]))
            kv, vv = _pairs(k_ref[_sub(u), :]), _pairs(v_ref[_sub(u), :])
            sc = _bdot(qm, kv, BNT) + b_refs[u][0].reshape(N_PAIRS, 2 * ATT_BK, ATT_BQ)
            p = jnp.exp(sc - _per_head_cols(lsewin[band]))
            _store_pairs(dv_ref, _bdot(p, dom, BTN), _sub(u))
            ds = p * (_bdot(dom, vv, BNT) + _per_head_cols(cpwin[band]))
            _store_pairs(dk_ref, _bdot(ds, qm, BTN), _sub(u))

    cur = pl.BlockSpec((rows, SWA_WIDTH), lambda r, t: (t, r))
    dk, dv = pl.pallas_call(
        body, name=f"att_dkv_d{dilation}", grid=(dilation, nb),
        in_specs=[cur, cur] + _band_specs(length, rows) * 4 + _bias_specs(nb, sub,ATT_BK, ATT_BQ),
        out_specs=[cur, cur],
        out_shape=[jax.ShapeDtypeStruct(view, BF16)] * 2,
        compiler_params=_params("parallel", "parallel"),
    )(k.reshape(view), v.reshape(view), *([q.reshape(view)] * 3), *([dop.reshape(view)] * 3),
      *([lse.reshape(view)] * 3), *([cp.reshape(view)] * 3), *([bias_t] * sub))
    return dk.reshape(s, SWA_WIDTH), dv.reshape(s, SWA_WIDTH)


def _pattern_weights(lses):
    m = lses[0]
    for l in lses[1:]:
        m = jnp.maximum(m, l)
    es = [jnp.exp(l - m) for l in lses]
    den = es[0]
    for e in es[1:]:
        den = den + e
    return [e / den for e in es]


def _combine_fwd(outs, lses):
    s = outs[0].shape[0]
    tm = min(512, s)
    npat = len(outs)

    def body(*refs):
        ws = _pattern_weights([r[...] for r in refs[npat:2 * npat]])
        o = ws[0] * refs[0][...]
        for p in range(1, npat):
            o = o + ws[p] * refs[p][...]
        refs[2 * npat][...] = o.astype(BF16)

    blk = pl.BlockSpec((tm, SWA_WIDTH), lambda i: (i, 0))
    return pl.pallas_call(
        body, name="swa_combine_fwd", grid=(s // tm,), in_specs=[blk] * (2 * npat), out_specs=blk,
        out_shape=jax.ShapeDtypeStruct((s, SWA_WIDTH), BF16), compiler_params=_params("parallel"),
    )(*outs, *lses)


def _combine_bwd(d_out, outs, lses, bd):
    s = d_out.shape[0]
    tm = min(512, s)
    npat = len(outs)

    def body(*refs):
        d_ref, bd_ref = refs[0], refs[1 + 2 * npat]
        o_refs, l_refs = refs[1:1 + npat], refs[1 + npat:1 + 2 * npat]
        out_refs = refs[2 + 2 * npat:]
        ws = _pattern_weights([r[...] for r in l_refs])
        dov = d_ref[...]
        o = ws[0] * o_refs[0][...]
        for p in range(1, npat):
            o = o + ws[p] * o_refs[p][...]
        rd = _group_sum(dov * o, bd_ref[...])
        for p in range(npat):
            out_refs[p][...] = (ws[p] * dov).astype(BF16)
            out_refs[npat + p][...] = -ws[p] * rd

    blk = pl.BlockSpec((tm, SWA_WIDTH), lambda i: (i, 0))
    res = pl.pallas_call(
        body, name="swa_combine_bwd", grid=(s // tm,),
        in_specs=[blk] * (1 + 2 * npat) + [pl.BlockSpec((SWA_WIDTH, SWA_WIDTH), lambda i: (0, 0))],
        out_specs=[blk] * (2 * npat),
        out_shape=[jax.ShapeDtypeStruct((s, SWA_WIDTH), BF16)] * npat + [jax.ShapeDtypeStruct((s, SWA_WIDTH), F32)] * npat,
        compiler_params=_params("parallel"),
    )(d_out, *outs, *lses, bd)
    return res[:npat], res[npat:]


def _rel_bias_grad(dbs, buckets):
    npat = len(dbs)

    def body(*refs):
        db_refs, bk_refs, o_ref = refs[:npat], refs[npat:2 * npat], refs[2 * npat]
        row = lax.broadcasted_iota(jnp.int32, (REL_BUCKETS, LANE), 0)
        lane = lax.broadcasted_iota(jnp.int32, (REL_BUCKETS, LANE), 1)
        tiles = [[db_refs[p][0, h] + db_refs[p][1, h] + db_refs[p][2, h] + db_refs[p][3, h] for h in range(SWA_HEADS)]
                 for p in range(npat)]
        bks = [r[...] for r in bk_refs]

        def one_bucket(b, acc):
            for h in range(SWA_HEADS):
                tot = jnp.zeros((1, 1), F32)
                for p in range(npat):
                    sel = jnp.where(bks[p] == b, tiles[p][h], 0.0)
                    tot = tot + jnp.sum(jnp.sum(sel, axis=1, keepdims=True), axis=0, keepdims=True)
                acc = acc + jnp.where((row == b) & (lane == h), tot, 0.0)
            return acc

        o_ref[...] = lax.fori_loop(0, REL_BUCKETS, one_bucket, jnp.zeros((REL_BUCKETS, LANE), F32))

    full4 = pl.BlockSpec((4, SWA_HEADS, ATT_BQ, ATT_BK), lambda: (0, 0, 0, 0))
    full2 = pl.BlockSpec((ATT_BQ, ATT_BK), lambda: (0, 0))
    return pl.pallas_call(
        body, name="rel_bias_grad", in_specs=[full4] * npat + [full2] * npat,
        out_specs=pl.BlockSpec((REL_BUCKETS, LANE), lambda: (0, 0)),
        out_shape=jax.ShapeDtypeStruct((REL_BUCKETS, LANE), F32),
        compiler_params=pltpu.CompilerParams(vmem_limit_bytes=V7X_VMEM_LIMIT_BYTES),
    )(*dbs, *buckets)


def _swa_forward(p_pad, qw_row, kw_row, rel_bias, bd):
    q, k, v = _swa_pre_fwd(p_pad, qw_row, kw_row, bd)
    outs, lses = [], []
    for _, dil in DILATION_PATTERNS:
        o, lse = _att_fwd2(q, k, v, _bias_tiles(rel_bias, dil, True), dil)
        outs.append(o)
        lses.append(lse)
    return _combine_fwd(outs, lses), (q, k, v, outs, lses)


def _swa_backward(d_out, p_pad, qw_row, kw_row, rel_bias, bd, saved, dp_all):
    q, k, v, outs, lses = saved
    dops, cps = _combine_bwd(d_out, outs, lses, bd)
    dqs, dks, dvs, dbs, buckets = [], [], [], [], []
    for p, (_, dil) in enumerate(DILATION_PATTERNS):
        dq, db = _att_dq2(q, k, v, dops[p], lses[p], cps[p], _bias_tiles(rel_bias, dil, True), dil)
        dk, dv = _att_dkv2(q, k, v, dops[p], lses[p], cps[p], _bias_tiles(rel_bias, dil, False), dil)
        dqs.append(dq)
        dks.append(dk)
        dvs.append(dv)
        dbs.append(db)
        buckets.append(jnp.asarray(_band_tables(dil, True)[1]))
    dp, dqw, dkw = _swa_pre_bwd(dqs, dks, dvs, p_pad, qw_row, kw_row, bd, dp_all)
    return dp, dqw, dkw, _rel_bias_grad(dbs, buckets)


def _lane_row(v):
    flat = v.reshape(-1).astype(F32)
    return jnp.zeros((1, LANE), F32).at[0, :flat.shape[0]].set(flat)


W_IN_SHARD = N_IN // N_DEV
W_IN_RUNS = ((0, NAT_Z, 0), (NAT_Z, NAT_AB, OFF_Z), (NAT_AB, NAT_B, OFF_AB), (NAT_B, N_IN, OFF_B))


def _w_in_pieces(shard):
    lo, hi = shard * W_IN_SHARD, (shard + 1) * W_IN_SHARD
    out = []
    for first, last, dst in W_IN_RUNS:
        a, b = max(lo, first), min(hi, last)
        if a < b:
            out.append((a - lo, b - a, dst + a - first))
    return out


def _w_in_from_slabs(w3):
    nd, r, _ = w3.shape

    def body(w_ref, o_ref):
        o_ref[:, OFF_AB:N_PAD] = jnp.zeros((r, N_PAD - OFF_AB), w3.dtype)
        for sh in range(nd):
            for src, length, dst in _w_in_pieces(sh):
                o_ref[:, dst:dst + length] = w_ref[sh, :, src:src + length]

    return pl.pallas_call(
        body, name="w_in_from_slabs", out_shape=jax.ShapeDtypeStruct((r, N_PAD), w3.dtype),
        compiler_params=pltpu.CompilerParams(vmem_limit_bytes=V7X_VMEM_LIMIT_BYTES),
    )(w3)


def _w_in_grad_slabs(dw_pad, dtype):
    r = dw_pad.shape[0]

    def body(dw_ref, o_ref):
        for sh in range(N_DEV):
            for src, length, dst in _w_in_pieces(sh):
                o_ref[sh, :, src:src + length] = dw_ref[:, dst:dst + length].astype(dtype)

    return pl.pallas_call(
        body, name="w_in_grad_slabs", out_shape=jax.ShapeDtypeStruct((N_DEV, r, W_IN_SHARD), dtype),
        compiler_params=pltpu.CompilerParams(vmem_limit_bytes=V7X_VMEM_LIMIT_BYTES),
    )(dw_pad)


LATE = ("w_out", "ffn2_w_gate", "ffn2_w_up", "ffn2_w_down")
TRANSPOSED = ("ffn1_w_gate", "ffn1_w_up", "ffn2_w_gate", "ffn2_w_up")


def _late_weights(slabs):
    return {n: g.reshape(N_DEV * g.shape[1], g.shape[2]) for n, g in zip(LATE, slabs)}


def _local_step(x, tgt, wts, small, late_shards=None):
    bd = _head_block_diag()
    alog_row, dt_row = _lane_row(small["a_log"]), _lane_row(small["dt_bias"])
    gnorm_row = small["gdn_norm_w"].reshape(1, GDN_HEAD_DIM)
    qw_row = jnp.tile(small["q_norm_w"].reshape(-1), SWA_HEADS).reshape(1, SWA_WIDTH)
    kw_row = jnp.tile(small["k_norm_w"].reshape(-1), SWA_HEADS).reshape(1, SWA_WIDTH)
    rel_bias = small["rel_bias"]
    exchange = late_shards is not None
    dw_dtype = BF16 if exchange else F32

    x1, sv1, (wg1, wu1, wd1), got, first, (n2, r2) = _ffn_forward(
        x, small["ffn1_norm"], wts.get("ffn1_w_gate"), wts.get("ffn1_w_up"), wts.get("ffn1_w_down"), "ffn1",
        gather=[late_shards["ffn1_w_down"], late_shards["w_in"]] if exchange else (),
        head=[late_shards["ffn1_w_gate"], late_shards["ffn1_w_up"], late_shards["conv_w"]] if exchange else (),
        next_norm=small["mix_norm"])
    win_pad = _w_in_from_slabs(got[0]) if exchange else wts["w_in_pad"]
    conv_w = first[0].reshape(N_DEV, -1)[:, :QKV_A // N_DEV * CONV_WIDTH].reshape(QKV_A, CONV_WIDTH) if exchange \
        else small["conv_w"]
    conv_wt = jnp.zeros((8, QKV_A), F32).at[:CONV_WIDTH].set(conv_w.T)
    p_pad = _matmul([(n2, win_pad)], tm=256, tn=N_PAD, tk=D_MODEL, name="w_in")
    o_a, sva, gathered = _gdn_forward(p_pad, conv_wt, alog_row, dt_row, gnorm_row,
                                      gather=[late_shards[n] for n in LATE] if exchange else ())
    if exchange:
        wts = {**wts, **_late_weights(gathered)}
    wo_a, wo_b = wts["w_out"][:GDN_WIDTH], wts["w_out"][GDN_WIDTH:]
    o_b, svb = _swa_forward(p_pad, qw_row, kw_row, rel_bias, bd)
    x2, n3, r3 = _matmul([(o_a, wo_a), (o_b, wo_b)], tm=512, tn=D_MODEL, tk=GDN_WIDTH, name="w_out", res=x1,
                         norm_fwd=small["ffn2_norm"])
    (loss_row, dx3, d_final), sv2, _, _, _, _ = _ffn_forward(
        x2, small["ffn2_norm"], wts["ffn2_w_gate"], wts["ffn2_w_up"], wts["ffn2_w_down"], "ffn2", normed=(n3, r3),
        loss=(small["final_norm"], tgt))

    dx2, d_ffn2_norm, dwg2, dwu2, dwd2, _ = _ffn_backward(
        dx3, x2, small["ffn2_norm"], wts["ffn2_w_gate"], wts["ffn2_w_up"], wts["ffn2_w_down"], sv2, "ffn2", dw_dtype)
    d_oa = _matmul([(dx2, wo_a)], tb=True, tm=512, tn=GDN_WIDTH, tk=D_MODEL, name="w_out_da")
    d_ob = _matmul([(dx2, wo_b)], tb=True, tm=512, tn=SWA_WIDTH, tk=D_MODEL, name="w_out_db")
    dwo_a = _matmul([(o_a, dx2)], ta=True, tm=GDN_WIDTH, tn=D_MODEL, tk=2048, name="w_out_dwa", out_dtype=dw_dtype)
    dwo_b = _matmul([(o_b, dx2)], ta=True, tm=SWA_WIDTH, tn=D_MODEL, tk=2048, name="w_out_dwb", out_dtype=dw_dtype)

    late_grads = [_row_slabs(jnp.concatenate([dwo_a, dwo_b], axis=0)), dwg2, dwu2, dwd2]
    dp_all, dconv, gate_sums, d_gnorm, received = _gdn_backward(
        d_oa, p_pad, conv_wt, alog_row, dt_row, gnorm_row, sva, scatter=late_grads if exchange else ())
    if exchange:
        late_grads = received
    dp_all, dqw, dkw, d_rel = _swa_backward(d_ob, p_pad, qw_row, kw_row, rel_bias, bd, svb, dp_all)
    dw_pad = _matmul([(n2, dp_all)], ta=True, tm=D_MODEL, tn=N_PAD // 3, tk=2048, name="w_in_dw")
    dx1, d_mix_norm = _matmul([(dp_all, win_pad)], tb=True, tm=512, tn=D_MODEL, tk=N_PAD, name="w_in_dn",
                              norm_bwd=(x1, r2, small["mix_norm"], dx2))
    d_w_in = _w_in_grad_slabs(dw_pad, dw_dtype)
    dx, d_ffn1_norm, dwg1, dwu1, dwd1, got = _ffn_backward(
        dx1, x, small["ffn1_norm"], wg1, wu1, wd1, sv1, "ffn1", dw_dtype,
        scatter=[d_w_in] if exchange else None)
    if exchange:
        d_w_in = got[0]

    grads = {
        "ffn1_norm": d_ffn1_norm, "ffn1_w_gate": dwg1, "ffn1_w_up": dwu1, "ffn1_w_down": dwd1,
        "mix_norm": d_mix_norm, "w_in": d_w_in, "conv_w": dconv[:CONV_WIDTH].T,
        "a_log": gate_sums[0, :8].reshape(2, GDN_HEADS), "dt_bias": gate_sums[1, :8].reshape(2, GDN_HEADS),
        "gdn_norm_w": d_gnorm, "q_norm_w": dqw.reshape(SWA_HEADS, SWA_HEAD_DIM).sum(0, keepdims=True),
        "k_norm_w": dkw.reshape(SWA_HEADS, SWA_HEAD_DIM).sum(0, keepdims=True), "rel_bias": d_rel[:, :SWA_HEADS],
        "ffn2_norm": d_ffn2_norm, "final_norm": d_final, **dict(zip(LATE, late_grads)),
    }
    return loss_row, dx, grads


MESH_IDS = pl.DeviceIdType.MESH
ANY = pl.BlockSpec(memory_space=pl.ANY)


def _adamw(parts, w, m, v, name):
    nparts, r, n = parts.shape
    tr = r
    for cand in (256, 176, 128, 104, 64, 8):
        if r % cand == 0:
            tr = cand
            break
    bc1 = 1.0 - ADAM_B1 ** ADAM_STEP
    bc2 = 1.0 - ADAM_B2 ** ADAM_STEP

    def body(p_ref, w_ref, m_ref, v_ref, g_ref, d_ref, nm_ref, nv_ref):
        g = p_ref[0].astype(F32)
        for k in range(1, nparts):
            g = g + p_ref[k].astype(F32)
        mn = ADAM_B1 * m_ref[...] + (1.0 - ADAM_B1) * g
        vn = ADAM_B2 * v_ref[...] + (1.0 - ADAM_B2) * (g * g)
        m_hat = mn / bc1
        v_hat = vn / bc2
        g_ref[...] = g
        nm_ref[...] = mn
        nv_ref[...] = vn
        d_ref[...] = -ADAM_LR * (m_hat / (jnp.sqrt(v_hat) + ADAM_EPS) + ADAM_WD * w_ref[...])

    blk = pl.BlockSpec((tr, n), lambda i: (i, 0))
    return pl.pallas_call(
        body, name=name, grid=(r // tr,),
        in_specs=[pl.BlockSpec((nparts, tr, n), lambda i: (0, i, 0)), blk, blk, blk],
        out_specs=[blk] * 4, out_shape=[jax.ShapeDtypeStruct((r, n), F32)] * 4,
        compiler_params=_params("parallel"),
    )(parts, w, m, v)


def _mesh_place():
    x, y, c = lax.axis_index("x"), lax.axis_index("y"), lax.axis_index("c")
    return x, y, c, [(1 - x, y), (x, 1 - y), (1 - x, 1 - y)]


def _gather_phases(x_refs, out_refs, send_sems, recv_sems, local_sems):
    na = len(x_refs)

    def place():
        x, y, c, chips = _mesh_place()
        return (x, y, c), (x, y, 1 - c), chips, c

    def slab(i, px, py, pc):
        return out_refs[i].at[4 * px + 2 * py + pc]

    def copy(i, k, block, to, src=None):
        return pltpu.make_async_remote_copy(
            src_ref=slab(i, *block) if src is None else src, dst_ref=slab(i, *block),
            send_sem=send_sems.at[i, k], recv_sem=recv_sems.at[i, k], device_id=to, device_id_type=MESH_IDS)

    def own(i, me):
        return pltpu.make_async_copy(x_refs[i], slab(i, *me), local_sems.at[i])

    def sends(i, me, sibling, chips, c):
        return [copy(i, 0, me, sibling, src=x_refs[i])] + [copy(i, 1 + j, me, (*chip, c), src=x_refs[i])
                                                          for j, chip in enumerate(chips)]

    def start():
        me, sibling, chips, c = place()
        for i in range(na):
            own(i, me).start()
            for cp in sends(i, me, sibling, chips, c):
                cp.start()

    def forward():
        me, sibling, chips, c = place()
        for j, chip in enumerate(chips):
            for i in range(na):
                copy(i, 1 + j, (*chip, c), me).wait_recv()
                copy(i, 4 + j, (*chip, c), sibling).start()

    def finish():
        me, sibling, chips, c = place()
        for i in range(na):
            copy(i, 0, sibling, me).wait_recv()
        for j, chip in enumerate(chips):
            for i in range(na):
                copy(i, 4 + j, (*chip, 1 - c), me).wait_recv()
        for i in range(na):
            for cp in sends(i, me, sibling, chips, c):
                cp.wait_send()
            for j, chip in enumerate(chips):
                copy(i, 4 + j, (*chip, c), sibling).wait_send()
            own(i, me).wait()

    return start, forward, finish


def _gather_semaphores(na):
    return [pltpu.SemaphoreType.DMA((na, 7)), pltpu.SemaphoreType.DMA((na, 7)), pltpu.SemaphoreType.DMA((na,))]


def _scatter_phases(g_refs, out_refs, send_sems, recv_sems, local_sems):
    na = len(g_refs)

    def place(m):
        x, y, c = lax.axis_index("x"), lax.axis_index("y"), lax.axis_index("c")
        px = 1 - x if m & 4 else x
        py = 1 - y if m & 2 else y
        pc = 1 - c if m & 1 else c
        return 4 * x + 2 * y + c, (px, py, pc), 4 * px + 2 * py + pc

    def own(i):
        me, _, _ = place(0)
        return pltpu.make_async_copy(g_refs[i].at[me], out_refs[i].at[me], local_sems.at[i])

    def start():
        for i in range(na):
            own(i).start()
            for m in range(1, N_DEV):
                me, peer, peer_idx = place(m)
                pltpu.make_async_remote_copy(
                    src_ref=g_refs[i].at[peer_idx], dst_ref=out_refs[i].at[me], send_sem=send_sems.at[i, m - 1],
                    recv_sem=recv_sems.at[i, m - 1], device_id=peer, device_id_type=MESH_IDS).start()

    def finish():
        for i in range(na):
            for m in range(1, N_DEV):
                me, peer, peer_idx = place(m)
                cp = pltpu.make_async_remote_copy(
                    src_ref=g_refs[i].at[peer_idx], dst_ref=out_refs[i].at[peer_idx], send_sem=send_sems.at[i, m - 1],
                    recv_sem=recv_sems.at[i, m - 1], device_id=peer, device_id_type=MESH_IDS)
                cp.wait_recv()
                cp.wait_send()
            own(i).wait()

    return start, finish


def _all_gather_many(vs, name):
    na = len(vs)

    def body(*refs):
        x_refs, out_refs = refs[:na], refs[na:2 * na]
        for step in _gather_phases(x_refs, out_refs, *refs[2 * na:]):
            step()

    return pl.pallas_call(
        body, name=name, in_specs=[ANY] * na, out_specs=[ANY] * na,
        out_shape=[jax.ShapeDtypeStruct((N_DEV,) + v.shape, v.dtype) for v in vs],
        scratch_shapes=_gather_semaphores(na),
        compiler_params=pltpu.CompilerParams(vmem_limit_bytes=V7X_VMEM_LIMIT_BYTES),
    )(*vs)


BIG = ("ffn1_w_gate", "ffn1_w_up", "ffn1_w_down", "w_in", "w_out", "ffn2_w_gate", "ffn2_w_up", "ffn2_w_down")
SMALL = ("ffn1_norm", "mix_norm", "a_log", "dt_bias", "gdn_norm_w", "q_norm_w", "k_norm_w", "rel_bias",
         "ffn2_norm", "final_norm")
WEIGHTS = ("ffn1_norm", "ffn1_w_gate", "ffn1_w_up", "ffn1_w_down", "mix_norm", "w_in", "conv_w", "a_log", "dt_bias",
           "gdn_norm_w", "q_norm_w", "k_norm_w", "rel_bias", "w_out", "ffn2_norm", "ffn2_w_gate", "ffn2_w_up",
           "ffn2_w_down", "final_norm")


def _pack(arrays, width, row_multiple):
    flat = jnp.concatenate([a.reshape(-1) for a in arrays])
    rows = -(-flat.shape[0] // width)
    rows = -(-rows // row_multiple) * row_multiple
    return jnp.pad(flat, (0, rows * width - flat.shape[0])).reshape(rows, width)


def _unpack(packed, shapes):
    flat = packed.reshape(-1)
    out, pos = [], 0
    for shp in shapes:
        size = int(np.prod(shp))
        out.append(flat[pos:pos + size].reshape(shp))
        pos += size
    return out


def kernel(x, ffn1_norm, ffn1_w_gate, ffn1_w_up, ffn1_w_down, mix_norm, w_in, conv_w, a_log, dt_bias, gdn_norm_w, q_norm_w, k_norm_w, rel_bias, w_out, ffn2_norm, ffn2_w_gate, ffn2_w_up, ffn2_w_down, final_norm, loss_target, m_ffn1_norm, m_ffn1_w_gate, m_ffn1_w_up, m_ffn1_w_down, m_mix_norm, m_w_in, m_conv_w, m_a_log, m_dt_bias, m_gdn_norm_w, m_q_norm_w, m_k_norm_w, m_rel_bias, m_w_out, m_ffn2_norm, m_ffn2_w_gate, m_ffn2_w_up, m_ffn2_w_down, m_final_norm, v_ffn1_norm, v_ffn1_w_gate, v_ffn1_w_up, v_ffn1_w_down, v_mix_norm, v_w_in, v_conv_w, v_a_log, v_dt_bias, v_gdn_norm_w, v_q_norm_w, v_k_norm_w, v_rel_bias, v_w_out, v_ffn2_norm, v_ffn2_w_gate, v_ffn2_w_up, v_ffn2_w_down, v_final_norm):
    w = dict(ffn1_norm=ffn1_norm, ffn1_w_gate=ffn1_w_gate, ffn1_w_up=ffn1_w_up, ffn1_w_down=ffn1_w_down, mix_norm=mix_norm, w_in=w_in, conv_w=conv_w, a_log=a_log, dt_bias=dt_bias, gdn_norm_w=gdn_norm_w, q_norm_w=q_norm_w, k_norm_w=k_norm_w, rel_bias=rel_bias, w_out=w_out, ffn2_norm=ffn2_norm, ffn2_w_gate=ffn2_w_gate, ffn2_w_up=ffn2_w_up, ffn2_w_down=ffn2_w_down, final_norm=final_norm)
    mom = dict(ffn1_norm=m_ffn1_norm, ffn1_w_gate=m_ffn1_w_gate, ffn1_w_up=m_ffn1_w_up, ffn1_w_down=m_ffn1_w_down, mix_norm=m_mix_norm, w_in=m_w_in, conv_w=m_conv_w, a_log=m_a_log, dt_bias=m_dt_bias, gdn_norm_w=m_gdn_norm_w, q_norm_w=m_q_norm_w, k_norm_w=m_k_norm_w, rel_bias=m_rel_bias, w_out=m_w_out, ffn2_norm=m_ffn2_norm, ffn2_w_gate=m_ffn2_w_gate, ffn2_w_up=m_ffn2_w_up, ffn2_w_down=m_ffn2_w_down, final_norm=m_final_norm)
    var = dict(ffn1_norm=v_ffn1_norm, ffn1_w_gate=v_ffn1_w_gate, ffn1_w_up=v_ffn1_w_up, ffn1_w_down=v_ffn1_w_down, mix_norm=v_mix_norm, w_in=v_w_in, conv_w=v_conv_w, a_log=v_a_log, dt_bias=v_dt_bias, gdn_norm_w=v_gdn_norm_w, q_norm_w=v_q_norm_w, k_norm_w=v_k_norm_w, rel_bias=v_rel_bias, w_out=v_w_out, ffn2_norm=v_ffn2_norm, ffn2_w_gate=v_ffn2_w_gate, ffn2_w_up=v_ffn2_w_up, ffn2_w_down=v_ffn2_w_down, final_norm=v_final_norm)
    ix, iy, ic = lax.axis_index("x"), lax.axis_index("y"), lax.axis_index("c")
    me = 4 * ix + 2 * iy + ic

    def local(a, n):
        return jnp.swapaxes(a[0], 0, 1) if n in TRANSPOSED else a[0]

    shard = {n: local(w[n], n) for n in BIG}

    conv_shard_shape = w["conv_w"][0].shape
    small = {n: w[n][0] if n not in ("rel_bias",) else w[n] for n in SMALL}
    small = {n: (a.reshape(1, -1) if n.endswith("norm") else a) for n, a in small.items()}
    shards = {n: shard[n].astype(BF16) for n in BIG}
    shards["conv_w"] = _pack([w["conv_w"][0]], LANE, 8)
    loss_row, grad_x, grads = _local_step(x[0], loss_target[0], {}, small, late_shards=shards)

    big_out = [[], [], [], []]
    for n in BIG:
        for kind, val in enumerate(_adamw(grads[n], shard[n], local(mom[n], n), local(var[n], n), f"{n}_adamw")):
            big_out[kind].append(jnp.swapaxes(val, 0, 1) if n in TRANSPOSED else val)

    small_names = SMALL + ("conv_w",)
    small_shapes = [grads[n].shape for n in small_names] + [(1, 1)]
    g_small = _pack([grads[n] for n in small_names] + [loss_row[:, :1]], LANE, 8)
    all_small = _all_gather_many([g_small], "gather_small_grads")[0]
    riders = [jnp.zeros(shp, F32) for shp in small_shapes[len(SMALL):]]
    ws = _pack([w[n].reshape(grads[n].shape) for n in SMALL] + riders, LANE, 8)
    ms = _pack([mom[n].reshape(grads[n].shape) for n in SMALL] + riders, LANE, 8)
    vs = _pack([var[n].reshape(grads[n].shape) for n in SMALL] + riders, LANE, 8)
    small_out = [_unpack(a, small_shapes) for a in _adamw(all_small, ws, ms, vs, "adamw_small")]
    loss = small_out[0][-1][0, 0]
    conv_g = lax.dynamic_slice_in_dim(small_out[0][len(SMALL)], me * conv_shard_shape[0], conv_shard_shape[0], axis=0)
    conv_out = [_unpack(a, [conv_shard_shape])[0] for a in _adamw(
        _pack([conv_g], LANE, 8)[None], _pack([w["conv_w"][0]], LANE, 8), _pack([mom["conv_w"][0]], LANE, 8),
        _pack([var["conv_w"][0]], LANE, 8), "adamw_conv")]

    def leaf(kind, n):
        if n in BIG:
            val = big_out[kind][BIG.index(n)]
        elif n == "conv_w":
            val = conv_out[kind]
        else:
            val = small_out[kind][SMALL.index(n)]
        return val.reshape(w[n].shape)

    outs = [loss, grad_x[None]]
    for kind in range(4):
        outs += [leaf(kind, n) for n in WEIGHTS]
    return tuple(outs)
```
